```python
import math
import jax, jax.numpy as jnp
from jax import lax
import numpy as np

D_MODEL = 1024
BATCH = 8
SEQ = 4096
DEPTH = 1

CHUNK = 64
MIX_WIDTH = D_MODEL
SB_WIDTH = D_MODEL // 2
SB_HEADS = 8
SB_HEAD_DIM = SB_WIDTH // SB_HEADS
SB_Q_BLOCK = 128
GLA_WIDTH = MIX_WIDTH - SB_WIDTH
GLA_HEADS = 4
GLA_KEY_DIM = GLA_WIDTH // 2 // GLA_HEADS
GLA_VAL_DIM = GLA_WIDTH // GLA_HEADS
GLA_GATE_RANK = 16
GLA_TAU = 16.0
D_FF = 2816
CONV_WIDTH = 3
LN_EPS = 1e-5
RMS_EPS = 1e-6
DN_ALPHA = (2.0 * DEPTH) ** 0.25
DN_BETA = (8.0 * DEPTH) ** -0.25

IN_SIZES = (SB_WIDTH, SB_WIDTH, SB_WIDTH,
            GLA_HEADS * GLA_KEY_DIM, GLA_HEADS * GLA_KEY_DIM, GLA_WIDTH, GLA_WIDTH,
            GLA_GATE_RANK)
IN_SPLITS = tuple(int(v) for v in np.cumsum(IN_SIZES)[:-1])
IN_WIDTH = int(sum(IN_SIZES))

kernel_name = "hybrid_stickbreak_gla_convffn_deepnorm"


def layer_norm(x, g, b):
    xf = x.astype(jnp.float32)
    mu = jnp.mean(xf, axis=-1, keepdims=True)
    var = jnp.mean(jnp.square(xf - mu), axis=-1, keepdims=True)
    y = (xf - mu) * lax.rsqrt(var + LN_EPS) * g.astype(jnp.float32) + b.astype(jnp.float32)
    return y.astype(x.dtype)


def stick_breaking_attention(q, k, v):
    S = q.shape[2]
    scale = SB_HEAD_DIM ** -0.5
    outs = []
    for q0 in range(0, S, SB_Q_BLOCK):
        L = q0 + SB_Q_BLOCK
        qb = q[:, :, q0:L].astype(jnp.float32)
        kb = k[:, :, :L].astype(jnp.float32)
        vb = v[:, :, :L].astype(jnp.float32)
        z = jnp.einsum("bhqd,bhkd->bhqk", qb, kb) * scale
        qpos = q0 + jnp.arange(SB_Q_BLOCK)[:, None]
        kpos = jnp.arange(L)[None, :]
        strict = kpos < qpos
        log_1m = jnp.where(strict, jax.nn.log_sigmoid(-z), 0.0)
        suffix = lax.cumsum(log_1m, axis=3, reverse=True) - log_1m
        log_w = jax.nn.log_sigmoid(z) + suffix
        w = jnp.where(strict, jnp.exp(log_w), 0.0)
        outs.append(jnp.einsum("bhqk,bhkd->bhqd", w, vb))
    return jnp.concatenate(outs, axis=2).astype(v.dtype)


def gla_chunked(q, k, v, log_a):
    B, S, H, Dk = q.shape
    Dv = v.shape[-1]
    N = S // CHUNK
    f32 = jnp.float32
    q = (q.astype(f32) * Dk ** -0.5).reshape(B, N, CHUNK, H, Dk)
    k = k.astype(f32).reshape(B, N, CHUNK, H, Dk)
    v = v.astype(f32).reshape(B, N, CHUNK, H, Dv)
    g = log_a.astype(f32).reshape(B, N, CHUNK, H, Dk)
    b = jnp.cumsum(g, axis=2)
    b_ref = b[:, :, CHUNK // 2 - 1:CHUNK // 2]
    q_in = q * jnp.exp(b - b_ref)
    k_in = k * jnp.exp(b_ref - b)
    scores = jnp.einsum("bnthd,bnshd->bnhts", q_in, k_in)
    causal = jnp.tril(jnp.ones((CHUNK, CHUNK), dtype=bool))
    scores = jnp.where(causal, scores, 0.0)
    o_intra = jnp.einsum("bnhts,bnshv->bnthv", scores, v)
    b_last = b[:, :, -1]
    k_dec = k * jnp.exp(b_last[:, :, None] - b)
    chunk_upd = jnp.einsum("bnshk,bnshv->bnhkv", k_dec, v)
    decay = jnp.exp(b_last)

    def step(state, inp):
        dec, upd = inp
        return dec[..., None] * state + upd, state

    init = jnp.zeros((B, H, Dk, Dv), f32)
    _, prev = lax.scan(step, init, (jnp.moveaxis(decay, 1, 0), jnp.moveaxis(chunk_upd, 1, 0)))
    prev = jnp.moveaxis(prev, 0, 1)
    o_inter = jnp.einsum("bnthk,bnhkv->bnthv", q * jnp.exp(b), prev)
    return (o_intra + o_inter).reshape(B, S, H, Dv)


def causal_depthwise_conv(u, w, bias):
    C = u.shape[-1]
    y = lax.conv_general_dilated(
        u, w[:, None, :].astype(u.dtype), window_strides=(1,),
        padding=[(CONV_WIDTH - 1, 0)], dimension_numbers=("NWC", "WIO", "NWC"),
        feature_group_count=C)
    return y + bias


def hybrid_layer(x, w_in, gate_up, gate_bias, gla_norm_g, w_out, ln1_g, ln1_b,
                 w_up, conv_w, conv_b, w_down, ln2_g, ln2_b):
    B, S, _ = x.shape
    proj = x @ w_in
    sb_q, sb_k, sb_v, gq, gk, gv, gg, ga = jnp.split(proj, IN_SPLITS, axis=-1)

    def to_heads(t):
        return t.reshape(B, S, SB_HEADS, SB_HEAD_DIM).transpose(0, 2, 1, 3)

    sb_o = stick_breaking_attention(to_heads(sb_q), to_heads(sb_k), to_heads(sb_v))
    sb_o = sb_o.transpose(0, 2, 1, 3).reshape(B, S, SB_WIDTH)

    log_a = jax.nn.log_sigmoid((ga @ gate_up + gate_bias).astype(jnp.float32)) / GLA_TAU
    o = gla_chunked(gq.reshape(B, S, GLA_HEADS, GLA_KEY_DIM),
                    gk.reshape(B, S, GLA_HEADS, GLA_KEY_DIM),
                    gv.reshape(B, S, GLA_HEADS, GLA_VAL_DIM),
                    log_a.reshape(B, S, GLA_HEADS, GLA_KEY_DIM))
    o = o * lax.rsqrt(jnp.mean(jnp.square(o), axis=-1, keepdims=True) + RMS_EPS)
    o = o * gla_norm_g.astype(jnp.float32)
    gla_o = (o.reshape(B, S, GLA_WIDTH) * jax.nn.silu(gg.astype(jnp.float32))).astype(x.dtype)

    mix = jnp.concatenate([sb_o, gla_o], axis=-1) @ w_out
    h = layer_norm(DN_ALPHA * x + mix, ln1_g, ln1_b)

    u = causal_depthwise_conv(h @ w_up, conv_w, conv_b)
    a, c = jnp.split(u, 2, axis=-1)
    f = (jax.nn.gelu(a, approximate=False) * c) @ w_down
    return layer_norm(DN_ALPHA * h + f, ln2_g, ln2_b)


def _fwd_setup_inputs(seed: int = 0) -> dict:
    key = jax.random.key(seed)
    ks = jax.random.split(key, 24)
    d = D_MODEL
    nrm = lambda k, shape, s: jax.random.normal(k, shape, jnp.float32) * s
    in_scale = d ** -0.5
    pieces = [
        nrm(ks[1], (DEPTH, d, SB_WIDTH), in_scale),
        nrm(ks[2], (DEPTH, d, SB_WIDTH), in_scale),
        nrm(ks[3], (DEPTH, d, SB_WIDTH), in_scale * DN_BETA),
        nrm(ks[4], (DEPTH, d, GLA_HEADS * GLA_KEY_DIM), in_scale),
        nrm(ks[5], (DEPTH, d, GLA_HEADS * GLA_KEY_DIM), in_scale),
        nrm(ks[6], (DEPTH, d, GLA_WIDTH), in_scale * DN_BETA),
        nrm(ks[7], (DEPTH, d, GLA_WIDTH), in_scale),
        nrm(ks[8], (DEPTH, d, GLA_GATE_RANK), in_scale),
    ]
    return {
        "x": nrm(ks[0], (BATCH, SEQ, d), 1.0),
        "w_in": jnp.concatenate(pieces, axis=-1),
        "gate_up": nrm(ks[9], (DEPTH, GLA_GATE_RANK, GLA_HEADS * GLA_KEY_DIM), GLA_GATE_RANK ** -0.5),
        "gate_bias": nrm(ks[10], (DEPTH, GLA_HEADS * GLA_KEY_DIM), 0.1),
        "gla_norm_g": 1.0 + nrm(ks[11], (DEPTH, GLA_VAL_DIM), 0.02),
        "w_out": nrm(ks[12], (DEPTH, MIX_WIDTH, d), MIX_WIDTH ** -0.5 * DN_BETA),
        "ln1_g": 1.0 + nrm(ks[13], (DEPTH, d), 0.02),
        "ln1_b": nrm(ks[14], (DEPTH, d), 0.02),
        "w_up": nrm(ks[15], (DEPTH, d, 2 * D_FF), d ** -0.5 * DN_BETA),
        "conv_w": nrm(ks[16], (DEPTH, CONV_WIDTH, 2 * D_FF), CONV_WIDTH ** -0.5),
        "conv_b": nrm(ks[17], (DEPTH, 2 * D_FF), 0.02),
        "w_down": nrm(ks[18], (DEPTH, D_FF, d), D_FF ** -0.5 * DN_BETA),
        "ln2_g": 1.0 + nrm(ks[19], (DEPTH, d), 0.02),
        "ln2_b": nrm(ks[20], (DEPTH, d), 0.02),
    }


def _fwd_reference(x, w_in, gate_up, gate_bias, gla_norm_g, w_out, ln1_g, ln1_b,
              w_up, conv_w, conv_b, w_down, ln2_g, ln2_b):
    for l in range(DEPTH):
        x = hybrid_layer(x, w_in[l], gate_up[l], gate_bias[l], gla_norm_g[l], w_out[l],
                         ln1_g[l], ln1_b[l], w_up[l], conv_w[l], conv_b[l], w_down[l],
                         ln2_g[l], ln2_b[l])
    return x


import jax as _jax
import jax.numpy as _jnp

TWIN_FORMAT = 'train_step'
FWD_PARAMS = ['x', 'w_in', 'gate_up', 'gate_bias', 'gla_norm_g', 'w_out', 'ln1_g', 'ln1_b', 'w_up', 'conv_w', 'conv_b', 'w_down', 'ln2_g', 'ln2_b']
TWIN_WEIGHTS = ['w_in', 'gate_up', 'gate_bias', 'gla_norm_g', 'w_out', 'ln1_g', 'ln1_b', 'w_up', 'conv_w', 'conv_b', 'w_down', 'ln2_g', 'ln2_b']
TWIN_DIFF_INPUT = 'x'
TWIN_INPUTS = ['x', 'w_in', 'gate_up', 'gate_bias', 'gla_norm_g', 'w_out', 'ln1_g', 'ln1_b', 'w_up', 'conv_w', 'conv_b', 'w_down', 'ln2_g', 'ln2_b', 'loss_target', 'm_w_in', 'm_gate_up', 'm_gate_bias', 'm_gla_norm_g', 'm_w_out', 'm_ln1_g', 'm_ln1_b', 'm_w_up', 'm_conv_w', 'm_conv_b', 'm_w_down', 'm_ln2_g', 'm_ln2_b', 'v_w_in', 'v_gate_up', 'v_gate_bias', 'v_gla_norm_g', 'v_w_out', 'v_ln1_g', 'v_ln1_b', 'v_w_up', 'v_conv_w', 'v_conv_b', 'v_w_down', 'v_ln2_g', 'v_ln2_b']
TWIN_OUTPUTS = ['loss', 'grad_x', 'grad_w_in', 'grad_gate_up', 'grad_gate_bias', 'grad_gla_norm_g', 'grad_w_out', 'grad_ln1_g', 'grad_ln1_b', 'grad_w_up', 'grad_conv_w', 'grad_conv_b', 'grad_w_down', 'grad_ln2_g', 'grad_ln2_b', 'delta_w_in', 'delta_gate_up', 'delta_gate_bias', 'delta_gla_norm_g', 'delta_w_out', 'delta_ln1_g', 'delta_ln1_b', 'delta_w_up', 'delta_conv_w', 'delta_conv_b', 'delta_w_down', 'delta_ln2_g', 'delta_ln2_b', 'new_m_w_in', 'new_m_gate_up', 'new_m_gate_bias', 'new_m_gla_norm_g', 'new_m_w_out', 'new_m_ln1_g', 'new_m_ln1_b', 'new_m_w_up', 'new_m_conv_w', 'new_m_conv_b', 'new_m_w_down', 'new_m_ln2_g', 'new_m_ln2_b', 'new_v_w_in', 'new_v_gate_up', 'new_v_gate_bias', 'new_v_gla_norm_g', 'new_v_w_out', 'new_v_ln1_g', 'new_v_ln1_b', 'new_v_w_up', 'new_v_conv_w', 'new_v_conv_b', 'new_v_w_down', 'new_v_ln2_g', 'new_v_ln2_b']
TWIN_LEAF_KINDS = {'loss': 'loss', 'grad_x': 'grad_x', 'grad_w_in': 'grad_w', 'grad_gate_up': 'grad_w', 'grad_gate_bias': 'grad_w', 'grad_gla_norm_g': 'grad_w', 'grad_w_out': 'grad_w', 'grad_ln1_g': 'grad_w', 'grad_ln1_b': 'grad_w', 'grad_w_up': 'grad_w', 'grad_conv_w': 'grad_w', 'grad_conv_b': 'grad_w', 'grad_w_down': 'grad_w', 'grad_ln2_g': 'grad_w', 'grad_ln2_b': 'grad_w', 'delta_w_in': 'delta_w', 'delta_gate_up': 'delta_w', 'delta_gate_bias': 'delta_w', 'delta_gla_norm_g': 'delta_w', 'delta_w_out': 'delta_w', 'delta_ln1_g': 'delta_w', 'delta_ln1_b': 'delta_w', 'delta_w_up': 'delta_w', 'delta_conv_w': 'delta_w', 'delta_conv_b': 'delta_w', 'delta_w_down': 'delta_w', 'delta_ln2_g': 'delta_w', 'delta_ln2_b': 'delta_w', 'new_m_w_in': 'new_m', 'new_m_gate_up': 'new_m', 'new_m_gate_bias': 'new_m', 'new_m_gla_norm_g': 'new_m', 'new_m_w_out': 'new_m', 'new_m_ln1_g': 'new_m', 'new_m_ln1_b': 'new_m', 'new_m_w_up': 'new_m', 'new_m_conv_w': 'new_m', 'new_m_conv_b': 'new_m', 'new_m_w_down': 'new_m', 'new_m_ln2_g': 'new_m', 'new_m_ln2_b': 'new_m', 'new_v_w_in': 'new_v', 'new_v_gate_up': 'new_v', 'new_v_gate_bias': 'new_v', 'new_v_gla_norm_g': 'new_v', 'new_v_w_out': 'new_v', 'new_v_ln1_g': 'new_v', 'new_v_ln1_b': 'new_v', 'new_v_w_up': 'new_v', 'new_v_conv_w': 'new_v', 'new_v_conv_b': 'new_v', 'new_v_w_down': 'new_v', 'new_v_ln2_g': 'new_v', 'new_v_ln2_b': 'new_v'}


def _forward(args):
    return _fwd_reference(*[args[k] for k in FWD_PARAMS])


def _output_shape():
    def fwd():
        inp = _fwd_setup_inputs(0)
        return _fwd_reference(*[inp[k] for k in FWD_PARAMS])
    out = _jax.eval_shape(fwd)
    return out.shape, out.dtype

N_MICROBATCH = 1
ADAM_LR = 0.001
ADAM_B1 = 0.9
ADAM_B2 = 0.999
ADAM_EPS = 1e-08
ADAM_WD = 0.01
ADAM_STEP = 10
PER_EXAMPLE_BATCH_AXIS = {'x': 0, 'loss_target': 0}
SHARED_INPUTS = []
_WEIGHT_DTYPES = {'w_in': _jnp.float32, 'gate_up': _jnp.float32, 'gate_bias': _jnp.float32, 'gla_norm_g': _jnp.float32, 'w_out': _jnp.float32, 'ln1_g': _jnp.float32, 'ln1_b': _jnp.float32, 'w_up': _jnp.float32, 'conv_w': _jnp.float32, 'conv_b': _jnp.float32, 'w_down': _jnp.float32, 'ln2_g': _jnp.float32, 'ln2_b': _jnp.float32}
MOMENT_SCALE = {'w_in': 5.770477e-02, 'gate_up': 8.059177e-03, 'gate_bias': 3.217395e-02, 'gla_norm_g': 1.086221e-01, 'w_out': 7.271737e-02, 'ln1_g': 1.171092e+00, 'ln1_b': 5.595890e-01, 'w_up': 1.968435e-02, 'conv_w': 1.180689e-02, 'conv_b': 2.369278e-02, 'w_down': 3.214525e-02, 'ln2_g': 3.200892e+01, 'ln2_b': 8.011130e-01}


def _to_microbatches(a, axis):
    t = _jnp.moveaxis(a, axis, 0)
    t = t.reshape((N_MICROBATCH, t.shape[0] // N_MICROBATCH) + t.shape[1:])
    return _jnp.moveaxis(t, 1, axis + 1)


def setup_inputs(seed: int = 0) -> dict:
    inp = _fwd_setup_inputs(seed)
    key = _jax.random.fold_in(_jax.random.key(seed), 7919)
    shape, _ = _output_shape()
    out = dict(inp)
    out["loss_target"] = _jax.random.normal(_jax.random.fold_in(key, 0), shape, _jnp.float32)
    for i, name in enumerate(TWIN_WEIGHTS):
        w = inp[name].astype(_jnp.float32)
        if MOMENT_SCALE is None:
            s = _jnp.sqrt(_jnp.mean(_jnp.square(w)) + 1e-30)
        else:
            s = MOMENT_SCALE[name]
        km, kv = _jax.random.split(_jax.random.fold_in(key, i + 1))
        out[name] = w
        out["m_" + name] = s * _jax.random.normal(km, w.shape, _jnp.float32)
        out["v_" + name] = (s * s) * _jax.random.uniform(kv, w.shape, _jnp.float32, 0.5, 1.5)
    if N_MICROBATCH > 1:
        for name, axis in PER_EXAMPLE_BATCH_AXIS.items():
            out[name] = _to_microbatches(out[name], axis)
    return {'x': out['x'], 'w_in': out['w_in'], 'gate_up': out['gate_up'], 'gate_bias': out['gate_bias'], 'gla_norm_g': out['gla_norm_g'], 'w_out': out['w_out'], 'ln1_g': out['ln1_g'], 'ln1_b': out['ln1_b'], 'w_up': out['w_up'], 'conv_w': out['conv_w'], 'conv_b': out['conv_b'], 'w_down': out['w_down'], 'ln2_g': out['ln2_g'], 'ln2_b': out['ln2_b'], 'loss_target': out['loss_target'], 'm_w_in': out['m_w_in'], 'm_gate_up': out['m_gate_up'], 'm_gate_bias': out['m_gate_bias'], 'm_gla_norm_g': out['m_gla_norm_g'], 'm_w_out': out['m_w_out'], 'm_ln1_g': out['m_ln1_g'], 'm_ln1_b': out['m_ln1_b'], 'm_w_up': out['m_w_up'], 'm_conv_w': out['m_conv_w'], 'm_conv_b': out['m_conv_b'], 'm_w_down': out['m_w_down'], 'm_ln2_g': out['m_ln2_g'], 'm_ln2_b': out['m_ln2_b'], 'v_w_in': out['v_w_in'], 'v_gate_up': out['v_gate_up'], 'v_gate_bias': out['v_gate_bias'], 'v_gla_norm_g': out['v_gla_norm_g'], 'v_w_out': out['v_w_out'], 'v_ln1_g': out['v_ln1_g'], 'v_ln1_b': out['v_ln1_b'], 'v_w_up': out['v_w_up'], 'v_conv_w': out['v_conv_w'], 'v_conv_b': out['v_conv_b'], 'v_w_down': out['v_w_down'], 'v_ln2_g': out['v_ln2_g'], 'v_ln2_b': out['v_ln2_b']}


def _loss(weights, diff, rest, loss_target):
    with _jax.named_scope("forward"):
        args = {**rest, TWIN_DIFF_INPUT: diff, **{k: w.astype(_WEIGHT_DTYPES[k]) for k, w in weights.items()}}
        y = _forward(args)
    with _jax.named_scope("loss_head"):
        err = _jnp.square(y.astype(_jnp.float32) - loss_target)
        return 0.5 * _jnp.sum(_jnp.mean(err, axis=-1)) if err.ndim else 0.5 * err


def _adamw(w, g, m, v):
    m = ADAM_B1 * m + (1.0 - ADAM_B1) * g
    v = ADAM_B2 * v + (1.0 - ADAM_B2) * _jnp.square(g)
    m_hat = m / (1.0 - ADAM_B1 ** ADAM_STEP)
    v_hat = v / (1.0 - ADAM_B2 ** ADAM_STEP)
    delta = -ADAM_LR * (m_hat / (_jnp.sqrt(v_hat) + ADAM_EPS) + ADAM_WD * w)
    return delta, m, v


def reference(x, w_in, gate_up, gate_bias, gla_norm_g, w_out, ln1_g, ln1_b, w_up, conv_w, conv_b, w_down, ln2_g, ln2_b, loss_target, m_w_in, m_gate_up, m_gate_bias, m_gla_norm_g, m_w_out, m_ln1_g, m_ln1_b, m_w_up, m_conv_w, m_conv_b, m_w_down, m_ln2_g, m_ln2_b, v_w_in, v_gate_up, v_gate_bias, v_gla_norm_g, v_w_out, v_ln1_g, v_ln1_b, v_w_up, v_conv_w, v_conv_b, v_w_down, v_ln2_g, v_ln2_b):
    given = dict(x=x, w_in=w_in, gate_up=gate_up, gate_bias=gate_bias, gla_norm_g=gla_norm_g, w_out=w_out, ln1_g=ln1_g, ln1_b=ln1_b, w_up=w_up, conv_w=conv_w, conv_b=conv_b, w_down=w_down, ln2_g=ln2_g, ln2_b=ln2_b, loss_target=loss_target, m_w_in=m_w_in, m_gate_up=m_gate_up, m_gate_bias=m_gate_bias, m_gla_norm_g=m_gla_norm_g, m_w_out=m_w_out, m_ln1_g=m_ln1_g, m_ln1_b=m_ln1_b, m_w_up=m_w_up, m_conv_w=m_conv_w, m_conv_b=m_conv_b, m_w_down=m_w_down, m_ln2_g=m_ln2_g, m_ln2_b=m_ln2_b, v_w_in=v_w_in, v_gate_up=v_gate_up, v_gate_bias=v_gate_bias, v_gla_norm_g=v_gla_norm_g, v_w_out=v_w_out, v_ln1_g=v_ln1_g, v_ln1_b=v_ln1_b, v_w_up=v_w_up, v_conv_w=v_conv_w, v_conv_b=v_conv_b, v_w_down=v_w_down, v_ln2_g=v_ln2_g, v_ln2_b=v_ln2_b)
    weights = {n: given[n] for n in TWIN_WEIGHTS}
    shared = {n: given[n] for n in SHARED_INPUTS}
    per_example = {n: given[n] for n in ['x']}
    grad_fn = _jax.value_and_grad(_loss, argnums=(0, 1))

    def one_microbatch(ex, loss_target):
        ex = dict(ex)
        diff = ex.pop(TWIN_DIFF_INPUT)
        return grad_fn(weights, diff, {**shared, **ex}, loss_target)

    if N_MICROBATCH == 1:
        loss, (grad_w, grad_x) = one_microbatch(per_example, given["loss_target"])
    else:
        def body(carry, xs):
            loss_sum, grad_sum = carry
            l_k, (gw_k, gx_k) = one_microbatch(xs[0], xs[1])
            with _jax.named_scope("update"):
                return (loss_sum + l_k, _jax.tree.map(_jnp.add, grad_sum, gw_k)), gx_k

        init = (_jnp.zeros((), _jnp.float32), _jax.tree.map(_jnp.zeros_like, weights))
        (loss, grad_w), grad_x = _jax.lax.scan(body, init, (per_example, given["loss_target"]))
    with _jax.named_scope("update"):
        delta_w, new_m, new_v = {}, {}, {}
        for n in TWIN_WEIGHTS:
            delta_w[n], new_m[n], new_v[n] = _adamw(weights[n], grad_w[n], given["m_" + n], given["v_" + n])
    return (loss, grad_x, *[grad_w[n] for n in TWIN_WEIGHTS], *[delta_w[n] for n in TWIN_WEIGHTS],
            *[new_m[n] for n in TWIN_WEIGHTS], *[new_v[n] for n in TWIN_WEIGHTS])
```

```python
import functools
import math

import jax
import jax.numpy as jnp
from jax import lax
from jax.experimental import pallas as pl
from jax.experimental.pallas import tpu as pltpu

F32 = jnp.float32
BF16 = jnp.bfloat16

D_MODEL = 1024
SB_WIDTH = 512
GLA_KEYS = 256
GLA_WIDTH = 512
GATE_RANK = 16
IN_WIDTH = 3088
IN_PAD = 3200
D_FF = 2816
CHUNK = 64
LN_EPS = 1e-5
RMS_EPS = 1e-6
ALPHA = 2.0 ** 0.25
GLA_TAU = 16.0
SB_SCALE = 0.125
GLA_SCALE = 0.125
LANES = 128
SB_BLK = 256
GLA_ROWS = 256
VMEM_LIMIT = 56 * 1024 * 1024

ADAM_LR, ADAM_B1, ADAM_B2, ADAM_EPS, ADAM_WD, ADAM_STEP = 0.001, 0.9, 0.999, 1e-08, 0.01, 10

MESH = pl.DeviceIdType.MESH


def _cparams(*sem):
    return pltpu.CompilerParams(dimension_semantics=sem, vmem_limit_bytes=VMEM_LIMIT)


def _dot(a, b):
    return jnp.dot(a, b, preferred_element_type=F32)


def _dot_nt(a, b):
    return lax.dot_general(a, b, (((1,), (1,)), ((), ())), preferred_element_type=F32)


def _dot_tn(a, b):
    return lax.dot_general(a, b, (((0,), (0,)), ((), ())), preferred_element_type=F32)


def _split2(x):
    hi = x.astype(BF16)
    lo = (x - hi.astype(F32)).astype(BF16)
    return hi, lo


def _split3(x):
    hi = x.astype(BF16)
    r = x - hi.astype(F32)
    mid = r.astype(BF16)
    lo = (r - mid.astype(F32)).astype(BF16)
    return hi, mid, lo


def _softplus(z):
    return jnp.maximum(z, 0.0) + jnp.log(1.0 + jnp.exp(-jnp.abs(z)))


def _sigmoid(z):
    return 1.0 / (1.0 + jnp.exp(-z))


def _mm(a, b, *, m, n, k, tm, tn, tk, ta=False, tb=False, a_spec=None, b_spec=None, o_spec=None,
        out_shape=None, out_dtype=F32, add=None, add_scale=1.0, name):
    nk = k // tk
    dn = (((0 if ta else 1,), (1 if tb else 0,)), ((), ()))

    def body(*refs):
        if add is None:
            a_ref, b_ref, o_ref, acc_ref = refs
        else:
            a_ref, b_ref, add_ref, o_ref, acc_ref = refs
        kk = pl.program_id(2)
        part = lax.dot_general(a_ref[...].astype(BF16), b_ref[...].astype(BF16), dn, preferred_element_type=F32)

        @pl.when(kk == 0)
        def _():
            acc_ref[...] = part

        @pl.when(kk > 0)
        def _():
            acc_ref[...] += part

        @pl.when(kk == nk - 1)
        def _():
            r = acc_ref[...]
            if add is not None:
                r = r + add_scale * add_ref[...]
            o_ref[...] = r.astype(out_dtype)

    if a_spec is None:
        a_spec = pl.BlockSpec((tk, tm), lambda i, j, kk: (kk, i)) if ta else pl.BlockSpec((tm, tk), lambda i, j, kk: (i, kk))
    if b_spec is None:
        b_spec = pl.BlockSpec((tn, tk), lambda i, j, kk: (j, kk)) if tb else pl.BlockSpec((tk, tn), lambda i, j, kk: (kk, j))
    if o_spec is None:
        o_spec = pl.BlockSpec((tm, tn), lambda i, j, kk: (i, j))
    if out_shape is None:
        out_shape = jax.ShapeDtypeStruct((m, n), out_dtype)
    in_specs = [a_spec, b_spec]
    args = [a, b]
    if add is not None:
        in_specs.append(pl.BlockSpec((tm, tn), lambda i, j, kk: (i, j)))
        args.append(add)
    return pl.pallas_call(
        body, name=name, grid=(m // tm, n // tn, nk), in_specs=in_specs, out_specs=o_spec, out_shape=out_shape,
        scratch_shapes=[pltpu.VMEM((tm, tn), F32)],
        compiler_params=_cparams("parallel", "parallel", "arbitrary"),
    )(*args)


def _sb_tile(qh, kj, diag, strict, u_strict, r_in):
    z = _dot_nt(qh, kj)
    sp = _softplus(z)
    l1m = -sp
    lsz = z - sp
    if diag:
        l1m = jnp.where(strict, l1m, 0.0)
    hi, lo = _split2(l1m)
    cs = _dot(hi, u_strict) + _dot(lo, u_strict) + r_in
    w = jnp.exp(lsz + cs)
    if diag:
        w = jnp.where(strict, w, 0.0)
    return l1m, lsz, w


def _sb_consts():
    row = lax.broadcasted_iota(jnp.int32, (SB_BLK, SB_BLK), 0)
    col = lax.broadcasted_iota(jnp.int32, (SB_BLK, SB_BLK), 1)
    strict = col < row
    u_strict = (row > col).astype(BF16)
    u_pre = (row < col).astype(BF16)
    lane = lax.broadcasted_iota(jnp.int32, (1, LANES), 1)
    return strict, u_strict, u_pre, lane


def _sb_fwd(proj, t):
    nq = t // SB_BLK

    def body(q_ref, k_ref, v_ref, o_ref, l_ref):
        i = pl.program_id(1)
        strict, u_strict, _, lane = _sb_consts()
        qf = q_ref[...] * SB_SCALE
        acc = jnp.zeros((SB_BLK, LANES), F32)
        lsave = jnp.zeros((SB_BLK, LANES), F32)
        for hh in range(2):
            hm = (lane // 64) == hh
            qh = jnp.where(hm, qf, 0.0).astype(BF16)

            def step(j, carry, diag):
                r, a = carry
                rows = pl.ds(pl.multiple_of(j * SB_BLK, SB_BLK), SB_BLK)
                kj = k_ref[rows, :].astype(BF16)
                vj = jnp.where(hm, v_ref[rows, :], 0.0).astype(BF16)
                l1m, _, w = _sb_tile(qh, kj, diag, strict, u_strict, r)
                a = a + _dot(w.astype(BF16), vj)
                r = r + jnp.sum(l1m, axis=1, keepdims=True)
                return r, a

            r, acc = step(i, (jnp.zeros((SB_BLK, 1), F32), acc), True)
            r, acc = lax.fori_loop(0, i, lambda jj, c: step(i - 1 - jj, c, False), (r, acc))
            lsave = jnp.where(hm, r, lsave)
        o_ref[...] = acc.astype(BF16)
        l_ref[...] = lsave

    return pl.pallas_call(
        body, name="sb_fwd", grid=(4, nq),
        in_specs=[pl.BlockSpec((SB_BLK, LANES), lambda p, i: (i, p)),
                  pl.BlockSpec((t, LANES), lambda p, i: (0, 4 + p)),
                  pl.BlockSpec((t, LANES), lambda p, i: (0, 8 + p))],
        out_specs=[pl.BlockSpec((SB_BLK, LANES), lambda p, i: (i, p)),
                   pl.BlockSpec((SB_BLK, LANES), lambda p, i: (i, p))],
        out_shape=[jax.ShapeDtypeStruct((t, SB_WIDTH), BF16), jax.ShapeDtypeStruct((t, SB_WIDTH), F32)],
        compiler_params=_cparams("parallel", "parallel"),
    )(proj, proj, proj)


def _sb_bwd(proj, dcat, lsum, t):
    nq = t // SB_BLK

    def body(q_ref, k_ref, v_ref, do_ref, l_ref, dq_ref, dk_ref, dv_ref):
        i = pl.program_id(1)

        @pl.when(i == 0)
        def _():
            dk_ref[...] = jnp.zeros_like(dk_ref)
            dv_ref[...] = jnp.zeros_like(dv_ref)

        strict, u_strict, u_pre, lane = _sb_consts()
        qf = q_ref[...] * SB_SCALE
        dof = do_ref[...]
        lblk = l_ref[...]
        dq = jnp.zeros((SB_BLK, LANES), F32)
        for hh in range(2):
            hm = (lane // 64) == hh
            qh = jnp.where(hm, qf, 0.0).astype(BF16)
            doh = jnp.where(hm, dof, 0.0).astype(BF16)
            ltot = jnp.min(jnp.where(hm, lblk, 0.0), axis=1, keepdims=True)

            def step(j, carry, diag):
                pre_l, pre_g, dqa = carry
                rows = pl.ds(pl.multiple_of(j * SB_BLK, SB_BLK), SB_BLK)
                kf = k_ref[rows, :]
                kj = kf.astype(BF16)
                kjm = jnp.where(hm, kf, 0.0).astype(BF16)
                vj = v_ref[rows, :].astype(BF16)
                z = _dot_nt(qh, kj)
                sp = _softplus(z)
                l1m = -sp
                lsz = z - sp
                if diag:
                    l1m = jnp.where(strict, l1m, 0.0)
                rs = jnp.sum(l1m, axis=1, keepdims=True)
                r_in = ltot - (pre_l + rs)
                hi, lo = _split2(l1m)
                cs = _dot(hi, u_strict) + _dot(lo, u_strict) + r_in
                w = jnp.exp(lsz + cs)
                if diag:
                    w = jnp.where(strict, w, 0.0)
                dw = _dot_nt(doh, vj)
                g = w * dw
                ghi, glo = _split2(g)
                gpre = _dot(ghi, u_pre) + _dot(glo, u_pre) + pre_g
                sig = jnp.exp(lsz)
                dz = g * (1.0 - sig) - gpre * sig
                if diag:
                    dz = jnp.where(strict, dz, 0.0)
                dzb = dz.astype(BF16)
                dv_ref[rows, :] += _dot_tn(w.astype(BF16), doh)
                dk_ref[rows, :] += _dot_tn(dzb, qh)
                dqa = dqa + _dot(dzb, kjm)
                return pre_l + rs, pre_g + jnp.sum(g, axis=1, keepdims=True), dqa

            zero = jnp.zeros((SB_BLK, 1), F32)
            carry = lax.fori_loop(0, i, lambda j, c: step(j, c, False), (zero, zero, dq))
            _, _, dq = step(i, carry, True)
        dq_ref[...] = dq * SB_SCALE

    return pl.pallas_call(
        body, name="sb_bwd", grid=(4, nq),
        in_specs=[pl.BlockSpec((SB_BLK, LANES), lambda p, i: (i, p)),
                  pl.BlockSpec((t, LANES), lambda p, i: (0, 4 + p)),
                  pl.BlockSpec((t, LANES), lambda p, i: (0, 8 + p)),
                  pl.BlockSpec((SB_BLK, LANES), lambda p, i: (i, p)),
                  pl.BlockSpec((SB_BLK, LANES), lambda p, i: (i, p))],
        out_specs=[pl.BlockSpec((SB_BLK, LANES), lambda p, i: (i, p)),
                   pl.BlockSpec((t, LANES), lambda p, i: (0, p)),
                   pl.BlockSpec((t, LANES), lambda p, i: (0, p))],
        out_shape=[jax.ShapeDtypeStruct((t, SB_WIDTH), F32)] * 3,
        compiler_params=_cparams("parallel", "arbitrary"),
    )(proj, proj, proj, dcat, lsum)


def _gla_consts():
    r = lax.broadcasted_iota(jnp.int32, (CHUNK, CHUNK), 0)
    c = lax.broadcasted_iota(jnp.int32, (CHUNK, CHUNK), 1)
    causal = c <= r
    lc = causal.astype(BF16)
    lct = (c >= r).astype(BF16)
    rowid = lax.broadcasted_iota(jnp.int32, (CHUNK, 1), 0)
    lane = lax.broadcasted_iota(jnp.int32, (1, LANES), 1)
    sr = lax.broadcasted_iota(jnp.int32, (LANES, 2 * LANES), 0)
    sc = lax.broadcasted_iota(jnp.int32, (LANES, 2 * LANES), 1)
    blockdiag = (sr // 64) == (sc // LANES)
    return causal, lc, lct, rowid, lane, blockdiag


def _dot3(u, x):
    hi, mid, lo = _split3(x)
    return _dot(u, hi) + _dot(u, mid) + _dot(u, lo)


def _row_to_col(row):
    return jnp.transpose(jnp.broadcast_to(row, (LANES, LANES)))


def _gla_gates(ga_ref, gup_ref, gbias_ref):
    pre = _dot(ga_ref[...].astype(BF16), gup_ref[...].astype(BF16)) + gbias_ref[...]
    log_a = (jnp.minimum(pre, 0.0) - jnp.log(1.0 + jnp.exp(-jnp.abs(pre)))) / GLA_TAU
    return pre, log_a


def _gla_chunk_terms(g2, q2, k2, lc, rowid):
    b = _dot3(lc, g2)
    b_ref = jnp.sum(jnp.where(rowid == CHUNK // 2 - 1, b, 0.0), axis=0, keepdims=True)
    b_last = jnp.sum(jnp.where(rowid == CHUNK - 1, b, 0.0), axis=0, keepdims=True)
    qs = q2 * GLA_SCALE
    e_q = jnp.exp(b - b_ref)
    e_k = jnp.exp(b_ref - b)
    e_d = jnp.exp(b_last - b)
    e_b = jnp.exp(b)
    decay = jnp.exp(b_last)
    return dict(qs=qs, e_q=e_q, e_k=e_k, e_d=e_d, e_b=e_b, decay=decay,
                qi=qs * e_q, ki=k2 * e_k, kd=k2 * e_d, qb=qs * e_b)


def _gla_fwd(proj, gate_up_pad, gate_bias, gnorm, t):
    nsteps = t // GLA_ROWS
    cps = GLA_ROWS // CHUNK

    def body(q_ref, k_ref, v_ref, gg_ref, ga_ref, gup_ref, gbias_ref, gn_ref, o_ref, oraw_ref, st_ref, s_scr):
        i = pl.program_id(0)

        @pl.when(i == 0)
        def _():
            s_scr[...] = jnp.zeros_like(s_scr)

        causal, lc, _, rowid, lane, blockdiag = _gla_consts()
        _, log_a = _gla_gates(ga_ref, gup_ref, gbias_ref)
        gn = gn_ref[...]
        for cc in range(cps):
            rows = slice(cc * CHUNK, (cc + 1) * CHUNK)
            for p in range(2):
                kl = slice(p * LANES, (p + 1) * LANES)
                vl = slice(p * 2 * LANES, (p + 1) * 2 * LANES)
                tm = _gla_chunk_terms(log_a[rows, kl], q_ref[rows, kl], k_ref[rows, kl], lc, rowid)
                v2 = v_ref[rows, vl]
                v2b = v2.astype(BF16)
                s_prev = s_scr[p]
                st_ref[cc, p] = s_prev
                kib = tm["ki"].astype(BF16)
                o_inter = _dot(tm["qb"].astype(BF16), s_prev.astype(BF16))
                outs = []
                for hh in range(2):
                    hm = (lane // 64) == hh
                    a = _dot_nt(jnp.where(hm, tm["qi"], 0.0).astype(BF16), kib)
                    a = jnp.where(causal, a, 0.0)
                    outs.append(_dot(a.astype(BF16), v2b[:, hh * LANES:(hh + 1) * LANES]))
                o2 = jnp.concatenate(outs, axis=1) + o_inter
                upd = jnp.where(blockdiag, _dot_tn(tm["kd"].astype(BF16), v2b), 0.0)
                dcol = _row_to_col(tm["decay"])
                s_scr[p] = s_prev * jnp.concatenate([dcol, dcol], axis=1) + upd
                oraw_ref[rows, vl] = o2
                for hh in range(2):
                    hl = slice(hh * LANES, (hh + 1) * LANES)
                    oh = o2[:, hl]
                    gl = slice(p * 2 * LANES + hh * LANES, p * 2 * LANES + (hh + 1) * LANES)
                    rinv = lax.rsqrt(jnp.mean(oh * oh, axis=1, keepdims=True) + RMS_EPS)
                    gg = gg_ref[rows, gl]
                    o_ref[rows, gl] = (oh * rinv * gn * (gg * _sigmoid(gg))).astype(BF16)

    cb = lambda w, idx: pl.BlockSpec((GLA_ROWS, w), lambda i: (i, idx))
    full = lambda shp: pl.BlockSpec(shp, lambda i: tuple(0 for _ in shp))
    return pl.pallas_call(
        body, name="gla_fwd", grid=(nsteps,),
        in_specs=[cb(256, 6), cb(256, 7), cb(512, 4), cb(512, 5), cb(128, 24),
                  full((LANES, GLA_KEYS)), full((1, GLA_KEYS)), full((1, LANES))],
        out_specs=[pl.BlockSpec((GLA_ROWS, GLA_WIDTH), lambda i: (i, 0)),
                   pl.BlockSpec((GLA_ROWS, GLA_WIDTH), lambda i: (i, 0)),
                   pl.BlockSpec((cps, 2, LANES, 2 * LANES), lambda i: (i, 0, 0, 0))],
        out_shape=[jax.ShapeDtypeStruct((t, GLA_WIDTH), BF16), jax.ShapeDtypeStruct((t, GLA_WIDTH), F32),
                   jax.ShapeDtypeStruct((t // CHUNK, 2, LANES, 2 * LANES), F32)],
        scratch_shapes=[pltpu.VMEM((2, LANES, 2 * LANES), F32)],
        compiler_params=_cparams("arbitrary"),
    )(proj, proj, proj, proj, proj, gate_up_pad, gate_bias, gnorm)


def _gla_bwd(proj, dcat, oraw, states, gate_up_pad, gate_bias, gnorm, t):
    nsteps = t // GLA_ROWS
    cps = GLA_ROWS // CHUNK
    wout = 2 * GLA_KEYS + 2 * GLA_WIDTH + LANES

    def body(q_ref, k_ref, v_ref, gg_ref, ga_ref, do_ref, oraw_ref, st_ref, gup_ref, gbias_ref, gn_ref,
             d_ref, dgup_ref, dgbias_ref, dgn_ref, ds_scr, dpre_scr):
        i = pl.program_id(0)

        @pl.when(i == 0)
        def _():
            ds_scr[...] = jnp.zeros_like(ds_scr)
            dgup_ref[...] = jnp.zeros_like(dgup_ref)
            dgbias_ref[...] = jnp.zeros_like(dgbias_ref)
            dgn_ref[...] = jnp.zeros_like(dgn_ref)

        causal, lc, lct, rowid, lane, blockdiag = _gla_consts()
        pre, log_a = _gla_gates(ga_ref, gup_ref, gbias_ref)
        gn = gn_ref[...]
        dgn = jnp.zeros((1, LANES), F32)
        for cc in reversed(range(cps)):
            rows = slice(cc * CHUNK, (cc + 1) * CHUNK)
            for p in range(2):
                kl = slice(p * LANES, (p + 1) * LANES)
                vl = slice(p * 2 * LANES, (p + 1) * 2 * LANES)
                tm = _gla_chunk_terms(log_a[rows, kl], q_ref[rows, kl], k_ref[rows, kl], lc, rowid)
                v2b = v_ref[rows, vl].astype(BF16)
                s_prev = st_ref[cc, p]
                ds2 = ds_scr[p]
                dos = []
                for hh in range(2):
                    gl = slice(p * 2 * LANES + hh * LANES, p * 2 * LANES + (hh + 1) * LANES)
                    oh = oraw_ref[rows, gl]
                    rinv = lax.rsqrt(jnp.mean(oh * oh, axis=1, keepdims=True) + RMS_EPS)
                    on = oh * rinv
                    gg = gg_ref[rows, gl]
                    sg = _sigmoid(gg)
                    sil = gg * sg
                    dgo = do_ref[rows, gl]
                    d_ref[rows, 2 * GLA_KEYS + GLA_WIDTH + gl.start:2 * GLA_KEYS + GLA_WIDTH + gl.stop] = (
                        dgo * on * gn * (sg * (1.0 + gg * (1.0 - sg))))
                    dgn = dgn + jnp.sum(dgo * sil * on, axis=0, keepdims=True)
                    don = dgo * sil * gn
                    dos.append(rinv * (don - on * jnp.mean(don * on, axis=1, keepdims=True)))
                do2b = jnp.concatenate(dos, axis=1).astype(BF16)
                qib = tm["qi"].astype(BF16)
                kib = tm["ki"].astype(BF16)
                kdb = tm["kd"].astype(BF16)
                qbb = tm["qb"].astype(BF16)
                ds2b = ds2.astype(BF16)
                dqi = jnp.zeros((CHUNK, LANES), F32)
                dki = jnp.zeros((CHUNK, LANES), F32)
                dvs = []
                for hh in range(2):
                    hm = (lane // 64) == hh
                    hl = slice(hh * LANES, (hh + 1) * LANES)
                    a = jnp.where(causal, _dot_nt(jnp.where(hm, tm["qi"], 0.0).astype(BF16), kib), 0.0).astype(BF16)
                    da = jnp.where(causal, _dot_nt(do2b[:, hl], v2b[:, hl]), 0.0).astype(BF16)
                    dvs.append(_dot_tn(a, do2b[:, hl]))
                    dqi = dqi + jnp.where(hm, _dot(da, kib), 0.0)
                    dki = dki + jnp.where(hm, _dot_tn(da, qib), 0.0)
                dv2 = jnp.concatenate(dvs, axis=1) + _dot(kdb, ds2b)
                dqb = _dot_nt(do2b, s_prev.astype(BF16))
                dkd = _dot_nt(v2b, ds2b)
                dcol = _row_to_col(tm["decay"])
                dsp = jnp.where(blockdiag, _dot_tn(qbb, do2b), 0.0) + ds2 * jnp.concatenate([dcol, dcol], axis=1)
                ddecay_col = jnp.sum(ds2 * s_prev, axis=1, keepdims=True)
                ddecay_row = jnp.transpose(jnp.broadcast_to(ddecay_col, (LANES, LANES)))[0:1, :]
                ds_scr[p] = dsp
                dqs = dqi * tm["e_q"] + dqb * tm["e_b"]
                dk = dki * tm["e_k"] + dkd * tm["e_d"]
                t_qi = dqi * tm["qi"]
                t_ki = dki * tm["ki"]
                t_kd = dkd * tm["kd"]
                db = t_qi - t_ki + dqb * tm["qb"] - t_kd
                db_ref = jnp.sum(t_ki - t_qi, axis=0, keepdims=True)
                db_last = jnp.sum(t_kd, axis=0, keepdims=True) + ddecay_row * tm["decay"]
                db = db + jnp.where(rowid == CHUNK // 2 - 1, db_ref, 0.0) + jnp.where(rowid == CHUNK - 1, db_last, 0.0)
                dg = _dot3(lct, db)
                d_ref[rows, p * LANES:(p + 1) * LANES] = dqs * GLA_SCALE
                d_ref[rows, GLA_KEYS + p * LANES:GLA_KEYS + (p + 1) * LANES] = dk
                d_ref[rows, 2 * GLA_KEYS + p * 2 * LANES:2 * GLA_KEYS + (p + 1) * 2 * LANES] = dv2
                dpre_scr[rows, kl] = dg
        dpre = dpre_scr[...] * (1.0 / GLA_TAU) * _sigmoid(-pre)
        dpb = dpre.astype(BF16)
        dgn_ref[...] += dgn
        dgbias_ref[...] += jnp.sum(dpre, axis=0, keepdims=True)
        dgup_ref[...] += _dot_tn(ga_ref[...].astype(BF16), dpb)
        d_ref[:, 2 * GLA_KEYS + 2 * GLA_WIDTH:] = _dot_nt(dpb, gup_ref[...].astype(BF16))

    rev = lambda i: nsteps - 1 - i
    cb = lambda w, idx: pl.BlockSpec((GLA_ROWS, w), lambda i: (rev(i), idx))
    full = lambda shp: pl.BlockSpec(shp, lambda i: tuple(0 for _ in shp))
    return pl.pallas_call(
        body, name="gla_bwd", grid=(nsteps,),
        in_specs=[cb(256, 6), cb(256, 7), cb(512, 4), cb(512, 5), cb(128, 24), cb(512, 1), cb(512, 0),
                  pl.BlockSpec((cps, 2, LANES, 2 * LANES), lambda i: (rev(i), 0, 0, 0)),
                  full((LANES, GLA_KEYS)), full((1, GLA_KEYS)), full((1, LANES))],
        out_specs=[pl.BlockSpec((GLA_ROWS, wout), lambda i: (rev(i), 0)),
                   full((LANES, GLA_KEYS)), full((1, GLA_KEYS)), full((1, LANES))],
        out_shape=[jax.ShapeDtypeStruct((t, wout), F32), jax.ShapeDtypeStruct((LANES, GLA_KEYS), F32),
                   jax.ShapeDtypeStruct((1, GLA_KEYS), F32), jax.ShapeDtypeStruct((1, LANES), F32)],
        scratch_shapes=[pltpu.VMEM((2, LANES, 2 * LANES), F32), pltpu.VMEM((GLA_ROWS, GLA_KEYS), F32)],
        compiler_params=_cparams("arbitrary"),
    )(proj, proj, proj, proj, proj, dcat, oraw, states, gate_up_pad, gate_bias, gnorm)


def _ln_stats(r):
    mu = jnp.mean(r, axis=1, keepdims=True)
    xc = r - mu
    rstd = lax.rsqrt(jnp.mean(xc * xc, axis=1, keepdims=True) + LN_EPS)
    return xc * rstd, rstd


def _ln_bwd(dy_g, xhat, rstd):
    return rstd * (dy_g - jnp.mean(dy_g, axis=1, keepdims=True) - xhat * jnp.mean(dy_g * xhat, axis=1, keepdims=True))


def _outproj_ln1(sb_o, gla_o, x, w_out, g1, b1, t, tm=256):
    def body(sb_ref, gl_ref, x_ref, w_ref, g_ref, b_ref, xhat_ref, rstd_ref, h_ref):
        mix = _dot(sb_ref[...], w_ref[0:SB_WIDTH, :]) + _dot(gl_ref[...], w_ref[SB_WIDTH:, :])
        xhat, rstd = _ln_stats(ALPHA * x_ref[...] + mix)
        xhat_ref[...] = xhat
        rstd_ref[...] = rstd
        h_ref[...] = (xhat * g_ref[...] + b_ref[...]).astype(BF16)

    row = lambda w: pl.BlockSpec((tm, w), lambda i: (i, 0))
    full = lambda shp: pl.BlockSpec(shp, lambda i: (0, 0))
    return pl.pallas_call(
        body, name="outproj_ln1", grid=(t // tm,),
        in_specs=[row(SB_WIDTH), row(GLA_WIDTH), row(D_MODEL), full((D_MODEL, D_MODEL)), full((1, D_MODEL)), full((1, D_MODEL))],
        out_specs=[row(D_MODEL), row(1), row(D_MODEL)],
        out_shape=[jax.ShapeDtypeStruct((t, D_MODEL), F32), jax.ShapeDtypeStruct((t, 1), F32),
                   jax.ShapeDtypeStruct((t, D_MODEL), BF16)],
        compiler_params=_cparams("parallel"),
    )(sb_o, gla_o, x, w_out, g1, b1)


_INV_SQRT2 = 1.0 / math.sqrt(2.0)
_INV_SQRT2PI = 1.0 / math.sqrt(2.0 * math.pi)


def _conv3(xs, w_ref, b_ref, half):
    return (w_ref[half, 0:1, :] * pltpu.roll(xs, 2, 0) + w_ref[half, 1:2, :] * pltpu.roll(xs, 1, 0)
            + w_ref[half, 2:3, :] * xs + b_ref[half])


def _conv_gelu_fwd(up3, conv_w3, conv_b3, t, tr=512, ct=256):
    nct = D_FF // ct
    hb = tr // 8

    def body(cur_ref, prev_ref, w_ref, b_ref, gm_ref):
        i = pl.program_id(1)
        keep = (i > 0).astype(F32)
        us = []
        for half in range(2):
            xs = jnp.concatenate([prev_ref[half] * keep, cur_ref[half]], axis=0)
            us.append(_conv3(xs, w_ref, b_ref, half)[8:, :])
        a, c = us
        gelu = 0.5 * a * (1.0 + lax.erf(a * _INV_SQRT2))
        gm_ref[...] = (gelu * c).astype(BF16)

    return pl.pallas_call(
        body, name="conv_gelu_fwd", grid=(nct, t // tr),
        in_specs=[pl.BlockSpec((2, tr, ct), lambda j, i: (0, i, j)),
                  pl.BlockSpec((2, 8, ct), lambda j, i: (0, jnp.maximum(i * hb - 1, 0), j)),
                  pl.BlockSpec((2, 3, ct), lambda j, i: (0, 0, j)),
                  pl.BlockSpec((2, 1, ct), lambda j, i: (0, 0, j))],
        out_specs=pl.BlockSpec((tr, ct), lambda j, i: (i, j)),
        out_shape=jax.ShapeDtypeStruct((t, D_FF), BF16),
        compiler_params=_cparams("parallel", "parallel"),
    )(up3, up3, conv_w3, conv_b3)


def _conv_gelu_bwd(up3, dgm, conv_w3, conv_b3, t, tr=512, ct=256):
    nct = D_FF // ct
    nrt = t // tr
    hb = tr // 8
    n = tr + 16

    def body(cur_ref, prev_ref, next_ref, dg_ref, dgn_ref, w_ref, b_ref, dup_ref, dcw_ref, dcb_ref):
        i = pl.program_id(1)

        @pl.when(i == 0)
        def _():
            dcw_ref[...] = jnp.zeros_like(dcw_ref)
            dcb_ref[...] = jnp.zeros_like(dcb_ref)

        keep_prev = (i > 0).astype(F32)
        keep_next = (i < nrt - 1).astype(F32)
        xs, xm1, xm2, us = [], [], [], []
        for half in range(2):
            x = jnp.concatenate([prev_ref[half] * keep_prev, cur_ref[half], next_ref[half]], axis=0)
            xs.append(x)
            xm1.append(pltpu.roll(x, 1, 0))
            xm2.append(pltpu.roll(x, 2, 0))
            us.append(w_ref[half, 0:1, :] * xm2[half] + w_ref[half, 1:2, :] * xm1[half]
                      + w_ref[half, 2:3, :] * x + b_ref[half])
        a, c = us
        dg = jnp.concatenate([jnp.zeros((8, ct), F32), dg_ref[...], dgn_ref[...] * keep_next], axis=0)
        cdf = 0.5 * (1.0 + lax.erf(a * _INV_SQRT2))
        pdf = jnp.exp(-0.5 * a * a) * _INV_SQRT2PI
        dus = [dg * c * (cdf + a * pdf), dg * (a * cdf)]
        rid = lax.broadcasted_iota(jnp.int32, (8, 1), 0)
        for half in range(2):
            du = dus[half]
            dup = (w_ref[half, 2:3, :] * du + w_ref[half, 1:2, :] * pltpu.roll(du, n - 1, 0)
                   + w_ref[half, 0:1, :] * pltpu.roll(du, n - 2, 0))
            dup_ref[half] = dup[8:tr + 8, :].astype(BF16)
            duc = du[8:tr + 8, :]
            s0 = jnp.sum(duc * xm2[half][8:tr + 8, :], axis=0, keepdims=True)
            s1 = jnp.sum(duc * xm1[half][8:tr + 8, :], axis=0, keepdims=True)
            s2 = jnp.sum(duc * xs[half][8:tr + 8, :], axis=0, keepdims=True)
            dcw_ref[half] += jnp.where(rid == 0, s0, jnp.where(rid == 1, s1, jnp.where(rid == 2, s2, 0.0)))
            dcb_ref[half] += jnp.sum(duc, axis=0, keepdims=True)

    last8 = t // 8 - 1
    return pl.pallas_call(
        body, name="conv_gelu_bwd", grid=(nct, nrt),
        in_specs=[pl.BlockSpec((2, tr, ct), lambda j, i: (0, i, j)),
                  pl.BlockSpec((2, 8, ct), lambda j, i: (0, jnp.maximum(i * hb - 1, 0), j)),
                  pl.BlockSpec((2, 8, ct), lambda j, i: (0, jnp.minimum((i + 1) * hb, last8), j)),
                  pl.BlockSpec((tr, ct), lambda j, i: (i, j)),
                  pl.BlockSpec((8, ct), lambda j, i: (jnp.minimum((i + 1) * hb, last8), j)),
                  pl.BlockSpec((2, 3, ct), lambda j, i: (0, 0, j)),
                  pl.BlockSpec((2, 1, ct), lambda j, i: (0, 0, j))],
        out_specs=[pl.BlockSpec((2, tr, ct), lambda j, i: (0, i, j)),
                   pl.BlockSpec((2, 8, ct), lambda j, i: (0, 0, j)),
                   pl.BlockSpec((2, 1, ct), lambda j, i: (0, 0, j))],
        out_shape=[jax.ShapeDtypeStruct((2, t, D_FF), BF16), jax.ShapeDtypeStruct((2, 8, D_FF), F32),
                   jax.ShapeDtypeStruct((2, 1, D_FF), F32)],
        compiler_params=_cparams("parallel", "arbitrary"),
    )(up3, up3, up3, dgm, dgm, conv_w3, conv_b3)


def _down_ln2_loss(gm, w_down, xhat1, g1, b1, g2, b2, target, t, tm=256):
    def body(gm_ref, w_ref, xh_ref, g1_ref, b1_ref, g2_ref, b2_ref, tg_ref, dr_ref, loss_ref, dg_ref, db_ref):
        i = pl.program_id(0)

        @pl.when(i == 0)
        def _():
            loss_ref[...] = jnp.zeros_like(loss_ref)
            dg_ref[...] = jnp.zeros_like(dg_ref)
            db_ref[...] = jnp.zeros_like(db_ref)

        h = xh_ref[...] * g1_ref[...] + b1_ref[...]
        xhat, rstd = _ln_stats(ALPHA * h + _dot(gm_ref[...], w_ref[...]))
        err = xhat * g2_ref[...] + b2_ref[...] - tg_ref[...]
        loss_ref[...] += 0.5 * jnp.sum(jnp.sum(err * err, axis=1, keepdims=True), axis=0, keepdims=True) / D_MODEL
        dy = err * (1.0 / D_MODEL)
        dg_ref[...] += jnp.sum(dy * xhat, axis=0, keepdims=True)
        db_ref[...] += jnp.sum(dy, axis=0, keepdims=True)
        dr_ref[...] = _ln_bwd(dy * g2_ref[...], xhat, rstd)

    row = lambda w: pl.BlockSpec((tm, w), lambda i: (i, 0))
    full = lambda shp: pl.BlockSpec(shp, lambda i: (0, 0))
    vec = full((1, D_MODEL))
    return pl.pallas_call(
        body, name="down_ln2_loss", grid=(t // tm,),
        in_specs=[row(D_FF), full((D_FF, D_MODEL)), row(D_MODEL), vec, vec, vec, vec, row(D_MODEL)],
        out_specs=[row(D_MODEL), full((1, 1)), vec, vec],
        out_shape=[jax.ShapeDtypeStruct((t, D_MODEL), F32), jax.ShapeDtypeStruct((1, 1), F32),
                   jax.ShapeDtypeStruct((1, D_MODEL), F32), jax.ShapeDtypeStruct((1, D_MODEL), F32)],
        compiler_params=_cparams("arbitrary"),
    )(gm, w_down, xhat1, g1, b1, g2, b2, target)


def _dh_ln1_bwd(dup3, w_up, dr2, xhat1, rstd1, g1, t, tm=512, tk=1408):
    nkh = D_FF // tk
    nk = 2 * nkh

    def body(a_ref, w_ref, dr2_ref, xh_ref, rs_ref, g_ref, dr1_ref, dg_ref, db_ref, acc_ref):
        i = pl.program_id(0)
        kk = pl.program_id(1)

        @pl.when((i == 0) & (kk == 0))
        def _():
            dg_ref[...] = jnp.zeros_like(dg_ref)
            db_ref[...] = jnp.zeros_like(db_ref)

        part = _dot_nt(a_ref[...], w_ref[...])

        @pl.when(kk == 0)
        def _():
            acc_ref[...] = part

        @pl.when(kk > 0)
        def _():
            acc_ref[...] += part

        @pl.when(kk == nk - 1)
        def _():
            dh = acc_ref[...] + ALPHA * dr2_ref[...]
            xhat = xh_ref[...]
            dg_ref[...] += jnp.sum(dh * xhat, axis=0, keepdims=True)
            db_ref[...] += jnp.sum(dh, axis=0, keepdims=True)
            dr1_ref[...] = _ln_bwd(dh * g_ref[...], xhat, rs_ref[...])

    row = lambda w: pl.BlockSpec((tm, w), lambda i, kk: (i, 0))
    vec = pl.BlockSpec((1, D_MODEL), lambda i, kk: (0, 0))
    return pl.pallas_call(
        body, name="dh_ln1_bwd", grid=(t // tm, nk),
        in_specs=[pl.BlockSpec((None, tm, tk), lambda i, kk: (kk // nkh, i, kk % nkh)),
                  pl.BlockSpec((D_MODEL, tk), lambda i, kk: (0, kk)),
                  row(D_MODEL), row(D_MODEL), row(1), vec],
        out_specs=[row(D_MODEL), vec, vec],
        out_shape=[jax.ShapeDtypeStruct((t, D_MODEL), F32), jax.ShapeDtypeStruct((1, D_MODEL), F32),
                   jax.ShapeDtypeStruct((1, D_MODEL), F32)],
        scratch_shapes=[pltpu.VMEM((tm, D_MODEL), F32)],
        compiler_params=_cparams("arbitrary", "arbitrary"),
    )(dup3, w_up, dr2, xhat1, rstd1, g1)


def _adamw(w, g, m, v, name):
    rows, cols = w.shape
    tr = rows
    for cand in (256, 128, 64, 32, 16, 8):
        if rows % cand == 0 and rows > cand:
            tr = cand
            break
    c1 = 1.0 / (1.0 - ADAM_B1 ** ADAM_STEP)
    c2 = 1.0 / (1.0 - ADAM_B2 ** ADAM_STEP)

    def body(w_ref, g_ref, m_ref, v_ref, d_ref, nm_ref, nv_ref):
        gv = g_ref[...]
        nm = ADAM_B1 * m_ref[...] + (1.0 - ADAM_B1) * gv
        nv = ADAM_B2 * v_ref[...] + (1.0 - ADAM_B2) * (gv * gv)
        d_ref[...] = -ADAM_LR * ((nm * c1) / (jnp.sqrt(nv * c2) + ADAM_EPS) + ADAM_WD * w_ref[...])
        nm_ref[...] = nm
        nv_ref[...] = nv

    spec = pl.BlockSpec((tr, cols), lambda i: (i, 0))
    out = jax.ShapeDtypeStruct((rows, cols), F32)
    return pl.pallas_call(
        body, name=name, grid=(rows // tr,), in_specs=[spec] * 4, out_specs=[spec] * 3, out_shape=[out] * 3,
        compiler_params=_cparams("parallel"),
    )(w, g, m, v)


def _local_step(x, target, w_in_p, gate_up_pad, gate_bias, gnorm, w_out, ln1_g, ln1_b, w_up, conv_w3, conv_b3,
                w_down, ln2_g, ln2_b):
    t = x.shape[0]
    proj = _mm(x, w_in_p, m=t, n=IN_PAD, k=D_MODEL, tm=512, tn=640, tk=D_MODEL, name="proj")
    sb_o, lsum = _sb_fwd(proj, t)
    gla_o, oraw, states = _gla_fwd(proj, gate_up_pad, gate_bias, gnorm, t)
    xhat1, rstd1, h_bf = _outproj_ln1(sb_o, gla_o, x, w_out, ln1_g, ln1_b, t)
    tn_up = 256
    nup = D_FF // tn_up
    up3 = _mm(h_bf, w_up, m=t, n=2 * D_FF, k=D_MODEL, tm=512, tn=tn_up, tk=D_MODEL, name="up",
              o_spec=pl.BlockSpec((None, 512, tn_up), lambda i, j, kk: (j // nup, i, j % nup)),
              out_shape=jax.ShapeDtypeStruct((2, t, D_FF), F32))
    gm = _conv_gelu_fwd(up3, conv_w3, conv_b3, t)
    dr2, loss, dln2_g, dln2_b = _down_ln2_loss(gm, w_down, xhat1, ln1_g, ln1_b, ln2_g, ln2_b, target, t)
    dgm = _mm(dr2, w_down, m=t, n=D_FF, k=D_MODEL, tm=512, tn=256, tk=D_MODEL, tb=True, name="dgm")
    dw_down = _mm(gm, dr2, m=D_FF, n=D_MODEL, k=t, tm=256, tn=512, tk=min(t, 1024), ta=True, name="dw_down")
    dup3, dcw, dcb = _conv_gelu_bwd(up3, dgm, conv_w3, conv_b3, t)
    dr1, dln1_g, dln1_b = _dh_ln1_bwd(dup3, w_up, dr2, xhat1, rstd1, ln1_g, t)
    tkw = min(t, 1024)
    dw_up = _mm(h_bf, dup3, m=D_MODEL, n=2 * D_FF, k=t, tm=512, tn=tn_up, tk=tkw, ta=True, name="dw_up",
                b_spec=pl.BlockSpec((None, tkw, tn_up), lambda i, j, kk: (j // nup, kk, j % nup)))
    dcat = _mm(dr1, w_out, m=t, n=D_MODEL, k=D_MODEL, tm=512, tn=512, tk=D_MODEL, tb=True, name="dcat")
    dw_out_sb = _mm(sb_o, dr1, m=SB_WIDTH, n=D_MODEL, k=t, tm=512, tn=512, tk=tkw, ta=True, name="dw_out_sb")
    dw_out_gla = _mm(gla_o, dr1, m=GLA_WIDTH, n=D_MODEL, k=t, tm=512, tn=512, tk=tkw, ta=True, name="dw_out_gla")
    dsq, dsk, dsv = _sb_bwd(proj, dcat, lsum, t)
    dgla, dgup_pad, dgbias, dgnorm = _gla_bwd(proj, dcat, oraw, states, gate_up_pad, gate_bias, gnorm, t)
    dproj = jnp.concatenate([dsq, dsk, dsv, dgla], axis=1)
    dx = _mm(dproj, w_in_p, m=t, n=D_MODEL, k=IN_PAD, tm=512, tn=512, tk=640, tb=True, add=dr1, add_scale=ALPHA, name="dx")
    dw_in = _mm(x, dproj, m=D_MODEL, n=IN_PAD, k=t, tm=512, tn=640, tk=tkw, ta=True, name="dw_in")
    grads = dict(
        w_in=dw_in[:, :IN_WIDTH], gate_up=dgup_pad[:GATE_RANK], gate_bias=dgbias, gla_norm_g=dgnorm,
        w_out=jnp.concatenate([dw_out_sb, dw_out_gla], axis=0), ln1_g=dln1_g, ln1_b=dln1_b,
        w_up=dw_up, conv_w=jnp.concatenate([dcw[0, :3], dcw[1, :3]], axis=1),
        conv_b=jnp.concatenate([dcb[0], dcb[1]], axis=1), w_down=dw_down, ln2_g=dln2_g, ln2_b=dln2_b)
    return loss, dx, grads


SLAB_ROWS = 3200
HALF_ROWS = SLAB_ROWS // 2
SMALL_ROWS = 8
VEC_ROWS = 16
W_IN_S, W_UP_S, W_OUT_S, W_DOWN_S = IN_WIDTH // 4, 2 * D_FF // 4, D_MODEL // 4, D_FF // 4
VEC_SIZES = (("gate_bias", GLA_KEYS), ("gla_norm_g", LANES), ("ln1_g", D_MODEL), ("ln1_b", D_MODEL),
             ("conv_b", 2 * D_FF), ("ln2_g", D_MODEL), ("ln2_b", D_MODEL))


def _rows(a):
    flat = a.reshape(-1)
    pad = (-flat.shape[0]) % D_MODEL
    if pad:
        flat = jnp.pad(flat, (0, pad))
    return flat.reshape(-1, D_MODEL)


def _pad_rows(a, rows):
    return jnp.pad(a, ((0, rows - a.shape[0]), (0, 0)))


def _pack_matrices(w_in_s, w_up_s, w_out_s, w_down_s, extra=()):
    parts = [_rows(w_in_s), _rows(w_up_s), _rows(w_out_s), _rows(w_down_s)] + [_rows(e) for e in extra]
    return _pad_rows(jnp.concatenate(parts, axis=0), SLAB_ROWS)


def _unpack_matrices(slab):
    o = 0
    out = []
    for rows, shape in ((W_IN_S, (D_MODEL, W_IN_S)), (W_UP_S, (D_MODEL, W_UP_S)), (W_OUT_S, (W_OUT_S, D_MODEL)),
                        (W_DOWN_S, (W_DOWN_S, D_MODEL))):
        out.append(slab[o:o + rows].reshape(shape))
        o += rows
    return out, o


def _pack_vec(d):
    flat = jnp.concatenate([d[n].reshape(-1) for n, _ in VEC_SIZES])
    return _pad_rows(_rows(flat), VEC_ROWS)


def _unpack_vec(v):
    flat = v.reshape(-1)
    out, o = {}, 0
    for n, size in VEC_SIZES:
        out[n] = flat[o:o + size].reshape(1, size)
        o += size
    return out


HBM_SPEC = pl.BlockSpec(memory_space=pltpu.HBM)


def _position():
    x, y, c = lax.axis_index("x"), lax.axis_index("y"), lax.axis_index("c")
    chips = [(1 - x, y), (x, 1 - y), (1 - x, 1 - y)]
    return x, y, c, chips


def _remote(src, dst, send_sems, recv_sems, k, to):
    return pltpu.make_async_remote_copy(src_ref=src, dst_ref=dst, send_sem=send_sems.at[k], recv_sem=recv_sems.at[k],
                                        device_id=to, device_id_type=MESH)


def _gather_weights(slab, small):
    def body(slab_ref, small_ref, out_ref, osm_ref, send_sems, recv_sems, local_sems):
        x, y, c, chips = _position()
        me, sib = (x, y, c), (x, y, 1 - c)
        k_me = 2 * x + y

        def half(k, cc):
            return out_ref.at[k, pl.ds(cc * HALF_ROWS, HALF_ROWS), :]

        mine = pltpu.make_async_copy(slab_ref, out_ref.at[k_me], local_sems.at[0])
        mine_small = pltpu.make_async_copy(small_ref, osm_ref.at[k_me], local_sems.at[1])
        mine.start()
        mine_small.start()
        my_half = slab_ref.at[pl.ds(c * HALF_ROWS, HALF_ROWS), :]
        first = [_remote(my_half, half(k_me, c), send_sems, recv_sems, j, (*chip, c)) for j, chip in enumerate(chips)]
        smalls = [_remote(small_ref, osm_ref.at[k_me], send_sems, recv_sems, 3 + j, (*chip, c)) for j, chip in enumerate(chips)]
        for cp in first + smalls:
            cp.start()
        passed = []
        for j, (cx, cy) in enumerate(chips):
            k = 2 * cx + cy
            _remote(half(k, c), half(k, c), send_sems, recv_sems, j, me).wait_recv()
            fwd = _remote(half(k, c), half(k, c), send_sems, recv_sems, 6 + j, sib)
            fwd.start()
            passed.append(fwd)
        for j, (cx, cy) in enumerate(chips):
            k = 2 * cx + cy
            _remote(half(k, 1 - c), half(k, 1 - c), send_sems, recv_sems, 6 + j, me).wait_recv()
            _remote(osm_ref.at[k], osm_ref.at[k], send_sems, recv_sems, 3 + j, me).wait_recv()
        for cp in first + smalls + passed:
            cp.wait_send()
        mine.wait()
        mine_small.wait()

    return pl.pallas_call(
        body, name="gather_weights", in_specs=[HBM_SPEC, HBM_SPEC], out_specs=[HBM_SPEC, HBM_SPEC],
        out_shape=[jax.ShapeDtypeStruct((4, SLAB_ROWS, D_MODEL), BF16), jax.ShapeDtypeStruct((4, SMALL_ROWS, D_MODEL), F32)],
        scratch_shapes=[pltpu.SemaphoreType.DMA((9,)), pltpu.SemaphoreType.DMA((9,)), pltpu.SemaphoreType.DMA((2,))],
    )(slab, small)


def _exchange_sibling(g):
    def body(g_ref, keep_ref, recv_ref, send_sems, recv_sems, local_sem):
        x, y, c, _ = _position()
        loc = pltpu.make_async_copy(g_ref.at[:, pl.ds(c * HALF_ROWS, HALF_ROWS), :], keep_ref, local_sem.at[0])
        loc.start()
        send = _remote(g_ref.at[:, pl.ds((1 - c) * HALF_ROWS, HALF_ROWS), :], recv_ref, send_sems, recv_sems, 0, (x, y, 1 - c))
        send.start()
        send.wait()
        loc.wait()

    shape = jax.ShapeDtypeStruct((4, HALF_ROWS, D_MODEL), F32)
    return pl.pallas_call(
        body, name="exchange_sibling", in_specs=[HBM_SPEC], out_specs=[HBM_SPEC, HBM_SPEC], out_shape=[shape, shape],
        scratch_shapes=[pltpu.SemaphoreType.DMA((1,)), pltpu.SemaphoreType.DMA((1,)), pltpu.SemaphoreType.DMA((1,))],
    )(g)


def _exchange_chips(p, vec):
    def body(p_ref, vec_ref, own_ref, recv_ref, vrecv_ref, send_sems, recv_sems, local_sems):
        x, y, c, chips = _position()
        k_me = 2 * x + y
        my_id = 4 * x + 2 * y + c
        loc = pltpu.make_async_copy(p_ref.at[k_me], own_ref, local_sems.at[0])
        vloc = pltpu.make_async_copy(vec_ref, vrecv_ref.at[my_id], local_sems.at[1])
        loc.start()
        vloc.start()
        sends = []
        for j, (cx, cy) in enumerate(chips):
            sends.append(_remote(p_ref.at[2 * cx + cy], recv_ref.at[j], send_sems, recv_sems, j, (cx, cy, c)))
        for r in range(1, 8):
            rx, ry, rc = (r >> 2) & 1, (r >> 1) & 1, r & 1
            peer = (1 - x if rx else x, 1 - y if ry else y, 1 - c if rc else c)
            sends.append(_remote(vec_ref, vrecv_ref.at[my_id], send_sems, recv_sems, 2 + r, peer))
        for cp in sends:
            cp.start()
        for j in range(3):
            _remote(recv_ref.at[j], recv_ref.at[j], send_sems, recv_sems, j, (x, y, c)).wait_recv()
        for r in range(1, 8):
            _remote(vec_ref, vrecv_ref.at[0], send_sems, recv_sems, 2 + r, (x, y, c)).wait_recv()
        for cp in sends:
            cp.wait_send()
        loc.wait()
        vloc.wait()

    return pl.pallas_call(
        body, name="exchange_chips", in_specs=[HBM_SPEC, HBM_SPEC], out_specs=[HBM_SPEC, HBM_SPEC, HBM_SPEC],
        out_shape=[jax.ShapeDtypeStruct((HALF_ROWS, D_MODEL), F32), jax.ShapeDtypeStruct((3, HALF_ROWS, D_MODEL), F32),
                   jax.ShapeDtypeStruct((8, VEC_ROWS, D_MODEL), F32)],
        scratch_shapes=[pltpu.SemaphoreType.DMA((10,)), pltpu.SemaphoreType.DMA((10,)), pltpu.SemaphoreType.DMA((2,))],
    )(p, vec)


def _reunite_sibling(s):
    def body(s_ref, full_ref, send_sems, recv_sems, local_sem):
        x, y, c, _ = _position()
        mine = full_ref.at[pl.ds(c * HALF_ROWS, HALF_ROWS), :]
        loc = pltpu.make_async_copy(s_ref, mine, local_sem.at[0])
        loc.start()
        send = _remote(s_ref, mine, send_sems, recv_sems, 0, (x, y, 1 - c))
        send.start()
        send.wait_send()
        theirs = full_ref.at[pl.ds((1 - c) * HALF_ROWS, HALF_ROWS), :]
        _remote(s_ref, theirs, send_sems, recv_sems, 0, (x, y, c)).wait_recv()
        loc.wait()

    return pl.pallas_call(
        body, name="reunite_sibling", in_specs=[HBM_SPEC], out_specs=HBM_SPEC,
        out_shape=jax.ShapeDtypeStruct((SLAB_ROWS, D_MODEL), F32),
        scratch_shapes=[pltpu.SemaphoreType.DMA((1,)), pltpu.SemaphoreType.DMA((1,)), pltpu.SemaphoreType.DMA((1,))],
    )(s)


def _add_n(parts, name, tr=400):
    rows = parts[0].shape[0]
    tr = min(tr, rows)

    def body(*refs):
        acc = refs[0][...]
        for r in refs[1:-1]:
            acc = acc + r[...]
        refs[-1][...] = acc

    spec = pl.BlockSpec((tr, D_MODEL), lambda i: (i, 0))
    return pl.pallas_call(
        body, name=name, grid=(rows // tr,), in_specs=[spec] * len(parts), out_specs=spec,
        out_shape=jax.ShapeDtypeStruct((rows, D_MODEL), F32), compiler_params=_cparams("parallel"),
    )(*parts)


def kernel(x, w_in, gate_up, gate_bias, gla_norm_g, w_out, ln1_g, ln1_b, w_up, conv_w, conv_b, w_down, ln2_g, ln2_b, loss_target, m_w_in, m_gate_up, m_gate_bias, m_gla_norm_g, m_w_out, m_ln1_g, m_ln1_b, m_w_up, m_conv_w, m_conv_b, m_w_down, m_ln2_g, m_ln2_b, v_w_in, v_gate_up, v_gate_bias, v_gla_norm_g, v_w_out, v_ln1_g, v_ln1_b, v_w_up, v_conv_w, v_conv_b, v_w_down, v_ln2_g, v_ln2_b):
    t = x.shape[1]
    slab = _pack_matrices(w_in[0], w_up[0], w_out[0], w_down[0]).astype(BF16)
    small = _pad_rows(jnp.concatenate([_rows(conv_w[0]), _rows(gate_up[0])], axis=0), SMALL_ROWS)
    gathered, gsmall = _gather_weights(slab, small)
    mats = [_unpack_matrices(gathered[k])[0] for k in range(4)]
    w_in_f = jnp.pad(jnp.concatenate([m[0] for m in mats], axis=1), ((0, 0), (0, IN_PAD - IN_WIDTH)))
    w_up_f = jnp.concatenate([m[1] for m in mats], axis=1)
    w_out_f = jnp.concatenate([m[2] for m in mats], axis=0)
    w_down_f = jnp.concatenate([m[3] for m in mats], axis=0)
    conv_w_f = jnp.concatenate([gsmall[k, :5].reshape(-1)[:3 * W_UP_S].reshape(3, W_UP_S) for k in range(4)], axis=1)
    gate_up_f = jnp.concatenate([gsmall[k, 5].reshape(GATE_RANK, GLA_KEYS // 4) for k in range(4)], axis=1)
    conv_w3 = jnp.transpose(conv_w_f.reshape(3, 2, D_FF), (1, 0, 2))
    conv_b3 = conv_b.reshape(2, 1, D_FF)
    gate_up_pad = jnp.pad(gate_up_f, ((0, LANES - GATE_RANK), (0, 0)))

    loss_part, dx, g = _local_step(x[0], loss_target[0], w_in_f, gate_up_pad, gate_bias, gla_norm_g, w_out_f, ln1_g, ln1_b,
                                   w_up_f, conv_w3, conv_b3, w_down_f, ln2_g, ln2_b)
    loss = lax.psum(loss_part[0, 0], ("x", "y", "c"))

    slabs = []
    for k in range(4):
        slabs.append(_pack_matrices(
            g["w_in"][:, k * W_IN_S:(k + 1) * W_IN_S], g["w_up"][:, k * W_UP_S:(k + 1) * W_UP_S],
            g["w_out"][k * W_OUT_S:(k + 1) * W_OUT_S], g["w_down"][k * W_DOWN_S:(k + 1) * W_DOWN_S],
            extra=(g["conv_w"][:, k * W_UP_S:(k + 1) * W_UP_S], g["gate_up"][:, k * 64:(k + 1) * 64])))
    gslab = jnp.stack(slabs, axis=0)
    keep, from_sib = _exchange_sibling(gslab)
    chip_sum = _add_n([keep.reshape(4 * HALF_ROWS, D_MODEL), from_sib.reshape(4 * HALF_ROWS, D_MODEL)], "add_sibling")
    own, others, vecs = _exchange_chips(chip_sum.reshape(4, HALF_ROWS, D_MODEL), _pack_vec(g))
    half_sum = _add_n([own, others[0], others[1], others[2]], "add_chips")
    gsum = _reunite_sibling(half_sum)
    gvec = _add_n([vecs[d] for d in range(8)], "add_vecs")

    (g_w_in, g_w_up, g_w_out, g_w_down), o = _unpack_matrices(gsum)
    g_conv_w = gsum[o:o + 5].reshape(-1)[:3 * W_UP_S].reshape(3, W_UP_S)
    g_gate_up = gsum[o + 5].reshape(GATE_RANK, GLA_KEYS // 4)
    gv = _unpack_vec(gvec)

    grads = dict(w_in=g_w_in[None], gate_up=g_gate_up[None], gate_bias=gv["gate_bias"], gla_norm_g=gv["gla_norm_g"],
                 w_out=g_w_out[None], ln1_g=gv["ln1_g"], ln1_b=gv["ln1_b"], w_up=g_w_up[None], conv_w=g_conv_w[None],
                 conv_b=gv["conv_b"], w_down=g_w_down[None], ln2_g=gv["ln2_g"], ln2_b=gv["ln2_b"])
    weights = dict(w_in=w_in, gate_up=gate_up, gate_bias=gate_bias, gla_norm_g=gla_norm_g, w_out=w_out, ln1_g=ln1_g,
                   ln1_b=ln1_b, w_up=w_up, conv_w=conv_w, conv_b=conv_b, w_down=w_down, ln2_g=ln2_g, ln2_b=ln2_b)
    ms = dict(w_in=m_w_in, gate_up=m_gate_up, gate_bias=m_gate_bias, gla_norm_g=m_gla_norm_g, w_out=m_w_out, ln1_g=m_ln1_g,
              ln1_b=m_ln1_b, w_up=m_w_up, conv_w=m_conv_w, conv_b=m_conv_b, w_down=m_w_down, ln2_g=m_ln2_g, ln2_b=m_ln2_b)
    vs = dict(w_in=v_w_in, gate_up=v_gate_up, gate_bias=v_gate_bias, gla_norm_g=v_gla_norm_g, w_out=v_w_out, ln1_g=v_ln1_g,
              ln1_b=v_ln1_b, w_up=v_w_up, conv_w=v_conv_w, conv_b=v_conv_b, w_down=v_w_down, ln2_g=v_ln2_g, ln2_b=v_ln2_b)
    names = ["w_in", "gate_up", "gate_bias", "gla_norm_g", "w_out", "ln1_g", "ln1_b", "w_up", "conv_w", "conv_b", "w_down",
             "ln2_g", "ln2_b"]
    delta, new_m, new_v = {}, {}, {}
    for n in ("w_in", "gate_up", "w_out", "w_up", "conv_w", "w_down"):
        d, nm, nv = _adamw(weights[n][0], grads[n][0], ms[n][0], vs[n][0], "adamw_" + n)
        delta[n], new_m[n], new_v[n] = d[None], nm[None], nv[None]
    d, nm, nv = _adamw(_pack_vec(weights), gvec, _pack_vec(ms), _pack_vec(vs), "adamw_vectors")
    for dst, src in ((delta, d), (new_m, nm), (new_v, nv)):
        dst.update(_unpack_vec(src))
    return (loss, dx[None], *[grads[n] for n in names], *[delta[n] for n in names], *[new_m[n] for n in names],
            *[new_v[n] for n in names])
```

```python
import functools
import math

import jax
import jax.numpy as jnp
from jax import lax
from jax.experimental import pallas as pl
from jax.experimental.pallas import tpu as pltpu

F32 = jnp.float32
BF16 = jnp.bfloat16

D_MODEL = 1024
SB_WIDTH = 512
GLA_KEYS = 256
GLA_WIDTH = 512
GATE_RANK = 16
IN_WIDTH = 3088
IN_PAD = 3200
D_FF = 2816
CHUNK = 64
LN_EPS = 1e-5
RMS_EPS = 1e-6
ALPHA = 2.0 ** 0.25
GLA_TAU = 16.0
SB_SCALE = 0.125
GLA_SCALE = 0.125
LANES = 128
SB_BLK = 256
SB_CUT = -100.0
GLA_ROWS = 256
VMEM_LIMIT = 56 * 1024 * 1024

ADAM_LR, ADAM_B1, ADAM_B2, ADAM_EPS, ADAM_WD, ADAM_STEP = 0.001, 0.9, 0.999, 1e-08, 0.01, 10

MESH = pl.DeviceIdType.MESH


def _cparams(*sem):
    return pltpu.CompilerParams(dimension_semantics=sem, vmem_limit_bytes=VMEM_LIMIT)


def _dot(a, b):
    return jnp.dot(a, b, preferred_element_type=F32)


def _dot_nt(a, b):
    return lax.dot_general(a, b, (((1,), (1,)), ((), ())), preferred_element_type=F32)


def _dot_tn(a, b):
    return lax.dot_general(a, b, (((0,), (0,)), ((), ())), preferred_element_type=F32)


def _split2(x):
    hi = x.astype(BF16)
    lo = (x - hi.astype(F32)).astype(BF16)
    return hi, lo


def _split3(x):
    hi = x.astype(BF16)
    r = x - hi.astype(F32)
    mid = r.astype(BF16)
    lo = (r - mid.astype(F32)).astype(BF16)
    return hi, mid, lo


def _softplus(z):
    return jnp.maximum(z, 0.0) + jnp.log(1.0 + jnp.exp(-jnp.abs(z)))


def _sigmoid(z):
    return 1.0 / (1.0 + jnp.exp(-z))


def _mm(a, b, *, m, n, k, tm, tn, tk, ta=False, tb=False, a_spec=None, b_spec=None, o_spec=None,
        out_shape=None, out_dtype=F32, add=None, add_scale=1.0, name):
    nk = k // tk
    dn = (((0 if ta else 1,), (1 if tb else 0,)), ((), ()))

    def body(*refs):
        a_ref, b_ref = refs[:2]
        add_ref = refs[2] if add is not None else None
        o_ref = refs[3 if add is not None else 2]
        part = lax.dot_general(a_ref[...].astype(BF16), b_ref[...].astype(BF16), dn, preferred_element_type=F32)

        def finish(r):
            if add is not None:
                r = r + add_scale * add_ref[...]
            o_ref[...] = r.astype(out_dtype)

        if nk == 1:
            finish(part)
            return
        acc_ref = refs[-1]
        kk = pl.program_id(2)

        @pl.when(kk == 0)
        def _():
            acc_ref[...] = part

        @pl.when((kk > 0) & (kk < nk - 1))
        def _():
            acc_ref[...] += part

        @pl.when(kk == nk - 1)
        def _():
            finish(acc_ref[...] + part)

    if a_spec is None:
        a_spec = pl.BlockSpec((tk, tm), lambda i, j, kk: (kk, i)) if ta else pl.BlockSpec((tm, tk), lambda i, j, kk: (i, kk))
    if b_spec is None:
        b_spec = pl.BlockSpec((tn, tk), lambda i, j, kk: (j, kk)) if tb else pl.BlockSpec((tk, tn), lambda i, j, kk: (kk, j))
    if o_spec is None:
        o_spec = pl.BlockSpec((tm, tn), lambda i, j, kk: (i, j))
    if out_shape is None:
        out_shape = jax.ShapeDtypeStruct((m, n), out_dtype)
    in_specs = [a_spec, b_spec]
    args = [a, b]
    if add is not None:
        in_specs.append(pl.BlockSpec((tm, tn), lambda i, j, kk: (i, j)))
        args.append(add)
    return pl.pallas_call(
        body, name=name, grid=(m // tm, n // tn, nk), in_specs=in_specs, out_specs=o_spec, out_shape=out_shape,
        scratch_shapes=[pltpu.VMEM((tm, tn), F32)] if nk > 1 else [],
        compiler_params=_cparams("parallel", "parallel", "arbitrary"),
    )(*args)


def _sb_tile(qh, kj, diag, strict, u_strict, r_in):
    z = _dot_nt(qh, kj)
    sp = _softplus(z)
    l1m = -sp
    lsz = z - sp
    if diag:
        l1m = jnp.where(strict, l1m, 0.0)
    hi, lo = _split2(l1m)
    cs = _dot(hi, u_strict) + _dot(lo, u_strict) + r_in
    w = jnp.exp(lsz + cs)
    if diag:
        w = jnp.where(strict, w, 0.0)
    return l1m, lsz, w


def _sb_consts():
    row = lax.broadcasted_iota(jnp.int32, (SB_BLK, SB_BLK), 0)
    col = lax.broadcasted_iota(jnp.int32, (SB_BLK, SB_BLK), 1)
    strict = col < row
    u_strict = (row > col).astype(BF16)
    u_pre = (row < col).astype(BF16)
    lane = lax.broadcasted_iota(jnp.int32, (1, LANES), 1)
    return strict, u_strict, u_pre, lane


def _sb_fwd(proj, t):
    nq = t // SB_BLK

    def body(q_ref, k_ref, v_ref, o_ref):
        i = pl.program_id(1)
        strict, u_strict, _, lane = _sb_consts()
        qf = q_ref[...] * SB_SCALE
        acc = jnp.zeros((SB_BLK, LANES), F32)
        for hh in range(2):
            hm = (lane // 64) == hh
            qh = jnp.where(hm, qf, 0.0).astype(BF16)

            def step(j, r, a, diag):
                rows = pl.ds(pl.multiple_of(j * SB_BLK, SB_BLK), SB_BLK)
                kj = k_ref[rows, :].astype(BF16)
                vj = jnp.where(hm, v_ref[rows, :], 0.0).astype(BF16)
                l1m, _, w = _sb_tile(qh, kj, diag, strict, u_strict, r)
                return r + jnp.sum(l1m, axis=1, keepdims=True), a + _dot(w.astype(BF16), vj)

            r, acc = step(i, jnp.zeros((SB_BLK, 1), F32), acc, True)
            _, _, acc = lax.while_loop(
                lambda c: (c[0] >= 0) & (jnp.max(c[1]) > SB_CUT),
                lambda c: (c[0] - 1,) + step(c[0], c[1], c[2], False),
                (i - 1, r, acc))
        o_ref[...] = acc.astype(BF16)

    return pl.pallas_call(
        body, name="sb_fwd", grid=(4, nq),
        in_specs=[pl.BlockSpec((SB_BLK, LANES), lambda p, i: (i, p)),
                  pl.BlockSpec((t, LANES), lambda p, i: (0, 4 + p)),
                  pl.BlockSpec((t, LANES), lambda p, i: (0, 8 + p))],
        out_specs=pl.BlockSpec((SB_BLK, LANES), lambda p, i: (i, p)),
        out_shape=jax.ShapeDtypeStruct((t, SB_WIDTH), BF16),
        compiler_params=_cparams("parallel", "parallel"),
    )(proj, proj, proj)


def _sb_bwd(proj, dcat, t):
    nq = t // SB_BLK

    def body(q_ref, k_ref, v_ref, do_ref, dq_ref, dk_ref, dv_ref, r_scr):
        i = pl.program_id(1)

        @pl.when(i == 0)
        def _():
            dk_ref[...] = jnp.zeros_like(dk_ref)
            dv_ref[...] = jnp.zeros_like(dv_ref)

        strict, u_strict, u_pre, lane = _sb_consts()
        qf = q_ref[...] * SB_SCALE
        dof = do_ref[...]
        dq = jnp.zeros((SB_BLK, LANES), F32)
        for hh in range(2):
            hm = (lane // 64) == hh
            qh = jnp.where(hm, qf, 0.0).astype(BF16)
            doh = jnp.where(hm, dof, 0.0).astype(BF16)

            def scan(j, r, diag):
                rows = pl.ds(pl.multiple_of(j * SB_BLK, SB_BLK), SB_BLK)
                l1m = -_softplus(_dot_nt(qh, k_ref[rows, :].astype(BF16)))
                if diag:
                    l1m = jnp.where(strict, l1m, 0.0)
                r_scr[j] = r
                return r + jnp.sum(l1m, axis=1, keepdims=True)

            r = scan(i, jnp.zeros((SB_BLK, 1), F32), True)
            jstop, _ = lax.while_loop(
                lambda c: (c[0] >= 0) & (jnp.max(c[1]) > SB_CUT),
                lambda c: (c[0] - 1, scan(c[0], c[1], False)),
                (i - 1, r))

            def step(j, carry, diag):
                pre_g, dqa = carry
                rows = pl.ds(pl.multiple_of(j * SB_BLK, SB_BLK), SB_BLK)
                kf = k_ref[rows, :]
                kj = kf.astype(BF16)
                kjm = jnp.where(hm, kf, 0.0).astype(BF16)
                vj = v_ref[rows, :].astype(BF16)
                _, lsz, w = _sb_tile(qh, kj, diag, strict, u_strict, r_scr[j])
                dw = _dot_nt(doh, vj)
                g = w * dw
                ghi, glo = _split2(g)
                gpre = _dot(ghi, u_pre) + _dot(glo, u_pre) + pre_g
                sig = jnp.exp(lsz)
                dz = g * (1.0 - sig) - gpre * sig
                if diag:
                    dz = jnp.where(strict, dz, 0.0)
                dzb = dz.astype(BF16)
                dv_ref[rows, :] += _dot_tn(w.astype(BF16), doh)
                dk_ref[rows, :] += _dot_tn(dzb, qh)
                dqa = dqa + _dot(dzb, kjm)
                return pre_g + jnp.sum(g, axis=1, keepdims=True), dqa

            carry = lax.fori_loop(jstop + 1, i, lambda j, c: step(j, c, False), (jnp.zeros((SB_BLK, 1), F32), dq))
            _, dq = step(i, carry, True)
        dq_ref[...] = dq * SB_SCALE

    return pl.pallas_call(
        body, name="sb_bwd", grid=(4, nq),
        in_specs=[pl.BlockSpec((SB_BLK, LANES), lambda p, i: (i, p)),
                  pl.BlockSpec((t, LANES), lambda p, i: (0, 4 + p)),
                  pl.BlockSpec((t, LANES), lambda p, i: (0, 8 + p)),
                  pl.BlockSpec((SB_BLK, LANES), lambda p, i: (i, p))],
        out_specs=[pl.BlockSpec((SB_BLK, LANES), lambda p, i: (i, p)),
                   pl.BlockSpec((t, LANES), lambda p, i: (0, p)),
                   pl.BlockSpec((t, LANES), lambda p, i: (0, p))],
        out_shape=[jax.ShapeDtypeStruct((t, SB_WIDTH), F32)] * 3,
        scratch_shapes=[pltpu.VMEM((nq, SB_BLK, 1), F32)],
        compiler_params=_cparams("parallel", "arbitrary"),
    )(proj, proj, proj, dcat)


def _gla_consts():
    r = lax.broadcasted_iota(jnp.int32, (CHUNK, CHUNK), 0)
    c = lax.broadcasted_iota(jnp.int32, (CHUNK, CHUNK), 1)
    causal = c <= r
    lc = causal.astype(BF16)
    lct = (c >= r).astype(BF16)
    rowid = lax.broadcasted_iota(jnp.int32, (CHUNK, 1), 0)
    lane = lax.broadcasted_iota(jnp.int32, (1, LANES), 1)
    sr = lax.broadcasted_iota(jnp.int32, (LANES, 2 * LANES), 0)
    sc = lax.broadcasted_iota(jnp.int32, (LANES, 2 * LANES), 1)
    blockdiag = (sr // 64) == (sc // LANES)
    return causal, lc, lct, rowid, lane, blockdiag


def _dot3(u, x):
    hi, mid, lo = _split3(x)
    return _dot(u, hi) + _dot(u, mid) + _dot(u, lo)


def _row_to_col(row):
    return jnp.transpose(jnp.broadcast_to(row, (LANES, LANES)))


def _gla_gates(ga_ref, gup_ref, gbias_ref):
    pre = _dot(ga_ref[...].astype(BF16), gup_ref[...].astype(BF16)) + gbias_ref[...]
    log_a = (jnp.minimum(pre, 0.0) - jnp.log(1.0 + jnp.exp(-jnp.abs(pre)))) / GLA_TAU
    return pre, log_a


def _gla_chunk_terms(g2, q2, k2, lc, rowid):
    b = _dot3(lc, g2)
    b_ref = jnp.sum(jnp.where(rowid == CHUNK // 2 - 1, b, 0.0), axis=0, keepdims=True)
    b_last = jnp.sum(jnp.where(rowid == CHUNK - 1, b, 0.0), axis=0, keepdims=True)
    qs = q2 * GLA_SCALE
    e_q = jnp.exp(b - b_ref)
    e_k = jnp.exp(b_ref - b)
    e_d = jnp.exp(b_last - b)
    e_b = jnp.exp(b)
    decay = jnp.exp(b_last)
    return dict(qs=qs, e_q=e_q, e_k=e_k, e_d=e_d, e_b=e_b, decay=decay,
                qi=qs * e_q, ki=k2 * e_k, kd=k2 * e_d, qb=qs * e_b)


def _gla_fwd(proj, gate_up_pad, gate_bias, gnorm, t):
    nsteps = t // GLA_ROWS
    cps = GLA_ROWS // CHUNK

    def body(q_ref, k_ref, v_ref, gg_ref, ga_ref, gup_ref, gbias_ref, gn_ref, o_ref, oraw_ref, st_ref, s_scr):
        i = pl.program_id(0)

        @pl.when(i == 0)
        def _():
            s_scr[...] = jnp.zeros_like(s_scr)

        causal, lc, _, rowid, lane, blockdiag = _gla_consts()
        _, log_a = _gla_gates(ga_ref, gup_ref, gbias_ref)
        gn = gn_ref[...]
        for cc in range(cps):
            rows = slice(cc * CHUNK, (cc + 1) * CHUNK)
            for p in range(2):
                kl = slice(p * LANES, (p + 1) * LANES)
                vl = slice(p * 2 * LANES, (p + 1) * 2 * LANES)
                tm = _gla_chunk_terms(log_a[rows, kl], q_ref[rows, kl], k_ref[rows, kl], lc, rowid)
                v2 = v_ref[rows, vl]
                v2b = v2.astype(BF16)
                s_prev = s_scr[p]
                st_ref[cc, p] = s_prev
                kib = tm["ki"].astype(BF16)
                o_inter = _dot(tm["qb"].astype(BF16), s_prev.astype(BF16))
                outs = []
                for hh in range(2):
                    hm = (lane // 64) == hh
                    a = _dot_nt(jnp.where(hm, tm["qi"], 0.0).astype(BF16), kib)
                    a = jnp.where(causal, a, 0.0)
                    outs.append(_dot(a.astype(BF16), v2b[:, hh * LANES:(hh + 1) * LANES]))
                o2 = jnp.concatenate(outs, axis=1) + o_inter
                upd = jnp.where(blockdiag, _dot_tn(tm["kd"].astype(BF16), v2b), 0.0)
                dcol = _row_to_col(tm["decay"])
                s_scr[p] = s_prev * jnp.concatenate([dcol, dcol], axis=1) + upd
                oraw_ref[rows, vl] = o2
                for hh in range(2):
                    hl = slice(hh * LANES, (hh + 1) * LANES)
                    oh = o2[:, hl]
                    gl = slice(p * 2 * LANES + hh * LANES, p * 2 * LANES + (hh + 1) * LANES)
                    rinv = lax.rsqrt(jnp.mean(oh * oh, axis=1, keepdims=True) + RMS_EPS)
                    gg = gg_ref[rows, gl]
                    o_ref[rows, gl] = (oh * rinv * gn * (gg * _sigmoid(gg))).astype(BF16)

    cb = lambda w, idx: pl.BlockSpec((GLA_ROWS, w), lambda i: (i, idx))
    full = lambda shp: pl.BlockSpec(shp, lambda i: tuple(0 for _ in shp))
    return pl.pallas_call(
        body, name="gla_fwd", grid=(nsteps,),
        in_specs=[cb(256, 6), cb(256, 7), cb(512, 4), cb(512, 5), cb(128, 24),
                  full((LANES, GLA_KEYS)), full((1, GLA_KEYS)), full((1, LANES))],
        out_specs=[pl.BlockSpec((GLA_ROWS, GLA_WIDTH), lambda i: (i, 0)),
                   pl.BlockSpec((GLA_ROWS, GLA_WIDTH), lambda i: (i, 0)),
                   pl.BlockSpec((cps, 2, LANES, 2 * LANES), lambda i: (i, 0, 0, 0))],
        out_shape=[jax.ShapeDtypeStruct((t, GLA_WIDTH), BF16), jax.ShapeDtypeStruct((t, GLA_WIDTH), F32),
                   jax.ShapeDtypeStruct((t // CHUNK, 2, LANES, 2 * LANES), F32)],
        scratch_shapes=[pltpu.VMEM((2, LANES, 2 * LANES), F32)],
        compiler_params=_cparams("arbitrary"),
    )(proj, proj, proj, proj, proj, gate_up_pad, gate_bias, gnorm)


def _gla_bwd(proj, dcat, oraw, states, gate_up_pad, gate_bias, gnorm, t):
    nsteps = t // GLA_ROWS
    cps = GLA_ROWS // CHUNK
    wout = 2 * GLA_KEYS + 2 * GLA_WIDTH + LANES

    def body(q_ref, k_ref, v_ref, gg_ref, ga_ref, do_ref, oraw_ref, st_ref, gup_ref, gbias_ref, gn_ref,
             d_ref, dgup_ref, dgbias_ref, dgn_ref, ds_scr, dpre_scr):
        i = pl.program_id(0)

        @pl.when(i == 0)
        def _():
            ds_scr[...] = jnp.zeros_like(ds_scr)
            dgup_ref[...] = jnp.zeros_like(dgup_ref)
            dgbias_ref[...] = jnp.zeros_like(dgbias_ref)
            dgn_ref[...] = jnp.zeros_like(dgn_ref)

        causal, lc, lct, rowid, lane, blockdiag = _gla_consts()
        pre, log_a = _gla_gates(ga_ref, gup_ref, gbias_ref)
        gn = gn_ref[...]
        dgn = jnp.zeros((1, LANES), F32)
        for cc in reversed(range(cps)):
            rows = slice(cc * CHUNK, (cc + 1) * CHUNK)
            for p in range(2):
                kl = slice(p * LANES, (p + 1) * LANES)
                vl = slice(p * 2 * LANES, (p + 1) * 2 * LANES)
                tm = _gla_chunk_terms(log_a[rows, kl], q_ref[rows, kl], k_ref[rows, kl], lc, rowid)
                v2b = v_ref[rows, vl].astype(BF16)
                s_prev = st_ref[cc, p]
                ds2 = ds_scr[p]
                dos = []
                for hh in range(2):
                    gl = slice(p * 2 * LANES + hh * LANES, p * 2 * LANES + (hh + 1) * LANES)
                    oh = oraw_ref[rows, gl]
                    rinv = lax.rsqrt(jnp.mean(oh * oh, axis=1, keepdims=True) + RMS_EPS)
                    on = oh * rinv
                    gg = gg_ref[rows, gl]
                    sg = _sigmoid(gg)
                    sil = gg * sg
                    dgo = do_ref[rows, gl]
                    d_ref[rows, 2 * GLA_KEYS + GLA_WIDTH + gl.start:2 * GLA_KEYS + GLA_WIDTH + gl.stop] = (
                        dgo * on * gn * (sg * (1.0 + gg * (1.0 - sg))))
                    dgn = dgn + jnp.sum(dgo * sil * on, axis=0, keepdims=True)
                    don = dgo * sil * gn
                    dos.append(rinv * (don - on * jnp.mean(don * on, axis=1, keepdims=True)))
                do2b = jnp.concatenate(dos, axis=1).astype(BF16)
                qib = tm["qi"].astype(BF16)
                kib = tm["ki"].astype(BF16)
                kdb = tm["kd"].astype(BF16)
                qbb = tm["qb"].astype(BF16)
                ds2b = ds2.astype(BF16)
                dqi = jnp.zeros((CHUNK, LANES), F32)
                dki = jnp.zeros((CHUNK, LANES), F32)
                dvs = []
                for hh in range(2):
                    hm = (lane // 64) == hh
                    hl = slice(hh * LANES, (hh + 1) * LANES)
                    a = jnp.where(causal, _dot_nt(jnp.where(hm, tm["qi"], 0.0).astype(BF16), kib), 0.0).astype(BF16)
                    da = jnp.where(causal, _dot_nt(do2b[:, hl], v2b[:, hl]), 0.0).astype(BF16)
                    dvs.append(_dot_tn(a, do2b[:, hl]))
                    dqi = dqi + jnp.where(hm, _dot(da, kib), 0.0)
                    dki = dki + jnp.where(hm, _dot_tn(da, qib), 0.0)
                dv2 = jnp.concatenate(dvs, axis=1) + _dot(kdb, ds2b)
                dqb = _dot_nt(do2b, s_prev.astype(BF16))
                dkd = _dot_nt(v2b, ds2b)
                dcol = _row_to_col(tm["decay"])
                dsp = jnp.where(blockdiag, _dot_tn(qbb, do2b), 0.0) + ds2 * jnp.concatenate([dcol, dcol], axis=1)
                ddecay_col = jnp.sum(ds2 * s_prev, axis=1, keepdims=True)
                ddecay_row = jnp.transpose(jnp.broadcast_to(ddecay_col, (LANES, LANES)))[0:1, :]
                ds_scr[p] = dsp
                dqs = dqi * tm["e_q"] + dqb * tm["e_b"]
                dk = dki * tm["e_k"] + dkd * tm["e_d"]
                t_qi = dqi * tm["qi"]
                t_ki = dki * tm["ki"]
                t_kd = dkd * tm["kd"]
                db = t_qi - t_ki + dqb * tm["qb"] - t_kd
                db_ref = jnp.sum(t_ki - t_qi, axis=0, keepdims=True)
                db_last = jnp.sum(t_kd, axis=0, keepdims=True) + ddecay_row * tm["decay"]
                db = db + jnp.where(rowid == CHUNK // 2 - 1, db_ref, 0.0) + jnp.where(rowid == CHUNK - 1, db_last, 0.0)
                dg = _dot3(lct, db)
                d_ref[rows, p * LANES:(p + 1) * LANES] = dqs * GLA_SCALE
                d_ref[rows, GLA_KEYS + p * LANES:GLA_KEYS + (p + 1) * LANES] = dk
                d_ref[rows, 2 * GLA_KEYS + p * 2 * LANES:2 * GLA_KEYS + (p + 1) * 2 * LANES] = dv2
                dpre_scr[rows, kl] = dg
        dpre = dpre_scr[...] * (1.0 / GLA_TAU) * _sigmoid(-pre)
        dpb = dpre.astype(BF16)
        dgn_ref[...] += dgn
        dgbias_ref[...] += jnp.sum(dpre, axis=0, keepdims=True)
        dgup_ref[...] += _dot_tn(ga_ref[...].astype(BF16), dpb)
        d_ref[:, 2 * GLA_KEYS + 2 * GLA_WIDTH:] = _dot_nt(dpb, gup_ref[...].astype(BF16))

    rev = lambda i: nsteps - 1 - i
    cb = lambda w, idx: pl.BlockSpec((GLA_ROWS, w), lambda i: (rev(i), idx))
    full = lambda shp: pl.BlockSpec(shp, lambda i: tuple(0 for _ in shp))
    return pl.pallas_call(
        body, name="gla_bwd", grid=(nsteps,),
        in_specs=[cb(256, 6), cb(256, 7), cb(512, 4), cb(512, 5), cb(128, 24), cb(512, 1), cb(512, 0),
                  pl.BlockSpec((cps, 2, LANES, 2 * LANES), lambda i: (rev(i), 0, 0, 0)),
                  full((LANES, GLA_KEYS)), full((1, GLA_KEYS)), full((1, LANES))],
        out_specs=[pl.BlockSpec((GLA_ROWS, wout), lambda i: (rev(i), 0)),
                   full((LANES, GLA_KEYS)), full((1, GLA_KEYS)), full((1, LANES))],
        out_shape=[jax.ShapeDtypeStruct((t, wout), F32), jax.ShapeDtypeStruct((LANES, GLA_KEYS), F32),
                   jax.ShapeDtypeStruct((1, GLA_KEYS), F32), jax.ShapeDtypeStruct((1, LANES), F32)],
        scratch_shapes=[pltpu.VMEM((2, LANES, 2 * LANES), F32), pltpu.VMEM((GLA_ROWS, GLA_KEYS), F32)],
        compiler_params=_cparams("arbitrary"),
    )(proj, proj, proj, proj, proj, dcat, oraw, states, gate_up_pad, gate_bias, gnorm)


def _ln_stats(r):
    mu = jnp.mean(r, axis=1, keepdims=True)
    xc = r - mu
    rstd = lax.rsqrt(jnp.mean(xc * xc, axis=1, keepdims=True) + LN_EPS)
    return xc * rstd, rstd


def _ln_bwd(dy_g, xhat, rstd):
    return rstd * (dy_g - jnp.mean(dy_g, axis=1, keepdims=True) - xhat * jnp.mean(dy_g * xhat, axis=1, keepdims=True))


def _outproj_ln1(sb_o, gla_o, x, w_out, g1, b1, t, tm=256):
    def body(sb_ref, gl_ref, x_ref, w_ref, g_ref, b_ref, xhat_ref, rstd_ref, h_ref):
        mix = _dot(sb_ref[...], w_ref[0:SB_WIDTH, :]) + _dot(gl_ref[...], w_ref[SB_WIDTH:, :])
        xhat, rstd = _ln_stats(ALPHA * x_ref[...] + mix)
        xhat_ref[...] = xhat
        rstd_ref[...] = rstd
        h_ref[...] = (xhat * g_ref[...] + b_ref[...]).astype(BF16)

    row = lambda w: pl.BlockSpec((tm, w), lambda i: (i, 0))
    full = lambda shp: pl.BlockSpec(shp, lambda i: (0, 0))
    return pl.pallas_call(
        body, name="outproj_ln1", grid=(t // tm,),
        in_specs=[row(SB_WIDTH), row(GLA_WIDTH), row(D_MODEL), full((D_MODEL, D_MODEL)), full((1, D_MODEL)), full((1, D_MODEL))],
        out_specs=[row(D_MODEL), row(1), row(D_MODEL)],
        out_shape=[jax.ShapeDtypeStruct((t, D_MODEL), F32), jax.ShapeDtypeStruct((t, 1), F32),
                   jax.ShapeDtypeStruct((t, D_MODEL), BF16)],
        compiler_params=_cparams("parallel"),
    )(sb_o, gla_o, x, w_out, g1, b1)


_INV_SQRT2 = 1.0 / math.sqrt(2.0)
_INV_SQRT2PI = 1.0 / math.sqrt(2.0 * math.pi)


def _conv3(xs, w_ref, b_ref, half):
    return (w_ref[half, 0:1, :] * pltpu.roll(xs, 2, 0) + w_ref[half, 1:2, :] * pltpu.roll(xs, 1, 0)
            + w_ref[half, 2:3, :] * xs + b_ref[half])


def _conv_gelu_fwd(up3, conv_w3, conv_b3, t, tr=512, ct=256):
    nct = D_FF // ct
    hb = tr // 8

    def body(cur_ref, prev_ref, w_ref, b_ref, gm_ref):
        i = pl.program_id(1)
        keep = (i > 0).astype(F32)
        us = []
        for half in range(2):
            xs = jnp.concatenate([prev_ref[half] * keep, cur_ref[half]], axis=0)
            us.append(_conv3(xs, w_ref, b_ref, half)[8:, :])
        a, c = us
        gelu = 0.5 * a * (1.0 + lax.erf(a * _INV_SQRT2))
        gm_ref[...] = (gelu * c).astype(BF16)

    return pl.pallas_call(
        body, name="conv_gelu_fwd", grid=(nct, t // tr),
        in_specs=[pl.BlockSpec((2, tr, ct), lambda j, i: (0, i, j)),
                  pl.BlockSpec((2, 8, ct), lambda j, i: (0, jnp.maximum(i * hb - 1, 0), j)),
                  pl.BlockSpec((2, 3, ct), lambda j, i: (0, 0, j)),
                  pl.BlockSpec((2, 1, ct), lambda j, i: (0, 0, j))],
        out_specs=pl.BlockSpec((tr, ct), lambda j, i: (i, j)),
        out_shape=jax.ShapeDtypeStruct((t, D_FF), BF16),
        compiler_params=_cparams("parallel", "parallel"),
    )(up3, up3, conv_w3, conv_b3)


def _conv_gelu_bwd(up3, dgm, conv_w3, conv_b3, t, tr=512, ct=256):
    nct = D_FF // ct
    nrt = t // tr
    hb = tr // 8
    n = tr + 16

    def body(cur_ref, prev_ref, next_ref, dg_ref, dgn_ref, w_ref, b_ref, dup_ref, dcw_ref, dcb_ref):
        i = pl.program_id(1)

        @pl.when(i == 0)
        def _():
            dcw_ref[...] = jnp.zeros_like(dcw_ref)
            dcb_ref[...] = jnp.zeros_like(dcb_ref)

        keep_prev = (i > 0).astype(F32)
        keep_next = (i < nrt - 1).astype(F32)
        xs, xm1, xm2, us = [], [], [], []
        for half in range(2):
            x = jnp.concatenate([prev_ref[half] * keep_prev, cur_ref[half], next_ref[half]], axis=0)
            xs.append(x)
            xm1.append(pltpu.roll(x, 1, 0))
            xm2.append(pltpu.roll(x, 2, 0))
            us.append(w_ref[half, 0:1, :] * xm2[half] + w_ref[half, 1:2, :] * xm1[half]
                      + w_ref[half, 2:3, :] * x + b_ref[half])
        a, c = us
        dg = jnp.concatenate([jnp.zeros((8, ct), F32), dg_ref[...], dgn_ref[...] * keep_next], axis=0)
        cdf = 0.5 * (1.0 + lax.erf(a * _INV_SQRT2))
        pdf = jnp.exp(-0.5 * a * a) * _INV_SQRT2PI
        dus = [dg * c * (cdf + a * pdf), dg * (a * cdf)]
        rid = lax.broadcasted_iota(jnp.int32, (8, 1), 0)
        for half in range(2):
            du = dus[half]
            dup = (w_ref[half, 2:3, :] * du + w_ref[half, 1:2, :] * pltpu.roll(du, n - 1, 0)
                   + w_ref[half, 0:1, :] * pltpu.roll(du, n - 2, 0))
            dup_ref[half] = dup[8:tr + 8, :].astype(BF16)
            duc = du[8:tr + 8, :]
            s0 = jnp.sum(duc * xm2[half][8:tr + 8, :], axis=0, keepdims=True)
            s1 = jnp.sum(duc * xm1[half][8:tr + 8, :], axis=0, keepdims=True)
            s2 = jnp.sum(duc * xs[half][8:tr + 8, :], axis=0, keepdims=True)
            dcw_ref[half] += jnp.where(rid == 0, s0, jnp.where(rid == 1, s1, jnp.where(rid == 2, s2, 0.0)))
            dcb_ref[half] += jnp.sum(duc, axis=0, keepdims=True)

    last8 = t // 8 - 1
    return pl.pallas_call(
        body, name="conv_gelu_bwd", grid=(nct, nrt),
        in_specs=[pl.BlockSpec((2, tr, ct), lambda j, i: (0, i, j)),
                  pl.BlockSpec((2, 8, ct), lambda j, i: (0, jnp.maximum(i * hb - 1, 0), j)),
                  pl.BlockSpec((2, 8, ct), lambda j, i: (0, jnp.minimum((i + 1) * hb, last8), j)),
                  pl.BlockSpec((tr, ct), lambda j, i: (i, j)),
                  pl.BlockSpec((8, ct), lambda j, i: (jnp.minimum((i + 1) * hb, last8), j)),
                  pl.BlockSpec((2, 3, ct), lambda j, i: (0, 0, j)),
                  pl.BlockSpec((2, 1, ct), lambda j, i: (0, 0, j))],
        out_specs=[pl.BlockSpec((2, tr, ct), lambda j, i: (0, i, j)),
                   pl.BlockSpec((2, 8, ct), lambda j, i: (0, 0, j)),
                   pl.BlockSpec((2, 1, ct), lambda j, i: (0, 0, j))],
        out_shape=[jax.ShapeDtypeStruct((2, t, D_FF), BF16), jax.ShapeDtypeStruct((2, 8, D_FF), F32),
                   jax.ShapeDtypeStruct((2, 1, D_FF), F32)],
        compiler_params=_cparams("parallel", "arbitrary"),
    )(up3, up3, up3, dgm, dgm, conv_w3, conv_b3)


def _down_ln2_loss(gm, w_down, xhat1, g1, b1, g2, b2, target, t, tm=256):
    def body(gm_ref, w_ref, xh_ref, g1_ref, b1_ref, g2_ref, b2_ref, tg_ref, dr_ref, loss_ref, dg_ref, db_ref):
        i = pl.program_id(0)

        @pl.when(i == 0)
        def _():
            loss_ref[...] = jnp.zeros_like(loss_ref)
            dg_ref[...] = jnp.zeros_like(dg_ref)
            db_ref[...] = jnp.zeros_like(db_ref)

        h = xh_ref[...] * g1_ref[...] + b1_ref[...]
        xhat, rstd = _ln_stats(ALPHA * h + _dot(gm_ref[...], w_ref[...]))
        err = xhat * g2_ref[...] + b2_ref[...] - tg_ref[...]
        loss_ref[...] += 0.5 * jnp.sum(jnp.sum(err * err, axis=1, keepdims=True), axis=0, keepdims=True) / D_MODEL
        dy = err * (1.0 / D_MODEL)
        dg_ref[...] += jnp.sum(dy * xhat, axis=0, keepdims=True)
        db_ref[...] += jnp.sum(dy, axis=0, keepdims=True)
        dr_ref[...] = _ln_bwd(dy * g2_ref[...], xhat, rstd)

    row = lambda w: pl.BlockSpec((tm, w), lambda i: (i, 0))
    full = lambda shp: pl.BlockSpec(shp, lambda i: (0, 0))
    vec = full((1, D_MODEL))
    return pl.pallas_call(
        body, name="down_ln2_loss", grid=(t // tm,),
        in_specs=[row(D_FF), full((D_FF, D_MODEL)), row(D_MODEL), vec, vec, vec, vec, row(D_MODEL)],
        out_specs=[row(D_MODEL), full((1, 1)), vec, vec],
        out_shape=[jax.ShapeDtypeStruct((t, D_MODEL), F32), jax.ShapeDtypeStruct((1, 1), F32),
                   jax.ShapeDtypeStruct((1, D_MODEL), F32), jax.ShapeDtypeStruct((1, D_MODEL), F32)],
        compiler_params=_cparams("arbitrary"),
    )(gm, w_down, xhat1, g1, b1, g2, b2, target)


def _dh_ln1_bwd(dup3, w_up4, dr2, xhat1, rstd1, g1, t, tm=512):
    tk = 2 * D_FF // 4
    nkh = D_FF // tk
    nk = 2 * nkh

    def body(a_ref, w_ref, dr2_ref, xh_ref, rs_ref, g_ref, dr1_ref, dg_ref, db_ref, acc_ref):
        i = pl.program_id(0)
        kk = pl.program_id(1)

        @pl.when((i == 0) & (kk == 0))
        def _():
            dg_ref[...] = jnp.zeros_like(dg_ref)
            db_ref[...] = jnp.zeros_like(db_ref)

        part = _dot_nt(a_ref[...], w_ref[...])

        @pl.when(kk == 0)
        def _():
            acc_ref[...] = part

        @pl.when(kk > 0)
        def _():
            acc_ref[...] += part

        @pl.when(kk == nk - 1)
        def _():
            dh = acc_ref[...] + ALPHA * dr2_ref[...]
            xhat = xh_ref[...]
            dg_ref[...] += jnp.sum(dh * xhat, axis=0, keepdims=True)
            db_ref[...] += jnp.sum(dh, axis=0, keepdims=True)
            dr1_ref[...] = _ln_bwd(dh * g_ref[...], xhat, rs_ref[...])

    row = lambda w: pl.BlockSpec((tm, w), lambda i, kk: (i, 0))
    vec = pl.BlockSpec((1, D_MODEL), lambda i, kk: (0, 0))
    return pl.pallas_call(
        body, name="dh_ln1_bwd", grid=(t // tm, nk),
        in_specs=[pl.BlockSpec((None, tm, tk), lambda i, kk: (kk // nkh, i, kk % nkh)),
                  pl.BlockSpec((None, D_MODEL, tk), lambda i, kk: (kk, 0, 0)),
                  row(D_MODEL), row(D_MODEL), row(1), vec],
        out_specs=[row(D_MODEL), vec, vec],
        out_shape=[jax.ShapeDtypeStruct((t, D_MODEL), F32), jax.ShapeDtypeStruct((1, D_MODEL), F32),
                   jax.ShapeDtypeStruct((1, D_MODEL), F32)],
        scratch_shapes=[pltpu.VMEM((tm, D_MODEL), F32)],
        compiler_params=_cparams("arbitrary", "arbitrary"),
    )(dup3, w_up4, dr2, xhat1, rstd1, g1)


def _adamw(w, g, m, v, name):
    rows, cols = w.shape
    tr = rows
    for cand in (256, 128, 64, 32, 16, 8):
        if rows % cand == 0 and rows > cand:
            tr = cand
            break
    c1 = 1.0 / (1.0 - ADAM_B1 ** ADAM_STEP)
    c2 = 1.0 / (1.0 - ADAM_B2 ** ADAM_STEP)

    def body(w_ref, g_ref, m_ref, v_ref, d_ref, nm_ref, nv_ref):
        gv = g_ref[...]
        nm = ADAM_B1 * m_ref[...] + (1.0 - ADAM_B1) * gv
        nv = ADAM_B2 * v_ref[...] + (1.0 - ADAM_B2) * (gv * gv)
        d_ref[...] = -ADAM_LR * ((nm * c1) / (jnp.sqrt(nv * c2) + ADAM_EPS) + ADAM_WD * w_ref[...])
        nm_ref[...] = nm
        nv_ref[...] = nv

    spec = pl.BlockSpec((tr, cols), lambda i: (i, 0))
    out = jax.ShapeDtypeStruct((rows, cols), F32)
    return pl.pallas_call(
        body, name=name, grid=(rows // tr,), in_specs=[spec] * 4, out_specs=[spec] * 3, out_shape=[out] * 3,
        compiler_params=_cparams("parallel"),
    )(w, g, m, v)


def _local_step(x, target, w_in_p, gate_up_pad, gate_bias, gnorm, w_out, ln1_g, ln1_b, w_up4, conv_w3, conv_b3,
                w_down, ln2_g, ln2_b):
    t = x.shape[0]
    tq = min(t, 1024)
    proj = _mm(x, w_in_p, m=t, n=IN_PAD, k=D_MODEL, tm=tq, tn=640, tk=D_MODEL, name="proj")
    sb_o = _sb_fwd(proj, t)
    gla_o, oraw, states = _gla_fwd(proj, gate_up_pad, gate_bias, gnorm, t)
    xhat1, rstd1, h_bf = _outproj_ln1(sb_o, gla_o, x, w_out, ln1_g, ln1_b, t)
    up3 = _mm(h_bf, w_up4, m=t, n=2 * D_FF, k=D_MODEL, tm=512, tn=W_UP_S, tk=D_MODEL, name="up",
              b_spec=pl.BlockSpec((None, D_MODEL, W_UP_S), lambda i, j, kk: (j, 0, 0)),
              o_spec=pl.BlockSpec((None, 512, W_UP_S), lambda i, j, kk: (j // 2, i, j % 2)),
              out_shape=jax.ShapeDtypeStruct((2, t, D_FF), F32))
    gm = _conv_gelu_fwd(up3, conv_w3, conv_b3, t)
    dr2, loss, dln2_g, dln2_b = _down_ln2_loss(gm, w_down, xhat1, ln1_g, ln1_b, ln2_g, ln2_b, target, t)
    dgm = _mm(dr2, w_down, m=t, n=D_FF, k=D_MODEL, tm=512, tn=W_UP_S, tk=D_MODEL, tb=True, name="dgm")
    dw_down = _mm(gm, dr2, m=D_FF, n=D_MODEL, k=t, tm=W_UP_S, tn=D_MODEL, tk=512, ta=True, name="dw_down")
    dup3, dcw, dcb = _conv_gelu_bwd(up3, dgm, conv_w3, conv_b3, t)
    dr1, dln1_g, dln1_b = _dh_ln1_bwd(dup3, w_up4, dr2, xhat1, rstd1, ln1_g, t)
    dw_up4 = _mm(h_bf, dup3, m=D_MODEL, n=2 * D_FF, k=t, tm=512, tn=W_UP_S, tk=tq, ta=True, name="dw_up",
                 b_spec=pl.BlockSpec((None, tq, W_UP_S), lambda i, j, kk: (j // 2, kk, j % 2)),
                 o_spec=pl.BlockSpec((None, 512, W_UP_S), lambda i, j, kk: (j, i, 0)),
                 out_shape=jax.ShapeDtypeStruct((4, D_MODEL, W_UP_S), F32))
    dcat = _mm(dr1, w_out, m=t, n=D_MODEL, k=D_MODEL, tm=tq, tn=512, tk=D_MODEL, tb=True, name="dcat")
    dw_out_sb = _mm(sb_o, dr1, m=SB_WIDTH, n=D_MODEL, k=t, tm=512, tn=D_MODEL, tk=tq, ta=True, name="dw_out_sb")
    dw_out_gla = _mm(gla_o, dr1, m=GLA_WIDTH, n=D_MODEL, k=t, tm=512, tn=D_MODEL, tk=tq, ta=True, name="dw_out_gla")
    dsq, dsk, dsv = _sb_bwd(proj, dcat, t)
    dgla, dgup_pad, dgbias, dgnorm = _gla_bwd(proj, dcat, oraw, states, gate_up_pad, gate_bias, gnorm, t)
    dproj = jnp.concatenate([dsq, dsk, dsv, dgla], axis=1).astype(BF16)
    dx = _mm(dproj, w_in_p, m=t, n=D_MODEL, k=IN_PAD, tm=tq, tn=D_MODEL, tk=640, tb=True, add=dr1, add_scale=ALPHA, name="dx")
    dw_in_p = _mm(x, dproj, m=D_MODEL, n=IN_PAD, k=t, tm=512, tn=IN_PAD, tk=512, ta=True, name="dw_in")
    grads = dict(
        w_in=dw_in_p, gate_up=dgup_pad[:GATE_RANK], gate_bias=dgbias, gla_norm_g=dgnorm,
        w_out=jnp.concatenate([dw_out_sb, dw_out_gla], axis=0), ln1_g=dln1_g, ln1_b=dln1_b,
        w_up=dw_up4, conv_w=jnp.concatenate([dcw[0, :3], dcw[1, :3]], axis=1),
        conv_b=jnp.concatenate([dcb[0], dcb[1]], axis=1), w_down=dw_down, ln2_g=dln2_g, ln2_b=dln2_b)
    return loss, dx, grads


W_IN_S, W_UP_S, W_OUT_S, W_DOWN_S = IN_WIDTH // 4, 2 * D_FF // 4, D_MODEL // 4, D_FF // 4
SHARD_SHAPES = ((D_MODEL, W_IN_S), (D_MODEL, W_UP_S), (W_OUT_S, D_MODEL), (W_DOWN_S, D_MODEL))
ADD_ROWS = (256, 256, 128, 176)
N_MATS = 4
SMALL_ROWS = 8
VEC_SIZES = (("gate_bias", GLA_KEYS), ("gla_norm_g", LANES), ("ln1_g", D_MODEL), ("ln1_b", D_MODEL),
             ("conv_b", 2 * D_FF), ("ln2_g", D_MODEL), ("ln2_b", D_MODEL))
SMALL_GRADS = VEC_SIZES + (("conv_w", 3 * 2 * D_FF), ("gate_up", GATE_RANK * GLA_KEYS))


def _rows(a):
    flat = a.reshape(-1)
    pad = (-flat.shape[0]) % D_MODEL
    if pad:
        flat = jnp.pad(flat, (0, pad))
    return flat.reshape(-1, D_MODEL)


def _pad_rows(a, rows):
    return jnp.pad(a, ((0, rows - a.shape[0]), (0, 0)))


def _pack_vec(d, sizes, rows):
    flat = jnp.concatenate([d[n].reshape(-1) for n, _ in sizes])
    return _pad_rows(_rows(flat), rows)


def _unpack_vec(v, sizes):
    flat = v.reshape(-1)
    out, o = {}, 0
    for n, size in sizes:
        out[n] = flat[o:o + size].reshape(1, size)
        o += size
    return out


VEC_ROWS = 16
GRAD_VEC_ROWS = 32


HBM_SPEC = pl.BlockSpec(memory_space=pltpu.HBM)


def _position():
    x, y, c = lax.axis_index("x"), lax.axis_index("y"), lax.axis_index("c")
    chips = [(1 - x, y), (x, 1 - y), (1 - x, 1 - y)]
    return x, y, c, chips


def _remote(src, dst, send_sems, recv_sems, k, to):
    return pltpu.make_async_remote_copy(src_ref=src, dst_ref=dst, send_sem=send_sems.at[k], recv_sem=recv_sems.at[k],
                                        device_id=to, device_id_type=MESH)


def _gather_weights(shards, small):
    n_ici = 3 * N_MATS

    def body(*refs):
        in_refs, small_ref = refs[:N_MATS], refs[N_MATS]
        out_refs, osm_ref = refs[N_MATS + 1:2 * N_MATS + 1], refs[2 * N_MATS + 1]
        send_sems, recv_sems, local_sems = refs[2 * N_MATS + 2:]
        x, y, c, chips = _position()
        me, sib = (x, y, c), (x, y, 1 - c)
        k_me = 2 * x + y

        def half(m, k, cc):
            h = SHARD_SHAPES[m][0] // 2
            return out_refs[m].at[k, pl.ds(cc * h, h), :]

        local = [pltpu.make_async_copy(in_refs[m], out_refs[m].at[k_me], local_sems.at[m]) for m in range(N_MATS)]
        local.append(pltpu.make_async_copy(small_ref, osm_ref.at[k_me], local_sems.at[N_MATS]))
        for cp in local:
            cp.start()
        first = []
        for m in range(N_MATS):
            h = SHARD_SHAPES[m][0] // 2
            my_half = in_refs[m].at[pl.ds(c * h, h), :]
            first += [_remote(my_half, half(m, k_me, c), send_sems, recv_sems, 3 * m + j, (*chip, c))
                      for j, chip in enumerate(chips)]
        first += [_remote(small_ref, osm_ref.at[k_me], send_sems, recv_sems, n_ici + j, (*chip, c))
                  for j, chip in enumerate(chips)]
        for cp in first:
            cp.start()
        passed = []
        for j, (cx, cy) in enumerate(chips):
            k = 2 * cx + cy
            for m in range(N_MATS):
                _remote(half(m, k, c), half(m, k, c), send_sems, recv_sems, 3 * m + j, me).wait_recv()
                fwd = _remote(half(m, k, c), half(m, k, c), send_sems, recv_sems, n_ici + 3 + 3 * m + j, sib)
                fwd.start()
                passed.append(fwd)
        for j, (cx, cy) in enumerate(chips):
            k = 2 * cx + cy
            for m in range(N_MATS):
                _remote(half(m, k, 1 - c), half(m, k, 1 - c), send_sems, recv_sems, n_ici + 3 + 3 * m + j, me).wait_recv()
            _remote(osm_ref.at[k], osm_ref.at[k], send_sems, recv_sems, n_ici + j, me).wait_recv()
        for cp in first + passed:
            cp.wait_send()
        for cp in local:
            cp.wait()

    n_sems = 2 * n_ici + 3
    return pl.pallas_call(
        body, name="gather_weights", in_specs=[HBM_SPEC] * (N_MATS + 1), out_specs=[HBM_SPEC] * (N_MATS + 1),
        out_shape=[jax.ShapeDtypeStruct((4,) + s, BF16) for s in SHARD_SHAPES]
        + [jax.ShapeDtypeStruct((4, SMALL_ROWS, D_MODEL), F32)],
        scratch_shapes=[pltpu.SemaphoreType.DMA((n_sems,)), pltpu.SemaphoreType.DMA((n_sems,)),
                        pltpu.SemaphoreType.DMA((N_MATS + 1,))],
    )(*shards, small)


def _exchange_sibling(gs):
    def body(*refs):
        g_refs, r_refs = refs[:N_MATS], refs[N_MATS:2 * N_MATS]
        send_sems, recv_sems = refs[2 * N_MATS:]
        x, y, c, _ = _position()
        copies = []
        for m in range(N_MATS):
            h = SHARD_SHAPES[m][0] // 2
            for k in range(4):
                copies.append(_remote(g_refs[m].at[k, pl.ds((1 - c) * h, h), :], r_refs[m].at[k], send_sems, recv_sems,
                                      4 * m + k, (x, y, 1 - c)))
        for cp in copies:
            cp.start()
        for cp in copies:
            cp.wait()

    return pl.pallas_call(
        body, name="exchange_sibling", in_specs=[HBM_SPEC] * N_MATS, out_specs=[HBM_SPEC] * N_MATS,
        out_shape=[jax.ShapeDtypeStruct((4, r // 2, cl), F32) for r, cl in SHARD_SHAPES],
        scratch_shapes=[pltpu.SemaphoreType.DMA((4 * N_MATS,)), pltpu.SemaphoreType.DMA((4 * N_MATS,))],
    )(*gs)


def _add_sibling(g, r, c_arr, tr, name):
    _, rows, cols = g.shape
    nb = rows // 2 // tr

    def body(c_ref, g_ref, r_ref, o_ref):
        o_ref[...] = g_ref[...] + r_ref[...]

    spec = pl.BlockSpec((None, tr, cols), lambda k, i, c: (k, i, 0))
    return pl.pallas_call(
        body, name=name,
        grid_spec=pltpu.PrefetchScalarGridSpec(
            num_scalar_prefetch=1, grid=(4, nb),
            in_specs=[pl.BlockSpec((None, tr, cols), lambda k, i, c: (k, c[0] * nb + i, 0)), spec], out_specs=spec),
        out_shape=jax.ShapeDtypeStruct((4, rows // 2, cols), F32), compiler_params=_cparams("parallel", "parallel"),
    )(c_arr, g, r)


def _exchange_chips(ps, vec):
    n_ici = 3 * N_MATS

    def body(*refs):
        p_refs, vec_ref = refs[:N_MATS], refs[N_MATS]
        r_refs, vrecv_ref = refs[N_MATS + 1:2 * N_MATS + 1], refs[2 * N_MATS + 1]
        send_sems, recv_sems, local_sem = refs[2 * N_MATS + 2:]
        x, y, c, chips = _position()
        my_id = 4 * x + 2 * y + c
        vloc = pltpu.make_async_copy(vec_ref, vrecv_ref.at[my_id], local_sem.at[0])
        vloc.start()
        sends = []
        for m in range(N_MATS):
            for j, (cx, cy) in enumerate(chips):
                sends.append(_remote(p_refs[m].at[2 * cx + cy], r_refs[m].at[j], send_sems, recv_sems, 3 * m + j, (cx, cy, c)))
        for r in range(1, 8):
            peer = (1 - x if r & 4 else x, 1 - y if r & 2 else y, 1 - c if r & 1 else c)
            sends.append(_remote(vec_ref, vrecv_ref.at[my_id], send_sems, recv_sems, n_ici + r - 1, peer))
        for cp in sends:
            cp.start()
        for m in range(N_MATS):
            for j in range(3):
                _remote(r_refs[m].at[j], r_refs[m].at[j], send_sems, recv_sems, 3 * m + j, (x, y, c)).wait_recv()
        for r in range(1, 8):
            _remote(vec_ref, vrecv_ref.at[0], send_sems, recv_sems, n_ici + r - 1, (x, y, c)).wait_recv()
        for cp in sends:
            cp.wait_send()
        vloc.wait()

    n_sems = n_ici + 7
    return pl.pallas_call(
        body, name="exchange_chips", in_specs=[HBM_SPEC] * (N_MATS + 1), out_specs=[HBM_SPEC] * (N_MATS + 1),
        out_shape=[jax.ShapeDtypeStruct((3, r // 2, cl), F32) for r, cl in SHARD_SHAPES]
        + [jax.ShapeDtypeStruct((8, GRAD_VEC_ROWS, D_MODEL), F32)],
        scratch_shapes=[pltpu.SemaphoreType.DMA((n_sems,)), pltpu.SemaphoreType.DMA((n_sems,)), pltpu.SemaphoreType.DMA((1,))],
    )(*ps, vec)


def _add_chips(p, r, kc_arr, tr, name):
    _, h, cols = p.shape
    nb = h // tr

    def body(kc_ref, p_ref, r0_ref, r1_ref, r2_ref, o_ref):
        o_ref[...] = ((p_ref[...] + r0_ref[...]) + r1_ref[...]) + r2_ref[...]

    rspec = lambda j: pl.BlockSpec((None, tr, cols), lambda i, kc: (j, i, 0))
    return pl.pallas_call(
        body, name=name,
        grid_spec=pltpu.PrefetchScalarGridSpec(
            num_scalar_prefetch=1, grid=(nb,),
            in_specs=[pl.BlockSpec((None, tr, cols), lambda i, kc: (kc[0], i, 0)), rspec(0), rspec(1), rspec(2)],
            out_specs=pl.BlockSpec((tr, cols), lambda i, kc: (kc[1] * nb + i, 0))),
        out_shape=jax.ShapeDtypeStruct((2 * h, cols), F32), compiler_params=_cparams("parallel"),
    )(kc_arr, p, r, r, r)


def _reunite_sibling(fs):
    n_chunks = 2

    def body(*refs):
        in_refs, out_refs = refs[:N_MATS], refs[N_MATS:2 * N_MATS]
        send_sems, recv_sems = refs[2 * N_MATS:]
        x, y, c, _ = _position()
        sends, recvs = [], []
        for m in range(N_MATS):
            ch = SHARD_SHAPES[m][0] // 2 // n_chunks
            for q in range(n_chunks):
                mine = pl.ds((c * n_chunks + q) * ch, ch)
                theirs = pl.ds(((1 - c) * n_chunks + q) * ch, ch)
                s = n_chunks * m + q
                sends.append(_remote(in_refs[m].at[mine, :], out_refs[m].at[mine, :], send_sems, recv_sems, s, (x, y, 1 - c)))
                recvs.append(_remote(in_refs[m].at[theirs, :], out_refs[m].at[theirs, :], send_sems, recv_sems, s, (x, y, c)))
        for cp in sends:
            cp.start()
        for cp in recvs:
            cp.wait_recv()
        for cp in sends:
            cp.wait_send()

    n_sems = n_chunks * N_MATS
    return pl.pallas_call(
        body, name="reunite_sibling", in_specs=[HBM_SPEC] * N_MATS, out_specs=[HBM_SPEC] * N_MATS,
        out_shape=[jax.ShapeDtypeStruct(s, F32) for s in SHARD_SHAPES],
        input_output_aliases={m: m for m in range(N_MATS)},
        scratch_shapes=[pltpu.SemaphoreType.DMA((n_sems,)), pltpu.SemaphoreType.DMA((n_sems,))],
    )(*fs)


def _sum_vecs(v):
    def body(v_ref, o_ref):
        acc = v_ref[0]
        for d in range(1, 8):
            acc = acc + v_ref[d]
        o_ref[...] = acc

    return pl.pallas_call(body, name="sum_vecs", out_shape=jax.ShapeDtypeStruct(v.shape[1:], F32))(v)


def kernel(x, w_in, gate_up, gate_bias, gla_norm_g, w_out, ln1_g, ln1_b, w_up, conv_w, conv_b, w_down, ln2_g, ln2_b, loss_target, m_w_in, m_gate_up, m_gate_bias, m_gla_norm_g, m_w_out, m_ln1_g, m_ln1_b, m_w_up, m_conv_w, m_conv_b, m_w_down, m_ln2_g, m_ln2_b, v_w_in, v_gate_up, v_gate_bias, v_gla_norm_g, v_w_out, v_ln1_g, v_ln1_b, v_w_up, v_conv_w, v_conv_b, v_w_down, v_ln2_g, v_ln2_b):
    xi, yi, ci = lax.axis_index("x"), lax.axis_index("y"), lax.axis_index("c")
    k_me = 2 * xi + yi
    shards = [w_in[0].astype(BF16), w_up[0].astype(BF16), w_out[0].astype(BF16), w_down[0].astype(BF16)]
    small = _pad_rows(jnp.concatenate([_rows(conv_w[0]), _rows(gate_up[0])], axis=0), SMALL_ROWS)
    w_in4, w_up4, w_out4, w_down4, gsmall = _gather_weights(shards, small)
    w_in_p = jnp.pad(jnp.concatenate([w_in4[k] for k in range(4)], axis=1), ((0, 0), (0, IN_PAD - IN_WIDTH)))
    conv_w_f = jnp.concatenate([gsmall[k, :5].reshape(-1)[:3 * W_UP_S].reshape(3, W_UP_S) for k in range(4)], axis=1)
    gate_up_f = jnp.concatenate([gsmall[k, 5].reshape(GATE_RANK, GLA_KEYS // 4) for k in range(4)], axis=1)
    conv_w3 = jnp.transpose(conv_w_f.reshape(3, 2, D_FF), (1, 0, 2))
    conv_b3 = conv_b.reshape(2, 1, D_FF)
    gate_up_pad = jnp.pad(gate_up_f, ((0, LANES - GATE_RANK), (0, 0)))

    loss_part, dx, g = _local_step(x[0], loss_target[0], w_in_p, gate_up_pad, gate_bias, gla_norm_g,
                                   w_out4.reshape(D_MODEL, D_MODEL), ln1_g, ln1_b, w_up4, conv_w3, conv_b3,
                                   w_down4.reshape(D_FF, D_MODEL), ln2_g, ln2_b)
    loss = lax.psum(loss_part[0, 0], ("x", "y", "c"))

    gs = [jnp.stack([g["w_in"][:, k * W_IN_S:(k + 1) * W_IN_S] for k in range(4)], axis=0), g["w_up"],
          g["w_out"].reshape(4, W_OUT_S, D_MODEL), g["w_down"].reshape(4, W_DOWN_S, D_MODEL)]
    c_arr = ci.astype(jnp.int32).reshape(1)
    kc_arr = jnp.stack([k_me, ci]).astype(jnp.int32)
    from_sib = _exchange_sibling(gs)
    ps = [_add_sibling(gs[m], from_sib[m], c_arr, ADD_ROWS[m], "add_sibling_%d" % m) for m in range(N_MATS)]
    *others, vecs = _exchange_chips(ps, _pack_vec(g, SMALL_GRADS, GRAD_VEC_ROWS))
    halves = [_add_chips(ps[m], others[m], kc_arr, ADD_ROWS[m], "add_chips_%d" % m) for m in range(N_MATS)]
    g_w_in, g_w_up, g_w_out, g_w_down = _reunite_sibling(halves)
    gsmall_sum = _unpack_vec(_sum_vecs(vecs), SMALL_GRADS)
    g_conv_w = lax.dynamic_slice_in_dim(gsmall_sum["conv_w"].reshape(3, 2 * D_FF), k_me * W_UP_S, W_UP_S, axis=1)
    g_gate_up = lax.dynamic_slice_in_dim(gsmall_sum["gate_up"].reshape(GATE_RANK, GLA_KEYS), k_me * (GLA_KEYS // 4),
                                         GLA_KEYS // 4, axis=1)
    gv = gsmall_sum
    gvec = _pack_vec(gv, VEC_SIZES, VEC_ROWS)

    grads = dict(w_in=g_w_in[None], gate_up=g_gate_up[None], gate_bias=gv["gate_bias"], gla_norm_g=gv["gla_norm_g"],
                 w_out=g_w_out[None], ln1_g=gv["ln1_g"], ln1_b=gv["ln1_b"], w_up=g_w_up[None], conv_w=g_conv_w[None],
                 conv_b=gv["conv_b"], w_down=g_w_down[None], ln2_g=gv["ln2_g"], ln2_b=gv["ln2_b"])
    weights = dict(w_in=w_in, gate_up=gate_up, gate_bias=gate_bias, gla_norm_g=gla_norm_g, w_out=w_out, ln1_g=ln1_g,
                   ln1_b=ln1_b, w_up=w_up, conv_w=conv_w, conv_b=conv_b, w_down=w_down, ln2_g=ln2_g, ln2_b=ln2_b)
    ms = dict(w_in=m_w_in, gate_up=m_gate_up, gate_bias=m_gate_bias, gla_norm_g=m_gla_norm_g, w_out=m_w_out, ln1_g=m_ln1_g,
              ln1_b=m_ln1_b, w_up=m_w_up, conv_w=m_conv_w, conv_b=m_conv_b, w_down=m_w_down, ln2_g=m_ln2_g, ln2_b=m_ln2_b)
    vs = dict(w_in=v_w_in, gate_up=v_gate_up, gate_bias=v_gate_bias, gla_norm_g=v_gla_norm_g, w_out=v_w_out, ln1_g=v_ln1_g,
              ln1_b=v_ln1_b, w_up=v_w_up, conv_w=v_conv_w, conv_b=v_conv_b, w_down=v_w_down, ln2_g=v_ln2_g, ln2_b=v_ln2_b)
    names = ["w_in", "gate_up", "gate_bias", "gla_norm_g", "w_out", "ln1_g", "ln1_b", "w_up", "conv_w", "conv_b", "w_down",
             "ln2_g", "ln2_b"]
    delta, new_m, new_v = {}, {}, {}
    for n in ("w_in", "gate_up", "w_out", "w_up", "conv_w", "w_down"):
        d, nm, nv = _adamw(weights[n][0], grads[n][0], ms[n][0], vs[n][0], "adamw_" + n)
        delta[n], new_m[n], new_v[n] = d[None], nm[None], nv[None]
    d, nm, nv = _adamw(_pack_vec(weights, VEC_SIZES, VEC_ROWS), gvec, _pack_vec(ms, VEC_SIZES, VEC_ROWS),
                       _pack_vec(vs, VEC_SIZES, VEC_ROWS), "adamw_vectors")
    for dst, src in ((delta, d), (new_m, nm), (new_v, nv)):
        dst.update(_unpack_vec(src, VEC_SIZES))
    return (loss, dx[None], *[grads[n] for n in names], *[delta[n] for n in names], *[new_m[n] for n in names],
            *[new_v[n] for n in names])
```

```python
import functools
import math

import jax
import jax.numpy as jnp
from jax import lax
from jax.experimental import pallas as pl
from jax.experimental.pallas import tpu as pltpu

F32 = jnp.float32
BF16 = jnp.bfloat16

D_MODEL = 1024
SB_WIDTH = 512
GLA_KEYS = 256
GLA_WIDTH = 512
GATE_RANK = 16
IN_WIDTH = 3088
IN_PAD = 3200
D_FF = 2816
CHUNK = 64
LN_EPS = 1e-5
RMS_EPS = 1e-6
ALPHA = 2.0 ** 0.25
GLA_TAU = 16.0
SB_SCALE = 0.125
GLA_SCALE = 0.125
LANES = 128
SB_BLK = 256
SB_CUT = -100.0
GLA_ROWS = 256
VMEM_LIMIT = 56 * 1024 * 1024

ADAM_LR, ADAM_B1, ADAM_B2, ADAM_EPS, ADAM_WD, ADAM_STEP = 0.001, 0.9, 0.999, 1e-08, 0.01, 10

MESH = pl.DeviceIdType.MESH


def _cparams(*sem):
    return pltpu.CompilerParams(dimension_semantics=sem, vmem_limit_bytes=VMEM_LIMIT)


def _dot(a, b):
    return jnp.dot(a, b, preferred_element_type=F32)


def _dot_nt(a, b):
    return lax.dot_general(a, b, (((1,), (1,)), ((), ())), preferred_element_type=F32)


def _dot_tn(a, b):
    return lax.dot_general(a, b, (((0,), (0,)), ((), ())), preferred_element_type=F32)


def _split2(x):
    hi = x.astype(BF16)
    lo = (x - hi.astype(F32)).astype(BF16)
    return hi, lo


def _split3(x):
    hi = x.astype(BF16)
    r = x - hi.astype(F32)
    mid = r.astype(BF16)
    lo = (r - mid.astype(F32)).astype(BF16)
    return hi, mid, lo


def _softplus(z):
    return jnp.maximum(z, 0.0) + jnp.log(1.0 + jnp.exp(-jnp.abs(z)))


def _sigmoid(z):
    return 1.0 / (1.0 + jnp.exp(-z))


def _mm(a, b, *, m, n, k, tm, tn, tk, ta=False, tb=False, a_spec=None, b_spec=None, o_spec=None,
        out_shape=None, out_dtype=F32, add=None, add_scale=1.0, name):
    nk = k // tk
    dn = (((0 if ta else 1,), (1 if tb else 0,)), ((), ()))

    def body(*refs):
        a_ref, b_ref = refs[:2]
        add_ref = refs[2] if add is not None else None
        o_ref = refs[3 if add is not None else 2]
        part = lax.dot_general(a_ref[...].astype(BF16), b_ref[...].astype(BF16), dn, preferred_element_type=F32)

        def finish(r):
            if add is not None:
                r = r + add_scale * add_ref[...]
            o_ref[...] = r.astype(out_dtype)

        if nk == 1:
            finish(part)
            return
        acc_ref = refs[-1]
        kk = pl.program_id(2)

        @pl.when(kk == 0)
        def _():
            acc_ref[...] = part

        @pl.when((kk > 0) & (kk < nk - 1))
        def _():
            acc_ref[...] += part

        @pl.when(kk == nk - 1)
        def _():
            finish(acc_ref[...] + part)

    if a_spec is None:
        a_spec = pl.BlockSpec((tk, tm), lambda i, j, kk: (kk, i)) if ta else pl.BlockSpec((tm, tk), lambda i, j, kk: (i, kk))
    if b_spec is None:
        b_spec = pl.BlockSpec((tn, tk), lambda i, j, kk: (j, kk)) if tb else pl.BlockSpec((tk, tn), lambda i, j, kk: (kk, j))
    if o_spec is None:
        o_spec = pl.BlockSpec((tm, tn), lambda i, j, kk: (i, j))
    if out_shape is None:
        out_shape = jax.ShapeDtypeStruct((m, n), out_dtype)
    in_specs = [a_spec, b_spec]
    args = [a, b]
    if add is not None:
        in_specs.append(pl.BlockSpec((tm, tn), lambda i, j, kk: (i, j)))
        args.append(add)
    return pl.pallas_call(
        body, name=name, grid=(m // tm, n // tn, nk), in_specs=in_specs, out_specs=o_spec, out_shape=out_shape,
        scratch_shapes=[pltpu.VMEM((tm, tn), F32)] if nk > 1 else [],
        compiler_params=_cparams("parallel", "parallel", "arbitrary"),
    )(*args)


def _sb_tile(qh, kj, diag, strict, u_strict, r_in):
    z = _dot_nt(qh, kj)
    sp = _softplus(z)
    l1m = -sp
    lsz = z - sp
    if diag:
        l1m = jnp.where(strict, l1m, 0.0)
    hi, lo = _split2(l1m)
    cs = _dot(hi, u_strict) + _dot(lo, u_strict) + r_in
    w = jnp.exp(lsz + cs)
    if diag:
        w = jnp.where(strict, w, 0.0)
    return l1m, lsz, w


def _sb_consts():
    row = lax.broadcasted_iota(jnp.int32, (SB_BLK, SB_BLK), 0)
    col = lax.broadcasted_iota(jnp.int32, (SB_BLK, SB_BLK), 1)
    strict = col < row
    u_strict = (row > col).astype(BF16)
    u_pre = (row < col).astype(BF16)
    lane = lax.broadcasted_iota(jnp.int32, (1, LANES), 1)
    return strict, u_strict, u_pre, lane


def _sb_fwd(proj, t, shards, shapes):
    nq = t // SB_BLK

    nm = len(shards)

    def body(q_ref, k_ref, v_ref, *rest):
        w_refs, o_ref, g_refs = rest[:nm], rest[nm], rest[nm + 1:2 * nm + 1]
        sems = rest[2 * nm + 1:]
        p = pl.program_id(0)
        i = pl.program_id(1)

        @pl.when((p == 0) & (i == 0))
        def _():
            local, sends, _ = _gather_ici(w_refs, g_refs, shapes, *sems)
            for cp in local + sends:
                cp.start()

        @pl.when((p == 3) & (i == nq - 1))
        def _():
            local, sends, recvs = _gather_ici(w_refs, g_refs, shapes, *sems)
            for cp in recvs:
                cp.wait_recv()
            for cp in sends:
                cp.wait_send()
            for cp in local:
                cp.wait()

        strict, u_strict, _, lane = _sb_consts()
        qf = q_ref[...] * SB_SCALE
        acc = jnp.zeros((SB_BLK, LANES), F32)
        for hh in range(2):
            hm = (lane // 64) == hh
            qh = jnp.where(hm, qf, 0.0).astype(BF16)

            def step(j, r, a, diag):
                rows = pl.ds(pl.multiple_of(j * SB_BLK, SB_BLK), SB_BLK)
                kj = k_ref[rows, :].astype(BF16)
                vj = jnp.where(hm, v_ref[rows, :], 0.0).astype(BF16)
                l1m, _, w = _sb_tile(qh, kj, diag, strict, u_strict, r)
                return r + jnp.sum(l1m, axis=1, keepdims=True), a + _dot(w.astype(BF16), vj)

            r, acc = step(i, jnp.zeros((SB_BLK, 1), F32), acc, True)
            _, _, acc = lax.while_loop(
                lambda c: (c[0] >= 0) & (jnp.max(c[1]) > SB_CUT),
                lambda c: (c[0] - 1,) + step(c[0], c[1], c[2], False),
                (i - 1, r, acc))
        o_ref[...] = acc.astype(BF16)

    return pl.pallas_call(
        body, name="sb_fwd", grid=(4, nq),
        in_specs=[pl.BlockSpec((SB_BLK, LANES), lambda p, i: (i, p)),
                  pl.BlockSpec((t, LANES), lambda p, i: (0, 4 + p)),
                  pl.BlockSpec((t, LANES), lambda p, i: (0, 8 + p))] + [HBM_SPEC] * nm,
        out_specs=[pl.BlockSpec((SB_BLK, LANES), lambda p, i: (i, p))] + [HBM_SPEC] * nm,
        out_shape=[jax.ShapeDtypeStruct((t, SB_WIDTH), BF16)] + [jax.ShapeDtypeStruct((4,) + s, BF16) for s in shapes],
        scratch_shapes=[pltpu.SemaphoreType.DMA((3 * nm,)), pltpu.SemaphoreType.DMA((3 * nm,)),
                        pltpu.SemaphoreType.DMA((nm,))],
        compiler_params=_cparams("arbitrary", "arbitrary"),
    )(proj, proj, proj, *shards)


def _sb_bwd(proj, dcat, t, ps):
    nq = t // SB_BLK
    nm = len(ps)

    def body(q_ref, k_ref, v_ref, do_ref, *rest):
        p_refs = rest[:nm]
        dq_ref, dk_ref, dv_ref = rest[nm:nm + 3]
        got_refs = rest[nm + 3:2 * nm + 3]
        r_scr, send_sems, recv_sems = rest[2 * nm + 3:]
        p = pl.program_id(0)
        i = pl.program_id(1)

        @pl.when((p == 0) & (i == 0))
        def _():
            sends, _ = _reduce_ici(p_refs, got_refs, send_sems, recv_sems)
            for cp in sends:
                cp.start()

        @pl.when((p == 3) & (i == nq - 1))
        def _():
            sends, recvs = _reduce_ici(p_refs, got_refs, send_sems, recv_sems)
            for cp in recvs:
                cp.wait_recv()
            for cp in sends:
                cp.wait_send()

        @pl.when(i == 0)
        def _():
            dk_ref[...] = jnp.zeros_like(dk_ref)
            dv_ref[...] = jnp.zeros_like(dv_ref)

        strict, u_strict, u_pre, lane = _sb_consts()
        qf = q_ref[...] * SB_SCALE
        dof = do_ref[...]
        dq = jnp.zeros((SB_BLK, LANES), F32)
        for hh in range(2):
            hm = (lane // 64) == hh
            qh = jnp.where(hm, qf, 0.0).astype(BF16)
            doh = jnp.where(hm, dof, 0.0).astype(BF16)

            def scan(j, r, diag):
                rows = pl.ds(pl.multiple_of(j * SB_BLK, SB_BLK), SB_BLK)
                l1m = -_softplus(_dot_nt(qh, k_ref[rows, :].astype(BF16)))
                if diag:
                    l1m = jnp.where(strict, l1m, 0.0)
                r_scr[j] = r
                return r + jnp.sum(l1m, axis=1, keepdims=True)

            r = scan(i, jnp.zeros((SB_BLK, 1), F32), True)
            jstop, _ = lax.while_loop(
                lambda c: (c[0] >= 0) & (jnp.max(c[1]) > SB_CUT),
                lambda c: (c[0] - 1, scan(c[0], c[1], False)),
                (i - 1, r))

            def step(j, carry, diag):
                pre_g, dqa = carry
                rows = pl.ds(pl.multiple_of(j * SB_BLK, SB_BLK), SB_BLK)
                kf = k_ref[rows, :]
                kj = kf.astype(BF16)
                kjm = jnp.where(hm, kf, 0.0).astype(BF16)
                vj = v_ref[rows, :].astype(BF16)
                _, lsz, w = _sb_tile(qh, kj, diag, strict, u_strict, r_scr[j])
                dw = _dot_nt(doh, vj)
                g = w * dw
                ghi, glo = _split2(g)
                gpre = _dot(ghi, u_pre) + _dot(glo, u_pre) + pre_g
                sig = jnp.exp(lsz)
                dz = g * (1.0 - sig) - gpre * sig
                if diag:
                    dz = jnp.where(strict, dz, 0.0)
                dzb = dz.astype(BF16)
                dv_ref[rows, :] += _dot_tn(w.astype(BF16), doh)
                dk_ref[rows, :] += _dot_tn(dzb, qh)
                dqa = dqa + _dot(dzb, kjm)
                return pre_g + jnp.sum(g, axis=1, keepdims=True), dqa

            carry = lax.fori_loop(jstop + 1, i, lambda j, c: step(j, c, False), (jnp.zeros((SB_BLK, 1), F32), dq))
            _, dq = step(i, carry, True)
        dq_ref[...] = dq * SB_SCALE

    return pl.pallas_call(
        body, name="sb_bwd", grid=(4, nq),
        in_specs=[pl.BlockSpec((SB_BLK, LANES), lambda p, i: (i, p)),
                  pl.BlockSpec((t, LANES), lambda p, i: (0, 4 + p)),
                  pl.BlockSpec((t, LANES), lambda p, i: (0, 8 + p)),
                  pl.BlockSpec((SB_BLK, LANES), lambda p, i: (i, p))] + [HBM_SPEC] * nm,
        out_specs=[pl.BlockSpec((SB_BLK, LANES), lambda p, i: (i, p)),
                   pl.BlockSpec((t, LANES), lambda p, i: (0, p)),
                   pl.BlockSpec((t, LANES), lambda p, i: (0, p))] + [HBM_SPEC] * nm,
        out_shape=[jax.ShapeDtypeStruct((t, SB_WIDTH), F32)] * 3
        + [jax.ShapeDtypeStruct((3,) + a.shape[1:], a.dtype) for a in ps],
        scratch_shapes=[pltpu.VMEM((nq, SB_BLK, 1), F32), pltpu.SemaphoreType.DMA((3 * nm,)),
                        pltpu.SemaphoreType.DMA((3 * nm,))],
        compiler_params=_cparams("arbitrary", "arbitrary"),
    )(proj, proj, proj, dcat, *ps)


def _gla_consts():
    r = lax.broadcasted_iota(jnp.int32, (CHUNK, CHUNK), 0)
    c = lax.broadcasted_iota(jnp.int32, (CHUNK, CHUNK), 1)
    causal = c <= r
    lc = causal.astype(BF16)
    lct = (c >= r).astype(BF16)
    rowid = lax.broadcasted_iota(jnp.int32, (CHUNK, 1), 0)
    lane = lax.broadcasted_iota(jnp.int32, (1, LANES), 1)
    sr = lax.broadcasted_iota(jnp.int32, (LANES, 2 * LANES), 0)
    sc = lax.broadcasted_iota(jnp.int32, (LANES, 2 * LANES), 1)
    blockdiag = (sr // 64) == (sc // LANES)
    return causal, lc, lct, rowid, lane, blockdiag


def _dot3(u, x):
    hi, mid, lo = _split3(x)
    return _dot(u, hi) + _dot(u, mid) + _dot(u, lo)


def _row_to_col(row):
    return jnp.transpose(jnp.broadcast_to(row, (LANES, LANES)))


def _gla_gates(ga_ref, gup_ref, gbias_ref):
    pre = _dot(ga_ref[...].astype(BF16), gup_ref[...].astype(BF16)) + gbias_ref[...]
    log_a = (jnp.minimum(pre, 0.0) - jnp.log(1.0 + jnp.exp(-jnp.abs(pre)))) / GLA_TAU
    return pre, log_a


def _gla_chunk_terms(g2, q2, k2, lc, rowid):
    b = _dot3(lc, g2)
    b_ref = jnp.sum(jnp.where(rowid == CHUNK // 2 - 1, b, 0.0), axis=0, keepdims=True)
    b_last = jnp.sum(jnp.where(rowid == CHUNK - 1, b, 0.0), axis=0, keepdims=True)
    qs = q2 * GLA_SCALE
    e_q = jnp.exp(b - b_ref)
    e_k = jnp.exp(b_ref - b)
    e_d = jnp.exp(b_last - b)
    e_b = jnp.exp(b)
    decay = jnp.exp(b_last)
    return dict(qs=qs, e_q=e_q, e_k=e_k, e_d=e_d, e_b=e_b, decay=decay,
                qi=qs * e_q, ki=k2 * e_k, kd=k2 * e_d, qb=qs * e_b)


def _gla_fwd(proj, gate_up_pad, gate_bias, gnorm, t, gathered, shapes):
    nsteps = t // GLA_ROWS
    cps = GLA_ROWS // CHUNK
    nm = len(gathered)

    def body(q_ref, k_ref, v_ref, gg_ref, ga_ref, gup_ref, gbias_ref, gn_ref, *rest):
        gin_refs = rest[:nm]
        o_ref, oraw_ref, st_ref = rest[nm:nm + 3]
        gout_refs = rest[nm + 3:2 * nm + 3]
        s_scr, send_sems, recv_sems = rest[2 * nm + 3:]
        i = pl.program_id(0)

        @pl.when(i == 0)
        def _():
            sends, _ = _gather_d2d(gin_refs, gout_refs, shapes, send_sems, recv_sems, 0)
            for cp in sends:
                cp.start()

        @pl.when(i == nsteps - 1)
        def _():
            sends, recvs = _gather_d2d(gin_refs, gout_refs, shapes, send_sems, recv_sems, 0)
            for cp in recvs:
                cp.wait_recv()
            for cp in sends:
                cp.wait_send()

        @pl.when(i == 0)
        def _():
            s_scr[...] = jnp.zeros_like(s_scr)

        causal, lc, _, rowid, lane, blockdiag = _gla_consts()
        _, log_a = _gla_gates(ga_ref, gup_ref, gbias_ref)
        gn = gn_ref[...]
        for cc in range(cps):
            rows = slice(cc * CHUNK, (cc + 1) * CHUNK)
            for p in range(2):
                kl = slice(p * LANES, (p + 1) * LANES)
                vl = slice(p * 2 * LANES, (p + 1) * 2 * LANES)
                tm = _gla_chunk_terms(log_a[rows, kl], q_ref[rows, kl], k_ref[rows, kl], lc, rowid)
                v2 = v_ref[rows, vl]
                v2b = v2.astype(BF16)
                s_prev = s_scr[p]
                st_ref[cc, p] = s_prev
                kib = tm["ki"].astype(BF16)
                o_inter = _dot(tm["qb"].astype(BF16), s_prev.astype(BF16))
                outs = []
                for hh in range(2):
                    hm = (lane // 64) == hh
                    a = _dot_nt(jnp.where(hm, tm["qi"], 0.0).astype(BF16), kib)
                    a = jnp.where(causal, a, 0.0)
                    outs.append(_dot(a.astype(BF16), v2b[:, hh * LANES:(hh + 1) * LANES]))
                o2 = jnp.concatenate(outs, axis=1) + o_inter
                upd = jnp.where(blockdiag, _dot_tn(tm["kd"].astype(BF16), v2b), 0.0)
                dcol = _row_to_col(tm["decay"])
                s_scr[p] = s_prev * jnp.concatenate([dcol, dcol], axis=1) + upd
                oraw_ref[rows, vl] = o2
                for hh in range(2):
                    hl = slice(hh * LANES, (hh + 1) * LANES)
                    oh = o2[:, hl]
                    gl = slice(p * 2 * LANES + hh * LANES, p * 2 * LANES + (hh + 1) * LANES)
                    rinv = lax.rsqrt(jnp.mean(oh * oh, axis=1, keepdims=True) + RMS_EPS)
                    gg = gg_ref[rows, gl]
                    o_ref[rows, gl] = (oh * rinv * gn * (gg * _sigmoid(gg))).astype(BF16)

    cb = lambda w, idx: pl.BlockSpec((GLA_ROWS, w), lambda i: (i, idx))
    full = lambda shp: pl.BlockSpec(shp, lambda i: tuple(0 for _ in shp))
    return pl.pallas_call(
        body, name="gla_fwd", grid=(nsteps,),
        in_specs=[cb(256, 6), cb(256, 7), cb(512, 4), cb(512, 5), cb(128, 24),
                  full((LANES, GLA_KEYS)), full((1, GLA_KEYS)), full((1, LANES))] + [HBM_SPEC] * nm,
        out_specs=[pl.BlockSpec((GLA_ROWS, GLA_WIDTH), lambda i: (i, 0)),
                   pl.BlockSpec((GLA_ROWS, GLA_WIDTH), lambda i: (i, 0)),
                   pl.BlockSpec((cps, 2, LANES, 2 * LANES), lambda i: (i, 0, 0, 0))] + [HBM_SPEC] * nm,
        out_shape=[jax.ShapeDtypeStruct((t, GLA_WIDTH), BF16), jax.ShapeDtypeStruct((t, GLA_WIDTH), F32),
                   jax.ShapeDtypeStruct((t // CHUNK, 2, LANES, 2 * LANES), F32)]
        + [jax.ShapeDtypeStruct(a.shape, a.dtype) for a in gathered],
        input_output_aliases={8 + m: 3 + m for m in range(nm)},
        scratch_shapes=[pltpu.VMEM((2, LANES, 2 * LANES), F32), pltpu.SemaphoreType.DMA((3 * nm,)),
                        pltpu.SemaphoreType.DMA((3 * nm,))],
        compiler_params=_cparams("arbitrary"),
    )(proj, proj, proj, proj, proj, gate_up_pad, gate_bias, gnorm, *gathered)


def _gla_bwd(proj, dcat, oraw, states, gate_up_pad, gate_bias, gnorm, t):
    nsteps = t // GLA_ROWS
    cps = GLA_ROWS // CHUNK
    wout = 2 * GLA_KEYS + 2 * GLA_WIDTH + LANES

    def body(q_ref, k_ref, v_ref, gg_ref, ga_ref, do_ref, oraw_ref, st_ref, gup_ref, gbias_ref, gn_ref,
             d_ref, dgup_ref, dgbias_ref, dgn_ref, ds_scr, dpre_scr):
        i = pl.program_id(0)

        @pl.when(i == 0)
        def _():
            ds_scr[...] = jnp.zeros_like(ds_scr)
            dgup_ref[...] = jnp.zeros_like(dgup_ref)
            dgbias_ref[...] = jnp.zeros_like(dgbias_ref)
            dgn_ref[...] = jnp.zeros_like(dgn_ref)

        causal, lc, lct, rowid, lane, blockdiag = _gla_consts()
        pre, log_a = _gla_gates(ga_ref, gup_ref, gbias_ref)
        gn = gn_ref[...]
        dgn = jnp.zeros((1, LANES), F32)
        for cc in reversed(range(cps)):
            rows = slice(cc * CHUNK, (cc + 1) * CHUNK)
            for p in range(2):
                kl = slice(p * LANES, (p + 1) * LANES)
                vl = slice(p * 2 * LANES, (p + 1) * 2 * LANES)
                tm = _gla_chunk_terms(log_a[rows, kl], q_ref[rows, kl], k_ref[rows, kl], lc, rowid)
                v2b = v_ref[rows, vl].astype(BF16)
                s_prev = st_ref[cc, p]
                ds2 = ds_scr[p]
                dos = []
                for hh in range(2):
                    gl = slice(p * 2 * LANES + hh * LANES, p * 2 * LANES + (hh + 1) * LANES)
                    oh = oraw_ref[rows, gl]
                    rinv = lax.rsqrt(jnp.mean(oh * oh, axis=1, keepdims=True) + RMS_EPS)
                    on = oh * rinv
                    gg = gg_ref[rows, gl]
                    sg = _sigmoid(gg)
                    sil = gg * sg
                    dgo = do_ref[rows, gl]
                    d_ref[rows, 2 * GLA_KEYS + GLA_WIDTH + gl.start:2 * GLA_KEYS + GLA_WIDTH + gl.stop] = (
                        dgo * on * gn * (sg * (1.0 + gg * (1.0 - sg))))
                    dgn = dgn + jnp.sum(dgo * sil * on, axis=0, keepdims=True)
                    don = dgo * sil * gn
                    dos.append(rinv * (don - on * jnp.mean(don * on, axis=1, keepdims=True)))
                do2b = jnp.concatenate(dos, axis=1).astype(BF16)
                qib = tm["qi"].astype(BF16)
                kib = tm["ki"].astype(BF16)
                kdb = tm["kd"].astype(BF16)
                qbb = tm["qb"].astype(BF16)
                ds2b = ds2.astype(BF16)
                dqi = jnp.zeros((CHUNK, LANES), F32)
                dki = jnp.zeros((CHUNK, LANES), F32)
                dvs = []
                for hh in range(2):
                    hm = (lane // 64) == hh
                    hl = slice(hh * LANES, (hh + 1) * LANES)
                    a = jnp.where(causal, _dot_nt(jnp.where(hm, tm["qi"], 0.0).astype(BF16), kib), 0.0).astype(BF16)
                    da = jnp.where(causal, _dot_nt(do2b[:, hl], v2b[:, hl]), 0.0).astype(BF16)
                    dvs.append(_dot_tn(a, do2b[:, hl]))
                    dqi = dqi + jnp.where(hm, _dot(da, kib), 0.0)
                    dki = dki + jnp.where(hm, _dot_tn(da, qib), 0.0)
                dv2 = jnp.concatenate(dvs, axis=1) + _dot(kdb, ds2b)
                dqb = _dot_nt(do2b, s_prev.astype(BF16))
                dkd = _dot_nt(v2b, ds2b)
                dcol = _row_to_col(tm["decay"])
                dsp = jnp.where(blockdiag, _dot_tn(qbb, do2b), 0.0) + ds2 * jnp.concatenate([dcol, dcol], axis=1)
                ddecay_col = jnp.sum(ds2 * s_prev, axis=1, keepdims=True)
                ddecay_row = jnp.transpose(jnp.broadcast_to(ddecay_col, (LANES, LANES)))[0:1, :]
                ds_scr[p] = dsp
                dqs = dqi * tm["e_q"] + dqb * tm["e_b"]
                dk = dki * tm["e_k"] + dkd * tm["e_d"]
                t_qi = dqi * tm["qi"]
                t_ki = dki * tm["ki"]
                t_kd = dkd * tm["kd"]
                db = t_qi - t_ki + dqb * tm["qb"] - t_kd
                db_ref = jnp.sum(t_ki - t_qi, axis=0, keepdims=True)
                db_last = jnp.sum(t_kd, axis=0, keepdims=True) + ddecay_row * tm["decay"]
                db = db + jnp.where(rowid == CHUNK // 2 - 1, db_ref, 0.0) + jnp.where(rowid == CHUNK - 1, db_last, 0.0)
                dg = _dot3(lct, db)
                d_ref[rows, p * LANES:(p + 1) * LANES] = dqs * GLA_SCALE
                d_ref[rows, GLA_KEYS + p * LANES:GLA_KEYS + (p + 1) * LANES] = dk
                d_ref[rows, 2 * GLA_KEYS + p * 2 * LANES:2 * GLA_KEYS + (p + 1) * 2 * LANES] = dv2
                dpre_scr[rows, kl] = dg
        dpre = dpre_scr[...] * (1.0 / GLA_TAU) * _sigmoid(-pre)
        dpb = dpre.astype(BF16)
        dgn_ref[...] += dgn
        dgbias_ref[...] += jnp.sum(dpre, axis=0, keepdims=True)
        dgup_ref[...] += _dot_tn(ga_ref[...].astype(BF16), dpb)
        d_ref[:, 2 * GLA_KEYS + 2 * GLA_WIDTH:] = _dot_nt(dpb, gup_ref[...].astype(BF16))

    rev = lambda i: nsteps - 1 - i
    cb = lambda w, idx: pl.BlockSpec((GLA_ROWS, w), lambda i: (rev(i), idx))
    full = lambda shp: pl.BlockSpec(shp, lambda i: tuple(0 for _ in shp))
    return pl.pallas_call(
        body, name="gla_bwd", grid=(nsteps,),
        in_specs=[cb(256, 6), cb(256, 7), cb(512, 4), cb(512, 5), cb(128, 24), cb(512, 1), cb(512, 0),
                  pl.BlockSpec((cps, 2, LANES, 2 * LANES), lambda i: (rev(i), 0, 0, 0)),
                  full((LANES, GLA_KEYS)), full((1, GLA_KEYS)), full((1, LANES))],
        out_specs=[pl.BlockSpec((GLA_ROWS, wout), lambda i: (rev(i), 0)),
                   full((LANES, GLA_KEYS)), full((1, GLA_KEYS)), full((1, LANES))],
        out_shape=[jax.ShapeDtypeStruct((t, wout), F32), jax.ShapeDtypeStruct((LANES, GLA_KEYS), F32),
                   jax.ShapeDtypeStruct((1, GLA_KEYS), F32), jax.ShapeDtypeStruct((1, LANES), F32)],
        scratch_shapes=[pltpu.VMEM((2, LANES, 2 * LANES), F32), pltpu.VMEM((GLA_ROWS, GLA_KEYS), F32)],
        compiler_params=_cparams("arbitrary"),
    )(proj, proj, proj, proj, proj, dcat, oraw, states, gate_up_pad, gate_bias, gnorm)


def _ln_stats(r):
    mu = jnp.mean(r, axis=1, keepdims=True)
    xc = r - mu
    rstd = lax.rsqrt(jnp.mean(xc * xc, axis=1, keepdims=True) + LN_EPS)
    return xc * rstd, rstd


def _ln_bwd(dy_g, xhat, rstd):
    return rstd * (dy_g - jnp.mean(dy_g, axis=1, keepdims=True) - xhat * jnp.mean(dy_g * xhat, axis=1, keepdims=True))


def _outproj_ln1(sb_o, gla_o, x, w_out, g1, b1, t, tm=256):
    def body(sb_ref, gl_ref, x_ref, w_ref, g_ref, b_ref, xhat_ref, rstd_ref, h_ref):
        mix = _dot(sb_ref[...], w_ref[0:SB_WIDTH, :]) + _dot(gl_ref[...], w_ref[SB_WIDTH:, :])
        xhat, rstd = _ln_stats(ALPHA * x_ref[...] + mix)
        xhat_ref[...] = xhat
        rstd_ref[...] = rstd
        h_ref[...] = (xhat * g_ref[...] + b_ref[...]).astype(BF16)

    row = lambda w: pl.BlockSpec((tm, w), lambda i: (i, 0))
    full = lambda shp: pl.BlockSpec(shp, lambda i: (0, 0))
    return pl.pallas_call(
        body, name="outproj_ln1", grid=(t // tm,),
        in_specs=[row(SB_WIDTH), row(GLA_WIDTH), row(D_MODEL), full((D_MODEL, D_MODEL)), full((1, D_MODEL)), full((1, D_MODEL))],
        out_specs=[row(D_MODEL), row(1), row(D_MODEL)],
        out_shape=[jax.ShapeDtypeStruct((t, D_MODEL), F32), jax.ShapeDtypeStruct((t, 1), F32),
                   jax.ShapeDtypeStruct((t, D_MODEL), BF16)],
        compiler_params=_cparams("parallel"),
    )(sb_o, gla_o, x, w_out, g1, b1)


_INV_SQRT2 = 1.0 / math.sqrt(2.0)
_INV_SQRT2PI = 1.0 / math.sqrt(2.0 * math.pi)


def _conv3(xs, w_ref, b_ref, half):
    return (w_ref[half, 0:1, :] * pltpu.roll(xs, 2, 0) + w_ref[half, 1:2, :] * pltpu.roll(xs, 1, 0)
            + w_ref[half, 2:3, :] * xs + b_ref[half])


def _conv_gelu_fwd(up3, conv_w3, conv_b3, t, tr=512, ct=256):
    nct = D_FF // ct
    hb = tr // 8

    def body(cur_ref, prev_ref, w_ref, b_ref, gm_ref):
        i = pl.program_id(1)
        keep = (i > 0).astype(F32)
        us = []
        for half in range(2):
            xs = jnp.concatenate([prev_ref[half] * keep, cur_ref[half]], axis=0)
            us.append(_conv3(xs, w_ref, b_ref, half)[8:, :])
        a, c = us
        gelu = 0.5 * a * (1.0 + lax.erf(a * _INV_SQRT2))
        gm_ref[...] = (gelu * c).astype(BF16)

    return pl.pallas_call(
        body, name="conv_gelu_fwd", grid=(nct, t // tr),
        in_specs=[pl.BlockSpec((2, tr, ct), lambda j, i: (0, i, j)),
                  pl.BlockSpec((2, 8, ct), lambda j, i: (0, jnp.maximum(i * hb - 1, 0), j)),
                  pl.BlockSpec((2, 3, ct), lambda j, i: (0, 0, j)),
                  pl.BlockSpec((2, 1, ct), lambda j, i: (0, 0, j))],
        out_specs=pl.BlockSpec((tr, ct), lambda j, i: (i, j)),
        out_shape=jax.ShapeDtypeStruct((t, D_FF), BF16),
        compiler_params=_cparams("parallel", "parallel"),
    )(up3, up3, conv_w3, conv_b3)


def _conv_gelu_bwd(up3, dgm, conv_w3, conv_b3, t, tr=512, ct=256):
    nct = D_FF // ct
    nrt = t // tr
    hb = tr // 8
    n = tr + 16

    def body(cur_ref, prev_ref, next_ref, dg_ref, dgn_ref, w_ref, b_ref, dup_ref, dcw_ref, dcb_ref):
        i = pl.program_id(1)

        @pl.when(i == 0)
        def _():
            dcw_ref[...] = jnp.zeros_like(dcw_ref)
            dcb_ref[...] = jnp.zeros_like(dcb_ref)

        keep_prev = (i > 0).astype(F32)
        keep_next = (i < nrt - 1).astype(F32)
        xs, xm1, xm2, us = [], [], [], []
        for half in range(2):
            x = jnp.concatenate([prev_ref[half] * keep_prev, cur_ref[half], next_ref[half]], axis=0)
            xs.append(x)
            xm1.append(pltpu.roll(x, 1, 0))
            xm2.append(pltpu.roll(x, 2, 0))
            us.append(w_ref[half, 0:1, :] * xm2[half] + w_ref[half, 1:2, :] * xm1[half]
                      + w_ref[half, 2:3, :] * x + b_ref[half])
        a, c = us
        dg = jnp.concatenate([jnp.zeros((8, ct), F32), dg_ref[...], dgn_ref[...] * keep_next], axis=0)
        cdf = 0.5 * (1.0 + lax.erf(a * _INV_SQRT2))
        pdf = jnp.exp(-0.5 * a * a) * _INV_SQRT2PI
        dus = [dg * c * (cdf + a * pdf), dg * (a * cdf)]
        rid = lax.broadcasted_iota(jnp.int32, (8, 1), 0)
        for half in range(2):
            du = dus[half]
            dup = (w_ref[half, 2:3, :] * du + w_ref[half, 1:2, :] * pltpu.roll(du, n - 1, 0)
                   + w_ref[half, 0:1, :] * pltpu.roll(du, n - 2, 0))
            dup_ref[half] = dup[8:tr + 8, :].astype(BF16)
            duc = du[8:tr + 8, :]
            s0 = jnp.sum(duc * xm2[half][8:tr + 8, :], axis=0, keepdims=True)
            s1 = jnp.sum(duc * xm1[half][8:tr + 8, :], axis=0, keepdims=True)
            s2 = jnp.sum(duc * xs[half][8:tr + 8, :], axis=0, keepdims=True)
            dcw_ref[half] += jnp.where(rid == 0, s0, jnp.where(rid == 1, s1, jnp.where(rid == 2, s2, 0.0)))
            dcb_ref[half] += jnp.sum(duc, axis=0, keepdims=True)

    last8 = t // 8 - 1
    return pl.pallas_call(
        body, name="conv_gelu_bwd", grid=(nct, nrt),
        in_specs=[pl.BlockSpec((2, tr, ct), lambda j, i: (0, i, j)),
                  pl.BlockSpec((2, 8, ct), lambda j, i: (0, jnp.maximum(i * hb - 1, 0), j)),
                  pl.BlockSpec((2, 8, ct), lambda j, i: (0, jnp.minimum((i + 1) * hb, last8), j)),
                  pl.BlockSpec((tr, ct), lambda j, i: (i, j)),
                  pl.BlockSpec((8, ct), lambda j, i: (jnp.minimum((i + 1) * hb, last8), j)),
                  pl.BlockSpec((2, 3, ct), lambda j, i: (0, 0, j)),
                  pl.BlockSpec((2, 1, ct), lambda j, i: (0, 0, j))],
        out_specs=[pl.BlockSpec((2, tr, ct), lambda j, i: (0, i, j)),
                   pl.BlockSpec((2, 8, ct), lambda j, i: (0, 0, j)),
                   pl.BlockSpec((2, 1, ct), lambda j, i: (0, 0, j))],
        out_shape=[jax.ShapeDtypeStruct((2, t, D_FF), BF16), jax.ShapeDtypeStruct((2, 8, D_FF), F32),
                   jax.ShapeDtypeStruct((2, 1, D_FF), F32)],
        compiler_params=_cparams("parallel", "arbitrary"),
    )(up3, up3, up3, dgm, dgm, conv_w3, conv_b3)


def _down_ln2_loss(gm, w_down, xhat1, g1, b1, g2, b2, target, t, tm=256):
    def body(gm_ref, w_ref, xh_ref, g1_ref, b1_ref, g2_ref, b2_ref, tg_ref, dr_ref, loss_ref, dg_ref, db_ref):
        i = pl.program_id(0)

        @pl.when(i == 0)
        def _():
            loss_ref[...] = jnp.zeros_like(loss_ref)
            dg_ref[...] = jnp.zeros_like(dg_ref)
            db_ref[...] = jnp.zeros_like(db_ref)

        h = xh_ref[...] * g1_ref[...] + b1_ref[...]
        xhat, rstd = _ln_stats(ALPHA * h + _dot(gm_ref[...], w_ref[...]))
        err = xhat * g2_ref[...] + b2_ref[...] - tg_ref[...]
        loss_ref[...] += 0.5 * jnp.sum(jnp.sum(err * err, axis=1, keepdims=True), axis=0, keepdims=True) / D_MODEL
        dy = err * (1.0 / D_MODEL)
        dg_ref[...] += jnp.sum(dy * xhat, axis=0, keepdims=True)
        db_ref[...] += jnp.sum(dy, axis=0, keepdims=True)
        dr_ref[...] = _ln_bwd(dy * g2_ref[...], xhat, rstd)

    row = lambda w: pl.BlockSpec((tm, w), lambda i: (i, 0))
    full = lambda shp: pl.BlockSpec(shp, lambda i: (0, 0))
    vec = full((1, D_MODEL))
    return pl.pallas_call(
        body, name="down_ln2_loss", grid=(t // tm,),
        in_specs=[row(D_FF), full((D_FF, D_MODEL)), row(D_MODEL), vec, vec, vec, vec, row(D_MODEL)],
        out_specs=[row(D_MODEL), full((1, 1)), vec, vec],
        out_shape=[jax.ShapeDtypeStruct((t, D_MODEL), F32), jax.ShapeDtypeStruct((1, 1), F32),
                   jax.ShapeDtypeStruct((1, D_MODEL), F32), jax.ShapeDtypeStruct((1, D_MODEL), F32)],
        compiler_params=_cparams("arbitrary"),
    )(gm, w_down, xhat1, g1, b1, g2, b2, target)


def _dh_ln1_bwd(dup3, w_up4, dr2, xhat1, rstd1, g1, t, tm=512):
    tk = 2 * D_FF // 4
    nkh = D_FF // tk
    nk = 2 * nkh

    def body(a_ref, w_ref, dr2_ref, xh_ref, rs_ref, g_ref, dr1_ref, dg_ref, db_ref, acc_ref):
        i = pl.program_id(0)
        kk = pl.program_id(1)

        @pl.when((i == 0) & (kk == 0))
        def _():
            dg_ref[...] = jnp.zeros_like(dg_ref)
            db_ref[...] = jnp.zeros_like(db_ref)

        part = _dot_nt(a_ref[...], w_ref[...])

        @pl.when(kk == 0)
        def _():
            acc_ref[...] = part

        @pl.when(kk > 0)
        def _():
            acc_ref[...] += part

        @pl.when(kk == nk - 1)
        def _():
            dh = acc_ref[...] + ALPHA * dr2_ref[...]
            xhat = xh_ref[...]
            dg_ref[...] += jnp.sum(dh * xhat, axis=0, keepdims=True)
            db_ref[...] += jnp.sum(dh, axis=0, keepdims=True)
            dr1_ref[...] = _ln_bwd(dh * g_ref[...], xhat, rs_ref[...])

    row = lambda w: pl.BlockSpec((tm, w), lambda i, kk: (i, 0))
    vec = pl.BlockSpec((1, D_MODEL), lambda i, kk: (0, 0))
    return pl.pallas_call(
        body, name="dh_ln1_bwd", grid=(t // tm, nk),
        in_specs=[pl.BlockSpec((None, tm, tk), lambda i, kk: (kk // nkh, i, kk % nkh)),
                  pl.BlockSpec((None, D_MODEL, tk), lambda i, kk: (kk, 0, 0)),
                  row(D_MODEL), row(D_MODEL), row(1), vec],
        out_specs=[row(D_MODEL), vec, vec],
        out_shape=[jax.ShapeDtypeStruct((t, D_MODEL), F32), jax.ShapeDtypeStruct((1, D_MODEL), F32),
                   jax.ShapeDtypeStruct((1, D_MODEL), F32)],
        scratch_shapes=[pltpu.VMEM((tm, D_MODEL), F32)],
        compiler_params=_cparams("arbitrary", "arbitrary"),
    )(dup3, w_up4, dr2, xhat1, rstd1, g1)


def _adamw(w, g, m, v, name):
    rows, cols = w.shape
    tr = rows
    for cand in (256, 128, 64, 32, 16, 8):
        if rows % cand == 0 and rows > cand:
            tr = cand
            break
    c1 = 1.0 / (1.0 - ADAM_B1 ** ADAM_STEP)
    c2 = 1.0 / (1.0 - ADAM_B2 ** ADAM_STEP)

    def body(w_ref, g_ref, m_ref, v_ref, d_ref, nm_ref, nv_ref):
        gv = g_ref[...]
        nm = ADAM_B1 * m_ref[...] + (1.0 - ADAM_B1) * gv
        nv = ADAM_B2 * v_ref[...] + (1.0 - ADAM_B2) * (gv * gv)
        d_ref[...] = -ADAM_LR * ((nm * c1) / (jnp.sqrt(nv * c2) + ADAM_EPS) + ADAM_WD * w_ref[...])
        nm_ref[...] = nm
        nv_ref[...] = nv

    spec = pl.BlockSpec((tr, cols), lambda i: (i, 0))
    out = jax.ShapeDtypeStruct((rows, cols), F32)
    return pl.pallas_call(
        body, name=name, grid=(rows // tr,), in_specs=[spec] * 4, out_specs=[spec] * 3, out_shape=[out] * 3,
        compiler_params=_cparams("parallel"),
    )(w, g, m, v)


def _local_step(x, target, w_in_p, late_shards, gate_up_pad, gate_bias, gnorm, ln1_g, ln1_b, conv_w3, conv_b3,
                ln2_g, ln2_b, c_arr, kc_arr):
    t = x.shape[0]
    tq = min(t, 1024)
    proj = _mm(x, w_in_p, m=t, n=IN_PAD, k=D_MODEL, tm=tq, tn=640, tk=D_MODEL, name="proj")
    sb_o, *partly = _sb_fwd(proj, t, late_shards, LATE_SHAPES)
    gla_o, oraw, states, w_up4, w_out4, w_down4 = _gla_fwd(proj, gate_up_pad, gate_bias, gnorm, t, partly, LATE_SHAPES)
    w_out = w_out4.reshape(D_MODEL, D_MODEL)
    w_down = w_down4.reshape(D_FF, D_MODEL)
    xhat1, rstd1, h_bf = _outproj_ln1(sb_o, gla_o, x, w_out, ln1_g, ln1_b, t)
    up3 = _mm(h_bf, w_up4, m=t, n=2 * D_FF, k=D_MODEL, tm=512, tn=W_UP_S, tk=D_MODEL, name="up",
              b_spec=pl.BlockSpec((None, D_MODEL, W_UP_S), lambda i, j, kk: (j, 0, 0)),
              o_spec=pl.BlockSpec((None, 512, W_UP_S), lambda i, j, kk: (j // 2, i, j % 2)),
              out_shape=jax.ShapeDtypeStruct((2, t, D_FF), F32))
    gm = _conv_gelu_fwd(up3, conv_w3, conv_b3, t)
    dr2, loss, dln2_g, dln2_b = _down_ln2_loss(gm, w_down, xhat1, ln1_g, ln1_b, ln2_g, ln2_b, target, t)
    dgm = _mm(dr2, w_down, m=t, n=D_FF, k=D_MODEL, tm=512, tn=W_UP_S, tk=D_MODEL, tb=True, name="dgm")
    dw_down = _mm(gm, dr2, m=D_FF, n=D_MODEL, k=t, tm=W_UP_S, tn=D_MODEL, tk=512, ta=True, name="dw_down")
    dup3, dcw, dcb = _conv_gelu_bwd(up3, dgm, conv_w3, conv_b3, t)
    dr1, dln1_g, dln1_b = _dh_ln1_bwd(dup3, w_up4, dr2, xhat1, rstd1, ln1_g, t)
    dw_up4 = _mm(h_bf, dup3, m=D_MODEL, n=2 * D_FF, k=t, tm=512, tn=W_UP_S, tk=tq, ta=True, name="dw_up",
                 b_spec=pl.BlockSpec((None, tq, W_UP_S), lambda i, j, kk: (j // 2, kk, j % 2)),
                 o_spec=pl.BlockSpec((None, 512, W_UP_S), lambda i, j, kk: (j, i, 0)),
                 out_shape=jax.ShapeDtypeStruct((4, D_MODEL, W_UP_S), F32))
    dcat = _mm(dr1, w_out, m=t, n=D_MODEL, k=D_MODEL, tm=tq, tn=512, tk=D_MODEL, tb=True, name="dcat")
    dw_out_sb = _mm(sb_o, dr1, m=SB_WIDTH, n=D_MODEL, k=t, tm=512, tn=D_MODEL, tk=tq, ta=True, name="dw_out_sb")
    dw_out_gla = _mm(gla_o, dr1, m=GLA_WIDTH, n=D_MODEL, k=t, tm=512, tn=D_MODEL, tk=tq, ta=True, name="dw_out_gla")
    gs = [dw_up4, jnp.concatenate([dw_out_sb, dw_out_gla], axis=0).reshape(4, W_OUT_S, D_MODEL),
          dw_down.reshape(4, W_DOWN_S, D_MODEL)]
    from_sib = _exchange_sibling(gs, LATE_SHAPES, "exchange_sibling_late")
    ps = [_add_sibling(gs[m], from_sib[m], c_arr, LATE_ADD_ROWS[m], "add_sibling_late_%d" % m) for m in range(3)]
    dsq, dsk, dsv, *others = _sb_bwd(proj, dcat, t, ps)
    late_sums = [_add_chips(ps[m], others[m], kc_arr, LATE_ADD_ROWS[m], "add_chips_late_%d" % m) for m in range(3)]
    dgla, dgup_pad, dgbias, dgnorm = _gla_bwd(proj, dcat, oraw, states, gate_up_pad, gate_bias, gnorm, t)
    dproj = jnp.concatenate([dsq, dsk, dsv, dgla], axis=1).astype(BF16)
    dx = _mm(dproj, w_in_p, m=t, n=D_MODEL, k=IN_PAD, tm=tq, tn=D_MODEL, tk=640, tb=True, add=dr1, add_scale=ALPHA, name="dx")
    dw_in_p = _mm(x, dproj, m=D_MODEL, n=IN_PAD, k=t, tm=512, tn=IN_PAD, tk=512, ta=True, name="dw_in")
    small = dict(
        gate_up=dgup_pad[:GATE_RANK], gate_bias=dgbias, gla_norm_g=dgnorm, ln1_g=dln1_g, ln1_b=dln1_b,
        conv_w=jnp.concatenate([dcw[0, :3], dcw[1, :3]], axis=1), conv_b=jnp.concatenate([dcb[0], dcb[1]], axis=1),
        ln2_g=dln2_g, ln2_b=dln2_b)
    return loss, dx, late_sums, dw_in_p, small


W_IN_S, W_UP_S, W_OUT_S, W_DOWN_S = IN_WIDTH // 4, 2 * D_FF // 4, D_MODEL // 4, D_FF // 4
SHARD_SHAPES = ((D_MODEL, W_IN_S), (D_MODEL, W_UP_S), (W_OUT_S, D_MODEL), (W_DOWN_S, D_MODEL))
ADD_ROWS = (256, 256, 128, 176)
LATE_SHAPES, LATE_ADD_ROWS = SHARD_SHAPES[1:], ADD_ROWS[1:]
SMALL_ROWS = 8
VEC_SIZES = (("gate_bias", GLA_KEYS), ("gla_norm_g", LANES), ("ln1_g", D_MODEL), ("ln1_b", D_MODEL),
             ("conv_b", 2 * D_FF), ("ln2_g", D_MODEL), ("ln2_b", D_MODEL))
SMALL_GRADS = VEC_SIZES + (("conv_w", 3 * 2 * D_FF), ("gate_up", GATE_RANK * GLA_KEYS))


def _rows(a):
    flat = a.reshape(-1)
    pad = (-flat.shape[0]) % D_MODEL
    if pad:
        flat = jnp.pad(flat, (0, pad))
    return flat.reshape(-1, D_MODEL)


def _pad_rows(a, rows):
    return jnp.pad(a, ((0, rows - a.shape[0]), (0, 0)))


def _pack_vec(d, sizes, rows):
    flat = jnp.concatenate([d[n].reshape(-1) for n, _ in sizes])
    return _pad_rows(_rows(flat), rows)


def _unpack_vec(v, sizes):
    flat = v.reshape(-1)
    out, o = {}, 0
    for n, size in sizes:
        out[n] = flat[o:o + size].reshape(1, size)
        o += size
    return out


VEC_ROWS = 16
GRAD_VEC_ROWS = 32


HBM_SPEC = pl.BlockSpec(memory_space=pltpu.HBM)


def _position():
    x, y, c = lax.axis_index("x"), lax.axis_index("y"), lax.axis_index("c")
    chips = [(1 - x, y), (x, 1 - y), (1 - x, 1 - y)]
    return x, y, c, chips


def _remote(src, dst, send_sems, recv_sems, k, to):
    return pltpu.make_async_remote_copy(src_ref=src, dst_ref=dst, send_sem=send_sems.at[k], recv_sem=recv_sems.at[k],
                                        device_id=to, device_id_type=MESH)


def _gather_ici(in_refs, out_refs, shapes, send_sems, recv_sems, local_sems):
    x, y, c, chips = _position()
    k_me = 2 * x + y
    local, sends, recvs = [], [], []
    for m, (rows, _) in enumerate(shapes):
        h = rows // 2
        local.append(pltpu.make_async_copy(in_refs[m], out_refs[m].at[k_me], local_sems.at[m]))
        for j, (cx, cy) in enumerate(chips):
            sends.append(_remote(in_refs[m].at[pl.ds(c * h, h), :], out_refs[m].at[k_me, pl.ds(c * h, h), :],
                                 send_sems, recv_sems, 3 * m + j, (cx, cy, c)))
            landed = out_refs[m].at[2 * cx + cy, pl.ds(c * h, h), :]
            recvs.append(_remote(landed, landed, send_sems, recv_sems, 3 * m + j, (x, y, c)))
    return local, sends, recvs


def _gather_d2d(src_refs, dst_refs, shapes, send_sems, recv_sems, base):
    x, y, c, chips = _position()
    sends, recvs = [], []
    for m, (rows, _) in enumerate(shapes):
        h = rows // 2
        for j, (cx, cy) in enumerate(chips):
            k = 2 * cx + cy
            sends.append(_remote(src_refs[m].at[k, pl.ds(c * h, h), :], dst_refs[m].at[k, pl.ds(c * h, h), :],
                                 send_sems, recv_sems, base + 3 * m + j, (x, y, 1 - c)))
            landed = dst_refs[m].at[k, pl.ds((1 - c) * h, h), :]
            recvs.append(_remote(landed, landed, send_sems, recv_sems, base + 3 * m + j, (x, y, c)))
    return sends, recvs


def _gather_weights(shards, small, shapes):
    nm = len(shards)
    n_ici = 3 * nm

    def body(*refs):
        in_refs, small_ref = refs[:nm], refs[nm]
        out_refs, osm_ref = refs[nm + 1:2 * nm + 1], refs[2 * nm + 1]
        send_sems, recv_sems, local_sems = refs[2 * nm + 2:]
        x, y, c, chips = _position()
        k_me = 2 * x + y
        local, sends, recvs = _gather_ici(in_refs, out_refs, shapes, send_sems, recv_sems, local_sems)
        local.append(pltpu.make_async_copy(small_ref, osm_ref.at[k_me], local_sems.at[nm]))
        for j, (cx, cy) in enumerate(chips):
            sends.append(_remote(small_ref, osm_ref.at[k_me], send_sems, recv_sems, n_ici + j, (cx, cy, c)))
        for cp in local + sends:
            cp.start()
        for cp in recvs:
            cp.wait_recv()
        fsends, frecvs = _gather_d2d(out_refs, out_refs, shapes, send_sems, recv_sems, n_ici + 3)
        for cp in fsends:
            cp.start()
        for j, (cx, cy) in enumerate(chips):
            k = 2 * cx + cy
            frecvs.append(_remote(osm_ref.at[k], osm_ref.at[k], send_sems, recv_sems, n_ici + j, (x, y, c)))
        for cp in frecvs:
            cp.wait_recv()
        for cp in sends + fsends:
            cp.wait_send()
        for cp in local:
            cp.wait()

    n_sems = 2 * n_ici + 3
    return pl.pallas_call(
        body, name="gather_weights", in_specs=[HBM_SPEC] * (nm + 1), out_specs=[HBM_SPEC] * (nm + 1),
        out_shape=[jax.ShapeDtypeStruct((4,) + s, BF16) for s in shapes]
        + [jax.ShapeDtypeStruct((4, SMALL_ROWS, D_MODEL), F32)],
        scratch_shapes=[pltpu.SemaphoreType.DMA((n_sems,)), pltpu.SemaphoreType.DMA((n_sems,)),
                        pltpu.SemaphoreType.DMA((nm + 1,))],
    )(*shards, small)


def _exchange_sibling(gs, shapes, name):
    nm = len(gs)

    def body(*refs):
        g_refs, r_refs = refs[:nm], refs[nm:2 * nm]
        send_sems, recv_sems = refs[2 * nm:]
        x, y, c, _ = _position()
        copies = []
        for m, (rows, _) in enumerate(shapes):
            h = rows // 2
            for k in range(4):
                copies.append(_remote(g_refs[m].at[k, pl.ds((1 - c) * h, h), :], r_refs[m].at[k], send_sems, recv_sems,
                                      4 * m + k, (x, y, 1 - c)))
        for cp in copies:
            cp.start()
        for cp in copies:
            cp.wait()

    return pl.pallas_call(
        body, name=name, in_specs=[HBM_SPEC] * nm, out_specs=[HBM_SPEC] * nm,
        out_shape=[jax.ShapeDtypeStruct((4, r // 2, cl), F32) for r, cl in shapes],
        scratch_shapes=[pltpu.SemaphoreType.DMA((4 * nm,)), pltpu.SemaphoreType.DMA((4 * nm,))],
    )(*gs)


def _add_sibling(g, r, c_arr, tr, name):
    _, rows, cols = g.shape
    nb = rows // 2 // tr

    def body(c_ref, g_ref, r_ref, o_ref):
        o_ref[...] = (g_ref[...] + r_ref[...]).astype(BF16)

    spec = pl.BlockSpec((None, tr, cols), lambda k, i, c: (k, i, 0))
    return pl.pallas_call(
        body, name=name,
        grid_spec=pltpu.PrefetchScalarGridSpec(
            num_scalar_prefetch=1, grid=(4, nb),
            in_specs=[pl.BlockSpec((None, tr, cols), lambda k, i, c: (k, c[0] * nb + i, 0)), spec], out_specs=spec),
        out_shape=jax.ShapeDtypeStruct((4, rows // 2, cols), BF16), compiler_params=_cparams("parallel", "parallel"),
    )(c_arr, g, r)


def _reduce_ici(p_refs, r_refs, send_sems, recv_sems):
    x, y, c, chips = _position()
    sends, recvs = [], []
    for m in range(len(p_refs)):
        for j, (cx, cy) in enumerate(chips):
            sends.append(_remote(p_refs[m].at[2 * cx + cy], r_refs[m].at[j], send_sems, recv_sems, 3 * m + j, (cx, cy, c)))
            recvs.append(_remote(r_refs[m].at[j], r_refs[m].at[j], send_sems, recv_sems, 3 * m + j, (x, y, c)))
    return sends, recvs


def _exchange_chips(ps, vec):
    nm = len(ps)
    n_ici = 3 * nm

    def body(*refs):
        p_refs, vec_ref = refs[:nm], refs[nm]
        r_refs, vrecv_ref = refs[nm + 1:2 * nm + 1], refs[2 * nm + 1]
        send_sems, recv_sems, local_sem = refs[2 * nm + 2:]
        x, y, c, _ = _position()
        my_id = 4 * x + 2 * y + c
        vloc = pltpu.make_async_copy(vec_ref, vrecv_ref.at[my_id], local_sem.at[0])
        vloc.start()
        sends, recvs = _reduce_ici(p_refs, r_refs, send_sems, recv_sems)
        for r in range(1, 8):
            peer = (1 - x if r & 4 else x, 1 - y if r & 2 else y, 1 - c if r & 1 else c)
            sends.append(_remote(vec_ref, vrecv_ref.at[my_id], send_sems, recv_sems, n_ici + r - 1, peer))
            recvs.append(_remote(vec_ref, vrecv_ref.at[0], send_sems, recv_sems, n_ici + r - 1, (x, y, c)))
        for cp in sends:
            cp.start()
        for cp in recvs:
            cp.wait_recv()
        for cp in sends:
            cp.wait_send()
        vloc.wait()

    n_sems = n_ici + 7
    return pl.pallas_call(
        body, name="exchange_chips", in_specs=[HBM_SPEC] * (nm + 1), out_specs=[HBM_SPEC] * (nm + 1),
        out_shape=[jax.ShapeDtypeStruct((3,) + p.shape[1:], p.dtype) for p in ps]
        + [jax.ShapeDtypeStruct((8, GRAD_VEC_ROWS, D_MODEL), F32)],
        scratch_shapes=[pltpu.SemaphoreType.DMA((n_sems,)), pltpu.SemaphoreType.DMA((n_sems,)), pltpu.SemaphoreType.DMA((1,))],
    )(*ps, vec)


def _add_chips(p, r, kc_arr, tr, name):
    _, h, cols = p.shape
    nb = h // tr

    def body(kc_ref, p_ref, r0_ref, r1_ref, r2_ref, o_ref):
        o_ref[...] = ((p_ref[...].astype(F32) + r0_ref[...].astype(F32)) + r1_ref[...].astype(F32)) + r2_ref[...].astype(F32)

    rspec = lambda j: pl.BlockSpec((None, tr, cols), lambda i, kc: (j, i, 0))
    return pl.pallas_call(
        body, name=name,
        grid_spec=pltpu.PrefetchScalarGridSpec(
            num_scalar_prefetch=1, grid=(nb,),
            in_specs=[pl.BlockSpec((None, tr, cols), lambda i, kc: (kc[0], i, 0)), rspec(0), rspec(1), rspec(2)],
            out_specs=pl.BlockSpec((tr, cols), lambda i, kc: (kc[1] * nb + i, 0))),
        out_shape=jax.ShapeDtypeStruct((2 * h, cols), F32), compiler_params=_cparams("parallel"),
    )(kc_arr, p, r, r, r)


def _reunite_sibling(fs, shapes):
    n_chunks = 2
    nm = len(fs)

    def body(*refs):
        in_refs, out_refs = refs[:nm], refs[nm:2 * nm]
        send_sems, recv_sems = refs[2 * nm:]
        x, y, c, _ = _position()
        sends, recvs = [], []
        for m in range(nm):
            ch = shapes[m][0] // 2 // n_chunks
            for q in range(n_chunks):
                mine = pl.ds((c * n_chunks + q) * ch, ch)
                theirs = pl.ds(((1 - c) * n_chunks + q) * ch, ch)
                s = n_chunks * m + q
                sends.append(_remote(in_refs[m].at[mine, :], out_refs[m].at[mine, :], send_sems, recv_sems, s, (x, y, 1 - c)))
                recvs.append(_remote(in_refs[m].at[theirs, :], out_refs[m].at[theirs, :], send_sems, recv_sems, s, (x, y, c)))
        for cp in sends:
            cp.start()
        for cp in recvs:
            cp.wait_recv()
        for cp in sends:
            cp.wait_send()

    n_sems = n_chunks * nm
    return pl.pallas_call(
        body, name="reunite_sibling", in_specs=[HBM_SPEC] * nm, out_specs=[HBM_SPEC] * nm,
        out_shape=[jax.ShapeDtypeStruct(s, F32) for s in shapes],
        input_output_aliases={m: m for m in range(nm)},
        scratch_shapes=[pltpu.SemaphoreType.DMA((n_sems,)), pltpu.SemaphoreType.DMA((n_sems,))],
    )(*fs)


def _sum_vecs(v):
    def body(v_ref, o_ref):
        acc = v_ref[0]
        for d in range(1, 8):
            acc = acc + v_ref[d]
        o_ref[...] = acc

    return pl.pallas_call(body, name="sum_vecs", out_shape=jax.ShapeDtypeStruct(v.shape[1:], F32))(v)


def kernel(x, w_in, gate_up, gate_bias, gla_norm_g, w_out, ln1_g, ln1_b, w_up, conv_w, conv_b, w_down, ln2_g, ln2_b, loss_target, m_w_in, m_gate_up, m_gate_bias, m_gla_norm_g, m_w_out, m_ln1_g, m_ln1_b, m_w_up, m_conv_w, m_conv_b, m_w_down, m_ln2_g, m_ln2_b, v_w_in, v_gate_up, v_gate_bias, v_gla_norm_g, v_w_out, v_ln1_g, v_ln1_b, v_w_up, v_conv_w, v_conv_b, v_w_down, v_ln2_g, v_ln2_b):
    xi, yi, ci = lax.axis_index("x"), lax.axis_index("y"), lax.axis_index("c")
    k_me = 2 * xi + yi
    c_arr = ci.astype(jnp.int32).reshape(1)
    kc_arr = jnp.stack([k_me, ci]).astype(jnp.int32)
    small = _pad_rows(jnp.concatenate([_rows(conv_w[0]), _rows(gate_up[0])], axis=0), SMALL_ROWS)
    w_in4, gsmall = _gather_weights([w_in[0].astype(BF16)], small, SHARD_SHAPES[:1])
    late_shards = [w_up[0].astype(BF16), w_out[0].astype(BF16), w_down[0].astype(BF16)]
    w_in_p = jnp.pad(jnp.concatenate([w_in4[k] for k in range(4)], axis=1), ((0, 0), (0, IN_PAD - IN_WIDTH)))
    conv_w_f = jnp.concatenate([gsmall[k, :5].reshape(-1)[:3 * W_UP_S].reshape(3, W_UP_S) for k in range(4)], axis=1)
    gate_up_f = jnp.concatenate([gsmall[k, 5].reshape(GATE_RANK, GLA_KEYS // 4) for k in range(4)], axis=1)
    conv_w3 = jnp.transpose(conv_w_f.reshape(3, 2, D_FF), (1, 0, 2))
    conv_b3 = conv_b.reshape(2, 1, D_FF)
    gate_up_pad = jnp.pad(gate_up_f, ((0, LANES - GATE_RANK), (0, 0)))

    loss_part, dx, late_sums, dw_in_p, g = _local_step(
        x[0], loss_target[0], w_in_p, late_shards, gate_up_pad, gate_bias, gla_norm_g, ln1_g, ln1_b, conv_w3, conv_b3,
        ln2_g, ln2_b, c_arr, kc_arr)
    loss = lax.psum(loss_part[0, 0], ("x", "y", "c"))

    g_in4 = jnp.stack([dw_in_p[:, k * W_IN_S:(k + 1) * W_IN_S] for k in range(4)], axis=0)
    from_sib, = _exchange_sibling([g_in4], SHARD_SHAPES[:1], "exchange_sibling_w_in")
    p_in = _add_sibling(g_in4, from_sib, c_arr, ADD_ROWS[0], "add_sibling_w_in")
    others_in, vecs = _exchange_chips([p_in], _pack_vec(g, SMALL_GRADS, GRAD_VEC_ROWS))
    sum_in = _add_chips(p_in, others_in, kc_arr, ADD_ROWS[0], "add_chips_w_in")
    g_w_in, g_w_up, g_w_out, g_w_down = _reunite_sibling([sum_in] + late_sums, SHARD_SHAPES)
    gsmall_sum = _unpack_vec(_sum_vecs(vecs), SMALL_GRADS)
    g_conv_w = lax.dynamic_slice_in_dim(gsmall_sum["conv_w"].reshape(3, 2 * D_FF), k_me * W_UP_S, W_UP_S, axis=1)
    g_gate_up = lax.dynamic_slice_in_dim(gsmall_sum["gate_up"].reshape(GATE_RANK, GLA_KEYS), k_me * (GLA_KEYS // 4),
                                         GLA_KEYS // 4, axis=1)
    gv = gsmall_sum
    gvec = _pack_vec(gv, VEC_SIZES, VEC_ROWS)

    grads = dict(w_in=g_w_in[None], gate_up=g_gate_up[None], gate_bias=gv["gate_bias"], gla_norm_g=gv["gla_norm_g"],
                 w_out=g_w_out[None], ln1_g=gv["ln1_g"], ln1_b=gv["ln1_b"], w_up=g_w_up[None], conv_w=g_conv_w[None],
                 conv_b=gv["conv_b"], w_down=g_w_down[None], ln2_g=gv["ln2_g"], ln2_b=gv["ln2_b"])
    weights = dict(w_in=w_in, gate_up=gate_up, gate_bias=gate_bias, gla_norm_g=gla_norm_g, w_out=w_out, ln1_g=ln1_g,
                   ln1_b=ln1_b, w_up=w_up, conv_w=conv_w, conv_b=conv_b, w_down=w_down, ln2_g=ln2_g, ln2_b=ln2_b)
    ms = dict(w_in=m_w_in, gate_up=m_gate_up, gate_bias=m_gate_bias, gla_norm_g=m_gla_norm_g, w_out=m_w_out, ln1_g=m_ln1_g,
              ln1_b=m_ln1_b, w_up=m_w_up, conv_w=m_conv_w, conv_b=m_conv_b, w_down=m_w_down, ln2_g=m_ln2_g, ln2_b=m_ln2_b)
    vs = dict(w_in=v_w_in, gate_up=v_gate_up, gate_bias=v_gate_bias, gla_norm_g=v_gla_norm_g, w_out=v_w_out, ln1_g=v_ln1_g,
              ln1_b=v_ln1_b, w_up=v_w_up, conv_w=v_conv_w, conv_b=v_conv_b, w_down=v_w_down, ln2_g=v_ln2_g, ln2_b=v_ln2_b)
    names = ["w_in", "gate_up", "gate_bias", "gla_norm_g", "w_out", "ln1_g", "ln1_b", "w_up", "conv_w", "conv_b", "w_down",
             "ln2_g", "ln2_b"]
    delta, new_m, new_v = {}, {}, {}
    for n in ("w_in", "gate_up", "w_out", "w_up", "conv_w", "w_down"):
        d, nm, nv = _adamw(weights[n][0], grads[n][0], ms[n][0], vs[n][0], "adamw_" + n)
        delta[n], new_m[n], new_v[n] = d[None], nm[None], nv[None]
    d, nm, nv = _adamw(_pack_vec(weights, VEC_SIZES, VEC_ROWS), gvec, _pack_vec(ms, VEC_SIZES, VEC_ROWS),
                       _pack_vec(vs, VEC_SIZES, VEC_ROWS), "adamw_vectors")
    for dst, src in ((delta, d), (new_m, nm), (new_v, nv)):
        dst.update(_unpack_vec(src, VEC_SIZES))
    return (loss, dx[None], *[grads[n] for n in names], *[delta[n] for n in names], *[new_m[n] for n in names],
            *[new_v[n] for n in names])
```

```python
import functools
import math

import jax
import jax.numpy as jnp
from jax import lax
from jax.experimental import pallas as pl
from jax.experimental.pallas import tpu as pltpu

F32 = jnp.float32
BF16 = jnp.bfloat16

D_MODEL = 1024
SB_WIDTH = 512
GLA_KEYS = 256
GLA_WIDTH = 512
GATE_RANK = 16
IN_WIDTH = 3088
IN_PAD = 3200
D_FF = 2816
CHUNK = 64
LN_EPS = 1e-5
RMS_EPS = 1e-6
ALPHA = 2.0 ** 0.25
GLA_TAU = 16.0
SB_SCALE = 0.125
GLA_SCALE = 0.125
LANES = 128
SB_BLK = 256
SB_CUT = -100.0
GLA_ROWS = 256
VMEM_LIMIT = 56 * 1024 * 1024

ADAM_LR, ADAM_B1, ADAM_B2, ADAM_EPS, ADAM_WD, ADAM_STEP = 0.001, 0.9, 0.999, 1e-08, 0.01, 10

MESH = pl.DeviceIdType.MESH


def _cparams(*sem):
    return pltpu.CompilerParams(dimension_semantics=sem, vmem_limit_bytes=VMEM_LIMIT)


def _dot(a, b):
    return jnp.dot(a, b, preferred_element_type=F32)


def _dot_nt(a, b):
    return lax.dot_general(a, b, (((1,), (1,)), ((), ())), preferred_element_type=F32)


def _dot_tn(a, b):
    return lax.dot_general(a, b, (((0,), (0,)), ((), ())), preferred_element_type=F32)


def _split2(x):
    hi = x.astype(BF16)
    lo = (x - hi.astype(F32)).astype(BF16)
    return hi, lo


def _split3(x):
    hi = x.astype(BF16)
    r = x - hi.astype(F32)
    mid = r.astype(BF16)
    lo = (r - mid.astype(F32)).astype(BF16)
    return hi, mid, lo


def _softplus(z):
    return jnp.maximum(z, 0.0) + jnp.log(1.0 + jnp.exp(-jnp.abs(z)))


def _sigmoid(z):
    return 1.0 / (1.0 + jnp.exp(-z))


def _mm(a, b, *, m, n, k, tm, tn, tk, ta=False, tb=False, a_spec=None, b_spec=None, o_spec=None,
        out_shape=None, out_dtype=F32, add=None, add_scale=1.0, name):
    nk = k // tk
    dn = (((0 if ta else 1,), (1 if tb else 0,)), ((), ()))

    def body(*refs):
        a_ref, b_ref = refs[:2]
        add_ref = refs[2] if add is not None else None
        o_ref = refs[3 if add is not None else 2]
        part = lax.dot_general(a_ref[...].astype(BF16), b_ref[...].astype(BF16), dn, preferred_element_type=F32)

        def finish(r):
            if add is not None:
                r = r + add_scale * add_ref[...]
            o_ref[...] = r.astype(out_dtype)

        if nk == 1:
            finish(part)
            return
        acc_ref = refs[-1]
        kk = pl.program_id(2)

        @pl.when(kk == 0)
        def _():
            acc_ref[...] = part

        @pl.when((kk > 0) & (kk < nk - 1))
        def _():
            acc_ref[...] += part

        @pl.when(kk == nk - 1)
        def _():
            finish(acc_ref[...] + part)

    if a_spec is None:
        a_spec = pl.BlockSpec((tk, tm), lambda i, j, kk: (kk, i)) if ta else pl.BlockSpec((tm, tk), lambda i, j, kk: (i, kk))
    if b_spec is None:
        b_spec = pl.BlockSpec((tn, tk), lambda i, j, kk: (j, kk)) if tb else pl.BlockSpec((tk, tn), lambda i, j, kk: (kk, j))
    if o_spec is None:
        o_spec = pl.BlockSpec((tm, tn), lambda i, j, kk: (i, j))
    if out_shape is None:
        out_shape = jax.ShapeDtypeStruct((m, n), out_dtype)
    in_specs = [a_spec, b_spec]
    args = [a, b]
    if add is not None:
        in_specs.append(pl.BlockSpec((tm, tn), lambda i, j, kk: (i, j)))
        args.append(add)
    return pl.pallas_call(
        body, name=name, grid=(m // tm, n // tn, nk), in_specs=in_specs, out_specs=o_spec, out_shape=out_shape,
        scratch_shapes=[pltpu.VMEM((tm, tn), F32)] if nk > 1 else [],
        compiler_params=_cparams("parallel", "parallel", "arbitrary"),
    )(*args)


def _sb_tile(qh, kj, diag, strict, u_strict, r_in):
    z = _dot_nt(qh, kj)
    sp = _softplus(z)
    l1m = -sp
    lsz = z - sp
    if diag:
        l1m = jnp.where(strict, l1m, 0.0)
    hi, lo = _split2(l1m)
    cs = _dot(hi, u_strict) + _dot(lo, u_strict) + r_in
    w = jnp.exp(lsz + cs)
    if diag:
        w = jnp.where(strict, w, 0.0)
    return l1m, lsz, w


def _sb_consts():
    row = lax.broadcasted_iota(jnp.int32, (SB_BLK, SB_BLK), 0)
    col = lax.broadcasted_iota(jnp.int32, (SB_BLK, SB_BLK), 1)
    strict = col < row
    u_strict = (row > col).astype(BF16)
    u_pre = (row < col).astype(BF16)
    lane = lax.broadcasted_iota(jnp.int32, (1, LANES), 1)
    return strict, u_strict, u_pre, lane


def _sb_fwd(proj, t, shards, shapes):
    nq = t // SB_BLK

    nm = len(shards)

    def body(q_ref, k_ref, v_ref, *rest):
        w_refs, o_ref, g_refs = rest[:nm], rest[nm], rest[nm + 1:2 * nm + 1]
        sems = rest[2 * nm + 1:]
        p = pl.program_id(0)
        i = pl.program_id(1)

        @pl.when((p == 0) & (i == 0))
        def _():
            local, sends, _ = _gather_ici(w_refs, g_refs, shapes, *sems)
            for cp in local + sends:
                cp.start()

        @pl.when((p == 3) & (i == nq - 1))
        def _():
            local, sends, recvs = _gather_ici(w_refs, g_refs, shapes, *sems)
            for cp in recvs:
                cp.wait_recv()
            for cp in sends:
                cp.wait_send()
            for cp in local:
                cp.wait()

        strict, u_strict, _, lane = _sb_consts()
        qf = q_ref[...] * SB_SCALE
        hms = [(lane // 64) == hh for hh in range(2)]
        qhs = [jnp.where(hm, qf, 0.0).astype(BF16) for hm in hms]

        def step(j, r0, r1, a, diag):
            rows = pl.ds(pl.multiple_of(j * SB_BLK, SB_BLK), SB_BLK)
            kj = k_ref[rows, :].astype(BF16)
            vf = v_ref[rows, :]
            rs = []
            for hh, r in enumerate((r0, r1)):
                l1m, _, w = _sb_tile(qhs[hh], kj, diag, strict, u_strict, r)
                a = a + _dot(w.astype(BF16), jnp.where(hms[hh], vf, 0.0).astype(BF16))
                rs.append(r + jnp.sum(l1m, axis=1, keepdims=True))
            return rs[0], rs[1], a

        zero = jnp.zeros((SB_BLK, 1), F32)
        r0, r1, acc = step(i, zero, zero, jnp.zeros((SB_BLK, LANES), F32), True)
        _, _, _, acc = lax.while_loop(
            lambda c: (c[0] >= 0) & (jnp.maximum(jnp.max(c[1]), jnp.max(c[2])) > SB_CUT),
            lambda c: (c[0] - 1,) + step(c[0], c[1], c[2], c[3], False),
            (i - 1, r0, r1, acc))
        o_ref[...] = acc.astype(BF16)

    return pl.pallas_call(
        body, name="sb_fwd", grid=(4, nq),
        in_specs=[pl.BlockSpec((SB_BLK, LANES), lambda p, i: (i, p)),
                  pl.BlockSpec((t, LANES), lambda p, i: (0, 4 + p)),
                  pl.BlockSpec((t, LANES), lambda p, i: (0, 8 + p))] + [HBM_SPEC] * nm,
        out_specs=[pl.BlockSpec((SB_BLK, LANES), lambda p, i: (i, p))] + [HBM_SPEC] * nm,
        out_shape=[jax.ShapeDtypeStruct((t, SB_WIDTH), BF16)] + [jax.ShapeDtypeStruct((4,) + s, BF16) for s in shapes],
        scratch_shapes=[pltpu.SemaphoreType.DMA((3 * nm,)), pltpu.SemaphoreType.DMA((3 * nm,)),
                        pltpu.SemaphoreType.DMA((nm,))],
        compiler_params=_cparams("arbitrary", "arbitrary"),
    )(proj, proj, proj, *shards)


def _sb_bwd(proj, dcat, t, ps):
    nq = t // SB_BLK
    nm = len(ps)

    def body(q_ref, k_ref, v_ref, do_ref, *rest):
        p_refs = rest[:nm]
        dq_ref, dk_ref, dv_ref = rest[nm:nm + 3]
        got_refs = rest[nm + 3:2 * nm + 3]
        r_scr, send_sems, recv_sems = rest[2 * nm + 3:]
        p = pl.program_id(0)
        i = pl.program_id(1)

        @pl.when((p == 0) & (i == 0))
        def _():
            sends, _ = _reduce_ici(p_refs, got_refs, send_sems, recv_sems)
            for cp in sends:
                cp.start()

        @pl.when((p == 3) & (i == nq - 1))
        def _():
            sends, recvs = _reduce_ici(p_refs, got_refs, send_sems, recv_sems)
            for cp in recvs:
                cp.wait_recv()
            for cp in sends:
                cp.wait_send()

        @pl.when(i == 0)
        def _():
            dk_ref[...] = jnp.zeros_like(dk_ref)
            dv_ref[...] = jnp.zeros_like(dv_ref)

        strict, u_strict, u_pre, lane = _sb_consts()
        qf = q_ref[...] * SB_SCALE
        dof = do_ref[...]
        hms = [(lane // 64) == hh for hh in range(2)]
        qhs = [jnp.where(hm, qf, 0.0).astype(BF16) for hm in hms]
        dohs = [jnp.where(hm, dof, 0.0).astype(BF16) for hm in hms]

        def scan(j, r0, r1, diag):
            rows = pl.ds(pl.multiple_of(j * SB_BLK, SB_BLK), SB_BLK)
            kj = k_ref[rows, :].astype(BF16)
            out = []
            for hh, r in enumerate((r0, r1)):
                l1m = -_softplus(_dot_nt(qhs[hh], kj))
                if diag:
                    l1m = jnp.where(strict, l1m, 0.0)
                r_scr[hh, j] = r
                out.append(r + jnp.sum(l1m, axis=1, keepdims=True))
            return out[0], out[1]

        zero = jnp.zeros((SB_BLK, 1), F32)
        r0, r1 = scan(i, zero, zero, True)
        jstop, _, _ = lax.while_loop(
            lambda c: (c[0] >= 0) & (jnp.maximum(jnp.max(c[1]), jnp.max(c[2])) > SB_CUT),
            lambda c: (c[0] - 1,) + scan(c[0], c[1], c[2], False),
            (i - 1, r0, r1))

        def step(j, carry, diag):
            pre_g0, pre_g1, dqa = carry
            rows = pl.ds(pl.multiple_of(j * SB_BLK, SB_BLK), SB_BLK)
            kf = k_ref[rows, :]
            kj = kf.astype(BF16)
            vj = v_ref[rows, :].astype(BF16)
            dv = jnp.zeros((SB_BLK, LANES), F32)
            dk = jnp.zeros((SB_BLK, LANES), F32)
            pre = []
            for hh, pre_g in enumerate((pre_g0, pre_g1)):
                _, lsz, w = _sb_tile(qhs[hh], kj, diag, strict, u_strict, r_scr[hh, j])
                g = w * _dot_nt(dohs[hh], vj)
                ghi, glo = _split2(g)
                gpre = _dot(ghi, u_pre) + _dot(glo, u_pre) + pre_g
                sig = jnp.exp(lsz)
                dz = g * (1.0 - sig) - gpre * sig
                if diag:
                    dz = jnp.where(strict, dz, 0.0)
                dzb = dz.astype(BF16)
                dv = dv + _dot_tn(w.astype(BF16), dohs[hh])
                dk = dk + _dot_tn(dzb, qhs[hh])
                dqa = dqa + _dot(dzb, jnp.where(hms[hh], kf, 0.0).astype(BF16))
                pre.append(pre_g + jnp.sum(g, axis=1, keepdims=True))
            dv_ref[rows, :] += dv
            dk_ref[rows, :] += dk
            return pre[0], pre[1], dqa

        carry = lax.fori_loop(jstop + 1, i, lambda j, c: step(j, c, False), (zero, zero, jnp.zeros((SB_BLK, LANES), F32)))
        _, _, dq = step(i, carry, True)
        dq_ref[...] = dq * SB_SCALE

    return pl.pallas_call(
        body, name="sb_bwd", grid=(4, nq),
        in_specs=[pl.BlockSpec((SB_BLK, LANES), lambda p, i: (i, p)),
                  pl.BlockSpec((t, LANES), lambda p, i: (0, 4 + p)),
                  pl.BlockSpec((t, LANES), lambda p, i: (0, 8 + p)),
                  pl.BlockSpec((SB_BLK, LANES), lambda p, i: (i, p))] + [HBM_SPEC] * nm,
        out_specs=[pl.BlockSpec((SB_BLK, LANES), lambda p, i: (i, p)),
                   pl.BlockSpec((t, LANES), lambda p, i: (0, p)),
                   pl.BlockSpec((t, LANES), lambda p, i: (0, p))] + [HBM_SPEC] * nm,
        out_shape=[jax.ShapeDtypeStruct((t, SB_WIDTH), F32)] * 3
        + [jax.ShapeDtypeStruct((3,) + a.shape[1:], a.dtype) for a in ps],
        scratch_shapes=[pltpu.VMEM((2, nq, SB_BLK, 1), F32), pltpu.SemaphoreType.DMA((3 * nm,)),
                        pltpu.SemaphoreType.DMA((3 * nm,))],
        compiler_params=_cparams("arbitrary", "arbitrary"),
    )(proj, proj, proj, dcat, *ps)


def _gla_consts():
    r = lax.broadcasted_iota(jnp.int32, (CHUNK, CHUNK), 0)
    c = lax.broadcasted_iota(jnp.int32, (CHUNK, CHUNK), 1)
    causal = c <= r
    lc = causal.astype(BF16)
    lct = (c >= r).astype(BF16)
    rowid = lax.broadcasted_iota(jnp.int32, (CHUNK, 1), 0)
    lane = lax.broadcasted_iota(jnp.int32, (1, LANES), 1)
    sr = lax.broadcasted_iota(jnp.int32, (LANES, 2 * LANES), 0)
    sc = lax.broadcasted_iota(jnp.int32, (LANES, 2 * LANES), 1)
    blockdiag = (sr // 64) == (sc // LANES)
    return causal, lc, lct, rowid, lane, blockdiag


def _dot3(u, x):
    hi, mid, lo = _split3(x)
    return _dot(u, hi) + _dot(u, mid) + _dot(u, lo)


def _row_to_col(row):
    return jnp.transpose(jnp.broadcast_to(row, (LANES, LANES)))


def _gla_gates(ga_ref, gup_ref, gbias_ref):
    pre = _dot(ga_ref[...].astype(BF16), gup_ref[...].astype(BF16)) + gbias_ref[...]
    log_a = (jnp.minimum(pre, 0.0) - jnp.log(1.0 + jnp.exp(-jnp.abs(pre)))) / GLA_TAU
    return pre, log_a


def _gla_chunk_terms(g2, q2, k2, lc, rowid):
    b = _dot3(lc, g2)
    b_ref = jnp.sum(jnp.where(rowid == CHUNK // 2 - 1, b, 0.0), axis=0, keepdims=True)
    b_last = jnp.sum(jnp.where(rowid == CHUNK - 1, b, 0.0), axis=0, keepdims=True)
    qs = q2 * GLA_SCALE
    e_q = jnp.exp(b - b_ref)
    e_k = jnp.exp(b_ref - b)
    e_d = jnp.exp(b_last - b)
    e_b = jnp.exp(b)
    decay = jnp.exp(b_last)
    return dict(qs=qs, e_q=e_q, e_k=e_k, e_d=e_d, e_b=e_b, decay=decay,
                qi=qs * e_q, ki=k2 * e_k, kd=k2 * e_d, qb=qs * e_b)


def _gla_fwd(proj, gate_up_pad, gate_bias, gnorm, t, gathered, shapes):
    nsteps = t // GLA_ROWS
    cps = GLA_ROWS // CHUNK
    nm = len(gathered)

    def body(q_ref, k_ref, v_ref, gg_ref, ga_ref, gup_ref, gbias_ref, gn_ref, *rest):
        gin_refs = rest[:nm]
        o_ref, oraw_ref, st_ref = rest[nm:nm + 3]
        gout_refs = rest[nm + 3:2 * nm + 3]
        s_scr, send_sems, recv_sems = rest[2 * nm + 3:]
        i = pl.program_id(0)

        @pl.when(i == 0)
        def _():
            sends, _ = _gather_d2d(gin_refs, gout_refs, shapes, send_sems, recv_sems, 0)
            for cp in sends:
                cp.start()

        @pl.when(i == nsteps - 1)
        def _():
            sends, recvs = _gather_d2d(gin_refs, gout_refs, shapes, send_sems, recv_sems, 0)
            for cp in recvs:
                cp.wait_recv()
            for cp in sends:
                cp.wait_send()

        @pl.when(i == 0)
        def _():
            s_scr[...] = jnp.zeros_like(s_scr)

        causal, lc, _, rowid, lane, blockdiag = _gla_consts()
        _, log_a = _gla_gates(ga_ref, gup_ref, gbias_ref)
        gn = gn_ref[...]
        for cc in range(cps):
            rows = slice(cc * CHUNK, (cc + 1) * CHUNK)
            for p in range(2):
                kl = slice(p * LANES, (p + 1) * LANES)
                vl = slice(p * 2 * LANES, (p + 1) * 2 * LANES)
                tm = _gla_chunk_terms(log_a[rows, kl], q_ref[rows, kl], k_ref[rows, kl], lc, rowid)
                v2 = v_ref[rows, vl]
                v2b = v2.astype(BF16)
                s_prev = s_scr[p]
                st_ref[cc, p] = s_prev
                kib = tm["ki"].astype(BF16)
                o_inter = _dot(tm["qb"].astype(BF16), s_prev.astype(BF16))
                outs = []
                for hh in range(2):
                    hm = (lane // 64) == hh
                    a = _dot_nt(jnp.where(hm, tm["qi"], 0.0).astype(BF16), kib)
                    a = jnp.where(causal, a, 0.0)
                    outs.append(_dot(a.astype(BF16), v2b[:, hh * LANES:(hh + 1) * LANES]))
                o2 = jnp.concatenate(outs, axis=1) + o_inter
                upd = jnp.where(blockdiag, _dot_tn(tm["kd"].astype(BF16), v2b), 0.0)
                dcol = _row_to_col(tm["decay"])
                s_scr[p] = s_prev * jnp.concatenate([dcol, dcol], axis=1) + upd
                oraw_ref[rows, vl] = o2
                for hh in range(2):
                    hl = slice(hh * LANES, (hh + 1) * LANES)
                    oh = o2[:, hl]
                    gl = slice(p * 2 * LANES + hh * LANES, p * 2 * LANES + (hh + 1) * LANES)
                    rinv = lax.rsqrt(jnp.mean(oh * oh, axis=1, keepdims=True) + RMS_EPS)
                    gg = gg_ref[rows, gl]
                    o_ref[rows, gl] = (oh * rinv * gn * (gg * _sigmoid(gg))).astype(BF16)

    cb = lambda w, idx: pl.BlockSpec((GLA_ROWS, w), lambda i: (i, idx))
    full = lambda shp: pl.BlockSpec(shp, lambda i: tuple(0 for _ in shp))
    return pl.pallas_call(
        body, name="gla_fwd", grid=(nsteps,),
        in_specs=[cb(256, 6), cb(256, 7), cb(512, 4), cb(512, 5), cb(128, 24),
                  full((LANES, GLA_KEYS)), full((1, GLA_KEYS)), full((1, LANES))] + [HBM_SPEC] * nm,
        out_specs=[pl.BlockSpec((GLA_ROWS, GLA_WIDTH), lambda i: (i, 0)),
                   pl.BlockSpec((GLA_ROWS, GLA_WIDTH), lambda i: (i, 0)),
                   pl.BlockSpec((cps, 2, LANES, 2 * LANES), lambda i: (i, 0, 0, 0))] + [HBM_SPEC] * nm,
        out_shape=[jax.ShapeDtypeStruct((t, GLA_WIDTH), BF16), jax.ShapeDtypeStruct((t, GLA_WIDTH), F32),
                   jax.ShapeDtypeStruct((t // CHUNK, 2, LANES, 2 * LANES), F32)]
        + [jax.ShapeDtypeStruct(a.shape, a.dtype) for a in gathered],
        input_output_aliases={8 + m: 3 + m for m in range(nm)},
        scratch_shapes=[pltpu.VMEM((2, LANES, 2 * LANES), F32), pltpu.SemaphoreType.DMA((3 * nm,)),
                        pltpu.SemaphoreType.DMA((3 * nm,))],
        compiler_params=_cparams("arbitrary"),
    )(proj, proj, proj, proj, proj, gate_up_pad, gate_bias, gnorm, *gathered)


def _gla_bwd(proj, dcat, oraw, states, gate_up_pad, gate_bias, gnorm, t):
    nsteps = t // GLA_ROWS
    cps = GLA_ROWS // CHUNK
    wout = 2 * GLA_KEYS + 2 * GLA_WIDTH + LANES

    def body(q_ref, k_ref, v_ref, gg_ref, ga_ref, do_ref, oraw_ref, st_ref, gup_ref, gbias_ref, gn_ref,
             d_ref, dgup_ref, dgbias_ref, dgn_ref, ds_scr, dpre_scr):
        i = pl.program_id(0)

        @pl.when(i == 0)
        def _():
            ds_scr[...] = jnp.zeros_like(ds_scr)
            dgup_ref[...] = jnp.zeros_like(dgup_ref)
            dgbias_ref[...] = jnp.zeros_like(dgbias_ref)
            dgn_ref[...] = jnp.zeros_like(dgn_ref)

        causal, lc, lct, rowid, lane, blockdiag = _gla_consts()
        pre, log_a = _gla_gates(ga_ref, gup_ref, gbias_ref)
        gn = gn_ref[...]
        dgn = jnp.zeros((1, LANES), F32)
        for cc in reversed(range(cps)):
            rows = slice(cc * CHUNK, (cc + 1) * CHUNK)
            for p in range(2):
                kl = slice(p * LANES, (p + 1) * LANES)
                vl = slice(p * 2 * LANES, (p + 1) * 2 * LANES)
                tm = _gla_chunk_terms(log_a[rows, kl], q_ref[rows, kl], k_ref[rows, kl], lc, rowid)
                v2b = v_ref[rows, vl].astype(BF16)
                s_prev = st_ref[cc, p]
                ds2 = ds_scr[p]
                dos = []
                for hh in range(2):
                    gl = slice(p * 2 * LANES + hh * LANES, p * 2 * LANES + (hh + 1) * LANES)
                    oh = oraw_ref[rows, gl]
                    rinv = lax.rsqrt(jnp.mean(oh * oh, axis=1, keepdims=True) + RMS_EPS)
                    on = oh * rinv
                    gg = gg_ref[rows, gl]
                    sg = _sigmoid(gg)
                    sil = gg * sg
                    dgo = do_ref[rows, gl]
                    d_ref[rows, 2 * GLA_KEYS + GLA_WIDTH + gl.start:2 * GLA_KEYS + GLA_WIDTH + gl.stop] = (
                        dgo * on * gn * (sg * (1.0 + gg * (1.0 - sg))))
                    dgn = dgn + jnp.sum(dgo * sil * on, axis=0, keepdims=True)
                    don = dgo * sil * gn
                    dos.append(rinv * (don - on * jnp.mean(don * on, axis=1, keepdims=True)))
                do2b = jnp.concatenate(dos, axis=1).astype(BF16)
                qib = tm["qi"].astype(BF16)
                kib = tm["ki"].astype(BF16)
                kdb = tm["kd"].astype(BF16)
                qbb = tm["qb"].astype(BF16)
                ds2b = ds2.astype(BF16)
                dqi = jnp.zeros((CHUNK, LANES), F32)
                dki = jnp.zeros((CHUNK, LANES), F32)
                dvs = []
                for hh in range(2):
                    hm = (lane // 64) == hh
                    hl = slice(hh * LANES, (hh + 1) * LANES)
                    a = jnp.where(causal, _dot_nt(jnp.where(hm, tm["qi"], 0.0).astype(BF16), kib), 0.0).astype(BF16)
                    da = jnp.where(causal, _dot_nt(do2b[:, hl], v2b[:, hl]), 0.0).astype(BF16)
                    dvs.append(_dot_tn(a, do2b[:, hl]))
                    dqi = dqi + jnp.where(hm, _dot(da, kib), 0.0)
                    dki = dki + jnp.where(hm, _dot_tn(da, qib), 0.0)
                dv2 = jnp.concatenate(dvs, axis=1) + _dot(kdb, ds2b)
                dqb = _dot_nt(do2b, s_prev.astype(BF16))
                dkd = _dot_nt(v2b, ds2b)
                dcol = _row_to_col(tm["decay"])
                dsp = jnp.where(blockdiag, _dot_tn(qbb, do2b), 0.0) + ds2 * jnp.concatenate([dcol, dcol], axis=1)
                ddecay_col = jnp.sum(ds2 * s_prev, axis=1, keepdims=True)
                ddecay_row = jnp.transpose(jnp.broadcast_to(ddecay_col, (LANES, LANES)))[0:1, :]
                ds_scr[p] = dsp
                dqs = dqi * tm["e_q"] + dqb * tm["e_b"]
                dk = dki * tm["e_k"] + dkd * tm["e_d"]
                t_qi = dqi * tm["qi"]
                t_ki = dki * tm["ki"]
                t_kd = dkd * tm["kd"]
                db = t_qi - t_ki + dqb * tm["qb"] - t_kd
                db_ref = jnp.sum(t_ki - t_qi, axis=0, keepdims=True)
                db_last = jnp.sum(t_kd, axis=0, keepdims=True) + ddecay_row * tm["decay"]
                db = db + jnp.where(rowid == CHUNK // 2 - 1, db_ref, 0.0) + jnp.where(rowid == CHUNK - 1, db_last, 0.0)
                dg = _dot3(lct, db)
                d_ref[rows, p * LANES:(p + 1) * LANES] = dqs * GLA_SCALE
                d_ref[rows, GLA_KEYS + p * LANES:GLA_KEYS + (p + 1) * LANES] = dk
                d_ref[rows, 2 * GLA_KEYS + p * 2 * LANES:2 * GLA_KEYS + (p + 1) * 2 * LANES] = dv2
                dpre_scr[rows, kl] = dg
        dpre = dpre_scr[...] * (1.0 / GLA_TAU) * _sigmoid(-pre)
        dpb = dpre.astype(BF16)
        dgn_ref[...] += dgn
        dgbias_ref[...] += jnp.sum(dpre, axis=0, keepdims=True)
        dgup_ref[...] += _dot_tn(ga_ref[...].astype(BF16), dpb)
        d_ref[:, 2 * GLA_KEYS + 2 * GLA_WIDTH:] = _dot_nt(dpb, gup_ref[...].astype(BF16))

    rev = lambda i: nsteps - 1 - i
    cb = lambda w, idx: pl.BlockSpec((GLA_ROWS, w), lambda i: (rev(i), idx))
    full = lambda shp: pl.BlockSpec(shp, lambda i: tuple(0 for _ in shp))
    return pl.pallas_call(
        body, name="gla_bwd", grid=(nsteps,),
        in_specs=[cb(256, 6), cb(256, 7), cb(512, 4), cb(512, 5), cb(128, 24), cb(512, 1), cb(512, 0),
                  pl.BlockSpec((cps, 2, LANES, 2 * LANES), lambda i: (rev(i), 0, 0, 0)),
                  full((LANES, GLA_KEYS)), full((1, GLA_KEYS)), full((1, LANES))],
        out_specs=[pl.BlockSpec((GLA_ROWS, wout), lambda i: (rev(i), 0)),
                   full((LANES, GLA_KEYS)), full((1, GLA_KEYS)), full((1, LANES))],
        out_shape=[jax.ShapeDtypeStruct((t, wout), F32), jax.ShapeDtypeStruct((LANES, GLA_KEYS), F32),
                   jax.ShapeDtypeStruct((1, GLA_KEYS), F32), jax.ShapeDtypeStruct((1, LANES), F32)],
        scratch_shapes=[pltpu.VMEM((2, LANES, 2 * LANES), F32), pltpu.VMEM((GLA_ROWS, GLA_KEYS), F32)],
        compiler_params=_cparams("arbitrary"),
    )(proj, proj, proj, proj, proj, dcat, oraw, states, gate_up_pad, gate_bias, gnorm)


def _ln_stats(r):
    mu = jnp.mean(r, axis=1, keepdims=True)
    xc = r - mu
    rstd = lax.rsqrt(jnp.mean(xc * xc, axis=1, keepdims=True) + LN_EPS)
    return xc * rstd, rstd


def _ln_bwd(dy_g, xhat, rstd):
    return rstd * (dy_g - jnp.mean(dy_g, axis=1, keepdims=True) - xhat * jnp.mean(dy_g * xhat, axis=1, keepdims=True))


def _outproj_ln1(sb_o, gla_o, x, w_out, g1, b1, t, tm=256):
    def body(sb_ref, gl_ref, x_ref, w_ref, g_ref, b_ref, xhat_ref, rstd_ref, h_ref):
        mix = _dot(sb_ref[...], w_ref[0:SB_WIDTH, :]) + _dot(gl_ref[...], w_ref[SB_WIDTH:, :])
        xhat, rstd = _ln_stats(ALPHA * x_ref[...] + mix)
        xhat_ref[...] = xhat
        rstd_ref[...] = rstd
        h_ref[...] = (xhat * g_ref[...] + b_ref[...]).astype(BF16)

    row = lambda w: pl.BlockSpec((tm, w), lambda i: (i, 0))
    full = lambda shp: pl.BlockSpec(shp, lambda i: (0, 0))
    return pl.pallas_call(
        body, name="outproj_ln1", grid=(t // tm,),
        in_specs=[row(SB_WIDTH), row(GLA_WIDTH), row(D_MODEL), full((D_MODEL, D_MODEL)), full((1, D_MODEL)), full((1, D_MODEL))],
        out_specs=[row(D_MODEL), row(1), row(D_MODEL)],
        out_shape=[jax.ShapeDtypeStruct((t, D_MODEL), F32), jax.ShapeDtypeStruct((t, 1), F32),
                   jax.ShapeDtypeStruct((t, D_MODEL), BF16)],
        compiler_params=_cparams("parallel"),
    )(sb_o, gla_o, x, w_out, g1, b1)


_INV_SQRT2 = 1.0 / math.sqrt(2.0)
_INV_SQRT2PI = 1.0 / math.sqrt(2.0 * math.pi)


def _conv3(xs, w_ref, b_ref, half):
    return (w_ref[half, 0:1, :] * pltpu.roll(xs, 2, 0) + w_ref[half, 1:2, :] * pltpu.roll(xs, 1, 0)
            + w_ref[half, 2:3, :] * xs + b_ref[half])


def _conv_gelu_fwd(up3, conv_w3, conv_b3, t, tr=512, ct=256):
    nct = D_FF // ct
    hb = tr // 8

    def body(cur_ref, prev_ref, w_ref, b_ref, gm_ref):
        i = pl.program_id(1)
        keep = (i > 0).astype(F32)
        us = []
        for half in range(2):
            xs = jnp.concatenate([prev_ref[half] * keep, cur_ref[half]], axis=0)
            us.append(_conv3(xs, w_ref, b_ref, half)[8:, :])
        a, c = us
        gelu = 0.5 * a * (1.0 + lax.erf(a * _INV_SQRT2))
        gm_ref[...] = (gelu * c).astype(BF16)

    return pl.pallas_call(
        body, name="conv_gelu_fwd", grid=(nct, t // tr),
        in_specs=[pl.BlockSpec((2, tr, ct), lambda j, i: (0, i, j)),
                  pl.BlockSpec((2, 8, ct), lambda j, i: (0, jnp.maximum(i * hb - 1, 0), j)),
                  pl.BlockSpec((2, 3, ct), lambda j, i: (0, 0, j)),
                  pl.BlockSpec((2, 1, ct), lambda j, i: (0, 0, j))],
        out_specs=pl.BlockSpec((tr, ct), lambda j, i: (i, j)),
        out_shape=jax.ShapeDtypeStruct((t, D_FF), BF16),
        compiler_params=_cparams("parallel", "parallel"),
    )(up3, up3, conv_w3, conv_b3)


def _conv_gelu_bwd(up3, dgm, conv_w3, conv_b3, t, tr=512, ct=256):
    nct = D_FF // ct
    nrt = t // tr
    hb = tr // 8
    n = tr + 16

    def body(cur_ref, prev_ref, next_ref, dg_ref, dgn_ref, w_ref, b_ref, dup_ref, dcw_ref, dcb_ref):
        i = pl.program_id(1)

        @pl.when(i == 0)
        def _():
            dcw_ref[...] = jnp.zeros_like(dcw_ref)
            dcb_ref[...] = jnp.zeros_like(dcb_ref)

        keep_prev = (i > 0).astype(F32)
        keep_next = (i < nrt - 1).astype(F32)
        xs, xm1, xm2, us = [], [], [], []
        for half in range(2):
            x = jnp.concatenate([prev_ref[half] * keep_prev, cur_ref[half], next_ref[half]], axis=0)
            xs.append(x)
            xm1.append(pltpu.roll(x, 1, 0))
            xm2.append(pltpu.roll(x, 2, 0))
            us.append(w_ref[half, 0:1, :] * xm2[half] + w_ref[half, 1:2, :] * xm1[half]
                      + w_ref[half, 2:3, :] * x + b_ref[half])
        a, c = us
        dg = jnp.concatenate([jnp.zeros((8, ct), F32), dg_ref[...], dgn_ref[...] * keep_next], axis=0)
        cdf = 0.5 * (1.0 + lax.erf(a * _INV_SQRT2))
        pdf = jnp.exp(-0.5 * a * a) * _INV_SQRT2PI
        dus = [dg * c * (cdf + a * pdf), dg * (a * cdf)]
        rid = lax.broadcasted_iota(jnp.int32, (8, 1), 0)
        for half in range(2):
            du = dus[half]
            dup = (w_ref[half, 2:3, :] * du + w_ref[half, 1:2, :] * pltpu.roll(du, n - 1, 0)
                   + w_ref[half, 0:1, :] * pltpu.roll(du, n - 2, 0))
            dup_ref[half] = dup[8:tr + 8, :].astype(BF16)
            duc = du[8:tr + 8, :]
            s0 = jnp.sum(duc * xm2[half][8:tr + 8, :], axis=0, keepdims=True)
            s1 = jnp.sum(duc * xm1[half][8:tr + 8, :], axis=0, keepdims=True)
            s2 = jnp.sum(duc * xs[half][8:tr + 8, :], axis=0, keepdims=True)
            dcw_ref[half] += jnp.where(rid == 0, s0, jnp.where(rid == 1, s1, jnp.where(rid == 2, s2, 0.0)))
            dcb_ref[half] += jnp.sum(duc, axis=0, keepdims=True)

    last8 = t // 8 - 1
    return pl.pallas_call(
        body, name="conv_gelu_bwd", grid=(nct, nrt),
        in_specs=[pl.BlockSpec((2, tr, ct), lambda j, i: (0, i, j)),
                  pl.BlockSpec((2, 8, ct), lambda j, i: (0, jnp.maximum(i * hb - 1, 0), j)),
                  pl.BlockSpec((2, 8, ct), lambda j, i: (0, jnp.minimum((i + 1) * hb, last8), j)),
                  pl.BlockSpec((tr, ct), lambda j, i: (i, j)),
                  pl.BlockSpec((8, ct), lambda j, i: (jnp.minimum((i + 1) * hb, last8), j)),
                  pl.BlockSpec((2, 3, ct), lambda j, i: (0, 0, j)),
                  pl.BlockSpec((2, 1, ct), lambda j, i: (0, 0, j))],
        out_specs=[pl.BlockSpec((2, tr, ct), lambda j, i: (0, i, j)),
                   pl.BlockSpec((2, 8, ct), lambda j, i: (0, 0, j)),
                   pl.BlockSpec((2, 1, ct), lambda j, i: (0, 0, j))],
        out_shape=[jax.ShapeDtypeStruct((2, t, D_FF), BF16), jax.ShapeDtypeStruct((2, 8, D_FF), F32),
                   jax.ShapeDtypeStruct((2, 1, D_FF), F32)],
        compiler_params=_cparams("parallel", "arbitrary"),
    )(up3, up3, up3, dgm, dgm, conv_w3, conv_b3)


def _down_ln2_loss(gm, w_down, xhat1, g1, b1, g2, b2, target, t, tm=256):
    def body(gm_ref, w_ref, xh_ref, g1_ref, b1_ref, g2_ref, b2_ref, tg_ref, dr_ref, loss_ref, dg_ref, db_ref):
        i = pl.program_id(0)

        @pl.when(i == 0)
        def _():
            loss_ref[...] = jnp.zeros_like(loss_ref)
            dg_ref[...] = jnp.zeros_like(dg_ref)
            db_ref[...] = jnp.zeros_like(db_ref)

        h = xh_ref[...] * g1_ref[...] + b1_ref[...]
        xhat, rstd = _ln_stats(ALPHA * h + _dot(gm_ref[...], w_ref[...]))
        err = xhat * g2_ref[...] + b2_ref[...] - tg_ref[...]
        loss_ref[...] += 0.5 * jnp.sum(jnp.sum(err * err, axis=1, keepdims=True), axis=0, keepdims=True) / D_MODEL
        dy = err * (1.0 / D_MODEL)
        dg_ref[...] += jnp.sum(dy * xhat, axis=0, keepdims=True)
        db_ref[...] += jnp.sum(dy, axis=0, keepdims=True)
        dr_ref[...] = _ln_bwd(dy * g2_ref[...], xhat, rstd)

    row = lambda w: pl.BlockSpec((tm, w), lambda i: (i, 0))
    full = lambda shp: pl.BlockSpec(shp, lambda i: (0, 0))
    vec = full((1, D_MODEL))
    return pl.pallas_call(
        body, name="down_ln2_loss", grid=(t // tm,),
        in_specs=[row(D_FF), full((D_FF, D_MODEL)), row(D_MODEL), vec, vec, vec, vec, row(D_MODEL)],
        out_specs=[row(D_MODEL), full((1, 1)), vec, vec],
        out_shape=[jax.ShapeDtypeStruct((t, D_MODEL), F32), jax.ShapeDtypeStruct((1, 1), F32),
                   jax.ShapeDtypeStruct((1, D_MODEL), F32), jax.ShapeDtypeStruct((1, D_MODEL), F32)],
        compiler_params=_cparams("arbitrary"),
    )(gm, w_down, xhat1, g1, b1, g2, b2, target)


def _dh_ln1_bwd(dup3, w_up4, dr2, xhat1, rstd1, g1, t, tm=512):
    tk = 2 * D_FF // 4
    nkh = D_FF // tk
    nk = 2 * nkh

    def body(a_ref, w_ref, dr2_ref, xh_ref, rs_ref, g_ref, dr1_ref, dg_ref, db_ref, acc_ref):
        i = pl.program_id(0)
        kk = pl.program_id(1)

        @pl.when((i == 0) & (kk == 0))
        def _():
            dg_ref[...] = jnp.zeros_like(dg_ref)
            db_ref[...] = jnp.zeros_like(db_ref)

        part = _dot_nt(a_ref[...], w_ref[...])

        @pl.when(kk == 0)
        def _():
            acc_ref[...] = part

        @pl.when(kk > 0)
        def _():
            acc_ref[...] += part

        @pl.when(kk == nk - 1)
        def _():
            dh = acc_ref[...] + ALPHA * dr2_ref[...]
            xhat = xh_ref[...]
            dg_ref[...] += jnp.sum(dh * xhat, axis=0, keepdims=True)
            db_ref[...] += jnp.sum(dh, axis=0, keepdims=True)
            dr1_ref[...] = _ln_bwd(dh * g_ref[...], xhat, rs_ref[...])

    row = lambda w: pl.BlockSpec((tm, w), lambda i, kk: (i, 0))
    vec = pl.BlockSpec((1, D_MODEL), lambda i, kk: (0, 0))
    return pl.pallas_call(
        body, name="dh_ln1_bwd", grid=(t // tm, nk),
        in_specs=[pl.BlockSpec((None, tm, tk), lambda i, kk: (kk // nkh, i, kk % nkh)),
                  pl.BlockSpec((None, D_MODEL, tk), lambda i, kk: (kk, 0, 0)),
                  row(D_MODEL), row(D_MODEL), row(1), vec],
        out_specs=[row(D_MODEL), vec, vec],
        out_shape=[jax.ShapeDtypeStruct((t, D_MODEL), F32), jax.ShapeDtypeStruct((1, D_MODEL), F32),
                   jax.ShapeDtypeStruct((1, D_MODEL), F32)],
        scratch_shapes=[pltpu.VMEM((tm, D_MODEL), F32)],
        compiler_params=_cparams("arbitrary", "arbitrary"),
    )(dup3, w_up4, dr2, xhat1, rstd1, g1)


def _adamw(w, g, m, v, name):
    rows, cols = w.shape
    tr = rows
    for cand in (256, 128, 64, 32, 16, 8):
        if rows % cand == 0 and rows > cand:
            tr = cand
            break
    c1 = 1.0 / (1.0 - ADAM_B1 ** ADAM_STEP)
    c2 = 1.0 / (1.0 - ADAM_B2 ** ADAM_STEP)

    def body(w_ref, g_ref, m_ref, v_ref, d_ref, nm_ref, nv_ref):
        gv = g_ref[...]
        nm = ADAM_B1 * m_ref[...] + (1.0 - ADAM_B1) * gv
        nv = ADAM_B2 * v_ref[...] + (1.0 - ADAM_B2) * (gv * gv)
        d_ref[...] = -ADAM_LR * ((nm * c1) / (jnp.sqrt(nv * c2) + ADAM_EPS) + ADAM_WD * w_ref[...])
        nm_ref[...] = nm
        nv_ref[...] = nv

    spec = pl.BlockSpec((tr, cols), lambda i: (i, 0))
    out = jax.ShapeDtypeStruct((rows, cols), F32)
    return pl.pallas_call(
        body, name=name, grid=(rows // tr,), in_specs=[spec] * 4, out_specs=[spec] * 3, out_shape=[out] * 3,
        compiler_params=_cparams("parallel"),
    )(w, g, m, v)


def _local_step(x, target, w_in_p, late_shards, gate_up_pad, gate_bias, gnorm, ln1_g, ln1_b, conv_w3, conv_b3,
                ln2_g, ln2_b, c_arr, kc_arr):
    t = x.shape[0]
    tq = min(t, 1024)
    proj = _mm(x, w_in_p, m=t, n=IN_PAD, k=D_MODEL, tm=tq, tn=640, tk=D_MODEL, name="proj")
    sb_o, *partly = _sb_fwd(proj, t, late_shards, LATE_SHAPES)
    gla_o, oraw, states, w_up4, w_out4, w_down4 = _gla_fwd(proj, gate_up_pad, gate_bias, gnorm, t, partly, LATE_SHAPES)
    w_out = w_out4.reshape(D_MODEL, D_MODEL)
    w_down = w_down4.reshape(D_FF, D_MODEL)
    xhat1, rstd1, h_bf = _outproj_ln1(sb_o, gla_o, x, w_out, ln1_g, ln1_b, t)
    up3 = _mm(h_bf, w_up4, m=t, n=2 * D_FF, k=D_MODEL, tm=tq, tn=W_UP_S, tk=D_MODEL, name="up",
              b_spec=pl.BlockSpec((None, D_MODEL, W_UP_S), lambda i, j, kk: (j, 0, 0)),
              o_spec=pl.BlockSpec((None, tq, W_UP_S), lambda i, j, kk: (j // 2, i, j % 2)),
              out_shape=jax.ShapeDtypeStruct((2, t, D_FF), F32))
    gm = _conv_gelu_fwd(up3, conv_w3, conv_b3, t, tr=tq)
    dr2, loss, dln2_g, dln2_b = _down_ln2_loss(gm, w_down, xhat1, ln1_g, ln1_b, ln2_g, ln2_b, target, t)
    dgm = _mm(dr2, w_down, m=t, n=D_FF, k=D_MODEL, tm=tq, tn=W_UP_S, tk=D_MODEL, tb=True, name="dgm")
    dw_down = _mm(gm, dr2, m=D_FF, n=D_MODEL, k=t, tm=W_UP_S, tn=D_MODEL, tk=512, ta=True, name="dw_down")
    dup3, dcw, dcb = _conv_gelu_bwd(up3, dgm, conv_w3, conv_b3, t)
    dr1, dln1_g, dln1_b = _dh_ln1_bwd(dup3, w_up4, dr2, xhat1, rstd1, ln1_g, t)
    dw_up4 = _mm(h_bf, dup3, m=D_MODEL, n=2 * D_FF, k=t, tm=512, tn=W_UP_S, tk=tq, ta=True, name="dw_up",
                 b_spec=pl.BlockSpec((None, tq, W_UP_S), lambda i, j, kk: (j // 2, kk, j % 2)),
                 o_spec=pl.BlockSpec((None, 512, W_UP_S), lambda i, j, kk: (j, i, 0)),
                 out_shape=jax.ShapeDtypeStruct((4, D_MODEL, W_UP_S), F32))
    dcat = _mm(dr1, w_out, m=t, n=D_MODEL, k=D_MODEL, tm=tq, tn=512, tk=D_MODEL, tb=True, name="dcat")
    dw_out_sb = _mm(sb_o, dr1, m=SB_WIDTH, n=D_MODEL, k=t, tm=512, tn=D_MODEL, tk=tq, ta=True, name="dw_out_sb")
    dw_out_gla = _mm(gla_o, dr1, m=GLA_WIDTH, n=D_MODEL, k=t, tm=512, tn=D_MODEL, tk=tq, ta=True, name="dw_out_gla")
    gs = [dw_up4, jnp.concatenate([dw_out_sb, dw_out_gla], axis=0).reshape(4, W_OUT_S, D_MODEL),
          dw_down.reshape(4, W_DOWN_S, D_MODEL)]
    from_sib = _exchange_sibling(gs, LATE_SHAPES, "exchange_sibling_late")
    ps = [_add_sibling(gs[m], from_sib[m], c_arr, LATE_ADD_ROWS[m], "add_sibling_late_%d" % m) for m in range(3)]
    dsq, dsk, dsv, *others = _sb_bwd(proj, dcat, t, ps)
    late_sums = [_add_chips(ps[m], others[m], kc_arr, LATE_ADD_ROWS[m], "add_chips_late_%d" % m) for m in range(3)]
    dgla, dgup_pad, dgbias, dgnorm = _gla_bwd(proj, dcat, oraw, states, gate_up_pad, gate_bias, gnorm, t)
    dproj = jnp.concatenate([dsq, dsk, dsv, dgla], axis=1).astype(BF16)
    dx = _mm(dproj, w_in_p, m=t, n=D_MODEL, k=IN_PAD, tm=tq, tn=D_MODEL, tk=640, tb=True, add=dr1, add_scale=ALPHA, name="dx")
    dw_in_p = _mm(x, dproj, m=D_MODEL, n=IN_PAD, k=t, tm=512, tn=IN_PAD, tk=512, ta=True, name="dw_in")
    small = dict(
        gate_up=dgup_pad[:GATE_RANK], gate_bias=dgbias, gla_norm_g=dgnorm, ln1_g=dln1_g, ln1_b=dln1_b,
        conv_w=jnp.concatenate([dcw[0, :3], dcw[1, :3]], axis=1), conv_b=jnp.concatenate([dcb[0], dcb[1]], axis=1),
        ln2_g=dln2_g, ln2_b=dln2_b)
    return loss, dx, late_sums, dw_in_p, small


W_IN_S, W_UP_S, W_OUT_S, W_DOWN_S = IN_WIDTH // 4, 2 * D_FF // 4, D_MODEL // 4, D_FF // 4
SHARD_SHAPES = ((D_MODEL, W_IN_S), (D_MODEL, W_UP_S), (W_OUT_S, D_MODEL), (W_DOWN_S, D_MODEL))
ADD_ROWS = (256, 256, 128, 176)
LATE_SHAPES, LATE_ADD_ROWS = SHARD_SHAPES[1:], ADD_ROWS[1:]
SMALL_ROWS = 8
VEC_SIZES = (("gate_bias", GLA_KEYS), ("gla_norm_g", LANES), ("ln1_g", D_MODEL), ("ln1_b", D_MODEL),
             ("conv_b", 2 * D_FF), ("ln2_g", D_MODEL), ("ln2_b", D_MODEL))
SMALL_GRADS = VEC_SIZES + (("conv_w", 3 * 2 * D_FF), ("gate_up", GATE_RANK * GLA_KEYS), ("loss", 1))


def _rows(a):
    flat = a.reshape(-1)
    pad = (-flat.shape[0]) % D_MODEL
    if pad:
        flat = jnp.pad(flat, (0, pad))
    return flat.reshape(-1, D_MODEL)


def _pad_rows(a, rows):
    return jnp.pad(a, ((0, rows - a.shape[0]), (0, 0)))


def _pack_vec(d, sizes, rows):
    flat = jnp.concatenate([d[n].reshape(-1) for n, _ in sizes])
    return _pad_rows(_rows(flat), rows)


def _unpack_vec(v, sizes):
    flat = v.reshape(-1)
    out, o = {}, 0
    for n, size in sizes:
        out[n] = flat[o:o + size].reshape(1, size)
        o += size
    return out


VEC_ROWS = 16
GRAD_VEC_ROWS = 32


HBM_SPEC = pl.BlockSpec(memory_space=pltpu.HBM)


def _position():
    x, y, c = lax.axis_index("x"), lax.axis_index("y"), lax.axis_index("c")
    chips = [(1 - x, y), (x, 1 - y), (1 - x, 1 - y)]
    return x, y, c, chips


def _remote(src, dst, send_sems, recv_sems, k, to):
    return pltpu.make_async_remote_copy(src_ref=src, dst_ref=dst, send_sem=send_sems.at[k], recv_sem=recv_sems.at[k],
                                        device_id=to, device_id_type=MESH)


def _gather_ici(in_refs, out_refs, shapes, send_sems, recv_sems, local_sems):
    x, y, c, chips = _position()
    k_me = 2 * x + y
    local, sends, recvs = [], [], []
    for m, (rows, _) in enumerate(shapes):
        h = rows // 2
        local.append(pltpu.make_async_copy(in_refs[m], out_refs[m].at[k_me], local_sems.at[m]))
        for j, (cx, cy) in enumerate(chips):
            sends.append(_remote(in_refs[m].at[pl.ds(c * h, h), :], out_refs[m].at[k_me, pl.ds(c * h, h), :],
                                 send_sems, recv_sems, 3 * m + j, (cx, cy, c)))
            landed = out_refs[m].at[2 * cx + cy, pl.ds(c * h, h), :]
            recvs.append(_remote(landed, landed, send_sems, recv_sems, 3 * m + j, (x, y, c)))
    return local, sends, recvs


def _gather_d2d(src_refs, dst_refs, shapes, send_sems, recv_sems, base):
    x, y, c, chips = _position()
    sends, recvs = [], []
    for m, (rows, _) in enumerate(shapes):
        h = rows // 2
        for j, (cx, cy) in enumerate(chips):
            k = 2 * cx + cy
            sends.append(_remote(src_refs[m].at[k, pl.ds(c * h, h), :], dst_refs[m].at[k, pl.ds(c * h, h), :],
                                 send_sems, recv_sems, base + 3 * m + j, (x, y, 1 - c)))
            landed = dst_refs[m].at[k, pl.ds((1 - c) * h, h), :]
            recvs.append(_remote(landed, landed, send_sems, recv_sems, base + 3 * m + j, (x, y, c)))
    return sends, recvs


def _gather_weights(shards, small, shapes):
    nm = len(shards)
    n_ici = 3 * nm

    def body(*refs):
        in_refs, small_ref = refs[:nm], refs[nm]
        out_refs, osm_ref = refs[nm + 1:2 * nm + 1], refs[2 * nm + 1]
        send_sems, recv_sems, local_sems = refs[2 * nm + 2:]
        x, y, c, chips = _position()
        k_me = 2 * x + y
        local, sends, recvs = _gather_ici(in_refs, out_refs, shapes, send_sems, recv_sems, local_sems)
        local.append(pltpu.make_async_copy(small_ref, osm_ref.at[k_me], local_sems.at[nm]))
        for j, (cx, cy) in enumerate(chips):
            sends.append(_remote(small_ref, osm_ref.at[k_me], send_sems, recv_sems, n_ici + j, (cx, cy, c)))
        for cp in local + sends:
            cp.start()
        for cp in recvs:
            cp.wait_recv()
        fsends, frecvs = _gather_d2d(out_refs, out_refs, shapes, send_sems, recv_sems, n_ici + 3)
        for cp in fsends:
            cp.start()
        for j, (cx, cy) in enumerate(chips):
            k = 2 * cx + cy
            frecvs.append(_remote(osm_ref.at[k], osm_ref.at[k], send_sems, recv_sems, n_ici + j, (x, y, c)))
        for cp in frecvs:
            cp.wait_recv()
        for cp in sends + fsends:
            cp.wait_send()
        for cp in local:
            cp.wait()

    n_sems = 2 * n_ici + 3
    return pl.pallas_call(
        body, name="gather_weights", in_specs=[HBM_SPEC] * (nm + 1), out_specs=[HBM_SPEC] * (nm + 1),
        out_shape=[jax.ShapeDtypeStruct((4,) + s, BF16) for s in shapes]
        + [jax.ShapeDtypeStruct((4, SMALL_ROWS, D_MODEL), F32)],
        scratch_shapes=[pltpu.SemaphoreType.DMA((n_sems,)), pltpu.SemaphoreType.DMA((n_sems,)),
                        pltpu.SemaphoreType.DMA((nm + 1,))],
    )(*shards, small)


def _exchange_sibling(gs, shapes, name):
    nm = len(gs)

    def body(*refs):
        g_refs, r_refs = refs[:nm], refs[nm:2 * nm]
        send_sems, recv_sems = refs[2 * nm:]
        x, y, c, _ = _position()
        copies = []
        for m, (rows, _) in enumerate(shapes):
            h = rows // 2
            for k in range(4):
                copies.append(_remote(g_refs[m].at[k, pl.ds((1 - c) * h, h), :], r_refs[m].at[k], send_sems, recv_sems,
                                      4 * m + k, (x, y, 1 - c)))
        for cp in copies:
            cp.start()
        for cp in copies:
            cp.wait()

    return pl.pallas_call(
        body, name=name, in_specs=[HBM_SPEC] * nm, out_specs=[HBM_SPEC] * nm,
        out_shape=[jax.ShapeDtypeStruct((4, r // 2, cl), F32) for r, cl in shapes],
        scratch_shapes=[pltpu.SemaphoreType.DMA((4 * nm,)), pltpu.SemaphoreType.DMA((4 * nm,))],
    )(*gs)


def _add_sibling(g, r, c_arr, tr, name):
    _, rows, cols = g.shape
    nb = rows // 2 // tr

    def body(c_ref, g_ref, r_ref, o_ref):
        o_ref[...] = (g_ref[...] + r_ref[...]).astype(BF16)

    spec = pl.BlockSpec((None, tr, cols), lambda k, i, c: (k, i, 0))
    return pl.pallas_call(
        body, name=name,
        grid_spec=pltpu.PrefetchScalarGridSpec(
            num_scalar_prefetch=1, grid=(4, nb),
            in_specs=[pl.BlockSpec((None, tr, cols), lambda k, i, c: (k, c[0] * nb + i, 0)), spec], out_specs=spec),
        out_shape=jax.ShapeDtypeStruct((4, rows // 2, cols), BF16), compiler_params=_cparams("parallel", "parallel"),
    )(c_arr, g, r)


def _reduce_ici(p_refs, r_refs, send_sems, recv_sems):
    x, y, c, chips = _position()
    sends, recvs = [], []
    for m in range(len(p_refs)):
        for j, (cx, cy) in enumerate(chips):
            sends.append(_remote(p_refs[m].at[2 * cx + cy], r_refs[m].at[j], send_sems, recv_sems, 3 * m + j, (cx, cy, c)))
            recvs.append(_remote(r_refs[m].at[j], r_refs[m].at[j], send_sems, recv_sems, 3 * m + j, (x, y, c)))
    return sends, recvs


def _exchange_chips(ps, vec):
    nm = len(ps)
    n_ici = 3 * nm

    def body(*refs):
        p_refs, vec_ref = refs[:nm], refs[nm]
        r_refs, vrecv_ref = refs[nm + 1:2 * nm + 1], refs[2 * nm + 1]
        send_sems, recv_sems, local_sem = refs[2 * nm + 2:]
        x, y, c, _ = _position()
        my_id = 4 * x + 2 * y + c
        vloc = pltpu.make_async_copy(vec_ref, vrecv_ref.at[my_id], local_sem.at[0])
        vloc.start()
        sends, recvs = _reduce_ici(p_refs, r_refs, send_sems, recv_sems)
        for r in range(1, 8):
            peer = (1 - x if r & 4 else x, 1 - y if r & 2 else y, 1 - c if r & 1 else c)
            sends.append(_remote(vec_ref, vrecv_ref.at[my_id], send_sems, recv_sems, n_ici + r - 1, peer))
            recvs.append(_remote(vec_ref, vrecv_ref.at[0], send_sems, recv_sems, n_ici + r - 1, (x, y, c)))
        for cp in sends:
            cp.start()
        for cp in recvs:
            cp.wait_recv()
        for cp in sends:
            cp.wait_send()
        vloc.wait()

    n_sems = n_ici + 7
    return pl.pallas_call(
        body, name="exchange_chips", in_specs=[HBM_SPEC] * (nm + 1), out_specs=[HBM_SPEC] * (nm + 1),
        out_shape=[jax.ShapeDtypeStruct((3,) + p.shape[1:], p.dtype) for p in ps]
        + [jax.ShapeDtypeStruct((8, GRAD_VEC_ROWS, D_MODEL), F32)],
        scratch_shapes=[pltpu.SemaphoreType.DMA((n_sems,)), pltpu.SemaphoreType.DMA((n_sems,)), pltpu.SemaphoreType.DMA((1,))],
    )(*ps, vec)


def _add_chips(p, r, kc_arr, tr, name):
    _, h, cols = p.shape
    nb = h // tr

    def body(kc_ref, p_ref, r0_ref, r1_ref, r2_ref, o_ref):
        o_ref[...] = ((p_ref[...].astype(F32) + r0_ref[...].astype(F32)) + r1_ref[...].astype(F32)) + r2_ref[...].astype(F32)

    rspec = lambda j: pl.BlockSpec((None, tr, cols), lambda i, kc: (j, i, 0))
    return pl.pallas_call(
        body, name=name,
        grid_spec=pltpu.PrefetchScalarGridSpec(
            num_scalar_prefetch=1, grid=(nb,),
            in_specs=[pl.BlockSpec((None, tr, cols), lambda i, kc: (kc[0], i, 0)), rspec(0), rspec(1), rspec(2)],
            out_specs=pl.BlockSpec((tr, cols), lambda i, kc: (kc[1] * nb + i, 0))),
        out_shape=jax.ShapeDtypeStruct((2 * h, cols), F32), compiler_params=_cparams("parallel"),
    )(kc_arr, p, r, r, r)


def _reunite_sibling(fs, shapes):
    n_chunks = 2
    nm = len(fs)

    def body(*refs):
        in_refs, out_refs = refs[:nm], refs[nm:2 * nm]
        send_sems, recv_sems = refs[2 * nm:]
        x, y, c, _ = _position()
        sends, recvs = [], []
        for m in range(nm):
            ch = shapes[m][0] // 2 // n_chunks
            for q in range(n_chunks):
                mine = pl.ds((c * n_chunks + q) * ch, ch)
                theirs = pl.ds(((1 - c) * n_chunks + q) * ch, ch)
                s = n_chunks * m + q
                sends.append(_remote(in_refs[m].at[mine, :], out_refs[m].at[mine, :], send_sems, recv_sems, s, (x, y, 1 - c)))
                recvs.append(_remote(in_refs[m].at[theirs, :], out_refs[m].at[theirs, :], send_sems, recv_sems, s, (x, y, c)))
        for cp in sends:
            cp.start()
        for cp in recvs:
            cp.wait_recv()
        for cp in sends:
            cp.wait_send()

    n_sems = n_chunks * nm
    return pl.pallas_call(
        body, name="reunite_sibling", in_specs=[HBM_SPEC] * nm, out_specs=[HBM_SPEC] * nm,
        out_shape=[jax.ShapeDtypeStruct(s, F32) for s in shapes],
        input_output_aliases={m: m for m in range(nm)},
        scratch_shapes=[pltpu.SemaphoreType.DMA((n_sems,)), pltpu.SemaphoreType.DMA((n_sems,))],
    )(*fs)


def _sum_vecs(v):
    def body(v_ref, o_ref):
        acc = v_ref[0]
        for d in range(1, 8):
            acc = acc + v_ref[d]
        o_ref[...] = acc

    return pl.pallas_call(body, name="sum_vecs", out_shape=jax.ShapeDtypeStruct(v.shape[1:], F32))(v)


def kernel(x, w_in, gate_up, gate_bias, gla_norm_g, w_out, ln1_g, ln1_b, w_up, conv_w, conv_b, w_down, ln2_g, ln2_b, loss_target, m_w_in, m_gate_up, m_gate_bias, m_gla_norm_g, m_w_out, m_ln1_g, m_ln1_b, m_w_up, m_conv_w, m_conv_b, m_w_down, m_ln2_g, m_ln2_b, v_w_in, v_gate_up, v_gate_bias, v_gla_norm_g, v_w_out, v_ln1_g, v_ln1_b, v_w_up, v_conv_w, v_conv_b, v_w_down, v_ln2_g, v_ln2_b):
    xi, yi, ci = lax.axis_index("x"), lax.axis_index("y"), lax.axis_index("c")
    k_me = 2 * xi + yi
    c_arr = ci.astype(jnp.int32).reshape(1)
    kc_arr = jnp.stack([k_me, ci]).astype(jnp.int32)
    small = _pad_rows(jnp.concatenate([_rows(conv_w[0]), _rows(gate_up[0])], axis=0), SMALL_ROWS)
    w_in4, gsmall = _gather_weights([w_in[0].astype(BF16)], small, SHARD_SHAPES[:1])
    late_shards = [w_up[0].astype(BF16), w_out[0].astype(BF16), w_down[0].astype(BF16)]
    w_in_p = jnp.pad(jnp.concatenate([w_in4[k] for k in range(4)], axis=1), ((0, 0), (0, IN_PAD - IN_WIDTH)))
    conv_w_f = jnp.concatenate([gsmall[k, :5].reshape(-1)[:3 * W_UP_S].reshape(3, W_UP_S) for k in range(4)], axis=1)
    gate_up_f = jnp.concatenate([gsmall[k, 5].reshape(GATE_RANK, GLA_KEYS // 4) for k in range(4)], axis=1)
    conv_w3 = jnp.transpose(conv_w_f.reshape(3, 2, D_FF), (1, 0, 2))
    conv_b3 = conv_b.reshape(2, 1, D_FF)
    gate_up_pad = jnp.pad(gate_up_f, ((0, LANES - GATE_RANK), (0, 0)))

    loss_part, dx, late_sums, dw_in_p, g = _local_step(
        x[0], loss_target[0], w_in_p, late_shards, gate_up_pad, gate_bias, gla_norm_g, ln1_g, ln1_b, conv_w3, conv_b3,
        ln2_g, ln2_b, c_arr, kc_arr)
    g["loss"] = loss_part

    g_in4 = jnp.stack([dw_in_p[:, k * W_IN_S:(k + 1) * W_IN_S] for k in range(4)], axis=0)
    from_sib, = _exchange_sibling([g_in4], SHARD_SHAPES[:1], "exchange_sibling_w_in")
    p_in = _add_sibling(g_in4, from_sib, c_arr, ADD_ROWS[0], "add_sibling_w_in")
    others_in, vecs = _exchange_chips([p_in], _pack_vec(g, SMALL_GRADS, GRAD_VEC_ROWS))
    sum_in = _add_chips(p_in, others_in, kc_arr, ADD_ROWS[0], "add_chips_w_in")
    g_w_in, g_w_up, g_w_out, g_w_down = _reunite_sibling([sum_in] + late_sums, SHARD_SHAPES)
    gsmall_sum = _unpack_vec(_sum_vecs(vecs), SMALL_GRADS)
    g_conv_w = lax.dynamic_slice_in_dim(gsmall_sum["conv_w"].reshape(3, 2 * D_FF), k_me * W_UP_S, W_UP_S, axis=1)
    g_gate_up = lax.dynamic_slice_in_dim(gsmall_sum["gate_up"].reshape(GATE_RANK, GLA_KEYS), k_me * (GLA_KEYS // 4),
                                         GLA_KEYS // 4, axis=1)
    gv = gsmall_sum
    loss = gv["loss"][0, 0]
    gvec = _pack_vec(gv, VEC_SIZES, VEC_ROWS)

    grads = dict(w_in=g_w_in[None], gate_up=g_gate_up[None], gate_bias=gv["gate_bias"], gla_norm_g=gv["gla_norm_g"],
                 w_out=g_w_out[None], ln1_g=gv["ln1_g"], ln1_b=gv["ln1_b"], w_up=g_w_up[None], conv_w=g_conv_w[None],
                 conv_b=gv["conv_b"], w_down=g_w_down[None], ln2_g=gv["ln2_g"], ln2_b=gv["ln2_b"])
    weights = dict(w_in=w_in, gate_up=gate_up, gate_bias=gate_bias, gla_norm_g=gla_norm_g, w_out=w_out, ln1_g=ln1_g,
                   ln1_b=ln1_b, w_up=w_up, conv_w=conv_w, conv_b=conv_b, w_down=w_down, ln2_g=ln2_g, ln2_b=ln2_b)
    ms = dict(w_in=m_w_in, gate_up=m_gate_up, gate_bias=m_gate_bias, gla_norm_g=m_gla_norm_g, w_out=m_w_out, ln1_g=m_ln1_g,
              ln1_b=m_ln1_b, w_up=m_w_up, conv_w=m_conv_w, conv_b=m_conv_b, w_down=m_w_down, ln2_g=m_ln2_g, ln2_b=m_ln2_b)
    vs = dict(w_in=v_w_in, gate_up=v_gate_up, gate_bias=v_gate_bias, gla_norm_g=v_gla_norm_g, w_out=v_w_out, ln1_g=v_ln1_g,
              ln1_b=v_ln1_b, w_up=v_w_up, conv_w=v_conv_w, conv_b=v_conv_b, w_down=v_w_down, ln2_g=v_ln2_g, ln2_b=v_ln2_b)
    names = ["w_in", "gate_up", "gate_bias", "gla_norm_g", "w_out", "ln1_g", "ln1_b", "w_up", "conv_w", "conv_b", "w_down",
             "ln2_g", "ln2_b"]
    delta, new_m, new_v = {}, {}, {}
    for n in ("w_in", "gate_up", "w_out", "w_up", "conv_w", "w_down"):
        d, nm, nv = _adamw(weights[n][0], grads[n][0], ms[n][0], vs[n][0], "adamw_" + n)
        delta[n], new_m[n], new_v[n] = d[None], nm[None], nv[None]
    d, nm, nv = _adamw(_pack_vec(weights, VEC_SIZES, VEC_ROWS), gvec, _pack_vec(ms, VEC_SIZES, VEC_ROWS),
                       _pack_vec(vs, VEC_SIZES, VEC_ROWS), "adamw_vectors")
    for dst, src in ((delta, d), (new_m, nm), (new_v, nv)):
        dst.update(_unpack_vec(src, VEC_SIZES))
    return (loss, dx[None], *[grads[n] for n in names], *[delta[n] for n in names], *[new_m[n] for n in names],
            *[new_v[n] for n in names])
```

```python
import functools
import math

import jax
import jax.numpy as jnp
from jax import lax
from jax.experimental import pallas as pl
from jax.experimental.pallas import tpu as pltpu

F32 = jnp.float32
BF16 = jnp.bfloat16

D_MODEL = 1024
SB_WIDTH = 512
GLA_KEYS = 256
GLA_WIDTH = 512
GATE_RANK = 16
IN_WIDTH = 3088
IN_PAD = 3200
D_FF = 2816
CHUNK = 64
LN_EPS = 1e-5
RMS_EPS = 1e-6
ALPHA = 2.0 ** 0.25
GLA_TAU = 16.0
SB_SCALE = 0.125
GLA_SCALE = 0.125
LANES = 128
SB_BLK = 256
SB_CUT = -100.0
GLA_ROWS = 256
VMEM_LIMIT = 56 * 1024 * 1024

ADAM_LR, ADAM_B1, ADAM_B2, ADAM_EPS, ADAM_WD, ADAM_STEP = 0.001, 0.9, 0.999, 1e-08, 0.01, 10

MESH = pl.DeviceIdType.MESH


def _cparams(*sem):
    return pltpu.CompilerParams(dimension_semantics=sem, vmem_limit_bytes=VMEM_LIMIT)


def _dot(a, b):
    return jnp.dot(a, b, preferred_element_type=F32)


def _dot_nt(a, b):
    return lax.dot_general(a, b, (((1,), (1,)), ((), ())), preferred_element_type=F32)


def _dot_tn(a, b):
    return lax.dot_general(a, b, (((0,), (0,)), ((), ())), preferred_element_type=F32)


def _split2(x):
    hi = x.astype(BF16)
    lo = (x - hi.astype(F32)).astype(BF16)
    return hi, lo


def _split3(x):
    hi = x.astype(BF16)
    r = x - hi.astype(F32)
    mid = r.astype(BF16)
    lo = (r - mid.astype(F32)).astype(BF16)
    return hi, mid, lo


def _softplus(z):
    return jnp.maximum(z, 0.0) + jnp.log(1.0 + jnp.exp(-jnp.abs(z)))


def _sigmoid(z):
    return 1.0 / (1.0 + jnp.exp(-z))


def _mm(a, b, *, m, n, k, tm, tn, tk, ta=False, tb=False, a_spec=None, b_spec=None, o_spec=None,
        out_shape=None, out_dtype=F32, add=None, add_scale=1.0, name):
    nk = k // tk
    dn = (((0 if ta else 1,), (1 if tb else 0,)), ((), ()))

    def body(*refs):
        a_ref, b_ref = refs[:2]
        add_ref = refs[2] if add is not None else None
        o_ref = refs[3 if add is not None else 2]
        part = lax.dot_general(a_ref[...].astype(BF16), b_ref[...].astype(BF16), dn, preferred_element_type=F32)

        def finish(r):
            if add is not None:
                r = r + add_scale * add_ref[...]
            o_ref[...] = r.astype(out_dtype)

        if nk == 1:
            finish(part)
            return
        acc_ref = refs[-1]
        kk = pl.program_id(2)

        @pl.when(kk == 0)
        def _():
            acc_ref[...] = part

        @pl.when((kk > 0) & (kk < nk - 1))
        def _():
            acc_ref[...] += part

        @pl.when(kk == nk - 1)
        def _():
            finish(acc_ref[...] + part)

    if a_spec is None:
        a_spec = pl.BlockSpec((tk, tm), lambda i, j, kk: (kk, i)) if ta else pl.BlockSpec((tm, tk), lambda i, j, kk: (i, kk))
    if b_spec is None:
        b_spec = pl.BlockSpec((tn, tk), lambda i, j, kk: (j, kk)) if tb else pl.BlockSpec((tk, tn), lambda i, j, kk: (kk, j))
    if o_spec is None:
        o_spec = pl.BlockSpec((tm, tn), lambda i, j, kk: (i, j))
    if out_shape is None:
        out_shape = jax.ShapeDtypeStruct((m, n), out_dtype)
    in_specs = [a_spec, b_spec]
    args = [a, b]
    if add is not None:
        in_specs.append(pl.BlockSpec((tm, tn), lambda i, j, kk: (i, j)))
        args.append(add)
    return pl.pallas_call(
        body, name=name, grid=(m // tm, n // tn, nk), in_specs=in_specs, out_specs=o_spec, out_shape=out_shape,
        scratch_shapes=[pltpu.VMEM((tm, tn), F32)] if nk > 1 else [],
        compiler_params=_cparams("parallel", "parallel", "arbitrary"),
    )(*args)


def _sb_tile(qh, kj, diag, strict, u_strict, r_in):
    z = _dot_nt(qh, kj)
    sp = _softplus(z)
    l1m = -sp
    lsz = z - sp
    if diag:
        l1m = jnp.where(strict, l1m, 0.0)
    hi, lo = _split2(l1m)
    cs = _dot(hi, u_strict) + _dot(lo, u_strict) + r_in
    w = jnp.exp(lsz + cs)
    if diag:
        w = jnp.where(strict, w, 0.0)
    return l1m, lsz, w


def _sb_consts():
    row = lax.broadcasted_iota(jnp.int32, (SB_BLK, SB_BLK), 0)
    col = lax.broadcasted_iota(jnp.int32, (SB_BLK, SB_BLK), 1)
    strict = col < row
    u_strict = (row > col).astype(BF16)
    u_pre = (row < col).astype(BF16)
    lane = lax.broadcasted_iota(jnp.int32, (1, LANES), 1)
    return strict, u_strict, u_pre, lane


def _sb_fwd(proj, t, shards, shapes):
    nq = t // SB_BLK

    nm = len(shards)

    def body(q_ref, k_ref, v_ref, *rest):
        w_refs, o_ref, g_refs = rest[:nm], rest[nm], rest[nm + 1:2 * nm + 1]
        sems = rest[2 * nm + 1:]
        p = pl.program_id(0)
        i = pl.program_id(1)

        @pl.when((p == 0) & (i == 0))
        def _():
            local, sends, _ = _gather_ici(w_refs, g_refs, shapes, *sems)
            for cp in local + sends:
                cp.start()

        @pl.when((p == 3) & (i == nq - 1))
        def _():
            local, sends, recvs = _gather_ici(w_refs, g_refs, shapes, *sems)
            for cp in recvs:
                cp.wait_recv()
            for cp in sends:
                cp.wait_send()
            for cp in local:
                cp.wait()

        strict, u_strict, _, lane = _sb_consts()
        qf = q_ref[...] * SB_SCALE
        hms = [(lane // 64) == hh for hh in range(2)]
        qhs = [jnp.where(hm, qf, 0.0).astype(BF16) for hm in hms]

        def step(j, r0, r1, a, diag):
            rows = pl.ds(pl.multiple_of(j * SB_BLK, SB_BLK), SB_BLK)
            kj = k_ref[rows, :].astype(BF16)
            vf = v_ref[rows, :]
            rs = []
            for hh, r in enumerate((r0, r1)):
                l1m, _, w = _sb_tile(qhs[hh], kj, diag, strict, u_strict, r)
                a = a + _dot(w.astype(BF16), jnp.where(hms[hh], vf, 0.0).astype(BF16))
                rs.append(r + jnp.sum(l1m, axis=1, keepdims=True))
            return rs[0], rs[1], a

        zero = jnp.zeros((SB_BLK, 1), F32)
        r0, r1, acc = step(i, zero, zero, jnp.zeros((SB_BLK, LANES), F32), True)
        _, _, _, acc = lax.while_loop(
            lambda c: (c[0] >= 0) & (jnp.maximum(jnp.max(c[1]), jnp.max(c[2])) > SB_CUT),
            lambda c: (c[0] - 1,) + step(c[0], c[1], c[2], c[3], False),
            (i - 1, r0, r1, acc))
        o_ref[...] = acc.astype(BF16)

    return pl.pallas_call(
        body, name="sb_fwd", grid=(4, nq),
        in_specs=[pl.BlockSpec((SB_BLK, LANES), lambda p, i: (i, p)),
                  pl.BlockSpec((t, LANES), lambda p, i: (0, 4 + p)),
                  pl.BlockSpec((t, LANES), lambda p, i: (0, 8 + p))] + [HBM_SPEC] * nm,
        out_specs=[pl.BlockSpec((SB_BLK, LANES), lambda p, i: (i, p))] + [HBM_SPEC] * nm,
        out_shape=[jax.ShapeDtypeStruct((t, SB_WIDTH), BF16)] + [jax.ShapeDtypeStruct((4,) + s, BF16) for s in shapes],
        scratch_shapes=[pltpu.SemaphoreType.DMA((3 * nm,)), pltpu.SemaphoreType.DMA((3 * nm,)),
                        pltpu.SemaphoreType.DMA((nm,))],
        compiler_params=_cparams("arbitrary", "arbitrary"),
    )(proj, proj, proj, *shards)


def _sb_bwd(proj, dcat, t, ps):
    nq = t // SB_BLK
    nm = len(ps)

    def body(q_ref, k_ref, v_ref, do_ref, *rest):
        p_refs = rest[:nm]
        dq_ref, dk_ref, dv_ref = rest[nm:nm + 3]
        got_refs = rest[nm + 3:2 * nm + 3]
        r_scr, send_sems, recv_sems = rest[2 * nm + 3:]
        p = pl.program_id(0)
        i = pl.program_id(1)

        @pl.when((p == 0) & (i == 0))
        def _():
            sends, _ = _reduce_ici(p_refs, got_refs, send_sems, recv_sems)
            for cp in sends:
                cp.start()

        @pl.when((p == 3) & (i == nq - 1))
        def _():
            sends, recvs = _reduce_ici(p_refs, got_refs, send_sems, recv_sems)
            for cp in recvs:
                cp.wait_recv()
            for cp in sends:
                cp.wait_send()

        @pl.when(i == 0)
        def _():
            dk_ref[...] = jnp.zeros_like(dk_ref)
            dv_ref[...] = jnp.zeros_like(dv_ref)

        strict, u_strict, u_pre, lane = _sb_consts()
        qf = q_ref[...] * SB_SCALE
        dof = do_ref[...]
        hms = [(lane // 64) == hh for hh in range(2)]
        qhs = [jnp.where(hm, qf, 0.0).astype(BF16) for hm in hms]
        dohs = [jnp.where(hm, dof, 0.0).astype(BF16) for hm in hms]

        def scan(j, r0, r1, diag):
            rows = pl.ds(pl.multiple_of(j * SB_BLK, SB_BLK), SB_BLK)
            kj = k_ref[rows, :].astype(BF16)
            out = []
            for hh, r in enumerate((r0, r1)):
                l1m = -_softplus(_dot_nt(qhs[hh], kj))
                if diag:
                    l1m = jnp.where(strict, l1m, 0.0)
                r_scr[hh, j] = r
                out.append(r + jnp.sum(l1m, axis=1, keepdims=True))
            return out[0], out[1]

        zero = jnp.zeros((SB_BLK, 1), F32)
        r0, r1 = scan(i, zero, zero, True)
        jstop, _, _ = lax.while_loop(
            lambda c: (c[0] >= 0) & (jnp.maximum(jnp.max(c[1]), jnp.max(c[2])) > SB_CUT),
            lambda c: (c[0] - 1,) + scan(c[0], c[1], c[2], False),
            (i - 1, r0, r1))

        def step(j, carry, diag):
            pre_g0, pre_g1, dqa = carry
            rows = pl.ds(pl.multiple_of(j * SB_BLK, SB_BLK), SB_BLK)
            kf = k_ref[rows, :]
            kj = kf.astype(BF16)
            vj = v_ref[rows, :].astype(BF16)
            dv = jnp.zeros((SB_BLK, LANES), F32)
            dk = jnp.zeros((SB_BLK, LANES), F32)
            pre = []
            for hh, pre_g in enumerate((pre_g0, pre_g1)):
                _, lsz, w = _sb_tile(qhs[hh], kj, diag, strict, u_strict, r_scr[hh, j])
                g = w * _dot_nt(dohs[hh], vj)
                ghi, glo = _split2(g)
                gpre = _dot(ghi, u_pre) + _dot(glo, u_pre) + pre_g
                sig = jnp.exp(lsz)
                dz = g * (1.0 - sig) - gpre * sig
                if diag:
                    dz = jnp.where(strict, dz, 0.0)
                dzb = dz.astype(BF16)
                dv = dv + _dot_tn(w.astype(BF16), dohs[hh])
                dk = dk + _dot_tn(dzb, qhs[hh])
                dqa = dqa + _dot(dzb, jnp.where(hms[hh], kf, 0.0).astype(BF16))
                pre.append(pre_g + jnp.sum(g, axis=1, keepdims=True))
            dv_ref[rows, :] += dv
            dk_ref[rows, :] += dk
            return pre[0], pre[1], dqa

        carry = lax.fori_loop(jstop + 1, i, lambda j, c: step(j, c, False), (zero, zero, jnp.zeros((SB_BLK, LANES), F32)))
        _, _, dq = step(i, carry, True)
        dq_ref[...] = dq * SB_SCALE

    return pl.pallas_call(
        body, name="sb_bwd", grid=(4, nq),
        in_specs=[pl.BlockSpec((SB_BLK, LANES), lambda p, i: (i, p)),
                  pl.BlockSpec((t, LANES), lambda p, i: (0, 4 + p)),
                  pl.BlockSpec((t, LANES), lambda p, i: (0, 8 + p)),
                  pl.BlockSpec((SB_BLK, LANES), lambda p, i: (i, p))] + [HBM_SPEC] * nm,
        out_specs=[pl.BlockSpec((SB_BLK, LANES), lambda p, i: (i, p)),
                   pl.BlockSpec((t, LANES), lambda p, i: (0, p)),
                   pl.BlockSpec((t, LANES), lambda p, i: (0, p))] + [HBM_SPEC] * nm,
        out_shape=[jax.ShapeDtypeStruct((t, SB_WIDTH), F32)] * 3
        + [jax.ShapeDtypeStruct((3,) + a.shape[1:], a.dtype) for a in ps],
        scratch_shapes=[pltpu.VMEM((2, nq, SB_BLK, 1), F32), pltpu.SemaphoreType.DMA((3 * nm,)),
                        pltpu.SemaphoreType.DMA((3 * nm,))],
        compiler_params=_cparams("arbitrary", "arbitrary"),
    )(proj, proj, proj, dcat, *ps)


def _gla_consts():
    r = lax.broadcasted_iota(jnp.int32, (CHUNK, CHUNK), 0)
    c = lax.broadcasted_iota(jnp.int32, (CHUNK, CHUNK), 1)
    causal = c <= r
    lc = causal.astype(BF16)
    lct = (c >= r).astype(BF16)
    rowid = lax.broadcasted_iota(jnp.int32, (CHUNK, 1), 0)
    lane = lax.broadcasted_iota(jnp.int32, (1, LANES), 1)
    sr = lax.broadcasted_iota(jnp.int32, (LANES, 2 * LANES), 0)
    sc = lax.broadcasted_iota(jnp.int32, (LANES, 2 * LANES), 1)
    blockdiag = (sr // 64) == (sc // LANES)
    return causal, lc, lct, rowid, lane, blockdiag


def _dot3(u, x):
    hi, mid, lo = _split3(x)
    return _dot(u, hi) + _dot(u, mid) + _dot(u, lo)


def _row_to_col(row):
    return jnp.transpose(jnp.broadcast_to(row, (LANES, LANES)))


def _gla_gates(ga_ref, gup_ref, gbias_ref):
    pre = _dot(ga_ref[...].astype(BF16), gup_ref[...].astype(BF16)) + gbias_ref[...]
    log_a = (jnp.minimum(pre, 0.0) - jnp.log(1.0 + jnp.exp(-jnp.abs(pre)))) / GLA_TAU
    return pre, log_a


def _gla_chunk_terms(g2, q2, k2, lc, rowid):
    b = _dot3(lc, g2)
    b_ref = jnp.sum(jnp.where(rowid == CHUNK // 2 - 1, b, 0.0), axis=0, keepdims=True)
    b_last = jnp.sum(jnp.where(rowid == CHUNK - 1, b, 0.0), axis=0, keepdims=True)
    qs = q2 * GLA_SCALE
    e_q = jnp.exp(b - b_ref)
    e_k = jnp.exp(b_ref - b)
    e_d = jnp.exp(b_last - b)
    e_b = jnp.exp(b)
    decay = jnp.exp(b_last)
    return dict(qs=qs, e_q=e_q, e_k=e_k, e_d=e_d, e_b=e_b, decay=decay,
                qi=qs * e_q, ki=k2 * e_k, kd=k2 * e_d, qb=qs * e_b)


def _gla_fwd(proj, gate_up_pad, gate_bias, gnorm, t, gathered, shapes):
    nsteps = t // GLA_ROWS
    cps = GLA_ROWS // CHUNK
    nm = len(gathered)

    def body(q_ref, k_ref, v_ref, gg_ref, ga_ref, gup_ref, gbias_ref, gn_ref, *rest):
        gin_refs = rest[:nm]
        o_ref, oraw_ref, st_ref = rest[nm:nm + 3]
        gout_refs = rest[nm + 3:2 * nm + 3]
        s_scr, send_sems, recv_sems = rest[2 * nm + 3:]
        i = pl.program_id(0)

        @pl.when(i == 0)
        def _():
            sends, _ = _gather_d2d(gin_refs, gout_refs, shapes, send_sems, recv_sems, 0)
            for cp in sends:
                cp.start()

        @pl.when(i == nsteps - 1)
        def _():
            sends, recvs = _gather_d2d(gin_refs, gout_refs, shapes, send_sems, recv_sems, 0)
            for cp in recvs:
                cp.wait_recv()
            for cp in sends:
                cp.wait_send()

        @pl.when(i == 0)
        def _():
            s_scr[...] = jnp.zeros_like(s_scr)

        causal, lc, _, rowid, lane, blockdiag = _gla_consts()
        _, log_a = _gla_gates(ga_ref, gup_ref, gbias_ref)
        gn = gn_ref[...]
        state = [s_scr[0], s_scr[1]]
        for cc in range(cps):
            rows = slice(cc * CHUNK, (cc + 1) * CHUNK)
            for p in range(2):
                kl = slice(p * LANES, (p + 1) * LANES)
                vl = slice(p * 2 * LANES, (p + 1) * 2 * LANES)
                tm = _gla_chunk_terms(log_a[rows, kl], q_ref[rows, kl], k_ref[rows, kl], lc, rowid)
                v2 = v_ref[rows, vl]
                v2b = v2.astype(BF16)
                s_prev = state[p]
                st_ref[cc, p] = s_prev
                kib = tm["ki"].astype(BF16)
                o_inter = _dot(tm["qb"].astype(BF16), s_prev.astype(BF16))
                outs = []
                for hh in range(2):
                    hm = (lane // 64) == hh
                    a = _dot_nt(jnp.where(hm, tm["qi"], 0.0).astype(BF16), kib)
                    a = jnp.where(causal, a, 0.0)
                    outs.append(_dot(a.astype(BF16), v2b[:, hh * LANES:(hh + 1) * LANES]))
                o2 = jnp.concatenate(outs, axis=1) + o_inter
                upd = jnp.where(blockdiag, _dot_tn(tm["kd"].astype(BF16), v2b), 0.0)
                dcol = _row_to_col(tm["decay"])
                state[p] = s_prev * jnp.concatenate([dcol, dcol], axis=1) + upd
                oraw_ref[rows, vl] = o2
                for hh in range(2):
                    hl = slice(hh * LANES, (hh + 1) * LANES)
                    oh = o2[:, hl]
                    gl = slice(p * 2 * LANES + hh * LANES, p * 2 * LANES + (hh + 1) * LANES)
                    rinv = lax.rsqrt(jnp.mean(oh * oh, axis=1, keepdims=True) + RMS_EPS)
                    gg = gg_ref[rows, gl]
                    o_ref[rows, gl] = (oh * rinv * gn * (gg * _sigmoid(gg))).astype(BF16)
        s_scr[0] = state[0]
        s_scr[1] = state[1]

    cb = lambda w, idx: pl.BlockSpec((GLA_ROWS, w), lambda i: (i, idx))
    full = lambda shp: pl.BlockSpec(shp, lambda i: tuple(0 for _ in shp))
    return pl.pallas_call(
        body, name="gla_fwd", grid=(nsteps,),
        in_specs=[cb(256, 6), cb(256, 7), cb(512, 4), cb(512, 5), cb(128, 24),
                  full((LANES, GLA_KEYS)), full((1, GLA_KEYS)), full((1, LANES))] + [HBM_SPEC] * nm,
        out_specs=[pl.BlockSpec((GLA_ROWS, GLA_WIDTH), lambda i: (i, 0)),
                   pl.BlockSpec((GLA_ROWS, GLA_WIDTH), lambda i: (i, 0)),
                   pl.BlockSpec((cps, 2, LANES, 2 * LANES), lambda i: (i, 0, 0, 0))] + [HBM_SPEC] * nm,
        out_shape=[jax.ShapeDtypeStruct((t, GLA_WIDTH), BF16), jax.ShapeDtypeStruct((t, GLA_WIDTH), F32),
                   jax.ShapeDtypeStruct((t // CHUNK, 2, LANES, 2 * LANES), F32)]
        + [jax.ShapeDtypeStruct(a.shape, a.dtype) for a in gathered],
        input_output_aliases={8 + m: 3 + m for m in range(nm)},
        scratch_shapes=[pltpu.VMEM((2, LANES, 2 * LANES), F32), pltpu.SemaphoreType.DMA((3 * nm,)),
                        pltpu.SemaphoreType.DMA((3 * nm,))],
        compiler_params=_cparams("arbitrary"),
    )(proj, proj, proj, proj, proj, gate_up_pad, gate_bias, gnorm, *gathered)


def _gla_bwd(proj, dcat, oraw, states, gate_up_pad, gate_bias, gnorm, t):
    nsteps = t // GLA_ROWS
    cps = GLA_ROWS // CHUNK
    wout = 2 * GLA_KEYS + 2 * GLA_WIDTH + LANES

    def body(q_ref, k_ref, v_ref, gg_ref, ga_ref, do_ref, oraw_ref, st_ref, gup_ref, gbias_ref, gn_ref,
             d_ref, dgup_ref, dgbias_ref, dgn_ref, ds_scr):
        i = pl.program_id(0)

        @pl.when(i == 0)
        def _():
            ds_scr[...] = jnp.zeros_like(ds_scr)
            dgup_ref[...] = jnp.zeros_like(dgup_ref)
            dgbias_ref[...] = jnp.zeros_like(dgbias_ref)
            dgn_ref[...] = jnp.zeros_like(dgn_ref)

        causal, lc, lct, rowid, lane, blockdiag = _gla_consts()
        pre, log_a = _gla_gates(ga_ref, gup_ref, gbias_ref)
        gn = gn_ref[...]
        dgn = jnp.zeros((1, LANES), F32)
        dstate = [ds_scr[0], ds_scr[1]]
        dgs = [[None, None] for _ in range(cps)]
        for cc in reversed(range(cps)):
            rows = slice(cc * CHUNK, (cc + 1) * CHUNK)
            for p in range(2):
                kl = slice(p * LANES, (p + 1) * LANES)
                vl = slice(p * 2 * LANES, (p + 1) * 2 * LANES)
                tm = _gla_chunk_terms(log_a[rows, kl], q_ref[rows, kl], k_ref[rows, kl], lc, rowid)
                v2b = v_ref[rows, vl].astype(BF16)
                s_prev = st_ref[cc, p]
                ds2 = dstate[p]
                dos = []
                for hh in range(2):
                    gl = slice(p * 2 * LANES + hh * LANES, p * 2 * LANES + (hh + 1) * LANES)
                    oh = oraw_ref[rows, gl]
                    rinv = lax.rsqrt(jnp.mean(oh * oh, axis=1, keepdims=True) + RMS_EPS)
                    on = oh * rinv
                    gg = gg_ref[rows, gl]
                    sg = _sigmoid(gg)
                    sil = gg * sg
                    dgo = do_ref[rows, gl]
                    d_ref[rows, 2 * GLA_KEYS + GLA_WIDTH + gl.start:2 * GLA_KEYS + GLA_WIDTH + gl.stop] = (
                        dgo * on * gn * (sg * (1.0 + gg * (1.0 - sg))))
                    dgn = dgn + jnp.sum(dgo * sil * on, axis=0, keepdims=True)
                    don = dgo * sil * gn
                    dos.append(rinv * (don - on * jnp.mean(don * on, axis=1, keepdims=True)))
                do2b = jnp.concatenate(dos, axis=1).astype(BF16)
                qib = tm["qi"].astype(BF16)
                kib = tm["ki"].astype(BF16)
                kdb = tm["kd"].astype(BF16)
                qbb = tm["qb"].astype(BF16)
                ds2b = ds2.astype(BF16)
                dqi = jnp.zeros((CHUNK, LANES), F32)
                dki = jnp.zeros((CHUNK, LANES), F32)
                dvs = []
                for hh in range(2):
                    hm = (lane // 64) == hh
                    hl = slice(hh * LANES, (hh + 1) * LANES)
                    a = jnp.where(causal, _dot_nt(jnp.where(hm, tm["qi"], 0.0).astype(BF16), kib), 0.0).astype(BF16)
                    da = jnp.where(causal, _dot_nt(do2b[:, hl], v2b[:, hl]), 0.0).astype(BF16)
                    dvs.append(_dot_tn(a, do2b[:, hl]))
                    dqi = dqi + jnp.where(hm, _dot(da, kib), 0.0)
                    dki = dki + jnp.where(hm, _dot_tn(da, qib), 0.0)
                dv2 = jnp.concatenate(dvs, axis=1) + _dot(kdb, ds2b)
                dqb = _dot_nt(do2b, s_prev.astype(BF16))
                dkd = _dot_nt(v2b, ds2b)
                dcol = _row_to_col(tm["decay"])
                dsp = jnp.where(blockdiag, _dot_tn(qbb, do2b), 0.0) + ds2 * jnp.concatenate([dcol, dcol], axis=1)
                ddecay_col = jnp.sum(ds2 * s_prev, axis=1, keepdims=True)
                ddecay_row = jnp.transpose(jnp.broadcast_to(ddecay_col, (LANES, LANES)))[0:1, :]
                dstate[p] = dsp
                dqs = dqi * tm["e_q"] + dqb * tm["e_b"]
                dk = dki * tm["e_k"] + dkd * tm["e_d"]
                t_qi = dqi * tm["qi"]
                t_ki = dki * tm["ki"]
                t_kd = dkd * tm["kd"]
                db = t_qi - t_ki + dqb * tm["qb"] - t_kd
                db_ref = jnp.sum(t_ki - t_qi, axis=0, keepdims=True)
                db_last = jnp.sum(t_kd, axis=0, keepdims=True) + ddecay_row * tm["decay"]
                db = db + jnp.where(rowid == CHUNK // 2 - 1, db_ref, 0.0) + jnp.where(rowid == CHUNK - 1, db_last, 0.0)
                dg = _dot3(lct, db)
                d_ref[rows, p * LANES:(p + 1) * LANES] = dqs * GLA_SCALE
                d_ref[rows, GLA_KEYS + p * LANES:GLA_KEYS + (p + 1) * LANES] = dk
                d_ref[rows, 2 * GLA_KEYS + p * 2 * LANES:2 * GLA_KEYS + (p + 1) * 2 * LANES] = dv2
                dgs[cc][p] = dg
        ds_scr[0] = dstate[0]
        ds_scr[1] = dstate[1]
        dlog_a = jnp.concatenate([jnp.concatenate(row, axis=1) for row in dgs], axis=0)
        dpre = dlog_a * (1.0 / GLA_TAU) * _sigmoid(-pre)
        dpb = dpre.astype(BF16)
        dgn_ref[...] += dgn
        dgbias_ref[...] += jnp.sum(dpre, axis=0, keepdims=True)
        dgup_ref[...] += _dot_tn(ga_ref[...].astype(BF16), dpb)
        d_ref[:, 2 * GLA_KEYS + 2 * GLA_WIDTH:] = _dot_nt(dpb, gup_ref[...].astype(BF16))

    rev = lambda i: nsteps - 1 - i
    cb = lambda w, idx: pl.BlockSpec((GLA_ROWS, w), lambda i: (rev(i), idx))
    full = lambda shp: pl.BlockSpec(shp, lambda i: tuple(0 for _ in shp))
    return pl.pallas_call(
        body, name="gla_bwd", grid=(nsteps,),
        in_specs=[cb(256, 6), cb(256, 7), cb(512, 4), cb(512, 5), cb(128, 24), cb(512, 1), cb(512, 0),
                  pl.BlockSpec((cps, 2, LANES, 2 * LANES), lambda i: (rev(i), 0, 0, 0)),
                  full((LANES, GLA_KEYS)), full((1, GLA_KEYS)), full((1, LANES))],
        out_specs=[pl.BlockSpec((GLA_ROWS, wout), lambda i: (rev(i), 0)),
                   full((LANES, GLA_KEYS)), full((1, GLA_KEYS)), full((1, LANES))],
        out_shape=[jax.ShapeDtypeStruct((t, wout), F32), jax.ShapeDtypeStruct((LANES, GLA_KEYS), F32),
                   jax.ShapeDtypeStruct((1, GLA_KEYS), F32), jax.ShapeDtypeStruct((1, LANES), F32)],
        scratch_shapes=[pltpu.VMEM((2, LANES, 2 * LANES), F32)],
        compiler_params=_cparams("arbitrary"),
    )(proj, proj, proj, proj, proj, dcat, oraw, states, gate_up_pad, gate_bias, gnorm)


def _ln_stats(r):
    mu = jnp.mean(r, axis=1, keepdims=True)
    xc = r - mu
    rstd = lax.rsqrt(jnp.mean(xc * xc, axis=1, keepdims=True) + LN_EPS)
    return xc * rstd, rstd


def _ln_bwd(dy_g, xhat, rstd):
    return rstd * (dy_g - jnp.mean(dy_g, axis=1, keepdims=True) - xhat * jnp.mean(dy_g * xhat, axis=1, keepdims=True))


def _outproj_ln1(sb_o, gla_o, x, w_out, g1, b1, t, tm=256):
    def body(sb_ref, gl_ref, x_ref, w_ref, g_ref, b_ref, xhat_ref, rstd_ref, h_ref):
        mix = _dot(sb_ref[...], w_ref[0:SB_WIDTH, :]) + _dot(gl_ref[...], w_ref[SB_WIDTH:, :])
        xhat, rstd = _ln_stats(ALPHA * x_ref[...] + mix)
        xhat_ref[...] = xhat
        rstd_ref[...] = rstd
        h_ref[...] = (xhat * g_ref[...] + b_ref[...]).astype(BF16)

    row = lambda w: pl.BlockSpec((tm, w), lambda i: (i, 0))
    full = lambda shp: pl.BlockSpec(shp, lambda i: (0, 0))
    return pl.pallas_call(
        body, name="outproj_ln1", grid=(t // tm,),
        in_specs=[row(SB_WIDTH), row(GLA_WIDTH), row(D_MODEL), full((D_MODEL, D_MODEL)), full((1, D_MODEL)), full((1, D_MODEL))],
        out_specs=[row(D_MODEL), row(1), row(D_MODEL)],
        out_shape=[jax.ShapeDtypeStruct((t, D_MODEL), F32), jax.ShapeDtypeStruct((t, 1), F32),
                   jax.ShapeDtypeStruct((t, D_MODEL), BF16)],
        compiler_params=_cparams("parallel"),
    )(sb_o, gla_o, x, w_out, g1, b1)


_INV_SQRT2 = 1.0 / math.sqrt(2.0)
_INV_SQRT2PI = 1.0 / math.sqrt(2.0 * math.pi)


def _conv3(xs, w_ref, b_ref, half):
    return (w_ref[half, 0:1, :] * pltpu.roll(xs, 2, 0) + w_ref[half, 1:2, :] * pltpu.roll(xs, 1, 0)
            + w_ref[half, 2:3, :] * xs + b_ref[half])


HALO = 16


def _conv_gelu_fwd(up3, conv_w3, conv_b3, t, tr=512, ct=256):
    nct = D_FF // ct
    hb = tr // HALO

    def body(cur_ref, prev_ref, w_ref, b_ref, gm_ref):
        i = pl.program_id(1)
        keep = (i > 0).astype(F32)
        us = []
        for half in range(2):
            xs = jnp.concatenate([prev_ref[half].astype(F32) * keep, cur_ref[half].astype(F32)], axis=0)
            us.append(_conv3(xs, w_ref, b_ref, half)[HALO:, :])
        a, c = us
        gelu = 0.5 * a * (1.0 + lax.erf(a * _INV_SQRT2))
        gm_ref[...] = (gelu * c).astype(BF16)

    return pl.pallas_call(
        body, name="conv_gelu_fwd", grid=(nct, t // tr),
        in_specs=[pl.BlockSpec((2, tr, ct), lambda j, i: (0, i, j)),
                  pl.BlockSpec((2, HALO, ct), lambda j, i: (0, jnp.maximum(i * hb - 1, 0), j)),
                  pl.BlockSpec((2, 3, ct), lambda j, i: (0, 0, j)),
                  pl.BlockSpec((2, 1, ct), lambda j, i: (0, 0, j))],
        out_specs=pl.BlockSpec((tr, ct), lambda j, i: (i, j)),
        out_shape=jax.ShapeDtypeStruct((t, D_FF), BF16),
        compiler_params=_cparams("parallel", "parallel"),
    )(up3, up3, conv_w3, conv_b3)


def _conv_gelu_bwd(up3, dgm, conv_w3, conv_b3, t, tr=512, ct=256):
    nct = D_FF // ct
    nrt = t // tr
    hb = tr // HALO
    n = tr + 2 * HALO
    lo, hi = HALO, tr + HALO

    def body(cur_ref, prev_ref, next_ref, dg_ref, dgn_ref, w_ref, b_ref, dup_ref, dcw_ref, dcb_ref):
        i = pl.program_id(1)

        @pl.when(i == 0)
        def _():
            dcw_ref[...] = jnp.zeros_like(dcw_ref)
            dcb_ref[...] = jnp.zeros_like(dcb_ref)

        keep_prev = (i > 0).astype(F32)
        keep_next = (i < nrt - 1).astype(F32)
        xs, xm1, xm2, us = [], [], [], []
        for half in range(2):
            x = jnp.concatenate([prev_ref[half].astype(F32) * keep_prev, cur_ref[half].astype(F32),
                                 next_ref[half].astype(F32)], axis=0)
            xs.append(x)
            xm1.append(pltpu.roll(x, 1, 0))
            xm2.append(pltpu.roll(x, 2, 0))
            us.append(w_ref[half, 0:1, :] * xm2[half] + w_ref[half, 1:2, :] * xm1[half]
                      + w_ref[half, 2:3, :] * x + b_ref[half])
        a, c = us
        dg = jnp.concatenate([jnp.zeros((HALO, ct), F32), dg_ref[...].astype(F32),
                              dgn_ref[...].astype(F32) * keep_next], axis=0)
        cdf = 0.5 * (1.0 + lax.erf(a * _INV_SQRT2))
        pdf = jnp.exp(-0.5 * a * a) * _INV_SQRT2PI
        dus = [dg * c * (cdf + a * pdf), dg * (a * cdf)]
        rid = lax.broadcasted_iota(jnp.int32, (8, 1), 0)
        for half in range(2):
            du = dus[half]
            dup = (w_ref[half, 2:3, :] * du + w_ref[half, 1:2, :] * pltpu.roll(du, n - 1, 0)
                   + w_ref[half, 0:1, :] * pltpu.roll(du, n - 2, 0))
            dup_ref[half] = dup[lo:hi, :].astype(BF16)
            duc = du[lo:hi, :]
            s0 = jnp.sum(duc * xm2[half][lo:hi, :], axis=0, keepdims=True)
            s1 = jnp.sum(duc * xm1[half][lo:hi, :], axis=0, keepdims=True)
            s2 = jnp.sum(duc * xs[half][lo:hi, :], axis=0, keepdims=True)
            dcw_ref[half] += jnp.where(rid == 0, s0, jnp.where(rid == 1, s1, jnp.where(rid == 2, s2, 0.0)))
            dcb_ref[half] += jnp.sum(duc, axis=0, keepdims=True)

    last = t // HALO - 1
    return pl.pallas_call(
        body, name="conv_gelu_bwd", grid=(nct, nrt),
        in_specs=[pl.BlockSpec((2, tr, ct), lambda j, i: (0, i, j)),
                  pl.BlockSpec((2, HALO, ct), lambda j, i: (0, jnp.maximum(i * hb - 1, 0), j)),
                  pl.BlockSpec((2, HALO, ct), lambda j, i: (0, jnp.minimum((i + 1) * hb, last), j)),
                  pl.BlockSpec((tr, ct), lambda j, i: (i, j)),
                  pl.BlockSpec((HALO, ct), lambda j, i: (jnp.minimum((i + 1) * hb, last), j)),
                  pl.BlockSpec((2, 3, ct), lambda j, i: (0, 0, j)),
                  pl.BlockSpec((2, 1, ct), lambda j, i: (0, 0, j))],
        out_specs=[pl.BlockSpec((2, tr, ct), lambda j, i: (0, i, j)),
                   pl.BlockSpec((2, 8, ct), lambda j, i: (0, 0, j)),
                   pl.BlockSpec((2, 1, ct), lambda j, i: (0, 0, j))],
        out_shape=[jax.ShapeDtypeStruct((2, t, D_FF), BF16), jax.ShapeDtypeStruct((2, 8, D_FF), F32),
                   jax.ShapeDtypeStruct((2, 1, D_FF), F32)],
        compiler_params=_cparams("parallel", "arbitrary"),
    )(up3, up3, up3, dgm, dgm, conv_w3, conv_b3)


def _down_ln2_loss(gm, w_down, xhat1, g1, b1, g2, b2, target, t, tm=256):
    def body(gm_ref, w_ref, xh_ref, g1_ref, b1_ref, g2_ref, b2_ref, tg_ref, dr_ref, loss_ref, dg_ref, db_ref):
        i = pl.program_id(0)

        @pl.when(i == 0)
        def _():
            loss_ref[...] = jnp.zeros_like(loss_ref)
            dg_ref[...] = jnp.zeros_like(dg_ref)
            db_ref[...] = jnp.zeros_like(db_ref)

        h = xh_ref[...] * g1_ref[...] + b1_ref[...]
        xhat, rstd = _ln_stats(ALPHA * h + _dot(gm_ref[...], w_ref[...]))
        err = xhat * g2_ref[...] + b2_ref[...] - tg_ref[...]
        loss_ref[...] += 0.5 * jnp.sum(jnp.sum(err * err, axis=1, keepdims=True), axis=0, keepdims=True) / D_MODEL
        dy = err * (1.0 / D_MODEL)
        dg_ref[...] += jnp.sum(dy * xhat, axis=0, keepdims=True)
        db_ref[...] += jnp.sum(dy, axis=0, keepdims=True)
        dr_ref[...] = _ln_bwd(dy * g2_ref[...], xhat, rstd)

    row = lambda w: pl.BlockSpec((tm, w), lambda i: (i, 0))
    full = lambda shp: pl.BlockSpec(shp, lambda i: (0, 0))
    vec = full((1, D_MODEL))
    return pl.pallas_call(
        body, name="down_ln2_loss", grid=(t // tm,),
        in_specs=[row(D_FF), full((D_FF, D_MODEL)), row(D_MODEL), vec, vec, vec, vec, row(D_MODEL)],
        out_specs=[row(D_MODEL), full((1, 1)), vec, vec],
        out_shape=[jax.ShapeDtypeStruct((t, D_MODEL), F32), jax.ShapeDtypeStruct((1, 1), F32),
                   jax.ShapeDtypeStruct((1, D_MODEL), F32), jax.ShapeDtypeStruct((1, D_MODEL), F32)],
        compiler_params=_cparams("arbitrary"),
    )(gm, w_down, xhat1, g1, b1, g2, b2, target)


def _dh_ln1_bwd(dup3, w_up4, dr2, xhat1, rstd1, g1, t, tm=512):
    tk = 2 * D_FF // 4
    nkh = D_FF // tk
    nk = 2 * nkh

    def body(a_ref, w_ref, dr2_ref, xh_ref, rs_ref, g_ref, dr1_ref, dg_ref, db_ref, acc_ref):
        i = pl.program_id(0)
        kk = pl.program_id(1)

        @pl.when((i == 0) & (kk == 0))
        def _():
            dg_ref[...] = jnp.zeros_like(dg_ref)
            db_ref[...] = jnp.zeros_like(db_ref)

        part = _dot_nt(a_ref[...], w_ref[...])

        @pl.when(kk == 0)
        def _():
            acc_ref[...] = part

        @pl.when(kk > 0)
        def _():
            acc_ref[...] += part

        @pl.when(kk == nk - 1)
        def _():
            dh = acc_ref[...] + ALPHA * dr2_ref[...]
            xhat = xh_ref[...]
            dg_ref[...] += jnp.sum(dh * xhat, axis=0, keepdims=True)
            db_ref[...] += jnp.sum(dh, axis=0, keepdims=True)
            dr1_ref[...] = _ln_bwd(dh * g_ref[...], xhat, rs_ref[...])

    row = lambda w: pl.BlockSpec((tm, w), lambda i, kk: (i, 0))
    vec = pl.BlockSpec((1, D_MODEL), lambda i, kk: (0, 0))
    return pl.pallas_call(
        body, name="dh_ln1_bwd", grid=(t // tm, nk),
        in_specs=[pl.BlockSpec((None, tm, tk), lambda i, kk: (kk // nkh, i, kk % nkh)),
                  pl.BlockSpec((None, D_MODEL, tk), lambda i, kk: (kk, 0, 0)),
                  row(D_MODEL), row(D_MODEL), row(1), vec],
        out_specs=[row(D_MODEL), vec, vec],
        out_shape=[jax.ShapeDtypeStruct((t, D_MODEL), F32), jax.ShapeDtypeStruct((1, D_MODEL), F32),
                   jax.ShapeDtypeStruct((1, D_MODEL), F32)],
        scratch_shapes=[pltpu.VMEM((tm, D_MODEL), F32)],
        compiler_params=_cparams("arbitrary", "arbitrary"),
    )(dup3, w_up4, dr2, xhat1, rstd1, g1)


def _adamw(w, g, m, v, name):
    rows, cols = w.shape
    tr = rows
    for cand in (256, 128, 64, 32, 16, 8):
        if rows % cand == 0 and rows > cand:
            tr = cand
            break
    c1 = 1.0 / (1.0 - ADAM_B1 ** ADAM_STEP)
    c2 = 1.0 / (1.0 - ADAM_B2 ** ADAM_STEP)

    def body(w_ref, g_ref, m_ref, v_ref, d_ref, nm_ref, nv_ref):
        gv = g_ref[...]
        nm = ADAM_B1 * m_ref[...] + (1.0 - ADAM_B1) * gv
        nv = ADAM_B2 * v_ref[...] + (1.0 - ADAM_B2) * (gv * gv)
        d_ref[...] = -ADAM_LR * ((nm * c1) / (jnp.sqrt(nv * c2) + ADAM_EPS) + ADAM_WD * w_ref[...])
        nm_ref[...] = nm
        nv_ref[...] = nv

    spec = pl.BlockSpec((tr, cols), lambda i: (i, 0))
    out = jax.ShapeDtypeStruct((rows, cols), F32)
    return pl.pallas_call(
        body, name=name, grid=(rows // tr,), in_specs=[spec] * 4, out_specs=[spec] * 3, out_shape=[out] * 3,
        compiler_params=_cparams("parallel"),
    )(w, g, m, v)


def _local_step(x, target, w_in_p, late_shards, gate_up_pad, gate_bias, gnorm, ln1_g, ln1_b, conv_w3, conv_b3,
                ln2_g, ln2_b, c_arr, kc_arr):
    t = x.shape[0]
    tq = min(t, 1024)
    proj = _mm(x, w_in_p, m=t, n=IN_PAD, k=D_MODEL, tm=tq, tn=640, tk=D_MODEL, name="proj")
    sb_o, *partly = _sb_fwd(proj, t, late_shards, LATE_SHAPES)
    gla_o, oraw, states, w_up4, w_out4, w_down4 = _gla_fwd(proj, gate_up_pad, gate_bias, gnorm, t, partly, LATE_SHAPES)
    w_out = w_out4.reshape(D_MODEL, D_MODEL)
    w_down = w_down4.reshape(D_FF, D_MODEL)
    xhat1, rstd1, h_bf = _outproj_ln1(sb_o, gla_o, x, w_out, ln1_g, ln1_b, t)
    up3 = _mm(h_bf, w_up4, m=t, n=2 * D_FF, k=D_MODEL, tm=tq, tn=W_UP_S, tk=D_MODEL, name="up",
              b_spec=pl.BlockSpec((None, D_MODEL, W_UP_S), lambda i, j, kk: (j, 0, 0)),
              o_spec=pl.BlockSpec((None, tq, W_UP_S), lambda i, j, kk: (j // 2, i, j % 2)),
              out_shape=jax.ShapeDtypeStruct((2, t, D_FF), BF16), out_dtype=BF16)
    gm = _conv_gelu_fwd(up3, conv_w3, conv_b3, t, tr=tq)
    dr2, loss, dln2_g, dln2_b = _down_ln2_loss(gm, w_down, xhat1, ln1_g, ln1_b, ln2_g, ln2_b, target, t)
    dgm = _mm(dr2, w_down, m=t, n=D_FF, k=D_MODEL, tm=tq, tn=W_UP_S, tk=D_MODEL, tb=True, out_dtype=BF16, name="dgm")
    dw_down = _mm(gm, dr2, m=D_FF, n=D_MODEL, k=t, tm=W_UP_S, tn=D_MODEL, tk=512, ta=True, name="dw_down")
    dup3, dcw, dcb = _conv_gelu_bwd(up3, dgm, conv_w3, conv_b3, t)
    dr1, dln1_g, dln1_b = _dh_ln1_bwd(dup3, w_up4, dr2, xhat1, rstd1, ln1_g, t)
    dw_up4 = _mm(h_bf, dup3, m=D_MODEL, n=2 * D_FF, k=t, tm=512, tn=W_UP_S, tk=tq, ta=True, name="dw_up",
                 b_spec=pl.BlockSpec((None, tq, W_UP_S), lambda i, j, kk: (j // 2, kk, j % 2)),
                 o_spec=pl.BlockSpec((None, 512, W_UP_S), lambda i, j, kk: (j, i, 0)),
                 out_shape=jax.ShapeDtypeStruct((4, D_MODEL, W_UP_S), F32))
    dcat = _mm(dr1, w_out, m=t, n=D_MODEL, k=D_MODEL, tm=tq, tn=512, tk=D_MODEL, tb=True, name="dcat")
    dw_out_sb = _mm(sb_o, dr1, m=SB_WIDTH, n=D_MODEL, k=t, tm=512, tn=D_MODEL, tk=tq, ta=True, name="dw_out_sb")
    dw_out_gla = _mm(gla_o, dr1, m=GLA_WIDTH, n=D_MODEL, k=t, tm=512, tn=D_MODEL, tk=tq, ta=True, name="dw_out_gla")
    gs = [dw_up4, jnp.concatenate([dw_out_sb, dw_out_gla], axis=0).reshape(4, W_OUT_S, D_MODEL),
          dw_down.reshape(4, W_DOWN_S, D_MODEL)]
    from_sib = _exchange_sibling(gs, LATE_SHAPES, "exchange_sibling_late")
    ps = [_add_sibling(gs[m], from_sib[m], c_arr, LATE_ADD_ROWS[m], "add_sibling_late_%d" % m) for m in range(3)]
    dsq, dsk, dsv, *others = _sb_bwd(proj, dcat, t, ps)
    late_sums = [_add_chips(ps[m], others[m], kc_arr, LATE_ADD_ROWS[m], "add_chips_late_%d" % m) for m in range(3)]
    dgla, dgup_pad, dgbias, dgnorm = _gla_bwd(proj, dcat, oraw, states, gate_up_pad, gate_bias, gnorm, t)
    dproj = jnp.concatenate([dsq, dsk, dsv, dgla], axis=1).astype(BF16)
    dx = _mm(dproj, w_in_p, m=t, n=D_MODEL, k=IN_PAD, tm=tq, tn=D_MODEL, tk=640, tb=True, add=dr1, add_scale=ALPHA, name="dx")
    dw_in_p = _mm(x, dproj, m=D_MODEL, n=IN_PAD, k=t, tm=512, tn=IN_PAD, tk=512, ta=True, name="dw_in")
    small = dict(
        gate_up=dgup_pad[:GATE_RANK], gate_bias=dgbias, gla_norm_g=dgnorm, ln1_g=dln1_g, ln1_b=dln1_b,
        conv_w=jnp.concatenate([dcw[0, :3], dcw[1, :3]], axis=1), conv_b=jnp.concatenate([dcb[0], dcb[1]], axis=1),
        ln2_g=dln2_g, ln2_b=dln2_b)
    return loss, dx, late_sums, dw_in_p, small


W_IN_S, W_UP_S, W_OUT_S, W_DOWN_S = IN_WIDTH // 4, 2 * D_FF // 4, D_MODEL // 4, D_FF // 4
SHARD_SHAPES = ((D_MODEL, W_IN_S), (D_MODEL, W_UP_S), (W_OUT_S, D_MODEL), (W_DOWN_S, D_MODEL))
ADD_ROWS = (256, 256, 128, 176)
LATE_SHAPES, LATE_ADD_ROWS = SHARD_SHAPES[1:], ADD_ROWS[1:]
SMALL_ROWS = 8
VEC_SIZES = (("gate_bias", GLA_KEYS), ("gla_norm_g", LANES), ("ln1_g", D_MODEL), ("ln1_b", D_MODEL),
             ("conv_b", 2 * D_FF), ("ln2_g", D_MODEL), ("ln2_b", D_MODEL))
SMALL_GRADS = VEC_SIZES + (("conv_w", 3 * 2 * D_FF), ("gate_up", GATE_RANK * GLA_KEYS), ("loss", 1))


def _rows(a):
    flat = a.reshape(-1)
    pad = (-flat.shape[0]) % D_MODEL
    if pad:
        flat = jnp.pad(flat, (0, pad))
    return flat.reshape(-1, D_MODEL)


def _pad_rows(a, rows):
    return jnp.pad(a, ((0, rows - a.shape[0]), (0, 0)))


def _pack_vec(d, sizes, rows):
    flat = jnp.concatenate([d[n].reshape(-1) for n, _ in sizes])
    return _pad_rows(_rows(flat), rows)


def _unpack_vec(v, sizes):
    flat = v.reshape(-1)
    out, o = {}, 0
    for n, size in sizes:
        out[n] = flat[o:o + size].reshape(1, size)
        o += size
    return out


VEC_ROWS = 16
GRAD_VEC_ROWS = 32


HBM_SPEC = pl.BlockSpec(memory_space=pltpu.HBM)


def _position():
    x, y, c = lax.axis_index("x"), lax.axis_index("y"), lax.axis_index("c")
    chips = [(1 - x, y), (x, 1 - y), (1 - x, 1 - y)]
    return x, y, c, chips


def _remote(src, dst, send_sems, recv_sems, k, to):
    return pltpu.make_async_remote_copy(src_ref=src, dst_ref=dst, send_sem=send_sems.at[k], recv_sem=recv_sems.at[k],
                                        device_id=to, device_id_type=MESH)


def _gather_ici(in_refs, out_refs, shapes, send_sems, recv_sems, local_sems):
    x, y, c, chips = _position()
    k_me = 2 * x + y
    local, sends, recvs = [], [], []
    for m, (rows, _) in enumerate(shapes):
        h = rows // 2
        local.append(pltpu.make_async_copy(in_refs[m], out_refs[m].at[k_me], local_sems.at[m]))
        for j, (cx, cy) in enumerate(chips):
            sends.append(_remote(in_refs[m].at[pl.ds(c * h, h), :], out_refs[m].at[k_me, pl.ds(c * h, h), :],
                                 send_sems, recv_sems, 3 * m + j, (cx, cy, c)))
            landed = out_refs[m].at[2 * cx + cy, pl.ds(c * h, h), :]
            recvs.append(_remote(landed, landed, send_sems, recv_sems, 3 * m + j, (x, y, c)))
    return local, sends, recvs


def _gather_d2d(src_refs, dst_refs, shapes, send_sems, recv_sems, base):
    x, y, c, chips = _position()
    sends, recvs = [], []
    for m, (rows, _) in enumerate(shapes):
        h = rows // 2
        for j, (cx, cy) in enumerate(chips):
            k = 2 * cx + cy
            sends.append(_remote(src_refs[m].at[k, pl.ds(c * h, h), :], dst_refs[m].at[k, pl.ds(c * h, h), :],
                                 send_sems, recv_sems, base + 3 * m + j, (x, y, 1 - c)))
            landed = dst_refs[m].at[k, pl.ds((1 - c) * h, h), :]
            recvs.append(_remote(landed, landed, send_sems, recv_sems, base + 3 * m + j, (x, y, c)))
    return sends, recvs


def _gather_weights(shards, small, shapes):
    nm = len(shards)
    n_ici = 3 * nm

    def body(*refs):
        in_refs, small_ref = refs[:nm], refs[nm]
        out_refs, osm_ref = refs[nm + 1:2 * nm + 1], refs[2 * nm + 1]
        send_sems, recv_sems, local_sems = refs[2 * nm + 2:]
        x, y, c, chips = _position()
        k_me = 2 * x + y
        local, sends, recvs = _gather_ici(in_refs, out_refs, shapes, send_sems, recv_sems, local_sems)
        local.append(pltpu.make_async_copy(small_ref, osm_ref.at[k_me], local_sems.at[nm]))
        for j, (cx, cy) in enumerate(chips):
            sends.append(_remote(small_ref, osm_ref.at[k_me], send_sems, recv_sems, n_ici + j, (cx, cy, c)))
        for cp in local + sends:
            cp.start()
        for cp in recvs:
            cp.wait_recv()
        fsends, frecvs = _gather_d2d(out_refs, out_refs, shapes, send_sems, recv_sems, n_ici + 3)
        for cp in fsends:
            cp.start()
        for j, (cx, cy) in enumerate(chips):
            k = 2 * cx + cy
            frecvs.append(_remote(osm_ref.at[k], osm_ref.at[k], send_sems, recv_sems, n_ici + j, (x, y, c)))
        for cp in frecvs:
            cp.wait_recv()
        for cp in sends + fsends:
            cp.wait_send()
        for cp in local:
            cp.wait()

    n_sems = 2 * n_ici + 3
    return pl.pallas_call(
        body, name="gather_weights", in_specs=[HBM_SPEC] * (nm + 1), out_specs=[HBM_SPEC] * (nm + 1),
        out_shape=[jax.ShapeDtypeStruct((4,) + s, BF16) for s in shapes]
        + [jax.ShapeDtypeStruct((4, SMALL_ROWS, D_MODEL), F32)],
        scratch_shapes=[pltpu.SemaphoreType.DMA((n_sems,)), pltpu.SemaphoreType.DMA((n_sems,)),
                        pltpu.SemaphoreType.DMA((nm + 1,))],
    )(*shards, small)


def _exchange_sibling(gs, shapes, name):
    nm = len(gs)

    def body(*refs):
        g_refs, r_refs = refs[:nm], refs[nm:2 * nm]
        send_sems, recv_sems = refs[2 * nm:]
        x, y, c, _ = _position()
        copies = []
        for m, (rows, _) in enumerate(shapes):
            h = rows // 2
            for k in range(4):
                copies.append(_remote(g_refs[m].at[k, pl.ds((1 - c) * h, h), :], r_refs[m].at[k], send_sems, recv_sems,
                                      4 * m + k, (x, y, 1 - c)))
        for cp in copies:
            cp.start()
        for cp in copies:
            cp.wait()

    return pl.pallas_call(
        body, name=name, in_specs=[HBM_SPEC] * nm, out_specs=[HBM_SPEC] * nm,
        out_shape=[jax.ShapeDtypeStruct((4, r // 2, cl), F32) for r, cl in shapes],
        scratch_shapes=[pltpu.SemaphoreType.DMA((4 * nm,)), pltpu.SemaphoreType.DMA((4 * nm,))],
    )(*gs)


def _add_sibling(g, r, c_arr, tr, name):
    _, rows, cols = g.shape
    nb = rows // 2 // tr

    def body(c_ref, g_ref, r_ref, o_ref):
        o_ref[...] = (g_ref[...] + r_ref[...]).astype(BF16)

    spec = pl.BlockSpec((None, tr, cols), lambda k, i, c: (k, i, 0))
    return pl.pallas_call(
        body, name=name,
        grid_spec=pltpu.PrefetchScalarGridSpec(
            num_scalar_prefetch=1, grid=(4, nb),
            in_specs=[pl.BlockSpec((None, tr, cols), lambda k, i, c: (k, c[0] * nb + i, 0)), spec], out_specs=spec),
        out_shape=jax.ShapeDtypeStruct((4, rows // 2, cols), BF16), compiler_params=_cparams("parallel", "parallel"),
    )(c_arr, g, r)


def _reduce_ici(p_refs, r_refs, send_sems, recv_sems):
    x, y, c, chips = _position()
    sends, recvs = [], []
    for m in range(len(p_refs)):
        for j, (cx, cy) in enumerate(chips):
            sends.append(_remote(p_refs[m].at[2 * cx + cy], r_refs[m].at[j], send_sems, recv_sems, 3 * m + j, (cx, cy, c)))
            recvs.append(_remote(r_refs[m].at[j], r_refs[m].at[j], send_sems, recv_sems, 3 * m + j, (x, y, c)))
    return sends, recvs


def _exchange_chips(ps, vec):
    nm = len(ps)
    n_ici = 3 * nm

    def body(*refs):
        p_refs, vec_ref = refs[:nm], refs[nm]
        r_refs, vrecv_ref = refs[nm + 1:2 * nm + 1], refs[2 * nm + 1]
        send_sems, recv_sems, local_sem = refs[2 * nm + 2:]
        x, y, c, _ = _position()
        my_id = 4 * x + 2 * y + c
        vloc = pltpu.make_async_copy(vec_ref, vrecv_ref.at[my_id], local_sem.at[0])
        vloc.start()
        sends, recvs = _reduce_ici(p_refs, r_refs, send_sems, recv_sems)
        for r in range(1, 8):
            peer = (1 - x if r & 4 else x, 1 - y if r & 2 else y, 1 - c if r & 1 else c)
            sends.append(_remote(vec_ref, vrecv_ref.at[my_id], send_sems, recv_sems, n_ici + r - 1, peer))
            recvs.append(_remote(vec_ref, vrecv_ref.at[0], send_sems, recv_sems, n_ici + r - 1, (x, y, c)))
        for cp in sends:
            cp.start()
        for cp in recvs:
            cp.wait_recv()
        for cp in sends:
            cp.wait_send()
        vloc.wait()

    n_sems = n_ici + 7
    return pl.pallas_call(
        body, name="exchange_chips", in_specs=[HBM_SPEC] * (nm + 1), out_specs=[HBM_SPEC] * (nm + 1),
        out_shape=[jax.ShapeDtypeStruct((3,) + p.shape[1:], p.dtype) for p in ps]
        + [jax.ShapeDtypeStruct((8, GRAD_VEC_ROWS, D_MODEL), F32)],
        scratch_shapes=[pltpu.SemaphoreType.DMA((n_sems,)), pltpu.SemaphoreType.DMA((n_sems,)), pltpu.SemaphoreType.DMA((1,))],
    )(*ps, vec)


def _add_chips(p, r, kc_arr, tr, name):
    _, h, cols = p.shape
    nb = h // tr

    def body(kc_ref, p_ref, r0_ref, r1_ref, r2_ref, o_ref):
        o_ref[...] = ((p_ref[...].astype(F32) + r0_ref[...].astype(F32)) + r1_ref[...].astype(F32)) + r2_ref[...].astype(F32)

    rspec = lambda j: pl.BlockSpec((None, tr, cols), lambda i, kc: (j, i, 0))
    return pl.pallas_call(
        body, name=name,
        grid_spec=pltpu.PrefetchScalarGridSpec(
            num_scalar_prefetch=1, grid=(nb,),
            in_specs=[pl.BlockSpec((None, tr, cols), lambda i, kc: (kc[0], i, 0)), rspec(0), rspec(1), rspec(2)],
            out_specs=pl.BlockSpec((tr, cols), lambda i, kc: (kc[1] * nb + i, 0))),
        out_shape=jax.ShapeDtypeStruct((2 * h, cols), F32), compiler_params=_cparams("parallel"),
    )(kc_arr, p, r, r, r)


def _reunite_sibling(fs, shapes):
    n_chunks = 2
    nm = len(fs)

    def body(*refs):
        in_refs, out_refs = refs[:nm], refs[nm:2 * nm]
        send_sems, recv_sems = refs[2 * nm:]
        x, y, c, _ = _position()
        sends, recvs = [], []
        for m in range(nm):
            ch = shapes[m][0] // 2 // n_chunks
            for q in range(n_chunks):
                mine = pl.ds((c * n_chunks + q) * ch, ch)
                theirs = pl.ds(((1 - c) * n_chunks + q) * ch, ch)
                s = n_chunks * m + q
                sends.append(_remote(in_refs[m].at[mine, :], out_refs[m].at[mine, :], send_sems, recv_sems, s, (x, y, 1 - c)))
                recvs.append(_remote(in_refs[m].at[theirs, :], out_refs[m].at[theirs, :], send_sems, recv_sems, s, (x, y, c)))
        for cp in sends:
            cp.start()
        for cp in recvs:
            cp.wait_recv()
        for cp in sends:
            cp.wait_send()

    n_sems = n_chunks * nm
    return pl.pallas_call(
        body, name="reunite_sibling", in_specs=[HBM_SPEC] * nm, out_specs=[HBM_SPEC] * nm,
        out_shape=[jax.ShapeDtypeStruct(s, F32) for s in shapes],
        input_output_aliases={m: m for m in range(nm)},
        scratch_shapes=[pltpu.SemaphoreType.DMA((n_sems,)), pltpu.SemaphoreType.DMA((n_sems,))],
    )(*fs)


def _sum_vecs(v):
    def body(v_ref, o_ref):
        acc = v_ref[0]
        for d in range(1, 8):
            acc = acc + v_ref[d]
        o_ref[...] = acc

    return pl.pallas_call(body, name="sum_vecs", out_shape=jax.ShapeDtypeStruct(v.shape[1:], F32))(v)


def kernel(x, w_in, gate_up, gate_bias, gla_norm_g, w_out, ln1_g, ln1_b, w_up, conv_w, conv_b, w_down, ln2_g, ln2_b, loss_target, m_w_in, m_gate_up, m_gate_bias, m_gla_norm_g, m_w_out, m_ln1_g, m_ln1_b, m_w_up, m_conv_w, m_conv_b, m_w_down, m_ln2_g, m_ln2_b, v_w_in, v_gate_up, v_gate_bias, v_gla_norm_g, v_w_out, v_ln1_g, v_ln1_b, v_w_up, v_conv_w, v_conv_b, v_w_down, v_ln2_g, v_ln2_b):
    xi, yi, ci = lax.axis_index("x"), lax.axis_index("y"), lax.axis_index("c")
    k_me = 2 * xi + yi
    c_arr = ci.astype(jnp.int32).reshape(1)
    kc_arr = jnp.stack([k_me, ci]).astype(jnp.int32)
    small = _pad_rows(jnp.concatenate([_rows(conv_w[0]), _rows(gate_up[0])], axis=0), SMALL_ROWS)
    w_in4, gsmall = _gather_weights([w_in[0].astype(BF16)], small, SHARD_SHAPES[:1])
    late_shards = [w_up[0].astype(BF16), w_out[0].astype(BF16), w_down[0].astype(BF16)]
    w_in_p = jnp.pad(jnp.concatenate([w_in4[k] for k in range(4)], axis=1), ((0, 0), (0, IN_PAD - IN_WIDTH)))
    conv_w_f = jnp.concatenate([gsmall[k, :5].reshape(-1)[:3 * W_UP_S].reshape(3, W_UP_S) for k in range(4)], axis=1)
    gate_up_f = jnp.concatenate([gsmall[k, 5].reshape(GATE_RANK, GLA_KEYS // 4) for k in range(4)], axis=1)
    conv_w3 = jnp.transpose(conv_w_f.reshape(3, 2, D_FF), (1, 0, 2))
    conv_b3 = conv_b.reshape(2, 1, D_FF)
    gate_up_pad = jnp.pad(gate_up_f, ((0, LANES - GATE_RANK), (0, 0)))

    loss_part, dx, late_sums, dw_in_p, g = _local_step(
        x[0], loss_target[0], w_in_p, late_shards, gate_up_pad, gate_bias, gla_norm_g, ln1_g, ln1_b, conv_w3, conv_b3,
        ln2_g, ln2_b, c_arr, kc_arr)
    g["loss"] = loss_part

    g_in4 = jnp.stack([dw_in_p[:, k * W_IN_S:(k + 1) * W_IN_S] for k in range(4)], axis=0)
    from_sib, = _exchange_sibling([g_in4], SHARD_SHAPES[:1], "exchange_sibling_w_in")
    p_in = _add_sibling(g_in4, from_sib, c_arr, ADD_ROWS[0], "add_sibling_w_in")
    others_in, vecs = _exchange_chips([p_in], _pack_vec(g, SMALL_GRADS, GRAD_VEC_ROWS))
    sum_in = _add_chips(p_in, others_in, kc_arr, ADD_ROWS[0], "add_chips_w_in")
    g_w_in, g_w_up, g_w_out, g_w_down = _reunite_sibling([sum_in] + late_sums, SHARD_SHAPES)
    gsmall_sum = _unpack_vec(_sum_vecs(vecs), SMALL_GRADS)
    g_conv_w = lax.dynamic_slice_in_dim(gsmall_sum["conv_w"].reshape(3, 2 * D_FF), k_me * W_UP_S, W_UP_S, axis=1)
    g_gate_up = lax.dynamic_slice_in_dim(gsmall_sum["gate_up"].reshape(GATE_RANK, GLA_KEYS), k_me * (GLA_KEYS // 4),
                                         GLA_KEYS // 4, axis=1)
    gv = gsmall_sum
    loss = gv["loss"][0, 0]
    gvec = _pack_vec(gv, VEC_SIZES, VEC_ROWS)

    grads = dict(w_in=g_w_in[None], gate_up=g_gate_up[None], gate_bias=gv["gate_bias"], gla_norm_g=gv["gla_norm_g"],
                 w_out=g_w_out[None], ln1_g=gv["ln1_g"], ln1_b=gv["ln1_b"], w_up=g_w_up[None], conv_w=g_conv_w[None],
                 conv_b=gv["conv_b"], w_down=g_w_down[None], ln2_g=gv["ln2_g"], ln2_b=gv["ln2_b"])
    weights = dict(w_in=w_in, gate_up=gate_up, gate_bias=gate_bias, gla_norm_g=gla_norm_g, w_out=w_out, ln1_g=ln1_g,
                   ln1_b=ln1_b, w_up=w_up, conv_w=conv_w, conv_b=conv_b, w_down=w_down, ln2_g=ln2_g, ln2_b=ln2_b)
    ms = dict(w_in=m_w_in, gate_up=m_gate_up, gate_bias=m_gate_bias, gla_norm_g=m_gla_norm_g, w_out=m_w_out, ln1_g=m_ln1_g,
              ln1_b=m_ln1_b, w_up=m_w_up, conv_w=m_conv_w, conv_b=m_conv_b, w_down=m_w_down, ln2_g=m_ln2_g, ln2_b=m_ln2_b)
    vs = dict(w_in=v_w_in, gate_up=v_gate_up, gate_bias=v_gate_bias, gla_norm_g=v_gla_norm_g, w_out=v_w_out, ln1_g=v_ln1_g,
              ln1_b=v_ln1_b, w_up=v_w_up, conv_w=v_conv_w, conv_b=v_conv_b, w_down=v_w_down, ln2_g=v_ln2_g, ln2_b=v_ln2_b)
    names = ["w_in", "gate_up", "gate_bias", "gla_norm_g", "w_out", "ln1_g", "ln1_b", "w_up", "conv_w", "conv_b", "w_down",
             "ln2_g", "ln2_b"]
    delta, new_m, new_v = {}, {}, {}
    for n in ("w_in", "gate_up", "w_out", "w_up", "conv_w", "w_down"):
        d, nm, nv = _adamw(weights[n][0], grads[n][0], ms[n][0], vs[n][0], "adamw_" + n)
        delta[n], new_m[n], new_v[n] = d[None], nm[None], nv[None]
    d, nm, nv = _adamw(_pack_vec(weights, VEC_SIZES, VEC_ROWS), gvec, _pack_vec(ms, VEC_SIZES, VEC_ROWS),
                       _pack_vec(vs, VEC_SIZES, VEC_ROWS), "adamw_vectors")
    for dst, src in ((delta, d), (new_m, nm), (new_v, nv)):
        dst.update(_unpack_vec(src, VEC_SIZES))
    return (loss, dx[None], *[grads[n] for n in names], *[delta[n] for n in names], *[new_m[n] for n in names],
            *[new_v[n] for n in names])
```

```python
import functools
import math

import jax
import jax.numpy as jnp
from jax import lax
from jax.experimental import pallas as pl
from jax.experimental.pallas import tpu as pltpu

F32 = jnp.float32
BF16 = jnp.bfloat16

D_MODEL = 1024
SB_WIDTH = 512
GLA_KEYS = 256
GLA_WIDTH = 512
GATE_RANK = 16
IN_WIDTH = 3088
IN_PAD = 3200
D_FF = 2816
CHUNK = 64
LN_EPS = 1e-5
RMS_EPS = 1e-6
ALPHA = 2.0 ** 0.25
GLA_TAU = 16.0
SB_SCALE = 0.125
GLA_SCALE = 0.125
LANES = 128
SB_BLK = 256
SB_CUT = -100.0
GLA_ROWS = 256
VMEM_LIMIT = 56 * 1024 * 1024

ADAM_LR, ADAM_B1, ADAM_B2, ADAM_EPS, ADAM_WD, ADAM_STEP = 0.001, 0.9, 0.999, 1e-08, 0.01, 10

MESH = pl.DeviceIdType.MESH


def _cparams(*sem):
    return pltpu.CompilerParams(dimension_semantics=sem, vmem_limit_bytes=VMEM_LIMIT)


def _dot(a, b):
    return jnp.dot(a, b, preferred_element_type=F32)


def _dot_nt(a, b):
    return lax.dot_general(a, b, (((1,), (1,)), ((), ())), preferred_element_type=F32)


def _dot_tn(a, b):
    return lax.dot_general(a, b, (((0,), (0,)), ((), ())), preferred_element_type=F32)


def _split3(x):
    hi = x.astype(BF16)
    r = x - hi.astype(F32)
    mid = r.astype(BF16)
    lo = (r - mid.astype(F32)).astype(BF16)
    return hi, mid, lo


def _softplus(z):
    return jnp.maximum(z, 0.0) + jnp.log(1.0 + jnp.exp(-jnp.abs(z)))


def _sigmoid(z):
    return 1.0 / (1.0 + jnp.exp(-z))


def _mm(a, b, *, m, n, k, tm, tn, tk, ta=False, tb=False, a_spec=None, b_spec=None, o_spec=None,
        out_shape=None, out_dtype=F32, add=None, add_scale=1.0, name):
    nk = k // tk
    dn = (((0 if ta else 1,), (1 if tb else 0,)), ((), ()))

    def body(*refs):
        a_ref, b_ref = refs[:2]
        add_ref = refs[2] if add is not None else None
        o_ref = refs[3 if add is not None else 2]
        part = lax.dot_general(a_ref[...].astype(BF16), b_ref[...].astype(BF16), dn, preferred_element_type=F32)

        def finish(r):
            if add is not None:
                r = r + add_scale * add_ref[...]
            o_ref[...] = r.astype(out_dtype)

        if nk == 1:
            finish(part)
            return
        acc_ref = refs[-1]
        kk = pl.program_id(2)

        @pl.when(kk == 0)
        def _():
            acc_ref[...] = part

        @pl.when((kk > 0) & (kk < nk - 1))
        def _():
            acc_ref[...] += part

        @pl.when(kk == nk - 1)
        def _():
            finish(acc_ref[...] + part)

    if a_spec is None:
        a_spec = pl.BlockSpec((tk, tm), lambda i, j, kk: (kk, i)) if ta else pl.BlockSpec((tm, tk), lambda i, j, kk: (i, kk))
    if b_spec is None:
        b_spec = pl.BlockSpec((tn, tk), lambda i, j, kk: (j, kk)) if tb else pl.BlockSpec((tk, tn), lambda i, j, kk: (kk, j))
    if o_spec is None:
        o_spec = pl.BlockSpec((tm, tn), lambda i, j, kk: (i, j))
    if out_shape is None:
        out_shape = jax.ShapeDtypeStruct((m, n), out_dtype)
    in_specs = [a_spec, b_spec]
    args = [a, b]
    if add is not None:
        in_specs.append(pl.BlockSpec((tm, tn), lambda i, j, kk: (i, j)))
        args.append(add)
    return pl.pallas_call(
        body, name=name, grid=(m // tm, n // tn, nk), in_specs=in_specs, out_specs=o_spec, out_shape=out_shape,
        scratch_shapes=[pltpu.VMEM((tm, tn), F32)] if nk > 1 else [],
        compiler_params=_cparams("parallel", "parallel", "arbitrary"),
    )(*args)


def _sb_tile(qh, kj, diag, strict, u_strict, r_in):
    z = _dot_nt(qh, kj)
    sp = _softplus(z)
    l1m = -sp
    lsz = z - sp
    if diag:
        l1m = jnp.where(strict, l1m, 0.0)
    cs = _dot(l1m.astype(BF16), u_strict) + r_in
    w = jnp.exp(lsz + cs)
    if diag:
        w = jnp.where(strict, w, 0.0)
    return l1m, lsz, w


def _sb_consts():
    row = lax.broadcasted_iota(jnp.int32, (SB_BLK, SB_BLK), 0)
    col = lax.broadcasted_iota(jnp.int32, (SB_BLK, SB_BLK), 1)
    strict = col < row
    u_strict = (row > col).astype(BF16)
    u_pre = (row < col).astype(BF16)
    lane = lax.broadcasted_iota(jnp.int32, (1, LANES), 1)
    return strict, u_strict, u_pre, lane


def _sb_fwd(proj, t, shards, shapes):
    nq = t // SB_BLK

    nm = len(shards)

    def body(q_ref, k_ref, v_ref, *rest):
        w_refs, o_ref, g_refs = rest[:nm], rest[nm], rest[nm + 1:2 * nm + 1]
        sems = rest[2 * nm + 1:]
        p = pl.program_id(0)
        i = pl.program_id(1)

        @pl.when((p == 0) & (i == 0))
        def _():
            local, sends, _ = _gather_ici(w_refs, g_refs, shapes, *sems)
            for cp in local + sends:
                cp.start()

        @pl.when((p == 3) & (i == nq - 1))
        def _():
            local, sends, recvs = _gather_ici(w_refs, g_refs, shapes, *sems)
            for cp in recvs:
                cp.wait_recv()
            for cp in sends:
                cp.wait_send()
            for cp in local:
                cp.wait()

        strict, u_strict, _, lane = _sb_consts()
        qf = q_ref[...] * SB_SCALE
        hms = [(lane // 64) == hh for hh in range(2)]
        qhs = [jnp.where(hm, qf, 0.0).astype(BF16) for hm in hms]

        def step(j, r0, r1, a, diag):
            rows = pl.ds(pl.multiple_of(j * SB_BLK, SB_BLK), SB_BLK)
            kj = k_ref[rows, :].astype(BF16)
            vf = v_ref[rows, :]
            rs = []
            for hh, r in enumerate((r0, r1)):
                l1m, _, w = _sb_tile(qhs[hh], kj, diag, strict, u_strict, r)
                a = a + _dot(w.astype(BF16), jnp.where(hms[hh], vf, 0.0).astype(BF16))
                rs.append(r + jnp.sum(l1m, axis=1, keepdims=True))
            return rs[0], rs[1], a

        zero = jnp.zeros((SB_BLK, 1), F32)
        r0, r1, acc = step(i, zero, zero, jnp.zeros((SB_BLK, LANES), F32), True)
        _, _, _, acc = lax.while_loop(
            lambda c: (c[0] >= 0) & (jnp.maximum(jnp.max(c[1]), jnp.max(c[2])) > SB_CUT),
            lambda c: (c[0] - 1,) + step(c[0], c[1], c[2], c[3], False),
            (i - 1, r0, r1, acc))
        o_ref[...] = acc.astype(BF16)

    return pl.pallas_call(
        body, name="sb_fwd", grid=(4, nq),
        in_specs=[pl.BlockSpec((SB_BLK, LANES), lambda p, i: (i, p)),
                  pl.BlockSpec((t, LANES), lambda p, i: (0, 4 + p)),
                  pl.BlockSpec((t, LANES), lambda p, i: (0, 8 + p))] + [HBM_SPEC] * nm,
        out_specs=[pl.BlockSpec((SB_BLK, LANES), lambda p, i: (i, p))] + [HBM_SPEC] * nm,
        out_shape=[jax.ShapeDtypeStruct((t, SB_WIDTH), BF16)] + [jax.ShapeDtypeStruct((4,) + s, BF16) for s in shapes],
        scratch_shapes=[pltpu.SemaphoreType.DMA((3 * nm,)), pltpu.SemaphoreType.DMA((3 * nm,)),
                        pltpu.SemaphoreType.DMA((nm,))],
        compiler_params=_cparams("arbitrary", "arbitrary"),
    )(proj, proj, proj, *shards)


def _sb_bwd(proj, dcat, t, ps):
    nq = t // SB_BLK
    nm = len(ps)

    def body(q_ref, k_ref, v_ref, do_ref, *rest):
        p_refs = rest[:nm]
        dq_ref, dk_ref, dv_ref = rest[nm:nm + 3]
        got_refs = rest[nm + 3:2 * nm + 3]
        r_scr, send_sems, recv_sems = rest[2 * nm + 3:]
        p = pl.program_id(0)
        i = pl.program_id(1)

        @pl.when((p == 0) & (i == 0))
        def _():
            sends, _ = _reduce_ici(p_refs, got_refs, send_sems, recv_sems)
            for cp in sends:
                cp.start()

        @pl.when((p == 3) & (i == nq - 1))
        def _():
            sends, recvs = _reduce_ici(p_refs, got_refs, send_sems, recv_sems)
            for cp in recvs:
                cp.wait_recv()
            for cp in sends:
                cp.wait_send()

        @pl.when(i == 0)
        def _():
            dk_ref[...] = jnp.zeros_like(dk_ref)
            dv_ref[...] = jnp.zeros_like(dv_ref)

        strict, u_strict, u_pre, lane = _sb_consts()
        qf = q_ref[...] * SB_SCALE
        dof = do_ref[...]
        hms = [(lane // 64) == hh for hh in range(2)]
        qhs = [jnp.where(hm, qf, 0.0).astype(BF16) for hm in hms]
        dohs = [jnp.where(hm, dof, 0.0).astype(BF16) for hm in hms]

        def scan(j, r0, r1, diag):
            rows = pl.ds(pl.multiple_of(j * SB_BLK, SB_BLK), SB_BLK)
            kj = k_ref[rows, :].astype(BF16)
            out = []
            for hh, r in enumerate((r0, r1)):
                l1m = -_softplus(_dot_nt(qhs[hh], kj))
                if diag:
                    l1m = jnp.where(strict, l1m, 0.0)
                r_scr[hh, j] = r
                out.append(r + jnp.sum(l1m, axis=1, keepdims=True))
            return out[0], out[1]

        zero = jnp.zeros((SB_BLK, 1), F32)
        r0, r1 = scan(i, zero, zero, True)
        jstop, _, _ = lax.while_loop(
            lambda c: (c[0] >= 0) & (jnp.maximum(jnp.max(c[1]), jnp.max(c[2])) > SB_CUT),
            lambda c: (c[0] - 1,) + scan(c[0], c[1], c[2], False),
            (i - 1, r0, r1))

        def step(j, carry, diag):
            pre_g0, pre_g1, dqa = carry
            rows = pl.ds(pl.multiple_of(j * SB_BLK, SB_BLK), SB_BLK)
            kf = k_ref[rows, :]
            kj = kf.astype(BF16)
            vj = v_ref[rows, :].astype(BF16)
            dv = jnp.zeros((SB_BLK, LANES), F32)
            dk = jnp.zeros((SB_BLK, LANES), F32)
            pre = []
            for hh, pre_g in enumerate((pre_g0, pre_g1)):
                _, lsz, w = _sb_tile(qhs[hh], kj, diag, strict, u_strict, r_scr[hh, j])
                g = w * _dot_nt(dohs[hh], vj)
                gpre = _dot(g.astype(BF16), u_pre) + pre_g
                sig = jnp.exp(lsz)
                dz = g * (1.0 - sig) - gpre * sig
                if diag:
                    dz = jnp.where(strict, dz, 0.0)
                dzb = dz.astype(BF16)
                dv = dv + _dot_tn(w.astype(BF16), dohs[hh])
                dk = dk + _dot_tn(dzb, qhs[hh])
                dqa = dqa + _dot(dzb, jnp.where(hms[hh], kf, 0.0).astype(BF16))
                pre.append(pre_g + jnp.sum(g, axis=1, keepdims=True))
            dv_ref[rows, :] += dv
            dk_ref[rows, :] += dk
            return pre[0], pre[1], dqa

        carry = lax.fori_loop(jstop + 1, i, lambda j, c: step(j, c, False), (zero, zero, jnp.zeros((SB_BLK, LANES), F32)))
        _, _, dq = step(i, carry, True)
        dq_ref[...] = dq * SB_SCALE

    return pl.pallas_call(
        body, name="sb_bwd", grid=(4, nq),
        in_specs=[pl.BlockSpec((SB_BLK, LANES), lambda p, i: (i, p)),
                  pl.BlockSpec((t, LANES), lambda p, i: (0, 4 + p)),
                  pl.BlockSpec((t, LANES), lambda p, i: (0, 8 + p)),
                  pl.BlockSpec((SB_BLK, LANES), lambda p, i: (i, p))] + [HBM_SPEC] * nm,
        out_specs=[pl.BlockSpec((SB_BLK, LANES), lambda p, i: (i, p)),
                   pl.BlockSpec((t, LANES), lambda p, i: (0, p)),
                   pl.BlockSpec((t, LANES), lambda p, i: (0, p))] + [HBM_SPEC] * nm,
        out_shape=[jax.ShapeDtypeStruct((t, SB_WIDTH), F32)] * 3
        + [jax.ShapeDtypeStruct((3,) + a.shape[1:], a.dtype) for a in ps],
        scratch_shapes=[pltpu.VMEM((2, nq, SB_BLK, 1), F32), pltpu.SemaphoreType.DMA((3 * nm,)),
                        pltpu.SemaphoreType.DMA((3 * nm,))],
        compiler_params=_cparams("arbitrary", "arbitrary"),
    )(proj, proj, proj, dcat, *ps)


def _gla_consts():
    r = lax.broadcasted_iota(jnp.int32, (CHUNK, CHUNK), 0)
    c = lax.broadcasted_iota(jnp.int32, (CHUNK, CHUNK), 1)
    causal = c <= r
    lc = causal.astype(BF16)
    lct = (c >= r).astype(BF16)
    rowid = lax.broadcasted_iota(jnp.int32, (CHUNK, 1), 0)
    lane = lax.broadcasted_iota(jnp.int32, (1, LANES), 1)
    sr = lax.broadcasted_iota(jnp.int32, (LANES, 2 * LANES), 0)
    sc = lax.broadcasted_iota(jnp.int32, (LANES, 2 * LANES), 1)
    blockdiag = (sr // 64) == (sc // LANES)
    return causal, lc, lct, rowid, lane, blockdiag


def _dot3(u, x):
    hi, mid, lo = _split3(x)
    return _dot(u, hi) + _dot(u, mid) + _dot(u, lo)


def _row_to_col(row):
    return jnp.transpose(jnp.broadcast_to(row, (LANES, LANES)))


def _gla_gates(ga_ref, gup_ref, gbias_ref):
    pre = _dot(ga_ref[...].astype(BF16), gup_ref[...].astype(BF16)) + gbias_ref[...]
    log_a = (jnp.minimum(pre, 0.0) - jnp.log(1.0 + jnp.exp(-jnp.abs(pre)))) / GLA_TAU
    return pre, log_a


def _gla_chunk_terms(g2, q2, k2, lc, rowid):
    b = _dot3(lc, g2)
    b_ref = jnp.sum(jnp.where(rowid == CHUNK // 2 - 1, b, 0.0), axis=0, keepdims=True)
    b_last = jnp.sum(jnp.where(rowid == CHUNK - 1, b, 0.0), axis=0, keepdims=True)
    qs = q2 * GLA_SCALE
    e_q = jnp.exp(b - b_ref)
    e_k = jnp.exp(b_ref - b)
    e_d = jnp.exp(b_last - b)
    e_b = jnp.exp(b)
    decay = jnp.exp(b_last)
    return dict(qs=qs, e_q=e_q, e_k=e_k, e_d=e_d, e_b=e_b, decay=decay,
                qi=qs * e_q, ki=k2 * e_k, kd=k2 * e_d, qb=qs * e_b)


def _gla_fwd(proj, gate_up_pad, gate_bias, gnorm, t, gathered, shapes):
    nsteps = t // GLA_ROWS
    cps = GLA_ROWS // CHUNK
    nm = len(gathered)

    def body(q_ref, k_ref, v_ref, gg_ref, ga_ref, gup_ref, gbias_ref, gn_ref, *rest):
        gin_refs = rest[:nm]
        o_ref, oraw_ref, st_ref = rest[nm:nm + 3]
        gout_refs = rest[nm + 3:2 * nm + 3]
        s_scr, send_sems, recv_sems = rest[2 * nm + 3:]
        i = pl.program_id(0)

        @pl.when(i == 0)
        def _():
            sends, _ = _gather_d2d(gin_refs, gout_refs, shapes, send_sems, recv_sems, 0)
            for cp in sends:
                cp.start()

        @pl.when(i == nsteps - 1)
        def _():
            sends, recvs = _gather_d2d(gin_refs, gout_refs, shapes, send_sems, recv_sems, 0)
            for cp in recvs:
                cp.wait_recv()
            for cp in sends:
                cp.wait_send()

        @pl.when(i == 0)
        def _():
            s_scr[...] = jnp.zeros_like(s_scr)

        causal, lc, _, rowid, lane, blockdiag = _gla_consts()
        _, log_a = _gla_gates(ga_ref, gup_ref, gbias_ref)
        gn = gn_ref[...]
        state = [s_scr[0], s_scr[1]]
        for cc in range(cps):
            rows = slice(cc * CHUNK, (cc + 1) * CHUNK)
            for p in range(2):
                kl = slice(p * LANES, (p + 1) * LANES)
                vl = slice(p * 2 * LANES, (p + 1) * 2 * LANES)
                tm = _gla_chunk_terms(log_a[rows, kl], q_ref[rows, kl], k_ref[rows, kl], lc, rowid)
                v2 = v_ref[rows, vl]
                v2b = v2.astype(BF16)
                s_prev = state[p]
                st_ref[cc, p] = s_prev
                kib = tm["ki"].astype(BF16)
                o_inter = _dot(tm["qb"].astype(BF16), s_prev.astype(BF16))
                outs = []
                for hh in range(2):
                    hm = (lane // 64) == hh
                    a = _dot_nt(jnp.where(hm, tm["qi"], 0.0).astype(BF16), kib)
                    a = jnp.where(causal, a, 0.0)
                    outs.append(_dot(a.astype(BF16), v2b[:, hh * LANES:(hh + 1) * LANES]))
                o2 = jnp.concatenate(outs, axis=1) + o_inter
                upd = jnp.where(blockdiag, _dot_tn(tm["kd"].astype(BF16), v2b), 0.0)
                dcol = _row_to_col(tm["decay"])
                state[p] = s_prev * jnp.concatenate([dcol, dcol], axis=1) + upd
                oraw_ref[rows, vl] = o2
                for hh in range(2):
                    hl = slice(hh * LANES, (hh + 1) * LANES)
                    oh = o2[:, hl]
                    gl = slice(p * 2 * LANES + hh * LANES, p * 2 * LANES + (hh + 1) * LANES)
                    rinv = lax.rsqrt(jnp.mean(oh * oh, axis=1, keepdims=True) + RMS_EPS)
                    gg = gg_ref[rows, gl]
                    o_ref[rows, gl] = (oh * rinv * gn * (gg * _sigmoid(gg))).astype(BF16)
        s_scr[0] = state[0]
        s_scr[1] = state[1]

    cb = lambda w, idx: pl.BlockSpec((GLA_ROWS, w), lambda i: (i, idx))
    full = lambda shp: pl.BlockSpec(shp, lambda i: tuple(0 for _ in shp))
    return pl.pallas_call(
        body, name="gla_fwd", grid=(nsteps,),
        in_specs=[cb(256, 6), cb(256, 7), cb(512, 4), cb(512, 5), cb(128, 24),
                  full((LANES, GLA_KEYS)), full((1, GLA_KEYS)), full((1, LANES))] + [HBM_SPEC] * nm,
        out_specs=[pl.BlockSpec((GLA_ROWS, GLA_WIDTH), lambda i: (i, 0)),
                   pl.BlockSpec((GLA_ROWS, GLA_WIDTH), lambda i: (i, 0)),
                   pl.BlockSpec((cps, 2, LANES, 2 * LANES), lambda i: (i, 0, 0, 0))] + [HBM_SPEC] * nm,
        out_shape=[jax.ShapeDtypeStruct((t, GLA_WIDTH), BF16), jax.ShapeDtypeStruct((t, GLA_WIDTH), F32),
                   jax.ShapeDtypeStruct((t // CHUNK, 2, LANES, 2 * LANES), F32)]
        + [jax.ShapeDtypeStruct(a.shape, a.dtype) for a in gathered],
        input_output_aliases={8 + m: 3 + m for m in range(nm)},
        scratch_shapes=[pltpu.VMEM((2, LANES, 2 * LANES), F32), pltpu.SemaphoreType.DMA((3 * nm,)),
                        pltpu.SemaphoreType.DMA((3 * nm,))],
        compiler_params=_cparams("arbitrary"),
    )(proj, proj, proj, proj, proj, gate_up_pad, gate_bias, gnorm, *gathered)


def _gla_bwd(proj, dcat, oraw, states, gate_up_pad, gate_bias, gnorm, t):
    nsteps = t // GLA_ROWS
    cps = GLA_ROWS // CHUNK
    wout = 2 * GLA_KEYS + 2 * GLA_WIDTH + LANES

    def body(q_ref, k_ref, v_ref, gg_ref, ga_ref, do_ref, oraw_ref, st_ref, gup_ref, gbias_ref, gn_ref,
             d_ref, dgup_ref, dgbias_ref, dgn_ref, ds_scr):
        i = pl.program_id(0)

        @pl.when(i == 0)
        def _():
            ds_scr[...] = jnp.zeros_like(ds_scr)
            dgup_ref[...] = jnp.zeros_like(dgup_ref)
            dgbias_ref[...] = jnp.zeros_like(dgbias_ref)
            dgn_ref[...] = jnp.zeros_like(dgn_ref)

        causal, lc, lct, rowid, lane, blockdiag = _gla_consts()
        pre, log_a = _gla_gates(ga_ref, gup_ref, gbias_ref)
        gn = gn_ref[...]
        dgn = jnp.zeros((1, LANES), F32)
        dstate = [ds_scr[0], ds_scr[1]]
        dgs = [[None, None] for _ in range(cps)]
        for cc in reversed(range(cps)):
            rows = slice(cc * CHUNK, (cc + 1) * CHUNK)
            for p in range(2):
                kl = slice(p * LANES, (p + 1) * LANES)
                vl = slice(p * 2 * LANES, (p + 1) * 2 * LANES)
                tm = _gla_chunk_terms(log_a[rows, kl], q_ref[rows, kl], k_ref[rows, kl], lc, rowid)
                v2b = v_ref[rows, vl].astype(BF16)
                s_prev = st_ref[cc, p]
                ds2 = dstate[p]
                dos = []
                for hh in range(2):
                    gl = slice(p * 2 * LANES + hh * LANES, p * 2 * LANES + (hh + 1) * LANES)
                    oh = oraw_ref[rows, gl]
                    rinv = lax.rsqrt(jnp.mean(oh * oh, axis=1, keepdims=True) + RMS_EPS)
                    on = oh * rinv
                    gg = gg_ref[rows, gl]
                    sg = _sigmoid(gg)
                    sil = gg * sg
                    dgo = do_ref[rows, gl]
                    d_ref[rows, 2 * GLA_KEYS + GLA_WIDTH + gl.start:2 * GLA_KEYS + GLA_WIDTH + gl.stop] = (
                        dgo * on * gn * (sg * (1.0 + gg * (1.0 - sg))))
                    dgn = dgn + jnp.sum(dgo * sil * on, axis=0, keepdims=True)
                    don = dgo * sil * gn
                    dos.append(rinv * (don - on * jnp.mean(don * on, axis=1, keepdims=True)))
                do2b = jnp.concatenate(dos, axis=1).astype(BF16)
                qib = tm["qi"].astype(BF16)
                kib = tm["ki"].astype(BF16)
                kdb = tm["kd"].astype(BF16)
                qbb = tm["qb"].astype(BF16)
                ds2b = ds2.astype(BF16)
                dqi = jnp.zeros((CHUNK, LANES), F32)
                dki = jnp.zeros((CHUNK, LANES), F32)
                dvs = []
                for hh in range(2):
                    hm = (lane // 64) == hh
                    hl = slice(hh * LANES, (hh + 1) * LANES)
                    a = jnp.where(causal, _dot_nt(jnp.where(hm, tm["qi"], 0.0).astype(BF16), kib), 0.0).astype(BF16)
                    da = jnp.where(causal, _dot_nt(do2b[:, hl], v2b[:, hl]), 0.0).astype(BF16)
                    dvs.append(_dot_tn(a, do2b[:, hl]))
                    dqi = dqi + jnp.where(hm, _dot(da, kib), 0.0)
                    dki = dki + jnp.where(hm, _dot_tn(da, qib), 0.0)
                dv2 = jnp.concatenate(dvs, axis=1) + _dot(kdb, ds2b)
                dqb = _dot_nt(do2b, s_prev.astype(BF16))
                dkd = _dot_nt(v2b, ds2b)
                dcol = _row_to_col(tm["decay"])
                dsp = jnp.where(blockdiag, _dot_tn(qbb, do2b), 0.0) + ds2 * jnp.concatenate([dcol, dcol], axis=1)
                ddecay_col = jnp.sum(ds2 * s_prev, axis=1, keepdims=True)
                ddecay_row = jnp.transpose(jnp.broadcast_to(ddecay_col, (LANES, LANES)))[0:1, :]
                dstate[p] = dsp
                dqs = dqi * tm["e_q"] + dqb * tm["e_b"]
                dk = dki * tm["e_k"] + dkd * tm["e_d"]
                t_qi = dqi * tm["qi"]
                t_ki = dki * tm["ki"]
                t_kd = dkd * tm["kd"]
                db = t_qi - t_ki + dqb * tm["qb"] - t_kd
                db_ref = jnp.sum(t_ki - t_qi, axis=0, keepdims=True)
                db_last = jnp.sum(t_kd, axis=0, keepdims=True) + ddecay_row * tm["decay"]
                db = db + jnp.where(rowid == CHUNK // 2 - 1, db_ref, 0.0) + jnp.where(rowid == CHUNK - 1, db_last, 0.0)
                dg = _dot3(lct, db)
                d_ref[rows, p * LANES:(p + 1) * LANES] = dqs * GLA_SCALE
                d_ref[rows, GLA_KEYS + p * LANES:GLA_KEYS + (p + 1) * LANES] = dk
                d_ref[rows, 2 * GLA_KEYS + p * 2 * LANES:2 * GLA_KEYS + (p + 1) * 2 * LANES] = dv2
                dgs[cc][p] = dg
        ds_scr[0] = dstate[0]
        ds_scr[1] = dstate[1]
        dlog_a = jnp.concatenate([jnp.concatenate(row, axis=1) for row in dgs], axis=0)
        dpre = dlog_a * (1.0 / GLA_TAU) * _sigmoid(-pre)
        dpb = dpre.astype(BF16)
        dgn_ref[...] += dgn
        dgbias_ref[...] += jnp.sum(dpre, axis=0, keepdims=True)
        dgup_ref[...] += _dot_tn(ga_ref[...].astype(BF16), dpb)
        d_ref[:, 2 * GLA_KEYS + 2 * GLA_WIDTH:] = _dot_nt(dpb, gup_ref[...].astype(BF16))

    rev = lambda i: nsteps - 1 - i
    cb = lambda w, idx: pl.BlockSpec((GLA_ROWS, w), lambda i: (rev(i), idx))
    full = lambda shp: pl.BlockSpec(shp, lambda i: tuple(0 for _ in shp))
    return pl.pallas_call(
        body, name="gla_bwd", grid=(nsteps,),
        in_specs=[cb(256, 6), cb(256, 7), cb(512, 4), cb(512, 5), cb(128, 24), cb(512, 1), cb(512, 0),
                  pl.BlockSpec((cps, 2, LANES, 2 * LANES), lambda i: (rev(i), 0, 0, 0)),
                  full((LANES, GLA_KEYS)), full((1, GLA_KEYS)), full((1, LANES))],
        out_specs=[pl.BlockSpec((GLA_ROWS, wout), lambda i: (rev(i), 0)),
                   full((LANES, GLA_KEYS)), full((1, GLA_KEYS)), full((1, LANES))],
        out_shape=[jax.ShapeDtypeStruct((t, wout), F32), jax.ShapeDtypeStruct((LANES, GLA_KEYS), F32),
                   jax.ShapeDtypeStruct((1, GLA_KEYS), F32), jax.ShapeDtypeStruct((1, LANES), F32)],
        scratch_shapes=[pltpu.VMEM((2, LANES, 2 * LANES), F32)],
        compiler_params=_cparams("arbitrary"),
    )(proj, proj, proj, proj, proj, dcat, oraw, states, gate_up_pad, gate_bias, gnorm)


def _ln_stats(r):
    mu = jnp.mean(r, axis=1, keepdims=True)
    xc = r - mu
    rstd = lax.rsqrt(jnp.mean(xc * xc, axis=1, keepdims=True) + LN_EPS)
    return xc * rstd, rstd


def _ln_bwd(dy_g, xhat, rstd):
    return rstd * (dy_g - jnp.mean(dy_g, axis=1, keepdims=True) - xhat * jnp.mean(dy_g * xhat, axis=1, keepdims=True))


def _outproj_ln1(sb_o, gla_o, x, w_out, g1, b1, t, tm=256):
    def body(sb_ref, gl_ref, x_ref, w_ref, g_ref, b_ref, xhat_ref, rstd_ref, h_ref):
        mix = _dot(sb_ref[...], w_ref[0:SB_WIDTH, :]) + _dot(gl_ref[...], w_ref[SB_WIDTH:, :])
        xhat, rstd = _ln_stats(ALPHA * x_ref[...] + mix)
        xhat_ref[...] = xhat
        rstd_ref[...] = rstd
        h_ref[...] = (xhat * g_ref[...] + b_ref[...]).astype(BF16)

    row = lambda w: pl.BlockSpec((tm, w), lambda i: (i, 0))
    full = lambda shp: pl.BlockSpec(shp, lambda i: (0, 0))
    return pl.pallas_call(
        body, name="outproj_ln1", grid=(t // tm,),
        in_specs=[row(SB_WIDTH), row(GLA_WIDTH), row(D_MODEL), full((D_MODEL, D_MODEL)), full((1, D_MODEL)), full((1, D_MODEL))],
        out_specs=[row(D_MODEL), row(1), row(D_MODEL)],
        out_shape=[jax.ShapeDtypeStruct((t, D_MODEL), F32), jax.ShapeDtypeStruct((t, 1), F32),
                   jax.ShapeDtypeStruct((t, D_MODEL), BF16)],
        compiler_params=_cparams("parallel"),
    )(sb_o, gla_o, x, w_out, g1, b1)


_INV_SQRT2 = 1.0 / math.sqrt(2.0)
_INV_SQRT2PI = 1.0 / math.sqrt(2.0 * math.pi)


def _conv3(xs, w_ref, b_ref, half):
    return (w_ref[half, 0:1, :] * pltpu.roll(xs, 2, 0) + w_ref[half, 1:2, :] * pltpu.roll(xs, 1, 0)
            + w_ref[half, 2:3, :] * xs + b_ref[half])


HALO = 16


def _conv_gelu_fwd(up3, conv_w3, conv_b3, t, tr=512, ct=256):
    nct = D_FF // ct
    hb = tr // HALO

    def body(cur_ref, prev_ref, w_ref, b_ref, gm_ref):
        i = pl.program_id(1)
        keep = (i > 0).astype(F32)
        us = []
        for half in range(2):
            xs = jnp.concatenate([prev_ref[half].astype(F32) * keep, cur_ref[half].astype(F32)], axis=0)
            us.append(_conv3(xs, w_ref, b_ref, half)[HALO:, :])
        a, c = us
        gelu = 0.5 * a * (1.0 + lax.erf(a * _INV_SQRT2))
        gm_ref[...] = (gelu * c).astype(BF16)

    return pl.pallas_call(
        body, name="conv_gelu_fwd", grid=(nct, t // tr),
        in_specs=[pl.BlockSpec((2, tr, ct), lambda j, i: (0, i, j)),
                  pl.BlockSpec((2, HALO, ct), lambda j, i: (0, jnp.maximum(i * hb - 1, 0), j)),
                  pl.BlockSpec((2, 3, ct), lambda j, i: (0, 0, j)),
                  pl.BlockSpec((2, 1, ct), lambda j, i: (0, 0, j))],
        out_specs=pl.BlockSpec((tr, ct), lambda j, i: (i, j)),
        out_shape=jax.ShapeDtypeStruct((t, D_FF), BF16),
        compiler_params=_cparams("parallel", "parallel"),
    )(up3, up3, conv_w3, conv_b3)


def _conv_gelu_bwd(up3, dgm, conv_w3, conv_b3, t, tr=512, ct=256):
    nct = D_FF // ct
    nrt = t // tr
    hb = tr // HALO
    n = tr + 2 * HALO
    lo, hi = HALO, tr + HALO

    def body(cur_ref, prev_ref, next_ref, dg_ref, dgn_ref, w_ref, b_ref, dup_ref, dcw_ref, dcb_ref):
        i = pl.program_id(1)

        @pl.when(i == 0)
        def _():
            dcw_ref[...] = jnp.zeros_like(dcw_ref)
            dcb_ref[...] = jnp.zeros_like(dcb_ref)

        keep_prev = (i > 0).astype(F32)
        keep_next = (i < nrt - 1).astype(F32)
        xs, xm1, xm2, us = [], [], [], []
        for half in range(2):
            x = jnp.concatenate([prev_ref[half].astype(F32) * keep_prev, cur_ref[half].astype(F32),
                                 next_ref[half].astype(F32)], axis=0)
            xs.append(x)
            xm1.append(pltpu.roll(x, 1, 0))
            xm2.append(pltpu.roll(x, 2, 0))
            us.append(w_ref[half, 0:1, :] * xm2[half] + w_ref[half, 1:2, :] * xm1[half]
                      + w_ref[half, 2:3, :] * x + b_ref[half])
        a, c = us
        dg = jnp.concatenate([jnp.zeros((HALO, ct), F32), dg_ref[...].astype(F32),
                              dgn_ref[...].astype(F32) * keep_next], axis=0)
        cdf = 0.5 * (1.0 + lax.erf(a * _INV_SQRT2))
        pdf = jnp.exp(-0.5 * a * a) * _INV_SQRT2PI
        dus = [dg * c * (cdf + a * pdf), dg * (a * cdf)]
        rid = lax.broadcasted_iota(jnp.int32, (8, 1), 0)
        for half in range(2):
            du = dus[half]
            dup = (w_ref[half, 2:3, :] * du + w_ref[half, 1:2, :] * pltpu.roll(du, n - 1, 0)
                   + w_ref[half, 0:1, :] * pltpu.roll(du, n - 2, 0))
            dup_ref[half] = dup[lo:hi, :].astype(BF16)
            duc = du[lo:hi, :]
            s0 = jnp.sum(duc * xm2[half][lo:hi, :], axis=0, keepdims=True)
            s1 = jnp.sum(duc * xm1[half][lo:hi, :], axis=0, keepdims=True)
            s2 = jnp.sum(duc * xs[half][lo:hi, :], axis=0, keepdims=True)
            dcw_ref[half] += jnp.where(rid == 0, s0, jnp.where(rid == 1, s1, jnp.where(rid == 2, s2, 0.0)))
            dcb_ref[half] += jnp.sum(duc, axis=0, keepdims=True)

    last = t // HALO - 1
    return pl.pallas_call(
        body, name="conv_gelu_bwd", grid=(nct, nrt),
        in_specs=[pl.BlockSpec((2, tr, ct), lambda j, i: (0, i, j)),
                  pl.BlockSpec((2, HALO, ct), lambda j, i: (0, jnp.maximum(i * hb - 1, 0), j)),
                  pl.BlockSpec((2, HALO, ct), lambda j, i: (0, jnp.minimum((i + 1) * hb, last), j)),
                  pl.BlockSpec((tr, ct), lambda j, i: (i, j)),
                  pl.BlockSpec((HALO, ct), lambda j, i: (jnp.minimum((i + 1) * hb, last), j)),
                  pl.BlockSpec((2, 3, ct), lambda j, i: (0, 0, j)),
                  pl.BlockSpec((2, 1, ct), lambda j, i: (0, 0, j))],
        out_specs=[pl.BlockSpec((2, tr, ct), lambda j, i: (0, i, j)),
                   pl.BlockSpec((2, 8, ct), lambda j, i: (0, 0, j)),
                   pl.BlockSpec((2, 1, ct), lambda j, i: (0, 0, j))],
        out_shape=[jax.ShapeDtypeStruct((2, t, D_FF), BF16), jax.ShapeDtypeStruct((2, 8, D_FF), F32),
                   jax.ShapeDtypeStruct((2, 1, D_FF), F32)],
        compiler_params=_cparams("parallel", "arbitrary"),
    )(up3, up3, up3, dgm, dgm, conv_w3, conv_b3)


def _down_ln2_loss(gm, w_down, xhat1, g1, b1, g2, b2, target, t, tm=256):
    def body(gm_ref, w_ref, xh_ref, g1_ref, b1_ref, g2_ref, b2_ref, tg_ref, dr_ref, loss_ref, dg_ref, db_ref):
        i = pl.program_id(0)

        @pl.when(i == 0)
        def _():
            loss_ref[...] = jnp.zeros_like(loss_ref)
            dg_ref[...] = jnp.zeros_like(dg_ref)
            db_ref[...] = jnp.zeros_like(db_ref)

        h = xh_ref[...] * g1_ref[...] + b1_ref[...]
        xhat, rstd = _ln_stats(ALPHA * h + _dot(gm_ref[...], w_ref[...]))
        err = xhat * g2_ref[...] + b2_ref[...] - tg_ref[...]
        loss_ref[...] += 0.5 * jnp.sum(jnp.sum(err * err, axis=1, keepdims=True), axis=0, keepdims=True) / D_MODEL
        dy = err * (1.0 / D_MODEL)
        dg_ref[...] += jnp.sum(dy * xhat, axis=0, keepdims=True)
        db_ref[...] += jnp.sum(dy, axis=0, keepdims=True)
        dr_ref[...] = _ln_bwd(dy * g2_ref[...], xhat, rstd)

    row = lambda w: pl.BlockSpec((tm, w), lambda i: (i, 0))
    full = lambda shp: pl.BlockSpec(shp, lambda i: (0, 0))
    vec = full((1, D_MODEL))
    return pl.pallas_call(
        body, name="down_ln2_loss", grid=(t // tm,),
        in_specs=[row(D_FF), full((D_FF, D_MODEL)), row(D_MODEL), vec, vec, vec, vec, row(D_MODEL)],
        out_specs=[row(D_MODEL), full((1, 1)), vec, vec],
        out_shape=[jax.ShapeDtypeStruct((t, D_MODEL), F32), jax.ShapeDtypeStruct((1, 1), F32),
                   jax.ShapeDtypeStruct((1, D_MODEL), F32), jax.ShapeDtypeStruct((1, D_MODEL), F32)],
        compiler_params=_cparams("arbitrary"),
    )(gm, w_down, xhat1, g1, b1, g2, b2, target)


def _dh_ln1_bwd(dup3, w_up4, dr2, xhat1, rstd1, g1, t, tm=512):
    tk = 2 * D_FF // 4
    nkh = D_FF // tk
    nk = 2 * nkh

    def body(a_ref, w_ref, dr2_ref, xh_ref, rs_ref, g_ref, dr1_ref, dg_ref, db_ref, acc_ref):
        i = pl.program_id(0)
        kk = pl.program_id(1)

        @pl.when((i == 0) & (kk == 0))
        def _():
            dg_ref[...] = jnp.zeros_like(dg_ref)
            db_ref[...] = jnp.zeros_like(db_ref)

        part = _dot_nt(a_ref[...], w_ref[...])

        @pl.when(kk == 0)
        def _():
            acc_ref[...] = part

        @pl.when(kk > 0)
        def _():
            acc_ref[...] += part

        @pl.when(kk == nk - 1)
        def _():
            dh = acc_ref[...] + ALPHA * dr2_ref[...]
            xhat = xh_ref[...]
            dg_ref[...] += jnp.sum(dh * xhat, axis=0, keepdims=True)
            db_ref[...] += jnp.sum(dh, axis=0, keepdims=True)
            dr1_ref[...] = _ln_bwd(dh * g_ref[...], xhat, rs_ref[...])

    row = lambda w: pl.BlockSpec((tm, w), lambda i, kk: (i, 0))
    vec = pl.BlockSpec((1, D_MODEL), lambda i, kk: (0, 0))
    return pl.pallas_call(
        body, name="dh_ln1_bwd", grid=(t // tm, nk),
        in_specs=[pl.BlockSpec((None, tm, tk), lambda i, kk: (kk // nkh, i, kk % nkh)),
                  pl.BlockSpec((None, D_MODEL, tk), lambda i, kk: (kk, 0, 0)),
                  row(D_MODEL), row(D_MODEL), row(1), vec],
        out_specs=[row(D_MODEL), vec, vec],
        out_shape=[jax.ShapeDtypeStruct((t, D_MODEL), F32), jax.ShapeDtypeStruct((1, D_MODEL), F32),
                   jax.ShapeDtypeStruct((1, D_MODEL), F32)],
        scratch_shapes=[pltpu.VMEM((tm, D_MODEL), F32)],
        compiler_params=_cparams("arbitrary", "arbitrary"),
    )(dup3, w_up4, dr2, xhat1, rstd1, g1)


def _adamw(w, g, m, v, name):
    rows, cols = w.shape
    tr = rows
    for cand in (256, 128, 64, 32, 16, 8):
        if rows % cand == 0 and rows > cand:
            tr = cand
            break
    c1 = 1.0 / (1.0 - ADAM_B1 ** ADAM_STEP)
    c2 = 1.0 / (1.0 - ADAM_B2 ** ADAM_STEP)

    def body(w_ref, g_ref, m_ref, v_ref, d_ref, nm_ref, nv_ref):
        gv = g_ref[...]
        nm = ADAM_B1 * m_ref[...] + (1.0 - ADAM_B1) * gv
        nv = ADAM_B2 * v_ref[...] + (1.0 - ADAM_B2) * (gv * gv)
        d_ref[...] = -ADAM_LR * ((nm * c1) / (jnp.sqrt(nv * c2) + ADAM_EPS) + ADAM_WD * w_ref[...])
        nm_ref[...] = nm
        nv_ref[...] = nv

    spec = pl.BlockSpec((tr, cols), lambda i: (i, 0))
    out = jax.ShapeDtypeStruct((rows, cols), F32)
    return pl.pallas_call(
        body, name=name, grid=(rows // tr,), in_specs=[spec] * 4, out_specs=[spec] * 3, out_shape=[out] * 3,
        compiler_params=_cparams("parallel"),
    )(w, g, m, v)


def _local_step(x, target, w_in_p, late_shards, gate_up_pad, gate_bias, gnorm, ln1_g, ln1_b, conv_w3, conv_b3,
                ln2_g, ln2_b, c_arr, kc_arr):
    t = x.shape[0]
    tq = min(t, 1024)
    proj = _mm(x, w_in_p, m=t, n=IN_PAD, k=D_MODEL, tm=tq, tn=640, tk=D_MODEL, name="proj")
    sb_o, *partly = _sb_fwd(proj, t, late_shards, LATE_SHAPES)
    gla_o, oraw, states, w_up4, w_out4, w_down4 = _gla_fwd(proj, gate_up_pad, gate_bias, gnorm, t, partly, LATE_SHAPES)
    w_out = w_out4.reshape(D_MODEL, D_MODEL)
    w_down = w_down4.reshape(D_FF, D_MODEL)
    xhat1, rstd1, h_bf = _outproj_ln1(sb_o, gla_o, x, w_out, ln1_g, ln1_b, t)
    up3 = _mm(h_bf, w_up4, m=t, n=2 * D_FF, k=D_MODEL, tm=tq, tn=W_UP_S, tk=D_MODEL, name="up",
              b_spec=pl.BlockSpec((None, D_MODEL, W_UP_S), lambda i, j, kk: (j, 0, 0)),
              o_spec=pl.BlockSpec((None, tq, W_UP_S), lambda i, j, kk: (j // 2, i, j % 2)),
              out_shape=jax.ShapeDtypeStruct((2, t, D_FF), BF16), out_dtype=BF16)
    gm = _conv_gelu_fwd(up3, conv_w3, conv_b3, t, tr=tq)
    dr2, loss, dln2_g, dln2_b = _down_ln2_loss(gm, w_down, xhat1, ln1_g, ln1_b, ln2_g, ln2_b, target, t)
    dgm = _mm(dr2, w_down, m=t, n=D_FF, k=D_MODEL, tm=tq, tn=W_UP_S, tk=D_MODEL, tb=True, out_dtype=BF16, name="dgm")
    dw_down = _mm(gm, dr2, m=D_FF, n=D_MODEL, k=t, tm=W_UP_S, tn=D_MODEL, tk=tq, ta=True, name="dw_down")
    dup3, dcw, dcb = _conv_gelu_bwd(up3, dgm, conv_w3, conv_b3, t)
    dr1, dln1_g, dln1_b = _dh_ln1_bwd(dup3, w_up4, dr2, xhat1, rstd1, ln1_g, t)
    dw_up4 = _mm(h_bf, dup3, m=D_MODEL, n=2 * D_FF, k=t, tm=512, tn=W_UP_S, tk=tq, ta=True, name="dw_up",
                 b_spec=pl.BlockSpec((None, tq, W_UP_S), lambda i, j, kk: (j // 2, kk, j % 2)),
                 o_spec=pl.BlockSpec((None, 512, W_UP_S), lambda i, j, kk: (j, i, 0)),
                 out_shape=jax.ShapeDtypeStruct((4, D_MODEL, W_UP_S), F32))
    dcat = _mm(dr1, w_out, m=t, n=D_MODEL, k=D_MODEL, tm=tq, tn=512, tk=D_MODEL, tb=True, name="dcat")
    dw_out_sb = _mm(sb_o, dr1, m=SB_WIDTH, n=D_MODEL, k=t, tm=512, tn=D_MODEL, tk=tq, ta=True, name="dw_out_sb")
    dw_out_gla = _mm(gla_o, dr1, m=GLA_WIDTH, n=D_MODEL, k=t, tm=512, tn=D_MODEL, tk=tq, ta=True, name="dw_out_gla")
    gs = [dw_up4, jnp.concatenate([dw_out_sb, dw_out_gla], axis=0).reshape(4, W_OUT_S, D_MODEL),
          dw_down.reshape(4, W_DOWN_S, D_MODEL)]
    from_sib = _exchange_sibling(gs, LATE_SHAPES, "exchange_sibling_late")
    ps = [_add_sibling(gs[m], from_sib[m], c_arr, LATE_ADD_ROWS[m], "add_sibling_late_%d" % m) for m in range(3)]
    dsq, dsk, dsv, *others = _sb_bwd(proj, dcat, t, ps)
    late_sums = [_add_chips(ps[m], others[m], kc_arr, LATE_ADD_ROWS[m], "add_chips_late_%d" % m) for m in range(3)]
    dgla, dgup_pad, dgbias, dgnorm = _gla_bwd(proj, dcat, oraw, states, gate_up_pad, gate_bias, gnorm, t)
    dproj = jnp.concatenate([dsq, dsk, dsv, dgla], axis=1).astype(BF16)
    dx = _mm(dproj, w_in_p, m=t, n=D_MODEL, k=IN_PAD, tm=512, tn=D_MODEL, tk=IN_PAD, tb=True, add=dr1, add_scale=ALPHA, name="dx")
    dw_in_p = _mm(x, dproj, m=D_MODEL, n=IN_PAD, k=t, tm=512, tn=IN_PAD, tk=tq, ta=True, name="dw_in")
    small = dict(
        gate_up=dgup_pad[:GATE_RANK], gate_bias=dgbias, gla_norm_g=dgnorm, ln1_g=dln1_g, ln1_b=dln1_b,
        conv_w=jnp.concatenate([dcw[0, :3], dcw[1, :3]], axis=1), conv_b=jnp.concatenate([dcb[0], dcb[1]], axis=1),
        ln2_g=dln2_g, ln2_b=dln2_b)
    return loss, dx, late_sums, dw_in_p, small


W_IN_S, W_UP_S, W_OUT_S, W_DOWN_S = IN_WIDTH // 4, 2 * D_FF // 4, D_MODEL // 4, D_FF // 4
SHARD_SHAPES = ((D_MODEL, W_IN_S), (D_MODEL, W_UP_S), (W_OUT_S, D_MODEL), (W_DOWN_S, D_MODEL))
ADD_ROWS = (256, 256, 128, 176)
LATE_SHAPES, LATE_ADD_ROWS = SHARD_SHAPES[1:], ADD_ROWS[1:]
SMALL_ROWS = 8
VEC_SIZES = (("gate_bias", GLA_KEYS), ("gla_norm_g", LANES), ("ln1_g", D_MODEL), ("ln1_b", D_MODEL),
             ("conv_b", 2 * D_FF), ("ln2_g", D_MODEL), ("ln2_b", D_MODEL))
SMALL_GRADS = VEC_SIZES + (("conv_w", 3 * 2 * D_FF), ("gate_up", GATE_RANK * GLA_KEYS), ("loss", 1))


def _rows(a):
    flat = a.reshape(-1)
    pad = (-flat.shape[0]) % D_MODEL
    if pad:
        flat = jnp.pad(flat, (0, pad))
    return flat.reshape(-1, D_MODEL)


def _pad_rows(a, rows):
    return jnp.pad(a, ((0, rows - a.shape[0]), (0, 0)))


def _pack_vec(d, sizes, rows):
    flat = jnp.concatenate([d[n].reshape(-1) for n, _ in sizes])
    return _pad_rows(_rows(flat), rows)


def _unpack_vec(v, sizes):
    flat = v.reshape(-1)
    out, o = {}, 0
    for n, size in sizes:
        out[n] = flat[o:o + size].reshape(1, size)
        o += size
    return out


VEC_ROWS = 16
GRAD_VEC_ROWS = 32


HBM_SPEC = pl.BlockSpec(memory_space=pltpu.HBM)


def _position():
    x, y, c = lax.axis_index("x"), lax.axis_index("y"), lax.axis_index("c")
    chips = [(1 - x, y), (x, 1 - y), (1 - x, 1 - y)]
    return x, y, c, chips


def _remote(src, dst, send_sems, recv_sems, k, to):
    return pltpu.make_async_remote_copy(src_ref=src, dst_ref=dst, send_sem=send_sems.at[k], recv_sem=recv_sems.at[k],
                                        device_id=to, device_id_type=MESH)


def _gather_ici(in_refs, out_refs, shapes, send_sems, recv_sems, local_sems):
    x, y, c, chips = _position()
    k_me = 2 * x + y
    local, sends, recvs = [], [], []
    for m, (rows, _) in enumerate(shapes):
        h = rows // 2
        local.append(pltpu.make_async_copy(in_refs[m], out_refs[m].at[k_me], local_sems.at[m]))
        for j, (cx, cy) in enumerate(chips):
            sends.append(_remote(in_refs[m].at[pl.ds(c * h, h), :], out_refs[m].at[k_me, pl.ds(c * h, h), :],
                                 send_sems, recv_sems, 3 * m + j, (cx, cy, c)))
            landed = out_refs[m].at[2 * cx + cy, pl.ds(c * h, h), :]
            recvs.append(_remote(landed, landed, send_sems, recv_sems, 3 * m + j, (x, y, c)))
    return local, sends, recvs


def _gather_d2d(src_refs, dst_refs, shapes, send_sems, recv_sems, base):
    x, y, c, chips = _position()
    sends, recvs = [], []
    for m, (rows, _) in enumerate(shapes):
        h = rows // 2
        for j, (cx, cy) in enumerate(chips):
            k = 2 * cx + cy
            sends.append(_remote(src_refs[m].at[k, pl.ds(c * h, h), :], dst_refs[m].at[k, pl.ds(c * h, h), :],
                                 send_sems, recv_sems, base + 3 * m + j, (x, y, 1 - c)))
            landed = dst_refs[m].at[k, pl.ds((1 - c) * h, h), :]
            recvs.append(_remote(landed, landed, send_sems, recv_sems, base + 3 * m + j, (x, y, c)))
    return sends, recvs


def _gather_weights(shards, small, shapes):
    nm = len(shards)
    n_ici = 3 * nm

    def body(*refs):
        in_refs, small_ref = refs[:nm], refs[nm]
        out_refs, osm_ref = refs[nm + 1:2 * nm + 1], refs[2 * nm + 1]
        send_sems, recv_sems, local_sems = refs[2 * nm + 2:]
        x, y, c, chips = _position()
        k_me = 2 * x + y
        local, sends, recvs = _gather_ici(in_refs, out_refs, shapes, send_sems, recv_sems, local_sems)
        local.append(pltpu.make_async_copy(small_ref, osm_ref.at[k_me], local_sems.at[nm]))
        for j, (cx, cy) in enumerate(chips):
            sends.append(_remote(small_ref, osm_ref.at[k_me], send_sems, recv_sems, n_ici + j, (cx, cy, c)))
        for cp in local + sends:
            cp.start()
        for cp in recvs:
            cp.wait_recv()
        fsends, frecvs = _gather_d2d(out_refs, out_refs, shapes, send_sems, recv_sems, n_ici + 3)
        for cp in fsends:
            cp.start()
        for j, (cx, cy) in enumerate(chips):
            k = 2 * cx + cy
            frecvs.append(_remote(osm_ref.at[k], osm_ref.at[k], send_sems, recv_sems, n_ici + j, (x, y, c)))
        for cp in frecvs:
            cp.wait_recv()
        for cp in sends + fsends:
            cp.wait_send()
        for cp in local:
            cp.wait()

    n_sems = 2 * n_ici + 3
    return pl.pallas_call(
        body, name="gather_weights", in_specs=[HBM_SPEC] * (nm + 1), out_specs=[HBM_SPEC] * (nm + 1),
        out_shape=[jax.ShapeDtypeStruct((4,) + s, BF16) for s in shapes]
        + [jax.ShapeDtypeStruct((4, SMALL_ROWS, D_MODEL), F32)],
        scratch_shapes=[pltpu.SemaphoreType.DMA((n_sems,)), pltpu.SemaphoreType.DMA((n_sems,)),
                        pltpu.SemaphoreType.DMA((nm + 1,))],
    )(*shards, small)


def _exchange_sibling(gs, shapes, name):
    nm = len(gs)

    def body(*refs):
        g_refs, r_refs = refs[:nm], refs[nm:2 * nm]
        send_sems, recv_sems = refs[2 * nm:]
        x, y, c, _ = _position()
        copies = []
        for m, (rows, _) in enumerate(shapes):
            h = rows // 2
            for k in range(4):
                copies.append(_remote(g_refs[m].at[k, pl.ds((1 - c) * h, h), :], r_refs[m].at[k], send_sems, recv_sems,
                                      4 * m + k, (x, y, 1 - c)))
        for cp in copies:
            cp.start()
        for cp in copies:
            cp.wait()

    return pl.pallas_call(
        body, name=name, in_specs=[HBM_SPEC] * nm, out_specs=[HBM_SPEC] * nm,
        out_shape=[jax.ShapeDtypeStruct((4, r // 2, cl), F32) for r, cl in shapes],
        scratch_shapes=[pltpu.SemaphoreType.DMA((4 * nm,)), pltpu.SemaphoreType.DMA((4 * nm,))],
    )(*gs)


def _add_sibling(g, r, c_arr, tr, name):
    _, rows, cols = g.shape
    nb = rows // 2 // tr

    def body(c_ref, g_ref, r_ref, o_ref):
        o_ref[...] = (g_ref[...] + r_ref[...]).astype(BF16)

    spec = pl.BlockSpec((None, tr, cols), lambda k, i, c: (k, i, 0))
    return pl.pallas_call(
        body, name=name,
        grid_spec=pltpu.PrefetchScalarGridSpec(
            num_scalar_prefetch=1, grid=(4, nb),
            in_specs=[pl.BlockSpec((None, tr, cols), lambda k, i, c: (k, c[0] * nb + i, 0)), spec], out_specs=spec),
        out_shape=jax.ShapeDtypeStruct((4, rows // 2, cols), BF16), compiler_params=_cparams("parallel", "parallel"),
    )(c_arr, g, r)


def _reduce_ici(p_refs, r_refs, send_sems, recv_sems):
    x, y, c, chips = _position()
    sends, recvs = [], []
    for m in range(len(p_refs)):
        for j, (cx, cy) in enumerate(chips):
            sends.append(_remote(p_refs[m].at[2 * cx + cy], r_refs[m].at[j], send_sems, recv_sems, 3 * m + j, (cx, cy, c)))
            recvs.append(_remote(r_refs[m].at[j], r_refs[m].at[j], send_sems, recv_sems, 3 * m + j, (x, y, c)))
    return sends, recvs


def _exchange_chips(ps, vec):
    nm = len(ps)
    n_ici = 3 * nm

    def body(*refs):
        p_refs, vec_ref = refs[:nm], refs[nm]
        r_refs, vrecv_ref = refs[nm + 1:2 * nm + 1], refs[2 * nm + 1]
        send_sems, recv_sems, local_sem = refs[2 * nm + 2:]
        x, y, c, _ = _position()
        my_id = 4 * x + 2 * y + c
        vloc = pltpu.make_async_copy(vec_ref, vrecv_ref.at[my_id], local_sem.at[0])
        vloc.start()
        sends, recvs = _reduce_ici(p_refs, r_refs, send_sems, recv_sems)
        for r in range(1, 8):
            peer = (1 - x if r & 4 else x, 1 - y if r & 2 else y, 1 - c if r & 1 else c)
            sends.append(_remote(vec_ref, vrecv_ref.at[my_id], send_sems, recv_sems, n_ici + r - 1, peer))
            recvs.append(_remote(vec_ref, vrecv_ref.at[0], send_sems, recv_sems, n_ici + r - 1, (x, y, c)))
        for cp in sends:
            cp.start()
        for cp in recvs:
            cp.wait_recv()
        for cp in sends:
            cp.wait_send()
        vloc.wait()

    n_sems = n_ici + 7
    return pl.pallas_call(
        body, name="exchange_chips", in_specs=[HBM_SPEC] * (nm + 1), out_specs=[HBM_SPEC] * (nm + 1),
        out_shape=[jax.ShapeDtypeStruct((3,) + p.shape[1:], p.dtype) for p in ps]
        + [jax.ShapeDtypeStruct((8, GRAD_VEC_ROWS, D_MODEL), F32)],
        scratch_shapes=[pltpu.SemaphoreType.DMA((n_sems,)), pltpu.SemaphoreType.DMA((n_sems,)), pltpu.SemaphoreType.DMA((1,))],
    )(*ps, vec)


def _add_chips(p, r, kc_arr, tr, name):
    _, h, cols = p.shape
    nb = h // tr

    def body(kc_ref, p_ref, r0_ref, r1_ref, r2_ref, o_ref):
        o_ref[...] = ((p_ref[...].astype(F32) + r0_ref[...].astype(F32)) + r1_ref[...].astype(F32)) + r2_ref[...].astype(F32)

    rspec = lambda j: pl.BlockSpec((None, tr, cols), lambda i, kc: (j, i, 0))
    return pl.pallas_call(
        body, name=name,
        grid_spec=pltpu.PrefetchScalarGridSpec(
            num_scalar_prefetch=1, grid=(nb,),
            in_specs=[pl.BlockSpec((None, tr, cols), lambda i, kc: (kc[0], i, 0)), rspec(0), rspec(1), rspec(2)],
            out_specs=pl.BlockSpec((tr, cols), lambda i, kc: (kc[1] * nb + i, 0))),
        out_shape=jax.ShapeDtypeStruct((2 * h, cols), F32), compiler_params=_cparams("parallel"),
    )(kc_arr, p, r, r, r)


def _reunite_sibling(fs, shapes):
    n_chunks = 2
    nm = len(fs)

    def body(*refs):
        in_refs, out_refs = refs[:nm], refs[nm:2 * nm]
        send_sems, recv_sems = refs[2 * nm:]
        x, y, c, _ = _position()
        sends, recvs = [], []
        for m in range(nm):
            ch = shapes[m][0] // 2 // n_chunks
            for q in range(n_chunks):
                mine = pl.ds((c * n_chunks + q) * ch, ch)
                theirs = pl.ds(((1 - c) * n_chunks + q) * ch, ch)
                s = n_chunks * m + q
                sends.append(_remote(in_refs[m].at[mine, :], out_refs[m].at[mine, :], send_sems, recv_sems, s, (x, y, 1 - c)))
                recvs.append(_remote(in_refs[m].at[theirs, :], out_refs[m].at[theirs, :], send_sems, recv_sems, s, (x, y, c)))
        for cp in sends:
            cp.start()
        for cp in recvs:
            cp.wait_recv()
        for cp in sends:
            cp.wait_send()

    n_sems = n_chunks * nm
    return pl.pallas_call(
        body, name="reunite_sibling", in_specs=[HBM_SPEC] * nm, out_specs=[HBM_SPEC] * nm,
        out_shape=[jax.ShapeDtypeStruct(s, F32) for s in shapes],
        input_output_aliases={m: m for m in range(nm)},
        scratch_shapes=[pltpu.SemaphoreType.DMA((n_sems,)), pltpu.SemaphoreType.DMA((n_sems,))],
    )(*fs)


def _sum_vecs(v):
    def body(v_ref, o_ref):
        acc = v_ref[0]
        for d in range(1, 8):
            acc = acc + v_ref[d]
        o_ref[...] = acc

    return pl.pallas_call(body, name="sum_vecs", out_shape=jax.ShapeDtypeStruct(v.shape[1:], F32))(v)


def kernel(x, w_in, gate_up, gate_bias, gla_norm_g, w_out, ln1_g, ln1_b, w_up, conv_w, conv_b, w_down, ln2_g, ln2_b, loss_target, m_w_in, m_gate_up, m_gate_bias, m_gla_norm_g, m_w_out, m_ln1_g, m_ln1_b, m_w_up, m_conv_w, m_conv_b, m_w_down, m_ln2_g, m_ln2_b, v_w_in, v_gate_up, v_gate_bias, v_gla_norm_g, v_w_out, v_ln1_g, v_ln1_b, v_w_up, v_conv_w, v_conv_b, v_w_down, v_ln2_g, v_ln2_b):
    xi, yi, ci = lax.axis_index("x"), lax.axis_index("y"), lax.axis_index("c")
    k_me = 2 * xi + yi
    c_arr = ci.astype(jnp.int32).reshape(1)
    kc_arr = jnp.stack([k_me, ci]).astype(jnp.int32)
    small = _pad_rows(jnp.concatenate([_rows(conv_w[0]), _rows(gate_up[0])], axis=0), SMALL_ROWS)
    w_in4, gsmall = _gather_weights([w_in[0].astype(BF16)], small, SHARD_SHAPES[:1])
    late_shards = [w_up[0].astype(BF16), w_out[0].astype(BF16), w_down[0].astype(BF16)]
    w_in_p = jnp.pad(jnp.concatenate([w_in4[k] for k in range(4)], axis=1), ((0, 0), (0, IN_PAD - IN_WIDTH)))
    conv_w_f = jnp.concatenate([gsmall[k, :5].reshape(-1)[:3 * W_UP_S].reshape(3, W_UP_S) for k in range(4)], axis=1)
    gate_up_f = jnp.concatenate([gsmall[k, 5].reshape(GATE_RANK, GLA_KEYS // 4) for k in range(4)], axis=1)
    conv_w3 = jnp.transpose(conv_w_f.reshape(3, 2, D_FF), (1, 0, 2))
    conv_b3 = conv_b.reshape(2, 1, D_FF)
    gate_up_pad = jnp.pad(gate_up_f, ((0, LANES - GATE_RANK), (0, 0)))

    loss_part, dx, late_sums, dw_in_p, g = _local_step(
        x[0], loss_target[0], w_in_p, late_shards, gate_up_pad, gate_bias, gla_norm_g, ln1_g, ln1_b, conv_w3, conv_b3,
        ln2_g, ln2_b, c_arr, kc_arr)
    g["loss"] = loss_part

    g_in4 = jnp.stack([dw_in_p[:, k * W_IN_S:(k + 1) * W_IN_S] for k in range(4)], axis=0)
    from_sib, = _exchange_sibling([g_in4], SHARD_SHAPES[:1], "exchange_sibling_w_in")
    p_in = _add_sibling(g_in4, from_sib, c_arr, ADD_ROWS[0], "add_sibling_w_in")
    others_in, vecs = _exchange_chips([p_in], _pack_vec(g, SMALL_GRADS, GRAD_VEC_ROWS))
    sum_in = _add_chips(p_in, others_in, kc_arr, ADD_ROWS[0], "add_chips_w_in")
    g_w_in, g_w_up, g_w_out, g_w_down = _reunite_sibling([sum_in] + late_sums, SHARD_SHAPES)
    gsmall_sum = _unpack_vec(_sum_vecs(vecs), SMALL_GRADS)
    g_conv_w = lax.dynamic_slice_in_dim(gsmall_sum["conv_w"].reshape(3, 2 * D_FF), k_me * W_UP_S, W_UP_S, axis=1)
    g_gate_up = lax.dynamic_slice_in_dim(gsmall_sum["gate_up"].reshape(GATE_RANK, GLA_KEYS), k_me * (GLA_KEYS // 4),
                                         GLA_KEYS // 4, axis=1)
    gv = gsmall_sum
    loss = gv["loss"][0, 0]
    gvec = _pack_vec(gv, VEC_SIZES, VEC_ROWS)

    grads = dict(w_in=g_w_in[None], gate_up=g_gate_up[None], gate_bias=gv["gate_bias"], gla_norm_g=gv["gla_norm_g"],
                 w_out=g_w_out[None], ln1_g=gv["ln1_g"], ln1_b=gv["ln1_b"], w_up=g_w_up[None], conv_w=g_conv_w[None],
                 conv_b=gv["conv_b"], w_down=g_w_down[None], ln2_g=gv["ln2_g"], ln2_b=gv["ln2_b"])
    weights = dict(w_in=w_in, gate_up=gate_up, gate_bias=gate_bias, gla_norm_g=gla_norm_g, w_out=w_out, ln1_g=ln1_g,
                   ln1_b=ln1_b, w_up=w_up, conv_w=conv_w, conv_b=conv_b, w_down=w_down, ln2_g=ln2_g, ln2_b=ln2_b)
    ms = dict(w_in=m_w_in, gate_up=m_gate_up, gate_bias=m_gate_bias, gla_norm_g=m_gla_norm_g, w_out=m_w_out, ln1_g=m_ln1_g,
              ln1_b=m_ln1_b, w_up=m_w_up, conv_w=m_conv_w, conv_b=m_conv_b, w_down=m_w_down, ln2_g=m_ln2_g, ln2_b=m_ln2_b)
    vs = dict(w_in=v_w_in, gate_up=v_gate_up, gate_bias=v_gate_bias, gla_norm_g=v_gla_norm_g, w_out=v_w_out, ln1_g=v_ln1_g,
              ln1_b=v_ln1_b, w_up=v_w_up, conv_w=v_conv_w, conv_b=v_conv_b, w_down=v_w_down, ln2_g=v_ln2_g, ln2_b=v_ln2_b)
    names = ["w_in", "gate_up", "gate_bias", "gla_norm_g", "w_out", "ln1_g", "ln1_b", "w_up", "conv_w", "conv_b", "w_down",
             "ln2_g", "ln2_b"]
    delta, new_m, new_v = {}, {}, {}
    for n in ("w_in", "gate_up", "w_out", "w_up", "conv_w", "w_down"):
        tr_ = jnp.transpose if n == "w_in" else (lambda a: a)
        d, nm, nv = _adamw(tr_(weights[n][0]), tr_(grads[n][0]), tr_(ms[n][0]), tr_(vs[n][0]), "adamw_" + n)
        delta[n], new_m[n], new_v[n] = tr_(d)[None], tr_(nm)[None], tr_(nv)[None]
    d, nm, nv = _adamw(_pack_vec(weights, VEC_SIZES, VEC_ROWS), gvec, _pack_vec(ms, VEC_SIZES, VEC_ROWS),
                       _pack_vec(vs, VEC_SIZES, VEC_ROWS), "adamw_vectors")
    for dst, src in ((delta, d), (new_m, nm), (new_v, nv)):
        dst.update(_unpack_vec(src, VEC_SIZES))
    return (loss, dx[None], *[grads[n] for n in names], *[delta[n] for n in names], *[new_m[n] for n in names],
            *[new_v[n] for n in names])
```

```python
import functools
import math

import jax
import jax.numpy as jnp
from jax import lax
from jax.experimental import pallas as pl
from jax.experimental.pallas import tpu as pltpu

F32 = jnp.float32
BF16 = jnp.bfloat16

D_MODEL = 1024
SB_WIDTH = 512
GLA_KEYS = 256
GLA_WIDTH = 512
GATE_RANK = 16
IN_WIDTH = 3088
IN_PAD = 3200
D_FF = 2816
CHUNK = 64
LN_EPS = 1e-5
RMS_EPS = 1e-6
ALPHA = 2.0 ** 0.25
GLA_TAU = 16.0
SB_SCALE = 0.125
GLA_SCALE = 0.125
LANES = 128
SB_BLK = 256
SB_CUT = -100.0
GLA_ROWS = 256
VMEM_LIMIT = 56 * 1024 * 1024

ADAM_LR, ADAM_B1, ADAM_B2, ADAM_EPS, ADAM_WD, ADAM_STEP = 0.001, 0.9, 0.999, 1e-08, 0.01, 10

MESH = pl.DeviceIdType.MESH


def _cparams(*sem):
    return pltpu.CompilerParams(dimension_semantics=sem, vmem_limit_bytes=VMEM_LIMIT)


def _dot(a, b):
    return jnp.dot(a, b, preferred_element_type=F32)


def _dot_nt(a, b):
    return lax.dot_general(a, b, (((1,), (1,)), ((), ())), preferred_element_type=F32)


def _dot_tn(a, b):
    return lax.dot_general(a, b, (((0,), (0,)), ((), ())), preferred_element_type=F32)


def _split3(x):
    hi = x.astype(BF16)
    r = x - hi.astype(F32)
    mid = r.astype(BF16)
    lo = (r - mid.astype(F32)).astype(BF16)
    return hi, mid, lo


def _softplus(z):
    return jnp.maximum(z, 0.0) + jnp.log(1.0 + jnp.exp(-jnp.abs(z)))


def _sigmoid(z):
    return 1.0 / (1.0 + jnp.exp(-z))


def _mm(a, b, *, m, n, k, tm, tn, tk, ta=False, tb=False, a_spec=None, b_spec=None, o_spec=None,
        out_shape=None, out_dtype=F32, add=None, add_scale=1.0, gather=None, name):
    nk = k // tk
    dn = (((0 if ta else 1,), (1 if tb else 0,)), ((), ()))
    shards, shapes = gather if gather else ((), ())
    ng = len(shards)
    n_add = int(add is not None)
    steps = (m // tm, n // tn, nk)

    def body(*refs):
        a_ref, b_ref = refs[:2]
        add_ref = refs[2] if add is not None else None
        w_refs = refs[2 + n_add:2 + n_add + ng]
        o_ref = refs[2 + n_add + ng]
        g_refs = refs[3 + n_add + ng:3 + n_add + 2 * ng]
        scratch = refs[3 + n_add + 2 * ng:]
        if ng:
            sems = scratch[-3:]
            ids = [pl.program_id(d) for d in range(3)]

            @pl.when((ids[0] == 0) & (ids[1] == 0) & (ids[2] == 0))
            def _():
                local, sends, _ = _gather_ici(w_refs, g_refs, shapes, *sems)
                for cp in local + sends:
                    cp.start()

            @pl.when((ids[0] == steps[0] - 1) & (ids[1] == steps[1] - 1) & (ids[2] == steps[2] - 1))
            def _():
                local, sends, recvs = _gather_ici(w_refs, g_refs, shapes, *sems)
                for cp in recvs:
                    cp.wait_recv()
                for cp in sends:
                    cp.wait_send()
                for cp in local:
                    cp.wait()

        part = lax.dot_general(a_ref[...].astype(BF16), b_ref[...].astype(BF16), dn, preferred_element_type=F32)

        def finish(r):
            if add is not None:
                r = r + add_scale * add_ref[...]
            o_ref[...] = r.astype(out_dtype)

        if nk == 1:
            finish(part)
            return
        acc_ref = scratch[0]
        kk = pl.program_id(2)

        @pl.when(kk == 0)
        def _():
            acc_ref[...] = part

        @pl.when((kk > 0) & (kk < nk - 1))
        def _():
            acc_ref[...] += part

        @pl.when(kk == nk - 1)
        def _():
            finish(acc_ref[...] + part)

    if a_spec is None:
        a_spec = pl.BlockSpec((tk, tm), lambda i, j, kk: (kk, i)) if ta else pl.BlockSpec((tm, tk), lambda i, j, kk: (i, kk))
    if b_spec is None:
        b_spec = pl.BlockSpec((tn, tk), lambda i, j, kk: (j, kk)) if tb else pl.BlockSpec((tk, tn), lambda i, j, kk: (kk, j))
    if o_spec is None:
        o_spec = pl.BlockSpec((tm, tn), lambda i, j, kk: (i, j))
    if out_shape is None:
        out_shape = jax.ShapeDtypeStruct((m, n), out_dtype)
    in_specs = [a_spec, b_spec]
    args = [a, b]
    if add is not None:
        in_specs.append(pl.BlockSpec((tm, tn), lambda i, j, kk: (i, j)))
        args.append(add)
    scratch = [pltpu.VMEM((tm, tn), F32)] if nk > 1 else []
    if not ng:
        return pl.pallas_call(
            body, name=name, grid=steps, in_specs=in_specs, out_specs=o_spec, out_shape=out_shape,
            scratch_shapes=scratch, compiler_params=_cparams("parallel", "parallel", "arbitrary"),
        )(*args)
    scratch += [pltpu.SemaphoreType.DMA((3 * ng,)), pltpu.SemaphoreType.DMA((3 * ng,)), pltpu.SemaphoreType.DMA((ng,))]
    return pl.pallas_call(
        body, name=name, grid=steps, in_specs=in_specs + [HBM_SPEC] * ng, out_specs=[o_spec] + [HBM_SPEC] * ng,
        out_shape=[out_shape] + [jax.ShapeDtypeStruct((4,) + s, BF16) for s in shapes],
        scratch_shapes=scratch, compiler_params=_cparams("arbitrary", "arbitrary", "arbitrary"),
    )(*args, *shards)


def _sb_tile(qh, kj, diag, strict, u_strict, r_in):
    z = _dot_nt(qh, kj)
    sp = _softplus(z)
    l1m = -sp
    lsz = z - sp
    if diag:
        l1m = jnp.where(strict, l1m, 0.0)
    cs = _dot(l1m.astype(BF16), u_strict) + r_in
    w = jnp.exp(lsz + cs)
    if diag:
        w = jnp.where(strict, w, 0.0)
    return l1m, lsz, w


def _sb_consts():
    row = lax.broadcasted_iota(jnp.int32, (SB_BLK, SB_BLK), 0)
    col = lax.broadcasted_iota(jnp.int32, (SB_BLK, SB_BLK), 1)
    strict = col < row
    u_strict = (row > col).astype(BF16)
    u_pre = (row < col).astype(BF16)
    lane = lax.broadcasted_iota(jnp.int32, (1, LANES), 1)
    return strict, u_strict, u_pre, lane


def _sb_fwd(proj, t, shards, shapes):
    nq = t // SB_BLK

    nm = len(shards)

    def body(q_ref, k_ref, v_ref, *rest):
        w_refs, o_ref, g_refs = rest[:nm], rest[nm], rest[nm + 1:2 * nm + 1]
        sems = rest[2 * nm + 1:]
        p = pl.program_id(0)
        i = pl.program_id(1)

        @pl.when((p == 0) & (i == 0))
        def _():
            local, sends, _ = _gather_ici(w_refs, g_refs, shapes, *sems)
            for cp in local + sends:
                cp.start()

        @pl.when((p == 3) & (i == nq - 1))
        def _():
            local, sends, recvs = _gather_ici(w_refs, g_refs, shapes, *sems)
            for cp in recvs:
                cp.wait_recv()
            for cp in sends:
                cp.wait_send()
            for cp in local:
                cp.wait()

        strict, u_strict, _, lane = _sb_consts()
        qf = q_ref[...] * SB_SCALE
        hms = [(lane // 64) == hh for hh in range(2)]
        qhs = [jnp.where(hm, qf, 0.0).astype(BF16) for hm in hms]

        def step(j, r0, r1, a, diag):
            rows = pl.ds(pl.multiple_of(j * SB_BLK, SB_BLK), SB_BLK)
            kj = k_ref[rows, :].astype(BF16)
            vf = v_ref[rows, :]
            rs = []
            for hh, r in enumerate((r0, r1)):
                l1m, _, w = _sb_tile(qhs[hh], kj, diag, strict, u_strict, r)
                a = a + _dot(w.astype(BF16), jnp.where(hms[hh], vf, 0.0).astype(BF16))
                rs.append(r + jnp.sum(l1m, axis=1, keepdims=True))
            return rs[0], rs[1], a

        zero = jnp.zeros((SB_BLK, 1), F32)
        r0, r1, acc = step(i, zero, zero, jnp.zeros((SB_BLK, LANES), F32), True)
        _, _, _, acc = lax.while_loop(
            lambda c: (c[0] >= 0) & (jnp.maximum(jnp.max(c[1]), jnp.max(c[2])) > SB_CUT),
            lambda c: (c[0] - 1,) + step(c[0], c[1], c[2], c[3], False),
            (i - 1, r0, r1, acc))
        o_ref[...] = acc.astype(BF16)

    return pl.pallas_call(
        body, name="sb_fwd", grid=(4, nq),
        in_specs=[pl.BlockSpec((SB_BLK, LANES), lambda p, i: (i, p)),
                  pl.BlockSpec((t, LANES), lambda p, i: (0, 4 + p)),
                  pl.BlockSpec((t, LANES), lambda p, i: (0, 8 + p))] + [HBM_SPEC] * nm,
        out_specs=[pl.BlockSpec((SB_BLK, LANES), lambda p, i: (i, p))] + [HBM_SPEC] * nm,
        out_shape=[jax.ShapeDtypeStruct((t, SB_WIDTH), BF16)] + [jax.ShapeDtypeStruct((4,) + s, BF16) for s in shapes],
        scratch_shapes=[pltpu.SemaphoreType.DMA((3 * nm,)), pltpu.SemaphoreType.DMA((3 * nm,)),
                        pltpu.SemaphoreType.DMA((nm,))],
        compiler_params=_cparams("arbitrary", "arbitrary"),
    )(proj, proj, proj, *shards)


def _sb_bwd(proj, dcat, t, ps):
    nq = t // SB_BLK
    nm = len(ps)

    def body(q_ref, k_ref, v_ref, do_ref, *rest):
        p_refs = rest[:nm]
        dq_ref, dk_ref, dv_ref = rest[nm:nm + 3]
        got_refs = rest[nm + 3:2 * nm + 3]
        r_scr, send_sems, recv_sems = rest[2 * nm + 3:]
        p = pl.program_id(0)
        i = pl.program_id(1)

        @pl.when((p == 0) & (i == 0))
        def _():
            sends, _ = _reduce_ici(p_refs, got_refs, send_sems, recv_sems)
            for cp in sends:
                cp.start()

        @pl.when((p == 3) & (i == nq - 1))
        def _():
            sends, recvs = _reduce_ici(p_refs, got_refs, send_sems, recv_sems)
            for cp in recvs:
                cp.wait_recv()
            for cp in sends:
                cp.wait_send()

        @pl.when(i == 0)
        def _():
            dk_ref[...] = jnp.zeros_like(dk_ref)
            dv_ref[...] = jnp.zeros_like(dv_ref)

        strict, u_strict, u_pre, lane = _sb_consts()
        qf = q_ref[...] * SB_SCALE
        dof = do_ref[...]
        hms = [(lane // 64) == hh for hh in range(2)]
        qhs = [jnp.where(hm, qf, 0.0).astype(BF16) for hm in hms]
        dohs = [jnp.where(hm, dof, 0.0).astype(BF16) for hm in hms]

        def scan(j, r0, r1, diag):
            rows = pl.ds(pl.multiple_of(j * SB_BLK, SB_BLK), SB_BLK)
            kj = k_ref[rows, :].astype(BF16)
            out = []
            for hh, r in enumerate((r0, r1)):
                l1m = -_softplus(_dot_nt(qhs[hh], kj))
                if diag:
                    l1m = jnp.where(strict, l1m, 0.0)
                r_scr[hh, j] = r
                out.append(r + jnp.sum(l1m, axis=1, keepdims=True))
            return out[0], out[1]

        zero = jnp.zeros((SB_BLK, 1), F32)
        r0, r1 = scan(i, zero, zero, True)
        jstop, _, _ = lax.while_loop(
            lambda c: (c[0] >= 0) & (jnp.maximum(jnp.max(c[1]), jnp.max(c[2])) > SB_CUT),
            lambda c: (c[0] - 1,) + scan(c[0], c[1], c[2], False),
            (i - 1, r0, r1))

        def step(j, carry, diag):
            pre_g0, pre_g1, dqa = carry
            rows = pl.ds(pl.multiple_of(j * SB_BLK, SB_BLK), SB_BLK)
            kf = k_ref[rows, :]
            kj = kf.astype(BF16)
            vj = v_ref[rows, :].astype(BF16)
            dv = jnp.zeros((SB_BLK, LANES), F32)
            dk = jnp.zeros((SB_BLK, LANES), F32)
            pre = []
            for hh, pre_g in enumerate((pre_g0, pre_g1)):
                _, lsz, w = _sb_tile(qhs[hh], kj, diag, strict, u_strict, r_scr[hh, j])
                g = w * _dot_nt(dohs[hh], vj)
                gpre = _dot(g.astype(BF16), u_pre) + pre_g
                sig = jnp.exp(lsz)
                dz = g * (1.0 - sig) - gpre * sig
                if diag:
                    dz = jnp.where(strict, dz, 0.0)
                dzb = dz.astype(BF16)
                dv = dv + _dot_tn(w.astype(BF16), dohs[hh])
                dk = dk + _dot_tn(dzb, qhs[hh])
                dqa = dqa + _dot(dzb, jnp.where(hms[hh], kf, 0.0).astype(BF16))
                pre.append(pre_g + jnp.sum(g, axis=1, keepdims=True))
            dv_ref[rows, :] += dv
            dk_ref[rows, :] += dk
            return pre[0], pre[1], dqa

        carry = lax.fori_loop(jstop + 1, i, lambda j, c: step(j, c, False), (zero, zero, jnp.zeros((SB_BLK, LANES), F32)))
        _, _, dq = step(i, carry, True)
        dq_ref[...] = dq * SB_SCALE

    return pl.pallas_call(
        body, name="sb_bwd", grid=(4, nq),
        in_specs=[pl.BlockSpec((SB_BLK, LANES), lambda p, i: (i, p)),
                  pl.BlockSpec((t, LANES), lambda p, i: (0, 4 + p)),
                  pl.BlockSpec((t, LANES), lambda p, i: (0, 8 + p)),
                  pl.BlockSpec((SB_BLK, LANES), lambda p, i: (i, p))] + [HBM_SPEC] * nm,
        out_specs=[pl.BlockSpec((SB_BLK, LANES), lambda p, i: (i, p)),
                   pl.BlockSpec((t, LANES), lambda p, i: (0, p)),
                   pl.BlockSpec((t, LANES), lambda p, i: (0, p))] + [HBM_SPEC] * nm,
        out_shape=[jax.ShapeDtypeStruct((t, SB_WIDTH), F32)] * 3
        + [jax.ShapeDtypeStruct((3,) + a.shape[1:], a.dtype) for a in ps],
        scratch_shapes=[pltpu.VMEM((2, nq, SB_BLK, 1), F32), pltpu.SemaphoreType.DMA((3 * nm,)),
                        pltpu.SemaphoreType.DMA((3 * nm,))],
        compiler_params=_cparams("arbitrary", "arbitrary"),
    )(proj, proj, proj, dcat, *ps)


def _gla_consts():
    r = lax.broadcasted_iota(jnp.int32, (CHUNK, CHUNK), 0)
    c = lax.broadcasted_iota(jnp.int32, (CHUNK, CHUNK), 1)
    causal = c <= r
    lc = causal.astype(BF16)
    lct = (c >= r).astype(BF16)
    rowid = lax.broadcasted_iota(jnp.int32, (CHUNK, 1), 0)
    lane = lax.broadcasted_iota(jnp.int32, (1, LANES), 1)
    sr = lax.broadcasted_iota(jnp.int32, (LANES, 2 * LANES), 0)
    sc = lax.broadcasted_iota(jnp.int32, (LANES, 2 * LANES), 1)
    blockdiag = (sr // 64) == (sc // LANES)
    return causal, lc, lct, rowid, lane, blockdiag


def _dot3(u, x):
    hi, mid, lo = _split3(x)
    return _dot(u, hi) + _dot(u, mid) + _dot(u, lo)


def _row_to_col(row):
    return jnp.transpose(jnp.broadcast_to(row, (LANES, LANES)))


def _gla_gates(ga_ref, gup_ref, gbias_ref):
    pre = _dot(ga_ref[...].astype(BF16), gup_ref[...].astype(BF16)) + gbias_ref[...]
    log_a = (jnp.minimum(pre, 0.0) - jnp.log(1.0 + jnp.exp(-jnp.abs(pre)))) / GLA_TAU
    return pre, log_a


def _gla_chunk_terms(g2, q2, k2, lc, rowid):
    b = _dot3(lc, g2)
    b_ref = jnp.sum(jnp.where(rowid == CHUNK // 2 - 1, b, 0.0), axis=0, keepdims=True)
    b_last = jnp.sum(jnp.where(rowid == CHUNK - 1, b, 0.0), axis=0, keepdims=True)
    qs = q2 * GLA_SCALE
    e_q = jnp.exp(b - b_ref)
    e_k = jnp.exp(b_ref - b)
    e_d = jnp.exp(b_last - b)
    e_b = jnp.exp(b)
    decay = jnp.exp(b_last)
    return dict(qs=qs, e_q=e_q, e_k=e_k, e_d=e_d, e_b=e_b, decay=decay,
                qi=qs * e_q, ki=k2 * e_k, kd=k2 * e_d, qb=qs * e_b)


def _gla_fwd(proj, gate_up_pad, gate_bias, gnorm, t, gathered, shapes):
    nsteps = t // GLA_ROWS
    cps = GLA_ROWS // CHUNK
    nm = len(gathered)

    def body(q_ref, k_ref, v_ref, gg_ref, ga_ref, gup_ref, gbias_ref, gn_ref, *rest):
        gin_refs = rest[:nm]
        o_ref, oraw_ref, st_ref = rest[nm:nm + 3]
        gout_refs = rest[nm + 3:2 * nm + 3]
        s_scr, send_sems, recv_sems = rest[2 * nm + 3:]
        i = pl.program_id(0)

        @pl.when(i == 0)
        def _():
            sends, _ = _gather_d2d(gin_refs, gout_refs, shapes, send_sems, recv_sems, 0)
            for cp in sends:
                cp.start()

        @pl.when(i == nsteps - 1)
        def _():
            sends, recvs = _gather_d2d(gin_refs, gout_refs, shapes, send_sems, recv_sems, 0)
            for cp in recvs:
                cp.wait_recv()
            for cp in sends:
                cp.wait_send()

        @pl.when(i == 0)
        def _():
            s_scr[...] = jnp.zeros_like(s_scr)

        causal, lc, _, rowid, lane, blockdiag = _gla_consts()
        _, log_a = _gla_gates(ga_ref, gup_ref, gbias_ref)
        gn = gn_ref[...]
        state = [s_scr[0], s_scr[1]]
        for cc in range(cps):
            rows = slice(cc * CHUNK, (cc + 1) * CHUNK)
            for p in range(2):
                kl = slice(p * LANES, (p + 1) * LANES)
                vl = slice(p * 2 * LANES, (p + 1) * 2 * LANES)
                tm = _gla_chunk_terms(log_a[rows, kl], q_ref[rows, kl], k_ref[rows, kl], lc, rowid)
                v2 = v_ref[rows, vl]
                v2b = v2.astype(BF16)
                s_prev = state[p]
                st_ref[cc, p] = s_prev
                kib = tm["ki"].astype(BF16)
                o_inter = _dot(tm["qb"].astype(BF16), s_prev.astype(BF16))
                outs = []
                for hh in range(2):
                    hm = (lane // 64) == hh
                    a = _dot_nt(jnp.where(hm, tm["qi"], 0.0).astype(BF16), kib)
                    a = jnp.where(causal, a, 0.0)
                    outs.append(_dot(a.astype(BF16), v2b[:, hh * LANES:(hh + 1) * LANES]))
                o2 = jnp.concatenate(outs, axis=1) + o_inter
                upd = jnp.where(blockdiag, _dot_tn(tm["kd"].astype(BF16), v2b), 0.0)
                dcol = _row_to_col(tm["decay"])
                state[p] = s_prev * jnp.concatenate([dcol, dcol], axis=1) + upd
                oraw_ref[rows, vl] = o2
                for hh in range(2):
                    hl = slice(hh * LANES, (hh + 1) * LANES)
                    oh = o2[:, hl]
                    gl = slice(p * 2 * LANES + hh * LANES, p * 2 * LANES + (hh + 1) * LANES)
                    rinv = lax.rsqrt(jnp.mean(oh * oh, axis=1, keepdims=True) + RMS_EPS)
                    gg = gg_ref[rows, gl]
                    o_ref[rows, gl] = (oh * rinv * gn * (gg * _sigmoid(gg))).astype(BF16)
        s_scr[0] = state[0]
        s_scr[1] = state[1]

    cb = lambda w, idx: pl.BlockSpec((GLA_ROWS, w), lambda i: (i, idx))
    full = lambda shp: pl.BlockSpec(shp, lambda i: tuple(0 for _ in shp))
    return pl.pallas_call(
        body, name="gla_fwd", grid=(nsteps,),
        in_specs=[cb(256, 6), cb(256, 7), cb(512, 4), cb(512, 5), cb(128, 24),
                  full((LANES, GLA_KEYS)), full((1, GLA_KEYS)), full((1, LANES))] + [HBM_SPEC] * nm,
        out_specs=[pl.BlockSpec((GLA_ROWS, GLA_WIDTH), lambda i: (i, 0)),
                   pl.BlockSpec((GLA_ROWS, GLA_WIDTH), lambda i: (i, 0)),
                   pl.BlockSpec((cps, 2, LANES, 2 * LANES), lambda i: (i, 0, 0, 0))] + [HBM_SPEC] * nm,
        out_shape=[jax.ShapeDtypeStruct((t, GLA_WIDTH), BF16), jax.ShapeDtypeStruct((t, GLA_WIDTH), F32),
                   jax.ShapeDtypeStruct((t // CHUNK, 2, LANES, 2 * LANES), F32)]
        + [jax.ShapeDtypeStruct(a.shape, a.dtype) for a in gathered],
        input_output_aliases={8 + m: 3 + m for m in range(nm)},
        scratch_shapes=[pltpu.VMEM((2, LANES, 2 * LANES), F32), pltpu.SemaphoreType.DMA((3 * nm,)),
                        pltpu.SemaphoreType.DMA((3 * nm,))],
        compiler_params=_cparams("arbitrary"),
    )(proj, proj, proj, proj, proj, gate_up_pad, gate_bias, gnorm, *gathered)


def _gla_bwd(proj, dcat, oraw, states, gate_up_pad, gate_bias, gnorm, t):
    nsteps = t // GLA_ROWS
    cps = GLA_ROWS // CHUNK
    wout = 2 * GLA_KEYS + 2 * GLA_WIDTH + LANES

    def body(q_ref, k_ref, v_ref, gg_ref, ga_ref, do_ref, oraw_ref, st_ref, gup_ref, gbias_ref, gn_ref,
             d_ref, dgup_ref, dgbias_ref, dgn_ref, ds_scr):
        i = pl.program_id(0)

        @pl.when(i == 0)
        def _():
            ds_scr[...] = jnp.zeros_like(ds_scr)
            dgup_ref[...] = jnp.zeros_like(dgup_ref)
            dgbias_ref[...] = jnp.zeros_like(dgbias_ref)
            dgn_ref[...] = jnp.zeros_like(dgn_ref)

        causal, lc, lct, rowid, lane, blockdiag = _gla_consts()
        pre, log_a = _gla_gates(ga_ref, gup_ref, gbias_ref)
        gn = gn_ref[...]
        dgn = jnp.zeros((1, LANES), F32)
        dstate = [ds_scr[0], ds_scr[1]]
        dgs = [[None, None] for _ in range(cps)]
        for cc in reversed(range(cps)):
            rows = slice(cc * CHUNK, (cc + 1) * CHUNK)
            for p in range(2):
                kl = slice(p * LANES, (p + 1) * LANES)
                vl = slice(p * 2 * LANES, (p + 1) * 2 * LANES)
                tm = _gla_chunk_terms(log_a[rows, kl], q_ref[rows, kl], k_ref[rows, kl], lc, rowid)
                v2b = v_ref[rows, vl].astype(BF16)
                s_prev = st_ref[cc, p]
                ds2 = dstate[p]
                dos = []
                for hh in range(2):
                    gl = slice(p * 2 * LANES + hh * LANES, p * 2 * LANES + (hh + 1) * LANES)
                    oh = oraw_ref[rows, gl]
                    rinv = lax.rsqrt(jnp.mean(oh * oh, axis=1, keepdims=True) + RMS_EPS)
                    on = oh * rinv
                    gg = gg_ref[rows, gl]
                    sg = _sigmoid(gg)
                    sil = gg * sg
                    dgo = do_ref[rows, gl]
                    d_ref[rows, 2 * GLA_KEYS + GLA_WIDTH + gl.start:2 * GLA_KEYS + GLA_WIDTH + gl.stop] = (
                        dgo * on * gn * (sg * (1.0 + gg * (1.0 - sg))))
                    dgn = dgn + jnp.sum(dgo * sil * on, axis=0, keepdims=True)
                    don = dgo * sil * gn
                    dos.append(rinv * (don - on * jnp.mean(don * on, axis=1, keepdims=True)))
                do2b = jnp.concatenate(dos, axis=1).astype(BF16)
                qib = tm["qi"].astype(BF16)
                kib = tm["ki"].astype(BF16)
                kdb = tm["kd"].astype(BF16)
                qbb = tm["qb"].astype(BF16)
                ds2b = ds2.astype(BF16)
                dqi = jnp.zeros((CHUNK, LANES), F32)
                dki = jnp.zeros((CHUNK, LANES), F32)
                dvs = []
                for hh in range(2):
                    hm = (lane // 64) == hh
                    hl = slice(hh * LANES, (hh + 1) * LANES)
                    a = jnp.where(causal, _dot_nt(jnp.where(hm, tm["qi"], 0.0).astype(BF16), kib), 0.0).astype(BF16)
                    da = jnp.where(causal, _dot_nt(do2b[:, hl], v2b[:, hl]), 0.0).astype(BF16)
                    dvs.append(_dot_tn(a, do2b[:, hl]))
                    dqi = dqi + jnp.where(hm, _dot(da, kib), 0.0)
                    dki = dki + jnp.where(hm, _dot_tn(da, qib), 0.0)
                dv2 = jnp.concatenate(dvs, axis=1) + _dot(kdb, ds2b)
                dqb = _dot_nt(do2b, s_prev.astype(BF16))
                dkd = _dot_nt(v2b, ds2b)
                dcol = _row_to_col(tm["decay"])
                dsp = jnp.where(blockdiag, _dot_tn(qbb, do2b), 0.0) + ds2 * jnp.concatenate([dcol, dcol], axis=1)
                ddecay_col = jnp.sum(ds2 * s_prev, axis=1, keepdims=True)
                ddecay_row = jnp.transpose(jnp.broadcast_to(ddecay_col, (LANES, LANES)))[0:1, :]
                dstate[p] = dsp
                dqs = dqi * tm["e_q"] + dqb * tm["e_b"]
                dk = dki * tm["e_k"] + dkd * tm["e_d"]
                t_qi = dqi * tm["qi"]
                t_ki = dki * tm["ki"]
                t_kd = dkd * tm["kd"]
                db = t_qi - t_ki + dqb * tm["qb"] - t_kd
                db_ref = jnp.sum(t_ki - t_qi, axis=0, keepdims=True)
                db_last = jnp.sum(t_kd, axis=0, keepdims=True) + ddecay_row * tm["decay"]
                db = db + jnp.where(rowid == CHUNK // 2 - 1, db_ref, 0.0) + jnp.where(rowid == CHUNK - 1, db_last, 0.0)
                dg = _dot3(lct, db)
                d_ref[rows, p * LANES:(p + 1) * LANES] = dqs * GLA_SCALE
                d_ref[rows, GLA_KEYS + p * LANES:GLA_KEYS + (p + 1) * LANES] = dk
                d_ref[rows, 2 * GLA_KEYS + p * 2 * LANES:2 * GLA_KEYS + (p + 1) * 2 * LANES] = dv2
                dgs[cc][p] = dg
        ds_scr[0] = dstate[0]
        ds_scr[1] = dstate[1]
        dlog_a = jnp.concatenate([jnp.concatenate(row, axis=1) for row in dgs], axis=0)
        dpre = dlog_a * (1.0 / GLA_TAU) * _sigmoid(-pre)
        dpb = dpre.astype(BF16)
        dgn_ref[...] += dgn
        dgbias_ref[...] += jnp.sum(dpre, axis=0, keepdims=True)
        dgup_ref[...] += _dot_tn(ga_ref[...].astype(BF16), dpb)
        d_ref[:, 2 * GLA_KEYS + 2 * GLA_WIDTH:] = _dot_nt(dpb, gup_ref[...].astype(BF16))

    rev = lambda i: nsteps - 1 - i
    cb = lambda w, idx: pl.BlockSpec((GLA_ROWS, w), lambda i: (rev(i), idx))
    full = lambda shp: pl.BlockSpec(shp, lambda i: tuple(0 for _ in shp))
    return pl.pallas_call(
        body, name="gla_bwd", grid=(nsteps,),
        in_specs=[cb(256, 6), cb(256, 7), cb(512, 4), cb(512, 5), cb(128, 24), cb(512, 1), cb(512, 0),
                  pl.BlockSpec((cps, 2, LANES, 2 * LANES), lambda i: (rev(i), 0, 0, 0)),
                  full((LANES, GLA_KEYS)), full((1, GLA_KEYS)), full((1, LANES))],
        out_specs=[pl.BlockSpec((GLA_ROWS, wout), lambda i: (rev(i), 0)),
                   full((LANES, GLA_KEYS)), full((1, GLA_KEYS)), full((1, LANES))],
        out_shape=[jax.ShapeDtypeStruct((t, wout), F32), jax.ShapeDtypeStruct((LANES, GLA_KEYS), F32),
                   jax.ShapeDtypeStruct((1, GLA_KEYS), F32), jax.ShapeDtypeStruct((1, LANES), F32)],
        scratch_shapes=[pltpu.VMEM((2, LANES, 2 * LANES), F32)],
        compiler_params=_cparams("arbitrary"),
    )(proj, proj, proj, proj, proj, dcat, oraw, states, gate_up_pad, gate_bias, gnorm)


def _ln_stats(r):
    mu = jnp.mean(r, axis=1, keepdims=True)
    xc = r - mu
    rstd = lax.rsqrt(jnp.mean(xc * xc, axis=1, keepdims=True) + LN_EPS)
    return xc * rstd, rstd


def _ln_bwd(dy_g, xhat, rstd):
    return rstd * (dy_g - jnp.mean(dy_g, axis=1, keepdims=True) - xhat * jnp.mean(dy_g * xhat, axis=1, keepdims=True))


def _outproj_ln1(sb_o, gla_o, x, w_out, g1, b1, t, tm=256):
    def body(sb_ref, gl_ref, x_ref, w_ref, g_ref, b_ref, xhat_ref, rstd_ref, h_ref):
        mix = _dot(sb_ref[...], w_ref[0:SB_WIDTH, :]) + _dot(gl_ref[...], w_ref[SB_WIDTH:, :])
        xhat, rstd = _ln_stats(ALPHA * x_ref[...] + mix)
        xhat_ref[...] = xhat
        rstd_ref[...] = rstd
        h_ref[...] = (xhat * g_ref[...] + b_ref[...]).astype(BF16)

    row = lambda w: pl.BlockSpec((tm, w), lambda i: (i, 0))
    full = lambda shp: pl.BlockSpec(shp, lambda i: (0, 0))
    return pl.pallas_call(
        body, name="outproj_ln1", grid=(t // tm,),
        in_specs=[row(SB_WIDTH), row(GLA_WIDTH), row(D_MODEL), full((D_MODEL, D_MODEL)), full((1, D_MODEL)), full((1, D_MODEL))],
        out_specs=[row(D_MODEL), row(1), row(D_MODEL)],
        out_shape=[jax.ShapeDtypeStruct((t, D_MODEL), F32), jax.ShapeDtypeStruct((t, 1), F32),
                   jax.ShapeDtypeStruct((t, D_MODEL), BF16)],
        compiler_params=_cparams("parallel"),
    )(sb_o, gla_o, x, w_out, g1, b1)


_INV_SQRT2 = 1.0 / math.sqrt(2.0)
_INV_SQRT2PI = 1.0 / math.sqrt(2.0 * math.pi)


def _conv3(xs, w_ref, b_ref, half):
    return (w_ref[half, 0:1, :] * pltpu.roll(xs, 2, 0) + w_ref[half, 1:2, :] * pltpu.roll(xs, 1, 0)
            + w_ref[half, 2:3, :] * xs + b_ref[half])


HALO = 16


def _conv_gelu_fwd(up3, conv_w3, conv_b3, t, tr=512, ct=256):
    nct = D_FF // ct
    hb = tr // HALO

    def body(cur_ref, prev_ref, w_ref, b_ref, gm_ref):
        i = pl.program_id(1)
        keep = (i > 0).astype(F32)
        us = []
        for half in range(2):
            xs = jnp.concatenate([prev_ref[half].astype(F32) * keep, cur_ref[half].astype(F32)], axis=0)
            us.append(_conv3(xs, w_ref, b_ref, half)[HALO:, :])
        a, c = us
        gelu = 0.5 * a * (1.0 + lax.erf(a * _INV_SQRT2))
        gm_ref[...] = (gelu * c).astype(BF16)

    return pl.pallas_call(
        body, name="conv_gelu_fwd", grid=(nct, t // tr),
        in_specs=[pl.BlockSpec((2, tr, ct), lambda j, i: (0, i, j)),
                  pl.BlockSpec((2, HALO, ct), lambda j, i: (0, jnp.maximum(i * hb - 1, 0), j)),
                  pl.BlockSpec((2, 3, ct), lambda j, i: (0, 0, j)),
                  pl.BlockSpec((2, 1, ct), lambda j, i: (0, 0, j))],
        out_specs=pl.BlockSpec((tr, ct), lambda j, i: (i, j)),
        out_shape=jax.ShapeDtypeStruct((t, D_FF), BF16),
        compiler_params=_cparams("parallel", "parallel"),
    )(up3, up3, conv_w3, conv_b3)


def _conv_gelu_bwd(up3, dgm, conv_w3, conv_b3, t, tr=512, ct=256):
    nct = D_FF // ct
    nrt = t // tr
    hb = tr // HALO
    n = tr + 2 * HALO
    lo, hi = HALO, tr + HALO

    def body(cur_ref, prev_ref, next_ref, dg_ref, dgn_ref, w_ref, b_ref, dup_ref, dcw_ref, dcb_ref):
        i = pl.program_id(1)

        @pl.when(i == 0)
        def _():
            dcw_ref[...] = jnp.zeros_like(dcw_ref)
            dcb_ref[...] = jnp.zeros_like(dcb_ref)

        keep_prev = (i > 0).astype(F32)
        keep_next = (i < nrt - 1).astype(F32)
        xs, xm1, xm2, us = [], [], [], []
        for half in range(2):
            x = jnp.concatenate([prev_ref[half].astype(F32) * keep_prev, cur_ref[half].astype(F32),
                                 next_ref[half].astype(F32)], axis=0)
            xs.append(x)
            xm1.append(pltpu.roll(x, 1, 0))
            xm2.append(pltpu.roll(x, 2, 0))
            us.append(w_ref[half, 0:1, :] * xm2[half] + w_ref[half, 1:2, :] * xm1[half]
                      + w_ref[half, 2:3, :] * x + b_ref[half])
        a, c = us
        dg = jnp.concatenate([jnp.zeros((HALO, ct), F32), dg_ref[...].astype(F32),
                              dgn_ref[...].astype(F32) * keep_next], axis=0)
        cdf = 0.5 * (1.0 + lax.erf(a * _INV_SQRT2))
        pdf = jnp.exp(-0.5 * a * a) * _INV_SQRT2PI
        dus = [dg * c * (cdf + a * pdf), dg * (a * cdf)]
        rid = lax.broadcasted_iota(jnp.int32, (8, 1), 0)
        for half in range(2):
            du = dus[half]
            dup = (w_ref[half, 2:3, :] * du + w_ref[half, 1:2, :] * pltpu.roll(du, n - 1, 0)
                   + w_ref[half, 0:1, :] * pltpu.roll(du, n - 2, 0))
            dup_ref[half] = dup[lo:hi, :].astype(BF16)
            duc = du[lo:hi, :]
            s0 = jnp.sum(duc * xm2[half][lo:hi, :], axis=0, keepdims=True)
            s1 = jnp.sum(duc * xm1[half][lo:hi, :], axis=0, keepdims=True)
            s2 = jnp.sum(duc * xs[half][lo:hi, :], axis=0, keepdims=True)
            dcw_ref[half] += jnp.where(rid == 0, s0, jnp.where(rid == 1, s1, jnp.where(rid == 2, s2, 0.0)))
            dcb_ref[half] += jnp.sum(duc, axis=0, keepdims=True)

    last = t // HALO - 1
    return pl.pallas_call(
        body, name="conv_gelu_bwd", grid=(nct, nrt),
        in_specs=[pl.BlockSpec((2, tr, ct), lambda j, i: (0, i, j)),
                  pl.BlockSpec((2, HALO, ct), lambda j, i: (0, jnp.maximum(i * hb - 1, 0), j)),
                  pl.BlockSpec((2, HALO, ct), lambda j, i: (0, jnp.minimum((i + 1) * hb, last), j)),
                  pl.BlockSpec((tr, ct), lambda j, i: (i, j)),
                  pl.BlockSpec((HALO, ct), lambda j, i: (jnp.minimum((i + 1) * hb, last), j)),
                  pl.BlockSpec((2, 3, ct), lambda j, i: (0, 0, j)),
                  pl.BlockSpec((2, 1, ct), lambda j, i: (0, 0, j))],
        out_specs=[pl.BlockSpec((2, tr, ct), lambda j, i: (0, i, j)),
                   pl.BlockSpec((2, 8, ct), lambda j, i: (0, 0, j)),
                   pl.BlockSpec((2, 1, ct), lambda j, i: (0, 0, j))],
        out_shape=[jax.ShapeDtypeStruct((2, t, D_FF), BF16), jax.ShapeDtypeStruct((2, 8, D_FF), F32),
                   jax.ShapeDtypeStruct((2, 1, D_FF), F32)],
        compiler_params=_cparams("parallel", "arbitrary"),
    )(up3, up3, up3, dgm, dgm, conv_w3, conv_b3)


def _down_ln2_loss(gm, w_down, xhat1, g1, b1, g2, b2, target, t, tm=256):
    def body(gm_ref, w_ref, xh_ref, g1_ref, b1_ref, g2_ref, b2_ref, tg_ref, dr_ref, loss_ref, dg_ref, db_ref):
        i = pl.program_id(0)

        @pl.when(i == 0)
        def _():
            loss_ref[...] = jnp.zeros_like(loss_ref)
            dg_ref[...] = jnp.zeros_like(dg_ref)
            db_ref[...] = jnp.zeros_like(db_ref)

        h = xh_ref[...] * g1_ref[...] + b1_ref[...]
        xhat, rstd = _ln_stats(ALPHA * h + _dot(gm_ref[...], w_ref[...]))
        err = xhat * g2_ref[...] + b2_ref[...] - tg_ref[...]
        loss_ref[...] += 0.5 * jnp.sum(jnp.sum(err * err, axis=1, keepdims=True), axis=0, keepdims=True) / D_MODEL
        dy = err * (1.0 / D_MODEL)
        dg_ref[...] += jnp.sum(dy * xhat, axis=0, keepdims=True)
        db_ref[...] += jnp.sum(dy, axis=0, keepdims=True)
        dr_ref[...] = _ln_bwd(dy * g2_ref[...], xhat, rstd)

    row = lambda w: pl.BlockSpec((tm, w), lambda i: (i, 0))
    full = lambda shp: pl.BlockSpec(shp, lambda i: (0, 0))
    vec = full((1, D_MODEL))
    return pl.pallas_call(
        body, name="down_ln2_loss", grid=(t // tm,),
        in_specs=[row(D_FF), full((D_FF, D_MODEL)), row(D_MODEL), vec, vec, vec, vec, row(D_MODEL)],
        out_specs=[row(D_MODEL), full((1, 1)), vec, vec],
        out_shape=[jax.ShapeDtypeStruct((t, D_MODEL), F32), jax.ShapeDtypeStruct((1, 1), F32),
                   jax.ShapeDtypeStruct((1, D_MODEL), F32), jax.ShapeDtypeStruct((1, D_MODEL), F32)],
        compiler_params=_cparams("arbitrary"),
    )(gm, w_down, xhat1, g1, b1, g2, b2, target)


def _dh_ln1_bwd(dup3, w_up4, dr2, xhat1, rstd1, g1, t, tm=512):
    tk = 2 * D_FF // 4
    nkh = D_FF // tk
    nk = 2 * nkh

    def body(a_ref, w_ref, dr2_ref, xh_ref, rs_ref, g_ref, dr1_ref, dg_ref, db_ref, acc_ref):
        i = pl.program_id(0)
        kk = pl.program_id(1)

        @pl.when((i == 0) & (kk == 0))
        def _():
            dg_ref[...] = jnp.zeros_like(dg_ref)
            db_ref[...] = jnp.zeros_like(db_ref)

        part = _dot_nt(a_ref[...], w_ref[...])

        @pl.when(kk == 0)
        def _():
            acc_ref[...] = part

        @pl.when(kk > 0)
        def _():
            acc_ref[...] += part

        @pl.when(kk == nk - 1)
        def _():
            dh = acc_ref[...] + ALPHA * dr2_ref[...]
            xhat = xh_ref[...]
            dg_ref[...] += jnp.sum(dh * xhat, axis=0, keepdims=True)
            db_ref[...] += jnp.sum(dh, axis=0, keepdims=True)
            dr1_ref[...] = _ln_bwd(dh * g_ref[...], xhat, rs_ref[...])

    row = lambda w: pl.BlockSpec((tm, w), lambda i, kk: (i, 0))
    vec = pl.BlockSpec((1, D_MODEL), lambda i, kk: (0, 0))
    return pl.pallas_call(
        body, name="dh_ln1_bwd", grid=(t // tm, nk),
        in_specs=[pl.BlockSpec((None, tm, tk), lambda i, kk: (kk // nkh, i, kk % nkh)),
                  pl.BlockSpec((None, D_MODEL, tk), lambda i, kk: (kk, 0, 0)),
                  row(D_MODEL), row(D_MODEL), row(1), vec],
        out_specs=[row(D_MODEL), vec, vec],
        out_shape=[jax.ShapeDtypeStruct((t, D_MODEL), F32), jax.ShapeDtypeStruct((1, D_MODEL), F32),
                   jax.ShapeDtypeStruct((1, D_MODEL), F32)],
        scratch_shapes=[pltpu.VMEM((tm, D_MODEL), F32)],
        compiler_params=_cparams("arbitrary", "arbitrary"),
    )(dup3, w_up4, dr2, xhat1, rstd1, g1)


def _adamw(w, g, m, v, name):
    rows, cols = w.shape
    tr = rows
    for cand in (256, 128, 64, 32, 16, 8):
        if rows % cand == 0 and rows > cand:
            tr = cand
            break
    c1 = 1.0 / (1.0 - ADAM_B1 ** ADAM_STEP)
    c2 = 1.0 / (1.0 - ADAM_B2 ** ADAM_STEP)

    def body(w_ref, g_ref, m_ref, v_ref, d_ref, nm_ref, nv_ref):
        gv = g_ref[...]
        nm = ADAM_B1 * m_ref[...] + (1.0 - ADAM_B1) * gv
        nv = ADAM_B2 * v_ref[...] + (1.0 - ADAM_B2) * (gv * gv)
        d_ref[...] = -ADAM_LR * ((nm * c1) / (jnp.sqrt(nv * c2) + ADAM_EPS) + ADAM_WD * w_ref[...])
        nm_ref[...] = nm
        nv_ref[...] = nv

    spec = pl.BlockSpec((tr, cols), lambda i: (i, 0))
    out = jax.ShapeDtypeStruct((rows, cols), F32)
    return pl.pallas_call(
        body, name=name, grid=(rows // tr,), in_specs=[spec] * 4, out_specs=[spec] * 3, out_shape=[out] * 3,
        compiler_params=_cparams("parallel"),
    )(w, g, m, v)


def _local_step(x, target, w_in_p, late_shards, gate_up_pad, gate_bias, gnorm, ln1_g, ln1_b, conv_w3, conv_b3,
                ln2_g, ln2_b, c_arr, kc_arr):
    t = x.shape[0]
    tq = min(t, 1024)
    proj, out_partly, down_partly = _mm(x, w_in_p, m=t, n=IN_PAD, k=D_MODEL, tm=tq, tn=640, tk=D_MODEL, name="proj",
                                        gather=(late_shards[1:], LATE_SHAPES[1:]))
    sb_o, up_partly = _sb_fwd(proj, t, late_shards[:1], LATE_SHAPES[:1])
    gla_o, oraw, states, w_up4, w_out4, w_down4 = _gla_fwd(proj, gate_up_pad, gate_bias, gnorm, t,
                                                           [up_partly, out_partly, down_partly], LATE_SHAPES)
    w_out = w_out4.reshape(D_MODEL, D_MODEL)
    w_down = w_down4.reshape(D_FF, D_MODEL)
    xhat1, rstd1, h_bf = _outproj_ln1(sb_o, gla_o, x, w_out, ln1_g, ln1_b, t)
    up3 = _mm(h_bf, w_up4, m=t, n=2 * D_FF, k=D_MODEL, tm=tq, tn=W_UP_S, tk=D_MODEL, name="up",
              b_spec=pl.BlockSpec((None, D_MODEL, W_UP_S), lambda i, j, kk: (j, 0, 0)),
              o_spec=pl.BlockSpec((None, tq, W_UP_S), lambda i, j, kk: (j // 2, i, j % 2)),
              out_shape=jax.ShapeDtypeStruct((2, t, D_FF), BF16), out_dtype=BF16)
    gm = _conv_gelu_fwd(up3, conv_w3, conv_b3, t, tr=tq)
    dr2, loss, dln2_g, dln2_b = _down_ln2_loss(gm, w_down, xhat1, ln1_g, ln1_b, ln2_g, ln2_b, target, t)
    dgm = _mm(dr2, w_down, m=t, n=D_FF, k=D_MODEL, tm=tq, tn=W_UP_S, tk=D_MODEL, tb=True, out_dtype=BF16, name="dgm")
    dw_down = _mm(gm, dr2, m=D_FF, n=D_MODEL, k=t, tm=W_UP_S, tn=D_MODEL, tk=tq, ta=True, name="dw_down")
    dup3, dcw, dcb = _conv_gelu_bwd(up3, dgm, conv_w3, conv_b3, t)
    dr1, dln1_g, dln1_b = _dh_ln1_bwd(dup3, w_up4, dr2, xhat1, rstd1, ln1_g, t)
    dw_up4 = _mm(h_bf, dup3, m=D_MODEL, n=2 * D_FF, k=t, tm=512, tn=W_UP_S, tk=t, ta=True, name="dw_up",
                 b_spec=pl.BlockSpec((None, t, W_UP_S), lambda i, j, kk: (j // 2, kk, j % 2)),
                 o_spec=pl.BlockSpec((None, 512, W_UP_S), lambda i, j, kk: (j, i, 0)),
                 out_shape=jax.ShapeDtypeStruct((4, D_MODEL, W_UP_S), F32))
    dcat = _mm(dr1, w_out, m=t, n=D_MODEL, k=D_MODEL, tm=tq, tn=512, tk=D_MODEL, tb=True, name="dcat")
    dw_out_sb = _mm(sb_o, dr1, m=SB_WIDTH, n=D_MODEL, k=t, tm=512, tn=D_MODEL, tk=tq, ta=True, name="dw_out_sb")
    dw_out_gla = _mm(gla_o, dr1, m=GLA_WIDTH, n=D_MODEL, k=t, tm=512, tn=D_MODEL, tk=tq, ta=True, name="dw_out_gla")
    gs = [dw_up4, jnp.concatenate([dw_out_sb, dw_out_gla], axis=0).reshape(4, W_OUT_S, D_MODEL),
          dw_down.reshape(4, W_DOWN_S, D_MODEL)]
    from_sib = _exchange_sibling(gs, LATE_SHAPES, "exchange_sibling_late")
    ps = [_add_sibling(gs[m], from_sib[m], c_arr, LATE_ADD_ROWS[m], "add_sibling_late_%d" % m) for m in range(3)]
    dsq, dsk, dsv, *others = _sb_bwd(proj, dcat, t, ps)
    late_sums = [_add_chips(ps[m], others[m], kc_arr, LATE_ADD_ROWS[m], "add_chips_late_%d" % m) for m in range(3)]
    dgla, dgup_pad, dgbias, dgnorm = _gla_bwd(proj, dcat, oraw, states, gate_up_pad, gate_bias, gnorm, t)
    dproj = jnp.concatenate([dsq, dsk, dsv, dgla], axis=1).astype(BF16)
    dx = _mm(dproj, w_in_p, m=t, n=D_MODEL, k=IN_PAD, tm=512, tn=D_MODEL, tk=IN_PAD, tb=True, add=dr1, add_scale=ALPHA, name="dx")
    dw_in_p = _mm(x, dproj, m=D_MODEL, n=IN_PAD, k=t, tm=512, tn=IN_PAD, tk=tq, ta=True, name="dw_in")
    small = dict(
        gate_up=dgup_pad[:GATE_RANK], gate_bias=dgbias, gla_norm_g=dgnorm, ln1_g=dln1_g, ln1_b=dln1_b,
        conv_w=jnp.concatenate([dcw[0, :3], dcw[1, :3]], axis=1), conv_b=jnp.concatenate([dcb[0], dcb[1]], axis=1),
        ln2_g=dln2_g, ln2_b=dln2_b)
    return loss, dx, late_sums, dw_in_p, small


W_IN_S, W_UP_S, W_OUT_S, W_DOWN_S = IN_WIDTH // 4, 2 * D_FF // 4, D_MODEL // 4, D_FF // 4
SHARD_SHAPES = ((D_MODEL, W_IN_S), (D_MODEL, W_UP_S), (W_OUT_S, D_MODEL), (W_DOWN_S, D_MODEL))
ADD_ROWS = (256, 256, 128, 176)
LATE_SHAPES, LATE_ADD_ROWS = SHARD_SHAPES[1:], ADD_ROWS[1:]
SMALL_ROWS = 8
VEC_SIZES = (("gate_bias", GLA_KEYS), ("gla_norm_g", LANES), ("ln1_g", D_MODEL), ("ln1_b", D_MODEL),
             ("conv_b", 2 * D_FF), ("ln2_g", D_MODEL), ("ln2_b", D_MODEL))
SMALL_GRADS = VEC_SIZES + (("conv_w", 3 * 2 * D_FF), ("gate_up", GATE_RANK * GLA_KEYS), ("loss", 1))


def _rows(a):
    flat = a.reshape(-1)
    pad = (-flat.shape[0]) % D_MODEL
    if pad:
        flat = jnp.pad(flat, (0, pad))
    return flat.reshape(-1, D_MODEL)


def _pad_rows(a, rows):
    return jnp.pad(a, ((0, rows - a.shape[0]), (0, 0)))


def _pack_vec(d, sizes, rows):
    flat = jnp.concatenate([d[n].reshape(-1) for n, _ in sizes])
    return _pad_rows(_rows(flat), rows)


def _unpack_vec(v, sizes):
    flat = v.reshape(-1)
    out, o = {}, 0
    for n, size in sizes:
        out[n] = flat[o:o + size].reshape(1, size)
        o += size
    return out


VEC_ROWS = 16
GRAD_VEC_ROWS = 32


HBM_SPEC = pl.BlockSpec(memory_space=pltpu.HBM)


def _position():
    x, y, c = lax.axis_index("x"), lax.axis_index("y"), lax.axis_index("c")
    chips = [(1 - x, y), (x, 1 - y), (1 - x, 1 - y)]
    return x, y, c, chips


def _remote(src, dst, send_sems, recv_sems, k, to):
    return pltpu.make_async_remote_copy(src_ref=src, dst_ref=dst, send_sem=send_sems.at[k], recv_sem=recv_sems.at[k],
                                        device_id=to, device_id_type=MESH)


def _gather_ici(in_refs, out_refs, shapes, send_sems, recv_sems, local_sems):
    x, y, c, chips = _position()
    k_me = 2 * x + y
    local, sends, recvs = [], [], []
    for m, (rows, _) in enumerate(shapes):
        h = rows // 2
        local.append(pltpu.make_async_copy(in_refs[m], out_refs[m].at[k_me], local_sems.at[m]))
        for j, (cx, cy) in enumerate(chips):
            sends.append(_remote(in_refs[m].at[pl.ds(c * h, h), :], out_refs[m].at[k_me, pl.ds(c * h, h), :],
                                 send_sems, recv_sems, 3 * m + j, (cx, cy, c)))
            landed = out_refs[m].at[2 * cx + cy, pl.ds(c * h, h), :]
            recvs.append(_remote(landed, landed, send_sems, recv_sems, 3 * m + j, (x, y, c)))
    return local, sends, recvs


def _gather_d2d(src_refs, dst_refs, shapes, send_sems, recv_sems, base):
    x, y, c, chips = _position()
    sends, recvs = [], []
    for m, (rows, _) in enumerate(shapes):
        h = rows // 2
        for j, (cx, cy) in enumerate(chips):
            k = 2 * cx + cy
            sends.append(_remote(src_refs[m].at[k, pl.ds(c * h, h), :], dst_refs[m].at[k, pl.ds(c * h, h), :],
                                 send_sems, recv_sems, base + 3 * m + j, (x, y, 1 - c)))
            landed = dst_refs[m].at[k, pl.ds((1 - c) * h, h), :]
            recvs.append(_remote(landed, landed, send_sems, recv_sems, base + 3 * m + j, (x, y, c)))
    return sends, recvs


def _gather_weights(shards, small, shapes):
    nm = len(shards)
    n_ici = 3 * nm

    def body(*refs):
        in_refs, small_ref = refs[:nm], refs[nm]
        out_refs, osm_ref = refs[nm + 1:2 * nm + 1], refs[2 * nm + 1]
        send_sems, recv_sems, local_sems = refs[2 * nm + 2:]
        x, y, c, chips = _position()
        k_me = 2 * x + y
        local, sends, recvs = _gather_ici(in_refs, out_refs, shapes, send_sems, recv_sems, local_sems)
        local.append(pltpu.make_async_copy(small_ref, osm_ref.at[k_me], local_sems.at[nm]))
        for j, (cx, cy) in enumerate(chips):
            sends.append(_remote(small_ref, osm_ref.at[k_me], send_sems, recv_sems, n_ici + j, (cx, cy, c)))
        for cp in local + sends:
            cp.start()
        for cp in recvs:
            cp.wait_recv()
        fsends, frecvs = _gather_d2d(out_refs, out_refs, shapes, send_sems, recv_sems, n_ici + 3)
        for cp in fsends:
            cp.start()
        for j, (cx, cy) in enumerate(chips):
            k = 2 * cx + cy
            frecvs.append(_remote(osm_ref.at[k], osm_ref.at[k], send_sems, recv_sems, n_ici + j, (x, y, c)))
        for cp in frecvs:
            cp.wait_recv()
        for cp in sends + fsends:
            cp.wait_send()
        for cp in local:
            cp.wait()

    n_sems = 2 * n_ici + 3
    return pl.pallas_call(
        body, name="gather_weights", in_specs=[HBM_SPEC] * (nm + 1), out_specs=[HBM_SPEC] * (nm + 1),
        out_shape=[jax.ShapeDtypeStruct((4,) + s, BF16) for s in shapes]
        + [jax.ShapeDtypeStruct((4, SMALL_ROWS, D_MODEL), F32)],
        scratch_shapes=[pltpu.SemaphoreType.DMA((n_sems,)), pltpu.SemaphoreType.DMA((n_sems,)),
                        pltpu.SemaphoreType.DMA((nm + 1,))],
    )(*shards, small)


def _exchange_sibling(gs, shapes, name):
    nm = len(gs)

    def body(*refs):
        g_refs, r_refs = refs[:nm], refs[nm:2 * nm]
        send_sems, recv_sems = refs[2 * nm:]
        x, y, c, _ = _position()
        copies = []
        for m, (rows, _) in enumerate(shapes):
            h = rows // 2
            for k in range(4):
                copies.append(_remote(g_refs[m].at[k, pl.ds((1 - c) * h, h), :], r_refs[m].at[k], send_sems, recv_sems,
                                      4 * m + k, (x, y, 1 - c)))
        for cp in copies:
            cp.start()
        for cp in copies:
            cp.wait()

    return pl.pallas_call(
        body, name=name, in_specs=[HBM_SPEC] * nm, out_specs=[HBM_SPEC] * nm,
        out_shape=[jax.ShapeDtypeStruct((4, r // 2, cl), F32) for r, cl in shapes],
        scratch_shapes=[pltpu.SemaphoreType.DMA((4 * nm,)), pltpu.SemaphoreType.DMA((4 * nm,))],
    )(*gs)


def _add_sibling(g, r, c_arr, tr, name):
    _, rows, cols = g.shape
    nb = rows // 2 // tr

    def body(c_ref, g_ref, r_ref, o_ref):
        o_ref[...] = (g_ref[...] + r_ref[...]).astype(BF16)

    spec = pl.BlockSpec((None, tr, cols), lambda k, i, c: (k, i, 0))
    return pl.pallas_call(
        body, name=name,
        grid_spec=pltpu.PrefetchScalarGridSpec(
            num_scalar_prefetch=1, grid=(4, nb),
            in_specs=[pl.BlockSpec((None, tr, cols), lambda k, i, c: (k, c[0] * nb + i, 0)), spec], out_specs=spec),
        out_shape=jax.ShapeDtypeStruct((4, rows // 2, cols), BF16), compiler_params=_cparams("parallel", "parallel"),
    )(c_arr, g, r)


def _reduce_ici(p_refs, r_refs, send_sems, recv_sems):
    x, y, c, chips = _position()
    sends, recvs = [], []
    for m in range(len(p_refs)):
        for j, (cx, cy) in enumerate(chips):
            sends.append(_remote(p_refs[m].at[2 * cx + cy], r_refs[m].at[j], send_sems, recv_sems, 3 * m + j, (cx, cy, c)))
            recvs.append(_remote(r_refs[m].at[j], r_refs[m].at[j], send_sems, recv_sems, 3 * m + j, (x, y, c)))
    return sends, recvs


def _exchange_chips(ps, vec):
    nm = len(ps)
    n_ici = 3 * nm

    def body(*refs):
        p_refs, vec_ref = refs[:nm], refs[nm]
        r_refs, vrecv_ref = refs[nm + 1:2 * nm + 1], refs[2 * nm + 1]
        send_sems, recv_sems, local_sem = refs[2 * nm + 2:]
        x, y, c, _ = _position()
        my_id = 4 * x + 2 * y + c
        vloc = pltpu.make_async_copy(vec_ref, vrecv_ref.at[my_id], local_sem.at[0])
        vloc.start()
        sends, recvs = _reduce_ici(p_refs, r_refs, send_sems, recv_sems)
        for r in range(1, 8):
            peer = (1 - x if r & 4 else x, 1 - y if r & 2 else y, 1 - c if r & 1 else c)
            sends.append(_remote(vec_ref, vrecv_ref.at[my_id], send_sems, recv_sems, n_ici + r - 1, peer))
            recvs.append(_remote(vec_ref, vrecv_ref.at[0], send_sems, recv_sems, n_ici + r - 1, (x, y, c)))
        for cp in sends:
            cp.start()
        for cp in recvs:
            cp.wait_recv()
        for cp in sends:
            cp.wait_send()
        vloc.wait()

    n_sems = n_ici + 7
    return pl.pallas_call(
        body, name="exchange_chips", in_specs=[HBM_SPEC] * (nm + 1), out_specs=[HBM_SPEC] * (nm + 1),
        out_shape=[jax.ShapeDtypeStruct((3,) + p.shape[1:], p.dtype) for p in ps]
        + [jax.ShapeDtypeStruct((8, GRAD_VEC_ROWS, D_MODEL), F32)],
        scratch_shapes=[pltpu.SemaphoreType.DMA((n_sems,)), pltpu.SemaphoreType.DMA((n_sems,)), pltpu.SemaphoreType.DMA((1,))],
    )(*ps, vec)


def _add_chips(p, r, kc_arr, tr, name):
    _, h, cols = p.shape
    nb = h // tr

    def body(kc_ref, p_ref, r0_ref, r1_ref, r2_ref, o_ref):
        o_ref[...] = ((p_ref[...].astype(F32) + r0_ref[...].astype(F32)) + r1_ref[...].astype(F32)) + r2_ref[...].astype(F32)

    rspec = lambda j: pl.BlockSpec((None, tr, cols), lambda i, kc: (j, i, 0))
    return pl.pallas_call(
        body, name=name,
        grid_spec=pltpu.PrefetchScalarGridSpec(
            num_scalar_prefetch=1, grid=(nb,),
            in_specs=[pl.BlockSpec((None, tr, cols), lambda i, kc: (kc[0], i, 0)), rspec(0), rspec(1), rspec(2)],
            out_specs=pl.BlockSpec((tr, cols), lambda i, kc: (kc[1] * nb + i, 0))),
        out_shape=jax.ShapeDtypeStruct((2 * h, cols), F32), compiler_params=_cparams("parallel"),
    )(kc_arr, p, r, r, r)


def _reunite_sibling(fs, shapes):
    n_chunks = 2
    nm = len(fs)

    def body(*refs):
        in_refs, out_refs = refs[:nm], refs[nm:2 * nm]
        send_sems, recv_sems = refs[2 * nm:]
        x, y, c, _ = _position()
        sends, recvs = [], []
        for m in range(nm):
            ch = shapes[m][0] // 2 // n_chunks
            for q in range(n_chunks):
                mine = pl.ds((c * n_chunks + q) * ch, ch)
                theirs = pl.ds(((1 - c) * n_chunks + q) * ch, ch)
                s = n_chunks * m + q
                sends.append(_remote(in_refs[m].at[mine, :], out_refs[m].at[mine, :], send_sems, recv_sems, s, (x, y, 1 - c)))
                recvs.append(_remote(in_refs[m].at[theirs, :], out_refs[m].at[theirs, :], send_sems, recv_sems, s, (x, y, c)))
        for cp in sends:
            cp.start()
        for cp in recvs:
            cp.wait_recv()
        for cp in sends:
            cp.wait_send()

    n_sems = n_chunks * nm
    return pl.pallas_call(
        body, name="reunite_sibling", in_specs=[HBM_SPEC] * nm, out_specs=[HBM_SPEC] * nm,
        out_shape=[jax.ShapeDtypeStruct(s, F32) for s in shapes],
        input_output_aliases={m: m for m in range(nm)},
        scratch_shapes=[pltpu.SemaphoreType.DMA((n_sems,)), pltpu.SemaphoreType.DMA((n_sems,))],
    )(*fs)


def _sum_vecs(v):
    def body(v_ref, o_ref):
        acc = v_ref[0]
        for d in range(1, 8):
            acc = acc + v_ref[d]
        o_ref[...] = acc

    return pl.pallas_call(body, name="sum_vecs", out_shape=jax.ShapeDtypeStruct(v.shape[1:], F32))(v)


def kernel(x, w_in, gate_up, gate_bias, gla_norm_g, w_out, ln1_g, ln1_b, w_up, conv_w, conv_b, w_down, ln2_g, ln2_b, loss_target, m_w_in, m_gate_up, m_gate_bias, m_gla_norm_g, m_w_out, m_ln1_g, m_ln1_b, m_w_up, m_conv_w, m_conv_b, m_w_down, m_ln2_g, m_ln2_b, v_w_in, v_gate_up, v_gate_bias, v_gla_norm_g, v_w_out, v_ln1_g, v_ln1_b, v_w_up, v_conv_w, v_conv_b, v_w_down, v_ln2_g, v_ln2_b):
    xi, yi, ci = lax.axis_index("x"), lax.axis_index("y"), lax.axis_index("c")
    k_me = 2 * xi + yi
    c_arr = ci.astype(jnp.int32).reshape(1)
    kc_arr = jnp.stack([k_me, ci]).astype(jnp.int32)
    small = _pad_rows(jnp.concatenate([_rows(conv_w[0]), _rows(gate_up[0])], axis=0), SMALL_ROWS)
    w_in4, gsmall = _gather_weights([w_in[0].astype(BF16)], small, SHARD_SHAPES[:1])
    late_shards = [w_up[0].astype(BF16), w_out[0].astype(BF16), w_down[0].astype(BF16)]
    w_in_p = jnp.pad(jnp.concatenate([w_in4[k] for k in range(4)], axis=1), ((0, 0), (0, IN_PAD - IN_WIDTH)))
    conv_w_f = jnp.concatenate([gsmall[k, :5].reshape(-1)[:3 * W_UP_S].reshape(3, W_UP_S) for k in range(4)], axis=1)
    gate_up_f = jnp.concatenate([gsmall[k, 5].reshape(GATE_RANK, GLA_KEYS // 4) for k in range(4)], axis=1)
    conv_w3 = jnp.transpose(conv_w_f.reshape(3, 2, D_FF), (1, 0, 2))
    conv_b3 = conv_b.reshape(2, 1, D_FF)
    gate_up_pad = jnp.pad(gate_up_f, ((0, LANES - GATE_RANK), (0, 0)))

    loss_part, dx, late_sums, dw_in_p, g = _local_step(
        x[0], loss_target[0], w_in_p, late_shards, gate_up_pad, gate_bias, gla_norm_g, ln1_g, ln1_b, conv_w3, conv_b3,
        ln2_g, ln2_b, c_arr, kc_arr)
    g["loss"] = loss_part

    g_in4 = jnp.stack([dw_in_p[:, k * W_IN_S:(k + 1) * W_IN_S] for k in range(4)], axis=0)
    from_sib, = _exchange_sibling([g_in4], SHARD_SHAPES[:1], "exchange_sibling_w_in")
    p_in = _add_sibling(g_in4, from_sib, c_arr, ADD_ROWS[0], "add_sibling_w_in")
    others_in, vecs = _exchange_chips([p_in], _pack_vec(g, SMALL_GRADS, GRAD_VEC_ROWS))
    sum_in = _add_chips(p_in, others_in, kc_arr, ADD_ROWS[0], "add_chips_w_in")
    g_w_in, g_w_up, g_w_out, g_w_down = _reunite_sibling([sum_in] + late_sums, SHARD_SHAPES)
    gsmall_sum = _unpack_vec(_sum_vecs(vecs), SMALL_GRADS)
    g_conv_w = lax.dynamic_slice_in_dim(gsmall_sum["conv_w"].reshape(3, 2 * D_FF), k_me * W_UP_S, W_UP_S, axis=1)
    g_gate_up = lax.dynamic_slice_in_dim(gsmall_sum["gate_up"].reshape(GATE_RANK, GLA_KEYS), k_me * (GLA_KEYS // 4),
                                         GLA_KEYS // 4, axis=1)
    gv = gsmall_sum
    loss = gv["loss"][0, 0]
    gvec = _pack_vec(gv, VEC_SIZES, VEC_ROWS)

    grads = dict(w_in=g_w_in[None], gate_up=g_gate_up[None], gate_bias=gv["gate_bias"], gla_norm_g=gv["gla_norm_g"],
                 w_out=g_w_out[None], ln1_g=gv["ln1_g"], ln1_b=gv["ln1_b"], w_up=g_w_up[None], conv_w=g_conv_w[None],
                 conv_b=gv["conv_b"], w_down=g_w_down[None], ln2_g=gv["ln2_g"], ln2_b=gv["ln2_b"])
    weights = dict(w_in=w_in, gate_up=gate_up, gate_bias=gate_bias, gla_norm_g=gla_norm_g, w_out=w_out, ln1_g=ln1_g,
                   ln1_b=ln1_b, w_up=w_up, conv_w=conv_w, conv_b=conv_b, w_down=w_down, ln2_g=ln2_g, ln2_b=ln2_b)
    ms = dict(w_in=m_w_in, gate_up=m_gate_up, gate_bias=m_gate_bias, gla_norm_g=m_gla_norm_g, w_out=m_w_out, ln1_g=m_ln1_g,
              ln1_b=m_ln1_b, w_up=m_w_up, conv_w=m_conv_w, conv_b=m_conv_b, w_down=m_w_down, ln2_g=m_ln2_g, ln2_b=m_ln2_b)
    vs = dict(w_in=v_w_in, gate_up=v_gate_up, gate_bias=v_gate_bias, gla_norm_g=v_gla_norm_g, w_out=v_w_out, ln1_g=v_ln1_g,
              ln1_b=v_ln1_b, w_up=v_w_up, conv_w=v_conv_w, conv_b=v_conv_b, w_down=v_w_down, ln2_g=v_ln2_g, ln2_b=v_ln2_b)
    names = ["w_in", "gate_up", "gate_bias", "gla_norm_g", "w_out", "ln1_g", "ln1_b", "w_up", "conv_w", "conv_b", "w_down",
             "ln2_g", "ln2_b"]
    delta, new_m, new_v = {}, {}, {}
    for n in ("w_in", "gate_up", "w_out", "w_up", "conv_w", "w_down"):
        tr_ = jnp.transpose if n == "w_in" else (lambda a: a)
        d, nm, nv = _adamw(tr_(weights[n][0]), tr_(grads[n][0]), tr_(ms[n][0]), tr_(vs[n][0]), "adamw_" + n)
        delta[n], new_m[n], new_v[n] = tr_(d)[None], tr_(nm)[None], tr_(nv)[None]
    d, nm, nv = _adamw(_pack_vec(weights, VEC_SIZES, VEC_ROWS), gvec, _pack_vec(ms, VEC_SIZES, VEC_ROWS),
                       _pack_vec(vs, VEC_SIZES, VEC_ROWS), "adamw_vectors")
    for dst, src in ((delta, d), (new_m, nm), (new_v, nv)):
        dst.update(_unpack_vec(src, VEC_SIZES))
    return (loss, dx[None], *[grads[n] for n in names], *[delta[n] for n in names], *[new_m[n] for n in names],
            *[new_v[n] for n in names])
```

```python
import functools
import math

import jax
import jax.numpy as jnp
from jax import lax
from jax.experimental import pallas as pl
from jax.experimental.pallas import tpu as pltpu

F32 = jnp.float32
BF16 = jnp.bfloat16

D_MODEL = 1024
SB_WIDTH = 512
GLA_KEYS = 256
GLA_WIDTH = 512
GATE_RANK = 16
IN_WIDTH = 3088
IN_PAD = 3200
D_FF = 2816
CHUNK = 64
LN_EPS = 1e-5
RMS_EPS = 1e-6
ALPHA = 2.0 ** 0.25
GLA_TAU = 16.0
SB_SCALE = 0.125
GLA_SCALE = 0.125
LANES = 128
SB_BLK = 256
SB_CUT = -100.0
GLA_ROWS = 256
VMEM_LIMIT = 56 * 1024 * 1024

ADAM_LR, ADAM_B1, ADAM_B2, ADAM_EPS, ADAM_WD, ADAM_STEP = 0.001, 0.9, 0.999, 1e-08, 0.01, 10

MESH = pl.DeviceIdType.MESH


def _cparams(*sem):
    return pltpu.CompilerParams(dimension_semantics=sem, vmem_limit_bytes=VMEM_LIMIT)


def _dot(a, b):
    return jnp.dot(a, b, preferred_element_type=F32)


def _dot_nt(a, b):
    return lax.dot_general(a, b, (((1,), (1,)), ((), ())), preferred_element_type=F32)


def _dot_tn(a, b):
    return lax.dot_general(a, b, (((0,), (0,)), ((), ())), preferred_element_type=F32)


def _split3(x):
    hi = x.astype(BF16)
    r = x - hi.astype(F32)
    mid = r.astype(BF16)
    lo = (r - mid.astype(F32)).astype(BF16)
    return hi, mid, lo


def _softplus(z):
    return jnp.maximum(z, 0.0) + jnp.log(1.0 + jnp.exp(-jnp.abs(z)))


def _sigmoid(z):
    return 1.0 / (1.0 + jnp.exp(-z))


def _mm(a, b, *, m, n, k, tm, tn, tk, ta=False, tb=False, a_spec=None, b_spec=None, o_spec=None,
        out_shape=None, out_dtype=F32, add=None, add_scale=1.0, rider=None, name):
    nk = k // tk
    dn = (((0 if ta else 1,), (1 if tb else 0,)), ((), ()))
    n_in = len(rider.inputs) if rider else 0
    n_out = len(rider.out_shapes) if rider else 0
    n_add = int(add is not None)
    steps = (m // tm, n // tn, nk)

    def body(*refs):
        a_ref, b_ref = refs[:2]
        add_ref = refs[2] if add is not None else None
        rin = refs[2 + n_add:2 + n_add + n_in]
        o_ref = refs[2 + n_add + n_in]
        rout = refs[3 + n_add + n_in:3 + n_add + n_in + n_out]
        scratch = refs[3 + n_add + n_in + n_out:]
        if rider:
            sems = scratch[len(scratch) - len(rider.sems):]
            ids = [pl.program_id(d) for d in range(3)]
            _ride(rider, rin, rout, sems, (ids[0] == 0) & (ids[1] == 0) & (ids[2] == 0),
                  (ids[0] == steps[0] - 1) & (ids[1] == steps[1] - 1) & (ids[2] == steps[2] - 1))

        part = lax.dot_general(a_ref[...].astype(BF16), b_ref[...].astype(BF16), dn, preferred_element_type=F32)

        def finish(r):
            if add is not None:
                r = r + add_scale * add_ref[...]
            o_ref[...] = r.astype(out_dtype)

        if nk == 1:
            finish(part)
            return
        acc_ref = scratch[0]
        kk = pl.program_id(2)

        @pl.when(kk == 0)
        def _():
            acc_ref[...] = part

        @pl.when((kk > 0) & (kk < nk - 1))
        def _():
            acc_ref[...] += part

        @pl.when(kk == nk - 1)
        def _():
            finish(acc_ref[...] + part)

    if a_spec is None:
        a_spec = pl.BlockSpec((tk, tm), lambda i, j, kk: (kk, i)) if ta else pl.BlockSpec((tm, tk), lambda i, j, kk: (i, kk))
    if b_spec is None:
        b_spec = pl.BlockSpec((tn, tk), lambda i, j, kk: (j, kk)) if tb else pl.BlockSpec((tk, tn), lambda i, j, kk: (kk, j))
    if o_spec is None:
        o_spec = pl.BlockSpec((tm, tn), lambda i, j, kk: (i, j))
    if out_shape is None:
        out_shape = jax.ShapeDtypeStruct((m, n), out_dtype)
    in_specs = [a_spec, b_spec]
    args = [a, b]
    if add is not None:
        in_specs.append(pl.BlockSpec((tm, tn), lambda i, j, kk: (i, j)))
        args.append(add)
    scratch = [pltpu.VMEM((tm, tn), F32)] if nk > 1 else []
    if not rider:
        return pl.pallas_call(
            body, name=name, grid=steps, in_specs=in_specs, out_specs=o_spec, out_shape=out_shape,
            scratch_shapes=scratch, compiler_params=_cparams("parallel", "parallel", "arbitrary"),
        )(*args)
    return pl.pallas_call(
        body, name=name, grid=steps, in_specs=in_specs + [HBM_SPEC] * n_in, out_specs=[o_spec] + [HBM_SPEC] * n_out,
        out_shape=[out_shape] + list(rider.out_shapes),
        scratch_shapes=scratch + [pltpu.SemaphoreType.DMA((s,)) for s in rider.sems],
        compiler_params=_cparams("arbitrary", "arbitrary", "arbitrary"),
    )(*args, *rider.inputs)


class _Rider:
    def __init__(self, inputs, out_shapes, sems, copies):
        self.inputs, self.out_shapes, self.sems, self.copies = list(inputs), list(out_shapes), tuple(sems), copies


def _ride(rider, in_refs, out_refs, sems, first, last):
    @pl.when(first)
    def _():
        local, sends, _ = rider.copies(in_refs, out_refs, sems)
        for cp in local + sends:
            cp.start()

    @pl.when(last)
    def _():
        local, sends, recvs = rider.copies(in_refs, out_refs, sems)
        for cp in recvs:
            cp.wait_recv()
        for cp in sends:
            cp.wait_send()
        for cp in local:
            cp.wait()


def _run(rider, name):
    n_in, n_out = len(rider.inputs), len(rider.out_shapes)

    def body(*refs):
        local, sends, recvs = rider.copies(refs[:n_in], refs[n_in:n_in + n_out], refs[n_in + n_out:])
        for cp in local + sends:
            cp.start()
        for cp in recvs:
            cp.wait_recv()
        for cp in sends:
            cp.wait_send()
        for cp in local:
            cp.wait()

    return pl.pallas_call(
        body, name=name, in_specs=[HBM_SPEC] * n_in, out_specs=[HBM_SPEC] * n_out, out_shape=list(rider.out_shapes),
        scratch_shapes=[pltpu.SemaphoreType.DMA((s,)) for s in rider.sems],
    )(*rider.inputs)


def _sb_tile(qh, kj, diag, strict, u_strict, r_in):
    z = _dot_nt(qh, kj)
    sp = _softplus(z)
    l1m = -sp
    lsz = z - sp
    if diag:
        l1m = jnp.where(strict, l1m, 0.0)
    cs = _dot(l1m.astype(BF16), u_strict) + r_in
    w = jnp.exp(lsz + cs)
    if diag:
        w = jnp.where(strict, w, 0.0)
    return l1m, lsz, w


def _sb_consts():
    row = lax.broadcasted_iota(jnp.int32, (SB_BLK, SB_BLK), 0)
    col = lax.broadcasted_iota(jnp.int32, (SB_BLK, SB_BLK), 1)
    strict = col < row
    u_strict = (row > col).astype(BF16)
    u_pre = (row < col).astype(BF16)
    lane = lax.broadcasted_iota(jnp.int32, (1, LANES), 1)
    return strict, u_strict, u_pre, lane


def _sb_fwd(proj, t, rider):
    nq = t // SB_BLK
    n_in, n_out = len(rider.inputs), len(rider.out_shapes)

    def body(q_ref, k_ref, v_ref, *rest):
        rin, o_ref, rout = rest[:n_in], rest[n_in], rest[n_in + 1:n_in + 1 + n_out]
        p = pl.program_id(0)
        i = pl.program_id(1)
        _ride(rider, rin, rout, rest[n_in + 1 + n_out:], (p == 0) & (i == 0), (p == 3) & (i == nq - 1))

        strict, u_strict, _, lane = _sb_consts()
        qf = q_ref[...] * SB_SCALE
        hms = [(lane // 64) == hh for hh in range(2)]
        qhs = [jnp.where(hm, qf, 0.0).astype(BF16) for hm in hms]

        def step(j, r0, r1, a, diag):
            rows = pl.ds(pl.multiple_of(j * SB_BLK, SB_BLK), SB_BLK)
            kj = k_ref[rows, :].astype(BF16)
            vf = v_ref[rows, :]
            rs = []
            for hh, r in enumerate((r0, r1)):
                l1m, _, w = _sb_tile(qhs[hh], kj, diag, strict, u_strict, r)
                a = a + _dot(w.astype(BF16), jnp.where(hms[hh], vf, 0.0).astype(BF16))
                rs.append(r + jnp.sum(l1m, axis=1, keepdims=True))
            return rs[0], rs[1], a

        zero = jnp.zeros((SB_BLK, 1), F32)
        r0, r1, acc = step(i, zero, zero, jnp.zeros((SB_BLK, LANES), F32), True)
        _, _, _, acc = lax.while_loop(
            lambda c: (c[0] >= 0) & (jnp.maximum(jnp.max(c[1]), jnp.max(c[2])) > SB_CUT),
            lambda c: (c[0] - 1,) + step(c[0], c[1], c[2], c[3], False),
            (i - 1, r0, r1, acc))
        o_ref[...] = acc.astype(BF16)

    return pl.pallas_call(
        body, name="sb_fwd", grid=(4, nq),
        in_specs=[pl.BlockSpec((SB_BLK, LANES), lambda p, i: (i, p)),
                  pl.BlockSpec((t, LANES), lambda p, i: (0, 4 + p)),
                  pl.BlockSpec((t, LANES), lambda p, i: (0, 8 + p))] + [HBM_SPEC] * n_in,
        out_specs=[pl.BlockSpec((SB_BLK, LANES), lambda p, i: (i, p))] + [HBM_SPEC] * n_out,
        out_shape=[jax.ShapeDtypeStruct((t, SB_WIDTH), BF16)] + rider.out_shapes,
        scratch_shapes=[pltpu.SemaphoreType.DMA((s,)) for s in rider.sems],
        compiler_params=_cparams("arbitrary", "arbitrary"),
    )(proj, proj, proj, *rider.inputs)


def _sb_bwd(proj, dcat, t, rider):
    nq = t // SB_BLK
    n_in, n_out = len(rider.inputs), len(rider.out_shapes)

    def body(q_ref, k_ref, v_ref, do_ref, *rest):
        rin = rest[:n_in]
        dq_ref, dk_ref, dv_ref = rest[n_in:n_in + 3]
        rout = rest[n_in + 3:n_in + 3 + n_out]
        r_scr = rest[n_in + 3 + n_out]
        p = pl.program_id(0)
        i = pl.program_id(1)
        _ride(rider, rin, rout, rest[n_in + 4 + n_out:], (p == 0) & (i == 0), (p == 3) & (i == nq - 1))

        @pl.when(i == 0)
        def _():
            dk_ref[...] = jnp.zeros_like(dk_ref)
            dv_ref[...] = jnp.zeros_like(dv_ref)

        strict, u_strict, u_pre, lane = _sb_consts()
        qf = q_ref[...] * SB_SCALE
        dof = do_ref[...]
        hms = [(lane // 64) == hh for hh in range(2)]
        qhs = [jnp.where(hm, qf, 0.0).astype(BF16) for hm in hms]
        dohs = [jnp.where(hm, dof, 0.0).astype(BF16) for hm in hms]

        def scan(j, r0, r1, diag):
            rows = pl.ds(pl.multiple_of(j * SB_BLK, SB_BLK), SB_BLK)
            kj = k_ref[rows, :].astype(BF16)
            out = []
            for hh, r in enumerate((r0, r1)):
                l1m = -_softplus(_dot_nt(qhs[hh], kj))
                if diag:
                    l1m = jnp.where(strict, l1m, 0.0)
                r_scr[hh, j] = r
                out.append(r + jnp.sum(l1m, axis=1, keepdims=True))
            return out[0], out[1]

        zero = jnp.zeros((SB_BLK, 1), F32)
        r0, r1 = scan(i, zero, zero, True)
        jstop, _, _ = lax.while_loop(
            lambda c: (c[0] >= 0) & (jnp.maximum(jnp.max(c[1]), jnp.max(c[2])) > SB_CUT),
            lambda c: (c[0] - 1,) + scan(c[0], c[1], c[2], False),
            (i - 1, r0, r1))

        def step(j, carry, diag):
            pre_g0, pre_g1, dqa = carry
            rows = pl.ds(pl.multiple_of(j * SB_BLK, SB_BLK), SB_BLK)
            kf = k_ref[rows, :]
            kj = kf.astype(BF16)
            vj = v_ref[rows, :].astype(BF16)
            dv = jnp.zeros((SB_BLK, LANES), F32)
            dk = jnp.zeros((SB_BLK, LANES), F32)
            pre = []
            for hh, pre_g in enumerate((pre_g0, pre_g1)):
                _, lsz, w = _sb_tile(qhs[hh], kj, diag, strict, u_strict, r_scr[hh, j])
                g = w * _dot_nt(dohs[hh], vj)
                gpre = _dot(g.astype(BF16), u_pre) + pre_g
                sig = jnp.exp(lsz)
                dz = g * (1.0 - sig) - gpre * sig
                if diag:
                    dz = jnp.where(strict, dz, 0.0)
                dzb = dz.astype(BF16)
                dv = dv + _dot_tn(w.astype(BF16), dohs[hh])
                dk = dk + _dot_tn(dzb, qhs[hh])
                dqa = dqa + _dot(dzb, jnp.where(hms[hh], kf, 0.0).astype(BF16))
                pre.append(pre_g + jnp.sum(g, axis=1, keepdims=True))
            dv_ref[rows, :] += dv
            dk_ref[rows, :] += dk
            return pre[0], pre[1], dqa

        carry = lax.fori_loop(jstop + 1, i, lambda j, c: step(j, c, False), (zero, zero, jnp.zeros((SB_BLK, LANES), F32)))
        _, _, dq = step(i, carry, True)
        dq_ref[...] = dq * SB_SCALE

    return pl.pallas_call(
        body, name="sb_bwd", grid=(4, nq),
        in_specs=[pl.BlockSpec((SB_BLK, LANES), lambda p, i: (i, p)),
                  pl.BlockSpec((t, LANES), lambda p, i: (0, 4 + p)),
                  pl.BlockSpec((t, LANES), lambda p, i: (0, 8 + p)),
                  pl.BlockSpec((SB_BLK, LANES), lambda p, i: (i, p))] + [HBM_SPEC] * n_in,
        out_specs=[pl.BlockSpec((SB_BLK, LANES), lambda p, i: (i, p)),
                   pl.BlockSpec((t, LANES), lambda p, i: (0, p)),
                   pl.BlockSpec((t, LANES), lambda p, i: (0, p))] + [HBM_SPEC] * n_out,
        out_shape=[jax.ShapeDtypeStruct((t, SB_WIDTH), F32)] * 3 + rider.out_shapes,
        scratch_shapes=[pltpu.VMEM((2, nq, SB_BLK, 1), F32)] + [pltpu.SemaphoreType.DMA((s,)) for s in rider.sems],
        compiler_params=_cparams("arbitrary", "arbitrary"),
    )(proj, proj, proj, dcat, *rider.inputs)


def _gla_consts():
    r = lax.broadcasted_iota(jnp.int32, (CHUNK, CHUNK), 0)
    c = lax.broadcasted_iota(jnp.int32, (CHUNK, CHUNK), 1)
    causal = c <= r
    lc = causal.astype(BF16)
    lct = (c >= r).astype(BF16)
    rowid = lax.broadcasted_iota(jnp.int32, (CHUNK, 1), 0)
    lane = lax.broadcasted_iota(jnp.int32, (1, LANES), 1)
    sr = lax.broadcasted_iota(jnp.int32, (LANES, 2 * LANES), 0)
    sc = lax.broadcasted_iota(jnp.int32, (LANES, 2 * LANES), 1)
    blockdiag = (sr // 64) == (sc // LANES)
    return causal, lc, lct, rowid, lane, blockdiag


def _dot3(u, x):
    hi, mid, lo = _split3(x)
    return _dot(u, hi) + _dot(u, mid) + _dot(u, lo)


def _row_to_col(row):
    return jnp.transpose(jnp.broadcast_to(row, (LANES, LANES)))


def _gla_gates(ga_ref, gup_ref, gbias_ref):
    pre = _dot(ga_ref[...].astype(BF16), gup_ref[...].astype(BF16)) + gbias_ref[...]
    log_a = (jnp.minimum(pre, 0.0) - jnp.log(1.0 + jnp.exp(-jnp.abs(pre)))) / GLA_TAU
    return pre, log_a


def _gla_chunk_terms(g2, q2, k2, lc, rowid):
    b = _dot3(lc, g2)
    b_ref = jnp.sum(jnp.where(rowid == CHUNK // 2 - 1, b, 0.0), axis=0, keepdims=True)
    b_last = jnp.sum(jnp.where(rowid == CHUNK - 1, b, 0.0), axis=0, keepdims=True)
    qs = q2 * GLA_SCALE
    e_q = jnp.exp(b - b_ref)
    e_k = jnp.exp(b_ref - b)
    e_d = jnp.exp(b_last - b)
    e_b = jnp.exp(b)
    decay = jnp.exp(b_last)
    return dict(qs=qs, e_q=e_q, e_k=e_k, e_d=e_d, e_b=e_b, decay=decay,
                qi=qs * e_q, ki=k2 * e_k, kd=k2 * e_d, qb=qs * e_b)


def _gla_fwd(proj, gate_up_pad, gate_bias, gnorm, t, rider):
    nsteps = t // GLA_ROWS
    cps = GLA_ROWS // CHUNK
    nm = len(rider.inputs)

    def body(q_ref, k_ref, v_ref, gg_ref, ga_ref, gup_ref, gbias_ref, gn_ref, *rest):
        o_ref, oraw_ref, st_ref = rest[nm:nm + 3]
        s_scr = rest[2 * nm + 3]
        i = pl.program_id(0)
        _ride(rider, rest[:nm], rest[nm + 3:2 * nm + 3], rest[2 * nm + 4:], i == 0, i == nsteps - 1)

        @pl.when(i == 0)
        def _():
            s_scr[...] = jnp.zeros_like(s_scr)

        causal, lc, _, rowid, lane, blockdiag = _gla_consts()
        _, log_a = _gla_gates(ga_ref, gup_ref, gbias_ref)
        gn = gn_ref[...]
        state = [s_scr[0], s_scr[1]]
        for cc in range(cps):
            rows = slice(cc * CHUNK, (cc + 1) * CHUNK)
            for p in range(2):
                kl = slice(p * LANES, (p + 1) * LANES)
                vl = slice(p * 2 * LANES, (p + 1) * 2 * LANES)
                tm = _gla_chunk_terms(log_a[rows, kl], q_ref[rows, kl], k_ref[rows, kl], lc, rowid)
                v2 = v_ref[rows, vl]
                v2b = v2.astype(BF16)
                s_prev = state[p]
                st_ref[cc, p] = s_prev
                kib = tm["ki"].astype(BF16)
                o_inter = _dot(tm["qb"].astype(BF16), s_prev.astype(BF16))
                outs = []
                for hh in range(2):
                    hm = (lane // 64) == hh
                    a = _dot_nt(jnp.where(hm, tm["qi"], 0.0).astype(BF16), kib)
                    a = jnp.where(causal, a, 0.0)
                    outs.append(_dot(a.astype(BF16), v2b[:, hh * LANES:(hh + 1) * LANES]))
                o2 = jnp.concatenate(outs, axis=1) + o_inter
                upd = jnp.where(blockdiag, _dot_tn(tm["kd"].astype(BF16), v2b), 0.0)
                dcol = _row_to_col(tm["decay"])
                state[p] = s_prev * jnp.concatenate([dcol, dcol], axis=1) + upd
                oraw_ref[rows, vl] = o2
                for hh in range(2):
                    hl = slice(hh * LANES, (hh + 1) * LANES)
                    oh = o2[:, hl]
                    gl = slice(p * 2 * LANES + hh * LANES, p * 2 * LANES + (hh + 1) * LANES)
                    rinv = lax.rsqrt(jnp.mean(oh * oh, axis=1, keepdims=True) + RMS_EPS)
                    gg = gg_ref[rows, gl]
                    o_ref[rows, gl] = (oh * rinv * gn * (gg * _sigmoid(gg))).astype(BF16)
        s_scr[0] = state[0]
        s_scr[1] = state[1]

    cb = lambda w, idx: pl.BlockSpec((GLA_ROWS, w), lambda i: (i, idx))
    full = lambda shp: pl.BlockSpec(shp, lambda i: tuple(0 for _ in shp))
    return pl.pallas_call(
        body, name="gla_fwd", grid=(nsteps,),
        in_specs=[cb(256, 6), cb(256, 7), cb(512, 4), cb(512, 5), cb(128, 24),
                  full((LANES, GLA_KEYS)), full((1, GLA_KEYS)), full((1, LANES))] + [HBM_SPEC] * nm,
        out_specs=[pl.BlockSpec((GLA_ROWS, GLA_WIDTH), lambda i: (i, 0)),
                   pl.BlockSpec((GLA_ROWS, GLA_WIDTH), lambda i: (i, 0)),
                   pl.BlockSpec((cps, 2, LANES, 2 * LANES), lambda i: (i, 0, 0, 0))] + [HBM_SPEC] * nm,
        out_shape=[jax.ShapeDtypeStruct((t, GLA_WIDTH), BF16), jax.ShapeDtypeStruct((t, GLA_WIDTH), F32),
                   jax.ShapeDtypeStruct((t // CHUNK, 2, LANES, 2 * LANES), F32)]
        + rider.out_shapes,
        input_output_aliases={8 + m: 3 + m for m in range(nm)},
        scratch_shapes=[pltpu.VMEM((2, LANES, 2 * LANES), F32)] + [pltpu.SemaphoreType.DMA((s,)) for s in rider.sems],
        compiler_params=_cparams("arbitrary"),
    )(proj, proj, proj, proj, proj, gate_up_pad, gate_bias, gnorm, *rider.inputs)


def _gla_bwd(proj, dcat, oraw, states, gate_up_pad, gate_bias, gnorm, t):
    nsteps = t // GLA_ROWS
    cps = GLA_ROWS // CHUNK
    wout = 2 * GLA_KEYS + 2 * GLA_WIDTH + LANES

    def body(q_ref, k_ref, v_ref, gg_ref, ga_ref, do_ref, oraw_ref, st_ref, gup_ref, gbias_ref, gn_ref,
             d_ref, dgup_ref, dgbias_ref, dgn_ref, ds_scr):
        i = pl.program_id(0)

        @pl.when(i == 0)
        def _():
            ds_scr[...] = jnp.zeros_like(ds_scr)
            dgup_ref[...] = jnp.zeros_like(dgup_ref)
            dgbias_ref[...] = jnp.zeros_like(dgbias_ref)
            dgn_ref[...] = jnp.zeros_like(dgn_ref)

        causal, lc, lct, rowid, lane, blockdiag = _gla_consts()
        pre, log_a = _gla_gates(ga_ref, gup_ref, gbias_ref)
        gn = gn_ref[...]
        dgn = jnp.zeros((1, LANES), F32)
        dstate = [ds_scr[0], ds_scr[1]]
        dgs = [[None, None] for _ in range(cps)]
        for cc in reversed(range(cps)):
            rows = slice(cc * CHUNK, (cc + 1) * CHUNK)
            for p in range(2):
                kl = slice(p * LANES, (p + 1) * LANES)
                vl = slice(p * 2 * LANES, (p + 1) * 2 * LANES)
                tm = _gla_chunk_terms(log_a[rows, kl], q_ref[rows, kl], k_ref[rows, kl], lc, rowid)
                v2b = v_ref[rows, vl].astype(BF16)
                s_prev = st_ref[cc, p]
                ds2 = dstate[p]
                dos = []
                for hh in range(2):
                    gl = slice(p * 2 * LANES + hh * LANES, p * 2 * LANES + (hh + 1) * LANES)
                    oh = oraw_ref[rows, gl]
                    rinv = lax.rsqrt(jnp.mean(oh * oh, axis=1, keepdims=True) + RMS_EPS)
                    on = oh * rinv
                    gg = gg_ref[rows, gl]
                    sg = _sigmoid(gg)
                    sil = gg * sg
                    dgo = do_ref[rows, gl]
                    d_ref[rows, 2 * GLA_KEYS + GLA_WIDTH + gl.start:2 * GLA_KEYS + GLA_WIDTH + gl.stop] = (
                        dgo * on * gn * (sg * (1.0 + gg * (1.0 - sg))))
                    dgn = dgn + jnp.sum(dgo * sil * on, axis=0, keepdims=True)
                    don = dgo * sil * gn
                    dos.append(rinv * (don - on * jnp.mean(don * on, axis=1, keepdims=True)))
                do2b = jnp.concatenate(dos, axis=1).astype(BF16)
                qib = tm["qi"].astype(BF16)
                kib = tm["ki"].astype(BF16)
                kdb = tm["kd"].astype(BF16)
                qbb = tm["qb"].astype(BF16)
                ds2b = ds2.astype(BF16)
                dqi = jnp.zeros((CHUNK, LANES), F32)
                dki = jnp.zeros((CHUNK, LANES), F32)
                dvs = []
                for hh in range(2):
                    hm = (lane // 64) == hh
                    hl = slice(hh * LANES, (hh + 1) * LANES)
                    a = jnp.where(causal, _dot_nt(jnp.where(hm, tm["qi"], 0.0).astype(BF16), kib), 0.0).astype(BF16)
                    da = jnp.where(causal, _dot_nt(do2b[:, hl], v2b[:, hl]), 0.0).astype(BF16)
                    dvs.append(_dot_tn(a, do2b[:, hl]))
                    dqi = dqi + jnp.where(hm, _dot(da, kib), 0.0)
                    dki = dki + jnp.where(hm, _dot_tn(da, qib), 0.0)
                dv2 = jnp.concatenate(dvs, axis=1) + _dot(kdb, ds2b)
                dqb = _dot_nt(do2b, s_prev.astype(BF16))
                dkd = _dot_nt(v2b, ds2b)
                dcol = _row_to_col(tm["decay"])
                dsp = jnp.where(blockdiag, _dot_tn(qbb, do2b), 0.0) + ds2 * jnp.concatenate([dcol, dcol], axis=1)
                ddecay_col = jnp.sum(ds2 * s_prev, axis=1, keepdims=True)
                ddecay_row = jnp.transpose(jnp.broadcast_to(ddecay_col, (LANES, LANES)))[0:1, :]
                dstate[p] = dsp
                dqs = dqi * tm["e_q"] + dqb * tm["e_b"]
                dk = dki * tm["e_k"] + dkd * tm["e_d"]
                t_qi = dqi * tm["qi"]
                t_ki = dki * tm["ki"]
                t_kd = dkd * tm["kd"]
                db = t_qi - t_ki + dqb * tm["qb"] - t_kd
                db_ref = jnp.sum(t_ki - t_qi, axis=0, keepdims=True)
                db_last = jnp.sum(t_kd, axis=0, keepdims=True) + ddecay_row * tm["decay"]
                db = db + jnp.where(rowid == CHUNK // 2 - 1, db_ref, 0.0) + jnp.where(rowid == CHUNK - 1, db_last, 0.0)
                dg = _dot3(lct, db)
                d_ref[rows, p * LANES:(p + 1) * LANES] = dqs * GLA_SCALE
                d_ref[rows, GLA_KEYS + p * LANES:GLA_KEYS + (p + 1) * LANES] = dk
                d_ref[rows, 2 * GLA_KEYS + p * 2 * LANES:2 * GLA_KEYS + (p + 1) * 2 * LANES] = dv2
                dgs[cc][p] = dg
        ds_scr[0] = dstate[0]
        ds_scr[1] = dstate[1]
        dlog_a = jnp.concatenate([jnp.concatenate(row, axis=1) for row in dgs], axis=0)
        dpre = dlog_a * (1.0 / GLA_TAU) * _sigmoid(-pre)
        dpb = dpre.astype(BF16)
        dgn_ref[...] += dgn
        dgbias_ref[...] += jnp.sum(dpre, axis=0, keepdims=True)
        dgup_ref[...] += _dot_tn(ga_ref[...].astype(BF16), dpb)
        d_ref[:, 2 * GLA_KEYS + 2 * GLA_WIDTH:] = _dot_nt(dpb, gup_ref[...].astype(BF16))

    rev = lambda i: nsteps - 1 - i
    cb = lambda w, idx: pl.BlockSpec((GLA_ROWS, w), lambda i: (rev(i), idx))
    full = lambda shp: pl.BlockSpec(shp, lambda i: tuple(0 for _ in shp))
    return pl.pallas_call(
        body, name="gla_bwd", grid=(nsteps,),
        in_specs=[cb(256, 6), cb(256, 7), cb(512, 4), cb(512, 5), cb(128, 24), cb(512, 1), cb(512, 0),
                  pl.BlockSpec((cps, 2, LANES, 2 * LANES), lambda i: (rev(i), 0, 0, 0)),
                  full((LANES, GLA_KEYS)), full((1, GLA_KEYS)), full((1, LANES))],
        out_specs=[pl.BlockSpec((GLA_ROWS, wout), lambda i: (rev(i), 0)),
                   full((LANES, GLA_KEYS)), full((1, GLA_KEYS)), full((1, LANES))],
        out_shape=[jax.ShapeDtypeStruct((t, wout), F32), jax.ShapeDtypeStruct((LANES, GLA_KEYS), F32),
                   jax.ShapeDtypeStruct((1, GLA_KEYS), F32), jax.ShapeDtypeStruct((1, LANES), F32)],
        scratch_shapes=[pltpu.VMEM((2, LANES, 2 * LANES), F32)],
        compiler_params=_cparams("arbitrary"),
    )(proj, proj, proj, proj, proj, dcat, oraw, states, gate_up_pad, gate_bias, gnorm)


def _ln_stats(r):
    mu = jnp.mean(r, axis=1, keepdims=True)
    xc = r - mu
    rstd = lax.rsqrt(jnp.mean(xc * xc, axis=1, keepdims=True) + LN_EPS)
    return xc * rstd, rstd


def _ln_bwd(dy_g, xhat, rstd):
    return rstd * (dy_g - jnp.mean(dy_g, axis=1, keepdims=True) - xhat * jnp.mean(dy_g * xhat, axis=1, keepdims=True))


def _outproj_ln1(sb_o, gla_o, x, w_out, g1, b1, t, tm=256):
    def body(sb_ref, gl_ref, x_ref, w_ref, g_ref, b_ref, xhat_ref, rstd_ref, h_ref):
        mix = _dot(sb_ref[...], w_ref[0:SB_WIDTH, :]) + _dot(gl_ref[...], w_ref[SB_WIDTH:, :])
        xhat, rstd = _ln_stats(ALPHA * x_ref[...] + mix)
        xhat_ref[...] = xhat
        rstd_ref[...] = rstd
        h_ref[...] = (xhat * g_ref[...] + b_ref[...]).astype(BF16)

    row = lambda w: pl.BlockSpec((tm, w), lambda i: (i, 0))
    full = lambda shp: pl.BlockSpec(shp, lambda i: (0, 0))
    return pl.pallas_call(
        body, name="outproj_ln1", grid=(t // tm,),
        in_specs=[row(SB_WIDTH), row(GLA_WIDTH), row(D_MODEL), full((D_MODEL, D_MODEL)), full((1, D_MODEL)), full((1, D_MODEL))],
        out_specs=[row(D_MODEL), row(1), row(D_MODEL)],
        out_shape=[jax.ShapeDtypeStruct((t, D_MODEL), F32), jax.ShapeDtypeStruct((t, 1), F32),
                   jax.ShapeDtypeStruct((t, D_MODEL), BF16)],
        compiler_params=_cparams("parallel"),
    )(sb_o, gla_o, x, w_out, g1, b1)


_INV_SQRT2 = 1.0 / math.sqrt(2.0)
_INV_SQRT2PI = 1.0 / math.sqrt(2.0 * math.pi)


def _conv3(xs, w_ref, b_ref, half):
    return (w_ref[half, 0:1, :] * pltpu.roll(xs, 2, 0) + w_ref[half, 1:2, :] * pltpu.roll(xs, 1, 0)
            + w_ref[half, 2:3, :] * xs + b_ref[half])


HALO = 16


def _conv_gelu_fwd(up3, conv_w3, conv_b3, t, tr=512, ct=256):
    nct = D_FF // ct
    hb = tr // HALO

    def body(cur_ref, prev_ref, w_ref, b_ref, gm_ref):
        i = pl.program_id(1)
        keep = (i > 0).astype(F32)
        us = []
        for half in range(2):
            xs = jnp.concatenate([prev_ref[half].astype(F32) * keep, cur_ref[half].astype(F32)], axis=0)
            us.append(_conv3(xs, w_ref, b_ref, half)[HALO:, :])
        a, c = us
        gelu = 0.5 * a * (1.0 + lax.erf(a * _INV_SQRT2))
        gm_ref[...] = (gelu * c).astype(BF16)

    return pl.pallas_call(
        body, name="conv_gelu_fwd", grid=(nct, t // tr),
        in_specs=[pl.BlockSpec((2, tr, ct), lambda j, i: (0, i, j)),
                  pl.BlockSpec((2, HALO, ct), lambda j, i: (0, jnp.maximum(i * hb - 1, 0), j)),
                  pl.BlockSpec((2, 3, ct), lambda j, i: (0, 0, j)),
                  pl.BlockSpec((2, 1, ct), lambda j, i: (0, 0, j))],
        out_specs=pl.BlockSpec((tr, ct), lambda j, i: (i, j)),
        out_shape=jax.ShapeDtypeStruct((t, D_FF), BF16),
        compiler_params=_cparams("parallel", "parallel"),
    )(up3, up3, conv_w3, conv_b3)


def _conv_gelu_bwd(up3, dgm, conv_w3, conv_b3, t, tr=512, ct=256):
    nct = D_FF // ct
    nrt = t // tr
    hb = tr // HALO
    n = tr + 2 * HALO
    lo, hi = HALO, tr + HALO

    def body(cur_ref, prev_ref, next_ref, dg_ref, dgn_ref, w_ref, b_ref, dup_ref, dcw_ref, dcb_ref):
        i = pl.program_id(1)

        @pl.when(i == 0)
        def _():
            dcw_ref[...] = jnp.zeros_like(dcw_ref)
            dcb_ref[...] = jnp.zeros_like(dcb_ref)

        keep_prev = (i > 0).astype(F32)
        keep_next = (i < nrt - 1).astype(F32)
        xs, xm1, xm2, us = [], [], [], []
        for half in range(2):
            x = jnp.concatenate([prev_ref[half].astype(F32) * keep_prev, cur_ref[half].astype(F32),
                                 next_ref[half].astype(F32)], axis=0)
            xs.append(x)
            xm1.append(pltpu.roll(x, 1, 0))
            xm2.append(pltpu.roll(x, 2, 0))
            us.append(w_ref[half, 0:1, :] * xm2[half] + w_ref[half, 1:2, :] * xm1[half]
                      + w_ref[half, 2:3, :] * x + b_ref[half])
        a, c = us
        dg = jnp.concatenate([jnp.zeros((HALO, ct), F32), dg_ref[...].astype(F32),
                              dgn_ref[...].astype(F32) * keep_next], axis=0)
        cdf = 0.5 * (1.0 + lax.erf(a * _INV_SQRT2))
        pdf = jnp.exp(-0.5 * a * a) * _INV_SQRT2PI
        dus = [dg * c * (cdf + a * pdf), dg * (a * cdf)]
        rid = lax.broadcasted_iota(jnp.int32, (8, 1), 0)
        for half in range(2):
            du = dus[half]
            dup = (w_ref[half, 2:3, :] * du + w_ref[half, 1:2, :] * pltpu.roll(du, n - 1, 0)
                   + w_ref[half, 0:1, :] * pltpu.roll(du, n - 2, 0))
            dup_ref[half] = dup[lo:hi, :].astype(BF16)
            duc = du[lo:hi, :]
            s0 = jnp.sum(duc * xm2[half][lo:hi, :], axis=0, keepdims=True)
            s1 = jnp.sum(duc * xm1[half][lo:hi, :], axis=0, keepdims=True)
            s2 = jnp.sum(duc * xs[half][lo:hi, :], axis=0, keepdims=True)
            dcw_ref[half] += jnp.where(rid == 0, s0, jnp.where(rid == 1, s1, jnp.where(rid == 2, s2, 0.0)))
            dcb_ref[half] += jnp.sum(duc, axis=0, keepdims=True)

    last = t // HALO - 1
    return pl.pallas_call(
        body, name="conv_gelu_bwd", grid=(nct, nrt),
        in_specs=[pl.BlockSpec((2, tr, ct), lambda j, i: (0, i, j)),
                  pl.BlockSpec((2, HALO, ct), lambda j, i: (0, jnp.maximum(i * hb - 1, 0), j)),
                  pl.BlockSpec((2, HALO, ct), lambda j, i: (0, jnp.minimum((i + 1) * hb, last), j)),
                  pl.BlockSpec((tr, ct), lambda j, i: (i, j)),
                  pl.BlockSpec((HALO, ct), lambda j, i: (jnp.minimum((i + 1) * hb, last), j)),
                  pl.BlockSpec((2, 3, ct), lambda j, i: (0, 0, j)),
                  pl.BlockSpec((2, 1, ct), lambda j, i: (0, 0, j))],
        out_specs=[pl.BlockSpec((2, tr, ct), lambda j, i: (0, i, j)),
                   pl.BlockSpec((2, 8, ct), lambda j, i: (0, 0, j)),
                   pl.BlockSpec((2, 1, ct), lambda j, i: (0, 0, j))],
        out_shape=[jax.ShapeDtypeStruct((2, t, D_FF), BF16), jax.ShapeDtypeStruct((2, 8, D_FF), F32),
                   jax.ShapeDtypeStruct((2, 1, D_FF), F32)],
        compiler_params=_cparams("parallel", "arbitrary"),
    )(up3, up3, up3, dgm, dgm, conv_w3, conv_b3)


def _down_ln2_loss(gm, w_down, xhat1, g1, b1, g2, b2, target, t, tm=256):
    def body(gm_ref, w_ref, xh_ref, g1_ref, b1_ref, g2_ref, b2_ref, tg_ref, dr_ref, loss_ref, dg_ref, db_ref):
        i = pl.program_id(0)

        @pl.when(i == 0)
        def _():
            loss_ref[...] = jnp.zeros_like(loss_ref)
            dg_ref[...] = jnp.zeros_like(dg_ref)
            db_ref[...] = jnp.zeros_like(db_ref)

        h = xh_ref[...] * g1_ref[...] + b1_ref[...]
        xhat, rstd = _ln_stats(ALPHA * h + _dot(gm_ref[...], w_ref[...]))
        err = xhat * g2_ref[...] + b2_ref[...] - tg_ref[...]
        loss_ref[...] += 0.5 * jnp.sum(jnp.sum(err * err, axis=1, keepdims=True), axis=0, keepdims=True) / D_MODEL
        dy = err * (1.0 / D_MODEL)
        dg_ref[...] += jnp.sum(dy * xhat, axis=0, keepdims=True)
        db_ref[...] += jnp.sum(dy, axis=0, keepdims=True)
        dr_ref[...] = _ln_bwd(dy * g2_ref[...], xhat, rstd)

    row = lambda w: pl.BlockSpec((tm, w), lambda i: (i, 0))
    full = lambda shp: pl.BlockSpec(shp, lambda i: (0, 0))
    vec = full((1, D_MODEL))
    return pl.pallas_call(
        body, name="down_ln2_loss", grid=(t // tm,),
        in_specs=[row(D_FF), full((D_FF, D_MODEL)), row(D_MODEL), vec, vec, vec, vec, row(D_MODEL)],
        out_specs=[row(D_MODEL), full((1, 1)), vec, vec],
        out_shape=[jax.ShapeDtypeStruct((t, D_MODEL), F32), jax.ShapeDtypeStruct((1, 1), F32),
                   jax.ShapeDtypeStruct((1, D_MODEL), F32), jax.ShapeDtypeStruct((1, D_MODEL), F32)],
        compiler_params=_cparams("arbitrary"),
    )(gm, w_down, xhat1, g1, b1, g2, b2, target)


def _dh_ln1_bwd(dup3, w_up4, dr2, xhat1, rstd1, g1, t, tm=512):
    tk = 2 * D_FF // 4
    nkh = D_FF // tk
    nk = 2 * nkh

    def body(a_ref, w_ref, dr2_ref, xh_ref, rs_ref, g_ref, dr1_ref, dg_ref, db_ref, acc_ref):
        i = pl.program_id(0)
        kk = pl.program_id(1)

        @pl.when((i == 0) & (kk == 0))
        def _():
            dg_ref[...] = jnp.zeros_like(dg_ref)
            db_ref[...] = jnp.zeros_like(db_ref)

        part = _dot_nt(a_ref[...], w_ref[...])

        @pl.when(kk == 0)
        def _():
            acc_ref[...] = part

        @pl.when(kk > 0)
        def _():
            acc_ref[...] += part

        @pl.when(kk == nk - 1)
        def _():
            dh = acc_ref[...] + ALPHA * dr2_ref[...]
            xhat = xh_ref[...]
            dg_ref[...] += jnp.sum(dh * xhat, axis=0, keepdims=True)
            db_ref[...] += jnp.sum(dh, axis=0, keepdims=True)
            dr1_ref[...] = _ln_bwd(dh * g_ref[...], xhat, rs_ref[...])

    row = lambda w: pl.BlockSpec((tm, w), lambda i, kk: (i, 0))
    vec = pl.BlockSpec((1, D_MODEL), lambda i, kk: (0, 0))
    return pl.pallas_call(
        body, name="dh_ln1_bwd", grid=(t // tm, nk),
        in_specs=[pl.BlockSpec((None, tm, tk), lambda i, kk: (kk // nkh, i, kk % nkh)),
                  pl.BlockSpec((None, D_MODEL, tk), lambda i, kk: (kk, 0, 0)),
                  row(D_MODEL), row(D_MODEL), row(1), vec],
        out_specs=[row(D_MODEL), vec, vec],
        out_shape=[jax.ShapeDtypeStruct((t, D_MODEL), F32), jax.ShapeDtypeStruct((1, D_MODEL), F32),
                   jax.ShapeDtypeStruct((1, D_MODEL), F32)],
        scratch_shapes=[pltpu.VMEM((tm, D_MODEL), F32)],
        compiler_params=_cparams("arbitrary", "arbitrary"),
    )(dup3, w_up4, dr2, xhat1, rstd1, g1)


def _adamw(w, g, m, v, name):
    rows, cols = w.shape
    tr = rows
    for cand in (256, 128, 64, 32, 16, 8):
        if rows % cand == 0 and rows > cand:
            tr = cand
            break
    c1 = 1.0 / (1.0 - ADAM_B1 ** ADAM_STEP)
    c2 = 1.0 / (1.0 - ADAM_B2 ** ADAM_STEP)

    def body(w_ref, g_ref, m_ref, v_ref, d_ref, nm_ref, nv_ref):
        gv = g_ref[...]
        nm = ADAM_B1 * m_ref[...] + (1.0 - ADAM_B1) * gv
        nv = ADAM_B2 * v_ref[...] + (1.0 - ADAM_B2) * (gv * gv)
        d_ref[...] = -ADAM_LR * ((nm * c1) / (jnp.sqrt(nv * c2) + ADAM_EPS) + ADAM_WD * w_ref[...])
        nm_ref[...] = nm
        nv_ref[...] = nv

    spec = pl.BlockSpec((tr, cols), lambda i: (i, 0))
    out = jax.ShapeDtypeStruct((rows, cols), F32)
    return pl.pallas_call(
        body, name=name, grid=(rows // tr,), in_specs=[spec] * 4, out_specs=[spec] * 3, out_shape=[out] * 3,
        compiler_params=_cparams("parallel"),
    )(w, g, m, v)


def _local_step(x, target, w_in_p, late_shards, gate_up_pad, gate_bias, gnorm, ln1_g, ln1_b, conv_w3, conv_b3,
                ln2_g, ln2_b, c_arr, kc_arr):
    t = x.shape[0]
    tq = min(t, 1024)
    proj, out_partly, down_partly = _mm(x, w_in_p, m=t, n=IN_PAD, k=D_MODEL, tm=tq, tn=640, tk=D_MODEL, name="proj",
                                        rider=_gather_rider(late_shards[1:], LATE_SHAPES[1:]))
    sb_o, up_partly = _sb_fwd(proj, t, _gather_rider(late_shards[:1], LATE_SHAPES[:1]))
    gla_o, oraw, states, w_up4, w_out4, w_down4 = _gla_fwd(
        proj, gate_up_pad, gate_bias, gnorm, t, _forward_rider([up_partly, out_partly, down_partly], LATE_SHAPES))
    w_out = w_out4.reshape(D_MODEL, D_MODEL)
    w_down = w_down4.reshape(D_FF, D_MODEL)
    xhat1, rstd1, h_bf = _outproj_ln1(sb_o, gla_o, x, w_out, ln1_g, ln1_b, t)
    up3 = _mm(h_bf, w_up4, m=t, n=2 * D_FF, k=D_MODEL, tm=tq, tn=W_UP_S, tk=D_MODEL, name="up",
              b_spec=pl.BlockSpec((None, D_MODEL, W_UP_S), lambda i, j, kk: (j, 0, 0)),
              o_spec=pl.BlockSpec((None, tq, W_UP_S), lambda i, j, kk: (j // 2, i, j % 2)),
              out_shape=jax.ShapeDtypeStruct((2, t, D_FF), BF16), out_dtype=BF16)
    gm = _conv_gelu_fwd(up3, conv_w3, conv_b3, t, tr=tq)
    dr2, loss, dln2_g, dln2_b = _down_ln2_loss(gm, w_down, xhat1, ln1_g, ln1_b, ln2_g, ln2_b, target, t)
    dgm = _mm(dr2, w_down, m=t, n=D_FF, k=D_MODEL, tm=tq, tn=W_UP_S, tk=D_MODEL, tb=True, out_dtype=BF16, name="dgm")
    dw_down = _mm(gm, dr2, m=D_FF, n=D_MODEL, k=t, tm=W_UP_S, tn=D_MODEL, tk=tq, ta=True, name="dw_down")
    dup3, dcw, dcb = _conv_gelu_bwd(up3, dgm, conv_w3, conv_b3, t)
    dr1, dln1_g, dln1_b = _dh_ln1_bwd(dup3, w_up4, dr2, xhat1, rstd1, ln1_g, t)
    dw_up4 = _mm(h_bf, dup3, m=D_MODEL, n=2 * D_FF, k=t, tm=512, tn=W_UP_S, tk=t, ta=True, name="dw_up",
                 b_spec=pl.BlockSpec((None, t, W_UP_S), lambda i, j, kk: (j // 2, kk, j % 2)),
                 o_spec=pl.BlockSpec((None, 512, W_UP_S), lambda i, j, kk: (j, i, 0)),
                 out_shape=jax.ShapeDtypeStruct((4, D_MODEL, W_UP_S), F32))
    dw_out_sb = _mm(sb_o, dr1, m=SB_WIDTH, n=D_MODEL, k=t, tm=512, tn=D_MODEL, tk=tq, ta=True, name="dw_out_sb")
    dw_out_gla = _mm(gla_o, dr1, m=GLA_WIDTH, n=D_MODEL, k=t, tm=512, tn=D_MODEL, tk=tq, ta=True, name="dw_out_gla")
    gs = [dw_up4, jnp.concatenate([dw_out_sb, dw_out_gla], axis=0).reshape(4, W_OUT_S, D_MODEL),
          dw_down.reshape(4, W_DOWN_S, D_MODEL)]
    dcat, *from_sib = _mm(dr1, w_out, m=t, n=D_MODEL, k=D_MODEL, tm=tq, tn=512, tk=D_MODEL, tb=True, name="dcat",
                          rider=_sibling_rider(gs, LATE_SHAPES))
    ps = [_add_sibling(gs[m], from_sib[m], c_arr, LATE_ADD_ROWS[m], "add_sibling_late_%d" % m) for m in range(3)]
    dsq, dsk, dsv, *others = _sb_bwd(proj, dcat, t, _chips_rider(ps))
    late_sums = [_add_chips(ps[m], others[m], kc_arr, LATE_ADD_ROWS[m], "add_chips_late_%d" % m) for m in range(3)]
    dgla, dgup_pad, dgbias, dgnorm = _gla_bwd(proj, dcat, oraw, states, gate_up_pad, gate_bias, gnorm, t)
    dproj = jnp.concatenate([dsq, dsk, dsv, dgla], axis=1).astype(BF16)
    dw_in_p = _mm(x, dproj, m=D_MODEL, n=IN_PAD, k=t, tm=512, tn=IN_PAD, tk=tq, ta=True, name="dw_in")
    small = dict(
        gate_up=dgup_pad[:GATE_RANK], gate_bias=dgbias, gla_norm_g=dgnorm, ln1_g=dln1_g, ln1_b=dln1_b,
        conv_w=jnp.concatenate([dcw[0, :3], dcw[1, :3]], axis=1), conv_b=jnp.concatenate([dcb[0], dcb[1]], axis=1),
        ln2_g=dln2_g, ln2_b=dln2_b, loss=loss)
    g_in4 = jnp.stack([dw_in_p[:, k * W_IN_S:(k + 1) * W_IN_S] for k in range(4)], axis=0)
    from_sib_in, = _run(_sibling_rider([g_in4], SHARD_SHAPES[:1]), "exchange_sibling_w_in")
    p_in = _add_sibling(g_in4, from_sib_in, c_arr, ADD_ROWS[0], "add_sibling_w_in")
    dx, others_in, vecs = _mm(dproj, w_in_p, m=t, n=D_MODEL, k=IN_PAD, tm=512, tn=D_MODEL, tk=IN_PAD, tb=True, add=dr1,
                              add_scale=ALPHA, name="dx",
                              rider=_chips_rider([p_in], _pack_vec(small, SMALL_GRADS, GRAD_VEC_ROWS)))
    sum_in = _add_chips(p_in, others_in, kc_arr, ADD_ROWS[0], "add_chips_w_in")
    return dx, [sum_in] + late_sums, vecs


W_IN_S, W_UP_S, W_OUT_S, W_DOWN_S = IN_WIDTH // 4, 2 * D_FF // 4, D_MODEL // 4, D_FF // 4
SHARD_SHAPES = ((D_MODEL, W_IN_S), (D_MODEL, W_UP_S), (W_OUT_S, D_MODEL), (W_DOWN_S, D_MODEL))
ADD_ROWS = (256, 256, 128, 176)
LATE_SHAPES, LATE_ADD_ROWS = SHARD_SHAPES[1:], ADD_ROWS[1:]
SMALL_ROWS = 8
VEC_SIZES = (("gate_bias", GLA_KEYS), ("gla_norm_g", LANES), ("ln1_g", D_MODEL), ("ln1_b", D_MODEL),
             ("conv_b", 2 * D_FF), ("ln2_g", D_MODEL), ("ln2_b", D_MODEL))
SMALL_GRADS = VEC_SIZES + (("conv_w", 3 * 2 * D_FF), ("gate_up", GATE_RANK * GLA_KEYS), ("loss", 1))


def _rows(a):
    flat = a.reshape(-1)
    pad = (-flat.shape[0]) % D_MODEL
    if pad:
        flat = jnp.pad(flat, (0, pad))
    return flat.reshape(-1, D_MODEL)


def _pad_rows(a, rows):
    return jnp.pad(a, ((0, rows - a.shape[0]), (0, 0)))


def _pack_vec(d, sizes, rows):
    flat = jnp.concatenate([d[n].reshape(-1) for n, _ in sizes])
    return _pad_rows(_rows(flat), rows)


def _unpack_vec(v, sizes):
    flat = v.reshape(-1)
    out, o = {}, 0
    for n, size in sizes:
        out[n] = flat[o:o + size].reshape(1, size)
        o += size
    return out


VEC_ROWS = 16
GRAD_VEC_ROWS = 32


HBM_SPEC = pl.BlockSpec(memory_space=pltpu.HBM)


def _position():
    x, y, c = lax.axis_index("x"), lax.axis_index("y"), lax.axis_index("c")
    chips = [(1 - x, y), (x, 1 - y), (1 - x, 1 - y)]
    return x, y, c, chips


def _remote(src, dst, send_sems, recv_sems, k, to):
    return pltpu.make_async_remote_copy(src_ref=src, dst_ref=dst, send_sem=send_sems.at[k], recv_sem=recv_sems.at[k],
                                        device_id=to, device_id_type=MESH)


def _gather_ici(in_refs, out_refs, shapes, send_sems, recv_sems, local_sems):
    x, y, c, chips = _position()
    k_me = 2 * x + y
    local, sends, recvs = [], [], []
    for m, (rows, _) in enumerate(shapes):
        h = rows // 2
        local.append(pltpu.make_async_copy(in_refs[m], out_refs[m].at[k_me], local_sems.at[m]))
        for j, (cx, cy) in enumerate(chips):
            sends.append(_remote(in_refs[m].at[pl.ds(c * h, h), :], out_refs[m].at[k_me, pl.ds(c * h, h), :],
                                 send_sems, recv_sems, 3 * m + j, (cx, cy, c)))
            landed = out_refs[m].at[2 * cx + cy, pl.ds(c * h, h), :]
            recvs.append(_remote(landed, landed, send_sems, recv_sems, 3 * m + j, (x, y, c)))
    return local, sends, recvs


def _gather_d2d(src_refs, dst_refs, shapes, send_sems, recv_sems, base):
    x, y, c, chips = _position()
    sends, recvs = [], []
    for m, (rows, _) in enumerate(shapes):
        h = rows // 2
        for j, (cx, cy) in enumerate(chips):
            k = 2 * cx + cy
            sends.append(_remote(src_refs[m].at[k, pl.ds(c * h, h), :], dst_refs[m].at[k, pl.ds(c * h, h), :],
                                 send_sems, recv_sems, base + 3 * m + j, (x, y, 1 - c)))
            landed = dst_refs[m].at[k, pl.ds((1 - c) * h, h), :]
            recvs.append(_remote(landed, landed, send_sems, recv_sems, base + 3 * m + j, (x, y, c)))
    return sends, recvs


def _gather_weights(shards, small, shapes):
    nm = len(shards)
    n_ici = 3 * nm

    def body(*refs):
        in_refs, small_ref = refs[:nm], refs[nm]
        out_refs, osm_ref = refs[nm + 1:2 * nm + 1], refs[2 * nm + 1]
        send_sems, recv_sems, local_sems = refs[2 * nm + 2:]
        x, y, c, chips = _position()
        k_me = 2 * x + y
        local, sends, recvs = _gather_ici(in_refs, out_refs, shapes, send_sems, recv_sems, local_sems)
        local.append(pltpu.make_async_copy(small_ref, osm_ref.at[k_me], local_sems.at[nm]))
        for j, (cx, cy) in enumerate(chips):
            sends.append(_remote(small_ref, osm_ref.at[k_me], send_sems, recv_sems, n_ici + j, (cx, cy, c)))
        for cp in local + sends:
            cp.start()
        for cp in recvs:
            cp.wait_recv()
        fsends, frecvs = _gather_d2d(out_refs, out_refs, shapes, send_sems, recv_sems, n_ici + 3)
        for cp in fsends:
            cp.start()
        for j, (cx, cy) in enumerate(chips):
            k = 2 * cx + cy
            frecvs.append(_remote(osm_ref.at[k], osm_ref.at[k], send_sems, recv_sems, n_ici + j, (x, y, c)))
        for cp in frecvs:
            cp.wait_recv()
        for cp in sends + fsends:
            cp.wait_send()
        for cp in local:
            cp.wait()

    n_sems = 2 * n_ici + 3
    return pl.pallas_call(
        body, name="gather_weights", in_specs=[HBM_SPEC] * (nm + 1), out_specs=[HBM_SPEC] * (nm + 1),
        out_shape=[jax.ShapeDtypeStruct((4,) + s, BF16) for s in shapes]
        + [jax.ShapeDtypeStruct((4, SMALL_ROWS, D_MODEL), F32)],
        scratch_shapes=[pltpu.SemaphoreType.DMA((n_sems,)), pltpu.SemaphoreType.DMA((n_sems,)),
                        pltpu.SemaphoreType.DMA((nm + 1,))],
    )(*shards, small)


def _gather_rider(shards, shapes):
    n = len(shards)
    return _Rider(shards, [jax.ShapeDtypeStruct((4,) + s, BF16) for s in shapes], (3 * n, 3 * n, n),
                  lambda ins, outs, sems: _gather_ici(ins, outs, shapes, *sems))


def _forward_rider(gathered, shapes):
    n = len(gathered)
    return _Rider(gathered, [jax.ShapeDtypeStruct(a.shape, a.dtype) for a in gathered], (3 * n, 3 * n),
                  lambda ins, outs, sems: ([],) + _gather_d2d(ins, outs, shapes, sems[0], sems[1], 0))


def _sibling_rider(gs, shapes):
    def copies(ins, outs, sems):
        x, y, c, _ = _position()
        both = []
        for m, (rows, _) in enumerate(shapes):
            h = rows // 2
            for k in range(4):
                both.append(_remote(ins[m].at[k, pl.ds((1 - c) * h, h), :], outs[m].at[k], sems[0], sems[1],
                                    4 * m + k, (x, y, 1 - c)))
        return [], both, both

    return _Rider(gs, [jax.ShapeDtypeStruct((4, r // 2, cl), F32) for r, cl in shapes], (4 * len(gs), 4 * len(gs)), copies)


def _add_sibling(g, r, c_arr, tr, name):
    _, rows, cols = g.shape
    nb = rows // 2 // tr

    def body(c_ref, g_ref, r_ref, o_ref):
        o_ref[...] = (g_ref[...] + r_ref[...]).astype(BF16)

    spec = pl.BlockSpec((None, tr, cols), lambda k, i, c: (k, i, 0))
    return pl.pallas_call(
        body, name=name,
        grid_spec=pltpu.PrefetchScalarGridSpec(
            num_scalar_prefetch=1, grid=(4, nb),
            in_specs=[pl.BlockSpec((None, tr, cols), lambda k, i, c: (k, c[0] * nb + i, 0)), spec], out_specs=spec),
        out_shape=jax.ShapeDtypeStruct((4, rows // 2, cols), BF16), compiler_params=_cparams("parallel", "parallel"),
    )(c_arr, g, r)


def _reduce_ici(p_refs, r_refs, send_sems, recv_sems):
    x, y, c, chips = _position()
    sends, recvs = [], []
    for m in range(len(p_refs)):
        for j, (cx, cy) in enumerate(chips):
            sends.append(_remote(p_refs[m].at[2 * cx + cy], r_refs[m].at[j], send_sems, recv_sems, 3 * m + j, (cx, cy, c)))
            recvs.append(_remote(r_refs[m].at[j], r_refs[m].at[j], send_sems, recv_sems, 3 * m + j, (x, y, c)))
    return sends, recvs


def _chips_rider(ps, vec=None):
    nm = len(ps)
    n_ici = 3 * nm

    def copies(ins, outs, sems):
        sends, recvs = _reduce_ici(ins[:nm], outs[:nm], sems[0], sems[1])
        if vec is None:
            return [], sends, recvs
        x, y, c, _ = _position()
        my_id = 4 * x + 2 * y + c
        vec_ref, vrecv_ref = ins[nm], outs[nm]
        local = [pltpu.make_async_copy(vec_ref, vrecv_ref.at[my_id], sems[2].at[0])]
        for r in range(1, 8):
            peer = (1 - x if r & 4 else x, 1 - y if r & 2 else y, 1 - c if r & 1 else c)
            sends.append(_remote(vec_ref, vrecv_ref.at[my_id], sems[0], sems[1], n_ici + r - 1, peer))
            recvs.append(_remote(vec_ref, vrecv_ref.at[0], sems[0], sems[1], n_ici + r - 1, (x, y, c)))
        return local, sends, recvs

    out_shapes = [jax.ShapeDtypeStruct((3,) + p.shape[1:], p.dtype) for p in ps]
    if vec is None:
        return _Rider(ps, out_shapes, (n_ici, n_ici), copies)
    return _Rider(list(ps) + [vec], out_shapes + [jax.ShapeDtypeStruct((8, GRAD_VEC_ROWS, D_MODEL), F32)],
                  (n_ici + 7, n_ici + 7, 1), copies)


def _add_chips(p, r, kc_arr, tr, name):
    _, h, cols = p.shape
    nb = h // tr

    def body(kc_ref, p_ref, r0_ref, r1_ref, r2_ref, o_ref):
        o_ref[...] = ((p_ref[...].astype(F32) + r0_ref[...].astype(F32)) + r1_ref[...].astype(F32)) + r2_ref[...].astype(F32)

    rspec = lambda j: pl.BlockSpec((None, tr, cols), lambda i, kc: (j, i, 0))
    return pl.pallas_call(
        body, name=name,
        grid_spec=pltpu.PrefetchScalarGridSpec(
            num_scalar_prefetch=1, grid=(nb,),
            in_specs=[pl.BlockSpec((None, tr, cols), lambda i, kc: (kc[0], i, 0)), rspec(0), rspec(1), rspec(2)],
            out_specs=pl.BlockSpec((tr, cols), lambda i, kc: (kc[1] * nb + i, 0))),
        out_shape=jax.ShapeDtypeStruct((2 * h, cols), F32), compiler_params=_cparams("parallel"),
    )(kc_arr, p, r, r, r)


def _reunite_sibling(fs, shapes):
    n_chunks = 2
    nm = len(fs)

    def body(*refs):
        in_refs, out_refs = refs[:nm], refs[nm:2 * nm]
        send_sems, recv_sems = refs[2 * nm:]
        x, y, c, _ = _position()
        sends, recvs = [], []
        for m in range(nm):
            ch = shapes[m][0] // 2 // n_chunks
            for q in range(n_chunks):
                mine = pl.ds((c * n_chunks + q) * ch, ch)
                theirs = pl.ds(((1 - c) * n_chunks + q) * ch, ch)
                s = n_chunks * m + q
                sends.append(_remote(in_refs[m].at[mine, :], out_refs[m].at[mine, :], send_sems, recv_sems, s, (x, y, 1 - c)))
                recvs.append(_remote(in_refs[m].at[theirs, :], out_refs[m].at[theirs, :], send_sems, recv_sems, s, (x, y, c)))
        for cp in sends:
            cp.start()
        for cp in recvs:
            cp.wait_recv()
        for cp in sends:
            cp.wait_send()

    n_sems = n_chunks * nm
    return pl.pallas_call(
        body, name="reunite_sibling", in_specs=[HBM_SPEC] * nm, out_specs=[HBM_SPEC] * nm,
        out_shape=[jax.ShapeDtypeStruct(s, F32) for s in shapes],
        input_output_aliases={m: m for m in range(nm)},
        scratch_shapes=[pltpu.SemaphoreType.DMA((n_sems,)), pltpu.SemaphoreType.DMA((n_sems,))],
    )(*fs)


def _sum_vecs(v):
    def body(v_ref, o_ref):
        acc = v_ref[0]
        for d in range(1, 8):
            acc = acc + v_ref[d]
        o_ref[...] = acc

    return pl.pallas_call(body, name="sum_vecs", out_shape=jax.ShapeDtypeStruct(v.shape[1:], F32))(v)


def kernel(x, w_in, gate_up, gate_bias, gla_norm_g, w_out, ln1_g, ln1_b, w_up, conv_w, conv_b, w_down, ln2_g, ln2_b, loss_target, m_w_in, m_gate_up, m_gate_bias, m_gla_norm_g, m_w_out, m_ln1_g, m_ln1_b, m_w_up, m_conv_w, m_conv_b, m_w_down, m_ln2_g, m_ln2_b, v_w_in, v_gate_up, v_gate_bias, v_gla_norm_g, v_w_out, v_ln1_g, v_ln1_b, v_w_up, v_conv_w, v_conv_b, v_w_down, v_ln2_g, v_ln2_b):
    xi, yi, ci = lax.axis_index("x"), lax.axis_index("y"), lax.axis_index("c")
    k_me = 2 * xi + yi
    c_arr = ci.astype(jnp.int32).reshape(1)
    kc_arr = jnp.stack([k_me, ci]).astype(jnp.int32)
    small = _pad_rows(jnp.concatenate([_rows(conv_w[0]), _rows(gate_up[0])], axis=0), SMALL_ROWS)
    w_in4, gsmall = _gather_weights([w_in[0].astype(BF16)], small, SHARD_SHAPES[:1])
    late_shards = [w_up[0].astype(BF16), w_out[0].astype(BF16), w_down[0].astype(BF16)]
    w_in_p = jnp.pad(jnp.concatenate([w_in4[k] for k in range(4)], axis=1), ((0, 0), (0, IN_PAD - IN_WIDTH)))
    conv_w_f = jnp.concatenate([gsmall[k, :5].reshape(-1)[:3 * W_UP_S].reshape(3, W_UP_S) for k in range(4)], axis=1)
    gate_up_f = jnp.concatenate([gsmall[k, 5].reshape(GATE_RANK, GLA_KEYS // 4) for k in range(4)], axis=1)
    conv_w3 = jnp.transpose(conv_w_f.reshape(3, 2, D_FF), (1, 0, 2))
    conv_b3 = conv_b.reshape(2, 1, D_FF)
    gate_up_pad = jnp.pad(gate_up_f, ((0, LANES - GATE_RANK), (0, 0)))

    dx, sums, vecs = _local_step(
        x[0], loss_target[0], w_in_p, late_shards, gate_up_pad, gate_bias, gla_norm_g, ln1_g, ln1_b, conv_w3, conv_b3,
        ln2_g, ln2_b, c_arr, kc_arr)
    g_w_in, g_w_up, g_w_out, g_w_down = _reunite_sibling(sums, SHARD_SHAPES)
    gsmall_sum = _unpack_vec(_sum_vecs(vecs), SMALL_GRADS)
    g_conv_w = lax.dynamic_slice_in_dim(gsmall_sum["conv_w"].reshape(3, 2 * D_FF), k_me * W_UP_S, W_UP_S, axis=1)
    g_gate_up = lax.dynamic_slice_in_dim(gsmall_sum["gate_up"].reshape(GATE_RANK, GLA_KEYS), k_me * (GLA_KEYS // 4),
                                         GLA_KEYS // 4, axis=1)
    gv = gsmall_sum
    loss = gv["loss"][0, 0]
    gvec = _pack_vec(gv, VEC_SIZES, VEC_ROWS)

    grads = dict(w_in=g_w_in[None], gate_up=g_gate_up[None], gate_bias=gv["gate_bias"], gla_norm_g=gv["gla_norm_g"],
                 w_out=g_w_out[None], ln1_g=gv["ln1_g"], ln1_b=gv["ln1_b"], w_up=g_w_up[None], conv_w=g_conv_w[None],
                 conv_b=gv["conv_b"], w_down=g_w_down[None], ln2_g=gv["ln2_g"], ln2_b=gv["ln2_b"])
    weights = dict(w_in=w_in, gate_up=gate_up, gate_bias=gate_bias, gla_norm_g=gla_norm_g, w_out=w_out, ln1_g=ln1_g,
                   ln1_b=ln1_b, w_up=w_up, conv_w=conv_w, conv_b=conv_b, w_down=w_down, ln2_g=ln2_g, ln2_b=ln2_b)
    ms = dict(w_in=m_w_in, gate_up=m_gate_up, gate_bias=m_gate_bias, gla_norm_g=m_gla_norm_g, w_out=m_w_out, ln1_g=m_ln1_g,
              ln1_b=m_ln1_b, w_up=m_w_up, conv_w=m_conv_w, conv_b=m_conv_b, w_down=m_w_down, ln2_g=m_ln2_g, ln2_b=m_ln2_b)
    vs = dict(w_in=v_w_in, gate_up=v_gate_up, gate_bias=v_gate_bias, gla_norm_g=v_gla_norm_g, w_out=v_w_out, ln1_g=v_ln1_g,
              ln1_b=v_ln1_b, w_up=v_w_up, conv_w=v_conv_w, conv_b=v_conv_b, w_down=v_w_down, ln2_g=v_ln2_g, ln2_b=v_ln2_b)
    names = ["w_in", "gate_up", "gate_bias", "gla_norm_g", "w_out", "ln1_g", "ln1_b", "w_up", "conv_w", "conv_b", "w_down",
             "ln2_g", "ln2_b"]
    delta, new_m, new_v = {}, {}, {}
    for n in ("w_in", "gate_up", "w_out", "w_up", "conv_w", "w_down"):
        tr_ = jnp.transpose if n == "w_in" else (lambda a: a)
        d, nm, nv = _adamw(tr_(weights[n][0]), tr_(grads[n][0]), tr_(ms[n][0]), tr_(vs[n][0]), "adamw_" + n)
        delta[n], new_m[n], new_v[n] = tr_(d)[None], tr_(nm)[None], tr_(nv)[None]
    d, nm, nv = _adamw(_pack_vec(weights, VEC_SIZES, VEC_ROWS), gvec, _pack_vec(ms, VEC_SIZES, VEC_ROWS),
                       _pack_vec(vs, VEC_SIZES, VEC_ROWS), "adamw_vectors")
    for dst, src in ((delta, d), (new_m, nm), (new_v, nv)):
        dst.update(_unpack_vec(src, VEC_SIZES))
    return (loss, dx[None], *[grads[n] for n in names], *[delta[n] for n in names], *[new_m[n] for n in names],
            *[new_v[n] for n in names])
```

```python
import functools
import math

import jax
import jax.numpy as jnp
from jax import lax
from jax.experimental import pallas as pl
from jax.experimental.pallas import tpu as pltpu

F32 = jnp.float32
BF16 = jnp.bfloat16

D_MODEL = 1024
SB_WIDTH = 512
GLA_KEYS = 256
GLA_WIDTH = 512
GATE_RANK = 16
IN_WIDTH = 3088
IN_PAD = 3200
D_FF = 2816
CHUNK = 64
LN_EPS = 1e-5
RMS_EPS = 1e-6
ALPHA = 2.0 ** 0.25
GLA_TAU = 16.0
SB_SCALE = 0.125
GLA_SCALE = 0.125
LANES = 128
SB_BLK = 256
SB_CUT = -100.0
GLA_ROWS = 256
VMEM_LIMIT = 56 * 1024 * 1024

ADAM_LR, ADAM_B1, ADAM_B2, ADAM_EPS, ADAM_WD, ADAM_STEP = 0.001, 0.9, 0.999, 1e-08, 0.01, 10

MESH = pl.DeviceIdType.MESH


def _cparams(*sem):
    return pltpu.CompilerParams(dimension_semantics=sem, vmem_limit_bytes=VMEM_LIMIT)


def _dot(a, b):
    return jnp.dot(a, b, preferred_element_type=F32)


def _dot_nt(a, b):
    return lax.dot_general(a, b, (((1,), (1,)), ((), ())), preferred_element_type=F32)


def _dot_tn(a, b):
    return lax.dot_general(a, b, (((0,), (0,)), ((), ())), preferred_element_type=F32)


def _split3(x):
    hi = x.astype(BF16)
    r = x - hi.astype(F32)
    mid = r.astype(BF16)
    lo = (r - mid.astype(F32)).astype(BF16)
    return hi, mid, lo


def _softplus(z):
    return jnp.maximum(z, 0.0) + jnp.log(1.0 + jnp.exp(-jnp.abs(z)))


def _sigmoid(z):
    return 1.0 / (1.0 + jnp.exp(-z))


def _mm(a, b, *, m, n, k, tm, tn, tk, ta=False, tb=False, a_spec=None, b_spec=None, o_spec=None,
        out_shape=None, out_dtype=F32, add=None, add_scale=1.0, rider=None, name):
    nk = k // tk
    dn = (((0 if ta else 1,), (1 if tb else 0,)), ((), ()))
    n_in = len(rider.inputs) if rider else 0
    n_out = len(rider.out_shapes) if rider else 0
    n_add = int(add is not None)
    steps = (m // tm, n // tn, nk)

    def body(*refs):
        a_ref, b_ref = refs[:2]
        add_ref = refs[2] if add is not None else None
        rin = refs[2 + n_add:2 + n_add + n_in]
        o_ref = refs[2 + n_add + n_in]
        rout = refs[3 + n_add + n_in:3 + n_add + n_in + n_out]
        scratch = refs[3 + n_add + n_in + n_out:]
        if rider:
            sems = scratch[len(scratch) - len(rider.sems):]
            ids = [pl.program_id(d) for d in range(3)]
            _ride(rider, rin, rout, sems, (ids[0] == 0) & (ids[1] == 0) & (ids[2] == 0),
                  (ids[0] == steps[0] - 1) & (ids[1] == steps[1] - 1) & (ids[2] == steps[2] - 1))

        part = lax.dot_general(a_ref[...].astype(BF16), b_ref[...].astype(BF16), dn, preferred_element_type=F32)

        def finish(r):
            if add is not None:
                r = r + add_scale * add_ref[...]
            o_ref[...] = r.astype(out_dtype)

        if nk == 1:
            finish(part)
            return
        acc_ref = scratch[0]
        kk = pl.program_id(2)

        @pl.when(kk == 0)
        def _():
            acc_ref[...] = part

        @pl.when((kk > 0) & (kk < nk - 1))
        def _():
            acc_ref[...] += part

        @pl.when(kk == nk - 1)
        def _():
            finish(acc_ref[...] + part)

    if a_spec is None:
        a_spec = pl.BlockSpec((tk, tm), lambda i, j, kk: (kk, i)) if ta else pl.BlockSpec((tm, tk), lambda i, j, kk: (i, kk))
    if b_spec is None:
        b_spec = pl.BlockSpec((tn, tk), lambda i, j, kk: (j, kk)) if tb else pl.BlockSpec((tk, tn), lambda i, j, kk: (kk, j))
    if o_spec is None:
        o_spec = pl.BlockSpec((tm, tn), lambda i, j, kk: (i, j))
    if out_shape is None:
        out_shape = jax.ShapeDtypeStruct((m, n), out_dtype)
    in_specs = [a_spec, b_spec]
    args = [a, b]
    if add is not None:
        in_specs.append(pl.BlockSpec((tm, tn), lambda i, j, kk: (i, j)))
        args.append(add)
    scratch = [pltpu.VMEM((tm, tn), F32)] if nk > 1 else []
    if not rider:
        return pl.pallas_call(
            body, name=name, grid=steps, in_specs=in_specs, out_specs=o_spec, out_shape=out_shape,
            scratch_shapes=scratch, compiler_params=_cparams("parallel", "parallel", "arbitrary"),
        )(*args)
    return pl.pallas_call(
        body, name=name, grid=steps, in_specs=in_specs + [HBM_SPEC] * n_in, out_specs=[o_spec] + [HBM_SPEC] * n_out,
        out_shape=[out_shape] + list(rider.out_shapes),
        input_output_aliases={len(args) + i: 1 + o for i, o in rider.aliases},
        scratch_shapes=scratch + [pltpu.SemaphoreType.DMA((s,)) for s in rider.sems],
        compiler_params=_cparams("arbitrary", "arbitrary", "arbitrary"),
    )(*args, *rider.inputs)


class _Rider:
    def __init__(self, inputs, out_shapes, sems, copies, aliases=()):
        self.inputs, self.out_shapes, self.sems, self.copies = list(inputs), list(out_shapes), tuple(sems), copies
        self.aliases = tuple(aliases)


def _join(a, b):
    na_in, na_out, na_sems = len(a.inputs), len(a.out_shapes), len(a.sems)

    def copies(ins, outs, sems):
        first = a.copies(ins[:na_in], outs[:na_out], sems[:na_sems])
        second = b.copies(ins[na_in:], outs[na_out:], sems[na_sems:])
        return tuple(u + v for u, v in zip(first, second))

    return _Rider(a.inputs + b.inputs, a.out_shapes + b.out_shapes, a.sems + b.sems, copies,
                  a.aliases + tuple((i + na_in, o + na_out) for i, o in b.aliases))


def _ride(rider, in_refs, out_refs, sems, first, last):
    @pl.when(first)
    def _():
        local, sends, _ = rider.copies(in_refs, out_refs, sems)
        for cp in local + sends:
            cp.start()

    @pl.when(last)
    def _():
        local, sends, recvs = rider.copies(in_refs, out_refs, sems)
        for cp in recvs:
            cp.wait_recv()
        for cp in sends:
            cp.wait_send()
        for cp in local:
            cp.wait()


def _run(rider, name):
    n_in, n_out = len(rider.inputs), len(rider.out_shapes)

    def body(*refs):
        local, sends, recvs = rider.copies(refs[:n_in], refs[n_in:n_in + n_out], refs[n_in + n_out:])
        for cp in local + sends:
            cp.start()
        for cp in recvs:
            cp.wait_recv()
        for cp in sends:
            cp.wait_send()
        for cp in local:
            cp.wait()

    return pl.pallas_call(
        body, name=name, in_specs=[HBM_SPEC] * n_in, out_specs=[HBM_SPEC] * n_out, out_shape=list(rider.out_shapes),
        scratch_shapes=[pltpu.SemaphoreType.DMA((s,)) for s in rider.sems],
    )(*rider.inputs)


def _sb_tile(qh, kj, diag, strict, u_strict, r_in):
    z = _dot_nt(qh, kj)
    sp = _softplus(z)
    l1m = -sp
    lsz = z - sp
    if diag:
        l1m = jnp.where(strict, l1m, 0.0)
    cs = _dot(l1m.astype(BF16), u_strict) + r_in
    w = jnp.exp(lsz + cs)
    if diag:
        w = jnp.where(strict, w, 0.0)
    return l1m, lsz, w


def _sb_consts():
    row = lax.broadcasted_iota(jnp.int32, (SB_BLK, SB_BLK), 0)
    col = lax.broadcasted_iota(jnp.int32, (SB_BLK, SB_BLK), 1)
    strict = col < row
    u_strict = (row > col).astype(BF16)
    u_pre = (row < col).astype(BF16)
    lane = lax.broadcasted_iota(jnp.int32, (1, LANES), 1)
    return strict, u_strict, u_pre, lane


def _sb_fwd(proj, t, rider):
    nq = t // SB_BLK
    n_in, n_out = len(rider.inputs), len(rider.out_shapes)

    def body(q_ref, k_ref, v_ref, *rest):
        rin, o_ref, sv_ref, rout = rest[:n_in], rest[n_in], rest[n_in + 1], rest[n_in + 2:n_in + 2 + n_out]
        p = pl.program_id(0)
        i = pl.program_id(1)
        _ride(rider, rin, rout, rest[n_in + 2 + n_out:], (p == 0) & (i == 0), (p == 3) & (i == nq - 1))

        strict, u_strict, _, lane = _sb_consts()
        qf = q_ref[...] * SB_SCALE
        hms = [(lane // 64) == hh for hh in range(2)]
        qhs = [jnp.where(hm, qf, 0.0).astype(BF16) for hm in hms]

        def step(j, r0, r1, a, sv, diag):
            rows = pl.ds(pl.multiple_of(j * SB_BLK, SB_BLK), SB_BLK)
            kj = k_ref[rows, :].astype(BF16)
            vf = v_ref[rows, :]
            rs = []
            for hh, r in enumerate((r0, r1)):
                l1m, _, w = _sb_tile(qhs[hh], kj, diag, strict, u_strict, r)
                a = a + _dot(w.astype(BF16), jnp.where(hms[hh], vf, 0.0).astype(BF16))
                rs.append(r + jnp.sum(l1m, axis=1, keepdims=True))
                sv = jnp.where(lane == hh * 64 + (i - j + 1), rs[hh], sv)
            return rs[0], rs[1], a, sv

        zero = jnp.zeros((SB_BLK, 1), F32)
        acc0 = jnp.zeros((SB_BLK, LANES), F32)
        r0, r1, acc, sv = step(i, zero, zero, acc0, acc0, True)
        _, _, _, acc, sv = lax.while_loop(
            lambda c: (c[0] >= 0) & (jnp.maximum(jnp.max(c[1]), jnp.max(c[2])) > SB_CUT),
            lambda c: (c[0] - 1,) + step(c[0], c[1], c[2], c[3], c[4], False),
            (i - 1, r0, r1, acc, sv))
        o_ref[...] = acc.astype(BF16)
        sv_ref[...] = sv

    return pl.pallas_call(
        body, name="sb_fwd", grid=(4, nq),
        in_specs=[pl.BlockSpec((SB_BLK, LANES), lambda p, i: (i, p)),
                  pl.BlockSpec((t, LANES), lambda p, i: (0, 4 + p)),
                  pl.BlockSpec((t, LANES), lambda p, i: (0, 8 + p))] + [HBM_SPEC] * n_in,
        out_specs=[pl.BlockSpec((SB_BLK, LANES), lambda p, i: (i, p))] * 2 + [HBM_SPEC] * n_out,
        out_shape=[jax.ShapeDtypeStruct((t, SB_WIDTH), BF16), jax.ShapeDtypeStruct((t, SB_WIDTH), F32)] + rider.out_shapes,
        scratch_shapes=[pltpu.SemaphoreType.DMA((s,)) for s in rider.sems],
        compiler_params=_cparams("arbitrary", "arbitrary"),
    )(proj, proj, proj, *rider.inputs)


def _sb_bwd(proj, dcat, mass, t, rider):
    nq = t // SB_BLK
    n_in, n_out = len(rider.inputs), len(rider.out_shapes)

    def body(q_ref, k_ref, v_ref, do_ref, sv_ref, *rest):
        rin = rest[:n_in]
        dq_ref, dk_ref, dv_ref = rest[n_in:n_in + 3]
        rout = rest[n_in + 3:n_in + 3 + n_out]
        p = pl.program_id(0)
        i = pl.program_id(1)
        _ride(rider, rin, rout, rest[n_in + 3 + n_out:], (p == 0) & (i == 0), (p == 3) & (i == nq - 1))

        @pl.when(i == 0)
        def _():
            dk_ref[...] = jnp.zeros_like(dk_ref)
            dv_ref[...] = jnp.zeros_like(dv_ref)

        strict, u_strict, u_pre, lane = _sb_consts()
        qf = q_ref[...] * SB_SCALE
        dof = do_ref[...]
        hms = [(lane // 64) == hh for hh in range(2)]
        qhs = [jnp.where(hm, qf, 0.0).astype(BF16) for hm in hms]
        dohs = [jnp.where(hm, dof, 0.0).astype(BF16) for hm in hms]

        sv = sv_ref[...]
        zero = jnp.zeros((SB_BLK, 1), F32)

        def mass_right(hh, d):
            return jnp.sum(jnp.where(lane == hh * 64 + d, sv, 0.0), axis=1, keepdims=True)

        dstop = lax.while_loop(
            lambda d: (i - d >= 0) & (jnp.maximum(jnp.max(mass_right(0, d)), jnp.max(mass_right(1, d))) > SB_CUT),
            lambda d: d + 1, 1)
        jstop = i - dstop

        def step(j, carry, diag):
            pre_g0, pre_g1, dqa = carry
            rows = pl.ds(pl.multiple_of(j * SB_BLK, SB_BLK), SB_BLK)
            kf = k_ref[rows, :]
            kj = kf.astype(BF16)
            vj = v_ref[rows, :].astype(BF16)
            dv = jnp.zeros((SB_BLK, LANES), F32)
            dk = jnp.zeros((SB_BLK, LANES), F32)
            pre = []
            for hh, pre_g in enumerate((pre_g0, pre_g1)):
                _, lsz, w = _sb_tile(qhs[hh], kj, diag, strict, u_strict, zero if diag else mass_right(hh, i - j))
                g = w * _dot_nt(dohs[hh], vj)
                gpre = _dot(g.astype(BF16), u_pre) + pre_g
                sig = jnp.exp(lsz)
                dz = g * (1.0 - sig) - gpre * sig
                if diag:
                    dz = jnp.where(strict, dz, 0.0)
                dzb = dz.astype(BF16)
                dv = dv + _dot_tn(w.astype(BF16), dohs[hh])
                dk = dk + _dot_tn(dzb, qhs[hh])
                dqa = dqa + _dot(dzb, jnp.where(hms[hh], kf, 0.0).astype(BF16))
                pre.append(pre_g + jnp.sum(g, axis=1, keepdims=True))
            dv_ref[rows, :] += dv
            dk_ref[rows, :] += dk
            return pre[0], pre[1], dqa

        carry = lax.fori_loop(jstop + 1, i, lambda j, c: step(j, c, False), (zero, zero, jnp.zeros((SB_BLK, LANES), F32)))
        _, _, dq = step(i, carry, True)
        dq_ref[...] = dq * SB_SCALE

    return pl.pallas_call(
        body, name="sb_bwd", grid=(4, nq),
        in_specs=[pl.BlockSpec((SB_BLK, LANES), lambda p, i: (i, p)),
                  pl.BlockSpec((t, LANES), lambda p, i: (0, 4 + p)),
                  pl.BlockSpec((t, LANES), lambda p, i: (0, 8 + p)),
                  pl.BlockSpec((SB_BLK, LANES), lambda p, i: (i, p)),
                  pl.BlockSpec((SB_BLK, LANES), lambda p, i: (i, p))] + [HBM_SPEC] * n_in,
        out_specs=[pl.BlockSpec((SB_BLK, LANES), lambda p, i: (i, p)),
                   pl.BlockSpec((t, LANES), lambda p, i: (0, p)),
                   pl.BlockSpec((t, LANES), lambda p, i: (0, p))] + [HBM_SPEC] * n_out,
        out_shape=[jax.ShapeDtypeStruct((t, SB_WIDTH), F32)] * 3 + rider.out_shapes,
        scratch_shapes=[pltpu.SemaphoreType.DMA((s,)) for s in rider.sems],
        compiler_params=_cparams("arbitrary", "arbitrary"),
    )(proj, proj, proj, dcat, mass, *rider.inputs)


def _gla_consts():
    r = lax.broadcasted_iota(jnp.int32, (CHUNK, CHUNK), 0)
    c = lax.broadcasted_iota(jnp.int32, (CHUNK, CHUNK), 1)
    causal = c <= r
    lc = causal.astype(BF16)
    lct = (c >= r).astype(BF16)
    rowid = lax.broadcasted_iota(jnp.int32, (CHUNK, 1), 0)
    lane = lax.broadcasted_iota(jnp.int32, (1, LANES), 1)
    sr = lax.broadcasted_iota(jnp.int32, (LANES, 2 * LANES), 0)
    sc = lax.broadcasted_iota(jnp.int32, (LANES, 2 * LANES), 1)
    blockdiag = (sr // 64) == (sc // LANES)
    return causal, lc, lct, rowid, lane, blockdiag


def _dot3(u, x):
    hi, mid, lo = _split3(x)
    return _dot(u, hi) + _dot(u, mid) + _dot(u, lo)


def _row_to_col(row):
    return jnp.transpose(jnp.broadcast_to(row, (LANES, LANES)))


def _gla_gates(ga_ref, gup_ref, gbias_ref):
    pre = _dot(ga_ref[...].astype(BF16), gup_ref[...].astype(BF16)) + gbias_ref[...]
    log_a = (jnp.minimum(pre, 0.0) - jnp.log(1.0 + jnp.exp(-jnp.abs(pre)))) / GLA_TAU
    return pre, log_a


def _gla_chunk_terms(g2, q2, k2, lc, rowid):
    b = _dot3(lc, g2)
    b_ref = jnp.sum(jnp.where(rowid == CHUNK // 2 - 1, b, 0.0), axis=0, keepdims=True)
    b_last = jnp.sum(jnp.where(rowid == CHUNK - 1, b, 0.0), axis=0, keepdims=True)
    qs = q2 * GLA_SCALE
    e_q = jnp.exp(b - b_ref)
    e_k = jnp.exp(b_ref - b)
    e_d = jnp.exp(b_last - b)
    e_b = jnp.exp(b)
    decay = jnp.exp(b_last)
    return dict(qs=qs, e_q=e_q, e_k=e_k, e_d=e_d, e_b=e_b, decay=decay,
                qi=qs * e_q, ki=k2 * e_k, kd=k2 * e_d, qb=qs * e_b)


def _gla_fwd(proj, gate_up_pad, gate_bias, gnorm, t, rider):
    nsteps = t // GLA_ROWS
    cps = GLA_ROWS // CHUNK
    n_in, n_out = len(rider.inputs), len(rider.out_shapes)

    def body(q_ref, k_ref, v_ref, gg_ref, ga_ref, gup_ref, gbias_ref, gn_ref, *rest):
        o_ref, oraw_ref, st_ref = rest[n_in:n_in + 3]
        s_scr = rest[n_in + 3 + n_out]
        i = pl.program_id(0)
        _ride(rider, rest[:n_in], rest[n_in + 3:n_in + 3 + n_out], rest[n_in + 4 + n_out:], i == 0, i == nsteps - 1)

        @pl.when(i == 0)
        def _():
            s_scr[...] = jnp.zeros_like(s_scr)

        causal, lc, _, rowid, lane, blockdiag = _gla_consts()
        _, log_a = _gla_gates(ga_ref, gup_ref, gbias_ref)
        gn = gn_ref[...]
        state = [s_scr[0], s_scr[1]]
        for cc in range(cps):
            rows = slice(cc * CHUNK, (cc + 1) * CHUNK)
            for p in range(2):
                kl = slice(p * LANES, (p + 1) * LANES)
                vl = slice(p * 2 * LANES, (p + 1) * 2 * LANES)
                tm = _gla_chunk_terms(log_a[rows, kl], q_ref[rows, kl], k_ref[rows, kl], lc, rowid)
                v2 = v_ref[rows, vl]
                v2b = v2.astype(BF16)
                s_prev = state[p]
                st_ref[cc, p] = s_prev
                kib = tm["ki"].astype(BF16)
                o_inter = _dot(tm["qb"].astype(BF16), s_prev.astype(BF16))
                outs = []
                for hh in range(2):
                    hm = (lane // 64) == hh
                    a = _dot_nt(jnp.where(hm, tm["qi"], 0.0).astype(BF16), kib)
                    a = jnp.where(causal, a, 0.0)
                    outs.append(_dot(a.astype(BF16), v2b[:, hh * LANES:(hh + 1) * LANES]))
                o2 = jnp.concatenate(outs, axis=1) + o_inter
                upd = jnp.where(blockdiag, _dot_tn(tm["kd"].astype(BF16), v2b), 0.0)
                dcol = _row_to_col(tm["decay"])
                state[p] = s_prev * jnp.concatenate([dcol, dcol], axis=1) + upd
                oraw_ref[rows, vl] = o2
                for hh in range(2):
                    hl = slice(hh * LANES, (hh + 1) * LANES)
                    oh = o2[:, hl]
                    gl = slice(p * 2 * LANES + hh * LANES, p * 2 * LANES + (hh + 1) * LANES)
                    rinv = lax.rsqrt(jnp.mean(oh * oh, axis=1, keepdims=True) + RMS_EPS)
                    gg = gg_ref[rows, gl]
                    o_ref[rows, gl] = (oh * rinv * gn * (gg * _sigmoid(gg))).astype(BF16)
        s_scr[0] = state[0]
        s_scr[1] = state[1]

    cb = lambda w, idx: pl.BlockSpec((GLA_ROWS, w), lambda i: (i, idx))
    full = lambda shp: pl.BlockSpec(shp, lambda i: tuple(0 for _ in shp))
    return pl.pallas_call(
        body, name="gla_fwd", grid=(nsteps,),
        in_specs=[cb(256, 6), cb(256, 7), cb(512, 4), cb(512, 5), cb(128, 24),
                  full((LANES, GLA_KEYS)), full((1, GLA_KEYS)), full((1, LANES))] + [HBM_SPEC] * n_in,
        out_specs=[pl.BlockSpec((GLA_ROWS, GLA_WIDTH), lambda i: (i, 0)),
                   pl.BlockSpec((GLA_ROWS, GLA_WIDTH), lambda i: (i, 0)),
                   pl.BlockSpec((cps, 2, LANES, 2 * LANES), lambda i: (i, 0, 0, 0))] + [HBM_SPEC] * n_out,
        out_shape=[jax.ShapeDtypeStruct((t, GLA_WIDTH), BF16), jax.ShapeDtypeStruct((t, GLA_WIDTH), F32),
                   jax.ShapeDtypeStruct((t // CHUNK, 2, LANES, 2 * LANES), F32)]
        + rider.out_shapes,
        input_output_aliases={8 + i: 3 + o for i, o in rider.aliases},
        scratch_shapes=[pltpu.VMEM((2, LANES, 2 * LANES), F32)] + [pltpu.SemaphoreType.DMA((s,)) for s in rider.sems],
        compiler_params=_cparams("arbitrary"),
    )(proj, proj, proj, proj, proj, gate_up_pad, gate_bias, gnorm, *rider.inputs)


def _gla_bwd(proj, dcat, oraw, states, gate_up_pad, gate_bias, gnorm, t):
    nsteps = t // GLA_ROWS
    cps = GLA_ROWS // CHUNK
    wout = 2 * GLA_KEYS + 2 * GLA_WIDTH + LANES

    def body(q_ref, k_ref, v_ref, gg_ref, ga_ref, do_ref, oraw_ref, st_ref, gup_ref, gbias_ref, gn_ref,
             d_ref, dgup_ref, dgbias_ref, dgn_ref, ds_scr):
        i = pl.program_id(0)

        @pl.when(i == 0)
        def _():
            ds_scr[...] = jnp.zeros_like(ds_scr)
            dgup_ref[...] = jnp.zeros_like(dgup_ref)
            dgbias_ref[...] = jnp.zeros_like(dgbias_ref)
            dgn_ref[...] = jnp.zeros_like(dgn_ref)

        causal, lc, lct, rowid, lane, blockdiag = _gla_consts()
        pre, log_a = _gla_gates(ga_ref, gup_ref, gbias_ref)
        gn = gn_ref[...]
        dgn = jnp.zeros((1, LANES), F32)
        dstate = [ds_scr[0], ds_scr[1]]
        dgs = [[None, None] for _ in range(cps)]
        for cc in reversed(range(cps)):
            rows = slice(cc * CHUNK, (cc + 1) * CHUNK)
            for p in range(2):
                kl = slice(p * LANES, (p + 1) * LANES)
                vl = slice(p * 2 * LANES, (p + 1) * 2 * LANES)
                tm = _gla_chunk_terms(log_a[rows, kl], q_ref[rows, kl], k_ref[rows, kl], lc, rowid)
                v2b = v_ref[rows, vl].astype(BF16)
                s_prev = st_ref[cc, p]
                ds2 = dstate[p]
                dos = []
                for hh in range(2):
                    gl = slice(p * 2 * LANES + hh * LANES, p * 2 * LANES + (hh + 1) * LANES)
                    oh = oraw_ref[rows, gl]
                    rinv = lax.rsqrt(jnp.mean(oh * oh, axis=1, keepdims=True) + RMS_EPS)
                    on = oh * rinv
                    gg = gg_ref[rows, gl]
                    sg = _sigmoid(gg)
                    sil = gg * sg
                    dgo = do_ref[rows, gl]
                    d_ref[rows, 2 * GLA_KEYS + GLA_WIDTH + gl.start:2 * GLA_KEYS + GLA_WIDTH + gl.stop] = (
                        dgo * on * gn * (sg * (1.0 + gg * (1.0 - sg))))
                    dgn = dgn + jnp.sum(dgo * sil * on, axis=0, keepdims=True)
                    don = dgo * sil * gn
                    dos.append(rinv * (don - on * jnp.mean(don * on, axis=1, keepdims=True)))
                do2b = jnp.concatenate(dos, axis=1).astype(BF16)
                qib = tm["qi"].astype(BF16)
                kib = tm["ki"].astype(BF16)
                kdb = tm["kd"].astype(BF16)
                qbb = tm["qb"].astype(BF16)
                ds2b = ds2.astype(BF16)
                dqi = jnp.zeros((CHUNK, LANES), F32)
                dki = jnp.zeros((CHUNK, LANES), F32)
                dvs = []
                for hh in range(2):
                    hm = (lane // 64) == hh
                    hl = slice(hh * LANES, (hh + 1) * LANES)
                    a = jnp.where(causal, _dot_nt(jnp.where(hm, tm["qi"], 0.0).astype(BF16), kib), 0.0).astype(BF16)
                    da = jnp.where(causal, _dot_nt(do2b[:, hl], v2b[:, hl]), 0.0).astype(BF16)
                    dvs.append(_dot_tn(a, do2b[:, hl]))
                    dqi = dqi + jnp.where(hm, _dot(da, kib), 0.0)
                    dki = dki + jnp.where(hm, _dot_tn(da, qib), 0.0)
                dv2 = jnp.concatenate(dvs, axis=1) + _dot(kdb, ds2b)
                dqb = _dot_nt(do2b, s_prev.astype(BF16))
                dkd = _dot_nt(v2b, ds2b)
                dcol = _row_to_col(tm["decay"])
                dsp = jnp.where(blockdiag, _dot_tn(qbb, do2b), 0.0) + ds2 * jnp.concatenate([dcol, dcol], axis=1)
                ddecay_col = jnp.sum(ds2 * s_prev, axis=1, keepdims=True)
                ddecay_row = jnp.transpose(jnp.broadcast_to(ddecay_col, (LANES, LANES)))[0:1, :]
                dstate[p] = dsp
                dqs = dqi * tm["e_q"] + dqb * tm["e_b"]
                dk = dki * tm["e_k"] + dkd * tm["e_d"]
                t_qi = dqi * tm["qi"]
                t_ki = dki * tm["ki"]
                t_kd = dkd * tm["kd"]
                db = t_qi - t_ki + dqb * tm["qb"] - t_kd
                db_ref = jnp.sum(t_ki - t_qi, axis=0, keepdims=True)
                db_last = jnp.sum(t_kd, axis=0, keepdims=True) + ddecay_row * tm["decay"]
                db = db + jnp.where(rowid == CHUNK // 2 - 1, db_ref, 0.0) + jnp.where(rowid == CHUNK - 1, db_last, 0.0)
                dg = _dot3(lct, db)
                d_ref[rows, p * LANES:(p + 1) * LANES] = dqs * GLA_SCALE
                d_ref[rows, GLA_KEYS + p * LANES:GLA_KEYS + (p + 1) * LANES] = dk
                d_ref[rows, 2 * GLA_KEYS + p * 2 * LANES:2 * GLA_KEYS + (p + 1) * 2 * LANES] = dv2
                dgs[cc][p] = dg
        ds_scr[0] = dstate[0]
        ds_scr[1] = dstate[1]
        dlog_a = jnp.concatenate([jnp.concatenate(row, axis=1) for row in dgs], axis=0)
        dpre = dlog_a * (1.0 / GLA_TAU) * _sigmoid(-pre)
        dpb = dpre.astype(BF16)
        dgn_ref[...] += dgn
        dgbias_ref[...] += jnp.sum(dpre, axis=0, keepdims=True)
        dgup_ref[...] += _dot_tn(ga_ref[...].astype(BF16), dpb)
        d_ref[:, 2 * GLA_KEYS + 2 * GLA_WIDTH:] = _dot_nt(dpb, gup_ref[...].astype(BF16))

    rev = lambda i: nsteps - 1 - i
    cb = lambda w, idx: pl.BlockSpec((GLA_ROWS, w), lambda i: (rev(i), idx))
    full = lambda shp: pl.BlockSpec(shp, lambda i: tuple(0 for _ in shp))
    return pl.pallas_call(
        body, name="gla_bwd", grid=(nsteps,),
        in_specs=[cb(256, 6), cb(256, 7), cb(512, 4), cb(512, 5), cb(128, 24), cb(512, 1), cb(512, 0),
                  pl.BlockSpec((cps, 2, LANES, 2 * LANES), lambda i: (rev(i), 0, 0, 0)),
                  full((LANES, GLA_KEYS)), full((1, GLA_KEYS)), full((1, LANES))],
        out_specs=[pl.BlockSpec((GLA_ROWS, wout), lambda i: (rev(i), 0)),
                   full((LANES, GLA_KEYS)), full((1, GLA_KEYS)), full((1, LANES))],
        out_shape=[jax.ShapeDtypeStruct((t, wout), F32), jax.ShapeDtypeStruct((LANES, GLA_KEYS), F32),
                   jax.ShapeDtypeStruct((1, GLA_KEYS), F32), jax.ShapeDtypeStruct((1, LANES), F32)],
        scratch_shapes=[pltpu.VMEM((2, LANES, 2 * LANES), F32)],
        compiler_params=_cparams("arbitrary"),
    )(proj, proj, proj, proj, proj, dcat, oraw, states, gate_up_pad, gate_bias, gnorm)


def _ln_stats(r):
    mu = jnp.mean(r, axis=1, keepdims=True)
    xc = r - mu
    rstd = lax.rsqrt(jnp.mean(xc * xc, axis=1, keepdims=True) + LN_EPS)
    return xc * rstd, rstd


def _ln_bwd(dy_g, xhat, rstd):
    return rstd * (dy_g - jnp.mean(dy_g, axis=1, keepdims=True) - xhat * jnp.mean(dy_g * xhat, axis=1, keepdims=True))


def _outproj_ln1(sb_o, gla_o, x, w_out, g1, b1, t, tm=256):
    def body(sb_ref, gl_ref, x_ref, w_ref, g_ref, b_ref, xhat_ref, rstd_ref, h_ref):
        mix = _dot(sb_ref[...], w_ref[0:SB_WIDTH, :]) + _dot(gl_ref[...], w_ref[SB_WIDTH:, :])
        xhat, rstd = _ln_stats(ALPHA * x_ref[...] + mix)
        xhat_ref[...] = xhat
        rstd_ref[...] = rstd
        h_ref[...] = (xhat * g_ref[...] + b_ref[...]).astype(BF16)

    row = lambda w: pl.BlockSpec((tm, w), lambda i: (i, 0))
    full = lambda shp: pl.BlockSpec(shp, lambda i: (0, 0))
    return pl.pallas_call(
        body, name="outproj_ln1", grid=(t // tm,),
        in_specs=[row(SB_WIDTH), row(GLA_WIDTH), row(D_MODEL), full((D_MODEL, D_MODEL)), full((1, D_MODEL)), full((1, D_MODEL))],
        out_specs=[row(D_MODEL), row(1), row(D_MODEL)],
        out_shape=[jax.ShapeDtypeStruct((t, D_MODEL), F32), jax.ShapeDtypeStruct((t, 1), F32),
                   jax.ShapeDtypeStruct((t, D_MODEL), BF16)],
        compiler_params=_cparams("parallel"),
    )(sb_o, gla_o, x, w_out, g1, b1)


_INV_SQRT2 = 1.0 / math.sqrt(2.0)
_INV_SQRT2PI = 1.0 / math.sqrt(2.0 * math.pi)


def _conv3(xs, w_ref, b_ref, half):
    return (w_ref[half, 0:1, :] * pltpu.roll(xs, 2, 0) + w_ref[half, 1:2, :] * pltpu.roll(xs, 1, 0)
            + w_ref[half, 2:3, :] * xs + b_ref[half])


HALO = 16


def _conv_gelu_fwd(up3, conv_w3, conv_b3, t, tr=512, ct=256):
    nct = D_FF // ct
    hb = tr // HALO

    def body(cur_ref, prev_ref, w_ref, b_ref, gm_ref):
        i = pl.program_id(1)
        keep = (i > 0).astype(F32)
        us = []
        for half in range(2):
            xs = jnp.concatenate([prev_ref[half].astype(F32) * keep, cur_ref[half].astype(F32)], axis=0)
            us.append(_conv3(xs, w_ref, b_ref, half)[HALO:, :])
        a, c = us
        gelu = 0.5 * a * (1.0 + lax.erf(a * _INV_SQRT2))
        gm_ref[...] = (gelu * c).astype(BF16)

    return pl.pallas_call(
        body, name="conv_gelu_fwd", grid=(nct, t // tr),
        in_specs=[pl.BlockSpec((2, tr, ct), lambda j, i: (0, i, j)),
                  pl.BlockSpec((2, HALO, ct), lambda j, i: (0, jnp.maximum(i * hb - 1, 0), j)),
                  pl.BlockSpec((2, 3, ct), lambda j, i: (0, 0, j)),
                  pl.BlockSpec((2, 1, ct), lambda j, i: (0, 0, j))],
        out_specs=pl.BlockSpec((tr, ct), lambda j, i: (i, j)),
        out_shape=jax.ShapeDtypeStruct((t, D_FF), BF16),
        compiler_params=_cparams("parallel", "parallel"),
    )(up3, up3, conv_w3, conv_b3)


def _conv_gelu_bwd(up3, dgm, conv_w3, conv_b3, t, tr=512, ct=256):
    nct = D_FF // ct
    nrt = t // tr
    hb = tr // HALO
    n = tr + 2 * HALO
    lo, hi = HALO, tr + HALO

    def body(cur_ref, prev_ref, next_ref, dg_ref, dgn_ref, w_ref, b_ref, dup_ref, dcw_ref, dcb_ref):
        i = pl.program_id(1)

        @pl.when(i == 0)
        def _():
            dcw_ref[...] = jnp.zeros_like(dcw_ref)
            dcb_ref[...] = jnp.zeros_like(dcb_ref)

        keep_prev = (i > 0).astype(F32)
        keep_next = (i < nrt - 1).astype(F32)
        xs, xm1, xm2, us = [], [], [], []
        for half in range(2):
            x = jnp.concatenate([prev_ref[half].astype(F32) * keep_prev, cur_ref[half].astype(F32),
                                 next_ref[half].astype(F32)], axis=0)
            xs.append(x)
            xm1.append(pltpu.roll(x, 1, 0))
            xm2.append(pltpu.roll(x, 2, 0))
            us.append(w_ref[half, 0:1, :] * xm2[half] + w_ref[half, 1:2, :] * xm1[half]
                      + w_ref[half, 2:3, :] * x + b_ref[half])
        a, c = us
        dg = jnp.concatenate([jnp.zeros((HALO, ct), F32), dg_ref[...].astype(F32),
                              dgn_ref[...].astype(F32) * keep_next], axis=0)
        cdf = 0.5 * (1.0 + lax.erf(a * _INV_SQRT2))
        pdf = jnp.exp(-0.5 * a * a) * _INV_SQRT2PI
        dus = [dg * c * (cdf + a * pdf), dg * (a * cdf)]
        rid = lax.broadcasted_iota(jnp.int32, (8, 1), 0)
        for half in range(2):
            du = dus[half]
            dup = (w_ref[half, 2:3, :] * du + w_ref[half, 1:2, :] * pltpu.roll(du, n - 1, 0)
                   + w_ref[half, 0:1, :] * pltpu.roll(du, n - 2, 0))
            dup_ref[half] = dup[lo:hi, :].astype(BF16)
            duc = du[lo:hi, :]
            s0 = jnp.sum(duc * xm2[half][lo:hi, :], axis=0, keepdims=True)
            s1 = jnp.sum(duc * xm1[half][lo:hi, :], axis=0, keepdims=True)
            s2 = jnp.sum(duc * xs[half][lo:hi, :], axis=0, keepdims=True)
            dcw_ref[half] += jnp.where(rid == 0, s0, jnp.where(rid == 1, s1, jnp.where(rid == 2, s2, 0.0)))
            dcb_ref[half] += jnp.sum(duc, axis=0, keepdims=True)

    last = t // HALO - 1
    return pl.pallas_call(
        body, name="conv_gelu_bwd", grid=(nct, nrt),
        in_specs=[pl.BlockSpec((2, tr, ct), lambda j, i: (0, i, j)),
                  pl.BlockSpec((2, HALO, ct), lambda j, i: (0, jnp.maximum(i * hb - 1, 0), j)),
                  pl.BlockSpec((2, HALO, ct), lambda j, i: (0, jnp.minimum((i + 1) * hb, last), j)),
                  pl.BlockSpec((tr, ct), lambda j, i: (i, j)),
                  pl.BlockSpec((HALO, ct), lambda j, i: (jnp.minimum((i + 1) * hb, last), j)),
                  pl.BlockSpec((2, 3, ct), lambda j, i: (0, 0, j)),
                  pl.BlockSpec((2, 1, ct), lambda j, i: (0, 0, j))],
        out_specs=[pl.BlockSpec((2, tr, ct), lambda j, i: (0, i, j)),
                   pl.BlockSpec((2, 8, ct), lambda j, i: (0, 0, j)),
                   pl.BlockSpec((2, 1, ct), lambda j, i: (0, 0, j))],
        out_shape=[jax.ShapeDtypeStruct((2, t, D_FF), BF16), jax.ShapeDtypeStruct((2, 8, D_FF), F32),
                   jax.ShapeDtypeStruct((2, 1, D_FF), F32)],
        compiler_params=_cparams("parallel", "arbitrary"),
    )(up3, up3, up3, dgm, dgm, conv_w3, conv_b3)


def _down_ln2_loss(gm, w_down, xhat1, g1, b1, g2, b2, target, t, tm=256):
    def body(gm_ref, w_ref, xh_ref, g1_ref, b1_ref, g2_ref, b2_ref, tg_ref, dr_ref, loss_ref, dg_ref, db_ref):
        i = pl.program_id(0)

        @pl.when(i == 0)
        def _():
            loss_ref[...] = jnp.zeros_like(loss_ref)
            dg_ref[...] = jnp.zeros_like(dg_ref)
            db_ref[...] = jnp.zeros_like(db_ref)

        h = xh_ref[...] * g1_ref[...] + b1_ref[...]
        xhat, rstd = _ln_stats(ALPHA * h + _dot(gm_ref[...], w_ref[...]))
        err = xhat * g2_ref[...] + b2_ref[...] - tg_ref[...]
        loss_ref[...] += 0.5 * jnp.sum(jnp.sum(err * err, axis=1, keepdims=True), axis=0, keepdims=True) / D_MODEL
        dy = err * (1.0 / D_MODEL)
        dg_ref[...] += jnp.sum(dy * xhat, axis=0, keepdims=True)
        db_ref[...] += jnp.sum(dy, axis=0, keepdims=True)
        dr_ref[...] = _ln_bwd(dy * g2_ref[...], xhat, rstd)

    row = lambda w: pl.BlockSpec((tm, w), lambda i: (i, 0))
    full = lambda shp: pl.BlockSpec(shp, lambda i: (0, 0))
    vec = full((1, D_MODEL))
    return pl.pallas_call(
        body, name="down_ln2_loss", grid=(t // tm,),
        in_specs=[row(D_FF), full((D_FF, D_MODEL)), row(D_MODEL), vec, vec, vec, vec, row(D_MODEL)],
        out_specs=[row(D_MODEL), full((1, 1)), vec, vec],
        out_shape=[jax.ShapeDtypeStruct((t, D_MODEL), F32), jax.ShapeDtypeStruct((1, 1), F32),
                   jax.ShapeDtypeStruct((1, D_MODEL), F32), jax.ShapeDtypeStruct((1, D_MODEL), F32)],
        compiler_params=_cparams("arbitrary"),
    )(gm, w_down, xhat1, g1, b1, g2, b2, target)


def _dh_ln1_bwd(dup3, w_up4, dr2, xhat1, rstd1, g1, t, tm=512):
    tk = 2 * D_FF // 4
    nkh = D_FF // tk
    nk = 2 * nkh

    def body(a_ref, w_ref, dr2_ref, xh_ref, rs_ref, g_ref, dr1_ref, dg_ref, db_ref, acc_ref):
        i = pl.program_id(0)
        kk = pl.program_id(1)

        @pl.when((i == 0) & (kk == 0))
        def _():
            dg_ref[...] = jnp.zeros_like(dg_ref)
            db_ref[...] = jnp.zeros_like(db_ref)

        part = _dot_nt(a_ref[...], w_ref[...])

        @pl.when(kk == 0)
        def _():
            acc_ref[...] = part

        @pl.when(kk > 0)
        def _():
            acc_ref[...] += part

        @pl.when(kk == nk - 1)
        def _():
            dh = acc_ref[...] + ALPHA * dr2_ref[...]
            xhat = xh_ref[...]
            dg_ref[...] += jnp.sum(dh * xhat, axis=0, keepdims=True)
            db_ref[...] += jnp.sum(dh, axis=0, keepdims=True)
            dr1_ref[...] = _ln_bwd(dh * g_ref[...], xhat, rs_ref[...])

    row = lambda w: pl.BlockSpec((tm, w), lambda i, kk: (i, 0))
    vec = pl.BlockSpec((1, D_MODEL), lambda i, kk: (0, 0))
    return pl.pallas_call(
        body, name="dh_ln1_bwd", grid=(t // tm, nk),
        in_specs=[pl.BlockSpec((None, tm, tk), lambda i, kk: (kk // nkh, i, kk % nkh)),
                  pl.BlockSpec((None, D_MODEL, tk), lambda i, kk: (kk, 0, 0)),
                  row(D_MODEL), row(D_MODEL), row(1), vec],
        out_specs=[row(D_MODEL), vec, vec],
        out_shape=[jax.ShapeDtypeStruct((t, D_MODEL), F32), jax.ShapeDtypeStruct((1, D_MODEL), F32),
                   jax.ShapeDtypeStruct((1, D_MODEL), F32)],
        scratch_shapes=[pltpu.VMEM((tm, D_MODEL), F32)],
        compiler_params=_cparams("arbitrary", "arbitrary"),
    )(dup3, w_up4, dr2, xhat1, rstd1, g1)


def _adamw(w, g, m, v, name):
    rows, cols = w.shape
    tr = rows
    for cand in (256, 128, 64, 32, 16, 8):
        if rows % cand == 0 and rows > cand:
            tr = cand
            break
    c1 = 1.0 / (1.0 - ADAM_B1 ** ADAM_STEP)
    c2 = 1.0 / (1.0 - ADAM_B2 ** ADAM_STEP)

    def body(w_ref, g_ref, m_ref, v_ref, d_ref, nm_ref, nv_ref):
        gv = g_ref[...]
        nm = ADAM_B1 * m_ref[...] + (1.0 - ADAM_B1) * gv
        nv = ADAM_B2 * v_ref[...] + (1.0 - ADAM_B2) * (gv * gv)
        d_ref[...] = -ADAM_LR * ((nm * c1) / (jnp.sqrt(nv * c2) + ADAM_EPS) + ADAM_WD * w_ref[...])
        nm_ref[...] = nm
        nv_ref[...] = nv

    spec = pl.BlockSpec((tr, cols), lambda i: (i, 0))
    out = jax.ShapeDtypeStruct((rows, cols), F32)
    return pl.pallas_call(
        body, name=name, grid=(rows // tr,), in_specs=[spec] * 4, out_specs=[spec] * 3, out_shape=[out] * 3,
        compiler_params=_cparams("parallel"),
    )(w, g, m, v)


def _local_step(x, target, w_in_p, late_shards, gate_up_pad, gate_bias, gnorm, ln1_g, ln1_b, conv_w3, conv_b3,
                ln2_g, ln2_b, c_arr, kc_arr):
    t = x.shape[0]
    tq = min(t, 1024)
    s_up, s_out, s_down = late_shards
    sh_up, sh_out, sh_down = LATE_SHAPES
    proj, out_partly = _mm(x, w_in_p, m=t, n=IN_PAD, k=D_MODEL, tm=tq, tn=640, tk=D_MODEL, name="proj",
                           rider=_gather_rider([s_out], [sh_out]))
    sb_o, sb_mass, up_partly = _sb_fwd(proj, t, _gather_rider([s_up], [sh_up]))
    gla_o, oraw, states, w_up4, w_out4, down_partly = _gla_fwd(
        proj, gate_up_pad, gate_bias, gnorm, t,
        _join(_forward_rider([up_partly, out_partly], [sh_up, sh_out]), _gather_rider([s_down], [sh_down])))
    w_out = w_out4.reshape(D_MODEL, D_MODEL)
    xhat1, rstd1, h_bf = _outproj_ln1(sb_o, gla_o, x, w_out, ln1_g, ln1_b, t)
    up3, w_down4 = _mm(h_bf, w_up4, m=t, n=2 * D_FF, k=D_MODEL, tm=tq, tn=W_UP_S, tk=D_MODEL, name="up",
                       b_spec=pl.BlockSpec((None, D_MODEL, W_UP_S), lambda i, j, kk: (j, 0, 0)),
                       o_spec=pl.BlockSpec((None, tq, W_UP_S), lambda i, j, kk: (j // 2, i, j % 2)),
                       out_shape=jax.ShapeDtypeStruct((2, t, D_FF), BF16), out_dtype=BF16,
                       rider=_forward_rider([down_partly], [sh_down]))
    w_down = w_down4.reshape(D_FF, D_MODEL)
    gm = _conv_gelu_fwd(up3, conv_w3, conv_b3, t, tr=tq)
    dr2, loss, dln2_g, dln2_b = _down_ln2_loss(gm, w_down, xhat1, ln1_g, ln1_b, ln2_g, ln2_b, target, t)
    dgm = _mm(dr2, w_down, m=t, n=D_FF, k=D_MODEL, tm=tq, tn=W_UP_S, tk=D_MODEL, tb=True, out_dtype=BF16, name="dgm")
    dw_down = _mm(gm, dr2, m=D_FF, n=D_MODEL, k=t, tm=W_UP_S, tn=D_MODEL, tk=tq, ta=True, name="dw_down")
    dup3, dcw, dcb = _conv_gelu_bwd(up3, dgm, conv_w3, conv_b3, t)
    dr1, dln1_g, dln1_b = _dh_ln1_bwd(dup3, w_up4, dr2, xhat1, rstd1, ln1_g, t)
    dw_up4 = _mm(h_bf, dup3, m=D_MODEL, n=2 * D_FF, k=t, tm=512, tn=W_UP_S, tk=t, ta=True, name="dw_up",
                 b_spec=pl.BlockSpec((None, t, W_UP_S), lambda i, j, kk: (j // 2, kk, j % 2)),
                 o_spec=pl.BlockSpec((None, 512, W_UP_S), lambda i, j, kk: (j, i, 0)),
                 out_shape=jax.ShapeDtypeStruct((4, D_MODEL, W_UP_S), F32))
    dw_out_sb = _mm(sb_o, dr1, m=SB_WIDTH, n=D_MODEL, k=t, tm=512, tn=D_MODEL, tk=tq, ta=True, name="dw_out_sb")
    dw_out_gla = _mm(gla_o, dr1, m=GLA_WIDTH, n=D_MODEL, k=t, tm=512, tn=D_MODEL, tk=tq, ta=True, name="dw_out_gla")
    gs = [dw_up4, jnp.concatenate([dw_out_sb, dw_out_gla], axis=0).reshape(4, W_OUT_S, D_MODEL),
          dw_down.reshape(4, W_DOWN_S, D_MODEL)]
    dcat, *from_sib = _mm(dr1, w_out, m=t, n=D_MODEL, k=D_MODEL, tm=tq, tn=512, tk=D_MODEL, tb=True, name="dcat",
                          rider=_sibling_rider(gs, LATE_SHAPES))
    ps = [_add_sibling(gs[m], from_sib[m], c_arr, LATE_ADD_ROWS[m], "add_sibling_late_%d" % m) for m in range(3)]
    dsq, dsk, dsv, *others = _sb_bwd(proj, dcat, sb_mass, t, _chips_rider(ps))
    late_sums = [_add_chips(ps[m], others[m], kc_arr, LATE_ADD_ROWS[m], "add_chips_late_%d" % m) for m in range(3)]
    dgla, dgup_pad, dgbias, dgnorm = _gla_bwd(proj, dcat, oraw, states, gate_up_pad, gate_bias, gnorm, t)
    dproj = jnp.concatenate([dsq, dsk, dsv, dgla], axis=1).astype(BF16)
    dw_in_p = _mm(x, dproj, m=D_MODEL, n=IN_PAD, k=t, tm=512, tn=IN_PAD, tk=tq, ta=True, name="dw_in")
    small = dict(
        gate_up=dgup_pad[:GATE_RANK], gate_bias=dgbias, gla_norm_g=dgnorm, ln1_g=dln1_g, ln1_b=dln1_b,
        conv_w=jnp.concatenate([dcw[0, :3], dcw[1, :3]], axis=1), conv_b=jnp.concatenate([dcb[0], dcb[1]], axis=1),
        ln2_g=dln2_g, ln2_b=dln2_b, loss=loss)
    g_in4 = jnp.stack([dw_in_p[:, k * W_IN_S:(k + 1) * W_IN_S] for k in range(4)], axis=0)
    from_sib_in, = _run(_sibling_rider([g_in4], SHARD_SHAPES[:1]), "exchange_sibling_w_in")
    p_in = _add_sibling(g_in4, from_sib_in, c_arr, ADD_ROWS[0], "add_sibling_w_in")
    dx, others_in, vecs = _mm(dproj, w_in_p, m=t, n=D_MODEL, k=IN_PAD, tm=512, tn=D_MODEL, tk=IN_PAD, tb=True, add=dr1,
                              add_scale=ALPHA, name="dx",
                              rider=_chips_rider([p_in], _pack_vec(small, SMALL_GRADS, GRAD_VEC_ROWS)))
    sum_in = _add_chips(p_in, others_in, kc_arr, ADD_ROWS[0], "add_chips_w_in")
    return dx, [sum_in] + late_sums, vecs


W_IN_S, W_UP_S, W_OUT_S, W_DOWN_S = IN_WIDTH // 4, 2 * D_FF // 4, D_MODEL // 4, D_FF // 4
SHARD_SHAPES = ((D_MODEL, W_IN_S), (D_MODEL, W_UP_S), (W_OUT_S, D_MODEL), (W_DOWN_S, D_MODEL))
ADD_ROWS = (256, 256, 128, 176)
LATE_SHAPES, LATE_ADD_ROWS = SHARD_SHAPES[1:], ADD_ROWS[1:]
SMALL_ROWS = 8
VEC_SIZES = (("gate_bias", GLA_KEYS), ("gla_norm_g", LANES), ("ln1_g", D_MODEL), ("ln1_b", D_MODEL),
             ("conv_b", 2 * D_FF), ("ln2_g", D_MODEL), ("ln2_b", D_MODEL))
SMALL_GRADS = VEC_SIZES + (("conv_w", 3 * 2 * D_FF), ("gate_up", GATE_RANK * GLA_KEYS), ("loss", 1))


def _rows(a):
    flat = a.reshape(-1)
    pad = (-flat.shape[0]) % D_MODEL
    if pad:
        flat = jnp.pad(flat, (0, pad))
    return flat.reshape(-1, D_MODEL)


def _pad_rows(a, rows):
    return jnp.pad(a, ((0, rows - a.shape[0]), (0, 0)))


def _pack_vec(d, sizes, rows):
    flat = jnp.concatenate([d[n].reshape(-1) for n, _ in sizes])
    return _pad_rows(_rows(flat), rows)


def _unpack_vec(v, sizes):
    flat = v.reshape(-1)
    out, o = {}, 0
    for n, size in sizes:
        out[n] = flat[o:o + size].reshape(1, size)
        o += size
    return out


VEC_ROWS = 16
GRAD_VEC_ROWS = 32


HBM_SPEC = pl.BlockSpec(memory_space=pltpu.HBM)


def _position():
    x, y, c = lax.axis_index("x"), lax.axis_index("y"), lax.axis_index("c")
    chips = [(1 - x, y), (x, 1 - y), (1 - x, 1 - y)]
    return x, y, c, chips


def _remote(src, dst, send_sems, recv_sems, k, to):
    return pltpu.make_async_remote_copy(src_ref=src, dst_ref=dst, send_sem=send_sems.at[k], recv_sem=recv_sems.at[k],
                                        device_id=to, device_id_type=MESH)


def _gather_ici(in_refs, out_refs, shapes, send_sems, recv_sems, local_sems):
    x, y, c, chips = _position()
    k_me = 2 * x + y
    local, sends, recvs = [], [], []
    for m, (rows, _) in enumerate(shapes):
        h = rows // 2
        local.append(pltpu.make_async_copy(in_refs[m], out_refs[m].at[k_me], local_sems.at[m]))
        for j, (cx, cy) in enumerate(chips):
            sends.append(_remote(in_refs[m].at[pl.ds(c * h, h), :], out_refs[m].at[k_me, pl.ds(c * h, h), :],
                                 send_sems, recv_sems, 3 * m + j, (cx, cy, c)))
            landed = out_refs[m].at[2 * cx + cy, pl.ds(c * h, h), :]
            recvs.append(_remote(landed, landed, send_sems, recv_sems, 3 * m + j, (x, y, c)))
    return local, sends, recvs


def _gather_d2d(src_refs, dst_refs, shapes, send_sems, recv_sems, base):
    x, y, c, chips = _position()
    sends, recvs = [], []
    for m, (rows, _) in enumerate(shapes):
        h = rows // 2
        for j, (cx, cy) in enumerate(chips):
            k = 2 * cx + cy
            sends.append(_remote(src_refs[m].at[k, pl.ds(c * h, h), :], dst_refs[m].at[k, pl.ds(c * h, h), :],
                                 send_sems, recv_sems, base + 3 * m + j, (x, y, 1 - c)))
            landed = dst_refs[m].at[k, pl.ds((1 - c) * h, h), :]
            recvs.append(_remote(landed, landed, send_sems, recv_sems, base + 3 * m + j, (x, y, c)))
    return sends, recvs


def _gather_weights(shards, small, shapes):
    nm = len(shards)
    n_ici = 3 * nm

    def body(*refs):
        in_refs, small_ref = refs[:nm], refs[nm]
        out_refs, osm_ref = refs[nm + 1:2 * nm + 1], refs[2 * nm + 1]
        send_sems, recv_sems, local_sems = refs[2 * nm + 2:]
        x, y, c, chips = _position()
        k_me = 2 * x + y
        local, sends, recvs = _gather_ici(in_refs, out_refs, shapes, send_sems, recv_sems, local_sems)
        local.append(pltpu.make_async_copy(small_ref, osm_ref.at[k_me], local_sems.at[nm]))
        for j, (cx, cy) in enumerate(chips):
            sends.append(_remote(small_ref, osm_ref.at[k_me], send_sems, recv_sems, n_ici + j, (cx, cy, c)))
        for cp in local + sends:
            cp.start()
        for cp in recvs:
            cp.wait_recv()
        fsends, frecvs = _gather_d2d(out_refs, out_refs, shapes, send_sems, recv_sems, n_ici + 3)
        for cp in fsends:
            cp.start()
        for j, (cx, cy) in enumerate(chips):
            k = 2 * cx + cy
            frecvs.append(_remote(osm_ref.at[k], osm_ref.at[k], send_sems, recv_sems, n_ici + j, (x, y, c)))
        for cp in frecvs:
            cp.wait_recv()
        for cp in sends + fsends:
            cp.wait_send()
        for cp in local:
            cp.wait()

    n_sems = 2 * n_ici + 3
    return pl.pallas_call(
        body, name="gather_weights", in_specs=[HBM_SPEC] * (nm + 1), out_specs=[HBM_SPEC] * (nm + 1),
        out_shape=[jax.ShapeDtypeStruct((4,) + s, BF16) for s in shapes]
        + [jax.ShapeDtypeStruct((4, SMALL_ROWS, D_MODEL), F32)],
        scratch_shapes=[pltpu.SemaphoreType.DMA((n_sems,)), pltpu.SemaphoreType.DMA((n_sems,)),
                        pltpu.SemaphoreType.DMA((nm + 1,))],
    )(*shards, small)


def _gather_rider(shards, shapes):
    n = len(shards)
    return _Rider(shards, [jax.ShapeDtypeStruct((4,) + s, BF16) for s in shapes], (3 * n, 3 * n, n),
                  lambda ins, outs, sems: _gather_ici(ins, outs, shapes, *sems))


def _forward_rider(gathered, shapes):
    n = len(gathered)
    return _Rider(gathered, [jax.ShapeDtypeStruct(a.shape, a.dtype) for a in gathered], (3 * n, 3 * n),
                  lambda ins, outs, sems: ([],) + _gather_d2d(ins, outs, shapes, sems[0], sems[1], 0),
                  aliases=[(m, m) for m in range(n)])


def _sibling_rider(gs, shapes):
    def copies(ins, outs, sems):
        x, y, c, _ = _position()
        both = []
        for m, (rows, _) in enumerate(shapes):
            h = rows // 2
            for k in range(4):
                both.append(_remote(ins[m].at[k, pl.ds((1 - c) * h, h), :], outs[m].at[k], sems[0], sems[1],
                                    4 * m + k, (x, y, 1 - c)))
        return [], both, both

    return _Rider(gs, [jax.ShapeDtypeStruct((4, r // 2, cl), F32) for r, cl in shapes], (4 * len(gs), 4 * len(gs)), copies)


def _add_sibling(g, r, c_arr, tr, name):
    _, rows, cols = g.shape
    nb = rows // 2 // tr

    def body(c_ref, g_ref, r_ref, o_ref):
        o_ref[...] = (g_ref[...] + r_ref[...]).astype(BF16)

    spec = pl.BlockSpec((None, tr, cols), lambda k, i, c: (k, i, 0))
    return pl.pallas_call(
        body, name=name,
        grid_spec=pltpu.PrefetchScalarGridSpec(
            num_scalar_prefetch=1, grid=(4, nb),
            in_specs=[pl.BlockSpec((None, tr, cols), lambda k, i, c: (k, c[0] * nb + i, 0)), spec], out_specs=spec),
        out_shape=jax.ShapeDtypeStruct((4, rows // 2, cols), BF16), compiler_params=_cparams("parallel", "parallel"),
    )(c_arr, g, r)


def _reduce_ici(p_refs, r_refs, send_sems, recv_sems):
    x, y, c, chips = _position()
    sends, recvs = [], []
    for m in range(len(p_refs)):
        for j, (cx, cy) in enumerate(chips):
            sends.append(_remote(p_refs[m].at[2 * cx + cy], r_refs[m].at[j], send_sems, recv_sems, 3 * m + j, (cx, cy, c)))
            recvs.append(_remote(r_refs[m].at[j], r_refs[m].at[j], send_sems, recv_sems, 3 * m + j, (x, y, c)))
    return sends, recvs


def _chips_rider(ps, vec=None):
    nm = len(ps)
    n_ici = 3 * nm

    def copies(ins, outs, sems):
        sends, recvs = _reduce_ici(ins[:nm], outs[:nm], sems[0], sems[1])
        if vec is None:
            return [], sends, recvs
        x, y, c, _ = _position()
        my_id = 4 * x + 2 * y + c
        vec_ref, vrecv_ref = ins[nm], outs[nm]
        local = [pltpu.make_async_copy(vec_ref, vrecv_ref.at[my_id], sems[2].at[0])]
        for r in range(1, 8):
            peer = (1 - x if r & 4 else x, 1 - y if r & 2 else y, 1 - c if r & 1 else c)
            sends.append(_remote(vec_ref, vrecv_ref.at[my_id], sems[0], sems[1], n_ici + r - 1, peer))
            recvs.append(_remote(vec_ref, vrecv_ref.at[0], sems[0], sems[1], n_ici + r - 1, (x, y, c)))
        return local, sends, recvs

    out_shapes = [jax.ShapeDtypeStruct((3,) + p.shape[1:], p.dtype) for p in ps]
    if vec is None:
        return _Rider(ps, out_shapes, (n_ici, n_ici), copies)
    return _Rider(list(ps) + [vec], out_shapes + [jax.ShapeDtypeStruct((8, GRAD_VEC_ROWS, D_MODEL), F32)],
                  (n_ici + 7, n_ici + 7, 1), copies)


def _add_chips(p, r, kc_arr, tr, name):
    _, h, cols = p.shape
    nb = h // tr

    def body(kc_ref, p_ref, r0_ref, r1_ref, r2_ref, o_ref):
        o_ref[...] = ((p_ref[...].astype(F32) + r0_ref[...].astype(F32)) + r1_ref[...].astype(F32)) + r2_ref[...].astype(F32)

    rspec = lambda j: pl.BlockSpec((None, tr, cols), lambda i, kc: (j, i, 0))
    return pl.pallas_call(
        body, name=name,
        grid_spec=pltpu.PrefetchScalarGridSpec(
            num_scalar_prefetch=1, grid=(nb,),
            in_specs=[pl.BlockSpec((None, tr, cols), lambda i, kc: (kc[0], i, 0)), rspec(0), rspec(1), rspec(2)],
            out_specs=pl.BlockSpec((tr, cols), lambda i, kc: (kc[1] * nb + i, 0))),
        out_shape=jax.ShapeDtypeStruct((2 * h, cols), F32), compiler_params=_cparams("parallel"),
    )(kc_arr, p, r, r, r)


def _reunite_sibling(fs, shapes):
    n_chunks = 2
    nm = len(fs)

    def body(*refs):
        in_refs, out_refs = refs[:nm], refs[nm:2 * nm]
        send_sems, recv_sems = refs[2 * nm:]
        x, y, c, _ = _position()
        sends, recvs = [], []
        for m in range(nm):
            ch = shapes[m][0] // 2 // n_chunks
            for q in range(n_chunks):
                mine = pl.ds((c * n_chunks + q) * ch, ch)
                theirs = pl.ds(((1 - c) * n_chunks + q) * ch, ch)
                s = n_chunks * m + q
                sends.append(_remote(in_refs[m].at[mine, :], out_refs[m].at[mine, :], send_sems, recv_sems, s, (x, y, 1 - c)))
                recvs.append(_remote(in_refs[m].at[theirs, :], out_refs[m].at[theirs, :], send_sems, recv_sems, s, (x, y, c)))
        for cp in sends:
            cp.start()
        for cp in recvs:
            cp.wait_recv()
        for cp in sends:
            cp.wait_send()

    n_sems = n_chunks * nm
    return pl.pallas_call(
        body, name="reunite_sibling", in_specs=[HBM_SPEC] * nm, out_specs=[HBM_SPEC] * nm,
        out_shape=[jax.ShapeDtypeStruct(s, F32) for s in shapes],
        input_output_aliases={m: m for m in range(nm)},
        scratch_shapes=[pltpu.SemaphoreType.DMA((n_sems,)), pltpu.SemaphoreType.DMA((n_sems,))],
    )(*fs)


def _sum_vecs(v):
    def body(v_ref, o_ref):
        acc = v_ref[0]
        for d in range(1, 8):
            acc = acc + v_ref[d]
        o_ref[...] = acc

    return pl.pallas_call(body, name="sum_vecs", out_shape=jax.ShapeDtypeStruct(v.shape[1:], F32))(v)


def kernel(x, w_in, gate_up, gate_bias, gla_norm_g, w_out, ln1_g, ln1_b, w_up, conv_w, conv_b, w_down, ln2_g, ln2_b, loss_target, m_w_in, m_gate_up, m_gate_bias, m_gla_norm_g, m_w_out, m_ln1_g, m_ln1_b, m_w_up, m_conv_w, m_conv_b, m_w_down, m_ln2_g, m_ln2_b, v_w_in, v_gate_up, v_gate_bias, v_gla_norm_g, v_w_out, v_ln1_g, v_ln1_b, v_w_up, v_conv_w, v_conv_b, v_w_down, v_ln2_g, v_ln2_b):
    xi, yi, ci = lax.axis_index("x"), lax.axis_index("y"), lax.axis_index("c")
    k_me = 2 * xi + yi
    c_arr = ci.astype(jnp.int32).reshape(1)
    kc_arr = jnp.stack([k_me, ci]).astype(jnp.int32)
    small = _pad_rows(jnp.concatenate([_rows(conv_w[0]), _rows(gate_up[0])], axis=0), SMALL_ROWS)
    w_in4, gsmall = _gather_weights([w_in[0].astype(BF16)], small, SHARD_SHAPES[:1])
    late_shards = [w_up[0].astype(BF16), w_out[0].astype(BF16), w_down[0].astype(BF16)]
    w_in_p = jnp.pad(jnp.concatenate([w_in4[k] for k in range(4)], axis=1), ((0, 0), (0, IN_PAD - IN_WIDTH)))
    conv_w_f = jnp.concatenate([gsmall[k, :5].reshape(-1)[:3 * W_UP_S].reshape(3, W_UP_S) for k in range(4)], axis=1)
    gate_up_f = jnp.concatenate([gsmall[k, 5].reshape(GATE_RANK, GLA_KEYS // 4) for k in range(4)], axis=1)
    conv_w3 = jnp.transpose(conv_w_f.reshape(3, 2, D_FF), (1, 0, 2))
    conv_b3 = conv_b.reshape(2, 1, D_FF)
    gate_up_pad = jnp.pad(gate_up_f, ((0, LANES - GATE_RANK), (0, 0)))

    dx, sums, vecs = _local_step(
        x[0], loss_target[0], w_in_p, late_shards, gate_up_pad, gate_bias, gla_norm_g, ln1_g, ln1_b, conv_w3, conv_b3,
        ln2_g, ln2_b, c_arr, kc_arr)
    g_w_in, g_w_up, g_w_out, g_w_down = _reunite_sibling(sums, SHARD_SHAPES)
    gsmall_sum = _unpack_vec(_sum_vecs(vecs), SMALL_GRADS)
    g_conv_w = lax.dynamic_slice_in_dim(gsmall_sum["conv_w"].reshape(3, 2 * D_FF), k_me * W_UP_S, W_UP_S, axis=1)
    g_gate_up = lax.dynamic_slice_in_dim(gsmall_sum["gate_up"].reshape(GATE_RANK, GLA_KEYS), k_me * (GLA_KEYS // 4),
                                         GLA_KEYS // 4, axis=1)
    gv = gsmall_sum
    loss = gv["loss"][0, 0]
    gvec = _pack_vec(gv, VEC_SIZES, VEC_ROWS)

    grads = dict(w_in=g_w_in[None], gate_up=g_gate_up[None], gate_bias=gv["gate_bias"], gla_norm_g=gv["gla_norm_g"],
                 w_out=g_w_out[None], ln1_g=gv["ln1_g"], ln1_b=gv["ln1_b"], w_up=g_w_up[None], conv_w=g_conv_w[None],
                 conv_b=gv["conv_b"], w_down=g_w_down[None], ln2_g=gv["ln2_g"], ln2_b=gv["ln2_b"])
    weights = dict(w_in=w_in, gate_up=gate_up, gate_bias=gate_bias, gla_norm_g=gla_norm_g, w_out=w_out, ln1_g=ln1_g,
                   ln1_b=ln1_b, w_up=w_up, conv_w=conv_w, conv_b=conv_b, w_down=w_down, ln2_g=ln2_g, ln2_b=ln2_b)
    ms = dict(w_in=m_w_in, gate_up=m_gate_up, gate_bias=m_gate_bias, gla_norm_g=m_gla_norm_g, w_out=m_w_out, ln1_g=m_ln1_g,
              ln1_b=m_ln1_b, w_up=m_w_up, conv_w=m_conv_w, conv_b=m_conv_b, w_down=m_w_down, ln2_g=m_ln2_g, ln2_b=m_ln2_b)
    vs = dict(w_in=v_w_in, gate_up=v_gate_up, gate_bias=v_gate_bias, gla_norm_g=v_gla_norm_g, w_out=v_w_out, ln1_g=v_ln1_g,
              ln1_b=v_ln1_b, w_up=v_w_up, conv_w=v_conv_w, conv_b=v_conv_b, w_down=v_w_down, ln2_g=v_ln2_g, ln2_b=v_ln2_b)
    names = ["w_in", "gate_up", "gate_bias", "gla_norm_g", "w_out", "ln1_g", "ln1_b", "w_up", "conv_w", "conv_b", "w_down",
             "ln2_g", "ln2_b"]
    delta, new_m, new_v = {}, {}, {}
    for n in ("w_in", "gate_up", "w_out", "w_up", "conv_w", "w_down"):
        tr_ = jnp.transpose if n == "w_in" else (lambda a: a)
        d, nm, nv = _adamw(tr_(weights[n][0]), tr_(grads[n][0]), tr_(ms[n][0]), tr_(vs[n][0]), "adamw_" + n)
        delta[n], new_m[n], new_v[n] = tr_(d)[None], tr_(nm)[None], tr_(nv)[None]
    d, nm, nv = _adamw(_pack_vec(weights, VEC_SIZES, VEC_ROWS), gvec, _pack_vec(ms, VEC_SIZES, VEC_ROWS),
                       _pack_vec(vs, VEC_SIZES, VEC_ROWS), "adamw_vectors")
    for dst, src in ((delta, d), (new_m, nm), (new_v, nv)):
        dst.update(_unpack_vec(src, VEC_SIZES))
    return (loss, dx[None], *[grads[n] for n in names], *[delta[n] for n in names], *[new_m[n] for n in names],
            *[new_v[n] for n in names])
```

```python
import functools
import math

import jax
import jax.numpy as jnp
from jax import lax
from jax.experimental import pallas as pl
from jax.experimental.pallas import tpu as pltpu

F32 = jnp.float32
BF16 = jnp.bfloat16

D_MODEL = 1024
SB_WIDTH = 512
GLA_KEYS = 256
GLA_WIDTH = 512
GATE_RANK = 16
IN_WIDTH = 3088
IN_PAD = 3200
D_FF = 2816
CHUNK = 64
LN_EPS = 1e-5
RMS_EPS = 1e-6
ALPHA = 2.0 ** 0.25
GLA_TAU = 16.0
SB_SCALE = 0.125
GLA_SCALE = 0.125
LANES = 128
SB_BLK = 256
SB_CUT = -100.0
GLA_ROWS = 256
VMEM_LIMIT = 56 * 1024 * 1024

ADAM_LR, ADAM_B1, ADAM_B2, ADAM_EPS, ADAM_WD, ADAM_STEP = 0.001, 0.9, 0.999, 1e-08, 0.01, 10

MESH = pl.DeviceIdType.MESH


def _cparams(*sem):
    return pltpu.CompilerParams(dimension_semantics=sem, vmem_limit_bytes=VMEM_LIMIT)


def _dot(a, b):
    return jnp.dot(a, b, preferred_element_type=F32)


def _dot_nt(a, b):
    return lax.dot_general(a, b, (((1,), (1,)), ((), ())), preferred_element_type=F32)


def _dot_tn(a, b):
    return lax.dot_general(a, b, (((0,), (0,)), ((), ())), preferred_element_type=F32)


def _split3(x):
    hi = x.astype(BF16)
    r = x - hi.astype(F32)
    mid = r.astype(BF16)
    lo = (r - mid.astype(F32)).astype(BF16)
    return hi, mid, lo


def _softplus(z):
    return jnp.maximum(z, 0.0) + jnp.log(1.0 + jnp.exp(-jnp.abs(z)))


def _sigmoid(z):
    return 1.0 / (1.0 + jnp.exp(-z))


def _mm(a, b, *, m, n, k, tm, tn, tk, ta=False, tb=False, a_spec=None, b_spec=None, o_spec=None,
        out_shape=None, out_dtype=F32, add=None, add_scale=1.0, rider=None, name):
    nk = k // tk
    dn = (((0 if ta else 1,), (1 if tb else 0,)), ((), ()))
    n_in = len(rider.inputs) if rider else 0
    n_out = len(rider.out_shapes) if rider else 0
    n_add = int(add is not None)
    steps = (m // tm, n // tn, nk)

    def body(*refs):
        a_ref, b_ref = refs[:2]
        add_ref = refs[2] if add is not None else None
        rin = refs[2 + n_add:2 + n_add + n_in]
        o_ref = refs[2 + n_add + n_in]
        rout = refs[3 + n_add + n_in:3 + n_add + n_in + n_out]
        scratch = refs[3 + n_add + n_in + n_out:]
        if rider:
            sems = scratch[len(scratch) - len(rider.sems):]
            ids = [pl.program_id(d) for d in range(3)]
            _ride(rider, rin, rout, sems, (ids[0] == 0) & (ids[1] == 0) & (ids[2] == 0),
                  (ids[0] == steps[0] - 1) & (ids[1] == steps[1] - 1) & (ids[2] == steps[2] - 1))

        part = lax.dot_general(a_ref[...].astype(BF16), b_ref[...].astype(BF16), dn, preferred_element_type=F32)

        def finish(r):
            if add is not None:
                r = r + add_scale * add_ref[...]
            o_ref[...] = r.astype(out_dtype)

        if nk == 1:
            finish(part)
            return
        acc_ref = scratch[0]
        kk = pl.program_id(2)

        @pl.when(kk == 0)
        def _():
            acc_ref[...] = part

        @pl.when((kk > 0) & (kk < nk - 1))
        def _():
            acc_ref[...] += part

        @pl.when(kk == nk - 1)
        def _():
            finish(acc_ref[...] + part)

    if a_spec is None:
        a_spec = pl.BlockSpec((tk, tm), lambda i, j, kk: (kk, i)) if ta else pl.BlockSpec((tm, tk), lambda i, j, kk: (i, kk))
    if b_spec is None:
        b_spec = pl.BlockSpec((tn, tk), lambda i, j, kk: (j, kk)) if tb else pl.BlockSpec((tk, tn), lambda i, j, kk: (kk, j))
    if o_spec is None:
        o_spec = pl.BlockSpec((tm, tn), lambda i, j, kk: (i, j))
    if out_shape is None:
        out_shape = jax.ShapeDtypeStruct((m, n), out_dtype)
    in_specs = [a_spec, b_spec]
    args = [a, b]
    if add is not None:
        in_specs.append(pl.BlockSpec((tm, tn), lambda i, j, kk: (i, j)))
        args.append(add)
    scratch = [pltpu.VMEM((tm, tn), F32)] if nk > 1 else []
    if not rider:
        return pl.pallas_call(
            body, name=name, grid=steps, in_specs=in_specs, out_specs=o_spec, out_shape=out_shape,
            scratch_shapes=scratch, compiler_params=_cparams("parallel", "parallel", "arbitrary"),
        )(*args)
    return pl.pallas_call(
        body, name=name, grid=steps, in_specs=in_specs + [HBM_SPEC] * n_in, out_specs=[o_spec] + [HBM_SPEC] * n_out,
        out_shape=[out_shape] + list(rider.out_shapes),
        input_output_aliases={len(args) + i: 1 + o for i, o in rider.aliases},
        scratch_shapes=scratch + [pltpu.SemaphoreType.DMA((s,)) for s in rider.sems],
        compiler_params=_cparams("arbitrary", "arbitrary", "arbitrary"),
    )(*args, *rider.inputs)


class _Rider:
    def __init__(self, inputs, out_shapes, sems, copies, aliases=()):
        self.inputs, self.out_shapes, self.sems, self.copies = list(inputs), list(out_shapes), tuple(sems), copies
        self.aliases = tuple(aliases)


def _join(a, b):
    na_in, na_out, na_sems = len(a.inputs), len(a.out_shapes), len(a.sems)

    def copies(ins, outs, sems):
        first = a.copies(ins[:na_in], outs[:na_out], sems[:na_sems])
        second = b.copies(ins[na_in:], outs[na_out:], sems[na_sems:])
        return tuple(u + v for u, v in zip(first, second))

    return _Rider(a.inputs + b.inputs, a.out_shapes + b.out_shapes, a.sems + b.sems, copies,
                  a.aliases + tuple((i + na_in, o + na_out) for i, o in b.aliases))


def _ride(rider, in_refs, out_refs, sems, first, last):
    @pl.when(first)
    def _():
        local, sends, _ = rider.copies(in_refs, out_refs, sems)
        for cp in local + sends:
            cp.start()

    @pl.when(last)
    def _():
        local, sends, recvs = rider.copies(in_refs, out_refs, sems)
        for cp in recvs:
            cp.wait_recv()
        for cp in sends:
            cp.wait_send()
        for cp in local:
            cp.wait()


def _run(rider, name):
    n_in, n_out = len(rider.inputs), len(rider.out_shapes)

    def body(*refs):
        local, sends, recvs = rider.copies(refs[:n_in], refs[n_in:n_in + n_out], refs[n_in + n_out:])
        for cp in local + sends:
            cp.start()
        for cp in recvs:
            cp.wait_recv()
        for cp in sends:
            cp.wait_send()
        for cp in local:
            cp.wait()

    return pl.pallas_call(
        body, name=name, in_specs=[HBM_SPEC] * n_in, out_specs=[HBM_SPEC] * n_out, out_shape=list(rider.out_shapes),
        scratch_shapes=[pltpu.SemaphoreType.DMA((s,)) for s in rider.sems],
    )(*rider.inputs)


def _sb_tile(qh, kj, diag, strict, u_strict, r_in):
    z = _dot_nt(qh, kj)
    sp = _softplus(z)
    l1m = -sp
    lsz = z - sp
    if diag:
        l1m = jnp.where(strict, l1m, 0.0)
    cs = _dot(l1m.astype(BF16), u_strict) + r_in
    w = jnp.exp(lsz + cs)
    if diag:
        w = jnp.where(strict, w, 0.0)
    return l1m, lsz, w


def _sb_consts():
    row = lax.broadcasted_iota(jnp.int32, (SB_BLK, SB_BLK), 0)
    col = lax.broadcasted_iota(jnp.int32, (SB_BLK, SB_BLK), 1)
    strict = col < row
    u_strict = (row > col).astype(BF16)
    u_pre = (row < col).astype(BF16)
    lane = lax.broadcasted_iota(jnp.int32, (1, LANES), 1)
    return strict, u_strict, u_pre, lane


def _sb_fwd(proj, t, rider):
    nq = t // SB_BLK
    n_in, n_out = len(rider.inputs), len(rider.out_shapes)

    def body(q_ref, k_ref, v_ref, *rest):
        rin, o_ref, sv_ref, rout = rest[:n_in], rest[n_in], rest[n_in + 1], rest[n_in + 2:n_in + 2 + n_out]
        p = pl.program_id(0)
        i = pl.program_id(1)
        _ride(rider, rin, rout, rest[n_in + 2 + n_out:], (p == 0) & (i == 0), (p == 3) & (i == nq - 1))

        strict, u_strict, _, lane = _sb_consts()
        qf = q_ref[...] * SB_SCALE
        hms = [(lane // 64) == hh for hh in range(2)]
        qhs = [jnp.where(hm, qf, 0.0).astype(BF16) for hm in hms]

        def step(j, r0, r1, a, sv, diag, keep=None):
            rows = pl.ds(pl.multiple_of(j * SB_BLK, SB_BLK), SB_BLK)
            kj = k_ref[rows, :].astype(BF16)
            vf = v_ref[rows, :]
            rs = []
            for hh, r in enumerate((r0, r1)):
                l1m, _, w = _sb_tile(qhs[hh], kj, diag, strict, u_strict, r)
                pv = _dot(w.astype(BF16), jnp.where(hms[hh], vf, 0.0).astype(BF16))
                mass = jnp.sum(l1m, axis=1, keepdims=True)
                if keep is not None:
                    pv, mass = jnp.where(keep, pv, 0.0), jnp.where(keep, mass, 0.0)
                a = a + pv
                rs.append(r + mass)
                sv = jnp.where(lane == hh * 64 + (i - j + 1), rs[hh], sv)
            return rs[0], rs[1], a, sv

        zero = jnp.zeros((SB_BLK, 1), F32)
        acc0 = jnp.zeros((SB_BLK, LANES), F32)
        r0, r1, acc, sv = step(i, zero, zero, acc0, acc0, True)
        r0, r1, acc, sv = step(jnp.maximum(i - 1, 0), r0, r1, acc, sv, False, keep=i > 0)
        _, _, _, acc, sv = lax.while_loop(
            lambda c: (c[0] >= 0) & (jnp.maximum(jnp.max(c[1]), jnp.max(c[2])) > SB_CUT),
            lambda c: (c[0] - 1,) + step(c[0], c[1], c[2], c[3], c[4], False),
            (i - 2, r0, r1, acc, sv))
        o_ref[...] = acc.astype(BF16)
        sv_ref[...] = sv

    return pl.pallas_call(
        body, name="sb_fwd", grid=(4, nq),
        in_specs=[pl.BlockSpec((SB_BLK, LANES), lambda p, i: (i, p)),
                  pl.BlockSpec((t, LANES), lambda p, i: (0, 4 + p)),
                  pl.BlockSpec((t, LANES), lambda p, i: (0, 8 + p))] + [HBM_SPEC] * n_in,
        out_specs=[pl.BlockSpec((SB_BLK, LANES), lambda p, i: (i, p))] * 2 + [HBM_SPEC] * n_out,
        out_shape=[jax.ShapeDtypeStruct((t, SB_WIDTH), BF16), jax.ShapeDtypeStruct((t, SB_WIDTH), F32)] + rider.out_shapes,
        scratch_shapes=[pltpu.SemaphoreType.DMA((s,)) for s in rider.sems],
        compiler_params=_cparams("arbitrary", "arbitrary"),
    )(proj, proj, proj, *rider.inputs)


def _sb_bwd(proj, dcat, mass, t, rider):
    nq = t // SB_BLK
    n_in, n_out = len(rider.inputs), len(rider.out_shapes)

    def body(q_ref, k_ref, v_ref, do_ref, sv_ref, *rest):
        rin = rest[:n_in]
        dq_ref, dk_ref, dv_ref = rest[n_in:n_in + 3]
        rout = rest[n_in + 3:n_in + 3 + n_out]
        p = pl.program_id(0)
        i = pl.program_id(1)
        _ride(rider, rin, rout, rest[n_in + 3 + n_out:], (p == 0) & (i == 0), (p == 3) & (i == nq - 1))

        @pl.when(i == 0)
        def _():
            dk_ref[...] = jnp.zeros_like(dk_ref)
            dv_ref[...] = jnp.zeros_like(dv_ref)

        strict, u_strict, u_pre, lane = _sb_consts()
        qf = q_ref[...] * SB_SCALE
        dof = do_ref[...]
        hms = [(lane // 64) == hh for hh in range(2)]
        qhs = [jnp.where(hm, qf, 0.0).astype(BF16) for hm in hms]
        dohs = [jnp.where(hm, dof, 0.0).astype(BF16) for hm in hms]

        sv = sv_ref[...]
        zero = jnp.zeros((SB_BLK, 1), F32)

        def mass_right(hh, d):
            return jnp.sum(jnp.where(lane == hh * 64 + d, sv, 0.0), axis=1, keepdims=True)

        dstop = lax.while_loop(
            lambda d: (i - d >= 0) & (jnp.maximum(jnp.max(mass_right(0, d)), jnp.max(mass_right(1, d))) > SB_CUT),
            lambda d: d + 1, 1)
        jstop = i - dstop

        def step(j, carry, diag, keep=None):
            pre_g0, pre_g1, dqa = carry
            rows = pl.ds(pl.multiple_of(j * SB_BLK, SB_BLK), SB_BLK)
            kf = k_ref[rows, :]
            kj = kf.astype(BF16)
            vj = v_ref[rows, :].astype(BF16)
            dv = jnp.zeros((SB_BLK, LANES), F32)
            dk = jnp.zeros((SB_BLK, LANES), F32)
            dqj = jnp.zeros((SB_BLK, LANES), F32)
            pre = []
            for hh, pre_g in enumerate((pre_g0, pre_g1)):
                _, lsz, w = _sb_tile(qhs[hh], kj, diag, strict, u_strict, zero if diag else mass_right(hh, i - j))
                g = w * _dot_nt(dohs[hh], vj)
                gpre = _dot(g.astype(BF16), u_pre) + pre_g
                sig = jnp.exp(lsz)
                dz = g * (1.0 - sig) - gpre * sig
                if diag:
                    dz = jnp.where(strict, dz, 0.0)
                dzb = dz.astype(BF16)
                dv = dv + _dot_tn(w.astype(BF16), dohs[hh])
                dk = dk + _dot_tn(dzb, qhs[hh])
                dqj = dqj + _dot(dzb, jnp.where(hms[hh], kf, 0.0).astype(BF16))
                gsum = jnp.sum(g, axis=1, keepdims=True)
                pre.append(pre_g + (gsum if keep is None else jnp.where(keep, gsum, 0.0)))
            if keep is not None:
                dv, dk, dqj = jnp.where(keep, dv, 0.0), jnp.where(keep, dk, 0.0), jnp.where(keep, dqj, 0.0)
            dv_ref[rows, :] += dv
            dk_ref[rows, :] += dk
            return pre[0], pre[1], dqa + dqj

        carry = lax.fori_loop(jstop + 1, i - 1, lambda j, c: step(j, c, False), (zero, zero, jnp.zeros((SB_BLK, LANES), F32)))
        carry = step(jnp.maximum(i - 1, 0), carry, False, keep=i > 0)
        _, _, dq = step(i, carry, True)
        dq_ref[...] = dq * SB_SCALE

    return pl.pallas_call(
        body, name="sb_bwd", grid=(4, nq),
        in_specs=[pl.BlockSpec((SB_BLK, LANES), lambda p, i: (i, p)),
                  pl.BlockSpec((t, LANES), lambda p, i: (0, 4 + p)),
                  pl.BlockSpec((t, LANES), lambda p, i: (0, 8 + p)),
                  pl.BlockSpec((SB_BLK, LANES), lambda p, i: (i, p)),
                  pl.BlockSpec((SB_BLK, LANES), lambda p, i: (i, p))] + [HBM_SPEC] * n_in,
        out_specs=[pl.BlockSpec((SB_BLK, LANES), lambda p, i: (i, p)),
                   pl.BlockSpec((t, LANES), lambda p, i: (0, p)),
                   pl.BlockSpec((t, LANES), lambda p, i: (0, p))] + [HBM_SPEC] * n_out,
        out_shape=[jax.ShapeDtypeStruct((t, SB_WIDTH), F32)] * 3 + rider.out_shapes,
        scratch_shapes=[pltpu.SemaphoreType.DMA((s,)) for s in rider.sems],
        compiler_params=_cparams("arbitrary", "arbitrary"),
    )(proj, proj, proj, dcat, mass, *rider.inputs)


def _gla_consts():
    r = lax.broadcasted_iota(jnp.int32, (CHUNK, CHUNK), 0)
    c = lax.broadcasted_iota(jnp.int32, (CHUNK, CHUNK), 1)
    causal = c <= r
    lc = causal.astype(BF16)
    lct = (c >= r).astype(BF16)
    rowid = lax.broadcasted_iota(jnp.int32, (CHUNK, 1), 0)
    lane = lax.broadcasted_iota(jnp.int32, (1, LANES), 1)
    sr = lax.broadcasted_iota(jnp.int32, (LANES, 2 * LANES), 0)
    sc = lax.broadcasted_iota(jnp.int32, (LANES, 2 * LANES), 1)
    blockdiag = (sr // 64) == (sc // LANES)
    return causal, lc, lct, rowid, lane, blockdiag


def _dot3(u, x):
    hi, mid, lo = _split3(x)
    return _dot(u, hi) + _dot(u, mid) + _dot(u, lo)


def _row_to_col(row):
    return jnp.transpose(jnp.broadcast_to(row, (LANES, LANES)))


def _gla_gates(ga_ref, gup_ref, gbias_ref):
    pre = _dot(ga_ref[...].astype(BF16), gup_ref[...].astype(BF16)) + gbias_ref[...]
    log_a = (jnp.minimum(pre, 0.0) - jnp.log(1.0 + jnp.exp(-jnp.abs(pre)))) / GLA_TAU
    return pre, log_a


def _gla_chunk_terms(g2, q2, k2, lc, rowid):
    b = _dot3(lc, g2)
    b_ref = jnp.sum(jnp.where(rowid == CHUNK // 2 - 1, b, 0.0), axis=0, keepdims=True)
    b_last = jnp.sum(jnp.where(rowid == CHUNK - 1, b, 0.0), axis=0, keepdims=True)
    qs = q2 * GLA_SCALE
    e_q = jnp.exp(b - b_ref)
    e_k = jnp.exp(b_ref - b)
    e_d = jnp.exp(b_last - b)
    e_b = jnp.exp(b)
    decay = jnp.exp(b_last)
    return dict(qs=qs, e_q=e_q, e_k=e_k, e_d=e_d, e_b=e_b, decay=decay,
                qi=qs * e_q, ki=k2 * e_k, kd=k2 * e_d, qb=qs * e_b)


def _gla_fwd(proj, gate_up_pad, gate_bias, gnorm, t, rider):
    nsteps = t // GLA_ROWS
    cps = GLA_ROWS // CHUNK
    n_in, n_out = len(rider.inputs), len(rider.out_shapes)

    def body(q_ref, k_ref, v_ref, gg_ref, ga_ref, gup_ref, gbias_ref, gn_ref, *rest):
        o_ref, oraw_ref, st_ref = rest[n_in:n_in + 3]
        s_scr = rest[n_in + 3 + n_out]
        i = pl.program_id(0)
        _ride(rider, rest[:n_in], rest[n_in + 3:n_in + 3 + n_out], rest[n_in + 4 + n_out:], i == 0, i == nsteps - 1)

        @pl.when(i == 0)
        def _():
            s_scr[...] = jnp.zeros_like(s_scr)

        causal, lc, _, rowid, lane, blockdiag = _gla_consts()
        _, log_a = _gla_gates(ga_ref, gup_ref, gbias_ref)
        gn = gn_ref[...]
        state = [s_scr[0], s_scr[1]]
        for cc in range(cps):
            rows = slice(cc * CHUNK, (cc + 1) * CHUNK)
            for p in range(2):
                kl = slice(p * LANES, (p + 1) * LANES)
                vl = slice(p * 2 * LANES, (p + 1) * 2 * LANES)
                tm = _gla_chunk_terms(log_a[rows, kl], q_ref[rows, kl], k_ref[rows, kl], lc, rowid)
                v2 = v_ref[rows, vl]
                v2b = v2.astype(BF16)
                s_prev = state[p]
                st_ref[cc, p] = s_prev
                kib = tm["ki"].astype(BF16)
                o_inter = _dot(tm["qb"].astype(BF16), s_prev.astype(BF16))
                outs = []
                for hh in range(2):
                    hm = (lane // 64) == hh
                    a = _dot_nt(jnp.where(hm, tm["qi"], 0.0).astype(BF16), kib)
                    a = jnp.where(causal, a, 0.0)
                    outs.append(_dot(a.astype(BF16), v2b[:, hh * LANES:(hh + 1) * LANES]))
                o2 = jnp.concatenate(outs, axis=1) + o_inter
                upd = jnp.where(blockdiag, _dot_tn(tm["kd"].astype(BF16), v2b), 0.0)
                dcol = _row_to_col(tm["decay"])
                state[p] = s_prev * jnp.concatenate([dcol, dcol], axis=1) + upd
                oraw_ref[rows, vl] = o2
                for hh in range(2):
                    hl = slice(hh * LANES, (hh + 1) * LANES)
                    oh = o2[:, hl]
                    gl = slice(p * 2 * LANES + hh * LANES, p * 2 * LANES + (hh + 1) * LANES)
                    rinv = lax.rsqrt(jnp.mean(oh * oh, axis=1, keepdims=True) + RMS_EPS)
                    gg = gg_ref[rows, gl]
                    o_ref[rows, gl] = (oh * rinv * gn * (gg * _sigmoid(gg))).astype(BF16)
        s_scr[0] = state[0]
        s_scr[1] = state[1]

    cb = lambda w, idx: pl.BlockSpec((GLA_ROWS, w), lambda i: (i, idx))
    full = lambda shp: pl.BlockSpec(shp, lambda i: tuple(0 for _ in shp))
    return pl.pallas_call(
        body, name="gla_fwd", grid=(nsteps,),
        in_specs=[cb(256, 6), cb(256, 7), cb(512, 4), cb(512, 5), cb(128, 24),
                  full((LANES, GLA_KEYS)), full((1, GLA_KEYS)), full((1, LANES))] + [HBM_SPEC] * n_in,
        out_specs=[pl.BlockSpec((GLA_ROWS, GLA_WIDTH), lambda i: (i, 0)),
                   pl.BlockSpec((GLA_ROWS, GLA_WIDTH), lambda i: (i, 0)),
                   pl.BlockSpec((cps, 2, LANES, 2 * LANES), lambda i: (i, 0, 0, 0))] + [HBM_SPEC] * n_out,
        out_shape=[jax.ShapeDtypeStruct((t, GLA_WIDTH), BF16), jax.ShapeDtypeStruct((t, GLA_WIDTH), F32),
                   jax.ShapeDtypeStruct((t // CHUNK, 2, LANES, 2 * LANES), F32)]
        + rider.out_shapes,
        input_output_aliases={8 + i: 3 + o for i, o in rider.aliases},
        scratch_shapes=[pltpu.VMEM((2, LANES, 2 * LANES), F32)] + [pltpu.SemaphoreType.DMA((s,)) for s in rider.sems],
        compiler_params=_cparams("arbitrary"),
    )(proj, proj, proj, proj, proj, gate_up_pad, gate_bias, gnorm, *rider.inputs)


def _gla_bwd(proj, dcat, oraw, states, gate_up_pad, gate_bias, gnorm, t):
    nsteps = t // GLA_ROWS
    cps = GLA_ROWS // CHUNK
    wout = 2 * GLA_KEYS + 2 * GLA_WIDTH + LANES

    def body(q_ref, k_ref, v_ref, gg_ref, ga_ref, do_ref, oraw_ref, st_ref, gup_ref, gbias_ref, gn_ref,
             d_ref, dgup_ref, dgbias_ref, dgn_ref, ds_scr):
        i = pl.program_id(0)

        @pl.when(i == 0)
        def _():
            ds_scr[...] = jnp.zeros_like(ds_scr)
            dgup_ref[...] = jnp.zeros_like(dgup_ref)
            dgbias_ref[...] = jnp.zeros_like(dgbias_ref)
            dgn_ref[...] = jnp.zeros_like(dgn_ref)

        causal, lc, lct, rowid, lane, blockdiag = _gla_consts()
        pre, log_a = _gla_gates(ga_ref, gup_ref, gbias_ref)
        gn = gn_ref[...]
        dgn = jnp.zeros((1, LANES), F32)
        dstate = [ds_scr[0], ds_scr[1]]
        dgs = [[None, None] for _ in range(cps)]
        for cc in reversed(range(cps)):
            rows = slice(cc * CHUNK, (cc + 1) * CHUNK)
            for p in range(2):
                kl = slice(p * LANES, (p + 1) * LANES)
                vl = slice(p * 2 * LANES, (p + 1) * 2 * LANES)
                tm = _gla_chunk_terms(log_a[rows, kl], q_ref[rows, kl], k_ref[rows, kl], lc, rowid)
                v2b = v_ref[rows, vl].astype(BF16)
                s_prev = st_ref[cc, p]
                ds2 = dstate[p]
                dos = []
                for hh in range(2):
                    gl = slice(p * 2 * LANES + hh * LANES, p * 2 * LANES + (hh + 1) * LANES)
                    oh = oraw_ref[rows, gl]
                    rinv = lax.rsqrt(jnp.mean(oh * oh, axis=1, keepdims=True) + RMS_EPS)
                    on = oh * rinv
                    gg = gg_ref[rows, gl]
                    sg = _sigmoid(gg)
                    sil = gg * sg
                    dgo = do_ref[rows, gl]
                    d_ref[rows, 2 * GLA_KEYS + GLA_WIDTH + gl.start:2 * GLA_KEYS + GLA_WIDTH + gl.stop] = (
                        dgo * on * gn * (sg * (1.0 + gg * (1.0 - sg))))
                    dgn = dgn + jnp.sum(dgo * sil * on, axis=0, keepdims=True)
                    don = dgo * sil * gn
                    dos.append(rinv * (don - on * jnp.mean(don * on, axis=1, keepdims=True)))
                do2b = jnp.concatenate(dos, axis=1).astype(BF16)
                qib = tm["qi"].astype(BF16)
                kib = tm["ki"].astype(BF16)
                kdb = tm["kd"].astype(BF16)
                qbb = tm["qb"].astype(BF16)
                ds2b = ds2.astype(BF16)
                dqi = jnp.zeros((CHUNK, LANES), F32)
                dki = jnp.zeros((CHUNK, LANES), F32)
                dvs = []
                for hh in range(2):
                    hm = (lane // 64) == hh
                    hl = slice(hh * LANES, (hh + 1) * LANES)
                    a = jnp.where(causal, _dot_nt(jnp.where(hm, tm["qi"], 0.0).astype(BF16), kib), 0.0).astype(BF16)
                    da = jnp.where(causal, _dot_nt(do2b[:, hl], v2b[:, hl]), 0.0).astype(BF16)
                    dvs.append(_dot_tn(a, do2b[:, hl]))
                    dqi = dqi + jnp.where(hm, _dot(da, kib), 0.0)
                    dki = dki + jnp.where(hm, _dot_tn(da, qib), 0.0)
                dv2 = jnp.concatenate(dvs, axis=1) + _dot(kdb, ds2b)
                dqb = _dot_nt(do2b, s_prev.astype(BF16))
                dkd = _dot_nt(v2b, ds2b)
                dcol = _row_to_col(tm["decay"])
                dsp = jnp.where(blockdiag, _dot_tn(qbb, do2b), 0.0) + ds2 * jnp.concatenate([dcol, dcol], axis=1)
                ddecay_col = jnp.sum(ds2 * s_prev, axis=1, keepdims=True)
                ddecay_row = jnp.transpose(jnp.broadcast_to(ddecay_col, (LANES, LANES)))[0:1, :]
                dstate[p] = dsp
                dqs = dqi * tm["e_q"] + dqb * tm["e_b"]
                dk = dki * tm["e_k"] + dkd * tm["e_d"]
                t_qi = dqi * tm["qi"]
                t_ki = dki * tm["ki"]
                t_kd = dkd * tm["kd"]
                db = t_qi - t_ki + dqb * tm["qb"] - t_kd
                db_ref = jnp.sum(t_ki - t_qi, axis=0, keepdims=True)
                db_last = jnp.sum(t_kd, axis=0, keepdims=True) + ddecay_row * tm["decay"]
                db = db + jnp.where(rowid == CHUNK // 2 - 1, db_ref, 0.0) + jnp.where(rowid == CHUNK - 1, db_last, 0.0)
                dg = _dot3(lct, db)
                d_ref[rows, p * LANES:(p + 1) * LANES] = dqs * GLA_SCALE
                d_ref[rows, GLA_KEYS + p * LANES:GLA_KEYS + (p + 1) * LANES] = dk
                d_ref[rows, 2 * GLA_KEYS + p * 2 * LANES:2 * GLA_KEYS + (p + 1) * 2 * LANES] = dv2
                dgs[cc][p] = dg
        ds_scr[0] = dstate[0]
        ds_scr[1] = dstate[1]
        dlog_a = jnp.concatenate([jnp.concatenate(row, axis=1) for row in dgs], axis=0)
        dpre = dlog_a * (1.0 / GLA_TAU) * _sigmoid(-pre)
        dpb = dpre.astype(BF16)
        dgn_ref[...] += dgn
        dgbias_ref[...] += jnp.sum(dpre, axis=0, keepdims=True)
        dgup_ref[...] += _dot_tn(ga_ref[...].astype(BF16), dpb)
        d_ref[:, 2 * GLA_KEYS + 2 * GLA_WIDTH:] = _dot_nt(dpb, gup_ref[...].astype(BF16))

    rev = lambda i: nsteps - 1 - i
    cb = lambda w, idx: pl.BlockSpec((GLA_ROWS, w), lambda i: (rev(i), idx))
    full = lambda shp: pl.BlockSpec(shp, lambda i: tuple(0 for _ in shp))
    return pl.pallas_call(
        body, name="gla_bwd", grid=(nsteps,),
        in_specs=[cb(256, 6), cb(256, 7), cb(512, 4), cb(512, 5), cb(128, 24), cb(512, 1), cb(512, 0),
                  pl.BlockSpec((cps, 2, LANES, 2 * LANES), lambda i: (rev(i), 0, 0, 0)),
                  full((LANES, GLA_KEYS)), full((1, GLA_KEYS)), full((1, LANES))],
        out_specs=[pl.BlockSpec((GLA_ROWS, wout), lambda i: (rev(i), 0)),
                   full((LANES, GLA_KEYS)), full((1, GLA_KEYS)), full((1, LANES))],
        out_shape=[jax.ShapeDtypeStruct((t, wout), F32), jax.ShapeDtypeStruct((LANES, GLA_KEYS), F32),
                   jax.ShapeDtypeStruct((1, GLA_KEYS), F32), jax.ShapeDtypeStruct((1, LANES), F32)],
        scratch_shapes=[pltpu.VMEM((2, LANES, 2 * LANES), F32)],
        compiler_params=_cparams("arbitrary"),
    )(proj, proj, proj, proj, proj, dcat, oraw, states, gate_up_pad, gate_bias, gnorm)


def _ln_stats(r):
    mu = jnp.mean(r, axis=1, keepdims=True)
    xc = r - mu
    rstd = lax.rsqrt(jnp.mean(xc * xc, axis=1, keepdims=True) + LN_EPS)
    return xc * rstd, rstd


def _ln_bwd(dy_g, xhat, rstd):
    return rstd * (dy_g - jnp.mean(dy_g, axis=1, keepdims=True) - xhat * jnp.mean(dy_g * xhat, axis=1, keepdims=True))


def _outproj_ln1(sb_o, gla_o, x, w_out, g1, b1, t, tm=256):
    def body(sb_ref, gl_ref, x_ref, w_ref, g_ref, b_ref, xhat_ref, rstd_ref, h_ref):
        mix = _dot(sb_ref[...], w_ref[0:SB_WIDTH, :]) + _dot(gl_ref[...], w_ref[SB_WIDTH:, :])
        xhat, rstd = _ln_stats(ALPHA * x_ref[...] + mix)
        xhat_ref[...] = xhat
        rstd_ref[...] = rstd
        h_ref[...] = (xhat * g_ref[...] + b_ref[...]).astype(BF16)

    row = lambda w: pl.BlockSpec((tm, w), lambda i: (i, 0))
    full = lambda shp: pl.BlockSpec(shp, lambda i: (0, 0))
    return pl.pallas_call(
        body, name="outproj_ln1", grid=(t // tm,),
        in_specs=[row(SB_WIDTH), row(GLA_WIDTH), row(D_MODEL), full((D_MODEL, D_MODEL)), full((1, D_MODEL)), full((1, D_MODEL))],
        out_specs=[row(D_MODEL), row(1), row(D_MODEL)],
        out_shape=[jax.ShapeDtypeStruct((t, D_MODEL), F32), jax.ShapeDtypeStruct((t, 1), F32),
                   jax.ShapeDtypeStruct((t, D_MODEL), BF16)],
        compiler_params=_cparams("parallel"),
    )(sb_o, gla_o, x, w_out, g1, b1)


_INV_SQRT2 = 1.0 / math.sqrt(2.0)
_INV_SQRT2PI = 1.0 / math.sqrt(2.0 * math.pi)


def _conv3(xs, w_ref, b_ref, half):
    return (w_ref[half, 0:1, :] * pltpu.roll(xs, 2, 0) + w_ref[half, 1:2, :] * pltpu.roll(xs, 1, 0)
            + w_ref[half, 2:3, :] * xs + b_ref[half])


HALO = 16


def _conv_gelu_fwd(up3, conv_w3, conv_b3, t, tr=512, ct=256):
    nct = D_FF // ct
    hb = tr // HALO

    def body(cur_ref, prev_ref, w_ref, b_ref, gm_ref):
        i = pl.program_id(1)
        keep = (i > 0).astype(F32)
        us = []
        for half in range(2):
            xs = jnp.concatenate([prev_ref[half].astype(F32) * keep, cur_ref[half].astype(F32)], axis=0)
            us.append(_conv3(xs, w_ref, b_ref, half)[HALO:, :])
        a, c = us
        gelu = 0.5 * a * (1.0 + lax.erf(a * _INV_SQRT2))
        gm_ref[...] = (gelu * c).astype(BF16)

    return pl.pallas_call(
        body, name="conv_gelu_fwd", grid=(nct, t // tr),
        in_specs=[pl.BlockSpec((2, tr, ct), lambda j, i: (0, i, j)),
                  pl.BlockSpec((2, HALO, ct), lambda j, i: (0, jnp.maximum(i * hb - 1, 0), j)),
                  pl.BlockSpec((2, 3, ct), lambda j, i: (0, 0, j)),
                  pl.BlockSpec((2, 1, ct), lambda j, i: (0, 0, j))],
        out_specs=pl.BlockSpec((tr, ct), lambda j, i: (i, j)),
        out_shape=jax.ShapeDtypeStruct((t, D_FF), BF16),
        compiler_params=_cparams("parallel", "parallel"),
    )(up3, up3, conv_w3, conv_b3)


def _conv_gelu_bwd(up3, dgm, conv_w3, conv_b3, t, tr=512, ct=256):
    nct = D_FF // ct
    nrt = t // tr
    hb = tr // HALO
    n = tr + 2 * HALO
    lo, hi = HALO, tr + HALO

    def body(cur_ref, prev_ref, next_ref, dg_ref, dgn_ref, w_ref, b_ref, dup_ref, dcw_ref, dcb_ref):
        i = pl.program_id(1)

        @pl.when(i == 0)
        def _():
            dcw_ref[...] = jnp.zeros_like(dcw_ref)
            dcb_ref[...] = jnp.zeros_like(dcb_ref)

        keep_prev = (i > 0).astype(F32)
        keep_next = (i < nrt - 1).astype(F32)
        xs, xm1, xm2, us = [], [], [], []
        for half in range(2):
            x = jnp.concatenate([prev_ref[half].astype(F32) * keep_prev, cur_ref[half].astype(F32),
                                 next_ref[half].astype(F32)], axis=0)
            xs.append(x)
            xm1.append(pltpu.roll(x, 1, 0))
            xm2.append(pltpu.roll(x, 2, 0))
            us.append(w_ref[half, 0:1, :] * xm2[half] + w_ref[half, 1:2, :] * xm1[half]
                      + w_ref[half, 2:3, :] * x + b_ref[half])
        a, c = us
        dg = jnp.concatenate([jnp.zeros((HALO, ct), F32), dg_ref[...].astype(F32),
                              dgn_ref[...].astype(F32) * keep_next], axis=0)
        cdf = 0.5 * (1.0 + lax.erf(a * _INV_SQRT2))
        pdf = jnp.exp(-0.5 * a * a) * _INV_SQRT2PI
        dus = [dg * c * (cdf + a * pdf), dg * (a * cdf)]
        rid = lax.broadcasted_iota(jnp.int32, (8, 1), 0)
        for half in range(2):
            du = dus[half]
            dup = (w_ref[half, 2:3, :] * du + w_ref[half, 1:2, :] * pltpu.roll(du, n - 1, 0)
                   + w_ref[half, 0:1, :] * pltpu.roll(du, n - 2, 0))
            dup_ref[half] = dup[lo:hi, :].astype(BF16)
            duc = du[lo:hi, :]
            s0 = jnp.sum(duc * xm2[half][lo:hi, :], axis=0, keepdims=True)
            s1 = jnp.sum(duc * xm1[half][lo:hi, :], axis=0, keepdims=True)
            s2 = jnp.sum(duc * xs[half][lo:hi, :], axis=0, keepdims=True)
            dcw_ref[half] += jnp.where(rid == 0, s0, jnp.where(rid == 1, s1, jnp.where(rid == 2, s2, 0.0)))
            dcb_ref[half] += jnp.sum(duc, axis=0, keepdims=True)

    last = t // HALO - 1
    return pl.pallas_call(
        body, name="conv_gelu_bwd", grid=(nct, nrt),
        in_specs=[pl.BlockSpec((2, tr, ct), lambda j, i: (0, i, j)),
                  pl.BlockSpec((2, HALO, ct), lambda j, i: (0, jnp.maximum(i * hb - 1, 0), j)),
                  pl.BlockSpec((2, HALO, ct), lambda j, i: (0, jnp.minimum((i + 1) * hb, last), j)),
                  pl.BlockSpec((tr, ct), lambda j, i: (i, j)),
                  pl.BlockSpec((HALO, ct), lambda j, i: (jnp.minimum((i + 1) * hb, last), j)),
                  pl.BlockSpec((2, 3, ct), lambda j, i: (0, 0, j)),
                  pl.BlockSpec((2, 1, ct), lambda j, i: (0, 0, j))],
        out_specs=[pl.BlockSpec((2, tr, ct), lambda j, i: (0, i, j)),
                   pl.BlockSpec((2, 8, ct), lambda j, i: (0, 0, j)),
                   pl.BlockSpec((2, 1, ct), lambda j, i: (0, 0, j))],
        out_shape=[jax.ShapeDtypeStruct((2, t, D_FF), BF16), jax.ShapeDtypeStruct((2, 8, D_FF), F32),
                   jax.ShapeDtypeStruct((2, 1, D_FF), F32)],
        compiler_params=_cparams("parallel", "arbitrary"),
    )(up3, up3, up3, dgm, dgm, conv_w3, conv_b3)


def _down_ln2_loss(gm, w_down, xhat1, g1, b1, g2, b2, target, t, tm=256):
    def body(gm_ref, w_ref, xh_ref, g1_ref, b1_ref, g2_ref, b2_ref, tg_ref, dr_ref, loss_ref, dg_ref, db_ref):
        i = pl.program_id(0)

        @pl.when(i == 0)
        def _():
            loss_ref[...] = jnp.zeros_like(loss_ref)
            dg_ref[...] = jnp.zeros_like(dg_ref)
            db_ref[...] = jnp.zeros_like(db_ref)

        h = xh_ref[...] * g1_ref[...] + b1_ref[...]
        xhat, rstd = _ln_stats(ALPHA * h + _dot(gm_ref[...], w_ref[...]))
        err = xhat * g2_ref[...] + b2_ref[...] - tg_ref[...]
        loss_ref[...] += 0.5 * jnp.sum(jnp.sum(err * err, axis=1, keepdims=True), axis=0, keepdims=True) / D_MODEL
        dy = err * (1.0 / D_MODEL)
        dg_ref[...] += jnp.sum(dy * xhat, axis=0, keepdims=True)
        db_ref[...] += jnp.sum(dy, axis=0, keepdims=True)
        dr_ref[...] = _ln_bwd(dy * g2_ref[...], xhat, rstd)

    row = lambda w: pl.BlockSpec((tm, w), lambda i: (i, 0))
    full = lambda shp: pl.BlockSpec(shp, lambda i: (0, 0))
    vec = full((1, D_MODEL))
    return pl.pallas_call(
        body, name="down_ln2_loss", grid=(t // tm,),
        in_specs=[row(D_FF), full((D_FF, D_MODEL)), row(D_MODEL), vec, vec, vec, vec, row(D_MODEL)],
        out_specs=[row(D_MODEL), full((1, 1)), vec, vec],
        out_shape=[jax.ShapeDtypeStruct((t, D_MODEL), F32), jax.ShapeDtypeStruct((1, 1), F32),
                   jax.ShapeDtypeStruct((1, D_MODEL), F32), jax.ShapeDtypeStruct((1, D_MODEL), F32)],
        compiler_params=_cparams("arbitrary"),
    )(gm, w_down, xhat1, g1, b1, g2, b2, target)


def _dh_ln1_bwd(dup3, w_up4, dr2, xhat1, rstd1, g1, t, tm=512):
    tk = 2 * D_FF // 4
    nkh = D_FF // tk
    nk = 2 * nkh

    def body(a_ref, w_ref, dr2_ref, xh_ref, rs_ref, g_ref, dr1_ref, dg_ref, db_ref, acc_ref):
        i = pl.program_id(0)
        kk = pl.program_id(1)

        @pl.when((i == 0) & (kk == 0))
        def _():
            dg_ref[...] = jnp.zeros_like(dg_ref)
            db_ref[...] = jnp.zeros_like(db_ref)

        part = _dot_nt(a_ref[...], w_ref[...])

        @pl.when(kk == 0)
        def _():
            acc_ref[...] = part

        @pl.when(kk > 0)
        def _():
            acc_ref[...] += part

        @pl.when(kk == nk - 1)
        def _():
            dh = acc_ref[...] + ALPHA * dr2_ref[...]
            xhat = xh_ref[...]
            dg_ref[...] += jnp.sum(dh * xhat, axis=0, keepdims=True)
            db_ref[...] += jnp.sum(dh, axis=0, keepdims=True)
            dr1_ref[...] = _ln_bwd(dh * g_ref[...], xhat, rs_ref[...])

    row = lambda w: pl.BlockSpec((tm, w), lambda i, kk: (i, 0))
    vec = pl.BlockSpec((1, D_MODEL), lambda i, kk: (0, 0))
    return pl.pallas_call(
        body, name="dh_ln1_bwd", grid=(t // tm, nk),
        in_specs=[pl.BlockSpec((None, tm, tk), lambda i, kk: (kk // nkh, i, kk % nkh)),
                  pl.BlockSpec((None, D_MODEL, tk), lambda i, kk: (kk, 0, 0)),
                  row(D_MODEL), row(D_MODEL), row(1), vec],
        out_specs=[row(D_MODEL), vec, vec],
        out_shape=[jax.ShapeDtypeStruct((t, D_MODEL), F32), jax.ShapeDtypeStruct((1, D_MODEL), F32),
                   jax.ShapeDtypeStruct((1, D_MODEL), F32)],
        scratch_shapes=[pltpu.VMEM((tm, D_MODEL), F32)],
        compiler_params=_cparams("arbitrary", "arbitrary"),
    )(dup3, w_up4, dr2, xhat1, rstd1, g1)


def _adamw(w, g, m, v, name):
    rows, cols = w.shape
    tr = rows
    for cand in (256, 128, 64, 32, 16, 8):
        if rows % cand == 0 and rows > cand:
            tr = cand
            break
    c1 = 1.0 / (1.0 - ADAM_B1 ** ADAM_STEP)
    c2 = 1.0 / (1.0 - ADAM_B2 ** ADAM_STEP)

    def body(w_ref, g_ref, m_ref, v_ref, d_ref, nm_ref, nv_ref):
        gv = g_ref[...]
        nm = ADAM_B1 * m_ref[...] + (1.0 - ADAM_B1) * gv
        nv = ADAM_B2 * v_ref[...] + (1.0 - ADAM_B2) * (gv * gv)
        d_ref[...] = -ADAM_LR * ((nm * c1) / (jnp.sqrt(nv * c2) + ADAM_EPS) + ADAM_WD * w_ref[...])
        nm_ref[...] = nm
        nv_ref[...] = nv

    spec = pl.BlockSpec((tr, cols), lambda i: (i, 0))
    out = jax.ShapeDtypeStruct((rows, cols), F32)
    return pl.pallas_call(
        body, name=name, grid=(rows // tr,), in_specs=[spec] * 4, out_specs=[spec] * 3, out_shape=[out] * 3,
        compiler_params=_cparams("parallel"),
    )(w, g, m, v)


def _local_step(x, target, w_in_p, late_shards, gate_up_pad, gate_bias, gnorm, ln1_g, ln1_b, conv_w3, conv_b3,
                ln2_g, ln2_b, c_arr, kc_arr):
    t = x.shape[0]
    tq = min(t, 1024)
    s_up, s_out, s_down = late_shards
    sh_up, sh_out, sh_down = LATE_SHAPES
    proj, out_partly = _mm(x, w_in_p, m=t, n=IN_PAD, k=D_MODEL, tm=tq, tn=640, tk=D_MODEL, name="proj",
                           rider=_gather_rider([s_out], [sh_out]))
    sb_o, sb_mass, up_partly = _sb_fwd(proj, t, _gather_rider([s_up], [sh_up]))
    gla_o, oraw, states, w_up4, w_out4, down_partly = _gla_fwd(
        proj, gate_up_pad, gate_bias, gnorm, t,
        _join(_forward_rider([up_partly, out_partly], [sh_up, sh_out]), _gather_rider([s_down], [sh_down])))
    w_out = w_out4.reshape(D_MODEL, D_MODEL)
    xhat1, rstd1, h_bf = _outproj_ln1(sb_o, gla_o, x, w_out, ln1_g, ln1_b, t)
    up3, w_down4 = _mm(h_bf, w_up4, m=t, n=2 * D_FF, k=D_MODEL, tm=tq, tn=W_UP_S, tk=D_MODEL, name="up",
                       b_spec=pl.BlockSpec((None, D_MODEL, W_UP_S), lambda i, j, kk: (j, 0, 0)),
                       o_spec=pl.BlockSpec((None, tq, W_UP_S), lambda i, j, kk: (j // 2, i, j % 2)),
                       out_shape=jax.ShapeDtypeStruct((2, t, D_FF), BF16), out_dtype=BF16,
                       rider=_forward_rider([down_partly], [sh_down]))
    w_down = w_down4.reshape(D_FF, D_MODEL)
    gm = _conv_gelu_fwd(up3, conv_w3, conv_b3, t, tr=tq)
    dr2, loss, dln2_g, dln2_b = _down_ln2_loss(gm, w_down, xhat1, ln1_g, ln1_b, ln2_g, ln2_b, target, t)
    dgm = _mm(dr2, w_down, m=t, n=D_FF, k=D_MODEL, tm=tq, tn=W_UP_S, tk=D_MODEL, tb=True, out_dtype=BF16, name="dgm")
    dw_down = _mm(gm, dr2, m=D_FF, n=D_MODEL, k=t, tm=W_UP_S, tn=D_MODEL, tk=tq, ta=True, name="dw_down")
    dup3, dcw, dcb = _conv_gelu_bwd(up3, dgm, conv_w3, conv_b3, t)
    dr1, dln1_g, dln1_b = _dh_ln1_bwd(dup3, w_up4, dr2, xhat1, rstd1, ln1_g, t)
    dw_up4 = _mm(h_bf, dup3, m=D_MODEL, n=2 * D_FF, k=t, tm=512, tn=W_UP_S, tk=t, ta=True, name="dw_up",
                 b_spec=pl.BlockSpec((None, t, W_UP_S), lambda i, j, kk: (j // 2, kk, j % 2)),
                 o_spec=pl.BlockSpec((None, 512, W_UP_S), lambda i, j, kk: (j, i, 0)),
                 out_shape=jax.ShapeDtypeStruct((4, D_MODEL, W_UP_S), F32))
    dw_out_sb = _mm(sb_o, dr1, m=SB_WIDTH, n=D_MODEL, k=t, tm=512, tn=D_MODEL, tk=tq, ta=True, name="dw_out_sb")
    dw_out_gla = _mm(gla_o, dr1, m=GLA_WIDTH, n=D_MODEL, k=t, tm=512, tn=D_MODEL, tk=tq, ta=True, name="dw_out_gla")
    gs = [dw_up4, jnp.concatenate([dw_out_sb, dw_out_gla], axis=0).reshape(4, W_OUT_S, D_MODEL),
          dw_down.reshape(4, W_DOWN_S, D_MODEL)]
    dcat, *from_sib = _mm(dr1, w_out, m=t, n=D_MODEL, k=D_MODEL, tm=tq, tn=512, tk=D_MODEL, tb=True, name="dcat",
                          rider=_sibling_rider(gs, LATE_SHAPES))
    ps = [_add_sibling(gs[m], from_sib[m], c_arr, LATE_ADD_ROWS[m], "add_sibling_late_%d" % m) for m in range(3)]
    dsq, dsk, dsv, *others = _sb_bwd(proj, dcat, sb_mass, t, _chips_rider(ps))
    late_sums = [_add_chips(ps[m], others[m], kc_arr, LATE_ADD_ROWS[m], "add_chips_late_%d" % m) for m in range(3)]
    dgla, dgup_pad, dgbias, dgnorm = _gla_bwd(proj, dcat, oraw, states, gate_up_pad, gate_bias, gnorm, t)
    dproj = jnp.concatenate([dsq, dsk, dsv, dgla], axis=1).astype(BF16)
    dw_in_p = _mm(x, dproj, m=D_MODEL, n=IN_PAD, k=t, tm=512, tn=IN_PAD, tk=tq, ta=True, name="dw_in")
    small = dict(
        gate_up=dgup_pad[:GATE_RANK], gate_bias=dgbias, gla_norm_g=dgnorm, ln1_g=dln1_g, ln1_b=dln1_b,
        conv_w=jnp.concatenate([dcw[0, :3], dcw[1, :3]], axis=1), conv_b=jnp.concatenate([dcb[0], dcb[1]], axis=1),
        ln2_g=dln2_g, ln2_b=dln2_b, loss=loss)
    g_in4 = jnp.stack([dw_in_p[:, k * W_IN_S:(k + 1) * W_IN_S] for k in range(4)], axis=0)
    from_sib_in, = _run(_sibling_rider([g_in4], SHARD_SHAPES[:1]), "exchange_sibling_w_in")
    p_in = _add_sibling(g_in4, from_sib_in, c_arr, ADD_ROWS[0], "add_sibling_w_in")
    dx, others_in, vecs = _mm(dproj, w_in_p, m=t, n=D_MODEL, k=IN_PAD, tm=512, tn=D_MODEL, tk=IN_PAD, tb=True, add=dr1,
                              add_scale=ALPHA, name="dx",
                              rider=_chips_rider([p_in], _pack_vec(small, SMALL_GRADS, GRAD_VEC_ROWS)))
    sum_in = _add_chips(p_in, others_in, kc_arr, ADD_ROWS[0], "add_chips_w_in")
    return dx, [sum_in] + late_sums, vecs


W_IN_S, W_UP_S, W_OUT_S, W_DOWN_S = IN_WIDTH // 4, 2 * D_FF // 4, D_MODEL // 4, D_FF // 4
SHARD_SHAPES = ((D_MODEL, W_IN_S), (D_MODEL, W_UP_S), (W_OUT_S, D_MODEL), (W_DOWN_S, D_MODEL))
ADD_ROWS = (256, 256, 128, 176)
LATE_SHAPES, LATE_ADD_ROWS = SHARD_SHAPES[1:], ADD_ROWS[1:]
SMALL_ROWS = 8
VEC_SIZES = (("gate_bias", GLA_KEYS), ("gla_norm_g", LANES), ("ln1_g", D_MODEL), ("ln1_b", D_MODEL),
             ("conv_b", 2 * D_FF), ("ln2_g", D_MODEL), ("ln2_b", D_MODEL))
SMALL_GRADS = VEC_SIZES + (("conv_w", 3 * 2 * D_FF), ("gate_up", GATE_RANK * GLA_KEYS), ("loss", 1))


def _rows(a):
    flat = a.reshape(-1)
    pad = (-flat.shape[0]) % D_MODEL
    if pad:
        flat = jnp.pad(flat, (0, pad))
    return flat.reshape(-1, D_MODEL)


def _pad_rows(a, rows):
    return jnp.pad(a, ((0, rows - a.shape[0]), (0, 0)))


def _pack_vec(d, sizes, rows):
    flat = jnp.concatenate([d[n].reshape(-1) for n, _ in sizes])
    return _pad_rows(_rows(flat), rows)


def _unpack_vec(v, sizes):
    flat = v.reshape(-1)
    out, o = {}, 0
    for n, size in sizes:
        out[n] = flat[o:o + size].reshape(1, size)
        o += size
    return out


VEC_ROWS = 16
GRAD_VEC_ROWS = 32


HBM_SPEC = pl.BlockSpec(memory_space=pltpu.HBM)


def _position():
    x, y, c = lax.axis_index("x"), lax.axis_index("y"), lax.axis_index("c")
    chips = [(1 - x, y), (x, 1 - y), (1 - x, 1 - y)]
    return x, y, c, chips


def _remote(src, dst, send_sems, recv_sems, k, to):
    return pltpu.make_async_remote_copy(src_ref=src, dst_ref=dst, send_sem=send_sems.at[k], recv_sem=recv_sems.at[k],
                                        device_id=to, device_id_type=MESH)


def _gather_ici(in_refs, out_refs, shapes, send_sems, recv_sems, local_sems):
    x, y, c, chips = _position()
    k_me = 2 * x + y
    local, sends, recvs = [], [], []
    for m, (rows, _) in enumerate(shapes):
        h = rows // 2
        local.append(pltpu.make_async_copy(in_refs[m], out_refs[m].at[k_me], local_sems.at[m]))
        for j, (cx, cy) in enumerate(chips):
            sends.append(_remote(in_refs[m].at[pl.ds(c * h, h), :], out_refs[m].at[k_me, pl.ds(c * h, h), :],
                                 send_sems, recv_sems, 3 * m + j, (cx, cy, c)))
            landed = out_refs[m].at[2 * cx + cy, pl.ds(c * h, h), :]
            recvs.append(_remote(landed, landed, send_sems, recv_sems, 3 * m + j, (x, y, c)))
    return local, sends, recvs


def _gather_d2d(src_refs, dst_refs, shapes, send_sems, recv_sems, base):
    x, y, c, chips = _position()
    sends, recvs = [], []
    for m, (rows, _) in enumerate(shapes):
        h = rows // 2
        for j, (cx, cy) in enumerate(chips):
            k = 2 * cx + cy
            sends.append(_remote(src_refs[m].at[k, pl.ds(c * h, h), :], dst_refs[m].at[k, pl.ds(c * h, h), :],
                                 send_sems, recv_sems, base + 3 * m + j, (x, y, 1 - c)))
            landed = dst_refs[m].at[k, pl.ds((1 - c) * h, h), :]
            recvs.append(_remote(landed, landed, send_sems, recv_sems, base + 3 * m + j, (x, y, c)))
    return sends, recvs


def _gather_weights(shards, small, shapes):
    nm = len(shards)
    n_ici = 3 * nm

    def body(*refs):
        in_refs, small_ref = refs[:nm], refs[nm]
        out_refs, osm_ref = refs[nm + 1:2 * nm + 1], refs[2 * nm + 1]
        send_sems, recv_sems, local_sems = refs[2 * nm + 2:]
        x, y, c, chips = _position()
        k_me = 2 * x + y
        local, sends, recvs = _gather_ici(in_refs, out_refs, shapes, send_sems, recv_sems, local_sems)
        local.append(pltpu.make_async_copy(small_ref, osm_ref.at[k_me], local_sems.at[nm]))
        for j, (cx, cy) in enumerate(chips):
            sends.append(_remote(small_ref, osm_ref.at[k_me], send_sems, recv_sems, n_ici + j, (cx, cy, c)))
        for cp in local + sends:
            cp.start()
        for cp in recvs:
            cp.wait_recv()
        fsends, frecvs = _gather_d2d(out_refs, out_refs, shapes, send_sems, recv_sems, n_ici + 3)
        for cp in fsends:
            cp.start()
        for j, (cx, cy) in enumerate(chips):
            k = 2 * cx + cy
            frecvs.append(_remote(osm_ref.at[k], osm_ref.at[k], send_sems, recv_sems, n_ici + j, (x, y, c)))
        for cp in frecvs:
            cp.wait_recv()
        for cp in sends + fsends:
            cp.wait_send()
        for cp in local:
            cp.wait()

    n_sems = 2 * n_ici + 3
    return pl.pallas_call(
        body, name="gather_weights", in_specs=[HBM_SPEC] * (nm + 1), out_specs=[HBM_SPEC] * (nm + 1),
        out_shape=[jax.ShapeDtypeStruct((4,) + s, BF16) for s in shapes]
        + [jax.ShapeDtypeStruct((4, SMALL_ROWS, D_MODEL), F32)],
        scratch_shapes=[pltpu.SemaphoreType.DMA((n_sems,)), pltpu.SemaphoreType.DMA((n_sems,)),
                        pltpu.SemaphoreType.DMA((nm + 1,))],
    )(*shards, small)


def _gather_rider(shards, shapes):
    n = len(shards)
    return _Rider(shards, [jax.ShapeDtypeStruct((4,) + s, BF16) for s in shapes], (3 * n, 3 * n, n),
                  lambda ins, outs, sems: _gather_ici(ins, outs, shapes, *sems))


def _forward_rider(gathered, shapes):
    n = len(gathered)
    return _Rider(gathered, [jax.ShapeDtypeStruct(a.shape, a.dtype) for a in gathered], (3 * n, 3 * n),
                  lambda ins, outs, sems: ([],) + _gather_d2d(ins, outs, shapes, sems[0], sems[1], 0),
                  aliases=[(m, m) for m in range(n)])


def _sibling_rider(gs, shapes):
    def copies(ins, outs, sems):
        x, y, c, _ = _position()
        both = []
        for m, (rows, _) in enumerate(shapes):
            h = rows // 2
            for k in range(4):
                both.append(_remote(ins[m].at[k, pl.ds((1 - c) * h, h), :], outs[m].at[k], sems[0], sems[1],
                                    4 * m + k, (x, y, 1 - c)))
        return [], both, both

    return _Rider(gs, [jax.ShapeDtypeStruct((4, r // 2, cl), F32) for r, cl in shapes], (4 * len(gs), 4 * len(gs)), copies)


def _add_sibling(g, r, c_arr, tr, name):
    _, rows, cols = g.shape
    nb = rows // 2 // tr

    def body(c_ref, g_ref, r_ref, o_ref):
        o_ref[...] = (g_ref[...] + r_ref[...]).astype(BF16)

    spec = pl.BlockSpec((None, tr, cols), lambda k, i, c: (k, i, 0))
    return pl.pallas_call(
        body, name=name,
        grid_spec=pltpu.PrefetchScalarGridSpec(
            num_scalar_prefetch=1, grid=(4, nb),
            in_specs=[pl.BlockSpec((None, tr, cols), lambda k, i, c: (k, c[0] * nb + i, 0)), spec], out_specs=spec),
        out_shape=jax.ShapeDtypeStruct((4, rows // 2, cols), BF16), compiler_params=_cparams("parallel", "parallel"),
    )(c_arr, g, r)


def _reduce_ici(p_refs, r_refs, send_sems, recv_sems):
    x, y, c, chips = _position()
    sends, recvs = [], []
    for m in range(len(p_refs)):
        for j, (cx, cy) in enumerate(chips):
            sends.append(_remote(p_refs[m].at[2 * cx + cy], r_refs[m].at[j], send_sems, recv_sems, 3 * m + j, (cx, cy, c)))
            recvs.append(_remote(r_refs[m].at[j], r_refs[m].at[j], send_sems, recv_sems, 3 * m + j, (x, y, c)))
    return sends, recvs


def _chips_rider(ps, vec=None):
    nm = len(ps)
    n_ici = 3 * nm

    def copies(ins, outs, sems):
        sends, recvs = _reduce_ici(ins[:nm], outs[:nm], sems[0], sems[1])
        if vec is None:
            return [], sends, recvs
        x, y, c, _ = _position()
        my_id = 4 * x + 2 * y + c
        vec_ref, vrecv_ref = ins[nm], outs[nm]
        local = [pltpu.make_async_copy(vec_ref, vrecv_ref.at[my_id], sems[2].at[0])]
        for r in range(1, 8):
            peer = (1 - x if r & 4 else x, 1 - y if r & 2 else y, 1 - c if r & 1 else c)
            sends.append(_remote(vec_ref, vrecv_ref.at[my_id], sems[0], sems[1], n_ici + r - 1, peer))
            recvs.append(_remote(vec_ref, vrecv_ref.at[0], sems[0], sems[1], n_ici + r - 1, (x, y, c)))
        return local, sends, recvs

    out_shapes = [jax.ShapeDtypeStruct((3,) + p.shape[1:], p.dtype) for p in ps]
    if vec is None:
        return _Rider(ps, out_shapes, (n_ici, n_ici), copies)
    return _Rider(list(ps) + [vec], out_shapes + [jax.ShapeDtypeStruct((8, GRAD_VEC_ROWS, D_MODEL), F32)],
                  (n_ici + 7, n_ici + 7, 1), copies)


def _add_chips(p, r, kc_arr, tr, name):
    _, h, cols = p.shape
    nb = h // tr

    def body(kc_ref, p_ref, r0_ref, r1_ref, r2_ref, o_ref):
        o_ref[...] = ((p_ref[...].astype(F32) + r0_ref[...].astype(F32)) + r1_ref[...].astype(F32)) + r2_ref[...].astype(F32)

    rspec = lambda j: pl.BlockSpec((None, tr, cols), lambda i, kc: (j, i, 0))
    return pl.pallas_call(
        body, name=name,
        grid_spec=pltpu.PrefetchScalarGridSpec(
            num_scalar_prefetch=1, grid=(nb,),
            in_specs=[pl.BlockSpec((None, tr, cols), lambda i, kc: (kc[0], i, 0)), rspec(0), rspec(1), rspec(2)],
            out_specs=pl.BlockSpec((tr, cols), lambda i, kc: (kc[1] * nb + i, 0))),
        out_shape=jax.ShapeDtypeStruct((2 * h, cols), F32), compiler_params=_cparams("parallel"),
    )(kc_arr, p, r, r, r)


def _reunite_sibling(fs, shapes):
    n_chunks = 2
    nm = len(fs)

    def body(*refs):
        in_refs, out_refs = refs[:nm], refs[nm:2 * nm]
        send_sems, recv_sems = refs[2 * nm:]
        x, y, c, _ = _position()
        sends, recvs = [], []
        for m in range(nm):
            ch = shapes[m][0] // 2 // n_chunks
            for q in range(n_chunks):
                mine = pl.ds((c * n_chunks + q) * ch, ch)
                theirs = pl.ds(((1 - c) * n_chunks + q) * ch, ch)
                s = n_chunks * m + q
                sends.append(_remote(in_refs[m].at[mine, :], out_refs[m].at[mine, :], send_sems, recv_sems, s, (x, y, 1 - c)))
                recvs.append(_remote(in_refs[m].at[theirs, :], out_refs[m].at[theirs, :], send_sems, recv_sems, s, (x, y, c)))
        for cp in sends:
            cp.start()
        for cp in recvs:
            cp.wait_recv()
        for cp in sends:
            cp.wait_send()

    n_sems = n_chunks * nm
    return pl.pallas_call(
        body, name="reunite_sibling", in_specs=[HBM_SPEC] * nm, out_specs=[HBM_SPEC] * nm,
        out_shape=[jax.ShapeDtypeStruct(s, F32) for s in shapes],
        input_output_aliases={m: m for m in range(nm)},
        scratch_shapes=[pltpu.SemaphoreType.DMA((n_sems,)), pltpu.SemaphoreType.DMA((n_sems,))],
    )(*fs)


def _sum_vecs(v):
    def body(v_ref, o_ref):
        acc = v_ref[0]
        for d in range(1, 8):
            acc = acc + v_ref[d]
        o_ref[...] = acc

    return pl.pallas_call(body, name="sum_vecs", out_shape=jax.ShapeDtypeStruct(v.shape[1:], F32))(v)


def kernel(x, w_in, gate_up, gate_bias, gla_norm_g, w_out, ln1_g, ln1_b, w_up, conv_w, conv_b, w_down, ln2_g, ln2_b, loss_target, m_w_in, m_gate_up, m_gate_bias, m_gla_norm_g, m_w_out, m_ln1_g, m_ln1_b, m_w_up, m_conv_w, m_conv_b, m_w_down, m_ln2_g, m_ln2_b, v_w_in, v_gate_up, v_gate_bias, v_gla_norm_g, v_w_out, v_ln1_g, v_ln1_b, v_w_up, v_conv_w, v_conv_b, v_w_down, v_ln2_g, v_ln2_b):
    xi, yi, ci = lax.axis_index("x"), lax.axis_index("y"), lax.axis_index("c")
    k_me = 2 * xi + yi
    c_arr = ci.astype(jnp.int32).reshape(1)
    kc_arr = jnp.stack([k_me, ci]).astype(jnp.int32)
    small = _pad_rows(jnp.concatenate([_rows(conv_w[0]), _rows(gate_up[0])], axis=0), SMALL_ROWS)
    w_in4, gsmall = _gather_weights([w_in[0].astype(BF16)], small, SHARD_SHAPES[:1])
    late_shards = [w_up[0].astype(BF16), w_out[0].astype(BF16), w_down[0].astype(BF16)]
    w_in_p = jnp.pad(jnp.concatenate([w_in4[k] for k in range(4)], axis=1), ((0, 0), (0, IN_PAD - IN_WIDTH)))
    conv_w_f = jnp.concatenate([gsmall[k, :5].reshape(-1)[:3 * W_UP_S].reshape(3, W_UP_S) for k in range(4)], axis=1)
    gate_up_f = jnp.concatenate([gsmall[k, 5].reshape(GATE_RANK, GLA_KEYS // 4) for k in range(4)], axis=1)
    conv_w3 = jnp.transpose(conv_w_f.reshape(3, 2, D_FF), (1, 0, 2))
    conv_b3 = conv_b.reshape(2, 1, D_FF)
    gate_up_pad = jnp.pad(gate_up_f, ((0, LANES - GATE_RANK), (0, 0)))

    dx, sums, vecs = _local_step(
        x[0], loss_target[0], w_in_p, late_shards, gate_up_pad, gate_bias, gla_norm_g, ln1_g, ln1_b, conv_w3, conv_b3,
        ln2_g, ln2_b, c_arr, kc_arr)
    g_w_in, g_w_up, g_w_out, g_w_down = _reunite_sibling(sums, SHARD_SHAPES)
    gsmall_sum = _unpack_vec(_sum_vecs(vecs), SMALL_GRADS)
    g_conv_w = lax.dynamic_slice_in_dim(gsmall_sum["conv_w"].reshape(3, 2 * D_FF), k_me * W_UP_S, W_UP_S, axis=1)
    g_gate_up = lax.dynamic_slice_in_dim(gsmall_sum["gate_up"].reshape(GATE_RANK, GLA_KEYS), k_me * (GLA_KEYS // 4),
                                         GLA_KEYS // 4, axis=1)
    gv = gsmall_sum
    loss = gv["loss"][0, 0]
    gvec = _pack_vec(gv, VEC_SIZES, VEC_ROWS)

    grads = dict(w_in=g_w_in[None], gate_up=g_gate_up[None], gate_bias=gv["gate_bias"], gla_norm_g=gv["gla_norm_g"],
                 w_out=g_w_out[None], ln1_g=gv["ln1_g"], ln1_b=gv["ln1_b"], w_up=g_w_up[None], conv_w=g_conv_w[None],
                 conv_b=gv["conv_b"], w_down=g_w_down[None], ln2_g=gv["ln2_g"], ln2_b=gv["ln2_b"])
    weights = dict(w_in=w_in, gate_up=gate_up, gate_bias=gate_bias, gla_norm_g=gla_norm_g, w_out=w_out, ln1_g=ln1_g,
                   ln1_b=ln1_b, w_up=w_up, conv_w=conv_w, conv_b=conv_b, w_down=w_down, ln2_g=ln2_g, ln2_b=ln2_b)
    ms = dict(w_in=m_w_in, gate_up=m_gate_up, gate_bias=m_gate_bias, gla_norm_g=m_gla_norm_g, w_out=m_w_out, ln1_g=m_ln1_g,
              ln1_b=m_ln1_b, w_up=m_w_up, conv_w=m_conv_w, conv_b=m_conv_b, w_down=m_w_down, ln2_g=m_ln2_g, ln2_b=m_ln2_b)
    vs = dict(w_in=v_w_in, gate_up=v_gate_up, gate_bias=v_gate_bias, gla_norm_g=v_gla_norm_g, w_out=v_w_out, ln1_g=v_ln1_g,
              ln1_b=v_ln1_b, w_up=v_w_up, conv_w=v_conv_w, conv_b=v_conv_b, w_down=v_w_down, ln2_g=v_ln2_g, ln2_b=v_ln2_b)
    names = ["w_in", "gate_up", "gate_bias", "gla_norm_g", "w_out", "ln1_g", "ln1_b", "w_up", "conv_w", "conv_b", "w_down",
             "ln2_g", "ln2_b"]
    delta, new_m, new_v = {}, {}, {}
    for n in ("w_in", "gate_up", "w_out", "w_up", "conv_w", "w_down"):
        tr_ = jnp.transpose if n == "w_in" else (lambda a: a)
        d, nm, nv = _adamw(tr_(weights[n][0]), tr_(grads[n][0]), tr_(ms[n][0]), tr_(vs[n][0]), "adamw_" + n)
        delta[n], new_m[n], new_v[n] = tr_(d)[None], tr_(nm)[None], tr_(nv)[None]
    d, nm, nv = _adamw(_pack_vec(weights, VEC_SIZES, VEC_ROWS), gvec, _pack_vec(ms, VEC_SIZES, VEC_ROWS),
                       _pack_vec(vs, VEC_SIZES, VEC_ROWS), "adamw_vectors")
    for dst, src in ((delta, d), (new_m, nm), (new_v, nv)):
        dst.update(_unpack_vec(src, VEC_SIZES))
    return (loss, dx[None], *[grads[n] for n in names], *[delta[n] for n in names], *[new_m[n] for n in names],
            *[new_v[n] for n in names])
```

```python
import functools
import math

import jax
import jax.numpy as jnp
from jax import lax
from jax.experimental import pallas as pl
from jax.experimental.pallas import tpu as pltpu

F32 = jnp.float32
BF16 = jnp.bfloat16

D_MODEL = 1024
SB_WIDTH = 512
GLA_KEYS = 256
GLA_WIDTH = 512
GATE_RANK = 16
IN_WIDTH = 3088
IN_PAD = 3200
D_FF = 2816
CHUNK = 64
LN_EPS = 1e-5
RMS_EPS = 1e-6
ALPHA = 2.0 ** 0.25
GLA_TAU = 16.0
SB_SCALE = 0.125
GLA_SCALE = 0.125
LANES = 128
SB_BLK = 256
SB_CUT = -100.0
GLA_ROWS = 256
VMEM_LIMIT = 56 * 1024 * 1024

ADAM_LR, ADAM_B1, ADAM_B2, ADAM_EPS, ADAM_WD, ADAM_STEP = 0.001, 0.9, 0.999, 1e-08, 0.01, 10

MESH = pl.DeviceIdType.MESH


def _cparams(*sem):
    return pltpu.CompilerParams(dimension_semantics=sem, vmem_limit_bytes=VMEM_LIMIT)


def _dot(a, b):
    return jnp.dot(a, b, preferred_element_type=F32)


def _dot_nt(a, b):
    return lax.dot_general(a, b, (((1,), (1,)), ((), ())), preferred_element_type=F32)


def _dot_tn(a, b):
    return lax.dot_general(a, b, (((0,), (0,)), ((), ())), preferred_element_type=F32)


def _split3(x):
    hi = x.astype(BF16)
    r = x - hi.astype(F32)
    mid = r.astype(BF16)
    lo = (r - mid.astype(F32)).astype(BF16)
    return hi, mid, lo


def _softplus(z):
    return jnp.maximum(z, 0.0) + jnp.log(1.0 + jnp.exp(-jnp.abs(z)))


def _sigmoid(z):
    return 1.0 / (1.0 + jnp.exp(-z))


def _mm(a, b, *, m, n, k, tm, tn, tk, ta=False, tb=False, a_spec=None, b_spec=None, o_spec=None,
        out_shape=None, out_dtype=F32, add=None, add_scale=1.0, rider=None, name):
    nk = k // tk
    dn = (((0 if ta else 1,), (1 if tb else 0,)), ((), ()))
    n_in = len(rider.inputs) if rider else 0
    n_out = len(rider.out_shapes) if rider else 0
    n_add = int(add is not None)
    steps = (m // tm, n // tn, nk)

    def body(*refs):
        a_ref, b_ref = refs[:2]
        add_ref = refs[2] if add is not None else None
        rin = refs[2 + n_add:2 + n_add + n_in]
        o_ref = refs[2 + n_add + n_in]
        rout = refs[3 + n_add + n_in:3 + n_add + n_in + n_out]
        scratch = refs[3 + n_add + n_in + n_out:]
        if rider:
            sems = scratch[len(scratch) - len(rider.sems):]
            ids = [pl.program_id(d) for d in range(3)]
            _ride(rider, rin, rout, sems, (ids[0] == 0) & (ids[1] == 0) & (ids[2] == 0),
                  (ids[0] == steps[0] - 1) & (ids[1] == steps[1] - 1) & (ids[2] == steps[2] - 1))

        part = lax.dot_general(a_ref[...].astype(BF16), b_ref[...].astype(BF16), dn, preferred_element_type=F32)

        def finish(r):
            if add is not None:
                r = r + add_scale * add_ref[...]
            o_ref[...] = r.astype(out_dtype)

        if nk == 1:
            finish(part)
            return
        acc_ref = scratch[0]
        kk = pl.program_id(2)

        @pl.when(kk == 0)
        def _():
            acc_ref[...] = part

        @pl.when((kk > 0) & (kk < nk - 1))
        def _():
            acc_ref[...] += part

        @pl.when(kk == nk - 1)
        def _():
            finish(acc_ref[...] + part)

    if a_spec is None:
        a_spec = pl.BlockSpec((tk, tm), lambda i, j, kk: (kk, i)) if ta else pl.BlockSpec((tm, tk), lambda i, j, kk: (i, kk))
    if b_spec is None:
        b_spec = pl.BlockSpec((tn, tk), lambda i, j, kk: (j, kk)) if tb else pl.BlockSpec((tk, tn), lambda i, j, kk: (kk, j))
    if o_spec is None:
        o_spec = pl.BlockSpec((tm, tn), lambda i, j, kk: (i, j))
    if out_shape is None:
        out_shape = jax.ShapeDtypeStruct((m, n), out_dtype)
    in_specs = [a_spec, b_spec]
    args = [a, b]
    if add is not None:
        in_specs.append(pl.BlockSpec((tm, tn), lambda i, j, kk: (i, j)))
        args.append(add)
    scratch = [pltpu.VMEM((tm, tn), F32)] if nk > 1 else []
    if not rider:
        return pl.pallas_call(
            body, name=name, grid=steps, in_specs=in_specs, out_specs=o_spec, out_shape=out_shape,
            scratch_shapes=scratch, compiler_params=_cparams("parallel", "parallel", "arbitrary"),
        )(*args)
    return pl.pallas_call(
        body, name=name, grid=steps, in_specs=in_specs + [HBM_SPEC] * n_in, out_specs=[o_spec] + [HBM_SPEC] * n_out,
        out_shape=[out_shape] + list(rider.out_shapes),
        input_output_aliases={len(args) + i: 1 + o for i, o in rider.aliases},
        scratch_shapes=scratch + [pltpu.SemaphoreType.DMA((s,)) for s in rider.sems],
        compiler_params=_cparams("arbitrary", "arbitrary", "arbitrary"),
    )(*args, *rider.inputs)


def _col_offsets(pieces):
    offs, o = [], 0
    for a in pieces:
        offs.append(o)
        o += a.shape[1]
    return offs


def _dx(pieces, w_in_p, dr1, t, rider, tm=512):
    offs = _col_offsets(pieces)
    npc = len(pieces)
    n_in, n_out = len(rider.inputs), len(rider.out_shapes)
    steps = t // tm

    def body(*refs):
        p_refs, w_ref, add_ref = refs[:npc], refs[npc], refs[npc + 1]
        rin = refs[npc + 2:npc + 2 + n_in]
        o_ref = refs[npc + 2 + n_in]
        rout = refs[npc + 3 + n_in:npc + 3 + n_in + n_out]
        i = pl.program_id(0)
        _ride(rider, rin, rout, refs[npc + 3 + n_in + n_out:], i == 0, i == steps - 1)
        acc = ALPHA * add_ref[...]
        for p_ref, off in zip(p_refs, offs):
            acc = acc + _dot_nt(p_ref[...].astype(BF16), w_ref[:, off:off + p_ref.shape[1]])
        o_ref[...] = acc

    row = lambda w: pl.BlockSpec((tm, w), lambda i: (i, 0))
    return pl.pallas_call(
        body, name="dx", grid=(steps,),
        in_specs=[row(a.shape[1]) for a in pieces] + [pl.BlockSpec(w_in_p.shape, lambda i: (0, 0)), row(D_MODEL)]
        + [HBM_SPEC] * n_in,
        out_specs=[row(D_MODEL)] + [HBM_SPEC] * n_out,
        out_shape=[jax.ShapeDtypeStruct((t, D_MODEL), F32)] + rider.out_shapes,
        scratch_shapes=[pltpu.SemaphoreType.DMA((s,)) for s in rider.sems],
        compiler_params=_cparams("arbitrary"),
    )(*pieces, w_in_p, dr1, *rider.inputs)


def _dw_in(x, pieces, t, tm=512, tk=512):
    offs = _col_offsets(pieces)
    npc = len(pieces)
    n = offs[-1] + pieces[-1].shape[1]
    nk = t // tk
    assert nk >= 2

    def body(*refs):
        x_ref, p_refs, o_ref, acc_ref = refs[0], refs[1:1 + npc], refs[1 + npc], refs[2 + npc]
        kk = pl.program_id(1)
        xb = x_ref[...].astype(BF16)
        for p_ref, off in zip(p_refs, offs):
            cols = slice(off, off + p_ref.shape[1])
            part = _dot_tn(xb, p_ref[...].astype(BF16))

            @pl.when(kk == 0)
            def _():
                acc_ref[:, cols] = part

            @pl.when((kk > 0) & (kk < nk - 1))
            def _():
                acc_ref[:, cols] += part

            @pl.when(kk == nk - 1)
            def _():
                o_ref[:, cols] = acc_ref[:, cols] + part

    return pl.pallas_call(
        body, name="dw_in", grid=(D_MODEL // tm, nk),
        in_specs=[pl.BlockSpec((tk, tm), lambda i, kk: (kk, i))]
        + [pl.BlockSpec((tk, a.shape[1]), lambda i, kk: (kk, 0)) for a in pieces],
        out_specs=pl.BlockSpec((tm, n), lambda i, kk: (i, 0)),
        out_shape=jax.ShapeDtypeStruct((D_MODEL, n), F32),
        scratch_shapes=[pltpu.VMEM((tm, n), F32)],
        compiler_params=_cparams("parallel", "arbitrary"),
    )(x, *pieces)


class _Rider:
    def __init__(self, inputs, out_shapes, sems, copies, aliases=()):
        self.inputs, self.out_shapes, self.sems, self.copies = list(inputs), list(out_shapes), tuple(sems), copies
        self.aliases = tuple(aliases)


def _join(a, b):
    na_in, na_out, na_sems = len(a.inputs), len(a.out_shapes), len(a.sems)

    def copies(ins, outs, sems):
        first = a.copies(ins[:na_in], outs[:na_out], sems[:na_sems])
        second = b.copies(ins[na_in:], outs[na_out:], sems[na_sems:])
        return tuple(u + v for u, v in zip(first, second))

    return _Rider(a.inputs + b.inputs, a.out_shapes + b.out_shapes, a.sems + b.sems, copies,
                  a.aliases + tuple((i + na_in, o + na_out) for i, o in b.aliases))


def _ride(rider, in_refs, out_refs, sems, first, last):
    @pl.when(first)
    def _():
        local, sends, _ = rider.copies(in_refs, out_refs, sems)
        for cp in local + sends:
            cp.start()

    @pl.when(last)
    def _():
        local, sends, recvs = rider.copies(in_refs, out_refs, sems)
        for cp in recvs:
            cp.wait_recv()
        for cp in sends:
            cp.wait_send()
        for cp in local:
            cp.wait()


def _run(rider, name):
    n_in, n_out = len(rider.inputs), len(rider.out_shapes)

    def body(*refs):
        local, sends, recvs = rider.copies(refs[:n_in], refs[n_in:n_in + n_out], refs[n_in + n_out:])
        for cp in local + sends:
            cp.start()
        for cp in recvs:
            cp.wait_recv()
        for cp in sends:
            cp.wait_send()
        for cp in local:
            cp.wait()

    return pl.pallas_call(
        body, name=name, in_specs=[HBM_SPEC] * n_in, out_specs=[HBM_SPEC] * n_out, out_shape=list(rider.out_shapes),
        scratch_shapes=[pltpu.SemaphoreType.DMA((s,)) for s in rider.sems],
    )(*rider.inputs)


def _sb_tile(qh, kj, diag, strict, u_strict, r_in):
    z = _dot_nt(qh, kj)
    sp = _softplus(z)
    l1m = -sp
    lsz = z - sp
    if diag:
        l1m = jnp.where(strict, l1m, 0.0)
    cs = _dot(l1m.astype(BF16), u_strict) + r_in
    w = jnp.exp(lsz + cs)
    if diag:
        w = jnp.where(strict, w, 0.0)
    return l1m, lsz, w


def _sb_consts():
    row = lax.broadcasted_iota(jnp.int32, (SB_BLK, SB_BLK), 0)
    col = lax.broadcasted_iota(jnp.int32, (SB_BLK, SB_BLK), 1)
    strict = col < row
    u_strict = (row > col).astype(BF16)
    u_pre = (row < col).astype(BF16)
    lane = lax.broadcasted_iota(jnp.int32, (1, LANES), 1)
    return strict, u_strict, u_pre, lane


def _sb_fwd(proj, t, rider):
    nq = t // SB_BLK
    n_in, n_out = len(rider.inputs), len(rider.out_shapes)

    def body(q_ref, k_ref, v_ref, *rest):
        rin, o_ref, sv_ref, rout = rest[:n_in], rest[n_in], rest[n_in + 1], rest[n_in + 2:n_in + 2 + n_out]
        p = pl.program_id(0)
        i = pl.program_id(1)
        _ride(rider, rin, rout, rest[n_in + 2 + n_out:], (p == 0) & (i == 0), (p == 3) & (i == nq - 1))

        strict, u_strict, _, lane = _sb_consts()
        qf = q_ref[...] * SB_SCALE
        hms = [(lane // 64) == hh for hh in range(2)]
        qhs = [jnp.where(hm, qf, 0.0).astype(BF16) for hm in hms]

        def step(j, r0, r1, a, sv, diag, keep=None):
            rows = pl.ds(pl.multiple_of(j * SB_BLK, SB_BLK), SB_BLK)
            kj = k_ref[rows, :].astype(BF16)
            vf = v_ref[rows, :]
            rs = []
            for hh, r in enumerate((r0, r1)):
                l1m, _, w = _sb_tile(qhs[hh], kj, diag, strict, u_strict, r)
                pv = _dot(w.astype(BF16), jnp.where(hms[hh], vf, 0.0).astype(BF16))
                mass = jnp.sum(l1m, axis=1, keepdims=True)
                if keep is not None:
                    pv, mass = jnp.where(keep, pv, 0.0), jnp.where(keep, mass, 0.0)
                a = a + pv
                rs.append(r + mass)
                sv = jnp.where(lane == hh * 64 + (i - j + 1), rs[hh], sv)
            return rs[0], rs[1], a, sv

        zero = jnp.zeros((SB_BLK, 1), F32)
        acc0 = jnp.zeros((SB_BLK, LANES), F32)
        r0, r1, acc, sv = step(i, zero, zero, acc0, acc0, True)
        r0, r1, acc, sv = step(jnp.maximum(i - 1, 0), r0, r1, acc, sv, False, keep=i > 0)
        _, _, _, acc, sv = lax.while_loop(
            lambda c: (c[0] >= 0) & (jnp.maximum(jnp.max(c[1]), jnp.max(c[2])) > SB_CUT),
            lambda c: (c[0] - 1,) + step(c[0], c[1], c[2], c[3], c[4], False),
            (i - 2, r0, r1, acc, sv))
        o_ref[...] = acc.astype(BF16)
        sv_ref[...] = sv

    return pl.pallas_call(
        body, name="sb_fwd", grid=(4, nq),
        in_specs=[pl.BlockSpec((SB_BLK, LANES), lambda p, i: (i, p)),
                  pl.BlockSpec((t, LANES), lambda p, i: (0, 4 + p)),
                  pl.BlockSpec((t, LANES), lambda p, i: (0, 8 + p))] + [HBM_SPEC] * n_in,
        out_specs=[pl.BlockSpec((SB_BLK, LANES), lambda p, i: (i, p))] * 2 + [HBM_SPEC] * n_out,
        out_shape=[jax.ShapeDtypeStruct((t, SB_WIDTH), BF16), jax.ShapeDtypeStruct((t, SB_WIDTH), F32)] + rider.out_shapes,
        scratch_shapes=[pltpu.SemaphoreType.DMA((s,)) for s in rider.sems],
        compiler_params=_cparams("arbitrary", "arbitrary"),
    )(proj, proj, proj, *rider.inputs)


def _sb_bwd(proj, dcat, mass, t, rider):
    nq = t // SB_BLK
    n_in, n_out = len(rider.inputs), len(rider.out_shapes)

    def body(q_ref, k_ref, v_ref, do_ref, sv_ref, *rest):
        rin = rest[:n_in]
        dq_ref, dk_ref, dv_ref = rest[n_in:n_in + 3]
        rout = rest[n_in + 3:n_in + 3 + n_out]
        p = pl.program_id(0)
        i = pl.program_id(1)
        _ride(rider, rin, rout, rest[n_in + 3 + n_out:], (p == 0) & (i == 0), (p == 3) & (i == nq - 1))

        @pl.when(i == 0)
        def _():
            dk_ref[...] = jnp.zeros_like(dk_ref)
            dv_ref[...] = jnp.zeros_like(dv_ref)

        strict, u_strict, u_pre, lane = _sb_consts()
        qf = q_ref[...] * SB_SCALE
        dof = do_ref[...]
        hms = [(lane // 64) == hh for hh in range(2)]
        qhs = [jnp.where(hm, qf, 0.0).astype(BF16) for hm in hms]
        dohs = [jnp.where(hm, dof, 0.0).astype(BF16) for hm in hms]

        sv = sv_ref[...]
        zero = jnp.zeros((SB_BLK, 1), F32)

        def mass_right(hh, d):
            return jnp.sum(jnp.where(lane == hh * 64 + d, sv, 0.0), axis=1, keepdims=True)

        dstop = lax.while_loop(
            lambda d: (i - d >= 0) & (jnp.maximum(jnp.max(mass_right(0, d)), jnp.max(mass_right(1, d))) > SB_CUT),
            lambda d: d + 1, 1)
        jstop = i - dstop

        def step(j, carry, diag, keep=None):
            pre_g0, pre_g1, dqa = carry
            rows = pl.ds(pl.multiple_of(j * SB_BLK, SB_BLK), SB_BLK)
            kf = k_ref[rows, :]
            kj = kf.astype(BF16)
            vj = v_ref[rows, :].astype(BF16)
            dv = jnp.zeros((SB_BLK, LANES), F32)
            dk = jnp.zeros((SB_BLK, LANES), F32)
            dqj = jnp.zeros((SB_BLK, LANES), F32)
            pre = []
            for hh, pre_g in enumerate((pre_g0, pre_g1)):
                _, lsz, w = _sb_tile(qhs[hh], kj, diag, strict, u_strict, zero if diag else mass_right(hh, i - j))
                g = w * _dot_nt(dohs[hh], vj)
                gpre = _dot(g.astype(BF16), u_pre) + pre_g
                sig = jnp.exp(lsz)
                dz = g * (1.0 - sig) - gpre * sig
                if diag:
                    dz = jnp.where(strict, dz, 0.0)
                dzb = dz.astype(BF16)
                dv = dv + _dot_tn(w.astype(BF16), dohs[hh])
                dk = dk + _dot_tn(dzb, qhs[hh])
                dqj = dqj + _dot(dzb, jnp.where(hms[hh], kf, 0.0).astype(BF16))
                gsum = jnp.sum(g, axis=1, keepdims=True)
                pre.append(pre_g + (gsum if keep is None else jnp.where(keep, gsum, 0.0)))
            if keep is not None:
                dv, dk, dqj = jnp.where(keep, dv, 0.0), jnp.where(keep, dk, 0.0), jnp.where(keep, dqj, 0.0)
            dv_ref[rows, :] += dv
            dk_ref[rows, :] += dk
            return pre[0], pre[1], dqa + dqj

        carry = lax.fori_loop(jstop + 1, i - 1, lambda j, c: step(j, c, False), (zero, zero, jnp.zeros((SB_BLK, LANES), F32)))
        carry = step(jnp.maximum(i - 1, 0), carry, False, keep=i > 0)
        _, _, dq = step(i, carry, True)
        dq_ref[...] = dq * SB_SCALE

    return pl.pallas_call(
        body, name="sb_bwd", grid=(4, nq),
        in_specs=[pl.BlockSpec((SB_BLK, LANES), lambda p, i: (i, p)),
                  pl.BlockSpec((t, LANES), lambda p, i: (0, 4 + p)),
                  pl.BlockSpec((t, LANES), lambda p, i: (0, 8 + p)),
                  pl.BlockSpec((SB_BLK, LANES), lambda p, i: (i, p)),
                  pl.BlockSpec((SB_BLK, LANES), lambda p, i: (i, p))] + [HBM_SPEC] * n_in,
        out_specs=[pl.BlockSpec((SB_BLK, LANES), lambda p, i: (i, p)),
                   pl.BlockSpec((t, LANES), lambda p, i: (0, p)),
                   pl.BlockSpec((t, LANES), lambda p, i: (0, p))] + [HBM_SPEC] * n_out,
        out_shape=[jax.ShapeDtypeStruct((t, SB_WIDTH), F32)] * 3 + rider.out_shapes,
        scratch_shapes=[pltpu.SemaphoreType.DMA((s,)) for s in rider.sems],
        compiler_params=_cparams("arbitrary", "arbitrary"),
    )(proj, proj, proj, dcat, mass, *rider.inputs)


def _gla_consts():
    r = lax.broadcasted_iota(jnp.int32, (CHUNK, CHUNK), 0)
    c = lax.broadcasted_iota(jnp.int32, (CHUNK, CHUNK), 1)
    causal = c <= r
    lc = causal.astype(BF16)
    lct = (c >= r).astype(BF16)
    rowid = lax.broadcasted_iota(jnp.int32, (CHUNK, 1), 0)
    lane = lax.broadcasted_iota(jnp.int32, (1, LANES), 1)
    sr = lax.broadcasted_iota(jnp.int32, (LANES, 2 * LANES), 0)
    sc = lax.broadcasted_iota(jnp.int32, (LANES, 2 * LANES), 1)
    blockdiag = (sr // 64) == (sc // LANES)
    return causal, lc, lct, rowid, lane, blockdiag


def _dot3(u, x):
    hi, mid, lo = _split3(x)
    return _dot(u, hi) + _dot(u, mid) + _dot(u, lo)


def _row_to_col(row):
    return jnp.transpose(jnp.broadcast_to(row, (LANES, LANES)))


def _gla_gates(ga_ref, gup_ref, gbias_ref):
    pre = _dot(ga_ref[...].astype(BF16), gup_ref[...].astype(BF16)) + gbias_ref[...]
    log_a = (jnp.minimum(pre, 0.0) - jnp.log(1.0 + jnp.exp(-jnp.abs(pre)))) / GLA_TAU
    return pre, log_a


def _gla_chunk_terms(g2, q2, k2, lc, rowid):
    b = _dot3(lc, g2)
    b_ref = jnp.sum(jnp.where(rowid == CHUNK // 2 - 1, b, 0.0), axis=0, keepdims=True)
    b_last = jnp.sum(jnp.where(rowid == CHUNK - 1, b, 0.0), axis=0, keepdims=True)
    qs = q2 * GLA_SCALE
    e_q = jnp.exp(b - b_ref)
    e_k = jnp.exp(b_ref - b)
    e_d = jnp.exp(b_last - b)
    e_b = jnp.exp(b)
    decay = jnp.exp(b_last)
    return dict(qs=qs, e_q=e_q, e_k=e_k, e_d=e_d, e_b=e_b, decay=decay,
                qi=qs * e_q, ki=k2 * e_k, kd=k2 * e_d, qb=qs * e_b)


def _gla_fwd(proj, gate_up_pad, gate_bias, gnorm, t, rider):
    nsteps = t // GLA_ROWS
    cps = GLA_ROWS // CHUNK
    n_in, n_out = len(rider.inputs), len(rider.out_shapes)

    def body(q_ref, k_ref, v_ref, gg_ref, ga_ref, gup_ref, gbias_ref, gn_ref, *rest):
        o_ref, oraw_ref, st_ref = rest[n_in:n_in + 3]
        s_scr = rest[n_in + 3 + n_out]
        i = pl.program_id(0)
        _ride(rider, rest[:n_in], rest[n_in + 3:n_in + 3 + n_out], rest[n_in + 4 + n_out:], i == 0, i == nsteps - 1)

        @pl.when(i == 0)
        def _():
            s_scr[...] = jnp.zeros_like(s_scr)

        causal, lc, _, rowid, lane, blockdiag = _gla_consts()
        _, log_a = _gla_gates(ga_ref, gup_ref, gbias_ref)
        gn = gn_ref[...]
        state = [s_scr[0], s_scr[1]]
        for cc in range(cps):
            rows = slice(cc * CHUNK, (cc + 1) * CHUNK)
            for p in range(2):
                kl = slice(p * LANES, (p + 1) * LANES)
                vl = slice(p * 2 * LANES, (p + 1) * 2 * LANES)
                tm = _gla_chunk_terms(log_a[rows, kl], q_ref[rows, kl], k_ref[rows, kl], lc, rowid)
                v2 = v_ref[rows, vl]
                v2b = v2.astype(BF16)
                s_prev = state[p]
                st_ref[cc, p] = s_prev
                kib = tm["ki"].astype(BF16)
                o_inter = _dot(tm["qb"].astype(BF16), s_prev.astype(BF16))
                outs = []
                for hh in range(2):
                    hm = (lane // 64) == hh
                    a = _dot_nt(jnp.where(hm, tm["qi"], 0.0).astype(BF16), kib)
                    a = jnp.where(causal, a, 0.0)
                    outs.append(_dot(a.astype(BF16), v2b[:, hh * LANES:(hh + 1) * LANES]))
                o2 = jnp.concatenate(outs, axis=1) + o_inter
                upd = jnp.where(blockdiag, _dot_tn(tm["kd"].astype(BF16), v2b), 0.0)
                dcol = _row_to_col(tm["decay"])
                state[p] = s_prev * jnp.concatenate([dcol, dcol], axis=1) + upd
                oraw_ref[rows, vl] = o2
                for hh in range(2):
                    hl = slice(hh * LANES, (hh + 1) * LANES)
                    oh = o2[:, hl]
                    gl = slice(p * 2 * LANES + hh * LANES, p * 2 * LANES + (hh + 1) * LANES)
                    rinv = lax.rsqrt(jnp.mean(oh * oh, axis=1, keepdims=True) + RMS_EPS)
                    gg = gg_ref[rows, gl]
                    o_ref[rows, gl] = (oh * rinv * gn * (gg * _sigmoid(gg))).astype(BF16)
        s_scr[0] = state[0]
        s_scr[1] = state[1]

    cb = lambda w, idx: pl.BlockSpec((GLA_ROWS, w), lambda i: (i, idx))
    full = lambda shp: pl.BlockSpec(shp, lambda i: tuple(0 for _ in shp))
    return pl.pallas_call(
        body, name="gla_fwd", grid=(nsteps,),
        in_specs=[cb(256, 6), cb(256, 7), cb(512, 4), cb(512, 5), cb(128, 24),
                  full((LANES, GLA_KEYS)), full((1, GLA_KEYS)), full((1, LANES))] + [HBM_SPEC] * n_in,
        out_specs=[pl.BlockSpec((GLA_ROWS, GLA_WIDTH), lambda i: (i, 0)),
                   pl.BlockSpec((GLA_ROWS, GLA_WIDTH), lambda i: (i, 0)),
                   pl.BlockSpec((cps, 2, LANES, 2 * LANES), lambda i: (i, 0, 0, 0))] + [HBM_SPEC] * n_out,
        out_shape=[jax.ShapeDtypeStruct((t, GLA_WIDTH), BF16), jax.ShapeDtypeStruct((t, GLA_WIDTH), F32),
                   jax.ShapeDtypeStruct((t // CHUNK, 2, LANES, 2 * LANES), F32)]
        + rider.out_shapes,
        input_output_aliases={8 + i: 3 + o for i, o in rider.aliases},
        scratch_shapes=[pltpu.VMEM((2, LANES, 2 * LANES), F32)] + [pltpu.SemaphoreType.DMA((s,)) for s in rider.sems],
        compiler_params=_cparams("arbitrary"),
    )(proj, proj, proj, proj, proj, gate_up_pad, gate_bias, gnorm, *rider.inputs)


def _gla_bwd(proj, dcat, oraw, states, gate_up_pad, gate_bias, gnorm, t):
    nsteps = t // GLA_ROWS
    cps = GLA_ROWS // CHUNK
    wout = 2 * GLA_KEYS + 2 * GLA_WIDTH + LANES

    def body(q_ref, k_ref, v_ref, gg_ref, ga_ref, do_ref, oraw_ref, st_ref, gup_ref, gbias_ref, gn_ref,
             d_ref, dgup_ref, dgbias_ref, dgn_ref, ds_scr):
        i = pl.program_id(0)

        @pl.when(i == 0)
        def _():
            ds_scr[...] = jnp.zeros_like(ds_scr)
            dgup_ref[...] = jnp.zeros_like(dgup_ref)
            dgbias_ref[...] = jnp.zeros_like(dgbias_ref)
            dgn_ref[...] = jnp.zeros_like(dgn_ref)

        causal, lc, lct, rowid, lane, blockdiag = _gla_consts()
        pre, log_a = _gla_gates(ga_ref, gup_ref, gbias_ref)
        gn = gn_ref[...]
        dgn = jnp.zeros((1, LANES), F32)
        dstate = [ds_scr[0], ds_scr[1]]
        dgs = [[None, None] for _ in range(cps)]
        for cc in reversed(range(cps)):
            rows = slice(cc * CHUNK, (cc + 1) * CHUNK)
            for p in range(2):
                kl = slice(p * LANES, (p + 1) * LANES)
                vl = slice(p * 2 * LANES, (p + 1) * 2 * LANES)
                tm = _gla_chunk_terms(log_a[rows, kl], q_ref[rows, kl], k_ref[rows, kl], lc, rowid)
                v2b = v_ref[rows, vl].astype(BF16)
                s_prev = st_ref[cc, p]
                ds2 = dstate[p]
                dos = []
                for hh in range(2):
                    gl = slice(p * 2 * LANES + hh * LANES, p * 2 * LANES + (hh + 1) * LANES)
                    oh = oraw_ref[rows, gl]
                    rinv = lax.rsqrt(jnp.mean(oh * oh, axis=1, keepdims=True) + RMS_EPS)
                    on = oh * rinv
                    gg = gg_ref[rows, gl]
                    sg = _sigmoid(gg)
                    sil = gg * sg
                    dgo = do_ref[rows, gl]
                    d_ref[rows, 2 * GLA_KEYS + GLA_WIDTH + gl.start:2 * GLA_KEYS + GLA_WIDTH + gl.stop] = (
                        dgo * on * gn * (sg * (1.0 + gg * (1.0 - sg))))
                    dgn = dgn + jnp.sum(dgo * sil * on, axis=0, keepdims=True)
                    don = dgo * sil * gn
                    dos.append(rinv * (don - on * jnp.mean(don * on, axis=1, keepdims=True)))
                do2b = jnp.concatenate(dos, axis=1).astype(BF16)
                qib = tm["qi"].astype(BF16)
                kib = tm["ki"].astype(BF16)
                kdb = tm["kd"].astype(BF16)
                qbb = tm["qb"].astype(BF16)
                ds2b = ds2.astype(BF16)
                dqi = jnp.zeros((CHUNK, LANES), F32)
                dki = jnp.zeros((CHUNK, LANES), F32)
                dvs = []
                for hh in range(2):
                    hm = (lane // 64) == hh
                    hl = slice(hh * LANES, (hh + 1) * LANES)
                    a = jnp.where(causal, _dot_nt(jnp.where(hm, tm["qi"], 0.0).astype(BF16), kib), 0.0).astype(BF16)
                    da = jnp.where(causal, _dot_nt(do2b[:, hl], v2b[:, hl]), 0.0).astype(BF16)
                    dvs.append(_dot_tn(a, do2b[:, hl]))
                    dqi = dqi + jnp.where(hm, _dot(da, kib), 0.0)
                    dki = dki + jnp.where(hm, _dot_tn(da, qib), 0.0)
                dv2 = jnp.concatenate(dvs, axis=1) + _dot(kdb, ds2b)
                dqb = _dot_nt(do2b, s_prev.astype(BF16))
                dkd = _dot_nt(v2b, ds2b)
                dcol = _row_to_col(tm["decay"])
                dsp = jnp.where(blockdiag, _dot_tn(qbb, do2b), 0.0) + ds2 * jnp.concatenate([dcol, dcol], axis=1)
                ddecay_col = jnp.sum(ds2 * s_prev, axis=1, keepdims=True)
                ddecay_row = jnp.transpose(jnp.broadcast_to(ddecay_col, (LANES, LANES)))[0:1, :]
                dstate[p] = dsp
                dqs = dqi * tm["e_q"] + dqb * tm["e_b"]
                dk = dki * tm["e_k"] + dkd * tm["e_d"]
                t_qi = dqi * tm["qi"]
                t_ki = dki * tm["ki"]
                t_kd = dkd * tm["kd"]
                db = t_qi - t_ki + dqb * tm["qb"] - t_kd
                db_ref = jnp.sum(t_ki - t_qi, axis=0, keepdims=True)
                db_last = jnp.sum(t_kd, axis=0, keepdims=True) + ddecay_row * tm["decay"]
                db = db + jnp.where(rowid == CHUNK // 2 - 1, db_ref, 0.0) + jnp.where(rowid == CHUNK - 1, db_last, 0.0)
                dg = _dot3(lct, db)
                d_ref[rows, p * LANES:(p + 1) * LANES] = dqs * GLA_SCALE
                d_ref[rows, GLA_KEYS + p * LANES:GLA_KEYS + (p + 1) * LANES] = dk
                d_ref[rows, 2 * GLA_KEYS + p * 2 * LANES:2 * GLA_KEYS + (p + 1) * 2 * LANES] = dv2
                dgs[cc][p] = dg
        ds_scr[0] = dstate[0]
        ds_scr[1] = dstate[1]
        dlog_a = jnp.concatenate([jnp.concatenate(row, axis=1) for row in dgs], axis=0)
        dpre = dlog_a * (1.0 / GLA_TAU) * _sigmoid(-pre)
        dpb = dpre.astype(BF16)
        dgn_ref[...] += dgn
        dgbias_ref[...] += jnp.sum(dpre, axis=0, keepdims=True)
        dgup_ref[...] += _dot_tn(ga_ref[...].astype(BF16), dpb)
        d_ref[:, 2 * GLA_KEYS + 2 * GLA_WIDTH:] = _dot_nt(dpb, gup_ref[...].astype(BF16))

    rev = lambda i: nsteps - 1 - i
    cb = lambda w, idx: pl.BlockSpec((GLA_ROWS, w), lambda i: (rev(i), idx))
    full = lambda shp: pl.BlockSpec(shp, lambda i: tuple(0 for _ in shp))
    return pl.pallas_call(
        body, name="gla_bwd", grid=(nsteps,),
        in_specs=[cb(256, 6), cb(256, 7), cb(512, 4), cb(512, 5), cb(128, 24), cb(512, 1), cb(512, 0),
                  pl.BlockSpec((cps, 2, LANES, 2 * LANES), lambda i: (rev(i), 0, 0, 0)),
                  full((LANES, GLA_KEYS)), full((1, GLA_KEYS)), full((1, LANES))],
        out_specs=[pl.BlockSpec((GLA_ROWS, wout), lambda i: (rev(i), 0)),
                   full((LANES, GLA_KEYS)), full((1, GLA_KEYS)), full((1, LANES))],
        out_shape=[jax.ShapeDtypeStruct((t, wout), F32), jax.ShapeDtypeStruct((LANES, GLA_KEYS), F32),
                   jax.ShapeDtypeStruct((1, GLA_KEYS), F32), jax.ShapeDtypeStruct((1, LANES), F32)],
        scratch_shapes=[pltpu.VMEM((2, LANES, 2 * LANES), F32)],
        compiler_params=_cparams("arbitrary"),
    )(proj, proj, proj, proj, proj, dcat, oraw, states, gate_up_pad, gate_bias, gnorm)


def _ln_stats(r):
    mu = jnp.mean(r, axis=1, keepdims=True)
    xc = r - mu
    rstd = lax.rsqrt(jnp.mean(xc * xc, axis=1, keepdims=True) + LN_EPS)
    return xc * rstd, rstd


def _ln_bwd(dy_g, xhat, rstd):
    return rstd * (dy_g - jnp.mean(dy_g, axis=1, keepdims=True) - xhat * jnp.mean(dy_g * xhat, axis=1, keepdims=True))


def _outproj_ln1(sb_o, gla_o, x, w_out, g1, b1, t, tm=256):
    def body(sb_ref, gl_ref, x_ref, w_ref, g_ref, b_ref, xhat_ref, rstd_ref, h_ref):
        mix = _dot(sb_ref[...], w_ref[0:SB_WIDTH, :]) + _dot(gl_ref[...], w_ref[SB_WIDTH:, :])
        xhat, rstd = _ln_stats(ALPHA * x_ref[...] + mix)
        xhat_ref[...] = xhat
        rstd_ref[...] = rstd
        h_ref[...] = (xhat * g_ref[...] + b_ref[...]).astype(BF16)

    row = lambda w: pl.BlockSpec((tm, w), lambda i: (i, 0))
    full = lambda shp: pl.BlockSpec(shp, lambda i: (0, 0))
    return pl.pallas_call(
        body, name="outproj_ln1", grid=(t // tm,),
        in_specs=[row(SB_WIDTH), row(GLA_WIDTH), row(D_MODEL), full((D_MODEL, D_MODEL)), full((1, D_MODEL)), full((1, D_MODEL))],
        out_specs=[row(D_MODEL), row(1), row(D_MODEL)],
        out_shape=[jax.ShapeDtypeStruct((t, D_MODEL), F32), jax.ShapeDtypeStruct((t, 1), F32),
                   jax.ShapeDtypeStruct((t, D_MODEL), BF16)],
        compiler_params=_cparams("parallel"),
    )(sb_o, gla_o, x, w_out, g1, b1)


_INV_SQRT2 = 1.0 / math.sqrt(2.0)
_INV_SQRT2PI = 1.0 / math.sqrt(2.0 * math.pi)


def _conv3(xs, w_ref, b_ref, half):
    return (w_ref[half, 0:1, :] * pltpu.roll(xs, 2, 0) + w_ref[half, 1:2, :] * pltpu.roll(xs, 1, 0)
            + w_ref[half, 2:3, :] * xs + b_ref[half])


HALO = 16


def _conv_gelu_fwd(up3, conv_w3, conv_b3, t, tr=512, ct=256):
    nct = D_FF // ct
    hb = tr // HALO

    def body(cur_ref, prev_ref, w_ref, b_ref, gm_ref):
        i = pl.program_id(1)
        keep = (i > 0).astype(F32)
        us = []
        for half in range(2):
            xs = jnp.concatenate([prev_ref[half].astype(F32) * keep, cur_ref[half].astype(F32)], axis=0)
            us.append(_conv3(xs, w_ref, b_ref, half)[HALO:, :])
        a, c = us
        gelu = 0.5 * a * (1.0 + lax.erf(a * _INV_SQRT2))
        gm_ref[...] = (gelu * c).astype(BF16)

    return pl.pallas_call(
        body, name="conv_gelu_fwd", grid=(nct, t // tr),
        in_specs=[pl.BlockSpec((2, tr, ct), lambda j, i: (0, i, j)),
                  pl.BlockSpec((2, HALO, ct), lambda j, i: (0, jnp.maximum(i * hb - 1, 0), j)),
                  pl.BlockSpec((2, 3, ct), lambda j, i: (0, 0, j)),
                  pl.BlockSpec((2, 1, ct), lambda j, i: (0, 0, j))],
        out_specs=pl.BlockSpec((tr, ct), lambda j, i: (i, j)),
        out_shape=jax.ShapeDtypeStruct((t, D_FF), BF16),
        compiler_params=_cparams("parallel", "parallel"),
    )(up3, up3, conv_w3, conv_b3)


def _conv_gelu_bwd(up3, dgm, conv_w3, conv_b3, t, tr=512, ct=256):
    nct = D_FF // ct
    nrt = t // tr
    hb = tr // HALO
    n = tr + 2 * HALO
    lo, hi = HALO, tr + HALO

    def body(cur_ref, prev_ref, next_ref, dg_ref, dgn_ref, w_ref, b_ref, dup_ref, dcw_ref, dcb_ref):
        i = pl.program_id(1)

        @pl.when(i == 0)
        def _():
            dcw_ref[...] = jnp.zeros_like(dcw_ref)
            dcb_ref[...] = jnp.zeros_like(dcb_ref)

        keep_prev = (i > 0).astype(F32)
        keep_next = (i < nrt - 1).astype(F32)
        xs, xm1, xm2, us = [], [], [], []
        for half in range(2):
            x = jnp.concatenate([prev_ref[half].astype(F32) * keep_prev, cur_ref[half].astype(F32),
                                 next_ref[half].astype(F32)], axis=0)
            xs.append(x)
            xm1.append(pltpu.roll(x, 1, 0))
            xm2.append(pltpu.roll(x, 2, 0))
            us.append(w_ref[half, 0:1, :] * xm2[half] + w_ref[half, 1:2, :] * xm1[half]
                      + w_ref[half, 2:3, :] * x + b_ref[half])
        a, c = us
        dg = jnp.concatenate([jnp.zeros((HALO, ct), F32), dg_ref[...].astype(F32),
                              dgn_ref[...].astype(F32) * keep_next], axis=0)
        cdf = 0.5 * (1.0 + lax.erf(a * _INV_SQRT2))
        pdf = jnp.exp(-0.5 * a * a) * _INV_SQRT2PI
        dus = [dg * c * (cdf + a * pdf), dg * (a * cdf)]
        rid = lax.broadcasted_iota(jnp.int32, (8, 1), 0)
        for half in range(2):
            du = dus[half]
            dup = (w_ref[half, 2:3, :] * du + w_ref[half, 1:2, :] * pltpu.roll(du, n - 1, 0)
                   + w_ref[half, 0:1, :] * pltpu.roll(du, n - 2, 0))
            dup_ref[half] = dup[lo:hi, :].astype(BF16)
            duc = du[lo:hi, :]
            s0 = jnp.sum(duc * xm2[half][lo:hi, :], axis=0, keepdims=True)
            s1 = jnp.sum(duc * xm1[half][lo:hi, :], axis=0, keepdims=True)
            s2 = jnp.sum(duc * xs[half][lo:hi, :], axis=0, keepdims=True)
            dcw_ref[half] += jnp.where(rid == 0, s0, jnp.where(rid == 1, s1, jnp.where(rid == 2, s2, 0.0)))
            dcb_ref[half] += jnp.sum(duc, axis=0, keepdims=True)

    last = t // HALO - 1
    return pl.pallas_call(
        body, name="conv_gelu_bwd", grid=(nct, nrt),
        in_specs=[pl.BlockSpec((2, tr, ct), lambda j, i: (0, i, j)),
                  pl.BlockSpec((2, HALO, ct), lambda j, i: (0, jnp.maximum(i * hb - 1, 0), j)),
                  pl.BlockSpec((2, HALO, ct), lambda j, i: (0, jnp.minimum((i + 1) * hb, last), j)),
                  pl.BlockSpec((tr, ct), lambda j, i: (i, j)),
                  pl.BlockSpec((HALO, ct), lambda j, i: (jnp.minimum((i + 1) * hb, last), j)),
                  pl.BlockSpec((2, 3, ct), lambda j, i: (0, 0, j)),
                  pl.BlockSpec((2, 1, ct), lambda j, i: (0, 0, j))],
        out_specs=[pl.BlockSpec((2, tr, ct), lambda j, i: (0, i, j)),
                   pl.BlockSpec((2, 8, ct), lambda j, i: (0, 0, j)),
                   pl.BlockSpec((2, 1, ct), lambda j, i: (0, 0, j))],
        out_shape=[jax.ShapeDtypeStruct((2, t, D_FF), BF16), jax.ShapeDtypeStruct((2, 8, D_FF), F32),
                   jax.ShapeDtypeStruct((2, 1, D_FF), F32)],
        compiler_params=_cparams("parallel", "arbitrary"),
    )(up3, up3, up3, dgm, dgm, conv_w3, conv_b3)


def _down_ln2_loss(gm, w_down, xhat1, g1, b1, g2, b2, target, t, tm=256):
    def body(gm_ref, w_ref, xh_ref, g1_ref, b1_ref, g2_ref, b2_ref, tg_ref, dr_ref, loss_ref, dg_ref, db_ref):
        i = pl.program_id(0)

        @pl.when(i == 0)
        def _():
            loss_ref[...] = jnp.zeros_like(loss_ref)
            dg_ref[...] = jnp.zeros_like(dg_ref)
            db_ref[...] = jnp.zeros_like(db_ref)

        h = xh_ref[...] * g1_ref[...] + b1_ref[...]
        xhat, rstd = _ln_stats(ALPHA * h + _dot(gm_ref[...], w_ref[...]))
        err = xhat * g2_ref[...] + b2_ref[...] - tg_ref[...]
        loss_ref[...] += 0.5 * jnp.sum(jnp.sum(err * err, axis=1, keepdims=True), axis=0, keepdims=True) / D_MODEL
        dy = err * (1.0 / D_MODEL)
        dg_ref[...] += jnp.sum(dy * xhat, axis=0, keepdims=True)
        db_ref[...] += jnp.sum(dy, axis=0, keepdims=True)
        dr_ref[...] = _ln_bwd(dy * g2_ref[...], xhat, rstd)

    row = lambda w: pl.BlockSpec((tm, w), lambda i: (i, 0))
    full = lambda shp: pl.BlockSpec(shp, lambda i: (0, 0))
    vec = full((1, D_MODEL))
    return pl.pallas_call(
        body, name="down_ln2_loss", grid=(t // tm,),
        in_specs=[row(D_FF), full((D_FF, D_MODEL)), row(D_MODEL), vec, vec, vec, vec, row(D_MODEL)],
        out_specs=[row(D_MODEL), full((1, 1)), vec, vec],
        out_shape=[jax.ShapeDtypeStruct((t, D_MODEL), F32), jax.ShapeDtypeStruct((1, 1), F32),
                   jax.ShapeDtypeStruct((1, D_MODEL), F32), jax.ShapeDtypeStruct((1, D_MODEL), F32)],
        compiler_params=_cparams("arbitrary"),
    )(gm, w_down, xhat1, g1, b1, g2, b2, target)


def _dh_ln1_bwd(dup3, w_up4, dr2, xhat1, rstd1, g1, t, tm=256):
    ws = 2 * D_FF // 4

    def body(a_ref, w_ref, dr2_ref, xh_ref, rs_ref, g_ref, dr1_ref, dg_ref, db_ref):
        i = pl.program_id(0)

        @pl.when(i == 0)
        def _():
            dg_ref[...] = jnp.zeros_like(dg_ref)
            db_ref[...] = jnp.zeros_like(db_ref)

        dh = ALPHA * dr2_ref[...]
        for s in range(4):
            dh = dh + _dot_nt(a_ref[s // 2, :, (s % 2) * ws:(s % 2 + 1) * ws], w_ref[s])
        xhat = xh_ref[...]
        dg_ref[...] += jnp.sum(dh * xhat, axis=0, keepdims=True)
        db_ref[...] += jnp.sum(dh, axis=0, keepdims=True)
        dr1_ref[...] = _ln_bwd(dh * g_ref[...], xhat, rs_ref[...])

    row = lambda w: pl.BlockSpec((tm, w), lambda i: (i, 0))
    vec = pl.BlockSpec((1, D_MODEL), lambda i: (0, 0))
    return pl.pallas_call(
        body, name="dh_ln1_bwd", grid=(t // tm,),
        in_specs=[pl.BlockSpec((2, tm, D_FF), lambda i: (0, i, 0)),
                  pl.BlockSpec((4, D_MODEL, ws), lambda i: (0, 0, 0)),
                  row(D_MODEL), row(D_MODEL), row(1), vec],
        out_specs=[row(D_MODEL), vec, vec],
        out_shape=[jax.ShapeDtypeStruct((t, D_MODEL), F32), jax.ShapeDtypeStruct((1, D_MODEL), F32),
                   jax.ShapeDtypeStruct((1, D_MODEL), F32)],
        compiler_params=_cparams("arbitrary"),
    )(dup3, w_up4, dr2, xhat1, rstd1, g1)


def _adamw(w, g, m, v, name):
    rows, cols = w.shape
    tr = rows
    for cand in (256, 128, 64, 32, 16, 8):
        if rows % cand == 0 and rows > cand:
            tr = cand
            break
    c1 = 1.0 / (1.0 - ADAM_B1 ** ADAM_STEP)
    c2 = 1.0 / (1.0 - ADAM_B2 ** ADAM_STEP)

    def body(w_ref, g_ref, m_ref, v_ref, d_ref, nm_ref, nv_ref):
        gv = g_ref[...]
        nm = ADAM_B1 * m_ref[...] + (1.0 - ADAM_B1) * gv
        nv = ADAM_B2 * v_ref[...] + (1.0 - ADAM_B2) * (gv * gv)
        d_ref[...] = -ADAM_LR * ((nm * c1) / (jnp.sqrt(nv * c2) + ADAM_EPS) + ADAM_WD * w_ref[...])
        nm_ref[...] = nm
        nv_ref[...] = nv

    spec = pl.BlockSpec((tr, cols), lambda i: (i, 0))
    out = jax.ShapeDtypeStruct((rows, cols), F32)
    return pl.pallas_call(
        body, name=name, grid=(rows // tr,), in_specs=[spec] * 4, out_specs=[spec] * 3, out_shape=[out] * 3,
        compiler_params=_cparams("parallel"),
    )(w, g, m, v)


def _local_step(x, target, w_in_p, late_shards, gate_up_pad, gate_bias, gnorm, ln1_g, ln1_b, conv_w3, conv_b3,
                ln2_g, ln2_b, c_arr, kc_arr):
    t = x.shape[0]
    tq = min(t, 1024)
    s_up, s_out, s_down = late_shards
    sh_up, sh_out, sh_down = LATE_SHAPES
    proj, out_partly = _mm(x, w_in_p, m=t, n=IN_PAD, k=D_MODEL, tm=tq, tn=640, tk=D_MODEL, name="proj",
                           rider=_gather_rider([s_out], [sh_out]))
    sb_o, sb_mass, up_partly = _sb_fwd(proj, t, _gather_rider([s_up], [sh_up]))
    gla_o, oraw, states, w_up4, w_out4, down_partly = _gla_fwd(
        proj, gate_up_pad, gate_bias, gnorm, t,
        _join(_forward_rider([up_partly, out_partly], [sh_up, sh_out]), _gather_rider([s_down], [sh_down])))
    w_out = w_out4.reshape(D_MODEL, D_MODEL)
    xhat1, rstd1, h_bf = _outproj_ln1(sb_o, gla_o, x, w_out, ln1_g, ln1_b, t)
    up3, w_down4 = _mm(h_bf, w_up4, m=t, n=2 * D_FF, k=D_MODEL, tm=tq, tn=W_UP_S, tk=D_MODEL, name="up",
                       b_spec=pl.BlockSpec((None, D_MODEL, W_UP_S), lambda i, j, kk: (j, 0, 0)),
                       o_spec=pl.BlockSpec((None, tq, W_UP_S), lambda i, j, kk: (j // 2, i, j % 2)),
                       out_shape=jax.ShapeDtypeStruct((2, t, D_FF), BF16), out_dtype=BF16,
                       rider=_forward_rider([down_partly], [sh_down]))
    w_down = w_down4.reshape(D_FF, D_MODEL)
    gm = _conv_gelu_fwd(up3, conv_w3, conv_b3, t, tr=tq)
    dr2, loss, dln2_g, dln2_b = _down_ln2_loss(gm, w_down, xhat1, ln1_g, ln1_b, ln2_g, ln2_b, target, t)
    dgm = _mm(dr2, w_down, m=t, n=D_FF, k=D_MODEL, tm=tq, tn=W_UP_S, tk=D_MODEL, tb=True, out_dtype=BF16, name="dgm")
    dw_down = _mm(gm, dr2, m=D_FF, n=D_MODEL, k=t, tm=W_UP_S, tn=D_MODEL, tk=tq, ta=True, name="dw_down")
    dup3, dcw, dcb = _conv_gelu_bwd(up3, dgm, conv_w3, conv_b3, t)
    dr1, dln1_g, dln1_b = _dh_ln1_bwd(dup3, w_up4, dr2, xhat1, rstd1, ln1_g, t)
    dw_up4 = _mm(h_bf, dup3, m=D_MODEL, n=2 * D_FF, k=t, tm=512, tn=W_UP_S, tk=t, ta=True, name="dw_up",
                 b_spec=pl.BlockSpec((None, t, W_UP_S), lambda i, j, kk: (j // 2, kk, j % 2)),
                 o_spec=pl.BlockSpec((None, 512, W_UP_S), lambda i, j, kk: (j, i, 0)),
                 out_shape=jax.ShapeDtypeStruct((4, D_MODEL, W_UP_S), F32))
    dw_out_sb = _mm(sb_o, dr1, m=SB_WIDTH, n=D_MODEL, k=t, tm=512, tn=D_MODEL, tk=tq, ta=True, name="dw_out_sb")
    dw_out_gla = _mm(gla_o, dr1, m=GLA_WIDTH, n=D_MODEL, k=t, tm=512, tn=D_MODEL, tk=tq, ta=True, name="dw_out_gla")
    gs = [dw_up4, jnp.concatenate([dw_out_sb, dw_out_gla], axis=0).reshape(4, W_OUT_S, D_MODEL),
          dw_down.reshape(4, W_DOWN_S, D_MODEL)]
    dcat, *from_sib = _mm(dr1, w_out, m=t, n=D_MODEL, k=D_MODEL, tm=tq, tn=512, tk=D_MODEL, tb=True, name="dcat",
                          rider=_sibling_rider(gs, LATE_SHAPES))
    ps = [_add_sibling(gs[m], from_sib[m], c_arr, LATE_ADD_ROWS[m], "add_sibling_late_%d" % m) for m in range(3)]
    dsq, dsk, dsv, *others = _sb_bwd(proj, dcat, sb_mass, t, _chips_rider(ps))
    late_sums = [_add_chips(ps[m], others[m], kc_arr, LATE_ADD_ROWS[m], "add_chips_late_%d" % m) for m in range(3)]
    dgla, dgup_pad, dgbias, dgnorm = _gla_bwd(proj, dcat, oraw, states, gate_up_pad, gate_bias, gnorm, t)
    dproj = [dsq, dsk, dsv, dgla]
    dw_in_p = _dw_in(x, dproj, t)
    small = dict(
        gate_up=dgup_pad[:GATE_RANK], gate_bias=dgbias, gla_norm_g=dgnorm, ln1_g=dln1_g, ln1_b=dln1_b,
        conv_w=jnp.concatenate([dcw[0, :3], dcw[1, :3]], axis=1), conv_b=jnp.concatenate([dcb[0], dcb[1]], axis=1),
        ln2_g=dln2_g, ln2_b=dln2_b, loss=loss)
    g_in = dw_in_p[None]
    from_sib_in, = _run(_sibling_rider([g_in], [(D_MODEL, IN_PAD)]), "exchange_sibling_w_in")
    half_in = _add_sibling(g_in, from_sib_in, c_arr, ADD_ROWS[0], "add_sibling_w_in")[0]
    p_in = jnp.stack([half_in[:, k * W_IN_S:(k + 1) * W_IN_S] for k in range(4)], axis=0)
    dx, others_in, vecs = _dx(dproj, w_in_p, dr1, t, _chips_rider([p_in], _pack_vec(small, SMALL_GRADS, GRAD_VEC_ROWS)))
    sum_in = _add_chips(p_in, others_in, kc_arr, ADD_ROWS[0], "add_chips_w_in")
    return dx, [sum_in] + late_sums, vecs


W_IN_S, W_UP_S, W_OUT_S, W_DOWN_S = IN_WIDTH // 4, 2 * D_FF // 4, D_MODEL // 4, D_FF // 4
SHARD_SHAPES = ((D_MODEL, W_IN_S), (D_MODEL, W_UP_S), (W_OUT_S, D_MODEL), (W_DOWN_S, D_MODEL))
ADD_ROWS = (256, 256, 128, 176)
LATE_SHAPES, LATE_ADD_ROWS = SHARD_SHAPES[1:], ADD_ROWS[1:]
SMALL_ROWS = 8
VEC_SIZES = (("gate_bias", GLA_KEYS), ("gla_norm_g", LANES), ("ln1_g", D_MODEL), ("ln1_b", D_MODEL),
             ("conv_b", 2 * D_FF), ("ln2_g", D_MODEL), ("ln2_b", D_MODEL))
SMALL_GRADS = VEC_SIZES + (("conv_w", 3 * 2 * D_FF), ("gate_up", GATE_RANK * GLA_KEYS), ("loss", 1))


def _rows(a):
    flat = a.reshape(-1)
    pad = (-flat.shape[0]) % D_MODEL
    if pad:
        flat = jnp.pad(flat, (0, pad))
    return flat.reshape(-1, D_MODEL)


def _pad_rows(a, rows):
    return jnp.pad(a, ((0, rows - a.shape[0]), (0, 0)))


def _pack_vec(d, sizes, rows):
    flat = jnp.concatenate([d[n].reshape(-1) for n, _ in sizes])
    return _pad_rows(_rows(flat), rows)


def _unpack_vec(v, sizes):
    flat = v.reshape(-1)
    out, o = {}, 0
    for n, size in sizes:
        out[n] = flat[o:o + size].reshape(1, size)
        o += size
    return out


VEC_ROWS = 16
GRAD_VEC_ROWS = 32


HBM_SPEC = pl.BlockSpec(memory_space=pltpu.HBM)


def _position():
    x, y, c = lax.axis_index("x"), lax.axis_index("y"), lax.axis_index("c")
    chips = [(1 - x, y), (x, 1 - y), (1 - x, 1 - y)]
    return x, y, c, chips


def _remote(src, dst, send_sems, recv_sems, k, to):
    return pltpu.make_async_remote_copy(src_ref=src, dst_ref=dst, send_sem=send_sems.at[k], recv_sem=recv_sems.at[k],
                                        device_id=to, device_id_type=MESH)


def _gather_ici(in_refs, out_refs, shapes, send_sems, recv_sems, local_sems):
    x, y, c, chips = _position()
    k_me = 2 * x + y
    local, sends, recvs = [], [], []
    for m, (rows, _) in enumerate(shapes):
        h = rows // 2
        local.append(pltpu.make_async_copy(in_refs[m], out_refs[m].at[k_me], local_sems.at[m]))
        for j, (cx, cy) in enumerate(chips):
            sends.append(_remote(in_refs[m].at[pl.ds(c * h, h), :], out_refs[m].at[k_me, pl.ds(c * h, h), :],
                                 send_sems, recv_sems, 3 * m + j, (cx, cy, c)))
            landed = out_refs[m].at[2 * cx + cy, pl.ds(c * h, h), :]
            recvs.append(_remote(landed, landed, send_sems, recv_sems, 3 * m + j, (x, y, c)))
    return local, sends, recvs


def _gather_d2d(src_refs, dst_refs, shapes, send_sems, recv_sems, base):
    x, y, c, chips = _position()
    sends, recvs = [], []
    for m, (rows, _) in enumerate(shapes):
        h = rows // 2
        for j, (cx, cy) in enumerate(chips):
            k = 2 * cx + cy
            sends.append(_remote(src_refs[m].at[k, pl.ds(c * h, h), :], dst_refs[m].at[k, pl.ds(c * h, h), :],
                                 send_sems, recv_sems, base + 3 * m + j, (x, y, 1 - c)))
            landed = dst_refs[m].at[k, pl.ds((1 - c) * h, h), :]
            recvs.append(_remote(landed, landed, send_sems, recv_sems, base + 3 * m + j, (x, y, c)))
    return sends, recvs


def _gather_weights(shards, small, shapes):
    nm = len(shards)
    n_ici = 3 * nm

    def body(*refs):
        in_refs, small_ref = refs[:nm], refs[nm]
        out_refs, osm_ref = refs[nm + 1:2 * nm + 1], refs[2 * nm + 1]
        send_sems, recv_sems, local_sems = refs[2 * nm + 2:]
        x, y, c, chips = _position()
        k_me = 2 * x + y
        local, sends, recvs = _gather_ici(in_refs, out_refs, shapes, send_sems, recv_sems, local_sems)
        local.append(pltpu.make_async_copy(small_ref, osm_ref.at[k_me], local_sems.at[nm]))
        for j, (cx, cy) in enumerate(chips):
            sends.append(_remote(small_ref, osm_ref.at[k_me], send_sems, recv_sems, n_ici + j, (cx, cy, c)))
        for cp in local + sends:
            cp.start()
        for cp in recvs:
            cp.wait_recv()
        fsends, frecvs = _gather_d2d(out_refs, out_refs, shapes, send_sems, recv_sems, n_ici + 3)
        for cp in fsends:
            cp.start()
        for j, (cx, cy) in enumerate(chips):
            k = 2 * cx + cy
            frecvs.append(_remote(osm_ref.at[k], osm_ref.at[k], send_sems, recv_sems, n_ici + j, (x, y, c)))
        for cp in frecvs:
            cp.wait_recv()
        for cp in sends + fsends:
            cp.wait_send()
        for cp in local:
            cp.wait()

    n_sems = 2 * n_ici + 3
    return pl.pallas_call(
        body, name="gather_weights", in_specs=[HBM_SPEC] * (nm + 1), out_specs=[HBM_SPEC] * (nm + 1),
        out_shape=[jax.ShapeDtypeStruct((4,) + s, BF16) for s in shapes]
        + [jax.ShapeDtypeStruct((4, SMALL_ROWS, D_MODEL), F32)],
        scratch_shapes=[pltpu.SemaphoreType.DMA((n_sems,)), pltpu.SemaphoreType.DMA((n_sems,)),
                        pltpu.SemaphoreType.DMA((nm + 1,))],
    )(*shards, small)


def _gather_rider(shards, shapes):
    n = len(shards)
    return _Rider(shards, [jax.ShapeDtypeStruct((4,) + s, BF16) for s in shapes], (3 * n, 3 * n, n),
                  lambda ins, outs, sems: _gather_ici(ins, outs, shapes, *sems))


def _forward_rider(gathered, shapes):
    n = len(gathered)
    return _Rider(gathered, [jax.ShapeDtypeStruct(a.shape, a.dtype) for a in gathered], (3 * n, 3 * n),
                  lambda ins, outs, sems: ([],) + _gather_d2d(ins, outs, shapes, sems[0], sems[1], 0),
                  aliases=[(m, m) for m in range(n)])


def _sibling_rider(gs, shapes):
    def copies(ins, outs, sems):
        x, y, c, _ = _position()
        both = []
        for m, (rows, _) in enumerate(shapes):
            h = rows // 2
            for k in range(gs[m].shape[0]):
                both.append(_remote(ins[m].at[k, pl.ds((1 - c) * h, h), :], outs[m].at[k], sems[0], sems[1],
                                    4 * m + k, (x, y, 1 - c)))
        return [], both, both

    return _Rider(gs, [jax.ShapeDtypeStruct((g.shape[0], r // 2, cl), F32) for g, (r, cl) in zip(gs, shapes)],
                  (4 * len(gs), 4 * len(gs)), copies)


def _add_sibling(g, r, c_arr, tr, name):
    nblk, rows, cols = g.shape
    nb = rows // 2 // tr

    def body(c_ref, g_ref, r_ref, o_ref):
        o_ref[...] = (g_ref[...] + r_ref[...]).astype(BF16)

    spec = pl.BlockSpec((None, tr, cols), lambda k, i, c: (k, i, 0))
    return pl.pallas_call(
        body, name=name,
        grid_spec=pltpu.PrefetchScalarGridSpec(
            num_scalar_prefetch=1, grid=(nblk, nb),
            in_specs=[pl.BlockSpec((None, tr, cols), lambda k, i, c: (k, c[0] * nb + i, 0)), spec], out_specs=spec),
        out_shape=jax.ShapeDtypeStruct((nblk, rows // 2, cols), BF16), compiler_params=_cparams("parallel", "parallel"),
    )(c_arr, g, r)


def _reduce_ici(p_refs, r_refs, send_sems, recv_sems):
    x, y, c, chips = _position()
    sends, recvs = [], []
    for m in range(len(p_refs)):
        for j, (cx, cy) in enumerate(chips):
            sends.append(_remote(p_refs[m].at[2 * cx + cy], r_refs[m].at[j], send_sems, recv_sems, 3 * m + j, (cx, cy, c)))
            recvs.append(_remote(r_refs[m].at[j], r_refs[m].at[j], send_sems, recv_sems, 3 * m + j, (x, y, c)))
    return sends, recvs


def _chips_rider(ps, vec=None):
    nm = len(ps)
    n_ici = 3 * nm

    def copies(ins, outs, sems):
        sends, recvs = _reduce_ici(ins[:nm], outs[:nm], sems[0], sems[1])
        if vec is None:
            return [], sends, recvs
        x, y, c, _ = _position()
        my_id = 4 * x + 2 * y + c
        vec_ref, vrecv_ref = ins[nm], outs[nm]
        local = [pltpu.make_async_copy(vec_ref, vrecv_ref.at[my_id], sems[2].at[0])]
        for r in range(1, 8):
            peer = (1 - x if r & 4 else x, 1 - y if r & 2 else y, 1 - c if r & 1 else c)
            sends.append(_remote(vec_ref, vrecv_ref.at[my_id], sems[0], sems[1], n_ici + r - 1, peer))
            recvs.append(_remote(vec_ref, vrecv_ref.at[0], sems[0], sems[1], n_ici + r - 1, (x, y, c)))
        return local, sends, recvs

    out_shapes = [jax.ShapeDtypeStruct((3,) + p.shape[1:], p.dtype) for p in ps]
    if vec is None:
        return _Rider(ps, out_shapes, (n_ici, n_ici), copies)
    return _Rider(list(ps) + [vec], out_shapes + [jax.ShapeDtypeStruct((8, GRAD_VEC_ROWS, D_MODEL), F32)],
                  (n_ici + 7, n_ici + 7, 1), copies)


def _add_chips(p, r, kc_arr, tr, name):
    _, h, cols = p.shape
    nb = h // tr

    def body(kc_ref, p_ref, r0_ref, r1_ref, r2_ref, o_ref):
        o_ref[...] = ((p_ref[...].astype(F32) + r0_ref[...].astype(F32)) + r1_ref[...].astype(F32)) + r2_ref[...].astype(F32)

    rspec = lambda j: pl.BlockSpec((None, tr, cols), lambda i, kc: (j, i, 0))
    return pl.pallas_call(
        body, name=name,
        grid_spec=pltpu.PrefetchScalarGridSpec(
            num_scalar_prefetch=1, grid=(nb,),
            in_specs=[pl.BlockSpec((None, tr, cols), lambda i, kc: (kc[0], i, 0)), rspec(0), rspec(1), rspec(2)],
            out_specs=pl.BlockSpec((tr, cols), lambda i, kc: (kc[1] * nb + i, 0))),
        out_shape=jax.ShapeDtypeStruct((2 * h, cols), F32), compiler_params=_cparams("parallel"),
    )(kc_arr, p, r, r, r)


def _reunite_sibling(fs, shapes):
    n_chunks = 2
    nm = len(fs)

    def body(*refs):
        in_refs, out_refs = refs[:nm], refs[nm:2 * nm]
        send_sems, recv_sems = refs[2 * nm:]
        x, y, c, _ = _position()
        sends, recvs = [], []
        for m in range(nm):
            ch = shapes[m][0] // 2 // n_chunks
            for q in range(n_chunks):
                mine = pl.ds((c * n_chunks + q) * ch, ch)
                theirs = pl.ds(((1 - c) * n_chunks + q) * ch, ch)
                s = n_chunks * m + q
                sends.append(_remote(in_refs[m].at[mine, :], out_refs[m].at[mine, :], send_sems, recv_sems, s, (x, y, 1 - c)))
                recvs.append(_remote(in_refs[m].at[theirs, :], out_refs[m].at[theirs, :], send_sems, recv_sems, s, (x, y, c)))
        for cp in sends:
            cp.start()
        for cp in recvs:
            cp.wait_recv()
        for cp in sends:
            cp.wait_send()

    n_sems = n_chunks * nm
    return pl.pallas_call(
        body, name="reunite_sibling", in_specs=[HBM_SPEC] * nm, out_specs=[HBM_SPEC] * nm,
        out_shape=[jax.ShapeDtypeStruct(s, F32) for s in shapes],
        input_output_aliases={m: m for m in range(nm)},
        scratch_shapes=[pltpu.SemaphoreType.DMA((n_sems,)), pltpu.SemaphoreType.DMA((n_sems,))],
    )(*fs)


def _sum_vecs(v):
    def body(v_ref, o_ref):
        acc = v_ref[0]
        for d in range(1, 8):
            acc = acc + v_ref[d]
        o_ref[...] = acc

    return pl.pallas_call(body, name="sum_vecs", out_shape=jax.ShapeDtypeStruct(v.shape[1:], F32))(v)


def kernel(x, w_in, gate_up, gate_bias, gla_norm_g, w_out, ln1_g, ln1_b, w_up, conv_w, conv_b, w_down, ln2_g, ln2_b, loss_target, m_w_in, m_gate_up, m_gate_bias, m_gla_norm_g, m_w_out, m_ln1_g, m_ln1_b, m_w_up, m_conv_w, m_conv_b, m_w_down, m_ln2_g, m_ln2_b, v_w_in, v_gate_up, v_gate_bias, v_gla_norm_g, v_w_out, v_ln1_g, v_ln1_b, v_w_up, v_conv_w, v_conv_b, v_w_down, v_ln2_g, v_ln2_b):
    xi, yi, ci = lax.axis_index("x"), lax.axis_index("y"), lax.axis_index("c")
    k_me = 2 * xi + yi
    c_arr = ci.astype(jnp.int32).reshape(1)
    kc_arr = jnp.stack([k_me, ci]).astype(jnp.int32)
    small = _pad_rows(jnp.concatenate([_rows(conv_w[0]), _rows(gate_up[0])], axis=0), SMALL_ROWS)
    w_in4, gsmall = _gather_weights([w_in[0].astype(BF16)], small, SHARD_SHAPES[:1])
    late_shards = [w_up[0].astype(BF16), w_out[0].astype(BF16), w_down[0].astype(BF16)]
    w_in_p = jnp.pad(jnp.concatenate([w_in4[k] for k in range(4)], axis=1), ((0, 0), (0, IN_PAD - IN_WIDTH)))
    conv_w_f = jnp.concatenate([gsmall[k, :5].reshape(-1)[:3 * W_UP_S].reshape(3, W_UP_S) for k in range(4)], axis=1)
    gate_up_f = jnp.concatenate([gsmall[k, 5].reshape(GATE_RANK, GLA_KEYS // 4) for k in range(4)], axis=1)
    conv_w3 = jnp.transpose(conv_w_f.reshape(3, 2, D_FF), (1, 0, 2))
    conv_b3 = conv_b.reshape(2, 1, D_FF)
    gate_up_pad = jnp.pad(gate_up_f, ((0, LANES - GATE_RANK), (0, 0)))

    dx, sums, vecs = _local_step(
        x[0], loss_target[0], w_in_p, late_shards, gate_up_pad, gate_bias, gla_norm_g, ln1_g, ln1_b, conv_w3, conv_b3,
        ln2_g, ln2_b, c_arr, kc_arr)
    g_w_in, g_w_up, g_w_out, g_w_down = _reunite_sibling(sums, SHARD_SHAPES)
    gsmall_sum = _unpack_vec(_sum_vecs(vecs), SMALL_GRADS)
    g_conv_w = lax.dynamic_slice_in_dim(gsmall_sum["conv_w"].reshape(3, 2 * D_FF), k_me * W_UP_S, W_UP_S, axis=1)
    g_gate_up = lax.dynamic_slice_in_dim(gsmall_sum["gate_up"].reshape(GATE_RANK, GLA_KEYS), k_me * (GLA_KEYS // 4),
                                         GLA_KEYS // 4, axis=1)
    gv = gsmall_sum
    loss = gv["loss"][0, 0]
    gvec = _pack_vec(gv, VEC_SIZES, VEC_ROWS)

    grads = dict(w_in=g_w_in[None], gate_up=g_gate_up[None], gate_bias=gv["gate_bias"], gla_norm_g=gv["gla_norm_g"],
                 w_out=g_w_out[None], ln1_g=gv["ln1_g"], ln1_b=gv["ln1_b"], w_up=g_w_up[None], conv_w=g_conv_w[None],
                 conv_b=gv["conv_b"], w_down=g_w_down[None], ln2_g=gv["ln2_g"], ln2_b=gv["ln2_b"])
    weights = dict(w_in=w_in, gate_up=gate_up, gate_bias=gate_bias, gla_norm_g=gla_norm_g, w_out=w_out, ln1_g=ln1_g,
                   ln1_b=ln1_b, w_up=w_up, conv_w=conv_w, conv_b=conv_b, w_down=w_down, ln2_g=ln2_g, ln2_b=ln2_b)
    ms = dict(w_in=m_w_in, gate_up=m_gate_up, gate_bias=m_gate_bias, gla_norm_g=m_gla_norm_g, w_out=m_w_out, ln1_g=m_ln1_g,
              ln1_b=m_ln1_b, w_up=m_w_up, conv_w=m_conv_w, conv_b=m_conv_b, w_down=m_w_down, ln2_g=m_ln2_g, ln2_b=m_ln2_b)
    vs = dict(w_in=v_w_in, gate_up=v_gate_up, gate_bias=v_gate_bias, gla_norm_g=v_gla_norm_g, w_out=v_w_out, ln1_g=v_ln1_g,
              ln1_b=v_ln1_b, w_up=v_w_up, conv_w=v_conv_w, conv_b=v_conv_b, w_down=v_w_down, ln2_g=v_ln2_g, ln2_b=v_ln2_b)
    names = ["w_in", "gate_up", "gate_bias", "gla_norm_g", "w_out", "ln1_g", "ln1_b", "w_up", "conv_w", "conv_b", "w_down",
             "ln2_g", "ln2_b"]
    delta, new_m, new_v = {}, {}, {}
    for n in ("w_in", "gate_up", "w_out", "w_up", "conv_w", "w_down"):
        tr_ = jnp.transpose if n == "w_in" else (lambda a: a)
        d, nm, nv = _adamw(tr_(weights[n][0]), tr_(grads[n][0]), tr_(ms[n][0]), tr_(vs[n][0]), "adamw_" + n)
        delta[n], new_m[n], new_v[n] = tr_(d)[None], tr_(nm)[None], tr_(nv)[None]
    d, nm, nv = _adamw(_pack_vec(weights, VEC_SIZES, VEC_ROWS), gvec, _pack_vec(ms, VEC_SIZES, VEC_ROWS),
                       _pack_vec(vs, VEC_SIZES, VEC_ROWS), "adamw_vectors")
    for dst, src in ((delta, d), (new_m, nm), (new_v, nv)):
        dst.update(_unpack_vec(src, VEC_SIZES))
    return (loss, dx[None], *[grads[n] for n in names], *[delta[n] for n in names], *[new_m[n] for n in names],
            *[new_v[n] for n in names])
```

```python
import functools
import math

import jax
import jax.numpy as jnp
from jax import lax
from jax.experimental import pallas as pl
from jax.experimental.pallas import tpu as pltpu

F32 = jnp.float32
BF16 = jnp.bfloat16

D_MODEL = 1024
SB_WIDTH = 512
GLA_KEYS = 256
GLA_WIDTH = 512
GATE_RANK = 16
IN_WIDTH = 3088
IN_PAD = 3200
D_FF = 2816
CHUNK = 64
LN_EPS = 1e-5
RMS_EPS = 1e-6
ALPHA = 2.0 ** 0.25
GLA_TAU = 16.0
SB_SCALE = 0.125
GLA_SCALE = 0.125
LANES = 128
SB_BLK = 256
SB_CUT = -100.0
GLA_ROWS = 256
VMEM_LIMIT = 56 * 1024 * 1024

ADAM_LR, ADAM_B1, ADAM_B2, ADAM_EPS, ADAM_WD, ADAM_STEP = 0.001, 0.9, 0.999, 1e-08, 0.01, 10

MESH = pl.DeviceIdType.MESH


def _cparams(*sem):
    return pltpu.CompilerParams(dimension_semantics=sem, vmem_limit_bytes=VMEM_LIMIT)


def _dot(a, b):
    return jnp.dot(a, b, preferred_element_type=F32)


def _dot_nt(a, b):
    return lax.dot_general(a, b, (((1,), (1,)), ((), ())), preferred_element_type=F32)


def _dot_tn(a, b):
    return lax.dot_general(a, b, (((0,), (0,)), ((), ())), preferred_element_type=F32)


def _split3(x):
    hi = x.astype(BF16)
    r = x - hi.astype(F32)
    mid = r.astype(BF16)
    lo = (r - mid.astype(F32)).astype(BF16)
    return hi, mid, lo


def _softplus(z):
    return jnp.maximum(z, 0.0) + jnp.log(1.0 + jnp.exp(-jnp.abs(z)))


def _sigmoid(z):
    return 1.0 / (1.0 + jnp.exp(-z))


def _mm(a, b, *, m, n, k, tm, tn, tk, ta=False, tb=False, a_spec=None, b_spec=None, o_spec=None,
        out_shape=None, out_dtype=F32, add=None, add_scale=1.0, rider=None, name):
    nk = k // tk
    dn = (((0 if ta else 1,), (1 if tb else 0,)), ((), ()))
    n_in = len(rider.inputs) if rider else 0
    n_out = len(rider.out_shapes) if rider else 0
    n_add = int(add is not None)
    steps = (m // tm, n // tn, nk)

    def body(*refs):
        a_ref, b_ref = refs[:2]
        add_ref = refs[2] if add is not None else None
        rin = refs[2 + n_add:2 + n_add + n_in]
        o_ref = refs[2 + n_add + n_in]
        rout = refs[3 + n_add + n_in:3 + n_add + n_in + n_out]
        scratch = refs[3 + n_add + n_in + n_out:]
        if rider:
            sems = scratch[len(scratch) - len(rider.sems):]
            ids = [pl.program_id(d) for d in range(3)]
            _ride(rider, rin, rout, sems, (ids[0] == 0) & (ids[1] == 0) & (ids[2] == 0),
                  (ids[0] == steps[0] - 1) & (ids[1] == steps[1] - 1) & (ids[2] == steps[2] - 1))

        part = lax.dot_general(a_ref[...].astype(BF16), b_ref[...].astype(BF16), dn, preferred_element_type=F32)

        def finish(r):
            if add is not None:
                r = r + add_scale * add_ref[...]
            o_ref[...] = r.astype(out_dtype)

        if nk == 1:
            finish(part)
            return
        acc_ref = scratch[0]
        kk = pl.program_id(2)

        @pl.when(kk == 0)
        def _():
            acc_ref[...] = part

        @pl.when((kk > 0) & (kk < nk - 1))
        def _():
            acc_ref[...] += part

        @pl.when(kk == nk - 1)
        def _():
            finish(acc_ref[...] + part)

    if a_spec is None:
        a_spec = pl.BlockSpec((tk, tm), lambda i, j, kk: (kk, i)) if ta else pl.BlockSpec((tm, tk), lambda i, j, kk: (i, kk))
    if b_spec is None:
        b_spec = pl.BlockSpec((tn, tk), lambda i, j, kk: (j, kk)) if tb else pl.BlockSpec((tk, tn), lambda i, j, kk: (kk, j))
    if o_spec is None:
        o_spec = pl.BlockSpec((tm, tn), lambda i, j, kk: (i, j))
    if out_shape is None:
        out_shape = jax.ShapeDtypeStruct((m, n), out_dtype)
    in_specs = [a_spec, b_spec]
    args = [a, b]
    if add is not None:
        in_specs.append(pl.BlockSpec((tm, tn), lambda i, j, kk: (i, j)))
        args.append(add)
    scratch = [pltpu.VMEM((tm, tn), F32)] if nk > 1 else []
    if not rider:
        return pl.pallas_call(
            body, name=name, grid=steps, in_specs=in_specs, out_specs=o_spec, out_shape=out_shape,
            scratch_shapes=scratch, compiler_params=_cparams("parallel", "parallel", "arbitrary"),
        )(*args)
    return pl.pallas_call(
        body, name=name, grid=steps, in_specs=in_specs + [HBM_SPEC] * n_in, out_specs=[o_spec] + [HBM_SPEC] * n_out,
        out_shape=[out_shape] + list(rider.out_shapes),
        input_output_aliases={len(args) + i: 1 + o for i, o in rider.aliases},
        scratch_shapes=scratch + [pltpu.SemaphoreType.DMA((s,)) for s in rider.sems],
        compiler_params=_cparams("arbitrary", "arbitrary", "arbitrary"),
    )(*args, *rider.inputs)


def _col_offsets(pieces):
    offs, o = [], 0
    for a in pieces:
        offs.append(o)
        o += a.shape[1]
    return offs


def _dx(pieces, w_in_p, dr1, t, rider, tm=512):
    offs = _col_offsets(pieces)
    npc = len(pieces)
    n_in, n_out = len(rider.inputs), len(rider.out_shapes)
    steps = t // tm

    def body(*refs):
        p_refs, w_ref, add_ref = refs[:npc], refs[npc], refs[npc + 1]
        rin = refs[npc + 2:npc + 2 + n_in]
        o_ref = refs[npc + 2 + n_in]
        rout = refs[npc + 3 + n_in:npc + 3 + n_in + n_out]
        i = pl.program_id(0)
        _ride(rider, rin, rout, refs[npc + 3 + n_in + n_out:], i == 0, i == steps - 1)
        acc = ALPHA * add_ref[...]
        for p_ref, off in zip(p_refs, offs):
            acc = acc + _dot_nt(p_ref[...].astype(BF16), w_ref[:, off:off + p_ref.shape[1]])
        o_ref[...] = acc

    row = lambda w: pl.BlockSpec((tm, w), lambda i: (i, 0))
    return pl.pallas_call(
        body, name="dx", grid=(steps,),
        in_specs=[row(a.shape[1]) for a in pieces] + [pl.BlockSpec(w_in_p.shape, lambda i: (0, 0)), row(D_MODEL)]
        + [HBM_SPEC] * n_in,
        out_specs=[row(D_MODEL)] + [HBM_SPEC] * n_out,
        out_shape=[jax.ShapeDtypeStruct((t, D_MODEL), F32)] + rider.out_shapes,
        scratch_shapes=[pltpu.SemaphoreType.DMA((s,)) for s in rider.sems],
        compiler_params=_cparams("arbitrary"),
    )(*pieces, w_in_p, dr1, *rider.inputs)


def _dw_in(x, pieces, t, tm=512, tk=512):
    offs = _col_offsets(pieces)
    npc = len(pieces)
    n = offs[-1] + pieces[-1].shape[1]
    nk = t // tk
    assert nk >= 2

    def body(*refs):
        x_ref, p_refs, o_ref, acc_ref = refs[0], refs[1:1 + npc], refs[1 + npc], refs[2 + npc]
        kk = pl.program_id(1)
        xb = x_ref[...].astype(BF16)
        for p_ref, off in zip(p_refs, offs):
            cols = slice(off, off + p_ref.shape[1])
            part = _dot_tn(xb, p_ref[...].astype(BF16))

            @pl.when(kk == 0)
            def _():
                acc_ref[:, cols] = part

            @pl.when((kk > 0) & (kk < nk - 1))
            def _():
                acc_ref[:, cols] += part

            @pl.when(kk == nk - 1)
            def _():
                o_ref[:, cols] = acc_ref[:, cols] + part

    return pl.pallas_call(
        body, name="dw_in", grid=(D_MODEL // tm, nk),
        in_specs=[pl.BlockSpec((tk, tm), lambda i, kk: (kk, i))]
        + [pl.BlockSpec((tk, a.shape[1]), lambda i, kk: (kk, 0)) for a in pieces],
        out_specs=pl.BlockSpec((tm, n), lambda i, kk: (i, 0)),
        out_shape=jax.ShapeDtypeStruct((D_MODEL, n), F32),
        scratch_shapes=[pltpu.VMEM((tm, n), F32)],
        compiler_params=_cparams("parallel", "arbitrary"),
    )(x, *pieces)


class _Rider:
    def __init__(self, inputs, out_shapes, sems, copies, aliases=()):
        self.inputs, self.out_shapes, self.sems, self.copies = list(inputs), list(out_shapes), tuple(sems), copies
        self.aliases = tuple(aliases)


def _join(a, b):
    na_in, na_out, na_sems = len(a.inputs), len(a.out_shapes), len(a.sems)

    def copies(ins, outs, sems):
        first = a.copies(ins[:na_in], outs[:na_out], sems[:na_sems])
        second = b.copies(ins[na_in:], outs[na_out:], sems[na_sems:])
        return tuple(u + v for u, v in zip(first, second))

    return _Rider(a.inputs + b.inputs, a.out_shapes + b.out_shapes, a.sems + b.sems, copies,
                  a.aliases + tuple((i + na_in, o + na_out) for i, o in b.aliases))


def _ride(rider, in_refs, out_refs, sems, first, last):
    @pl.when(first)
    def _():
        local, sends, _ = rider.copies(in_refs, out_refs, sems)
        for cp in local + sends:
            cp.start()

    @pl.when(last)
    def _():
        local, sends, recvs = rider.copies(in_refs, out_refs, sems)
        for cp in recvs:
            cp.wait_recv()
        for cp in sends:
            cp.wait_send()
        for cp in local:
            cp.wait()


def _run(rider, name):
    n_in, n_out = len(rider.inputs), len(rider.out_shapes)

    def body(*refs):
        local, sends, recvs = rider.copies(refs[:n_in], refs[n_in:n_in + n_out], refs[n_in + n_out:])
        for cp in local + sends:
            cp.start()
        for cp in recvs:
            cp.wait_recv()
        for cp in sends:
            cp.wait_send()
        for cp in local:
            cp.wait()

    return pl.pallas_call(
        body, name=name, in_specs=[HBM_SPEC] * n_in, out_specs=[HBM_SPEC] * n_out, out_shape=list(rider.out_shapes),
        scratch_shapes=[pltpu.SemaphoreType.DMA((s,)) for s in rider.sems],
    )(*rider.inputs)


def _sb_tile(qh, kj, diag, strict, u_strict, r_in):
    z = _dot_nt(qh, kj)
    sp = _softplus(z)
    l1m = -sp
    lsz = z - sp
    if diag:
        l1m = jnp.where(strict, l1m, 0.0)
    cs = _dot(l1m.astype(BF16), u_strict) + r_in
    w = jnp.exp(lsz + cs)
    if diag:
        w = jnp.where(strict, w, 0.0)
    return l1m, lsz, w


def _sb_consts():
    row = lax.broadcasted_iota(jnp.int32, (SB_BLK, SB_BLK), 0)
    col = lax.broadcasted_iota(jnp.int32, (SB_BLK, SB_BLK), 1)
    strict = col < row
    u_strict = (row > col).astype(BF16)
    u_pre = (row < col).astype(BF16)
    lane = lax.broadcasted_iota(jnp.int32, (1, LANES), 1)
    return strict, u_strict, u_pre, lane


def _sb_fwd(proj, t, rider):
    nq = t // SB_BLK
    n_in, n_out = len(rider.inputs), len(rider.out_shapes)

    def body(q_ref, k_ref, v_ref, *rest):
        rin, o_ref, sv_ref, rout = rest[:n_in], rest[n_in], rest[n_in + 1], rest[n_in + 2:n_in + 2 + n_out]
        p = pl.program_id(0)
        i = pl.program_id(1)
        _ride(rider, rin, rout, rest[n_in + 2 + n_out:], (p == 0) & (i == 0), (p == 3) & (i == nq - 1))

        strict, u_strict, _, lane = _sb_consts()
        qf = q_ref[...] * SB_SCALE
        hms = [(lane // 64) == hh for hh in range(2)]
        qhs = [jnp.where(hm, qf, 0.0).astype(BF16) for hm in hms]

        def step(j, r0, r1, a, sv, diag, keep=None):
            rows = pl.ds(pl.multiple_of(j * SB_BLK, SB_BLK), SB_BLK)
            kj = k_ref[rows, :].astype(BF16)
            vf = v_ref[rows, :]
            rs = []
            for hh, r in enumerate((r0, r1)):
                l1m, _, w = _sb_tile(qhs[hh], kj, diag, strict, u_strict, r)
                pv = _dot(w.astype(BF16), jnp.where(hms[hh], vf, 0.0).astype(BF16))
                mass = jnp.sum(l1m, axis=1, keepdims=True)
                if keep is not None:
                    pv, mass = jnp.where(keep, pv, 0.0), jnp.where(keep, mass, 0.0)
                a = a + pv
                rs.append(r + mass)
                sv = jnp.where(lane == hh * 64 + (i - j + 1), rs[hh], sv)
            return rs[0], rs[1], a, sv

        zero = jnp.zeros((SB_BLK, 1), F32)
        acc0 = jnp.zeros((SB_BLK, LANES), F32)
        r0, r1, acc, sv = step(i, zero, zero, acc0, acc0, True)
        r0, r1, acc, sv = step(jnp.maximum(i - 1, 0), r0, r1, acc, sv, False, keep=i > 0)
        _, _, _, acc, sv = lax.while_loop(
            lambda c: (c[0] >= 0) & (jnp.maximum(jnp.max(c[1]), jnp.max(c[2])) > SB_CUT),
            lambda c: (c[0] - 1,) + step(c[0], c[1], c[2], c[3], c[4], False),
            (i - 2, r0, r1, acc, sv))
        o_ref[...] = acc.astype(BF16)
        sv_ref[...] = sv

    return pl.pallas_call(
        body, name="sb_fwd", grid=(4, nq),
        in_specs=[pl.BlockSpec((SB_BLK, LANES), lambda p, i: (i, p)),
                  pl.BlockSpec((t, LANES), lambda p, i: (0, 4 + p)),
                  pl.BlockSpec((t, LANES), lambda p, i: (0, 8 + p))] + [HBM_SPEC] * n_in,
        out_specs=[pl.BlockSpec((SB_BLK, LANES), lambda p, i: (i, p))] * 2 + [HBM_SPEC] * n_out,
        out_shape=[jax.ShapeDtypeStruct((t, SB_WIDTH), BF16), jax.ShapeDtypeStruct((t, SB_WIDTH), F32)] + rider.out_shapes,
        scratch_shapes=[pltpu.SemaphoreType.DMA((s,)) for s in rider.sems],
        compiler_params=_cparams("arbitrary", "arbitrary"),
    )(proj, proj, proj, *rider.inputs)


def _sb_bwd(proj, dcat, mass, t, rider):
    nq = t // SB_BLK
    n_in, n_out = len(rider.inputs), len(rider.out_shapes)

    def body(q_ref, k_ref, v_ref, do_ref, sv_ref, *rest):
        rin = rest[:n_in]
        dq_ref, dk_ref, dv_ref = rest[n_in:n_in + 3]
        rout = rest[n_in + 3:n_in + 3 + n_out]
        p = pl.program_id(0)
        i = pl.program_id(1)
        _ride(rider, rin, rout, rest[n_in + 3 + n_out:], (p == 0) & (i == 0), (p == 3) & (i == nq - 1))

        @pl.when(i == 0)
        def _():
            dk_ref[...] = jnp.zeros_like(dk_ref)
            dv_ref[...] = jnp.zeros_like(dv_ref)

        strict, u_strict, u_pre, lane = _sb_consts()
        qf = q_ref[...] * SB_SCALE
        dof = do_ref[...]
        hms = [(lane // 64) == hh for hh in range(2)]
        qhs = [jnp.where(hm, qf, 0.0).astype(BF16) for hm in hms]
        dohs = [jnp.where(hm, dof, 0.0).astype(BF16) for hm in hms]

        sv = sv_ref[...]
        zero = jnp.zeros((SB_BLK, 1), F32)

        def mass_right(hh, d):
            return jnp.sum(jnp.where(lane == hh * 64 + d, sv, 0.0), axis=1, keepdims=True)

        dstop = lax.while_loop(
            lambda d: (i - d >= 0) & (jnp.maximum(jnp.max(mass_right(0, d)), jnp.max(mass_right(1, d))) > SB_CUT),
            lambda d: d + 1, 1)
        jstop = i - dstop

        def step(j, carry, diag, keep=None):
            pre_g0, pre_g1, dqa = carry
            rows = pl.ds(pl.multiple_of(j * SB_BLK, SB_BLK), SB_BLK)
            kf = k_ref[rows, :]
            kj = kf.astype(BF16)
            vj = v_ref[rows, :].astype(BF16)
            dv = jnp.zeros((SB_BLK, LANES), F32)
            dk = jnp.zeros((SB_BLK, LANES), F32)
            dqj = jnp.zeros((SB_BLK, LANES), F32)
            pre = []
            for hh, pre_g in enumerate((pre_g0, pre_g1)):
                _, lsz, w = _sb_tile(qhs[hh], kj, diag, strict, u_strict, zero if diag else mass_right(hh, i - j))
                g = w * _dot_nt(dohs[hh], vj)
                gpre = _dot(g.astype(BF16), u_pre) + pre_g
                sig = jnp.exp(lsz)
                dz = g * (1.0 - sig) - gpre * sig
                if diag:
                    dz = jnp.where(strict, dz, 0.0)
                dzb = dz.astype(BF16)
                dv = dv + _dot_tn(w.astype(BF16), dohs[hh])
                dk = dk + _dot_tn(dzb, qhs[hh])
                dqj = dqj + _dot(dzb, jnp.where(hms[hh], kf, 0.0).astype(BF16))
                gsum = jnp.sum(g, axis=1, keepdims=True)
                pre.append(pre_g + (gsum if keep is None else jnp.where(keep, gsum, 0.0)))
            if keep is not None:
                dv, dk, dqj = jnp.where(keep, dv, 0.0), jnp.where(keep, dk, 0.0), jnp.where(keep, dqj, 0.0)
            dv_ref[rows, :] += dv
            dk_ref[rows, :] += dk
            return pre[0], pre[1], dqa + dqj

        carry = lax.fori_loop(jstop + 1, i - 1, lambda j, c: step(j, c, False), (zero, zero, jnp.zeros((SB_BLK, LANES), F32)))
        carry = step(jnp.maximum(i - 1, 0), carry, False, keep=i > 0)
        _, _, dq = step(i, carry, True)
        dq_ref[...] = dq * SB_SCALE

    return pl.pallas_call(
        body, name="sb_bwd", grid=(4, nq),
        in_specs=[pl.BlockSpec((SB_BLK, LANES), lambda p, i: (i, p)),
                  pl.BlockSpec((t, LANES), lambda p, i: (0, 4 + p)),
                  pl.BlockSpec((t, LANES), lambda p, i: (0, 8 + p)),
                  pl.BlockSpec((SB_BLK, LANES), lambda p, i: (i, p)),
                  pl.BlockSpec((SB_BLK, LANES), lambda p, i: (i, p))] + [HBM_SPEC] * n_in,
        out_specs=[pl.BlockSpec((SB_BLK, LANES), lambda p, i: (i, p)),
                   pl.BlockSpec((t, LANES), lambda p, i: (0, p)),
                   pl.BlockSpec((t, LANES), lambda p, i: (0, p))] + [HBM_SPEC] * n_out,
        out_shape=[jax.ShapeDtypeStruct((t, SB_WIDTH), F32)] * 3 + rider.out_shapes,
        scratch_shapes=[pltpu.SemaphoreType.DMA((s,)) for s in rider.sems],
        compiler_params=_cparams("arbitrary", "arbitrary"),
    )(proj, proj, proj, dcat, mass, *rider.inputs)


def _gla_consts():
    r = lax.broadcasted_iota(jnp.int32, (GLA_ROWS, GLA_ROWS), 0)
    c = lax.broadcasted_iota(jnp.int32, (GLA_ROWS, GLA_ROWS), 1)
    same = (r // CHUNK) == (c // CHUNK)
    causal = same & (c <= r)
    upto_mid = c % CHUNK <= CHUNK // 2 - 1
    fwd_stack = jnp.concatenate([causal, same & upto_mid, same], axis=0).astype(BF16)
    bwd_stack = jnp.concatenate([same & (c >= r), same & (r % CHUNK <= CHUNK // 2 - 1), same], axis=1).astype(BF16)
    rowid = lax.broadcasted_iota(jnp.int32, (GLA_ROWS, 1), 0)
    lane = lax.broadcasted_iota(jnp.int32, (1, LANES), 1)
    sr = lax.broadcasted_iota(jnp.int32, (LANES, 2 * LANES), 0)
    sc = lax.broadcasted_iota(jnp.int32, (LANES, 2 * LANES), 1)
    blockdiag = (sr // 64) == (sc // LANES)
    return causal, fwd_stack, bwd_stack, rowid, lane, blockdiag


def _dot3(u, x):
    hi, mid, lo = _split3(x)
    return _dot(u, hi) + _dot(u, mid) + _dot(u, lo)


def _row_to_col(row):
    return jnp.transpose(jnp.broadcast_to(row, (LANES, LANES)))


def _gla_gates(ga_ref, gup_ref, gbias_ref):
    pre = _dot(ga_ref[...].astype(BF16), gup_ref[...].astype(BF16)) + gbias_ref[...]
    log_a = (jnp.minimum(pre, 0.0) - jnp.log(1.0 + jnp.exp(-jnp.abs(pre)))) / GLA_TAU
    return pre, log_a


def _gla_terms(g2, q2, k2, fwd_stack):
    bs = _dot3(fwd_stack, g2)
    b, b_ref, b_last = bs[:GLA_ROWS], bs[GLA_ROWS:2 * GLA_ROWS], bs[2 * GLA_ROWS:]
    qs = q2 * GLA_SCALE
    e_q = jnp.exp(b - b_ref)
    e_k = jnp.exp(b_ref - b)
    e_d = jnp.exp(b_last - b)
    e_b = jnp.exp(b)
    decay = jnp.exp(b_last)
    return dict(qs=qs, e_q=e_q, e_k=e_k, e_d=e_d, e_b=e_b, decay=decay,
                qi=qs * e_q, ki=k2 * e_k, kd=k2 * e_d, qb=qs * e_b)


def _gla_fwd(proj, gate_up_pad, gate_bias, gnorm, t, rider):
    nsteps = t // GLA_ROWS
    cps = GLA_ROWS // CHUNK
    n_in, n_out = len(rider.inputs), len(rider.out_shapes)

    def body(q_ref, k_ref, v_ref, gg_ref, ga_ref, gup_ref, gbias_ref, gn_ref, *rest):
        o_ref, oraw_ref, st_ref = rest[n_in:n_in + 3]
        s_scr = rest[n_in + 3 + n_out]
        i = pl.program_id(0)
        _ride(rider, rest[:n_in], rest[n_in + 3:n_in + 3 + n_out], rest[n_in + 4 + n_out:], i == 0, i == nsteps - 1)

        @pl.when(i == 0)
        def _():
            s_scr[...] = jnp.zeros_like(s_scr)

        causal, fwd_stack, _, _, lane, blockdiag = _gla_consts()
        _, log_a = _gla_gates(ga_ref, gup_ref, gbias_ref)
        gn = gn_ref[...]
        for p in range(2):
            kl = slice(p * LANES, (p + 1) * LANES)
            vl = slice(p * 2 * LANES, (p + 1) * 2 * LANES)
            tm = _gla_terms(log_a[:, kl], q_ref[:, kl], k_ref[:, kl], fwd_stack)
            v2b = v_ref[:, vl].astype(BF16)
            kib = tm["ki"].astype(BF16)
            kdb = tm["kd"].astype(BF16)
            qbb = tm["qb"].astype(BF16)
            intra = []
            for hh in range(2):
                hm = (lane // 64) == hh
                a = jnp.where(causal, _dot_nt(jnp.where(hm, tm["qi"], 0.0).astype(BF16), kib), 0.0)
                intra.append(_dot(a.astype(BF16), v2b[:, hh * LANES:(hh + 1) * LANES]))
            state = s_scr[p]
            inter = []
            for cc in range(cps):
                rows = slice(cc * CHUNK, (cc + 1) * CHUNK)
                st_ref[cc, p] = state
                inter.append(_dot(qbb[rows], state.astype(BF16)))
                upd = jnp.where(blockdiag, _dot_tn(kdb[rows], v2b[rows]), 0.0)
                dcol = _row_to_col(tm["decay"][cc * CHUNK:cc * CHUNK + 1])
                state = state * jnp.concatenate([dcol, dcol], axis=1) + upd
            s_scr[p] = state
            o2 = jnp.concatenate(intra, axis=1) + jnp.concatenate(inter, axis=0)
            oraw_ref[:, vl] = o2
            for hh in range(2):
                oh = o2[:, hh * LANES:(hh + 1) * LANES]
                gl = slice(p * 2 * LANES + hh * LANES, p * 2 * LANES + (hh + 1) * LANES)
                rinv = lax.rsqrt(jnp.mean(oh * oh, axis=1, keepdims=True) + RMS_EPS)
                gg = gg_ref[:, gl]
                o_ref[:, gl] = (oh * rinv * gn * (gg * _sigmoid(gg))).astype(BF16)

    cb = lambda w, idx: pl.BlockSpec((GLA_ROWS, w), lambda i: (i, idx))
    full = lambda shp: pl.BlockSpec(shp, lambda i: tuple(0 for _ in shp))
    return pl.pallas_call(
        body, name="gla_fwd", grid=(nsteps,),
        in_specs=[cb(256, 6), cb(256, 7), cb(512, 4), cb(512, 5), cb(128, 24),
                  full((LANES, GLA_KEYS)), full((1, GLA_KEYS)), full((1, LANES))] + [HBM_SPEC] * n_in,
        out_specs=[pl.BlockSpec((GLA_ROWS, GLA_WIDTH), lambda i: (i, 0)),
                   pl.BlockSpec((GLA_ROWS, GLA_WIDTH), lambda i: (i, 0)),
                   pl.BlockSpec((cps, 2, LANES, 2 * LANES), lambda i: (i, 0, 0, 0))] + [HBM_SPEC] * n_out,
        out_shape=[jax.ShapeDtypeStruct((t, GLA_WIDTH), BF16), jax.ShapeDtypeStruct((t, GLA_WIDTH), F32),
                   jax.ShapeDtypeStruct((t // CHUNK, 2, LANES, 2 * LANES), F32)]
        + rider.out_shapes,
        input_output_aliases={8 + i: 3 + o for i, o in rider.aliases},
        scratch_shapes=[pltpu.VMEM((2, LANES, 2 * LANES), F32)] + [pltpu.SemaphoreType.DMA((s,)) for s in rider.sems],
        compiler_params=_cparams("arbitrary"),
    )(proj, proj, proj, proj, proj, gate_up_pad, gate_bias, gnorm, *rider.inputs)


def _gla_bwd(proj, dcat, oraw, states, gate_up_pad, gate_bias, gnorm, t):
    nsteps = t // GLA_ROWS
    cps = GLA_ROWS // CHUNK
    wout = 2 * GLA_KEYS + 2 * GLA_WIDTH + LANES

    def body(q_ref, k_ref, v_ref, gg_ref, ga_ref, do_ref, oraw_ref, st_ref, gup_ref, gbias_ref, gn_ref,
             d_ref, dgup_ref, dgbias_ref, dgn_ref, ds_scr):
        i = pl.program_id(0)

        @pl.when(i == 0)
        def _():
            ds_scr[...] = jnp.zeros_like(ds_scr)
            dgup_ref[...] = jnp.zeros_like(dgup_ref)
            dgbias_ref[...] = jnp.zeros_like(dgbias_ref)
            dgn_ref[...] = jnp.zeros_like(dgn_ref)

        causal, fwd_stack, bwd_stack, rowid, lane, blockdiag = _gla_consts()
        pre, log_a = _gla_gates(ga_ref, gup_ref, gbias_ref)
        gn = gn_ref[...]
        dgn = jnp.zeros((1, LANES), F32)
        dgs = []
        for p in range(2):
            kl = slice(p * LANES, (p + 1) * LANES)
            vl = slice(p * 2 * LANES, (p + 1) * 2 * LANES)
            tm = _gla_terms(log_a[:, kl], q_ref[:, kl], k_ref[:, kl], fwd_stack)
            v2b = v_ref[:, vl].astype(BF16)
            dos = []
            for hh in range(2):
                gl = slice(p * 2 * LANES + hh * LANES, p * 2 * LANES + (hh + 1) * LANES)
                oh = oraw_ref[:, gl]
                rinv = lax.rsqrt(jnp.mean(oh * oh, axis=1, keepdims=True) + RMS_EPS)
                on = oh * rinv
                gg = gg_ref[:, gl]
                sg = _sigmoid(gg)
                sil = gg * sg
                dgo = do_ref[:, gl]
                d_ref[:, 2 * GLA_KEYS + GLA_WIDTH + gl.start:2 * GLA_KEYS + GLA_WIDTH + gl.stop] = (
                    dgo * on * gn * (sg * (1.0 + gg * (1.0 - sg))))
                dgn = dgn + jnp.sum(dgo * sil * on, axis=0, keepdims=True)
                don = dgo * sil * gn
                dos.append(rinv * (don - on * jnp.mean(don * on, axis=1, keepdims=True)))
            do2b = jnp.concatenate(dos, axis=1).astype(BF16)
            qib = tm["qi"].astype(BF16)
            kib = tm["ki"].astype(BF16)
            kdb = tm["kd"].astype(BF16)
            qbb = tm["qb"].astype(BF16)
            dqi = jnp.zeros((GLA_ROWS, LANES), F32)
            dki = jnp.zeros((GLA_ROWS, LANES), F32)
            dvs = []
            for hh in range(2):
                hm = (lane // 64) == hh
                hl = slice(hh * LANES, (hh + 1) * LANES)
                a = jnp.where(causal, _dot_nt(jnp.where(hm, tm["qi"], 0.0).astype(BF16), kib), 0.0).astype(BF16)
                da = jnp.where(causal, _dot_nt(do2b[:, hl], v2b[:, hl]), 0.0).astype(BF16)
                dvs.append(_dot_tn(a, do2b[:, hl]))
                dqi = dqi + jnp.where(hm, _dot(da, kib), 0.0)
                dki = dki + jnp.where(hm, _dot_tn(da, qib), 0.0)
            ds2 = ds_scr[p]
            dv_st, dqb, dkd, dd = [None] * cps, [None] * cps, [None] * cps, [None] * cps
            for cc in reversed(range(cps)):
                rows = slice(cc * CHUNK, (cc + 1) * CHUNK)
                s_prev = st_ref[cc, p]
                ds2b = ds2.astype(BF16)
                dv_st[cc] = _dot(kdb[rows], ds2b)
                dqb[cc] = _dot_nt(do2b[rows], s_prev.astype(BF16))
                dkd[cc] = _dot_nt(v2b[rows], ds2b)
                decay_row = tm["decay"][cc * CHUNK:cc * CHUNK + 1]
                ddecay_col = jnp.sum(ds2 * s_prev, axis=1, keepdims=True)
                ddecay_row = jnp.transpose(jnp.broadcast_to(ddecay_col, (LANES, LANES)))[0:1, :]
                dd[cc] = jnp.broadcast_to(ddecay_row * decay_row, (CHUNK, LANES))
                dcol = _row_to_col(decay_row)
                ds2 = jnp.where(blockdiag, _dot_tn(qbb[rows], do2b[rows]), 0.0) + ds2 * jnp.concatenate([dcol, dcol], axis=1)
            ds_scr[p] = ds2
            dv2 = jnp.concatenate(dvs, axis=1) + jnp.concatenate(dv_st, axis=0)
            dqb = jnp.concatenate(dqb, axis=0)
            dkd = jnp.concatenate(dkd, axis=0)
            dqs = dqi * tm["e_q"] + dqb * tm["e_b"]
            dk = dki * tm["e_k"] + dkd * tm["e_d"]
            t_qi = dqi * tm["qi"]
            t_ki = dki * tm["ki"]
            t_kd = dkd * tm["kd"]
            db = t_qi - t_ki + dqb * tm["qb"] - t_kd
            to_mid = t_ki - t_qi
            to_last = t_kd + jnp.where(rowid % CHUNK == CHUNK - 1, jnp.concatenate(dd, axis=0), 0.0)
            dgs.append(_dot3(bwd_stack, jnp.concatenate([db, to_mid, to_last], axis=0)))
            d_ref[:, p * LANES:(p + 1) * LANES] = dqs * GLA_SCALE
            d_ref[:, GLA_KEYS + p * LANES:GLA_KEYS + (p + 1) * LANES] = dk
            d_ref[:, 2 * GLA_KEYS + p * 2 * LANES:2 * GLA_KEYS + (p + 1) * 2 * LANES] = dv2
        dlog_a = jnp.concatenate(dgs, axis=1)
        dpre = dlog_a * (1.0 / GLA_TAU) * _sigmoid(-pre)
        dpb = dpre.astype(BF16)
        dgn_ref[...] += dgn
        dgbias_ref[...] += jnp.sum(dpre, axis=0, keepdims=True)
        dgup_ref[...] += _dot_tn(ga_ref[...].astype(BF16), dpb)
        d_ref[:, 2 * GLA_KEYS + 2 * GLA_WIDTH:] = _dot_nt(dpb, gup_ref[...].astype(BF16))

    rev = lambda i: nsteps - 1 - i
    cb = lambda w, idx: pl.BlockSpec((GLA_ROWS, w), lambda i: (rev(i), idx))
    full = lambda shp: pl.BlockSpec(shp, lambda i: tuple(0 for _ in shp))
    return pl.pallas_call(
        body, name="gla_bwd", grid=(nsteps,),
        in_specs=[cb(256, 6), cb(256, 7), cb(512, 4), cb(512, 5), cb(128, 24), cb(512, 1), cb(512, 0),
                  pl.BlockSpec((cps, 2, LANES, 2 * LANES), lambda i: (rev(i), 0, 0, 0)),
                  full((LANES, GLA_KEYS)), full((1, GLA_KEYS)), full((1, LANES))],
        out_specs=[pl.BlockSpec((GLA_ROWS, wout), lambda i: (rev(i), 0)),
                   full((LANES, GLA_KEYS)), full((1, GLA_KEYS)), full((1, LANES))],
        out_shape=[jax.ShapeDtypeStruct((t, wout), F32), jax.ShapeDtypeStruct((LANES, GLA_KEYS), F32),
                   jax.ShapeDtypeStruct((1, GLA_KEYS), F32), jax.ShapeDtypeStruct((1, LANES), F32)],
        scratch_shapes=[pltpu.VMEM((2, LANES, 2 * LANES), F32)],
        compiler_params=_cparams("arbitrary"),
    )(proj, proj, proj, proj, proj, dcat, oraw, states, gate_up_pad, gate_bias, gnorm)


def _ln_stats(r):
    mu = jnp.mean(r, axis=1, keepdims=True)
    xc = r - mu
    rstd = lax.rsqrt(jnp.mean(xc * xc, axis=1, keepdims=True) + LN_EPS)
    return xc * rstd, rstd


def _ln_bwd(dy_g, xhat, rstd):
    return rstd * (dy_g - jnp.mean(dy_g, axis=1, keepdims=True) - xhat * jnp.mean(dy_g * xhat, axis=1, keepdims=True))


def _outproj_ln1(sb_o, gla_o, x, w_out, g1, b1, t, tm=256):
    def body(sb_ref, gl_ref, x_ref, w_ref, g_ref, b_ref, xhat_ref, rstd_ref, h_ref):
        mix = _dot(sb_ref[...], w_ref[0:SB_WIDTH, :]) + _dot(gl_ref[...], w_ref[SB_WIDTH:, :])
        xhat, rstd = _ln_stats(ALPHA * x_ref[...] + mix)
        xhat_ref[...] = xhat
        rstd_ref[...] = rstd
        h_ref[...] = (xhat * g_ref[...] + b_ref[...]).astype(BF16)

    row = lambda w: pl.BlockSpec((tm, w), lambda i: (i, 0))
    full = lambda shp: pl.BlockSpec(shp, lambda i: (0, 0))
    return pl.pallas_call(
        body, name="outproj_ln1", grid=(t // tm,),
        in_specs=[row(SB_WIDTH), row(GLA_WIDTH), row(D_MODEL), full((D_MODEL, D_MODEL)), full((1, D_MODEL)), full((1, D_MODEL))],
        out_specs=[row(D_MODEL), row(1), row(D_MODEL)],
        out_shape=[jax.ShapeDtypeStruct((t, D_MODEL), F32), jax.ShapeDtypeStruct((t, 1), F32),
                   jax.ShapeDtypeStruct((t, D_MODEL), BF16)],
        compiler_params=_cparams("parallel"),
    )(sb_o, gla_o, x, w_out, g1, b1)


_INV_SQRT2 = 1.0 / math.sqrt(2.0)
_INV_SQRT2PI = 1.0 / math.sqrt(2.0 * math.pi)


def _conv3(xs, w_ref, b_ref, half):
    return (w_ref[half, 0:1, :] * pltpu.roll(xs, 2, 0) + w_ref[half, 1:2, :] * pltpu.roll(xs, 1, 0)
            + w_ref[half, 2:3, :] * xs + b_ref[half])


HALO = 16


def _conv_gelu_fwd(up3, conv_w3, conv_b3, t, tr=512, ct=256):
    nct = D_FF // ct
    hb = tr // HALO

    def body(cur_ref, prev_ref, w_ref, b_ref, gm_ref):
        i = pl.program_id(1)
        keep = (i > 0).astype(F32)
        us = []
        for half in range(2):
            xs = jnp.concatenate([prev_ref[half].astype(F32) * keep, cur_ref[half].astype(F32)], axis=0)
            us.append(_conv3(xs, w_ref, b_ref, half)[HALO:, :])
        a, c = us
        gelu = 0.5 * a * (1.0 + lax.erf(a * _INV_SQRT2))
        gm_ref[...] = (gelu * c).astype(BF16)

    return pl.pallas_call(
        body, name="conv_gelu_fwd", grid=(nct, t // tr),
        in_specs=[pl.BlockSpec((2, tr, ct), lambda j, i: (0, i, j)),
                  pl.BlockSpec((2, HALO, ct), lambda j, i: (0, jnp.maximum(i * hb - 1, 0), j)),
                  pl.BlockSpec((2, 3, ct), lambda j, i: (0, 0, j)),
                  pl.BlockSpec((2, 1, ct), lambda j, i: (0, 0, j))],
        out_specs=pl.BlockSpec((tr, ct), lambda j, i: (i, j)),
        out_shape=jax.ShapeDtypeStruct((t, D_FF), BF16),
        compiler_params=_cparams("parallel", "parallel"),
    )(up3, up3, conv_w3, conv_b3)


def _conv_gelu_bwd(up3, dgm, conv_w3, conv_b3, t, tr=512, ct=256):
    nct = D_FF // ct
    nrt = t // tr
    hb = tr // HALO
    n = tr + 2 * HALO
    lo, hi = HALO, tr + HALO

    def body(cur_ref, prev_ref, next_ref, dg_ref, dgn_ref, w_ref, b_ref, dup_ref, dcw_ref, dcb_ref):
        i = pl.program_id(1)

        @pl.when(i == 0)
        def _():
            dcw_ref[...] = jnp.zeros_like(dcw_ref)
            dcb_ref[...] = jnp.zeros_like(dcb_ref)

        keep_prev = (i > 0).astype(F32)
        keep_next = (i < nrt - 1).astype(F32)
        xs, xm1, xm2, us = [], [], [], []
        for half in range(2):
            x = jnp.concatenate([prev_ref[half].astype(F32) * keep_prev, cur_ref[half].astype(F32),
                                 next_ref[half].astype(F32)], axis=0)
            xs.append(x)
            xm1.append(pltpu.roll(x, 1, 0))
            xm2.append(pltpu.roll(x, 2, 0))
            us.append(w_ref[half, 0:1, :] * xm2[half] + w_ref[half, 1:2, :] * xm1[half]
                      + w_ref[half, 2:3, :] * x + b_ref[half])
        a, c = us
        dg = jnp.concatenate([jnp.zeros((HALO, ct), F32), dg_ref[...].astype(F32),
                              dgn_ref[...].astype(F32) * keep_next], axis=0)
        cdf = 0.5 * (1.0 + lax.erf(a * _INV_SQRT2))
        pdf = jnp.exp(-0.5 * a * a) * _INV_SQRT2PI
        dus = [dg * c * (cdf + a * pdf), dg * (a * cdf)]
        rid = lax.broadcasted_iota(jnp.int32, (8, 1), 0)
        for half in range(2):
            du = dus[half]
            dup = (w_ref[half, 2:3, :] * du + w_ref[half, 1:2, :] * pltpu.roll(du, n - 1, 0)
                   + w_ref[half, 0:1, :] * pltpu.roll(du, n - 2, 0))
            dup_ref[half] = dup[lo:hi, :].astype(BF16)
            duc = du[lo:hi, :]
            s0 = jnp.sum(duc * xm2[half][lo:hi, :], axis=0, keepdims=True)
            s1 = jnp.sum(duc * xm1[half][lo:hi, :], axis=0, keepdims=True)
            s2 = jnp.sum(duc * xs[half][lo:hi, :], axis=0, keepdims=True)
            dcw_ref[half] += jnp.where(rid == 0, s0, jnp.where(rid == 1, s1, jnp.where(rid == 2, s2, 0.0)))
            dcb_ref[half] += jnp.sum(duc, axis=0, keepdims=True)

    last = t // HALO - 1
    return pl.pallas_call(
        body, name="conv_gelu_bwd", grid=(nct, nrt),
        in_specs=[pl.BlockSpec((2, tr, ct), lambda j, i: (0, i, j)),
                  pl.BlockSpec((2, HALO, ct), lambda j, i: (0, jnp.maximum(i * hb - 1, 0), j)),
                  pl.BlockSpec((2, HALO, ct), lambda j, i: (0, jnp.minimum((i + 1) * hb, last), j)),
                  pl.BlockSpec((tr, ct), lambda j, i: (i, j)),
                  pl.BlockSpec((HALO, ct), lambda j, i: (jnp.minimum((i + 1) * hb, last), j)),
                  pl.BlockSpec((2, 3, ct), lambda j, i: (0, 0, j)),
                  pl.BlockSpec((2, 1, ct), lambda j, i: (0, 0, j))],
        out_specs=[pl.BlockSpec((2, tr, ct), lambda j, i: (0, i, j)),
                   pl.BlockSpec((2, 8, ct), lambda j, i: (0, 0, j)),
                   pl.BlockSpec((2, 1, ct), lambda j, i: (0, 0, j))],
        out_shape=[jax.ShapeDtypeStruct((2, t, D_FF), BF16), jax.ShapeDtypeStruct((2, 8, D_FF), F32),
                   jax.ShapeDtypeStruct((2, 1, D_FF), F32)],
        compiler_params=_cparams("parallel", "arbitrary"),
    )(up3, up3, up3, dgm, dgm, conv_w3, conv_b3)


def _down_ln2_loss(gm, w_down, xhat1, g1, b1, g2, b2, target, t, tm=256):
    def body(gm_ref, w_ref, xh_ref, g1_ref, b1_ref, g2_ref, b2_ref, tg_ref, dr_ref, loss_ref, dg_ref, db_ref):
        i = pl.program_id(0)

        @pl.when(i == 0)
        def _():
            loss_ref[...] = jnp.zeros_like(loss_ref)
            dg_ref[...] = jnp.zeros_like(dg_ref)
            db_ref[...] = jnp.zeros_like(db_ref)

        h = xh_ref[...] * g1_ref[...] + b1_ref[...]
        xhat, rstd = _ln_stats(ALPHA * h + _dot(gm_ref[...], w_ref[...]))
        err = xhat * g2_ref[...] + b2_ref[...] - tg_ref[...]
        loss_ref[...] += 0.5 * jnp.sum(jnp.sum(err * err, axis=1, keepdims=True), axis=0, keepdims=True) / D_MODEL
        dy = err * (1.0 / D_MODEL)
        dg_ref[...] += jnp.sum(dy * xhat, axis=0, keepdims=True)
        db_ref[...] += jnp.sum(dy, axis=0, keepdims=True)
        dr_ref[...] = _ln_bwd(dy * g2_ref[...], xhat, rstd)

    row = lambda w: pl.BlockSpec((tm, w), lambda i: (i, 0))
    full = lambda shp: pl.BlockSpec(shp, lambda i: (0, 0))
    vec = full((1, D_MODEL))
    return pl.pallas_call(
        body, name="down_ln2_loss", grid=(t // tm,),
        in_specs=[row(D_FF), full((D_FF, D_MODEL)), row(D_MODEL), vec, vec, vec, vec, row(D_MODEL)],
        out_specs=[row(D_MODEL), full((1, 1)), vec, vec],
        out_shape=[jax.ShapeDtypeStruct((t, D_MODEL), F32), jax.ShapeDtypeStruct((1, 1), F32),
                   jax.ShapeDtypeStruct((1, D_MODEL), F32), jax.ShapeDtypeStruct((1, D_MODEL), F32)],
        compiler_params=_cparams("arbitrary"),
    )(gm, w_down, xhat1, g1, b1, g2, b2, target)


def _dh_ln1_bwd(dup3, w_up4, dr2, xhat1, rstd1, g1, t, tm=256):
    ws = 2 * D_FF // 4

    def body(a_ref, w_ref, dr2_ref, xh_ref, rs_ref, g_ref, dr1_ref, dg_ref, db_ref):
        i = pl.program_id(0)

        @pl.when(i == 0)
        def _():
            dg_ref[...] = jnp.zeros_like(dg_ref)
            db_ref[...] = jnp.zeros_like(db_ref)

        dh = ALPHA * dr2_ref[...]
        for s in range(4):
            dh = dh + _dot_nt(a_ref[s // 2, :, (s % 2) * ws:(s % 2 + 1) * ws], w_ref[s])
        xhat = xh_ref[...]
        dg_ref[...] += jnp.sum(dh * xhat, axis=0, keepdims=True)
        db_ref[...] += jnp.sum(dh, axis=0, keepdims=True)
        dr1_ref[...] = _ln_bwd(dh * g_ref[...], xhat, rs_ref[...])

    row = lambda w: pl.BlockSpec((tm, w), lambda i: (i, 0))
    vec = pl.BlockSpec((1, D_MODEL), lambda i: (0, 0))
    return pl.pallas_call(
        body, name="dh_ln1_bwd", grid=(t // tm,),
        in_specs=[pl.BlockSpec((2, tm, D_FF), lambda i: (0, i, 0)),
                  pl.BlockSpec((4, D_MODEL, ws), lambda i: (0, 0, 0)),
                  row(D_MODEL), row(D_MODEL), row(1), vec],
        out_specs=[row(D_MODEL), vec, vec],
        out_shape=[jax.ShapeDtypeStruct((t, D_MODEL), F32), jax.ShapeDtypeStruct((1, D_MODEL), F32),
                   jax.ShapeDtypeStruct((1, D_MODEL), F32)],
        compiler_params=_cparams("arbitrary"),
    )(dup3, w_up4, dr2, xhat1, rstd1, g1)


def _adamw(w, g, m, v, name):
    rows, cols = w.shape
    tr = rows
    for cand in (256, 128, 64, 32, 16, 8):
        if rows % cand == 0 and rows > cand:
            tr = cand
            break
    c1 = 1.0 / (1.0 - ADAM_B1 ** ADAM_STEP)
    c2 = 1.0 / (1.0 - ADAM_B2 ** ADAM_STEP)

    def body(w_ref, g_ref, m_ref, v_ref, d_ref, nm_ref, nv_ref):
        gv = g_ref[...]
        nm = ADAM_B1 * m_ref[...] + (1.0 - ADAM_B1) * gv
        nv = ADAM_B2 * v_ref[...] + (1.0 - ADAM_B2) * (gv * gv)
        d_ref[...] = -ADAM_LR * ((nm * c1) / (jnp.sqrt(nv * c2) + ADAM_EPS) + ADAM_WD * w_ref[...])
        nm_ref[...] = nm
        nv_ref[...] = nv

    spec = pl.BlockSpec((tr, cols), lambda i: (i, 0))
    out = jax.ShapeDtypeStruct((rows, cols), F32)
    return pl.pallas_call(
        body, name=name, grid=(rows // tr,), in_specs=[spec] * 4, out_specs=[spec] * 3, out_shape=[out] * 3,
        compiler_params=_cparams("parallel"),
    )(w, g, m, v)


def _local_step(x, target, w_in_p, late_shards, gate_up_pad, gate_bias, gnorm, ln1_g, ln1_b, conv_w3, conv_b3,
                ln2_g, ln2_b, c_arr, kc_arr):
    t = x.shape[0]
    tq = min(t, 1024)
    s_up, s_out, s_down = late_shards
    sh_up, sh_out, sh_down = LATE_SHAPES
    proj, out_partly = _mm(x, w_in_p, m=t, n=IN_PAD, k=D_MODEL, tm=tq, tn=640, tk=D_MODEL, name="proj",
                           rider=_gather_rider([s_out], [sh_out]))
    sb_o, sb_mass, up_partly = _sb_fwd(proj, t, _gather_rider([s_up], [sh_up]))
    gla_o, oraw, states, w_up4, w_out4, down_partly = _gla_fwd(
        proj, gate_up_pad, gate_bias, gnorm, t,
        _join(_forward_rider([up_partly, out_partly], [sh_up, sh_out]), _gather_rider([s_down], [sh_down])))
    w_out = w_out4.reshape(D_MODEL, D_MODEL)
    xhat1, rstd1, h_bf = _outproj_ln1(sb_o, gla_o, x, w_out, ln1_g, ln1_b, t)
    up3, w_down4 = _mm(h_bf, w_up4, m=t, n=2 * D_FF, k=D_MODEL, tm=tq, tn=W_UP_S, tk=D_MODEL, name="up",
                       b_spec=pl.BlockSpec((None, D_MODEL, W_UP_S), lambda i, j, kk: (j, 0, 0)),
                       o_spec=pl.BlockSpec((None, tq, W_UP_S), lambda i, j, kk: (j // 2, i, j % 2)),
                       out_shape=jax.ShapeDtypeStruct((2, t, D_FF), BF16), out_dtype=BF16,
                       rider=_forward_rider([down_partly], [sh_down]))
    w_down = w_down4.reshape(D_FF, D_MODEL)
    gm = _conv_gelu_fwd(up3, conv_w3, conv_b3, t, tr=tq)
    dr2, loss, dln2_g, dln2_b = _down_ln2_loss(gm, w_down, xhat1, ln1_g, ln1_b, ln2_g, ln2_b, target, t)
    dgm = _mm(dr2, w_down, m=t, n=D_FF, k=D_MODEL, tm=tq, tn=W_UP_S, tk=D_MODEL, tb=True, out_dtype=BF16, name="dgm")
    dw_down = _mm(gm, dr2, m=D_FF, n=D_MODEL, k=t, tm=W_UP_S, tn=D_MODEL, tk=tq, ta=True, name="dw_down")
    dup3, dcw, dcb = _conv_gelu_bwd(up3, dgm, conv_w3, conv_b3, t)
    dr1, dln1_g, dln1_b = _dh_ln1_bwd(dup3, w_up4, dr2, xhat1, rstd1, ln1_g, t)
    dw_up4 = _mm(h_bf, dup3, m=D_MODEL, n=2 * D_FF, k=t, tm=512, tn=W_UP_S, tk=t, ta=True, name="dw_up",
                 b_spec=pl.BlockSpec((None, t, W_UP_S), lambda i, j, kk: (j // 2, kk, j % 2)),
                 o_spec=pl.BlockSpec((None, 512, W_UP_S), lambda i, j, kk: (j, i, 0)),
                 out_shape=jax.ShapeDtypeStruct((4, D_MODEL, W_UP_S), F32))
    dw_out_sb = _mm(sb_o, dr1, m=SB_WIDTH, n=D_MODEL, k=t, tm=512, tn=D_MODEL, tk=tq, ta=True, name="dw_out_sb")
    dw_out_gla = _mm(gla_o, dr1, m=GLA_WIDTH, n=D_MODEL, k=t, tm=512, tn=D_MODEL, tk=tq, ta=True, name="dw_out_gla")
    gs = [dw_up4, jnp.concatenate([dw_out_sb, dw_out_gla], axis=0).reshape(4, W_OUT_S, D_MODEL),
          dw_down.reshape(4, W_DOWN_S, D_MODEL)]
    dcat, *from_sib = _mm(dr1, w_out, m=t, n=D_MODEL, k=D_MODEL, tm=tq, tn=512, tk=D_MODEL, tb=True, name="dcat",
                          rider=_sibling_rider(gs, LATE_SHAPES))
    ps = [_add_sibling(gs[m], from_sib[m], c_arr, LATE_ADD_ROWS[m], "add_sibling_late_%d" % m) for m in range(3)]
    dsq, dsk, dsv, *others = _sb_bwd(proj, dcat, sb_mass, t, _chips_rider(ps))
    late_sums = [_add_chips(ps[m], others[m], kc_arr, LATE_ADD_ROWS[m], "add_chips_late_%d" % m) for m in range(3)]
    dgla, dgup_pad, dgbias, dgnorm = _gla_bwd(proj, dcat, oraw, states, gate_up_pad, gate_bias, gnorm, t)
    dproj = [dsq, dsk, dsv, dgla]
    dw_in_p = _dw_in(x, dproj, t)
    small = dict(
        gate_up=dgup_pad[:GATE_RANK], gate_bias=dgbias, gla_norm_g=dgnorm, ln1_g=dln1_g, ln1_b=dln1_b,
        conv_w=jnp.concatenate([dcw[0, :3], dcw[1, :3]], axis=1), conv_b=jnp.concatenate([dcb[0], dcb[1]], axis=1),
        ln2_g=dln2_g, ln2_b=dln2_b, loss=loss)
    g_in = dw_in_p[None]
    from_sib_in, = _run(_sibling_rider([g_in], [(D_MODEL, IN_PAD)]), "exchange_sibling_w_in")
    half_in = _add_sibling(g_in, from_sib_in, c_arr, ADD_ROWS[0], "add_sibling_w_in")[0]
    p_in = jnp.stack([half_in[:, k * W_IN_S:(k + 1) * W_IN_S] for k in range(4)], axis=0)
    dx, others_in, vecs = _dx(dproj, w_in_p, dr1, t, _chips_rider([p_in], _pack_vec(small, SMALL_GRADS, GRAD_VEC_ROWS)))
    sum_in = _add_chips(p_in, others_in, kc_arr, ADD_ROWS[0], "add_chips_w_in")
    return dx, [sum_in] + late_sums, vecs


W_IN_S, W_UP_S, W_OUT_S, W_DOWN_S = IN_WIDTH // 4, 2 * D_FF // 4, D_MODEL // 4, D_FF // 4
SHARD_SHAPES = ((D_MODEL, W_IN_S), (D_MODEL, W_UP_S), (W_OUT_S, D_MODEL), (W_DOWN_S, D_MODEL))
ADD_ROWS = (256, 256, 128, 176)
LATE_SHAPES, LATE_ADD_ROWS = SHARD_SHAPES[1:], ADD_ROWS[1:]
SMALL_ROWS = 8
VEC_SIZES = (("gate_bias", GLA_KEYS), ("gla_norm_g", LANES), ("ln1_g", D_MODEL), ("ln1_b", D_MODEL),
             ("conv_b", 2 * D_FF), ("ln2_g", D_MODEL), ("ln2_b", D_MODEL))
SMALL_GRADS = VEC_SIZES + (("conv_w", 3 * 2 * D_FF), ("gate_up", GATE_RANK * GLA_KEYS), ("loss", 1))


def _rows(a):
    flat = a.reshape(-1)
    pad = (-flat.shape[0]) % D_MODEL
    if pad:
        flat = jnp.pad(flat, (0, pad))
    return flat.reshape(-1, D_MODEL)


def _pad_rows(a, rows):
    return jnp.pad(a, ((0, rows - a.shape[0]), (0, 0)))


def _pack_vec(d, sizes, rows):
    flat = jnp.concatenate([d[n].reshape(-1) for n, _ in sizes])
    return _pad_rows(_rows(flat), rows)


def _unpack_vec(v, sizes):
    flat = v.reshape(-1)
    out, o = {}, 0
    for n, size in sizes:
        out[n] = flat[o:o + size].reshape(1, size)
        o += size
    return out


VEC_ROWS = 16
GRAD_VEC_ROWS = 32


HBM_SPEC = pl.BlockSpec(memory_space=pltpu.HBM)


def _position():
    x, y, c = lax.axis_index("x"), lax.axis_index("y"), lax.axis_index("c")
    chips = [(1 - x, y), (x, 1 - y), (1 - x, 1 - y)]
    return x, y, c, chips


def _remote(src, dst, send_sems, recv_sems, k, to):
    return pltpu.make_async_remote_copy(src_ref=src, dst_ref=dst, send_sem=send_sems.at[k], recv_sem=recv_sems.at[k],
                                        device_id=to, device_id_type=MESH)


def _gather_ici(in_refs, out_refs, shapes, send_sems, recv_sems, local_sems):
    x, y, c, chips = _position()
    k_me = 2 * x + y
    local, sends, recvs = [], [], []
    for m, (rows, _) in enumerate(shapes):
        h = rows // 2
        local.append(pltpu.make_async_copy(in_refs[m], out_refs[m].at[k_me], local_sems.at[m]))
        for j, (cx, cy) in enumerate(chips):
            sends.append(_remote(in_refs[m].at[pl.ds(c * h, h), :], out_refs[m].at[k_me, pl.ds(c * h, h), :],
                                 send_sems, recv_sems, 3 * m + j, (cx, cy, c)))
            landed = out_refs[m].at[2 * cx + cy, pl.ds(c * h, h), :]
            recvs.append(_remote(landed, landed, send_sems, recv_sems, 3 * m + j, (x, y, c)))
    return local, sends, recvs


def _gather_d2d(src_refs, dst_refs, shapes, send_sems, recv_sems, base):
    x, y, c, chips = _position()
    sends, recvs = [], []
    for m, (rows, _) in enumerate(shapes):
        h = rows // 2
        for j, (cx, cy) in enumerate(chips):
            k = 2 * cx + cy
            sends.append(_remote(src_refs[m].at[k, pl.ds(c * h, h), :], dst_refs[m].at[k, pl.ds(c * h, h), :],
                                 send_sems, recv_sems, base + 3 * m + j, (x, y, 1 - c)))
            landed = dst_refs[m].at[k, pl.ds((1 - c) * h, h), :]
            recvs.append(_remote(landed, landed, send_sems, recv_sems, base + 3 * m + j, (x, y, c)))
    return sends, recvs


def _gather_weights(shards, small, shapes):
    nm = len(shards)
    n_ici = 3 * nm

    def body(*refs):
        in_refs, small_ref = refs[:nm], refs[nm]
        out_refs, osm_ref = refs[nm + 1:2 * nm + 1], refs[2 * nm + 1]
        send_sems, recv_sems, local_sems = refs[2 * nm + 2:]
        x, y, c, chips = _position()
        k_me = 2 * x + y
        local, sends, recvs = _gather_ici(in_refs, out_refs, shapes, send_sems, recv_sems, local_sems)
        local.append(pltpu.make_async_copy(small_ref, osm_ref.at[k_me], local_sems.at[nm]))
        for j, (cx, cy) in enumerate(chips):
            sends.append(_remote(small_ref, osm_ref.at[k_me], send_sems, recv_sems, n_ici + j, (cx, cy, c)))
        for cp in local + sends:
            cp.start()
        for cp in recvs:
            cp.wait_recv()
        fsends, frecvs = _gather_d2d(out_refs, out_refs, shapes, send_sems, recv_sems, n_ici + 3)
        for cp in fsends:
            cp.start()
        for j, (cx, cy) in enumerate(chips):
            k = 2 * cx + cy
            frecvs.append(_remote(osm_ref.at[k], osm_ref.at[k], send_sems, recv_sems, n_ici + j, (x, y, c)))
        for cp in frecvs:
            cp.wait_recv()
        for cp in sends + fsends:
            cp.wait_send()
        for cp in local:
            cp.wait()

    n_sems = 2 * n_ici + 3
    return pl.pallas_call(
        body, name="gather_weights", in_specs=[HBM_SPEC] * (nm + 1), out_specs=[HBM_SPEC] * (nm + 1),
        out_shape=[jax.ShapeDtypeStruct((4,) + s, BF16) for s in shapes]
        + [jax.ShapeDtypeStruct((4, SMALL_ROWS, D_MODEL), F32)],
        scratch_shapes=[pltpu.SemaphoreType.DMA((n_sems,)), pltpu.SemaphoreType.DMA((n_sems,)),
                        pltpu.SemaphoreType.DMA((nm + 1,))],
    )(*shards, small)


def _gather_rider(shards, shapes):
    n = len(shards)
    return _Rider(shards, [jax.ShapeDtypeStruct((4,) + s, BF16) for s in shapes], (3 * n, 3 * n, n),
                  lambda ins, outs, sems: _gather_ici(ins, outs, shapes, *sems))


def _forward_rider(gathered, shapes):
    n = len(gathered)
    return _Rider(gathered, [jax.ShapeDtypeStruct(a.shape, a.dtype) for a in gathered], (3 * n, 3 * n),
                  lambda ins, outs, sems: ([],) + _gather_d2d(ins, outs, shapes, sems[0], sems[1], 0),
                  aliases=[(m, m) for m in range(n)])


def _sibling_rider(gs, shapes):
    def copies(ins, outs, sems):
        x, y, c, _ = _position()
        both = []
        for m, (rows, _) in enumerate(shapes):
            h = rows // 2
            for k in range(gs[m].shape[0]):
                both.append(_remote(ins[m].at[k, pl.ds((1 - c) * h, h), :], outs[m].at[k], sems[0], sems[1],
                                    4 * m + k, (x, y, 1 - c)))
        return [], both, both

    return _Rider(gs, [jax.ShapeDtypeStruct((g.shape[0], r // 2, cl), F32) for g, (r, cl) in zip(gs, shapes)],
                  (4 * len(gs), 4 * len(gs)), copies)


def _add_sibling(g, r, c_arr, tr, name):
    nblk, rows, cols = g.shape
    nb = rows // 2 // tr

    def body(c_ref, g_ref, r_ref, o_ref):
        o_ref[...] = (g_ref[...] + r_ref[...]).astype(BF16)

    spec = pl.BlockSpec((None, tr, cols), lambda k, i, c: (k, i, 0))
    return pl.pallas_call(
        body, name=name,
        grid_spec=pltpu.PrefetchScalarGridSpec(
            num_scalar_prefetch=1, grid=(nblk, nb),
            in_specs=[pl.BlockSpec((None, tr, cols), lambda k, i, c: (k, c[0] * nb + i, 0)), spec], out_specs=spec),
        out_shape=jax.ShapeDtypeStruct((nblk, rows // 2, cols), BF16), compiler_params=_cparams("parallel", "parallel"),
    )(c_arr, g, r)


def _reduce_ici(p_refs, r_refs, send_sems, recv_sems):
    x, y, c, chips = _position()
    sends, recvs = [], []
    for m in range(len(p_refs)):
        for j, (cx, cy) in enumerate(chips):
            sends.append(_remote(p_refs[m].at[2 * cx + cy], r_refs[m].at[j], send_sems, recv_sems, 3 * m + j, (cx, cy, c)))
            recvs.append(_remote(r_refs[m].at[j], r_refs[m].at[j], send_sems, recv_sems, 3 * m + j, (x, y, c)))
    return sends, recvs


def _chips_rider(ps, vec=None):
    nm = len(ps)
    n_ici = 3 * nm

    def copies(ins, outs, sems):
        sends, recvs = _reduce_ici(ins[:nm], outs[:nm], sems[0], sems[1])
        if vec is None:
            return [], sends, recvs
        x, y, c, _ = _position()
        my_id = 4 * x + 2 * y + c
        vec_ref, vrecv_ref = ins[nm], outs[nm]
        local = [pltpu.make_async_copy(vec_ref, vrecv_ref.at[my_id], sems[2].at[0])]
        for r in range(1, 8):
            peer = (1 - x if r & 4 else x, 1 - y if r & 2 else y, 1 - c if r & 1 else c)
            sends.append(_remote(vec_ref, vrecv_ref.at[my_id], sems[0], sems[1], n_ici + r - 1, peer))
            recvs.append(_remote(vec_ref, vrecv_ref.at[0], sems[0], sems[1], n_ici + r - 1, (x, y, c)))
        return local, sends, recvs

    out_shapes = [jax.ShapeDtypeStruct((3,) + p.shape[1:], p.dtype) for p in ps]
    if vec is None:
        return _Rider(ps, out_shapes, (n_ici, n_ici), copies)
    return _Rider(list(ps) + [vec], out_shapes + [jax.ShapeDtypeStruct((8, GRAD_VEC_ROWS, D_MODEL), F32)],
                  (n_ici + 7, n_ici + 7, 1), copies)


def _add_chips(p, r, kc_arr, tr, name):
    _, h, cols = p.shape
    nb = h // tr

    def body(kc_ref, p_ref, r0_ref, r1_ref, r2_ref, o_ref):
        o_ref[...] = ((p_ref[...].astype(F32) + r0_ref[...].astype(F32)) + r1_ref[...].astype(F32)) + r2_ref[...].astype(F32)

    rspec = lambda j: pl.BlockSpec((None, tr, cols), lambda i, kc: (j, i, 0))
    return pl.pallas_call(
        body, name=name,
        grid_spec=pltpu.PrefetchScalarGridSpec(
            num_scalar_prefetch=1, grid=(nb,),
            in_specs=[pl.BlockSpec((None, tr, cols), lambda i, kc: (kc[0], i, 0)), rspec(0), rspec(1), rspec(2)],
            out_specs=pl.BlockSpec((tr, cols), lambda i, kc: (kc[1] * nb + i, 0))),
        out_shape=jax.ShapeDtypeStruct((2 * h, cols), F32), compiler_params=_cparams("parallel"),
    )(kc_arr, p, r, r, r)


def _reunite_sibling(fs, shapes):
    n_chunks = 2
    nm = len(fs)

    def body(*refs):
        in_refs, out_refs = refs[:nm], refs[nm:2 * nm]
        send_sems, recv_sems = refs[2 * nm:]
        x, y, c, _ = _position()
        sends, recvs = [], []
        for m in range(nm):
            ch = shapes[m][0] // 2 // n_chunks
            for q in range(n_chunks):
                mine = pl.ds((c * n_chunks + q) * ch, ch)
                theirs = pl.ds(((1 - c) * n_chunks + q) * ch, ch)
                s = n_chunks * m + q
                sends.append(_remote(in_refs[m].at[mine, :], out_refs[m].at[mine, :], send_sems, recv_sems, s, (x, y, 1 - c)))
                recvs.append(_remote(in_refs[m].at[theirs, :], out_refs[m].at[theirs, :], send_sems, recv_sems, s, (x, y, c)))
        for cp in sends:
            cp.start()
        for cp in recvs:
            cp.wait_recv()
        for cp in sends:
            cp.wait_send()

    n_sems = n_chunks * nm
    return pl.pallas_call(
        body, name="reunite_sibling", in_specs=[HBM_SPEC] * nm, out_specs=[HBM_SPEC] * nm,
        out_shape=[jax.ShapeDtypeStruct(s, F32) for s in shapes],
        input_output_aliases={m: m for m in range(nm)},
        scratch_shapes=[pltpu.SemaphoreType.DMA((n_sems,)), pltpu.SemaphoreType.DMA((n_sems,))],
    )(*fs)


def _sum_vecs(v):
    def body(v_ref, o_ref):
        acc = v_ref[0]
        for d in range(1, 8):
            acc = acc + v_ref[d]
        o_ref[...] = acc

    return pl.pallas_call(body, name="sum_vecs", out_shape=jax.ShapeDtypeStruct(v.shape[1:], F32))(v)


def kernel(x, w_in, gate_up, gate_bias, gla_norm_g, w_out, ln1_g, ln1_b, w_up, conv_w, conv_b, w_down, ln2_g, ln2_b, loss_target, m_w_in, m_gate_up, m_gate_bias, m_gla_norm_g, m_w_out, m_ln1_g, m_ln1_b, m_w_up, m_conv_w, m_conv_b, m_w_down, m_ln2_g, m_ln2_b, v_w_in, v_gate_up, v_gate_bias, v_gla_norm_g, v_w_out, v_ln1_g, v_ln1_b, v_w_up, v_conv_w, v_conv_b, v_w_down, v_ln2_g, v_ln2_b):
    xi, yi, ci = lax.axis_index("x"), lax.axis_index("y"), lax.axis_index("c")
    k_me = 2 * xi + yi
    c_arr = ci.astype(jnp.int32).reshape(1)
    kc_arr = jnp.stack([k_me, ci]).astype(jnp.int32)
    small = _pad_rows(jnp.concatenate([_rows(conv_w[0]), _rows(gate_up[0])], axis=0), SMALL_ROWS)
    w_in4, gsmall = _gather_weights([w_in[0].astype(BF16)], small, SHARD_SHAPES[:1])
    late_shards = [w_up[0].astype(BF16), w_out[0].astype(BF16), w_down[0].astype(BF16)]
    w_in_p = jnp.pad(jnp.concatenate([w_in4[k] for k in range(4)], axis=1), ((0, 0), (0, IN_PAD - IN_WIDTH)))
    conv_w_f = jnp.concatenate([gsmall[k, :5].reshape(-1)[:3 * W_UP_S].reshape(3, W_UP_S) for k in range(4)], axis=1)
    gate_up_f = jnp.concatenate([gsmall[k, 5].reshape(GATE_RANK, GLA_KEYS // 4) for k in range(4)], axis=1)
    conv_w3 = jnp.transpose(conv_w_f.reshape(3, 2, D_FF), (1, 0, 2))
    conv_b3 = conv_b.reshape(2, 1, D_FF)
    gate_up_pad = jnp.pad(gate_up_f, ((0, LANES - GATE_RANK), (0, 0)))

    dx, sums, vecs = _local_step(
        x[0], loss_target[0], w_in_p, late_shards, gate_up_pad, gate_bias, gla_norm_g, ln1_g, ln1_b, conv_w3, conv_b3,
        ln2_g, ln2_b, c_arr, kc_arr)
    g_w_in, g_w_up, g_w_out, g_w_down = _reunite_sibling(sums, SHARD_SHAPES)
    gsmall_sum = _unpack_vec(_sum_vecs(vecs), SMALL_GRADS)
    g_conv_w = lax.dynamic_slice_in_dim(gsmall_sum["conv_w"].reshape(3, 2 * D_FF), k_me * W_UP_S, W_UP_S, axis=1)
    g_gate_up = lax.dynamic_slice_in_dim(gsmall_sum["gate_up"].reshape(GATE_RANK, GLA_KEYS), k_me * (GLA_KEYS // 4),
                                         GLA_KEYS // 4, axis=1)
    gv = gsmall_sum
    loss = gv["loss"][0, 0]
    gvec = _pack_vec(gv, VEC_SIZES, VEC_ROWS)

    grads = dict(w_in=g_w_in[None], gate_up=g_gate_up[None], gate_bias=gv["gate_bias"], gla_norm_g=gv["gla_norm_g"],
                 w_out=g_w_out[None], ln1_g=gv["ln1_g"], ln1_b=gv["ln1_b"], w_up=g_w_up[None], conv_w=g_conv_w[None],
                 conv_b=gv["conv_b"], w_down=g_w_down[None], ln2_g=gv["ln2_g"], ln2_b=gv["ln2_b"])
    weights = dict(w_in=w_in, gate_up=gate_up, gate_bias=gate_bias, gla_norm_g=gla_norm_g, w_out=w_out, ln1_g=ln1_g,
                   ln1_b=ln1_b, w_up=w_up, conv_w=conv_w, conv_b=conv_b, w_down=w_down, ln2_g=ln2_g, ln2_b=ln2_b)
    ms = dict(w_in=m_w_in, gate_up=m_gate_up, gate_bias=m_gate_bias, gla_norm_g=m_gla_norm_g, w_out=m_w_out, ln1_g=m_ln1_g,
              ln1_b=m_ln1_b, w_up=m_w_up, conv_w=m_conv_w, conv_b=m_conv_b, w_down=m_w_down, ln2_g=m_ln2_g, ln2_b=m_ln2_b)
    vs = dict(w_in=v_w_in, gate_up=v_gate_up, gate_bias=v_gate_bias, gla_norm_g=v_gla_norm_g, w_out=v_w_out, ln1_g=v_ln1_g,
              ln1_b=v_ln1_b, w_up=v_w_up, conv_w=v_conv_w, conv_b=v_conv_b, w_down=v_w_down, ln2_g=v_ln2_g, ln2_b=v_ln2_b)
    names = ["w_in", "gate_up", "gate_bias", "gla_norm_g", "w_out", "ln1_g", "ln1_b", "w_up", "conv_w", "conv_b", "w_down",
             "ln2_g", "ln2_b"]
    delta, new_m, new_v = {}, {}, {}
    for n in ("w_in", "gate_up", "w_out", "w_up", "conv_w", "w_down"):
        tr_ = jnp.transpose if n == "w_in" else (lambda a: a)
        d, nm, nv = _adamw(tr_(weights[n][0]), tr_(grads[n][0]), tr_(ms[n][0]), tr_(vs[n][0]), "adamw_" + n)
        delta[n], new_m[n], new_v[n] = tr_(d)[None], tr_(nm)[None], tr_(nv)[None]
    d, nm, nv = _adamw(_pack_vec(weights, VEC_SIZES, VEC_ROWS), gvec, _pack_vec(ms, VEC_SIZES, VEC_ROWS),
                       _pack_vec(vs, VEC_SIZES, VEC_ROWS), "adamw_vectors")
    for dst, src in ((delta, d), (new_m, nm), (new_v, nv)):
        dst.update(_unpack_vec(src, VEC_SIZES))
    return (loss, dx[None], *[grads[n] for n in names], *[delta[n] for n in names], *[new_m[n] for n in names],
            *[new_v[n] for n in names])
```

```python
import functools
import math

import jax
import jax.numpy as jnp
from jax import lax
from jax.experimental import pallas as pl
from jax.experimental.pallas import tpu as pltpu

F32 = jnp.float32
BF16 = jnp.bfloat16

D_MODEL = 1024
SB_WIDTH = 512
GLA_KEYS = 256
GLA_WIDTH = 512
GATE_RANK = 16
IN_WIDTH = 3088
IN_PAD = 3200
D_FF = 2816
CHUNK = 64
LN_EPS = 1e-5
RMS_EPS = 1e-6
ALPHA = 2.0 ** 0.25
GLA_TAU = 16.0
SB_SCALE = 0.125
GLA_SCALE = 0.125
LANES = 128
SB_BLK = 256
SB_CUT = -100.0
GLA_ROWS = 256
VMEM_LIMIT = 56 * 1024 * 1024

ADAM_LR, ADAM_B1, ADAM_B2, ADAM_EPS, ADAM_WD, ADAM_STEP = 0.001, 0.9, 0.999, 1e-08, 0.01, 10

MESH = pl.DeviceIdType.MESH


def _cparams(*sem):
    return pltpu.CompilerParams(dimension_semantics=sem, vmem_limit_bytes=VMEM_LIMIT)


def _dot(a, b):
    return jnp.dot(a, b, preferred_element_type=F32)


def _dot_nt(a, b):
    return lax.dot_general(a, b, (((1,), (1,)), ((), ())), preferred_element_type=F32)


def _dot_tn(a, b):
    return lax.dot_general(a, b, (((0,), (0,)), ((), ())), preferred_element_type=F32)


def _split3(x):
    hi = x.astype(BF16)
    r = x - hi.astype(F32)
    mid = r.astype(BF16)
    lo = (r - mid.astype(F32)).astype(BF16)
    return hi, mid, lo


def _softplus(z):
    return jnp.maximum(z, 0.0) + jnp.log(1.0 + jnp.exp(-jnp.abs(z)))


def _sigmoid(z):
    return 1.0 / (1.0 + jnp.exp(-z))


def _mm(a, b, *, m, n, k, tm, tn, tk, ta=False, tb=False, a_spec=None, b_spec=None, o_spec=None,
        out_shape=None, out_dtype=F32, add=None, add_scale=1.0, rider=None, name):
    nk = k // tk
    dn = (((0 if ta else 1,), (1 if tb else 0,)), ((), ()))
    n_in = len(rider.inputs) if rider else 0
    n_out = len(rider.out_shapes) if rider else 0
    n_add = int(add is not None)
    steps = (m // tm, n // tn, nk)

    def body(*refs):
        a_ref, b_ref = refs[:2]
        add_ref = refs[2] if add is not None else None
        rin = refs[2 + n_add:2 + n_add + n_in]
        o_ref = refs[2 + n_add + n_in]
        rout = refs[3 + n_add + n_in:3 + n_add + n_in + n_out]
        scratch = refs[3 + n_add + n_in + n_out:]
        if rider:
            sems = scratch[len(scratch) - len(rider.sems):]
            ids = [pl.program_id(d) for d in range(3)]
            _ride(rider, rin, rout, sems, (ids[0] == 0) & (ids[1] == 0) & (ids[2] == 0),
                  (ids[0] == steps[0] - 1) & (ids[1] == steps[1] - 1) & (ids[2] == steps[2] - 1))

        part = lax.dot_general(a_ref[...].astype(BF16), b_ref[...].astype(BF16), dn, preferred_element_type=F32)

        def finish(r):
            if add is not None:
                r = r + add_scale * add_ref[...]
            o_ref[...] = r.astype(out_dtype)

        if nk == 1:
            finish(part)
            return
        acc_ref = scratch[0]
        kk = pl.program_id(2)

        @pl.when(kk == 0)
        def _():
            acc_ref[...] = part

        @pl.when((kk > 0) & (kk < nk - 1))
        def _():
            acc_ref[...] += part

        @pl.when(kk == nk - 1)
        def _():
            finish(acc_ref[...] + part)

    if a_spec is None:
        a_spec = pl.BlockSpec((tk, tm), lambda i, j, kk: (kk, i)) if ta else pl.BlockSpec((tm, tk), lambda i, j, kk: (i, kk))
    if b_spec is None:
        b_spec = pl.BlockSpec((tn, tk), lambda i, j, kk: (j, kk)) if tb else pl.BlockSpec((tk, tn), lambda i, j, kk: (kk, j))
    if o_spec is None:
        o_spec = pl.BlockSpec((tm, tn), lambda i, j, kk: (i, j))
    if out_shape is None:
        out_shape = jax.ShapeDtypeStruct((m, n), out_dtype)
    in_specs = [a_spec, b_spec]
    args = [a, b]
    if add is not None:
        in_specs.append(pl.BlockSpec((tm, tn), lambda i, j, kk: (i, j)))
        args.append(add)
    scratch = [pltpu.VMEM((tm, tn), F32)] if nk > 1 else []
    if not rider:
        return pl.pallas_call(
            body, name=name, grid=steps, in_specs=in_specs, out_specs=o_spec, out_shape=out_shape,
            scratch_shapes=scratch, compiler_params=_cparams("parallel", "parallel", "arbitrary"),
        )(*args)
    return pl.pallas_call(
        body, name=name, grid=steps, in_specs=in_specs + [HBM_SPEC] * n_in, out_specs=[o_spec] + [HBM_SPEC] * n_out,
        out_shape=[out_shape] + list(rider.out_shapes),
        input_output_aliases={len(args) + i: 1 + o for i, o in rider.aliases},
        scratch_shapes=scratch + [pltpu.SemaphoreType.DMA((s,)) for s in rider.sems],
        compiler_params=_cparams("arbitrary", "arbitrary", "arbitrary"),
    )(*args, *rider.inputs)


def _col_offsets(pieces):
    offs, o = [], 0
    for a in pieces:
        offs.append(o)
        o += a.shape[1]
    return offs


def _dx(pieces, w_in_p, dr1, t, rider, tm=512):
    offs = _col_offsets(pieces)
    npc = len(pieces)
    n_in, n_out = len(rider.inputs), len(rider.out_shapes)
    steps = t // tm

    def body(*refs):
        p_refs, w_ref, add_ref = refs[:npc], refs[npc], refs[npc + 1]
        rin = refs[npc + 2:npc + 2 + n_in]
        o_ref = refs[npc + 2 + n_in]
        rout = refs[npc + 3 + n_in:npc + 3 + n_in + n_out]
        i = pl.program_id(0)
        _ride(rider, rin, rout, refs[npc + 3 + n_in + n_out:], i == 0, i == steps - 1)
        acc = ALPHA * add_ref[...]
        for p_ref, off in zip(p_refs, offs):
            acc = acc + _dot_nt(p_ref[...].astype(BF16), w_ref[:, off:off + p_ref.shape[1]])
        o_ref[...] = acc

    row = lambda w: pl.BlockSpec((tm, w), lambda i: (i, 0))
    return pl.pallas_call(
        body, name="dx", grid=(steps,),
        in_specs=[row(a.shape[1]) for a in pieces] + [pl.BlockSpec(w_in_p.shape, lambda i: (0, 0)), row(D_MODEL)]
        + [HBM_SPEC] * n_in,
        out_specs=[row(D_MODEL)] + [HBM_SPEC] * n_out,
        out_shape=[jax.ShapeDtypeStruct((t, D_MODEL), F32)] + rider.out_shapes,
        scratch_shapes=[pltpu.SemaphoreType.DMA((s,)) for s in rider.sems],
        compiler_params=_cparams("arbitrary"),
    )(*pieces, w_in_p, dr1, *rider.inputs)


def _dw_in(x, pieces, t, tm=512, tk=512):
    offs = _col_offsets(pieces)
    npc = len(pieces)
    n = offs[-1] + pieces[-1].shape[1]
    nk = t // tk

    def body(*refs):
        x_ref, p_refs, o_ref, acc_ref = refs[0], refs[1:1 + npc], refs[1 + npc], refs[2 + npc]
        kk = pl.program_id(1)
        xb = x_ref[...].astype(BF16)
        for p_ref, off in zip(p_refs, offs):
            cols = slice(off, off + p_ref.shape[1])
            part = _dot_tn(xb, p_ref[...].astype(BF16))
            if nk == 1:
                o_ref[:, cols] = part
                continue

            @pl.when(kk == 0)
            def _():
                acc_ref[:, cols] = part

            @pl.when((kk > 0) & (kk < nk - 1))
            def _():
                acc_ref[:, cols] += part

            @pl.when(kk == nk - 1)
            def _():
                o_ref[:, cols] = acc_ref[:, cols] + part

    return pl.pallas_call(
        body, name="dw_in", grid=(D_MODEL // tm, nk),
        in_specs=[pl.BlockSpec((tk, tm), lambda i, kk: (kk, i))]
        + [pl.BlockSpec((tk, a.shape[1]), lambda i, kk: (kk, 0)) for a in pieces],
        out_specs=pl.BlockSpec((tm, n), lambda i, kk: (i, 0)),
        out_shape=jax.ShapeDtypeStruct((D_MODEL, n), F32),
        scratch_shapes=[pltpu.VMEM((tm, n), F32)],
        compiler_params=_cparams("parallel", "arbitrary"),
    )(x, *pieces)


class _Rider:
    def __init__(self, inputs, out_shapes, sems, copies, aliases=()):
        self.inputs, self.out_shapes, self.sems, self.copies = list(inputs), list(out_shapes), tuple(sems), copies
        self.aliases = tuple(aliases)


def _join(a, b):
    na_in, na_out, na_sems = len(a.inputs), len(a.out_shapes), len(a.sems)

    def copies(ins, outs, sems):
        first = a.copies(ins[:na_in], outs[:na_out], sems[:na_sems])
        second = b.copies(ins[na_in:], outs[na_out:], sems[na_sems:])
        return tuple(u + v for u, v in zip(first, second))

    return _Rider(a.inputs + b.inputs, a.out_shapes + b.out_shapes, a.sems + b.sems, copies,
                  a.aliases + tuple((i + na_in, o + na_out) for i, o in b.aliases))


def _ride(rider, in_refs, out_refs, sems, first, last):
    @pl.when(first)
    def _():
        local, sends, _ = rider.copies(in_refs, out_refs, sems)
        for cp in local + sends:
            cp.start()

    @pl.when(last)
    def _():
        local, sends, recvs = rider.copies(in_refs, out_refs, sems)
        for cp in recvs:
            cp.wait_recv()
        for cp in sends:
            cp.wait_send()
        for cp in local:
            cp.wait()


def _run(rider, name):
    n_in, n_out = len(rider.inputs), len(rider.out_shapes)

    def body(*refs):
        local, sends, recvs = rider.copies(refs[:n_in], refs[n_in:n_in + n_out], refs[n_in + n_out:])
        for cp in local + sends:
            cp.start()
        for cp in recvs:
            cp.wait_recv()
        for cp in sends:
            cp.wait_send()
        for cp in local:
            cp.wait()

    return pl.pallas_call(
        body, name=name, in_specs=[HBM_SPEC] * n_in, out_specs=[HBM_SPEC] * n_out, out_shape=list(rider.out_shapes),
        scratch_shapes=[pltpu.SemaphoreType.DMA((s,)) for s in rider.sems],
    )(*rider.inputs)


def _sb_tile(qh, kj, diag, strict, u_strict, r_in):
    z = _dot_nt(qh, kj)
    sp = _softplus(z)
    l1m = -sp
    lsz = z - sp
    if diag:
        l1m = jnp.where(strict, l1m, 0.0)
    cs = _dot(l1m.astype(BF16), u_strict) + r_in
    w = jnp.exp(lsz + cs)
    if diag:
        w = jnp.where(strict, w, 0.0)
    return l1m, lsz, w


def _sb_consts():
    row = lax.broadcasted_iota(jnp.int32, (SB_BLK, SB_BLK), 0)
    col = lax.broadcasted_iota(jnp.int32, (SB_BLK, SB_BLK), 1)
    strict = col < row
    u_strict = (row > col).astype(BF16)
    u_pre = (row < col).astype(BF16)
    lane = lax.broadcasted_iota(jnp.int32, (1, LANES), 1)
    return strict, u_strict, u_pre, lane


def _sb_fwd(proj, t, rider):
    nq = t // SB_BLK
    n_in, n_out = len(rider.inputs), len(rider.out_shapes)

    def body(q_ref, k_ref, v_ref, *rest):
        rin, o_ref, sv_ref, rout = rest[:n_in], rest[n_in], rest[n_in + 1], rest[n_in + 2:n_in + 2 + n_out]
        p = pl.program_id(0)
        i = pl.program_id(1)
        _ride(rider, rin, rout, rest[n_in + 2 + n_out:], (p == 0) & (i == 0), (p == 3) & (i == nq - 1))

        strict, u_strict, _, lane = _sb_consts()
        qf = q_ref[...] * SB_SCALE
        hms = [(lane // 64) == hh for hh in range(2)]
        qhs = [jnp.where(hm, qf, 0.0).astype(BF16) for hm in hms]

        def step(j, r0, r1, a, sv, diag, keep=None):
            rows = pl.ds(pl.multiple_of(j * SB_BLK, SB_BLK), SB_BLK)
            kj = k_ref[rows, :].astype(BF16)
            vf = v_ref[rows, :]
            rs = []
            for hh, r in enumerate((r0, r1)):
                l1m, _, w = _sb_tile(qhs[hh], kj, diag, strict, u_strict, r)
                pv = _dot(w.astype(BF16), jnp.where(hms[hh], vf, 0.0).astype(BF16))
                mass = jnp.sum(l1m, axis=1, keepdims=True)
                if keep is not None:
                    pv, mass = jnp.where(keep, pv, 0.0), jnp.where(keep, mass, 0.0)
                a = a + pv
                rs.append(r + mass)
                sv = jnp.where(lane == hh * 64 + (i - j + 1), rs[hh], sv)
            return rs[0], rs[1], a, sv

        zero = jnp.zeros((SB_BLK, 1), F32)
        acc0 = jnp.zeros((SB_BLK, LANES), F32)
        r0, r1, acc, sv = step(i, zero, zero, acc0, acc0, True)
        r0, r1, acc, sv = step(jnp.maximum(i - 1, 0), r0, r1, acc, sv, False, keep=i > 0)
        _, _, _, acc, sv = lax.while_loop(
            lambda c: (c[0] >= 0) & (jnp.maximum(jnp.max(c[1]), jnp.max(c[2])) > SB_CUT),
            lambda c: (c[0] - 1,) + step(c[0], c[1], c[2], c[3], c[4], False),
            (i - 2, r0, r1, acc, sv))
        o_ref[...] = acc.astype(BF16)
        sv_ref[...] = sv

    return pl.pallas_call(
        body, name="sb_fwd", grid=(4, nq),
        in_specs=[pl.BlockSpec((SB_BLK, LANES), lambda p, i: (i, p)),
                  pl.BlockSpec((t, LANES), lambda p, i: (0, 4 + p)),
                  pl.BlockSpec((t, LANES), lambda p, i: (0, 8 + p))] + [HBM_SPEC] * n_in,
        out_specs=[pl.BlockSpec((SB_BLK, LANES), lambda p, i: (i, p))] * 2 + [HBM_SPEC] * n_out,
        out_shape=[jax.ShapeDtypeStruct((t, SB_WIDTH), BF16), jax.ShapeDtypeStruct((t, SB_WIDTH), F32)] + rider.out_shapes,
        scratch_shapes=[pltpu.SemaphoreType.DMA((s,)) for s in rider.sems],
        compiler_params=_cparams("arbitrary", "arbitrary"),
    )(proj, proj, proj, *rider.inputs)


def _sb_bwd(proj, dcat, mass, t, rider):
    nq = t // SB_BLK
    n_in, n_out = len(rider.inputs), len(rider.out_shapes)

    def body(q_ref, k_ref, v_ref, do_ref, sv_ref, *rest):
        rin = rest[:n_in]
        dq_ref, dk_ref, dv_ref = rest[n_in:n_in + 3]
        rout = rest[n_in + 3:n_in + 3 + n_out]
        p = pl.program_id(0)
        i = pl.program_id(1)
        _ride(rider, rin, rout, rest[n_in + 3 + n_out:], (p == 0) & (i == 0), (p == 3) & (i == nq - 1))

        @pl.when(i == 0)
        def _():
            dk_ref[...] = jnp.zeros_like(dk_ref)
            dv_ref[...] = jnp.zeros_like(dv_ref)

        strict, u_strict, u_pre, lane = _sb_consts()
        qf = q_ref[...] * SB_SCALE
        dof = do_ref[...]
        hms = [(lane // 64) == hh for hh in range(2)]
        qhs = [jnp.where(hm, qf, 0.0).astype(BF16) for hm in hms]
        dohs = [jnp.where(hm, dof, 0.0).astype(BF16) for hm in hms]

        sv = sv_ref[...]
        zero = jnp.zeros((SB_BLK, 1), F32)

        def mass_right(hh, d):
            return jnp.sum(jnp.where(lane == hh * 64 + d, sv, 0.0), axis=1, keepdims=True)

        dstop = lax.while_loop(
            lambda d: (i - d >= 0) & (jnp.maximum(jnp.max(mass_right(0, d)), jnp.max(mass_right(1, d))) > SB_CUT),
            lambda d: d + 1, 1)
        jstop = i - dstop

        def step(j, carry, diag, keep=None):
            pre_g0, pre_g1, dqa = carry
            rows = pl.ds(pl.multiple_of(j * SB_BLK, SB_BLK), SB_BLK)
            kf = k_ref[rows, :]
            kj = kf.astype(BF16)
            vj = v_ref[rows, :].astype(BF16)
            dv = jnp.zeros((SB_BLK, LANES), F32)
            dk = jnp.zeros((SB_BLK, LANES), F32)
            dqj = jnp.zeros((SB_BLK, LANES), F32)
            pre = []
            for hh, pre_g in enumerate((pre_g0, pre_g1)):
                _, lsz, w = _sb_tile(qhs[hh], kj, diag, strict, u_strict, zero if diag else mass_right(hh, i - j))
                g = w * _dot_nt(dohs[hh], vj)
                gpre = _dot(g.astype(BF16), u_pre) + pre_g
                sig = jnp.exp(lsz)
                dz = g * (1.0 - sig) - gpre * sig
                if diag:
                    dz = jnp.where(strict, dz, 0.0)
                dzb = dz.astype(BF16)
                dv = dv + _dot_tn(w.astype(BF16), dohs[hh])
                dk = dk + _dot_tn(dzb, qhs[hh])
                dqj = dqj + _dot(dzb, jnp.where(hms[hh], kf, 0.0).astype(BF16))
                gsum = jnp.sum(g, axis=1, keepdims=True)
                pre.append(pre_g + (gsum if keep is None else jnp.where(keep, gsum, 0.0)))
            if keep is not None:
                dv, dk, dqj = jnp.where(keep, dv, 0.0), jnp.where(keep, dk, 0.0), jnp.where(keep, dqj, 0.0)
            dv_ref[rows, :] += dv
            dk_ref[rows, :] += dk
            return pre[0], pre[1], dqa + dqj

        carry = lax.fori_loop(jstop + 1, i - 1, lambda j, c: step(j, c, False), (zero, zero, jnp.zeros((SB_BLK, LANES), F32)))
        carry = step(jnp.maximum(i - 1, 0), carry, False, keep=i > 0)
        _, _, dq = step(i, carry, True)
        dq_ref[...] = (dq * SB_SCALE).astype(BF16)

    return pl.pallas_call(
        body, name="sb_bwd", grid=(4, nq),
        in_specs=[pl.BlockSpec((SB_BLK, LANES), lambda p, i: (i, p)),
                  pl.BlockSpec((t, LANES), lambda p, i: (0, 4 + p)),
                  pl.BlockSpec((t, LANES), lambda p, i: (0, 8 + p)),
                  pl.BlockSpec((SB_BLK, LANES), lambda p, i: (i, p)),
                  pl.BlockSpec((SB_BLK, LANES), lambda p, i: (i, p))] + [HBM_SPEC] * n_in,
        out_specs=[pl.BlockSpec((SB_BLK, LANES), lambda p, i: (i, p)),
                   pl.BlockSpec((t, LANES), lambda p, i: (0, p)),
                   pl.BlockSpec((t, LANES), lambda p, i: (0, p))] + [HBM_SPEC] * n_out,
        out_shape=[jax.ShapeDtypeStruct((t, SB_WIDTH), BF16)] + [jax.ShapeDtypeStruct((t, SB_WIDTH), F32)] * 2
        + rider.out_shapes,
        scratch_shapes=[pltpu.SemaphoreType.DMA((s,)) for s in rider.sems],
        compiler_params=_cparams("arbitrary", "arbitrary"),
    )(proj, proj, proj, dcat, mass, *rider.inputs)


def _gla_consts():
    r = lax.broadcasted_iota(jnp.int32, (GLA_ROWS, GLA_ROWS), 0)
    c = lax.broadcasted_iota(jnp.int32, (GLA_ROWS, GLA_ROWS), 1)
    same = (r // CHUNK) == (c // CHUNK)
    causal = same & (c <= r)
    upto_mid = c % CHUNK <= CHUNK // 2 - 1
    fwd_stack = jnp.concatenate([causal, same & upto_mid, same], axis=0).astype(BF16)
    bwd_stack = jnp.concatenate([same & (c >= r), same & (r % CHUNK <= CHUNK // 2 - 1), same], axis=1).astype(BF16)
    rowid = lax.broadcasted_iota(jnp.int32, (GLA_ROWS, 1), 0)
    lane = lax.broadcasted_iota(jnp.int32, (1, LANES), 1)
    sr = lax.broadcasted_iota(jnp.int32, (LANES, 2 * LANES), 0)
    sc = lax.broadcasted_iota(jnp.int32, (LANES, 2 * LANES), 1)
    blockdiag = (sr // 64) == (sc // LANES)
    return causal, fwd_stack, bwd_stack, rowid, lane, blockdiag


def _dot3(u, x):
    hi, mid, lo = _split3(x)
    return _dot(u, hi) + _dot(u, mid) + _dot(u, lo)


def _row_to_col(row):
    return jnp.transpose(jnp.broadcast_to(row, (LANES, LANES)))


def _gla_gates(ga_ref, gup_ref, gbias_ref):
    pre = _dot(ga_ref[...].astype(BF16), gup_ref[...].astype(BF16)) + gbias_ref[...]
    log_a = (jnp.minimum(pre, 0.0) - jnp.log(1.0 + jnp.exp(-jnp.abs(pre)))) / GLA_TAU
    return pre, log_a


def _gla_terms(g2, q2, k2, fwd_stack):
    bs = _dot3(fwd_stack, g2)
    b, b_ref, b_last = bs[:GLA_ROWS], bs[GLA_ROWS:2 * GLA_ROWS], bs[2 * GLA_ROWS:]
    qs = q2 * GLA_SCALE
    e_q = jnp.exp(b - b_ref)
    e_k = jnp.exp(b_ref - b)
    e_d = jnp.exp(b_last - b)
    e_b = jnp.exp(b)
    decay = jnp.exp(b_last)
    return dict(qs=qs, e_q=e_q, e_k=e_k, e_d=e_d, e_b=e_b, decay=decay,
                qi=qs * e_q, ki=k2 * e_k, kd=k2 * e_d, qb=qs * e_b)


def _gla_fwd(proj, gate_up_pad, gate_bias, gnorm, t, rider):
    nsteps = t // GLA_ROWS
    cps = GLA_ROWS // CHUNK
    n_in, n_out = len(rider.inputs), len(rider.out_shapes)

    def body(q_ref, k_ref, v_ref, gg_ref, ga_ref, gup_ref, gbias_ref, gn_ref, *rest):
        o_ref, oraw_ref, st_ref = rest[n_in:n_in + 3]
        s_scr = rest[n_in + 3 + n_out]
        i = pl.program_id(0)
        _ride(rider, rest[:n_in], rest[n_in + 3:n_in + 3 + n_out], rest[n_in + 4 + n_out:], i == 0, i == nsteps - 1)

        @pl.when(i == 0)
        def _():
            s_scr[...] = jnp.zeros_like(s_scr)

        _, fwd_stack, _, _, lane, blockdiag = _gla_consts()
        causal64 = (lax.broadcasted_iota(jnp.int32, (CHUNK, CHUNK), 1) <= lax.broadcasted_iota(jnp.int32, (CHUNK, CHUNK), 0))
        _, log_a = _gla_gates(ga_ref, gup_ref, gbias_ref)
        gn = gn_ref[...]
        for p in range(2):
            kl = slice(p * LANES, (p + 1) * LANES)
            vl = slice(p * 2 * LANES, (p + 1) * 2 * LANES)
            tm = _gla_terms(log_a[:, kl], q_ref[:, kl], k_ref[:, kl], fwd_stack)
            v2b = v_ref[:, vl].astype(BF16)
            kib = tm["ki"].astype(BF16)
            kdb = tm["kd"].astype(BF16)
            qbb = tm["qb"].astype(BF16)
            qihb = [jnp.where((lane // 64) == hh, tm["qi"], 0.0).astype(BF16) for hh in range(2)]
            state = s_scr[p]
            outs = []
            for cc in range(cps):
                rows = slice(cc * CHUNK, (cc + 1) * CHUNK)
                st_ref[cc, p] = state
                intra = []
                for hh in range(2):
                    a = jnp.where(causal64, _dot_nt(qihb[hh][rows], kib[rows]), 0.0)
                    intra.append(_dot(a.astype(BF16), v2b[rows, hh * LANES:(hh + 1) * LANES]))
                outs.append(jnp.concatenate(intra, axis=1) + _dot(qbb[rows], state.astype(BF16)))
                upd = jnp.where(blockdiag, _dot_tn(kdb[rows], v2b[rows]), 0.0)
                dcol = _row_to_col(tm["decay"][cc * CHUNK:cc * CHUNK + 1])
                state = state * jnp.concatenate([dcol, dcol], axis=1) + upd
            s_scr[p] = state
            o2 = jnp.concatenate(outs, axis=0)
            oraw_ref[:, vl] = o2
            for hh in range(2):
                oh = o2[:, hh * LANES:(hh + 1) * LANES]
                gl = slice(p * 2 * LANES + hh * LANES, p * 2 * LANES + (hh + 1) * LANES)
                rinv = lax.rsqrt(jnp.mean(oh * oh, axis=1, keepdims=True) + RMS_EPS)
                gg = gg_ref[:, gl]
                o_ref[:, gl] = (oh * rinv * gn * (gg * _sigmoid(gg))).astype(BF16)

    cb = lambda w, idx: pl.BlockSpec((GLA_ROWS, w), lambda i: (i, idx))
    full = lambda shp: pl.BlockSpec(shp, lambda i: tuple(0 for _ in shp))
    return pl.pallas_call(
        body, name="gla_fwd", grid=(nsteps,),
        in_specs=[cb(256, 6), cb(256, 7), cb(512, 4), cb(512, 5), cb(128, 24),
                  full((LANES, GLA_KEYS)), full((1, GLA_KEYS)), full((1, LANES))] + [HBM_SPEC] * n_in,
        out_specs=[pl.BlockSpec((GLA_ROWS, GLA_WIDTH), lambda i: (i, 0)),
                   pl.BlockSpec((GLA_ROWS, GLA_WIDTH), lambda i: (i, 0)),
                   pl.BlockSpec((cps, 2, LANES, 2 * LANES), lambda i: (i, 0, 0, 0))] + [HBM_SPEC] * n_out,
        out_shape=[jax.ShapeDtypeStruct((t, GLA_WIDTH), BF16), jax.ShapeDtypeStruct((t, GLA_WIDTH), F32),
                   jax.ShapeDtypeStruct((t // CHUNK, 2, LANES, 2 * LANES), F32)]
        + rider.out_shapes,
        input_output_aliases={8 + i: 3 + o for i, o in rider.aliases},
        scratch_shapes=[pltpu.VMEM((2, LANES, 2 * LANES), F32)] + [pltpu.SemaphoreType.DMA((s,)) for s in rider.sems],
        compiler_params=_cparams("arbitrary"),
    )(proj, proj, proj, proj, proj, gate_up_pad, gate_bias, gnorm, *rider.inputs)


def _gla_bwd(proj, dcat, oraw, states, gate_up_pad, gate_bias, gnorm, t):
    nsteps = t // GLA_ROWS
    cps = GLA_ROWS // CHUNK
    wout = 2 * GLA_KEYS + 2 * GLA_WIDTH + LANES

    def body(q_ref, k_ref, v_ref, gg_ref, ga_ref, do_ref, oraw_ref, st_ref, gup_ref, gbias_ref, gn_ref,
             d_ref, dgup_ref, dgbias_ref, dgn_ref, ds_scr):
        i = pl.program_id(0)

        @pl.when(i == 0)
        def _():
            ds_scr[...] = jnp.zeros_like(ds_scr)
            dgup_ref[...] = jnp.zeros_like(dgup_ref)
            dgbias_ref[...] = jnp.zeros_like(dgbias_ref)
            dgn_ref[...] = jnp.zeros_like(dgn_ref)

        causal, fwd_stack, bwd_stack, rowid, lane, blockdiag = _gla_consts()
        pre, log_a = _gla_gates(ga_ref, gup_ref, gbias_ref)
        gn = gn_ref[...]
        dgn = jnp.zeros((1, LANES), F32)
        dgs = []
        for p in range(2):
            kl = slice(p * LANES, (p + 1) * LANES)
            vl = slice(p * 2 * LANES, (p + 1) * 2 * LANES)
            tm = _gla_terms(log_a[:, kl], q_ref[:, kl], k_ref[:, kl], fwd_stack)
            v2b = v_ref[:, vl].astype(BF16)
            dos = []
            for hh in range(2):
                gl = slice(p * 2 * LANES + hh * LANES, p * 2 * LANES + (hh + 1) * LANES)
                oh = oraw_ref[:, gl]
                rinv = lax.rsqrt(jnp.mean(oh * oh, axis=1, keepdims=True) + RMS_EPS)
                on = oh * rinv
                gg = gg_ref[:, gl]
                sg = _sigmoid(gg)
                sil = gg * sg
                dgo = do_ref[:, gl]
                d_ref[:, 2 * GLA_KEYS + GLA_WIDTH + gl.start:2 * GLA_KEYS + GLA_WIDTH + gl.stop] = (
                    dgo * on * gn * (sg * (1.0 + gg * (1.0 - sg)))).astype(BF16)
                dgn = dgn + jnp.sum(dgo * sil * on, axis=0, keepdims=True)
                don = dgo * sil * gn
                dos.append(rinv * (don - on * jnp.mean(don * on, axis=1, keepdims=True)))
            do2b = jnp.concatenate(dos, axis=1).astype(BF16)
            qib = tm["qi"].astype(BF16)
            kib = tm["ki"].astype(BF16)
            kdb = tm["kd"].astype(BF16)
            qbb = tm["qb"].astype(BF16)
            dqi = jnp.zeros((GLA_ROWS, LANES), F32)
            dki = jnp.zeros((GLA_ROWS, LANES), F32)
            dvs = []
            for hh in range(2):
                hm = (lane // 64) == hh
                hl = slice(hh * LANES, (hh + 1) * LANES)
                a = jnp.where(causal, _dot_nt(jnp.where(hm, tm["qi"], 0.0).astype(BF16), kib), 0.0).astype(BF16)
                da = jnp.where(causal, _dot_nt(do2b[:, hl], v2b[:, hl]), 0.0).astype(BF16)
                dvs.append(_dot_tn(a, do2b[:, hl]))
                dqi = dqi + jnp.where(hm, _dot(da, kib), 0.0)
                dki = dki + jnp.where(hm, _dot_tn(da, qib), 0.0)
            ds2 = ds_scr[p]
            dv_st, dqb, dkd, dd = [None] * cps, [None] * cps, [None] * cps, [None] * cps
            for cc in reversed(range(cps)):
                rows = slice(cc * CHUNK, (cc + 1) * CHUNK)
                s_prev = st_ref[cc, p]
                ds2b = ds2.astype(BF16)
                dv_st[cc] = _dot(kdb[rows], ds2b)
                dqb[cc] = _dot_nt(do2b[rows], s_prev.astype(BF16))
                dkd[cc] = _dot_nt(v2b[rows], ds2b)
                decay_row = tm["decay"][cc * CHUNK:cc * CHUNK + 1]
                ddecay_col = jnp.sum(ds2 * s_prev, axis=1, keepdims=True)
                ddecay_row = jnp.transpose(jnp.broadcast_to(ddecay_col, (LANES, LANES)))[0:1, :]
                dd[cc] = jnp.broadcast_to(ddecay_row * decay_row, (CHUNK, LANES))
                dcol = _row_to_col(decay_row)
                ds2 = jnp.where(blockdiag, _dot_tn(qbb[rows], do2b[rows]), 0.0) + ds2 * jnp.concatenate([dcol, dcol], axis=1)
            ds_scr[p] = ds2
            dv2 = jnp.concatenate(dvs, axis=1) + jnp.concatenate(dv_st, axis=0)
            dqb = jnp.concatenate(dqb, axis=0)
            dkd = jnp.concatenate(dkd, axis=0)
            dqs = dqi * tm["e_q"] + dqb * tm["e_b"]
            dk = dki * tm["e_k"] + dkd * tm["e_d"]
            t_qi = dqi * tm["qi"]
            t_ki = dki * tm["ki"]
            t_kd = dkd * tm["kd"]
            db = t_qi - t_ki + dqb * tm["qb"] - t_kd
            to_mid = t_ki - t_qi
            to_last = t_kd + jnp.where(rowid % CHUNK == CHUNK - 1, jnp.concatenate(dd, axis=0), 0.0)
            dgs.append(_dot3(bwd_stack, jnp.concatenate([db, to_mid, to_last], axis=0)))
            d_ref[:, p * LANES:(p + 1) * LANES] = (dqs * GLA_SCALE).astype(BF16)
            d_ref[:, GLA_KEYS + p * LANES:GLA_KEYS + (p + 1) * LANES] = dk.astype(BF16)
            d_ref[:, 2 * GLA_KEYS + p * 2 * LANES:2 * GLA_KEYS + (p + 1) * 2 * LANES] = dv2.astype(BF16)
        dlog_a = jnp.concatenate(dgs, axis=1)
        dpre = dlog_a * (1.0 / GLA_TAU) * _sigmoid(-pre)
        dpb = dpre.astype(BF16)
        dgn_ref[...] += dgn
        dgbias_ref[...] += jnp.sum(dpre, axis=0, keepdims=True)
        dgup_ref[...] += _dot_tn(ga_ref[...].astype(BF16), dpb)
        d_ref[:, 2 * GLA_KEYS + 2 * GLA_WIDTH:] = _dot_nt(dpb, gup_ref[...].astype(BF16)).astype(BF16)

    rev = lambda i: nsteps - 1 - i
    cb = lambda w, idx: pl.BlockSpec((GLA_ROWS, w), lambda i: (rev(i), idx))
    full = lambda shp: pl.BlockSpec(shp, lambda i: tuple(0 for _ in shp))
    return pl.pallas_call(
        body, name="gla_bwd", grid=(nsteps,),
        in_specs=[cb(256, 6), cb(256, 7), cb(512, 4), cb(512, 5), cb(128, 24), cb(512, 1), cb(512, 0),
                  pl.BlockSpec((cps, 2, LANES, 2 * LANES), lambda i: (rev(i), 0, 0, 0)),
                  full((LANES, GLA_KEYS)), full((1, GLA_KEYS)), full((1, LANES))],
        out_specs=[pl.BlockSpec((GLA_ROWS, wout), lambda i: (rev(i), 0)),
                   full((LANES, GLA_KEYS)), full((1, GLA_KEYS)), full((1, LANES))],
        out_shape=[jax.ShapeDtypeStruct((t, wout), BF16), jax.ShapeDtypeStruct((LANES, GLA_KEYS), F32),
                   jax.ShapeDtypeStruct((1, GLA_KEYS), F32), jax.ShapeDtypeStruct((1, LANES), F32)],
        scratch_shapes=[pltpu.VMEM((2, LANES, 2 * LANES), F32)],
        compiler_params=_cparams("arbitrary"),
    )(proj, proj, proj, proj, proj, dcat, oraw, states, gate_up_pad, gate_bias, gnorm)


def _ln_stats(r):
    mu = jnp.mean(r, axis=1, keepdims=True)
    xc = r - mu
    rstd = lax.rsqrt(jnp.mean(xc * xc, axis=1, keepdims=True) + LN_EPS)
    return xc * rstd, rstd


def _ln_bwd(dy_g, xhat, rstd):
    return rstd * (dy_g - jnp.mean(dy_g, axis=1, keepdims=True) - xhat * jnp.mean(dy_g * xhat, axis=1, keepdims=True))


def _outproj_ln1(sb_o, gla_o, x, w_out, g1, b1, t, tm=256):
    def body(sb_ref, gl_ref, x_ref, w_ref, g_ref, b_ref, xhat_ref, rstd_ref, h_ref):
        mix = _dot(sb_ref[...], w_ref[0:SB_WIDTH, :]) + _dot(gl_ref[...], w_ref[SB_WIDTH:, :])
        xhat, rstd = _ln_stats(ALPHA * x_ref[...] + mix)
        xhat_ref[...] = xhat
        rstd_ref[...] = rstd
        h_ref[...] = (xhat * g_ref[...] + b_ref[...]).astype(BF16)

    row = lambda w: pl.BlockSpec((tm, w), lambda i: (i, 0))
    full = lambda shp: pl.BlockSpec(shp, lambda i: (0, 0))
    return pl.pallas_call(
        body, name="outproj_ln1", grid=(t // tm,),
        in_specs=[row(SB_WIDTH), row(GLA_WIDTH), row(D_MODEL), full((D_MODEL, D_MODEL)), full((1, D_MODEL)), full((1, D_MODEL))],
        out_specs=[row(D_MODEL), row(1), row(D_MODEL)],
        out_shape=[jax.ShapeDtypeStruct((t, D_MODEL), F32), jax.ShapeDtypeStruct((t, 1), F32),
                   jax.ShapeDtypeStruct((t, D_MODEL), BF16)],
        compiler_params=_cparams("parallel"),
    )(sb_o, gla_o, x, w_out, g1, b1)


_INV_SQRT2 = 1.0 / math.sqrt(2.0)
_INV_SQRT2PI = 1.0 / math.sqrt(2.0 * math.pi)


def _conv3(xs, w_ref, b_ref, half):
    return (w_ref[half, 0:1, :] * pltpu.roll(xs, 2, 0) + w_ref[half, 1:2, :] * pltpu.roll(xs, 1, 0)
            + w_ref[half, 2:3, :] * xs + b_ref[half])


HALO = 16


def _conv_gelu_fwd(up3, conv_w3, conv_b3, t, tr=512, ct=256):
    nct = D_FF // ct
    hb = tr // HALO

    def body(cur_ref, prev_ref, w_ref, b_ref, gm_ref):
        i = pl.program_id(1)
        keep = (i > 0).astype(F32)
        us = []
        for half in range(2):
            xs = jnp.concatenate([prev_ref[half].astype(F32) * keep, cur_ref[half].astype(F32)], axis=0)
            us.append(_conv3(xs, w_ref, b_ref, half)[HALO:, :])
        a, c = us
        gelu = 0.5 * a * (1.0 + lax.erf(a * _INV_SQRT2))
        gm_ref[...] = (gelu * c).astype(BF16)

    return pl.pallas_call(
        body, name="conv_gelu_fwd", grid=(nct, t // tr),
        in_specs=[pl.BlockSpec((2, tr, ct), lambda j, i: (0, i, j)),
                  pl.BlockSpec((2, HALO, ct), lambda j, i: (0, jnp.maximum(i * hb - 1, 0), j)),
                  pl.BlockSpec((2, 3, ct), lambda j, i: (0, 0, j)),
                  pl.BlockSpec((2, 1, ct), lambda j, i: (0, 0, j))],
        out_specs=pl.BlockSpec((tr, ct), lambda j, i: (i, j)),
        out_shape=jax.ShapeDtypeStruct((t, D_FF), BF16),
        compiler_params=_cparams("parallel", "parallel"),
    )(up3, up3, conv_w3, conv_b3)


def _conv_gelu_bwd(up3, dgm, conv_w3, conv_b3, t, tr=512, ct=256):
    nct = D_FF // ct
    nrt = t // tr
    hb = tr // HALO
    n = tr + 2 * HALO
    lo, hi = HALO, tr + HALO

    def body(cur_ref, prev_ref, next_ref, dg_ref, dgn_ref, w_ref, b_ref, dup_ref, dcw_ref, dcb_ref):
        i = pl.program_id(1)

        @pl.when(i == 0)
        def _():
            dcw_ref[...] = jnp.zeros_like(dcw_ref)
            dcb_ref[...] = jnp.zeros_like(dcb_ref)

        keep_prev = (i > 0).astype(F32)
        keep_next = (i < nrt - 1).astype(F32)
        xs, xm1, xm2, us = [], [], [], []
        for half in range(2):
            x = jnp.concatenate([prev_ref[half].astype(F32) * keep_prev, cur_ref[half].astype(F32),
                                 next_ref[half].astype(F32)], axis=0)
            xs.append(x)
            xm1.append(pltpu.roll(x, 1, 0))
            xm2.append(pltpu.roll(x, 2, 0))
            us.append(w_ref[half, 0:1, :] * xm2[half] + w_ref[half, 1:2, :] * xm1[half]
                      + w_ref[half, 2:3, :] * x + b_ref[half])
        a, c = us
        dg = jnp.concatenate([jnp.zeros((HALO, ct), F32), dg_ref[...].astype(F32),
                              dgn_ref[...].astype(F32) * keep_next], axis=0)
        cdf = 0.5 * (1.0 + lax.erf(a * _INV_SQRT2))
        pdf = jnp.exp(-0.5 * a * a) * _INV_SQRT2PI
        dus = [dg * c * (cdf + a * pdf), dg * (a * cdf)]
        rid = lax.broadcasted_iota(jnp.int32, (8, 1), 0)
        for half in range(2):
            du = dus[half]
            dup = (w_ref[half, 2:3, :] * du + w_ref[half, 1:2, :] * pltpu.roll(du, n - 1, 0)
                   + w_ref[half, 0:1, :] * pltpu.roll(du, n - 2, 0))
            dup_ref[half] = dup[lo:hi, :].astype(BF16)
            duc = du[lo:hi, :]
            s0 = jnp.sum(duc * xm2[half][lo:hi, :], axis=0, keepdims=True)
            s1 = jnp.sum(duc * xm1[half][lo:hi, :], axis=0, keepdims=True)
            s2 = jnp.sum(duc * xs[half][lo:hi, :], axis=0, keepdims=True)
            dcw_ref[half] += jnp.where(rid == 0, s0, jnp.where(rid == 1, s1, jnp.where(rid == 2, s2, 0.0)))
            dcb_ref[half] += jnp.sum(duc, axis=0, keepdims=True)

    last = t // HALO - 1
    return pl.pallas_call(
        body, name="conv_gelu_bwd", grid=(nct, nrt),
        in_specs=[pl.BlockSpec((2, tr, ct), lambda j, i: (0, i, j)),
                  pl.BlockSpec((2, HALO, ct), lambda j, i: (0, jnp.maximum(i * hb - 1, 0), j)),
                  pl.BlockSpec((2, HALO, ct), lambda j, i: (0, jnp.minimum((i + 1) * hb, last), j)),
                  pl.BlockSpec((tr, ct), lambda j, i: (i, j)),
                  pl.BlockSpec((HALO, ct), lambda j, i: (jnp.minimum((i + 1) * hb, last), j)),
                  pl.BlockSpec((2, 3, ct), lambda j, i: (0, 0, j)),
                  pl.BlockSpec((2, 1, ct), lambda j, i: (0, 0, j))],
        out_specs=[pl.BlockSpec((2, tr, ct), lambda j, i: (0, i, j)),
                   pl.BlockSpec((2, 8, ct), lambda j, i: (0, 0, j)),
                   pl.BlockSpec((2, 1, ct), lambda j, i: (0, 0, j))],
        out_shape=[jax.ShapeDtypeStruct((2, t, D_FF), BF16), jax.ShapeDtypeStruct((2, 8, D_FF), F32),
                   jax.ShapeDtypeStruct((2, 1, D_FF), F32)],
        compiler_params=_cparams("parallel", "arbitrary"),
    )(up3, up3, up3, dgm, dgm, conv_w3, conv_b3)


def _down_ln2_loss(gm, w_down, xhat1, g1, b1, g2, b2, target, t, tm=256):
    def body(gm_ref, w_ref, xh_ref, g1_ref, b1_ref, g2_ref, b2_ref, tg_ref, dr_ref, loss_ref, dg_ref, db_ref):
        i = pl.program_id(0)

        @pl.when(i == 0)
        def _():
            loss_ref[...] = jnp.zeros_like(loss_ref)
            dg_ref[...] = jnp.zeros_like(dg_ref)
            db_ref[...] = jnp.zeros_like(db_ref)

        h = xh_ref[...] * g1_ref[...] + b1_ref[...]
        xhat, rstd = _ln_stats(ALPHA * h + _dot(gm_ref[...], w_ref[...]))
        err = xhat * g2_ref[...] + b2_ref[...] - tg_ref[...]
        loss_ref[...] += 0.5 * jnp.sum(jnp.sum(err * err, axis=1, keepdims=True), axis=0, keepdims=True) / D_MODEL
        dy = err * (1.0 / D_MODEL)
        dg_ref[...] += jnp.sum(dy * xhat, axis=0, keepdims=True)
        db_ref[...] += jnp.sum(dy, axis=0, keepdims=True)
        dr_ref[...] = _ln_bwd(dy * g2_ref[...], xhat, rstd)

    row = lambda w: pl.BlockSpec((tm, w), lambda i: (i, 0))
    full = lambda shp: pl.BlockSpec(shp, lambda i: (0, 0))
    vec = full((1, D_MODEL))
    return pl.pallas_call(
        body, name="down_ln2_loss", grid=(t // tm,),
        in_specs=[row(D_FF), full((D_FF, D_MODEL)), row(D_MODEL), vec, vec, vec, vec, row(D_MODEL)],
        out_specs=[row(D_MODEL), full((1, 1)), vec, vec],
        out_shape=[jax.ShapeDtypeStruct((t, D_MODEL), F32), jax.ShapeDtypeStruct((1, 1), F32),
                   jax.ShapeDtypeStruct((1, D_MODEL), F32), jax.ShapeDtypeStruct((1, D_MODEL), F32)],
        compiler_params=_cparams("arbitrary"),
    )(gm, w_down, xhat1, g1, b1, g2, b2, target)


def _dh_ln1_bwd(dup3, w_up4, dr2, xhat1, rstd1, g1, t, tm=256):
    ws = 2 * D_FF // 4

    def body(a_ref, w_ref, dr2_ref, xh_ref, rs_ref, g_ref, dr1_ref, dg_ref, db_ref):
        i = pl.program_id(0)

        @pl.when(i == 0)
        def _():
            dg_ref[...] = jnp.zeros_like(dg_ref)
            db_ref[...] = jnp.zeros_like(db_ref)

        dh = ALPHA * dr2_ref[...]
        for s in range(4):
            dh = dh + _dot_nt(a_ref[s // 2, :, (s % 2) * ws:(s % 2 + 1) * ws], w_ref[s])
        xhat = xh_ref[...]
        dg_ref[...] += jnp.sum(dh * xhat, axis=0, keepdims=True)
        db_ref[...] += jnp.sum(dh, axis=0, keepdims=True)
        dr1_ref[...] = _ln_bwd(dh * g_ref[...], xhat, rs_ref[...])

    row = lambda w: pl.BlockSpec((tm, w), lambda i: (i, 0))
    vec = pl.BlockSpec((1, D_MODEL), lambda i: (0, 0))
    return pl.pallas_call(
        body, name="dh_ln1_bwd", grid=(t // tm,),
        in_specs=[pl.BlockSpec((2, tm, D_FF), lambda i: (0, i, 0)),
                  pl.BlockSpec((4, D_MODEL, ws), lambda i: (0, 0, 0)),
                  row(D_MODEL), row(D_MODEL), row(1), vec],
        out_specs=[row(D_MODEL), vec, vec],
        out_shape=[jax.ShapeDtypeStruct((t, D_MODEL), F32), jax.ShapeDtypeStruct((1, D_MODEL), F32),
                   jax.ShapeDtypeStruct((1, D_MODEL), F32)],
        compiler_params=_cparams("arbitrary"),
    )(dup3, w_up4, dr2, xhat1, rstd1, g1)


def _adamw(w, g, m, v, name):
    rows, cols = w.shape
    tr = rows
    for cand in (256, 128, 64, 32, 16, 8):
        if rows % cand == 0 and rows > cand:
            tr = cand
            break
    c1 = 1.0 / (1.0 - ADAM_B1 ** ADAM_STEP)
    c2 = 1.0 / (1.0 - ADAM_B2 ** ADAM_STEP)

    def body(w_ref, g_ref, m_ref, v_ref, d_ref, nm_ref, nv_ref):
        gv = g_ref[...]
        nm = ADAM_B1 * m_ref[...] + (1.0 - ADAM_B1) * gv
        nv = ADAM_B2 * v_ref[...] + (1.0 - ADAM_B2) * (gv * gv)
        d_ref[...] = -ADAM_LR * ((nm * c1) / (jnp.sqrt(nv * c2) + ADAM_EPS) + ADAM_WD * w_ref[...])
        nm_ref[...] = nm
        nv_ref[...] = nv

    spec = pl.BlockSpec((tr, cols), lambda i: (i, 0))
    out = jax.ShapeDtypeStruct((rows, cols), F32)
    return pl.pallas_call(
        body, name=name, grid=(rows // tr,), in_specs=[spec] * 4, out_specs=[spec] * 3, out_shape=[out] * 3,
        compiler_params=_cparams("parallel"),
    )(w, g, m, v)


def _local_step(x, target, w_in_p, late_shards, gate_up_pad, gate_bias, gnorm, ln1_g, ln1_b, conv_w3, conv_b3,
                ln2_g, ln2_b, c_arr, kc_arr):
    t = x.shape[0]
    tq = min(t, 1024)
    s_up, s_out, s_down = late_shards
    sh_up, sh_out, sh_down = LATE_SHAPES
    proj, out_partly = _mm(x, w_in_p, m=t, n=IN_PAD, k=D_MODEL, tm=tq, tn=640, tk=D_MODEL, name="proj",
                           rider=_gather_rider([s_out], [sh_out]))
    sb_o, sb_mass, up_partly = _sb_fwd(proj, t, _gather_rider([s_up], [sh_up]))
    gla_o, oraw, states, w_up4, w_out4, down_partly = _gla_fwd(
        proj, gate_up_pad, gate_bias, gnorm, t,
        _join(_forward_rider([up_partly, out_partly], [sh_up, sh_out]), _gather_rider([s_down], [sh_down])))
    w_out = w_out4.reshape(D_MODEL, D_MODEL)
    xhat1, rstd1, h_bf = _outproj_ln1(sb_o, gla_o, x, w_out, ln1_g, ln1_b, t)
    up3, w_down4 = _mm(h_bf, w_up4, m=t, n=2 * D_FF, k=D_MODEL, tm=tq, tn=W_UP_S, tk=D_MODEL, name="up",
                       b_spec=pl.BlockSpec((None, D_MODEL, W_UP_S), lambda i, j, kk: (j, 0, 0)),
                       o_spec=pl.BlockSpec((None, tq, W_UP_S), lambda i, j, kk: (j // 2, i, j % 2)),
                       out_shape=jax.ShapeDtypeStruct((2, t, D_FF), BF16), out_dtype=BF16,
                       rider=_forward_rider([down_partly], [sh_down]))
    w_down = w_down4.reshape(D_FF, D_MODEL)
    gm = _conv_gelu_fwd(up3, conv_w3, conv_b3, t, tr=tq)
    dr2, loss, dln2_g, dln2_b = _down_ln2_loss(gm, w_down, xhat1, ln1_g, ln1_b, ln2_g, ln2_b, target, t)
    dgm = _mm(dr2, w_down, m=t, n=D_FF, k=D_MODEL, tm=tq, tn=W_UP_S, tk=D_MODEL, tb=True, out_dtype=BF16, name="dgm")
    dw_down = _mm(gm, dr2, m=D_FF, n=D_MODEL, k=t, tm=W_UP_S, tn=D_MODEL, tk=tq, ta=True, name="dw_down")
    dup3, dcw, dcb = _conv_gelu_bwd(up3, dgm, conv_w3, conv_b3, t)
    dr1, dln1_g, dln1_b = _dh_ln1_bwd(dup3, w_up4, dr2, xhat1, rstd1, ln1_g, t)
    dw_up4 = _mm(h_bf, dup3, m=D_MODEL, n=2 * D_FF, k=t, tm=512, tn=W_UP_S, tk=t, ta=True, name="dw_up",
                 b_spec=pl.BlockSpec((None, t, W_UP_S), lambda i, j, kk: (j // 2, kk, j % 2)),
                 o_spec=pl.BlockSpec((None, 512, W_UP_S), lambda i, j, kk: (j, i, 0)),
                 out_shape=jax.ShapeDtypeStruct((4, D_MODEL, W_UP_S), F32))
    dw_out_sb = _mm(sb_o, dr1, m=SB_WIDTH, n=D_MODEL, k=t, tm=512, tn=D_MODEL, tk=tq, ta=True, name="dw_out_sb")
    dw_out_gla = _mm(gla_o, dr1, m=GLA_WIDTH, n=D_MODEL, k=t, tm=512, tn=D_MODEL, tk=tq, ta=True, name="dw_out_gla")
    gs = [dw_up4, jnp.concatenate([dw_out_sb, dw_out_gla], axis=0).reshape(4, W_OUT_S, D_MODEL),
          dw_down.reshape(4, W_DOWN_S, D_MODEL)]
    dcat, *from_sib = _mm(dr1, w_out, m=t, n=D_MODEL, k=D_MODEL, tm=tq, tn=512, tk=D_MODEL, tb=True, name="dcat",
                          rider=_sibling_rider(gs, LATE_SHAPES))
    ps = [_add_sibling(gs[m], from_sib[m], c_arr, LATE_ADD_ROWS[m], "add_sibling_late_%d" % m) for m in range(3)]
    dgla, dgup_pad, dgbias, dgnorm = _gla_bwd(proj, dcat, oraw, states, gate_up_pad, gate_bias, gnorm, t)
    small = dict(
        gate_up=dgup_pad[:GATE_RANK], gate_bias=dgbias, gla_norm_g=dgnorm, ln1_g=dln1_g, ln1_b=dln1_b,
        conv_w=jnp.concatenate([dcw[0, :3], dcw[1, :3]], axis=1), conv_b=jnp.concatenate([dcb[0], dcb[1]], axis=1),
        ln2_g=dln2_g, ln2_b=dln2_b, loss=loss)
    dsq, dsk, dsv, *others, vecs = _sb_bwd(proj, dcat, sb_mass, t,
                                           _chips_rider(ps, _pack_vec(small, SMALL_GRADS, GRAD_VEC_ROWS)))
    late_sums = [_add_chips(ps[m], others[m], kc_arr, LATE_ADD_ROWS[m], "add_chips_late_%d" % m) for m in range(3)]
    dproj = [dsq, dsk, dsv, dgla]
    dw_in_p = _dw_in(x, dproj, t, tk=tq)
    g_in = dw_in_p[None]
    from_sib_in, = _run(_sibling_rider([g_in], [(D_MODEL, IN_PAD)]), "exchange_sibling_w_in")
    half_in = _add_sibling(g_in, from_sib_in, c_arr, ADD_ROWS[0], "add_sibling_w_in")[0]
    p_in = jnp.stack([half_in[:, k * W_IN_S:(k + 1) * W_IN_S] for k in range(4)], axis=0)
    dx, others_in = _dx(dproj, w_in_p, dr1, t, _chips_rider([p_in]))
    sum_in = _add_chips(p_in, others_in, kc_arr, ADD_ROWS[0], "add_chips_w_in")
    return dx, [sum_in] + late_sums, vecs


W_IN_S, W_UP_S, W_OUT_S, W_DOWN_S = IN_WIDTH // 4, 2 * D_FF // 4, D_MODEL // 4, D_FF // 4
SHARD_SHAPES = ((D_MODEL, W_IN_S), (D_MODEL, W_UP_S), (W_OUT_S, D_MODEL), (W_DOWN_S, D_MODEL))
ADD_ROWS = (256, 256, 128, 176)
LATE_SHAPES, LATE_ADD_ROWS = SHARD_SHAPES[1:], ADD_ROWS[1:]
SMALL_ROWS = 8
VEC_SIZES = (("gate_bias", GLA_KEYS), ("gla_norm_g", LANES), ("ln1_g", D_MODEL), ("ln1_b", D_MODEL),
             ("conv_b", 2 * D_FF), ("ln2_g", D_MODEL), ("ln2_b", D_MODEL))
SMALL_GRADS = VEC_SIZES + (("conv_w", 3 * 2 * D_FF), ("gate_up", GATE_RANK * GLA_KEYS), ("loss", 1))


def _rows(a):
    flat = a.reshape(-1)
    pad = (-flat.shape[0]) % D_MODEL
    if pad:
        flat = jnp.pad(flat, (0, pad))
    return flat.reshape(-1, D_MODEL)


def _pad_rows(a, rows):
    return jnp.pad(a, ((0, rows - a.shape[0]), (0, 0)))


def _pack_vec(d, sizes, rows):
    flat = jnp.concatenate([d[n].reshape(-1) for n, _ in sizes])
    return _pad_rows(_rows(flat), rows)


def _unpack_vec(v, sizes):
    flat = v.reshape(-1)
    out, o = {}, 0
    for n, size in sizes:
        out[n] = flat[o:o + size].reshape(1, size)
        o += size
    return out


VEC_ROWS = 16
GRAD_VEC_ROWS = 32


HBM_SPEC = pl.BlockSpec(memory_space=pltpu.HBM)


def _position():
    x, y, c = lax.axis_index("x"), lax.axis_index("y"), lax.axis_index("c")
    chips = [(1 - x, y), (x, 1 - y), (1 - x, 1 - y)]
    return x, y, c, chips


def _remote(src, dst, send_sems, recv_sems, k, to):
    return pltpu.make_async_remote_copy(src_ref=src, dst_ref=dst, send_sem=send_sems.at[k], recv_sem=recv_sems.at[k],
                                        device_id=to, device_id_type=MESH)


def _gather_ici(in_refs, out_refs, shapes, send_sems, recv_sems, local_sems):
    x, y, c, chips = _position()
    k_me = 2 * x + y
    local, sends, recvs = [], [], []
    for m, (rows, _) in enumerate(shapes):
        h = rows // 2
        local.append(pltpu.make_async_copy(in_refs[m], out_refs[m].at[k_me], local_sems.at[m]))
        for j, (cx, cy) in enumerate(chips):
            sends.append(_remote(in_refs[m].at[pl.ds(c * h, h), :], out_refs[m].at[k_me, pl.ds(c * h, h), :],
                                 send_sems, recv_sems, 3 * m + j, (cx, cy, c)))
            landed = out_refs[m].at[2 * cx + cy, pl.ds(c * h, h), :]
            recvs.append(_remote(landed, landed, send_sems, recv_sems, 3 * m + j, (x, y, c)))
    return local, sends, recvs


def _gather_d2d(src_refs, dst_refs, shapes, send_sems, recv_sems, base):
    x, y, c, chips = _position()
    sends, recvs = [], []
    for m, (rows, _) in enumerate(shapes):
        h = rows // 2
        for j, (cx, cy) in enumerate(chips):
            k = 2 * cx + cy
            sends.append(_remote(src_refs[m].at[k, pl.ds(c * h, h), :], dst_refs[m].at[k, pl.ds(c * h, h), :],
                                 send_sems, recv_sems, base + 3 * m + j, (x, y, 1 - c)))
            landed = dst_refs[m].at[k, pl.ds((1 - c) * h, h), :]
            recvs.append(_remote(landed, landed, send_sems, recv_sems, base + 3 * m + j, (x, y, c)))
    return sends, recvs


def _gather_weights(shards, small, shapes):
    nm = len(shards)
    n_ici = 3 * nm

    def body(*refs):
        in_refs, small_ref = refs[:nm], refs[nm]
        out_refs, osm_ref = refs[nm + 1:2 * nm + 1], refs[2 * nm + 1]
        send_sems, recv_sems, local_sems = refs[2 * nm + 2:]
        x, y, c, chips = _position()
        k_me = 2 * x + y
        local, sends, recvs = _gather_ici(in_refs, out_refs, shapes, send_sems, recv_sems, local_sems)
        local.append(pltpu.make_async_copy(small_ref, osm_ref.at[k_me], local_sems.at[nm]))
        for j, (cx, cy) in enumerate(chips):
            sends.append(_remote(small_ref, osm_ref.at[k_me], send_sems, recv_sems, n_ici + j, (cx, cy, c)))
        for cp in local + sends:
            cp.start()
        for cp in recvs:
            cp.wait_recv()
        fsends, frecvs = _gather_d2d(out_refs, out_refs, shapes, send_sems, recv_sems, n_ici + 3)
        for cp in fsends:
            cp.start()
        for j, (cx, cy) in enumerate(chips):
            k = 2 * cx + cy
            frecvs.append(_remote(osm_ref.at[k], osm_ref.at[k], send_sems, recv_sems, n_ici + j, (x, y, c)))
        for cp in frecvs:
            cp.wait_recv()
        for cp in sends + fsends:
            cp.wait_send()
        for cp in local:
            cp.wait()

    n_sems = 2 * n_ici + 3
    return pl.pallas_call(
        body, name="gather_weights", in_specs=[HBM_SPEC] * (nm + 1), out_specs=[HBM_SPEC] * (nm + 1),
        out_shape=[jax.ShapeDtypeStruct((4,) + s, BF16) for s in shapes]
        + [jax.ShapeDtypeStruct((4, SMALL_ROWS, D_MODEL), F32)],
        scratch_shapes=[pltpu.SemaphoreType.DMA((n_sems,)), pltpu.SemaphoreType.DMA((n_sems,)),
                        pltpu.SemaphoreType.DMA((nm + 1,))],
    )(*shards, small)


def _gather_rider(shards, shapes):
    n = len(shards)
    return _Rider(shards, [jax.ShapeDtypeStruct((4,) + s, BF16) for s in shapes], (3 * n, 3 * n, n),
                  lambda ins, outs, sems: _gather_ici(ins, outs, shapes, *sems))


def _forward_rider(gathered, shapes):
    n = len(gathered)
    return _Rider(gathered, [jax.ShapeDtypeStruct(a.shape, a.dtype) for a in gathered], (3 * n, 3 * n),
                  lambda ins, outs, sems: ([],) + _gather_d2d(ins, outs, shapes, sems[0], sems[1], 0),
                  aliases=[(m, m) for m in range(n)])


def _sibling_rider(gs, shapes):
    def copies(ins, outs, sems):
        x, y, c, _ = _position()
        both = []
        for m, (rows, _) in enumerate(shapes):
            h = rows // 2
            for k in range(gs[m].shape[0]):
                both.append(_remote(ins[m].at[k, pl.ds((1 - c) * h, h), :], outs[m].at[k], sems[0], sems[1],
                                    4 * m + k, (x, y, 1 - c)))
        return [], both, both

    return _Rider(gs, [jax.ShapeDtypeStruct((g.shape[0], r // 2, cl), F32) for g, (r, cl) in zip(gs, shapes)],
                  (4 * len(gs), 4 * len(gs)), copies)


def _add_sibling(g, r, c_arr, tr, name):
    nblk, rows, cols = g.shape
    nb = rows // 2 // tr

    def body(c_ref, g_ref, r_ref, o_ref):
        o_ref[...] = (g_ref[...] + r_ref[...]).astype(BF16)

    spec = pl.BlockSpec((None, tr, cols), lambda k, i, c: (k, i, 0))
    return pl.pallas_call(
        body, name=name,
        grid_spec=pltpu.PrefetchScalarGridSpec(
            num_scalar_prefetch=1, grid=(nblk, nb),
            in_specs=[pl.BlockSpec((None, tr, cols), lambda k, i, c: (k, c[0] * nb + i, 0)), spec], out_specs=spec),
        out_shape=jax.ShapeDtypeStruct((nblk, rows // 2, cols), BF16), compiler_params=_cparams("parallel", "parallel"),
    )(c_arr, g, r)


def _reduce_ici(p_refs, r_refs, send_sems, recv_sems):
    x, y, c, chips = _position()
    sends, recvs = [], []
    for m in range(len(p_refs)):
        for j, (cx, cy) in enumerate(chips):
            sends.append(_remote(p_refs[m].at[2 * cx + cy], r_refs[m].at[j], send_sems, recv_sems, 3 * m + j, (cx, cy, c)))
            recvs.append(_remote(r_refs[m].at[j], r_refs[m].at[j], send_sems, recv_sems, 3 * m + j, (x, y, c)))
    return sends, recvs


def _chips_rider(ps, vec=None):
    nm = len(ps)
    n_ici = 3 * nm

    def copies(ins, outs, sems):
        sends, recvs = _reduce_ici(ins[:nm], outs[:nm], sems[0], sems[1])
        if vec is None:
            return [], sends, recvs
        x, y, c, _ = _position()
        my_id = 4 * x + 2 * y + c
        vec_ref, vrecv_ref = ins[nm], outs[nm]
        local = [pltpu.make_async_copy(vec_ref, vrecv_ref.at[my_id], sems[2].at[0])]
        for r in range(1, 8):
            peer = (1 - x if r & 4 else x, 1 - y if r & 2 else y, 1 - c if r & 1 else c)
            sends.append(_remote(vec_ref, vrecv_ref.at[my_id], sems[0], sems[1], n_ici + r - 1, peer))
            recvs.append(_remote(vec_ref, vrecv_ref.at[0], sems[0], sems[1], n_ici + r - 1, (x, y, c)))
        return local, sends, recvs

    out_shapes = [jax.ShapeDtypeStruct((3,) + p.shape[1:], p.dtype) for p in ps]
    if vec is None:
        return _Rider(ps, out_shapes, (n_ici, n_ici), copies)
    return _Rider(list(ps) + [vec], out_shapes + [jax.ShapeDtypeStruct((8, GRAD_VEC_ROWS, D_MODEL), F32)],
                  (n_ici + 7, n_ici + 7, 1), copies)


def _add_chips(p, r, kc_arr, tr, name):
    _, h, cols = p.shape
    nb = h // tr

    def body(kc_ref, p_ref, r0_ref, r1_ref, r2_ref, o_ref):
        o_ref[...] = ((p_ref[...].astype(F32) + r0_ref[...].astype(F32)) + r1_ref[...].astype(F32)) + r2_ref[...].astype(F32)

    rspec = lambda j: pl.BlockSpec((None, tr, cols), lambda i, kc: (j, i, 0))
    return pl.pallas_call(
        body, name=name,
        grid_spec=pltpu.PrefetchScalarGridSpec(
            num_scalar_prefetch=1, grid=(nb,),
            in_specs=[pl.BlockSpec((None, tr, cols), lambda i, kc: (kc[0], i, 0)), rspec(0), rspec(1), rspec(2)],
            out_specs=pl.BlockSpec((tr, cols), lambda i, kc: (kc[1] * nb + i, 0))),
        out_shape=jax.ShapeDtypeStruct((2 * h, cols), F32), compiler_params=_cparams("parallel"),
    )(kc_arr, p, r, r, r)


def _reunite_sibling(fs, shapes):
    n_chunks = 2
    nm = len(fs)

    def body(*refs):
        in_refs, out_refs = refs[:nm], refs[nm:2 * nm]
        send_sems, recv_sems = refs[2 * nm:]
        x, y, c, _ = _position()
        sends, recvs = [], []
        for m in range(nm):
            ch = shapes[m][0] // 2 // n_chunks
            for q in range(n_chunks):
                mine = pl.ds((c * n_chunks + q) * ch, ch)
                theirs = pl.ds(((1 - c) * n_chunks + q) * ch, ch)
                s = n_chunks * m + q
                sends.append(_remote(in_refs[m].at[mine, :], out_refs[m].at[mine, :], send_sems, recv_sems, s, (x, y, 1 - c)))
                recvs.append(_remote(in_refs[m].at[theirs, :], out_refs[m].at[theirs, :], send_sems, recv_sems, s, (x, y, c)))
        for cp in sends:
            cp.start()
        for cp in recvs:
            cp.wait_recv()
        for cp in sends:
            cp.wait_send()

    n_sems = n_chunks * nm
    return pl.pallas_call(
        body, name="reunite_sibling", in_specs=[HBM_SPEC] * nm, out_specs=[HBM_SPEC] * nm,
        out_shape=[jax.ShapeDtypeStruct(s, F32) for s in shapes],
        input_output_aliases={m: m for m in range(nm)},
        scratch_shapes=[pltpu.SemaphoreType.DMA((n_sems,)), pltpu.SemaphoreType.DMA((n_sems,))],
    )(*fs)


def _sum_vecs(v):
    def body(v_ref, o_ref):
        acc = v_ref[0]
        for d in range(1, 8):
            acc = acc + v_ref[d]
        o_ref[...] = acc

    return pl.pallas_call(body, name="sum_vecs", out_shape=jax.ShapeDtypeStruct(v.shape[1:], F32))(v)


def kernel(x, w_in, gate_up, gate_bias, gla_norm_g, w_out, ln1_g, ln1_b, w_up, conv_w, conv_b, w_down, ln2_g, ln2_b, loss_target, m_w_in, m_gate_up, m_gate_bias, m_gla_norm_g, m_w_out, m_ln1_g, m_ln1_b, m_w_up, m_conv_w, m_conv_b, m_w_down, m_ln2_g, m_ln2_b, v_w_in, v_gate_up, v_gate_bias, v_gla_norm_g, v_w_out, v_ln1_g, v_ln1_b, v_w_up, v_conv_w, v_conv_b, v_w_down, v_ln2_g, v_ln2_b):
    xi, yi, ci = lax.axis_index("x"), lax.axis_index("y"), lax.axis_index("c")
    k_me = 2 * xi + yi
    c_arr = ci.astype(jnp.int32).reshape(1)
    kc_arr = jnp.stack([k_me, ci]).astype(jnp.int32)
    small = _pad_rows(jnp.concatenate([_rows(conv_w[0]), _rows(gate_up[0])], axis=0), SMALL_ROWS)
    w_in4, gsmall = _gather_weights([w_in[0].astype(BF16)], small, SHARD_SHAPES[:1])
    late_shards = [w_up[0].astype(BF16), w_out[0].astype(BF16), w_down[0].astype(BF16)]
    w_in_p = jnp.pad(jnp.concatenate([w_in4[k] for k in range(4)], axis=1), ((0, 0), (0, IN_PAD - IN_WIDTH)))
    conv_w_f = jnp.concatenate([gsmall[k, :5].reshape(-1)[:3 * W_UP_S].reshape(3, W_UP_S) for k in range(4)], axis=1)
    gate_up_f = jnp.concatenate([gsmall[k, 5].reshape(GATE_RANK, GLA_KEYS // 4) for k in range(4)], axis=1)
    conv_w3 = jnp.transpose(conv_w_f.reshape(3, 2, D_FF), (1, 0, 2))
    conv_b3 = conv_b.reshape(2, 1, D_FF)
    gate_up_pad = jnp.pad(gate_up_f, ((0, LANES - GATE_RANK), (0, 0)))

    dx, sums, vecs = _local_step(
        x[0], loss_target[0], w_in_p, late_shards, gate_up_pad, gate_bias, gla_norm_g, ln1_g, ln1_b, conv_w3, conv_b3,
        ln2_g, ln2_b, c_arr, kc_arr)
    g_w_in, g_w_up, g_w_out, g_w_down = _reunite_sibling(sums, SHARD_SHAPES)
    gsmall_sum = _unpack_vec(_sum_vecs(vecs), SMALL_GRADS)
    g_conv_w = lax.dynamic_slice_in_dim(gsmall_sum["conv_w"].reshape(3, 2 * D_FF), k_me * W_UP_S, W_UP_S, axis=1)
    g_gate_up = lax.dynamic_slice_in_dim(gsmall_sum["gate_up"].reshape(GATE_RANK, GLA_KEYS), k_me * (GLA_KEYS // 4),
                                         GLA_KEYS // 4, axis=1)
    gv = gsmall_sum
    loss = gv["loss"][0, 0]
    gvec = _pack_vec(gv, VEC_SIZES, VEC_ROWS)

    grads = dict(w_in=g_w_in[None], gate_up=g_gate_up[None], gate_bias=gv["gate_bias"], gla_norm_g=gv["gla_norm_g"],
                 w_out=g_w_out[None], ln1_g=gv["ln1_g"], ln1_b=gv["ln1_b"], w_up=g_w_up[None], conv_w=g_conv_w[None],
                 conv_b=gv["conv_b"], w_down=g_w_down[None], ln2_g=gv["ln2_g"], ln2_b=gv["ln2_b"])
    weights = dict(w_in=w_in, gate_up=gate_up, gate_bias=gate_bias, gla_norm_g=gla_norm_g, w_out=w_out, ln1_g=ln1_g,
                   ln1_b=ln1_b, w_up=w_up, conv_w=conv_w, conv_b=conv_b, w_down=w_down, ln2_g=ln2_g, ln2_b=ln2_b)
    ms = dict(w_in=m_w_in, gate_up=m_gate_up, gate_bias=m_gate_bias, gla_norm_g=m_gla_norm_g, w_out=m_w_out, ln1_g=m_ln1_g,
              ln1_b=m_ln1_b, w_up=m_w_up, conv_w=m_conv_w, conv_b=m_conv_b, w_down=m_w_down, ln2_g=m_ln2_g, ln2_b=m_ln2_b)
    vs = dict(w_in=v_w_in, gate_up=v_gate_up, gate_bias=v_gate_bias, gla_norm_g=v_gla_norm_g, w_out=v_w_out, ln1_g=v_ln1_g,
              ln1_b=v_ln1_b, w_up=v_w_up, conv_w=v_conv_w, conv_b=v_conv_b, w_down=v_w_down, ln2_g=v_ln2_g, ln2_b=v_ln2_b)
    names = ["w_in", "gate_up", "gate_bias", "gla_norm_g", "w_out", "ln1_g", "ln1_b", "w_up", "conv_w", "conv_b", "w_down",
             "ln2_g", "ln2_b"]
    delta, new_m, new_v = {}, {}, {}
    for n in ("w_in", "gate_up", "w_out", "w_up", "conv_w", "w_down"):
        tr_ = jnp.transpose if n == "w_in" else (lambda a: a)
        d, nm, nv = _adamw(tr_(weights[n][0]), tr_(grads[n][0]), tr_(ms[n][0]), tr_(vs[n][0]), "adamw_" + n)
        delta[n], new_m[n], new_v[n] = tr_(d)[None], tr_(nm)[None], tr_(nv)[None]
    d, nm, nv = _adamw(_pack_vec(weights, VEC_SIZES, VEC_ROWS), gvec, _pack_vec(ms, VEC_SIZES, VEC_ROWS),
                       _pack_vec(vs, VEC_SIZES, VEC_ROWS), "adamw_vectors")
    for dst, src in ((delta, d), (new_m, nm), (new_v, nv)):
        dst.update(_unpack_vec(src, VEC_SIZES))
    return (loss, dx[None], *[grads[n] for n in names], *[delta[n] for n in names], *[new_m[n] for n in names],
            *[new_v[n] for n in names])
```

```python
import functools
import math

import jax
import jax.numpy as jnp
from jax import lax
from jax.experimental import pallas as pl
from jax.experimental.pallas import tpu as pltpu

F32 = jnp.float32
BF16 = jnp.bfloat16

D_MODEL = 1024
SB_WIDTH = 512
GLA_KEYS = 256
GLA_WIDTH = 512
GATE_RANK = 16
IN_WIDTH = 3088
IN_PAD = 3200
D_FF = 2816
CHUNK = 64
LN_EPS = 1e-5
RMS_EPS = 1e-6
ALPHA = 2.0 ** 0.25
GLA_TAU = 16.0
SB_SCALE = 0.125
GLA_SCALE = 0.125
LANES = 128
SB_BLK = 256
SB_CUT = -100.0
GLA_ROWS = 256
VMEM_LIMIT = 56 * 1024 * 1024

ADAM_LR, ADAM_B1, ADAM_B2, ADAM_EPS, ADAM_WD, ADAM_STEP = 0.001, 0.9, 0.999, 1e-08, 0.01, 10

MESH = pl.DeviceIdType.MESH


def _cparams(*sem):
    return pltpu.CompilerParams(dimension_semantics=sem, vmem_limit_bytes=VMEM_LIMIT)


def _dot(a, b):
    return jnp.dot(a, b, preferred_element_type=F32)


def _dot_nt(a, b):
    return lax.dot_general(a, b, (((1,), (1,)), ((), ())), preferred_element_type=F32)


def _dot_tn(a, b):
    return lax.dot_general(a, b, (((0,), (0,)), ((), ())), preferred_element_type=F32)


def _split3(x):
    hi = x.astype(BF16)
    r = x - hi.astype(F32)
    mid = r.astype(BF16)
    lo = (r - mid.astype(F32)).astype(BF16)
    return hi, mid, lo


def _softplus(z):
    return jnp.maximum(z, 0.0) + jnp.log(1.0 + jnp.exp(-jnp.abs(z)))


def _sigmoid(z):
    return 1.0 / (1.0 + jnp.exp(-z))


def _mm(a, b, *, m, n, k, tm, tn, tk, ta=False, tb=False, a_spec=None, b_spec=None, o_spec=None,
        out_shape=None, out_dtype=F32, add=None, add_scale=1.0, rider=None, name):
    nk = k // tk
    dn = (((0 if ta else 1,), (1 if tb else 0,)), ((), ()))
    n_in = len(rider.inputs) if rider else 0
    n_out = len(rider.out_shapes) if rider else 0
    n_add = int(add is not None)
    steps = (m // tm, n // tn, nk)

    def body(*refs):
        a_ref, b_ref = refs[:2]
        add_ref = refs[2] if add is not None else None
        rin = refs[2 + n_add:2 + n_add + n_in]
        o_ref = refs[2 + n_add + n_in]
        rout = refs[3 + n_add + n_in:3 + n_add + n_in + n_out]
        scratch = refs[3 + n_add + n_in + n_out:]
        if rider:
            sems = scratch[len(scratch) - len(rider.sems):]
            ids = [pl.program_id(d) for d in range(3)]
            _ride(rider, rin, rout, sems, (ids[0] == 0) & (ids[1] == 0) & (ids[2] == 0),
                  (ids[0] == steps[0] - 1) & (ids[1] == steps[1] - 1) & (ids[2] == steps[2] - 1))

        part = lax.dot_general(a_ref[...].astype(BF16), b_ref[...].astype(BF16), dn, preferred_element_type=F32)

        def finish(r):
            if add is not None:
                r = r + add_scale * add_ref[...]
            o_ref[...] = r.astype(out_dtype)

        if nk == 1:
            finish(part)
            return
        acc_ref = scratch[0]
        kk = pl.program_id(2)

        @pl.when(kk == 0)
        def _():
            acc_ref[...] = part

        @pl.when((kk > 0) & (kk < nk - 1))
        def _():
            acc_ref[...] += part

        @pl.when(kk == nk - 1)
        def _():
            finish(acc_ref[...] + part)

    if a_spec is None:
        a_spec = pl.BlockSpec((tk, tm), lambda i, j, kk: (kk, i)) if ta else pl.BlockSpec((tm, tk), lambda i, j, kk: (i, kk))
    if b_spec is None:
        b_spec = pl.BlockSpec((tn, tk), lambda i, j, kk: (j, kk)) if tb else pl.BlockSpec((tk, tn), lambda i, j, kk: (kk, j))
    if o_spec is None:
        o_spec = pl.BlockSpec((tm, tn), lambda i, j, kk: (i, j))
    if out_shape is None:
        out_shape = jax.ShapeDtypeStruct((m, n), out_dtype)
    in_specs = [a_spec, b_spec]
    args = [a, b]
    if add is not None:
        in_specs.append(pl.BlockSpec((tm, tn), lambda i, j, kk: (i, j)))
        args.append(add)
    scratch = [pltpu.VMEM((tm, tn), F32)] if nk > 1 else []
    if not rider:
        return pl.pallas_call(
            body, name=name, grid=steps, in_specs=in_specs, out_specs=o_spec, out_shape=out_shape,
            scratch_shapes=scratch, compiler_params=_cparams("parallel", "parallel", "arbitrary"),
        )(*args)
    return pl.pallas_call(
        body, name=name, grid=steps, in_specs=in_specs + [HBM_SPEC] * n_in, out_specs=[o_spec] + [HBM_SPEC] * n_out,
        out_shape=[out_shape] + list(rider.out_shapes),
        input_output_aliases={len(args) + i: 1 + o for i, o in rider.aliases},
        scratch_shapes=scratch + [pltpu.SemaphoreType.DMA((s,)) for s in rider.sems],
        compiler_params=_cparams("arbitrary", "arbitrary", "arbitrary"),
    )(*args, *rider.inputs)


def _col_offsets(pieces):
    offs, o = [], 0
    for a in pieces:
        offs.append(o)
        o += a.shape[1]
    return offs


def _dx(pieces, w_in_p, dr1, t, rider, tm=512):
    offs = _col_offsets(pieces)
    npc = len(pieces)
    n_in, n_out = len(rider.inputs), len(rider.out_shapes)
    steps = t // tm

    def body(*refs):
        p_refs, w_ref, add_ref = refs[:npc], refs[npc], refs[npc + 1]
        rin = refs[npc + 2:npc + 2 + n_in]
        o_ref = refs[npc + 2 + n_in]
        rout = refs[npc + 3 + n_in:npc + 3 + n_in + n_out]
        i = pl.program_id(0)
        _ride(rider, rin, rout, refs[npc + 3 + n_in + n_out:], i == 0, i == steps - 1)
        acc = ALPHA * add_ref[...]
        for p_ref, off in zip(p_refs, offs):
            acc = acc + _dot_nt(p_ref[...].astype(BF16), w_ref[:, off:off + p_ref.shape[1]])
        o_ref[...] = acc

    row = lambda w: pl.BlockSpec((tm, w), lambda i: (i, 0))
    return pl.pallas_call(
        body, name="dx", grid=(steps,),
        in_specs=[row(a.shape[1]) for a in pieces] + [pl.BlockSpec(w_in_p.shape, lambda i: (0, 0)), row(D_MODEL)]
        + [HBM_SPEC] * n_in,
        out_specs=[row(D_MODEL)] + [HBM_SPEC] * n_out,
        out_shape=[jax.ShapeDtypeStruct((t, D_MODEL), F32)] + rider.out_shapes,
        scratch_shapes=[pltpu.SemaphoreType.DMA((s,)) for s in rider.sems],
        compiler_params=_cparams("arbitrary"),
    )(*pieces, w_in_p, dr1, *rider.inputs)


def _dw_in(x, pieces, t, tm=512, tk=512):
    offs = _col_offsets(pieces)
    npc = len(pieces)
    n = offs[-1] + pieces[-1].shape[1]
    nk = t // tk

    def body(*refs):
        x_ref, p_refs, o_ref, acc_ref = refs[0], refs[1:1 + npc], refs[1 + npc], refs[2 + npc]
        kk = pl.program_id(1)
        xb = x_ref[...].astype(BF16)
        for p_ref, off in zip(p_refs, offs):
            cols = slice(off, off + p_ref.shape[1])
            part = _dot_tn(xb, p_ref[...].astype(BF16))
            if nk == 1:
                o_ref[:, cols] = part
                continue

            @pl.when(kk == 0)
            def _():
                acc_ref[:, cols] = part

            @pl.when((kk > 0) & (kk < nk - 1))
            def _():
                acc_ref[:, cols] += part

            @pl.when(kk == nk - 1)
            def _():
                o_ref[:, cols] = acc_ref[:, cols] + part

    return pl.pallas_call(
        body, name="dw_in", grid=(D_MODEL // tm, nk),
        in_specs=[pl.BlockSpec((tk, tm), lambda i, kk: (kk, i))]
        + [pl.BlockSpec((tk, a.shape[1]), lambda i, kk: (kk, 0)) for a in pieces],
        out_specs=pl.BlockSpec((tm, n), lambda i, kk: (i, 0)),
        out_shape=jax.ShapeDtypeStruct((D_MODEL, n), F32),
        scratch_shapes=[pltpu.VMEM((tm, n), F32)],
        compiler_params=_cparams("parallel", "arbitrary"),
    )(x, *pieces)


class _Rider:
    def __init__(self, inputs, out_shapes, sems, copies, aliases=()):
        self.inputs, self.out_shapes, self.sems, self.copies = list(inputs), list(out_shapes), tuple(sems), copies
        self.aliases = tuple(aliases)


def _join(a, b):
    na_in, na_out, na_sems = len(a.inputs), len(a.out_shapes), len(a.sems)

    def copies(ins, outs, sems):
        first = a.copies(ins[:na_in], outs[:na_out], sems[:na_sems])
        second = b.copies(ins[na_in:], outs[na_out:], sems[na_sems:])
        return tuple(u + v for u, v in zip(first, second))

    return _Rider(a.inputs + b.inputs, a.out_shapes + b.out_shapes, a.sems + b.sems, copies,
                  a.aliases + tuple((i + na_in, o + na_out) for i, o in b.aliases))


def _ride(rider, in_refs, out_refs, sems, first, last):
    @pl.when(first)
    def _():
        local, sends, _ = rider.copies(in_refs, out_refs, sems)
        for cp in local + sends:
            cp.start()

    @pl.when(last)
    def _():
        local, sends, recvs = rider.copies(in_refs, out_refs, sems)
        for cp in recvs:
            cp.wait_recv()
        for cp in sends:
            cp.wait_send()
        for cp in local:
            cp.wait()


def _run(rider, name):
    n_in, n_out = len(rider.inputs), len(rider.out_shapes)

    def body(*refs):
        local, sends, recvs = rider.copies(refs[:n_in], refs[n_in:n_in + n_out], refs[n_in + n_out:])
        for cp in local + sends:
            cp.start()
        for cp in recvs:
            cp.wait_recv()
        for cp in sends:
            cp.wait_send()
        for cp in local:
            cp.wait()

    return pl.pallas_call(
        body, name=name, in_specs=[HBM_SPEC] * n_in, out_specs=[HBM_SPEC] * n_out, out_shape=list(rider.out_shapes),
        scratch_shapes=[pltpu.SemaphoreType.DMA((s,)) for s in rider.sems],
    )(*rider.inputs)


def _sb_tile(qh, kj, diag, strict, u_strict, r_in):
    z = _dot_nt(qh, kj)
    sp = _softplus(z)
    l1m = -sp
    lsz = z - sp
    if diag:
        l1m = jnp.where(strict, l1m, 0.0)
    cs = _dot(l1m.astype(BF16), u_strict) + r_in
    w = jnp.exp(lsz + cs)
    if diag:
        w = jnp.where(strict, w, 0.0)
    return l1m, lsz, w


def _sb_consts():
    row = lax.broadcasted_iota(jnp.int32, (SB_BLK, SB_BLK), 0)
    col = lax.broadcasted_iota(jnp.int32, (SB_BLK, SB_BLK), 1)
    strict = col < row
    u_strict = (row > col).astype(BF16)
    u_pre = (row < col).astype(BF16)
    lane = lax.broadcasted_iota(jnp.int32, (1, LANES), 1)
    return strict, u_strict, u_pre, lane


def _sb_fwd(proj, t, rider):
    nq = t // SB_BLK
    n_in, n_out = len(rider.inputs), len(rider.out_shapes)

    def body(q_ref, k_ref, v_ref, *rest):
        rin, o_ref, sv_ref, rout = rest[:n_in], rest[n_in], rest[n_in + 1], rest[n_in + 2:n_in + 2 + n_out]
        p = pl.program_id(0)
        i = pl.program_id(1)
        _ride(rider, rin, rout, rest[n_in + 2 + n_out:], (p == 0) & (i == 0), (p == 3) & (i == nq - 1))

        strict, u_strict, _, lane = _sb_consts()
        qf = q_ref[...] * SB_SCALE
        hms = [(lane // 64) == hh for hh in range(2)]
        qhs = [jnp.where(hm, qf, 0.0).astype(BF16) for hm in hms]

        def step(j, r0, r1, a, sv, diag, keep=None):
            rows = pl.ds(pl.multiple_of(j * SB_BLK, SB_BLK), SB_BLK)
            kj = k_ref[rows, :].astype(BF16)
            vf = v_ref[rows, :]
            rs = []
            for hh, r in enumerate((r0, r1)):
                l1m, _, w = _sb_tile(qhs[hh], kj, diag, strict, u_strict, r)
                pv = _dot(w.astype(BF16), jnp.where(hms[hh], vf, 0.0).astype(BF16))
                mass = jnp.sum(l1m, axis=1, keepdims=True)
                if keep is not None:
                    pv, mass = jnp.where(keep, pv, 0.0), jnp.where(keep, mass, 0.0)
                a = a + pv
                rs.append(r + mass)
                sv = jnp.where(lane == hh * 64 + (i - j + 1), rs[hh], sv)
            return rs[0], rs[1], a, sv

        zero = jnp.zeros((SB_BLK, 1), F32)
        acc0 = jnp.zeros((SB_BLK, LANES), F32)
        r0, r1, acc, sv = step(i, zero, zero, acc0, acc0, True)
        r0, r1, acc, sv = step(jnp.maximum(i - 1, 0), r0, r1, acc, sv, False, keep=i > 0)
        _, _, _, acc, sv = lax.while_loop(
            lambda c: (c[0] >= 0) & (jnp.maximum(jnp.max(c[1]), jnp.max(c[2])) > SB_CUT),
            lambda c: (c[0] - 1,) + step(c[0], c[1], c[2], c[3], c[4], False),
            (i - 2, r0, r1, acc, sv))
        o_ref[...] = acc.astype(BF16)
        sv_ref[...] = sv

    return pl.pallas_call(
        body, name="sb_fwd", grid=(4, nq),
        in_specs=[pl.BlockSpec((SB_BLK, LANES), lambda p, i: (i, p)),
                  pl.BlockSpec((t, LANES), lambda p, i: (0, 4 + p)),
                  pl.BlockSpec((t, LANES), lambda p, i: (0, 8 + p))] + [HBM_SPEC] * n_in,
        out_specs=[pl.BlockSpec((SB_BLK, LANES), lambda p, i: (i, p))] * 2 + [HBM_SPEC] * n_out,
        out_shape=[jax.ShapeDtypeStruct((t, SB_WIDTH), BF16), jax.ShapeDtypeStruct((t, SB_WIDTH), F32)] + rider.out_shapes,
        scratch_shapes=[pltpu.SemaphoreType.DMA((s,)) for s in rider.sems],
        compiler_params=_cparams("arbitrary", "arbitrary"),
    )(proj, proj, proj, *rider.inputs)


def _sb_bwd(proj, dcat, mass, t, rider):
    nq = t // SB_BLK
    n_in, n_out = len(rider.inputs), len(rider.out_shapes)

    def body(q_ref, k_ref, v_ref, do_ref, sv_ref, *rest):
        rin = rest[:n_in]
        dq_ref, dk_ref, dv_ref = rest[n_in:n_in + 3]
        rout = rest[n_in + 3:n_in + 3 + n_out]
        p = pl.program_id(0)
        i = pl.program_id(1)
        _ride(rider, rin, rout, rest[n_in + 3 + n_out:], (p == 0) & (i == 0), (p == 3) & (i == nq - 1))

        @pl.when(i == 0)
        def _():
            dk_ref[...] = jnp.zeros_like(dk_ref)
            dv_ref[...] = jnp.zeros_like(dv_ref)

        strict, u_strict, u_pre, lane = _sb_consts()
        qf = q_ref[...] * SB_SCALE
        dof = do_ref[...]
        hms = [(lane // 64) == hh for hh in range(2)]
        qhs = [jnp.where(hm, qf, 0.0).astype(BF16) for hm in hms]
        dohs = [jnp.where(hm, dof, 0.0).astype(BF16) for hm in hms]

        sv = sv_ref[...]
        zero = jnp.zeros((SB_BLK, 1), F32)

        def mass_right(hh, d):
            return jnp.sum(jnp.where(lane == hh * 64 + d, sv, 0.0), axis=1, keepdims=True)

        dstop = lax.while_loop(
            lambda d: (i - d >= 0) & (jnp.maximum(jnp.max(mass_right(0, d)), jnp.max(mass_right(1, d))) > SB_CUT),
            lambda d: d + 1, 1)
        jstop = i - dstop

        def step(j, carry, diag, keep=None):
            pre_g0, pre_g1, dqa = carry
            rows = pl.ds(pl.multiple_of(j * SB_BLK, SB_BLK), SB_BLK)
            kf = k_ref[rows, :]
            kj = kf.astype(BF16)
            vj = v_ref[rows, :].astype(BF16)
            dv = jnp.zeros((SB_BLK, LANES), F32)
            dk = jnp.zeros((SB_BLK, LANES), F32)
            dqj = jnp.zeros((SB_BLK, LANES), F32)
            pre = []
            for hh, pre_g in enumerate((pre_g0, pre_g1)):
                _, lsz, w = _sb_tile(qhs[hh], kj, diag, strict, u_strict, zero if diag else mass_right(hh, i - j))
                g = w * _dot_nt(dohs[hh], vj)
                gpre = _dot(g.astype(BF16), u_pre) + pre_g
                sig = jnp.exp(lsz)
                dz = g * (1.0 - sig) - gpre * sig
                if diag:
                    dz = jnp.where(strict, dz, 0.0)
                dzb = dz.astype(BF16)
                dv = dv + _dot_tn(w.astype(BF16), dohs[hh])
                dk = dk + _dot_tn(dzb, qhs[hh])
                dqj = dqj + _dot(dzb, jnp.where(hms[hh], kf, 0.0).astype(BF16))
                gsum = jnp.sum(g, axis=1, keepdims=True)
                pre.append(pre_g + (gsum if keep is None else jnp.where(keep, gsum, 0.0)))
            if keep is not None:
                dv, dk, dqj = jnp.where(keep, dv, 0.0), jnp.where(keep, dk, 0.0), jnp.where(keep, dqj, 0.0)
            dv_ref[rows, :] += dv
            dk_ref[rows, :] += dk
            return pre[0], pre[1], dqa + dqj

        carry = lax.fori_loop(jstop + 1, i - 1, lambda j, c: step(j, c, False), (zero, zero, jnp.zeros((SB_BLK, LANES), F32)))
        carry = step(jnp.maximum(i - 1, 0), carry, False, keep=i > 0)
        _, _, dq = step(i, carry, True)
        dq_ref[...] = (dq * SB_SCALE).astype(BF16)

    return pl.pallas_call(
        body, name="sb_bwd", grid=(4, nq),
        in_specs=[pl.BlockSpec((SB_BLK, LANES), lambda p, i: (i, p)),
                  pl.BlockSpec((t, LANES), lambda p, i: (0, 4 + p)),
                  pl.BlockSpec((t, LANES), lambda p, i: (0, 8 + p)),
                  pl.BlockSpec((SB_BLK, LANES), lambda p, i: (i, p)),
                  pl.BlockSpec((SB_BLK, LANES), lambda p, i: (i, p))] + [HBM_SPEC] * n_in,
        out_specs=[pl.BlockSpec((SB_BLK, LANES), lambda p, i: (i, p)),
                   pl.BlockSpec((t, LANES), lambda p, i: (0, p)),
                   pl.BlockSpec((t, LANES), lambda p, i: (0, p))] + [HBM_SPEC] * n_out,
        out_shape=[jax.ShapeDtypeStruct((t, SB_WIDTH), BF16)] + [jax.ShapeDtypeStruct((t, SB_WIDTH), F32)] * 2
        + rider.out_shapes,
        scratch_shapes=[pltpu.SemaphoreType.DMA((s,)) for s in rider.sems],
        compiler_params=_cparams("arbitrary", "arbitrary"),
    )(proj, proj, proj, dcat, mass, *rider.inputs)


def _gla_consts():
    r = lax.broadcasted_iota(jnp.int32, (GLA_ROWS, GLA_ROWS), 0)
    c = lax.broadcasted_iota(jnp.int32, (GLA_ROWS, GLA_ROWS), 1)
    same = (r // CHUNK) == (c // CHUNK)
    causal = same & (c <= r)
    upto_mid = c % CHUNK <= CHUNK // 2 - 1
    fwd_stack = jnp.concatenate([causal, same & upto_mid, same], axis=0).astype(BF16)
    bwd_stack = jnp.concatenate([same & (c >= r), same & (r % CHUNK <= CHUNK // 2 - 1), same], axis=1).astype(BF16)
    rowid = lax.broadcasted_iota(jnp.int32, (GLA_ROWS, 1), 0)
    lane = lax.broadcasted_iota(jnp.int32, (1, LANES), 1)
    sr = lax.broadcasted_iota(jnp.int32, (LANES, 2 * LANES), 0)
    sc = lax.broadcasted_iota(jnp.int32, (LANES, 2 * LANES), 1)
    blockdiag = (sr // 64) == (sc // LANES)
    return causal, fwd_stack, bwd_stack, rowid, lane, blockdiag


def _dot3(u, x):
    hi, mid, lo = _split3(x)
    return _dot(u, hi) + _dot(u, mid) + _dot(u, lo)


def _row_to_col(row):
    return jnp.transpose(jnp.broadcast_to(row, (LANES, LANES)))


def _gla_gates(ga_ref, gup_ref, gbias_ref):
    pre = _dot(ga_ref[...].astype(BF16), gup_ref[...].astype(BF16)) + gbias_ref[...]
    log_a = (jnp.minimum(pre, 0.0) - jnp.log(1.0 + jnp.exp(-jnp.abs(pre)))) / GLA_TAU
    return pre, log_a


def _gla_terms(g2, q2, k2, fwd_stack):
    bs = _dot3(fwd_stack, g2)
    b, b_ref, b_last = bs[:GLA_ROWS], bs[GLA_ROWS:2 * GLA_ROWS], bs[2 * GLA_ROWS:]
    qs = q2 * GLA_SCALE
    e_q = jnp.exp(b - b_ref)
    e_k = jnp.exp(b_ref - b)
    e_d = jnp.exp(b_last - b)
    e_b = jnp.exp(b)
    decay = jnp.exp(b_last)
    return dict(qs=qs, e_q=e_q, e_k=e_k, e_d=e_d, e_b=e_b, decay=decay,
                qi=qs * e_q, ki=k2 * e_k, kd=k2 * e_d, qb=qs * e_b)


def _gla_fwd(proj, gate_up_pad, gate_bias, gnorm, t, rider):
    nsteps = t // GLA_ROWS
    cps = GLA_ROWS // CHUNK
    n_in, n_out = len(rider.inputs), len(rider.out_shapes)

    def body(q_ref, k_ref, v_ref, gg_ref, ga_ref, gup_ref, gbias_ref, gn_ref, *rest):
        o_ref, oraw_ref, st_ref = rest[n_in:n_in + 3]
        s_scr = rest[n_in + 3 + n_out]
        i = pl.program_id(0)
        _ride(rider, rest[:n_in], rest[n_in + 3:n_in + 3 + n_out], rest[n_in + 4 + n_out:], i == 0, i == nsteps - 1)

        @pl.when(i == 0)
        def _():
            s_scr[...] = jnp.zeros_like(s_scr)

        _, fwd_stack, _, _, lane, blockdiag = _gla_consts()
        causal64 = (lax.broadcasted_iota(jnp.int32, (CHUNK, CHUNK), 1) <= lax.broadcasted_iota(jnp.int32, (CHUNK, CHUNK), 0))
        _, log_a = _gla_gates(ga_ref, gup_ref, gbias_ref)
        gn = gn_ref[...]
        pre = []
        for p in range(2):
            kl = slice(p * LANES, (p + 1) * LANES)
            tm = _gla_terms(log_a[:, kl], q_ref[:, kl], k_ref[:, kl], fwd_stack)
            pre.append(dict(
                v2b=v_ref[:, p * 2 * LANES:(p + 1) * 2 * LANES].astype(BF16), kib=tm["ki"].astype(BF16),
                kdb=tm["kd"].astype(BF16), qbb=tm["qb"].astype(BF16), decay=tm["decay"],
                qihb=[jnp.where((lane // 64) == hh, tm["qi"], 0.0).astype(BF16) for hh in range(2)]))
        state = [s_scr[0], s_scr[1]]
        outs = [[], []]
        for cc in range(cps):
            rows = slice(cc * CHUNK, (cc + 1) * CHUNK)
            for p in range(2):
                w = pre[p]
                st_ref[cc, p] = state[p]
                intra = []
                for hh in range(2):
                    a = jnp.where(causal64, _dot_nt(w["qihb"][hh][rows], w["kib"][rows]), 0.0)
                    intra.append(_dot(a.astype(BF16), w["v2b"][rows, hh * LANES:(hh + 1) * LANES]))
                outs[p].append(jnp.concatenate(intra, axis=1) + _dot(w["qbb"][rows], state[p].astype(BF16)))
                upd = jnp.where(blockdiag, _dot_tn(w["kdb"][rows], w["v2b"][rows]), 0.0)
                dcol = _row_to_col(w["decay"][cc * CHUNK:cc * CHUNK + 1])
                state[p] = state[p] * jnp.concatenate([dcol, dcol], axis=1) + upd
        for p in range(2):
            s_scr[p] = state[p]
            vl = slice(p * 2 * LANES, (p + 1) * 2 * LANES)
            o2 = jnp.concatenate(outs[p], axis=0)
            oraw_ref[:, vl] = o2
            for hh in range(2):
                oh = o2[:, hh * LANES:(hh + 1) * LANES]
                gl = slice(p * 2 * LANES + hh * LANES, p * 2 * LANES + (hh + 1) * LANES)
                rinv = lax.rsqrt(jnp.mean(oh * oh, axis=1, keepdims=True) + RMS_EPS)
                gg = gg_ref[:, gl]
                o_ref[:, gl] = (oh * rinv * gn * (gg * _sigmoid(gg))).astype(BF16)

    cb = lambda w, idx: pl.BlockSpec((GLA_ROWS, w), lambda i: (i, idx))
    full = lambda shp: pl.BlockSpec(shp, lambda i: tuple(0 for _ in shp))
    return pl.pallas_call(
        body, name="gla_fwd", grid=(nsteps,),
        in_specs=[cb(256, 6), cb(256, 7), cb(512, 4), cb(512, 5), cb(128, 24),
                  full((LANES, GLA_KEYS)), full((1, GLA_KEYS)), full((1, LANES))] + [HBM_SPEC] * n_in,
        out_specs=[pl.BlockSpec((GLA_ROWS, GLA_WIDTH), lambda i: (i, 0)),
                   pl.BlockSpec((GLA_ROWS, GLA_WIDTH), lambda i: (i, 0)),
                   pl.BlockSpec((cps, 2, LANES, 2 * LANES), lambda i: (i, 0, 0, 0))] + [HBM_SPEC] * n_out,
        out_shape=[jax.ShapeDtypeStruct((t, GLA_WIDTH), BF16), jax.ShapeDtypeStruct((t, GLA_WIDTH), F32),
                   jax.ShapeDtypeStruct((t // CHUNK, 2, LANES, 2 * LANES), F32)]
        + rider.out_shapes,
        input_output_aliases={8 + i: 3 + o for i, o in rider.aliases},
        scratch_shapes=[pltpu.VMEM((2, LANES, 2 * LANES), F32)] + [pltpu.SemaphoreType.DMA((s,)) for s in rider.sems],
        compiler_params=_cparams("arbitrary"),
    )(proj, proj, proj, proj, proj, gate_up_pad, gate_bias, gnorm, *rider.inputs)


def _gla_bwd(proj, dcat, oraw, states, gate_up_pad, gate_bias, gnorm, t):
    nsteps = t // GLA_ROWS
    cps = GLA_ROWS // CHUNK
    wout = 2 * GLA_KEYS + 2 * GLA_WIDTH + LANES

    def body(q_ref, k_ref, v_ref, gg_ref, ga_ref, do_ref, oraw_ref, st_ref, gup_ref, gbias_ref, gn_ref,
             d_ref, dgup_ref, dgbias_ref, dgn_ref, ds_scr):
        i = pl.program_id(0)

        @pl.when(i == 0)
        def _():
            ds_scr[...] = jnp.zeros_like(ds_scr)
            dgup_ref[...] = jnp.zeros_like(dgup_ref)
            dgbias_ref[...] = jnp.zeros_like(dgbias_ref)
            dgn_ref[...] = jnp.zeros_like(dgn_ref)

        causal, fwd_stack, bwd_stack, rowid, lane, blockdiag = _gla_consts()
        pre, log_a = _gla_gates(ga_ref, gup_ref, gbias_ref)
        gn = gn_ref[...]
        dgn = jnp.zeros((1, LANES), F32)
        dgs = []
        pre_p = []
        for p in range(2):
            kl = slice(p * LANES, (p + 1) * LANES)
            vl = slice(p * 2 * LANES, (p + 1) * 2 * LANES)
            tm = _gla_terms(log_a[:, kl], q_ref[:, kl], k_ref[:, kl], fwd_stack)
            v2b = v_ref[:, vl].astype(BF16)
            dos = []
            for hh in range(2):
                gl = slice(p * 2 * LANES + hh * LANES, p * 2 * LANES + (hh + 1) * LANES)
                oh = oraw_ref[:, gl]
                rinv = lax.rsqrt(jnp.mean(oh * oh, axis=1, keepdims=True) + RMS_EPS)
                on = oh * rinv
                gg = gg_ref[:, gl]
                sg = _sigmoid(gg)
                sil = gg * sg
                dgo = do_ref[:, gl]
                d_ref[:, 2 * GLA_KEYS + GLA_WIDTH + gl.start:2 * GLA_KEYS + GLA_WIDTH + gl.stop] = (
                    dgo * on * gn * (sg * (1.0 + gg * (1.0 - sg)))).astype(BF16)
                dgn = dgn + jnp.sum(dgo * sil * on, axis=0, keepdims=True)
                don = dgo * sil * gn
                dos.append(rinv * (don - on * jnp.mean(don * on, axis=1, keepdims=True)))
            do2b = jnp.concatenate(dos, axis=1).astype(BF16)
            qib = tm["qi"].astype(BF16)
            kib = tm["ki"].astype(BF16)
            kdb = tm["kd"].astype(BF16)
            qbb = tm["qb"].astype(BF16)
            dqi = jnp.zeros((GLA_ROWS, LANES), F32)
            dki = jnp.zeros((GLA_ROWS, LANES), F32)
            dvs = []
            for hh in range(2):
                hm = (lane // 64) == hh
                hl = slice(hh * LANES, (hh + 1) * LANES)
                a = jnp.where(causal, _dot_nt(jnp.where(hm, tm["qi"], 0.0).astype(BF16), kib), 0.0).astype(BF16)
                da = jnp.where(causal, _dot_nt(do2b[:, hl], v2b[:, hl]), 0.0).astype(BF16)
                dvs.append(_dot_tn(a, do2b[:, hl]))
                dqi = dqi + jnp.where(hm, _dot(da, kib), 0.0)
                dki = dki + jnp.where(hm, _dot_tn(da, qib), 0.0)
            pre_p.append(dict(tm=tm, v2b=v2b, do2b=do2b, kdb=kdb, qbb=qbb, dqi=dqi, dki=dki, dvs=dvs))
        dstate = [ds_scr[0], ds_scr[1]]
        rec = [dict(dv_st=[None] * cps, dqb=[None] * cps, dkd=[None] * cps, dd=[None] * cps) for _ in range(2)]
        for cc in reversed(range(cps)):
            rows = slice(cc * CHUNK, (cc + 1) * CHUNK)
            for p in range(2):
                w, ds2 = pre_p[p], dstate[p]
                s_prev = st_ref[cc, p]
                ds2b = ds2.astype(BF16)
                rec[p]["dv_st"][cc] = _dot(w["kdb"][rows], ds2b)
                rec[p]["dqb"][cc] = _dot_nt(w["do2b"][rows], s_prev.astype(BF16))
                rec[p]["dkd"][cc] = _dot_nt(w["v2b"][rows], ds2b)
                decay_row = w["tm"]["decay"][cc * CHUNK:cc * CHUNK + 1]
                ddecay_col = jnp.sum(ds2 * s_prev, axis=1, keepdims=True)
                ddecay_row = jnp.transpose(jnp.broadcast_to(ddecay_col, (LANES, LANES)))[0:1, :]
                rec[p]["dd"][cc] = jnp.broadcast_to(ddecay_row * decay_row, (CHUNK, LANES))
                dcol = _row_to_col(decay_row)
                dstate[p] = (jnp.where(blockdiag, _dot_tn(w["qbb"][rows], w["do2b"][rows]), 0.0)
                             + ds2 * jnp.concatenate([dcol, dcol], axis=1))
        for p in range(2):
            ds_scr[p] = dstate[p]
            tm, dqi, dki = pre_p[p]["tm"], pre_p[p]["dqi"], pre_p[p]["dki"]
            dv2 = jnp.concatenate(pre_p[p]["dvs"], axis=1) + jnp.concatenate(rec[p]["dv_st"], axis=0)
            dqb = jnp.concatenate(rec[p]["dqb"], axis=0)
            dkd = jnp.concatenate(rec[p]["dkd"], axis=0)
            dd = rec[p]["dd"]
            dqs = dqi * tm["e_q"] + dqb * tm["e_b"]
            dk = dki * tm["e_k"] + dkd * tm["e_d"]
            t_qi = dqi * tm["qi"]
            t_ki = dki * tm["ki"]
            t_kd = dkd * tm["kd"]
            db = t_qi - t_ki + dqb * tm["qb"] - t_kd
            to_mid = t_ki - t_qi
            to_last = t_kd + jnp.where(rowid % CHUNK == CHUNK - 1, jnp.concatenate(dd, axis=0), 0.0)
            dgs.append(_dot3(bwd_stack, jnp.concatenate([db, to_mid, to_last], axis=0)))
            d_ref[:, p * LANES:(p + 1) * LANES] = (dqs * GLA_SCALE).astype(BF16)
            d_ref[:, GLA_KEYS + p * LANES:GLA_KEYS + (p + 1) * LANES] = dk.astype(BF16)
            d_ref[:, 2 * GLA_KEYS + p * 2 * LANES:2 * GLA_KEYS + (p + 1) * 2 * LANES] = dv2.astype(BF16)
        dlog_a = jnp.concatenate(dgs, axis=1)
        dpre = dlog_a * (1.0 / GLA_TAU) * _sigmoid(-pre)
        dpb = dpre.astype(BF16)
        dgn_ref[...] += dgn
        dgbias_ref[...] += jnp.sum(dpre, axis=0, keepdims=True)
        dgup_ref[...] += _dot_tn(ga_ref[...].astype(BF16), dpb)
        d_ref[:, 2 * GLA_KEYS + 2 * GLA_WIDTH:] = _dot_nt(dpb, gup_ref[...].astype(BF16)).astype(BF16)

    rev = lambda i: nsteps - 1 - i
    cb = lambda w, idx: pl.BlockSpec((GLA_ROWS, w), lambda i: (rev(i), idx))
    full = lambda shp: pl.BlockSpec(shp, lambda i: tuple(0 for _ in shp))
    return pl.pallas_call(
        body, name="gla_bwd", grid=(nsteps,),
        in_specs=[cb(256, 6), cb(256, 7), cb(512, 4), cb(512, 5), cb(128, 24), cb(512, 1), cb(512, 0),
                  pl.BlockSpec((cps, 2, LANES, 2 * LANES), lambda i: (rev(i), 0, 0, 0)),
                  full((LANES, GLA_KEYS)), full((1, GLA_KEYS)), full((1, LANES))],
        out_specs=[pl.BlockSpec((GLA_ROWS, wout), lambda i: (rev(i), 0)),
                   full((LANES, GLA_KEYS)), full((1, GLA_KEYS)), full((1, LANES))],
        out_shape=[jax.ShapeDtypeStruct((t, wout), BF16), jax.ShapeDtypeStruct((LANES, GLA_KEYS), F32),
                   jax.ShapeDtypeStruct((1, GLA_KEYS), F32), jax.ShapeDtypeStruct((1, LANES), F32)],
        scratch_shapes=[pltpu.VMEM((2, LANES, 2 * LANES), F32)],
        compiler_params=_cparams("arbitrary"),
    )(proj, proj, proj, proj, proj, dcat, oraw, states, gate_up_pad, gate_bias, gnorm)


def _ln_stats(r):
    mu = jnp.mean(r, axis=1, keepdims=True)
    xc = r - mu
    rstd = lax.rsqrt(jnp.mean(xc * xc, axis=1, keepdims=True) + LN_EPS)
    return xc * rstd, rstd


def _ln_bwd(dy_g, xhat, rstd):
    return rstd * (dy_g - jnp.mean(dy_g, axis=1, keepdims=True) - xhat * jnp.mean(dy_g * xhat, axis=1, keepdims=True))


def _outproj_ln1(sb_o, gla_o, x, w_out, g1, b1, t, tm=256):
    def body(sb_ref, gl_ref, x_ref, w_ref, g_ref, b_ref, xhat_ref, rstd_ref, h_ref):
        mix = _dot(sb_ref[...], w_ref[0:SB_WIDTH, :]) + _dot(gl_ref[...], w_ref[SB_WIDTH:, :])
        xhat, rstd = _ln_stats(ALPHA * x_ref[...] + mix)
        xhat_ref[...] = xhat
        rstd_ref[...] = rstd
        h_ref[...] = (xhat * g_ref[...] + b_ref[...]).astype(BF16)

    row = lambda w: pl.BlockSpec((tm, w), lambda i: (i, 0))
    full = lambda shp: pl.BlockSpec(shp, lambda i: (0, 0))
    return pl.pallas_call(
        body, name="outproj_ln1", grid=(t // tm,),
        in_specs=[row(SB_WIDTH), row(GLA_WIDTH), row(D_MODEL), full((D_MODEL, D_MODEL)), full((1, D_MODEL)), full((1, D_MODEL))],
        out_specs=[row(D_MODEL), row(1), row(D_MODEL)],
        out_shape=[jax.ShapeDtypeStruct((t, D_MODEL), F32), jax.ShapeDtypeStruct((t, 1), F32),
                   jax.ShapeDtypeStruct((t, D_MODEL), BF16)],
        compiler_params=_cparams("parallel"),
    )(sb_o, gla_o, x, w_out, g1, b1)


_INV_SQRT2 = 1.0 / math.sqrt(2.0)
_INV_SQRT2PI = 1.0 / math.sqrt(2.0 * math.pi)


def _conv3(xs, w_ref, b_ref, half):
    return (w_ref[half, 0:1, :] * pltpu.roll(xs, 2, 0) + w_ref[half, 1:2, :] * pltpu.roll(xs, 1, 0)
            + w_ref[half, 2:3, :] * xs + b_ref[half])


HALO = 16


def _conv_gelu_fwd(up3, conv_w3, conv_b3, t, tr=512, ct=256):
    nct = D_FF // ct
    hb = tr // HALO

    def body(cur_ref, prev_ref, w_ref, b_ref, gm_ref):
        i = pl.program_id(1)
        keep = (i > 0).astype(F32)
        us = []
        for half in range(2):
            xs = jnp.concatenate([prev_ref[half].astype(F32) * keep, cur_ref[half].astype(F32)], axis=0)
            us.append(_conv3(xs, w_ref, b_ref, half)[HALO:, :])
        a, c = us
        gelu = 0.5 * a * (1.0 + lax.erf(a * _INV_SQRT2))
        gm_ref[...] = (gelu * c).astype(BF16)

    return pl.pallas_call(
        body, name="conv_gelu_fwd", grid=(nct, t // tr),
        in_specs=[pl.BlockSpec((2, tr, ct), lambda j, i: (0, i, j)),
                  pl.BlockSpec((2, HALO, ct), lambda j, i: (0, jnp.maximum(i * hb - 1, 0), j)),
                  pl.BlockSpec((2, 3, ct), lambda j, i: (0, 0, j)),
                  pl.BlockSpec((2, 1, ct), lambda j, i: (0, 0, j))],
        out_specs=pl.BlockSpec((tr, ct), lambda j, i: (i, j)),
        out_shape=jax.ShapeDtypeStruct((t, D_FF), BF16),
        compiler_params=_cparams("parallel", "parallel"),
    )(up3, up3, conv_w3, conv_b3)


def _conv_gelu_bwd(up3, dgm, conv_w3, conv_b3, t, tr=512, ct=256):
    nct = D_FF // ct
    nrt = t // tr
    hb = tr // HALO
    n = tr + 2 * HALO
    lo, hi = HALO, tr + HALO

    def body(cur_ref, prev_ref, next_ref, dg_ref, dgn_ref, w_ref, b_ref, dup_ref, dcw_ref, dcb_ref):
        i = pl.program_id(1)

        @pl.when(i == 0)
        def _():
            dcw_ref[...] = jnp.zeros_like(dcw_ref)
            dcb_ref[...] = jnp.zeros_like(dcb_ref)

        keep_prev = (i > 0).astype(F32)
        keep_next = (i < nrt - 1).astype(F32)
        xs, xm1, xm2, us = [], [], [], []
        for half in range(2):
            x = jnp.concatenate([prev_ref[half].astype(F32) * keep_prev, cur_ref[half].astype(F32),
                                 next_ref[half].astype(F32)], axis=0)
            xs.append(x)
            xm1.append(pltpu.roll(x, 1, 0))
            xm2.append(pltpu.roll(x, 2, 0))
            us.append(w_ref[half, 0:1, :] * xm2[half] + w_ref[half, 1:2, :] * xm1[half]
                      + w_ref[half, 2:3, :] * x + b_ref[half])
        a, c = us
        dg = jnp.concatenate([jnp.zeros((HALO, ct), F32), dg_ref[...].astype(F32),
                              dgn_ref[...].astype(F32) * keep_next], axis=0)
        cdf = 0.5 * (1.0 + lax.erf(a * _INV_SQRT2))
        pdf = jnp.exp(-0.5 * a * a) * _INV_SQRT2PI
        dus = [dg * c * (cdf + a * pdf), dg * (a * cdf)]
        rid = lax.broadcasted_iota(jnp.int32, (8, 1), 0)
        for half in range(2):
            du = dus[half]
            dup = (w_ref[half, 2:3, :] * du + w_ref[half, 1:2, :] * pltpu.roll(du, n - 1, 0)
                   + w_ref[half, 0:1, :] * pltpu.roll(du, n - 2, 0))
            dup_ref[half] = dup[lo:hi, :].astype(BF16)
            duc = du[lo:hi, :]
            s0 = jnp.sum(duc * xm2[half][lo:hi, :], axis=0, keepdims=True)
            s1 = jnp.sum(duc * xm1[half][lo:hi, :], axis=0, keepdims=True)
            s2 = jnp.sum(duc * xs[half][lo:hi, :], axis=0, keepdims=True)
            dcw_ref[half] += jnp.where(rid == 0, s0, jnp.where(rid == 1, s1, jnp.where(rid == 2, s2, 0.0)))
            dcb_ref[half] += jnp.sum(duc, axis=0, keepdims=True)

    last = t // HALO - 1
    return pl.pallas_call(
        body, name="conv_gelu_bwd", grid=(nct, nrt),
        in_specs=[pl.BlockSpec((2, tr, ct), lambda j, i: (0, i, j)),
                  pl.BlockSpec((2, HALO, ct), lambda j, i: (0, jnp.maximum(i * hb - 1, 0), j)),
                  pl.BlockSpec((2, HALO, ct), lambda j, i: (0, jnp.minimum((i + 1) * hb, last), j)),
                  pl.BlockSpec((tr, ct), lambda j, i: (i, j)),
                  pl.BlockSpec((HALO, ct), lambda j, i: (jnp.minimum((i + 1) * hb, last), j)),
                  pl.BlockSpec((2, 3, ct), lambda j, i: (0, 0, j)),
                  pl.BlockSpec((2, 1, ct), lambda j, i: (0, 0, j))],
        out_specs=[pl.BlockSpec((2, tr, ct), lambda j, i: (0, i, j)),
                   pl.BlockSpec((2, 8, ct), lambda j, i: (0, 0, j)),
                   pl.BlockSpec((2, 1, ct), lambda j, i: (0, 0, j))],
        out_shape=[jax.ShapeDtypeStruct((2, t, D_FF), BF16), jax.ShapeDtypeStruct((2, 8, D_FF), F32),
                   jax.ShapeDtypeStruct((2, 1, D_FF), F32)],
        compiler_params=_cparams("parallel", "arbitrary"),
    )(up3, up3, up3, dgm, dgm, conv_w3, conv_b3)


def _down_ln2_loss(gm, w_down, xhat1, g1, b1, g2, b2, target, t, tm=256):
    def body(gm_ref, w_ref, xh_ref, g1_ref, b1_ref, g2_ref, b2_ref, tg_ref, dr_ref, loss_ref, dg_ref, db_ref):
        i = pl.program_id(0)

        @pl.when(i == 0)
        def _():
            loss_ref[...] = jnp.zeros_like(loss_ref)
            dg_ref[...] = jnp.zeros_like(dg_ref)
            db_ref[...] = jnp.zeros_like(db_ref)

        h = xh_ref[...] * g1_ref[...] + b1_ref[...]
        xhat, rstd = _ln_stats(ALPHA * h + _dot(gm_ref[...], w_ref[...]))
        err = xhat * g2_ref[...] + b2_ref[...] - tg_ref[...]
        loss_ref[...] += 0.5 * jnp.sum(jnp.sum(err * err, axis=1, keepdims=True), axis=0, keepdims=True) / D_MODEL
        dy = err * (1.0 / D_MODEL)
        dg_ref[...] += jnp.sum(dy * xhat, axis=0, keepdims=True)
        db_ref[...] += jnp.sum(dy, axis=0, keepdims=True)
        dr_ref[...] = _ln_bwd(dy * g2_ref[...], xhat, rstd)

    row = lambda w: pl.BlockSpec((tm, w), lambda i: (i, 0))
    full = lambda shp: pl.BlockSpec(shp, lambda i: (0, 0))
    vec = full((1, D_MODEL))
    return pl.pallas_call(
        body, name="down_ln2_loss", grid=(t // tm,),
        in_specs=[row(D_FF), full((D_FF, D_MODEL)), row(D_MODEL), vec, vec, vec, vec, row(D_MODEL)],
        out_specs=[row(D_MODEL), full((1, 1)), vec, vec],
        out_shape=[jax.ShapeDtypeStruct((t, D_MODEL), F32), jax.ShapeDtypeStruct((1, 1), F32),
                   jax.ShapeDtypeStruct((1, D_MODEL), F32), jax.ShapeDtypeStruct((1, D_MODEL), F32)],
        compiler_params=_cparams("arbitrary"),
    )(gm, w_down, xhat1, g1, b1, g2, b2, target)


def _dh_ln1_bwd(dup3, w_up4, dr2, xhat1, rstd1, g1, t, tm=256):
    ws = 2 * D_FF // 4

    def body(a_ref, w_ref, dr2_ref, xh_ref, rs_ref, g_ref, dr1_ref, dg_ref, db_ref):
        i = pl.program_id(0)

        @pl.when(i == 0)
        def _():
            dg_ref[...] = jnp.zeros_like(dg_ref)
            db_ref[...] = jnp.zeros_like(db_ref)

        dh = ALPHA * dr2_ref[...]
        for s in range(4):
            dh = dh + _dot_nt(a_ref[s // 2, :, (s % 2) * ws:(s % 2 + 1) * ws], w_ref[s])
        xhat = xh_ref[...]
        dg_ref[...] += jnp.sum(dh * xhat, axis=0, keepdims=True)
        db_ref[...] += jnp.sum(dh, axis=0, keepdims=True)
        dr1_ref[...] = _ln_bwd(dh * g_ref[...], xhat, rs_ref[...])

    row = lambda w: pl.BlockSpec((tm, w), lambda i: (i, 0))
    vec = pl.BlockSpec((1, D_MODEL), lambda i: (0, 0))
    return pl.pallas_call(
        body, name="dh_ln1_bwd", grid=(t // tm,),
        in_specs=[pl.BlockSpec((2, tm, D_FF), lambda i: (0, i, 0)),
                  pl.BlockSpec((4, D_MODEL, ws), lambda i: (0, 0, 0)),
                  row(D_MODEL), row(D_MODEL), row(1), vec],
        out_specs=[row(D_MODEL), vec, vec],
        out_shape=[jax.ShapeDtypeStruct((t, D_MODEL), F32), jax.ShapeDtypeStruct((1, D_MODEL), F32),
                   jax.ShapeDtypeStruct((1, D_MODEL), F32)],
        compiler_params=_cparams("arbitrary"),
    )(dup3, w_up4, dr2, xhat1, rstd1, g1)


def _adamw(w, g, m, v, name):
    rows, cols = w.shape
    tr = rows
    for cand in (256, 128, 64, 32, 16, 8):
        if rows % cand == 0 and rows > cand:
            tr = cand
            break
    c1 = 1.0 / (1.0 - ADAM_B1 ** ADAM_STEP)
    c2 = 1.0 / (1.0 - ADAM_B2 ** ADAM_STEP)

    def body(w_ref, g_ref, m_ref, v_ref, d_ref, nm_ref, nv_ref):
        gv = g_ref[...]
        nm = ADAM_B1 * m_ref[...] + (1.0 - ADAM_B1) * gv
        nv = ADAM_B2 * v_ref[...] + (1.0 - ADAM_B2) * (gv * gv)
        d_ref[...] = -ADAM_LR * ((nm * c1) / (jnp.sqrt(nv * c2) + ADAM_EPS) + ADAM_WD * w_ref[...])
        nm_ref[...] = nm
        nv_ref[...] = nv

    spec = pl.BlockSpec((tr, cols), lambda i: (i, 0))
    out = jax.ShapeDtypeStruct((rows, cols), F32)
    return pl.pallas_call(
        body, name=name, grid=(rows // tr,), in_specs=[spec] * 4, out_specs=[spec] * 3, out_shape=[out] * 3,
        compiler_params=_cparams("parallel"),
    )(w, g, m, v)


def _local_step(x, target, w_in_p, late_shards, gate_up_pad, gate_bias, gnorm, ln1_g, ln1_b, conv_w3, conv_b3,
                ln2_g, ln2_b, c_arr, kc_arr):
    t = x.shape[0]
    tq = min(t, 1024)
    s_up, s_out, s_down = late_shards
    sh_up, sh_out, sh_down = LATE_SHAPES
    proj, out_partly = _mm(x, w_in_p, m=t, n=IN_PAD, k=D_MODEL, tm=tq, tn=640, tk=D_MODEL, name="proj",
                           rider=_gather_rider([s_out], [sh_out]))
    sb_o, sb_mass, up_partly = _sb_fwd(proj, t, _gather_rider([s_up], [sh_up]))
    gla_o, oraw, states, w_up4, w_out4, down_partly = _gla_fwd(
        proj, gate_up_pad, gate_bias, gnorm, t,
        _join(_forward_rider([up_partly, out_partly], [sh_up, sh_out]), _gather_rider([s_down], [sh_down])))
    w_out = w_out4.reshape(D_MODEL, D_MODEL)
    xhat1, rstd1, h_bf = _outproj_ln1(sb_o, gla_o, x, w_out, ln1_g, ln1_b, t)
    up3, w_down4 = _mm(h_bf, w_up4, m=t, n=2 * D_FF, k=D_MODEL, tm=tq, tn=W_UP_S, tk=D_MODEL, name="up",
                       b_spec=pl.BlockSpec((None, D_MODEL, W_UP_S), lambda i, j, kk: (j, 0, 0)),
                       o_spec=pl.BlockSpec((None, tq, W_UP_S), lambda i, j, kk: (j // 2, i, j % 2)),
                       out_shape=jax.ShapeDtypeStruct((2, t, D_FF), BF16), out_dtype=BF16,
                       rider=_forward_rider([down_partly], [sh_down]))
    w_down = w_down4.reshape(D_FF, D_MODEL)
    gm = _conv_gelu_fwd(up3, conv_w3, conv_b3, t, tr=tq)
    dr2, loss, dln2_g, dln2_b = _down_ln2_loss(gm, w_down, xhat1, ln1_g, ln1_b, ln2_g, ln2_b, target, t)
    dgm = _mm(dr2, w_down, m=t, n=D_FF, k=D_MODEL, tm=tq, tn=W_UP_S, tk=D_MODEL, tb=True, out_dtype=BF16, name="dgm")
    dw_down = _mm(gm, dr2, m=D_FF, n=D_MODEL, k=t, tm=W_UP_S, tn=D_MODEL, tk=tq, ta=True, name="dw_down")
    dup3, dcw, dcb = _conv_gelu_bwd(up3, dgm, conv_w3, conv_b3, t)
    dr1, dln1_g, dln1_b = _dh_ln1_bwd(dup3, w_up4, dr2, xhat1, rstd1, ln1_g, t)
    dw_up4 = _mm(h_bf, dup3, m=D_MODEL, n=2 * D_FF, k=t, tm=512, tn=W_UP_S, tk=t, ta=True, name="dw_up",
                 b_spec=pl.BlockSpec((None, t, W_UP_S), lambda i, j, kk: (j // 2, kk, j % 2)),
                 o_spec=pl.BlockSpec((None, 512, W_UP_S), lambda i, j, kk: (j, i, 0)),
                 out_shape=jax.ShapeDtypeStruct((4, D_MODEL, W_UP_S), F32))
    dw_out_sb = _mm(sb_o, dr1, m=SB_WIDTH, n=D_MODEL, k=t, tm=512, tn=D_MODEL, tk=tq, ta=True, name="dw_out_sb")
    dw_out_gla = _mm(gla_o, dr1, m=GLA_WIDTH, n=D_MODEL, k=t, tm=512, tn=D_MODEL, tk=tq, ta=True, name="dw_out_gla")
    gs = [dw_up4, jnp.concatenate([dw_out_sb, dw_out_gla], axis=0).reshape(4, W_OUT_S, D_MODEL),
          dw_down.reshape(4, W_DOWN_S, D_MODEL)]
    dcat, *from_sib = _mm(dr1, w_out, m=t, n=D_MODEL, k=D_MODEL, tm=tq, tn=512, tk=D_MODEL, tb=True, name="dcat",
                          rider=_sibling_rider(gs, LATE_SHAPES))
    ps = [_add_sibling(gs[m], from_sib[m], c_arr, LATE_ADD_ROWS[m], "add_sibling_late_%d" % m) for m in range(3)]
    dgla, dgup_pad, dgbias, dgnorm = _gla_bwd(proj, dcat, oraw, states, gate_up_pad, gate_bias, gnorm, t)
    small = dict(
        gate_up=dgup_pad[:GATE_RANK], gate_bias=dgbias, gla_norm_g=dgnorm, ln1_g=dln1_g, ln1_b=dln1_b,
        conv_w=jnp.concatenate([dcw[0, :3], dcw[1, :3]], axis=1), conv_b=jnp.concatenate([dcb[0], dcb[1]], axis=1),
        ln2_g=dln2_g, ln2_b=dln2_b, loss=loss)
    dsq, dsk, dsv, *others, vecs = _sb_bwd(proj, dcat, sb_mass, t,
                                           _chips_rider(ps, _pack_vec(small, SMALL_GRADS, GRAD_VEC_ROWS)))
    late_sums = [_add_chips(ps[m], others[m], kc_arr, LATE_ADD_ROWS[m], "add_chips_late_%d" % m) for m in range(3)]
    dproj = [dsq, dsk, dsv, dgla]
    dw_in_p = _dw_in(x, dproj, t, tk=tq)
    g_in = dw_in_p[None]
    from_sib_in, = _run(_sibling_rider([g_in], [(D_MODEL, IN_PAD)]), "exchange_sibling_w_in")
    half_in = _add_sibling(g_in, from_sib_in, c_arr, ADD_ROWS[0], "add_sibling_w_in")[0]
    p_in = jnp.stack([half_in[:, k * W_IN_S:(k + 1) * W_IN_S] for k in range(4)], axis=0)
    dx, others_in = _dx(dproj, w_in_p, dr1, t, _chips_rider([p_in]))
    sum_in = _add_chips(p_in, others_in, kc_arr, ADD_ROWS[0], "add_chips_w_in")
    return dx, [sum_in] + late_sums, vecs


W_IN_S, W_UP_S, W_OUT_S, W_DOWN_S = IN_WIDTH // 4, 2 * D_FF // 4, D_MODEL // 4, D_FF // 4
SHARD_SHAPES = ((D_MODEL, W_IN_S), (D_MODEL, W_UP_S), (W_OUT_S, D_MODEL), (W_DOWN_S, D_MODEL))
ADD_ROWS = (256, 256, 128, 176)
LATE_SHAPES, LATE_ADD_ROWS = SHARD_SHAPES[1:], ADD_ROWS[1:]
SMALL_ROWS = 8
VEC_SIZES = (("gate_bias", GLA_KEYS), ("gla_norm_g", LANES), ("ln1_g", D_MODEL), ("ln1_b", D_MODEL),
             ("conv_b", 2 * D_FF), ("ln2_g", D_MODEL), ("ln2_b", D_MODEL))
SMALL_GRADS = VEC_SIZES + (("conv_w", 3 * 2 * D_FF), ("gate_up", GATE_RANK * GLA_KEYS), ("loss", 1))


def _rows(a):
    flat = a.reshape(-1)
    pad = (-flat.shape[0]) % D_MODEL
    if pad:
        flat = jnp.pad(flat, (0, pad))
    return flat.reshape(-1, D_MODEL)


def _pad_rows(a, rows):
    return jnp.pad(a, ((0, rows - a.shape[0]), (0, 0)))


def _pack_vec(d, sizes, rows):
    flat = jnp.concatenate([d[n].reshape(-1) for n, _ in sizes])
    return _pad_rows(_rows(flat), rows)


def _unpack_vec(v, sizes):
    flat = v.reshape(-1)
    out, o = {}, 0
    for n, size in sizes:
        out[n] = flat[o:o + size].reshape(1, size)
        o += size
    return out


VEC_ROWS = 16
GRAD_VEC_ROWS = 32


HBM_SPEC = pl.BlockSpec(memory_space=pltpu.HBM)


def _position():
    x, y, c = lax.axis_index("x"), lax.axis_index("y"), lax.axis_index("c")
    chips = [(1 - x, y), (x, 1 - y), (1 - x, 1 - y)]
    return x, y, c, chips


def _remote(src, dst, send_sems, recv_sems, k, to):
    return pltpu.make_async_remote_copy(src_ref=src, dst_ref=dst, send_sem=send_sems.at[k], recv_sem=recv_sems.at[k],
                                        device_id=to, device_id_type=MESH)


def _gather_ici(in_refs, out_refs, shapes, send_sems, recv_sems, local_sems):
    x, y, c, chips = _position()
    k_me = 2 * x + y
    local, sends, recvs = [], [], []
    for m, (rows, _) in enumerate(shapes):
        h = rows // 2
        local.append(pltpu.make_async_copy(in_refs[m], out_refs[m].at[k_me], local_sems.at[m]))
        for j, (cx, cy) in enumerate(chips):
            sends.append(_remote(in_refs[m].at[pl.ds(c * h, h), :], out_refs[m].at[k_me, pl.ds(c * h, h), :],
                                 send_sems, recv_sems, 3 * m + j, (cx, cy, c)))
            landed = out_refs[m].at[2 * cx + cy, pl.ds(c * h, h), :]
            recvs.append(_remote(landed, landed, send_sems, recv_sems, 3 * m + j, (x, y, c)))
    return local, sends, recvs


def _gather_d2d(src_refs, dst_refs, shapes, send_sems, recv_sems, base):
    x, y, c, chips = _position()
    sends, recvs = [], []
    for m, (rows, _) in enumerate(shapes):
        h = rows // 2
        for j, (cx, cy) in enumerate(chips):
            k = 2 * cx + cy
            sends.append(_remote(src_refs[m].at[k, pl.ds(c * h, h), :], dst_refs[m].at[k, pl.ds(c * h, h), :],
                                 send_sems, recv_sems, base + 3 * m + j, (x, y, 1 - c)))
            landed = dst_refs[m].at[k, pl.ds((1 - c) * h, h), :]
            recvs.append(_remote(landed, landed, send_sems, recv_sems, base + 3 * m + j, (x, y, c)))
    return sends, recvs


def _gather_weights(shards, small, shapes):
    nm = len(shards)
    n_ici = 3 * nm

    def body(*refs):
        in_refs, small_ref = refs[:nm], refs[nm]
        out_refs, osm_ref = refs[nm + 1:2 * nm + 1], refs[2 * nm + 1]
        send_sems, recv_sems, local_sems = refs[2 * nm + 2:]
        x, y, c, chips = _position()
        k_me = 2 * x + y
        local, sends, recvs = _gather_ici(in_refs, out_refs, shapes, send_sems, recv_sems, local_sems)
        local.append(pltpu.make_async_copy(small_ref, osm_ref.at[k_me], local_sems.at[nm]))
        for j, (cx, cy) in enumerate(chips):
            sends.append(_remote(small_ref, osm_ref.at[k_me], send_sems, recv_sems, n_ici + j, (cx, cy, c)))
        for cp in local + sends:
            cp.start()
        for cp in recvs:
            cp.wait_recv()
        fsends, frecvs = _gather_d2d(out_refs, out_refs, shapes, send_sems, recv_sems, n_ici + 3)
        for cp in fsends:
            cp.start()
        for j, (cx, cy) in enumerate(chips):
            k = 2 * cx + cy
            frecvs.append(_remote(osm_ref.at[k], osm_ref.at[k], send_sems, recv_sems, n_ici + j, (x, y, c)))
        for cp in frecvs:
            cp.wait_recv()
        for cp in sends + fsends:
            cp.wait_send()
        for cp in local:
            cp.wait()

    n_sems = 2 * n_ici + 3
    return pl.pallas_call(
        body, name="gather_weights", in_specs=[HBM_SPEC] * (nm + 1), out_specs=[HBM_SPEC] * (nm + 1),
        out_shape=[jax.ShapeDtypeStruct((4,) + s, BF16) for s in shapes]
        + [jax.ShapeDtypeStruct((4, SMALL_ROWS, D_MODEL), F32)],
        scratch_shapes=[pltpu.SemaphoreType.DMA((n_sems,)), pltpu.SemaphoreType.DMA((n_sems,)),
                        pltpu.SemaphoreType.DMA((nm + 1,))],
    )(*shards, small)


def _gather_rider(shards, shapes):
    n = len(shards)
    return _Rider(shards, [jax.ShapeDtypeStruct((4,) + s, BF16) for s in shapes], (3 * n, 3 * n, n),
                  lambda ins, outs, sems: _gather_ici(ins, outs, shapes, *sems))


def _forward_rider(gathered, shapes):
    n = len(gathered)
    return _Rider(gathered, [jax.ShapeDtypeStruct(a.shape, a.dtype) for a in gathered], (3 * n, 3 * n),
                  lambda ins, outs, sems: ([],) + _gather_d2d(ins, outs, shapes, sems[0], sems[1], 0),
                  aliases=[(m, m) for m in range(n)])


def _sibling_rider(gs, shapes):
    def copies(ins, outs, sems):
        x, y, c, _ = _position()
        both = []
        for m, (rows, _) in enumerate(shapes):
            h = rows // 2
            for k in range(gs[m].shape[0]):
                both.append(_remote(ins[m].at[k, pl.ds((1 - c) * h, h), :], outs[m].at[k], sems[0], sems[1],
                                    4 * m + k, (x, y, 1 - c)))
        return [], both, both

    return _Rider(gs, [jax.ShapeDtypeStruct((g.shape[0], r // 2, cl), F32) for g, (r, cl) in zip(gs, shapes)],
                  (4 * len(gs), 4 * len(gs)), copies)


def _add_sibling(g, r, c_arr, tr, name):
    nblk, rows, cols = g.shape
    nb = rows // 2 // tr

    def body(c_ref, g_ref, r_ref, o_ref):
        o_ref[...] = (g_ref[...] + r_ref[...]).astype(BF16)

    spec = pl.BlockSpec((None, tr, cols), lambda k, i, c: (k, i, 0))
    return pl.pallas_call(
        body, name=name,
        grid_spec=pltpu.PrefetchScalarGridSpec(
            num_scalar_prefetch=1, grid=(nblk, nb),
            in_specs=[pl.BlockSpec((None, tr, cols), lambda k, i, c: (k, c[0] * nb + i, 0)), spec], out_specs=spec),
        out_shape=jax.ShapeDtypeStruct((nblk, rows // 2, cols), BF16), compiler_params=_cparams("parallel", "parallel"),
    )(c_arr, g, r)


def _reduce_ici(p_refs, r_refs, send_sems, recv_sems):
    x, y, c, chips = _position()
    sends, recvs = [], []
    for m in range(len(p_refs)):
        for j, (cx, cy) in enumerate(chips):
            sends.append(_remote(p_refs[m].at[2 * cx + cy], r_refs[m].at[j], send_sems, recv_sems, 3 * m + j, (cx, cy, c)))
            recvs.append(_remote(r_refs[m].at[j], r_refs[m].at[j], send_sems, recv_sems, 3 * m + j, (x, y, c)))
    return sends, recvs


def _chips_rider(ps, vec=None):
    nm = len(ps)
    n_ici = 3 * nm

    def copies(ins, outs, sems):
        sends, recvs = _reduce_ici(ins[:nm], outs[:nm], sems[0], sems[1])
        if vec is None:
            return [], sends, recvs
        x, y, c, _ = _position()
        my_id = 4 * x + 2 * y + c
        vec_ref, vrecv_ref = ins[nm], outs[nm]
        local = [pltpu.make_async_copy(vec_ref, vrecv_ref.at[my_id], sems[2].at[0])]
        for r in range(1, 8):
            peer = (1 - x if r & 4 else x, 1 - y if r & 2 else y, 1 - c if r & 1 else c)
            sends.append(_remote(vec_ref, vrecv_ref.at[my_id], sems[0], sems[1], n_ici + r - 1, peer))
            recvs.append(_remote(vec_ref, vrecv_ref.at[0], sems[0], sems[1], n_ici + r - 1, (x, y, c)))
        return local, sends, recvs

    out_shapes = [jax.ShapeDtypeStruct((3,) + p.shape[1:], p.dtype) for p in ps]
    if vec is None:
        return _Rider(ps, out_shapes, (n_ici, n_ici), copies)
    return _Rider(list(ps) + [vec], out_shapes + [jax.ShapeDtypeStruct((8, GRAD_VEC_ROWS, D_MODEL), F32)],
                  (n_ici + 7, n_ici + 7, 1), copies)


def _add_chips(p, r, kc_arr, tr, name):
    _, h, cols = p.shape
    nb = h // tr

    def body(kc_ref, p_ref, r0_ref, r1_ref, r2_ref, o_ref):
        o_ref[...] = ((p_ref[...].astype(F32) + r0_ref[...].astype(F32)) + r1_ref[...].astype(F32)) + r2_ref[...].astype(F32)

    rspec = lambda j: pl.BlockSpec((None, tr, cols), lambda i, kc: (j, i, 0))
    return pl.pallas_call(
        body, name=name,
        grid_spec=pltpu.PrefetchScalarGridSpec(
            num_scalar_prefetch=1, grid=(nb,),
            in_specs=[pl.BlockSpec((None, tr, cols), lambda i, kc: (kc[0], i, 0)), rspec(0), rspec(1), rspec(2)],
            out_specs=pl.BlockSpec((tr, cols), lambda i, kc: (kc[1] * nb + i, 0))),
        out_shape=jax.ShapeDtypeStruct((2 * h, cols), F32), compiler_params=_cparams("parallel"),
    )(kc_arr, p, r, r, r)


def _reunite_sibling(fs, shapes):
    n_chunks = 2
    nm = len(fs)

    def body(*refs):
        in_refs, out_refs = refs[:nm], refs[nm:2 * nm]
        send_sems, recv_sems = refs[2 * nm:]
        x, y, c, _ = _position()
        sends, recvs = [], []
        for m in range(nm):
            ch = shapes[m][0] // 2 // n_chunks
            for q in range(n_chunks):
                mine = pl.ds((c * n_chunks + q) * ch, ch)
                theirs = pl.ds(((1 - c) * n_chunks + q) * ch, ch)
                s = n_chunks * m + q
                sends.append(_remote(in_refs[m].at[mine, :], out_refs[m].at[mine, :], send_sems, recv_sems, s, (x, y, 1 - c)))
                recvs.append(_remote(in_refs[m].at[theirs, :], out_refs[m].at[theirs, :], send_sems, recv_sems, s, (x, y, c)))
        for cp in sends:
            cp.start()
        for cp in recvs:
            cp.wait_recv()
        for cp in sends:
            cp.wait_send()

    n_sems = n_chunks * nm
    return pl.pallas_call(
        body, name="reunite_sibling", in_specs=[HBM_SPEC] * nm, out_specs=[HBM_SPEC] * nm,
        out_shape=[jax.ShapeDtypeStruct(s, F32) for s in shapes],
        input_output_aliases={m: m for m in range(nm)},
        scratch_shapes=[pltpu.SemaphoreType.DMA((n_sems,)), pltpu.SemaphoreType.DMA((n_sems,))],
    )(*fs)


def _sum_vecs(v):
    def body(v_ref, o_ref):
        acc = v_ref[0]
        for d in range(1, 8):
            acc = acc + v_ref[d]
        o_ref[...] = acc

    return pl.pallas_call(body, name="sum_vecs", out_shape=jax.ShapeDtypeStruct(v.shape[1:], F32))(v)


def kernel(x, w_in, gate_up, gate_bias, gla_norm_g, w_out, ln1_g, ln1_b, w_up, conv_w, conv_b, w_down, ln2_g, ln2_b, loss_target, m_w_in, m_gate_up, m_gate_bias, m_gla_norm_g, m_w_out, m_ln1_g, m_ln1_b, m_w_up, m_conv_w, m_conv_b, m_w_down, m_ln2_g, m_ln2_b, v_w_in, v_gate_up, v_gate_bias, v_gla_norm_g, v_w_out, v_ln1_g, v_ln1_b, v_w_up, v_conv_w, v_conv_b, v_w_down, v_ln2_g, v_ln2_b):
    xi, yi, ci = lax.axis_index("x"), lax.axis_index("y"), lax.axis_index("c")
    k_me = 2 * xi + yi
    c_arr = ci.astype(jnp.int32).reshape(1)
    kc_arr = jnp.stack([k_me, ci]).astype(jnp.int32)
    small = _pad_rows(jnp.concatenate([_rows(conv_w[0]), _rows(gate_up[0])], axis=0), SMALL_ROWS)
    w_in4, gsmall = _gather_weights([w_in[0].astype(BF16)], small, SHARD_SHAPES[:1])
    late_shards = [w_up[0].astype(BF16), w_out[0].astype(BF16), w_down[0].astype(BF16)]
    w_in_p = jnp.pad(jnp.concatenate([w_in4[k] for k in range(4)], axis=1), ((0, 0), (0, IN_PAD - IN_WIDTH)))
    conv_w_f = jnp.concatenate([gsmall[k, :5].reshape(-1)[:3 * W_UP_S].reshape(3, W_UP_S) for k in range(4)], axis=1)
    gate_up_f = jnp.concatenate([gsmall[k, 5].reshape(GATE_RANK, GLA_KEYS // 4) for k in range(4)], axis=1)
    conv_w3 = jnp.transpose(conv_w_f.reshape(3, 2, D_FF), (1, 0, 2))
    conv_b3 = conv_b.reshape(2, 1, D_FF)
    gate_up_pad = jnp.pad(gate_up_f, ((0, LANES - GATE_RANK), (0, 0)))

    dx, sums, vecs = _local_step(
        x[0], loss_target[0], w_in_p, late_shards, gate_up_pad, gate_bias, gla_norm_g, ln1_g, ln1_b, conv_w3, conv_b3,
        ln2_g, ln2_b, c_arr, kc_arr)
    g_w_in, g_w_up, g_w_out, g_w_down = _reunite_sibling(sums, SHARD_SHAPES)
    gsmall_sum = _unpack_vec(_sum_vecs(vecs), SMALL_GRADS)
    g_conv_w = lax.dynamic_slice_in_dim(gsmall_sum["conv_w"].reshape(3, 2 * D_FF), k_me * W_UP_S, W_UP_S, axis=1)
    g_gate_up = lax.dynamic_slice_in_dim(gsmall_sum["gate_up"].reshape(GATE_RANK, GLA_KEYS), k_me * (GLA_KEYS // 4),
                                         GLA_KEYS // 4, axis=1)
    gv = gsmall_sum
    loss = gv["loss"][0, 0]
    gvec = _pack_vec(gv, VEC_SIZES, VEC_ROWS)

    grads = dict(w_in=g_w_in[None], gate_up=g_gate_up[None], gate_bias=gv["gate_bias"], gla_norm_g=gv["gla_norm_g"],
                 w_out=g_w_out[None], ln1_g=gv["ln1_g"], ln1_b=gv["ln1_b"], w_up=g_w_up[None], conv_w=g_conv_w[None],
                 conv_b=gv["conv_b"], w_down=g_w_down[None], ln2_g=gv["ln2_g"], ln2_b=gv["ln2_b"])
    weights = dict(w_in=w_in, gate_up=gate_up, gate_bias=gate_bias, gla_norm_g=gla_norm_g, w_out=w_out, ln1_g=ln1_g,
                   ln1_b=ln1_b, w_up=w_up, conv_w=conv_w, conv_b=conv_b, w_down=w_down, ln2_g=ln2_g, ln2_b=ln2_b)
    ms = dict(w_in=m_w_in, gate_up=m_gate_up, gate_bias=m_gate_bias, gla_norm_g=m_gla_norm_g, w_out=m_w_out, ln1_g=m_ln1_g,
              ln1_b=m_ln1_b, w_up=m_w_up, conv_w=m_conv_w, conv_b=m_conv_b, w_down=m_w_down, ln2_g=m_ln2_g, ln2_b=m_ln2_b)
    vs = dict(w_in=v_w_in, gate_up=v_gate_up, gate_bias=v_gate_bias, gla_norm_g=v_gla_norm_g, w_out=v_w_out, ln1_g=v_ln1_g,
              ln1_b=v_ln1_b, w_up=v_w_up, conv_w=v_conv_w, conv_b=v_conv_b, w_down=v_w_down, ln2_g=v_ln2_g, ln2_b=v_ln2_b)
    names = ["w_in", "gate_up", "gate_bias", "gla_norm_g", "w_out", "ln1_g", "ln1_b", "w_up", "conv_w", "conv_b", "w_down",
             "ln2_g", "ln2_b"]
    delta, new_m, new_v = {}, {}, {}
    for n in ("w_in", "gate_up", "w_out", "w_up", "conv_w", "w_down"):
        tr_ = jnp.transpose if n == "w_in" else (lambda a: a)
        d, nm, nv = _adamw(tr_(weights[n][0]), tr_(grads[n][0]), tr_(ms[n][0]), tr_(vs[n][0]), "adamw_" + n)
        delta[n], new_m[n], new_v[n] = tr_(d)[None], tr_(nm)[None], tr_(nv)[None]
    d, nm, nv = _adamw(_pack_vec(weights, VEC_SIZES, VEC_ROWS), gvec, _pack_vec(ms, VEC_SIZES, VEC_ROWS),
                       _pack_vec(vs, VEC_SIZES, VEC_ROWS), "adamw_vectors")
    for dst, src in ((delta, d), (new_m, nm), (new_v, nv)):
        dst.update(_unpack_vec(src, VEC_SIZES))
    return (loss, dx[None], *[grads[n] for n in names], *[delta[n] for n in names], *[new_m[n] for n in names],
            *[new_v[n] for n in names])
```

```python
import functools
import math

import jax
import jax.numpy as jnp
from jax import lax
from jax.experimental import pallas as pl
from jax.experimental.pallas import tpu as pltpu

F32 = jnp.float32
BF16 = jnp.bfloat16

D_MODEL = 1024
SB_WIDTH = 512
GLA_KEYS = 256
GLA_WIDTH = 512
GATE_RANK = 16
IN_WIDTH = 3088
IN_PAD = 3200
D_FF = 2816
CHUNK = 64
LN_EPS = 1e-5
RMS_EPS = 1e-6
ALPHA = 2.0 ** 0.25
GLA_TAU = 16.0
SB_SCALE = 0.125
GLA_SCALE = 0.125
LANES = 128
SB_BLK = 256
SB_CUT = -100.0
GLA_ROWS = 256
VMEM_LIMIT = 56 * 1024 * 1024

ADAM_LR, ADAM_B1, ADAM_B2, ADAM_EPS, ADAM_WD, ADAM_STEP = 0.001, 0.9, 0.999, 1e-08, 0.01, 10

MESH = pl.DeviceIdType.MESH


def _cparams(*sem):
    return pltpu.CompilerParams(dimension_semantics=sem, vmem_limit_bytes=VMEM_LIMIT)


def _dot(a, b):
    return jnp.dot(a, b, preferred_element_type=F32)


def _dot_nt(a, b):
    return lax.dot_general(a, b, (((1,), (1,)), ((), ())), preferred_element_type=F32)


def _dot_tn(a, b):
    return lax.dot_general(a, b, (((0,), (0,)), ((), ())), preferred_element_type=F32)


def _split3(x):
    hi = x.astype(BF16)
    r = x - hi.astype(F32)
    mid = r.astype(BF16)
    lo = (r - mid.astype(F32)).astype(BF16)
    return hi, mid, lo


def _softplus(z):
    return jnp.maximum(z, 0.0) + jnp.log(1.0 + jnp.exp(-jnp.abs(z)))


def _sigmoid(z):
    return 1.0 / (1.0 + jnp.exp(-z))


def _mm(a, b, *, m, n, k, tm, tn, tk, ta=False, tb=False, a_spec=None, b_spec=None, o_spec=None,
        out_shape=None, out_dtype=F32, add=None, add_scale=1.0, rider=None, name):
    nk = k // tk
    dn = (((0 if ta else 1,), (1 if tb else 0,)), ((), ()))
    n_in = len(rider.inputs) if rider else 0
    n_out = len(rider.out_shapes) if rider else 0
    n_add = int(add is not None)
    steps = (m // tm, n // tn, nk)

    def body(*refs):
        a_ref, b_ref = refs[:2]
        add_ref = refs[2] if add is not None else None
        rin = refs[2 + n_add:2 + n_add + n_in]
        o_ref = refs[2 + n_add + n_in]
        rout = refs[3 + n_add + n_in:3 + n_add + n_in + n_out]
        scratch = refs[3 + n_add + n_in + n_out:]
        if rider:
            sems = scratch[len(scratch) - len(rider.sems):]
            ids = [pl.program_id(d) for d in range(3)]
            _ride(rider, rin, rout, sems, (ids[0] == 0) & (ids[1] == 0) & (ids[2] == 0),
                  (ids[0] == steps[0] - 1) & (ids[1] == steps[1] - 1) & (ids[2] == steps[2] - 1))

        part = lax.dot_general(a_ref[...].astype(BF16), b_ref[...].astype(BF16), dn, preferred_element_type=F32)

        def finish(r):
            if add is not None:
                r = r + add_scale * add_ref[...]
            o_ref[...] = r.astype(out_dtype)

        if nk == 1:
            finish(part)
            return
        acc_ref = scratch[0]
        kk = pl.program_id(2)

        @pl.when(kk == 0)
        def _():
            acc_ref[...] = part

        @pl.when((kk > 0) & (kk < nk - 1))
        def _():
            acc_ref[...] += part

        @pl.when(kk == nk - 1)
        def _():
            finish(acc_ref[...] + part)

    if a_spec is None:
        a_spec = pl.BlockSpec((tk, tm), lambda i, j, kk: (kk, i)) if ta else pl.BlockSpec((tm, tk), lambda i, j, kk: (i, kk))
    if b_spec is None:
        b_spec = pl.BlockSpec((tn, tk), lambda i, j, kk: (j, kk)) if tb else pl.BlockSpec((tk, tn), lambda i, j, kk: (kk, j))
    if o_spec is None:
        o_spec = pl.BlockSpec((tm, tn), lambda i, j, kk: (i, j))
    if out_shape is None:
        out_shape = jax.ShapeDtypeStruct((m, n), out_dtype)
    in_specs = [a_spec, b_spec]
    args = [a, b]
    if add is not None:
        in_specs.append(pl.BlockSpec((tm, tn), lambda i, j, kk: (i, j)))
        args.append(add)
    scratch = [pltpu.VMEM((tm, tn), F32)] if nk > 1 else []
    if not rider:
        return pl.pallas_call(
            body, name=name, grid=steps, in_specs=in_specs, out_specs=o_spec, out_shape=out_shape,
            scratch_shapes=scratch, compiler_params=_cparams("parallel", "parallel", "arbitrary"),
        )(*args)
    return pl.pallas_call(
        body, name=name, grid=steps, in_specs=in_specs + [HBM_SPEC] * n_in, out_specs=[o_spec] + [HBM_SPEC] * n_out,
        out_shape=[out_shape] + list(rider.out_shapes),
        input_output_aliases={len(args) + i: 1 + o for i, o in rider.aliases},
        scratch_shapes=scratch + [pltpu.SemaphoreType.DMA((s,)) for s in rider.sems],
        compiler_params=_cparams("arbitrary", "arbitrary", "arbitrary"),
    )(*args, *rider.inputs)


def _col_offsets(pieces):
    offs, o = [], 0
    for a in pieces:
        offs.append(o)
        o += a.shape[1]
    return offs


def _dx(pieces, w_in_p, dr1, t, rider, tm=512):
    offs = _col_offsets(pieces)
    npc = len(pieces)
    n_in, n_out = len(rider.inputs), len(rider.out_shapes)
    steps = t // tm

    def body(*refs):
        p_refs, w_ref, add_ref = refs[:npc], refs[npc], refs[npc + 1]
        rin = refs[npc + 2:npc + 2 + n_in]
        o_ref = refs[npc + 2 + n_in]
        rout = refs[npc + 3 + n_in:npc + 3 + n_in + n_out]
        i = pl.program_id(0)
        _ride(rider, rin, rout, refs[npc + 3 + n_in + n_out:], i == 0, i == steps - 1)
        acc = ALPHA * add_ref[...]
        for p_ref, off in zip(p_refs, offs):
            acc = acc + _dot_nt(p_ref[...].astype(BF16), w_ref[:, off:off + p_ref.shape[1]])
        o_ref[...] = acc

    row = lambda w: pl.BlockSpec((tm, w), lambda i: (i, 0))
    return pl.pallas_call(
        body, name="dx", grid=(steps,),
        in_specs=[row(a.shape[1]) for a in pieces] + [pl.BlockSpec(w_in_p.shape, lambda i: (0, 0)), row(D_MODEL)]
        + [HBM_SPEC] * n_in,
        out_specs=[row(D_MODEL)] + [HBM_SPEC] * n_out,
        out_shape=[jax.ShapeDtypeStruct((t, D_MODEL), F32)] + rider.out_shapes,
        scratch_shapes=[pltpu.SemaphoreType.DMA((s,)) for s in rider.sems],
        compiler_params=_cparams("arbitrary"),
    )(*pieces, w_in_p, dr1, *rider.inputs)


def _dw_out(sb_o, gla_o, dr1, t, tk):
    def body(sb_ref, gl_ref, dr_ref, o_ref):
        kk = pl.program_id(0)
        drb = dr_ref[...].astype(BF16)
        top, bottom = _dot_tn(sb_ref[...], drb), _dot_tn(gl_ref[...], drb)

        @pl.when(kk == 0)
        def _():
            o_ref[0:SB_WIDTH, :] = top
            o_ref[SB_WIDTH:, :] = bottom

        @pl.when(kk > 0)
        def _():
            o_ref[0:SB_WIDTH, :] += top
            o_ref[SB_WIDTH:, :] += bottom

    return pl.pallas_call(
        body, name="dw_out", grid=(t // tk,),
        in_specs=[pl.BlockSpec((tk, SB_WIDTH), lambda kk: (kk, 0)), pl.BlockSpec((tk, GLA_WIDTH), lambda kk: (kk, 0)),
                  pl.BlockSpec((tk, D_MODEL), lambda kk: (kk, 0))],
        out_specs=pl.BlockSpec((D_MODEL, D_MODEL), lambda kk: (0, 0)),
        out_shape=jax.ShapeDtypeStruct((D_MODEL, D_MODEL), F32),
        compiler_params=_cparams("arbitrary"),
    )(sb_o, gla_o, dr1)


def _dw_in(x, pieces, t, tm=512, tk=512):
    offs = _col_offsets(pieces)
    npc = len(pieces)
    n = offs[-1] + pieces[-1].shape[1]
    nk = t // tk

    def body(*refs):
        x_ref, p_refs, o_ref, acc_ref = refs[0], refs[1:1 + npc], refs[1 + npc], refs[2 + npc]
        kk = pl.program_id(1)
        xb = x_ref[...].astype(BF16)
        for p_ref, off in zip(p_refs, offs):
            cols = slice(off, off + p_ref.shape[1])
            part = _dot_tn(xb, p_ref[...].astype(BF16))
            if nk == 1:
                o_ref[:, cols] = part
                continue

            @pl.when(kk == 0)
            def _():
                acc_ref[:, cols] = part

            @pl.when((kk > 0) & (kk < nk - 1))
            def _():
                acc_ref[:, cols] += part

            @pl.when(kk == nk - 1)
            def _():
                o_ref[:, cols] = acc_ref[:, cols] + part

    return pl.pallas_call(
        body, name="dw_in", grid=(D_MODEL // tm, nk),
        in_specs=[pl.BlockSpec((tk, tm), lambda i, kk: (kk, i))]
        + [pl.BlockSpec((tk, a.shape[1]), lambda i, kk: (kk, 0)) for a in pieces],
        out_specs=pl.BlockSpec((tm, n), lambda i, kk: (i, 0)),
        out_shape=jax.ShapeDtypeStruct((D_MODEL, n), F32),
        scratch_shapes=[pltpu.VMEM((tm, n), F32)],
        compiler_params=_cparams("parallel", "arbitrary"),
    )(x, *pieces)


class _Rider:
    def __init__(self, inputs, out_shapes, sems, copies, aliases=()):
        self.inputs, self.out_shapes, self.sems, self.copies = list(inputs), list(out_shapes), tuple(sems), copies
        self.aliases = tuple(aliases)


def _join(a, b):
    na_in, na_out, na_sems = len(a.inputs), len(a.out_shapes), len(a.sems)

    def copies(ins, outs, sems):
        first = a.copies(ins[:na_in], outs[:na_out], sems[:na_sems])
        second = b.copies(ins[na_in:], outs[na_out:], sems[na_sems:])
        return tuple(u + v for u, v in zip(first, second))

    return _Rider(a.inputs + b.inputs, a.out_shapes + b.out_shapes, a.sems + b.sems, copies,
                  a.aliases + tuple((i + na_in, o + na_out) for i, o in b.aliases))


def _ride(rider, in_refs, out_refs, sems, first, last):
    @pl.when(first)
    def _():
        local, sends, _ = rider.copies(in_refs, out_refs, sems)
        for cp in local + sends:
            cp.start()

    @pl.when(last)
    def _():
        local, sends, recvs = rider.copies(in_refs, out_refs, sems)
        for cp in recvs:
            cp.wait_recv()
        for cp in sends:
            cp.wait_send()
        for cp in local:
            cp.wait()


def _run(rider, name):
    n_in, n_out = len(rider.inputs), len(rider.out_shapes)

    def body(*refs):
        local, sends, recvs = rider.copies(refs[:n_in], refs[n_in:n_in + n_out], refs[n_in + n_out:])
        for cp in local + sends:
            cp.start()
        for cp in recvs:
            cp.wait_recv()
        for cp in sends:
            cp.wait_send()
        for cp in local:
            cp.wait()

    return pl.pallas_call(
        body, name=name, in_specs=[HBM_SPEC] * n_in, out_specs=[HBM_SPEC] * n_out, out_shape=list(rider.out_shapes),
        scratch_shapes=[pltpu.SemaphoreType.DMA((s,)) for s in rider.sems],
    )(*rider.inputs)


def _sb_tile(qh, kj, diag, strict, u_strict, r_in):
    z = _dot_nt(qh, kj)
    sp = _softplus(z)
    l1m = -sp
    lsz = z - sp
    if diag:
        l1m = jnp.where(strict, l1m, 0.0)
    cs = _dot(l1m.astype(BF16), u_strict) + r_in
    w = jnp.exp(lsz + cs)
    if diag:
        w = jnp.where(strict, w, 0.0)
    return l1m, lsz, w


def _sb_consts():
    row = lax.broadcasted_iota(jnp.int32, (SB_BLK, SB_BLK), 0)
    col = lax.broadcasted_iota(jnp.int32, (SB_BLK, SB_BLK), 1)
    strict = col < row
    u_strict = (row > col).astype(BF16)
    u_pre = (row < col).astype(BF16)
    lane = lax.broadcasted_iota(jnp.int32, (1, LANES), 1)
    return strict, u_strict, u_pre, lane


def _sb_fwd(proj, t, rider):
    nq = t // SB_BLK
    n_in, n_out = len(rider.inputs), len(rider.out_shapes)

    def body(q_ref, k_ref, v_ref, *rest):
        rin, o_ref, sv_ref, rout = rest[:n_in], rest[n_in], rest[n_in + 1], rest[n_in + 2:n_in + 2 + n_out]
        p = pl.program_id(0)
        i = pl.program_id(1)
        _ride(rider, rin, rout, rest[n_in + 2 + n_out:], (p == 0) & (i == 0), (p == 3) & (i == nq - 1))

        strict, u_strict, _, lane = _sb_consts()
        qf = q_ref[...] * SB_SCALE
        hms = [(lane // 64) == hh for hh in range(2)]
        qhs = [jnp.where(hm, qf, 0.0).astype(BF16) for hm in hms]

        def step(j, r0, r1, a, sv, diag, keep=None):
            rows = pl.ds(pl.multiple_of(j * SB_BLK, SB_BLK), SB_BLK)
            kj = k_ref[rows, :].astype(BF16)
            vf = v_ref[rows, :]
            rs = []
            for hh, r in enumerate((r0, r1)):
                l1m, _, w = _sb_tile(qhs[hh], kj, diag, strict, u_strict, r)
                pv = _dot(w.astype(BF16), jnp.where(hms[hh], vf, 0.0).astype(BF16))
                mass = jnp.sum(l1m, axis=1, keepdims=True)
                if keep is not None:
                    pv, mass = jnp.where(keep, pv, 0.0), jnp.where(keep, mass, 0.0)
                a = a + pv
                rs.append(r + mass)
                sv = jnp.where(lane == hh * 64 + (i - j + 1), rs[hh], sv)
            return rs[0], rs[1], a, sv

        zero = jnp.zeros((SB_BLK, 1), F32)
        acc0 = jnp.zeros((SB_BLK, LANES), F32)
        r0, r1, acc, sv = step(i, zero, zero, acc0, acc0, True)
        r0, r1, acc, sv = step(jnp.maximum(i - 1, 0), r0, r1, acc, sv, False, keep=i > 0)
        _, _, _, acc, sv = lax.while_loop(
            lambda c: (c[0] >= 0) & (jnp.maximum(jnp.max(c[1]), jnp.max(c[2])) > SB_CUT),
            lambda c: (c[0] - 1,) + step(c[0], c[1], c[2], c[3], c[4], False),
            (i - 2, r0, r1, acc, sv))
        o_ref[...] = acc.astype(BF16)
        sv_ref[...] = sv

    return pl.pallas_call(
        body, name="sb_fwd", grid=(4, nq),
        in_specs=[pl.BlockSpec((SB_BLK, LANES), lambda p, i: (i, p)),
                  pl.BlockSpec((t, LANES), lambda p, i: (0, 4 + p)),
                  pl.BlockSpec((t, LANES), lambda p, i: (0, 8 + p))] + [HBM_SPEC] * n_in,
        out_specs=[pl.BlockSpec((SB_BLK, LANES), lambda p, i: (i, p))] * 2 + [HBM_SPEC] * n_out,
        out_shape=[jax.ShapeDtypeStruct((t, SB_WIDTH), BF16), jax.ShapeDtypeStruct((t, SB_WIDTH), F32)] + rider.out_shapes,
        scratch_shapes=[pltpu.SemaphoreType.DMA((s,)) for s in rider.sems],
        compiler_params=_cparams("arbitrary", "arbitrary"),
    )(proj, proj, proj, *rider.inputs)


def _sb_bwd(proj, dcat, mass, t, rider):
    nq = t // SB_BLK
    n_in, n_out = len(rider.inputs), len(rider.out_shapes)

    def body(q_ref, k_ref, v_ref, do_ref, sv_ref, *rest):
        rin = rest[:n_in]
        dq_ref, dk_ref, dv_ref = rest[n_in:n_in + 3]
        rout = rest[n_in + 3:n_in + 3 + n_out]
        p = pl.program_id(0)
        i = pl.program_id(1)
        _ride(rider, rin, rout, rest[n_in + 3 + n_out:], (p == 0) & (i == 0), (p == 3) & (i == nq - 1))

        @pl.when(i == 0)
        def _():
            dk_ref[...] = jnp.zeros_like(dk_ref)
            dv_ref[...] = jnp.zeros_like(dv_ref)

        strict, u_strict, u_pre, lane = _sb_consts()
        qf = q_ref[...] * SB_SCALE
        dof = do_ref[...]
        hms = [(lane // 64) == hh for hh in range(2)]
        qhs = [jnp.where(hm, qf, 0.0).astype(BF16) for hm in hms]
        dohs = [jnp.where(hm, dof, 0.0).astype(BF16) for hm in hms]

        sv = sv_ref[...]
        zero = jnp.zeros((SB_BLK, 1), F32)

        def mass_right(hh, d):
            return jnp.sum(jnp.where(lane == hh * 64 + d, sv, 0.0), axis=1, keepdims=True)

        dstop = lax.while_loop(
            lambda d: (i - d >= 0) & (jnp.maximum(jnp.max(mass_right(0, d)), jnp.max(mass_right(1, d))) > SB_CUT),
            lambda d: d + 1, 1)
        jstop = i - dstop

        def step(j, carry, diag, keep=None):
            pre_g0, pre_g1, dqa = carry
            rows = pl.ds(pl.multiple_of(j * SB_BLK, SB_BLK), SB_BLK)
            kf = k_ref[rows, :]
            kj = kf.astype(BF16)
            vj = v_ref[rows, :].astype(BF16)
            dv = jnp.zeros((SB_BLK, LANES), F32)
            dk = jnp.zeros((SB_BLK, LANES), F32)
            dqj = jnp.zeros((SB_BLK, LANES), F32)
            pre = []
            for hh, pre_g in enumerate((pre_g0, pre_g1)):
                _, lsz, w = _sb_tile(qhs[hh], kj, diag, strict, u_strict, zero if diag else mass_right(hh, i - j))
                g = w * _dot_nt(dohs[hh], vj)
                gpre = _dot(g.astype(BF16), u_pre) + pre_g
                sig = jnp.exp(lsz)
                dz = g * (1.0 - sig) - gpre * sig
                if diag:
                    dz = jnp.where(strict, dz, 0.0)
                dzb = dz.astype(BF16)
                dv = dv + _dot_tn(w.astype(BF16), dohs[hh])
                dk = dk + _dot_tn(dzb, qhs[hh])
                dqj = dqj + _dot(dzb, jnp.where(hms[hh], kf, 0.0).astype(BF16))
                gsum = jnp.sum(g, axis=1, keepdims=True)
                pre.append(pre_g + (gsum if keep is None else jnp.where(keep, gsum, 0.0)))
            if keep is not None:
                dv, dk, dqj = jnp.where(keep, dv, 0.0), jnp.where(keep, dk, 0.0), jnp.where(keep, dqj, 0.0)
            dv_ref[rows, :] += dv
            dk_ref[rows, :] += dk
            return pre[0], pre[1], dqa + dqj

        carry = lax.fori_loop(jstop + 1, i - 1, lambda j, c: step(j, c, False), (zero, zero, jnp.zeros((SB_BLK, LANES), F32)))
        carry = step(jnp.maximum(i - 1, 0), carry, False, keep=i > 0)
        _, _, dq = step(i, carry, True)
        dq_ref[...] = (dq * SB_SCALE).astype(BF16)

    return pl.pallas_call(
        body, name="sb_bwd", grid=(4, nq),
        in_specs=[pl.BlockSpec((SB_BLK, LANES), lambda p, i: (i, p)),
                  pl.BlockSpec((t, LANES), lambda p, i: (0, 4 + p)),
                  pl.BlockSpec((t, LANES), lambda p, i: (0, 8 + p)),
                  pl.BlockSpec((SB_BLK, LANES), lambda p, i: (i, p)),
                  pl.BlockSpec((SB_BLK, LANES), lambda p, i: (i, p))] + [HBM_SPEC] * n_in,
        out_specs=[pl.BlockSpec((SB_BLK, LANES), lambda p, i: (i, p)),
                   pl.BlockSpec((t, LANES), lambda p, i: (0, p)),
                   pl.BlockSpec((t, LANES), lambda p, i: (0, p))] + [HBM_SPEC] * n_out,
        out_shape=[jax.ShapeDtypeStruct((t, SB_WIDTH), BF16)] + [jax.ShapeDtypeStruct((t, SB_WIDTH), F32)] * 2
        + rider.out_shapes,
        scratch_shapes=[pltpu.SemaphoreType.DMA((s,)) for s in rider.sems],
        compiler_params=_cparams("arbitrary", "arbitrary"),
    )(proj, proj, proj, dcat, mass, *rider.inputs)


def _gla_consts():
    r = lax.broadcasted_iota(jnp.int32, (GLA_ROWS, GLA_ROWS), 0)
    c = lax.broadcasted_iota(jnp.int32, (GLA_ROWS, GLA_ROWS), 1)
    same = (r // CHUNK) == (c // CHUNK)
    causal = same & (c <= r)
    upto_mid = c % CHUNK <= CHUNK // 2 - 1
    fwd_stack = jnp.concatenate([causal, same & upto_mid, same], axis=0).astype(BF16)
    bwd_stack = jnp.concatenate([same & (c >= r), same & (r % CHUNK <= CHUNK // 2 - 1), same], axis=1).astype(BF16)
    rowid = lax.broadcasted_iota(jnp.int32, (GLA_ROWS, 1), 0)
    lane = lax.broadcasted_iota(jnp.int32, (1, LANES), 1)
    sr = lax.broadcasted_iota(jnp.int32, (LANES, 2 * LANES), 0)
    sc = lax.broadcasted_iota(jnp.int32, (LANES, 2 * LANES), 1)
    blockdiag = (sr // 64) == (sc // LANES)
    return causal, fwd_stack, bwd_stack, rowid, lane, blockdiag


def _dot3(u, x):
    hi, mid, lo = _split3(x)
    return _dot(u, hi) + _dot(u, mid) + _dot(u, lo)


def _row_to_col(row):
    return jnp.transpose(jnp.broadcast_to(row, (LANES, LANES)))


def _gla_gates(ga_ref, gup_ref, gbias_ref):
    pre = _dot(ga_ref[...].astype(BF16), gup_ref[...].astype(BF16)) + gbias_ref[...]
    log_a = (jnp.minimum(pre, 0.0) - jnp.log(1.0 + jnp.exp(-jnp.abs(pre)))) / GLA_TAU
    return pre, log_a


def _gla_terms(g2, q2, k2, fwd_stack):
    bs = _dot3(fwd_stack, g2)
    b, b_ref, b_last = bs[:GLA_ROWS], bs[GLA_ROWS:2 * GLA_ROWS], bs[2 * GLA_ROWS:]
    qs = q2 * GLA_SCALE
    e_q = jnp.exp(b - b_ref)
    e_k = jnp.exp(b_ref - b)
    e_d = jnp.exp(b_last - b)
    e_b = jnp.exp(b)
    decay = jnp.exp(b_last)
    return dict(qs=qs, e_q=e_q, e_k=e_k, e_d=e_d, e_b=e_b, decay=decay,
                qi=qs * e_q, ki=k2 * e_k, kd=k2 * e_d, qb=qs * e_b)


def _gla_fwd(proj, gate_up_pad, gate_bias, gnorm, t, rider):
    nsteps = t // GLA_ROWS
    cps = GLA_ROWS // CHUNK
    n_in, n_out = len(rider.inputs), len(rider.out_shapes)

    def body(q_ref, k_ref, v_ref, gg_ref, ga_ref, gup_ref, gbias_ref, gn_ref, *rest):
        o_ref, oraw_ref, st_ref = rest[n_in:n_in + 3]
        s_scr = rest[n_in + 3 + n_out]
        i = pl.program_id(0)
        _ride(rider, rest[:n_in], rest[n_in + 3:n_in + 3 + n_out], rest[n_in + 4 + n_out:], i == 0, i == nsteps - 1)

        @pl.when(i == 0)
        def _():
            s_scr[...] = jnp.zeros_like(s_scr)

        _, fwd_stack, _, _, lane, blockdiag = _gla_consts()
        causal64 = (lax.broadcasted_iota(jnp.int32, (CHUNK, CHUNK), 1) <= lax.broadcasted_iota(jnp.int32, (CHUNK, CHUNK), 0))
        _, log_a = _gla_gates(ga_ref, gup_ref, gbias_ref)
        gn = gn_ref[...]
        pre = []
        for p in range(2):
            kl = slice(p * LANES, (p + 1) * LANES)
            tm = _gla_terms(log_a[:, kl], q_ref[:, kl], k_ref[:, kl], fwd_stack)
            pre.append(dict(
                v2b=v_ref[:, p * 2 * LANES:(p + 1) * 2 * LANES].astype(BF16), kib=tm["ki"].astype(BF16),
                kdb=tm["kd"].astype(BF16), qbb=tm["qb"].astype(BF16), decay=tm["decay"],
                qihb=[jnp.where((lane // 64) == hh, tm["qi"], 0.0).astype(BF16) for hh in range(2)]))
        state = [s_scr[0], s_scr[1]]
        outs = [[], []]
        for cc in range(cps):
            rows = slice(cc * CHUNK, (cc + 1) * CHUNK)
            for p in range(2):
                w = pre[p]
                st_ref[cc, p] = state[p]
                intra = []
                for hh in range(2):
                    a = jnp.where(causal64, _dot_nt(w["qihb"][hh][rows], w["kib"][rows]), 0.0)
                    intra.append(_dot(a.astype(BF16), w["v2b"][rows, hh * LANES:(hh + 1) * LANES]))
                outs[p].append(jnp.concatenate(intra, axis=1) + _dot(w["qbb"][rows], state[p].astype(BF16)))
                upd = jnp.where(blockdiag, _dot_tn(w["kdb"][rows], w["v2b"][rows]), 0.0)
                dcol = _row_to_col(w["decay"][cc * CHUNK:cc * CHUNK + 1])
                state[p] = state[p] * jnp.concatenate([dcol, dcol], axis=1) + upd
        for p in range(2):
            s_scr[p] = state[p]
            vl = slice(p * 2 * LANES, (p + 1) * 2 * LANES)
            o2 = jnp.concatenate(outs[p], axis=0)
            oraw_ref[:, vl] = o2
            for hh in range(2):
                oh = o2[:, hh * LANES:(hh + 1) * LANES]
                gl = slice(p * 2 * LANES + hh * LANES, p * 2 * LANES + (hh + 1) * LANES)
                rinv = lax.rsqrt(jnp.mean(oh * oh, axis=1, keepdims=True) + RMS_EPS)
                gg = gg_ref[:, gl]
                o_ref[:, gl] = (oh * rinv * gn * (gg * _sigmoid(gg))).astype(BF16)

    cb = lambda w, idx: pl.BlockSpec((GLA_ROWS, w), lambda i: (i, idx))
    full = lambda shp: pl.BlockSpec(shp, lambda i: tuple(0 for _ in shp))
    return pl.pallas_call(
        body, name="gla_fwd", grid=(nsteps,),
        in_specs=[cb(256, 6), cb(256, 7), cb(512, 4), cb(512, 5), cb(128, 24),
                  full((LANES, GLA_KEYS)), full((1, GLA_KEYS)), full((1, LANES))] + [HBM_SPEC] * n_in,
        out_specs=[pl.BlockSpec((GLA_ROWS, GLA_WIDTH), lambda i: (i, 0)),
                   pl.BlockSpec((GLA_ROWS, GLA_WIDTH), lambda i: (i, 0)),
                   pl.BlockSpec((cps, 2, LANES, 2 * LANES), lambda i: (i, 0, 0, 0))] + [HBM_SPEC] * n_out,
        out_shape=[jax.ShapeDtypeStruct((t, GLA_WIDTH), BF16), jax.ShapeDtypeStruct((t, GLA_WIDTH), F32),
                   jax.ShapeDtypeStruct((t // CHUNK, 2, LANES, 2 * LANES), F32)]
        + rider.out_shapes,
        input_output_aliases={8 + i: 3 + o for i, o in rider.aliases},
        scratch_shapes=[pltpu.VMEM((2, LANES, 2 * LANES), F32)] + [pltpu.SemaphoreType.DMA((s,)) for s in rider.sems],
        compiler_params=_cparams("arbitrary"),
    )(proj, proj, proj, proj, proj, gate_up_pad, gate_bias, gnorm, *rider.inputs)


def _gla_bwd(proj, dcat, oraw, states, gate_up_pad, gate_bias, gnorm, t):
    nsteps = t // GLA_ROWS
    cps = GLA_ROWS // CHUNK
    wout = 2 * GLA_KEYS + 2 * GLA_WIDTH + LANES

    def body(q_ref, k_ref, v_ref, gg_ref, ga_ref, do_ref, oraw_ref, st_ref, gup_ref, gbias_ref, gn_ref,
             d_ref, dgup_ref, dgbias_ref, dgn_ref, ds_scr):
        i = pl.program_id(0)

        @pl.when(i == 0)
        def _():
            ds_scr[...] = jnp.zeros_like(ds_scr)
            dgup_ref[...] = jnp.zeros_like(dgup_ref)
            dgbias_ref[...] = jnp.zeros_like(dgbias_ref)
            dgn_ref[...] = jnp.zeros_like(dgn_ref)

        causal, fwd_stack, bwd_stack, rowid, lane, blockdiag = _gla_consts()
        pre, log_a = _gla_gates(ga_ref, gup_ref, gbias_ref)
        gn = gn_ref[...]
        dgn = jnp.zeros((1, LANES), F32)
        dgs = []
        pre_p = []
        for p in range(2):
            kl = slice(p * LANES, (p + 1) * LANES)
            vl = slice(p * 2 * LANES, (p + 1) * 2 * LANES)
            tm = _gla_terms(log_a[:, kl], q_ref[:, kl], k_ref[:, kl], fwd_stack)
            v2b = v_ref[:, vl].astype(BF16)
            dos = []
            for hh in range(2):
                gl = slice(p * 2 * LANES + hh * LANES, p * 2 * LANES + (hh + 1) * LANES)
                oh = oraw_ref[:, gl]
                rinv = lax.rsqrt(jnp.mean(oh * oh, axis=1, keepdims=True) + RMS_EPS)
                on = oh * rinv
                gg = gg_ref[:, gl]
                sg = _sigmoid(gg)
                sil = gg * sg
                dgo = do_ref[:, gl]
                d_ref[:, 2 * GLA_KEYS + GLA_WIDTH + gl.start:2 * GLA_KEYS + GLA_WIDTH + gl.stop] = (
                    dgo * on * gn * (sg * (1.0 + gg * (1.0 - sg)))).astype(BF16)
                dgn = dgn + jnp.sum(dgo * sil * on, axis=0, keepdims=True)
                don = dgo * sil * gn
                dos.append(rinv * (don - on * jnp.mean(don * on, axis=1, keepdims=True)))
            do2b = jnp.concatenate(dos, axis=1).astype(BF16)
            qib = tm["qi"].astype(BF16)
            kib = tm["ki"].astype(BF16)
            kdb = tm["kd"].astype(BF16)
            qbb = tm["qb"].astype(BF16)
            dqi = jnp.zeros((GLA_ROWS, LANES), F32)
            dki = jnp.zeros((GLA_ROWS, LANES), F32)
            dvs = []
            for hh in range(2):
                hm = (lane // 64) == hh
                hl = slice(hh * LANES, (hh + 1) * LANES)
                a = jnp.where(causal, _dot_nt(jnp.where(hm, tm["qi"], 0.0).astype(BF16), kib), 0.0).astype(BF16)
                da = jnp.where(causal, _dot_nt(do2b[:, hl], v2b[:, hl]), 0.0).astype(BF16)
                dvs.append(_dot_tn(a, do2b[:, hl]))
                dqi = dqi + jnp.where(hm, _dot(da, kib), 0.0)
                dki = dki + jnp.where(hm, _dot_tn(da, qib), 0.0)
            pre_p.append(dict(tm=tm, v2b=v2b, do2b=do2b, kdb=kdb, qbb=qbb, dqi=dqi, dki=dki, dvs=dvs))
        dstate = [ds_scr[0], ds_scr[1]]
        rec = [dict(dv_st=[None] * cps, dqb=[None] * cps, dkd=[None] * cps, dd=[None] * cps) for _ in range(2)]
        for cc in reversed(range(cps)):
            rows = slice(cc * CHUNK, (cc + 1) * CHUNK)
            for p in range(2):
                w, ds2 = pre_p[p], dstate[p]
                s_prev = st_ref[cc, p]
                ds2b = ds2.astype(BF16)
                rec[p]["dv_st"][cc] = _dot(w["kdb"][rows], ds2b)
                rec[p]["dqb"][cc] = _dot_nt(w["do2b"][rows], s_prev.astype(BF16))
                rec[p]["dkd"][cc] = _dot_nt(w["v2b"][rows], ds2b)
                decay_row = w["tm"]["decay"][cc * CHUNK:cc * CHUNK + 1]
                ddecay_col = jnp.sum(ds2 * s_prev, axis=1, keepdims=True)
                ddecay_row = jnp.transpose(jnp.broadcast_to(ddecay_col, (LANES, LANES)))[0:1, :]
                rec[p]["dd"][cc] = jnp.broadcast_to(ddecay_row * decay_row, (CHUNK, LANES))
                dcol = _row_to_col(decay_row)
                dstate[p] = (jnp.where(blockdiag, _dot_tn(w["qbb"][rows], w["do2b"][rows]), 0.0)
                             + ds2 * jnp.concatenate([dcol, dcol], axis=1))
        for p in range(2):
            ds_scr[p] = dstate[p]
            tm, dqi, dki = pre_p[p]["tm"], pre_p[p]["dqi"], pre_p[p]["dki"]
            dv2 = jnp.concatenate(pre_p[p]["dvs"], axis=1) + jnp.concatenate(rec[p]["dv_st"], axis=0)
            dqb = jnp.concatenate(rec[p]["dqb"], axis=0)
            dkd = jnp.concatenate(rec[p]["dkd"], axis=0)
            dd = rec[p]["dd"]
            dqs = dqi * tm["e_q"] + dqb * tm["e_b"]
            dk = dki * tm["e_k"] + dkd * tm["e_d"]
            t_qi = dqi * tm["qi"]
            t_ki = dki * tm["ki"]
            t_kd = dkd * tm["kd"]
            db = t_qi - t_ki + dqb * tm["qb"] - t_kd
            to_mid = t_ki - t_qi
            to_last = t_kd + jnp.where(rowid % CHUNK == CHUNK - 1, jnp.concatenate(dd, axis=0), 0.0)
            dgs.append(_dot3(bwd_stack, jnp.concatenate([db, to_mid, to_last], axis=0)))
            d_ref[:, p * LANES:(p + 1) * LANES] = (dqs * GLA_SCALE).astype(BF16)
            d_ref[:, GLA_KEYS + p * LANES:GLA_KEYS + (p + 1) * LANES] = dk.astype(BF16)
            d_ref[:, 2 * GLA_KEYS + p * 2 * LANES:2 * GLA_KEYS + (p + 1) * 2 * LANES] = dv2.astype(BF16)
        dlog_a = jnp.concatenate(dgs, axis=1)
        dpre = dlog_a * (1.0 / GLA_TAU) * _sigmoid(-pre)
        dpb = dpre.astype(BF16)
        dgn_ref[...] += dgn
        dgbias_ref[...] += jnp.sum(dpre, axis=0, keepdims=True)
        dgup_ref[...] += _dot_tn(ga_ref[...].astype(BF16), dpb)
        d_ref[:, 2 * GLA_KEYS + 2 * GLA_WIDTH:] = _dot_nt(dpb, gup_ref[...].astype(BF16)).astype(BF16)

    rev = lambda i: nsteps - 1 - i
    cb = lambda w, idx: pl.BlockSpec((GLA_ROWS, w), lambda i: (rev(i), idx))
    full = lambda shp: pl.BlockSpec(shp, lambda i: tuple(0 for _ in shp))
    return pl.pallas_call(
        body, name="gla_bwd", grid=(nsteps,),
        in_specs=[cb(256, 6), cb(256, 7), cb(512, 4), cb(512, 5), cb(128, 24), cb(512, 1), cb(512, 0),
                  pl.BlockSpec((cps, 2, LANES, 2 * LANES), lambda i: (rev(i), 0, 0, 0)),
                  full((LANES, GLA_KEYS)), full((1, GLA_KEYS)), full((1, LANES))],
        out_specs=[pl.BlockSpec((GLA_ROWS, wout), lambda i: (rev(i), 0)),
                   full((LANES, GLA_KEYS)), full((1, GLA_KEYS)), full((1, LANES))],
        out_shape=[jax.ShapeDtypeStruct((t, wout), BF16), jax.ShapeDtypeStruct((LANES, GLA_KEYS), F32),
                   jax.ShapeDtypeStruct((1, GLA_KEYS), F32), jax.ShapeDtypeStruct((1, LANES), F32)],
        scratch_shapes=[pltpu.VMEM((2, LANES, 2 * LANES), F32)],
        compiler_params=_cparams("arbitrary"),
    )(proj, proj, proj, proj, proj, dcat, oraw, states, gate_up_pad, gate_bias, gnorm)


def _ln_stats(r):
    mu = jnp.mean(r, axis=1, keepdims=True)
    xc = r - mu
    rstd = lax.rsqrt(jnp.mean(xc * xc, axis=1, keepdims=True) + LN_EPS)
    return xc * rstd, rstd


def _ln_bwd(dy_g, xhat, rstd):
    return rstd * (dy_g - jnp.mean(dy_g, axis=1, keepdims=True) - xhat * jnp.mean(dy_g * xhat, axis=1, keepdims=True))


def _outproj_ln1(sb_o, gla_o, x, w_out, g1, b1, t, tm=256):
    def body(sb_ref, gl_ref, x_ref, w_ref, g_ref, b_ref, xhat_ref, rstd_ref, h_ref):
        mix = _dot(sb_ref[...], w_ref[0:SB_WIDTH, :]) + _dot(gl_ref[...], w_ref[SB_WIDTH:, :])
        xhat, rstd = _ln_stats(ALPHA * x_ref[...] + mix)
        xhat_ref[...] = xhat
        rstd_ref[...] = rstd
        h_ref[...] = (xhat * g_ref[...] + b_ref[...]).astype(BF16)

    row = lambda w: pl.BlockSpec((tm, w), lambda i: (i, 0))
    full = lambda shp: pl.BlockSpec(shp, lambda i: (0, 0))
    return pl.pallas_call(
        body, name="outproj_ln1", grid=(t // tm,),
        in_specs=[row(SB_WIDTH), row(GLA_WIDTH), row(D_MODEL), full((D_MODEL, D_MODEL)), full((1, D_MODEL)), full((1, D_MODEL))],
        out_specs=[row(D_MODEL), row(1), row(D_MODEL)],
        out_shape=[jax.ShapeDtypeStruct((t, D_MODEL), F32), jax.ShapeDtypeStruct((t, 1), F32),
                   jax.ShapeDtypeStruct((t, D_MODEL), BF16)],
        compiler_params=_cparams("parallel"),
    )(sb_o, gla_o, x, w_out, g1, b1)


_INV_SQRT2 = 1.0 / math.sqrt(2.0)
_INV_SQRT2PI = 1.0 / math.sqrt(2.0 * math.pi)


def _conv3(xs, w_ref, b_ref, half):
    return (w_ref[half, 0:1, :] * pltpu.roll(xs, 2, 0) + w_ref[half, 1:2, :] * pltpu.roll(xs, 1, 0)
            + w_ref[half, 2:3, :] * xs + b_ref[half])


HALO = 16


def _conv_gelu_fwd(up3, conv_w3, conv_b3, t, tr=512, ct=256):
    nct = D_FF // ct
    hb = tr // HALO

    def body(cur_ref, prev_ref, w_ref, b_ref, gm_ref):
        i = pl.program_id(1)
        keep = (i > 0).astype(F32)
        us = []
        for half in range(2):
            xs = jnp.concatenate([prev_ref[half].astype(F32) * keep, cur_ref[half].astype(F32)], axis=0)
            us.append(_conv3(xs, w_ref, b_ref, half)[HALO:, :])
        a, c = us
        gelu = 0.5 * a * (1.0 + lax.erf(a * _INV_SQRT2))
        gm_ref[...] = (gelu * c).astype(BF16)

    return pl.pallas_call(
        body, name="conv_gelu_fwd", grid=(nct, t // tr),
        in_specs=[pl.BlockSpec((2, tr, ct), lambda j, i: (0, i, j)),
                  pl.BlockSpec((2, HALO, ct), lambda j, i: (0, jnp.maximum(i * hb - 1, 0), j)),
                  pl.BlockSpec((2, 3, ct), lambda j, i: (0, 0, j)),
                  pl.BlockSpec((2, 1, ct), lambda j, i: (0, 0, j))],
        out_specs=pl.BlockSpec((tr, ct), lambda j, i: (i, j)),
        out_shape=jax.ShapeDtypeStruct((t, D_FF), BF16),
        compiler_params=_cparams("parallel", "parallel"),
    )(up3, up3, conv_w3, conv_b3)


def _conv_gelu_bwd(up3, dr2b, w_down, conv_w3, conv_b3, t, tr=512, ct=256):
    nct = D_FF // ct
    nrt = t // tr
    hb = tr // HALO
    n = tr + 2 * HALO
    lo, hi = HALO, tr + HALO

    def body(cur_ref, prev_ref, next_ref, dr_ref, drn_ref, wd_ref, w_ref, b_ref, dup_ref, dcw_ref, dcb_ref):
        i = pl.program_id(1)

        @pl.when(i == 0)
        def _():
            dcw_ref[...] = jnp.zeros_like(dcw_ref)
            dcb_ref[...] = jnp.zeros_like(dcb_ref)

        keep_prev = (i > 0).astype(F32)
        keep_next = (i < nrt - 1).astype(F32)
        xs, xm1, xm2, us = [], [], [], []
        for half in range(2):
            x = jnp.concatenate([prev_ref[half].astype(F32) * keep_prev, cur_ref[half].astype(F32),
                                 next_ref[half].astype(F32)], axis=0)
            xs.append(x)
            xm1.append(pltpu.roll(x, 1, 0))
            xm2.append(pltpu.roll(x, 2, 0))
            us.append(w_ref[half, 0:1, :] * xm2[half] + w_ref[half, 1:2, :] * xm1[half]
                      + w_ref[half, 2:3, :] * x + b_ref[half])
        a, c = us
        wd = wd_ref[...]
        dg = jnp.concatenate([jnp.zeros((HALO, ct), F32), _dot_nt(dr_ref[...], wd),
                              _dot_nt(drn_ref[...], wd) * keep_next], axis=0)
        cdf = 0.5 * (1.0 + lax.erf(a * _INV_SQRT2))
        pdf = jnp.exp(-0.5 * a * a) * _INV_SQRT2PI
        dus = [dg * c * (cdf + a * pdf), dg * (a * cdf)]
        rid = lax.broadcasted_iota(jnp.int32, (8, 1), 0)
        for half in range(2):
            du = dus[half]
            dup = (w_ref[half, 2:3, :] * du + w_ref[half, 1:2, :] * pltpu.roll(du, n - 1, 0)
                   + w_ref[half, 0:1, :] * pltpu.roll(du, n - 2, 0))
            dup_ref[half] = dup[lo:hi, :].astype(BF16)
            duc = du[lo:hi, :]
            s0 = jnp.sum(duc * xm2[half][lo:hi, :], axis=0, keepdims=True)
            s1 = jnp.sum(duc * xm1[half][lo:hi, :], axis=0, keepdims=True)
            s2 = jnp.sum(duc * xs[half][lo:hi, :], axis=0, keepdims=True)
            dcw_ref[half] += jnp.where(rid == 0, s0, jnp.where(rid == 1, s1, jnp.where(rid == 2, s2, 0.0)))
            dcb_ref[half] += jnp.sum(duc, axis=0, keepdims=True)

    last = t // HALO - 1
    return pl.pallas_call(
        body, name="conv_gelu_bwd", grid=(nct, nrt),
        in_specs=[pl.BlockSpec((2, tr, ct), lambda j, i: (0, i, j)),
                  pl.BlockSpec((2, HALO, ct), lambda j, i: (0, jnp.maximum(i * hb - 1, 0), j)),
                  pl.BlockSpec((2, HALO, ct), lambda j, i: (0, jnp.minimum((i + 1) * hb, last), j)),
                  pl.BlockSpec((tr, D_MODEL), lambda j, i: (i, 0)),
                  pl.BlockSpec((HALO, D_MODEL), lambda j, i: (jnp.minimum((i + 1) * hb, last), 0)),
                  pl.BlockSpec((ct, D_MODEL), lambda j, i: (j, 0)),
                  pl.BlockSpec((2, 3, ct), lambda j, i: (0, 0, j)),
                  pl.BlockSpec((2, 1, ct), lambda j, i: (0, 0, j))],
        out_specs=[pl.BlockSpec((2, tr, ct), lambda j, i: (0, i, j)),
                   pl.BlockSpec((2, 8, ct), lambda j, i: (0, 0, j)),
                   pl.BlockSpec((2, 1, ct), lambda j, i: (0, 0, j))],
        out_shape=[jax.ShapeDtypeStruct((2, t, D_FF), BF16), jax.ShapeDtypeStruct((2, 8, D_FF), F32),
                   jax.ShapeDtypeStruct((2, 1, D_FF), F32)],
        compiler_params=_cparams("parallel", "arbitrary"),
    )(up3, up3, up3, dr2b, dr2b, w_down, conv_w3, conv_b3)


def _down_ln2_loss(gm, w_down, xhat1, g1, b1, g2, b2, target, t, tm=256):
    def body(gm_ref, w_ref, xh_ref, g1_ref, b1_ref, g2_ref, b2_ref, tg_ref, dr_ref, drb_ref, loss_ref, dg_ref, db_ref):
        i = pl.program_id(0)

        @pl.when(i == 0)
        def _():
            loss_ref[...] = jnp.zeros_like(loss_ref)
            dg_ref[...] = jnp.zeros_like(dg_ref)
            db_ref[...] = jnp.zeros_like(db_ref)

        h = xh_ref[...] * g1_ref[...] + b1_ref[...]
        xhat, rstd = _ln_stats(ALPHA * h + _dot(gm_ref[...], w_ref[...]))
        err = xhat * g2_ref[...] + b2_ref[...] - tg_ref[...]
        loss_ref[...] += 0.5 * jnp.sum(jnp.sum(err * err, axis=1, keepdims=True), axis=0, keepdims=True) / D_MODEL
        dy = err * (1.0 / D_MODEL)
        dg_ref[...] += jnp.sum(dy * xhat, axis=0, keepdims=True)
        db_ref[...] += jnp.sum(dy, axis=0, keepdims=True)
        dr = _ln_bwd(dy * g2_ref[...], xhat, rstd)
        dr_ref[...] = dr
        drb_ref[...] = dr.astype(BF16)

    row = lambda w: pl.BlockSpec((tm, w), lambda i: (i, 0))
    full = lambda shp: pl.BlockSpec(shp, lambda i: (0, 0))
    vec = full((1, D_MODEL))
    return pl.pallas_call(
        body, name="down_ln2_loss", grid=(t // tm,),
        in_specs=[row(D_FF), full((D_FF, D_MODEL)), row(D_MODEL), vec, vec, vec, vec, row(D_MODEL)],
        out_specs=[row(D_MODEL), row(D_MODEL), full((1, 1)), vec, vec],
        out_shape=[jax.ShapeDtypeStruct((t, D_MODEL), F32), jax.ShapeDtypeStruct((t, D_MODEL), BF16),
                   jax.ShapeDtypeStruct((1, 1), F32),
                   jax.ShapeDtypeStruct((1, D_MODEL), F32), jax.ShapeDtypeStruct((1, D_MODEL), F32)],
        compiler_params=_cparams("arbitrary"),
    )(gm, w_down, xhat1, g1, b1, g2, b2, target)


def _dh_ln1_bwd(dup3, w_up4, dr2, xhat1, rstd1, g1, t, tm=256):
    ws = 2 * D_FF // 4

    def body(a_ref, w_ref, dr2_ref, xh_ref, rs_ref, g_ref, dr1_ref, dg_ref, db_ref):
        i = pl.program_id(0)

        @pl.when(i == 0)
        def _():
            dg_ref[...] = jnp.zeros_like(dg_ref)
            db_ref[...] = jnp.zeros_like(db_ref)

        dh = ALPHA * dr2_ref[...]
        for s in range(4):
            dh = dh + _dot_nt(a_ref[s // 2, :, (s % 2) * ws:(s % 2 + 1) * ws], w_ref[s])
        xhat = xh_ref[...]
        dg_ref[...] += jnp.sum(dh * xhat, axis=0, keepdims=True)
        db_ref[...] += jnp.sum(dh, axis=0, keepdims=True)
        dr1_ref[...] = _ln_bwd(dh * g_ref[...], xhat, rs_ref[...])

    row = lambda w: pl.BlockSpec((tm, w), lambda i: (i, 0))
    vec = pl.BlockSpec((1, D_MODEL), lambda i: (0, 0))
    return pl.pallas_call(
        body, name="dh_ln1_bwd", grid=(t // tm,),
        in_specs=[pl.BlockSpec((2, tm, D_FF), lambda i: (0, i, 0)),
                  pl.BlockSpec((4, D_MODEL, ws), lambda i: (0, 0, 0)),
                  row(D_MODEL), row(D_MODEL), row(1), vec],
        out_specs=[row(D_MODEL), vec, vec],
        out_shape=[jax.ShapeDtypeStruct((t, D_MODEL), F32), jax.ShapeDtypeStruct((1, D_MODEL), F32),
                   jax.ShapeDtypeStruct((1, D_MODEL), F32)],
        compiler_params=_cparams("arbitrary"),
    )(dup3, w_up4, dr2, xhat1, rstd1, g1)


def _adamw(w, g, m, v, name):
    rows, cols = w.shape
    tr = rows
    for cand in (256, 128, 64, 32, 16, 8):
        if rows % cand == 0 and rows > cand:
            tr = cand
            break
    c1 = 1.0 / (1.0 - ADAM_B1 ** ADAM_STEP)
    c2 = 1.0 / (1.0 - ADAM_B2 ** ADAM_STEP)

    def body(w_ref, g_ref, m_ref, v_ref, d_ref, nm_ref, nv_ref):
        gv = g_ref[...]
        nm = ADAM_B1 * m_ref[...] + (1.0 - ADAM_B1) * gv
        nv = ADAM_B2 * v_ref[...] + (1.0 - ADAM_B2) * (gv * gv)
        d_ref[...] = -ADAM_LR * ((nm * c1) / (jnp.sqrt(nv * c2) + ADAM_EPS) + ADAM_WD * w_ref[...])
        nm_ref[...] = nm
        nv_ref[...] = nv

    spec = pl.BlockSpec((tr, cols), lambda i: (i, 0))
    out = jax.ShapeDtypeStruct((rows, cols), F32)
    return pl.pallas_call(
        body, name=name, grid=(rows // tr,), in_specs=[spec] * 4, out_specs=[spec] * 3, out_shape=[out] * 3,
        compiler_params=_cparams("parallel"),
    )(w, g, m, v)


def _local_step(x, target, w_in_p, late_shards, gate_up_pad, gate_bias, gnorm, ln1_g, ln1_b, conv_w3, conv_b3,
                ln2_g, ln2_b, c_arr, kc_arr):
    t = x.shape[0]
    tq = min(t, 1024)
    s_up, s_out, s_down = late_shards
    sh_up, sh_out, sh_down = LATE_SHAPES
    proj, out_partly = _mm(x, w_in_p, m=t, n=IN_PAD, k=D_MODEL, tm=tq, tn=640, tk=D_MODEL, name="proj",
                           rider=_gather_rider([s_out], [sh_out]))
    sb_o, sb_mass, up_partly = _sb_fwd(proj, t, _gather_rider([s_up], [sh_up]))
    gla_o, oraw, states, w_up4, w_out4, down_partly = _gla_fwd(
        proj, gate_up_pad, gate_bias, gnorm, t,
        _join(_forward_rider([up_partly, out_partly], [sh_up, sh_out]), _gather_rider([s_down], [sh_down])))
    w_out = w_out4.reshape(D_MODEL, D_MODEL)
    xhat1, rstd1, h_bf = _outproj_ln1(sb_o, gla_o, x, w_out, ln1_g, ln1_b, t)
    up3, w_down4 = _mm(h_bf, w_up4, m=t, n=2 * D_FF, k=D_MODEL, tm=tq, tn=W_UP_S, tk=D_MODEL, name="up",
                       b_spec=pl.BlockSpec((None, D_MODEL, W_UP_S), lambda i, j, kk: (j, 0, 0)),
                       o_spec=pl.BlockSpec((None, tq, W_UP_S), lambda i, j, kk: (j // 2, i, j % 2)),
                       out_shape=jax.ShapeDtypeStruct((2, t, D_FF), BF16), out_dtype=BF16,
                       rider=_forward_rider([down_partly], [sh_down]))
    w_down = w_down4.reshape(D_FF, D_MODEL)
    gm = _conv_gelu_fwd(up3, conv_w3, conv_b3, t, tr=tq)
    dr2, dr2b, loss, dln2_g, dln2_b = _down_ln2_loss(gm, w_down, xhat1, ln1_g, ln1_b, ln2_g, ln2_b, target, t)
    dw_down = _mm(gm, dr2b, m=D_FF, n=D_MODEL, k=t, tm=W_UP_S, tn=D_MODEL, tk=tq, ta=True, name="dw_down")
    dup3, dcw, dcb = _conv_gelu_bwd(up3, dr2b, w_down, conv_w3, conv_b3, t)
    dr1, dln1_g, dln1_b = _dh_ln1_bwd(dup3, w_up4, dr2, xhat1, rstd1, ln1_g, t)
    dw_up4 = _mm(h_bf, dup3, m=D_MODEL, n=2 * D_FF, k=t, tm=512, tn=W_UP_S, tk=t, ta=True, name="dw_up",
                 b_spec=pl.BlockSpec((None, t, W_UP_S), lambda i, j, kk: (j // 2, kk, j % 2)),
                 o_spec=pl.BlockSpec((None, 512, W_UP_S), lambda i, j, kk: (j, i, 0)),
                 out_shape=jax.ShapeDtypeStruct((4, D_MODEL, W_UP_S), F32))
    dw_out = _dw_out(sb_o, gla_o, dr1, t, tq)
    gs = [dw_up4, dw_out.reshape(4, W_OUT_S, D_MODEL), dw_down.reshape(4, W_DOWN_S, D_MODEL)]
    dcat, *from_sib = _mm(dr1, w_out, m=t, n=D_MODEL, k=D_MODEL, tm=tq, tn=512, tk=D_MODEL, tb=True, name="dcat",
                          rider=_sibling_rider(gs, LATE_SHAPES))
    ps = [_add_sibling(gs[m], from_sib[m], c_arr, LATE_ADD_ROWS[m], "add_sibling_late_%d" % m) for m in range(3)]
    dgla, dgup_pad, dgbias, dgnorm = _gla_bwd(proj, dcat, oraw, states, gate_up_pad, gate_bias, gnorm, t)
    small = dict(
        gate_up=dgup_pad[:GATE_RANK], gate_bias=dgbias, gla_norm_g=dgnorm, ln1_g=dln1_g, ln1_b=dln1_b,
        conv_w=jnp.concatenate([dcw[0, :3], dcw[1, :3]], axis=1), conv_b=jnp.concatenate([dcb[0], dcb[1]], axis=1),
        ln2_g=dln2_g, ln2_b=dln2_b, loss=loss)
    dsq, dsk, dsv, *others, vecs = _sb_bwd(proj, dcat, sb_mass, t,
                                           _chips_rider(ps, _pack_vec(small, SMALL_GRADS, GRAD_VEC_ROWS)))
    late_sums = [_add_chips(ps[m], others[m], kc_arr, LATE_ADD_ROWS[m], "add_chips_late_%d" % m) for m in range(3)]
    dproj = [dsq, dsk, dsv, dgla]
    dw_in_p = _dw_in(x, dproj, t, tk=tq)
    g_in = dw_in_p[None]
    from_sib_in, = _run(_sibling_rider([g_in], [(D_MODEL, IN_PAD)]), "exchange_sibling_w_in")
    half_in = _add_sibling(g_in, from_sib_in, c_arr, ADD_ROWS[0], "add_sibling_w_in")[0]
    p_in = jnp.stack([half_in[:, k * W_IN_S:(k + 1) * W_IN_S] for k in range(4)], axis=0)
    dx, others_in = _dx(dproj, w_in_p, dr1, t, _chips_rider([p_in]))
    sum_in = _add_chips(p_in, others_in, kc_arr, ADD_ROWS[0], "add_chips_w_in")
    return dx, [sum_in] + late_sums, vecs


W_IN_S, W_UP_S, W_OUT_S, W_DOWN_S = IN_WIDTH // 4, 2 * D_FF // 4, D_MODEL // 4, D_FF // 4
SHARD_SHAPES = ((D_MODEL, W_IN_S), (D_MODEL, W_UP_S), (W_OUT_S, D_MODEL), (W_DOWN_S, D_MODEL))
ADD_ROWS = (256, 256, 128, 176)
LATE_SHAPES, LATE_ADD_ROWS = SHARD_SHAPES[1:], ADD_ROWS[1:]
SMALL_ROWS = 8
VEC_SIZES = (("gate_bias", GLA_KEYS), ("gla_norm_g", LANES), ("ln1_g", D_MODEL), ("ln1_b", D_MODEL),
             ("conv_b", 2 * D_FF), ("ln2_g", D_MODEL), ("ln2_b", D_MODEL))
SMALL_GRADS = VEC_SIZES + (("conv_w", 3 * 2 * D_FF), ("gate_up", GATE_RANK * GLA_KEYS), ("loss", 1))


def _rows(a):
    flat = a.reshape(-1)
    pad = (-flat.shape[0]) % D_MODEL
    if pad:
        flat = jnp.pad(flat, (0, pad))
    return flat.reshape(-1, D_MODEL)


def _pad_rows(a, rows):
    return jnp.pad(a, ((0, rows - a.shape[0]), (0, 0)))


def _pack_vec(d, sizes, rows):
    flat = jnp.concatenate([d[n].reshape(-1) for n, _ in sizes])
    return _pad_rows(_rows(flat), rows)


def _unpack_vec(v, sizes):
    flat = v.reshape(-1)
    out, o = {}, 0
    for n, size in sizes:
        out[n] = flat[o:o + size].reshape(1, size)
        o += size
    return out


VEC_ROWS = 16
GRAD_VEC_ROWS = 32


HBM_SPEC = pl.BlockSpec(memory_space=pltpu.HBM)


def _position():
    x, y, c = lax.axis_index("x"), lax.axis_index("y"), lax.axis_index("c")
    chips = [(1 - x, y), (x, 1 - y), (1 - x, 1 - y)]
    return x, y, c, chips


def _remote(src, dst, send_sems, recv_sems, k, to):
    return pltpu.make_async_remote_copy(src_ref=src, dst_ref=dst, send_sem=send_sems.at[k], recv_sem=recv_sems.at[k],
                                        device_id=to, device_id_type=MESH)


def _gather_ici(in_refs, out_refs, shapes, send_sems, recv_sems, local_sems):
    x, y, c, chips = _position()
    k_me = 2 * x + y
    local, sends, recvs = [], [], []
    for m, (rows, _) in enumerate(shapes):
        h = rows // 2
        local.append(pltpu.make_async_copy(in_refs[m], out_refs[m].at[k_me], local_sems.at[m]))
        for j, (cx, cy) in enumerate(chips):
            sends.append(_remote(in_refs[m].at[pl.ds(c * h, h), :], out_refs[m].at[k_me, pl.ds(c * h, h), :],
                                 send_sems, recv_sems, 3 * m + j, (cx, cy, c)))
            landed = out_refs[m].at[2 * cx + cy, pl.ds(c * h, h), :]
            recvs.append(_remote(landed, landed, send_sems, recv_sems, 3 * m + j, (x, y, c)))
    return local, sends, recvs


def _gather_d2d(src_refs, dst_refs, shapes, send_sems, recv_sems, base):
    x, y, c, chips = _position()
    sends, recvs = [], []
    for m, (rows, _) in enumerate(shapes):
        h = rows // 2
        for j, (cx, cy) in enumerate(chips):
            k = 2 * cx + cy
            sends.append(_remote(src_refs[m].at[k, pl.ds(c * h, h), :], dst_refs[m].at[k, pl.ds(c * h, h), :],
                                 send_sems, recv_sems, base + 3 * m + j, (x, y, 1 - c)))
            landed = dst_refs[m].at[k, pl.ds((1 - c) * h, h), :]
            recvs.append(_remote(landed, landed, send_sems, recv_sems, base + 3 * m + j, (x, y, c)))
    return sends, recvs


def _gather_weights(shards, small, shapes):
    nm = len(shards)
    n_ici = 3 * nm

    def body(*refs):
        in_refs, small_ref = refs[:nm], refs[nm]
        out_refs, osm_ref = refs[nm + 1:2 * nm + 1], refs[2 * nm + 1]
        send_sems, recv_sems, local_sems = refs[2 * nm + 2:]
        x, y, c, chips = _position()
        k_me = 2 * x + y
        local, sends, recvs = _gather_ici(in_refs, out_refs, shapes, send_sems, recv_sems, local_sems)
        local.append(pltpu.make_async_copy(small_ref, osm_ref.at[k_me], local_sems.at[nm]))
        for j, (cx, cy) in enumerate(chips):
            sends.append(_remote(small_ref, osm_ref.at[k_me], send_sems, recv_sems, n_ici + j, (cx, cy, c)))
        for cp in local + sends:
            cp.start()
        for cp in recvs:
            cp.wait_recv()
        fsends, frecvs = _gather_d2d(out_refs, out_refs, shapes, send_sems, recv_sems, n_ici + 3)
        for cp in fsends:
            cp.start()
        for j, (cx, cy) in enumerate(chips):
            k = 2 * cx + cy
            frecvs.append(_remote(osm_ref.at[k], osm_ref.at[k], send_sems, recv_sems, n_ici + j, (x, y, c)))
        for cp in frecvs:
            cp.wait_recv()
        for cp in sends + fsends:
            cp.wait_send()
        for cp in local:
            cp.wait()

    n_sems = 2 * n_ici + 3
    return pl.pallas_call(
        body, name="gather_weights", in_specs=[HBM_SPEC] * (nm + 1), out_specs=[HBM_SPEC] * (nm + 1),
        out_shape=[jax.ShapeDtypeStruct((4,) + s, BF16) for s in shapes]
        + [jax.ShapeDtypeStruct((4, SMALL_ROWS, D_MODEL), F32)],
        scratch_shapes=[pltpu.SemaphoreType.DMA((n_sems,)), pltpu.SemaphoreType.DMA((n_sems,)),
                        pltpu.SemaphoreType.DMA((nm + 1,))],
    )(*shards, small)


def _gather_rider(shards, shapes):
    n = len(shards)
    return _Rider(shards, [jax.ShapeDtypeStruct((4,) + s, BF16) for s in shapes], (3 * n, 3 * n, n),
                  lambda ins, outs, sems: _gather_ici(ins, outs, shapes, *sems))


def _forward_rider(gathered, shapes):
    n = len(gathered)
    return _Rider(gathered, [jax.ShapeDtypeStruct(a.shape, a.dtype) for a in gathered], (3 * n, 3 * n),
                  lambda ins, outs, sems: ([],) + _gather_d2d(ins, outs, shapes, sems[0], sems[1], 0),
                  aliases=[(m, m) for m in range(n)])


def _sibling_rider(gs, shapes):
    def copies(ins, outs, sems):
        x, y, c, _ = _position()
        both = []
        for m, (rows, _) in enumerate(shapes):
            h = rows // 2
            for k in range(gs[m].shape[0]):
                both.append(_remote(ins[m].at[k, pl.ds((1 - c) * h, h), :], outs[m].at[k], sems[0], sems[1],
                                    4 * m + k, (x, y, 1 - c)))
        return [], both, both

    return _Rider(gs, [jax.ShapeDtypeStruct((g.shape[0], r // 2, cl), F32) for g, (r, cl) in zip(gs, shapes)],
                  (4 * len(gs), 4 * len(gs)), copies)


def _add_sibling(g, r, c_arr, tr, name):
    nblk, rows, cols = g.shape
    nb = rows // 2 // tr

    def body(c_ref, g_ref, r_ref, o_ref):
        o_ref[...] = (g_ref[...] + r_ref[...]).astype(BF16)

    spec = pl.BlockSpec((None, tr, cols), lambda k, i, c: (k, i, 0))
    return pl.pallas_call(
        body, name=name,
        grid_spec=pltpu.PrefetchScalarGridSpec(
            num_scalar_prefetch=1, grid=(nblk, nb),
            in_specs=[pl.BlockSpec((None, tr, cols), lambda k, i, c: (k, c[0] * nb + i, 0)), spec], out_specs=spec),
        out_shape=jax.ShapeDtypeStruct((nblk, rows // 2, cols), BF16), compiler_params=_cparams("parallel", "parallel"),
    )(c_arr, g, r)


def _reduce_ici(p_refs, r_refs, send_sems, recv_sems):
    x, y, c, chips = _position()
    sends, recvs = [], []
    for m in range(len(p_refs)):
        for j, (cx, cy) in enumerate(chips):
            sends.append(_remote(p_refs[m].at[2 * cx + cy], r_refs[m].at[j], send_sems, recv_sems, 3 * m + j, (cx, cy, c)))
            recvs.append(_remote(r_refs[m].at[j], r_refs[m].at[j], send_sems, recv_sems, 3 * m + j, (x, y, c)))
    return sends, recvs


def _chips_rider(ps, vec=None):
    nm = len(ps)
    n_ici = 3 * nm

    def copies(ins, outs, sems):
        sends, recvs = _reduce_ici(ins[:nm], outs[:nm], sems[0], sems[1])
        if vec is None:
            return [], sends, recvs
        x, y, c, _ = _position()
        my_id = 4 * x + 2 * y + c
        vec_ref, vrecv_ref = ins[nm], outs[nm]
        local = [pltpu.make_async_copy(vec_ref, vrecv_ref.at[my_id], sems[2].at[0])]
        for r in range(1, 8):
            peer = (1 - x if r & 4 else x, 1 - y if r & 2 else y, 1 - c if r & 1 else c)
            sends.append(_remote(vec_ref, vrecv_ref.at[my_id], sems[0], sems[1], n_ici + r - 1, peer))
            recvs.append(_remote(vec_ref, vrecv_ref.at[0], sems[0], sems[1], n_ici + r - 1, (x, y, c)))
        return local, sends, recvs

    out_shapes = [jax.ShapeDtypeStruct((3,) + p.shape[1:], p.dtype) for p in ps]
    if vec is None:
        return _Rider(ps, out_shapes, (n_ici, n_ici), copies)
    return _Rider(list(ps) + [vec], out_shapes + [jax.ShapeDtypeStruct((8, GRAD_VEC_ROWS, D_MODEL), F32)],
                  (n_ici + 7, n_ici + 7, 1), copies)


def _add_chips(p, r, kc_arr, tr, name):
    _, h, cols = p.shape
    nb = h // tr

    def body(kc_ref, p_ref, r0_ref, r1_ref, r2_ref, o_ref):
        o_ref[...] = ((p_ref[...].astype(F32) + r0_ref[...].astype(F32)) + r1_ref[...].astype(F32)) + r2_ref[...].astype(F32)

    rspec = lambda j: pl.BlockSpec((None, tr, cols), lambda i, kc: (j, i, 0))
    return pl.pallas_call(
        body, name=name,
        grid_spec=pltpu.PrefetchScalarGridSpec(
            num_scalar_prefetch=1, grid=(nb,),
            in_specs=[pl.BlockSpec((None, tr, cols), lambda i, kc: (kc[0], i, 0)), rspec(0), rspec(1), rspec(2)],
            out_specs=pl.BlockSpec((tr, cols), lambda i, kc: (kc[1] * nb + i, 0))),
        out_shape=jax.ShapeDtypeStruct((2 * h, cols), F32), compiler_params=_cparams("parallel"),
    )(kc_arr, p, r, r, r)


def _reunite_sibling(fs, shapes):
    n_chunks = 2
    nm = len(fs)

    def body(*refs):
        in_refs, out_refs = refs[:nm], refs[nm:2 * nm]
        send_sems, recv_sems = refs[2 * nm:]
        x, y, c, _ = _position()
        sends, recvs = [], []
        for m in range(nm):
            ch = shapes[m][0] // 2 // n_chunks
            for q in range(n_chunks):
                mine = pl.ds((c * n_chunks + q) * ch, ch)
                theirs = pl.ds(((1 - c) * n_chunks + q) * ch, ch)
                s = n_chunks * m + q
                sends.append(_remote(in_refs[m].at[mine, :], out_refs[m].at[mine, :], send_sems, recv_sems, s, (x, y, 1 - c)))
                recvs.append(_remote(in_refs[m].at[theirs, :], out_refs[m].at[theirs, :], send_sems, recv_sems, s, (x, y, c)))
        for cp in sends:
            cp.start()
        for cp in recvs:
            cp.wait_recv()
        for cp in sends:
            cp.wait_send()

    n_sems = n_chunks * nm
    return pl.pallas_call(
        body, name="reunite_sibling", in_specs=[HBM_SPEC] * nm, out_specs=[HBM_SPEC] * nm,
        out_shape=[jax.ShapeDtypeStruct(s, F32) for s in shapes],
        input_output_aliases={m: m for m in range(nm)},
        scratch_shapes=[pltpu.SemaphoreType.DMA((n_sems,)), pltpu.SemaphoreType.DMA((n_sems,))],
    )(*fs)


def _sum_vecs(v):
    def body(v_ref, o_ref):
        acc = v_ref[0]
        for d in range(1, 8):
            acc = acc + v_ref[d]
        o_ref[...] = acc

    return pl.pallas_call(body, name="sum_vecs", out_shape=jax.ShapeDtypeStruct(v.shape[1:], F32))(v)


def kernel(x, w_in, gate_up, gate_bias, gla_norm_g, w_out, ln1_g, ln1_b, w_up, conv_w, conv_b, w_down, ln2_g, ln2_b, loss_target, m_w_in, m_gate_up, m_gate_bias, m_gla_norm_g, m_w_out, m_ln1_g, m_ln1_b, m_w_up, m_conv_w, m_conv_b, m_w_down, m_ln2_g, m_ln2_b, v_w_in, v_gate_up, v_gate_bias, v_gla_norm_g, v_w_out, v_ln1_g, v_ln1_b, v_w_up, v_conv_w, v_conv_b, v_w_down, v_ln2_g, v_ln2_b):
    xi, yi, ci = lax.axis_index("x"), lax.axis_index("y"), lax.axis_index("c")
    k_me = 2 * xi + yi
    c_arr = ci.astype(jnp.int32).reshape(1)
    kc_arr = jnp.stack([k_me, ci]).astype(jnp.int32)
    small = _pad_rows(jnp.concatenate([_rows(conv_w[0]), _rows(gate_up[0])], axis=0), SMALL_ROWS)
    w_in4, gsmall = _gather_weights([w_in[0].astype(BF16)], small, SHARD_SHAPES[:1])
    late_shards = [w_up[0].astype(BF16), w_out[0].astype(BF16), w_down[0].astype(BF16)]
    w_in_p = jnp.pad(jnp.concatenate([w_in4[k] for k in range(4)], axis=1), ((0, 0), (0, IN_PAD - IN_WIDTH)))
    conv_w_f = jnp.concatenate([gsmall[k, :5].reshape(-1)[:3 * W_UP_S].reshape(3, W_UP_S) for k in range(4)], axis=1)
    gate_up_f = jnp.concatenate([gsmall[k, 5].reshape(GATE_RANK, GLA_KEYS // 4) for k in range(4)], axis=1)
    conv_w3 = jnp.transpose(conv_w_f.reshape(3, 2, D_FF), (1, 0, 2))
    conv_b3 = conv_b.reshape(2, 1, D_FF)
    gate_up_pad = jnp.pad(gate_up_f, ((0, LANES - GATE_RANK), (0, 0)))

    dx, sums, vecs = _local_step(
        x[0], loss_target[0], w_in_p, late_shards, gate_up_pad, gate_bias, gla_norm_g, ln1_g, ln1_b, conv_w3, conv_b3,
        ln2_g, ln2_b, c_arr, kc_arr)
    g_w_in, g_w_up, g_w_out, g_w_down = _reunite_sibling(sums, SHARD_SHAPES)
    gsmall_sum = _unpack_vec(_sum_vecs(vecs), SMALL_GRADS)
    g_conv_w = lax.dynamic_slice_in_dim(gsmall_sum["conv_w"].reshape(3, 2 * D_FF), k_me * W_UP_S, W_UP_S, axis=1)
    g_gate_up = lax.dynamic_slice_in_dim(gsmall_sum["gate_up"].reshape(GATE_RANK, GLA_KEYS), k_me * (GLA_KEYS // 4),
                                         GLA_KEYS // 4, axis=1)
    gv = gsmall_sum
    loss = gv["loss"][0, 0]
    gvec = _pack_vec(gv, VEC_SIZES, VEC_ROWS)

    grads = dict(w_in=g_w_in[None], gate_up=g_gate_up[None], gate_bias=gv["gate_bias"], gla_norm_g=gv["gla_norm_g"],
                 w_out=g_w_out[None], ln1_g=gv["ln1_g"], ln1_b=gv["ln1_b"], w_up=g_w_up[None], conv_w=g_conv_w[None],
                 conv_b=gv["conv_b"], w_down=g_w_down[None], ln2_g=gv["ln2_g"], ln2_b=gv["ln2_b"])
    weights = dict(w_in=w_in, gate_up=gate_up, gate_bias=gate_bias, gla_norm_g=gla_norm_g, w_out=w_out, ln1_g=ln1_g,
                   ln1_b=ln1_b, w_up=w_up, conv_w=conv_w, conv_b=conv_b, w_down=w_down, ln2_g=ln2_g, ln2_b=ln2_b)
    ms = dict(w_in=m_w_in, gate_up=m_gate_up, gate_bias=m_gate_bias, gla_norm_g=m_gla_norm_g, w_out=m_w_out, ln1_g=m_ln1_g,
              ln1_b=m_ln1_b, w_up=m_w_up, conv_w=m_conv_w, conv_b=m_conv_b, w_down=m_w_down, ln2_g=m_ln2_g, ln2_b=m_ln2_b)
    vs = dict(w_in=v_w_in, gate_up=v_gate_up, gate_bias=v_gate_bias, gla_norm_g=v_gla_norm_g, w_out=v_w_out, ln1_g=v_ln1_g,
              ln1_b=v_ln1_b, w_up=v_w_up, conv_w=v_conv_w, conv_b=v_conv_b, w_down=v_w_down, ln2_g=v_ln2_g, ln2_b=v_ln2_b)
    names = ["w_in", "gate_up", "gate_bias", "gla_norm_g", "w_out", "ln1_g", "ln1_b", "w_up", "conv_w", "conv_b", "w_down",
             "ln2_g", "ln2_b"]
    delta, new_m, new_v = {}, {}, {}
    for n in ("w_in", "gate_up", "w_out", "w_up", "conv_w", "w_down"):
        tr_ = jnp.transpose if n == "w_in" else (lambda a: a)
        d, nm, nv = _adamw(tr_(weights[n][0]), tr_(grads[n][0]), tr_(ms[n][0]), tr_(vs[n][0]), "adamw_" + n)
        delta[n], new_m[n], new_v[n] = tr_(d)[None], tr_(nm)[None], tr_(nv)[None]
    d, nm, nv = _adamw(_pack_vec(weights, VEC_SIZES, VEC_ROWS), gvec, _pack_vec(ms, VEC_SIZES, VEC_ROWS),
                       _pack_vec(vs, VEC_SIZES, VEC_ROWS), "adamw_vectors")
    for dst, src in ((delta, d), (new_m, nm), (new_v, nv)):
        dst.update(_unpack_vec(src, VEC_SIZES))
    return (loss, dx[None], *[grads[n] for n in names], *[delta[n] for n in names], *[new_m[n] for n in names],
            *[new_v[n] for n in names])
```

```python
import functools
import math

import jax
import jax.numpy as jnp
from jax import lax
from jax.experimental import pallas as pl
from jax.experimental.pallas import tpu as pltpu

F32 = jnp.float32
BF16 = jnp.bfloat16

D_MODEL = 1024
SB_WIDTH = 512
GLA_KEYS = 256
GLA_WIDTH = 512
GATE_RANK = 16
IN_WIDTH = 3088
IN_PAD = 3200
D_FF = 2816
CHUNK = 64
LN_EPS = 1e-5
RMS_EPS = 1e-6
ALPHA = 2.0 ** 0.25
GLA_TAU = 16.0
SB_SCALE = 0.125
GLA_SCALE = 0.125
LANES = 128
SB_BLK = 256
SB_CUT = -100.0
GLA_ROWS = 256
VMEM_LIMIT = 56 * 1024 * 1024

ADAM_LR, ADAM_B1, ADAM_B2, ADAM_EPS, ADAM_WD, ADAM_STEP = 0.001, 0.9, 0.999, 1e-08, 0.01, 10

MESH = pl.DeviceIdType.MESH


def _cparams(*sem):
    return pltpu.CompilerParams(dimension_semantics=sem, vmem_limit_bytes=VMEM_LIMIT)


def _dot(a, b):
    return jnp.dot(a, b, preferred_element_type=F32)


def _dot_nt(a, b):
    return lax.dot_general(a, b, (((1,), (1,)), ((), ())), preferred_element_type=F32)


def _dot_tn(a, b):
    return lax.dot_general(a, b, (((0,), (0,)), ((), ())), preferred_element_type=F32)


def _split3(x):
    hi = x.astype(BF16)
    r = x - hi.astype(F32)
    mid = r.astype(BF16)
    lo = (r - mid.astype(F32)).astype(BF16)
    return hi, mid, lo


def _softplus(z):
    return jnp.maximum(z, 0.0) + jnp.log(1.0 + jnp.exp(-jnp.abs(z)))


def _sigmoid(z):
    return 1.0 / (1.0 + jnp.exp(-z))


def _mm(a, b, *, m, n, k, tm, tn, tk, ta=False, tb=False, a_spec=None, b_spec=None, o_spec=None,
        out_shape=None, out_dtype=F32, add=None, add_scale=1.0, rider=None, name):
    nk = k // tk
    dn = (((0 if ta else 1,), (1 if tb else 0,)), ((), ()))
    n_in = len(rider.inputs) if rider else 0
    n_out = len(rider.out_shapes) if rider else 0
    n_add = int(add is not None)
    steps = (m // tm, n // tn, nk)

    def body(*refs):
        a_ref, b_ref = refs[:2]
        add_ref = refs[2] if add is not None else None
        rin = refs[2 + n_add:2 + n_add + n_in]
        o_ref = refs[2 + n_add + n_in]
        rout = refs[3 + n_add + n_in:3 + n_add + n_in + n_out]
        scratch = refs[3 + n_add + n_in + n_out:]
        if rider:
            sems = scratch[len(scratch) - len(rider.sems):]
            ids = [pl.program_id(d) for d in range(3)]
            _ride(rider, rin, rout, sems, (ids[0] == 0) & (ids[1] == 0) & (ids[2] == 0),
                  (ids[0] == steps[0] - 1) & (ids[1] == steps[1] - 1) & (ids[2] == steps[2] - 1))

        part = lax.dot_general(a_ref[...].astype(BF16), b_ref[...].astype(BF16), dn, preferred_element_type=F32)

        def finish(r):
            if add is not None:
                r = r + add_scale * add_ref[...]
            o_ref[...] = r.astype(out_dtype)

        if nk == 1:
            finish(part)
            return
        acc_ref = scratch[0]
        kk = pl.program_id(2)

        @pl.when(kk == 0)
        def _():
            acc_ref[...] = part

        @pl.when((kk > 0) & (kk < nk - 1))
        def _():
            acc_ref[...] += part

        @pl.when(kk == nk - 1)
        def _():
            finish(acc_ref[...] + part)

    if a_spec is None:
        a_spec = pl.BlockSpec((tk, tm), lambda i, j, kk: (kk, i)) if ta else pl.BlockSpec((tm, tk), lambda i, j, kk: (i, kk))
    if b_spec is None:
        b_spec = pl.BlockSpec((tn, tk), lambda i, j, kk: (j, kk)) if tb else pl.BlockSpec((tk, tn), lambda i, j, kk: (kk, j))
    if o_spec is None:
        o_spec = pl.BlockSpec((tm, tn), lambda i, j, kk: (i, j))
    if out_shape is None:
        out_shape = jax.ShapeDtypeStruct((m, n), out_dtype)
    in_specs = [a_spec, b_spec]
    args = [a, b]
    if add is not None:
        in_specs.append(pl.BlockSpec((tm, tn), lambda i, j, kk: (i, j)))
        args.append(add)
    scratch = [pltpu.VMEM((tm, tn), F32)] if nk > 1 else []
    if not rider:
        return pl.pallas_call(
            body, name=name, grid=steps, in_specs=in_specs, out_specs=o_spec, out_shape=out_shape,
            scratch_shapes=scratch, compiler_params=_cparams("parallel", "parallel", "arbitrary"),
        )(*args)
    return pl.pallas_call(
        body, name=name, grid=steps, in_specs=in_specs + [HBM_SPEC] * n_in, out_specs=[o_spec] + [HBM_SPEC] * n_out,
        out_shape=[out_shape] + list(rider.out_shapes),
        input_output_aliases={len(args) + i: 1 + o for i, o in rider.aliases},
        scratch_shapes=scratch + [pltpu.SemaphoreType.DMA((s,)) for s in rider.sems],
        compiler_params=_cparams("arbitrary", "arbitrary", "arbitrary"),
    )(*args, *rider.inputs)


def _col_offsets(pieces):
    offs, o = [], 0
    for a in pieces:
        offs.append(o)
        o += a.shape[1]
    return offs


def _dx(pieces, w_in_p, dr1, t, rider, tm=512):
    offs = _col_offsets(pieces)
    npc = len(pieces)
    n_in, n_out = len(rider.inputs), len(rider.out_shapes)
    steps = t // tm

    def body(*refs):
        p_refs, w_ref, add_ref = refs[:npc], refs[npc], refs[npc + 1]
        rin = refs[npc + 2:npc + 2 + n_in]
        o_ref = refs[npc + 2 + n_in]
        rout = refs[npc + 3 + n_in:npc + 3 + n_in + n_out]
        i = pl.program_id(0)
        _ride(rider, rin, rout, refs[npc + 3 + n_in + n_out:], i == 0, i == steps - 1)
        acc = ALPHA * add_ref[...]
        for p_ref, off in zip(p_refs, offs):
            acc = acc + _dot_nt(p_ref[...].astype(BF16), w_ref[:, off:off + p_ref.shape[1]])
        o_ref[...] = acc

    row = lambda w: pl.BlockSpec((tm, w), lambda i: (i, 0))
    return pl.pallas_call(
        body, name="dx", grid=(steps,),
        in_specs=[row(a.shape[1]) for a in pieces] + [pl.BlockSpec(w_in_p.shape, lambda i: (0, 0)), row(D_MODEL)]
        + [HBM_SPEC] * n_in,
        out_specs=[row(D_MODEL)] + [HBM_SPEC] * n_out,
        out_shape=[jax.ShapeDtypeStruct((t, D_MODEL), F32)] + rider.out_shapes,
        scratch_shapes=[pltpu.SemaphoreType.DMA((s,)) for s in rider.sems],
        compiler_params=_cparams("arbitrary"),
    )(*pieces, w_in_p, dr1, *rider.inputs)


def _dw_out(sb_o, gla_o, dr1, t, tk):
    def body(sb_ref, gl_ref, dr_ref, o_ref):
        kk = pl.program_id(0)
        drb = dr_ref[...].astype(BF16)
        top, bottom = _dot_tn(sb_ref[...], drb), _dot_tn(gl_ref[...], drb)

        @pl.when(kk == 0)
        def _():
            o_ref[0:SB_WIDTH, :] = top
            o_ref[SB_WIDTH:, :] = bottom

        @pl.when(kk > 0)
        def _():
            o_ref[0:SB_WIDTH, :] += top
            o_ref[SB_WIDTH:, :] += bottom

    return pl.pallas_call(
        body, name="dw_out", grid=(t // tk,),
        in_specs=[pl.BlockSpec((tk, SB_WIDTH), lambda kk: (kk, 0)), pl.BlockSpec((tk, GLA_WIDTH), lambda kk: (kk, 0)),
                  pl.BlockSpec((tk, D_MODEL), lambda kk: (kk, 0))],
        out_specs=pl.BlockSpec((D_MODEL, D_MODEL), lambda kk: (0, 0)),
        out_shape=jax.ShapeDtypeStruct((D_MODEL, D_MODEL), F32),
        compiler_params=_cparams("arbitrary"),
    )(sb_o, gla_o, dr1)


def _dw_in(x, pieces, t, tm=512, tk=512):
    offs = _col_offsets(pieces)
    npc = len(pieces)
    n = offs[-1] + pieces[-1].shape[1]
    nk = t // tk

    def body(*refs):
        x_ref, p_refs, o_ref, acc_ref = refs[0], refs[1:1 + npc], refs[1 + npc], refs[2 + npc]
        kk = pl.program_id(1)
        xb = x_ref[...].astype(BF16)
        for p_ref, off in zip(p_refs, offs):
            cols = slice(off, off + p_ref.shape[1])
            part = _dot_tn(xb, p_ref[...].astype(BF16))
            if nk == 1:
                o_ref[:, cols] = part
                continue

            @pl.when(kk == 0)
            def _():
                acc_ref[:, cols] = part

            @pl.when((kk > 0) & (kk < nk - 1))
            def _():
                acc_ref[:, cols] += part

            @pl.when(kk == nk - 1)
            def _():
                o_ref[:, cols] = acc_ref[:, cols] + part

    return pl.pallas_call(
        body, name="dw_in", grid=(D_MODEL // tm, nk),
        in_specs=[pl.BlockSpec((tk, tm), lambda i, kk: (kk, i))]
        + [pl.BlockSpec((tk, a.shape[1]), lambda i, kk: (kk, 0)) for a in pieces],
        out_specs=pl.BlockSpec((tm, n), lambda i, kk: (i, 0)),
        out_shape=jax.ShapeDtypeStruct((D_MODEL, n), F32),
        scratch_shapes=[pltpu.VMEM((tm, n), F32)],
        compiler_params=_cparams("parallel", "arbitrary"),
    )(x, *pieces)


class _Rider:
    def __init__(self, inputs, out_shapes, sems, copies, aliases=()):
        self.inputs, self.out_shapes, self.sems, self.copies = list(inputs), list(out_shapes), tuple(sems), copies
        self.aliases = tuple(aliases)


def _join(a, b):
    na_in, na_out, na_sems = len(a.inputs), len(a.out_shapes), len(a.sems)

    def copies(ins, outs, sems):
        first = a.copies(ins[:na_in], outs[:na_out], sems[:na_sems])
        second = b.copies(ins[na_in:], outs[na_out:], sems[na_sems:])
        return tuple(u + v for u, v in zip(first, second))

    return _Rider(a.inputs + b.inputs, a.out_shapes + b.out_shapes, a.sems + b.sems, copies,
                  a.aliases + tuple((i + na_in, o + na_out) for i, o in b.aliases))


def _ride(rider, in_refs, out_refs, sems, first, last):
    @pl.when(first)
    def _():
        local, sends, _ = rider.copies(in_refs, out_refs, sems)
        for cp in local + sends:
            cp.start()

    @pl.when(last)
    def _():
        local, sends, recvs = rider.copies(in_refs, out_refs, sems)
        for cp in recvs:
            cp.wait_recv()
        for cp in sends:
            cp.wait_send()
        for cp in local:
            cp.wait()


def _run(rider, name):
    n_in, n_out = len(rider.inputs), len(rider.out_shapes)

    def body(*refs):
        local, sends, recvs = rider.copies(refs[:n_in], refs[n_in:n_in + n_out], refs[n_in + n_out:])
        for cp in local + sends:
            cp.start()
        for cp in recvs:
            cp.wait_recv()
        for cp in sends:
            cp.wait_send()
        for cp in local:
            cp.wait()

    return pl.pallas_call(
        body, name=name, in_specs=[HBM_SPEC] * n_in, out_specs=[HBM_SPEC] * n_out, out_shape=list(rider.out_shapes),
        scratch_shapes=[pltpu.SemaphoreType.DMA((s,)) for s in rider.sems],
    )(*rider.inputs)


def _sb_tile(qh, kj, diag, strict, u_strict, r_in):
    z = _dot_nt(qh, kj)
    sp = _softplus(z)
    l1m = -sp
    lsz = z - sp
    if diag:
        l1m = jnp.where(strict, l1m, 0.0)
    cs = _dot(l1m.astype(BF16), u_strict) + r_in
    w = jnp.exp(lsz + cs)
    if diag:
        w = jnp.where(strict, w, 0.0)
    return l1m, lsz, w


def _sb_consts():
    row = lax.broadcasted_iota(jnp.int32, (SB_BLK, SB_BLK), 0)
    col = lax.broadcasted_iota(jnp.int32, (SB_BLK, SB_BLK), 1)
    strict = col < row
    u_strict = (row > col).astype(BF16)
    u_pre = (row < col).astype(BF16)
    lane = lax.broadcasted_iota(jnp.int32, (1, LANES), 1)
    return strict, u_strict, u_pre, lane


def _sb_fwd(proj, t, rider):
    nq = t // SB_BLK
    n_in, n_out = len(rider.inputs), len(rider.out_shapes)

    def body(q_ref, k_ref, v_ref, *rest):
        rin, o_ref, sv_ref, rout = rest[:n_in], rest[n_in], rest[n_in + 1], rest[n_in + 2:n_in + 2 + n_out]
        p = pl.program_id(0)
        i = pl.program_id(1)
        _ride(rider, rin, rout, rest[n_in + 2 + n_out:], (p == 0) & (i == 0), (p == 3) & (i == nq - 1))

        strict, u_strict, _, lane = _sb_consts()
        qf = q_ref[...] * SB_SCALE
        hms = [(lane // 64) == hh for hh in range(2)]
        qhs = [jnp.where(hm, qf, 0.0).astype(BF16) for hm in hms]

        def step(j, r0, r1, a, sv, diag, keep=None):
            rows = pl.ds(pl.multiple_of(j * SB_BLK, SB_BLK), SB_BLK)
            kj = k_ref[rows, :].astype(BF16)
            vf = v_ref[rows, :]
            rs = []
            for hh, r in enumerate((r0, r1)):
                l1m, _, w = _sb_tile(qhs[hh], kj, diag, strict, u_strict, r)
                pv = _dot(w.astype(BF16), jnp.where(hms[hh], vf, 0.0).astype(BF16))
                mass = jnp.sum(l1m, axis=1, keepdims=True)
                if keep is not None:
                    pv, mass = jnp.where(keep, pv, 0.0), jnp.where(keep, mass, 0.0)
                a = a + pv
                rs.append(r + mass)
                sv = jnp.where(lane == hh * 64 + (i - j + 1), rs[hh], sv)
            return rs[0], rs[1], a, sv

        zero = jnp.zeros((SB_BLK, 1), F32)
        acc0 = jnp.zeros((SB_BLK, LANES), F32)
        r0, r1, acc, sv = step(i, zero, zero, acc0, acc0, True)
        r0, r1, acc, sv = step(jnp.maximum(i - 1, 0), r0, r1, acc, sv, False, keep=i > 0)
        _, _, _, acc, sv = lax.while_loop(
            lambda c: (c[0] >= 0) & (jnp.maximum(jnp.max(c[1]), jnp.max(c[2])) > SB_CUT),
            lambda c: (c[0] - 1,) + step(c[0], c[1], c[2], c[3], c[4], False),
            (i - 2, r0, r1, acc, sv))
        o_ref[...] = acc.astype(BF16)
        sv_ref[...] = sv

    return pl.pallas_call(
        body, name="sb_fwd", grid=(4, nq),
        in_specs=[pl.BlockSpec((SB_BLK, LANES), lambda p, i: (i, p)),
                  pl.BlockSpec((t, LANES), lambda p, i: (0, 4 + p)),
                  pl.BlockSpec((t, LANES), lambda p, i: (0, 8 + p))] + [HBM_SPEC] * n_in,
        out_specs=[pl.BlockSpec((SB_BLK, LANES), lambda p, i: (i, p))] * 2 + [HBM_SPEC] * n_out,
        out_shape=[jax.ShapeDtypeStruct((t, SB_WIDTH), BF16), jax.ShapeDtypeStruct((t, SB_WIDTH), F32)] + rider.out_shapes,
        scratch_shapes=[pltpu.SemaphoreType.DMA((s,)) for s in rider.sems],
        compiler_params=_cparams("arbitrary", "arbitrary"),
    )(proj, proj, proj, *rider.inputs)


def _sb_bwd(proj, dcat, mass, t, rider):
    nq = t // SB_BLK
    n_in, n_out = len(rider.inputs), len(rider.out_shapes)

    def body(q_ref, k_ref, v_ref, do_ref, sv_ref, *rest):
        rin = rest[:n_in]
        dq_ref, dk_ref, dv_ref = rest[n_in:n_in + 3]
        rout = rest[n_in + 3:n_in + 3 + n_out]
        p = pl.program_id(0)
        i = pl.program_id(1)
        _ride(rider, rin, rout, rest[n_in + 3 + n_out:], (p == 0) & (i == 0), (p == 3) & (i == nq - 1))

        @pl.when(i == 0)
        def _():
            dk_ref[...] = jnp.zeros_like(dk_ref)
            dv_ref[...] = jnp.zeros_like(dv_ref)

        strict, u_strict, u_pre, lane = _sb_consts()
        qf = q_ref[...] * SB_SCALE
        dof = do_ref[...]
        hms = [(lane // 64) == hh for hh in range(2)]
        qhs = [jnp.where(hm, qf, 0.0).astype(BF16) for hm in hms]
        dohs = [jnp.where(hm, dof, 0.0).astype(BF16) for hm in hms]

        sv = sv_ref[...]
        zero = jnp.zeros((SB_BLK, 1), F32)

        def mass_right(hh, d):
            return jnp.sum(jnp.where(lane == hh * 64 + d, sv, 0.0), axis=1, keepdims=True)

        dstop = lax.while_loop(
            lambda d: (i - d >= 0) & (jnp.maximum(jnp.max(mass_right(0, d)), jnp.max(mass_right(1, d))) > SB_CUT),
            lambda d: d + 1, 1)
        jstop = i - dstop

        def step(j, carry, diag, keep=None):
            pre_g0, pre_g1, dqa = carry
            rows = pl.ds(pl.multiple_of(j * SB_BLK, SB_BLK), SB_BLK)
            kf = k_ref[rows, :]
            kj = kf.astype(BF16)
            vj = v_ref[rows, :].astype(BF16)
            dv = jnp.zeros((SB_BLK, LANES), F32)
            dk = jnp.zeros((SB_BLK, LANES), F32)
            dqj = jnp.zeros((SB_BLK, LANES), F32)
            pre = []
            for hh, pre_g in enumerate((pre_g0, pre_g1)):
                _, lsz, w = _sb_tile(qhs[hh], kj, diag, strict, u_strict, zero if diag else mass_right(hh, i - j))
                g = w * _dot_nt(dohs[hh], vj)
                gpre = _dot(g.astype(BF16), u_pre) + pre_g
                sig = jnp.exp(lsz)
                dz = g * (1.0 - sig) - gpre * sig
                if diag:
                    dz = jnp.where(strict, dz, 0.0)
                dzb = dz.astype(BF16)
                dv = dv + _dot_tn(w.astype(BF16), dohs[hh])
                dk = dk + _dot_tn(dzb, qhs[hh])
                dqj = dqj + _dot(dzb, jnp.where(hms[hh], kf, 0.0).astype(BF16))
                gsum = jnp.sum(g, axis=1, keepdims=True)
                pre.append(pre_g + (gsum if keep is None else jnp.where(keep, gsum, 0.0)))
            if keep is not None:
                dv, dk, dqj = jnp.where(keep, dv, 0.0), jnp.where(keep, dk, 0.0), jnp.where(keep, dqj, 0.0)
            dv_ref[rows, :] += dv
            dk_ref[rows, :] += dk
            return pre[0], pre[1], dqa + dqj

        carry = lax.fori_loop(jstop + 1, i - 1, lambda j, c: step(j, c, False), (zero, zero, jnp.zeros((SB_BLK, LANES), F32)))
        carry = step(jnp.maximum(i - 1, 0), carry, False, keep=i > 0)
        _, _, dq = step(i, carry, True)
        dq_ref[...] = (dq * SB_SCALE).astype(BF16)

    return pl.pallas_call(
        body, name="sb_bwd", grid=(4, nq),
        in_specs=[pl.BlockSpec((SB_BLK, LANES), lambda p, i: (i, p)),
                  pl.BlockSpec((t, LANES), lambda p, i: (0, 4 + p)),
                  pl.BlockSpec((t, LANES), lambda p, i: (0, 8 + p)),
                  pl.BlockSpec((SB_BLK, LANES), lambda p, i: (i, p)),
                  pl.BlockSpec((SB_BLK, LANES), lambda p, i: (i, p))] + [HBM_SPEC] * n_in,
        out_specs=[pl.BlockSpec((SB_BLK, LANES), lambda p, i: (i, p)),
                   pl.BlockSpec((t, LANES), lambda p, i: (0, p)),
                   pl.BlockSpec((t, LANES), lambda p, i: (0, p))] + [HBM_SPEC] * n_out,
        out_shape=[jax.ShapeDtypeStruct((t, SB_WIDTH), BF16)] + [jax.ShapeDtypeStruct((t, SB_WIDTH), F32)] * 2
        + rider.out_shapes,
        scratch_shapes=[pltpu.SemaphoreType.DMA((s,)) for s in rider.sems],
        compiler_params=_cparams("arbitrary", "arbitrary"),
    )(proj, proj, proj, dcat, mass, *rider.inputs)


def _gla_consts():
    r = lax.broadcasted_iota(jnp.int32, (GLA_ROWS, GLA_ROWS), 0)
    c = lax.broadcasted_iota(jnp.int32, (GLA_ROWS, GLA_ROWS), 1)
    same = (r // CHUNK) == (c // CHUNK)
    causal = same & (c <= r)
    upto_mid = c % CHUNK <= CHUNK // 2 - 1
    fwd_stack = jnp.concatenate([causal, same & upto_mid, same], axis=0).astype(BF16)
    bwd_stack = jnp.concatenate([same & (c >= r), same & (r % CHUNK <= CHUNK // 2 - 1), same], axis=1).astype(BF16)
    rowid = lax.broadcasted_iota(jnp.int32, (GLA_ROWS, 1), 0)
    lane = lax.broadcasted_iota(jnp.int32, (1, LANES), 1)
    sr = lax.broadcasted_iota(jnp.int32, (LANES, 2 * LANES), 0)
    sc = lax.broadcasted_iota(jnp.int32, (LANES, 2 * LANES), 1)
    blockdiag = (sr // 64) == (sc // LANES)
    return causal, fwd_stack, bwd_stack, rowid, lane, blockdiag


def _dot3(u, x):
    hi, mid, lo = _split3(x)
    return _dot(u, hi) + _dot(u, mid) + _dot(u, lo)


def _row_to_col(row):
    return jnp.transpose(jnp.broadcast_to(row, (LANES, LANES)))


def _gla_gates(ga_ref, gup_ref, gbias_ref):
    pre = _dot(ga_ref[...].astype(BF16), gup_ref[...].astype(BF16)) + gbias_ref[...]
    log_a = (jnp.minimum(pre, 0.0) - jnp.log(1.0 + jnp.exp(-jnp.abs(pre)))) / GLA_TAU
    return pre, log_a


def _gla_terms(g2, q2, k2, fwd_stack):
    bs = _dot3(fwd_stack, g2)
    b, b_ref, b_last = bs[:GLA_ROWS], bs[GLA_ROWS:2 * GLA_ROWS], bs[2 * GLA_ROWS:]
    qs = q2 * GLA_SCALE
    e_q = jnp.exp(b - b_ref)
    e_k = jnp.exp(b_ref - b)
    e_d = jnp.exp(b_last - b)
    e_b = jnp.exp(b)
    decay = jnp.exp(b_last)
    return dict(qs=qs, e_q=e_q, e_k=e_k, e_d=e_d, e_b=e_b, decay=decay,
                qi=qs * e_q, ki=k2 * e_k, kd=k2 * e_d, qb=qs * e_b)


def _gla_fwd(proj, gate_up_pad, gate_bias, gnorm, t, rider):
    nsteps = t // GLA_ROWS
    cps = GLA_ROWS // CHUNK
    n_in, n_out = len(rider.inputs), len(rider.out_shapes)

    def body(q_ref, k_ref, v_ref, gg_ref, ga_ref, gup_ref, gbias_ref, gn_ref, *rest):
        o_ref, oraw_ref, st_ref = rest[n_in:n_in + 3]
        s_scr = rest[n_in + 3 + n_out]
        i = pl.program_id(0)
        _ride(rider, rest[:n_in], rest[n_in + 3:n_in + 3 + n_out], rest[n_in + 4 + n_out:], i == 0, i == nsteps - 1)

        @pl.when(i == 0)
        def _():
            s_scr[...] = jnp.zeros_like(s_scr)

        _, fwd_stack, _, _, lane, blockdiag = _gla_consts()
        causal64 = (lax.broadcasted_iota(jnp.int32, (CHUNK, CHUNK), 1) <= lax.broadcasted_iota(jnp.int32, (CHUNK, CHUNK), 0))
        _, log_a = _gla_gates(ga_ref, gup_ref, gbias_ref)
        gn = gn_ref[...]
        pre = []
        for p in range(2):
            kl = slice(p * LANES, (p + 1) * LANES)
            tm = _gla_terms(log_a[:, kl], q_ref[:, kl], k_ref[:, kl], fwd_stack)
            pre.append(dict(
                v2b=v_ref[:, p * 2 * LANES:(p + 1) * 2 * LANES].astype(BF16), kib=tm["ki"].astype(BF16),
                kdb=tm["kd"].astype(BF16), qbb=tm["qb"].astype(BF16), decay=tm["decay"],
                qihb=[jnp.where((lane // 64) == hh, tm["qi"], 0.0).astype(BF16) for hh in range(2)]))
        state = [s_scr[0], s_scr[1]]
        outs = [[], []]
        for cc in range(cps):
            rows = slice(cc * CHUNK, (cc + 1) * CHUNK)
            for p in range(2):
                w = pre[p]
                st_ref[cc, p] = state[p]
                intra = []
                for hh in range(2):
                    a = jnp.where(causal64, _dot_nt(w["qihb"][hh][rows], w["kib"][rows]), 0.0)
                    intra.append(_dot(a.astype(BF16), w["v2b"][rows, hh * LANES:(hh + 1) * LANES]))
                outs[p].append(jnp.concatenate(intra, axis=1) + _dot(w["qbb"][rows], state[p].astype(BF16)))
                upd = jnp.where(blockdiag, _dot_tn(w["kdb"][rows], w["v2b"][rows]), 0.0)
                dcol = _row_to_col(w["decay"][cc * CHUNK:cc * CHUNK + 1])
                state[p] = state[p] * jnp.concatenate([dcol, dcol], axis=1) + upd
        for p in range(2):
            s_scr[p] = state[p]
            vl = slice(p * 2 * LANES, (p + 1) * 2 * LANES)
            o2 = jnp.concatenate(outs[p], axis=0)
            oraw_ref[:, vl] = o2
            for hh in range(2):
                oh = o2[:, hh * LANES:(hh + 1) * LANES]
                gl = slice(p * 2 * LANES + hh * LANES, p * 2 * LANES + (hh + 1) * LANES)
                rinv = lax.rsqrt(jnp.mean(oh * oh, axis=1, keepdims=True) + RMS_EPS)
                gg = gg_ref[:, gl]
                o_ref[:, gl] = (oh * rinv * gn * (gg * _sigmoid(gg))).astype(BF16)

    cb = lambda w, idx: pl.BlockSpec((GLA_ROWS, w), lambda i: (i, idx))
    full = lambda shp: pl.BlockSpec(shp, lambda i: tuple(0 for _ in shp))
    return pl.pallas_call(
        body, name="gla_fwd", grid=(nsteps,),
        in_specs=[cb(256, 6), cb(256, 7), cb(512, 4), cb(512, 5), cb(128, 24),
                  full((LANES, GLA_KEYS)), full((1, GLA_KEYS)), full((1, LANES))] + [HBM_SPEC] * n_in,
        out_specs=[pl.BlockSpec((GLA_ROWS, GLA_WIDTH), lambda i: (i, 0)),
                   pl.BlockSpec((GLA_ROWS, GLA_WIDTH), lambda i: (i, 0)),
                   pl.BlockSpec((cps, 2, LANES, 2 * LANES), lambda i: (i, 0, 0, 0))] + [HBM_SPEC] * n_out,
        out_shape=[jax.ShapeDtypeStruct((t, GLA_WIDTH), BF16), jax.ShapeDtypeStruct((t, GLA_WIDTH), F32),
                   jax.ShapeDtypeStruct((t // CHUNK, 2, LANES, 2 * LANES), F32)]
        + rider.out_shapes,
        input_output_aliases={8 + i: 3 + o for i, o in rider.aliases},
        scratch_shapes=[pltpu.VMEM((2, LANES, 2 * LANES), F32)] + [pltpu.SemaphoreType.DMA((s,)) for s in rider.sems],
        compiler_params=_cparams("arbitrary"),
    )(proj, proj, proj, proj, proj, gate_up_pad, gate_bias, gnorm, *rider.inputs)


def _gla_bwd(proj, dcat, oraw, states, gate_up_pad, gate_bias, gnorm, t):
    nsteps = t // GLA_ROWS
    cps = GLA_ROWS // CHUNK
    wout = 2 * GLA_KEYS + 2 * GLA_WIDTH + LANES

    def body(q_ref, k_ref, v_ref, gg_ref, ga_ref, do_ref, oraw_ref, st_ref, gup_ref, gbias_ref, gn_ref,
             d_ref, dgup_ref, dgbias_ref, dgn_ref, ds_scr):
        i = pl.program_id(0)

        @pl.when(i == 0)
        def _():
            ds_scr[...] = jnp.zeros_like(ds_scr)
            dgup_ref[...] = jnp.zeros_like(dgup_ref)
            dgbias_ref[...] = jnp.zeros_like(dgbias_ref)
            dgn_ref[...] = jnp.zeros_like(dgn_ref)

        causal, fwd_stack, bwd_stack, rowid, lane, blockdiag = _gla_consts()
        pre, log_a = _gla_gates(ga_ref, gup_ref, gbias_ref)
        gn = gn_ref[...]
        dgn = jnp.zeros((1, LANES), F32)
        dgs = []
        pre_p = []
        for p in range(2):
            kl = slice(p * LANES, (p + 1) * LANES)
            vl = slice(p * 2 * LANES, (p + 1) * 2 * LANES)
            tm = _gla_terms(log_a[:, kl], q_ref[:, kl], k_ref[:, kl], fwd_stack)
            v2b = v_ref[:, vl].astype(BF16)
            dos = []
            for hh in range(2):
                gl = slice(p * 2 * LANES + hh * LANES, p * 2 * LANES + (hh + 1) * LANES)
                oh = oraw_ref[:, gl]
                rinv = lax.rsqrt(jnp.mean(oh * oh, axis=1, keepdims=True) + RMS_EPS)
                on = oh * rinv
                gg = gg_ref[:, gl]
                sg = _sigmoid(gg)
                sil = gg * sg
                dgo = do_ref[:, gl]
                d_ref[:, 2 * GLA_KEYS + GLA_WIDTH + gl.start:2 * GLA_KEYS + GLA_WIDTH + gl.stop] = (
                    dgo * on * gn * (sg * (1.0 + gg * (1.0 - sg)))).astype(BF16)
                dgn = dgn + jnp.sum(dgo * sil * on, axis=0, keepdims=True)
                don = dgo * sil * gn
                dos.append(rinv * (don - on * jnp.mean(don * on, axis=1, keepdims=True)))
            do2b = jnp.concatenate(dos, axis=1).astype(BF16)
            qib = tm["qi"].astype(BF16)
            kib = tm["ki"].astype(BF16)
            kdb = tm["kd"].astype(BF16)
            qbb = tm["qb"].astype(BF16)
            dqi = jnp.zeros((GLA_ROWS, LANES), F32)
            dki = jnp.zeros((GLA_ROWS, LANES), F32)
            dvs = []
            for hh in range(2):
                hm = (lane // 64) == hh
                hl = slice(hh * LANES, (hh + 1) * LANES)
                a = jnp.where(causal, _dot_nt(jnp.where(hm, tm["qi"], 0.0).astype(BF16), kib), 0.0).astype(BF16)
                da = jnp.where(causal, _dot_nt(do2b[:, hl], v2b[:, hl]), 0.0).astype(BF16)
                dvs.append(_dot_tn(a, do2b[:, hl]))
                dqi = dqi + jnp.where(hm, _dot(da, kib), 0.0)
                dki = dki + jnp.where(hm, _dot_tn(da, qib), 0.0)
            pre_p.append(dict(tm=tm, v2b=v2b, do2b=do2b, kdb=kdb, qbb=qbb, dqi=dqi, dki=dki, dvs=dvs))
        dstate = [ds_scr[0], ds_scr[1]]
        rec = [dict(dv_st=[None] * cps, dqb=[None] * cps, dkd=[None] * cps, dd=[None] * cps) for _ in range(2)]
        for cc in reversed(range(cps)):
            rows = slice(cc * CHUNK, (cc + 1) * CHUNK)
            for p in range(2):
                w, ds2 = pre_p[p], dstate[p]
                s_prev = st_ref[cc, p]
                ds2b = ds2.astype(BF16)
                rec[p]["dv_st"][cc] = _dot(w["kdb"][rows], ds2b)
                rec[p]["dqb"][cc] = _dot_nt(w["do2b"][rows], s_prev.astype(BF16))
                rec[p]["dkd"][cc] = _dot_nt(w["v2b"][rows], ds2b)
                decay_row = w["tm"]["decay"][cc * CHUNK:cc * CHUNK + 1]
                ddecay_col = jnp.sum(ds2 * s_prev, axis=1, keepdims=True)
                ddecay_row = jnp.transpose(jnp.broadcast_to(ddecay_col, (LANES, LANES)))[0:1, :]
                rec[p]["dd"][cc] = jnp.broadcast_to(ddecay_row * decay_row, (CHUNK, LANES))
                dcol = _row_to_col(decay_row)
                dstate[p] = (jnp.where(blockdiag, _dot_tn(w["qbb"][rows], w["do2b"][rows]), 0.0)
                             + ds2 * jnp.concatenate([dcol, dcol], axis=1))
        for p in range(2):
            ds_scr[p] = dstate[p]
            tm, dqi, dki = pre_p[p]["tm"], pre_p[p]["dqi"], pre_p[p]["dki"]
            dv2 = jnp.concatenate(pre_p[p]["dvs"], axis=1) + jnp.concatenate(rec[p]["dv_st"], axis=0)
            dqb = jnp.concatenate(rec[p]["dqb"], axis=0)
            dkd = jnp.concatenate(rec[p]["dkd"], axis=0)
            dd = rec[p]["dd"]
            dqs = dqi * tm["e_q"] + dqb * tm["e_b"]
            dk = dki * tm["e_k"] + dkd * tm["e_d"]
            t_qi = dqi * tm["qi"]
            t_ki = dki * tm["ki"]
            t_kd = dkd * tm["kd"]
            db = t_qi - t_ki + dqb * tm["qb"] - t_kd
            to_mid = t_ki - t_qi
            to_last = t_kd + jnp.where(rowid % CHUNK == CHUNK - 1, jnp.concatenate(dd, axis=0), 0.0)
            dgs.append(_dot3(bwd_stack, jnp.concatenate([db, to_mid, to_last], axis=0)))
            d_ref[:, p * LANES:(p + 1) * LANES] = (dqs * GLA_SCALE).astype(BF16)
            d_ref[:, GLA_KEYS + p * LANES:GLA_KEYS + (p + 1) * LANES] = dk.astype(BF16)
            d_ref[:, 2 * GLA_KEYS + p * 2 * LANES:2 * GLA_KEYS + (p + 1) * 2 * LANES] = dv2.astype(BF16)
        dlog_a = jnp.concatenate(dgs, axis=1)
        dpre = dlog_a * (1.0 / GLA_TAU) * _sigmoid(-pre)
        dpb = dpre.astype(BF16)
        dgn_ref[...] += dgn
        dgbias_ref[...] += jnp.sum(dpre, axis=0, keepdims=True)
        dgup_ref[...] += _dot_tn(ga_ref[...].astype(BF16), dpb)
        d_ref[:, 2 * GLA_KEYS + 2 * GLA_WIDTH:] = _dot_nt(dpb, gup_ref[...].astype(BF16)).astype(BF16)

    rev = lambda i: nsteps - 1 - i
    cb = lambda w, idx: pl.BlockSpec((GLA_ROWS, w), lambda i: (rev(i), idx))
    full = lambda shp: pl.BlockSpec(shp, lambda i: tuple(0 for _ in shp))
    return pl.pallas_call(
        body, name="gla_bwd", grid=(nsteps,),
        in_specs=[cb(256, 6), cb(256, 7), cb(512, 4), cb(512, 5), cb(128, 24), cb(512, 1), cb(512, 0),
                  pl.BlockSpec((cps, 2, LANES, 2 * LANES), lambda i: (rev(i), 0, 0, 0)),
                  full((LANES, GLA_KEYS)), full((1, GLA_KEYS)), full((1, LANES))],
        out_specs=[pl.BlockSpec((GLA_ROWS, wout), lambda i: (rev(i), 0)),
                   full((LANES, GLA_KEYS)), full((1, GLA_KEYS)), full((1, LANES))],
        out_shape=[jax.ShapeDtypeStruct((t, wout), BF16), jax.ShapeDtypeStruct((LANES, GLA_KEYS), F32),
                   jax.ShapeDtypeStruct((1, GLA_KEYS), F32), jax.ShapeDtypeStruct((1, LANES), F32)],
        scratch_shapes=[pltpu.VMEM((2, LANES, 2 * LANES), F32)],
        compiler_params=_cparams("arbitrary"),
    )(proj, proj, proj, proj, proj, dcat, oraw, states, gate_up_pad, gate_bias, gnorm)


def _ln_stats(r):
    mu = jnp.mean(r, axis=1, keepdims=True)
    xc = r - mu
    rstd = lax.rsqrt(jnp.mean(xc * xc, axis=1, keepdims=True) + LN_EPS)
    return xc * rstd, rstd


def _ln_bwd(dy_g, xhat, rstd):
    return rstd * (dy_g - jnp.mean(dy_g, axis=1, keepdims=True) - xhat * jnp.mean(dy_g * xhat, axis=1, keepdims=True))


def _outproj_ln1(sb_o, gla_o, x, w_out, g1, b1, t, tm=256):
    def body(sb_ref, gl_ref, x_ref, w_ref, g_ref, b_ref, xhat_ref, rstd_ref, h_ref):
        mix = _dot(sb_ref[...], w_ref[0:SB_WIDTH, :]) + _dot(gl_ref[...], w_ref[SB_WIDTH:, :])
        xhat, rstd = _ln_stats(ALPHA * x_ref[...] + mix)
        xhat_ref[...] = xhat
        rstd_ref[...] = rstd
        h_ref[...] = (xhat * g_ref[...] + b_ref[...]).astype(BF16)

    row = lambda w: pl.BlockSpec((tm, w), lambda i: (i, 0))
    full = lambda shp: pl.BlockSpec(shp, lambda i: (0, 0))
    return pl.pallas_call(
        body, name="outproj_ln1", grid=(t // tm,),
        in_specs=[row(SB_WIDTH), row(GLA_WIDTH), row(D_MODEL), full((D_MODEL, D_MODEL)), full((1, D_MODEL)), full((1, D_MODEL))],
        out_specs=[row(D_MODEL), row(1), row(D_MODEL)],
        out_shape=[jax.ShapeDtypeStruct((t, D_MODEL), F32), jax.ShapeDtypeStruct((t, 1), F32),
                   jax.ShapeDtypeStruct((t, D_MODEL), BF16)],
        compiler_params=_cparams("parallel"),
    )(sb_o, gla_o, x, w_out, g1, b1)


_INV_SQRT2 = 1.0 / math.sqrt(2.0)
_INV_SQRT2PI = 1.0 / math.sqrt(2.0 * math.pi)


def _conv3(xs, w_ref, b_ref, half):
    return (w_ref[half, 0:1, :] * pltpu.roll(xs, 2, 0) + w_ref[half, 1:2, :] * pltpu.roll(xs, 1, 0)
            + w_ref[half, 2:3, :] * xs + b_ref[half])


HALO = 16


def _conv_gelu_fwd(up3, conv_w3, conv_b3, t, tr=512, ct=256):
    nct = D_FF // ct
    hb = tr // HALO

    def body(cur_ref, prev_ref, w_ref, b_ref, gm_ref):
        i = pl.program_id(1)
        keep = (i > 0).astype(F32)
        us = []
        for half in range(2):
            xs = jnp.concatenate([prev_ref[half].astype(F32) * keep, cur_ref[half].astype(F32)], axis=0)
            us.append(_conv3(xs, w_ref, b_ref, half)[HALO:, :])
        a, c = us
        gelu = 0.5 * a * (1.0 + lax.erf(a * _INV_SQRT2))
        gm_ref[...] = (gelu * c).astype(BF16)

    return pl.pallas_call(
        body, name="conv_gelu_fwd", grid=(nct, t // tr),
        in_specs=[pl.BlockSpec((2, tr, ct), lambda j, i: (0, i, j)),
                  pl.BlockSpec((2, HALO, ct), lambda j, i: (0, jnp.maximum(i * hb - 1, 0), j)),
                  pl.BlockSpec((2, 3, ct), lambda j, i: (0, 0, j)),
                  pl.BlockSpec((2, 1, ct), lambda j, i: (0, 0, j))],
        out_specs=pl.BlockSpec((tr, ct), lambda j, i: (i, j)),
        out_shape=jax.ShapeDtypeStruct((t, D_FF), BF16),
        compiler_params=_cparams("parallel", "parallel"),
    )(up3, up3, conv_w3, conv_b3)


def _conv_gelu_bwd(up3, dgm, conv_w3, conv_b3, t, tr=512, ct=256):
    nct = D_FF // ct
    nrt = t // tr
    hb = tr // HALO
    n = tr + 2 * HALO
    lo, hi = HALO, tr + HALO

    def body(cur_ref, prev_ref, next_ref, dg_ref, dgn_ref, w_ref, b_ref, dup_ref, dcw_ref, dcb_ref):
        i = pl.program_id(1)

        @pl.when(i == 0)
        def _():
            dcw_ref[...] = jnp.zeros_like(dcw_ref)
            dcb_ref[...] = jnp.zeros_like(dcb_ref)

        keep_prev = (i > 0).astype(F32)
        keep_next = (i < nrt - 1).astype(F32)
        xs, xm1, xm2, us = [], [], [], []
        for half in range(2):
            x = jnp.concatenate([prev_ref[half].astype(F32) * keep_prev, cur_ref[half].astype(F32),
                                 next_ref[half].astype(F32)], axis=0)
            xs.append(x)
            xm1.append(pltpu.roll(x, 1, 0))
            xm2.append(pltpu.roll(x, 2, 0))
            us.append(w_ref[half, 0:1, :] * xm2[half] + w_ref[half, 1:2, :] * xm1[half]
                      + w_ref[half, 2:3, :] * x + b_ref[half])
        a, c = us
        dg = jnp.concatenate([jnp.zeros((HALO, ct), F32), dg_ref[...].astype(F32),
                              dgn_ref[...].astype(F32) * keep_next], axis=0)
        cdf = 0.5 * (1.0 + lax.erf(a * _INV_SQRT2))
        pdf = jnp.exp(-0.5 * a * a) * _INV_SQRT2PI
        dus = [dg * c * (cdf + a * pdf), dg * (a * cdf)]
        rid = lax.broadcasted_iota(jnp.int32, (8, 1), 0)
        for half in range(2):
            du = dus[half]
            dup = (w_ref[half, 2:3, :] * du + w_ref[half, 1:2, :] * pltpu.roll(du, n - 1, 0)
                   + w_ref[half, 0:1, :] * pltpu.roll(du, n - 2, 0))
            dup_ref[half] = dup[lo:hi, :].astype(BF16)
            duc = du[lo:hi, :]
            s0 = jnp.sum(duc * xm2[half][lo:hi, :], axis=0, keepdims=True)
            s1 = jnp.sum(duc * xm1[half][lo:hi, :], axis=0, keepdims=True)
            s2 = jnp.sum(duc * xs[half][lo:hi, :], axis=0, keepdims=True)
            dcw_ref[half] += jnp.where(rid == 0, s0, jnp.where(rid == 1, s1, jnp.where(rid == 2, s2, 0.0)))
            dcb_ref[half] += jnp.sum(duc, axis=0, keepdims=True)

    last = t // HALO - 1
    return pl.pallas_call(
        body, name="conv_gelu_bwd", grid=(nct, nrt),
        in_specs=[pl.BlockSpec((2, tr, ct), lambda j, i: (0, i, j)),
                  pl.BlockSpec((2, HALO, ct), lambda j, i: (0, jnp.maximum(i * hb - 1, 0), j)),
                  pl.BlockSpec((2, HALO, ct), lambda j, i: (0, jnp.minimum((i + 1) * hb, last), j)),
                  pl.BlockSpec((tr, ct), lambda j, i: (i, j)),
                  pl.BlockSpec((HALO, ct), lambda j, i: (jnp.minimum((i + 1) * hb, last), j)),
                  pl.BlockSpec((2, 3, ct), lambda j, i: (0, 0, j)),
                  pl.BlockSpec((2, 1, ct), lambda j, i: (0, 0, j))],
        out_specs=[pl.BlockSpec((2, tr, ct), lambda j, i: (0, i, j)),
                   pl.BlockSpec((2, 8, ct), lambda j, i: (0, 0, j)),
                   pl.BlockSpec((2, 1, ct), lambda j, i: (0, 0, j))],
        out_shape=[jax.ShapeDtypeStruct((2, t, D_FF), BF16), jax.ShapeDtypeStruct((2, 8, D_FF), F32),
                   jax.ShapeDtypeStruct((2, 1, D_FF), F32)],
        compiler_params=_cparams("parallel", "arbitrary"),
    )(up3, up3, up3, dgm, dgm, conv_w3, conv_b3)


def _down_ln2_loss(gm, w_down, xhat1, g1, b1, g2, b2, target, t, tm=256):
    def body(gm_ref, w_ref, xh_ref, g1_ref, b1_ref, g2_ref, b2_ref, tg_ref, dr_ref, drb_ref, loss_ref, dg_ref, db_ref):
        i = pl.program_id(0)

        @pl.when(i == 0)
        def _():
            loss_ref[...] = jnp.zeros_like(loss_ref)
            dg_ref[...] = jnp.zeros_like(dg_ref)
            db_ref[...] = jnp.zeros_like(db_ref)

        h = xh_ref[...] * g1_ref[...] + b1_ref[...]
        xhat, rstd = _ln_stats(ALPHA * h + _dot(gm_ref[...], w_ref[...]))
        err = xhat * g2_ref[...] + b2_ref[...] - tg_ref[...]
        loss_ref[...] += 0.5 * jnp.sum(jnp.sum(err * err, axis=1, keepdims=True), axis=0, keepdims=True) / D_MODEL
        dy = err * (1.0 / D_MODEL)
        dg_ref[...] += jnp.sum(dy * xhat, axis=0, keepdims=True)
        db_ref[...] += jnp.sum(dy, axis=0, keepdims=True)
        dr = _ln_bwd(dy * g2_ref[...], xhat, rstd)
        dr_ref[...] = dr
        drb_ref[...] = dr.astype(BF16)

    row = lambda w: pl.BlockSpec((tm, w), lambda i: (i, 0))
    full = lambda shp: pl.BlockSpec(shp, lambda i: (0, 0))
    vec = full((1, D_MODEL))
    return pl.pallas_call(
        body, name="down_ln2_loss", grid=(t // tm,),
        in_specs=[row(D_FF), full((D_FF, D_MODEL)), row(D_MODEL), vec, vec, vec, vec, row(D_MODEL)],
        out_specs=[row(D_MODEL), row(D_MODEL), full((1, 1)), vec, vec],
        out_shape=[jax.ShapeDtypeStruct((t, D_MODEL), F32), jax.ShapeDtypeStruct((t, D_MODEL), BF16),
                   jax.ShapeDtypeStruct((1, 1), F32),
                   jax.ShapeDtypeStruct((1, D_MODEL), F32), jax.ShapeDtypeStruct((1, D_MODEL), F32)],
        compiler_params=_cparams("arbitrary"),
    )(gm, w_down, xhat1, g1, b1, g2, b2, target)


def _dh_ln1_bwd(dup3, w_up4, dr2, xhat1, rstd1, g1, t, tm=256):
    ws = 2 * D_FF // 4

    def body(a_ref, w_ref, dr2_ref, xh_ref, rs_ref, g_ref, dr1_ref, dg_ref, db_ref):
        i = pl.program_id(0)

        @pl.when(i == 0)
        def _():
            dg_ref[...] = jnp.zeros_like(dg_ref)
            db_ref[...] = jnp.zeros_like(db_ref)

        dh = ALPHA * dr2_ref[...]
        for s in range(4):
            dh = dh + _dot_nt(a_ref[s // 2, :, (s % 2) * ws:(s % 2 + 1) * ws], w_ref[s])
        xhat = xh_ref[...]
        dg_ref[...] += jnp.sum(dh * xhat, axis=0, keepdims=True)
        db_ref[...] += jnp.sum(dh, axis=0, keepdims=True)
        dr1_ref[...] = _ln_bwd(dh * g_ref[...], xhat, rs_ref[...])

    row = lambda w: pl.BlockSpec((tm, w), lambda i: (i, 0))
    vec = pl.BlockSpec((1, D_MODEL), lambda i: (0, 0))
    return pl.pallas_call(
        body, name="dh_ln1_bwd", grid=(t // tm,),
        in_specs=[pl.BlockSpec((2, tm, D_FF), lambda i: (0, i, 0)),
                  pl.BlockSpec((4, D_MODEL, ws), lambda i: (0, 0, 0)),
                  row(D_MODEL), row(D_MODEL), row(1), vec],
        out_specs=[row(D_MODEL), vec, vec],
        out_shape=[jax.ShapeDtypeStruct((t, D_MODEL), F32), jax.ShapeDtypeStruct((1, D_MODEL), F32),
                   jax.ShapeDtypeStruct((1, D_MODEL), F32)],
        compiler_params=_cparams("arbitrary"),
    )(dup3, w_up4, dr2, xhat1, rstd1, g1)


def _adamw(w, g, m, v, name):
    rows, cols = w.shape
    tr = rows
    for cand in (256, 128, 64, 32, 16, 8):
        if rows % cand == 0 and rows > cand:
            tr = cand
            break
    c1 = 1.0 / (1.0 - ADAM_B1 ** ADAM_STEP)
    c2 = 1.0 / (1.0 - ADAM_B2 ** ADAM_STEP)

    def body(w_ref, g_ref, m_ref, v_ref, d_ref, nm_ref, nv_ref):
        gv = g_ref[...]
        nm = ADAM_B1 * m_ref[...] + (1.0 - ADAM_B1) * gv
        nv = ADAM_B2 * v_ref[...] + (1.0 - ADAM_B2) * (gv * gv)
        d_ref[...] = -ADAM_LR * ((nm * c1) / (jnp.sqrt(nv * c2) + ADAM_EPS) + ADAM_WD * w_ref[...])
        nm_ref[...] = nm
        nv_ref[...] = nv

    spec = pl.BlockSpec((tr, cols), lambda i: (i, 0))
    out = jax.ShapeDtypeStruct((rows, cols), F32)
    return pl.pallas_call(
        body, name=name, grid=(rows // tr,), in_specs=[spec] * 4, out_specs=[spec] * 3, out_shape=[out] * 3,
        compiler_params=_cparams("parallel"),
    )(w, g, m, v)


def _local_step(x, target, w_in_p, late_shards, gate_up_pad, gate_bias, gnorm, ln1_g, ln1_b, conv_w3, conv_b3,
                ln2_g, ln2_b, c_arr, kc_arr):
    t = x.shape[0]
    tq = min(t, 1024)
    s_up, s_out, s_down = late_shards
    sh_up, sh_out, sh_down = LATE_SHAPES
    proj, out_partly = _mm(x, w_in_p, m=t, n=IN_PAD, k=D_MODEL, tm=512, tn=IN_PAD, tk=D_MODEL, name="proj",
                           rider=_gather_rider([s_out], [sh_out]))
    sb_o, sb_mass, up_partly = _sb_fwd(proj, t, _gather_rider([s_up], [sh_up]))
    gla_o, oraw, states, w_up4, w_out4, down_partly = _gla_fwd(
        proj, gate_up_pad, gate_bias, gnorm, t,
        _join(_forward_rider([up_partly, out_partly], [sh_up, sh_out]), _gather_rider([s_down], [sh_down])))
    w_out = w_out4.reshape(D_MODEL, D_MODEL)
    xhat1, rstd1, h_bf = _outproj_ln1(sb_o, gla_o, x, w_out, ln1_g, ln1_b, t)
    up3, w_down4 = _mm(h_bf, w_up4, m=t, n=2 * D_FF, k=D_MODEL, tm=tq, tn=W_UP_S, tk=D_MODEL, name="up",
                       b_spec=pl.BlockSpec((None, D_MODEL, W_UP_S), lambda i, j, kk: (j, 0, 0)),
                       o_spec=pl.BlockSpec((None, tq, W_UP_S), lambda i, j, kk: (j // 2, i, j % 2)),
                       out_shape=jax.ShapeDtypeStruct((2, t, D_FF), BF16), out_dtype=BF16,
                       rider=_forward_rider([down_partly], [sh_down]))
    w_down = w_down4.reshape(D_FF, D_MODEL)
    gm = _conv_gelu_fwd(up3, conv_w3, conv_b3, t, tr=tq)
    dr2, dr2b, loss, dln2_g, dln2_b = _down_ln2_loss(gm, w_down, xhat1, ln1_g, ln1_b, ln2_g, ln2_b, target, t)
    dgm = _mm(dr2b, w_down, m=t, n=D_FF, k=D_MODEL, tm=tq, tn=W_UP_S, tk=D_MODEL, tb=True, out_dtype=BF16, name="dgm")
    dw_down = _mm(gm, dr2b, m=D_FF, n=D_MODEL, k=t, tm=W_UP_S, tn=D_MODEL, tk=tq, ta=True, name="dw_down")
    dup3, dcw, dcb = _conv_gelu_bwd(up3, dgm, conv_w3, conv_b3, t)
    dr1, dln1_g, dln1_b = _dh_ln1_bwd(dup3, w_up4, dr2, xhat1, rstd1, ln1_g, t)
    dw_up4 = _mm(h_bf, dup3, m=D_MODEL, n=2 * D_FF, k=t, tm=512, tn=W_UP_S, tk=t, ta=True, name="dw_up",
                 b_spec=pl.BlockSpec((None, t, W_UP_S), lambda i, j, kk: (j // 2, kk, j % 2)),
                 o_spec=pl.BlockSpec((None, 512, W_UP_S), lambda i, j, kk: (j, i, 0)),
                 out_shape=jax.ShapeDtypeStruct((4, D_MODEL, W_UP_S), F32))
    dw_out = _dw_out(sb_o, gla_o, dr1, t, tq)
    gs = [dw_up4, dw_out.reshape(4, W_OUT_S, D_MODEL), dw_down.reshape(4, W_DOWN_S, D_MODEL)]
    dcat, *from_sib = _mm(dr1, w_out, m=t, n=D_MODEL, k=D_MODEL, tm=tq, tn=512, tk=D_MODEL, tb=True, name="dcat",
                          rider=_sibling_rider(gs, LATE_SHAPES))
    ps = [_add_sibling(gs[m], from_sib[m], c_arr, LATE_ADD_ROWS[m], "add_sibling_late_%d" % m) for m in range(3)]
    dgla, dgup_pad, dgbias, dgnorm = _gla_bwd(proj, dcat, oraw, states, gate_up_pad, gate_bias, gnorm, t)
    small = dict(
        gate_up=dgup_pad[:GATE_RANK], gate_bias=dgbias, gla_norm_g=dgnorm, ln1_g=dln1_g, ln1_b=dln1_b,
        conv_w=jnp.concatenate([dcw[0, :3], dcw[1, :3]], axis=1), conv_b=jnp.concatenate([dcb[0], dcb[1]], axis=1),
        ln2_g=dln2_g, ln2_b=dln2_b, loss=loss)
    dsq, dsk, dsv, *others, vecs = _sb_bwd(proj, dcat, sb_mass, t,
                                           _chips_rider(ps, _pack_vec(small, SMALL_GRADS, GRAD_VEC_ROWS)))
    late_sums = [_add_chips(ps[m], others[m], kc_arr, LATE_ADD_ROWS[m], "add_chips_late_%d" % m) for m in range(3)]
    dproj = [dsq, dsk, dsv, dgla]
    dw_in_p = _dw_in(x, dproj, t, tk=tq)
    g_in = dw_in_p[None]
    from_sib_in, = _run(_sibling_rider([g_in], [(D_MODEL, IN_PAD)]), "exchange_sibling_w_in")
    half_in = _add_sibling(g_in, from_sib_in, c_arr, ADD_ROWS[0], "add_sibling_w_in")[0]
    p_in = jnp.stack([half_in[:, k * W_IN_S:(k + 1) * W_IN_S] for k in range(4)], axis=0)
    dx, others_in = _dx(dproj, w_in_p, dr1, t, _chips_rider([p_in]))
    sum_in = _add_chips(p_in, others_in, kc_arr, ADD_ROWS[0], "add_chips_w_in")
    return dx, [sum_in] + late_sums, vecs


W_IN_S, W_UP_S, W_OUT_S, W_DOWN_S = IN_WIDTH // 4, 2 * D_FF // 4, D_MODEL // 4, D_FF // 4
SHARD_SHAPES = ((D_MODEL, W_IN_S), (D_MODEL, W_UP_S), (W_OUT_S, D_MODEL), (W_DOWN_S, D_MODEL))
ADD_ROWS = (256, 256, 128, 176)
LATE_SHAPES, LATE_ADD_ROWS = SHARD_SHAPES[1:], ADD_ROWS[1:]
SMALL_ROWS = 8
VEC_SIZES = (("gate_bias", GLA_KEYS), ("gla_norm_g", LANES), ("ln1_g", D_MODEL), ("ln1_b", D_MODEL),
             ("conv_b", 2 * D_FF), ("ln2_g", D_MODEL), ("ln2_b", D_MODEL))
SMALL_GRADS = VEC_SIZES + (("conv_w", 3 * 2 * D_FF), ("gate_up", GATE_RANK * GLA_KEYS), ("loss", 1))


def _rows(a):
    flat = a.reshape(-1)
    pad = (-flat.shape[0]) % D_MODEL
    if pad:
        flat = jnp.pad(flat, (0, pad))
    return flat.reshape(-1, D_MODEL)


def _pad_rows(a, rows):
    return jnp.pad(a, ((0, rows - a.shape[0]), (0, 0)))


def _pack_vec(d, sizes, rows):
    flat = jnp.concatenate([d[n].reshape(-1) for n, _ in sizes])
    return _pad_rows(_rows(flat), rows)


def _unpack_vec(v, sizes):
    flat = v.reshape(-1)
    out, o = {}, 0
    for n, size in sizes:
        out[n] = flat[o:o + size].reshape(1, size)
        o += size
    return out


VEC_ROWS = 16
GRAD_VEC_ROWS = 32


HBM_SPEC = pl.BlockSpec(memory_space=pltpu.HBM)


def _position():
    x, y, c = lax.axis_index("x"), lax.axis_index("y"), lax.axis_index("c")
    chips = [(1 - x, y), (x, 1 - y), (1 - x, 1 - y)]
    return x, y, c, chips


def _remote(src, dst, send_sems, recv_sems, k, to):
    return pltpu.make_async_remote_copy(src_ref=src, dst_ref=dst, send_sem=send_sems.at[k], recv_sem=recv_sems.at[k],
                                        device_id=to, device_id_type=MESH)


def _gather_ici(in_refs, out_refs, shapes, send_sems, recv_sems, local_sems):
    x, y, c, chips = _position()
    k_me = 2 * x + y
    local, sends, recvs = [], [], []
    for m, (rows, _) in enumerate(shapes):
        h = rows // 2
        local.append(pltpu.make_async_copy(in_refs[m], out_refs[m].at[k_me], local_sems.at[m]))
        for j, (cx, cy) in enumerate(chips):
            sends.append(_remote(in_refs[m].at[pl.ds(c * h, h), :], out_refs[m].at[k_me, pl.ds(c * h, h), :],
                                 send_sems, recv_sems, 3 * m + j, (cx, cy, c)))
            landed = out_refs[m].at[2 * cx + cy, pl.ds(c * h, h), :]
            recvs.append(_remote(landed, landed, send_sems, recv_sems, 3 * m + j, (x, y, c)))
    return local, sends, recvs


def _gather_d2d(src_refs, dst_refs, shapes, send_sems, recv_sems, base):
    x, y, c, chips = _position()
    sends, recvs = [], []
    for m, (rows, _) in enumerate(shapes):
        h = rows // 2
        for j, (cx, cy) in enumerate(chips):
            k = 2 * cx + cy
            sends.append(_remote(src_refs[m].at[k, pl.ds(c * h, h), :], dst_refs[m].at[k, pl.ds(c * h, h), :],
                                 send_sems, recv_sems, base + 3 * m + j, (x, y, 1 - c)))
            landed = dst_refs[m].at[k, pl.ds((1 - c) * h, h), :]
            recvs.append(_remote(landed, landed, send_sems, recv_sems, base + 3 * m + j, (x, y, c)))
    return sends, recvs


def _gather_weights(shards, small, shapes):
    nm = len(shards)
    n_ici = 3 * nm

    def body(*refs):
        in_refs, small_ref = refs[:nm], refs[nm]
        out_refs, osm_ref = refs[nm + 1:2 * nm + 1], refs[2 * nm + 1]
        send_sems, recv_sems, local_sems = refs[2 * nm + 2:]
        x, y, c, chips = _position()
        k_me = 2 * x + y
        local, sends, recvs = _gather_ici(in_refs, out_refs, shapes, send_sems, recv_sems, local_sems)
        local.append(pltpu.make_async_copy(small_ref, osm_ref.at[k_me], local_sems.at[nm]))
        for j, (cx, cy) in enumerate(chips):
            sends.append(_remote(small_ref, osm_ref.at[k_me], send_sems, recv_sems, n_ici + j, (cx, cy, c)))
        for cp in local + sends:
            cp.start()
        for cp in recvs:
            cp.wait_recv()
        fsends, frecvs = _gather_d2d(out_refs, out_refs, shapes, send_sems, recv_sems, n_ici + 3)
        for cp in fsends:
            cp.start()
        for j, (cx, cy) in enumerate(chips):
            k = 2 * cx + cy
            frecvs.append(_remote(osm_ref.at[k], osm_ref.at[k], send_sems, recv_sems, n_ici + j, (x, y, c)))
        for cp in frecvs:
            cp.wait_recv()
        for cp in sends + fsends:
            cp.wait_send()
        for cp in local:
            cp.wait()

    n_sems = 2 * n_ici + 3
    return pl.pallas_call(
        body, name="gather_weights", in_specs=[HBM_SPEC] * (nm + 1), out_specs=[HBM_SPEC] * (nm + 1),
        out_shape=[jax.ShapeDtypeStruct((4,) + s, BF16) for s in shapes]
        + [jax.ShapeDtypeStruct((4, SMALL_ROWS, D_MODEL), F32)],
        scratch_shapes=[pltpu.SemaphoreType.DMA((n_sems,)), pltpu.SemaphoreType.DMA((n_sems,)),
                        pltpu.SemaphoreType.DMA((nm + 1,))],
    )(*shards, small)


def _gather_rider(shards, shapes):
    n = len(shards)
    return _Rider(shards, [jax.ShapeDtypeStruct((4,) + s, BF16) for s in shapes], (3 * n, 3 * n, n),
                  lambda ins, outs, sems: _gather_ici(ins, outs, shapes, *sems))


def _forward_rider(gathered, shapes):
    n = len(gathered)
    return _Rider(gathered, [jax.ShapeDtypeStruct(a.shape, a.dtype) for a in gathered], (3 * n, 3 * n),
                  lambda ins, outs, sems: ([],) + _gather_d2d(ins, outs, shapes, sems[0], sems[1], 0),
                  aliases=[(m, m) for m in range(n)])


def _sibling_rider(gs, shapes):
    def copies(ins, outs, sems):
        x, y, c, _ = _position()
        both = []
        for m, (rows, _) in enumerate(shapes):
            h = rows // 2
            for k in range(gs[m].shape[0]):
                both.append(_remote(ins[m].at[k, pl.ds((1 - c) * h, h), :], outs[m].at[k], sems[0], sems[1],
                                    4 * m + k, (x, y, 1 - c)))
        return [], both, both

    return _Rider(gs, [jax.ShapeDtypeStruct((g.shape[0], r // 2, cl), F32) for g, (r, cl) in zip(gs, shapes)],
                  (4 * len(gs), 4 * len(gs)), copies)


def _add_sibling(g, r, c_arr, tr, name):
    nblk, rows, cols = g.shape
    nb = rows // 2 // tr

    def body(c_ref, g_ref, r_ref, o_ref):
        o_ref[...] = (g_ref[...] + r_ref[...]).astype(BF16)

    spec = pl.BlockSpec((None, tr, cols), lambda k, i, c: (k, i, 0))
    return pl.pallas_call(
        body, name=name,
        grid_spec=pltpu.PrefetchScalarGridSpec(
            num_scalar_prefetch=1, grid=(nblk, nb),
            in_specs=[pl.BlockSpec((None, tr, cols), lambda k, i, c: (k, c[0] * nb + i, 0)), spec], out_specs=spec),
        out_shape=jax.ShapeDtypeStruct((nblk, rows // 2, cols), BF16), compiler_params=_cparams("parallel", "parallel"),
    )(c_arr, g, r)


def _reduce_ici(p_refs, r_refs, send_sems, recv_sems):
    x, y, c, chips = _position()
    sends, recvs = [], []
    for m in range(len(p_refs)):
        for j, (cx, cy) in enumerate(chips):
            sends.append(_remote(p_refs[m].at[2 * cx + cy], r_refs[m].at[j], send_sems, recv_sems, 3 * m + j, (cx, cy, c)))
            recvs.append(_remote(r_refs[m].at[j], r_refs[m].at[j], send_sems, recv_sems, 3 * m + j, (x, y, c)))
    return sends, recvs


def _chips_rider(ps, vec=None):
    nm = len(ps)
    n_ici = 3 * nm

    def copies(ins, outs, sems):
        sends, recvs = _reduce_ici(ins[:nm], outs[:nm], sems[0], sems[1])
        if vec is None:
            return [], sends, recvs
        x, y, c, _ = _position()
        my_id = 4 * x + 2 * y + c
        vec_ref, vrecv_ref = ins[nm], outs[nm]
        local = [pltpu.make_async_copy(vec_ref, vrecv_ref.at[my_id], sems[2].at[0])]
        for r in range(1, 8):
            peer = (1 - x if r & 4 else x, 1 - y if r & 2 else y, 1 - c if r & 1 else c)
            sends.append(_remote(vec_ref, vrecv_ref.at[my_id], sems[0], sems[1], n_ici + r - 1, peer))
            recvs.append(_remote(vec_ref, vrecv_ref.at[0], sems[0], sems[1], n_ici + r - 1, (x, y, c)))
        return local, sends, recvs

    out_shapes = [jax.ShapeDtypeStruct((3,) + p.shape[1:], p.dtype) for p in ps]
    if vec is None:
        return _Rider(ps, out_shapes, (n_ici, n_ici), copies)
    return _Rider(list(ps) + [vec], out_shapes + [jax.ShapeDtypeStruct((8, GRAD_VEC_ROWS, D_MODEL), F32)],
                  (n_ici + 7, n_ici + 7, 1), copies)


def _add_chips(p, r, kc_arr, tr, name):
    _, h, cols = p.shape
    nb = h // tr

    def body(kc_ref, p_ref, r0_ref, r1_ref, r2_ref, o_ref):
        o_ref[...] = ((p_ref[...].astype(F32) + r0_ref[...].astype(F32)) + r1_ref[...].astype(F32)) + r2_ref[...].astype(F32)

    rspec = lambda j: pl.BlockSpec((None, tr, cols), lambda i, kc: (j, i, 0))
    return pl.pallas_call(
        body, name=name,
        grid_spec=pltpu.PrefetchScalarGridSpec(
            num_scalar_prefetch=1, grid=(nb,),
            in_specs=[pl.BlockSpec((None, tr, cols), lambda i, kc: (kc[0], i, 0)), rspec(0), rspec(1), rspec(2)],
            out_specs=pl.BlockSpec((tr, cols), lambda i, kc: (kc[1] * nb + i, 0))),
        out_shape=jax.ShapeDtypeStruct((2 * h, cols), F32), compiler_params=_cparams("parallel"),
    )(kc_arr, p, r, r, r)


def _reunite_sibling(fs, shapes):
    n_chunks = 2
    nm = len(fs)

    def body(*refs):
        in_refs, out_refs = refs[:nm], refs[nm:2 * nm]
        send_sems, recv_sems = refs[2 * nm:]
        x, y, c, _ = _position()
        sends, recvs = [], []
        for m in range(nm):
            ch = shapes[m][0] // 2 // n_chunks
            for q in range(n_chunks):
                mine = pl.ds((c * n_chunks + q) * ch, ch)
                theirs = pl.ds(((1 - c) * n_chunks + q) * ch, ch)
                s = n_chunks * m + q
                sends.append(_remote(in_refs[m].at[mine, :], out_refs[m].at[mine, :], send_sems, recv_sems, s, (x, y, 1 - c)))
                recvs.append(_remote(in_refs[m].at[theirs, :], out_refs[m].at[theirs, :], send_sems, recv_sems, s, (x, y, c)))
        for cp in sends:
            cp.start()
        for cp in recvs:
            cp.wait_recv()
        for cp in sends:
            cp.wait_send()

    n_sems = n_chunks * nm
    return pl.pallas_call(
        body, name="reunite_sibling", in_specs=[HBM_SPEC] * nm, out_specs=[HBM_SPEC] * nm,
        out_shape=[jax.ShapeDtypeStruct(s, F32) for s in shapes],
        input_output_aliases={m: m for m in range(nm)},
        scratch_shapes=[pltpu.SemaphoreType.DMA((n_sems,)), pltpu.SemaphoreType.DMA((n_sems,))],
    )(*fs)


def _sum_vecs(v):
    def body(v_ref, o_ref):
        acc = v_ref[0]
        for d in range(1, 8):
            acc = acc + v_ref[d]
        o_ref[...] = acc

    return pl.pallas_call(body, name="sum_vecs", out_shape=jax.ShapeDtypeStruct(v.shape[1:], F32))(v)


def kernel(x, w_in, gate_up, gate_bias, gla_norm_g, w_out, ln1_g, ln1_b, w_up, conv_w, conv_b, w_down, ln2_g, ln2_b, loss_target, m_w_in, m_gate_up, m_gate_bias, m_gla_norm_g, m_w_out, m_ln1_g, m_ln1_b, m_w_up, m_conv_w, m_conv_b, m_w_down, m_ln2_g, m_ln2_b, v_w_in, v_gate_up, v_gate_bias, v_gla_norm_g, v_w_out, v_ln1_g, v_ln1_b, v_w_up, v_conv_w, v_conv_b, v_w_down, v_ln2_g, v_ln2_b):
    xi, yi, ci = lax.axis_index("x"), lax.axis_index("y"), lax.axis_index("c")
    k_me = 2 * xi + yi
    c_arr = ci.astype(jnp.int32).reshape(1)
    kc_arr = jnp.stack([k_me, ci]).astype(jnp.int32)
    small = _pad_rows(jnp.concatenate([_rows(conv_w[0]), _rows(gate_up[0])], axis=0), SMALL_ROWS)
    w_in4, gsmall = _gather_weights([w_in[0].astype(BF16)], small, SHARD_SHAPES[:1])
    late_shards = [w_up[0].astype(BF16), w_out[0].astype(BF16), w_down[0].astype(BF16)]
    w_in_p = jnp.pad(jnp.concatenate([w_in4[k] for k in range(4)], axis=1), ((0, 0), (0, IN_PAD - IN_WIDTH)))
    conv_w_f = jnp.concatenate([gsmall[k, :5].reshape(-1)[:3 * W_UP_S].reshape(3, W_UP_S) for k in range(4)], axis=1)
    gate_up_f = jnp.concatenate([gsmall[k, 5].reshape(GATE_RANK, GLA_KEYS // 4) for k in range(4)], axis=1)
    conv_w3 = jnp.transpose(conv_w_f.reshape(3, 2, D_FF), (1, 0, 2))
    conv_b3 = conv_b.reshape(2, 1, D_FF)
    gate_up_pad = jnp.pad(gate_up_f, ((0, LANES - GATE_RANK), (0, 0)))

    dx, sums, vecs = _local_step(
        x[0], loss_target[0], w_in_p, late_shards, gate_up_pad, gate_bias, gla_norm_g, ln1_g, ln1_b, conv_w3, conv_b3,
        ln2_g, ln2_b, c_arr, kc_arr)
    g_w_in, g_w_up, g_w_out, g_w_down = _reunite_sibling(sums, SHARD_SHAPES)
    gsmall_sum = _unpack_vec(_sum_vecs(vecs), SMALL_GRADS)
    g_conv_w = lax.dynamic_slice_in_dim(gsmall_sum["conv_w"].reshape(3, 2 * D_FF), k_me * W_UP_S, W_UP_S, axis=1)
    g_gate_up = lax.dynamic_slice_in_dim(gsmall_sum["gate_up"].reshape(GATE_RANK, GLA_KEYS), k_me * (GLA_KEYS // 4),
                                         GLA_KEYS // 4, axis=1)
    gv = gsmall_sum
    loss = gv["loss"][0, 0]
    gvec = _pack_vec(gv, VEC_SIZES, VEC_ROWS)

    grads = dict(w_in=g_w_in[None], gate_up=g_gate_up[None], gate_bias=gv["gate_bias"], gla_norm_g=gv["gla_norm_g"],
                 w_out=g_w_out[None], ln1_g=gv["ln1_g"], ln1_b=gv["ln1_b"], w_up=g_w_up[None], conv_w=g_conv_w[None],
                 conv_b=gv["conv_b"], w_down=g_w_down[None], ln2_g=gv["ln2_g"], ln2_b=gv["ln2_b"])
    weights = dict(w_in=w_in, gate_up=gate_up, gate_bias=gate_bias, gla_norm_g=gla_norm_g, w_out=w_out, ln1_g=ln1_g,
                   ln1_b=ln1_b, w_up=w_up, conv_w=conv_w, conv_b=conv_b, w_down=w_down, ln2_g=ln2_g, ln2_b=ln2_b)
    ms = dict(w_in=m_w_in, gate_up=m_gate_up, gate_bias=m_gate_bias, gla_norm_g=m_gla_norm_g, w_out=m_w_out, ln1_g=m_ln1_g,
              ln1_b=m_ln1_b, w_up=m_w_up, conv_w=m_conv_w, conv_b=m_conv_b, w_down=m_w_down, ln2_g=m_ln2_g, ln2_b=m_ln2_b)
    vs = dict(w_in=v_w_in, gate_up=v_gate_up, gate_bias=v_gate_bias, gla_norm_g=v_gla_norm_g, w_out=v_w_out, ln1_g=v_ln1_g,
              ln1_b=v_ln1_b, w_up=v_w_up, conv_w=v_conv_w, conv_b=v_conv_b, w_down=v_w_down, ln2_g=v_ln2_g, ln2_b=v_ln2_b)
    names = ["w_in", "gate_up", "gate_bias", "gla_norm_g", "w_out", "ln1_g", "ln1_b", "w_up", "conv_w", "conv_b", "w_down",
             "ln2_g", "ln2_b"]
    delta, new_m, new_v = {}, {}, {}
    for n in ("w_in", "gate_up", "w_out", "w_up", "conv_w", "w_down"):
        tr_ = jnp.transpose if n == "w_in" else (lambda a: a)
        d, nm, nv = _adamw(tr_(weights[n][0]), tr_(grads[n][0]), tr_(ms[n][0]), tr_(vs[n][0]), "adamw_" + n)
        delta[n], new_m[n], new_v[n] = tr_(d)[None], tr_(nm)[None], tr_(nv)[None]
    d, nm, nv = _adamw(_pack_vec(weights, VEC_SIZES, VEC_ROWS), gvec, _pack_vec(ms, VEC_SIZES, VEC_ROWS),
                       _pack_vec(vs, VEC_SIZES, VEC_ROWS), "adamw_vectors")
    for dst, src in ((delta, d), (new_m, nm), (new_v, nv)):
        dst.update(_unpack_vec(src, VEC_SIZES))
    return (loss, dx[None], *[grads[n] for n in names], *[delta[n] for n in names], *[new_m[n] for n in names],
            *[new_v[n] for n in names])
```

```python
import functools
import math

import jax
import jax.numpy as jnp
from jax import lax
from jax.experimental import pallas as pl
from jax.experimental.pallas import tpu as pltpu

F32 = jnp.float32
BF16 = jnp.bfloat16

D_MODEL = 1024
SB_WIDTH = 512
GLA_KEYS = 256
GLA_WIDTH = 512
GATE_RANK = 16
IN_WIDTH = 3088
IN_PAD = 3200
D_FF = 2816
CHUNK = 64
LN_EPS = 1e-5
RMS_EPS = 1e-6
ALPHA = 2.0 ** 0.25
GLA_TAU = 16.0
SB_SCALE = 0.125
GLA_SCALE = 0.125
LANES = 128
SB_BLK = 256
SB_CUT = -100.0
GLA_ROWS = 256
VMEM_LIMIT = 56 * 1024 * 1024

ADAM_LR, ADAM_B1, ADAM_B2, ADAM_EPS, ADAM_WD, ADAM_STEP = 0.001, 0.9, 0.999, 1e-08, 0.01, 10

MESH = pl.DeviceIdType.MESH


def _cparams(*sem):
    return pltpu.CompilerParams(dimension_semantics=sem, vmem_limit_bytes=VMEM_LIMIT)


def _dot(a, b):
    return jnp.dot(a, b, preferred_element_type=F32)


def _dot_nt(a, b):
    return lax.dot_general(a, b, (((1,), (1,)), ((), ())), preferred_element_type=F32)


def _dot_tn(a, b):
    return lax.dot_general(a, b, (((0,), (0,)), ((), ())), preferred_element_type=F32)


def _split3(x):
    hi = x.astype(BF16)
    r = x - hi.astype(F32)
    mid = r.astype(BF16)
    lo = (r - mid.astype(F32)).astype(BF16)
    return hi, mid, lo


def _softplus(z):
    return jnp.maximum(z, 0.0) + jnp.log(1.0 + jnp.exp(-jnp.abs(z)))


def _sigmoid(z):
    return 1.0 / (1.0 + jnp.exp(-z))


def _mm(a, b, *, m, n, k, tm, tn, tk, ta=False, tb=False, a_spec=None, b_spec=None, o_spec=None,
        out_shape=None, out_dtype=F32, add=None, add_scale=1.0, rider=None, name):
    nk = k // tk
    dn = (((0 if ta else 1,), (1 if tb else 0,)), ((), ()))
    n_in = len(rider.inputs) if rider else 0
    n_out = len(rider.out_shapes) if rider else 0
    n_add = int(add is not None)
    steps = (m // tm, n // tn, nk)

    def body(*refs):
        a_ref, b_ref = refs[:2]
        add_ref = refs[2] if add is not None else None
        rin = refs[2 + n_add:2 + n_add + n_in]
        o_ref = refs[2 + n_add + n_in]
        rout = refs[3 + n_add + n_in:3 + n_add + n_in + n_out]
        scratch = refs[3 + n_add + n_in + n_out:]
        if rider:
            sems = scratch[len(scratch) - len(rider.sems):]
            ids = [pl.program_id(d) for d in range(3)]
            _ride(rider, rin, rout, sems, (ids[0] == 0) & (ids[1] == 0) & (ids[2] == 0),
                  (ids[0] == steps[0] - 1) & (ids[1] == steps[1] - 1) & (ids[2] == steps[2] - 1))

        part = lax.dot_general(a_ref[...].astype(BF16), b_ref[...].astype(BF16), dn, preferred_element_type=F32)

        def finish(r):
            if add is not None:
                r = r + add_scale * add_ref[...]
            o_ref[...] = r.astype(out_dtype)

        if nk == 1:
            finish(part)
            return
        acc_ref = scratch[0]
        kk = pl.program_id(2)

        @pl.when(kk == 0)
        def _():
            acc_ref[...] = part

        @pl.when((kk > 0) & (kk < nk - 1))
        def _():
            acc_ref[...] += part

        @pl.when(kk == nk - 1)
        def _():
            finish(acc_ref[...] + part)

    if a_spec is None:
        a_spec = pl.BlockSpec((tk, tm), lambda i, j, kk: (kk, i)) if ta else pl.BlockSpec((tm, tk), lambda i, j, kk: (i, kk))
    if b_spec is None:
        b_spec = pl.BlockSpec((tn, tk), lambda i, j, kk: (j, kk)) if tb else pl.BlockSpec((tk, tn), lambda i, j, kk: (kk, j))
    if o_spec is None:
        o_spec = pl.BlockSpec((tm, tn), lambda i, j, kk: (i, j))
    if out_shape is None:
        out_shape = jax.ShapeDtypeStruct((m, n), out_dtype)
    in_specs = [a_spec, b_spec]
    args = [a, b]
    if add is not None:
        in_specs.append(pl.BlockSpec((tm, tn), lambda i, j, kk: (i, j)))
        args.append(add)
    scratch = [pltpu.VMEM((tm, tn), F32)] if nk > 1 else []
    if not rider:
        return pl.pallas_call(
            body, name=name, grid=steps, in_specs=in_specs, out_specs=o_spec, out_shape=out_shape,
            scratch_shapes=scratch, compiler_params=_cparams("parallel", "parallel", "arbitrary"),
        )(*args)
    return pl.pallas_call(
        body, name=name, grid=steps, in_specs=in_specs + [HBM_SPEC] * n_in, out_specs=[o_spec] + [HBM_SPEC] * n_out,
        out_shape=[out_shape] + list(rider.out_shapes),
        input_output_aliases={len(args) + i: 1 + o for i, o in rider.aliases},
        scratch_shapes=scratch + [pltpu.SemaphoreType.DMA((s,)) for s in rider.sems],
        compiler_params=_cparams("arbitrary", "arbitrary", "arbitrary"),
    )(*args, *rider.inputs)


def _col_offsets(pieces):
    offs, o = [], 0
    for a in pieces:
        offs.append(o)
        o += a.shape[1]
    return offs


def _dx(pieces, w_in_p, dr1, t, rider, tm=512):
    offs = _col_offsets(pieces)
    npc = len(pieces)
    n_in, n_out = len(rider.inputs), len(rider.out_shapes)
    steps = t // tm

    def body(*refs):
        p_refs, w_ref, add_ref = refs[:npc], refs[npc], refs[npc + 1]
        rin = refs[npc + 2:npc + 2 + n_in]
        o_ref = refs[npc + 2 + n_in]
        rout = refs[npc + 3 + n_in:npc + 3 + n_in + n_out]
        i = pl.program_id(0)
        _ride(rider, rin, rout, refs[npc + 3 + n_in + n_out:], i == 0, i == steps - 1)
        acc = ALPHA * add_ref[...]
        for p_ref, off in zip(p_refs, offs):
            acc = acc + _dot_nt(p_ref[...].astype(BF16), w_ref[:, off:off + p_ref.shape[1]])
        o_ref[...] = acc

    row = lambda w: pl.BlockSpec((tm, w), lambda i: (i, 0))
    return pl.pallas_call(
        body, name="dx", grid=(steps,),
        in_specs=[row(a.shape[1]) for a in pieces] + [pl.BlockSpec(w_in_p.shape, lambda i: (0, 0)), row(D_MODEL)]
        + [HBM_SPEC] * n_in,
        out_specs=[row(D_MODEL)] + [HBM_SPEC] * n_out,
        out_shape=[jax.ShapeDtypeStruct((t, D_MODEL), F32)] + rider.out_shapes,
        scratch_shapes=[pltpu.SemaphoreType.DMA((s,)) for s in rider.sems],
        compiler_params=_cparams("arbitrary"),
    )(*pieces, w_in_p, dr1, *rider.inputs)


def _dw_out(sb_o, gla_o, dr1, t, tk):
    def body(sb_ref, gl_ref, dr_ref, o_ref):
        kk = pl.program_id(0)
        drb = dr_ref[...].astype(BF16)
        top, bottom = _dot_tn(sb_ref[...], drb), _dot_tn(gl_ref[...], drb)

        @pl.when(kk == 0)
        def _():
            o_ref[0:SB_WIDTH, :] = top
            o_ref[SB_WIDTH:, :] = bottom

        @pl.when(kk > 0)
        def _():
            o_ref[0:SB_WIDTH, :] += top
            o_ref[SB_WIDTH:, :] += bottom

    return pl.pallas_call(
        body, name="dw_out", grid=(t // tk,),
        in_specs=[pl.BlockSpec((tk, SB_WIDTH), lambda kk: (kk, 0)), pl.BlockSpec((tk, GLA_WIDTH), lambda kk: (kk, 0)),
                  pl.BlockSpec((tk, D_MODEL), lambda kk: (kk, 0))],
        out_specs=pl.BlockSpec((D_MODEL, D_MODEL), lambda kk: (0, 0)),
        out_shape=jax.ShapeDtypeStruct((D_MODEL, D_MODEL), F32),
        compiler_params=_cparams("arbitrary"),
    )(sb_o, gla_o, dr1)


def _dw_in(x, pieces, t, tm=512, tk=512):
    offs = _col_offsets(pieces)
    npc = len(pieces)
    n = offs[-1] + pieces[-1].shape[1]
    nk = t // tk

    def body(*refs):
        x_ref, p_refs, o_ref, acc_ref = refs[0], refs[1:1 + npc], refs[1 + npc], refs[2 + npc]
        kk = pl.program_id(1)
        xb = x_ref[...].astype(BF16)
        for p_ref, off in zip(p_refs, offs):
            cols = slice(off, off + p_ref.shape[1])
            part = _dot_tn(xb, p_ref[...].astype(BF16))
            if nk == 1:
                o_ref[:, cols] = part
                continue

            @pl.when(kk == 0)
            def _():
                acc_ref[:, cols] = part

            @pl.when((kk > 0) & (kk < nk - 1))
            def _():
                acc_ref[:, cols] += part

            @pl.when(kk == nk - 1)
            def _():
                o_ref[:, cols] = acc_ref[:, cols] + part

    return pl.pallas_call(
        body, name="dw_in", grid=(D_MODEL // tm, nk),
        in_specs=[pl.BlockSpec((tk, tm), lambda i, kk: (kk, i))]
        + [pl.BlockSpec((tk, a.shape[1]), lambda i, kk: (kk, 0)) for a in pieces],
        out_specs=pl.BlockSpec((tm, n), lambda i, kk: (i, 0)),
        out_shape=jax.ShapeDtypeStruct((D_MODEL, n), F32),
        scratch_shapes=[pltpu.VMEM((tm, n), F32)],
        compiler_params=_cparams("parallel", "arbitrary"),
    )(x, *pieces)


class _Rider:
    def __init__(self, inputs, out_shapes, sems, copies, aliases=()):
        self.inputs, self.out_shapes, self.sems, self.copies = list(inputs), list(out_shapes), tuple(sems), copies
        self.aliases = tuple(aliases)


def _join(a, b):
    na_in, na_out, na_sems = len(a.inputs), len(a.out_shapes), len(a.sems)

    def copies(ins, outs, sems):
        first = a.copies(ins[:na_in], outs[:na_out], sems[:na_sems])
        second = b.copies(ins[na_in:], outs[na_out:], sems[na_sems:])
        return tuple(u + v for u, v in zip(first, second))

    return _Rider(a.inputs + b.inputs, a.out_shapes + b.out_shapes, a.sems + b.sems, copies,
                  a.aliases + tuple((i + na_in, o + na_out) for i, o in b.aliases))


def _ride(rider, in_refs, out_refs, sems, first, last):
    @pl.when(first)
    def _():
        local, sends, _ = rider.copies(in_refs, out_refs, sems)
        for cp in local + sends:
            cp.start()

    @pl.when(last)
    def _():
        local, sends, recvs = rider.copies(in_refs, out_refs, sems)
        for cp in recvs:
            cp.wait_recv()
        for cp in sends:
            cp.wait_send()
        for cp in local:
            cp.wait()


def _run(rider, name):
    n_in, n_out = len(rider.inputs), len(rider.out_shapes)

    def body(*refs):
        local, sends, recvs = rider.copies(refs[:n_in], refs[n_in:n_in + n_out], refs[n_in + n_out:])
        for cp in local + sends:
            cp.start()
        for cp in recvs:
            cp.wait_recv()
        for cp in sends:
            cp.wait_send()
        for cp in local:
            cp.wait()

    return pl.pallas_call(
        body, name=name, in_specs=[HBM_SPEC] * n_in, out_specs=[HBM_SPEC] * n_out, out_shape=list(rider.out_shapes),
        scratch_shapes=[pltpu.SemaphoreType.DMA((s,)) for s in rider.sems],
    )(*rider.inputs)


def _sb_tile(qh, kj, diag, strict, u_strict, r_in):
    z = _dot_nt(qh, kj)
    sp = _softplus(z)
    l1m = -sp
    lsz = z - sp
    if diag:
        l1m = jnp.where(strict, l1m, 0.0)
    cs = _dot(l1m.astype(BF16), u_strict) + r_in
    w = jnp.exp(lsz + cs)
    if diag:
        w = jnp.where(strict, w, 0.0)
    return l1m, lsz, w


def _sb_consts():
    row = lax.broadcasted_iota(jnp.int32, (SB_BLK, SB_BLK), 0)
    col = lax.broadcasted_iota(jnp.int32, (SB_BLK, SB_BLK), 1)
    strict = col < row
    u_strict = (row > col).astype(BF16)
    u_pre = (row < col).astype(BF16)
    lane = lax.broadcasted_iota(jnp.int32, (1, LANES), 1)
    return strict, u_strict, u_pre, lane


def _sb_fwd(proj, t, rider):
    nq = t // SB_BLK
    n_in, n_out = len(rider.inputs), len(rider.out_shapes)

    def body(q_ref, k_ref, v_ref, *rest):
        rin, o_ref, sv_ref, rout = rest[:n_in], rest[n_in], rest[n_in + 1], rest[n_in + 2:n_in + 2 + n_out]
        p = pl.program_id(0)
        i = pl.program_id(1)
        _ride(rider, rin, rout, rest[n_in + 2 + n_out:], (p == 0) & (i == 0), (p == 3) & (i == nq - 1))

        strict, u_strict, _, lane = _sb_consts()
        qf = q_ref[...] * SB_SCALE
        hms = [(lane // 64) == hh for hh in range(2)]
        qhs = [jnp.where(hm, qf, 0.0).astype(BF16) for hm in hms]

        def step(j, r0, r1, a, sv, diag, keep=None):
            rows = pl.ds(pl.multiple_of(j * SB_BLK, SB_BLK), SB_BLK)
            kj = k_ref[rows, :].astype(BF16)
            vf = v_ref[rows, :]
            rs = []
            for hh, r in enumerate((r0, r1)):
                l1m, _, w = _sb_tile(qhs[hh], kj, diag, strict, u_strict, r)
                pv = _dot(w.astype(BF16), jnp.where(hms[hh], vf, 0.0).astype(BF16))
                mass = jnp.sum(l1m, axis=1, keepdims=True)
                if keep is not None:
                    pv, mass = jnp.where(keep, pv, 0.0), jnp.where(keep, mass, 0.0)
                a = a + pv
                rs.append(r + mass)
                sv = jnp.where(lane == hh * 64 + (i - j + 1), rs[hh], sv)
            return rs[0], rs[1], a, sv

        zero = jnp.zeros((SB_BLK, 1), F32)
        acc0 = jnp.zeros((SB_BLK, LANES), F32)
        r0, r1, acc, sv = step(i, zero, zero, acc0, acc0, True)
        r0, r1, acc, sv = step(jnp.maximum(i - 1, 0), r0, r1, acc, sv, False, keep=i > 0)
        _, _, _, acc, sv = lax.while_loop(
            lambda c: (c[0] >= 0) & (jnp.maximum(jnp.max(c[1]), jnp.max(c[2])) > SB_CUT),
            lambda c: (c[0] - 1,) + step(c[0], c[1], c[2], c[3], c[4], False),
            (i - 2, r0, r1, acc, sv))
        o_ref[...] = acc.astype(BF16)
        sv_ref[...] = sv

    return pl.pallas_call(
        body, name="sb_fwd", grid=(4, nq),
        in_specs=[pl.BlockSpec((SB_BLK, LANES), lambda p, i: (i, p)),
                  pl.BlockSpec((t, LANES), lambda p, i: (0, 4 + p)),
                  pl.BlockSpec((t, LANES), lambda p, i: (0, 8 + p))] + [HBM_SPEC] * n_in,
        out_specs=[pl.BlockSpec((SB_BLK, LANES), lambda p, i: (i, p))] * 2 + [HBM_SPEC] * n_out,
        out_shape=[jax.ShapeDtypeStruct((t, SB_WIDTH), BF16), jax.ShapeDtypeStruct((t, SB_WIDTH), F32)] + rider.out_shapes,
        scratch_shapes=[pltpu.SemaphoreType.DMA((s,)) for s in rider.sems],
        compiler_params=_cparams("arbitrary", "arbitrary"),
    )(proj, proj, proj, *rider.inputs)


def _sb_bwd(proj, dcat, mass, t, rider):
    nq = t // SB_BLK
    n_in, n_out = len(rider.inputs), len(rider.out_shapes)

    def body(q_ref, k_ref, v_ref, do_ref, sv_ref, *rest):
        rin = rest[:n_in]
        dq_ref, dk_ref, dv_ref = rest[n_in:n_in + 3]
        rout = rest[n_in + 3:n_in + 3 + n_out]
        p = pl.program_id(0)
        i = pl.program_id(1)
        _ride(rider, rin, rout, rest[n_in + 3 + n_out:], (p == 0) & (i == 0), (p == 3) & (i == nq - 1))

        @pl.when(i == 0)
        def _():
            dk_ref[...] = jnp.zeros_like(dk_ref)
            dv_ref[...] = jnp.zeros_like(dv_ref)

        strict, u_strict, u_pre, lane = _sb_consts()
        qf = q_ref[...] * SB_SCALE
        dof = do_ref[...]
        hms = [(lane // 64) == hh for hh in range(2)]
        qhs = [jnp.where(hm, qf, 0.0).astype(BF16) for hm in hms]
        dohs = [jnp.where(hm, dof, 0.0).astype(BF16) for hm in hms]

        sv = sv_ref[...]
        zero = jnp.zeros((SB_BLK, 1), F32)

        def mass_right(hh, d):
            return jnp.sum(jnp.where(lane == hh * 64 + d, sv, 0.0), axis=1, keepdims=True)

        dstop = lax.while_loop(
            lambda d: (i - d >= 0) & (jnp.maximum(jnp.max(mass_right(0, d)), jnp.max(mass_right(1, d))) > SB_CUT),
            lambda d: d + 1, 1)
        jstop = i - dstop

        def step(j, carry, diag, keep=None):
            pre_g0, pre_g1, dqa = carry
            rows = pl.ds(pl.multiple_of(j * SB_BLK, SB_BLK), SB_BLK)
            kf = k_ref[rows, :]
            kj = kf.astype(BF16)
            vj = v_ref[rows, :].astype(BF16)
            dv = jnp.zeros((SB_BLK, LANES), F32)
            dk = jnp.zeros((SB_BLK, LANES), F32)
            dqj = jnp.zeros((SB_BLK, LANES), F32)
            pre = []
            for hh, pre_g in enumerate((pre_g0, pre_g1)):
                _, lsz, w = _sb_tile(qhs[hh], kj, diag, strict, u_strict, zero if diag else mass_right(hh, i - j))
                g = w * _dot_nt(dohs[hh], vj)
                gpre = _dot(g.astype(BF16), u_pre) + pre_g
                sig = jnp.exp(lsz)
                dz = g * (1.0 - sig) - gpre * sig
                if diag:
                    dz = jnp.where(strict, dz, 0.0)
                dzb = dz.astype(BF16)
                dv = dv + _dot_tn(w.astype(BF16), dohs[hh])
                dk = dk + _dot_tn(dzb, qhs[hh])
                dqj = dqj + _dot(dzb, jnp.where(hms[hh], kf, 0.0).astype(BF16))
                gsum = jnp.sum(g, axis=1, keepdims=True)
                pre.append(pre_g + (gsum if keep is None else jnp.where(keep, gsum, 0.0)))
            if keep is not None:
                dv, dk, dqj = jnp.where(keep, dv, 0.0), jnp.where(keep, dk, 0.0), jnp.where(keep, dqj, 0.0)
            dv_ref[rows, :] += dv
            dk_ref[rows, :] += dk
            return pre[0], pre[1], dqa + dqj

        carry = lax.fori_loop(jstop + 1, i - 1, lambda j, c: step(j, c, False), (zero, zero, jnp.zeros((SB_BLK, LANES), F32)))
        carry = step(jnp.maximum(i - 1, 0), carry, False, keep=i > 0)
        _, _, dq = step(i, carry, True)
        dq_ref[...] = (dq * SB_SCALE).astype(BF16)

    return pl.pallas_call(
        body, name="sb_bwd", grid=(4, nq),
        in_specs=[pl.BlockSpec((SB_BLK, LANES), lambda p, i: (i, p)),
                  pl.BlockSpec((t, LANES), lambda p, i: (0, 4 + p)),
                  pl.BlockSpec((t, LANES), lambda p, i: (0, 8 + p)),
                  pl.BlockSpec((SB_BLK, LANES), lambda p, i: (i, p)),
                  pl.BlockSpec((SB_BLK, LANES), lambda p, i: (i, p))] + [HBM_SPEC] * n_in,
        out_specs=[pl.BlockSpec((SB_BLK, LANES), lambda p, i: (i, p)),
                   pl.BlockSpec((t, LANES), lambda p, i: (0, p)),
                   pl.BlockSpec((t, LANES), lambda p, i: (0, p))] + [HBM_SPEC] * n_out,
        out_shape=[jax.ShapeDtypeStruct((t, SB_WIDTH), BF16)] + [jax.ShapeDtypeStruct((t, SB_WIDTH), F32)] * 2
        + rider.out_shapes,
        scratch_shapes=[pltpu.SemaphoreType.DMA((s,)) for s in rider.sems],
        compiler_params=_cparams("arbitrary", "arbitrary"),
    )(proj, proj, proj, dcat, mass, *rider.inputs)


def _gla_consts():
    r = lax.broadcasted_iota(jnp.int32, (GLA_ROWS, GLA_ROWS), 0)
    c = lax.broadcasted_iota(jnp.int32, (GLA_ROWS, GLA_ROWS), 1)
    same = (r // CHUNK) == (c // CHUNK)
    causal = same & (c <= r)
    upto_mid = c % CHUNK <= CHUNK // 2 - 1
    fwd_stack = jnp.concatenate([causal, same & upto_mid, same], axis=0).astype(BF16)
    bwd_stack = jnp.concatenate([same & (c >= r), same & (r % CHUNK <= CHUNK // 2 - 1), same], axis=1).astype(BF16)
    rowid = lax.broadcasted_iota(jnp.int32, (GLA_ROWS, 1), 0)
    lane = lax.broadcasted_iota(jnp.int32, (1, LANES), 1)
    sr = lax.broadcasted_iota(jnp.int32, (LANES, 2 * LANES), 0)
    sc = lax.broadcasted_iota(jnp.int32, (LANES, 2 * LANES), 1)
    blockdiag = (sr // 64) == (sc // LANES)
    return causal, fwd_stack, bwd_stack, rowid, lane, blockdiag


def _dot3(u, x):
    hi, mid, lo = _split3(x)
    return _dot(u, hi) + _dot(u, mid) + _dot(u, lo)


def _row_to_col(row):
    return jnp.transpose(jnp.broadcast_to(row, (LANES, LANES)))


def _gla_gates(ga_ref, gup_ref, gbias_ref):
    pre = _dot(ga_ref[...].astype(BF16), gup_ref[...].astype(BF16)) + gbias_ref[...]
    log_a = (jnp.minimum(pre, 0.0) - jnp.log(1.0 + jnp.exp(-jnp.abs(pre)))) / GLA_TAU
    return pre, log_a


def _gla_terms(g2, q2, k2, fwd_stack):
    bs = _dot3(fwd_stack, g2)
    b, b_ref, b_last = bs[:GLA_ROWS], bs[GLA_ROWS:2 * GLA_ROWS], bs[2 * GLA_ROWS:]
    qs = q2 * GLA_SCALE
    e_q = jnp.exp(b - b_ref)
    e_k = jnp.exp(b_ref - b)
    e_d = jnp.exp(b_last - b)
    e_b = jnp.exp(b)
    decay = jnp.exp(b_last)
    return dict(qs=qs, e_q=e_q, e_k=e_k, e_d=e_d, e_b=e_b, decay=decay,
                qi=qs * e_q, ki=k2 * e_k, kd=k2 * e_d, qb=qs * e_b)


def _gla_fwd(proj, gate_up_pad, gate_bias, gnorm, t, rider):
    nsteps = t // GLA_ROWS
    cps = GLA_ROWS // CHUNK
    n_in, n_out = len(rider.inputs), len(rider.out_shapes)

    def body(q_ref, k_ref, v_ref, gg_ref, ga_ref, gup_ref, gbias_ref, gn_ref, *rest):
        o_ref, oraw_ref, st_ref = rest[n_in:n_in + 3]
        s_scr = rest[n_in + 3 + n_out]
        i = pl.program_id(0)
        _ride(rider, rest[:n_in], rest[n_in + 3:n_in + 3 + n_out], rest[n_in + 4 + n_out:], i == 0, i == nsteps - 1)

        @pl.when(i == 0)
        def _():
            s_scr[...] = jnp.zeros_like(s_scr)

        _, fwd_stack, _, _, lane, blockdiag = _gla_consts()
        causal64 = (lax.broadcasted_iota(jnp.int32, (CHUNK, CHUNK), 1) <= lax.broadcasted_iota(jnp.int32, (CHUNK, CHUNK), 0))
        _, log_a = _gla_gates(ga_ref, gup_ref, gbias_ref)
        gn = gn_ref[...]
        pre = []
        for p in range(2):
            kl = slice(p * LANES, (p + 1) * LANES)
            tm = _gla_terms(log_a[:, kl], q_ref[:, kl], k_ref[:, kl], fwd_stack)
            pre.append(dict(
                v2b=v_ref[:, p * 2 * LANES:(p + 1) * 2 * LANES].astype(BF16), kib=tm["ki"].astype(BF16),
                kdb=tm["kd"].astype(BF16), qbb=tm["qb"].astype(BF16), decay=tm["decay"],
                qihb=[jnp.where((lane // 64) == hh, tm["qi"], 0.0).astype(BF16) for hh in range(2)]))
        state = [s_scr[0], s_scr[1]]
        outs = [[], []]
        for cc in range(cps):
            rows = slice(cc * CHUNK, (cc + 1) * CHUNK)
            for p in range(2):
                w = pre[p]
                st_ref[cc, p] = state[p]
                intra = []
                for hh in range(2):
                    a = jnp.where(causal64, _dot_nt(w["qihb"][hh][rows], w["kib"][rows]), 0.0)
                    intra.append(_dot(a.astype(BF16), w["v2b"][rows, hh * LANES:(hh + 1) * LANES]))
                outs[p].append(jnp.concatenate(intra, axis=1) + _dot(w["qbb"][rows], state[p].astype(BF16)))
                upd = jnp.where(blockdiag, _dot_tn(w["kdb"][rows], w["v2b"][rows]), 0.0)
                dcol = _row_to_col(w["decay"][cc * CHUNK:cc * CHUNK + 1])
                state[p] = state[p] * jnp.concatenate([dcol, dcol], axis=1) + upd
        for p in range(2):
            s_scr[p] = state[p]
            vl = slice(p * 2 * LANES, (p + 1) * 2 * LANES)
            o2 = jnp.concatenate(outs[p], axis=0)
            oraw_ref[:, vl] = o2
            for hh in range(2):
                oh = o2[:, hh * LANES:(hh + 1) * LANES]
                gl = slice(p * 2 * LANES + hh * LANES, p * 2 * LANES + (hh + 1) * LANES)
                rinv = lax.rsqrt(jnp.mean(oh * oh, axis=1, keepdims=True) + RMS_EPS)
                gg = gg_ref[:, gl]
                o_ref[:, gl] = (oh * rinv * gn * (gg * _sigmoid(gg))).astype(BF16)

    cb = lambda w, idx: pl.BlockSpec((GLA_ROWS, w), lambda i: (i, idx))
    full = lambda shp: pl.BlockSpec(shp, lambda i: tuple(0 for _ in shp))
    return pl.pallas_call(
        body, name="gla_fwd", grid=(nsteps,),
        in_specs=[cb(256, 6), cb(256, 7), cb(512, 4), cb(512, 5), cb(128, 24),
                  full((LANES, GLA_KEYS)), full((1, GLA_KEYS)), full((1, LANES))] + [HBM_SPEC] * n_in,
        out_specs=[pl.BlockSpec((GLA_ROWS, GLA_WIDTH), lambda i: (i, 0)),
                   pl.BlockSpec((GLA_ROWS, GLA_WIDTH), lambda i: (i, 0)),
                   pl.BlockSpec((cps, 2, LANES, 2 * LANES), lambda i: (i, 0, 0, 0))] + [HBM_SPEC] * n_out,
        out_shape=[jax.ShapeDtypeStruct((t, GLA_WIDTH), BF16), jax.ShapeDtypeStruct((t, GLA_WIDTH), F32),
                   jax.ShapeDtypeStruct((t // CHUNK, 2, LANES, 2 * LANES), F32)]
        + rider.out_shapes,
        input_output_aliases={8 + i: 3 + o for i, o in rider.aliases},
        scratch_shapes=[pltpu.VMEM((2, LANES, 2 * LANES), F32)] + [pltpu.SemaphoreType.DMA((s,)) for s in rider.sems],
        compiler_params=_cparams("arbitrary"),
    )(proj, proj, proj, proj, proj, gate_up_pad, gate_bias, gnorm, *rider.inputs)


def _gla_bwd(proj, dcat, oraw, states, gate_up_pad, gate_bias, gnorm, t):
    nsteps = t // GLA_ROWS
    cps = GLA_ROWS // CHUNK
    wout = 2 * GLA_KEYS + 2 * GLA_WIDTH + LANES

    def body(q_ref, k_ref, v_ref, gg_ref, ga_ref, do_ref, oraw_ref, st_ref, gup_ref, gbias_ref, gn_ref,
             d_ref, dgup_ref, dgbias_ref, dgn_ref, ds_scr):
        i = pl.program_id(0)

        @pl.when(i == 0)
        def _():
            ds_scr[...] = jnp.zeros_like(ds_scr)
            dgup_ref[...] = jnp.zeros_like(dgup_ref)
            dgbias_ref[...] = jnp.zeros_like(dgbias_ref)
            dgn_ref[...] = jnp.zeros_like(dgn_ref)

        causal, fwd_stack, bwd_stack, rowid, lane, blockdiag = _gla_consts()
        pre, log_a = _gla_gates(ga_ref, gup_ref, gbias_ref)
        gn = gn_ref[...]
        dgn = jnp.zeros((1, LANES), F32)
        dgs = []
        pre_p = []
        for p in range(2):
            kl = slice(p * LANES, (p + 1) * LANES)
            vl = slice(p * 2 * LANES, (p + 1) * 2 * LANES)
            tm = _gla_terms(log_a[:, kl], q_ref[:, kl], k_ref[:, kl], fwd_stack)
            v2b = v_ref[:, vl].astype(BF16)
            dos = []
            for hh in range(2):
                gl = slice(p * 2 * LANES + hh * LANES, p * 2 * LANES + (hh + 1) * LANES)
                oh = oraw_ref[:, gl]
                rinv = lax.rsqrt(jnp.mean(oh * oh, axis=1, keepdims=True) + RMS_EPS)
                on = oh * rinv
                gg = gg_ref[:, gl]
                sg = _sigmoid(gg)
                sil = gg * sg
                dgo = do_ref[:, gl]
                d_ref[:, 2 * GLA_KEYS + GLA_WIDTH + gl.start:2 * GLA_KEYS + GLA_WIDTH + gl.stop] = (
                    dgo * on * gn * (sg * (1.0 + gg * (1.0 - sg)))).astype(BF16)
                dgn = dgn + jnp.sum(dgo * sil * on, axis=0, keepdims=True)
                don = dgo * sil * gn
                dos.append(rinv * (don - on * jnp.mean(don * on, axis=1, keepdims=True)))
            do2b = jnp.concatenate(dos, axis=1).astype(BF16)
            qib = tm["qi"].astype(BF16)
            kib = tm["ki"].astype(BF16)
            kdb = tm["kd"].astype(BF16)
            qbb = tm["qb"].astype(BF16)
            dqi = jnp.zeros((GLA_ROWS, LANES), F32)
            dki = jnp.zeros((GLA_ROWS, LANES), F32)
            dvs = []
            for hh in range(2):
                hm = (lane // 64) == hh
                hl = slice(hh * LANES, (hh + 1) * LANES)
                a = jnp.where(causal, _dot_nt(jnp.where(hm, tm["qi"], 0.0).astype(BF16), kib), 0.0).astype(BF16)
                da = jnp.where(causal, _dot_nt(do2b[:, hl], v2b[:, hl]), 0.0).astype(BF16)
                dvs.append(_dot_tn(a, do2b[:, hl]))
                dqi = dqi + jnp.where(hm, _dot(da, kib), 0.0)
                dki = dki + jnp.where(hm, _dot_tn(da, qib), 0.0)
            pre_p.append(dict(tm=tm, v2b=v2b, do2b=do2b, kdb=kdb, qbb=qbb, dqi=dqi, dki=dki, dvs=dvs))
        dstate = [ds_scr[0], ds_scr[1]]
        rec = [dict(dv_st=[None] * cps, dqb=[None] * cps, dkd=[None] * cps, dd=[None] * cps) for _ in range(2)]
        for cc in reversed(range(cps)):
            rows = slice(cc * CHUNK, (cc + 1) * CHUNK)
            for p in range(2):
                w, ds2 = pre_p[p], dstate[p]
                s_prev = st_ref[cc, p]
                ds2b = ds2.astype(BF16)
                rec[p]["dv_st"][cc] = _dot(w["kdb"][rows], ds2b)
                rec[p]["dqb"][cc] = _dot_nt(w["do2b"][rows], s_prev.astype(BF16))
                rec[p]["dkd"][cc] = _dot_nt(w["v2b"][rows], ds2b)
                decay_row = w["tm"]["decay"][cc * CHUNK:cc * CHUNK + 1]
                ddecay_col = jnp.sum(ds2 * s_prev, axis=1, keepdims=True)
                ddecay_row = jnp.transpose(jnp.broadcast_to(ddecay_col, (LANES, LANES)))[0:1, :]
                rec[p]["dd"][cc] = jnp.broadcast_to(ddecay_row * decay_row, (CHUNK, LANES))
                dcol = _row_to_col(decay_row)
                dstate[p] = (jnp.where(blockdiag, _dot_tn(w["qbb"][rows], w["do2b"][rows]), 0.0)
                             + ds2 * jnp.concatenate([dcol, dcol], axis=1))
        for p in range(2):
            ds_scr[p] = dstate[p]
            tm, dqi, dki = pre_p[p]["tm"], pre_p[p]["dqi"], pre_p[p]["dki"]
            dv2 = jnp.concatenate(pre_p[p]["dvs"], axis=1) + jnp.concatenate(rec[p]["dv_st"], axis=0)
            dqb = jnp.concatenate(rec[p]["dqb"], axis=0)
            dkd = jnp.concatenate(rec[p]["dkd"], axis=0)
            dd = rec[p]["dd"]
            dqs = dqi * tm["e_q"] + dqb * tm["e_b"]
            dk = dki * tm["e_k"] + dkd * tm["e_d"]
            t_qi = dqi * tm["qi"]
            t_ki = dki * tm["ki"]
            t_kd = dkd * tm["kd"]
            db = t_qi - t_ki + dqb * tm["qb"] - t_kd
            to_mid = t_ki - t_qi
            to_last = t_kd + jnp.where(rowid % CHUNK == CHUNK - 1, jnp.concatenate(dd, axis=0), 0.0)
            dgs.append(_dot3(bwd_stack, jnp.concatenate([db, to_mid, to_last], axis=0)))
            d_ref[:, p * LANES:(p + 1) * LANES] = (dqs * GLA_SCALE).astype(BF16)
            d_ref[:, GLA_KEYS + p * LANES:GLA_KEYS + (p + 1) * LANES] = dk.astype(BF16)
            d_ref[:, 2 * GLA_KEYS + p * 2 * LANES:2 * GLA_KEYS + (p + 1) * 2 * LANES] = dv2.astype(BF16)
        dlog_a = jnp.concatenate(dgs, axis=1)
        dpre = dlog_a * (1.0 / GLA_TAU) * _sigmoid(-pre)
        dpb = dpre.astype(BF16)
        dgn_ref[...] += dgn
        dgbias_ref[...] += jnp.sum(dpre, axis=0, keepdims=True)
        dgup_ref[...] += _dot_tn(ga_ref[...].astype(BF16), dpb)
        d_ref[:, 2 * GLA_KEYS + 2 * GLA_WIDTH:] = _dot_nt(dpb, gup_ref[...].astype(BF16)).astype(BF16)

    rev = lambda i: nsteps - 1 - i
    cb = lambda w, idx: pl.BlockSpec((GLA_ROWS, w), lambda i: (rev(i), idx))
    full = lambda shp: pl.BlockSpec(shp, lambda i: tuple(0 for _ in shp))
    return pl.pallas_call(
        body, name="gla_bwd", grid=(nsteps,),
        in_specs=[cb(256, 6), cb(256, 7), cb(512, 4), cb(512, 5), cb(128, 24), cb(512, 1), cb(512, 0),
                  pl.BlockSpec((cps, 2, LANES, 2 * LANES), lambda i: (rev(i), 0, 0, 0)),
                  full((LANES, GLA_KEYS)), full((1, GLA_KEYS)), full((1, LANES))],
        out_specs=[pl.BlockSpec((GLA_ROWS, wout), lambda i: (rev(i), 0)),
                   full((LANES, GLA_KEYS)), full((1, GLA_KEYS)), full((1, LANES))],
        out_shape=[jax.ShapeDtypeStruct((t, wout), BF16), jax.ShapeDtypeStruct((LANES, GLA_KEYS), F32),
                   jax.ShapeDtypeStruct((1, GLA_KEYS), F32), jax.ShapeDtypeStruct((1, LANES), F32)],
        scratch_shapes=[pltpu.VMEM((2, LANES, 2 * LANES), F32)],
        compiler_params=_cparams("arbitrary"),
    )(proj, proj, proj, proj, proj, dcat, oraw, states, gate_up_pad, gate_bias, gnorm)


def _ln_stats(r):
    mu = jnp.mean(r, axis=1, keepdims=True)
    xc = r - mu
    rstd = lax.rsqrt(jnp.mean(xc * xc, axis=1, keepdims=True) + LN_EPS)
    return xc * rstd, rstd


def _ln_bwd(dy_g, xhat, rstd):
    return rstd * (dy_g - jnp.mean(dy_g, axis=1, keepdims=True) - xhat * jnp.mean(dy_g * xhat, axis=1, keepdims=True))


def _outproj_ln1(sb_o, gla_o, x, w_out, g1, b1, t, tm=256):
    def body(sb_ref, gl_ref, x_ref, w_ref, g_ref, b_ref, xhat_ref, rstd_ref, h_ref):
        mix = _dot(sb_ref[...], w_ref[0:SB_WIDTH, :]) + _dot(gl_ref[...], w_ref[SB_WIDTH:, :])
        xhat, rstd = _ln_stats(ALPHA * x_ref[...] + mix)
        xhat_ref[...] = xhat
        rstd_ref[...] = rstd
        h_ref[...] = (xhat * g_ref[...] + b_ref[...]).astype(BF16)

    row = lambda w: pl.BlockSpec((tm, w), lambda i: (i, 0))
    full = lambda shp: pl.BlockSpec(shp, lambda i: (0, 0))
    return pl.pallas_call(
        body, name="outproj_ln1", grid=(t // tm,),
        in_specs=[row(SB_WIDTH), row(GLA_WIDTH), row(D_MODEL), full((D_MODEL, D_MODEL)), full((1, D_MODEL)), full((1, D_MODEL))],
        out_specs=[row(D_MODEL), row(1), row(D_MODEL)],
        out_shape=[jax.ShapeDtypeStruct((t, D_MODEL), F32), jax.ShapeDtypeStruct((t, 1), F32),
                   jax.ShapeDtypeStruct((t, D_MODEL), BF16)],
        compiler_params=_cparams("parallel"),
    )(sb_o, gla_o, x, w_out, g1, b1)


_INV_SQRT2 = 1.0 / math.sqrt(2.0)
_INV_SQRT2PI = 1.0 / math.sqrt(2.0 * math.pi)


def _conv3(xs, w_ref, b_ref, half):
    return (w_ref[half, 0:1, :] * pltpu.roll(xs, 2, 0) + w_ref[half, 1:2, :] * pltpu.roll(xs, 1, 0)
            + w_ref[half, 2:3, :] * xs + b_ref[half])


HALO = 16


def _conv_gelu_fwd(up3, conv_w3, conv_b3, t, tr=512, ct=256):
    nct = D_FF // ct
    hb = tr // HALO

    def body(cur_ref, prev_ref, w_ref, b_ref, gm_ref):
        i = pl.program_id(1)
        keep = (i > 0).astype(F32)
        us = []
        for half in range(2):
            xs = jnp.concatenate([prev_ref[half].astype(F32) * keep, cur_ref[half].astype(F32)], axis=0)
            us.append(_conv3(xs, w_ref, b_ref, half)[HALO:, :])
        a, c = us
        gelu = 0.5 * a * (1.0 + lax.erf(a * _INV_SQRT2))
        gm_ref[...] = (gelu * c).astype(BF16)

    return pl.pallas_call(
        body, name="conv_gelu_fwd", grid=(nct, t // tr),
        in_specs=[pl.BlockSpec((2, tr, ct), lambda j, i: (0, i, j)),
                  pl.BlockSpec((2, HALO, ct), lambda j, i: (0, jnp.maximum(i * hb - 1, 0), j)),
                  pl.BlockSpec((2, 3, ct), lambda j, i: (0, 0, j)),
                  pl.BlockSpec((2, 1, ct), lambda j, i: (0, 0, j))],
        out_specs=pl.BlockSpec((tr, ct), lambda j, i: (i, j)),
        out_shape=jax.ShapeDtypeStruct((t, D_FF), BF16),
        compiler_params=_cparams("parallel", "parallel"),
    )(up3, up3, conv_w3, conv_b3)


def _conv_gelu_bwd(up3, dgm, conv_w3, conv_b3, t, tr=512, ct=256):
    nct = D_FF // ct
    nrt = t // tr
    hb = tr // HALO
    n = tr + 2 * HALO
    lo, hi = HALO, tr + HALO

    def body(cur_ref, prev_ref, next_ref, dg_ref, dgn_ref, w_ref, b_ref, dup_ref, dcw_ref, dcb_ref):
        i = pl.program_id(1)

        @pl.when(i == 0)
        def _():
            dcw_ref[...] = jnp.zeros_like(dcw_ref)
            dcb_ref[...] = jnp.zeros_like(dcb_ref)

        keep_prev = (i > 0).astype(F32)
        keep_next = (i < nrt - 1).astype(F32)
        xs, xm1, xm2, us = [], [], [], []
        for half in range(2):
            x = jnp.concatenate([prev_ref[half].astype(F32) * keep_prev, cur_ref[half].astype(F32),
                                 next_ref[half].astype(F32)], axis=0)
            xs.append(x)
            xm1.append(pltpu.roll(x, 1, 0))
            xm2.append(pltpu.roll(x, 2, 0))
            us.append(w_ref[half, 0:1, :] * xm2[half] + w_ref[half, 1:2, :] * xm1[half]
                      + w_ref[half, 2:3, :] * x + b_ref[half])
        a, c = us
        dg = jnp.concatenate([jnp.zeros((HALO, ct), F32), dg_ref[...].astype(F32),
                              dgn_ref[...].astype(F32) * keep_next], axis=0)
        cdf = 0.5 * (1.0 + lax.erf(a * _INV_SQRT2))
        pdf = jnp.exp(-0.5 * a * a) * _INV_SQRT2PI
        dus = [dg * c * (cdf + a * pdf), dg * (a * cdf)]
        rid = lax.broadcasted_iota(jnp.int32, (8, 1), 0)
        for half in range(2):
            du = dus[half]
            dup = (w_ref[half, 2:3, :] * du + w_ref[half, 1:2, :] * pltpu.roll(du, n - 1, 0)
                   + w_ref[half, 0:1, :] * pltpu.roll(du, n - 2, 0))
            dup_ref[half] = dup[lo:hi, :].astype(BF16)
            duc = du[lo:hi, :]
            s0 = jnp.sum(duc * xm2[half][lo:hi, :], axis=0, keepdims=True)
            s1 = jnp.sum(duc * xm1[half][lo:hi, :], axis=0, keepdims=True)
            s2 = jnp.sum(duc * xs[half][lo:hi, :], axis=0, keepdims=True)
            dcw_ref[half] += jnp.where(rid == 0, s0, jnp.where(rid == 1, s1, jnp.where(rid == 2, s2, 0.0)))
            dcb_ref[half] += jnp.sum(duc, axis=0, keepdims=True)

    last = t // HALO - 1
    return pl.pallas_call(
        body, name="conv_gelu_bwd", grid=(nct, nrt),
        in_specs=[pl.BlockSpec((2, tr, ct), lambda j, i: (0, i, j)),
                  pl.BlockSpec((2, HALO, ct), lambda j, i: (0, jnp.maximum(i * hb - 1, 0), j)),
                  pl.BlockSpec((2, HALO, ct), lambda j, i: (0, jnp.minimum((i + 1) * hb, last), j)),
                  pl.BlockSpec((tr, ct), lambda j, i: (i, j)),
                  pl.BlockSpec((HALO, ct), lambda j, i: (jnp.minimum((i + 1) * hb, last), j)),
                  pl.BlockSpec((2, 3, ct), lambda j, i: (0, 0, j)),
                  pl.BlockSpec((2, 1, ct), lambda j, i: (0, 0, j))],
        out_specs=[pl.BlockSpec((2, tr, ct), lambda j, i: (0, i, j)),
                   pl.BlockSpec((2, 8, ct), lambda j, i: (0, 0, j)),
                   pl.BlockSpec((2, 1, ct), lambda j, i: (0, 0, j))],
        out_shape=[jax.ShapeDtypeStruct((2, t, D_FF), BF16), jax.ShapeDtypeStruct((2, 8, D_FF), F32),
                   jax.ShapeDtypeStruct((2, 1, D_FF), F32)],
        compiler_params=_cparams("parallel", "arbitrary"),
    )(up3, up3, up3, dgm, dgm, conv_w3, conv_b3)


def _down_ln2_loss(gm, w_down, xhat1, g1, b1, g2, b2, target, t, tm=256):
    def body(gm_ref, w_ref, xh_ref, g1_ref, b1_ref, g2_ref, b2_ref, tg_ref, dr_ref, drb_ref, loss_ref, dg_ref, db_ref):
        i = pl.program_id(0)

        @pl.when(i == 0)
        def _():
            loss_ref[...] = jnp.zeros_like(loss_ref)
            dg_ref[...] = jnp.zeros_like(dg_ref)
            db_ref[...] = jnp.zeros_like(db_ref)

        h = xh_ref[...] * g1_ref[...] + b1_ref[...]
        xhat, rstd = _ln_stats(ALPHA * h + _dot(gm_ref[...], w_ref[...]))
        err = xhat * g2_ref[...] + b2_ref[...] - tg_ref[...]
        loss_ref[...] += 0.5 * jnp.sum(jnp.sum(err * err, axis=1, keepdims=True), axis=0, keepdims=True) / D_MODEL
        dy = err * (1.0 / D_MODEL)
        dg_ref[...] += jnp.sum(dy * xhat, axis=0, keepdims=True)
        db_ref[...] += jnp.sum(dy, axis=0, keepdims=True)
        dr = _ln_bwd(dy * g2_ref[...], xhat, rstd)
        dr_ref[...] = dr
        drb_ref[...] = dr.astype(BF16)

    row = lambda w: pl.BlockSpec((tm, w), lambda i: (i, 0))
    full = lambda shp: pl.BlockSpec(shp, lambda i: (0, 0))
    vec = full((1, D_MODEL))
    return pl.pallas_call(
        body, name="down_ln2_loss", grid=(t // tm,),
        in_specs=[row(D_FF), full((D_FF, D_MODEL)), row(D_MODEL), vec, vec, vec, vec, row(D_MODEL)],
        out_specs=[row(D_MODEL), row(D_MODEL), full((1, 1)), vec, vec],
        out_shape=[jax.ShapeDtypeStruct((t, D_MODEL), F32), jax.ShapeDtypeStruct((t, D_MODEL), BF16),
                   jax.ShapeDtypeStruct((1, 1), F32),
                   jax.ShapeDtypeStruct((1, D_MODEL), F32), jax.ShapeDtypeStruct((1, D_MODEL), F32)],
        compiler_params=_cparams("arbitrary"),
    )(gm, w_down, xhat1, g1, b1, g2, b2, target)


def _dh_ln1_bwd(dup3, w_up4, dr2, xhat1, rstd1, g1, t, tm=256):
    ws = 2 * D_FF // 4

    def body(a_ref, w_ref, dr2_ref, xh_ref, rs_ref, g_ref, dr1_ref, dg_ref, db_ref):
        i = pl.program_id(0)

        @pl.when(i == 0)
        def _():
            dg_ref[...] = jnp.zeros_like(dg_ref)
            db_ref[...] = jnp.zeros_like(db_ref)

        dh = ALPHA * dr2_ref[...]
        for s in range(4):
            dh = dh + _dot_nt(a_ref[s // 2, :, (s % 2) * ws:(s % 2 + 1) * ws], w_ref[s])
        xhat = xh_ref[...]
        dg_ref[...] += jnp.sum(dh * xhat, axis=0, keepdims=True)
        db_ref[...] += jnp.sum(dh, axis=0, keepdims=True)
        dr1_ref[...] = _ln_bwd(dh * g_ref[...], xhat, rs_ref[...])

    row = lambda w: pl.BlockSpec((tm, w), lambda i: (i, 0))
    vec = pl.BlockSpec((1, D_MODEL), lambda i: (0, 0))
    return pl.pallas_call(
        body, name="dh_ln1_bwd", grid=(t // tm,),
        in_specs=[pl.BlockSpec((2, tm, D_FF), lambda i: (0, i, 0)),
                  pl.BlockSpec((4, D_MODEL, ws), lambda i: (0, 0, 0)),
                  row(D_MODEL), row(D_MODEL), row(1), vec],
        out_specs=[row(D_MODEL), vec, vec],
        out_shape=[jax.ShapeDtypeStruct((t, D_MODEL), F32), jax.ShapeDtypeStruct((1, D_MODEL), F32),
                   jax.ShapeDtypeStruct((1, D_MODEL), F32)],
        compiler_params=_cparams("arbitrary"),
    )(dup3, w_up4, dr2, xhat1, rstd1, g1)


def _adamw(w, g, m, v, name):
    rows, cols = w.shape
    tr = rows
    for cand in (256, 128, 64, 32, 16, 8):
        if rows % cand == 0 and rows > cand:
            tr = cand
            break
    c1 = 1.0 / (1.0 - ADAM_B1 ** ADAM_STEP)
    c2 = 1.0 / (1.0 - ADAM_B2 ** ADAM_STEP)

    def body(w_ref, g_ref, m_ref, v_ref, d_ref, nm_ref, nv_ref):
        gv = g_ref[...]
        nm = ADAM_B1 * m_ref[...] + (1.0 - ADAM_B1) * gv
        nv = ADAM_B2 * v_ref[...] + (1.0 - ADAM_B2) * (gv * gv)
        d_ref[...] = -ADAM_LR * ((nm * c1) / (jnp.sqrt(nv * c2) + ADAM_EPS) + ADAM_WD * w_ref[...])
        nm_ref[...] = nm
        nv_ref[...] = nv

    spec = pl.BlockSpec((tr, cols), lambda i: (i, 0))
    out = jax.ShapeDtypeStruct((rows, cols), F32)
    return pl.pallas_call(
        body, name=name, grid=(rows // tr,), in_specs=[spec] * 4, out_specs=[spec] * 3, out_shape=[out] * 3,
        compiler_params=_cparams("parallel"),
    )(w, g, m, v)


def _local_step(x, target, w_in_p, late_shards, gate_up_pad, gate_bias, gnorm, ln1_g, ln1_b, conv_w3, conv_b3,
                ln2_g, ln2_b, c_arr, kc_arr):
    t = x.shape[0]
    tq = min(t, 1024)
    s_up, s_out, s_down = late_shards
    sh_up, sh_out, sh_down = LATE_SHAPES
    proj, out_partly = _mm(x, w_in_p, m=t, n=IN_PAD, k=D_MODEL, tm=512, tn=IN_PAD, tk=D_MODEL, name="proj",
                           rider=_gather_rider([s_out], [sh_out]))
    sb_o, sb_mass, up_partly = _sb_fwd(proj, t, _gather_rider([s_up], [sh_up]))
    gla_o, oraw, states, w_up4, w_out4, down_partly = _gla_fwd(
        proj, gate_up_pad, gate_bias, gnorm, t,
        _join(_forward_rider([up_partly, out_partly], [sh_up, sh_out]), _gather_rider([s_down], [sh_down])))
    w_out = w_out4.reshape(D_MODEL, D_MODEL)
    xhat1, rstd1, h_bf = _outproj_ln1(sb_o, gla_o, x, w_out, ln1_g, ln1_b, t)
    up3, w_down4 = _mm(h_bf, w_up4, m=t, n=2 * D_FF, k=D_MODEL, tm=tq, tn=W_UP_S, tk=D_MODEL, name="up",
                       b_spec=pl.BlockSpec((None, D_MODEL, W_UP_S), lambda i, j, kk: (j, 0, 0)),
                       o_spec=pl.BlockSpec((None, tq, W_UP_S), lambda i, j, kk: (j // 2, i, j % 2)),
                       out_shape=jax.ShapeDtypeStruct((2, t, D_FF), BF16), out_dtype=BF16,
                       rider=_forward_rider([down_partly], [sh_down]))
    w_down = w_down4.reshape(D_FF, D_MODEL)
    gm = _conv_gelu_fwd(up3, conv_w3, conv_b3, t, tr=tq)
    dr2, dr2b, loss, dln2_g, dln2_b = _down_ln2_loss(gm, w_down, xhat1, ln1_g, ln1_b, ln2_g, ln2_b, target, t)
    dgm = _mm(dr2b, w_down, m=t, n=D_FF, k=D_MODEL, tm=tq, tn=W_UP_S, tk=D_MODEL, tb=True, out_dtype=BF16, name="dgm")
    dw_down = _mm(gm, dr2b, m=D_FF, n=D_MODEL, k=t, tm=W_UP_S, tn=D_MODEL, tk=tq, ta=True, name="dw_down")
    dup3, dcw, dcb = _conv_gelu_bwd(up3, dgm, conv_w3, conv_b3, t, tr=tq)
    dr1, dln1_g, dln1_b = _dh_ln1_bwd(dup3, w_up4, dr2, xhat1, rstd1, ln1_g, t)
    dw_up4 = _mm(h_bf, dup3, m=D_MODEL, n=2 * D_FF, k=t, tm=512, tn=W_UP_S, tk=t, ta=True, name="dw_up",
                 b_spec=pl.BlockSpec((None, t, W_UP_S), lambda i, j, kk: (j // 2, kk, j % 2)),
                 o_spec=pl.BlockSpec((None, 512, W_UP_S), lambda i, j, kk: (j, i, 0)),
                 out_shape=jax.ShapeDtypeStruct((4, D_MODEL, W_UP_S), F32))
    dw_out = _dw_out(sb_o, gla_o, dr1, t, tq)
    gs = [dw_up4, dw_out.reshape(4, W_OUT_S, D_MODEL), dw_down.reshape(4, W_DOWN_S, D_MODEL)]
    dcat, *from_sib = _mm(dr1, w_out, m=t, n=D_MODEL, k=D_MODEL, tm=tq, tn=512, tk=D_MODEL, tb=True, name="dcat",
                          rider=_sibling_rider(gs, LATE_SHAPES))
    ps = [_add_sibling(gs[m], from_sib[m], c_arr, LATE_ADD_ROWS[m], "add_sibling_late_%d" % m) for m in range(3)]
    dgla, dgup_pad, dgbias, dgnorm = _gla_bwd(proj, dcat, oraw, states, gate_up_pad, gate_bias, gnorm, t)
    small = dict(
        gate_up=dgup_pad[:GATE_RANK], gate_bias=dgbias, gla_norm_g=dgnorm, ln1_g=dln1_g, ln1_b=dln1_b,
        conv_w=jnp.concatenate([dcw[0, :3], dcw[1, :3]], axis=1), conv_b=jnp.concatenate([dcb[0], dcb[1]], axis=1),
        ln2_g=dln2_g, ln2_b=dln2_b, loss=loss)
    dsq, dsk, dsv, *others, vecs = _sb_bwd(proj, dcat, sb_mass, t,
                                           _chips_rider(ps, _pack_vec(small, SMALL_GRADS, GRAD_VEC_ROWS)))
    late_sums = [_add_chips(ps[m], others[m], kc_arr, LATE_ADD_ROWS[m], "add_chips_late_%d" % m) for m in range(3)]
    dproj = [dsq, dsk, dsv, dgla]
    dw_in_p = _dw_in(x, dproj, t, tk=tq)
    g_in = dw_in_p[None]
    from_sib_in, = _run(_sibling_rider([g_in], [(D_MODEL, IN_PAD)]), "exchange_sibling_w_in")
    half_in = _add_sibling(g_in, from_sib_in, c_arr, ADD_ROWS[0], "add_sibling_w_in")[0]
    p_in = jnp.stack([half_in[:, k * W_IN_S:(k + 1) * W_IN_S] for k in range(4)], axis=0)
    dx, others_in = _dx(dproj, w_in_p, dr1, t, _chips_rider([p_in]))
    sum_in = _add_chips(p_in, others_in, kc_arr, ADD_ROWS[0], "add_chips_w_in")
    return dx, [sum_in] + late_sums, vecs


W_IN_S, W_UP_S, W_OUT_S, W_DOWN_S = IN_WIDTH // 4, 2 * D_FF // 4, D_MODEL // 4, D_FF // 4
SHARD_SHAPES = ((D_MODEL, W_IN_S), (D_MODEL, W_UP_S), (W_OUT_S, D_MODEL), (W_DOWN_S, D_MODEL))
ADD_ROWS = (256, 256, 128, 176)
LATE_SHAPES, LATE_ADD_ROWS = SHARD_SHAPES[1:], ADD_ROWS[1:]
SMALL_ROWS = 8
VEC_SIZES = (("gate_bias", GLA_KEYS), ("gla_norm_g", LANES), ("ln1_g", D_MODEL), ("ln1_b", D_MODEL),
             ("conv_b", 2 * D_FF), ("ln2_g", D_MODEL), ("ln2_b", D_MODEL))
SMALL_GRADS = VEC_SIZES + (("conv_w", 3 * 2 * D_FF), ("gate_up", GATE_RANK * GLA_KEYS), ("loss", 1))


def _rows(a):
    flat = a.reshape(-1)
    pad = (-flat.shape[0]) % D_MODEL
    if pad:
        flat = jnp.pad(flat, (0, pad))
    return flat.reshape(-1, D_MODEL)


def _pad_rows(a, rows):
    return jnp.pad(a, ((0, rows - a.shape[0]), (0, 0)))


def _pack_vec(d, sizes, rows):
    flat = jnp.concatenate([d[n].reshape(-1) for n, _ in sizes])
    return _pad_rows(_rows(flat), rows)


def _unpack_vec(v, sizes):
    flat = v.reshape(-1)
    out, o = {}, 0
    for n, size in sizes:
        out[n] = flat[o:o + size].reshape(1, size)
        o += size
    return out


VEC_ROWS = 16
GRAD_VEC_ROWS = 32


HBM_SPEC = pl.BlockSpec(memory_space=pltpu.HBM)


def _position():
    x, y, c = lax.axis_index("x"), lax.axis_index("y"), lax.axis_index("c")
    chips = [(1 - x, y), (x, 1 - y), (1 - x, 1 - y)]
    return x, y, c, chips


def _remote(src, dst, send_sems, recv_sems, k, to):
    return pltpu.make_async_remote_copy(src_ref=src, dst_ref=dst, send_sem=send_sems.at[k], recv_sem=recv_sems.at[k],
                                        device_id=to, device_id_type=MESH)


def _gather_ici(in_refs, out_refs, shapes, send_sems, recv_sems, local_sems):
    x, y, c, chips = _position()
    k_me = 2 * x + y
    local, sends, recvs = [], [], []
    for m, (rows, _) in enumerate(shapes):
        h = rows // 2
        local.append(pltpu.make_async_copy(in_refs[m], out_refs[m].at[k_me], local_sems.at[m]))
        for j, (cx, cy) in enumerate(chips):
            sends.append(_remote(in_refs[m].at[pl.ds(c * h, h), :], out_refs[m].at[k_me, pl.ds(c * h, h), :],
                                 send_sems, recv_sems, 3 * m + j, (cx, cy, c)))
            landed = out_refs[m].at[2 * cx + cy, pl.ds(c * h, h), :]
            recvs.append(_remote(landed, landed, send_sems, recv_sems, 3 * m + j, (x, y, c)))
    return local, sends, recvs


def _gather_d2d(src_refs, dst_refs, shapes, send_sems, recv_sems, base):
    x, y, c, chips = _position()
    sends, recvs = [], []
    for m, (rows, _) in enumerate(shapes):
        h = rows // 2
        for j, (cx, cy) in enumerate(chips):
            k = 2 * cx + cy
            sends.append(_remote(src_refs[m].at[k, pl.ds(c * h, h), :], dst_refs[m].at[k, pl.ds(c * h, h), :],
                                 send_sems, recv_sems, base + 3 * m + j, (x, y, 1 - c)))
            landed = dst_refs[m].at[k, pl.ds((1 - c) * h, h), :]
            recvs.append(_remote(landed, landed, send_sems, recv_sems, base + 3 * m + j, (x, y, c)))
    return sends, recvs


def _gather_weights(shards, small, shapes):
    nm = len(shards)
    n_ici = 3 * nm

    def body(*refs):
        in_refs, small_ref = refs[:nm], refs[nm]
        out_refs, osm_ref = refs[nm + 1:2 * nm + 1], refs[2 * nm + 1]
        send_sems, recv_sems, local_sems = refs[2 * nm + 2:]
        x, y, c, chips = _position()
        k_me = 2 * x + y
        local, sends, recvs = _gather_ici(in_refs, out_refs, shapes, send_sems, recv_sems, local_sems)
        local.append(pltpu.make_async_copy(small_ref, osm_ref.at[k_me], local_sems.at[nm]))
        for j, (cx, cy) in enumerate(chips):
            sends.append(_remote(small_ref, osm_ref.at[k_me], send_sems, recv_sems, n_ici + j, (cx, cy, c)))
        for cp in local + sends:
            cp.start()
        fsends, frecvs = _gather_d2d(out_refs, out_refs, shapes, send_sems, recv_sems, n_ici + 3)
        for landed, onward in zip(recvs, fsends):
            landed.wait_recv()
            onward.start()
        for j, (cx, cy) in enumerate(chips):
            k = 2 * cx + cy
            frecvs.append(_remote(osm_ref.at[k], osm_ref.at[k], send_sems, recv_sems, n_ici + j, (x, y, c)))
        for cp in frecvs:
            cp.wait_recv()
        for cp in sends + fsends:
            cp.wait_send()
        for cp in local:
            cp.wait()

    n_sems = 2 * n_ici + 3
    return pl.pallas_call(
        body, name="gather_weights", in_specs=[HBM_SPEC] * (nm + 1), out_specs=[HBM_SPEC] * (nm + 1),
        out_shape=[jax.ShapeDtypeStruct((4,) + s, BF16) for s in shapes]
        + [jax.ShapeDtypeStruct((4, SMALL_ROWS, D_MODEL), F32)],
        scratch_shapes=[pltpu.SemaphoreType.DMA((n_sems,)), pltpu.SemaphoreType.DMA((n_sems,)),
                        pltpu.SemaphoreType.DMA((nm + 1,))],
    )(*shards, small)


def _gather_rider(shards, shapes):
    n = len(shards)
    return _Rider(shards, [jax.ShapeDtypeStruct((4,) + s, BF16) for s in shapes], (3 * n, 3 * n, n),
                  lambda ins, outs, sems: _gather_ici(ins, outs, shapes, *sems))


def _forward_rider(gathered, shapes):
    n = len(gathered)
    return _Rider(gathered, [jax.ShapeDtypeStruct(a.shape, a.dtype) for a in gathered], (3 * n, 3 * n),
                  lambda ins, outs, sems: ([],) + _gather_d2d(ins, outs, shapes, sems[0], sems[1], 0),
                  aliases=[(m, m) for m in range(n)])


def _sibling_rider(gs, shapes):
    def copies(ins, outs, sems):
        x, y, c, _ = _position()
        both = []
        for m, (rows, _) in enumerate(shapes):
            h = rows // 2
            for k in range(gs[m].shape[0]):
                both.append(_remote(ins[m].at[k, pl.ds((1 - c) * h, h), :], outs[m].at[k], sems[0], sems[1],
                                    4 * m + k, (x, y, 1 - c)))
        return [], both, both

    return _Rider(gs, [jax.ShapeDtypeStruct((g.shape[0], r // 2, cl), F32) for g, (r, cl) in zip(gs, shapes)],
                  (4 * len(gs), 4 * len(gs)), copies)


def _add_sibling(g, r, c_arr, tr, name):
    nblk, rows, cols = g.shape
    nb = rows // 2 // tr

    def body(c_ref, g_ref, r_ref, o_ref):
        o_ref[...] = (g_ref[...] + r_ref[...]).astype(BF16)

    spec = pl.BlockSpec((None, tr, cols), lambda k, i, c: (k, i, 0))
    return pl.pallas_call(
        body, name=name,
        grid_spec=pltpu.PrefetchScalarGridSpec(
            num_scalar_prefetch=1, grid=(nblk, nb),
            in_specs=[pl.BlockSpec((None, tr, cols), lambda k, i, c: (k, c[0] * nb + i, 0)), spec], out_specs=spec),
        out_shape=jax.ShapeDtypeStruct((nblk, rows // 2, cols), BF16), compiler_params=_cparams("parallel", "parallel"),
    )(c_arr, g, r)


def _reduce_ici(p_refs, r_refs, send_sems, recv_sems):
    x, y, c, chips = _position()
    sends, recvs = [], []
    for m in range(len(p_refs)):
        for j, (cx, cy) in enumerate(chips):
            sends.append(_remote(p_refs[m].at[2 * cx + cy], r_refs[m].at[j], send_sems, recv_sems, 3 * m + j, (cx, cy, c)))
            recvs.append(_remote(r_refs[m].at[j], r_refs[m].at[j], send_sems, recv_sems, 3 * m + j, (x, y, c)))
    return sends, recvs


def _chips_rider(ps, vec=None):
    nm = len(ps)
    n_ici = 3 * nm

    def copies(ins, outs, sems):
        sends, recvs = _reduce_ici(ins[:nm], outs[:nm], sems[0], sems[1])
        if vec is None:
            return [], sends, recvs
        x, y, c, _ = _position()
        my_id = 4 * x + 2 * y + c
        vec_ref, vrecv_ref = ins[nm], outs[nm]
        local = [pltpu.make_async_copy(vec_ref, vrecv_ref.at[my_id], sems[2].at[0])]
        for r in range(1, 8):
            peer = (1 - x if r & 4 else x, 1 - y if r & 2 else y, 1 - c if r & 1 else c)
            sends.append(_remote(vec_ref, vrecv_ref.at[my_id], sems[0], sems[1], n_ici + r - 1, peer))
            recvs.append(_remote(vec_ref, vrecv_ref.at[0], sems[0], sems[1], n_ici + r - 1, (x, y, c)))
        return local, sends, recvs

    out_shapes = [jax.ShapeDtypeStruct((3,) + p.shape[1:], p.dtype) for p in ps]
    if vec is None:
        return _Rider(ps, out_shapes, (n_ici, n_ici), copies)
    return _Rider(list(ps) + [vec], out_shapes + [jax.ShapeDtypeStruct((8, GRAD_VEC_ROWS, D_MODEL), F32)],
                  (n_ici + 7, n_ici + 7, 1), copies)


def _add_chips(p, r, kc_arr, tr, name):
    _, h, cols = p.shape
    nb = h // tr

    def body(kc_ref, p_ref, r0_ref, r1_ref, r2_ref, o_ref):
        o_ref[...] = ((p_ref[...].astype(F32) + r0_ref[...].astype(F32)) + r1_ref[...].astype(F32)) + r2_ref[...].astype(F32)

    rspec = lambda j: pl.BlockSpec((None, tr, cols), lambda i, kc: (j, i, 0))
    return pl.pallas_call(
        body, name=name,
        grid_spec=pltpu.PrefetchScalarGridSpec(
            num_scalar_prefetch=1, grid=(nb,),
            in_specs=[pl.BlockSpec((None, tr, cols), lambda i, kc: (kc[0], i, 0)), rspec(0), rspec(1), rspec(2)],
            out_specs=pl.BlockSpec((tr, cols), lambda i, kc: (kc[1] * nb + i, 0))),
        out_shape=jax.ShapeDtypeStruct((2 * h, cols), F32), compiler_params=_cparams("parallel"),
    )(kc_arr, p, r, r, r)


def _reunite_sibling(fs, shapes):
    n_chunks = 2
    nm = len(fs)

    def body(*refs):
        in_refs, out_refs = refs[:nm], refs[nm:2 * nm]
        send_sems, recv_sems = refs[2 * nm:]
        x, y, c, _ = _position()
        sends, recvs = [], []
        for m in range(nm):
            ch = shapes[m][0] // 2 // n_chunks
            for q in range(n_chunks):
                mine = pl.ds((c * n_chunks + q) * ch, ch)
                theirs = pl.ds(((1 - c) * n_chunks + q) * ch, ch)
                s = n_chunks * m + q
                sends.append(_remote(in_refs[m].at[mine, :], out_refs[m].at[mine, :], send_sems, recv_sems, s, (x, y, 1 - c)))
                recvs.append(_remote(in_refs[m].at[theirs, :], out_refs[m].at[theirs, :], send_sems, recv_sems, s, (x, y, c)))
        for cp in sends:
            cp.start()
        for cp in recvs:
            cp.wait_recv()
        for cp in sends:
            cp.wait_send()

    n_sems = n_chunks * nm
    return pl.pallas_call(
        body, name="reunite_sibling", in_specs=[HBM_SPEC] * nm, out_specs=[HBM_SPEC] * nm,
        out_shape=[jax.ShapeDtypeStruct(s, F32) for s in shapes],
        input_output_aliases={m: m for m in range(nm)},
        scratch_shapes=[pltpu.SemaphoreType.DMA((n_sems,)), pltpu.SemaphoreType.DMA((n_sems,))],
    )(*fs)


def _sum_vecs(v):
    def body(v_ref, o_ref):
        acc = v_ref[0]
        for d in range(1, 8):
            acc = acc + v_ref[d]
        o_ref[...] = acc

    return pl.pallas_call(body, name="sum_vecs", out_shape=jax.ShapeDtypeStruct(v.shape[1:], F32))(v)


def kernel(x, w_in, gate_up, gate_bias, gla_norm_g, w_out, ln1_g, ln1_b, w_up, conv_w, conv_b, w_down, ln2_g, ln2_b, loss_target, m_w_in, m_gate_up, m_gate_bias, m_gla_norm_g, m_w_out, m_ln1_g, m_ln1_b, m_w_up, m_conv_w, m_conv_b, m_w_down, m_ln2_g, m_ln2_b, v_w_in, v_gate_up, v_gate_bias, v_gla_norm_g, v_w_out, v_ln1_g, v_ln1_b, v_w_up, v_conv_w, v_conv_b, v_w_down, v_ln2_g, v_ln2_b):
    xi, yi, ci = lax.axis_index("x"), lax.axis_index("y"), lax.axis_index("c")
    k_me = 2 * xi + yi
    c_arr = ci.astype(jnp.int32).reshape(1)
    kc_arr = jnp.stack([k_me, ci]).astype(jnp.int32)
    small = _pad_rows(jnp.concatenate([_rows(conv_w[0]), _rows(gate_up[0])], axis=0), SMALL_ROWS)
    w_in4, gsmall = _gather_weights([w_in[0].astype(BF16)], small, SHARD_SHAPES[:1])
    late_shards = [w_up[0].astype(BF16), w_out[0].astype(BF16), w_down[0].astype(BF16)]
    w_in_p = jnp.pad(jnp.concatenate([w_in4[k] for k in range(4)], axis=1), ((0, 0), (0, IN_PAD - IN_WIDTH)))
    conv_w_f = jnp.concatenate([gsmall[k, :5].reshape(-1)[:3 * W_UP_S].reshape(3, W_UP_S) for k in range(4)], axis=1)
    gate_up_f = jnp.concatenate([gsmall[k, 5].reshape(GATE_RANK, GLA_KEYS // 4) for k in range(4)], axis=1)
    conv_w3 = jnp.transpose(conv_w_f.reshape(3, 2, D_FF), (1, 0, 2))
    conv_b3 = conv_b.reshape(2, 1, D_FF)
    gate_up_pad = jnp.pad(gate_up_f, ((0, LANES - GATE_RANK), (0, 0)))

    dx, sums, vecs = _local_step(
        x[0], loss_target[0], w_in_p, late_shards, gate_up_pad, gate_bias, gla_norm_g, ln1_g, ln1_b, conv_w3, conv_b3,
        ln2_g, ln2_b, c_arr, kc_arr)
    g_w_in, g_w_up, g_w_out, g_w_down = _reunite_sibling(sums, SHARD_SHAPES)
    gsmall_sum = _unpack_vec(_sum_vecs(vecs), SMALL_GRADS)
    g_conv_w = lax.dynamic_slice_in_dim(gsmall_sum["conv_w"].reshape(3, 2 * D_FF), k_me * W_UP_S, W_UP_S, axis=1)
    g_gate_up = lax.dynamic_slice_in_dim(gsmall_sum["gate_up"].reshape(GATE_RANK, GLA_KEYS), k_me * (GLA_KEYS // 4),
                                         GLA_KEYS // 4, axis=1)
    gv = gsmall_sum
    loss = gv["loss"][0, 0]
    gvec = _pack_vec(gv, VEC_SIZES, VEC_ROWS)

    grads = dict(w_in=g_w_in[None], gate_up=g_gate_up[None], gate_bias=gv["gate_bias"], gla_norm_g=gv["gla_norm_g"],
                 w_out=g_w_out[None], ln1_g=gv["ln1_g"], ln1_b=gv["ln1_b"], w_up=g_w_up[None], conv_w=g_conv_w[None],
                 conv_b=gv["conv_b"], w_down=g_w_down[None], ln2_g=gv["ln2_g"], ln2_b=gv["ln2_b"])
    weights = dict(w_in=w_in, gate_up=gate_up, gate_bias=gate_bias, gla_norm_g=gla_norm_g, w_out=w_out, ln1_g=ln1_g,
                   ln1_b=ln1_b, w_up=w_up, conv_w=conv_w, conv_b=conv_b, w_down=w_down, ln2_g=ln2_g, ln2_b=ln2_b)
    ms = dict(w_in=m_w_in, gate_up=m_gate_up, gate_bias=m_gate_bias, gla_norm_g=m_gla_norm_g, w_out=m_w_out, ln1_g=m_ln1_g,
              ln1_b=m_ln1_b, w_up=m_w_up, conv_w=m_conv_w, conv_b=m_conv_b, w_down=m_w_down, ln2_g=m_ln2_g, ln2_b=m_ln2_b)
    vs = dict(w_in=v_w_in, gate_up=v_gate_up, gate_bias=v_gate_bias, gla_norm_g=v_gla_norm_g, w_out=v_w_out, ln1_g=v_ln1_g,
              ln1_b=v_ln1_b, w_up=v_w_up, conv_w=v_conv_w, conv_b=v_conv_b, w_down=v_w_down, ln2_g=v_ln2_g, ln2_b=v_ln2_b)
    names = ["w_in", "gate_up", "gate_bias", "gla_norm_g", "w_out", "ln1_g", "ln1_b", "w_up", "conv_w", "conv_b", "w_down",
             "ln2_g", "ln2_b"]
    delta, new_m, new_v = {}, {}, {}
    for n in ("w_in", "gate_up", "w_out", "w_up", "conv_w", "w_down"):
        tr_ = jnp.transpose if n == "w_in" else (lambda a: a)
        d, nm, nv = _adamw(tr_(weights[n][0]), tr_(grads[n][0]), tr_(ms[n][0]), tr_(vs[n][0]), "adamw_" + n)
        delta[n], new_m[n], new_v[n] = tr_(d)[None], tr_(nm)[None], tr_(nv)[None]
    d, nm, nv = _adamw(_pack_vec(weights, VEC_SIZES, VEC_ROWS), gvec, _pack_vec(ms, VEC_SIZES, VEC_ROWS),
                       _pack_vec(vs, VEC_SIZES, VEC_ROWS), "adamw_vectors")
    for dst, src in ((delta, d), (new_m, nm), (new_v, nv)):
        dst.update(_unpack_vec(src, VEC_SIZES))
    return (loss, dx[None], *[grads[n] for n in names], *[delta[n] for n in names], *[new_m[n] for n in names],
            *[new_v[n] for n in names])
```

```python
import functools
import math

import jax
import jax.numpy as jnp
from jax import lax
from jax.experimental import pallas as pl
from jax.experimental.pallas import tpu as pltpu

F32 = jnp.float32
BF16 = jnp.bfloat16

D_MODEL = 1024
SB_WIDTH = 512
GLA_KEYS = 256
GLA_WIDTH = 512
GATE_RANK = 16
IN_WIDTH = 3088
IN_PAD = 3200
D_FF = 2816
CHUNK = 64
LN_EPS = 1e-5
RMS_EPS = 1e-6
ALPHA = 2.0 ** 0.25
GLA_TAU = 16.0
SB_SCALE = 0.125
GLA_SCALE = 0.125
LANES = 128
SB_BLK = 256
SB_CUT = -100.0
GLA_ROWS = 256
VMEM_LIMIT = 56 * 1024 * 1024

ADAM_LR, ADAM_B1, ADAM_B2, ADAM_EPS, ADAM_WD, ADAM_STEP = 0.001, 0.9, 0.999, 1e-08, 0.01, 10

MESH = pl.DeviceIdType.MESH


def _cparams(*sem):
    return pltpu.CompilerParams(dimension_semantics=sem, vmem_limit_bytes=VMEM_LIMIT)


def _dot(a, b):
    return jnp.dot(a, b, preferred_element_type=F32)


def _dot_nt(a, b):
    return lax.dot_general(a, b, (((1,), (1,)), ((), ())), preferred_element_type=F32)


def _dot_tn(a, b):
    return lax.dot_general(a, b, (((0,), (0,)), ((), ())), preferred_element_type=F32)


def _split3(x):
    hi = x.astype(BF16)
    r = x - hi.astype(F32)
    mid = r.astype(BF16)
    lo = (r - mid.astype(F32)).astype(BF16)
    return hi, mid, lo


def _softplus(z):
    return jnp.maximum(z, 0.0) + jnp.log(1.0 + jnp.exp(-jnp.abs(z)))


def _sigmoid(z):
    return 1.0 / (1.0 + jnp.exp(-z))


def _mm(a, b, *, m, n, k, tm, tn, tk, ta=False, tb=False, a_spec=None, b_spec=None, o_spec=None,
        out_shape=None, out_dtype=F32, add=None, add_scale=1.0, rider=None, name):
    nk = k // tk
    dn = (((0 if ta else 1,), (1 if tb else 0,)), ((), ()))
    n_in = len(rider.inputs) if rider else 0
    n_out = len(rider.out_shapes) if rider else 0
    n_add = int(add is not None)
    steps = (m // tm, n // tn, nk)

    def body(*refs):
        a_ref, b_ref = refs[:2]
        add_ref = refs[2] if add is not None else None
        rin = refs[2 + n_add:2 + n_add + n_in]
        o_ref = refs[2 + n_add + n_in]
        rout = refs[3 + n_add + n_in:3 + n_add + n_in + n_out]
        scratch = refs[3 + n_add + n_in + n_out:]
        if rider:
            sems = scratch[len(scratch) - len(rider.sems):]
            ids = [pl.program_id(d) for d in range(3)]
            _ride(rider, rin, rout, sems, (ids[0] == 0) & (ids[1] == 0) & (ids[2] == 0),
                  (ids[0] == steps[0] - 1) & (ids[1] == steps[1] - 1) & (ids[2] == steps[2] - 1))

        part = lax.dot_general(a_ref[...].astype(BF16), b_ref[...].astype(BF16), dn, preferred_element_type=F32)

        def finish(r):
            if add is not None:
                r = r + add_scale * add_ref[...]
            o_ref[...] = r.astype(out_dtype)

        if nk == 1:
            finish(part)
            return
        acc_ref = scratch[0]
        kk = pl.program_id(2)

        @pl.when(kk == 0)
        def _():
            acc_ref[...] = part

        @pl.when((kk > 0) & (kk < nk - 1))
        def _():
            acc_ref[...] += part

        @pl.when(kk == nk - 1)
        def _():
            finish(acc_ref[...] + part)

    if a_spec is None:
        a_spec = pl.BlockSpec((tk, tm), lambda i, j, kk: (kk, i)) if ta else pl.BlockSpec((tm, tk), lambda i, j, kk: (i, kk))
    if b_spec is None:
        b_spec = pl.BlockSpec((tn, tk), lambda i, j, kk: (j, kk)) if tb else pl.BlockSpec((tk, tn), lambda i, j, kk: (kk, j))
    if o_spec is None:
        o_spec = pl.BlockSpec((tm, tn), lambda i, j, kk: (i, j))
    if out_shape is None:
        out_shape = jax.ShapeDtypeStruct((m, n), out_dtype)
    in_specs = [a_spec, b_spec]
    args = [a, b]
    if add is not None:
        in_specs.append(pl.BlockSpec((tm, tn), lambda i, j, kk: (i, j)))
        args.append(add)
    scratch = [pltpu.VMEM((tm, tn), F32)] if nk > 1 else []
    if not rider:
        return pl.pallas_call(
            body, name=name, grid=steps, in_specs=in_specs, out_specs=o_spec, out_shape=out_shape,
            scratch_shapes=scratch, compiler_params=_cparams("parallel", "parallel", "arbitrary"),
        )(*args)
    return pl.pallas_call(
        body, name=name, grid=steps, in_specs=in_specs + [HBM_SPEC] * n_in, out_specs=[o_spec] + [HBM_SPEC] * n_out,
        out_shape=[out_shape] + list(rider.out_shapes),
        input_output_aliases={len(args) + i: 1 + o for i, o in rider.aliases},
        scratch_shapes=scratch + [pltpu.SemaphoreType.DMA((s,)) for s in rider.sems],
        compiler_params=_cparams("arbitrary", "arbitrary", "arbitrary"),
    )(*args, *rider.inputs)


def _col_offsets(pieces):
    offs, o = [], 0
    for a in pieces:
        offs.append(o)
        o += a.shape[1]
    return offs


def _dx(pieces, w_in_p, dr1, t, rider, tm=512):
    offs = _col_offsets(pieces)
    npc = len(pieces)
    n_in, n_out = len(rider.inputs), len(rider.out_shapes)
    steps = t // tm

    def body(*refs):
        p_refs, w_ref, add_ref = refs[:npc], refs[npc], refs[npc + 1]
        rin = refs[npc + 2:npc + 2 + n_in]
        o_ref = refs[npc + 2 + n_in]
        rout = refs[npc + 3 + n_in:npc + 3 + n_in + n_out]
        i = pl.program_id(0)
        _ride(rider, rin, rout, refs[npc + 3 + n_in + n_out:], i == 0, i == steps - 1)
        acc = ALPHA * add_ref[...]
        for p_ref, off in zip(p_refs, offs):
            acc = acc + _dot_nt(p_ref[...].astype(BF16), w_ref[:, off:off + p_ref.shape[1]])
        o_ref[...] = acc

    row = lambda w: pl.BlockSpec((tm, w), lambda i: (i, 0))
    return pl.pallas_call(
        body, name="dx", grid=(steps,),
        in_specs=[row(a.shape[1]) for a in pieces] + [pl.BlockSpec(w_in_p.shape, lambda i: (0, 0)), row(D_MODEL)]
        + [HBM_SPEC] * n_in,
        out_specs=[row(D_MODEL)] + [HBM_SPEC] * n_out,
        out_shape=[jax.ShapeDtypeStruct((t, D_MODEL), F32)] + rider.out_shapes,
        scratch_shapes=[pltpu.SemaphoreType.DMA((s,)) for s in rider.sems],
        compiler_params=_cparams("arbitrary"),
    )(*pieces, w_in_p, dr1, *rider.inputs)


def _dw_out(sb_o, gla_o, dr1, t, tk):
    def body(sb_ref, gl_ref, dr_ref, o_ref):
        kk = pl.program_id(0)
        drb = dr_ref[...].astype(BF16)
        top, bottom = _dot_tn(sb_ref[...], drb), _dot_tn(gl_ref[...], drb)

        @pl.when(kk == 0)
        def _():
            o_ref[0:SB_WIDTH, :] = top
            o_ref[SB_WIDTH:, :] = bottom

        @pl.when(kk > 0)
        def _():
            o_ref[0:SB_WIDTH, :] += top
            o_ref[SB_WIDTH:, :] += bottom

    return pl.pallas_call(
        body, name="dw_out", grid=(t // tk,),
        in_specs=[pl.BlockSpec((tk, SB_WIDTH), lambda kk: (kk, 0)), pl.BlockSpec((tk, GLA_WIDTH), lambda kk: (kk, 0)),
                  pl.BlockSpec((tk, D_MODEL), lambda kk: (kk, 0))],
        out_specs=pl.BlockSpec((D_MODEL, D_MODEL), lambda kk: (0, 0)),
        out_shape=jax.ShapeDtypeStruct((D_MODEL, D_MODEL), F32),
        compiler_params=_cparams("arbitrary"),
    )(sb_o, gla_o, dr1)


def _dw_in(x, pieces, t, tm=512, tk=512):
    offs = _col_offsets(pieces)
    npc = len(pieces)
    n = offs[-1] + pieces[-1].shape[1]
    nk = t // tk

    def body(*refs):
        x_ref, p_refs, o_ref, acc_ref = refs[0], refs[1:1 + npc], refs[1 + npc], refs[2 + npc]
        kk = pl.program_id(1)
        xb = x_ref[...].astype(BF16)
        for p_ref, off in zip(p_refs, offs):
            cols = slice(off, off + p_ref.shape[1])
            part = _dot_tn(xb, p_ref[...].astype(BF16))
            if nk == 1:
                o_ref[:, cols] = part
                continue

            @pl.when(kk == 0)
            def _():
                acc_ref[:, cols] = part

            @pl.when((kk > 0) & (kk < nk - 1))
            def _():
                acc_ref[:, cols] += part

            @pl.when(kk == nk - 1)
            def _():
                o_ref[:, cols] = acc_ref[:, cols] + part

    return pl.pallas_call(
        body, name="dw_in", grid=(D_MODEL // tm, nk),
        in_specs=[pl.BlockSpec((tk, tm), lambda i, kk: (kk, i))]
        + [pl.BlockSpec((tk, a.shape[1]), lambda i, kk: (kk, 0)) for a in pieces],
        out_specs=pl.BlockSpec((tm, n), lambda i, kk: (i, 0)),
        out_shape=jax.ShapeDtypeStruct((D_MODEL, n), F32),
        scratch_shapes=[pltpu.VMEM((tm, n), F32)],
        compiler_params=_cparams("parallel", "arbitrary"),
    )(x, *pieces)


class _Rider:
    def __init__(self, inputs, out_shapes, sems, copies, aliases=()):
        self.inputs, self.out_shapes, self.sems, self.copies = list(inputs), list(out_shapes), tuple(sems), copies
        self.aliases = tuple(aliases)


def _join(a, b):
    na_in, na_out, na_sems = len(a.inputs), len(a.out_shapes), len(a.sems)

    def copies(ins, outs, sems):
        first = a.copies(ins[:na_in], outs[:na_out], sems[:na_sems])
        second = b.copies(ins[na_in:], outs[na_out:], sems[na_sems:])
        return tuple(u + v for u, v in zip(first, second))

    return _Rider(a.inputs + b.inputs, a.out_shapes + b.out_shapes, a.sems + b.sems, copies,
                  a.aliases + tuple((i + na_in, o + na_out) for i, o in b.aliases))


def _ride(rider, in_refs, out_refs, sems, first, last):
    @pl.when(first)
    def _():
        local, sends, _ = rider.copies(in_refs, out_refs, sems)
        for cp in local + sends:
            cp.start()

    @pl.when(last)
    def _():
        local, sends, recvs = rider.copies(in_refs, out_refs, sems)
        for cp in recvs:
            cp.wait_recv()
        for cp in sends:
            cp.wait_send()
        for cp in local:
            cp.wait()


def _run(rider, name):
    n_in, n_out = len(rider.inputs), len(rider.out_shapes)

    def body(*refs):
        local, sends, recvs = rider.copies(refs[:n_in], refs[n_in:n_in + n_out], refs[n_in + n_out:])
        for cp in local + sends:
            cp.start()
        for cp in recvs:
            cp.wait_recv()
        for cp in sends:
            cp.wait_send()
        for cp in local:
            cp.wait()

    return pl.pallas_call(
        body, name=name, in_specs=[HBM_SPEC] * n_in, out_specs=[HBM_SPEC] * n_out, out_shape=list(rider.out_shapes),
        scratch_shapes=[pltpu.SemaphoreType.DMA((s,)) for s in rider.sems],
    )(*rider.inputs)


def _sb_tile(qh, kj, diag, strict, u_strict, r_in):
    z = _dot_nt(qh, kj)
    sp = _softplus(z)
    l1m = -sp
    lsz = z - sp
    if diag:
        l1m = jnp.where(strict, l1m, 0.0)
    cs = _dot(l1m.astype(BF16), u_strict) + r_in
    w = jnp.exp(lsz + cs)
    if diag:
        w = jnp.where(strict, w, 0.0)
    return l1m, lsz, w


def _sb_consts():
    row = lax.broadcasted_iota(jnp.int32, (SB_BLK, SB_BLK), 0)
    col = lax.broadcasted_iota(jnp.int32, (SB_BLK, SB_BLK), 1)
    strict = col < row
    u_strict = (row > col).astype(BF16)
    u_pre = (row < col).astype(BF16)
    lane = lax.broadcasted_iota(jnp.int32, (1, LANES), 1)
    return strict, u_strict, u_pre, lane


def _sb_fwd(proj, t, rider):
    nq = t // SB_BLK
    n_in, n_out = len(rider.inputs), len(rider.out_shapes)

    def body(q_ref, k_ref, v_ref, *rest):
        rin, o_ref, sv_ref, rout = rest[:n_in], rest[n_in], rest[n_in + 1], rest[n_in + 2:n_in + 2 + n_out]
        p = pl.program_id(0)
        i = pl.program_id(1)
        _ride(rider, rin, rout, rest[n_in + 2 + n_out:], (p == 0) & (i == 0), (p == 3) & (i == nq - 1))

        strict, u_strict, _, lane = _sb_consts()
        qf = q_ref[...] * SB_SCALE
        hms = [(lane // 64) == hh for hh in range(2)]
        qhs = [jnp.where(hm, qf, 0.0).astype(BF16) for hm in hms]

        def step(j, r0, r1, a, sv, diag, keep=None):
            rows = pl.ds(pl.multiple_of(j * SB_BLK, SB_BLK), SB_BLK)
            kj = k_ref[rows, :].astype(BF16)
            vf = v_ref[rows, :]
            rs = []
            for hh, r in enumerate((r0, r1)):
                l1m, _, w = _sb_tile(qhs[hh], kj, diag, strict, u_strict, r)
                pv = _dot(w.astype(BF16), jnp.where(hms[hh], vf, 0.0).astype(BF16))
                mass = jnp.sum(l1m, axis=1, keepdims=True)
                if keep is not None:
                    pv, mass = jnp.where(keep, pv, 0.0), jnp.where(keep, mass, 0.0)
                a = a + pv
                rs.append(r + mass)
                sv = jnp.where(lane == hh * 64 + (i - j + 1), rs[hh], sv)
            return rs[0], rs[1], a, sv

        zero = jnp.zeros((SB_BLK, 1), F32)
        acc0 = jnp.zeros((SB_BLK, LANES), F32)
        r0, r1, acc, sv = step(i, zero, zero, acc0, acc0, True)
        r0, r1, acc, sv = step(jnp.maximum(i - 1, 0), r0, r1, acc, sv, False, keep=i > 0)
        _, _, _, acc, sv = lax.while_loop(
            lambda c: (c[0] >= 0) & (jnp.maximum(jnp.max(c[1]), jnp.max(c[2])) > SB_CUT),
            lambda c: (c[0] - 1,) + step(c[0], c[1], c[2], c[3], c[4], False),
            (i - 2, r0, r1, acc, sv))
        o_ref[...] = acc.astype(BF16)
        sv_ref[...] = sv

    return pl.pallas_call(
        body, name="sb_fwd", grid=(4, nq),
        in_specs=[pl.BlockSpec((SB_BLK, LANES), lambda p, i: (i, p)),
                  pl.BlockSpec((t, LANES), lambda p, i: (0, 4 + p)),
                  pl.BlockSpec((t, LANES), lambda p, i: (0, 8 + p))] + [HBM_SPEC] * n_in,
        out_specs=[pl.BlockSpec((SB_BLK, LANES), lambda p, i: (i, p))] * 2 + [HBM_SPEC] * n_out,
        out_shape=[jax.ShapeDtypeStruct((t, SB_WIDTH), BF16), jax.ShapeDtypeStruct((t, SB_WIDTH), F32)] + rider.out_shapes,
        scratch_shapes=[pltpu.SemaphoreType.DMA((s,)) for s in rider.sems],
        compiler_params=_cparams("arbitrary", "arbitrary"),
    )(proj, proj, proj, *rider.inputs)


def _sb_bwd(proj, dcat, mass, t, rider):
    nq = t // SB_BLK
    n_in, n_out = len(rider.inputs), len(rider.out_shapes)

    def body(q_ref, k_ref, v_ref, do_ref, sv_ref, *rest):
        rin = rest[:n_in]
        dq_ref, dk_ref, dv_ref = rest[n_in:n_in + 3]
        rout = rest[n_in + 3:n_in + 3 + n_out]
        p = pl.program_id(0)
        i = pl.program_id(1)
        _ride(rider, rin, rout, rest[n_in + 3 + n_out:], (p == 0) & (i == 0), (p == 3) & (i == nq - 1))

        @pl.when(i == 0)
        def _():
            dk_ref[...] = jnp.zeros_like(dk_ref)
            dv_ref[...] = jnp.zeros_like(dv_ref)

        strict, u_strict, u_pre, lane = _sb_consts()
        qf = q_ref[...] * SB_SCALE
        dof = do_ref[...]
        hms = [(lane // 64) == hh for hh in range(2)]
        qhs = [jnp.where(hm, qf, 0.0).astype(BF16) for hm in hms]
        dohs = [jnp.where(hm, dof, 0.0).astype(BF16) for hm in hms]

        sv = sv_ref[...]
        zero = jnp.zeros((SB_BLK, 1), F32)

        def mass_right(hh, d):
            return jnp.sum(jnp.where(lane == hh * 64 + d, sv, 0.0), axis=1, keepdims=True)

        dstop = lax.while_loop(
            lambda d: (i - d >= 0) & (jnp.maximum(jnp.max(mass_right(0, d)), jnp.max(mass_right(1, d))) > SB_CUT),
            lambda d: d + 1, 1)
        jstop = i - dstop

        def step(j, carry, diag, keep=None):
            pre_g0, pre_g1, dqa = carry
            rows = pl.ds(pl.multiple_of(j * SB_BLK, SB_BLK), SB_BLK)
            kf = k_ref[rows, :]
            kj = kf.astype(BF16)
            vj = v_ref[rows, :].astype(BF16)
            dv = jnp.zeros((SB_BLK, LANES), F32)
            dk = jnp.zeros((SB_BLK, LANES), F32)
            dqj = jnp.zeros((SB_BLK, LANES), F32)
            pre = []
            for hh, pre_g in enumerate((pre_g0, pre_g1)):
                _, lsz, w = _sb_tile(qhs[hh], kj, diag, strict, u_strict, zero if diag else mass_right(hh, i - j))
                g = w * _dot_nt(dohs[hh], vj)
                gpre = _dot(g.astype(BF16), u_pre) + pre_g
                sig = jnp.exp(lsz)
                dz = g * (1.0 - sig) - gpre * sig
                if diag:
                    dz = jnp.where(strict, dz, 0.0)
                dzb = dz.astype(BF16)
                dv = dv + _dot_tn(w.astype(BF16), dohs[hh])
                dk = dk + _dot_tn(dzb, qhs[hh])
                dqj = dqj + _dot(dzb, jnp.where(hms[hh], kf, 0.0).astype(BF16))
                gsum = jnp.sum(g, axis=1, keepdims=True)
                pre.append(pre_g + (gsum if keep is None else jnp.where(keep, gsum, 0.0)))
            if keep is not None:
                dv, dk, dqj = jnp.where(keep, dv, 0.0), jnp.where(keep, dk, 0.0), jnp.where(keep, dqj, 0.0)
            dv_ref[rows, :] += dv
            dk_ref[rows, :] += dk
            return pre[0], pre[1], dqa + dqj

        carry = lax.fori_loop(jstop + 1, i - 1, lambda j, c: step(j, c, False), (zero, zero, jnp.zeros((SB_BLK, LANES), F32)))
        carry = step(jnp.maximum(i - 1, 0), carry, False, keep=i > 0)
        _, _, dq = step(i, carry, True)
        dq_ref[...] = (dq * SB_SCALE).astype(BF16)

    return pl.pallas_call(
        body, name="sb_bwd", grid=(4, nq),
        in_specs=[pl.BlockSpec((SB_BLK, LANES), lambda p, i: (i, p)),
                  pl.BlockSpec((t, LANES), lambda p, i: (0, 4 + p)),
                  pl.BlockSpec((t, LANES), lambda p, i: (0, 8 + p)),
                  pl.BlockSpec((SB_BLK, LANES), lambda p, i: (i, p)),
                  pl.BlockSpec((SB_BLK, LANES), lambda p, i: (i, p))] + [HBM_SPEC] * n_in,
        out_specs=[pl.BlockSpec((SB_BLK, LANES), lambda p, i: (i, p)),
                   pl.BlockSpec((t, LANES), lambda p, i: (0, p)),
                   pl.BlockSpec((t, LANES), lambda p, i: (0, p))] + [HBM_SPEC] * n_out,
        out_shape=[jax.ShapeDtypeStruct((t, SB_WIDTH), BF16)] + [jax.ShapeDtypeStruct((t, SB_WIDTH), F32)] * 2
        + rider.out_shapes,
        scratch_shapes=[pltpu.SemaphoreType.DMA((s,)) for s in rider.sems],
        compiler_params=_cparams("arbitrary", "arbitrary"),
    )(proj, proj, proj, dcat, mass, *rider.inputs)


def _gla_consts():
    r = lax.broadcasted_iota(jnp.int32, (GLA_ROWS, GLA_ROWS), 0)
    c = lax.broadcasted_iota(jnp.int32, (GLA_ROWS, GLA_ROWS), 1)
    same = (r // CHUNK) == (c // CHUNK)
    causal = same & (c <= r)
    upto_mid = c % CHUNK <= CHUNK // 2 - 1
    fwd_stack = jnp.concatenate([causal, same & upto_mid, same], axis=0).astype(BF16)
    bwd_stack = jnp.concatenate([same & (c >= r), same & (r % CHUNK <= CHUNK // 2 - 1), same], axis=1).astype(BF16)
    rowid = lax.broadcasted_iota(jnp.int32, (GLA_ROWS, 1), 0)
    lane = lax.broadcasted_iota(jnp.int32, (1, LANES), 1)
    sr = lax.broadcasted_iota(jnp.int32, (LANES, 2 * LANES), 0)
    sc = lax.broadcasted_iota(jnp.int32, (LANES, 2 * LANES), 1)
    blockdiag = (sr // 64) == (sc // LANES)
    return causal, fwd_stack, bwd_stack, rowid, lane, blockdiag


def _dot3(u, x):
    hi, mid, lo = _split3(x)
    return _dot(u, hi) + _dot(u, mid) + _dot(u, lo)


def _row_to_col(row):
    return jnp.transpose(jnp.broadcast_to(row, (LANES, LANES)))


def _gla_gates(ga_ref, gup_ref, gbias_ref):
    pre = _dot(ga_ref[...].astype(BF16), gup_ref[...].astype(BF16)) + gbias_ref[...]
    log_a = (jnp.minimum(pre, 0.0) - jnp.log(1.0 + jnp.exp(-jnp.abs(pre)))) / GLA_TAU
    return pre, log_a


def _gla_terms(g2, q2, k2, fwd_stack):
    bs = _dot3(fwd_stack, g2)
    b, b_ref, b_last = bs[:GLA_ROWS], bs[GLA_ROWS:2 * GLA_ROWS], bs[2 * GLA_ROWS:]
    qs = q2 * GLA_SCALE
    e_q = jnp.exp(b - b_ref)
    e_k = jnp.exp(b_ref - b)
    e_d = jnp.exp(b_last - b)
    e_b = jnp.exp(b)
    decay = jnp.exp(b_last)
    return dict(qs=qs, e_q=e_q, e_k=e_k, e_d=e_d, e_b=e_b, decay=decay,
                qi=qs * e_q, ki=k2 * e_k, kd=k2 * e_d, qb=qs * e_b)


def _gla_fwd(proj, gate_up_pad, gate_bias, gnorm, t, rider):
    nsteps = t // GLA_ROWS
    cps = GLA_ROWS // CHUNK
    n_in, n_out = len(rider.inputs), len(rider.out_shapes)

    def body(q_ref, k_ref, v_ref, gg_ref, ga_ref, gup_ref, gbias_ref, gn_ref, *rest):
        o_ref, oraw_ref, st_ref = rest[n_in:n_in + 3]
        s_scr = rest[n_in + 3 + n_out]
        i = pl.program_id(0)
        _ride(rider, rest[:n_in], rest[n_in + 3:n_in + 3 + n_out], rest[n_in + 4 + n_out:], i == 0, i == nsteps - 1)

        @pl.when(i == 0)
        def _():
            s_scr[...] = jnp.zeros_like(s_scr)

        _, fwd_stack, _, _, lane, blockdiag = _gla_consts()
        causal64 = (lax.broadcasted_iota(jnp.int32, (CHUNK, CHUNK), 1) <= lax.broadcasted_iota(jnp.int32, (CHUNK, CHUNK), 0))
        _, log_a = _gla_gates(ga_ref, gup_ref, gbias_ref)
        gn = gn_ref[...]
        pre = []
        for p in range(2):
            kl = slice(p * LANES, (p + 1) * LANES)
            tm = _gla_terms(log_a[:, kl], q_ref[:, kl], k_ref[:, kl], fwd_stack)
            pre.append(dict(
                v2b=v_ref[:, p * 2 * LANES:(p + 1) * 2 * LANES].astype(BF16), kib=tm["ki"].astype(BF16),
                kdb=tm["kd"].astype(BF16), qbb=tm["qb"].astype(BF16), decay=tm["decay"],
                qihb=[jnp.where((lane // 64) == hh, tm["qi"], 0.0).astype(BF16) for hh in range(2)]))
        state = [s_scr[0], s_scr[1]]
        outs = [[], []]
        for cc in range(cps):
            rows = slice(cc * CHUNK, (cc + 1) * CHUNK)
            for p in range(2):
                w = pre[p]
                st_ref[cc, p] = state[p]
                intra = []
                for hh in range(2):
                    a = jnp.where(causal64, _dot_nt(w["qihb"][hh][rows], w["kib"][rows]), 0.0)
                    intra.append(_dot(a.astype(BF16), w["v2b"][rows, hh * LANES:(hh + 1) * LANES]))
                outs[p].append(jnp.concatenate(intra, axis=1) + _dot(w["qbb"][rows], state[p].astype(BF16)))
                upd = jnp.where(blockdiag, _dot_tn(w["kdb"][rows], w["v2b"][rows]), 0.0)
                dcol = _row_to_col(w["decay"][cc * CHUNK:cc * CHUNK + 1])
                state[p] = state[p] * jnp.concatenate([dcol, dcol], axis=1) + upd
        for p in range(2):
            s_scr[p] = state[p]
            vl = slice(p * 2 * LANES, (p + 1) * 2 * LANES)
            o2 = jnp.concatenate(outs[p], axis=0)
            oraw_ref[:, vl] = o2
            for hh in range(2):
                oh = o2[:, hh * LANES:(hh + 1) * LANES]
                gl = slice(p * 2 * LANES + hh * LANES, p * 2 * LANES + (hh + 1) * LANES)
                rinv = lax.rsqrt(jnp.mean(oh * oh, axis=1, keepdims=True) + RMS_EPS)
                gg = gg_ref[:, gl]
                o_ref[:, gl] = (oh * rinv * gn * (gg * _sigmoid(gg))).astype(BF16)

    cb = lambda w, idx: pl.BlockSpec((GLA_ROWS, w), lambda i: (i, idx))
    full = lambda shp: pl.BlockSpec(shp, lambda i: tuple(0 for _ in shp))
    return pl.pallas_call(
        body, name="gla_fwd", grid=(nsteps,),
        in_specs=[cb(256, 6), cb(256, 7), cb(512, 4), cb(512, 5), cb(128, 24),
                  full((LANES, GLA_KEYS)), full((1, GLA_KEYS)), full((1, LANES))] + [HBM_SPEC] * n_in,
        out_specs=[pl.BlockSpec((GLA_ROWS, GLA_WIDTH), lambda i: (i, 0)),
                   pl.BlockSpec((GLA_ROWS, GLA_WIDTH), lambda i: (i, 0)),
                   pl.BlockSpec((cps, 2, LANES, 2 * LANES), lambda i: (i, 0, 0, 0))] + [HBM_SPEC] * n_out,
        out_shape=[jax.ShapeDtypeStruct((t, GLA_WIDTH), BF16), jax.ShapeDtypeStruct((t, GLA_WIDTH), F32),
                   jax.ShapeDtypeStruct((t // CHUNK, 2, LANES, 2 * LANES), F32)]
        + rider.out_shapes,
        input_output_aliases={8 + i: 3 + o for i, o in rider.aliases},
        scratch_shapes=[pltpu.VMEM((2, LANES, 2 * LANES), F32)] + [pltpu.SemaphoreType.DMA((s,)) for s in rider.sems],
        compiler_params=_cparams("arbitrary"),
    )(proj, proj, proj, proj, proj, gate_up_pad, gate_bias, gnorm, *rider.inputs)


def _gla_bwd(proj, dcat, oraw, states, gate_up_pad, gate_bias, gnorm, t):
    nsteps = t // GLA_ROWS
    cps = GLA_ROWS // CHUNK
    wout = 2 * GLA_KEYS + 2 * GLA_WIDTH + LANES

    def body(q_ref, k_ref, v_ref, gg_ref, ga_ref, do_ref, oraw_ref, st_ref, gup_ref, gbias_ref, gn_ref,
             d_ref, dgup_ref, dgbias_ref, dgn_ref, ds_scr):
        i = pl.program_id(0)

        @pl.when(i == 0)
        def _():
            ds_scr[...] = jnp.zeros_like(ds_scr)
            dgup_ref[...] = jnp.zeros_like(dgup_ref)
            dgbias_ref[...] = jnp.zeros_like(dgbias_ref)
            dgn_ref[...] = jnp.zeros_like(dgn_ref)

        causal, fwd_stack, bwd_stack, rowid, lane, blockdiag = _gla_consts()
        pre, log_a = _gla_gates(ga_ref, gup_ref, gbias_ref)
        gn = gn_ref[...]
        dgn = jnp.zeros((1, LANES), F32)
        dgs = []
        pre_p = []
        for p in range(2):
            kl = slice(p * LANES, (p + 1) * LANES)
            vl = slice(p * 2 * LANES, (p + 1) * 2 * LANES)
            tm = _gla_terms(log_a[:, kl], q_ref[:, kl], k_ref[:, kl], fwd_stack)
            v2b = v_ref[:, vl].astype(BF16)
            dos = []
            for hh in range(2):
                gl = slice(p * 2 * LANES + hh * LANES, p * 2 * LANES + (hh + 1) * LANES)
                oh = oraw_ref[:, gl]
                rinv = lax.rsqrt(jnp.mean(oh * oh, axis=1, keepdims=True) + RMS_EPS)
                on = oh * rinv
                gg = gg_ref[:, gl]
                sg = _sigmoid(gg)
                sil = gg * sg
                dgo = do_ref[:, gl]
                d_ref[:, 2 * GLA_KEYS + GLA_WIDTH + gl.start:2 * GLA_KEYS + GLA_WIDTH + gl.stop] = (
                    dgo * on * gn * (sg * (1.0 + gg * (1.0 - sg)))).astype(BF16)
                dgn = dgn + jnp.sum(dgo * sil * on, axis=0, keepdims=True)
                don = dgo * sil * gn
                dos.append(rinv * (don - on * jnp.mean(don * on, axis=1, keepdims=True)))
            do2b = jnp.concatenate(dos, axis=1).astype(BF16)
            qib = tm["qi"].astype(BF16)
            kib = tm["ki"].astype(BF16)
            kdb = tm["kd"].astype(BF16)
            qbb = tm["qb"].astype(BF16)
            dqi = jnp.zeros((GLA_ROWS, LANES), F32)
            dki = jnp.zeros((GLA_ROWS, LANES), F32)
            dvs = []
            for hh in range(2):
                hm = (lane // 64) == hh
                hl = slice(hh * LANES, (hh + 1) * LANES)
                a = jnp.where(causal, _dot_nt(jnp.where(hm, tm["qi"], 0.0).astype(BF16), kib), 0.0).astype(BF16)
                da = jnp.where(causal, _dot_nt(do2b[:, hl], v2b[:, hl]), 0.0).astype(BF16)
                dvs.append(_dot_tn(a, do2b[:, hl]))
                dqi = dqi + jnp.where(hm, _dot(da, kib), 0.0)
                dki = dki + jnp.where(hm, _dot_tn(da, qib), 0.0)
            pre_p.append(dict(tm=tm, v2b=v2b, do2b=do2b, kdb=kdb, qbb=qbb, dqi=dqi, dki=dki, dvs=dvs))
        dstate = [ds_scr[0], ds_scr[1]]
        rec = [dict(dv_st=[None] * cps, dqb=[None] * cps, dkd=[None] * cps, dd=[None] * cps) for _ in range(2)]
        for cc in reversed(range(cps)):
            rows = slice(cc * CHUNK, (cc + 1) * CHUNK)
            for p in range(2):
                w, ds2 = pre_p[p], dstate[p]
                s_prev = st_ref[cc, p]
                ds2b = ds2.astype(BF16)
                rec[p]["dv_st"][cc] = _dot(w["kdb"][rows], ds2b)
                rec[p]["dqb"][cc] = _dot_nt(w["do2b"][rows], s_prev.astype(BF16))
                rec[p]["dkd"][cc] = _dot_nt(w["v2b"][rows], ds2b)
                decay_row = w["tm"]["decay"][cc * CHUNK:cc * CHUNK + 1]
                ddecay_col = jnp.sum(ds2 * s_prev, axis=1, keepdims=True)
                ddecay_row = jnp.transpose(jnp.broadcast_to(ddecay_col, (LANES, LANES)))[0:1, :]
                rec[p]["dd"][cc] = jnp.broadcast_to(ddecay_row * decay_row, (CHUNK, LANES))
                dcol = _row_to_col(decay_row)
                dstate[p] = (jnp.where(blockdiag, _dot_tn(w["qbb"][rows], w["do2b"][rows]), 0.0)
                             + ds2 * jnp.concatenate([dcol, dcol], axis=1))
        for p in range(2):
            ds_scr[p] = dstate[p]
            tm, dqi, dki = pre_p[p]["tm"], pre_p[p]["dqi"], pre_p[p]["dki"]
            dv2 = jnp.concatenate(pre_p[p]["dvs"], axis=1) + jnp.concatenate(rec[p]["dv_st"], axis=0)
            dqb = jnp.concatenate(rec[p]["dqb"], axis=0)
            dkd = jnp.concatenate(rec[p]["dkd"], axis=0)
            dd = rec[p]["dd"]
            dqs = dqi * tm["e_q"] + dqb * tm["e_b"]
            dk = dki * tm["e_k"] + dkd * tm["e_d"]
            t_qi = dqi * tm["qi"]
            t_ki = dki * tm["ki"]
            t_kd = dkd * tm["kd"]
            db = t_qi - t_ki + dqb * tm["qb"] - t_kd
            to_mid = t_ki - t_qi
            to_last = t_kd + jnp.where(rowid % CHUNK == CHUNK - 1, jnp.concatenate(dd, axis=0), 0.0)
            dgs.append(_dot3(bwd_stack, jnp.concatenate([db, to_mid, to_last], axis=0)))
            d_ref[:, p * LANES:(p + 1) * LANES] = (dqs * GLA_SCALE).astype(BF16)
            d_ref[:, GLA_KEYS + p * LANES:GLA_KEYS + (p + 1) * LANES] = dk.astype(BF16)
            d_ref[:, 2 * GLA_KEYS + p * 2 * LANES:2 * GLA_KEYS + (p + 1) * 2 * LANES] = dv2.astype(BF16)
        dlog_a = jnp.concatenate(dgs, axis=1)
        dpre = dlog_a * (1.0 / GLA_TAU) * _sigmoid(-pre)
        dpb = dpre.astype(BF16)
        dgn_ref[...] += dgn
        dgbias_ref[...] += jnp.sum(dpre, axis=0, keepdims=True)
        dgup_ref[...] += _dot_tn(ga_ref[...].astype(BF16), dpb)
        d_ref[:, 2 * GLA_KEYS + 2 * GLA_WIDTH:] = _dot_nt(dpb, gup_ref[...].astype(BF16)).astype(BF16)

    rev = lambda i: nsteps - 1 - i
    cb = lambda w, idx: pl.BlockSpec((GLA_ROWS, w), lambda i: (rev(i), idx))
    full = lambda shp: pl.BlockSpec(shp, lambda i: tuple(0 for _ in shp))
    return pl.pallas_call(
        body, name="gla_bwd", grid=(nsteps,),
        in_specs=[cb(256, 6), cb(256, 7), cb(512, 4), cb(512, 5), cb(128, 24), cb(512, 1), cb(512, 0),
                  pl.BlockSpec((cps, 2, LANES, 2 * LANES), lambda i: (rev(i), 0, 0, 0)),
                  full((LANES, GLA_KEYS)), full((1, GLA_KEYS)), full((1, LANES))],
        out_specs=[pl.BlockSpec((GLA_ROWS, wout), lambda i: (rev(i), 0)),
                   full((LANES, GLA_KEYS)), full((1, GLA_KEYS)), full((1, LANES))],
        out_shape=[jax.ShapeDtypeStruct((t, wout), BF16), jax.ShapeDtypeStruct((LANES, GLA_KEYS), F32),
                   jax.ShapeDtypeStruct((1, GLA_KEYS), F32), jax.ShapeDtypeStruct((1, LANES), F32)],
        scratch_shapes=[pltpu.VMEM((2, LANES, 2 * LANES), F32)],
        compiler_params=_cparams("arbitrary"),
    )(proj, proj, proj, proj, proj, dcat, oraw, states, gate_up_pad, gate_bias, gnorm)


def _ln_stats(r):
    mu = jnp.mean(r, axis=1, keepdims=True)
    xc = r - mu
    rstd = lax.rsqrt(jnp.mean(xc * xc, axis=1, keepdims=True) + LN_EPS)
    return xc * rstd, rstd


def _ln_bwd(dy_g, xhat, rstd):
    return rstd * (dy_g - jnp.mean(dy_g, axis=1, keepdims=True) - xhat * jnp.mean(dy_g * xhat, axis=1, keepdims=True))


def _outproj_ln1(sb_o, gla_o, x, w_out, g1, b1, t, tm=256):
    def body(sb_ref, gl_ref, x_ref, w_ref, g_ref, b_ref, xhat_ref, rstd_ref, h_ref):
        for half in range(2):
            rows = slice(half * (tm // 2), (half + 1) * (tm // 2))
            mix = _dot(sb_ref[rows, :], w_ref[0:SB_WIDTH, :]) + _dot(gl_ref[rows, :], w_ref[SB_WIDTH:, :])
            xhat, rstd = _ln_stats(ALPHA * x_ref[rows, :] + mix)
            xhat_ref[rows, :] = xhat
            rstd_ref[rows, :] = rstd
            h_ref[rows, :] = (xhat * g_ref[...] + b_ref[...]).astype(BF16)

    row = lambda w: pl.BlockSpec((tm, w), lambda i: (i, 0))
    full = lambda shp: pl.BlockSpec(shp, lambda i: (0, 0))
    return pl.pallas_call(
        body, name="outproj_ln1", grid=(t // tm,),
        in_specs=[row(SB_WIDTH), row(GLA_WIDTH), row(D_MODEL), full((D_MODEL, D_MODEL)), full((1, D_MODEL)), full((1, D_MODEL))],
        out_specs=[row(D_MODEL), row(1), row(D_MODEL)],
        out_shape=[jax.ShapeDtypeStruct((t, D_MODEL), F32), jax.ShapeDtypeStruct((t, 1), F32),
                   jax.ShapeDtypeStruct((t, D_MODEL), BF16)],
        compiler_params=_cparams("parallel"),
    )(sb_o, gla_o, x, w_out, g1, b1)


_INV_SQRT2 = 1.0 / math.sqrt(2.0)
_INV_SQRT2PI = 1.0 / math.sqrt(2.0 * math.pi)


def _conv3(xs, w_ref, b_ref, half):
    return (w_ref[half, 0:1, :] * pltpu.roll(xs, 2, 0) + w_ref[half, 1:2, :] * pltpu.roll(xs, 1, 0)
            + w_ref[half, 2:3, :] * xs + b_ref[half])


HALO = 16


def _conv_gelu_fwd(up3, conv_w3, conv_b3, t, tr=512, ct=256):
    nct = D_FF // ct
    hb = tr // HALO

    def body(cur_ref, prev_ref, w_ref, b_ref, gm_ref):
        i = pl.program_id(1)
        keep = (i > 0).astype(F32)
        us = []
        for half in range(2):
            xs = jnp.concatenate([prev_ref[half].astype(F32) * keep, cur_ref[half].astype(F32)], axis=0)
            us.append(_conv3(xs, w_ref, b_ref, half)[HALO:, :])
        a, c = us
        gelu = 0.5 * a * (1.0 + lax.erf(a * _INV_SQRT2))
        gm_ref[...] = (gelu * c).astype(BF16)

    return pl.pallas_call(
        body, name="conv_gelu_fwd", grid=(nct, t // tr),
        in_specs=[pl.BlockSpec((2, tr, ct), lambda j, i: (0, i, j)),
                  pl.BlockSpec((2, HALO, ct), lambda j, i: (0, jnp.maximum(i * hb - 1, 0), j)),
                  pl.BlockSpec((2, 3, ct), lambda j, i: (0, 0, j)),
                  pl.BlockSpec((2, 1, ct), lambda j, i: (0, 0, j))],
        out_specs=pl.BlockSpec((tr, ct), lambda j, i: (i, j)),
        out_shape=jax.ShapeDtypeStruct((t, D_FF), BF16),
        compiler_params=_cparams("parallel", "parallel"),
    )(up3, up3, conv_w3, conv_b3)


def _conv_gelu_bwd(up3, dgm, conv_w3, conv_b3, t, tr=512, ct=256):
    nct = D_FF // ct
    nrt = t // tr
    hb = tr // HALO
    n = tr + 2 * HALO
    lo, hi = HALO, tr + HALO

    def body(cur_ref, prev_ref, next_ref, dg_ref, dgn_ref, w_ref, b_ref, dup_ref, dcw_ref, dcb_ref):
        i = pl.program_id(1)

        @pl.when(i == 0)
        def _():
            dcw_ref[...] = jnp.zeros_like(dcw_ref)
            dcb_ref[...] = jnp.zeros_like(dcb_ref)

        keep_prev = (i > 0).astype(F32)
        keep_next = (i < nrt - 1).astype(F32)
        xs, xm1, xm2, us = [], [], [], []
        for half in range(2):
            x = jnp.concatenate([prev_ref[half].astype(F32) * keep_prev, cur_ref[half].astype(F32),
                                 next_ref[half].astype(F32)], axis=0)
            xs.append(x)
            xm1.append(pltpu.roll(x, 1, 0))
            xm2.append(pltpu.roll(x, 2, 0))
            us.append(w_ref[half, 0:1, :] * xm2[half] + w_ref[half, 1:2, :] * xm1[half]
                      + w_ref[half, 2:3, :] * x + b_ref[half])
        a, c = us
        dg = jnp.concatenate([jnp.zeros((HALO, ct), F32), dg_ref[...].astype(F32),
                              dgn_ref[...].astype(F32) * keep_next], axis=0)
        cdf = 0.5 * (1.0 + lax.erf(a * _INV_SQRT2))
        pdf = jnp.exp(-0.5 * a * a) * _INV_SQRT2PI
        dus = [dg * c * (cdf + a * pdf), dg * (a * cdf)]
        rid = lax.broadcasted_iota(jnp.int32, (8, 1), 0)
        for half in range(2):
            du = dus[half]
            dup = (w_ref[half, 2:3, :] * du + w_ref[half, 1:2, :] * pltpu.roll(du, n - 1, 0)
                   + w_ref[half, 0:1, :] * pltpu.roll(du, n - 2, 0))
            dup_ref[half] = dup[lo:hi, :].astype(BF16)
            duc = du[lo:hi, :]
            s0 = jnp.sum(duc * xm2[half][lo:hi, :], axis=0, keepdims=True)
            s1 = jnp.sum(duc * xm1[half][lo:hi, :], axis=0, keepdims=True)
            s2 = jnp.sum(duc * xs[half][lo:hi, :], axis=0, keepdims=True)
            dcw_ref[half] += jnp.where(rid == 0, s0, jnp.where(rid == 1, s1, jnp.where(rid == 2, s2, 0.0)))
            dcb_ref[half] += jnp.sum(duc, axis=0, keepdims=True)

    last = t // HALO - 1
    return pl.pallas_call(
        body, name="conv_gelu_bwd", grid=(nct, nrt),
        in_specs=[pl.BlockSpec((2, tr, ct), lambda j, i: (0, i, j)),
                  pl.BlockSpec((2, HALO, ct), lambda j, i: (0, jnp.maximum(i * hb - 1, 0), j)),
                  pl.BlockSpec((2, HALO, ct), lambda j, i: (0, jnp.minimum((i + 1) * hb, last), j)),
                  pl.BlockSpec((tr, ct), lambda j, i: (i, j)),
                  pl.BlockSpec((HALO, ct), lambda j, i: (jnp.minimum((i + 1) * hb, last), j)),
                  pl.BlockSpec((2, 3, ct), lambda j, i: (0, 0, j)),
                  pl.BlockSpec((2, 1, ct), lambda j, i: (0, 0, j))],
        out_specs=[pl.BlockSpec((2, tr, ct), lambda j, i: (0, i, j)),
                   pl.BlockSpec((2, 8, ct), lambda j, i: (0, 0, j)),
                   pl.BlockSpec((2, 1, ct), lambda j, i: (0, 0, j))],
        out_shape=[jax.ShapeDtypeStruct((2, t, D_FF), BF16), jax.ShapeDtypeStruct((2, 8, D_FF), F32),
                   jax.ShapeDtypeStruct((2, 1, D_FF), F32)],
        compiler_params=_cparams("parallel", "arbitrary"),
    )(up3, up3, up3, dgm, dgm, conv_w3, conv_b3)


def _down_ln2_loss(gm, w_down, xhat1, g1, b1, g2, b2, target, t, tm=256):
    def body(gm_ref, w_ref, xh_ref, g1_ref, b1_ref, g2_ref, b2_ref, tg_ref, dr_ref, drb_ref, loss_ref, dg_ref, db_ref):
        i = pl.program_id(0)

        @pl.when(i == 0)
        def _():
            loss_ref[...] = jnp.zeros_like(loss_ref)
            dg_ref[...] = jnp.zeros_like(dg_ref)
            db_ref[...] = jnp.zeros_like(db_ref)

        loss = jnp.zeros((1, 1), F32)
        dg = jnp.zeros((1, D_MODEL), F32)
        db = jnp.zeros((1, D_MODEL), F32)
        for half in range(2):
            rows = slice(half * (tm // 2), (half + 1) * (tm // 2))
            h = xh_ref[rows, :] * g1_ref[...] + b1_ref[...]
            xhat, rstd = _ln_stats(ALPHA * h + _dot(gm_ref[rows, :], w_ref[...]))
            err = xhat * g2_ref[...] + b2_ref[...] - tg_ref[rows, :]
            loss = loss + 0.5 * jnp.sum(jnp.sum(err * err, axis=1, keepdims=True), axis=0, keepdims=True) / D_MODEL
            dy = err * (1.0 / D_MODEL)
            dg = dg + jnp.sum(dy * xhat, axis=0, keepdims=True)
            db = db + jnp.sum(dy, axis=0, keepdims=True)
            dr = _ln_bwd(dy * g2_ref[...], xhat, rstd)
            dr_ref[rows, :] = dr
            drb_ref[rows, :] = dr.astype(BF16)
        loss_ref[...] += loss
        dg_ref[...] += dg
        db_ref[...] += db

    row = lambda w: pl.BlockSpec((tm, w), lambda i: (i, 0))
    full = lambda shp: pl.BlockSpec(shp, lambda i: (0, 0))
    vec = full((1, D_MODEL))
    return pl.pallas_call(
        body, name="down_ln2_loss", grid=(t // tm,),
        in_specs=[row(D_FF), full((D_FF, D_MODEL)), row(D_MODEL), vec, vec, vec, vec, row(D_MODEL)],
        out_specs=[row(D_MODEL), row(D_MODEL), full((1, 1)), vec, vec],
        out_shape=[jax.ShapeDtypeStruct((t, D_MODEL), F32), jax.ShapeDtypeStruct((t, D_MODEL), BF16),
                   jax.ShapeDtypeStruct((1, 1), F32),
                   jax.ShapeDtypeStruct((1, D_MODEL), F32), jax.ShapeDtypeStruct((1, D_MODEL), F32)],
        compiler_params=_cparams("arbitrary"),
    )(gm, w_down, xhat1, g1, b1, g2, b2, target)


def _dh_ln1_bwd(dup3, w_up4, dr2, xhat1, rstd1, g1, t, tm=256):
    ws = 2 * D_FF // 4

    def body(a_ref, w_ref, dr2_ref, xh_ref, rs_ref, g_ref, dr1_ref, dg_ref, db_ref):
        i = pl.program_id(0)

        @pl.when(i == 0)
        def _():
            dg_ref[...] = jnp.zeros_like(dg_ref)
            db_ref[...] = jnp.zeros_like(db_ref)

        dg = jnp.zeros((1, D_MODEL), F32)
        db = jnp.zeros((1, D_MODEL), F32)
        for half in range(2):
            rows = slice(half * (tm // 2), (half + 1) * (tm // 2))
            dh = ALPHA * dr2_ref[rows, :]
            for s in range(4):
                dh = dh + _dot_nt(a_ref[s // 2, rows, (s % 2) * ws:(s % 2 + 1) * ws], w_ref[s])
            xhat = xh_ref[rows, :]
            dg = dg + jnp.sum(dh * xhat, axis=0, keepdims=True)
            db = db + jnp.sum(dh, axis=0, keepdims=True)
            dr1_ref[rows, :] = _ln_bwd(dh * g_ref[...], xhat, rs_ref[rows, :])
        dg_ref[...] += dg
        db_ref[...] += db

    row = lambda w: pl.BlockSpec((tm, w), lambda i: (i, 0))
    vec = pl.BlockSpec((1, D_MODEL), lambda i: (0, 0))
    return pl.pallas_call(
        body, name="dh_ln1_bwd", grid=(t // tm,),
        in_specs=[pl.BlockSpec((2, tm, D_FF), lambda i: (0, i, 0)),
                  pl.BlockSpec((4, D_MODEL, ws), lambda i: (0, 0, 0)),
                  row(D_MODEL), row(D_MODEL), row(1), vec],
        out_specs=[row(D_MODEL), vec, vec],
        out_shape=[jax.ShapeDtypeStruct((t, D_MODEL), F32), jax.ShapeDtypeStruct((1, D_MODEL), F32),
                   jax.ShapeDtypeStruct((1, D_MODEL), F32)],
        compiler_params=_cparams("arbitrary"),
    )(dup3, w_up4, dr2, xhat1, rstd1, g1)


def _adamw(w, g, m, v, name):
    rows, cols = w.shape
    tr = rows
    for cand in (256, 128, 64, 32, 16, 8):
        if rows % cand == 0 and rows > cand:
            tr = cand
            break
    c1 = 1.0 / (1.0 - ADAM_B1 ** ADAM_STEP)
    c2 = 1.0 / (1.0 - ADAM_B2 ** ADAM_STEP)

    def body(w_ref, g_ref, m_ref, v_ref, d_ref, nm_ref, nv_ref):
        gv = g_ref[...]
        nm = ADAM_B1 * m_ref[...] + (1.0 - ADAM_B1) * gv
        nv = ADAM_B2 * v_ref[...] + (1.0 - ADAM_B2) * (gv * gv)
        d_ref[...] = -ADAM_LR * ((nm * c1) / (jnp.sqrt(nv * c2) + ADAM_EPS) + ADAM_WD * w_ref[...])
        nm_ref[...] = nm
        nv_ref[...] = nv

    spec = pl.BlockSpec((tr, cols), lambda i: (i, 0))
    out = jax.ShapeDtypeStruct((rows, cols), F32)
    return pl.pallas_call(
        body, name=name, grid=(rows // tr,), in_specs=[spec] * 4, out_specs=[spec] * 3, out_shape=[out] * 3,
        compiler_params=_cparams("parallel"),
    )(w, g, m, v)


def _local_step(x, target, w_in_p, late_shards, gate_up_pad, gate_bias, gnorm, ln1_g, ln1_b, conv_w3, conv_b3,
                ln2_g, ln2_b, c_arr, kc_arr):
    t = x.shape[0]
    tq = min(t, 1024)
    s_up, s_out, s_down = late_shards
    sh_up, sh_out, sh_down = LATE_SHAPES
    proj, out_partly = _mm(x, w_in_p, m=t, n=IN_PAD, k=D_MODEL, tm=512, tn=IN_PAD, tk=D_MODEL, name="proj",
                           rider=_gather_rider([s_out], [sh_out]))
    sb_o, sb_mass, up_partly = _sb_fwd(proj, t, _gather_rider([s_up], [sh_up]))
    gla_o, oraw, states, w_up4, w_out4, down_partly = _gla_fwd(
        proj, gate_up_pad, gate_bias, gnorm, t,
        _join(_forward_rider([up_partly, out_partly], [sh_up, sh_out]), _gather_rider([s_down], [sh_down])))
    w_out = w_out4.reshape(D_MODEL, D_MODEL)
    xhat1, rstd1, h_bf = _outproj_ln1(sb_o, gla_o, x, w_out, ln1_g, ln1_b, t)
    up3, w_down4 = _mm(h_bf, w_up4, m=t, n=2 * D_FF, k=D_MODEL, tm=tq, tn=W_UP_S, tk=D_MODEL, name="up",
                       b_spec=pl.BlockSpec((None, D_MODEL, W_UP_S), lambda i, j, kk: (j, 0, 0)),
                       o_spec=pl.BlockSpec((None, tq, W_UP_S), lambda i, j, kk: (j // 2, i, j % 2)),
                       out_shape=jax.ShapeDtypeStruct((2, t, D_FF), BF16), out_dtype=BF16,
                       rider=_forward_rider([down_partly], [sh_down]))
    w_down = w_down4.reshape(D_FF, D_MODEL)
    gm = _conv_gelu_fwd(up3, conv_w3, conv_b3, t, tr=tq)
    dr2, dr2b, loss, dln2_g, dln2_b = _down_ln2_loss(gm, w_down, xhat1, ln1_g, ln1_b, ln2_g, ln2_b, target, t)
    dgm = _mm(dr2b, w_down, m=t, n=D_FF, k=D_MODEL, tm=tq, tn=W_UP_S, tk=D_MODEL, tb=True, out_dtype=BF16, name="dgm")
    dw_down = _mm(gm, dr2b, m=D_FF, n=D_MODEL, k=t, tm=W_UP_S, tn=D_MODEL, tk=tq, ta=True, name="dw_down")
    dup3, dcw, dcb = _conv_gelu_bwd(up3, dgm, conv_w3, conv_b3, t, tr=tq)
    dr1, dln1_g, dln1_b = _dh_ln1_bwd(dup3, w_up4, dr2, xhat1, rstd1, ln1_g, t)
    dw_up4 = _mm(h_bf, dup3, m=D_MODEL, n=2 * D_FF, k=t, tm=512, tn=W_UP_S, tk=t, ta=True, name="dw_up",
                 b_spec=pl.BlockSpec((None, t, W_UP_S), lambda i, j, kk: (j // 2, kk, j % 2)),
                 o_spec=pl.BlockSpec((None, 512, W_UP_S), lambda i, j, kk: (j, i, 0)),
                 out_shape=jax.ShapeDtypeStruct((4, D_MODEL, W_UP_S), F32))
    dw_out = _dw_out(sb_o, gla_o, dr1, t, tq)
    gs = [dw_up4, dw_out.reshape(4, W_OUT_S, D_MODEL), dw_down.reshape(4, W_DOWN_S, D_MODEL)]
    dcat, *from_sib = _mm(dr1, w_out, m=t, n=D_MODEL, k=D_MODEL, tm=tq, tn=512, tk=D_MODEL, tb=True, name="dcat",
                          rider=_sibling_rider(gs, LATE_SHAPES))
    ps = [_add_sibling(gs[m], from_sib[m], c_arr, LATE_ADD_ROWS[m], "add_sibling_late_%d" % m) for m in range(3)]
    dgla, dgup_pad, dgbias, dgnorm = _gla_bwd(proj, dcat, oraw, states, gate_up_pad, gate_bias, gnorm, t)
    small = dict(
        gate_up=dgup_pad[:GATE_RANK], gate_bias=dgbias, gla_norm_g=dgnorm, ln1_g=dln1_g, ln1_b=dln1_b,
        conv_w=jnp.concatenate([dcw[0, :3], dcw[1, :3]], axis=1), conv_b=jnp.concatenate([dcb[0], dcb[1]], axis=1),
        ln2_g=dln2_g, ln2_b=dln2_b, loss=loss)
    dsq, dsk, dsv, *others, vecs = _sb_bwd(proj, dcat, sb_mass, t,
                                           _chips_rider(ps, _pack_vec(small, SMALL_GRADS, GRAD_VEC_ROWS)))
    late_sums = [_add_chips(ps[m], others[m], kc_arr, LATE_ADD_ROWS[m], "add_chips_late_%d" % m) for m in range(3)]
    dproj = [dsq, dsk, dsv, dgla]
    dw_in_p = _dw_in(x, dproj, t, tk=tq)
    g_in = dw_in_p[None]
    from_sib_in, = _run(_sibling_rider([g_in], [(D_MODEL, IN_PAD)]), "exchange_sibling_w_in")
    half_in = _add_sibling(g_in, from_sib_in, c_arr, ADD_ROWS[0], "add_sibling_w_in")[0]
    p_in = jnp.stack([half_in[:, k * W_IN_S:(k + 1) * W_IN_S] for k in range(4)], axis=0)
    dx, others_in = _dx(dproj, w_in_p, dr1, t, _chips_rider([p_in]))
    sum_in = _add_chips(p_in, others_in, kc_arr, ADD_ROWS[0], "add_chips_w_in")
    return dx, [sum_in] + late_sums, vecs


W_IN_S, W_UP_S, W_OUT_S, W_DOWN_S = IN_WIDTH // 4, 2 * D_FF // 4, D_MODEL // 4, D_FF // 4
SHARD_SHAPES = ((D_MODEL, W_IN_S), (D_MODEL, W_UP_S), (W_OUT_S, D_MODEL), (W_DOWN_S, D_MODEL))
ADD_ROWS = (256, 256, 128, 176)
LATE_SHAPES, LATE_ADD_ROWS = SHARD_SHAPES[1:], ADD_ROWS[1:]
SMALL_ROWS = 8
VEC_SIZES = (("gate_bias", GLA_KEYS), ("gla_norm_g", LANES), ("ln1_g", D_MODEL), ("ln1_b", D_MODEL),
             ("conv_b", 2 * D_FF), ("ln2_g", D_MODEL), ("ln2_b", D_MODEL))
SMALL_GRADS = VEC_SIZES + (("conv_w", 3 * 2 * D_FF), ("gate_up", GATE_RANK * GLA_KEYS), ("loss", 1))


def _rows(a):
    flat = a.reshape(-1)
    pad = (-flat.shape[0]) % D_MODEL
    if pad:
        flat = jnp.pad(flat, (0, pad))
    return flat.reshape(-1, D_MODEL)


def _pad_rows(a, rows):
    return jnp.pad(a, ((0, rows - a.shape[0]), (0, 0)))


def _pack_vec(d, sizes, rows):
    flat = jnp.concatenate([d[n].reshape(-1) for n, _ in sizes])
    return _pad_rows(_rows(flat), rows)


def _unpack_vec(v, sizes):
    flat = v.reshape(-1)
    out, o = {}, 0
    for n, size in sizes:
        out[n] = flat[o:o + size].reshape(1, size)
        o += size
    return out


VEC_ROWS = 16
GRAD_VEC_ROWS = 32


HBM_SPEC = pl.BlockSpec(memory_space=pltpu.HBM)


def _position():
    x, y, c = lax.axis_index("x"), lax.axis_index("y"), lax.axis_index("c")
    chips = [(1 - x, y), (x, 1 - y), (1 - x, 1 - y)]
    return x, y, c, chips


def _remote(src, dst, send_sems, recv_sems, k, to):
    return pltpu.make_async_remote_copy(src_ref=src, dst_ref=dst, send_sem=send_sems.at[k], recv_sem=recv_sems.at[k],
                                        device_id=to, device_id_type=MESH)


def _gather_ici(in_refs, out_refs, shapes, send_sems, recv_sems, local_sems):
    x, y, c, chips = _position()
    k_me = 2 * x + y
    local, sends, recvs = [], [], []
    for m, (rows, _) in enumerate(shapes):
        h = rows // 2
        local.append(pltpu.make_async_copy(in_refs[m], out_refs[m].at[k_me], local_sems.at[m]))
        for j, (cx, cy) in enumerate(chips):
            sends.append(_remote(in_refs[m].at[pl.ds(c * h, h), :], out_refs[m].at[k_me, pl.ds(c * h, h), :],
                                 send_sems, recv_sems, 3 * m + j, (cx, cy, c)))
            landed = out_refs[m].at[2 * cx + cy, pl.ds(c * h, h), :]
            recvs.append(_remote(landed, landed, send_sems, recv_sems, 3 * m + j, (x, y, c)))
    return local, sends, recvs


def _gather_d2d(src_refs, dst_refs, shapes, send_sems, recv_sems, base):
    x, y, c, chips = _position()
    sends, recvs = [], []
    for m, (rows, _) in enumerate(shapes):
        h = rows // 2
        for j, (cx, cy) in enumerate(chips):
            k = 2 * cx + cy
            sends.append(_remote(src_refs[m].at[k, pl.ds(c * h, h), :], dst_refs[m].at[k, pl.ds(c * h, h), :],
                                 send_sems, recv_sems, base + 3 * m + j, (x, y, 1 - c)))
            landed = dst_refs[m].at[k, pl.ds((1 - c) * h, h), :]
            recvs.append(_remote(landed, landed, send_sems, recv_sems, base + 3 * m + j, (x, y, c)))
    return sends, recvs


def _gather_weights(shards, small, shapes):
    nm = len(shards)
    n_ici = 3 * nm

    def body(*refs):
        in_refs, small_ref = refs[:nm], refs[nm]
        out_refs, osm_ref = refs[nm + 1:2 * nm + 1], refs[2 * nm + 1]
        send_sems, recv_sems, local_sems = refs[2 * nm + 2:]
        x, y, c, chips = _position()
        k_me = 2 * x + y
        local, sends, recvs = _gather_ici(in_refs, out_refs, shapes, send_sems, recv_sems, local_sems)
        local.append(pltpu.make_async_copy(small_ref, osm_ref.at[k_me], local_sems.at[nm]))
        for j, (cx, cy) in enumerate(chips):
            sends.append(_remote(small_ref, osm_ref.at[k_me], send_sems, recv_sems, n_ici + j, (cx, cy, c)))
        for cp in local + sends:
            cp.start()
        fsends, frecvs = _gather_d2d(out_refs, out_refs, shapes, send_sems, recv_sems, n_ici + 3)
        for landed, onward in zip(recvs, fsends):
            landed.wait_recv()
            onward.start()
        for j, (cx, cy) in enumerate(chips):
            k = 2 * cx + cy
            frecvs.append(_remote(osm_ref.at[k], osm_ref.at[k], send_sems, recv_sems, n_ici + j, (x, y, c)))
        for cp in frecvs:
            cp.wait_recv()
        for cp in sends + fsends:
            cp.wait_send()
        for cp in local:
            cp.wait()

    n_sems = 2 * n_ici + 3
    return pl.pallas_call(
        body, name="gather_weights", in_specs=[HBM_SPEC] * (nm + 1), out_specs=[HBM_SPEC] * (nm + 1),
        out_shape=[jax.ShapeDtypeStruct((4,) + s, BF16) for s in shapes]
        + [jax.ShapeDtypeStruct((4, SMALL_ROWS, D_MODEL), F32)],
        scratch_shapes=[pltpu.SemaphoreType.DMA((n_sems,)), pltpu.SemaphoreType.DMA((n_sems,)),
                        pltpu.SemaphoreType.DMA((nm + 1,))],
    )(*shards, small)


def _gather_rider(shards, shapes):
    n = len(shards)
    return _Rider(shards, [jax.ShapeDtypeStruct((4,) + s, BF16) for s in shapes], (3 * n, 3 * n, n),
                  lambda ins, outs, sems: _gather_ici(ins, outs, shapes, *sems))


def _forward_rider(gathered, shapes):
    n = len(gathered)
    return _Rider(gathered, [jax.ShapeDtypeStruct(a.shape, a.dtype) for a in gathered], (3 * n, 3 * n),
                  lambda ins, outs, sems: ([],) + _gather_d2d(ins, outs, shapes, sems[0], sems[1], 0),
                  aliases=[(m, m) for m in range(n)])


def _sibling_rider(gs, shapes):
    def copies(ins, outs, sems):
        x, y, c, _ = _position()
        both = []
        for m, (rows, _) in enumerate(shapes):
            h = rows // 2
            for k in range(gs[m].shape[0]):
                both.append(_remote(ins[m].at[k, pl.ds((1 - c) * h, h), :], outs[m].at[k], sems[0], sems[1],
                                    4 * m + k, (x, y, 1 - c)))
        return [], both, both

    return _Rider(gs, [jax.ShapeDtypeStruct((g.shape[0], r // 2, cl), F32) for g, (r, cl) in zip(gs, shapes)],
                  (4 * len(gs), 4 * len(gs)), copies)


def _add_sibling(g, r, c_arr, tr, name):
    nblk, rows, cols = g.shape
    nb = rows // 2 // tr

    def body(c_ref, g_ref, r_ref, o_ref):
        o_ref[...] = (g_ref[...] + r_ref[...]).astype(BF16)

    spec = pl.BlockSpec((None, tr, cols), lambda k, i, c: (k, i, 0))
    return pl.pallas_call(
        body, name=name,
        grid_spec=pltpu.PrefetchScalarGridSpec(
            num_scalar_prefetch=1, grid=(nblk, nb),
            in_specs=[pl.BlockSpec((None, tr, cols), lambda k, i, c: (k, c[0] * nb + i, 0)), spec], out_specs=spec),
        out_shape=jax.ShapeDtypeStruct((nblk, rows // 2, cols), BF16), compiler_params=_cparams("parallel", "parallel"),
    )(c_arr, g, r)


def _reduce_ici(p_refs, r_refs, send_sems, recv_sems):
    x, y, c, chips = _position()
    sends, recvs = [], []
    for m in range(len(p_refs)):
        for j, (cx, cy) in enumerate(chips):
            sends.append(_remote(p_refs[m].at[2 * cx + cy], r_refs[m].at[j], send_sems, recv_sems, 3 * m + j, (cx, cy, c)))
            recvs.append(_remote(r_refs[m].at[j], r_refs[m].at[j], send_sems, recv_sems, 3 * m + j, (x, y, c)))
    return sends, recvs


def _chips_rider(ps, vec=None):
    nm = len(ps)
    n_ici = 3 * nm

    def copies(ins, outs, sems):
        sends, recvs = _reduce_ici(ins[:nm], outs[:nm], sems[0], sems[1])
        if vec is None:
            return [], sends, recvs
        x, y, c, _ = _position()
        my_id = 4 * x + 2 * y + c
        vec_ref, vrecv_ref = ins[nm], outs[nm]
        local = [pltpu.make_async_copy(vec_ref, vrecv_ref.at[my_id], sems[2].at[0])]
        for r in range(1, 8):
            peer = (1 - x if r & 4 else x, 1 - y if r & 2 else y, 1 - c if r & 1 else c)
            sends.append(_remote(vec_ref, vrecv_ref.at[my_id], sems[0], sems[1], n_ici + r - 1, peer))
            recvs.append(_remote(vec_ref, vrecv_ref.at[0], sems[0], sems[1], n_ici + r - 1, (x, y, c)))
        return local, sends, recvs

    out_shapes = [jax.ShapeDtypeStruct((3,) + p.shape[1:], p.dtype) for p in ps]
    if vec is None:
        return _Rider(ps, out_shapes, (n_ici, n_ici), copies)
    return _Rider(list(ps) + [vec], out_shapes + [jax.ShapeDtypeStruct((8, GRAD_VEC_ROWS, D_MODEL), F32)],
                  (n_ici + 7, n_ici + 7, 1), copies)


def _add_chips(p, r, kc_arr, tr, name):
    _, h, cols = p.shape
    nb = h // tr

    def body(kc_ref, p_ref, r0_ref, r1_ref, r2_ref, o_ref):
        o_ref[...] = ((p_ref[...].astype(F32) + r0_ref[...].astype(F32)) + r1_ref[...].astype(F32)) + r2_ref[...].astype(F32)

    rspec = lambda j: pl.BlockSpec((None, tr, cols), lambda i, kc: (j, i, 0))
    return pl.pallas_call(
        body, name=name,
        grid_spec=pltpu.PrefetchScalarGridSpec(
            num_scalar_prefetch=1, grid=(nb,),
            in_specs=[pl.BlockSpec((None, tr, cols), lambda i, kc: (kc[0], i, 0)), rspec(0), rspec(1), rspec(2)],
            out_specs=pl.BlockSpec((tr, cols), lambda i, kc: (kc[1] * nb + i, 0))),
        out_shape=jax.ShapeDtypeStruct((2 * h, cols), F32), compiler_params=_cparams("parallel"),
    )(kc_arr, p, r, r, r)


def _reunite_sibling(fs, shapes):
    n_chunks = 2
    nm = len(fs)

    def body(*refs):
        in_refs, out_refs = refs[:nm], refs[nm:2 * nm]
        send_sems, recv_sems = refs[2 * nm:]
        x, y, c, _ = _position()
        sends, recvs = [], []
        for m in range(nm):
            ch = shapes[m][0] // 2 // n_chunks
            for q in range(n_chunks):
                mine = pl.ds((c * n_chunks + q) * ch, ch)
                theirs = pl.ds(((1 - c) * n_chunks + q) * ch, ch)
                s = n_chunks * m + q
                sends.append(_remote(in_refs[m].at[mine, :], out_refs[m].at[mine, :], send_sems, recv_sems, s, (x, y, 1 - c)))
                recvs.append(_remote(in_refs[m].at[theirs, :], out_refs[m].at[theirs, :], send_sems, recv_sems, s, (x, y, c)))
        for cp in sends:
            cp.start()
        for cp in recvs:
            cp.wait_recv()
        for cp in sends:
            cp.wait_send()

    n_sems = n_chunks * nm
    return pl.pallas_call(
        body, name="reunite_sibling", in_specs=[HBM_SPEC] * nm, out_specs=[HBM_SPEC] * nm,
        out_shape=[jax.ShapeDtypeStruct(s, F32) for s in shapes],
        input_output_aliases={m: m for m in range(nm)},
        scratch_shapes=[pltpu.SemaphoreType.DMA((n_sems,)), pltpu.SemaphoreType.DMA((n_sems,))],
    )(*fs)


def _sum_vecs(v):
    def body(v_ref, o_ref):
        acc = v_ref[0]
        for d in range(1, 8):
            acc = acc + v_ref[d]
        o_ref[...] = acc

    return pl.pallas_call(body, name="sum_vecs", out_shape=jax.ShapeDtypeStruct(v.shape[1:], F32))(v)


def kernel(x, w_in, gate_up, gate_bias, gla_norm_g, w_out, ln1_g, ln1_b, w_up, conv_w, conv_b, w_down, ln2_g, ln2_b, loss_target, m_w_in, m_gate_up, m_gate_bias, m_gla_norm_g, m_w_out, m_ln1_g, m_ln1_b, m_w_up, m_conv_w, m_conv_b, m_w_down, m_ln2_g, m_ln2_b, v_w_in, v_gate_up, v_gate_bias, v_gla_norm_g, v_w_out, v_ln1_g, v_ln1_b, v_w_up, v_conv_w, v_conv_b, v_w_down, v_ln2_g, v_ln2_b):
    xi, yi, ci = lax.axis_index("x"), lax.axis_index("y"), lax.axis_index("c")
    k_me = 2 * xi + yi
    c_arr = ci.astype(jnp.int32).reshape(1)
    kc_arr = jnp.stack([k_me, ci]).astype(jnp.int32)
    small = _pad_rows(jnp.concatenate([_rows(conv_w[0]), _rows(gate_up[0])], axis=0), SMALL_ROWS)
    w_in4, gsmall = _gather_weights([w_in[0].astype(BF16)], small, SHARD_SHAPES[:1])
    late_shards = [w_up[0].astype(BF16), w_out[0].astype(BF16), w_down[0].astype(BF16)]
    w_in_p = jnp.pad(jnp.concatenate([w_in4[k] for k in range(4)], axis=1), ((0, 0), (0, IN_PAD - IN_WIDTH)))
    conv_w_f = jnp.concatenate([gsmall[k, :5].reshape(-1)[:3 * W_UP_S].reshape(3, W_UP_S) for k in range(4)], axis=1)
    gate_up_f = jnp.concatenate([gsmall[k, 5].reshape(GATE_RANK, GLA_KEYS // 4) for k in range(4)], axis=1)
    conv_w3 = jnp.transpose(conv_w_f.reshape(3, 2, D_FF), (1, 0, 2))
    conv_b3 = conv_b.reshape(2, 1, D_FF)
    gate_up_pad = jnp.pad(gate_up_f, ((0, LANES - GATE_RANK), (0, 0)))

    dx, sums, vecs = _local_step(
        x[0], loss_target[0], w_in_p, late_shards, gate_up_pad, gate_bias, gla_norm_g, ln1_g, ln1_b, conv_w3, conv_b3,
        ln2_g, ln2_b, c_arr, kc_arr)
    g_w_in, g_w_up, g_w_out, g_w_down = _reunite_sibling(sums, SHARD_SHAPES)
    gsmall_sum = _unpack_vec(_sum_vecs(vecs), SMALL_GRADS)
    g_conv_w = lax.dynamic_slice_in_dim(gsmall_sum["conv_w"].reshape(3, 2 * D_FF), k_me * W_UP_S, W_UP_S, axis=1)
    g_gate_up = lax.dynamic_slice_in_dim(gsmall_sum["gate_up"].reshape(GATE_RANK, GLA_KEYS), k_me * (GLA_KEYS // 4),
                                         GLA_KEYS // 4, axis=1)
    gv = gsmall_sum
    loss = gv["loss"][0, 0]
    gvec = _pack_vec(gv, VEC_SIZES, VEC_ROWS)

    grads = dict(w_in=g_w_in[None], gate_up=g_gate_up[None], gate_bias=gv["gate_bias"], gla_norm_g=gv["gla_norm_g"],
                 w_out=g_w_out[None], ln1_g=gv["ln1_g"], ln1_b=gv["ln1_b"], w_up=g_w_up[None], conv_w=g_conv_w[None],
                 conv_b=gv["conv_b"], w_down=g_w_down[None], ln2_g=gv["ln2_g"], ln2_b=gv["ln2_b"])
    weights = dict(w_in=w_in, gate_up=gate_up, gate_bias=gate_bias, gla_norm_g=gla_norm_g, w_out=w_out, ln1_g=ln1_g,
                   ln1_b=ln1_b, w_up=w_up, conv_w=conv_w, conv_b=conv_b, w_down=w_down, ln2_g=ln2_g, ln2_b=ln2_b)
    ms = dict(w_in=m_w_in, gate_up=m_gate_up, gate_bias=m_gate_bias, gla_norm_g=m_gla_norm_g, w_out=m_w_out, ln1_g=m_ln1_g,
              ln1_b=m_ln1_b, w_up=m_w_up, conv_w=m_conv_w, conv_b=m_conv_b, w_down=m_w_down, ln2_g=m_ln2_g, ln2_b=m_ln2_b)
    vs = dict(w_in=v_w_in, gate_up=v_gate_up, gate_bias=v_gate_bias, gla_norm_g=v_gla_norm_g, w_out=v_w_out, ln1_g=v_ln1_g,
              ln1_b=v_ln1_b, w_up=v_w_up, conv_w=v_conv_w, conv_b=v_conv_b, w_down=v_w_down, ln2_g=v_ln2_g, ln2_b=v_ln2_b)
    names = ["w_in", "gate_up", "gate_bias", "gla_norm_g", "w_out", "ln1_g", "ln1_b", "w_up", "conv_w", "conv_b", "w_down",
             "ln2_g", "ln2_b"]
    delta, new_m, new_v = {}, {}, {}
    for n in ("w_in", "gate_up", "w_out", "w_up", "conv_w", "w_down"):
        tr_ = jnp.transpose if n == "w_in" else (lambda a: a)
        d, nm, nv = _adamw(tr_(weights[n][0]), tr_(grads[n][0]), tr_(ms[n][0]), tr_(vs[n][0]), "adamw_" + n)
        delta[n], new_m[n], new_v[n] = tr_(d)[None], tr_(nm)[None], tr_(nv)[None]
    d, nm, nv = _adamw(_pack_vec(weights, VEC_SIZES, VEC_ROWS), gvec, _pack_vec(ms, VEC_SIZES, VEC_ROWS),
                       _pack_vec(vs, VEC_SIZES, VEC_ROWS), "adamw_vectors")
    for dst, src in ((delta, d), (new_m, nm), (new_v, nv)):
        dst.update(_unpack_vec(src, VEC_SIZES))
    return (loss, dx[None], *[grads[n] for n in names], *[delta[n] for n in names], *[new_m[n] for n in names],
            *[new_v[n] for n in names])
```

```python
import math

import jax
import jax.numpy as jnp
from jax import lax
from jax.experimental import pallas as pl
from jax.experimental.pallas import tpu as pltpu

F32 = jnp.float32
BF16 = jnp.bfloat16

D_MODEL = 1024
SB_WIDTH = 512
GLA_KEYS = 256
GLA_WIDTH = 512
GATE_RANK = 16
IN_WIDTH = 3088
IN_PAD = 3200
D_FF = 2816
CHUNK = 64
LN_EPS = 1e-5
RMS_EPS = 1e-6
ALPHA = 2.0 ** 0.25
GLA_TAU = 16.0
SB_SCALE = 0.125
GLA_SCALE = 0.125
LANES = 128
SB_BLK = 256
SB_CUT = -100.0
GLA_ROWS = 256
VMEM_LIMIT = 56 * 1024 * 1024

ADAM_LR, ADAM_B1, ADAM_B2, ADAM_EPS, ADAM_WD, ADAM_STEP = 0.001, 0.9, 0.999, 1e-08, 0.01, 10

MESH = pl.DeviceIdType.MESH


def _cparams(*sem):
    return pltpu.CompilerParams(dimension_semantics=sem, vmem_limit_bytes=VMEM_LIMIT)


def _dot(a, b):
    return jnp.dot(a, b, preferred_element_type=F32)


def _dot_nt(a, b):
    return lax.dot_general(a, b, (((1,), (1,)), ((), ())), preferred_element_type=F32)


def _dot_tn(a, b):
    return lax.dot_general(a, b, (((0,), (0,)), ((), ())), preferred_element_type=F32)


def _split3(x):
    hi = x.astype(BF16)
    r = x - hi.astype(F32)
    mid = r.astype(BF16)
    lo = (r - mid.astype(F32)).astype(BF16)
    return hi, mid, lo


def _softplus(z):
    return jnp.maximum(z, 0.0) + jnp.log(1.0 + jnp.exp(-jnp.abs(z)))


def _sigmoid(z):
    return 1.0 / (1.0 + jnp.exp(-z))


def _mm(a, b, *, m, n, k, tm, tn, tk, ta=False, tb=False, a_spec=None, b_spec=None, o_spec=None,
        out_shape=None, out_dtype=F32, add=None, add_scale=1.0, rider=None, name):
    nk = k // tk
    dn = (((0 if ta else 1,), (1 if tb else 0,)), ((), ()))
    n_in = len(rider.inputs) if rider else 0
    n_out = len(rider.out_shapes) if rider else 0
    n_add = int(add is not None)
    steps = (m // tm, n // tn, nk)

    def body(*refs):
        a_ref, b_ref = refs[:2]
        add_ref = refs[2] if add is not None else None
        rin = refs[2 + n_add:2 + n_add + n_in]
        o_ref = refs[2 + n_add + n_in]
        rout = refs[3 + n_add + n_in:3 + n_add + n_in + n_out]
        scratch = refs[3 + n_add + n_in + n_out:]
        if rider:
            sems = scratch[len(scratch) - len(rider.sems):]
            ids = [pl.program_id(d) for d in range(3)]
            _ride(rider, rin, rout, sems, (ids[0] == 0) & (ids[1] == 0) & (ids[2] == 0),
                  (ids[0] == steps[0] - 1) & (ids[1] == steps[1] - 1) & (ids[2] == steps[2] - 1))

        part = lax.dot_general(a_ref[...].astype(BF16), b_ref[...].astype(BF16), dn, preferred_element_type=F32)

        def finish(r):
            if add is not None:
                r = r + add_scale * add_ref[...]
            o_ref[...] = r.astype(out_dtype)

        if nk == 1:
            finish(part)
            return
        acc_ref = scratch[0]
        kk = pl.program_id(2)

        @pl.when(kk == 0)
        def _():
            acc_ref[...] = part

        @pl.when((kk > 0) & (kk < nk - 1))
        def _():
            acc_ref[...] += part

        @pl.when(kk == nk - 1)
        def _():
            finish(acc_ref[...] + part)

    if a_spec is None:
        a_spec = pl.BlockSpec((tk, tm), lambda i, j, kk: (kk, i)) if ta else pl.BlockSpec((tm, tk), lambda i, j, kk: (i, kk))
    if b_spec is None:
        b_spec = pl.BlockSpec((tn, tk), lambda i, j, kk: (j, kk)) if tb else pl.BlockSpec((tk, tn), lambda i, j, kk: (kk, j))
    if o_spec is None:
        o_spec = pl.BlockSpec((tm, tn), lambda i, j, kk: (i, j))
    if out_shape is None:
        out_shape = jax.ShapeDtypeStruct((m, n), out_dtype)
    in_specs = [a_spec, b_spec]
    args = [a, b]
    if add is not None:
        in_specs.append(pl.BlockSpec((tm, tn), lambda i, j, kk: (i, j)))
        args.append(add)
    scratch = [pltpu.VMEM((tm, tn), F32)] if nk > 1 else []
    if not rider:
        return pl.pallas_call(
            body, name=name, grid=steps, in_specs=in_specs, out_specs=o_spec, out_shape=out_shape,
            scratch_shapes=scratch, compiler_params=_cparams("parallel", "parallel", "arbitrary"),
        )(*args)
    return pl.pallas_call(
        body, name=name, grid=steps, in_specs=in_specs + [HBM_SPEC] * n_in, out_specs=[o_spec] + [HBM_SPEC] * n_out,
        out_shape=[out_shape] + list(rider.out_shapes),
        input_output_aliases={len(args) + i: 1 + o for i, o in rider.aliases},
        scratch_shapes=scratch + [pltpu.SemaphoreType.DMA((s,)) for s in rider.sems],
        compiler_params=_cparams("arbitrary", "arbitrary", "arbitrary"),
    )(*args, *rider.inputs)


def _col_offsets(pieces):
    offs, o = [], 0
    for a in pieces:
        offs.append(o)
        o += a.shape[1]
    return offs


def _dx(pieces, w_in_p, dr1, t, rider, tm=512):
    offs = _col_offsets(pieces)
    npc = len(pieces)
    n_in, n_out = len(rider.inputs), len(rider.out_shapes)
    steps = t // tm

    def body(*refs):
        p_refs, w_ref, add_ref = refs[:npc], refs[npc], refs[npc + 1]
        rin = refs[npc + 2:npc + 2 + n_in]
        o_ref = refs[npc + 2 + n_in]
        rout = refs[npc + 3 + n_in:npc + 3 + n_in + n_out]
        i = pl.program_id(0)
        _ride(rider, rin, rout, refs[npc + 3 + n_in + n_out:], i == 0, i == steps - 1)
        acc = ALPHA * add_ref[...]
        for p_ref, off in zip(p_refs, offs):
            acc = acc + _dot_nt(p_ref[...].astype(BF16), w_ref[:, off:off + p_ref.shape[1]])
        o_ref[...] = acc

    row = lambda w: pl.BlockSpec((tm, w), lambda i: (i, 0))
    return pl.pallas_call(
        body, name="dx", grid=(steps,),
        in_specs=[row(a.shape[1]) for a in pieces] + [pl.BlockSpec(w_in_p.shape, lambda i: (0, 0)), row(D_MODEL)]
        + [HBM_SPEC] * n_in,
        out_specs=[row(D_MODEL)] + [HBM_SPEC] * n_out,
        out_shape=[jax.ShapeDtypeStruct((t, D_MODEL), F32)] + rider.out_shapes,
        scratch_shapes=[pltpu.SemaphoreType.DMA((s,)) for s in rider.sems],
        compiler_params=_cparams("arbitrary"),
    )(*pieces, w_in_p, dr1, *rider.inputs)


def _dw_out(sb_o, gla_o, dr1, t, tk):
    def body(sb_ref, gl_ref, dr_ref, o_ref):
        kk = pl.program_id(0)
        drb = dr_ref[...].astype(BF16)
        top, bottom = _dot_tn(sb_ref[...], drb), _dot_tn(gl_ref[...], drb)

        @pl.when(kk == 0)
        def _():
            o_ref[0:SB_WIDTH, :] = top
            o_ref[SB_WIDTH:, :] = bottom

        @pl.when(kk > 0)
        def _():
            o_ref[0:SB_WIDTH, :] += top
            o_ref[SB_WIDTH:, :] += bottom

    return pl.pallas_call(
        body, name="dw_out", grid=(t // tk,),
        in_specs=[pl.BlockSpec((tk, SB_WIDTH), lambda kk: (kk, 0)), pl.BlockSpec((tk, GLA_WIDTH), lambda kk: (kk, 0)),
                  pl.BlockSpec((tk, D_MODEL), lambda kk: (kk, 0))],
        out_specs=pl.BlockSpec((D_MODEL, D_MODEL), lambda kk: (0, 0)),
        out_shape=jax.ShapeDtypeStruct((D_MODEL, D_MODEL), F32),
        compiler_params=_cparams("arbitrary"),
    )(sb_o, gla_o, dr1)


def _dw_in(x, pieces, t, tm=512, tk=512):
    offs = _col_offsets(pieces)
    npc = len(pieces)
    n = offs[-1] + pieces[-1].shape[1]
    nk = t // tk

    def body(*refs):
        x_ref, p_refs, o_ref, acc_ref = refs[0], refs[1:1 + npc], refs[1 + npc], refs[2 + npc]
        kk = pl.program_id(1)
        xb = x_ref[...].astype(BF16)
        for p_ref, off in zip(p_refs, offs):
            cols = slice(off, off + p_ref.shape[1])
            part = _dot_tn(xb, p_ref[...].astype(BF16))
            if nk == 1:
                o_ref[:, cols] = part
                continue

            @pl.when(kk == 0)
            def _():
                acc_ref[:, cols] = part

            @pl.when((kk > 0) & (kk < nk - 1))
            def _():
                acc_ref[:, cols] += part

            @pl.when(kk == nk - 1)
            def _():
                o_ref[:, cols] = acc_ref[:, cols] + part

    return pl.pallas_call(
        body, name="dw_in", grid=(D_MODEL // tm, nk),
        in_specs=[pl.BlockSpec((tk, tm), lambda i, kk: (kk, i))]
        + [pl.BlockSpec((tk, a.shape[1]), lambda i, kk: (kk, 0)) for a in pieces],
        out_specs=pl.BlockSpec((tm, n), lambda i, kk: (i, 0)),
        out_shape=jax.ShapeDtypeStruct((D_MODEL, n), F32),
        scratch_shapes=[pltpu.VMEM((tm, n), F32)],
        compiler_params=_cparams("parallel", "arbitrary"),
    )(x, *pieces)


class _Rider:
    def __init__(self, inputs, out_shapes, sems, copies, aliases=()):
        self.inputs, self.out_shapes, self.sems, self.copies = list(inputs), list(out_shapes), tuple(sems), copies
        self.aliases = tuple(aliases)


def _join(a, b):
    na_in, na_out, na_sems = len(a.inputs), len(a.out_shapes), len(a.sems)

    def copies(ins, outs, sems):
        first = a.copies(ins[:na_in], outs[:na_out], sems[:na_sems])
        second = b.copies(ins[na_in:], outs[na_out:], sems[na_sems:])
        return tuple(u + v for u, v in zip(first, second))

    return _Rider(a.inputs + b.inputs, a.out_shapes + b.out_shapes, a.sems + b.sems, copies,
                  a.aliases + tuple((i + na_in, o + na_out) for i, o in b.aliases))


def _ride(rider, in_refs, out_refs, sems, first, last):
    @pl.when(first)
    def _():
        local, sends, _ = rider.copies(in_refs, out_refs, sems)
        for cp in local + sends:
            cp.start()

    @pl.when(last)
    def _():
        local, sends, recvs = rider.copies(in_refs, out_refs, sems)
        for cp in recvs:
            cp.wait_recv()
        for cp in sends:
            cp.wait_send()
        for cp in local:
            cp.wait()


def _run(rider, name):
    n_in, n_out = len(rider.inputs), len(rider.out_shapes)

    def body(*refs):
        local, sends, recvs = rider.copies(refs[:n_in], refs[n_in:n_in + n_out], refs[n_in + n_out:])
        for cp in local + sends:
            cp.start()
        for cp in recvs:
            cp.wait_recv()
        for cp in sends:
            cp.wait_send()
        for cp in local:
            cp.wait()

    return pl.pallas_call(
        body, name=name, in_specs=[HBM_SPEC] * n_in, out_specs=[HBM_SPEC] * n_out, out_shape=list(rider.out_shapes),
        scratch_shapes=[pltpu.SemaphoreType.DMA((s,)) for s in rider.sems],
    )(*rider.inputs)


def _sb_tile(qh, kj, diag, strict, u_strict, r_in):
    z = _dot_nt(qh, kj)
    sp = _softplus(z)
    l1m = -sp
    lsz = z - sp
    if diag:
        l1m = jnp.where(strict, l1m, 0.0)
    cs = _dot(l1m.astype(BF16), u_strict) + r_in
    w = jnp.exp(lsz + cs)
    if diag:
        w = jnp.where(strict, w, 0.0)
    return l1m, lsz, w


def _sb_consts():
    row = lax.broadcasted_iota(jnp.int32, (SB_BLK, SB_BLK), 0)
    col = lax.broadcasted_iota(jnp.int32, (SB_BLK, SB_BLK), 1)
    strict = col < row
    u_strict = (row > col).astype(BF16)
    u_pre = (row < col).astype(BF16)
    lane = lax.broadcasted_iota(jnp.int32, (1, LANES), 1)
    return strict, u_strict, u_pre, lane


def _sb_fwd(proj, t, rider):
    nq = t // SB_BLK
    n_in, n_out = len(rider.inputs), len(rider.out_shapes)

    def body(q_ref, k_ref, v_ref, *rest):
        rin, o_ref, sv_ref, rout = rest[:n_in], rest[n_in], rest[n_in + 1], rest[n_in + 2:n_in + 2 + n_out]
        p = pl.program_id(0)
        i = pl.program_id(1)
        _ride(rider, rin, rout, rest[n_in + 2 + n_out:], (p == 0) & (i == 0), (p == 3) & (i == nq - 1))

        strict, u_strict, _, lane = _sb_consts()
        qf = q_ref[...] * SB_SCALE
        hms = [(lane // 64) == hh for hh in range(2)]
        qhs = [jnp.where(hm, qf, 0.0).astype(BF16) for hm in hms]

        def step(j, r0, r1, a, sv, diag, keep=None):
            rows = pl.ds(pl.multiple_of(j * SB_BLK, SB_BLK), SB_BLK)
            kj = k_ref[rows, :].astype(BF16)
            vf = v_ref[rows, :]
            rs = []
            for hh, r in enumerate((r0, r1)):
                l1m, _, w = _sb_tile(qhs[hh], kj, diag, strict, u_strict, r)
                pv = _dot(w.astype(BF16), jnp.where(hms[hh], vf, 0.0).astype(BF16))
                mass = jnp.sum(l1m, axis=1, keepdims=True)
                if keep is not None:
                    pv, mass = jnp.where(keep, pv, 0.0), jnp.where(keep, mass, 0.0)
                a = a + pv
                rs.append(r + mass)
                sv = jnp.where(lane == hh * 64 + (i - j + 1), rs[hh], sv)
            return rs[0], rs[1], a, sv

        zero = jnp.zeros((SB_BLK, 1), F32)
        acc0 = jnp.zeros((SB_BLK, LANES), F32)
        r0, r1, acc, sv = step(i, zero, zero, acc0, acc0, True)
        r0, r1, acc, sv = step(jnp.maximum(i - 1, 0), r0, r1, acc, sv, False, keep=i > 0)
        _, _, _, acc, sv = lax.while_loop(
            lambda c: (c[0] >= 0) & (jnp.maximum(jnp.max(c[1]), jnp.max(c[2])) > SB_CUT),
            lambda c: (c[0] - 1,) + step(c[0], c[1], c[2], c[3], c[4], False),
            (i - 2, r0, r1, acc, sv))
        o_ref[...] = acc.astype(BF16)
        sv_ref[...] = sv

    return pl.pallas_call(
        body, name="sb_fwd", grid=(4, nq),
        in_specs=[pl.BlockSpec((SB_BLK, LANES), lambda p, i: (i, p)),
                  pl.BlockSpec((t, LANES), lambda p, i: (0, 4 + p)),
                  pl.BlockSpec((t, LANES), lambda p, i: (0, 8 + p))] + [HBM_SPEC] * n_in,
        out_specs=[pl.BlockSpec((SB_BLK, LANES), lambda p, i: (i, p))] * 2 + [HBM_SPEC] * n_out,
        out_shape=[jax.ShapeDtypeStruct((t, SB_WIDTH), BF16), jax.ShapeDtypeStruct((t, SB_WIDTH), F32)] + rider.out_shapes,
        scratch_shapes=[pltpu.SemaphoreType.DMA((s,)) for s in rider.sems],
        compiler_params=_cparams("arbitrary", "arbitrary"),
    )(proj, proj, proj, *rider.inputs)


def _sb_bwd(proj, dcat, mass, t, rider):
    nq = t // SB_BLK
    n_in, n_out = len(rider.inputs), len(rider.out_shapes)

    def body(q_ref, k_ref, v_ref, do_ref, sv_ref, *rest):
        rin = rest[:n_in]
        dq_ref, dk_ref, dv_ref = rest[n_in:n_in + 3]
        rout = rest[n_in + 3:n_in + 3 + n_out]
        p = pl.program_id(0)
        i = pl.program_id(1)
        _ride(rider, rin, rout, rest[n_in + 3 + n_out:], (p == 0) & (i == 0), (p == 3) & (i == nq - 1))

        @pl.when(i == 0)
        def _():
            dk_ref[...] = jnp.zeros_like(dk_ref)
            dv_ref[...] = jnp.zeros_like(dv_ref)

        strict, u_strict, u_pre, lane = _sb_consts()
        qf = q_ref[...] * SB_SCALE
        dof = do_ref[...]
        hms = [(lane // 64) == hh for hh in range(2)]
        qhs = [jnp.where(hm, qf, 0.0).astype(BF16) for hm in hms]
        dohs = [jnp.where(hm, dof, 0.0).astype(BF16) for hm in hms]

        sv = sv_ref[...]
        zero = jnp.zeros((SB_BLK, 1), F32)

        def mass_right(hh, d):
            return jnp.sum(jnp.where(lane == hh * 64 + d, sv, 0.0), axis=1, keepdims=True)

        dstop = lax.while_loop(
            lambda d: (i - d >= 0) & (jnp.maximum(jnp.max(mass_right(0, d)), jnp.max(mass_right(1, d))) > SB_CUT),
            lambda d: d + 1, 1)
        jstop = i - dstop

        def step(j, carry, diag, keep=None):
            pre_g0, pre_g1, dqa = carry
            rows = pl.ds(pl.multiple_of(j * SB_BLK, SB_BLK), SB_BLK)
            kf = k_ref[rows, :]
            kj = kf.astype(BF16)
            vj = v_ref[rows, :].astype(BF16)
            dv = jnp.zeros((SB_BLK, LANES), F32)
            dk = jnp.zeros((SB_BLK, LANES), F32)
            dqj = jnp.zeros((SB_BLK, LANES), F32)
            pre = []
            for hh, pre_g in enumerate((pre_g0, pre_g1)):
                _, lsz, w = _sb_tile(qhs[hh], kj, diag, strict, u_strict, zero if diag else mass_right(hh, i - j))
                g = w * _dot_nt(dohs[hh], vj)
                gpre = _dot(g.astype(BF16), u_pre) + pre_g
                sig = jnp.exp(lsz)
                dz = g * (1.0 - sig) - gpre * sig
                if diag:
                    dz = jnp.where(strict, dz, 0.0)
                dzb = dz.astype(BF16)
                dv = dv + _dot_tn(w.astype(BF16), dohs[hh])
                dk = dk + _dot_tn(dzb, qhs[hh])
                dqj = dqj + _dot(dzb, jnp.where(hms[hh], kf, 0.0).astype(BF16))
                gsum = jnp.sum(g, axis=1, keepdims=True)
                pre.append(pre_g + (gsum if keep is None else jnp.where(keep, gsum, 0.0)))
            if keep is not None:
                dv, dk, dqj = jnp.where(keep, dv, 0.0), jnp.where(keep, dk, 0.0), jnp.where(keep, dqj, 0.0)
            dv_ref[rows, :] += dv
            dk_ref[rows, :] += dk
            return pre[0], pre[1], dqa + dqj

        carry = lax.fori_loop(jstop + 1, i - 1, lambda j, c: step(j, c, False), (zero, zero, jnp.zeros((SB_BLK, LANES), F32)))
        carry = step(jnp.maximum(i - 1, 0), carry, False, keep=i > 0)
        _, _, dq = step(i, carry, True)
        dq_ref[...] = (dq * SB_SCALE).astype(BF16)

    return pl.pallas_call(
        body, name="sb_bwd", grid=(4, nq),
        in_specs=[pl.BlockSpec((SB_BLK, LANES), lambda p, i: (i, p)),
                  pl.BlockSpec((t, LANES), lambda p, i: (0, 4 + p)),
                  pl.BlockSpec((t, LANES), lambda p, i: (0, 8 + p)),
                  pl.BlockSpec((SB_BLK, LANES), lambda p, i: (i, p)),
                  pl.BlockSpec((SB_BLK, LANES), lambda p, i: (i, p))] + [HBM_SPEC] * n_in,
        out_specs=[pl.BlockSpec((SB_BLK, LANES), lambda p, i: (i, p)),
                   pl.BlockSpec((t, LANES), lambda p, i: (0, p)),
                   pl.BlockSpec((t, LANES), lambda p, i: (0, p))] + [HBM_SPEC] * n_out,
        out_shape=[jax.ShapeDtypeStruct((t, SB_WIDTH), BF16)] + [jax.ShapeDtypeStruct((t, SB_WIDTH), F32)] * 2
        + rider.out_shapes,
        scratch_shapes=[pltpu.SemaphoreType.DMA((s,)) for s in rider.sems],
        compiler_params=_cparams("arbitrary", "arbitrary"),
    )(proj, proj, proj, dcat, mass, *rider.inputs)


def _gla_consts():
    r = lax.broadcasted_iota(jnp.int32, (GLA_ROWS, GLA_ROWS), 0)
    c = lax.broadcasted_iota(jnp.int32, (GLA_ROWS, GLA_ROWS), 1)
    same = (r // CHUNK) == (c // CHUNK)
    causal = same & (c <= r)
    upto_mid = c % CHUNK <= CHUNK // 2 - 1
    fwd_stack = jnp.concatenate([causal, same & upto_mid, same], axis=0).astype(BF16)
    bwd_stack = jnp.concatenate([same & (c >= r), same & (r % CHUNK <= CHUNK // 2 - 1), same], axis=1).astype(BF16)
    rowid = lax.broadcasted_iota(jnp.int32, (GLA_ROWS, 1), 0)
    lane = lax.broadcasted_iota(jnp.int32, (1, LANES), 1)
    sr = lax.broadcasted_iota(jnp.int32, (LANES, 2 * LANES), 0)
    sc = lax.broadcasted_iota(jnp.int32, (LANES, 2 * LANES), 1)
    blockdiag = (sr // 64) == (sc // LANES)
    return causal, fwd_stack, bwd_stack, rowid, lane, blockdiag


def _dot3(u, x):
    hi, mid, lo = _split3(x)
    return _dot(u, hi) + _dot(u, mid) + _dot(u, lo)


def _row_to_col(row):
    return jnp.transpose(jnp.broadcast_to(row, (LANES, LANES)))


def _gla_gates(ga_ref, gup_ref, gbias_ref):
    pre = _dot(ga_ref[...].astype(BF16), gup_ref[...].astype(BF16)) + gbias_ref[...]
    log_a = (jnp.minimum(pre, 0.0) - jnp.log(1.0 + jnp.exp(-jnp.abs(pre)))) / GLA_TAU
    return pre, log_a


def _gla_terms(g2, q2, k2, fwd_stack):
    bs = _dot3(fwd_stack, g2)
    b, b_ref, b_last = bs[:GLA_ROWS], bs[GLA_ROWS:2 * GLA_ROWS], bs[2 * GLA_ROWS:]
    qs = q2 * GLA_SCALE
    e_q = jnp.exp(b - b_ref)
    e_k = jnp.exp(b_ref - b)
    e_d = jnp.exp(b_last - b)
    e_b = jnp.exp(b)
    decay = jnp.exp(b_last)
    return dict(qs=qs, e_q=e_q, e_k=e_k, e_d=e_d, e_b=e_b, decay=decay,
                qi=qs * e_q, ki=k2 * e_k, kd=k2 * e_d, qb=qs * e_b)


def _gla_fwd(proj, gate_up_pad, gate_bias, gnorm, t, rider):
    nsteps = t // GLA_ROWS
    cps = GLA_ROWS // CHUNK
    n_in, n_out = len(rider.inputs), len(rider.out_shapes)

    def body(q_ref, k_ref, v_ref, gg_ref, ga_ref, gup_ref, gbias_ref, gn_ref, *rest):
        o_ref, oraw_ref, st_ref = rest[n_in:n_in + 3]
        s_scr = rest[n_in + 3 + n_out]
        i = pl.program_id(0)
        _ride(rider, rest[:n_in], rest[n_in + 3:n_in + 3 + n_out], rest[n_in + 4 + n_out:], i == 0, i == nsteps - 1)

        @pl.when(i == 0)
        def _():
            s_scr[...] = jnp.zeros_like(s_scr)

        _, fwd_stack, _, _, lane, blockdiag = _gla_consts()
        causal64 = (lax.broadcasted_iota(jnp.int32, (CHUNK, CHUNK), 1) <= lax.broadcasted_iota(jnp.int32, (CHUNK, CHUNK), 0))
        _, log_a = _gla_gates(ga_ref, gup_ref, gbias_ref)
        gn = gn_ref[...]
        pre = []
        for p in range(2):
            kl = slice(p * LANES, (p + 1) * LANES)
            tm = _gla_terms(log_a[:, kl], q_ref[:, kl], k_ref[:, kl], fwd_stack)
            pre.append(dict(
                v2b=v_ref[:, p * 2 * LANES:(p + 1) * 2 * LANES].astype(BF16), kib=tm["ki"].astype(BF16),
                kdb=tm["kd"].astype(BF16), qbb=tm["qb"].astype(BF16), decay=tm["decay"],
                qihb=[jnp.where((lane // 64) == hh, tm["qi"], 0.0).astype(BF16) for hh in range(2)]))
        state = [s_scr[0], s_scr[1]]
        outs = [[], []]
        for cc in range(cps):
            rows = slice(cc * CHUNK, (cc + 1) * CHUNK)
            for p in range(2):
                w = pre[p]
                st_ref[cc, p] = state[p]
                intra = []
                for hh in range(2):
                    a = jnp.where(causal64, _dot_nt(w["qihb"][hh][rows], w["kib"][rows]), 0.0)
                    intra.append(_dot(a.astype(BF16), w["v2b"][rows, hh * LANES:(hh + 1) * LANES]))
                outs[p].append(jnp.concatenate(intra, axis=1) + _dot(w["qbb"][rows], state[p].astype(BF16)))
                upd = jnp.where(blockdiag, _dot_tn(w["kdb"][rows], w["v2b"][rows]), 0.0)
                dcol = _row_to_col(w["decay"][cc * CHUNK:cc * CHUNK + 1])
                state[p] = state[p] * jnp.concatenate([dcol, dcol], axis=1) + upd
        for p in range(2):
            s_scr[p] = state[p]
            vl = slice(p * 2 * LANES, (p + 1) * 2 * LANES)
            o2 = jnp.concatenate(outs[p], axis=0)
            oraw_ref[:, vl] = o2
            for hh in range(2):
                oh = o2[:, hh * LANES:(hh + 1) * LANES]
                gl = slice(p * 2 * LANES + hh * LANES, p * 2 * LANES + (hh + 1) * LANES)
                rinv = lax.rsqrt(jnp.mean(oh * oh, axis=1, keepdims=True) + RMS_EPS)
                gg = gg_ref[:, gl]
                o_ref[:, gl] = (oh * rinv * gn * (gg * _sigmoid(gg))).astype(BF16)

    cb = lambda w, idx: pl.BlockSpec((GLA_ROWS, w), lambda i: (i, idx))
    full = lambda shp: pl.BlockSpec(shp, lambda i: tuple(0 for _ in shp))
    return pl.pallas_call(
        body, name="gla_fwd", grid=(nsteps,),
        in_specs=[cb(256, 6), cb(256, 7), cb(512, 4), cb(512, 5), cb(128, 24),
                  full((LANES, GLA_KEYS)), full((1, GLA_KEYS)), full((1, LANES))] + [HBM_SPEC] * n_in,
        out_specs=[pl.BlockSpec((GLA_ROWS, GLA_WIDTH), lambda i: (i, 0)),
                   pl.BlockSpec((GLA_ROWS, GLA_WIDTH), lambda i: (i, 0)),
                   pl.BlockSpec((cps, 2, LANES, 2 * LANES), lambda i: (i, 0, 0, 0))] + [HBM_SPEC] * n_out,
        out_shape=[jax.ShapeDtypeStruct((t, GLA_WIDTH), BF16), jax.ShapeDtypeStruct((t, GLA_WIDTH), F32),
                   jax.ShapeDtypeStruct((t // CHUNK, 2, LANES, 2 * LANES), F32)]
        + rider.out_shapes,
        input_output_aliases={8 + i: 3 + o for i, o in rider.aliases},
        scratch_shapes=[pltpu.VMEM((2, LANES, 2 * LANES), F32)] + [pltpu.SemaphoreType.DMA((s,)) for s in rider.sems],
        compiler_params=_cparams("arbitrary"),
    )(proj, proj, proj, proj, proj, gate_up_pad, gate_bias, gnorm, *rider.inputs)


def _gla_bwd(proj, dcat, oraw, states, gate_up_pad, gate_bias, gnorm, t, rider):
    nsteps = t // GLA_ROWS
    cps = GLA_ROWS // CHUNK
    wout = 2 * GLA_KEYS + 2 * GLA_WIDTH + LANES
    n_in, n_out = len(rider.inputs), len(rider.out_shapes)

    def body(q_ref, k_ref, v_ref, gg_ref, ga_ref, do_ref, oraw_ref, st_ref, gup_ref, gbias_ref, gn_ref, *rest):
        d_ref, dgup_ref, dgbias_ref, dgn_ref = rest[n_in:n_in + 4]
        ds_scr = rest[n_in + 4 + n_out]
        i = pl.program_id(0)
        _ride(rider, rest[:n_in], rest[n_in + 4:n_in + 4 + n_out], rest[n_in + 5 + n_out:], i == 0, i == nsteps - 1)

        @pl.when(i == 0)
        def _():
            ds_scr[...] = jnp.zeros_like(ds_scr)
            dgup_ref[...] = jnp.zeros_like(dgup_ref)
            dgbias_ref[...] = jnp.zeros_like(dgbias_ref)
            dgn_ref[...] = jnp.zeros_like(dgn_ref)

        causal, fwd_stack, bwd_stack, rowid, lane, blockdiag = _gla_consts()
        pre, log_a = _gla_gates(ga_ref, gup_ref, gbias_ref)
        gn = gn_ref[...]
        dgn = jnp.zeros((1, LANES), F32)
        dgs = []
        pre_p = []
        for p in range(2):
            kl = slice(p * LANES, (p + 1) * LANES)
            vl = slice(p * 2 * LANES, (p + 1) * 2 * LANES)
            tm = _gla_terms(log_a[:, kl], q_ref[:, kl], k_ref[:, kl], fwd_stack)
            v2b = v_ref[:, vl].astype(BF16)
            dos = []
            for hh in range(2):
                gl = slice(p * 2 * LANES + hh * LANES, p * 2 * LANES + (hh + 1) * LANES)
                oh = oraw_ref[:, gl]
                rinv = lax.rsqrt(jnp.mean(oh * oh, axis=1, keepdims=True) + RMS_EPS)
                on = oh * rinv
                gg = gg_ref[:, gl]
                sg = _sigmoid(gg)
                sil = gg * sg
                dgo = do_ref[:, gl]
                d_ref[:, 2 * GLA_KEYS + GLA_WIDTH + gl.start:2 * GLA_KEYS + GLA_WIDTH + gl.stop] = (
                    dgo * on * gn * (sg * (1.0 + gg * (1.0 - sg)))).astype(BF16)
                dgn = dgn + jnp.sum(dgo * sil * on, axis=0, keepdims=True)
                don = dgo * sil * gn
                dos.append(rinv * (don - on * jnp.mean(don * on, axis=1, keepdims=True)))
            do2b = jnp.concatenate(dos, axis=1).astype(BF16)
            qib = tm["qi"].astype(BF16)
            kib = tm["ki"].astype(BF16)
            kdb = tm["kd"].astype(BF16)
            qbb = tm["qb"].astype(BF16)
            dqi = jnp.zeros((GLA_ROWS, LANES), F32)
            dki = jnp.zeros((GLA_ROWS, LANES), F32)
            dvs = []
            for hh in range(2):
                hm = (lane // 64) == hh
                hl = slice(hh * LANES, (hh + 1) * LANES)
                a = jnp.where(causal, _dot_nt(jnp.where(hm, tm["qi"], 0.0).astype(BF16), kib), 0.0).astype(BF16)
                da = jnp.where(causal, _dot_nt(do2b[:, hl], v2b[:, hl]), 0.0).astype(BF16)
                dvs.append(_dot_tn(a, do2b[:, hl]))
                dqi = dqi + jnp.where(hm, _dot(da, kib), 0.0)
                dki = dki + jnp.where(hm, _dot_tn(da, qib), 0.0)
            pre_p.append(dict(tm=tm, v2b=v2b, do2b=do2b, kdb=kdb, qbb=qbb, dqi=dqi, dki=dki, dvs=dvs))
        dstate = [ds_scr[0], ds_scr[1]]
        rec = [dict(dv_st=[None] * cps, dqb=[None] * cps, dkd=[None] * cps, dd=[None] * cps) for _ in range(2)]
        for cc in reversed(range(cps)):
            rows = slice(cc * CHUNK, (cc + 1) * CHUNK)
            for p in range(2):
                w, ds2 = pre_p[p], dstate[p]
                s_prev = st_ref[cc, p]
                ds2b = ds2.astype(BF16)
                rec[p]["dv_st"][cc] = _dot(w["kdb"][rows], ds2b)
                rec[p]["dqb"][cc] = _dot_nt(w["do2b"][rows], s_prev.astype(BF16))
                rec[p]["dkd"][cc] = _dot_nt(w["v2b"][rows], ds2b)
                decay_row = w["tm"]["decay"][cc * CHUNK:cc * CHUNK + 1]
                ddecay_col = jnp.sum(ds2 * s_prev, axis=1, keepdims=True)
                ddecay_row = jnp.transpose(jnp.broadcast_to(ddecay_col, (LANES, LANES)))[0:1, :]
                rec[p]["dd"][cc] = jnp.broadcast_to(ddecay_row * decay_row, (CHUNK, LANES))
                dcol = _row_to_col(decay_row)
                dstate[p] = (jnp.where(blockdiag, _dot_tn(w["qbb"][rows], w["do2b"][rows]), 0.0)
                             + ds2 * jnp.concatenate([dcol, dcol], axis=1))
        for p in range(2):
            ds_scr[p] = dstate[p]
            tm, dqi, dki = pre_p[p]["tm"], pre_p[p]["dqi"], pre_p[p]["dki"]
            dv2 = jnp.concatenate(pre_p[p]["dvs"], axis=1) + jnp.concatenate(rec[p]["dv_st"], axis=0)
            dqb = jnp.concatenate(rec[p]["dqb"], axis=0)
            dkd = jnp.concatenate(rec[p]["dkd"], axis=0)
            dd = rec[p]["dd"]
            dqs = dqi * tm["e_q"] + dqb * tm["e_b"]
            dk = dki * tm["e_k"] + dkd * tm["e_d"]
            t_qi = dqi * tm["qi"]
            t_ki = dki * tm["ki"]
            t_kd = dkd * tm["kd"]
            db = t_qi - t_ki + dqb * tm["qb"] - t_kd
            to_mid = t_ki - t_qi
            to_last = t_kd + jnp.where(rowid % CHUNK == CHUNK - 1, jnp.concatenate(dd, axis=0), 0.0)
            dgs.append(_dot3(bwd_stack, jnp.concatenate([db, to_mid, to_last], axis=0)))
            d_ref[:, p * LANES:(p + 1) * LANES] = (dqs * GLA_SCALE).astype(BF16)
            d_ref[:, GLA_KEYS + p * LANES:GLA_KEYS + (p + 1) * LANES] = dk.astype(BF16)
            d_ref[:, 2 * GLA_KEYS + p * 2 * LANES:2 * GLA_KEYS + (p + 1) * 2 * LANES] = dv2.astype(BF16)
        dlog_a = jnp.concatenate(dgs, axis=1)
        dpre = dlog_a * (1.0 / GLA_TAU) * _sigmoid(-pre)
        dpb = dpre.astype(BF16)
        dgn_ref[...] += dgn
        dgbias_ref[...] += jnp.sum(dpre, axis=0, keepdims=True)
        dgup_ref[...] += _dot_tn(ga_ref[...].astype(BF16), dpb)
        d_ref[:, 2 * GLA_KEYS + 2 * GLA_WIDTH:] = _dot_nt(dpb, gup_ref[...].astype(BF16)).astype(BF16)

    rev = lambda i: nsteps - 1 - i
    cb = lambda w, idx: pl.BlockSpec((GLA_ROWS, w), lambda i: (rev(i), idx))
    full = lambda shp: pl.BlockSpec(shp, lambda i: tuple(0 for _ in shp))
    return pl.pallas_call(
        body, name="gla_bwd", grid=(nsteps,),
        in_specs=[cb(256, 6), cb(256, 7), cb(512, 4), cb(512, 5), cb(128, 24), cb(512, 1), cb(512, 0),
                  pl.BlockSpec((cps, 2, LANES, 2 * LANES), lambda i: (rev(i), 0, 0, 0)),
                  full((LANES, GLA_KEYS)), full((1, GLA_KEYS)), full((1, LANES))] + [HBM_SPEC] * n_in,
        out_specs=[pl.BlockSpec((GLA_ROWS, wout), lambda i: (rev(i), 0)),
                   full((LANES, GLA_KEYS)), full((1, GLA_KEYS)), full((1, LANES))] + [HBM_SPEC] * n_out,
        out_shape=[jax.ShapeDtypeStruct((t, wout), BF16), jax.ShapeDtypeStruct((LANES, GLA_KEYS), F32),
                   jax.ShapeDtypeStruct((1, GLA_KEYS), F32), jax.ShapeDtypeStruct((1, LANES), F32)] + rider.out_shapes,
        scratch_shapes=[pltpu.VMEM((2, LANES, 2 * LANES), F32)] + [pltpu.SemaphoreType.DMA((s,)) for s in rider.sems],
        compiler_params=_cparams("arbitrary"),
    )(proj, proj, proj, proj, proj, dcat, oraw, states, gate_up_pad, gate_bias, gnorm, *rider.inputs)


def _ln_stats(r):
    mu = jnp.mean(r, axis=1, keepdims=True)
    xc = r - mu
    rstd = lax.rsqrt(jnp.mean(xc * xc, axis=1, keepdims=True) + LN_EPS)
    return xc * rstd, rstd


def _ln_bwd(dy_g, xhat, rstd):
    return rstd * (dy_g - jnp.mean(dy_g, axis=1, keepdims=True) - xhat * jnp.mean(dy_g * xhat, axis=1, keepdims=True))


def _outproj_ln1(sb_o, gla_o, x, w_out, g1, b1, t, tm=256):
    def body(sb_ref, gl_ref, x_ref, w_ref, g_ref, b_ref, xhat_ref, rstd_ref, h_ref):
        mix = _dot(sb_ref[...], w_ref[0:SB_WIDTH, :]) + _dot(gl_ref[...], w_ref[SB_WIDTH:, :])
        xhat, rstd = _ln_stats(ALPHA * x_ref[...] + mix)
        xhat_ref[...] = xhat
        rstd_ref[...] = rstd
        h_ref[...] = (xhat * g_ref[...] + b_ref[...]).astype(BF16)

    row = lambda w: pl.BlockSpec((tm, w), lambda i: (i, 0))
    full = lambda shp: pl.BlockSpec(shp, lambda i: (0, 0))
    return pl.pallas_call(
        body, name="outproj_ln1", grid=(t // tm,),
        in_specs=[row(SB_WIDTH), row(GLA_WIDTH), row(D_MODEL), full((D_MODEL, D_MODEL)), full((1, D_MODEL)), full((1, D_MODEL))],
        out_specs=[row(D_MODEL), row(1), row(D_MODEL)],
        out_shape=[jax.ShapeDtypeStruct((t, D_MODEL), F32), jax.ShapeDtypeStruct((t, 1), F32),
                   jax.ShapeDtypeStruct((t, D_MODEL), BF16)],
        compiler_params=_cparams("parallel"),
    )(sb_o, gla_o, x, w_out, g1, b1)


_INV_SQRT2 = 1.0 / math.sqrt(2.0)
_INV_SQRT2PI = 1.0 / math.sqrt(2.0 * math.pi)


def _conv3(xs, w_ref, b_ref, half):
    return (w_ref[half, 0:1, :] * pltpu.roll(xs, 2, 0) + w_ref[half, 1:2, :] * pltpu.roll(xs, 1, 0)
            + w_ref[half, 2:3, :] * xs + b_ref[half])


HALO = 16


def _conv_gelu_fwd(up3, conv_w3, conv_b3, t, tr=512, ct=256):
    nct = D_FF // ct
    hb = tr // HALO

    def body(cur_ref, prev_ref, w_ref, b_ref, gm_ref):
        i = pl.program_id(1)
        keep = (i > 0).astype(F32)
        us = []
        for half in range(2):
            xs = jnp.concatenate([prev_ref[half].astype(F32) * keep, cur_ref[half].astype(F32)], axis=0)
            us.append(_conv3(xs, w_ref, b_ref, half)[HALO:, :])
        a, c = us
        gelu = 0.5 * a * (1.0 + lax.erf(a * _INV_SQRT2))
        gm_ref[...] = (gelu * c).astype(BF16)

    return pl.pallas_call(
        body, name="conv_gelu_fwd", grid=(nct, t // tr),
        in_specs=[pl.BlockSpec((2, tr, ct), lambda j, i: (0, i, j)),
                  pl.BlockSpec((2, HALO, ct), lambda j, i: (0, jnp.maximum(i * hb - 1, 0), j)),
                  pl.BlockSpec((2, 3, ct), lambda j, i: (0, 0, j)),
                  pl.BlockSpec((2, 1, ct), lambda j, i: (0, 0, j))],
        out_specs=pl.BlockSpec((tr, ct), lambda j, i: (i, j)),
        out_shape=jax.ShapeDtypeStruct((t, D_FF), BF16),
        compiler_params=_cparams("parallel", "parallel"),
    )(up3, up3, conv_w3, conv_b3)


def _conv_gelu_bwd(up3, dgm, conv_w3, conv_b3, t, tr=512, ct=256):
    nct = D_FF // ct
    nrt = t // tr
    hb = tr // HALO
    n = tr + 2 * HALO
    lo, hi = HALO, tr + HALO

    def body(cur_ref, prev_ref, next_ref, dg_ref, dgn_ref, w_ref, b_ref, dup_ref, dcw_ref, dcb_ref):
        i = pl.program_id(1)

        @pl.when(i == 0)
        def _():
            dcw_ref[...] = jnp.zeros_like(dcw_ref)
            dcb_ref[...] = jnp.zeros_like(dcb_ref)

        keep_prev = (i > 0).astype(F32)
        keep_next = (i < nrt - 1).astype(F32)
        xs, xm1, xm2, us = [], [], [], []
        for half in range(2):
            x = jnp.concatenate([prev_ref[half].astype(F32) * keep_prev, cur_ref[half].astype(F32),
                                 next_ref[half].astype(F32)], axis=0)
            xs.append(x)
            xm1.append(pltpu.roll(x, 1, 0))
            xm2.append(pltpu.roll(x, 2, 0))
            us.append(w_ref[half, 0:1, :] * xm2[half] + w_ref[half, 1:2, :] * xm1[half]
                      + w_ref[half, 2:3, :] * x + b_ref[half])
        a, c = us
        dg = jnp.concatenate([jnp.zeros((HALO, ct), F32), dg_ref[...].astype(F32),
                              dgn_ref[...].astype(F32) * keep_next], axis=0)
        cdf = 0.5 * (1.0 + lax.erf(a * _INV_SQRT2))
        pdf = jnp.exp(-0.5 * a * a) * _INV_SQRT2PI
        dus = [dg * c * (cdf + a * pdf), dg * (a * cdf)]
        rid = lax.broadcasted_iota(jnp.int32, (8, 1), 0)
        for half in range(2):
            du = dus[half]
            dup = (w_ref[half, 2:3, :] * du + w_ref[half, 1:2, :] * pltpu.roll(du, n - 1, 0)
                   + w_ref[half, 0:1, :] * pltpu.roll(du, n - 2, 0))
            dup_ref[half] = dup[lo:hi, :].astype(BF16)
            duc = du[lo:hi, :]
            s0 = jnp.sum(duc * xm2[half][lo:hi, :], axis=0, keepdims=True)
            s1 = jnp.sum(duc * xm1[half][lo:hi, :], axis=0, keepdims=True)
            s2 = jnp.sum(duc * xs[half][lo:hi, :], axis=0, keepdims=True)
            dcw_ref[half] += jnp.where(rid == 0, s0, jnp.where(rid == 1, s1, jnp.where(rid == 2, s2, 0.0)))
            dcb_ref[half] += jnp.sum(duc, axis=0, keepdims=True)

    last = t // HALO - 1
    return pl.pallas_call(
        body, name="conv_gelu_bwd", grid=(nct, nrt),
        in_specs=[pl.BlockSpec((2, tr, ct), lambda j, i: (0, i, j)),
                  pl.BlockSpec((2, HALO, ct), lambda j, i: (0, jnp.maximum(i * hb - 1, 0), j)),
                  pl.BlockSpec((2, HALO, ct), lambda j, i: (0, jnp.minimum((i + 1) * hb, last), j)),
                  pl.BlockSpec((tr, ct), lambda j, i: (i, j)),
                  pl.BlockSpec((HALO, ct), lambda j, i: (jnp.minimum((i + 1) * hb, last), j)),
                  pl.BlockSpec((2, 3, ct), lambda j, i: (0, 0, j)),
                  pl.BlockSpec((2, 1, ct), lambda j, i: (0, 0, j))],
        out_specs=[pl.BlockSpec((2, tr, ct), lambda j, i: (0, i, j)),
                   pl.BlockSpec((2, 8, ct), lambda j, i: (0, 0, j)),
                   pl.BlockSpec((2, 1, ct), lambda j, i: (0, 0, j))],
        out_shape=[jax.ShapeDtypeStruct((2, t, D_FF), BF16), jax.ShapeDtypeStruct((2, 8, D_FF), F32),
                   jax.ShapeDtypeStruct((2, 1, D_FF), F32)],
        compiler_params=_cparams("parallel", "arbitrary"),
    )(up3, up3, up3, dgm, dgm, conv_w3, conv_b3)


def _down_ln2_loss(gm, w_down, xhat1, g1, b1, g2, b2, target, t, tm=256):
    def body(gm_ref, w_ref, xh_ref, g1_ref, b1_ref, g2_ref, b2_ref, tg_ref, dr_ref, drb_ref, loss_ref, dg_ref, db_ref):
        i = pl.program_id(0)

        @pl.when(i == 0)
        def _():
            loss_ref[...] = jnp.zeros_like(loss_ref)
            dg_ref[...] = jnp.zeros_like(dg_ref)
            db_ref[...] = jnp.zeros_like(db_ref)

        h = xh_ref[...] * g1_ref[...] + b1_ref[...]
        xhat, rstd = _ln_stats(ALPHA * h + _dot(gm_ref[...], w_ref[...]))
        err = xhat * g2_ref[...] + b2_ref[...] - tg_ref[...]
        loss_ref[...] += 0.5 * jnp.sum(jnp.sum(err * err, axis=1, keepdims=True), axis=0, keepdims=True) / D_MODEL
        dy = err * (1.0 / D_MODEL)
        dg_ref[...] += jnp.sum(dy * xhat, axis=0, keepdims=True)
        db_ref[...] += jnp.sum(dy, axis=0, keepdims=True)
        dr = _ln_bwd(dy * g2_ref[...], xhat, rstd)
        dr_ref[...] = dr
        drb_ref[...] = dr.astype(BF16)

    row = lambda w: pl.BlockSpec((tm, w), lambda i: (i, 0))
    full = lambda shp: pl.BlockSpec(shp, lambda i: (0, 0))
    vec = full((1, D_MODEL))
    return pl.pallas_call(
        body, name="down_ln2_loss", grid=(t // tm,),
        in_specs=[row(D_FF), full((D_FF, D_MODEL)), row(D_MODEL), vec, vec, vec, vec, row(D_MODEL)],
        out_specs=[row(D_MODEL), row(D_MODEL), full((1, 1)), vec, vec],
        out_shape=[jax.ShapeDtypeStruct((t, D_MODEL), F32), jax.ShapeDtypeStruct((t, D_MODEL), BF16),
                   jax.ShapeDtypeStruct((1, 1), F32),
                   jax.ShapeDtypeStruct((1, D_MODEL), F32), jax.ShapeDtypeStruct((1, D_MODEL), F32)],
        compiler_params=_cparams("arbitrary"),
    )(gm, w_down, xhat1, g1, b1, g2, b2, target)


def _dh_ln1_bwd(dup3, w_up4, dr2, xhat1, rstd1, g1, t, tm=256):
    ws = 2 * D_FF // 4

    def body(a_ref, w_ref, dr2_ref, xh_ref, rs_ref, g_ref, dr1_ref, dg_ref, db_ref):
        i = pl.program_id(0)

        @pl.when(i == 0)
        def _():
            dg_ref[...] = jnp.zeros_like(dg_ref)
            db_ref[...] = jnp.zeros_like(db_ref)

        dh = ALPHA * dr2_ref[...]
        for s in range(4):
            dh = dh + _dot_nt(a_ref[s // 2, :, (s % 2) * ws:(s % 2 + 1) * ws], w_ref[s])
        xhat = xh_ref[...]
        dg_ref[...] += jnp.sum(dh * xhat, axis=0, keepdims=True)
        db_ref[...] += jnp.sum(dh, axis=0, keepdims=True)
        dr1_ref[...] = _ln_bwd(dh * g_ref[...], xhat, rs_ref[...])

    row = lambda w: pl.BlockSpec((tm, w), lambda i: (i, 0))
    vec = pl.BlockSpec((1, D_MODEL), lambda i: (0, 0))
    return pl.pallas_call(
        body, name="dh_ln1_bwd", grid=(t // tm,),
        in_specs=[pl.BlockSpec((2, tm, D_FF), lambda i: (0, i, 0)),
                  pl.BlockSpec((4, D_MODEL, ws), lambda i: (0, 0, 0)),
                  row(D_MODEL), row(D_MODEL), row(1), vec],
        out_specs=[row(D_MODEL), vec, vec],
        out_shape=[jax.ShapeDtypeStruct((t, D_MODEL), F32), jax.ShapeDtypeStruct((1, D_MODEL), F32),
                   jax.ShapeDtypeStruct((1, D_MODEL), F32)],
        compiler_params=_cparams("arbitrary"),
    )(dup3, w_up4, dr2, xhat1, rstd1, g1)


def _adamw(w, g, m, v, name):
    rows, cols = w.shape
    tr = rows
    for cand in (256, 128, 64, 32, 16, 8):
        if rows % cand == 0 and rows > cand:
            tr = cand
            break
    c1 = 1.0 / (1.0 - ADAM_B1 ** ADAM_STEP)
    c2 = 1.0 / (1.0 - ADAM_B2 ** ADAM_STEP)

    def body(w_ref, g_ref, m_ref, v_ref, d_ref, nm_ref, nv_ref):
        gv = g_ref[...]
        nm = ADAM_B1 * m_ref[...] + (1.0 - ADAM_B1) * gv
        nv = ADAM_B2 * v_ref[...] + (1.0 - ADAM_B2) * (gv * gv)
        d_ref[...] = -ADAM_LR * ((nm * c1) / (jnp.sqrt(nv * c2) + ADAM_EPS) + ADAM_WD * w_ref[...])
        nm_ref[...] = nm
        nv_ref[...] = nv

    spec = pl.BlockSpec((tr, cols), lambda i: (i, 0))
    out = jax.ShapeDtypeStruct((rows, cols), F32)
    return pl.pallas_call(
        body, name=name, grid=(rows // tr,), in_specs=[spec] * 4, out_specs=[spec] * 3, out_shape=[out] * 3,
        compiler_params=_cparams("parallel"),
    )(w, g, m, v)


def _local_step(x, target, w_in_p, late_shards, gate_up_pad, gate_bias, gnorm, ln1_g, ln1_b, conv_w3, conv_b3,
                ln2_g, ln2_b, c_arr, kc_arr):
    t = x.shape[0]
    tq = min(t, 1024)
    s_up, s_out, s_down = late_shards
    sh_up, sh_out, sh_down = LATE_SHAPES
    proj, out_partly = _mm(x, w_in_p, m=t, n=IN_PAD, k=D_MODEL, tm=512, tn=IN_PAD, tk=D_MODEL, name="proj",
                           rider=_gather_rider([s_out], [sh_out]))
    sb_o, sb_mass, up_partly = _sb_fwd(proj, t, _gather_rider([s_up], [sh_up]))
    gla_o, oraw, states, w_up4, w_out4, down_partly = _gla_fwd(
        proj, gate_up_pad, gate_bias, gnorm, t,
        _join(_forward_rider([up_partly, out_partly], [sh_up, sh_out]), _gather_rider([s_down], [sh_down])))
    w_out = w_out4.reshape(D_MODEL, D_MODEL)
    xhat1, rstd1, h_bf = _outproj_ln1(sb_o, gla_o, x, w_out, ln1_g, ln1_b, t)
    up3, w_down4 = _mm(h_bf, w_up4, m=t, n=2 * D_FF, k=D_MODEL, tm=tq, tn=W_UP_S, tk=D_MODEL, name="up",
                       b_spec=pl.BlockSpec((None, D_MODEL, W_UP_S), lambda i, j, kk: (j, 0, 0)),
                       o_spec=pl.BlockSpec((None, tq, W_UP_S), lambda i, j, kk: (j // 2, i, j % 2)),
                       out_shape=jax.ShapeDtypeStruct((2, t, D_FF), BF16), out_dtype=BF16,
                       rider=_forward_rider([down_partly], [sh_down]))
    w_down = w_down4.reshape(D_FF, D_MODEL)
    gm = _conv_gelu_fwd(up3, conv_w3, conv_b3, t, tr=tq)
    dr2, dr2b, loss, dln2_g, dln2_b = _down_ln2_loss(gm, w_down, xhat1, ln1_g, ln1_b, ln2_g, ln2_b, target, t)
    dgm = _mm(dr2b, w_down, m=t, n=D_FF, k=D_MODEL, tm=tq, tn=W_UP_S, tk=D_MODEL, tb=True, out_dtype=BF16, name="dgm")
    dw_down = _mm(gm, dr2b, m=D_FF, n=D_MODEL, k=t, tm=W_UP_S, tn=D_MODEL, tk=tq, ta=True, name="dw_down")
    dup3, dcw, dcb = _conv_gelu_bwd(up3, dgm, conv_w3, conv_b3, t, tr=tq)
    dr1, dln1_g, dln1_b = _dh_ln1_bwd(dup3, w_up4, dr2, xhat1, rstd1, ln1_g, t)
    dw_up4 = _mm(h_bf, dup3, m=D_MODEL, n=2 * D_FF, k=t, tm=512, tn=W_UP_S, tk=t, ta=True, name="dw_up",
                 b_spec=pl.BlockSpec((None, t, W_UP_S), lambda i, j, kk: (j // 2, kk, j % 2)),
                 o_spec=pl.BlockSpec((None, 512, W_UP_S), lambda i, j, kk: (j, i, 0)),
                 out_shape=jax.ShapeDtypeStruct((4, D_MODEL, W_UP_S), F32))
    dw_out = _dw_out(sb_o, gla_o, dr1, t, tq)
    gs = [dw_up4, dw_out.reshape(4, W_OUT_S, D_MODEL), dw_down.reshape(4, W_DOWN_S, D_MODEL)]
    dcat = _mm(dr1, w_out, m=t, n=D_MODEL, k=D_MODEL, tm=tq, tn=512, tk=D_MODEL, tb=True, name="dcat")
    dgla, dgup_pad, dgbias, dgnorm, *from_sib = _gla_bwd(proj, dcat, oraw, states, gate_up_pad, gate_bias, gnorm, t,
                                                         _sibling_rider(gs, LATE_SHAPES))
    ps = [_add_sibling(gs[m], from_sib[m], c_arr, LATE_ADD_ROWS[m], "add_sibling_late_%d" % m) for m in range(3)]
    small = dict(
        gate_up=dgup_pad[:GATE_RANK], gate_bias=dgbias, gla_norm_g=dgnorm, ln1_g=dln1_g, ln1_b=dln1_b,
        conv_w=jnp.concatenate([dcw[0, :3], dcw[1, :3]], axis=1), conv_b=jnp.concatenate([dcb[0], dcb[1]], axis=1),
        ln2_g=dln2_g, ln2_b=dln2_b, loss=loss)
    dsq, dsk, dsv, *others, vecs = _sb_bwd(proj, dcat, sb_mass, t,
                                           _chips_rider(ps, _pack_vec(small, SMALL_GRADS, GRAD_VEC_ROWS)))
    late_sums = [_add_chips(ps[m], others[m], kc_arr, LATE_ADD_ROWS[m], "add_chips_late_%d" % m) for m in range(3)]
    dproj = [dsq, dsk, dsv, dgla]
    dw_in_p = _dw_in(x, dproj, t, tk=tq)
    g_in = dw_in_p[None]
    from_sib_in, = _run(_sibling_rider([g_in], [(D_MODEL, IN_PAD)]), "exchange_sibling_w_in")
    half_in = _add_sibling(g_in, from_sib_in, c_arr, ADD_ROWS[0], "add_sibling_w_in")[0]
    p_in = jnp.stack([half_in[:, k * W_IN_S:(k + 1) * W_IN_S] for k in range(4)], axis=0)
    dx, others_in = _dx(dproj, w_in_p, dr1, t, _chips_rider([p_in]))
    sum_in = _add_chips(p_in, others_in, kc_arr, ADD_ROWS[0], "add_chips_w_in")
    return dx, [sum_in] + late_sums, vecs


W_IN_S, W_UP_S, W_OUT_S, W_DOWN_S = IN_WIDTH // 4, 2 * D_FF // 4, D_MODEL // 4, D_FF // 4
SHARD_SHAPES = ((D_MODEL, W_IN_S), (D_MODEL, W_UP_S), (W_OUT_S, D_MODEL), (W_DOWN_S, D_MODEL))
ADD_ROWS = (256, 256, 128, 176)
LATE_SHAPES, LATE_ADD_ROWS = SHARD_SHAPES[1:], ADD_ROWS[1:]
SMALL_ROWS = 8
VEC_SIZES = (("gate_bias", GLA_KEYS), ("gla_norm_g", LANES), ("ln1_g", D_MODEL), ("ln1_b", D_MODEL),
             ("conv_b", 2 * D_FF), ("ln2_g", D_MODEL), ("ln2_b", D_MODEL))
SMALL_GRADS = VEC_SIZES + (("conv_w", 3 * 2 * D_FF), ("gate_up", GATE_RANK * GLA_KEYS), ("loss", 1))


def _rows(a):
    flat = a.reshape(-1)
    pad = (-flat.shape[0]) % D_MODEL
    if pad:
        flat = jnp.pad(flat, (0, pad))
    return flat.reshape(-1, D_MODEL)


def _pad_rows(a, rows):
    return jnp.pad(a, ((0, rows - a.shape[0]), (0, 0)))


def _pack_vec(d, sizes, rows):
    flat = jnp.concatenate([d[n].reshape(-1) for n, _ in sizes])
    return _pad_rows(_rows(flat), rows)


def _unpack_vec(v, sizes):
    flat = v.reshape(-1)
    out, o = {}, 0
    for n, size in sizes:
        out[n] = flat[o:o + size].reshape(1, size)
        o += size
    return out


VEC_ROWS = 16
GRAD_VEC_ROWS = 32


HBM_SPEC = pl.BlockSpec(memory_space=pltpu.HBM)


def _position():
    x, y, c = lax.axis_index("x"), lax.axis_index("y"), lax.axis_index("c")
    chips = [(1 - x, y), (x, 1 - y), (1 - x, 1 - y)]
    return x, y, c, chips


def _remote(src, dst, send_sems, recv_sems, k, to):
    return pltpu.make_async_remote_copy(src_ref=src, dst_ref=dst, send_sem=send_sems.at[k], recv_sem=recv_sems.at[k],
                                        device_id=to, device_id_type=MESH)


def _gather_ici(in_refs, out_refs, shapes, send_sems, recv_sems, local_sems):
    x, y, c, chips = _position()
    k_me = 2 * x + y
    local, sends, recvs = [], [], []
    for m, (rows, _) in enumerate(shapes):
        h = rows // 2
        local.append(pltpu.make_async_copy(in_refs[m], out_refs[m].at[k_me], local_sems.at[m]))
        for j, (cx, cy) in enumerate(chips):
            sends.append(_remote(in_refs[m].at[pl.ds(c * h, h), :], out_refs[m].at[k_me, pl.ds(c * h, h), :],
                                 send_sems, recv_sems, 3 * m + j, (cx, cy, c)))
            landed = out_refs[m].at[2 * cx + cy, pl.ds(c * h, h), :]
            recvs.append(_remote(landed, landed, send_sems, recv_sems, 3 * m + j, (x, y, c)))
    return local, sends, recvs


def _gather_d2d(src_refs, dst_refs, shapes, send_sems, recv_sems, base):
    x, y, c, chips = _position()
    sends, recvs = [], []
    for m, (rows, _) in enumerate(shapes):
        h = rows // 2
        for j, (cx, cy) in enumerate(chips):
            k = 2 * cx + cy
            sends.append(_remote(src_refs[m].at[k, pl.ds(c * h, h), :], dst_refs[m].at[k, pl.ds(c * h, h), :],
                                 send_sems, recv_sems, base + 3 * m + j, (x, y, 1 - c)))
            landed = dst_refs[m].at[k, pl.ds((1 - c) * h, h), :]
            recvs.append(_remote(landed, landed, send_sems, recv_sems, base + 3 * m + j, (x, y, c)))
    return sends, recvs


def _gather_weights(shards, small, shapes):
    nm = len(shards)
    n_ici = 3 * nm

    def body(*refs):
        in_refs, small_ref = refs[:nm], refs[nm]
        out_refs, osm_ref = refs[nm + 1:2 * nm + 1], refs[2 * nm + 1]
        send_sems, recv_sems, local_sems = refs[2 * nm + 2:]
        x, y, c, chips = _position()
        k_me = 2 * x + y
        local, sends, recvs = _gather_ici(in_refs, out_refs, shapes, send_sems, recv_sems, local_sems)
        local.append(pltpu.make_async_copy(small_ref, osm_ref.at[k_me], local_sems.at[nm]))
        for j, (cx, cy) in enumerate(chips):
            sends.append(_remote(small_ref, osm_ref.at[k_me], send_sems, recv_sems, n_ici + j, (cx, cy, c)))
        for cp in local + sends:
            cp.start()
        fsends, frecvs = _gather_d2d(out_refs, out_refs, shapes, send_sems, recv_sems, n_ici + 3)
        for landed, onward in zip(recvs, fsends):
            landed.wait_recv()
            onward.start()
        for j, (cx, cy) in enumerate(chips):
            k = 2 * cx + cy
            frecvs.append(_remote(osm_ref.at[k], osm_ref.at[k], send_sems, recv_sems, n_ici + j, (x, y, c)))
        for cp in frecvs:
            cp.wait_recv()
        for cp in sends + fsends:
            cp.wait_send()
        for cp in local:
            cp.wait()

    n_sems = 2 * n_ici + 3
    return pl.pallas_call(
        body, name="gather_weights", in_specs=[HBM_SPEC] * (nm + 1), out_specs=[HBM_SPEC] * (nm + 1),
        out_shape=[jax.ShapeDtypeStruct((4,) + s, BF16) for s in shapes]
        + [jax.ShapeDtypeStruct((4, SMALL_ROWS, D_MODEL), F32)],
        scratch_shapes=[pltpu.SemaphoreType.DMA((n_sems,)), pltpu.SemaphoreType.DMA((n_sems,)),
                        pltpu.SemaphoreType.DMA((nm + 1,))],
    )(*shards, small)


def _gather_rider(shards, shapes):
    n = len(shards)
    return _Rider(shards, [jax.ShapeDtypeStruct((4,) + s, BF16) for s in shapes], (3 * n, 3 * n, n),
                  lambda ins, outs, sems: _gather_ici(ins, outs, shapes, *sems))


def _forward_rider(gathered, shapes):
    n = len(gathered)
    return _Rider(gathered, [jax.ShapeDtypeStruct(a.shape, a.dtype) for a in gathered], (3 * n, 3 * n),
                  lambda ins, outs, sems: ([],) + _gather_d2d(ins, outs, shapes, sems[0], sems[1], 0),
                  aliases=[(m, m) for m in range(n)])


def _sibling_rider(gs, shapes):
    def copies(ins, outs, sems):
        x, y, c, _ = _position()
        both = []
        for m, (rows, _) in enumerate(shapes):
            h = rows // 2
            for k in range(gs[m].shape[0]):
                both.append(_remote(ins[m].at[k, pl.ds((1 - c) * h, h), :], outs[m].at[k], sems[0], sems[1],
                                    4 * m + k, (x, y, 1 - c)))
        return [], both, both

    return _Rider(gs, [jax.ShapeDtypeStruct((g.shape[0], r // 2, cl), F32) for g, (r, cl) in zip(gs, shapes)],
                  (4 * len(gs), 4 * len(gs)), copies)


def _add_sibling(g, r, c_arr, tr, name):
    nblk, rows, cols = g.shape
    nb = rows // 2 // tr

    def body(c_ref, g_ref, r_ref, o_ref):
        o_ref[...] = (g_ref[...] + r_ref[...]).astype(BF16)

    spec = pl.BlockSpec((None, tr, cols), lambda k, i, c: (k, i, 0))
    return pl.pallas_call(
        body, name=name,
        grid_spec=pltpu.PrefetchScalarGridSpec(
            num_scalar_prefetch=1, grid=(nblk, nb),
            in_specs=[pl.BlockSpec((None, tr, cols), lambda k, i, c: (k, c[0] * nb + i, 0)), spec], out_specs=spec),
        out_shape=jax.ShapeDtypeStruct((nblk, rows // 2, cols), BF16), compiler_params=_cparams("parallel", "parallel"),
    )(c_arr, g, r)


def _reduce_ici(p_refs, r_refs, send_sems, recv_sems):
    x, y, c, chips = _position()
    sends, recvs = [], []
    for m in range(len(p_refs)):
        for j, (cx, cy) in enumerate(chips):
            sends.append(_remote(p_refs[m].at[2 * cx + cy], r_refs[m].at[j], send_sems, recv_sems, 3 * m + j, (cx, cy, c)))
            recvs.append(_remote(r_refs[m].at[j], r_refs[m].at[j], send_sems, recv_sems, 3 * m + j, (x, y, c)))
    return sends, recvs


def _chips_rider(ps, vec=None):
    nm = len(ps)
    n_ici = 3 * nm

    def copies(ins, outs, sems):
        sends, recvs = _reduce_ici(ins[:nm], outs[:nm], sems[0], sems[1])
        if vec is None:
            return [], sends, recvs
        x, y, c, _ = _position()
        my_id = 4 * x + 2 * y + c
        vec_ref, vrecv_ref = ins[nm], outs[nm]
        local = [pltpu.make_async_copy(vec_ref, vrecv_ref.at[my_id], sems[2].at[0])]
        for r in range(1, 8):
            peer = (1 - x if r & 4 else x, 1 - y if r & 2 else y, 1 - c if r & 1 else c)
            sends.append(_remote(vec_ref, vrecv_ref.at[my_id], sems[0], sems[1], n_ici + r - 1, peer))
            recvs.append(_remote(vec_ref, vrecv_ref.at[0], sems[0], sems[1], n_ici + r - 1, (x, y, c)))
        return local, sends, recvs

    out_shapes = [jax.ShapeDtypeStruct((3,) + p.shape[1:], p.dtype) for p in ps]
    if vec is None:
        return _Rider(ps, out_shapes, (n_ici, n_ici), copies)
    return _Rider(list(ps) + [vec], out_shapes + [jax.ShapeDtypeStruct((8, GRAD_VEC_ROWS, D_MODEL), F32)],
                  (n_ici + 7, n_ici + 7, 1), copies)


def _add_chips(p, r, kc_arr, tr, name):
    _, h, cols = p.shape
    nb = h // tr

    def body(kc_ref, p_ref, r0_ref, r1_ref, r2_ref, o_ref):
        o_ref[...] = ((p_ref[...].astype(F32) + r0_ref[...].astype(F32)) + r1_ref[...].astype(F32)) + r2_ref[...].astype(F32)

    rspec = lambda j: pl.BlockSpec((None, tr, cols), lambda i, kc: (j, i, 0))
    return pl.pallas_call(
        body, name=name,
        grid_spec=pltpu.PrefetchScalarGridSpec(
            num_scalar_prefetch=1, grid=(nb,),
            in_specs=[pl.BlockSpec((None, tr, cols), lambda i, kc: (kc[0], i, 0)), rspec(0), rspec(1), rspec(2)],
            out_specs=pl.BlockSpec((tr, cols), lambda i, kc: (kc[1] * nb + i, 0))),
        out_shape=jax.ShapeDtypeStruct((2 * h, cols), F32), compiler_params=_cparams("parallel"),
    )(kc_arr, p, r, r, r)


def _reunite_sibling(fs, shapes):
    n_chunks = 2
    nm = len(fs)

    def body(*refs):
        in_refs, out_refs = refs[:nm], refs[nm:2 * nm]
        send_sems, recv_sems = refs[2 * nm:]
        x, y, c, _ = _position()
        sends, recvs = [], []
        for m in range(nm):
            ch = shapes[m][0] // 2 // n_chunks
            for q in range(n_chunks):
                mine = pl.ds((c * n_chunks + q) * ch, ch)
                theirs = pl.ds(((1 - c) * n_chunks + q) * ch, ch)
                s = n_chunks * m + q
                sends.append(_remote(in_refs[m].at[mine, :], out_refs[m].at[mine, :], send_sems, recv_sems, s, (x, y, 1 - c)))
                recvs.append(_remote(in_refs[m].at[theirs, :], out_refs[m].at[theirs, :], send_sems, recv_sems, s, (x, y, c)))
        for cp in sends:
            cp.start()
        for cp in recvs:
            cp.wait_recv()
        for cp in sends:
            cp.wait_send()

    n_sems = n_chunks * nm
    return pl.pallas_call(
        body, name="reunite_sibling", in_specs=[HBM_SPEC] * nm, out_specs=[HBM_SPEC] * nm,
        out_shape=[jax.ShapeDtypeStruct(s, F32) for s in shapes],
        input_output_aliases={m: m for m in range(nm)},
        scratch_shapes=[pltpu.SemaphoreType.DMA((n_sems,)), pltpu.SemaphoreType.DMA((n_sems,))],
    )(*fs)


def _sum_vecs(v):
    def body(v_ref, o_ref):
        acc = v_ref[0]
        for d in range(1, 8):
            acc = acc + v_ref[d]
        o_ref[...] = acc

    return pl.pallas_call(body, name="sum_vecs", out_shape=jax.ShapeDtypeStruct(v.shape[1:], F32))(v)


def kernel(x, w_in, gate_up, gate_bias, gla_norm_g, w_out, ln1_g, ln1_b, w_up, conv_w, conv_b, w_down, ln2_g, ln2_b, loss_target, m_w_in, m_gate_up, m_gate_bias, m_gla_norm_g, m_w_out, m_ln1_g, m_ln1_b, m_w_up, m_conv_w, m_conv_b, m_w_down, m_ln2_g, m_ln2_b, v_w_in, v_gate_up, v_gate_bias, v_gla_norm_g, v_w_out, v_ln1_g, v_ln1_b, v_w_up, v_conv_w, v_conv_b, v_w_down, v_ln2_g, v_ln2_b):
    xi, yi, ci = lax.axis_index("x"), lax.axis_index("y"), lax.axis_index("c")
    k_me = 2 * xi + yi
    c_arr = ci.astype(jnp.int32).reshape(1)
    kc_arr = jnp.stack([k_me, ci]).astype(jnp.int32)
    small = _pad_rows(jnp.concatenate([_rows(conv_w[0]), _rows(gate_up[0])], axis=0), SMALL_ROWS)
    w_in4, gsmall = _gather_weights([w_in[0].astype(BF16)], small, SHARD_SHAPES[:1])
    late_shards = [w_up[0].astype(BF16), w_out[0].astype(BF16), w_down[0].astype(BF16)]
    w_in_p = jnp.pad(jnp.concatenate([w_in4[k] for k in range(4)], axis=1), ((0, 0), (0, IN_PAD - IN_WIDTH)))
    conv_w_f = jnp.concatenate([gsmall[k, :5].reshape(-1)[:3 * W_UP_S].reshape(3, W_UP_S) for k in range(4)], axis=1)
    gate_up_f = jnp.concatenate([gsmall[k, 5].reshape(GATE_RANK, GLA_KEYS // 4) for k in range(4)], axis=1)
    conv_w3 = jnp.transpose(conv_w_f.reshape(3, 2, D_FF), (1, 0, 2))
    conv_b3 = conv_b.reshape(2, 1, D_FF)
    gate_up_pad = jnp.pad(gate_up_f, ((0, LANES - GATE_RANK), (0, 0)))

    dx, sums, vecs = _local_step(
        x[0], loss_target[0], w_in_p, late_shards, gate_up_pad, gate_bias, gla_norm_g, ln1_g, ln1_b, conv_w3, conv_b3,
        ln2_g, ln2_b, c_arr, kc_arr)
    g_w_in, g_w_up, g_w_out, g_w_down = _reunite_sibling(sums, SHARD_SHAPES)
    gsmall_sum = _unpack_vec(_sum_vecs(vecs), SMALL_GRADS)
    g_conv_w = lax.dynamic_slice_in_dim(gsmall_sum["conv_w"].reshape(3, 2 * D_FF), k_me * W_UP_S, W_UP_S, axis=1)
    g_gate_up = lax.dynamic_slice_in_dim(gsmall_sum["gate_up"].reshape(GATE_RANK, GLA_KEYS), k_me * (GLA_KEYS // 4),
                                         GLA_KEYS // 4, axis=1)
    gv = gsmall_sum
    loss = gv["loss"][0, 0]
    gvec = _pack_vec(gv, VEC_SIZES, VEC_ROWS)

    grads = dict(w_in=g_w_in[None], gate_up=g_gate_up[None], gate_bias=gv["gate_bias"], gla_norm_g=gv["gla_norm_g"],
                 w_out=g_w_out[None], ln1_g=gv["ln1_g"], ln1_b=gv["ln1_b"], w_up=g_w_up[None], conv_w=g_conv_w[None],
                 conv_b=gv["conv_b"], w_down=g_w_down[None], ln2_g=gv["ln2_g"], ln2_b=gv["ln2_b"])
    weights = dict(w_in=w_in, gate_up=gate_up, gate_bias=gate_bias, gla_norm_g=gla_norm_g, w_out=w_out, ln1_g=ln1_g,
                   ln1_b=ln1_b, w_up=w_up, conv_w=conv_w, conv_b=conv_b, w_down=w_down, ln2_g=ln2_g, ln2_b=ln2_b)
    ms = dict(w_in=m_w_in, gate_up=m_gate_up, gate_bias=m_gate_bias, gla_norm_g=m_gla_norm_g, w_out=m_w_out, ln1_g=m_ln1_g,
              ln1_b=m_ln1_b, w_up=m_w_up, conv_w=m_conv_w, conv_b=m_conv_b, w_down=m_w_down, ln2_g=m_ln2_g, ln2_b=m_ln2_b)
    vs = dict(w_in=v_w_in, gate_up=v_gate_up, gate_bias=v_gate_bias, gla_norm_g=v_gla_norm_g, w_out=v_w_out, ln1_g=v_ln1_g,
              ln1_b=v_ln1_b, w_up=v_w_up, conv_w=v_conv_w, conv_b=v_conv_b, w_down=v_w_down, ln2_g=v_ln2_g, ln2_b=v_ln2_b)
    names = ["w_in", "gate_up", "gate_bias", "gla_norm_g", "w_out", "ln1_g", "ln1_b", "w_up", "conv_w", "conv_b", "w_down",
             "ln2_g", "ln2_b"]
    delta, new_m, new_v = {}, {}, {}
    for n in ("w_in", "gate_up", "w_out", "w_up", "conv_w", "w_down"):
        tr_ = jnp.transpose if n == "w_in" else (lambda a: a)
        d, nm, nv = _adamw(tr_(weights[n][0]), tr_(grads[n][0]), tr_(ms[n][0]), tr_(vs[n][0]), "adamw_" + n)
        delta[n], new_m[n], new_v[n] = tr_(d)[None], tr_(nm)[None], tr_(nv)[None]
    d, nm, nv = _adamw(_pack_vec(weights, VEC_SIZES, VEC_ROWS), gvec, _pack_vec(ms, VEC_SIZES, VEC_ROWS),
                       _pack_vec(vs, VEC_SIZES, VEC_ROWS), "adamw_vectors")
    for dst, src in ((delta, d), (new_m, nm), (new_v, nv)):
        dst.update(_unpack_vec(src, VEC_SIZES))
    return (loss, dx[None], *[grads[n] for n in names], *[delta[n] for n in names], *[new_m[n] for n in names],
            *[new_v[n] for n in names])
```

```python
import math

import jax
import jax.numpy as jnp
from jax import lax
from jax.experimental import pallas as pl
from jax.experimental.pallas import tpu as pltpu

F32 = jnp.float32
BF16 = jnp.bfloat16

D_MODEL = 1024
SB_WIDTH = 512
GLA_KEYS = 256
GLA_WIDTH = 512
GATE_RANK = 16
IN_WIDTH = 3088
IN_PAD = 3200
D_FF = 2816
CHUNK = 64
LN_EPS = 1e-5
RMS_EPS = 1e-6
ALPHA = 2.0 ** 0.25
GLA_TAU = 16.0
SB_SCALE = 0.125
GLA_SCALE = 0.125
LANES = 128
SB_BLK = 256
SB_CUT = -100.0
GLA_ROWS = 256
VMEM_LIMIT = 56 * 1024 * 1024

ADAM_LR, ADAM_B1, ADAM_B2, ADAM_EPS, ADAM_WD, ADAM_STEP = 0.001, 0.9, 0.999, 1e-08, 0.01, 10

MESH = pl.DeviceIdType.MESH


def _cparams(*sem):
    return pltpu.CompilerParams(dimension_semantics=sem, vmem_limit_bytes=VMEM_LIMIT)


def _dot(a, b):
    return jnp.dot(a, b, preferred_element_type=F32)


def _dot_nt(a, b):
    return lax.dot_general(a, b, (((1,), (1,)), ((), ())), preferred_element_type=F32)


def _dot_tn(a, b):
    return lax.dot_general(a, b, (((0,), (0,)), ((), ())), preferred_element_type=F32)


def _split3(x):
    hi = x.astype(BF16)
    r = x - hi.astype(F32)
    mid = r.astype(BF16)
    lo = (r - mid.astype(F32)).astype(BF16)
    return hi, mid, lo


def _softplus(z):
    return jnp.maximum(z, 0.0) + jnp.log(1.0 + jnp.exp(-jnp.abs(z)))


def _sigmoid(z):
    return 1.0 / (1.0 + jnp.exp(-z))


def _mm(a, b, *, m, n, k, tm, tn, tk, ta=False, tb=False, a_spec=None, b_spec=None, o_spec=None,
        out_shape=None, out_dtype=F32, add=None, add_scale=1.0, rider=None, name):
    nk = k // tk
    dn = (((0 if ta else 1,), (1 if tb else 0,)), ((), ()))
    n_in = len(rider.inputs) if rider else 0
    n_out = len(rider.out_shapes) if rider else 0
    n_add = int(add is not None)
    steps = (m // tm, n // tn, nk)

    def body(*refs):
        a_ref, b_ref = refs[:2]
        add_ref = refs[2] if add is not None else None
        rin = refs[2 + n_add:2 + n_add + n_in]
        o_ref = refs[2 + n_add + n_in]
        rout = refs[3 + n_add + n_in:3 + n_add + n_in + n_out]
        scratch = refs[3 + n_add + n_in + n_out:]
        if rider:
            sems = scratch[len(scratch) - len(rider.sems):]
            ids = [pl.program_id(d) for d in range(3)]
            _ride(rider, rin, rout, sems, (ids[0] == 0) & (ids[1] == 0) & (ids[2] == 0),
                  (ids[0] == steps[0] - 1) & (ids[1] == steps[1] - 1) & (ids[2] == steps[2] - 1))

        part = lax.dot_general(a_ref[...].astype(BF16), b_ref[...].astype(BF16), dn, preferred_element_type=F32)

        def finish(r):
            if add is not None:
                r = r + add_scale * add_ref[...]
            o_ref[...] = r.astype(out_dtype)

        if nk == 1:
            finish(part)
            return
        acc_ref = scratch[0]
        kk = pl.program_id(2)

        @pl.when(kk == 0)
        def _():
            acc_ref[...] = part

        @pl.when((kk > 0) & (kk < nk - 1))
        def _():
            acc_ref[...] += part

        @pl.when(kk == nk - 1)
        def _():
            finish(acc_ref[...] + part)

    if a_spec is None:
        a_spec = pl.BlockSpec((tk, tm), lambda i, j, kk: (kk, i)) if ta else pl.BlockSpec((tm, tk), lambda i, j, kk: (i, kk))
    if b_spec is None:
        b_spec = pl.BlockSpec((tn, tk), lambda i, j, kk: (j, kk)) if tb else pl.BlockSpec((tk, tn), lambda i, j, kk: (kk, j))
    if o_spec is None:
        o_spec = pl.BlockSpec((tm, tn), lambda i, j, kk: (i, j))
    if out_shape is None:
        out_shape = jax.ShapeDtypeStruct((m, n), out_dtype)
    in_specs = [a_spec, b_spec]
    args = [a, b]
    if add is not None:
        in_specs.append(pl.BlockSpec((tm, tn), lambda i, j, kk: (i, j)))
        args.append(add)
    scratch = [pltpu.VMEM((tm, tn), F32)] if nk > 1 else []
    if not rider:
        return pl.pallas_call(
            body, name=name, grid=steps, in_specs=in_specs, out_specs=o_spec, out_shape=out_shape,
            scratch_shapes=scratch, compiler_params=_cparams("parallel", "parallel", "arbitrary"),
        )(*args)
    return pl.pallas_call(
        body, name=name, grid=steps, in_specs=in_specs + [HBM_SPEC] * n_in, out_specs=[o_spec] + [HBM_SPEC] * n_out,
        out_shape=[out_shape] + list(rider.out_shapes),
        input_output_aliases={len(args) + i: 1 + o for i, o in rider.aliases},
        scratch_shapes=scratch + [pltpu.SemaphoreType.DMA((s,)) for s in rider.sems],
        compiler_params=_cparams("arbitrary", "arbitrary", "arbitrary"),
    )(*args, *rider.inputs)


def _col_offsets(pieces):
    offs, o = [], 0
    for a in pieces:
        offs.append(o)
        o += a.shape[1]
    return offs


def _dx(pieces, w_in_p, dr1, t, rider, tm=512):
    offs = _col_offsets(pieces)
    npc = len(pieces)
    n_in, n_out = len(rider.inputs), len(rider.out_shapes)
    steps = t // tm

    def body(*refs):
        p_refs, w_ref, add_ref = refs[:npc], refs[npc], refs[npc + 1]
        rin = refs[npc + 2:npc + 2 + n_in]
        o_ref = refs[npc + 2 + n_in]
        rout = refs[npc + 3 + n_in:npc + 3 + n_in + n_out]
        i = pl.program_id(0)
        _ride(rider, rin, rout, refs[npc + 3 + n_in + n_out:], i == 0, i == steps - 1)
        acc = ALPHA * add_ref[...]
        for p_ref, off in zip(p_refs, offs):
            acc = acc + _dot_nt(p_ref[...].astype(BF16), w_ref[:, off:off + p_ref.shape[1]])
        o_ref[...] = acc

    row = lambda w: pl.BlockSpec((tm, w), lambda i: (i, 0))
    return pl.pallas_call(
        body, name="dx", grid=(steps,),
        in_specs=[row(a.shape[1]) for a in pieces] + [pl.BlockSpec(w_in_p.shape, lambda i: (0, 0)), row(D_MODEL)]
        + [HBM_SPEC] * n_in,
        out_specs=[row(D_MODEL)] + [HBM_SPEC] * n_out,
        out_shape=[jax.ShapeDtypeStruct((t, D_MODEL), F32)] + rider.out_shapes,
        scratch_shapes=[pltpu.SemaphoreType.DMA((s,)) for s in rider.sems],
        compiler_params=_cparams("arbitrary"),
    )(*pieces, w_in_p, dr1, *rider.inputs)


def _dw_out(sb_o, gla_o, dr1, t, tk):
    def body(sb_ref, gl_ref, dr_ref, o_ref):
        kk = pl.program_id(0)
        drb = dr_ref[...].astype(BF16)
        top, bottom = _dot_tn(sb_ref[...], drb), _dot_tn(gl_ref[...], drb)

        @pl.when(kk == 0)
        def _():
            o_ref[0:SB_WIDTH, :] = top
            o_ref[SB_WIDTH:, :] = bottom

        @pl.when(kk > 0)
        def _():
            o_ref[0:SB_WIDTH, :] += top
            o_ref[SB_WIDTH:, :] += bottom

    return pl.pallas_call(
        body, name="dw_out", grid=(t // tk,),
        in_specs=[pl.BlockSpec((tk, SB_WIDTH), lambda kk: (kk, 0)), pl.BlockSpec((tk, GLA_WIDTH), lambda kk: (kk, 0)),
                  pl.BlockSpec((tk, D_MODEL), lambda kk: (kk, 0))],
        out_specs=pl.BlockSpec((D_MODEL, D_MODEL), lambda kk: (0, 0)),
        out_shape=jax.ShapeDtypeStruct((D_MODEL, D_MODEL), F32),
        compiler_params=_cparams("arbitrary"),
    )(sb_o, gla_o, dr1)


def _dw_in(x, pieces, t, tm=512, tk=512):
    offs = _col_offsets(pieces)
    npc = len(pieces)
    n = offs[-1] + pieces[-1].shape[1]
    nk = t // tk

    def body(*refs):
        x_ref, p_refs, o_ref, acc_ref = refs[0], refs[1:1 + npc], refs[1 + npc], refs[2 + npc]
        kk = pl.program_id(1)
        xb = x_ref[...].astype(BF16)
        for p_ref, off in zip(p_refs, offs):
            cols = slice(off, off + p_ref.shape[1])
            part = _dot_tn(xb, p_ref[...].astype(BF16))
            if nk == 1:
                o_ref[:, cols] = part
                continue

            @pl.when(kk == 0)
            def _():
                acc_ref[:, cols] = part

            @pl.when((kk > 0) & (kk < nk - 1))
            def _():
                acc_ref[:, cols] += part

            @pl.when(kk == nk - 1)
            def _():
                o_ref[:, cols] = acc_ref[:, cols] + part

    return pl.pallas_call(
        body, name="dw_in", grid=(D_MODEL // tm, nk),
        in_specs=[pl.BlockSpec((tk, tm), lambda i, kk: (kk, i))]
        + [pl.BlockSpec((tk, a.shape[1]), lambda i, kk: (kk, 0)) for a in pieces],
        out_specs=pl.BlockSpec((tm, n), lambda i, kk: (i, 0)),
        out_shape=jax.ShapeDtypeStruct((D_MODEL, n), F32),
        scratch_shapes=[pltpu.VMEM((tm, n), F32)],
        compiler_params=_cparams("parallel", "arbitrary"),
    )(x, *pieces)


class _Rider:
    def __init__(self, inputs, out_shapes, sems, copies, aliases=()):
        self.inputs, self.out_shapes, self.sems, self.copies = list(inputs), list(out_shapes), tuple(sems), copies
        self.aliases = tuple(aliases)


def _join(a, b):
    na_in, na_out, na_sems = len(a.inputs), len(a.out_shapes), len(a.sems)

    def copies(ins, outs, sems):
        first = a.copies(ins[:na_in], outs[:na_out], sems[:na_sems])
        second = b.copies(ins[na_in:], outs[na_out:], sems[na_sems:])
        return tuple(u + v for u, v in zip(first, second))

    return _Rider(a.inputs + b.inputs, a.out_shapes + b.out_shapes, a.sems + b.sems, copies,
                  a.aliases + tuple((i + na_in, o + na_out) for i, o in b.aliases))


def _ride(rider, in_refs, out_refs, sems, first, last):
    @pl.when(first)
    def _():
        local, sends, _ = rider.copies(in_refs, out_refs, sems)
        for cp in local + sends:
            cp.start()

    @pl.when(last)
    def _():
        local, sends, recvs = rider.copies(in_refs, out_refs, sems)
        for cp in recvs:
            cp.wait_recv()
        for cp in sends:
            cp.wait_send()
        for cp in local:
            cp.wait()


def _run(rider, name):
    n_in, n_out = len(rider.inputs), len(rider.out_shapes)

    def body(*refs):
        local, sends, recvs = rider.copies(refs[:n_in], refs[n_in:n_in + n_out], refs[n_in + n_out:])
        for cp in local + sends:
            cp.start()
        for cp in recvs:
            cp.wait_recv()
        for cp in sends:
            cp.wait_send()
        for cp in local:
            cp.wait()

    return pl.pallas_call(
        body, name=name, in_specs=[HBM_SPEC] * n_in, out_specs=[HBM_SPEC] * n_out, out_shape=list(rider.out_shapes),
        scratch_shapes=[pltpu.SemaphoreType.DMA((s,)) for s in rider.sems],
    )(*rider.inputs)


def _sb_tile(qh, kj, diag, strict, u_strict, r_in):
    z = _dot_nt(qh, kj)
    sp = _softplus(z)
    l1m = -sp
    lsz = z - sp
    if diag:
        l1m = jnp.where(strict, l1m, 0.0)
    cs = _dot(l1m.astype(BF16), u_strict) + r_in
    w = jnp.exp(lsz + cs)
    if diag:
        w = jnp.where(strict, w, 0.0)
    return l1m, lsz, w


def _sb_consts():
    row = lax.broadcasted_iota(jnp.int32, (SB_BLK, SB_BLK), 0)
    col = lax.broadcasted_iota(jnp.int32, (SB_BLK, SB_BLK), 1)
    strict = col < row
    u_strict = (row > col).astype(BF16)
    u_pre = (row < col).astype(BF16)
    lane = lax.broadcasted_iota(jnp.int32, (1, LANES), 1)
    return strict, u_strict, u_pre, lane


def _sb_fwd(proj, t, rider):
    nq = t // SB_BLK
    n_in, n_out = len(rider.inputs), len(rider.out_shapes)

    def body(q_ref, k_ref, v_ref, *rest):
        rin, o_ref, sv_ref, rout = rest[:n_in], rest[n_in], rest[n_in + 1], rest[n_in + 2:n_in + 2 + n_out]
        p = pl.program_id(0)
        i = pl.program_id(1)
        _ride(rider, rin, rout, rest[n_in + 2 + n_out:], (p == 0) & (i == 0), (p == 3) & (i == nq - 1))

        strict, u_strict, _, lane = _sb_consts()
        qf = q_ref[...] * SB_SCALE
        hms = [(lane // 64) == hh for hh in range(2)]
        qhs = [jnp.where(hm, qf, 0.0).astype(BF16) for hm in hms]

        def step(j, r0, r1, a, sv, diag, keep=None):
            rows = pl.ds(pl.multiple_of(j * SB_BLK, SB_BLK), SB_BLK)
            kj = k_ref[rows, :].astype(BF16)
            vf = v_ref[rows, :]
            rs = []
            for hh, r in enumerate((r0, r1)):
                l1m, _, w = _sb_tile(qhs[hh], kj, diag, strict, u_strict, r)
                pv = _dot(w.astype(BF16), jnp.where(hms[hh], vf, 0.0).astype(BF16))
                mass = jnp.sum(l1m, axis=1, keepdims=True)
                if keep is not None:
                    pv, mass = jnp.where(keep, pv, 0.0), jnp.where(keep, mass, 0.0)
                a = a + pv
                rs.append(r + mass)
                sv = jnp.where(lane == hh * 64 + (i - j + 1), rs[hh], sv)
            return rs[0], rs[1], a, sv

        zero = jnp.zeros((SB_BLK, 1), F32)
        acc0 = jnp.zeros((SB_BLK, LANES), F32)
        r0, r1, acc, sv = step(i, zero, zero, acc0, acc0, True)
        r0, r1, acc, sv = step(jnp.maximum(i - 1, 0), r0, r1, acc, sv, False, keep=i > 0)
        _, _, _, acc, sv = lax.while_loop(
            lambda c: (c[0] >= 0) & (jnp.maximum(jnp.max(c[1]), jnp.max(c[2])) > SB_CUT),
            lambda c: (c[0] - 1,) + step(c[0], c[1], c[2], c[3], c[4], False),
            (i - 2, r0, r1, acc, sv))
        o_ref[...] = acc.astype(BF16)
        sv_ref[...] = sv

    return pl.pallas_call(
        body, name="sb_fwd", grid=(4, nq),
        in_specs=[pl.BlockSpec((SB_BLK, LANES), lambda p, i: (i, p)),
                  pl.BlockSpec((t, LANES), lambda p, i: (0, 4 + p)),
                  pl.BlockSpec((t, LANES), lambda p, i: (0, 8 + p))] + [HBM_SPEC] * n_in,
        out_specs=[pl.BlockSpec((SB_BLK, LANES), lambda p, i: (i, p))] * 2 + [HBM_SPEC] * n_out,
        out_shape=[jax.ShapeDtypeStruct((t, SB_WIDTH), BF16), jax.ShapeDtypeStruct((t, SB_WIDTH), F32)] + rider.out_shapes,
        scratch_shapes=[pltpu.SemaphoreType.DMA((s,)) for s in rider.sems],
        compiler_params=_cparams("arbitrary", "arbitrary"),
    )(proj, proj, proj, *rider.inputs)


def _sb_bwd(proj, dcat, mass, t, rider):
    nq = t // SB_BLK
    n_in, n_out = len(rider.inputs), len(rider.out_shapes)

    def body(q_ref, k_ref, v_ref, do_ref, sv_ref, *rest):
        rin = rest[:n_in]
        dq_ref, dk_ref, dv_ref = rest[n_in:n_in + 3]
        rout = rest[n_in + 3:n_in + 3 + n_out]
        p = pl.program_id(0)
        i = pl.program_id(1)
        _ride(rider, rin, rout, rest[n_in + 3 + n_out:], (p == 0) & (i == 0), (p == 3) & (i == nq - 1))

        @pl.when(i == 0)
        def _():
            dk_ref[...] = jnp.zeros_like(dk_ref)
            dv_ref[...] = jnp.zeros_like(dv_ref)

        strict, u_strict, u_pre, lane = _sb_consts()
        qf = q_ref[...] * SB_SCALE
        dof = do_ref[...]
        hms = [(lane // 64) == hh for hh in range(2)]
        qhs = [jnp.where(hm, qf, 0.0).astype(BF16) for hm in hms]
        dohs = [jnp.where(hm, dof, 0.0).astype(BF16) for hm in hms]

        sv = sv_ref[...]
        zero = jnp.zeros((SB_BLK, 1), F32)

        def mass_right(hh, d):
            return jnp.sum(jnp.where(lane == hh * 64 + d, sv, 0.0), axis=1, keepdims=True)

        dstop = lax.while_loop(
            lambda d: (i - d >= 0) & (jnp.maximum(jnp.max(mass_right(0, d)), jnp.max(mass_right(1, d))) > SB_CUT),
            lambda d: d + 1, 1)
        jstop = i - dstop

        def step(j, carry, diag, keep=None):
            pre_g0, pre_g1, dqa = carry
            rows = pl.ds(pl.multiple_of(j * SB_BLK, SB_BLK), SB_BLK)
            kf = k_ref[rows, :]
            kj = kf.astype(BF16)
            vj = v_ref[rows, :].astype(BF16)
            dv = jnp.zeros((SB_BLK, LANES), F32)
            dk = jnp.zeros((SB_BLK, LANES), F32)
            dqj = jnp.zeros((SB_BLK, LANES), F32)
            pre = []
            for hh, pre_g in enumerate((pre_g0, pre_g1)):
                _, lsz, w = _sb_tile(qhs[hh], kj, diag, strict, u_strict, zero if diag else mass_right(hh, i - j))
                g = w * _dot_nt(dohs[hh], vj)
                gpre = _dot(g.astype(BF16), u_pre) + pre_g
                sig = jnp.exp(lsz)
                dz = g * (1.0 - sig) - gpre * sig
                if diag:
                    dz = jnp.where(strict, dz, 0.0)
                dzb = dz.astype(BF16)
                dv = dv + _dot_tn(w.astype(BF16), dohs[hh])
                dk = dk + _dot_tn(dzb, qhs[hh])
                dqj = dqj + _dot(dzb, jnp.where(hms[hh], kf, 0.0).astype(BF16))
                gsum = jnp.sum(g, axis=1, keepdims=True)
                pre.append(pre_g + (gsum if keep is None else jnp.where(keep, gsum, 0.0)))
            if keep is not None:
                dv, dk, dqj = jnp.where(keep, dv, 0.0), jnp.where(keep, dk, 0.0), jnp.where(keep, dqj, 0.0)
            dv_ref[rows, :] += dv
            dk_ref[rows, :] += dk
            return pre[0], pre[1], dqa + dqj

        carry = lax.fori_loop(jstop + 1, i - 1, lambda j, c: step(j, c, False), (zero, zero, jnp.zeros((SB_BLK, LANES), F32)))
        carry = step(jnp.maximum(i - 1, 0), carry, False, keep=i > 0)
        _, _, dq = step(i, carry, True)
        dq_ref[...] = (dq * SB_SCALE).astype(BF16)

    return pl.pallas_call(
        body, name="sb_bwd", grid=(4, nq),
        in_specs=[pl.BlockSpec((SB_BLK, LANES), lambda p, i: (i, p)),
                  pl.BlockSpec((t, LANES), lambda p, i: (0, 4 + p)),
                  pl.BlockSpec((t, LANES), lambda p, i: (0, 8 + p)),
                  pl.BlockSpec((SB_BLK, LANES), lambda p, i: (i, p)),
                  pl.BlockSpec((SB_BLK, LANES), lambda p, i: (i, p))] + [HBM_SPEC] * n_in,
        out_specs=[pl.BlockSpec((SB_BLK, LANES), lambda p, i: (i, p)),
                   pl.BlockSpec((t, LANES), lambda p, i: (0, p)),
                   pl.BlockSpec((t, LANES), lambda p, i: (0, p))] + [HBM_SPEC] * n_out,
        out_shape=[jax.ShapeDtypeStruct((t, SB_WIDTH), BF16)] + [jax.ShapeDtypeStruct((t, SB_WIDTH), F32)] * 2
        + rider.out_shapes,
        scratch_shapes=[pltpu.SemaphoreType.DMA((s,)) for s in rider.sems],
        compiler_params=_cparams("arbitrary", "arbitrary"),
    )(proj, proj, proj, dcat, mass, *rider.inputs)


def _gla_consts():
    r = lax.broadcasted_iota(jnp.int32, (GLA_ROWS, GLA_ROWS), 0)
    c = lax.broadcasted_iota(jnp.int32, (GLA_ROWS, GLA_ROWS), 1)
    same = (r // CHUNK) == (c // CHUNK)
    causal = same & (c <= r)
    upto_mid = c % CHUNK <= CHUNK // 2 - 1
    fwd_stack = jnp.concatenate([causal, same & upto_mid, same], axis=0).astype(BF16)
    bwd_stack = jnp.concatenate([same & (c >= r), same & (r % CHUNK <= CHUNK // 2 - 1), same], axis=1).astype(BF16)
    rowid = lax.broadcasted_iota(jnp.int32, (GLA_ROWS, 1), 0)
    lane = lax.broadcasted_iota(jnp.int32, (1, LANES), 1)
    sr = lax.broadcasted_iota(jnp.int32, (LANES, 2 * LANES), 0)
    sc = lax.broadcasted_iota(jnp.int32, (LANES, 2 * LANES), 1)
    blockdiag = (sr // 64) == (sc // LANES)
    return causal, fwd_stack, bwd_stack, rowid, lane, blockdiag


def _dot3(u, x):
    hi, mid, lo = _split3(x)
    return _dot(u, hi) + _dot(u, mid) + _dot(u, lo)


def _row_to_col(row):
    return jnp.transpose(jnp.broadcast_to(row, (LANES, LANES)))


def _gla_gates(ga_ref, gup_ref, gbias_ref):
    pre = _dot(ga_ref[...].astype(BF16), gup_ref[...].astype(BF16)) + gbias_ref[...]
    log_a = (jnp.minimum(pre, 0.0) - jnp.log(1.0 + jnp.exp(-jnp.abs(pre)))) / GLA_TAU
    return pre, log_a


def _gla_terms(g2, q2, k2, fwd_stack):
    bs = _dot3(fwd_stack, g2)
    b, b_ref, b_last = bs[:GLA_ROWS], bs[GLA_ROWS:2 * GLA_ROWS], bs[2 * GLA_ROWS:]
    qs = q2 * GLA_SCALE
    e_q = jnp.exp(b - b_ref)
    e_k = jnp.exp(b_ref - b)
    e_d = jnp.exp(b_last - b)
    e_b = jnp.exp(b)
    decay = jnp.exp(b_last)
    return dict(qs=qs, e_q=e_q, e_k=e_k, e_d=e_d, e_b=e_b, decay=decay,
                qi=qs * e_q, ki=k2 * e_k, kd=k2 * e_d, qb=qs * e_b)


def _gla_fwd(proj, gate_up_pad, gate_bias, gnorm, t, rider):
    nsteps = t // GLA_ROWS
    cps = GLA_ROWS // CHUNK
    n_in, n_out = len(rider.inputs), len(rider.out_shapes)

    def body(q_ref, k_ref, v_ref, gg_ref, ga_ref, gup_ref, gbias_ref, gn_ref, *rest):
        o_ref, oraw_ref, st_ref = rest[n_in:n_in + 3]
        s_scr = rest[n_in + 3 + n_out]
        i = pl.program_id(0)
        _ride(rider, rest[:n_in], rest[n_in + 3:n_in + 3 + n_out], rest[n_in + 4 + n_out:], i == 0, i == nsteps - 1)

        @pl.when(i == 0)
        def _():
            s_scr[...] = jnp.zeros_like(s_scr)

        _, fwd_stack, _, _, lane, blockdiag = _gla_consts()
        causal64 = (lax.broadcasted_iota(jnp.int32, (CHUNK, CHUNK), 1) <= lax.broadcasted_iota(jnp.int32, (CHUNK, CHUNK), 0))
        _, log_a = _gla_gates(ga_ref, gup_ref, gbias_ref)
        gn = gn_ref[...]
        pre = []
        for p in range(2):
            kl = slice(p * LANES, (p + 1) * LANES)
            tm = _gla_terms(log_a[:, kl], q_ref[:, kl], k_ref[:, kl], fwd_stack)
            pre.append(dict(
                v2b=v_ref[:, p * 2 * LANES:(p + 1) * 2 * LANES].astype(BF16), kib=tm["ki"].astype(BF16),
                kdb=tm["kd"].astype(BF16), qbb=tm["qb"].astype(BF16), decay=tm["decay"],
                qihb=[jnp.where((lane // 64) == hh, tm["qi"], 0.0).astype(BF16) for hh in range(2)]))
        state = [s_scr[0], s_scr[1]]
        outs = [[], []]
        for cc in range(cps):
            rows = slice(cc * CHUNK, (cc + 1) * CHUNK)
            for p in range(2):
                w = pre[p]
                st_ref[cc, p] = state[p]
                intra = []
                for hh in range(2):
                    a = jnp.where(causal64, _dot_nt(w["qihb"][hh][rows], w["kib"][rows]), 0.0)
                    intra.append(_dot(a.astype(BF16), w["v2b"][rows, hh * LANES:(hh + 1) * LANES]))
                outs[p].append(jnp.concatenate(intra, axis=1) + _dot(w["qbb"][rows], state[p].astype(BF16)))
                upd = jnp.where(blockdiag, _dot_tn(w["kdb"][rows], w["v2b"][rows]), 0.0)
                dcol = _row_to_col(w["decay"][cc * CHUNK:cc * CHUNK + 1])
                state[p] = state[p] * jnp.concatenate([dcol, dcol], axis=1) + upd
        for p in range(2):
            s_scr[p] = state[p]
            vl = slice(p * 2 * LANES, (p + 1) * 2 * LANES)
            o2 = jnp.concatenate(outs[p], axis=0)
            oraw_ref[:, vl] = o2
            for hh in range(2):
                oh = o2[:, hh * LANES:(hh + 1) * LANES]
                gl = slice(p * 2 * LANES + hh * LANES, p * 2 * LANES + (hh + 1) * LANES)
                rinv = lax.rsqrt(jnp.mean(oh * oh, axis=1, keepdims=True) + RMS_EPS)
                gg = gg_ref[:, gl]
                o_ref[:, gl] = (oh * rinv * gn * (gg * _sigmoid(gg))).astype(BF16)

    cb = lambda w, idx: pl.BlockSpec((GLA_ROWS, w), lambda i: (i, idx))
    full = lambda shp: pl.BlockSpec(shp, lambda i: tuple(0 for _ in shp))
    return pl.pallas_call(
        body, name="gla_fwd", grid=(nsteps,),
        in_specs=[cb(256, 6), cb(256, 7), cb(512, 4), cb(512, 5), cb(128, 24),
                  full((LANES, GLA_KEYS)), full((1, GLA_KEYS)), full((1, LANES))] + [HBM_SPEC] * n_in,
        out_specs=[pl.BlockSpec((GLA_ROWS, GLA_WIDTH), lambda i: (i, 0)),
                   pl.BlockSpec((GLA_ROWS, GLA_WIDTH), lambda i: (i, 0)),
                   pl.BlockSpec((cps, 2, LANES, 2 * LANES), lambda i: (i, 0, 0, 0))] + [HBM_SPEC] * n_out,
        out_shape=[jax.ShapeDtypeStruct((t, GLA_WIDTH), BF16), jax.ShapeDtypeStruct((t, GLA_WIDTH), F32),
                   jax.ShapeDtypeStruct((t // CHUNK, 2, LANES, 2 * LANES), F32)]
        + rider.out_shapes,
        input_output_aliases={8 + i: 3 + o for i, o in rider.aliases},
        scratch_shapes=[pltpu.VMEM((2, LANES, 2 * LANES), F32)] + [pltpu.SemaphoreType.DMA((s,)) for s in rider.sems],
        compiler_params=_cparams("arbitrary"),
    )(proj, proj, proj, proj, proj, gate_up_pad, gate_bias, gnorm, *rider.inputs)


def _gla_bwd(proj, dcat, oraw, states, gate_up_pad, gate_bias, gnorm, t, rider):
    nsteps = t // GLA_ROWS
    cps = GLA_ROWS // CHUNK
    wout = 2 * GLA_KEYS + 2 * GLA_WIDTH + LANES
    n_in, n_out = len(rider.inputs), len(rider.out_shapes)

    def body(q_ref, k_ref, v_ref, gg_ref, ga_ref, do_ref, oraw_ref, st_ref, gup_ref, gbias_ref, gn_ref, *rest):
        d_ref, dgup_ref, dgbias_ref, dgn_ref = rest[n_in:n_in + 4]
        ds_scr = rest[n_in + 4 + n_out]
        i = pl.program_id(0)
        _ride(rider, rest[:n_in], rest[n_in + 4:n_in + 4 + n_out], rest[n_in + 5 + n_out:], i == 0, i == nsteps - 1)

        @pl.when(i == 0)
        def _():
            ds_scr[...] = jnp.zeros_like(ds_scr)
            dgup_ref[...] = jnp.zeros_like(dgup_ref)
            dgbias_ref[...] = jnp.zeros_like(dgbias_ref)
            dgn_ref[...] = jnp.zeros_like(dgn_ref)

        causal, fwd_stack, bwd_stack, rowid, lane, blockdiag = _gla_consts()
        pre, log_a = _gla_gates(ga_ref, gup_ref, gbias_ref)
        gn = gn_ref[...]
        dgn = jnp.zeros((1, LANES), F32)
        dgs = []
        pre_p = []
        for p in range(2):
            kl = slice(p * LANES, (p + 1) * LANES)
            vl = slice(p * 2 * LANES, (p + 1) * 2 * LANES)
            tm = _gla_terms(log_a[:, kl], q_ref[:, kl], k_ref[:, kl], fwd_stack)
            v2b = v_ref[:, vl].astype(BF16)
            dos = []
            for hh in range(2):
                gl = slice(p * 2 * LANES + hh * LANES, p * 2 * LANES + (hh + 1) * LANES)
                oh = oraw_ref[:, gl]
                rinv = lax.rsqrt(jnp.mean(oh * oh, axis=1, keepdims=True) + RMS_EPS)
                on = oh * rinv
                gg = gg_ref[:, gl]
                sg = _sigmoid(gg)
                sil = gg * sg
                dgo = do_ref[:, gl]
                d_ref[:, 2 * GLA_KEYS + GLA_WIDTH + gl.start:2 * GLA_KEYS + GLA_WIDTH + gl.stop] = (
                    dgo * on * gn * (sg * (1.0 + gg * (1.0 - sg)))).astype(BF16)
                dgn = dgn + jnp.sum(dgo * sil * on, axis=0, keepdims=True)
                don = dgo * sil * gn
                dos.append(rinv * (don - on * jnp.mean(don * on, axis=1, keepdims=True)))
            do2b = jnp.concatenate(dos, axis=1).astype(BF16)
            qib = tm["qi"].astype(BF16)
            kib = tm["ki"].astype(BF16)
            kdb = tm["kd"].astype(BF16)
            qbb = tm["qb"].astype(BF16)
            dqi = jnp.zeros((GLA_ROWS, LANES), F32)
            dki = jnp.zeros((GLA_ROWS, LANES), F32)
            dvs = []
            for hh in range(2):
                hm = (lane // 64) == hh
                hl = slice(hh * LANES, (hh + 1) * LANES)
                a = jnp.where(causal, _dot_nt(jnp.where(hm, tm["qi"], 0.0).astype(BF16), kib), 0.0).astype(BF16)
                da = jnp.where(causal, _dot_nt(do2b[:, hl], v2b[:, hl]), 0.0).astype(BF16)
                dvs.append(_dot_tn(a, do2b[:, hl]))
                dqi = dqi + jnp.where(hm, _dot(da, kib), 0.0)
                dki = dki + jnp.where(hm, _dot_tn(da, qib), 0.0)
            pre_p.append(dict(tm=tm, v2b=v2b, do2b=do2b, kdb=kdb, qbb=qbb, dqi=dqi, dki=dki, dvs=dvs))
        dstate = [ds_scr[0], ds_scr[1]]
        rec = [dict(dv_st=[None] * cps, dqb=[None] * cps, dkd=[None] * cps, dd=[None] * cps) for _ in range(2)]
        for cc in reversed(range(cps)):
            rows = slice(cc * CHUNK, (cc + 1) * CHUNK)
            for p in range(2):
                w, ds2 = pre_p[p], dstate[p]
                s_prev = st_ref[cc, p]
                ds2b = ds2.astype(BF16)
                rec[p]["dv_st"][cc] = _dot(w["kdb"][rows], ds2b)
                rec[p]["dqb"][cc] = _dot_nt(w["do2b"][rows], s_prev.astype(BF16))
                rec[p]["dkd"][cc] = _dot_nt(w["v2b"][rows], ds2b)
                decay_row = w["tm"]["decay"][cc * CHUNK:cc * CHUNK + 1]
                ddecay_col = jnp.sum(ds2 * s_prev, axis=1, keepdims=True)
                ddecay_row = jnp.transpose(jnp.broadcast_to(ddecay_col, (LANES, LANES)))[0:1, :]
                rec[p]["dd"][cc] = jnp.broadcast_to(ddecay_row * decay_row, (CHUNK, LANES))
                dcol = _row_to_col(decay_row)
                dstate[p] = (jnp.where(blockdiag, _dot_tn(w["qbb"][rows], w["do2b"][rows]), 0.0)
                             + ds2 * jnp.concatenate([dcol, dcol], axis=1))
        for p in range(2):
            ds_scr[p] = dstate[p]
            tm, dqi, dki = pre_p[p]["tm"], pre_p[p]["dqi"], pre_p[p]["dki"]
            dv2 = jnp.concatenate(pre_p[p]["dvs"], axis=1) + jnp.concatenate(rec[p]["dv_st"], axis=0)
            dqb = jnp.concatenate(rec[p]["dqb"], axis=0)
            dkd = jnp.concatenate(rec[p]["dkd"], axis=0)
            dd = rec[p]["dd"]
            dqs = dqi * tm["e_q"] + dqb * tm["e_b"]
            dk = dki * tm["e_k"] + dkd * tm["e_d"]
            t_qi = dqi * tm["qi"]
            t_ki = dki * tm["ki"]
            t_kd = dkd * tm["kd"]
            db = t_qi - t_ki + dqb * tm["qb"] - t_kd
            to_mid = t_ki - t_qi
            to_last = t_kd + jnp.where(rowid % CHUNK == CHUNK - 1, jnp.concatenate(dd, axis=0), 0.0)
            dgs.append(_dot3(bwd_stack, jnp.concatenate([db, to_mid, to_last], axis=0)))
            d_ref[:, p * LANES:(p + 1) * LANES] = (dqs * GLA_SCALE).astype(BF16)
            d_ref[:, GLA_KEYS + p * LANES:GLA_KEYS + (p + 1) * LANES] = dk.astype(BF16)
            d_ref[:, 2 * GLA_KEYS + p * 2 * LANES:2 * GLA_KEYS + (p + 1) * 2 * LANES] = dv2.astype(BF16)
        dlog_a = jnp.concatenate(dgs, axis=1)
        dpre = dlog_a * (1.0 / GLA_TAU) * _sigmoid(-pre)
        dpb = dpre.astype(BF16)
        dgn_ref[...] += dgn
        dgbias_ref[...] += jnp.sum(dpre, axis=0, keepdims=True)
        dgup_ref[...] += _dot_tn(ga_ref[...].astype(BF16), dpb)
        d_ref[:, 2 * GLA_KEYS + 2 * GLA_WIDTH:] = _dot_nt(dpb, gup_ref[...].astype(BF16)).astype(BF16)

    rev = lambda i: nsteps - 1 - i
    cb = lambda w, idx: pl.BlockSpec((GLA_ROWS, w), lambda i: (rev(i), idx))
    full = lambda shp: pl.BlockSpec(shp, lambda i: tuple(0 for _ in shp))
    return pl.pallas_call(
        body, name="gla_bwd", grid=(nsteps,),
        in_specs=[cb(256, 6), cb(256, 7), cb(512, 4), cb(512, 5), cb(128, 24), cb(512, 1), cb(512, 0),
                  pl.BlockSpec((cps, 2, LANES, 2 * LANES), lambda i: (rev(i), 0, 0, 0)),
                  full((LANES, GLA_KEYS)), full((1, GLA_KEYS)), full((1, LANES))] + [HBM_SPEC] * n_in,
        out_specs=[pl.BlockSpec((GLA_ROWS, wout), lambda i: (rev(i), 0)),
                   full((LANES, GLA_KEYS)), full((1, GLA_KEYS)), full((1, LANES))] + [HBM_SPEC] * n_out,
        out_shape=[jax.ShapeDtypeStruct((t, wout), BF16), jax.ShapeDtypeStruct((LANES, GLA_KEYS), F32),
                   jax.ShapeDtypeStruct((1, GLA_KEYS), F32), jax.ShapeDtypeStruct((1, LANES), F32)] + rider.out_shapes,
        scratch_shapes=[pltpu.VMEM((2, LANES, 2 * LANES), F32)] + [pltpu.SemaphoreType.DMA((s,)) for s in rider.sems],
        compiler_params=_cparams("arbitrary"),
    )(proj, proj, proj, proj, proj, dcat, oraw, states, gate_up_pad, gate_bias, gnorm, *rider.inputs)


def _ln_stats(r):
    mu = jnp.mean(r, axis=1, keepdims=True)
    xc = r - mu
    rstd = lax.rsqrt(jnp.mean(xc * xc, axis=1, keepdims=True) + LN_EPS)
    return xc * rstd, rstd


def _ln_bwd(dy_g, xhat, rstd):
    return rstd * (dy_g - jnp.mean(dy_g, axis=1, keepdims=True) - xhat * jnp.mean(dy_g * xhat, axis=1, keepdims=True))


def _outproj_ln1(sb_o, gla_o, x, w_out, g1, b1, t, tm=512):
    def body(sb_ref, gl_ref, x_ref, w_ref, g_ref, b_ref, xhat_ref, rstd_ref, h_ref):
        mix = _dot(sb_ref[...], w_ref[0:SB_WIDTH, :]) + _dot(gl_ref[...], w_ref[SB_WIDTH:, :])
        xhat, rstd = _ln_stats(ALPHA * x_ref[...] + mix)
        xhat_ref[...] = xhat
        rstd_ref[...] = rstd
        h_ref[...] = (xhat * g_ref[...] + b_ref[...]).astype(BF16)

    row = lambda w: pl.BlockSpec((tm, w), lambda i: (i, 0))
    full = lambda shp: pl.BlockSpec(shp, lambda i: (0, 0))
    return pl.pallas_call(
        body, name="outproj_ln1", grid=(t // tm,),
        in_specs=[row(SB_WIDTH), row(GLA_WIDTH), row(D_MODEL), full((D_MODEL, D_MODEL)), full((1, D_MODEL)), full((1, D_MODEL))],
        out_specs=[row(D_MODEL), row(1), row(D_MODEL)],
        out_shape=[jax.ShapeDtypeStruct((t, D_MODEL), F32), jax.ShapeDtypeStruct((t, 1), F32),
                   jax.ShapeDtypeStruct((t, D_MODEL), BF16)],
        compiler_params=_cparams("parallel"),
    )(sb_o, gla_o, x, w_out, g1, b1)


_INV_SQRT2 = 1.0 / math.sqrt(2.0)
_INV_SQRT2PI = 1.0 / math.sqrt(2.0 * math.pi)


def _conv3(xs, w_ref, b_ref, half):
    return (w_ref[half, 0:1, :] * pltpu.roll(xs, 2, 0) + w_ref[half, 1:2, :] * pltpu.roll(xs, 1, 0)
            + w_ref[half, 2:3, :] * xs + b_ref[half])


HALO = 16


def _conv_gelu_fwd(up3, conv_w3, conv_b3, t, tr=512, ct=256):
    nct = D_FF // ct
    hb = tr // HALO

    def body(cur_ref, prev_ref, w_ref, b_ref, gm_ref):
        i = pl.program_id(1)
        keep = (i > 0).astype(F32)
        us = []
        for half in range(2):
            xs = jnp.concatenate([prev_ref[half].astype(F32) * keep, cur_ref[half].astype(F32)], axis=0)
            us.append(_conv3(xs, w_ref, b_ref, half)[HALO:, :])
        a, c = us
        gelu = 0.5 * a * (1.0 + lax.erf(a * _INV_SQRT2))
        gm_ref[...] = (gelu * c).astype(BF16)

    return pl.pallas_call(
        body, name="conv_gelu_fwd", grid=(nct, t // tr),
        in_specs=[pl.BlockSpec((2, tr, ct), lambda j, i: (0, i, j)),
                  pl.BlockSpec((2, HALO, ct), lambda j, i: (0, jnp.maximum(i * hb - 1, 0), j)),
                  pl.BlockSpec((2, 3, ct), lambda j, i: (0, 0, j)),
                  pl.BlockSpec((2, 1, ct), lambda j, i: (0, 0, j))],
        out_specs=pl.BlockSpec((tr, ct), lambda j, i: (i, j)),
        out_shape=jax.ShapeDtypeStruct((t, D_FF), BF16),
        compiler_params=_cparams("parallel", "parallel"),
    )(up3, up3, conv_w3, conv_b3)


def _conv_gelu_bwd(up3, dgm, conv_w3, conv_b3, t, tr=512, ct=256):
    nct = D_FF // ct
    nrt = t // tr
    hb = tr // HALO
    n = tr + 2 * HALO
    lo, hi = HALO, tr + HALO

    def body(cur_ref, prev_ref, next_ref, dg_ref, dgn_ref, w_ref, b_ref, dup_ref, dcw_ref, dcb_ref):
        i = pl.program_id(1)

        @pl.when(i == 0)
        def _():
            dcw_ref[...] = jnp.zeros_like(dcw_ref)
            dcb_ref[...] = jnp.zeros_like(dcb_ref)

        keep_prev = (i > 0).astype(F32)
        keep_next = (i < nrt - 1).astype(F32)
        xs, xm1, xm2, us = [], [], [], []
        for half in range(2):
            x = jnp.concatenate([prev_ref[half].astype(F32) * keep_prev, cur_ref[half].astype(F32),
                                 next_ref[half].astype(F32)], axis=0)
            xs.append(x)
            xm1.append(pltpu.roll(x, 1, 0))
            xm2.append(pltpu.roll(x, 2, 0))
            us.append(w_ref[half, 0:1, :] * xm2[half] + w_ref[half, 1:2, :] * xm1[half]
                      + w_ref[half, 2:3, :] * x + b_ref[half])
        a, c = us
        dg = jnp.concatenate([jnp.zeros((HALO, ct), F32), dg_ref[...].astype(F32),
                              dgn_ref[...].astype(F32) * keep_next], axis=0)
        cdf = 0.5 * (1.0 + lax.erf(a * _INV_SQRT2))
        pdf = jnp.exp(-0.5 * a * a) * _INV_SQRT2PI
        dus = [dg * c * (cdf + a * pdf), dg * (a * cdf)]
        rid = lax.broadcasted_iota(jnp.int32, (8, 1), 0)
        for half in range(2):
            du = dus[half]
            dup = (w_ref[half, 2:3, :] * du + w_ref[half, 1:2, :] * pltpu.roll(du, n - 1, 0)
                   + w_ref[half, 0:1, :] * pltpu.roll(du, n - 2, 0))
            dup_ref[half] = dup[lo:hi, :].astype(BF16)
            duc = du[lo:hi, :]
            s0 = jnp.sum(duc * xm2[half][lo:hi, :], axis=0, keepdims=True)
            s1 = jnp.sum(duc * xm1[half][lo:hi, :], axis=0, keepdims=True)
            s2 = jnp.sum(duc * xs[half][lo:hi, :], axis=0, keepdims=True)
            dcw_ref[half] += jnp.where(rid == 0, s0, jnp.where(rid == 1, s1, jnp.where(rid == 2, s2, 0.0)))
            dcb_ref[half] += jnp.sum(duc, axis=0, keepdims=True)

    last = t // HALO - 1
    return pl.pallas_call(
        body, name="conv_gelu_bwd", grid=(nct, nrt),
        in_specs=[pl.BlockSpec((2, tr, ct), lambda j, i: (0, i, j)),
                  pl.BlockSpec((2, HALO, ct), lambda j, i: (0, jnp.maximum(i * hb - 1, 0), j)),
                  pl.BlockSpec((2, HALO, ct), lambda j, i: (0, jnp.minimum((i + 1) * hb, last), j)),
                  pl.BlockSpec((tr, ct), lambda j, i: (i, j)),
                  pl.BlockSpec((HALO, ct), lambda j, i: (jnp.minimum((i + 1) * hb, last), j)),
                  pl.BlockSpec((2, 3, ct), lambda j, i: (0, 0, j)),
                  pl.BlockSpec((2, 1, ct), lambda j, i: (0, 0, j))],
        out_specs=[pl.BlockSpec((2, tr, ct), lambda j, i: (0, i, j)),
                   pl.BlockSpec((2, 8, ct), lambda j, i: (0, 0, j)),
                   pl.BlockSpec((2, 1, ct), lambda j, i: (0, 0, j))],
        out_shape=[jax.ShapeDtypeStruct((2, t, D_FF), BF16), jax.ShapeDtypeStruct((2, 8, D_FF), F32),
                   jax.ShapeDtypeStruct((2, 1, D_FF), F32)],
        compiler_params=_cparams("parallel", "arbitrary"),
    )(up3, up3, up3, dgm, dgm, conv_w3, conv_b3)


def _down_ln2_loss(gm, w_down, xhat1, g1, b1, g2, b2, target, t, tm=512):
    def body(gm_ref, w_ref, xh_ref, g1_ref, b1_ref, g2_ref, b2_ref, tg_ref, dr_ref, drb_ref, loss_ref, dg_ref, db_ref):
        i = pl.program_id(0)

        @pl.when(i == 0)
        def _():
            loss_ref[...] = jnp.zeros_like(loss_ref)
            dg_ref[...] = jnp.zeros_like(dg_ref)
            db_ref[...] = jnp.zeros_like(db_ref)

        h = xh_ref[...] * g1_ref[...] + b1_ref[...]
        xhat, rstd = _ln_stats(ALPHA * h + _dot(gm_ref[...], w_ref[...]))
        err = xhat * g2_ref[...] + b2_ref[...] - tg_ref[...]
        loss_ref[...] += 0.5 * jnp.sum(jnp.sum(err * err, axis=1, keepdims=True), axis=0, keepdims=True) / D_MODEL
        dy = err * (1.0 / D_MODEL)
        dg_ref[...] += jnp.sum(dy * xhat, axis=0, keepdims=True)
        db_ref[...] += jnp.sum(dy, axis=0, keepdims=True)
        dr = _ln_bwd(dy * g2_ref[...], xhat, rstd)
        dr_ref[...] = dr
        drb_ref[...] = dr.astype(BF16)

    row = lambda w: pl.BlockSpec((tm, w), lambda i: (i, 0))
    full = lambda shp: pl.BlockSpec(shp, lambda i: (0, 0))
    vec = full((1, D_MODEL))
    return pl.pallas_call(
        body, name="down_ln2_loss", grid=(t // tm,),
        in_specs=[row(D_FF), full((D_FF, D_MODEL)), row(D_MODEL), vec, vec, vec, vec, row(D_MODEL)],
        out_specs=[row(D_MODEL), row(D_MODEL), full((1, 1)), vec, vec],
        out_shape=[jax.ShapeDtypeStruct((t, D_MODEL), F32), jax.ShapeDtypeStruct((t, D_MODEL), BF16),
                   jax.ShapeDtypeStruct((1, 1), F32),
                   jax.ShapeDtypeStruct((1, D_MODEL), F32), jax.ShapeDtypeStruct((1, D_MODEL), F32)],
        compiler_params=_cparams("arbitrary"),
    )(gm, w_down, xhat1, g1, b1, g2, b2, target)


def _dh_ln1_bwd(dup3, w_up4, dr2, xhat1, rstd1, g1, t, tm=256):
    ws = 2 * D_FF // 4

    def body(a_ref, w_ref, dr2_ref, xh_ref, rs_ref, g_ref, dr1_ref, dg_ref, db_ref):
        i = pl.program_id(0)

        @pl.when(i == 0)
        def _():
            dg_ref[...] = jnp.zeros_like(dg_ref)
            db_ref[...] = jnp.zeros_like(db_ref)

        dh = ALPHA * dr2_ref[...]
        for s in range(4):
            dh = dh + _dot_nt(a_ref[s // 2, :, (s % 2) * ws:(s % 2 + 1) * ws], w_ref[s])
        xhat = xh_ref[...]
        dg_ref[...] += jnp.sum(dh * xhat, axis=0, keepdims=True)
        db_ref[...] += jnp.sum(dh, axis=0, keepdims=True)
        dr1_ref[...] = _ln_bwd(dh * g_ref[...], xhat, rs_ref[...])

    row = lambda w: pl.BlockSpec((tm, w), lambda i: (i, 0))
    vec = pl.BlockSpec((1, D_MODEL), lambda i: (0, 0))
    return pl.pallas_call(
        body, name="dh_ln1_bwd", grid=(t // tm,),
        in_specs=[pl.BlockSpec((2, tm, D_FF), lambda i: (0, i, 0)),
                  pl.BlockSpec((4, D_MODEL, ws), lambda i: (0, 0, 0)),
                  row(D_MODEL), row(D_MODEL), row(1), vec],
        out_specs=[row(D_MODEL), vec, vec],
        out_shape=[jax.ShapeDtypeStruct((t, D_MODEL), F32), jax.ShapeDtypeStruct((1, D_MODEL), F32),
                   jax.ShapeDtypeStruct((1, D_MODEL), F32)],
        compiler_params=_cparams("arbitrary"),
    )(dup3, w_up4, dr2, xhat1, rstd1, g1)


def _adamw(w, g, m, v, name):
    rows, cols = w.shape
    tr = rows
    for cand in (256, 128, 64, 32, 16, 8):
        if rows % cand == 0 and rows > cand:
            tr = cand
            break
    c1 = 1.0 / (1.0 - ADAM_B1 ** ADAM_STEP)
    c2 = 1.0 / (1.0 - ADAM_B2 ** ADAM_STEP)

    def body(w_ref, g_ref, m_ref, v_ref, d_ref, nm_ref, nv_ref):
        gv = g_ref[...]
        nm = ADAM_B1 * m_ref[...] + (1.0 - ADAM_B1) * gv
        nv = ADAM_B2 * v_ref[...] + (1.0 - ADAM_B2) * (gv * gv)
        d_ref[...] = -ADAM_LR * ((nm * c1) / (jnp.sqrt(nv * c2) + ADAM_EPS) + ADAM_WD * w_ref[...])
        nm_ref[...] = nm
        nv_ref[...] = nv

    spec = pl.BlockSpec((tr, cols), lambda i: (i, 0))
    out = jax.ShapeDtypeStruct((rows, cols), F32)
    return pl.pallas_call(
        body, name=name, grid=(rows // tr,), in_specs=[spec] * 4, out_specs=[spec] * 3, out_shape=[out] * 3,
        compiler_params=_cparams("parallel"),
    )(w, g, m, v)


def _local_step(x, target, w_in_p, late_shards, gate_up_pad, gate_bias, gnorm, ln1_g, ln1_b, conv_w3, conv_b3,
                ln2_g, ln2_b, c_arr, kc_arr):
    t = x.shape[0]
    tq = min(t, 1024)
    s_up, s_out, s_down = late_shards
    sh_up, sh_out, sh_down = LATE_SHAPES
    proj, out_partly = _mm(x, w_in_p, m=t, n=IN_PAD, k=D_MODEL, tm=512, tn=IN_PAD, tk=D_MODEL, name="proj",
                           rider=_gather_rider([s_out], [sh_out]))
    sb_o, sb_mass, up_partly = _sb_fwd(proj, t, _gather_rider([s_up], [sh_up]))
    gla_o, oraw, states, w_up4, w_out4, down_partly = _gla_fwd(
        proj, gate_up_pad, gate_bias, gnorm, t,
        _join(_forward_rider([up_partly, out_partly], [sh_up, sh_out]), _gather_rider([s_down], [sh_down])))
    w_out = w_out4.reshape(D_MODEL, D_MODEL)
    xhat1, rstd1, h_bf = _outproj_ln1(sb_o, gla_o, x, w_out, ln1_g, ln1_b, t)
    up3, w_down4 = _mm(h_bf, w_up4, m=t, n=2 * D_FF, k=D_MODEL, tm=tq, tn=W_UP_S, tk=D_MODEL, name="up",
                       b_spec=pl.BlockSpec((None, D_MODEL, W_UP_S), lambda i, j, kk: (j, 0, 0)),
                       o_spec=pl.BlockSpec((None, tq, W_UP_S), lambda i, j, kk: (j // 2, i, j % 2)),
                       out_shape=jax.ShapeDtypeStruct((2, t, D_FF), BF16), out_dtype=BF16,
                       rider=_forward_rider([down_partly], [sh_down]))
    w_down = w_down4.reshape(D_FF, D_MODEL)
    gm = _conv_gelu_fwd(up3, conv_w3, conv_b3, t, tr=tq)
    dr2, dr2b, loss, dln2_g, dln2_b = _down_ln2_loss(gm, w_down, xhat1, ln1_g, ln1_b, ln2_g, ln2_b, target, t)
    dgm = _mm(dr2b, w_down, m=t, n=D_FF, k=D_MODEL, tm=tq, tn=W_UP_S, tk=D_MODEL, tb=True, out_dtype=BF16, name="dgm")
    dw_down = _mm(gm, dr2b, m=D_FF, n=D_MODEL, k=t, tm=W_UP_S, tn=D_MODEL, tk=min(t, 2048), ta=True, name="dw_down")
    dup3, dcw, dcb = _conv_gelu_bwd(up3, dgm, conv_w3, conv_b3, t, tr=tq)
    dr1, dln1_g, dln1_b = _dh_ln1_bwd(dup3, w_up4, dr2, xhat1, rstd1, ln1_g, t)
    dw_up4 = _mm(h_bf, dup3, m=D_MODEL, n=2 * D_FF, k=t, tm=512, tn=W_UP_S, tk=t, ta=True, name="dw_up",
                 b_spec=pl.BlockSpec((None, t, W_UP_S), lambda i, j, kk: (j // 2, kk, j % 2)),
                 o_spec=pl.BlockSpec((None, 512, W_UP_S), lambda i, j, kk: (j, i, 0)),
                 out_shape=jax.ShapeDtypeStruct((4, D_MODEL, W_UP_S), F32))
    dw_out = _dw_out(sb_o, gla_o, dr1, t, tq)
    gs = [dw_up4, dw_out.reshape(4, W_OUT_S, D_MODEL), dw_down.reshape(4, W_DOWN_S, D_MODEL)]
    dcat = _mm(dr1, w_out, m=t, n=D_MODEL, k=D_MODEL, tm=tq, tn=512, tk=D_MODEL, tb=True, name="dcat")
    dgla, dgup_pad, dgbias, dgnorm, *from_sib = _gla_bwd(proj, dcat, oraw, states, gate_up_pad, gate_bias, gnorm, t,
                                                         _sibling_rider(gs, LATE_SHAPES))
    ps = [_add_sibling(gs[m], from_sib[m], c_arr, LATE_ADD_ROWS[m], "add_sibling_late_%d" % m) for m in range(3)]
    small = dict(
        gate_up=dgup_pad[:GATE_RANK], gate_bias=dgbias, gla_norm_g=dgnorm, ln1_g=dln1_g, ln1_b=dln1_b,
        conv_w=jnp.concatenate([dcw[0, :3], dcw[1, :3]], axis=1), conv_b=jnp.concatenate([dcb[0], dcb[1]], axis=1),
        ln2_g=dln2_g, ln2_b=dln2_b, loss=loss)
    dsq, dsk, dsv, *others, vecs = _sb_bwd(proj, dcat, sb_mass, t,
                                           _chips_rider(ps, _pack_vec(small, SMALL_GRADS, GRAD_VEC_ROWS)))
    late_sums = [_add_chips(ps[m], others[m], kc_arr, LATE_ADD_ROWS[m], "add_chips_late_%d" % m) for m in range(3)]
    dproj = [dsq, dsk, dsv, dgla]
    dw_in_p = _dw_in(x, dproj, t, tk=tq)
    g_in = dw_in_p[None]
    from_sib_in, = _run(_sibling_rider([g_in], [(D_MODEL, IN_PAD)]), "exchange_sibling_w_in")
    half_in = _add_sibling(g_in, from_sib_in, c_arr, ADD_ROWS[0], "add_sibling_w_in")[0]
    p_in = jnp.stack([half_in[:, k * W_IN_S:(k + 1) * W_IN_S] for k in range(4)], axis=0)
    dx, others_in = _dx(dproj, w_in_p, dr1, t, _chips_rider([p_in]))
    sum_in = _add_chips(p_in, others_in, kc_arr, ADD_ROWS[0], "add_chips_w_in")
    return dx, [sum_in] + late_sums, vecs


W_IN_S, W_UP_S, W_OUT_S, W_DOWN_S = IN_WIDTH // 4, 2 * D_FF // 4, D_MODEL // 4, D_FF // 4
SHARD_SHAPES = ((D_MODEL, W_IN_S), (D_MODEL, W_UP_S), (W_OUT_S, D_MODEL), (W_DOWN_S, D_MODEL))
ADD_ROWS = (256, 256, 128, 176)
LATE_SHAPES, LATE_ADD_ROWS = SHARD_SHAPES[1:], ADD_ROWS[1:]
SMALL_ROWS = 8
VEC_SIZES = (("gate_bias", GLA_KEYS), ("gla_norm_g", LANES), ("ln1_g", D_MODEL), ("ln1_b", D_MODEL),
             ("conv_b", 2 * D_FF), ("ln2_g", D_MODEL), ("ln2_b", D_MODEL))
SMALL_GRADS = VEC_SIZES + (("conv_w", 3 * 2 * D_FF), ("gate_up", GATE_RANK * GLA_KEYS), ("loss", 1))


def _rows(a):
    flat = a.reshape(-1)
    pad = (-flat.shape[0]) % D_MODEL
    if pad:
        flat = jnp.pad(flat, (0, pad))
    return flat.reshape(-1, D_MODEL)


def _pad_rows(a, rows):
    return jnp.pad(a, ((0, rows - a.shape[0]), (0, 0)))


def _pack_vec(d, sizes, rows):
    flat = jnp.concatenate([d[n].reshape(-1) for n, _ in sizes])
    return _pad_rows(_rows(flat), rows)


def _unpack_vec(v, sizes):
    flat = v.reshape(-1)
    out, o = {}, 0
    for n, size in sizes:
        out[n] = flat[o:o + size].reshape(1, size)
        o += size
    return out


VEC_ROWS = 16
GRAD_VEC_ROWS = 32


HBM_SPEC = pl.BlockSpec(memory_space=pltpu.HBM)


def _position():
    x, y, c = lax.axis_index("x"), lax.axis_index("y"), lax.axis_index("c")
    chips = [(1 - x, y), (x, 1 - y), (1 - x, 1 - y)]
    return x, y, c, chips


def _remote(src, dst, send_sems, recv_sems, k, to):
    return pltpu.make_async_remote_copy(src_ref=src, dst_ref=dst, send_sem=send_sems.at[k], recv_sem=recv_sems.at[k],
                                        device_id=to, device_id_type=MESH)


def _gather_ici(in_refs, out_refs, shapes, send_sems, recv_sems, local_sems):
    x, y, c, chips = _position()
    k_me = 2 * x + y
    local, sends, recvs = [], [], []
    for m, (rows, _) in enumerate(shapes):
        h = rows // 2
        local.append(pltpu.make_async_copy(in_refs[m], out_refs[m].at[k_me], local_sems.at[m]))
        for j, (cx, cy) in enumerate(chips):
            sends.append(_remote(in_refs[m].at[pl.ds(c * h, h), :], out_refs[m].at[k_me, pl.ds(c * h, h), :],
                                 send_sems, recv_sems, 3 * m + j, (cx, cy, c)))
            landed = out_refs[m].at[2 * cx + cy, pl.ds(c * h, h), :]
            recvs.append(_remote(landed, landed, send_sems, recv_sems, 3 * m + j, (x, y, c)))
    return local, sends, recvs


def _gather_d2d(src_refs, dst_refs, shapes, send_sems, recv_sems, base):
    x, y, c, chips = _position()
    sends, recvs = [], []
    for m, (rows, _) in enumerate(shapes):
        h = rows // 2
        for j, (cx, cy) in enumerate(chips):
            k = 2 * cx + cy
            sends.append(_remote(src_refs[m].at[k, pl.ds(c * h, h), :], dst_refs[m].at[k, pl.ds(c * h, h), :],
                                 send_sems, recv_sems, base + 3 * m + j, (x, y, 1 - c)))
            landed = dst_refs[m].at[k, pl.ds((1 - c) * h, h), :]
            recvs.append(_remote(landed, landed, send_sems, recv_sems, base + 3 * m + j, (x, y, c)))
    return sends, recvs


def _gather_weights(shards, small, shapes):
    nm = len(shards)
    n_ici = 3 * nm

    def body(*refs):
        in_refs, small_ref = refs[:nm], refs[nm]
        out_refs, osm_ref = refs[nm + 1:2 * nm + 1], refs[2 * nm + 1]
        send_sems, recv_sems, local_sems = refs[2 * nm + 2:]
        x, y, c, chips = _position()
        k_me = 2 * x + y
        local, sends, recvs = _gather_ici(in_refs, out_refs, shapes, send_sems, recv_sems, local_sems)
        local.append(pltpu.make_async_copy(small_ref, osm_ref.at[k_me], local_sems.at[nm]))
        for j, (cx, cy) in enumerate(chips):
            sends.append(_remote(small_ref, osm_ref.at[k_me], send_sems, recv_sems, n_ici + j, (cx, cy, c)))
        for cp in local + sends:
            cp.start()
        fsends, frecvs = _gather_d2d(out_refs, out_refs, shapes, send_sems, recv_sems, n_ici + 3)
        for landed, onward in zip(recvs, fsends):
            landed.wait_recv()
            onward.start()
        for j, (cx, cy) in enumerate(chips):
            k = 2 * cx + cy
            frecvs.append(_remote(osm_ref.at[k], osm_ref.at[k], send_sems, recv_sems, n_ici + j, (x, y, c)))
        for cp in frecvs:
            cp.wait_recv()
        for cp in sends + fsends:
            cp.wait_send()
        for cp in local:
            cp.wait()

    n_sems = 2 * n_ici + 3
    return pl.pallas_call(
        body, name="gather_weights", in_specs=[HBM_SPEC] * (nm + 1), out_specs=[HBM_SPEC] * (nm + 1),
        out_shape=[jax.ShapeDtypeStruct((4,) + s, BF16) for s in shapes]
        + [jax.ShapeDtypeStruct((4, SMALL_ROWS, D_MODEL), F32)],
        scratch_shapes=[pltpu.SemaphoreType.DMA((n_sems,)), pltpu.SemaphoreType.DMA((n_sems,)),
                        pltpu.SemaphoreType.DMA((nm + 1,))],
    )(*shards, small)


def _gather_rider(shards, shapes):
    n = len(shards)
    return _Rider(shards, [jax.ShapeDtypeStruct((4,) + s, BF16) for s in shapes], (3 * n, 3 * n, n),
                  lambda ins, outs, sems: _gather_ici(ins, outs, shapes, *sems))


def _forward_rider(gathered, shapes):
    n = len(gathered)
    return _Rider(gathered, [jax.ShapeDtypeStruct(a.shape, a.dtype) for a in gathered], (3 * n, 3 * n),
                  lambda ins, outs, sems: ([],) + _gather_d2d(ins, outs, shapes, sems[0], sems[1], 0),
                  aliases=[(m, m) for m in range(n)])


def _sibling_rider(gs, shapes):
    def copies(ins, outs, sems):
        x, y, c, _ = _position()
        both = []
        for m, (rows, _) in enumerate(shapes):
            h = rows // 2
            for k in range(gs[m].shape[0]):
                both.append(_remote(ins[m].at[k, pl.ds((1 - c) * h, h), :], outs[m].at[k], sems[0], sems[1],
                                    4 * m + k, (x, y, 1 - c)))
        return [], both, both

    return _Rider(gs, [jax.ShapeDtypeStruct((g.shape[0], r // 2, cl), F32) for g, (r, cl) in zip(gs, shapes)],
                  (4 * len(gs), 4 * len(gs)), copies)


def _add_sibling(g, r, c_arr, tr, name):
    nblk, rows, cols = g.shape
    nb = rows // 2 // tr

    def body(c_ref, g_ref, r_ref, o_ref):
        o_ref[...] = (g_ref[...] + r_ref[...]).astype(BF16)

    spec = pl.BlockSpec((None, tr, cols), lambda k, i, c: (k, i, 0))
    return pl.pallas_call(
        body, name=name,
        grid_spec=pltpu.PrefetchScalarGridSpec(
            num_scalar_prefetch=1, grid=(nblk, nb),
            in_specs=[pl.BlockSpec((None, tr, cols), lambda k, i, c: (k, c[0] * nb + i, 0)), spec], out_specs=spec),
        out_shape=jax.ShapeDtypeStruct((nblk, rows // 2, cols), BF16), compiler_params=_cparams("parallel", "parallel"),
    )(c_arr, g, r)


def _reduce_ici(p_refs, r_refs, send_sems, recv_sems):
    x, y, c, chips = _position()
    sends, recvs = [], []
    for m in range(len(p_refs)):
        for j, (cx, cy) in enumerate(chips):
            sends.append(_remote(p_refs[m].at[2 * cx + cy], r_refs[m].at[j], send_sems, recv_sems, 3 * m + j, (cx, cy, c)))
            recvs.append(_remote(r_refs[m].at[j], r_refs[m].at[j], send_sems, recv_sems, 3 * m + j, (x, y, c)))
    return sends, recvs


def _chips_rider(ps, vec=None):
    nm = len(ps)
    n_ici = 3 * nm

    def copies(ins, outs, sems):
        sends, recvs = _reduce_ici(ins[:nm], outs[:nm], sems[0], sems[1])
        if vec is None:
            return [], sends, recvs
        x, y, c, _ = _position()
        my_id = 4 * x + 2 * y + c
        vec_ref, vrecv_ref = ins[nm], outs[nm]
        local = [pltpu.make_async_copy(vec_ref, vrecv_ref.at[my_id], sems[2].at[0])]
        for r in range(1, 8):
            peer = (1 - x if r & 4 else x, 1 - y if r & 2 else y, 1 - c if r & 1 else c)
            sends.append(_remote(vec_ref, vrecv_ref.at[my_id], sems[0], sems[1], n_ici + r - 1, peer))
            recvs.append(_remote(vec_ref, vrecv_ref.at[0], sems[0], sems[1], n_ici + r - 1, (x, y, c)))
        return local, sends, recvs

    out_shapes = [jax.ShapeDtypeStruct((3,) + p.shape[1:], p.dtype) for p in ps]
    if vec is None:
        return _Rider(ps, out_shapes, (n_ici, n_ici), copies)
    return _Rider(list(ps) + [vec], out_shapes + [jax.ShapeDtypeStruct((8, GRAD_VEC_ROWS, D_MODEL), F32)],
                  (n_ici + 7, n_ici + 7, 1), copies)


def _add_chips(p, r, kc_arr, tr, name):
    _, h, cols = p.shape
    nb = h // tr

    def body(kc_ref, p_ref, r0_ref, r1_ref, r2_ref, o_ref):
        o_ref[...] = ((p_ref[...].astype(F32) + r0_ref[...].astype(F32)) + r1_ref[...].astype(F32)) + r2_ref[...].astype(F32)

    rspec = lambda j: pl.BlockSpec((None, tr, cols), lambda i, kc: (j, i, 0))
    return pl.pallas_call(
        body, name=name,
        grid_spec=pltpu.PrefetchScalarGridSpec(
            num_scalar_prefetch=1, grid=(nb,),
            in_specs=[pl.BlockSpec((None, tr, cols), lambda i, kc: (kc[0], i, 0)), rspec(0), rspec(1), rspec(2)],
            out_specs=pl.BlockSpec((tr, cols), lambda i, kc: (kc[1] * nb + i, 0))),
        out_shape=jax.ShapeDtypeStruct((2 * h, cols), F32), compiler_params=_cparams("parallel"),
    )(kc_arr, p, r, r, r)


def _reunite_sibling(fs, shapes):
    n_chunks = 2
    nm = len(fs)

    def body(*refs):
        in_refs, out_refs = refs[:nm], refs[nm:2 * nm]
        send_sems, recv_sems = refs[2 * nm:]
        x, y, c, _ = _position()
        sends, recvs = [], []
        for m in range(nm):
            ch = shapes[m][0] // 2 // n_chunks
            for q in range(n_chunks):
                mine = pl.ds((c * n_chunks + q) * ch, ch)
                theirs = pl.ds(((1 - c) * n_chunks + q) * ch, ch)
                s = n_chunks * m + q
                sends.append(_remote(in_refs[m].at[mine, :], out_refs[m].at[mine, :], send_sems, recv_sems, s, (x, y, 1 - c)))
                recvs.append(_remote(in_refs[m].at[theirs, :], out_refs[m].at[theirs, :], send_sems, recv_sems, s, (x, y, c)))
        for cp in sends:
            cp.start()
        for cp in recvs:
            cp.wait_recv()
        for cp in sends:
            cp.wait_send()

    n_sems = n_chunks * nm
    return pl.pallas_call(
        body, name="reunite_sibling", in_specs=[HBM_SPEC] * nm, out_specs=[HBM_SPEC] * nm,
        out_shape=[jax.ShapeDtypeStruct(s, F32) for s in shapes],
        input_output_aliases={m: m for m in range(nm)},
        scratch_shapes=[pltpu.SemaphoreType.DMA((n_sems,)), pltpu.SemaphoreType.DMA((n_sems,))],
    )(*fs)


def _sum_vecs(v):
    def body(v_ref, o_ref):
        acc = v_ref[0]
        for d in range(1, 8):
            acc = acc + v_ref[d]
        o_ref[...] = acc

    return pl.pallas_call(body, name="sum_vecs", out_shape=jax.ShapeDtypeStruct(v.shape[1:], F32))(v)


def kernel(x, w_in, gate_up, gate_bias, gla_norm_g, w_out, ln1_g, ln1_b, w_up, conv_w, conv_b, w_down, ln2_g, ln2_b, loss_target, m_w_in, m_gate_up, m_gate_bias, m_gla_norm_g, m_w_out, m_ln1_g, m_ln1_b, m_w_up, m_conv_w, m_conv_b, m_w_down, m_ln2_g, m_ln2_b, v_w_in, v_gate_up, v_gate_bias, v_gla_norm_g, v_w_out, v_ln1_g, v_ln1_b, v_w_up, v_conv_w, v_conv_b, v_w_down, v_ln2_g, v_ln2_b):
    xi, yi, ci = lax.axis_index("x"), lax.axis_index("y"), lax.axis_index("c")
    k_me = 2 * xi + yi
    c_arr = ci.astype(jnp.int32).reshape(1)
    kc_arr = jnp.stack([k_me, ci]).astype(jnp.int32)
    small = _pad_rows(jnp.concatenate([_rows(conv_w[0]), _rows(gate_up[0])], axis=0), SMALL_ROWS)
    w_in4, gsmall = _gather_weights([w_in[0].astype(BF16)], small, SHARD_SHAPES[:1])
    late_shards = [w_up[0].astype(BF16), w_out[0].astype(BF16), w_down[0].astype(BF16)]
    w_in_p = jnp.pad(jnp.concatenate([w_in4[k] for k in range(4)], axis=1), ((0, 0), (0, IN_PAD - IN_WIDTH)))
    conv_w_f = jnp.concatenate([gsmall[k, :5].reshape(-1)[:3 * W_UP_S].reshape(3, W_UP_S) for k in range(4)], axis=1)
    gate_up_f = jnp.concatenate([gsmall[k, 5].reshape(GATE_RANK, GLA_KEYS // 4) for k in range(4)], axis=1)
    conv_w3 = jnp.transpose(conv_w_f.reshape(3, 2, D_FF), (1, 0, 2))
    conv_b3 = conv_b.reshape(2, 1, D_FF)
    gate_up_pad = jnp.pad(gate_up_f, ((0, LANES - GATE_RANK), (0, 0)))

    dx, sums, vecs = _local_step(
        x[0], loss_target[0], w_in_p, late_shards, gate_up_pad, gate_bias, gla_norm_g, ln1_g, ln1_b, conv_w3, conv_b3,
        ln2_g, ln2_b, c_arr, kc_arr)
    g_w_in, g_w_up, g_w_out, g_w_down = _reunite_sibling(sums, SHARD_SHAPES)
    gsmall_sum = _unpack_vec(_sum_vecs(vecs), SMALL_GRADS)
    g_conv_w = lax.dynamic_slice_in_dim(gsmall_sum["conv_w"].reshape(3, 2 * D_FF), k_me * W_UP_S, W_UP_S, axis=1)
    g_gate_up = lax.dynamic_slice_in_dim(gsmall_sum["gate_up"].reshape(GATE_RANK, GLA_KEYS), k_me * (GLA_KEYS // 4),
                                         GLA_KEYS // 4, axis=1)
    gv = gsmall_sum
    loss = gv["loss"][0, 0]
    gvec = _pack_vec(gv, VEC_SIZES, VEC_ROWS)

    grads = dict(w_in=g_w_in[None], gate_up=g_gate_up[None], gate_bias=gv["gate_bias"], gla_norm_g=gv["gla_norm_g"],
                 w_out=g_w_out[None], ln1_g=gv["ln1_g"], ln1_b=gv["ln1_b"], w_up=g_w_up[None], conv_w=g_conv_w[None],
                 conv_b=gv["conv_b"], w_down=g_w_down[None], ln2_g=gv["ln2_g"], ln2_b=gv["ln2_b"])
    weights = dict(w_in=w_in, gate_up=gate_up, gate_bias=gate_bias, gla_norm_g=gla_norm_g, w_out=w_out, ln1_g=ln1_g,
                   ln1_b=ln1_b, w_up=w_up, conv_w=conv_w, conv_b=conv_b, w_down=w_down, ln2_g=ln2_g, ln2_b=ln2_b)
    ms = dict(w_in=m_w_in, gate_up=m_gate_up, gate_bias=m_gate_bias, gla_norm_g=m_gla_norm_g, w_out=m_w_out, ln1_g=m_ln1_g,
              ln1_b=m_ln1_b, w_up=m_w_up, conv_w=m_conv_w, conv_b=m_conv_b, w_down=m_w_down, ln2_g=m_ln2_g, ln2_b=m_ln2_b)
    vs = dict(w_in=v_w_in, gate_up=v_gate_up, gate_bias=v_gate_bias, gla_norm_g=v_gla_norm_g, w_out=v_w_out, ln1_g=v_ln1_g,
              ln1_b=v_ln1_b, w_up=v_w_up, conv_w=v_conv_w, conv_b=v_conv_b, w_down=v_w_down, ln2_g=v_ln2_g, ln2_b=v_ln2_b)
    names = ["w_in", "gate_up", "gate_bias", "gla_norm_g", "w_out", "ln1_g", "ln1_b", "w_up", "conv_w", "conv_b", "w_down",
             "ln2_g", "ln2_b"]
    delta, new_m, new_v = {}, {}, {}
    for n in ("w_in", "gate_up", "w_out", "w_up", "conv_w", "w_down"):
        tr_ = jnp.transpose if n == "w_in" else (lambda a: a)
        d, nm, nv = _adamw(tr_(weights[n][0]), tr_(grads[n][0]), tr_(ms[n][0]), tr_(vs[n][0]), "adamw_" + n)
        delta[n], new_m[n], new_v[n] = tr_(d)[None], tr_(nm)[None], tr_(nv)[None]
    d, nm, nv = _adamw(_pack_vec(weights, VEC_SIZES, VEC_ROWS), gvec, _pack_vec(ms, VEC_SIZES, VEC_ROWS),
                       _pack_vec(vs, VEC_SIZES, VEC_ROWS), "adamw_vectors")
    for dst, src in ((delta, d), (new_m, nm), (new_v, nv)):
        dst.update(_unpack_vec(src, VEC_SIZES))
    return (loss, dx[None], *[grads[n] for n in names], *[delta[n] for n in names], *[new_m[n] for n in names],
            *[new_v[n] for n in names])
```

```python
import math

import jax
import jax.numpy as jnp
from jax import lax
from jax.experimental import pallas as pl
from jax.experimental.pallas import tpu as pltpu

F32 = jnp.float32
BF16 = jnp.bfloat16

D_MODEL = 1024
SB_WIDTH = 512
GLA_KEYS = 256
GLA_WIDTH = 512
GATE_RANK = 16
IN_WIDTH = 3088
IN_PAD = 3200
D_FF = 2816
CHUNK = 64
LN_EPS = 1e-5
RMS_EPS = 1e-6
ALPHA = 2.0 ** 0.25
GLA_TAU = 16.0
SB_SCALE = 0.125
GLA_SCALE = 0.125
LANES = 128
SB_BLK = 256
SB_CUT = -100.0
GLA_ROWS = 256
VMEM_LIMIT = 56 * 1024 * 1024

ADAM_LR, ADAM_B1, ADAM_B2, ADAM_EPS, ADAM_WD, ADAM_STEP = 0.001, 0.9, 0.999, 1e-08, 0.01, 10

MESH = pl.DeviceIdType.MESH


def _cparams(*sem):
    return pltpu.CompilerParams(dimension_semantics=sem, vmem_limit_bytes=VMEM_LIMIT)


def _dot(a, b):
    return jnp.dot(a, b, preferred_element_type=F32)


def _dot_nt(a, b):
    return lax.dot_general(a, b, (((1,), (1,)), ((), ())), preferred_element_type=F32)


def _dot_tn(a, b):
    return lax.dot_general(a, b, (((0,), (0,)), ((), ())), preferred_element_type=F32)


def _split3(x):
    hi = x.astype(BF16)
    r = x - hi.astype(F32)
    mid = r.astype(BF16)
    lo = (r - mid.astype(F32)).astype(BF16)
    return hi, mid, lo


def _softplus(z):
    return jnp.maximum(z, 0.0) + jnp.log(1.0 + jnp.exp(-jnp.abs(z)))


def _sigmoid(z):
    return 1.0 / (1.0 + jnp.exp(-z))


def _mm(a, b, *, m, n, k, tm, tn, tk, ta=False, tb=False, a_spec=None, b_spec=None, o_spec=None,
        out_shape=None, out_dtype=F32, add=None, add_scale=1.0, rider=None, name):
    nk = k // tk
    dn = (((0 if ta else 1,), (1 if tb else 0,)), ((), ()))
    n_in = len(rider.inputs) if rider else 0
    n_out = len(rider.out_shapes) if rider else 0
    n_add = int(add is not None)
    steps = (m // tm, n // tn, nk)

    def body(*refs):
        a_ref, b_ref = refs[:2]
        add_ref = refs[2] if add is not None else None
        rin = refs[2 + n_add:2 + n_add + n_in]
        o_ref = refs[2 + n_add + n_in]
        rout = refs[3 + n_add + n_in:3 + n_add + n_in + n_out]
        scratch = refs[3 + n_add + n_in + n_out:]
        if rider:
            sems = scratch[len(scratch) - len(rider.sems):]
            ids = [pl.program_id(d) for d in range(3)]
            _ride(rider, rin, rout, sems, (ids[0] == 0) & (ids[1] == 0) & (ids[2] == 0),
                  (ids[0] == steps[0] - 1) & (ids[1] == steps[1] - 1) & (ids[2] == steps[2] - 1))

        part = lax.dot_general(a_ref[...].astype(BF16), b_ref[...].astype(BF16), dn, preferred_element_type=F32)

        def finish(r):
            if add is not None:
                r = r + add_scale * add_ref[...]
            o_ref[...] = r.astype(out_dtype)

        if nk == 1:
            finish(part)
            return
        acc_ref = scratch[0]
        kk = pl.program_id(2)

        @pl.when(kk == 0)
        def _():
            acc_ref[...] = part

        @pl.when((kk > 0) & (kk < nk - 1))
        def _():
            acc_ref[...] += part

        @pl.when(kk == nk - 1)
        def _():
            finish(acc_ref[...] + part)

    if a_spec is None:
        a_spec = pl.BlockSpec((tk, tm), lambda i, j, kk: (kk, i)) if ta else pl.BlockSpec((tm, tk), lambda i, j, kk: (i, kk))
    if b_spec is None:
        b_spec = pl.BlockSpec((tn, tk), lambda i, j, kk: (j, kk)) if tb else pl.BlockSpec((tk, tn), lambda i, j, kk: (kk, j))
    if o_spec is None:
        o_spec = pl.BlockSpec((tm, tn), lambda i, j, kk: (i, j))
    if out_shape is None:
        out_shape = jax.ShapeDtypeStruct((m, n), out_dtype)
    in_specs = [a_spec, b_spec]
    args = [a, b]
    if add is not None:
        in_specs.append(pl.BlockSpec((tm, tn), lambda i, j, kk: (i, j)))
        args.append(add)
    scratch = [pltpu.VMEM((tm, tn), F32)] if nk > 1 else []
    if not rider:
        return pl.pallas_call(
            body, name=name, grid=steps, in_specs=in_specs, out_specs=o_spec, out_shape=out_shape,
            scratch_shapes=scratch, compiler_params=_cparams("parallel", "parallel", "arbitrary"),
        )(*args)
    return pl.pallas_call(
        body, name=name, grid=steps, in_specs=in_specs + [HBM_SPEC] * n_in, out_specs=[o_spec] + [HBM_SPEC] * n_out,
        out_shape=[out_shape] + list(rider.out_shapes),
        input_output_aliases={len(args) + i: 1 + o for i, o in rider.aliases},
        scratch_shapes=scratch + [pltpu.SemaphoreType.DMA((s,)) for s in rider.sems],
        compiler_params=_cparams("arbitrary", "arbitrary", "arbitrary"),
    )(*args, *rider.inputs)


def _col_offsets(pieces):
    offs, o = [], 0
    for a in pieces:
        offs.append(o)
        o += a.shape[1]
    return offs


def _dx(pieces, w_in_p, dr1, t, rider, tm=512):
    offs = _col_offsets(pieces)
    npc = len(pieces)
    n_in, n_out = len(rider.inputs), len(rider.out_shapes)
    steps = t // tm

    def body(*refs):
        p_refs, w_ref, add_ref = refs[:npc], refs[npc], refs[npc + 1]
        rin = refs[npc + 2:npc + 2 + n_in]
        o_ref = refs[npc + 2 + n_in]
        rout = refs[npc + 3 + n_in:npc + 3 + n_in + n_out]
        i = pl.program_id(0)
        _ride(rider, rin, rout, refs[npc + 3 + n_in + n_out:], i == 0, i == steps - 1)
        acc = ALPHA * add_ref[...]
        for p_ref, off in zip(p_refs, offs):
            acc = acc + _dot_nt(p_ref[...].astype(BF16), w_ref[:, off:off + p_ref.shape[1]])
        o_ref[...] = acc

    row = lambda w: pl.BlockSpec((tm, w), lambda i: (i, 0))
    return pl.pallas_call(
        body, name="dx", grid=(steps,),
        in_specs=[row(a.shape[1]) for a in pieces] + [pl.BlockSpec(w_in_p.shape, lambda i: (0, 0)), row(D_MODEL)]
        + [HBM_SPEC] * n_in,
        out_specs=[row(D_MODEL)] + [HBM_SPEC] * n_out,
        out_shape=[jax.ShapeDtypeStruct((t, D_MODEL), F32)] + rider.out_shapes,
        scratch_shapes=[pltpu.SemaphoreType.DMA((s,)) for s in rider.sems],
        compiler_params=_cparams("arbitrary"),
    )(*pieces, w_in_p, dr1, *rider.inputs)


def _dw_out(sb_o, gla_o, dr1, t, tk):
    def body(sb_ref, gl_ref, dr_ref, o_ref):
        kk = pl.program_id(0)
        drb = dr_ref[...].astype(BF16)
        top, bottom = _dot_tn(sb_ref[...], drb), _dot_tn(gl_ref[...], drb)

        @pl.when(kk == 0)
        def _():
            o_ref[0:SB_WIDTH, :] = top
            o_ref[SB_WIDTH:, :] = bottom

        @pl.when(kk > 0)
        def _():
            o_ref[0:SB_WIDTH, :] += top
            o_ref[SB_WIDTH:, :] += bottom

    return pl.pallas_call(
        body, name="dw_out", grid=(t // tk,),
        in_specs=[pl.BlockSpec((tk, SB_WIDTH), lambda kk: (kk, 0)), pl.BlockSpec((tk, GLA_WIDTH), lambda kk: (kk, 0)),
                  pl.BlockSpec((tk, D_MODEL), lambda kk: (kk, 0))],
        out_specs=pl.BlockSpec((D_MODEL, D_MODEL), lambda kk: (0, 0)),
        out_shape=jax.ShapeDtypeStruct((D_MODEL, D_MODEL), F32),
        compiler_params=_cparams("arbitrary"),
    )(sb_o, gla_o, dr1)


def _dw_in(x, pieces, t, tm=512, tk=512):
    offs = _col_offsets(pieces)
    npc = len(pieces)
    n = offs[-1] + pieces[-1].shape[1]
    nk = t // tk

    def body(*refs):
        x_ref, p_refs, o_ref, acc_ref = refs[0], refs[1:1 + npc], refs[1 + npc], refs[2 + npc]
        kk = pl.program_id(1)
        xb = x_ref[...].astype(BF16)
        for p_ref, off in zip(p_refs, offs):
            cols = slice(off, off + p_ref.shape[1])
            part = _dot_tn(xb, p_ref[...].astype(BF16))
            if nk == 1:
                o_ref[:, cols] = part
                continue

            @pl.when(kk == 0)
            def _():
                acc_ref[:, cols] = part

            @pl.when((kk > 0) & (kk < nk - 1))
            def _():
                acc_ref[:, cols] += part

            @pl.when(kk == nk - 1)
            def _():
                o_ref[:, cols] = acc_ref[:, cols] + part

    return pl.pallas_call(
        body, name="dw_in", grid=(D_MODEL // tm, nk),
        in_specs=[pl.BlockSpec((tk, tm), lambda i, kk: (kk, i))]
        + [pl.BlockSpec((tk, a.shape[1]), lambda i, kk: (kk, 0)) for a in pieces],
        out_specs=pl.BlockSpec((tm, n), lambda i, kk: (i, 0)),
        out_shape=jax.ShapeDtypeStruct((D_MODEL, n), F32),
        scratch_shapes=[pltpu.VMEM((tm, n), F32)],
        compiler_params=_cparams("parallel", "arbitrary"),
    )(x, *pieces)


class _Rider:
    def __init__(self, inputs, out_shapes, sems, copies, aliases=()):
        self.inputs, self.out_shapes, self.sems, self.copies = list(inputs), list(out_shapes), tuple(sems), copies
        self.aliases = tuple(aliases)


def _join(a, b):
    na_in, na_out, na_sems = len(a.inputs), len(a.out_shapes), len(a.sems)

    def copies(ins, outs, sems):
        first = a.copies(ins[:na_in], outs[:na_out], sems[:na_sems])
        second = b.copies(ins[na_in:], outs[na_out:], sems[na_sems:])
        return tuple(u + v for u, v in zip(first, second))

    return _Rider(a.inputs + b.inputs, a.out_shapes + b.out_shapes, a.sems + b.sems, copies,
                  a.aliases + tuple((i + na_in, o + na_out) for i, o in b.aliases))


def _ride(rider, in_refs, out_refs, sems, first, last):
    @pl.when(first)
    def _():
        local, sends, _ = rider.copies(in_refs, out_refs, sems)
        for cp in local + sends:
            cp.start()

    @pl.when(last)
    def _():
        local, sends, recvs = rider.copies(in_refs, out_refs, sems)
        for cp in recvs:
            cp.wait_recv()
        for cp in sends:
            cp.wait_send()
        for cp in local:
            cp.wait()


def _run(rider, name):
    n_in, n_out = len(rider.inputs), len(rider.out_shapes)

    def body(*refs):
        local, sends, recvs = rider.copies(refs[:n_in], refs[n_in:n_in + n_out], refs[n_in + n_out:])
        for cp in local + sends:
            cp.start()
        for cp in recvs:
            cp.wait_recv()
        for cp in sends:
            cp.wait_send()
        for cp in local:
            cp.wait()

    return pl.pallas_call(
        body, name=name, in_specs=[HBM_SPEC] * n_in, out_specs=[HBM_SPEC] * n_out, out_shape=list(rider.out_shapes),
        scratch_shapes=[pltpu.SemaphoreType.DMA((s,)) for s in rider.sems],
    )(*rider.inputs)


def _sb_tile(qh, kj, diag, strict, u_strict, r_in):
    z = _dot_nt(qh, kj)
    sp = _softplus(z)
    l1m = -sp
    lsz = z - sp
    if diag:
        l1m = jnp.where(strict, l1m, 0.0)
    cs = _dot(l1m.astype(BF16), u_strict) + r_in
    w = jnp.exp(lsz + cs)
    if diag:
        w = jnp.where(strict, w, 0.0)
    return l1m, lsz, w


def _sb_consts():
    row = lax.broadcasted_iota(jnp.int32, (SB_BLK, SB_BLK), 0)
    col = lax.broadcasted_iota(jnp.int32, (SB_BLK, SB_BLK), 1)
    strict = col < row
    u_strict = (row > col).astype(BF16)
    u_pre = (row < col).astype(BF16)
    lane = lax.broadcasted_iota(jnp.int32, (1, LANES), 1)
    return strict, u_strict, u_pre, lane


def _sb_fwd(proj, t, rider):
    nq = t // SB_BLK
    n_in, n_out = len(rider.inputs), len(rider.out_shapes)

    def body(q_ref, k_ref, v_ref, *rest):
        rin, o_ref, sv_ref, rout = rest[:n_in], rest[n_in], rest[n_in + 1], rest[n_in + 2:n_in + 2 + n_out]
        p = pl.program_id(0)
        i = pl.program_id(1)
        _ride(rider, rin, rout, rest[n_in + 2 + n_out:], (p == 0) & (i == 0), (p == 3) & (i == nq - 1))

        strict, u_strict, _, lane = _sb_consts()
        qf = q_ref[...] * SB_SCALE
        hms = [(lane // 64) == hh for hh in range(2)]
        qhs = [jnp.where(hm, qf, 0.0).astype(BF16) for hm in hms]

        def step(j, r0, r1, a, sv, diag, keep=None):
            rows = pl.ds(pl.multiple_of(j * SB_BLK, SB_BLK), SB_BLK)
            kj = k_ref[rows, :].astype(BF16)
            vf = v_ref[rows, :]
            rs = []
            for hh, r in enumerate((r0, r1)):
                l1m, _, w = _sb_tile(qhs[hh], kj, diag, strict, u_strict, r)
                pv = _dot(w.astype(BF16), jnp.where(hms[hh], vf, 0.0).astype(BF16))
                mass = jnp.sum(l1m, axis=1, keepdims=True)
                if keep is not None:
                    pv, mass = jnp.where(keep, pv, 0.0), jnp.where(keep, mass, 0.0)
                a = a + pv
                rs.append(r + mass)
                sv = jnp.where(lane == hh * 64 + (i - j + 1), rs[hh], sv)
            return rs[0], rs[1], a, sv

        zero = jnp.zeros((SB_BLK, 1), F32)
        acc0 = jnp.zeros((SB_BLK, LANES), F32)
        r0, r1, acc, sv = step(i, zero, zero, acc0, acc0, True)
        r0, r1, acc, sv = step(jnp.maximum(i - 1, 0), r0, r1, acc, sv, False, keep=i > 0)
        _, _, _, acc, sv = lax.while_loop(
            lambda c: (c[0] >= 0) & (jnp.maximum(jnp.max(c[1]), jnp.max(c[2])) > SB_CUT),
            lambda c: (c[0] - 1,) + step(c[0], c[1], c[2], c[3], c[4], False),
            (i - 2, r0, r1, acc, sv))
        o_ref[...] = acc.astype(BF16)
        sv_ref[...] = sv

    return pl.pallas_call(
        body, name="sb_fwd", grid=(4, nq),
        in_specs=[pl.BlockSpec((SB_BLK, LANES), lambda p, i: (i, p)),
                  pl.BlockSpec((t, LANES), lambda p, i: (0, 4 + p)),
                  pl.BlockSpec((t, LANES), lambda p, i: (0, 8 + p))] + [HBM_SPEC] * n_in,
        out_specs=[pl.BlockSpec((SB_BLK, LANES), lambda p, i: (i, p))] * 2 + [HBM_SPEC] * n_out,
        out_shape=[jax.ShapeDtypeStruct((t, SB_WIDTH), BF16), jax.ShapeDtypeStruct((t, SB_WIDTH), F32)] + rider.out_shapes,
        scratch_shapes=[pltpu.SemaphoreType.DMA((s,)) for s in rider.sems],
        compiler_params=_cparams("arbitrary", "arbitrary"),
    )(proj, proj, proj, *rider.inputs)


def _sb_bwd(proj, dcat, mass, t, rider):
    nq = t // SB_BLK
    n_in, n_out = len(rider.inputs), len(rider.out_shapes)

    def body(q_ref, k_ref, v_ref, do_ref, sv_ref, *rest):
        rin = rest[:n_in]
        dq_ref, dk_ref, dv_ref = rest[n_in:n_in + 3]
        rout = rest[n_in + 3:n_in + 3 + n_out]
        p = pl.program_id(0)
        i = pl.program_id(1)
        _ride(rider, rin, rout, rest[n_in + 3 + n_out:], (p == 0) & (i == 0), (p == 3) & (i == nq - 1))

        @pl.when(i == 0)
        def _():
            dk_ref[...] = jnp.zeros_like(dk_ref)
            dv_ref[...] = jnp.zeros_like(dv_ref)

        strict, u_strict, u_pre, lane = _sb_consts()
        qf = q_ref[...] * SB_SCALE
        dof = do_ref[...]
        hms = [(lane // 64) == hh for hh in range(2)]
        qhs = [jnp.where(hm, qf, 0.0).astype(BF16) for hm in hms]
        dohs = [jnp.where(hm, dof, 0.0).astype(BF16) for hm in hms]

        sv = sv_ref[...]
        zero = jnp.zeros((SB_BLK, 1), F32)

        def mass_right(hh, d):
            return jnp.sum(jnp.where(lane == hh * 64 + d, sv, 0.0), axis=1, keepdims=True)

        dstop = lax.while_loop(
            lambda d: (i - d >= 0) & (jnp.maximum(jnp.max(mass_right(0, d)), jnp.max(mass_right(1, d))) > SB_CUT),
            lambda d: d + 1, 1)
        jstop = i - dstop

        def step(j, carry, diag, keep=None):
            pre_g0, pre_g1, dqa = carry
            rows = pl.ds(pl.multiple_of(j * SB_BLK, SB_BLK), SB_BLK)
            kf = k_ref[rows, :]
            kj = kf.astype(BF16)
            vj = v_ref[rows, :].astype(BF16)
            dv = jnp.zeros((SB_BLK, LANES), F32)
            dk = jnp.zeros((SB_BLK, LANES), F32)
            dqj = jnp.zeros((SB_BLK, LANES), F32)
            pre = []
            for hh, pre_g in enumerate((pre_g0, pre_g1)):
                _, lsz, w = _sb_tile(qhs[hh], kj, diag, strict, u_strict, zero if diag else mass_right(hh, i - j))
                g = w * _dot_nt(dohs[hh], vj)
                gpre = _dot(g.astype(BF16), u_pre) + pre_g
                sig = jnp.exp(lsz)
                dz = g * (1.0 - sig) - gpre * sig
                if diag:
                    dz = jnp.where(strict, dz, 0.0)
                dzb = dz.astype(BF16)
                dv = dv + _dot_tn(w.astype(BF16), dohs[hh])
                dk = dk + _dot_tn(dzb, qhs[hh])
                dqj = dqj + _dot(dzb, jnp.where(hms[hh], kf, 0.0).astype(BF16))
                gsum = jnp.sum(g, axis=1, keepdims=True)
                pre.append(pre_g + (gsum if keep is None else jnp.where(keep, gsum, 0.0)))
            if keep is not None:
                dv, dk, dqj = jnp.where(keep, dv, 0.0), jnp.where(keep, dk, 0.0), jnp.where(keep, dqj, 0.0)
            dv_ref[rows, :] += dv
            dk_ref[rows, :] += dk
            return pre[0], pre[1], dqa + dqj

        carry = lax.fori_loop(jstop + 1, i - 1, lambda j, c: step(j, c, False), (zero, zero, jnp.zeros((SB_BLK, LANES), F32)))
        carry = step(jnp.maximum(i - 1, 0), carry, False, keep=i > 0)
        _, _, dq = step(i, carry, True)
        dq_ref[...] = (dq * SB_SCALE).astype(BF16)

    return pl.pallas_call(
        body, name="sb_bwd", grid=(4, nq),
        in_specs=[pl.BlockSpec((SB_BLK, LANES), lambda p, i: (i, p)),
                  pl.BlockSpec((t, LANES), lambda p, i: (0, 4 + p)),
                  pl.BlockSpec((t, LANES), lambda p, i: (0, 8 + p)),
                  pl.BlockSpec((SB_BLK, LANES), lambda p, i: (i, p)),
                  pl.BlockSpec((SB_BLK, LANES), lambda p, i: (i, p))] + [HBM_SPEC] * n_in,
        out_specs=[pl.BlockSpec((SB_BLK, LANES), lambda p, i: (i, p)),
                   pl.BlockSpec((t, LANES), lambda p, i: (0, p)),
                   pl.BlockSpec((t, LANES), lambda p, i: (0, p))] + [HBM_SPEC] * n_out,
        out_shape=[jax.ShapeDtypeStruct((t, SB_WIDTH), BF16)] + [jax.ShapeDtypeStruct((t, SB_WIDTH), F32)] * 2
        + rider.out_shapes,
        scratch_shapes=[pltpu.SemaphoreType.DMA((s,)) for s in rider.sems],
        compiler_params=_cparams("arbitrary", "arbitrary"),
    )(proj, proj, proj, dcat, mass, *rider.inputs)


def _gla_consts():
    r = lax.broadcasted_iota(jnp.int32, (GLA_ROWS, GLA_ROWS), 0)
    c = lax.broadcasted_iota(jnp.int32, (GLA_ROWS, GLA_ROWS), 1)
    same = (r // CHUNK) == (c // CHUNK)
    causal = same & (c <= r)
    upto_mid = c % CHUNK <= CHUNK // 2 - 1
    fwd_stack = jnp.concatenate([causal, same & upto_mid, same], axis=0).astype(BF16)
    bwd_stack = jnp.concatenate([same & (c >= r), same & (r % CHUNK <= CHUNK // 2 - 1), same], axis=1).astype(BF16)
    rowid = lax.broadcasted_iota(jnp.int32, (GLA_ROWS, 1), 0)
    lane = lax.broadcasted_iota(jnp.int32, (1, LANES), 1)
    sr = lax.broadcasted_iota(jnp.int32, (LANES, 2 * LANES), 0)
    sc = lax.broadcasted_iota(jnp.int32, (LANES, 2 * LANES), 1)
    blockdiag = (sr // 64) == (sc // LANES)
    return causal, fwd_stack, bwd_stack, rowid, lane, blockdiag


def _dot3(u, x):
    hi, mid, lo = _split3(x)
    return _dot(u, hi) + _dot(u, mid) + _dot(u, lo)


def _row_to_col(row):
    return jnp.transpose(jnp.broadcast_to(row, (LANES, LANES)))


def _gla_gates(ga_ref, gup_ref, gbias_ref):
    pre = _dot(ga_ref[...].astype(BF16), gup_ref[...].astype(BF16)) + gbias_ref[...]
    log_a = (jnp.minimum(pre, 0.0) - jnp.log(1.0 + jnp.exp(-jnp.abs(pre)))) / GLA_TAU
    return pre, log_a


def _gla_terms(g2, q2, k2, fwd_stack):
    bs = _dot3(fwd_stack, g2)
    b, b_ref, b_last = bs[:GLA_ROWS], bs[GLA_ROWS:2 * GLA_ROWS], bs[2 * GLA_ROWS:]
    qs = q2 * GLA_SCALE
    e_q = jnp.exp(b - b_ref)
    e_k = jnp.exp(b_ref - b)
    e_d = jnp.exp(b_last - b)
    e_b = jnp.exp(b)
    decay = jnp.exp(b_last)
    return dict(qs=qs, e_q=e_q, e_k=e_k, e_d=e_d, e_b=e_b, decay=decay,
                qi=qs * e_q, ki=k2 * e_k, kd=k2 * e_d, qb=qs * e_b)


def _gla_fwd(proj, gate_up_pad, gate_bias, gnorm, t, rider):
    nsteps = t // GLA_ROWS
    cps = GLA_ROWS // CHUNK
    n_in, n_out = len(rider.inputs), len(rider.out_shapes)

    def body(q_ref, k_ref, v_ref, gg_ref, ga_ref, gup_ref, gbias_ref, gn_ref, *rest):
        o_ref, oraw_ref, st_ref = rest[n_in:n_in + 3]
        s_scr = rest[n_in + 3 + n_out]
        i = pl.program_id(0)
        _ride(rider, rest[:n_in], rest[n_in + 3:n_in + 3 + n_out], rest[n_in + 4 + n_out:], i == 0, i == nsteps - 1)

        @pl.when(i == 0)
        def _():
            s_scr[...] = jnp.zeros_like(s_scr)

        _, fwd_stack, _, _, lane, blockdiag = _gla_consts()
        causal64 = (lax.broadcasted_iota(jnp.int32, (CHUNK, CHUNK), 1) <= lax.broadcasted_iota(jnp.int32, (CHUNK, CHUNK), 0))
        _, log_a = _gla_gates(ga_ref, gup_ref, gbias_ref)
        gn = gn_ref[...]
        pre = []
        for p in range(2):
            kl = slice(p * LANES, (p + 1) * LANES)
            tm = _gla_terms(log_a[:, kl], q_ref[:, kl], k_ref[:, kl], fwd_stack)
            pre.append(dict(
                v2b=v_ref[:, p * 2 * LANES:(p + 1) * 2 * LANES].astype(BF16), kib=tm["ki"].astype(BF16),
                kdb=tm["kd"].astype(BF16), qbb=tm["qb"].astype(BF16), decay=tm["decay"],
                qihb=[jnp.where((lane // 64) == hh, tm["qi"], 0.0).astype(BF16) for hh in range(2)]))
        state = [s_scr[0], s_scr[1]]
        outs = [[], []]
        for cc in range(cps):
            rows = slice(cc * CHUNK, (cc + 1) * CHUNK)
            for p in range(2):
                w = pre[p]
                st_ref[cc, p] = state[p]
                intra = []
                for hh in range(2):
                    a = jnp.where(causal64, _dot_nt(w["qihb"][hh][rows], w["kib"][rows]), 0.0)
                    intra.append(_dot(a.astype(BF16), w["v2b"][rows, hh * LANES:(hh + 1) * LANES]))
                outs[p].append(jnp.concatenate(intra, axis=1) + _dot(w["qbb"][rows], state[p].astype(BF16)))
                upd = jnp.where(blockdiag, _dot_tn(w["kdb"][rows], w["v2b"][rows]), 0.0)
                dcol = _row_to_col(w["decay"][cc * CHUNK:cc * CHUNK + 1])
                state[p] = state[p] * jnp.concatenate([dcol, dcol], axis=1) + upd
        for p in range(2):
            s_scr[p] = state[p]
            vl = slice(p * 2 * LANES, (p + 1) * 2 * LANES)
            o2 = jnp.concatenate(outs[p], axis=0)
            oraw_ref[:, vl] = o2
            for hh in range(2):
                oh = o2[:, hh * LANES:(hh + 1) * LANES]
                gl = slice(p * 2 * LANES + hh * LANES, p * 2 * LANES + (hh + 1) * LANES)
                rinv = lax.rsqrt(jnp.mean(oh * oh, axis=1, keepdims=True) + RMS_EPS)
                gg = gg_ref[:, gl]
                o_ref[:, gl] = (oh * rinv * gn * (gg * _sigmoid(gg))).astype(BF16)

    cb = lambda w, idx: pl.BlockSpec((GLA_ROWS, w), lambda i: (i, idx))
    full = lambda shp: pl.BlockSpec(shp, lambda i: tuple(0 for _ in shp))
    return pl.pallas_call(
        body, name="gla_fwd", grid=(nsteps,),
        in_specs=[cb(256, 6), cb(256, 7), cb(512, 4), cb(512, 5), cb(128, 24),
                  full((LANES, GLA_KEYS)), full((1, GLA_KEYS)), full((1, LANES))] + [HBM_SPEC] * n_in,
        out_specs=[pl.BlockSpec((GLA_ROWS, GLA_WIDTH), lambda i: (i, 0)),
                   pl.BlockSpec((GLA_ROWS, GLA_WIDTH), lambda i: (i, 0)),
                   pl.BlockSpec((cps, 2, LANES, 2 * LANES), lambda i: (i, 0, 0, 0))] + [HBM_SPEC] * n_out,
        out_shape=[jax.ShapeDtypeStruct((t, GLA_WIDTH), BF16), jax.ShapeDtypeStruct((t, GLA_WIDTH), F32),
                   jax.ShapeDtypeStruct((t // CHUNK, 2, LANES, 2 * LANES), F32)]
        + rider.out_shapes,
        input_output_aliases={8 + i: 3 + o for i, o in rider.aliases},
        scratch_shapes=[pltpu.VMEM((2, LANES, 2 * LANES), F32)] + [pltpu.SemaphoreType.DMA((s,)) for s in rider.sems],
        compiler_params=_cparams("arbitrary"),
    )(proj, proj, proj, proj, proj, gate_up_pad, gate_bias, gnorm, *rider.inputs)


def _gla_bwd(proj, dcat, oraw, states, gate_up_pad, gate_bias, gnorm, t, rider):
    nsteps = t // GLA_ROWS
    cps = GLA_ROWS // CHUNK
    wout = 2 * GLA_KEYS + 2 * GLA_WIDTH + LANES
    n_in, n_out = len(rider.inputs), len(rider.out_shapes)

    def body(q_ref, k_ref, v_ref, gg_ref, ga_ref, do_ref, oraw_ref, st_ref, gup_ref, gbias_ref, gn_ref, *rest):
        d_ref, dgup_ref, dgbias_ref, dgn_ref = rest[n_in:n_in + 4]
        ds_scr = rest[n_in + 4 + n_out]
        i = pl.program_id(0)
        _ride(rider, rest[:n_in], rest[n_in + 4:n_in + 4 + n_out], rest[n_in + 5 + n_out:], i == 0, i == nsteps - 1)

        @pl.when(i == 0)
        def _():
            ds_scr[...] = jnp.zeros_like(ds_scr)
            dgup_ref[...] = jnp.zeros_like(dgup_ref)
            dgbias_ref[...] = jnp.zeros_like(dgbias_ref)
            dgn_ref[...] = jnp.zeros_like(dgn_ref)

        causal, fwd_stack, bwd_stack, rowid, lane, blockdiag = _gla_consts()
        pre, log_a = _gla_gates(ga_ref, gup_ref, gbias_ref)
        gn = gn_ref[...]
        dgn = jnp.zeros((1, LANES), F32)
        dgs = []
        pre_p = []
        for p in range(2):
            kl = slice(p * LANES, (p + 1) * LANES)
            vl = slice(p * 2 * LANES, (p + 1) * 2 * LANES)
            tm = _gla_terms(log_a[:, kl], q_ref[:, kl], k_ref[:, kl], fwd_stack)
            v2b = v_ref[:, vl].astype(BF16)
            dos = []
            for hh in range(2):
                gl = slice(p * 2 * LANES + hh * LANES, p * 2 * LANES + (hh + 1) * LANES)
                oh = oraw_ref[:, gl]
                rinv = lax.rsqrt(jnp.mean(oh * oh, axis=1, keepdims=True) + RMS_EPS)
                on = oh * rinv
                gg = gg_ref[:, gl]
                sg = _sigmoid(gg)
                sil = gg * sg
                dgo = do_ref[:, gl]
                d_ref[:, 2 * GLA_KEYS + GLA_WIDTH + gl.start:2 * GLA_KEYS + GLA_WIDTH + gl.stop] = (
                    dgo * on * gn * (sg * (1.0 + gg * (1.0 - sg)))).astype(BF16)
                dgn = dgn + jnp.sum(dgo * sil * on, axis=0, keepdims=True)
                don = dgo * sil * gn
                dos.append(rinv * (don - on * jnp.mean(don * on, axis=1, keepdims=True)))
            do2b = jnp.concatenate(dos, axis=1).astype(BF16)
            qib = tm["qi"].astype(BF16)
            kib = tm["ki"].astype(BF16)
            kdb = tm["kd"].astype(BF16)
            qbb = tm["qb"].astype(BF16)
            dqi = jnp.zeros((GLA_ROWS, LANES), F32)
            dki = jnp.zeros((GLA_ROWS, LANES), F32)
            dvs = []
            for hh in range(2):
                hm = (lane // 64) == hh
                hl = slice(hh * LANES, (hh + 1) * LANES)
                a = jnp.where(causal, _dot_nt(jnp.where(hm, tm["qi"], 0.0).astype(BF16), kib), 0.0).astype(BF16)
                da = jnp.where(causal, _dot_nt(do2b[:, hl], v2b[:, hl]), 0.0).astype(BF16)
                dvs.append(_dot_tn(a, do2b[:, hl]))
                dqi = dqi + jnp.where(hm, _dot(da, kib), 0.0)
                dki = dki + jnp.where(hm, _dot_tn(da, qib), 0.0)
            pre_p.append(dict(tm=tm, v2b=v2b, do2b=do2b, kdb=kdb, qbb=qbb, dqi=dqi, dki=dki, dvs=dvs))
        dstate = [ds_scr[0], ds_scr[1]]
        rec = [dict(dv_st=[None] * cps, dqb=[None] * cps, dkd=[None] * cps, dd=[None] * cps) for _ in range(2)]
        for cc in reversed(range(cps)):
            rows = slice(cc * CHUNK, (cc + 1) * CHUNK)
            for p in range(2):
                w, ds2 = pre_p[p], dstate[p]
                s_prev = st_ref[cc, p]
                ds2b = ds2.astype(BF16)
                rec[p]["dv_st"][cc] = _dot(w["kdb"][rows], ds2b)
                rec[p]["dqb"][cc] = _dot_nt(w["do2b"][rows], s_prev.astype(BF16))
                rec[p]["dkd"][cc] = _dot_nt(w["v2b"][rows], ds2b)
                decay_row = w["tm"]["decay"][cc * CHUNK:cc * CHUNK + 1]
                ddecay_col = jnp.sum(ds2 * s_prev, axis=1, keepdims=True)
                ddecay_row = jnp.transpose(jnp.broadcast_to(ddecay_col, (LANES, LANES)))[0:1, :]
                rec[p]["dd"][cc] = jnp.broadcast_to(ddecay_row * decay_row, (CHUNK, LANES))
                dcol = _row_to_col(decay_row)
                dstate[p] = (jnp.where(blockdiag, _dot_tn(w["qbb"][rows], w["do2b"][rows]), 0.0)
                             + ds2 * jnp.concatenate([dcol, dcol], axis=1))
        for p in range(2):
            ds_scr[p] = dstate[p]
            tm, dqi, dki = pre_p[p]["tm"], pre_p[p]["dqi"], pre_p[p]["dki"]
            dv2 = jnp.concatenate(pre_p[p]["dvs"], axis=1) + jnp.concatenate(rec[p]["dv_st"], axis=0)
            dqb = jnp.concatenate(rec[p]["dqb"], axis=0)
            dkd = jnp.concatenate(rec[p]["dkd"], axis=0)
            dd = rec[p]["dd"]
            dqs = dqi * tm["e_q"] + dqb * tm["e_b"]
            dk = dki * tm["e_k"] + dkd * tm["e_d"]
            t_qi = dqi * tm["qi"]
            t_ki = dki * tm["ki"]
            t_kd = dkd * tm["kd"]
            db = t_qi - t_ki + dqb * tm["qb"] - t_kd
            to_mid = t_ki - t_qi
            to_last = t_kd + jnp.where(rowid % CHUNK == CHUNK - 1, jnp.concatenate(dd, axis=0), 0.0)
            dgs.append(_dot3(bwd_stack, jnp.concatenate([db, to_mid, to_last], axis=0)))
            d_ref[:, p * LANES:(p + 1) * LANES] = (dqs * GLA_SCALE).astype(BF16)
            d_ref[:, GLA_KEYS + p * LANES:GLA_KEYS + (p + 1) * LANES] = dk.astype(BF16)
            d_ref[:, 2 * GLA_KEYS + p * 2 * LANES:2 * GLA_KEYS + (p + 1) * 2 * LANES] = dv2.astype(BF16)
        dlog_a = jnp.concatenate(dgs, axis=1)
        dpre = dlog_a * (1.0 / GLA_TAU) * _sigmoid(-pre)
        dpb = dpre.astype(BF16)
        dgn_ref[...] += dgn
        dgbias_ref[...] += jnp.sum(dpre, axis=0, keepdims=True)
        dgup_ref[...] += _dot_tn(ga_ref[...].astype(BF16), dpb)
        d_ref[:, 2 * GLA_KEYS + 2 * GLA_WIDTH:] = _dot_nt(dpb, gup_ref[...].astype(BF16)).astype(BF16)

    rev = lambda i: nsteps - 1 - i
    cb = lambda w, idx: pl.BlockSpec((GLA_ROWS, w), lambda i: (rev(i), idx))
    full = lambda shp: pl.BlockSpec(shp, lambda i: tuple(0 for _ in shp))
    return pl.pallas_call(
        body, name="gla_bwd", grid=(nsteps,),
        in_specs=[cb(256, 6), cb(256, 7), cb(512, 4), cb(512, 5), cb(128, 24), cb(512, 1), cb(512, 0),
                  pl.BlockSpec((cps, 2, LANES, 2 * LANES), lambda i: (rev(i), 0, 0, 0)),
                  full((LANES, GLA_KEYS)), full((1, GLA_KEYS)), full((1, LANES))] + [HBM_SPEC] * n_in,
        out_specs=[pl.BlockSpec((GLA_ROWS, wout), lambda i: (rev(i), 0)),
                   full((LANES, GLA_KEYS)), full((1, GLA_KEYS)), full((1, LANES))] + [HBM_SPEC] * n_out,
        out_shape=[jax.ShapeDtypeStruct((t, wout), BF16), jax.ShapeDtypeStruct((LANES, GLA_KEYS), F32),
                   jax.ShapeDtypeStruct((1, GLA_KEYS), F32), jax.ShapeDtypeStruct((1, LANES), F32)] + rider.out_shapes,
        scratch_shapes=[pltpu.VMEM((2, LANES, 2 * LANES), F32)] + [pltpu.SemaphoreType.DMA((s,)) for s in rider.sems],
        compiler_params=_cparams("arbitrary"),
    )(proj, proj, proj, proj, proj, dcat, oraw, states, gate_up_pad, gate_bias, gnorm, *rider.inputs)


def _ln_stats(r):
    mu = jnp.mean(r, axis=1, keepdims=True)
    xc = r - mu
    rstd = lax.rsqrt(jnp.mean(xc * xc, axis=1, keepdims=True) + LN_EPS)
    return xc * rstd, rstd


def _ln_bwd(dy_g, xhat, rstd):
    return rstd * (dy_g - jnp.mean(dy_g, axis=1, keepdims=True) - xhat * jnp.mean(dy_g * xhat, axis=1, keepdims=True))


def _outproj_ln1(sb_o, gla_o, x, w_out, g1, b1, t, tm=512):
    def body(sb_ref, gl_ref, x_ref, w_ref, g_ref, b_ref, xhat_ref, rstd_ref, h_ref):
        mix = _dot(sb_ref[...], w_ref[0:SB_WIDTH, :]) + _dot(gl_ref[...], w_ref[SB_WIDTH:, :])
        xhat, rstd = _ln_stats(ALPHA * x_ref[...] + mix)
        xhat_ref[...] = xhat
        rstd_ref[...] = rstd
        h_ref[...] = (xhat * g_ref[...] + b_ref[...]).astype(BF16)

    row = lambda w: pl.BlockSpec((tm, w), lambda i: (i, 0))
    full = lambda shp: pl.BlockSpec(shp, lambda i: (0, 0))
    return pl.pallas_call(
        body, name="outproj_ln1", grid=(t // tm,),
        in_specs=[row(SB_WIDTH), row(GLA_WIDTH), row(D_MODEL), full((D_MODEL, D_MODEL)), full((1, D_MODEL)), full((1, D_MODEL))],
        out_specs=[row(D_MODEL), row(1), row(D_MODEL)],
        out_shape=[jax.ShapeDtypeStruct((t, D_MODEL), F32), jax.ShapeDtypeStruct((t, 1), F32),
                   jax.ShapeDtypeStruct((t, D_MODEL), BF16)],
        compiler_params=_cparams("parallel"),
    )(sb_o, gla_o, x, w_out, g1, b1)


_INV_SQRT2 = 1.0 / math.sqrt(2.0)
_INV_SQRT2PI = 1.0 / math.sqrt(2.0 * math.pi)


def _conv3(xs, w_ref, b_ref, half):
    return (w_ref[half, 0:1, :] * pltpu.roll(xs, 2, 0) + w_ref[half, 1:2, :] * pltpu.roll(xs, 1, 0)
            + w_ref[half, 2:3, :] * xs + b_ref[half])


HALO = 16


def _conv_gelu_fwd(up3, conv_w3, conv_b3, t, tr=512, ct=256):
    nct = D_FF // ct
    hb = tr // HALO

    def body(cur_ref, prev_ref, w_ref, b_ref, gm_ref):
        i = pl.program_id(1)
        keep = (i > 0).astype(F32)
        us = []
        for half in range(2):
            xs = jnp.concatenate([prev_ref[half].astype(F32) * keep, cur_ref[half].astype(F32)], axis=0)
            us.append(_conv3(xs, w_ref, b_ref, half)[HALO:, :])
        a, c = us
        gelu = 0.5 * a * (1.0 + lax.erf(a * _INV_SQRT2))
        gm_ref[...] = (gelu * c).astype(BF16)

    return pl.pallas_call(
        body, name="conv_gelu_fwd", grid=(nct, t // tr),
        in_specs=[pl.BlockSpec((2, tr, ct), lambda j, i: (0, i, j)),
                  pl.BlockSpec((2, HALO, ct), lambda j, i: (0, jnp.maximum(i * hb - 1, 0), j)),
                  pl.BlockSpec((2, 3, ct), lambda j, i: (0, 0, j)),
                  pl.BlockSpec((2, 1, ct), lambda j, i: (0, 0, j))],
        out_specs=pl.BlockSpec((tr, ct), lambda j, i: (i, j)),
        out_shape=jax.ShapeDtypeStruct((t, D_FF), BF16),
        compiler_params=_cparams("parallel", "parallel"),
    )(up3, up3, conv_w3, conv_b3)


def _conv_gelu_bwd(up3, dgm, conv_w3, conv_b3, t, tr=512, ct=256):
    nct = D_FF // ct
    nrt = t // tr
    hb = tr // HALO
    n = tr + 2 * HALO
    lo, hi = HALO, tr + HALO

    def body(cur_ref, prev_ref, next_ref, dg_ref, dgn_ref, w_ref, b_ref, dup_ref, dcw_ref, dcb_ref):
        i = pl.program_id(1)

        @pl.when(i == 0)
        def _():
            dcw_ref[...] = jnp.zeros_like(dcw_ref)
            dcb_ref[...] = jnp.zeros_like(dcb_ref)

        keep_prev = (i > 0).astype(F32)
        keep_next = (i < nrt - 1).astype(F32)
        xs, xm1, xm2, us = [], [], [], []
        for half in range(2):
            x = jnp.concatenate([prev_ref[half].astype(F32) * keep_prev, cur_ref[half].astype(F32),
                                 next_ref[half].astype(F32)], axis=0)
            xs.append(x)
            xm1.append(pltpu.roll(x, 1, 0))
            xm2.append(pltpu.roll(x, 2, 0))
            us.append(w_ref[half, 0:1, :] * xm2[half] + w_ref[half, 1:2, :] * xm1[half]
                      + w_ref[half, 2:3, :] * x + b_ref[half])
        a, c = us
        dg = jnp.concatenate([jnp.zeros((HALO, ct), F32), dg_ref[...].astype(F32),
                              dgn_ref[...].astype(F32) * keep_next], axis=0)
        cdf = 0.5 * (1.0 + lax.erf(a * _INV_SQRT2))
        pdf = jnp.exp(-0.5 * a * a) * _INV_SQRT2PI
        dus = [dg * c * (cdf + a * pdf), dg * (a * cdf)]
        rid = lax.broadcasted_iota(jnp.int32, (8, 1), 0)
        for half in range(2):
            du = dus[half]
            dup = (w_ref[half, 2:3, :] * du + w_ref[half, 1:2, :] * pltpu.roll(du, n - 1, 0)
                   + w_ref[half, 0:1, :] * pltpu.roll(du, n - 2, 0))
            dup_ref[half] = dup[lo:hi, :].astype(BF16)
            duc = du[lo:hi, :]
            s0 = jnp.sum(duc * xm2[half][lo:hi, :], axis=0, keepdims=True)
            s1 = jnp.sum(duc * xm1[half][lo:hi, :], axis=0, keepdims=True)
            s2 = jnp.sum(duc * xs[half][lo:hi, :], axis=0, keepdims=True)
            dcw_ref[half] += jnp.where(rid == 0, s0, jnp.where(rid == 1, s1, jnp.where(rid == 2, s2, 0.0)))
            dcb_ref[half] += jnp.sum(duc, axis=0, keepdims=True)

    last = t // HALO - 1
    return pl.pallas_call(
        body, name="conv_gelu_bwd", grid=(nct, nrt),
        in_specs=[pl.BlockSpec((2, tr, ct), lambda j, i: (0, i, j)),
                  pl.BlockSpec((2, HALO, ct), lambda j, i: (0, jnp.maximum(i * hb - 1, 0), j)),
                  pl.BlockSpec((2, HALO, ct), lambda j, i: (0, jnp.minimum((i + 1) * hb, last), j)),
                  pl.BlockSpec((tr, ct), lambda j, i: (i, j)),
                  pl.BlockSpec((HALO, ct), lambda j, i: (jnp.minimum((i + 1) * hb, last), j)),
                  pl.BlockSpec((2, 3, ct), lambda j, i: (0, 0, j)),
                  pl.BlockSpec((2, 1, ct), lambda j, i: (0, 0, j))],
        out_specs=[pl.BlockSpec((2, tr, ct), lambda j, i: (0, i, j)),
                   pl.BlockSpec((2, 8, ct), lambda j, i: (0, 0, j)),
                   pl.BlockSpec((2, 1, ct), lambda j, i: (0, 0, j))],
        out_shape=[jax.ShapeDtypeStruct((2, t, D_FF), BF16), jax.ShapeDtypeStruct((2, 8, D_FF), F32),
                   jax.ShapeDtypeStruct((2, 1, D_FF), F32)],
        compiler_params=_cparams("parallel", "arbitrary"),
    )(up3, up3, up3, dgm, dgm, conv_w3, conv_b3)


def _down_ln2_loss(gm, w_down, xhat1, g1, b1, g2, b2, target, t, tm=512):
    def body(gm_ref, w_ref, xh_ref, g1_ref, b1_ref, g2_ref, b2_ref, tg_ref, dr_ref, drb_ref, loss_ref, dg_ref, db_ref):
        i = pl.program_id(0)

        @pl.when(i == 0)
        def _():
            loss_ref[...] = jnp.zeros_like(loss_ref)
            dg_ref[...] = jnp.zeros_like(dg_ref)
            db_ref[...] = jnp.zeros_like(db_ref)

        h = xh_ref[...] * g1_ref[...] + b1_ref[...]
        xhat, rstd = _ln_stats(ALPHA * h + _dot(gm_ref[...], w_ref[...]))
        err = xhat * g2_ref[...] + b2_ref[...] - tg_ref[...]
        loss_ref[...] += 0.5 * jnp.sum(jnp.sum(err * err, axis=1, keepdims=True), axis=0, keepdims=True) / D_MODEL
        dy = err * (1.0 / D_MODEL)
        dg_ref[...] += jnp.sum(dy * xhat, axis=0, keepdims=True)
        db_ref[...] += jnp.sum(dy, axis=0, keepdims=True)
        dr = _ln_bwd(dy * g2_ref[...], xhat, rstd)
        dr_ref[...] = dr
        drb_ref[...] = dr.astype(BF16)

    row = lambda w: pl.BlockSpec((tm, w), lambda i: (i, 0))
    full = lambda shp: pl.BlockSpec(shp, lambda i: (0, 0))
    vec = full((1, D_MODEL))
    return pl.pallas_call(
        body, name="down_ln2_loss", grid=(t // tm,),
        in_specs=[row(D_FF), full((D_FF, D_MODEL)), row(D_MODEL), vec, vec, vec, vec, row(D_MODEL)],
        out_specs=[row(D_MODEL), row(D_MODEL), full((1, 1)), vec, vec],
        out_shape=[jax.ShapeDtypeStruct((t, D_MODEL), F32), jax.ShapeDtypeStruct((t, D_MODEL), BF16),
                   jax.ShapeDtypeStruct((1, 1), F32),
                   jax.ShapeDtypeStruct((1, D_MODEL), F32), jax.ShapeDtypeStruct((1, D_MODEL), F32)],
        compiler_params=_cparams("arbitrary"),
    )(gm, w_down, xhat1, g1, b1, g2, b2, target)


def _dh_ln1_bwd(dup3, w_up4, dr2, xhat1, rstd1, g1, t, tm=256):
    ws = 2 * D_FF // 4

    def body(a_ref, w_ref, dr2_ref, xh_ref, rs_ref, g_ref, dr1_ref, dg_ref, db_ref):
        i = pl.program_id(0)

        @pl.when(i == 0)
        def _():
            dg_ref[...] = jnp.zeros_like(dg_ref)
            db_ref[...] = jnp.zeros_like(db_ref)

        dh = ALPHA * dr2_ref[...]
        for s in range(4):
            dh = dh + _dot_nt(a_ref[s // 2, :, (s % 2) * ws:(s % 2 + 1) * ws], w_ref[s])
        xhat = xh_ref[...]
        dg_ref[...] += jnp.sum(dh * xhat, axis=0, keepdims=True)
        db_ref[...] += jnp.sum(dh, axis=0, keepdims=True)
        dr1_ref[...] = _ln_bwd(dh * g_ref[...], xhat, rs_ref[...])

    row = lambda w: pl.BlockSpec((tm, w), lambda i: (i, 0))
    vec = pl.BlockSpec((1, D_MODEL), lambda i: (0, 0))
    return pl.pallas_call(
        body, name="dh_ln1_bwd", grid=(t // tm,),
        in_specs=[pl.BlockSpec((2, tm, D_FF), lambda i: (0, i, 0)),
                  pl.BlockSpec((4, D_MODEL, ws), lambda i: (0, 0, 0)),
                  row(D_MODEL), row(D_MODEL), row(1), vec],
        out_specs=[row(D_MODEL), vec, vec],
        out_shape=[jax.ShapeDtypeStruct((t, D_MODEL), F32), jax.ShapeDtypeStruct((1, D_MODEL), F32),
                   jax.ShapeDtypeStruct((1, D_MODEL), F32)],
        compiler_params=_cparams("arbitrary"),
    )(dup3, w_up4, dr2, xhat1, rstd1, g1)


def _adamw(w, g, m, v, name):
    rows, cols = w.shape
    tr = rows
    for cand in (256, 128, 64, 32, 16, 8):
        if rows % cand == 0 and rows > cand:
            tr = cand
            break
    c1 = 1.0 / (1.0 - ADAM_B1 ** ADAM_STEP)
    c2 = 1.0 / (1.0 - ADAM_B2 ** ADAM_STEP)

    def body(w_ref, g_ref, m_ref, v_ref, d_ref, nm_ref, nv_ref):
        gv = g_ref[...]
        nm = ADAM_B1 * m_ref[...] + (1.0 - ADAM_B1) * gv
        nv = ADAM_B2 * v_ref[...] + (1.0 - ADAM_B2) * (gv * gv)
        d_ref[...] = -ADAM_LR * ((nm * c1) / (jnp.sqrt(nv * c2) + ADAM_EPS) + ADAM_WD * w_ref[...])
        nm_ref[...] = nm
        nv_ref[...] = nv

    spec = pl.BlockSpec((tr, cols), lambda i: (i, 0))
    out = jax.ShapeDtypeStruct((rows, cols), F32)
    return pl.pallas_call(
        body, name=name, grid=(rows // tr,), in_specs=[spec] * 4, out_specs=[spec] * 3, out_shape=[out] * 3,
        compiler_params=_cparams("parallel"),
    )(w, g, m, v)


def _local_step(x, target, w_in_p, late_shards, gate_up_pad, gate_bias, gnorm, ln1_g, ln1_b, conv_w3, conv_b3,
                ln2_g, ln2_b, c_arr, kc_arr):
    t = x.shape[0]
    tq = min(t, 1024)
    s_up, s_out, s_down = late_shards
    sh_up, sh_out, sh_down = LATE_SHAPES
    proj, out_partly = _mm(x, w_in_p, m=t, n=IN_PAD, k=D_MODEL, tm=512, tn=IN_PAD, tk=D_MODEL, name="proj",
                           rider=_gather_rider([s_out], [sh_out]))
    sb_o, sb_mass, up_partly = _sb_fwd(proj, t, _gather_rider([s_up], [sh_up]))
    gla_o, oraw, states, w_up4, w_out4, down_partly = _gla_fwd(
        proj, gate_up_pad, gate_bias, gnorm, t,
        _join(_forward_rider([up_partly, out_partly], [sh_up, sh_out]), _gather_rider([s_down], [sh_down])))
    w_out = w_out4.reshape(D_MODEL, D_MODEL)
    xhat1, rstd1, h_bf = _outproj_ln1(sb_o, gla_o, x, w_out, ln1_g, ln1_b, t)
    tu = min(t, 2048)
    up3, w_down4 = _mm(h_bf, w_up4, m=t, n=2 * D_FF, k=D_MODEL, tm=tu, tn=W_UP_S, tk=D_MODEL, name="up",
                       b_spec=pl.BlockSpec((None, D_MODEL, W_UP_S), lambda i, j, kk: (j, 0, 0)),
                       o_spec=pl.BlockSpec((None, tu, W_UP_S), lambda i, j, kk: (j // 2, i, j % 2)),
                       out_shape=jax.ShapeDtypeStruct((2, t, D_FF), BF16), out_dtype=BF16,
                       rider=_forward_rider([down_partly], [sh_down]))
    w_down = w_down4.reshape(D_FF, D_MODEL)
    gm = _conv_gelu_fwd(up3, conv_w3, conv_b3, t, tr=min(t, 2048))
    dr2, dr2b, loss, dln2_g, dln2_b = _down_ln2_loss(gm, w_down, xhat1, ln1_g, ln1_b, ln2_g, ln2_b, target, t)
    dgm = _mm(dr2b, w_down, m=t, n=D_FF, k=D_MODEL, tm=min(t, 2048), tn=W_UP_S, tk=D_MODEL, tb=True, out_dtype=BF16,
              name="dgm")
    dw_down = _mm(gm, dr2b, m=D_FF, n=D_MODEL, k=t, tm=W_UP_S, tn=D_MODEL, tk=min(t, 2048), ta=True, name="dw_down")
    dup3, dcw, dcb = _conv_gelu_bwd(up3, dgm, conv_w3, conv_b3, t, tr=tq)
    dr1, dln1_g, dln1_b = _dh_ln1_bwd(dup3, w_up4, dr2, xhat1, rstd1, ln1_g, t)
    dw_up4 = _mm(h_bf, dup3, m=D_MODEL, n=2 * D_FF, k=t, tm=512, tn=W_UP_S, tk=t, ta=True, name="dw_up",
                 b_spec=pl.BlockSpec((None, t, W_UP_S), lambda i, j, kk: (j // 2, kk, j % 2)),
                 o_spec=pl.BlockSpec((None, 512, W_UP_S), lambda i, j, kk: (j, i, 0)),
                 out_shape=jax.ShapeDtypeStruct((4, D_MODEL, W_UP_S), F32))
    dw_out = _dw_out(sb_o, gla_o, dr1, t, tq)
    gs = [dw_up4, dw_out.reshape(4, W_OUT_S, D_MODEL), dw_down.reshape(4, W_DOWN_S, D_MODEL)]
    dcat = _mm(dr1, w_out, m=t, n=D_MODEL, k=D_MODEL, tm=tq, tn=512, tk=D_MODEL, tb=True, name="dcat")
    dgla, dgup_pad, dgbias, dgnorm, *from_sib = _gla_bwd(proj, dcat, oraw, states, gate_up_pad, gate_bias, gnorm, t,
                                                         _sibling_rider(gs, LATE_SHAPES))
    ps = [_add_sibling(gs[m], from_sib[m], c_arr, LATE_ADD_ROWS[m], "add_sibling_late_%d" % m) for m in range(3)]
    small = dict(
        gate_up=dgup_pad[:GATE_RANK], gate_bias=dgbias, gla_norm_g=dgnorm, ln1_g=dln1_g, ln1_b=dln1_b,
        conv_w=jnp.concatenate([dcw[0, :3], dcw[1, :3]], axis=1), conv_b=jnp.concatenate([dcb[0], dcb[1]], axis=1),
        ln2_g=dln2_g, ln2_b=dln2_b, loss=loss)
    dsq, dsk, dsv, *others, vecs = _sb_bwd(proj, dcat, sb_mass, t,
                                           _chips_rider(ps, _pack_vec(small, SMALL_GRADS, GRAD_VEC_ROWS)))
    late_sums = [_add_chips(ps[m], others[m], kc_arr, LATE_ADD_ROWS[m], "add_chips_late_%d" % m) for m in range(3)]
    dproj = [dsq, dsk, dsv, dgla]
    dw_in_p = _dw_in(x, dproj, t, tk=tq)
    g_in = dw_in_p[None]
    from_sib_in, = _run(_sibling_rider([g_in], [(D_MODEL, IN_PAD)]), "exchange_sibling_w_in")
    half_in = _add_sibling(g_in, from_sib_in, c_arr, ADD_ROWS[0], "add_sibling_w_in")[0]
    p_in = jnp.stack([half_in[:, k * W_IN_S:(k + 1) * W_IN_S] for k in range(4)], axis=0)
    dx, others_in = _dx(dproj, w_in_p, dr1, t, _chips_rider([p_in]))
    sum_in = _add_chips(p_in, others_in, kc_arr, ADD_ROWS[0], "add_chips_w_in")
    return dx, [sum_in] + late_sums, vecs


W_IN_S, W_UP_S, W_OUT_S, W_DOWN_S = IN_WIDTH // 4, 2 * D_FF // 4, D_MODEL // 4, D_FF // 4
SHARD_SHAPES = ((D_MODEL, W_IN_S), (D_MODEL, W_UP_S), (W_OUT_S, D_MODEL), (W_DOWN_S, D_MODEL))
ADD_ROWS = (256, 256, 128, 176)
LATE_SHAPES, LATE_ADD_ROWS = SHARD_SHAPES[1:], ADD_ROWS[1:]
SMALL_ROWS = 8
VEC_SIZES = (("gate_bias", GLA_KEYS), ("gla_norm_g", LANES), ("ln1_g", D_MODEL), ("ln1_b", D_MODEL),
             ("conv_b", 2 * D_FF), ("ln2_g", D_MODEL), ("ln2_b", D_MODEL))
SMALL_GRADS = VEC_SIZES + (("conv_w", 3 * 2 * D_FF), ("gate_up", GATE_RANK * GLA_KEYS), ("loss", 1))


def _rows(a):
    flat = a.reshape(-1)
    pad = (-flat.shape[0]) % D_MODEL
    if pad:
        flat = jnp.pad(flat, (0, pad))
    return flat.reshape(-1, D_MODEL)


def _pad_rows(a, rows):
    return jnp.pad(a, ((0, rows - a.shape[0]), (0, 0)))


def _pack_vec(d, sizes, rows):
    flat = jnp.concatenate([d[n].reshape(-1) for n, _ in sizes])
    return _pad_rows(_rows(flat), rows)


def _unpack_vec(v, sizes):
    flat = v.reshape(-1)
    out, o = {}, 0
    for n, size in sizes:
        out[n] = flat[o:o + size].reshape(1, size)
        o += size
    return out


VEC_ROWS = 16
GRAD_VEC_ROWS = 32


HBM_SPEC = pl.BlockSpec(memory_space=pltpu.HBM)


def _position():
    x, y, c = lax.axis_index("x"), lax.axis_index("y"), lax.axis_index("c")
    chips = [(1 - x, y), (x, 1 - y), (1 - x, 1 - y)]
    return x, y, c, chips


def _remote(src, dst, send_sems, recv_sems, k, to):
    return pltpu.make_async_remote_copy(src_ref=src, dst_ref=dst, send_sem=send_sems.at[k], recv_sem=recv_sems.at[k],
                                        device_id=to, device_id_type=MESH)


def _gather_ici(in_refs, out_refs, shapes, send_sems, recv_sems, local_sems):
    x, y, c, chips = _position()
    k_me = 2 * x + y
    local, sends, recvs = [], [], []
    for m, (rows, _) in enumerate(shapes):
        h = rows // 2
        local.append(pltpu.make_async_copy(in_refs[m], out_refs[m].at[k_me], local_sems.at[m]))
        for j, (cx, cy) in enumerate(chips):
            sends.append(_remote(in_refs[m].at[pl.ds(c * h, h), :], out_refs[m].at[k_me, pl.ds(c * h, h), :],
                                 send_sems, recv_sems, 3 * m + j, (cx, cy, c)))
            landed = out_refs[m].at[2 * cx + cy, pl.ds(c * h, h), :]
            recvs.append(_remote(landed, landed, send_sems, recv_sems, 3 * m + j, (x, y, c)))
    return local, sends, recvs


def _gather_d2d(src_refs, dst_refs, shapes, send_sems, recv_sems, base):
    x, y, c, chips = _position()
    sends, recvs = [], []
    for m, (rows, _) in enumerate(shapes):
        h = rows // 2
        for j, (cx, cy) in enumerate(chips):
            k = 2 * cx + cy
            sends.append(_remote(src_refs[m].at[k, pl.ds(c * h, h), :], dst_refs[m].at[k, pl.ds(c * h, h), :],
                                 send_sems, recv_sems, base + 3 * m + j, (x, y, 1 - c)))
            landed = dst_refs[m].at[k, pl.ds((1 - c) * h, h), :]
            recvs.append(_remote(landed, landed, send_sems, recv_sems, base + 3 * m + j, (x, y, c)))
    return sends, recvs


def _gather_weights(shards, small, shapes):
    nm = len(shards)
    n_ici = 3 * nm

    def body(*refs):
        in_refs, small_ref = refs[:nm], refs[nm]
        out_refs, osm_ref = refs[nm + 1:2 * nm + 1], refs[2 * nm + 1]
        send_sems, recv_sems, local_sems = refs[2 * nm + 2:]
        x, y, c, chips = _position()
        k_me = 2 * x + y
        local, sends, recvs = _gather_ici(in_refs, out_refs, shapes, send_sems, recv_sems, local_sems)
        local.append(pltpu.make_async_copy(small_ref, osm_ref.at[k_me], local_sems.at[nm]))
        for j, (cx, cy) in enumerate(chips):
            sends.append(_remote(small_ref, osm_ref.at[k_me], send_sems, recv_sems, n_ici + j, (cx, cy, c)))
        for cp in local + sends:
            cp.start()
        fsends, frecvs = _gather_d2d(out_refs, out_refs, shapes, send_sems, recv_sems, n_ici + 3)
        for landed, onward in zip(recvs, fsends):
            landed.wait_recv()
            onward.start()
        for j, (cx, cy) in enumerate(chips):
            k = 2 * cx + cy
            frecvs.append(_remote(osm_ref.at[k], osm_ref.at[k], send_sems, recv_sems, n_ici + j, (x, y, c)))
        for cp in frecvs:
            cp.wait_recv()
        for cp in sends + fsends:
            cp.wait_send()
        for cp in local:
            cp.wait()

    n_sems = 2 * n_ici + 3
    return pl.pallas_call(
        body, name="gather_weights", in_specs=[HBM_SPEC] * (nm + 1), out_specs=[HBM_SPEC] * (nm + 1),
        out_shape=[jax.ShapeDtypeStruct((4,) + s, BF16) for s in shapes]
        + [jax.ShapeDtypeStruct((4, SMALL_ROWS, D_MODEL), F32)],
        scratch_shapes=[pltpu.SemaphoreType.DMA((n_sems,)), pltpu.SemaphoreType.DMA((n_sems,)),
                        pltpu.SemaphoreType.DMA((nm + 1,))],
    )(*shards, small)


def _gather_rider(shards, shapes):
    n = len(shards)
    return _Rider(shards, [jax.ShapeDtypeStruct((4,) + s, BF16) for s in shapes], (3 * n, 3 * n, n),
                  lambda ins, outs, sems: _gather_ici(ins, outs, shapes, *sems))


def _forward_rider(gathered, shapes):
    n = len(gathered)
    return _Rider(gathered, [jax.ShapeDtypeStruct(a.shape, a.dtype) for a in gathered], (3 * n, 3 * n),
                  lambda ins, outs, sems: ([],) + _gather_d2d(ins, outs, shapes, sems[0], sems[1], 0),
                  aliases=[(m, m) for m in range(n)])


def _sibling_rider(gs, shapes):
    def copies(ins, outs, sems):
        x, y, c, _ = _position()
        both = []
        for m, (rows, _) in enumerate(shapes):
            h = rows // 2
            for k in range(gs[m].shape[0]):
                both.append(_remote(ins[m].at[k, pl.ds((1 - c) * h, h), :], outs[m].at[k], sems[0], sems[1],
                                    4 * m + k, (x, y, 1 - c)))
        return [], both, both

    return _Rider(gs, [jax.ShapeDtypeStruct((g.shape[0], r // 2, cl), F32) for g, (r, cl) in zip(gs, shapes)],
                  (4 * len(gs), 4 * len(gs)), copies)


def _add_sibling(g, r, c_arr, tr, name):
    nblk, rows, cols = g.shape
    nb = rows // 2 // tr

    def body(c_ref, g_ref, r_ref, o_ref):
        o_ref[...] = (g_ref[...] + r_ref[...]).astype(BF16)

    spec = pl.BlockSpec((None, tr, cols), lambda k, i, c: (k, i, 0))
    return pl.pallas_call(
        body, name=name,
        grid_spec=pltpu.PrefetchScalarGridSpec(
            num_scalar_prefetch=1, grid=(nblk, nb),
            in_specs=[pl.BlockSpec((None, tr, cols), lambda k, i, c: (k, c[0] * nb + i, 0)), spec], out_specs=spec),
        out_shape=jax.ShapeDtypeStruct((nblk, rows // 2, cols), BF16), compiler_params=_cparams("parallel", "parallel"),
    )(c_arr, g, r)


def _reduce_ici(p_refs, r_refs, send_sems, recv_sems):
    x, y, c, chips = _position()
    sends, recvs = [], []
    for m in range(len(p_refs)):
        for j, (cx, cy) in enumerate(chips):
            sends.append(_remote(p_refs[m].at[2 * cx + cy], r_refs[m].at[j], send_sems, recv_sems, 3 * m + j, (cx, cy, c)))
            recvs.append(_remote(r_refs[m].at[j], r_refs[m].at[j], send_sems, recv_sems, 3 * m + j, (x, y, c)))
    return sends, recvs


def _chips_rider(ps, vec=None):
    nm = len(ps)
    n_ici = 3 * nm

    def copies(ins, outs, sems):
        sends, recvs = _reduce_ici(ins[:nm], outs[:nm], sems[0], sems[1])
        if vec is None:
            return [], sends, recvs
        x, y, c, _ = _position()
        my_id = 4 * x + 2 * y + c
        vec_ref, vrecv_ref = ins[nm], outs[nm]
        local = [pltpu.make_async_copy(vec_ref, vrecv_ref.at[my_id], sems[2].at[0])]
        for r in range(1, 8):
            peer = (1 - x if r & 4 else x, 1 - y if r & 2 else y, 1 - c if r & 1 else c)
            sends.append(_remote(vec_ref, vrecv_ref.at[my_id], sems[0], sems[1], n_ici + r - 1, peer))
            recvs.append(_remote(vec_ref, vrecv_ref.at[0], sems[0], sems[1], n_ici + r - 1, (x, y, c)))
        return local, sends, recvs

    out_shapes = [jax.ShapeDtypeStruct((3,) + p.shape[1:], p.dtype) for p in ps]
    if vec is None:
        return _Rider(ps, out_shapes, (n_ici, n_ici), copies)
    return _Rider(list(ps) + [vec], out_shapes + [jax.ShapeDtypeStruct((8, GRAD_VEC_ROWS, D_MODEL), F32)],
                  (n_ici + 7, n_ici + 7, 1), copies)


def _add_chips(p, r, kc_arr, tr, name):
    _, h, cols = p.shape
    nb = h // tr

    def body(kc_ref, p_ref, r0_ref, r1_ref, r2_ref, o_ref):
        o_ref[...] = ((p_ref[...].astype(F32) + r0_ref[...].astype(F32)) + r1_ref[...].astype(F32)) + r2_ref[...].astype(F32)

    rspec = lambda j: pl.BlockSpec((None, tr, cols), lambda i, kc: (j, i, 0))
    return pl.pallas_call(
        body, name=name,
        grid_spec=pltpu.PrefetchScalarGridSpec(
            num_scalar_prefetch=1, grid=(nb,),
            in_specs=[pl.BlockSpec((None, tr, cols), lambda i, kc: (kc[0], i, 0)), rspec(0), rspec(1), rspec(2)],
            out_specs=pl.BlockSpec((tr, cols), lambda i, kc: (kc[1] * nb + i, 0))),
        out_shape=jax.ShapeDtypeStruct((2 * h, cols), F32), compiler_params=_cparams("parallel"),
    )(kc_arr, p, r, r, r)


def _reunite_sibling(fs, shapes):
    n_chunks = 2
    nm = len(fs)

    def body(*refs):
        in_refs, out_refs = refs[:nm], refs[nm:2 * nm]
        send_sems, recv_sems = refs[2 * nm:]
        x, y, c, _ = _position()
        sends, recvs = [], []
        for m in range(nm):
            ch = shapes[m][0] // 2 // n_chunks
            for q in range(n_chunks):
                mine = pl.ds((c * n_chunks + q) * ch, ch)
                theirs = pl.ds(((1 - c) * n_chunks + q) * ch, ch)
                s = n_chunks * m + q
                sends.append(_remote(in_refs[m].at[mine, :], out_refs[m].at[mine, :], send_sems, recv_sems, s, (x, y, 1 - c)))
                recvs.append(_remote(in_refs[m].at[theirs, :], out_refs[m].at[theirs, :], send_sems, recv_sems, s, (x, y, c)))
        for cp in sends:
            cp.start()
        for cp in recvs:
            cp.wait_recv()
        for cp in sends:
            cp.wait_send()

    n_sems = n_chunks * nm
    return pl.pallas_call(
        body, name="reunite_sibling", in_specs=[HBM_SPEC] * nm, out_specs=[HBM_SPEC] * nm,
        out_shape=[jax.ShapeDtypeStruct(s, F32) for s in shapes],
        input_output_aliases={m: m for m in range(nm)},
        scratch_shapes=[pltpu.SemaphoreType.DMA((n_sems,)), pltpu.SemaphoreType.DMA((n_sems,))],
    )(*fs)


def _sum_vecs(v):
    def body(v_ref, o_ref):
        acc = v_ref[0]
        for d in range(1, 8):
            acc = acc + v_ref[d]
        o_ref[...] = acc

    return pl.pallas_call(body, name="sum_vecs", out_shape=jax.ShapeDtypeStruct(v.shape[1:], F32))(v)


def kernel(x, w_in, gate_up, gate_bias, gla_norm_g, w_out, ln1_g, ln1_b, w_up, conv_w, conv_b, w_down, ln2_g, ln2_b, loss_target, m_w_in, m_gate_up, m_gate_bias, m_gla_norm_g, m_w_out, m_ln1_g, m_ln1_b, m_w_up, m_conv_w, m_conv_b, m_w_down, m_ln2_g, m_ln2_b, v_w_in, v_gate_up, v_gate_bias, v_gla_norm_g, v_w_out, v_ln1_g, v_ln1_b, v_w_up, v_conv_w, v_conv_b, v_w_down, v_ln2_g, v_ln2_b):
    xi, yi, ci = lax.axis_index("x"), lax.axis_index("y"), lax.axis_index("c")
    k_me = 2 * xi + yi
    c_arr = ci.astype(jnp.int32).reshape(1)
    kc_arr = jnp.stack([k_me, ci]).astype(jnp.int32)
    small = _pad_rows(jnp.concatenate([_rows(conv_w[0]), _rows(gate_up[0])], axis=0), SMALL_ROWS)
    w_in4, gsmall = _gather_weights([w_in[0].astype(BF16)], small, SHARD_SHAPES[:1])
    late_shards = [w_up[0].astype(BF16), w_out[0].astype(BF16), w_down[0].astype(BF16)]
    w_in_p = jnp.pad(jnp.concatenate([w_in4[k] for k in range(4)], axis=1), ((0, 0), (0, IN_PAD - IN_WIDTH)))
    conv_w_f = jnp.concatenate([gsmall[k, :5].reshape(-1)[:3 * W_UP_S].reshape(3, W_UP_S) for k in range(4)], axis=1)
    gate_up_f = jnp.concatenate([gsmall[k, 5].reshape(GATE_RANK, GLA_KEYS // 4) for k in range(4)], axis=1)
    conv_w3 = jnp.transpose(conv_w_f.reshape(3, 2, D_FF), (1, 0, 2))
    conv_b3 = conv_b.reshape(2, 1, D_FF)
    gate_up_pad = jnp.pad(gate_up_f, ((0, LANES - GATE_RANK), (0, 0)))

    dx, sums, vecs = _local_step(
        x[0], loss_target[0], w_in_p, late_shards, gate_up_pad, gate_bias, gla_norm_g, ln1_g, ln1_b, conv_w3, conv_b3,
        ln2_g, ln2_b, c_arr, kc_arr)
    g_w_in, g_w_up, g_w_out, g_w_down = _reunite_sibling(sums, SHARD_SHAPES)
    gsmall_sum = _unpack_vec(_sum_vecs(vecs), SMALL_GRADS)
    g_conv_w = lax.dynamic_slice_in_dim(gsmall_sum["conv_w"].reshape(3, 2 * D_FF), k_me * W_UP_S, W_UP_S, axis=1)
    g_gate_up = lax.dynamic_slice_in_dim(gsmall_sum["gate_up"].reshape(GATE_RANK, GLA_KEYS), k_me * (GLA_KEYS // 4),
                                         GLA_KEYS // 4, axis=1)
    gv = gsmall_sum
    loss = gv["loss"][0, 0]
    gvec = _pack_vec(gv, VEC_SIZES, VEC_ROWS)

    grads = dict(w_in=g_w_in[None], gate_up=g_gate_up[None], gate_bias=gv["gate_bias"], gla_norm_g=gv["gla_norm_g"],
                 w_out=g_w_out[None], ln1_g=gv["ln1_g"], ln1_b=gv["ln1_b"], w_up=g_w_up[None], conv_w=g_conv_w[None],
                 conv_b=gv["conv_b"], w_down=g_w_down[None], ln2_g=gv["ln2_g"], ln2_b=gv["ln2_b"])
    weights = dict(w_in=w_in, gate_up=gate_up, gate_bias=gate_bias, gla_norm_g=gla_norm_g, w_out=w_out, ln1_g=ln1_g,
                   ln1_b=ln1_b, w_up=w_up, conv_w=conv_w, conv_b=conv_b, w_down=w_down, ln2_g=ln2_g, ln2_b=ln2_b)
    ms = dict(w_in=m_w_in, gate_up=m_gate_up, gate_bias=m_gate_bias, gla_norm_g=m_gla_norm_g, w_out=m_w_out, ln1_g=m_ln1_g,
              ln1_b=m_ln1_b, w_up=m_w_up, conv_w=m_conv_w, conv_b=m_conv_b, w_down=m_w_down, ln2_g=m_ln2_g, ln2_b=m_ln2_b)
    vs = dict(w_in=v_w_in, gate_up=v_gate_up, gate_bias=v_gate_bias, gla_norm_g=v_gla_norm_g, w_out=v_w_out, ln1_g=v_ln1_g,
              ln1_b=v_ln1_b, w_up=v_w_up, conv_w=v_conv_w, conv_b=v_conv_b, w_down=v_w_down, ln2_g=v_ln2_g, ln2_b=v_ln2_b)
    names = ["w_in", "gate_up", "gate_bias", "gla_norm_g", "w_out", "ln1_g", "ln1_b", "w_up", "conv_w", "conv_b", "w_down",
             "ln2_g", "ln2_b"]
    delta, new_m, new_v = {}, {}, {}
    for n in ("w_in", "gate_up", "w_out", "w_up", "conv_w", "w_down"):
        tr_ = jnp.transpose if n == "w_in" else (lambda a: a)
        d, nm, nv = _adamw(tr_(weights[n][0]), tr_(grads[n][0]), tr_(ms[n][0]), tr_(vs[n][0]), "adamw_" + n)
        delta[n], new_m[n], new_v[n] = tr_(d)[None], tr_(nm)[None], tr_(nv)[None]
    d, nm, nv = _adamw(_pack_vec(weights, VEC_SIZES, VEC_ROWS), gvec, _pack_vec(ms, VEC_SIZES, VEC_ROWS),
                       _pack_vec(vs, VEC_SIZES, VEC_ROWS), "adamw_vectors")
    for dst, src in ((delta, d), (new_m, nm), (new_v, nv)):
        dst.update(_unpack_vec(src, VEC_SIZES))
    return (loss, dx[None], *[grads[n] for n in names], *[delta[n] for n in names], *[new_m[n] for n in names],
            *[new_v[n] for n in names])
```

```python
import math

import jax
import jax.numpy as jnp
from jax import lax
from jax.experimental import pallas as pl
from jax.experimental.pallas import tpu as pltpu

F32 = jnp.float32
BF16 = jnp.bfloat16

D_MODEL = 1024
SB_WIDTH = 512
GLA_KEYS = 256
GLA_WIDTH = 512
GATE_RANK = 16
IN_WIDTH = 3088
IN_PAD = 3200
D_FF = 2816
CHUNK = 64
LN_EPS = 1e-5
RMS_EPS = 1e-6
ALPHA = 2.0 ** 0.25
GLA_TAU = 16.0
SB_SCALE = 0.125
GLA_SCALE = 0.125
LANES = 128
SB_BLK = 256
SB_CUT = -100.0
GLA_ROWS = 256
VMEM_LIMIT = 56 * 1024 * 1024

ADAM_LR, ADAM_B1, ADAM_B2, ADAM_EPS, ADAM_WD, ADAM_STEP = 0.001, 0.9, 0.999, 1e-08, 0.01, 10

MESH = pl.DeviceIdType.MESH


def _cparams(*sem):
    return pltpu.CompilerParams(dimension_semantics=sem, vmem_limit_bytes=VMEM_LIMIT)


def _dot(a, b):
    return jnp.dot(a, b, preferred_element_type=F32)


def _dot_nt(a, b):
    return lax.dot_general(a, b, (((1,), (1,)), ((), ())), preferred_element_type=F32)


def _dot_tn(a, b):
    return lax.dot_general(a, b, (((0,), (0,)), ((), ())), preferred_element_type=F32)


def _split3(x):
    hi = x.astype(BF16)
    r = x - hi.astype(F32)
    mid = r.astype(BF16)
    lo = (r - mid.astype(F32)).astype(BF16)
    return hi, mid, lo


def _softplus(z):
    return jnp.maximum(z, 0.0) + jnp.log(1.0 + jnp.exp(-jnp.abs(z)))


def _sigmoid(z):
    return 1.0 / (1.0 + jnp.exp(-z))


def _mm(a, b, *, m, n, k, tm, tn, tk, ta=False, tb=False, a_spec=None, b_spec=None, o_spec=None,
        out_shape=None, out_dtype=F32, add=None, add_scale=1.0, rider=None, name):
    nk = k // tk
    dn = (((0 if ta else 1,), (1 if tb else 0,)), ((), ()))
    n_in = len(rider.inputs) if rider else 0
    n_out = len(rider.out_shapes) if rider else 0
    n_add = int(add is not None)
    steps = (m // tm, n // tn, nk)

    def body(*refs):
        a_ref, b_ref = refs[:2]
        add_ref = refs[2] if add is not None else None
        rin = refs[2 + n_add:2 + n_add + n_in]
        o_ref = refs[2 + n_add + n_in]
        rout = refs[3 + n_add + n_in:3 + n_add + n_in + n_out]
        scratch = refs[3 + n_add + n_in + n_out:]
        if rider:
            sems = scratch[len(scratch) - len(rider.sems):]
            ids = [pl.program_id(d) for d in range(3)]
            _ride(rider, rin, rout, sems, (ids[0] == 0) & (ids[1] == 0) & (ids[2] == 0),
                  (ids[0] == steps[0] - 1) & (ids[1] == steps[1] - 1) & (ids[2] == steps[2] - 1))

        part = lax.dot_general(a_ref[...].astype(BF16), b_ref[...].astype(BF16), dn, preferred_element_type=F32)

        def finish(r):
            if add is not None:
                r = r + add_scale * add_ref[...]
            o_ref[...] = r.astype(out_dtype)

        if nk == 1:
            finish(part)
            return
        acc_ref = scratch[0]
        kk = pl.program_id(2)

        @pl.when(kk == 0)
        def _():
            acc_ref[...] = part

        @pl.when((kk > 0) & (kk < nk - 1))
        def _():
            acc_ref[...] += part

        @pl.when(kk == nk - 1)
        def _():
            finish(acc_ref[...] + part)

    if a_spec is None:
        a_spec = pl.BlockSpec((tk, tm), lambda i, j, kk: (kk, i)) if ta else pl.BlockSpec((tm, tk), lambda i, j, kk: (i, kk))
    if b_spec is None:
        b_spec = pl.BlockSpec((tn, tk), lambda i, j, kk: (j, kk)) if tb else pl.BlockSpec((tk, tn), lambda i, j, kk: (kk, j))
    if o_spec is None:
        o_spec = pl.BlockSpec((tm, tn), lambda i, j, kk: (i, j))
    if out_shape is None:
        out_shape = jax.ShapeDtypeStruct((m, n), out_dtype)
    in_specs = [a_spec, b_spec]
    args = [a, b]
    if add is not None:
        in_specs.append(pl.BlockSpec((tm, tn), lambda i, j, kk: (i, j)))
        args.append(add)
    scratch = [pltpu.VMEM((tm, tn), F32)] if nk > 1 else []
    if not rider:
        return pl.pallas_call(
            body, name=name, grid=steps, in_specs=in_specs, out_specs=o_spec, out_shape=out_shape,
            scratch_shapes=scratch, compiler_params=_cparams("parallel", "parallel", "arbitrary"),
        )(*args)
    return pl.pallas_call(
        body, name=name, grid=steps, in_specs=in_specs + [HBM_SPEC] * n_in, out_specs=[o_spec] + [HBM_SPEC] * n_out,
        out_shape=[out_shape] + list(rider.out_shapes),
        input_output_aliases={len(args) + i: 1 + o for i, o in rider.aliases},
        scratch_shapes=scratch + [pltpu.SemaphoreType.DMA((s,)) for s in rider.sems],
        compiler_params=_cparams("arbitrary", "arbitrary", "arbitrary"),
    )(*args, *rider.inputs)


def _col_offsets(pieces):
    offs, o = [], 0
    for a in pieces:
        offs.append(o)
        o += a.shape[1]
    return offs


def _dx(pieces, w_in_p, dr1, t, rider, tm=512):
    offs = _col_offsets(pieces)
    npc = len(pieces)
    n_in, n_out = len(rider.inputs), len(rider.out_shapes)
    steps = t // tm

    def body(*refs):
        p_refs, w_ref, add_ref = refs[:npc], refs[npc], refs[npc + 1]
        rin = refs[npc + 2:npc + 2 + n_in]
        o_ref = refs[npc + 2 + n_in]
        rout = refs[npc + 3 + n_in:npc + 3 + n_in + n_out]
        i = pl.program_id(0)
        _ride(rider, rin, rout, refs[npc + 3 + n_in + n_out:], i == 0, i == steps - 1)
        acc = ALPHA * add_ref[...]
        for p_ref, off in zip(p_refs, offs):
            acc = acc + _dot_nt(p_ref[...].astype(BF16), w_ref[:, off:off + p_ref.shape[1]])
        o_ref[...] = acc

    row = lambda w: pl.BlockSpec((tm, w), lambda i: (i, 0))
    return pl.pallas_call(
        body, name="dx", grid=(steps,),
        in_specs=[row(a.shape[1]) for a in pieces] + [pl.BlockSpec(w_in_p.shape, lambda i: (0, 0)), row(D_MODEL)]
        + [HBM_SPEC] * n_in,
        out_specs=[row(D_MODEL)] + [HBM_SPEC] * n_out,
        out_shape=[jax.ShapeDtypeStruct((t, D_MODEL), F32)] + rider.out_shapes,
        scratch_shapes=[pltpu.SemaphoreType.DMA((s,)) for s in rider.sems],
        compiler_params=_cparams("arbitrary"),
    )(*pieces, w_in_p, dr1, *rider.inputs)


def _dw_out(sb_o, gla_o, dr1, t, tk):
    def body(sb_ref, gl_ref, dr_ref, o_ref):
        kk = pl.program_id(0)
        drb = dr_ref[...].astype(BF16)
        top, bottom = _dot_tn(sb_ref[...], drb), _dot_tn(gl_ref[...], drb)

        @pl.when(kk == 0)
        def _():
            o_ref[0:SB_WIDTH, :] = top
            o_ref[SB_WIDTH:, :] = bottom

        @pl.when(kk > 0)
        def _():
            o_ref[0:SB_WIDTH, :] += top
            o_ref[SB_WIDTH:, :] += bottom

    return pl.pallas_call(
        body, name="dw_out", grid=(t // tk,),
        in_specs=[pl.BlockSpec((tk, SB_WIDTH), lambda kk: (kk, 0)), pl.BlockSpec((tk, GLA_WIDTH), lambda kk: (kk, 0)),
                  pl.BlockSpec((tk, D_MODEL), lambda kk: (kk, 0))],
        out_specs=pl.BlockSpec((D_MODEL, D_MODEL), lambda kk: (0, 0)),
        out_shape=jax.ShapeDtypeStruct((D_MODEL, D_MODEL), F32),
        compiler_params=_cparams("arbitrary"),
    )(sb_o, gla_o, dr1)


def _dw_in(x, pieces, t, tm=512, tk=512):
    offs = _col_offsets(pieces)
    npc = len(pieces)
    n = offs[-1] + pieces[-1].shape[1]
    nk = t // tk

    def body(*refs):
        x_ref, p_refs, o_ref, acc_ref = refs[0], refs[1:1 + npc], refs[1 + npc], refs[2 + npc]
        kk = pl.program_id(1)
        xb = x_ref[...].astype(BF16)
        for p_ref, off in zip(p_refs, offs):
            cols = slice(off, off + p_ref.shape[1])
            part = _dot_tn(xb, p_ref[...].astype(BF16))
            if nk == 1:
                o_ref[:, cols] = part
                continue

            @pl.when(kk == 0)
            def _():
                acc_ref[:, cols] = part

            @pl.when((kk > 0) & (kk < nk - 1))
            def _():
                acc_ref[:, cols] += part

            @pl.when(kk == nk - 1)
            def _():
                o_ref[:, cols] = acc_ref[:, cols] + part

    return pl.pallas_call(
        body, name="dw_in", grid=(D_MODEL // tm, nk),
        in_specs=[pl.BlockSpec((tk, tm), lambda i, kk: (kk, i))]
        + [pl.BlockSpec((tk, a.shape[1]), lambda i, kk: (kk, 0)) for a in pieces],
        out_specs=pl.BlockSpec((tm, n), lambda i, kk: (i, 0)),
        out_shape=jax.ShapeDtypeStruct((D_MODEL, n), F32),
        scratch_shapes=[pltpu.VMEM((tm, n), F32)],
        compiler_params=_cparams("parallel", "arbitrary"),
    )(x, *pieces)


class _Rider:
    def __init__(self, inputs, out_shapes, sems, copies, aliases=()):
        self.inputs, self.out_shapes, self.sems, self.copies = list(inputs), list(out_shapes), tuple(sems), copies
        self.aliases = tuple(aliases)


def _join(a, b):
    na_in, na_out, na_sems = len(a.inputs), len(a.out_shapes), len(a.sems)

    def copies(ins, outs, sems):
        first = a.copies(ins[:na_in], outs[:na_out], sems[:na_sems])
        second = b.copies(ins[na_in:], outs[na_out:], sems[na_sems:])
        return tuple(u + v for u, v in zip(first, second))

    return _Rider(a.inputs + b.inputs, a.out_shapes + b.out_shapes, a.sems + b.sems, copies,
                  a.aliases + tuple((i + na_in, o + na_out) for i, o in b.aliases))


def _ride(rider, in_refs, out_refs, sems, first, last):
    @pl.when(first)
    def _():
        local, sends, _ = rider.copies(in_refs, out_refs, sems)
        for cp in local + sends:
            cp.start()

    @pl.when(last)
    def _():
        local, sends, recvs = rider.copies(in_refs, out_refs, sems)
        for cp in recvs:
            cp.wait_recv()
        for cp in sends:
            cp.wait_send()
        for cp in local:
            cp.wait()


def _run(rider, name):
    n_in, n_out = len(rider.inputs), len(rider.out_shapes)

    def body(*refs):
        local, sends, recvs = rider.copies(refs[:n_in], refs[n_in:n_in + n_out], refs[n_in + n_out:])
        for cp in local + sends:
            cp.start()
        for cp in recvs:
            cp.wait_recv()
        for cp in sends:
            cp.wait_send()
        for cp in local:
            cp.wait()

    return pl.pallas_call(
        body, name=name, in_specs=[HBM_SPEC] * n_in, out_specs=[HBM_SPEC] * n_out, out_shape=list(rider.out_shapes),
        scratch_shapes=[pltpu.SemaphoreType.DMA((s,)) for s in rider.sems],
    )(*rider.inputs)


def _sb_tile(qh, kj, diag, strict, u_strict, r_in):
    z = _dot_nt(qh, kj)
    sp = _softplus(z)
    l1m = -sp
    lsz = z - sp
    if diag:
        l1m = jnp.where(strict, l1m, 0.0)
    cs = _dot(l1m.astype(BF16), u_strict) + r_in
    w = jnp.exp(lsz + cs)
    if diag:
        w = jnp.where(strict, w, 0.0)
    return l1m, lsz, w


def _sb_consts():
    row = lax.broadcasted_iota(jnp.int32, (SB_BLK, SB_BLK), 0)
    col = lax.broadcasted_iota(jnp.int32, (SB_BLK, SB_BLK), 1)
    strict = col < row
    u_strict = (row > col).astype(BF16)
    u_pre = (row < col).astype(BF16)
    lane = lax.broadcasted_iota(jnp.int32, (1, LANES), 1)
    return strict, u_strict, u_pre, lane


def _sb_fwd(proj, t, rider):
    nq = t // SB_BLK
    n_in, n_out = len(rider.inputs), len(rider.out_shapes)

    def body(q_ref, k_ref, v_ref, *rest):
        rin, o_ref, sv_ref, rout = rest[:n_in], rest[n_in], rest[n_in + 1], rest[n_in + 2:n_in + 2 + n_out]
        p = pl.program_id(0)
        i = pl.program_id(1)
        _ride(rider, rin, rout, rest[n_in + 2 + n_out:], (p == 0) & (i == 0), (p == 3) & (i == nq - 1))

        strict, u_strict, _, lane = _sb_consts()
        qf = q_ref[...] * SB_SCALE
        hms = [(lane // 64) == hh for hh in range(2)]
        qhs = [jnp.where(hm, qf, 0.0).astype(BF16) for hm in hms]

        def step(j, r0, r1, a, sv, diag, keep=None):
            rows = pl.ds(pl.multiple_of(j * SB_BLK, SB_BLK), SB_BLK)
            kj = k_ref[rows, :].astype(BF16)
            vf = v_ref[rows, :]
            rs = []
            for hh, r in enumerate((r0, r1)):
                l1m, _, w = _sb_tile(qhs[hh], kj, diag, strict, u_strict, r)
                pv = _dot(w.astype(BF16), jnp.where(hms[hh], vf, 0.0).astype(BF16))
                mass = jnp.sum(l1m, axis=1, keepdims=True)
                if keep is not None:
                    pv, mass = jnp.where(keep, pv, 0.0), jnp.where(keep, mass, 0.0)
                a = a + pv
                rs.append(r + mass)
                sv = jnp.where(lane == hh * 64 + (i - j + 1), rs[hh], sv)
            return rs[0], rs[1], a, sv

        zero = jnp.zeros((SB_BLK, 1), F32)
        acc0 = jnp.zeros((SB_BLK, LANES), F32)
        r0, r1, acc, sv = step(i, zero, zero, acc0, acc0, True)
        r0, r1, acc, sv = step(jnp.maximum(i - 1, 0), r0, r1, acc, sv, False, keep=i > 0)
        _, _, _, acc, sv = lax.while_loop(
            lambda c: (c[0] >= 0) & (jnp.maximum(jnp.max(c[1]), jnp.max(c[2])) > SB_CUT),
            lambda c: (c[0] - 1,) + step(c[0], c[1], c[2], c[3], c[4], False),
            (i - 2, r0, r1, acc, sv))
        o_ref[...] = acc.astype(BF16)
        sv_ref[...] = sv

    return pl.pallas_call(
        body, name="sb_fwd", grid=(4, nq),
        in_specs=[pl.BlockSpec((SB_BLK, LANES), lambda p, i: (i, p)),
                  pl.BlockSpec((t, LANES), lambda p, i: (0, 4 + p)),
                  pl.BlockSpec((t, LANES), lambda p, i: (0, 8 + p))] + [HBM_SPEC] * n_in,
        out_specs=[pl.BlockSpec((SB_BLK, LANES), lambda p, i: (i, p))] * 2 + [HBM_SPEC] * n_out,
        out_shape=[jax.ShapeDtypeStruct((t, SB_WIDTH), BF16), jax.ShapeDtypeStruct((t, SB_WIDTH), F32)] + rider.out_shapes,
        scratch_shapes=[pltpu.SemaphoreType.DMA((s,)) for s in rider.sems],
        compiler_params=_cparams("arbitrary", "arbitrary"),
    )(proj, proj, proj, *rider.inputs)


def _sb_bwd(proj, dcat, mass, t, rider):
    nq = t // SB_BLK
    n_in, n_out = len(rider.inputs), len(rider.out_shapes)

    def body(q_ref, k_ref, v_ref, do_ref, sv_ref, *rest):
        rin = rest[:n_in]
        dq_ref, dk_ref, dv_ref = rest[n_in:n_in + 3]
        rout = rest[n_in + 3:n_in + 3 + n_out]
        p = pl.program_id(0)
        i = pl.program_id(1)
        _ride(rider, rin, rout, rest[n_in + 3 + n_out:], (p == 0) & (i == 0), (p == 3) & (i == nq - 1))

        @pl.when(i == 0)
        def _():
            dk_ref[...] = jnp.zeros_like(dk_ref)
            dv_ref[...] = jnp.zeros_like(dv_ref)

        strict, u_strict, u_pre, lane = _sb_consts()
        qf = q_ref[...] * SB_SCALE
        dof = do_ref[...]
        hms = [(lane // 64) == hh for hh in range(2)]
        qhs = [jnp.where(hm, qf, 0.0).astype(BF16) for hm in hms]
        dohs = [jnp.where(hm, dof, 0.0).astype(BF16) for hm in hms]

        sv = sv_ref[...]
        zero = jnp.zeros((SB_BLK, 1), F32)

        def mass_right(hh, d):
            return jnp.sum(jnp.where(lane == hh * 64 + d, sv, 0.0), axis=1, keepdims=True)

        dstop = lax.while_loop(
            lambda d: (i - d >= 0) & (jnp.maximum(jnp.max(mass_right(0, d)), jnp.max(mass_right(1, d))) > SB_CUT),
            lambda d: d + 1, 1)
        jstop = i - dstop

        def step(j, carry, diag, keep=None):
            pre_g0, pre_g1, dqa = carry
            rows = pl.ds(pl.multiple_of(j * SB_BLK, SB_BLK), SB_BLK)
            kf = k_ref[rows, :]
            kj = kf.astype(BF16)
            vj = v_ref[rows, :].astype(BF16)
            dv = jnp.zeros((SB_BLK, LANES), F32)
            dk = jnp.zeros((SB_BLK, LANES), F32)
            dqj = jnp.zeros((SB_BLK, LANES), F32)
            pre = []
            for hh, pre_g in enumerate((pre_g0, pre_g1)):
                _, lsz, w = _sb_tile(qhs[hh], kj, diag, strict, u_strict, zero if diag else mass_right(hh, i - j))
                g = w * _dot_nt(dohs[hh], vj)
                gpre = _dot(g.astype(BF16), u_pre) + pre_g
                sig = jnp.exp(lsz)
                dz = g * (1.0 - sig) - gpre * sig
                if diag:
                    dz = jnp.where(strict, dz, 0.0)
                dzb = dz.astype(BF16)
                dv = dv + _dot_tn(w.astype(BF16), dohs[hh])
                dk = dk + _dot_tn(dzb, qhs[hh])
                dqj = dqj + _dot(dzb, jnp.where(hms[hh], kf, 0.0).astype(BF16))
                gsum = jnp.sum(g, axis=1, keepdims=True)
                pre.append(pre_g + (gsum if keep is None else jnp.where(keep, gsum, 0.0)))
            if keep is not None:
                dv, dk, dqj = jnp.where(keep, dv, 0.0), jnp.where(keep, dk, 0.0), jnp.where(keep, dqj, 0.0)
            dv_ref[rows, :] += dv
            dk_ref[rows, :] += dk
            return pre[0], pre[1], dqa + dqj

        carry = lax.fori_loop(jstop + 1, i - 1, lambda j, c: step(j, c, False), (zero, zero, jnp.zeros((SB_BLK, LANES), F32)))
        carry = step(jnp.maximum(i - 1, 0), carry, False, keep=i > 0)
        _, _, dq = step(i, carry, True)
        dq_ref[...] = (dq * SB_SCALE).astype(BF16)

    return pl.pallas_call(
        body, name="sb_bwd", grid=(4, nq),
        in_specs=[pl.BlockSpec((SB_BLK, LANES), lambda p, i: (i, p)),
                  pl.BlockSpec((t, LANES), lambda p, i: (0, 4 + p)),
                  pl.BlockSpec((t, LANES), lambda p, i: (0, 8 + p)),
                  pl.BlockSpec((SB_BLK, LANES), lambda p, i: (i, p)),
                  pl.BlockSpec((SB_BLK, LANES), lambda p, i: (i, p))] + [HBM_SPEC] * n_in,
        out_specs=[pl.BlockSpec((SB_BLK, LANES), lambda p, i: (i, p)),
                   pl.BlockSpec((t, LANES), lambda p, i: (0, p)),
                   pl.BlockSpec((t, LANES), lambda p, i: (0, p))] + [HBM_SPEC] * n_out,
        out_shape=[jax.ShapeDtypeStruct((t, SB_WIDTH), BF16)] + [jax.ShapeDtypeStruct((t, SB_WIDTH), F32)] * 2
        + rider.out_shapes,
        scratch_shapes=[pltpu.SemaphoreType.DMA((s,)) for s in rider.sems],
        compiler_params=_cparams("arbitrary", "arbitrary"),
    )(proj, proj, proj, dcat, mass, *rider.inputs)


def _gla_consts():
    r = lax.broadcasted_iota(jnp.int32, (GLA_ROWS, GLA_ROWS), 0)
    c = lax.broadcasted_iota(jnp.int32, (GLA_ROWS, GLA_ROWS), 1)
    same = (r // CHUNK) == (c // CHUNK)
    causal = same & (c <= r)
    upto_mid = c % CHUNK <= CHUNK // 2 - 1
    fwd_stack = jnp.concatenate([causal, same & upto_mid, same], axis=0).astype(BF16)
    bwd_stack = jnp.concatenate([same & (c >= r), same & (r % CHUNK <= CHUNK // 2 - 1), same], axis=1).astype(BF16)
    rowid = lax.broadcasted_iota(jnp.int32, (GLA_ROWS, 1), 0)
    lane = lax.broadcasted_iota(jnp.int32, (1, LANES), 1)
    sr = lax.broadcasted_iota(jnp.int32, (LANES, 2 * LANES), 0)
    sc = lax.broadcasted_iota(jnp.int32, (LANES, 2 * LANES), 1)
    blockdiag = (sr // 64) == (sc // LANES)
    return causal, fwd_stack, bwd_stack, rowid, lane, blockdiag


def _dot3(u, x):
    hi, mid, lo = _split3(x)
    return _dot(u, hi) + _dot(u, mid) + _dot(u, lo)


def _row_to_col(row):
    return jnp.transpose(jnp.broadcast_to(row, (LANES, LANES)))


def _gla_gates(ga_ref, gup_ref, gbias_ref):
    pre = _dot(ga_ref[...].astype(BF16), gup_ref[...].astype(BF16)) + gbias_ref[...]
    log_a = (jnp.minimum(pre, 0.0) - jnp.log(1.0 + jnp.exp(-jnp.abs(pre)))) / GLA_TAU
    return pre, log_a


def _gla_terms(g2, q2, k2, fwd_stack):
    bs = _dot3(fwd_stack, g2)
    b, b_ref, b_last = bs[:GLA_ROWS], bs[GLA_ROWS:2 * GLA_ROWS], bs[2 * GLA_ROWS:]
    qs = q2 * GLA_SCALE
    e_q = jnp.exp(b - b_ref)
    e_k = jnp.exp(b_ref - b)
    e_d = jnp.exp(b_last - b)
    e_b = jnp.exp(b)
    decay = jnp.exp(b_last)
    return dict(qs=qs, e_q=e_q, e_k=e_k, e_d=e_d, e_b=e_b, decay=decay,
                qi=qs * e_q, ki=k2 * e_k, kd=k2 * e_d, qb=qs * e_b)


def _gla_fwd(proj, gate_up_pad, gate_bias, gnorm, t, rider):
    nsteps = t // GLA_ROWS
    cps = GLA_ROWS // CHUNK
    n_in, n_out = len(rider.inputs), len(rider.out_shapes)

    def body(q_ref, k_ref, v_ref, gg_ref, ga_ref, gup_ref, gbias_ref, gn_ref, *rest):
        o_ref, oraw_ref, st_ref = rest[n_in:n_in + 3]
        s_scr = rest[n_in + 3 + n_out]
        i = pl.program_id(0)
        _ride(rider, rest[:n_in], rest[n_in + 3:n_in + 3 + n_out], rest[n_in + 4 + n_out:], i == 0, i == nsteps - 1)

        @pl.when(i == 0)
        def _():
            s_scr[...] = jnp.zeros_like(s_scr)

        _, fwd_stack, _, _, lane, blockdiag = _gla_consts()
        causal64 = (lax.broadcasted_iota(jnp.int32, (CHUNK, CHUNK), 1) <= lax.broadcasted_iota(jnp.int32, (CHUNK, CHUNK), 0))
        _, log_a = _gla_gates(ga_ref, gup_ref, gbias_ref)
        gn = gn_ref[...]
        pre = []
        for p in range(2):
            kl = slice(p * LANES, (p + 1) * LANES)
            tm = _gla_terms(log_a[:, kl], q_ref[:, kl], k_ref[:, kl], fwd_stack)
            pre.append(dict(
                v2b=v_ref[:, p * 2 * LANES:(p + 1) * 2 * LANES].astype(BF16), kib=tm["ki"].astype(BF16),
                kdb=tm["kd"].astype(BF16), qbb=tm["qb"].astype(BF16), decay=tm["decay"],
                qihb=[jnp.where((lane // 64) == hh, tm["qi"], 0.0).astype(BF16) for hh in range(2)]))
        state = [s_scr[0], s_scr[1]]
        outs = [[], []]
        for cc in range(cps):
            rows = slice(cc * CHUNK, (cc + 1) * CHUNK)
            for p in range(2):
                w = pre[p]
                st_ref[cc, p] = state[p]
                intra = []
                for hh in range(2):
                    a = jnp.where(causal64, _dot_nt(w["qihb"][hh][rows], w["kib"][rows]), 0.0)
                    intra.append(_dot(a.astype(BF16), w["v2b"][rows, hh * LANES:(hh + 1) * LANES]))
                outs[p].append(jnp.concatenate(intra, axis=1) + _dot(w["qbb"][rows], state[p].astype(BF16)))
                upd = jnp.where(blockdiag, _dot_tn(w["kdb"][rows], w["v2b"][rows]), 0.0)
                dcol = _row_to_col(w["decay"][cc * CHUNK:cc * CHUNK + 1])
                state[p] = state[p] * jnp.concatenate([dcol, dcol], axis=1) + upd
        for p in range(2):
            s_scr[p] = state[p]
            vl = slice(p * 2 * LANES, (p + 1) * 2 * LANES)
            o2 = jnp.concatenate(outs[p], axis=0)
            oraw_ref[:, vl] = o2
            for hh in range(2):
                oh = o2[:, hh * LANES:(hh + 1) * LANES]
                gl = slice(p * 2 * LANES + hh * LANES, p * 2 * LANES + (hh + 1) * LANES)
                rinv = lax.rsqrt(jnp.mean(oh * oh, axis=1, keepdims=True) + RMS_EPS)
                gg = gg_ref[:, gl]
                o_ref[:, gl] = (oh * rinv * gn * (gg * _sigmoid(gg))).astype(BF16)

    cb = lambda w, idx: pl.BlockSpec((GLA_ROWS, w), lambda i: (i, idx))
    full = lambda shp: pl.BlockSpec(shp, lambda i: tuple(0 for _ in shp))
    return pl.pallas_call(
        body, name="gla_fwd", grid=(nsteps,),
        in_specs=[cb(256, 6), cb(256, 7), cb(512, 4), cb(512, 5), cb(128, 24),
                  full((LANES, GLA_KEYS)), full((1, GLA_KEYS)), full((1, LANES))] + [HBM_SPEC] * n_in,
        out_specs=[pl.BlockSpec((GLA_ROWS, GLA_WIDTH), lambda i: (i, 0)),
                   pl.BlockSpec((GLA_ROWS, GLA_WIDTH), lambda i: (i, 0)),
                   pl.BlockSpec((cps, 2, LANES, 2 * LANES), lambda i: (i, 0, 0, 0))] + [HBM_SPEC] * n_out,
        out_shape=[jax.ShapeDtypeStruct((t, GLA_WIDTH), BF16), jax.ShapeDtypeStruct((t, GLA_WIDTH), F32),
                   jax.ShapeDtypeStruct((t // CHUNK, 2, LANES, 2 * LANES), F32)]
        + rider.out_shapes,
        input_output_aliases={8 + i: 3 + o for i, o in rider.aliases},
        scratch_shapes=[pltpu.VMEM((2, LANES, 2 * LANES), F32)] + [pltpu.SemaphoreType.DMA((s,)) for s in rider.sems],
        compiler_params=_cparams("arbitrary"),
    )(proj, proj, proj, proj, proj, gate_up_pad, gate_bias, gnorm, *rider.inputs)


def _gla_bwd(proj, dcat, oraw, states, gate_up_pad, gate_bias, gnorm, t, rider):
    nsteps = t // GLA_ROWS
    cps = GLA_ROWS // CHUNK
    wout = 2 * GLA_KEYS + 2 * GLA_WIDTH + LANES
    n_in, n_out = len(rider.inputs), len(rider.out_shapes)

    def body(q_ref, k_ref, v_ref, gg_ref, ga_ref, do_ref, oraw_ref, st_ref, gup_ref, gbias_ref, gn_ref, *rest):
        d_ref, dgup_ref, dgbias_ref, dgn_ref = rest[n_in:n_in + 4]
        ds_scr = rest[n_in + 4 + n_out]
        i = pl.program_id(0)
        _ride(rider, rest[:n_in], rest[n_in + 4:n_in + 4 + n_out], rest[n_in + 5 + n_out:], i == 0, i == nsteps - 1)

        @pl.when(i == 0)
        def _():
            ds_scr[...] = jnp.zeros_like(ds_scr)
            dgup_ref[...] = jnp.zeros_like(dgup_ref)
            dgbias_ref[...] = jnp.zeros_like(dgbias_ref)
            dgn_ref[...] = jnp.zeros_like(dgn_ref)

        causal, fwd_stack, bwd_stack, rowid, lane, blockdiag = _gla_consts()
        pre, log_a = _gla_gates(ga_ref, gup_ref, gbias_ref)
        gn = gn_ref[...]
        dgn = jnp.zeros((1, LANES), F32)
        dgs = []
        pre_p = []
        for p in range(2):
            kl = slice(p * LANES, (p + 1) * LANES)
            vl = slice(p * 2 * LANES, (p + 1) * 2 * LANES)
            tm = _gla_terms(log_a[:, kl], q_ref[:, kl], k_ref[:, kl], fwd_stack)
            v2b = v_ref[:, vl].astype(BF16)
            dos = []
            for hh in range(2):
                gl = slice(p * 2 * LANES + hh * LANES, p * 2 * LANES + (hh + 1) * LANES)
                oh = oraw_ref[:, gl]
                rinv = lax.rsqrt(jnp.mean(oh * oh, axis=1, keepdims=True) + RMS_EPS)
                on = oh * rinv
                gg = gg_ref[:, gl]
                sg = _sigmoid(gg)
                sil = gg * sg
                dgo = do_ref[:, gl]
                d_ref[:, 2 * GLA_KEYS + GLA_WIDTH + gl.start:2 * GLA_KEYS + GLA_WIDTH + gl.stop] = (
                    dgo * on * gn * (sg * (1.0 + gg * (1.0 - sg)))).astype(BF16)
                dgn = dgn + jnp.sum(dgo * sil * on, axis=0, keepdims=True)
                don = dgo * sil * gn
                dos.append(rinv * (don - on * jnp.mean(don * on, axis=1, keepdims=True)))
            do2b = jnp.concatenate(dos, axis=1).astype(BF16)
            qib = tm["qi"].astype(BF16)
            kib = tm["ki"].astype(BF16)
            kdb = tm["kd"].astype(BF16)
            qbb = tm["qb"].astype(BF16)
            dqi = jnp.zeros((GLA_ROWS, LANES), F32)
            dki = jnp.zeros((GLA_ROWS, LANES), F32)
            dvs = []
            for hh in range(2):
                hm = (lane // 64) == hh
                hl = slice(hh * LANES, (hh + 1) * LANES)
                a = jnp.where(causal, _dot_nt(jnp.where(hm, tm["qi"], 0.0).astype(BF16), kib), 0.0).astype(BF16)
                da = jnp.where(causal, _dot_nt(do2b[:, hl], v2b[:, hl]), 0.0).astype(BF16)
                dvs.append(_dot_tn(a, do2b[:, hl]))
                dqi = dqi + jnp.where(hm, _dot(da, kib), 0.0)
                dki = dki + jnp.where(hm, _dot_tn(da, qib), 0.0)
            pre_p.append(dict(tm=tm, v2b=v2b, do2b=do2b, kdb=kdb, qbb=qbb, dqi=dqi, dki=dki, dvs=dvs))
        dstate = [ds_scr[0], ds_scr[1]]
        rec = [dict(dv_st=[None] * cps, dqb=[None] * cps, dkd=[None] * cps, dd=[None] * cps) for _ in range(2)]
        for cc in reversed(range(cps)):
            rows = slice(cc * CHUNK, (cc + 1) * CHUNK)
            for p in range(2):
                w, ds2 = pre_p[p], dstate[p]
                s_prev = st_ref[cc, p]
                ds2b = ds2.astype(BF16)
                rec[p]["dv_st"][cc] = _dot(w["kdb"][rows], ds2b)
                rec[p]["dqb"][cc] = _dot_nt(w["do2b"][rows], s_prev.astype(BF16))
                rec[p]["dkd"][cc] = _dot_nt(w["v2b"][rows], ds2b)
                decay_row = w["tm"]["decay"][cc * CHUNK:cc * CHUNK + 1]
                ddecay_col = jnp.sum(ds2 * s_prev, axis=1, keepdims=True)
                ddecay_row = jnp.transpose(jnp.broadcast_to(ddecay_col, (LANES, LANES)))[0:1, :]
                rec[p]["dd"][cc] = jnp.broadcast_to(ddecay_row * decay_row, (CHUNK, LANES))
                dcol = _row_to_col(decay_row)
                dstate[p] = (jnp.where(blockdiag, _dot_tn(w["qbb"][rows], w["do2b"][rows]), 0.0)
                             + ds2 * jnp.concatenate([dcol, dcol], axis=1))
        for p in range(2):
            ds_scr[p] = dstate[p]
            tm, dqi, dki = pre_p[p]["tm"], pre_p[p]["dqi"], pre_p[p]["dki"]
            dv2 = jnp.concatenate(pre_p[p]["dvs"], axis=1) + jnp.concatenate(rec[p]["dv_st"], axis=0)
            dqb = jnp.concatenate(rec[p]["dqb"], axis=0)
            dkd = jnp.concatenate(rec[p]["dkd"], axis=0)
            dd = rec[p]["dd"]
            dqs = dqi * tm["e_q"] + dqb * tm["e_b"]
            dk = dki * tm["e_k"] + dkd * tm["e_d"]
            t_qi = dqi * tm["qi"]
            t_ki = dki * tm["ki"]
            t_kd = dkd * tm["kd"]
            db = t_qi - t_ki + dqb * tm["qb"] - t_kd
            to_mid = t_ki - t_qi
            to_last = t_kd + jnp.where(rowid % CHUNK == CHUNK - 1, jnp.concatenate(dd, axis=0), 0.0)
            dgs.append(_dot3(bwd_stack, jnp.concatenate([db, to_mid, to_last], axis=0)))
            d_ref[:, p * LANES:(p + 1) * LANES] = (dqs * GLA_SCALE).astype(BF16)
            d_ref[:, GLA_KEYS + p * LANES:GLA_KEYS + (p + 1) * LANES] = dk.astype(BF16)
            d_ref[:, 2 * GLA_KEYS + p * 2 * LANES:2 * GLA_KEYS + (p + 1) * 2 * LANES] = dv2.astype(BF16)
        dlog_a = jnp.concatenate(dgs, axis=1)
        dpre = dlog_a * (1.0 / GLA_TAU) * _sigmoid(-pre)
        dpb = dpre.astype(BF16)
        dgn_ref[...] += dgn
        dgbias_ref[...] += jnp.sum(dpre, axis=0, keepdims=True)
        dgup_ref[...] += _dot_tn(ga_ref[...].astype(BF16), dpb)
        d_ref[:, 2 * GLA_KEYS + 2 * GLA_WIDTH:] = _dot_nt(dpb, gup_ref[...].astype(BF16)).astype(BF16)

    rev = lambda i: nsteps - 1 - i
    cb = lambda w, idx: pl.BlockSpec((GLA_ROWS, w), lambda i: (rev(i), idx))
    full = lambda shp: pl.BlockSpec(shp, lambda i: tuple(0 for _ in shp))
    return pl.pallas_call(
        body, name="gla_bwd", grid=(nsteps,),
        in_specs=[cb(256, 6), cb(256, 7), cb(512, 4), cb(512, 5), cb(128, 24), cb(512, 1), cb(512, 0),
                  pl.BlockSpec((cps, 2, LANES, 2 * LANES), lambda i: (rev(i), 0, 0, 0)),
                  full((LANES, GLA_KEYS)), full((1, GLA_KEYS)), full((1, LANES))] + [HBM_SPEC] * n_in,
        out_specs=[pl.BlockSpec((GLA_ROWS, wout), lambda i: (rev(i), 0)),
                   full((LANES, GLA_KEYS)), full((1, GLA_KEYS)), full((1, LANES))] + [HBM_SPEC] * n_out,
        out_shape=[jax.ShapeDtypeStruct((t, wout), BF16), jax.ShapeDtypeStruct((LANES, GLA_KEYS), F32),
                   jax.ShapeDtypeStruct((1, GLA_KEYS), F32), jax.ShapeDtypeStruct((1, LANES), F32)] + rider.out_shapes,
        scratch_shapes=[pltpu.VMEM((2, LANES, 2 * LANES), F32)] + [pltpu.SemaphoreType.DMA((s,)) for s in rider.sems],
        compiler_params=_cparams("arbitrary"),
    )(proj, proj, proj, proj, proj, dcat, oraw, states, gate_up_pad, gate_bias, gnorm, *rider.inputs)


def _ln_stats(r):
    mu = jnp.mean(r, axis=1, keepdims=True)
    xc = r - mu
    rstd = lax.rsqrt(jnp.mean(xc * xc, axis=1, keepdims=True) + LN_EPS)
    return xc * rstd, rstd


def _ln_bwd(dy_g, xhat, rstd):
    return rstd * (dy_g - jnp.mean(dy_g, axis=1, keepdims=True) - xhat * jnp.mean(dy_g * xhat, axis=1, keepdims=True))


def _outproj_ln1(sb_o, gla_o, x, w_out, g1, b1, t, tm=512):
    def body(sb_ref, gl_ref, x_ref, w_ref, g_ref, b_ref, xhat_ref, rstd_ref, h_ref):
        mix = _dot(sb_ref[...], w_ref[0:SB_WIDTH, :]) + _dot(gl_ref[...], w_ref[SB_WIDTH:, :])
        xhat, rstd = _ln_stats(ALPHA * x_ref[...] + mix)
        xhat_ref[...] = xhat
        rstd_ref[...] = rstd
        h_ref[...] = (xhat * g_ref[...] + b_ref[...]).astype(BF16)

    row = lambda w: pl.BlockSpec((tm, w), lambda i: (i, 0))
    full = lambda shp: pl.BlockSpec(shp, lambda i: (0, 0))
    return pl.pallas_call(
        body, name="outproj_ln1", grid=(t // tm,),
        in_specs=[row(SB_WIDTH), row(GLA_WIDTH), row(D_MODEL), full((D_MODEL, D_MODEL)), full((1, D_MODEL)), full((1, D_MODEL))],
        out_specs=[row(D_MODEL), row(1), row(D_MODEL)],
        out_shape=[jax.ShapeDtypeStruct((t, D_MODEL), F32), jax.ShapeDtypeStruct((t, 1), F32),
                   jax.ShapeDtypeStruct((t, D_MODEL), BF16)],
        compiler_params=_cparams("parallel"),
    )(sb_o, gla_o, x, w_out, g1, b1)


_INV_SQRT2 = 1.0 / math.sqrt(2.0)
_INV_SQRT2PI = 1.0 / math.sqrt(2.0 * math.pi)


def _conv3(xs, w_ref, b_ref, half):
    return (w_ref[half, 0:1, :] * pltpu.roll(xs, 2, 0) + w_ref[half, 1:2, :] * pltpu.roll(xs, 1, 0)
            + w_ref[half, 2:3, :] * xs + b_ref[half])


HALO = 16


def _conv_gelu_fwd(up3, conv_w3, conv_b3, t, tr=512, ct=256):
    nct = D_FF // ct
    hb = tr // HALO

    def body(cur_ref, prev_ref, w_ref, b_ref, gm_ref):
        i = pl.program_id(1)
        keep = (i > 0).astype(F32)
        us = []
        for half in range(2):
            xs = jnp.concatenate([prev_ref[half].astype(F32) * keep, cur_ref[half].astype(F32)], axis=0)
            us.append(_conv3(xs, w_ref, b_ref, half)[HALO:, :])
        a, c = us
        gelu = 0.5 * a * (1.0 + lax.erf(a * _INV_SQRT2))
        gm_ref[...] = (gelu * c).astype(BF16)

    return pl.pallas_call(
        body, name="conv_gelu_fwd", grid=(nct, t // tr),
        in_specs=[pl.BlockSpec((2, tr, ct), lambda j, i: (0, i, j)),
                  pl.BlockSpec((2, HALO, ct), lambda j, i: (0, jnp.maximum(i * hb - 1, 0), j)),
                  pl.BlockSpec((2, 3, ct), lambda j, i: (0, 0, j)),
                  pl.BlockSpec((2, 1, ct), lambda j, i: (0, 0, j))],
        out_specs=pl.BlockSpec((tr, ct), lambda j, i: (i, j)),
        out_shape=jax.ShapeDtypeStruct((t, D_FF), BF16),
        compiler_params=_cparams("parallel", "parallel"),
    )(up3, up3, conv_w3, conv_b3)


def _conv_gelu_bwd(up3, dgm, conv_w3, conv_b3, t, tr=512, ct=256):
    nct = D_FF // ct
    nrt = t // tr
    hb = tr // HALO
    n = tr + 2 * HALO
    lo, hi = HALO, tr + HALO

    def body(cur_ref, prev_ref, next_ref, dg_ref, dgn_ref, w_ref, b_ref, dup_ref, dcw_ref, dcb_ref):
        i = pl.program_id(1)

        @pl.when(i == 0)
        def _():
            dcw_ref[...] = jnp.zeros_like(dcw_ref)
            dcb_ref[...] = jnp.zeros_like(dcb_ref)

        keep_prev = (i > 0).astype(F32)
        keep_next = (i < nrt - 1).astype(F32)
        xs, xm1, xm2, us = [], [], [], []
        for half in range(2):
            x = jnp.concatenate([prev_ref[half].astype(F32) * keep_prev, cur_ref[half].astype(F32),
                                 next_ref[half].astype(F32)], axis=0)
            xs.append(x)
            xm1.append(pltpu.roll(x, 1, 0))
            xm2.append(pltpu.roll(x, 2, 0))
            us.append(w_ref[half, 0:1, :] * xm2[half] + w_ref[half, 1:2, :] * xm1[half]
                      + w_ref[half, 2:3, :] * x + b_ref[half])
        a, c = us
        dg = jnp.concatenate([jnp.zeros((HALO, ct), F32), dg_ref[...].astype(F32),
                              dgn_ref[...].astype(F32) * keep_next], axis=0)
        cdf = 0.5 * (1.0 + lax.erf(a * _INV_SQRT2))
        pdf = jnp.exp(-0.5 * a * a) * _INV_SQRT2PI
        dus = [dg * c * (cdf + a * pdf), dg * (a * cdf)]
        rid = lax.broadcasted_iota(jnp.int32, (8, 1), 0)
        for half in range(2):
            du = dus[half]
            dup = (w_ref[half, 2:3, :] * du + w_ref[half, 1:2, :] * pltpu.roll(du, n - 1, 0)
                   + w_ref[half, 0:1, :] * pltpu.roll(du, n - 2, 0))
            dup_ref[half] = dup[lo:hi, :].astype(BF16)
            duc = du[lo:hi, :]
            s0 = jnp.sum(duc * xm2[half][lo:hi, :], axis=0, keepdims=True)
            s1 = jnp.sum(duc * xm1[half][lo:hi, :], axis=0, keepdims=True)
            s2 = jnp.sum(duc * xs[half][lo:hi, :], axis=0, keepdims=True)
            dcw_ref[half] += jnp.where(rid == 0, s0, jnp.where(rid == 1, s1, jnp.where(rid == 2, s2, 0.0)))
            dcb_ref[half] += jnp.sum(duc, axis=0, keepdims=True)

    last = t // HALO - 1
    return pl.pallas_call(
        body, name="conv_gelu_bwd", grid=(nct, nrt),
        in_specs=[pl.BlockSpec((2, tr, ct), lambda j, i: (0, i, j)),
                  pl.BlockSpec((2, HALO, ct), lambda j, i: (0, jnp.maximum(i * hb - 1, 0), j)),
                  pl.BlockSpec((2, HALO, ct), lambda j, i: (0, jnp.minimum((i + 1) * hb, last), j)),
                  pl.BlockSpec((tr, ct), lambda j, i: (i, j)),
                  pl.BlockSpec((HALO, ct), lambda j, i: (jnp.minimum((i + 1) * hb, last), j)),
                  pl.BlockSpec((2, 3, ct), lambda j, i: (0, 0, j)),
                  pl.BlockSpec((2, 1, ct), lambda j, i: (0, 0, j))],
        out_specs=[pl.BlockSpec((2, tr, ct), lambda j, i: (0, i, j)),
                   pl.BlockSpec((2, 8, ct), lambda j, i: (0, 0, j)),
                   pl.BlockSpec((2, 1, ct), lambda j, i: (0, 0, j))],
        out_shape=[jax.ShapeDtypeStruct((2, t, D_FF), BF16), jax.ShapeDtypeStruct((2, 8, D_FF), F32),
                   jax.ShapeDtypeStruct((2, 1, D_FF), F32)],
        compiler_params=_cparams("parallel", "arbitrary"),
    )(up3, up3, up3, dgm, dgm, conv_w3, conv_b3)


def _down_ln2_loss(gm, w_down, xhat1, g1, b1, g2, b2, target, t, tm=512):
    def body(gm_ref, w_ref, xh_ref, g1_ref, b1_ref, g2_ref, b2_ref, tg_ref, dr_ref, drb_ref, loss_ref, dg_ref, db_ref):
        i = pl.program_id(0)

        @pl.when(i == 0)
        def _():
            loss_ref[...] = jnp.zeros_like(loss_ref)
            dg_ref[...] = jnp.zeros_like(dg_ref)
            db_ref[...] = jnp.zeros_like(db_ref)

        h = xh_ref[...] * g1_ref[...] + b1_ref[...]
        xhat, rstd = _ln_stats(ALPHA * h + _dot(gm_ref[...], w_ref[...]))
        err = xhat * g2_ref[...] + b2_ref[...] - tg_ref[...]
        loss_ref[...] += 0.5 * jnp.sum(jnp.sum(err * err, axis=1, keepdims=True), axis=0, keepdims=True) / D_MODEL
        dy = err * (1.0 / D_MODEL)
        dg_ref[...] += jnp.sum(dy * xhat, axis=0, keepdims=True)
        db_ref[...] += jnp.sum(dy, axis=0, keepdims=True)
        dr = _ln_bwd(dy * g2_ref[...], xhat, rstd)
        dr_ref[...] = dr
        drb_ref[...] = dr.astype(BF16)

    row = lambda w: pl.BlockSpec((tm, w), lambda i: (i, 0))
    full = lambda shp: pl.BlockSpec(shp, lambda i: (0, 0))
    vec = full((1, D_MODEL))
    return pl.pallas_call(
        body, name="down_ln2_loss", grid=(t // tm,),
        in_specs=[row(D_FF), full((D_FF, D_MODEL)), row(D_MODEL), vec, vec, vec, vec, row(D_MODEL)],
        out_specs=[row(D_MODEL), row(D_MODEL), full((1, 1)), vec, vec],
        out_shape=[jax.ShapeDtypeStruct((t, D_MODEL), F32), jax.ShapeDtypeStruct((t, D_MODEL), BF16),
                   jax.ShapeDtypeStruct((1, 1), F32),
                   jax.ShapeDtypeStruct((1, D_MODEL), F32), jax.ShapeDtypeStruct((1, D_MODEL), F32)],
        compiler_params=_cparams("arbitrary"),
    )(gm, w_down, xhat1, g1, b1, g2, b2, target)


def _dh_ln1_bwd(dup3, w_up4, dr2, xhat1, rstd1, g1, t, tm=256):
    ws = 2 * D_FF // 4

    def body(a_ref, w_ref, dr2_ref, xh_ref, rs_ref, g_ref, dr1_ref, dg_ref, db_ref):
        i = pl.program_id(0)

        @pl.when(i == 0)
        def _():
            dg_ref[...] = jnp.zeros_like(dg_ref)
            db_ref[...] = jnp.zeros_like(db_ref)

        dh = ALPHA * dr2_ref[...]
        for s in range(4):
            dh = dh + _dot_nt(a_ref[s // 2, :, (s % 2) * ws:(s % 2 + 1) * ws], w_ref[s])
        xhat = xh_ref[...]
        dg_ref[...] += jnp.sum(dh * xhat, axis=0, keepdims=True)
        db_ref[...] += jnp.sum(dh, axis=0, keepdims=True)
        dr1_ref[...] = _ln_bwd(dh * g_ref[...], xhat, rs_ref[...])

    row = lambda w: pl.BlockSpec((tm, w), lambda i: (i, 0))
    vec = pl.BlockSpec((1, D_MODEL), lambda i: (0, 0))
    return pl.pallas_call(
        body, name="dh_ln1_bwd", grid=(t // tm,),
        in_specs=[pl.BlockSpec((2, tm, D_FF), lambda i: (0, i, 0)),
                  pl.BlockSpec((4, D_MODEL, ws), lambda i: (0, 0, 0)),
                  row(D_MODEL), row(D_MODEL), row(1), vec],
        out_specs=[row(D_MODEL), vec, vec],
        out_shape=[jax.ShapeDtypeStruct((t, D_MODEL), F32), jax.ShapeDtypeStruct((1, D_MODEL), F32),
                   jax.ShapeDtypeStruct((1, D_MODEL), F32)],
        compiler_params=_cparams("arbitrary"),
    )(dup3, w_up4, dr2, xhat1, rstd1, g1)


def _adamw(w, g, m, v, name):
    rows, cols = w.shape
    tr = rows
    for cand in (256, 128, 64, 32, 16, 8):
        if rows % cand == 0 and rows > cand:
            tr = cand
            break
    c1 = 1.0 / (1.0 - ADAM_B1 ** ADAM_STEP)
    c2 = 1.0 / (1.0 - ADAM_B2 ** ADAM_STEP)

    def body(w_ref, g_ref, m_ref, v_ref, d_ref, nm_ref, nv_ref):
        gv = g_ref[...]
        nm = ADAM_B1 * m_ref[...] + (1.0 - ADAM_B1) * gv
        nv = ADAM_B2 * v_ref[...] + (1.0 - ADAM_B2) * (gv * gv)
        d_ref[...] = -ADAM_LR * ((nm * c1) / (jnp.sqrt(nv * c2) + ADAM_EPS) + ADAM_WD * w_ref[...])
        nm_ref[...] = nm
        nv_ref[...] = nv

    spec = pl.BlockSpec((tr, cols), lambda i: (i, 0))
    out = jax.ShapeDtypeStruct((rows, cols), F32)
    return pl.pallas_call(
        body, name=name, grid=(rows // tr,), in_specs=[spec] * 4, out_specs=[spec] * 3, out_shape=[out] * 3,
        compiler_params=_cparams("parallel"),
    )(w, g, m, v)


def _local_step(x, target, w_in_p, late_shards, gate_up_pad, gate_bias, gnorm, ln1_g, ln1_b, conv_w3, conv_b3,
                ln2_g, ln2_b, c_arr, kc_arr):
    t = x.shape[0]
    tq = min(t, 1024)
    s_up, s_out, s_down = late_shards
    sh_up, sh_out, sh_down = LATE_SHAPES
    proj, out_partly = _mm(x, w_in_p, m=t, n=IN_PAD, k=D_MODEL, tm=512, tn=IN_PAD, tk=D_MODEL, name="proj",
                           rider=_gather_rider([s_out], [sh_out]))
    sb_o, sb_mass, up_partly = _sb_fwd(proj, t, _gather_rider([s_up], [sh_up]))
    gla_o, oraw, states, w_up4, w_out4, down_partly = _gla_fwd(
        proj, gate_up_pad, gate_bias, gnorm, t,
        _join(_forward_rider([up_partly, out_partly], [sh_up, sh_out]), _gather_rider([s_down], [sh_down])))
    w_out = w_out4.reshape(D_MODEL, D_MODEL)
    xhat1, rstd1, h_bf = _outproj_ln1(sb_o, gla_o, x, w_out, ln1_g, ln1_b, t)
    up3, w_down4 = _mm(h_bf, w_up4, m=t, n=2 * D_FF, k=D_MODEL, tm=tq, tn=W_UP_S, tk=D_MODEL, name="up",
                       b_spec=pl.BlockSpec((None, D_MODEL, W_UP_S), lambda i, j, kk: (j, 0, 0)),
                       o_spec=pl.BlockSpec((None, tq, W_UP_S), lambda i, j, kk: (j // 2, i, j % 2)),
                       out_shape=jax.ShapeDtypeStruct((2, t, D_FF), BF16), out_dtype=BF16,
                       rider=_forward_rider([down_partly], [sh_down]))
    w_down = w_down4.reshape(D_FF, D_MODEL)
    gm = _conv_gelu_fwd(up3, conv_w3, conv_b3, t, tr=min(t, 2048))
    dr2, dr2b, loss, dln2_g, dln2_b = _down_ln2_loss(gm, w_down, xhat1, ln1_g, ln1_b, ln2_g, ln2_b, target, t)
    dgm = _mm(dr2b, w_down, m=t, n=D_FF, k=D_MODEL, tm=tq, tn=W_UP_S, tk=D_MODEL, tb=True, out_dtype=BF16, name="dgm")
    dw_down = _mm(gm, dr2b, m=D_FF, n=D_MODEL, k=t, tm=W_UP_S, tn=D_MODEL, tk=min(t, 2048), ta=True, name="dw_down")
    dup3, dcw, dcb = _conv_gelu_bwd(up3, dgm, conv_w3, conv_b3, t, tr=tq)
    dr1, dln1_g, dln1_b = _dh_ln1_bwd(dup3, w_up4, dr2, xhat1, rstd1, ln1_g, t)
    dw_up4 = _mm(h_bf, dup3, m=D_MODEL, n=2 * D_FF, k=t, tm=512, tn=W_UP_S, tk=t, ta=True, name="dw_up",
                 b_spec=pl.BlockSpec((None, t, W_UP_S), lambda i, j, kk: (j // 2, kk, j % 2)),
                 o_spec=pl.BlockSpec((None, 512, W_UP_S), lambda i, j, kk: (j, i, 0)),
                 out_shape=jax.ShapeDtypeStruct((4, D_MODEL, W_UP_S), F32))
    dw_out = _dw_out(sb_o, gla_o, dr1, t, tq)
    gs = [dw_up4, dw_out.reshape(4, W_OUT_S, D_MODEL), dw_down.reshape(4, W_DOWN_S, D_MODEL)]
    dcat = _mm(dr1, w_out, m=t, n=D_MODEL, k=D_MODEL, tm=tq, tn=512, tk=D_MODEL, tb=True, name="dcat")
    dgla, dgup_pad, dgbias, dgnorm, *from_sib = _gla_bwd(proj, dcat, oraw, states, gate_up_pad, gate_bias, gnorm, t,
                                                         _sibling_rider(gs, LATE_SHAPES))
    ps = [_add_sibling(gs[m], from_sib[m], c_arr, LATE_ADD_ROWS[m], "add_sibling_late_%d" % m) for m in range(3)]
    small = dict(
        gate_up=dgup_pad[:GATE_RANK], gate_bias=dgbias, gla_norm_g=dgnorm, ln1_g=dln1_g, ln1_b=dln1_b,
        conv_w=jnp.concatenate([dcw[0, :3], dcw[1, :3]], axis=1), conv_b=jnp.concatenate([dcb[0], dcb[1]], axis=1),
        ln2_g=dln2_g, ln2_b=dln2_b, loss=loss)
    dsq, dsk, dsv, *others, vecs = _sb_bwd(proj, dcat, sb_mass, t,
                                           _chips_rider(ps, _pack_vec(small, SMALL_GRADS, GRAD_VEC_ROWS)))
    late_sums = [_add_chips(ps[m], others[m], kc_arr, LATE_ADD_ROWS[m], "add_chips_late_%d" % m) for m in range(3)]
    dproj = [dsq, dsk, dsv, dgla]
    dw_in_p = _dw_in(x, dproj, t, tk=tq)
    g_in = dw_in_p[None]
    from_sib_in, = _run(_sibling_rider([g_in], [(D_MODEL, IN_PAD)]), "exchange_sibling_w_in")
    half_in = _add_sibling(g_in, from_sib_in, c_arr, ADD_ROWS[0], "add_sibling_w_in")[0]
    p_in = jnp.stack([half_in[:, k * W_IN_S:(k + 1) * W_IN_S] for k in range(4)], axis=0)
    dx, others_in = _dx(dproj, w_in_p, dr1, t, _chips_rider([p_in]))
    sum_in = _add_chips(p_in, others_in, kc_arr, ADD_ROWS[0], "add_chips_w_in")
    return dx, [sum_in] + late_sums, vecs


W_IN_S, W_UP_S, W_OUT_S, W_DOWN_S = IN_WIDTH // 4, 2 * D_FF // 4, D_MODEL // 4, D_FF // 4
SHARD_SHAPES = ((D_MODEL, W_IN_S), (D_MODEL, W_UP_S), (W_OUT_S, D_MODEL), (W_DOWN_S, D_MODEL))
ADD_ROWS = (256, 256, 128, 176)
LATE_SHAPES, LATE_ADD_ROWS = SHARD_SHAPES[1:], ADD_ROWS[1:]
SMALL_ROWS = 8
VEC_SIZES = (("gate_bias", GLA_KEYS), ("gla_norm_g", LANES), ("ln1_g", D_MODEL), ("ln1_b", D_MODEL),
             ("conv_b", 2 * D_FF), ("ln2_g", D_MODEL), ("ln2_b", D_MODEL))
SMALL_GRADS = VEC_SIZES + (("conv_w", 3 * 2 * D_FF), ("gate_up", GATE_RANK * GLA_KEYS), ("loss", 1))


def _rows(a):
    flat = a.reshape(-1)
    pad = (-flat.shape[0]) % D_MODEL
    if pad:
        flat = jnp.pad(flat, (0, pad))
    return flat.reshape(-1, D_MODEL)


def _pad_rows(a, rows):
    return jnp.pad(a, ((0, rows - a.shape[0]), (0, 0)))


def _pack_vec(d, sizes, rows):
    flat = jnp.concatenate([d[n].reshape(-1) for n, _ in sizes])
    return _pad_rows(_rows(flat), rows)


def _unpack_vec(v, sizes):
    flat = v.reshape(-1)
    out, o = {}, 0
    for n, size in sizes:
        out[n] = flat[o:o + size].reshape(1, size)
        o += size
    return out


VEC_ROWS = 16
GRAD_VEC_ROWS = 32


HBM_SPEC = pl.BlockSpec(memory_space=pltpu.HBM)


def _position():
    x, y, c = lax.axis_index("x"), lax.axis_index("y"), lax.axis_index("c")
    chips = [(1 - x, y), (x, 1 - y), (1 - x, 1 - y)]
    return x, y, c, chips


def _remote(src, dst, send_sems, recv_sems, k, to):
    return pltpu.make_async_remote_copy(src_ref=src, dst_ref=dst, send_sem=send_sems.at[k], recv_sem=recv_sems.at[k],
                                        device_id=to, device_id_type=MESH)


def _gather_ici(in_refs, out_refs, shapes, send_sems, recv_sems, local_sems):
    x, y, c, chips = _position()
    k_me = 2 * x + y
    local, sends, recvs = [], [], []
    for m, (rows, _) in enumerate(shapes):
        h = rows // 2
        local.append(pltpu.make_async_copy(in_refs[m], out_refs[m].at[k_me], local_sems.at[m]))
        for j, (cx, cy) in enumerate(chips):
            sends.append(_remote(in_refs[m].at[pl.ds(c * h, h), :], out_refs[m].at[k_me, pl.ds(c * h, h), :],
                                 send_sems, recv_sems, 3 * m + j, (cx, cy, c)))
            landed = out_refs[m].at[2 * cx + cy, pl.ds(c * h, h), :]
            recvs.append(_remote(landed, landed, send_sems, recv_sems, 3 * m + j, (x, y, c)))
    return local, sends, recvs


def _gather_d2d(src_refs, dst_refs, shapes, send_sems, recv_sems, base):
    x, y, c, chips = _position()
    sends, recvs = [], []
    for m, (rows, _) in enumerate(shapes):
        h = rows // 2
        for j, (cx, cy) in enumerate(chips):
            k = 2 * cx + cy
            sends.append(_remote(src_refs[m].at[k, pl.ds(c * h, h), :], dst_refs[m].at[k, pl.ds(c * h, h), :],
                                 send_sems, recv_sems, base + 3 * m + j, (x, y, 1 - c)))
            landed = dst_refs[m].at[k, pl.ds((1 - c) * h, h), :]
            recvs.append(_remote(landed, landed, send_sems, recv_sems, base + 3 * m + j, (x, y, c)))
    return sends, recvs


def _gather_weights(shards, small, shapes):
    nm = len(shards)
    n_ici = 3 * nm

    def body(*refs):
        in_refs, small_ref = refs[:nm], refs[nm]
        out_refs, osm_ref = refs[nm + 1:2 * nm + 1], refs[2 * nm + 1]
        send_sems, recv_sems, local_sems = refs[2 * nm + 2:]
        x, y, c, chips = _position()
        k_me = 2 * x + y
        local, sends, recvs = _gather_ici(in_refs, out_refs, shapes, send_sems, recv_sems, local_sems)
        local.append(pltpu.make_async_copy(small_ref, osm_ref.at[k_me], local_sems.at[nm]))
        for j, (cx, cy) in enumerate(chips):
            sends.append(_remote(small_ref, osm_ref.at[k_me], send_sems, recv_sems, n_ici + j, (cx, cy, c)))
        for cp in local + sends:
            cp.start()
        fsends, frecvs = _gather_d2d(out_refs, out_refs, shapes, send_sems, recv_sems, n_ici + 3)
        for landed, onward in zip(recvs, fsends):
            landed.wait_recv()
            onward.start()
        for j, (cx, cy) in enumerate(chips):
            k = 2 * cx + cy
            frecvs.append(_remote(osm_ref.at[k], osm_ref.at[k], send_sems, recv_sems, n_ici + j, (x, y, c)))
        for cp in frecvs:
            cp.wait_recv()
        for cp in sends + fsends:
            cp.wait_send()
        for cp in local:
            cp.wait()

    n_sems = 2 * n_ici + 3
    return pl.pallas_call(
        body, name="gather_weights", in_specs=[HBM_SPEC] * (nm + 1), out_specs=[HBM_SPEC] * (nm + 1),
        out_shape=[jax.ShapeDtypeStruct((4,) + s, BF16) for s in shapes]
        + [jax.ShapeDtypeStruct((4, SMALL_ROWS, D_MODEL), F32)],
        scratch_shapes=[pltpu.SemaphoreType.DMA((n_sems,)), pltpu.SemaphoreType.DMA((n_sems,)),
                        pltpu.SemaphoreType.DMA((nm + 1,))],
    )(*shards, small)


def _gather_rider(shards, shapes):
    n = len(shards)
    return _Rider(shards, [jax.ShapeDtypeStruct((4,) + s, BF16) for s in shapes], (3 * n, 3 * n, n),
                  lambda ins, outs, sems: _gather_ici(ins, outs, shapes, *sems))


def _forward_rider(gathered, shapes):
    n = len(gathered)
    return _Rider(gathered, [jax.ShapeDtypeStruct(a.shape, a.dtype) for a in gathered], (3 * n, 3 * n),
                  lambda ins, outs, sems: ([],) + _gather_d2d(ins, outs, shapes, sems[0], sems[1], 0),
                  aliases=[(m, m) for m in range(n)])


def _sibling_rider(gs, shapes):
    def copies(ins, outs, sems):
        x, y, c, _ = _position()
        both = []
        for m, (rows, _) in enumerate(shapes):
            h = rows // 2
            for k in range(gs[m].shape[0]):
                both.append(_remote(ins[m].at[k, pl.ds((1 - c) * h, h), :], outs[m].at[k], sems[0], sems[1],
                                    4 * m + k, (x, y, 1 - c)))
        return [], both, both

    return _Rider(gs, [jax.ShapeDtypeStruct((g.shape[0], r // 2, cl), F32) for g, (r, cl) in zip(gs, shapes)],
                  (4 * len(gs), 4 * len(gs)), copies)


def _add_sibling(g, r, c_arr, tr, name):
    nblk, rows, cols = g.shape
    nb = rows // 2 // tr

    def body(c_ref, g_ref, r_ref, o_ref):
        o_ref[...] = (g_ref[...] + r_ref[...]).astype(BF16)

    spec = pl.BlockSpec((None, tr, cols), lambda k, i, c: (k, i, 0))
    return pl.pallas_call(
        body, name=name,
        grid_spec=pltpu.PrefetchScalarGridSpec(
            num_scalar_prefetch=1, grid=(nblk, nb),
            in_specs=[pl.BlockSpec((None, tr, cols), lambda k, i, c: (k, c[0] * nb + i, 0)), spec], out_specs=spec),
        out_shape=jax.ShapeDtypeStruct((nblk, rows // 2, cols), BF16), compiler_params=_cparams("parallel", "parallel"),
    )(c_arr, g, r)


def _reduce_ici(p_refs, r_refs, send_sems, recv_sems):
    x, y, c, chips = _position()
    sends, recvs = [], []
    for m in range(len(p_refs)):
        for j, (cx, cy) in enumerate(chips):
            sends.append(_remote(p_refs[m].at[2 * cx + cy], r_refs[m].at[j], send_sems, recv_sems, 3 * m + j, (cx, cy, c)))
            recvs.append(_remote(r_refs[m].at[j], r_refs[m].at[j], send_sems, recv_sems, 3 * m + j, (x, y, c)))
    return sends, recvs


def _chips_rider(ps, vec=None):
    nm = len(ps)
    n_ici = 3 * nm

    def copies(ins, outs, sems):
        sends, recvs = _reduce_ici(ins[:nm], outs[:nm], sems[0], sems[1])
        if vec is None:
            return [], sends, recvs
        x, y, c, _ = _position()
        my_id = 4 * x + 2 * y + c
        vec_ref, vrecv_ref = ins[nm], outs[nm]
        local = [pltpu.make_async_copy(vec_ref, vrecv_ref.at[my_id], sems[2].at[0])]
        for r in range(1, 8):
            peer = (1 - x if r & 4 else x, 1 - y if r & 2 else y, 1 - c if r & 1 else c)
            sends.append(_remote(vec_ref, vrecv_ref.at[my_id], sems[0], sems[1], n_ici + r - 1, peer))
            recvs.append(_remote(vec_ref, vrecv_ref.at[0], sems[0], sems[1], n_ici + r - 1, (x, y, c)))
        return local, sends, recvs

    out_shapes = [jax.ShapeDtypeStruct((3,) + p.shape[1:], p.dtype) for p in ps]
    if vec is None:
        return _Rider(ps, out_shapes, (n_ici, n_ici), copies)
    return _Rider(list(ps) + [vec], out_shapes + [jax.ShapeDtypeStruct((8, GRAD_VEC_ROWS, D_MODEL), F32)],
                  (n_ici + 7, n_ici + 7, 1), copies)


def _add_chips(p, r, kc_arr, tr, name):
    _, h, cols = p.shape
    nb = h // tr

    def body(kc_ref, p_ref, r0_ref, r1_ref, r2_ref, o_ref):
        o_ref[...] = ((p_ref[...].astype(F32) + r0_ref[...].astype(F32)) + r1_ref[...].astype(F32)) + r2_ref[...].astype(F32)

    rspec = lambda j: pl.BlockSpec((None, tr, cols), lambda i, kc: (j, i, 0))
    return pl.pallas_call(
        body, name=name,
        grid_spec=pltpu.PrefetchScalarGridSpec(
            num_scalar_prefetch=1, grid=(nb,),
            in_specs=[pl.BlockSpec((None, tr, cols), lambda i, kc: (kc[0], i, 0)), rspec(0), rspec(1), rspec(2)],
            out_specs=pl.BlockSpec((tr, cols), lambda i, kc: (kc[1] * nb + i, 0))),
        out_shape=jax.ShapeDtypeStruct((2 * h, cols), F32), compiler_params=_cparams("parallel"),
    )(kc_arr, p, r, r, r)


def _reunite_sibling(fs, shapes):
    n_chunks = 2
    nm = len(fs)

    def body(*refs):
        in_refs, out_refs = refs[:nm], refs[nm:2 * nm]
        send_sems, recv_sems = refs[2 * nm:]
        x, y, c, _ = _position()
        sends, recvs = [], []
        for m in range(nm):
            ch = shapes[m][0] // 2 // n_chunks
            for q in range(n_chunks):
                mine = pl.ds((c * n_chunks + q) * ch, ch)
                theirs = pl.ds(((1 - c) * n_chunks + q) * ch, ch)
                s = n_chunks * m + q
                sends.append(_remote(in_refs[m].at[mine, :], out_refs[m].at[mine, :], send_sems, recv_sems, s, (x, y, 1 - c)))
                recvs.append(_remote(in_refs[m].at[theirs, :], out_refs[m].at[theirs, :], send_sems, recv_sems, s, (x, y, c)))
        for cp in sends:
            cp.start()
        for cp in recvs:
            cp.wait_recv()
        for cp in sends:
            cp.wait_send()

    n_sems = n_chunks * nm
    return pl.pallas_call(
        body, name="reunite_sibling", in_specs=[HBM_SPEC] * nm, out_specs=[HBM_SPEC] * nm,
        out_shape=[jax.ShapeDtypeStruct(s, F32) for s in shapes],
        input_output_aliases={m: m for m in range(nm)},
        scratch_shapes=[pltpu.SemaphoreType.DMA((n_sems,)), pltpu.SemaphoreType.DMA((n_sems,))],
    )(*fs)


def _sum_vecs(v):
    def body(v_ref, o_ref):
        acc = v_ref[0]
        for d in range(1, 8):
            acc = acc + v_ref[d]
        o_ref[...] = acc

    return pl.pallas_call(body, name="sum_vecs", out_shape=jax.ShapeDtypeStruct(v.shape[1:], F32))(v)


def kernel(x, w_in, gate_up, gate_bias, gla_norm_g, w_out, ln1_g, ln1_b, w_up, conv_w, conv_b, w_down, ln2_g, ln2_b, loss_target, m_w_in, m_gate_up, m_gate_bias, m_gla_norm_g, m_w_out, m_ln1_g, m_ln1_b, m_w_up, m_conv_w, m_conv_b, m_w_down, m_ln2_g, m_ln2_b, v_w_in, v_gate_up, v_gate_bias, v_gla_norm_g, v_w_out, v_ln1_g, v_ln1_b, v_w_up, v_conv_w, v_conv_b, v_w_down, v_ln2_g, v_ln2_b):
    xi, yi, ci = lax.axis_index("x"), lax.axis_index("y"), lax.axis_index("c")
    k_me = 2 * xi + yi
    c_arr = ci.astype(jnp.int32).reshape(1)
    kc_arr = jnp.stack([k_me, ci]).astype(jnp.int32)
    small = _pad_rows(jnp.concatenate([_rows(conv_w[0]), _rows(gate_up[0])], axis=0), SMALL_ROWS)
    w_in4, gsmall = _gather_weights([w_in[0].astype(BF16)], small, SHARD_SHAPES[:1])
    late_shards = [w_up[0].astype(BF16), w_out[0].astype(BF16), w_down[0].astype(BF16)]
    w_in_p = jnp.pad(jnp.concatenate([w_in4[k] for k in range(4)], axis=1), ((0, 0), (0, IN_PAD - IN_WIDTH)))
    conv_w_f = jnp.concatenate([gsmall[k, :5].reshape(-1)[:3 * W_UP_S].reshape(3, W_UP_S) for k in range(4)], axis=1)
    gate_up_f = jnp.concatenate([gsmall[k, 5].reshape(GATE_RANK, GLA_KEYS // 4) for k in range(4)], axis=1)
    conv_w3 = jnp.transpose(conv_w_f.reshape(3, 2, D_FF), (1, 0, 2))
    conv_b3 = conv_b.reshape(2, 1, D_FF)
    gate_up_pad = jnp.pad(gate_up_f, ((0, LANES - GATE_RANK), (0, 0)))

    dx, sums, vecs = _local_step(
        x[0], loss_target[0], w_in_p, late_shards, gate_up_pad, gate_bias, gla_norm_g, ln1_g, ln1_b, conv_w3, conv_b3,
        ln2_g, ln2_b, c_arr, kc_arr)
    g_w_in, g_w_up, g_w_out, g_w_down = _reunite_sibling(sums, SHARD_SHAPES)
    gsmall_sum = _unpack_vec(_sum_vecs(vecs), SMALL_GRADS)
    g_conv_w = lax.dynamic_slice_in_dim(gsmall_sum["conv_w"].reshape(3, 2 * D_FF), k_me * W_UP_S, W_UP_S, axis=1)
    g_gate_up = lax.dynamic_slice_in_dim(gsmall_sum["gate_up"].reshape(GATE_RANK, GLA_KEYS), k_me * (GLA_KEYS // 4),
                                         GLA_KEYS // 4, axis=1)
    gv = gsmall_sum
    loss = gv["loss"][0, 0]
    gvec = _pack_vec(gv, VEC_SIZES, VEC_ROWS)

    grads = dict(w_in=g_w_in[None], gate_up=g_gate_up[None], gate_bias=gv["gate_bias"], gla_norm_g=gv["gla_norm_g"],
                 w_out=g_w_out[None], ln1_g=gv["ln1_g"], ln1_b=gv["ln1_b"], w_up=g_w_up[None], conv_w=g_conv_w[None],
                 conv_b=gv["conv_b"], w_down=g_w_down[None], ln2_g=gv["ln2_g"], ln2_b=gv["ln2_b"])
    weights = dict(w_in=w_in, gate_up=gate_up, gate_bias=gate_bias, gla_norm_g=gla_norm_g, w_out=w_out, ln1_g=ln1_g,
                   ln1_b=ln1_b, w_up=w_up, conv_w=conv_w, conv_b=conv_b, w_down=w_down, ln2_g=ln2_g, ln2_b=ln2_b)
    ms = dict(w_in=m_w_in, gate_up=m_gate_up, gate_bias=m_gate_bias, gla_norm_g=m_gla_norm_g, w_out=m_w_out, ln1_g=m_ln1_g,
              ln1_b=m_ln1_b, w_up=m_w_up, conv_w=m_conv_w, conv_b=m_conv_b, w_down=m_w_down, ln2_g=m_ln2_g, ln2_b=m_ln2_b)
    vs = dict(w_in=v_w_in, gate_up=v_gate_up, gate_bias=v_gate_bias, gla_norm_g=v_gla_norm_g, w_out=v_w_out, ln1_g=v_ln1_g,
              ln1_b=v_ln1_b, w_up=v_w_up, conv_w=v_conv_w, conv_b=v_conv_b, w_down=v_w_down, ln2_g=v_ln2_g, ln2_b=v_ln2_b)
    names = ["w_in", "gate_up", "gate_bias", "gla_norm_g", "w_out", "ln1_g", "ln1_b", "w_up", "conv_w", "conv_b", "w_down",
             "ln2_g", "ln2_b"]
    delta, new_m, new_v = {}, {}, {}
    for n in ("w_in", "gate_up", "w_out", "w_up", "conv_w", "w_down"):
        tr_ = jnp.transpose if n == "w_in" else (lambda a: a)
        d, nm, nv = _adamw(tr_(weights[n][0]), tr_(grads[n][0]), tr_(ms[n][0]), tr_(vs[n][0]), "adamw_" + n)
        delta[n], new_m[n], new_v[n] = tr_(d)[None], tr_(nm)[None], tr_(nv)[None]
    d, nm, nv = _adamw(_pack_vec(weights, VEC_SIZES, VEC_ROWS), gvec, _pack_vec(ms, VEC_SIZES, VEC_ROWS),
                       _pack_vec(vs, VEC_SIZES, VEC_ROWS), "adamw_vectors")
    for dst, src in ((delta, d), (new_m, nm), (new_v, nv)):
        dst.update(_unpack_vec(src, VEC_SIZES))
    return (loss, dx[None], *[grads[n] for n in names], *[delta[n] for n in names], *[new_m[n] for n in names],
            *[new_v[n] for n in names])
```

```python
import math

import jax
import jax.numpy as jnp
from jax import lax
from jax.experimental import pallas as pl
from jax.experimental.pallas import tpu as pltpu

F32 = jnp.float32
BF16 = jnp.bfloat16

D_MODEL = 1024
SB_WIDTH = 512
GLA_KEYS = 256
GLA_WIDTH = 512
GATE_RANK = 16
IN_WIDTH = 3088
IN_PAD = 3200
D_FF = 2816
CHUNK = 64
LN_EPS = 1e-5
RMS_EPS = 1e-6
ALPHA = 2.0 ** 0.25
GLA_TAU = 16.0
SB_SCALE = 0.125
GLA_SCALE = 0.125
LANES = 128
SB_BLK = 256
SB_CUT = -100.0
GLA_ROWS = 256
VMEM_LIMIT = 56 * 1024 * 1024

ADAM_LR, ADAM_B1, ADAM_B2, ADAM_EPS, ADAM_WD, ADAM_STEP = 0.001, 0.9, 0.999, 1e-08, 0.01, 10

MESH = pl.DeviceIdType.MESH


def _cparams(*sem):
    return pltpu.CompilerParams(dimension_semantics=sem, vmem_limit_bytes=VMEM_LIMIT)


def _dot(a, b):
    return jnp.dot(a, b, preferred_element_type=F32)


def _dot_nt(a, b):
    return lax.dot_general(a, b, (((1,), (1,)), ((), ())), preferred_element_type=F32)


def _dot_tn(a, b):
    return lax.dot_general(a, b, (((0,), (0,)), ((), ())), preferred_element_type=F32)


def _split3(x):
    hi = x.astype(BF16)
    r = x - hi.astype(F32)
    mid = r.astype(BF16)
    lo = (r - mid.astype(F32)).astype(BF16)
    return hi, mid, lo


def _softplus(z):
    return jnp.maximum(z, 0.0) + jnp.log(1.0 + jnp.exp(-jnp.abs(z)))


def _sigmoid(z):
    return 1.0 / (1.0 + jnp.exp(-z))


def _mm(a, b, *, m, n, k, tm, tn, tk, ta=False, tb=False, a_spec=None, b_spec=None, o_spec=None,
        out_shape=None, out_dtype=F32, add=None, add_scale=1.0, rider=None, name):
    nk = k // tk
    dn = (((0 if ta else 1,), (1 if tb else 0,)), ((), ()))
    n_in = len(rider.inputs) if rider else 0
    n_out = len(rider.out_shapes) if rider else 0
    n_add = int(add is not None)
    steps = (m // tm, n // tn, nk)

    def body(*refs):
        a_ref, b_ref = refs[:2]
        add_ref = refs[2] if add is not None else None
        rin = refs[2 + n_add:2 + n_add + n_in]
        o_ref = refs[2 + n_add + n_in]
        rout = refs[3 + n_add + n_in:3 + n_add + n_in + n_out]
        scratch = refs[3 + n_add + n_in + n_out:]
        if rider:
            sems = scratch[len(scratch) - len(rider.sems):]
            ids = [pl.program_id(d) for d in range(3)]
            _ride(rider, rin, rout, sems, (ids[0] == 0) & (ids[1] == 0) & (ids[2] == 0),
                  (ids[0] == steps[0] - 1) & (ids[1] == steps[1] - 1) & (ids[2] == steps[2] - 1))

        part = lax.dot_general(a_ref[...].astype(BF16), b_ref[...].astype(BF16), dn, preferred_element_type=F32)

        def finish(r):
            if add is not None:
                r = r + add_scale * add_ref[...]
            o_ref[...] = r.astype(out_dtype)

        if nk == 1:
            finish(part)
            return
        acc_ref = scratch[0]
        kk = pl.program_id(2)

        @pl.when(kk == 0)
        def _():
            acc_ref[...] = part

        @pl.when((kk > 0) & (kk < nk - 1))
        def _():
            acc_ref[...] += part

        @pl.when(kk == nk - 1)
        def _():
            finish(acc_ref[...] + part)

    if a_spec is None:
        a_spec = pl.BlockSpec((tk, tm), lambda i, j, kk: (kk, i)) if ta else pl.BlockSpec((tm, tk), lambda i, j, kk: (i, kk))
    if b_spec is None:
        b_spec = pl.BlockSpec((tn, tk), lambda i, j, kk: (j, kk)) if tb else pl.BlockSpec((tk, tn), lambda i, j, kk: (kk, j))
    if o_spec is None:
        o_spec = pl.BlockSpec((tm, tn), lambda i, j, kk: (i, j))
    if out_shape is None:
        out_shape = jax.ShapeDtypeStruct((m, n), out_dtype)
    in_specs = [a_spec, b_spec]
    args = [a, b]
    if add is not None:
        in_specs.append(pl.BlockSpec((tm, tn), lambda i, j, kk: (i, j)))
        args.append(add)
    scratch = [pltpu.VMEM((tm, tn), F32)] if nk > 1 else []
    if not rider:
        return pl.pallas_call(
            body, name=name, grid=steps, in_specs=in_specs, out_specs=o_spec, out_shape=out_shape,
            scratch_shapes=scratch, compiler_params=_cparams("parallel", "parallel", "arbitrary"),
        )(*args)
    return pl.pallas_call(
        body, name=name, grid=steps, in_specs=in_specs + [HBM_SPEC] * n_in, out_specs=[o_spec] + [HBM_SPEC] * n_out,
        out_shape=[out_shape] + list(rider.out_shapes),
        input_output_aliases={len(args) + i: 1 + o for i, o in rider.aliases},
        scratch_shapes=scratch + [pltpu.SemaphoreType.DMA((s,)) for s in rider.sems],
        compiler_params=_cparams("arbitrary", "arbitrary", "arbitrary"),
    )(*args, *rider.inputs)


def _col_offsets(pieces):
    offs, o = [], 0
    for a in pieces:
        offs.append(o)
        o += a.shape[1]
    return offs


def _dx(pieces, w_in_p, dr1, t, rider, tm=512):
    offs = _col_offsets(pieces)
    npc = len(pieces)
    n_in, n_out = len(rider.inputs), len(rider.out_shapes)
    steps = t // tm

    def body(*refs):
        p_refs, w_ref, add_ref = refs[:npc], refs[npc], refs[npc + 1]
        rin = refs[npc + 2:npc + 2 + n_in]
        o_ref = refs[npc + 2 + n_in]
        rout = refs[npc + 3 + n_in:npc + 3 + n_in + n_out]
        i = pl.program_id(0)
        _ride(rider, rin, rout, refs[npc + 3 + n_in + n_out:], i == 0, i == steps - 1)
        acc = ALPHA * add_ref[...]
        for p_ref, off in zip(p_refs, offs):
            acc = acc + _dot_nt(p_ref[...].astype(BF16), w_ref[:, off:off + p_ref.shape[1]])
        o_ref[...] = acc

    row = lambda w: pl.BlockSpec((tm, w), lambda i: (i, 0))
    return pl.pallas_call(
        body, name="dx", grid=(steps,),
        in_specs=[row(a.shape[1]) for a in pieces] + [pl.BlockSpec(w_in_p.shape, lambda i: (0, 0)), row(D_MODEL)]
        + [HBM_SPEC] * n_in,
        out_specs=[row(D_MODEL)] + [HBM_SPEC] * n_out,
        out_shape=[jax.ShapeDtypeStruct((t, D_MODEL), F32)] + rider.out_shapes,
        scratch_shapes=[pltpu.SemaphoreType.DMA((s,)) for s in rider.sems],
        compiler_params=_cparams("arbitrary"),
    )(*pieces, w_in_p, dr1, *rider.inputs)


def _dw_out(sb_o, gla_o, dr1, t, tk):
    def body(sb_ref, gl_ref, dr_ref, o_ref):
        kk = pl.program_id(0)
        drb = dr_ref[...].astype(BF16)
        top, bottom = _dot_tn(sb_ref[...], drb), _dot_tn(gl_ref[...], drb)

        @pl.when(kk == 0)
        def _():
            o_ref[0:SB_WIDTH, :] = top
            o_ref[SB_WIDTH:, :] = bottom

        @pl.when(kk > 0)
        def _():
            o_ref[0:SB_WIDTH, :] += top
            o_ref[SB_WIDTH:, :] += bottom

    return pl.pallas_call(
        body, name="dw_out", grid=(t // tk,),
        in_specs=[pl.BlockSpec((tk, SB_WIDTH), lambda kk: (kk, 0)), pl.BlockSpec((tk, GLA_WIDTH), lambda kk: (kk, 0)),
                  pl.BlockSpec((tk, D_MODEL), lambda kk: (kk, 0))],
        out_specs=pl.BlockSpec((D_MODEL, D_MODEL), lambda kk: (0, 0)),
        out_shape=jax.ShapeDtypeStruct((D_MODEL, D_MODEL), F32),
        compiler_params=_cparams("arbitrary"),
    )(sb_o, gla_o, dr1)


def _dw_in(x, pieces, t, tm=512, tk=512):
    offs = _col_offsets(pieces)
    npc = len(pieces)
    n = offs[-1] + pieces[-1].shape[1]
    nk = t // tk

    def body(*refs):
        x_ref, p_refs, o_ref, acc_ref = refs[0], refs[1:1 + npc], refs[1 + npc], refs[2 + npc]
        kk = pl.program_id(1)
        xb = x_ref[...].astype(BF16)
        for p_ref, off in zip(p_refs, offs):
            cols = slice(off, off + p_ref.shape[1])
            part = _dot_tn(xb, p_ref[...].astype(BF16))
            if nk == 1:
                o_ref[:, cols] = part
                continue

            @pl.when(kk == 0)
            def _():
                acc_ref[:, cols] = part

            @pl.when((kk > 0) & (kk < nk - 1))
            def _():
                acc_ref[:, cols] += part

            @pl.when(kk == nk - 1)
            def _():
                o_ref[:, cols] = acc_ref[:, cols] + part

    return pl.pallas_call(
        body, name="dw_in", grid=(D_MODEL // tm, nk),
        in_specs=[pl.BlockSpec((tk, tm), lambda i, kk: (kk, i))]
        + [pl.BlockSpec((tk, a.shape[1]), lambda i, kk: (kk, 0)) for a in pieces],
        out_specs=pl.BlockSpec((tm, n), lambda i, kk: (i, 0)),
        out_shape=jax.ShapeDtypeStruct((D_MODEL, n), F32),
        scratch_shapes=[pltpu.VMEM((tm, n), F32)],
        compiler_params=_cparams("parallel", "arbitrary"),
    )(x, *pieces)


class _Rider:
    def __init__(self, inputs, out_shapes, sems, copies, aliases=()):
        self.inputs, self.out_shapes, self.sems, self.copies = list(inputs), list(out_shapes), tuple(sems), copies
        self.aliases = tuple(aliases)


def _join(a, b):
    na_in, na_out, na_sems = len(a.inputs), len(a.out_shapes), len(a.sems)

    def copies(ins, outs, sems):
        first = a.copies(ins[:na_in], outs[:na_out], sems[:na_sems])
        second = b.copies(ins[na_in:], outs[na_out:], sems[na_sems:])
        return tuple(u + v for u, v in zip(first, second))

    return _Rider(a.inputs + b.inputs, a.out_shapes + b.out_shapes, a.sems + b.sems, copies,
                  a.aliases + tuple((i + na_in, o + na_out) for i, o in b.aliases))


def _ride(rider, in_refs, out_refs, sems, first, last):
    @pl.when(first)
    def _():
        local, sends, _ = rider.copies(in_refs, out_refs, sems)
        for cp in local + sends:
            cp.start()

    @pl.when(last)
    def _():
        local, sends, recvs = rider.copies(in_refs, out_refs, sems)
        for cp in recvs:
            cp.wait_recv()
        for cp in sends:
            cp.wait_send()
        for cp in local:
            cp.wait()


def _run(rider, name):
    n_in, n_out = len(rider.inputs), len(rider.out_shapes)

    def body(*refs):
        local, sends, recvs = rider.copies(refs[:n_in], refs[n_in:n_in + n_out], refs[n_in + n_out:])
        for cp in local + sends:
            cp.start()
        for cp in recvs:
            cp.wait_recv()
        for cp in sends:
            cp.wait_send()
        for cp in local:
            cp.wait()

    return pl.pallas_call(
        body, name=name, in_specs=[HBM_SPEC] * n_in, out_specs=[HBM_SPEC] * n_out, out_shape=list(rider.out_shapes),
        scratch_shapes=[pltpu.SemaphoreType.DMA((s,)) for s in rider.sems],
    )(*rider.inputs)


def _sb_tile(qh, kj, diag, strict, u_strict, r_in):
    z = _dot_nt(qh, kj)
    sp = _softplus(z)
    l1m = -sp
    lsz = z - sp
    if diag:
        l1m = jnp.where(strict, l1m, 0.0)
    cs = _dot(l1m.astype(BF16), u_strict) + r_in
    w = jnp.exp(lsz + cs)
    if diag:
        w = jnp.where(strict, w, 0.0)
    return l1m, lsz, w


def _sb_consts():
    row = lax.broadcasted_iota(jnp.int32, (SB_BLK, SB_BLK), 0)
    col = lax.broadcasted_iota(jnp.int32, (SB_BLK, SB_BLK), 1)
    strict = col < row
    u_strict = (row > col).astype(BF16)
    u_pre = (row < col).astype(BF16)
    lane = lax.broadcasted_iota(jnp.int32, (1, LANES), 1)
    return strict, u_strict, u_pre, lane


def _sb_fwd(proj, t, rider):
    nq = t // SB_BLK
    n_in, n_out = len(rider.inputs), len(rider.out_shapes)

    def body(q_ref, k_ref, v_ref, *rest):
        rin, o_ref, sv_ref, rout = rest[:n_in], rest[n_in], rest[n_in + 1], rest[n_in + 2:n_in + 2 + n_out]
        p = pl.program_id(0)
        i = pl.program_id(1)
        _ride(rider, rin, rout, rest[n_in + 2 + n_out:], (p == 0) & (i == 0), (p == 3) & (i == nq - 1))

        strict, u_strict, _, lane = _sb_consts()
        qf = q_ref[...] * SB_SCALE
        hms = [(lane // 64) == hh for hh in range(2)]
        qhs = [jnp.where(hm, qf, 0.0).astype(BF16) for hm in hms]

        def step(j, r0, r1, a, sv, diag, keep=None):
            rows = pl.ds(pl.multiple_of(j * SB_BLK, SB_BLK), SB_BLK)
            kj = k_ref[rows, :].astype(BF16)
            vf = v_ref[rows, :]
            rs = []
            for hh, r in enumerate((r0, r1)):
                l1m, _, w = _sb_tile(qhs[hh], kj, diag, strict, u_strict, r)
                pv = _dot(w.astype(BF16), jnp.where(hms[hh], vf, 0.0).astype(BF16))
                mass = jnp.sum(l1m, axis=1, keepdims=True)
                if keep is not None:
                    pv, mass = jnp.where(keep, pv, 0.0), jnp.where(keep, mass, 0.0)
                a = a + pv
                rs.append(r + mass)
                sv = jnp.where(lane == hh * 64 + (i - j + 1), rs[hh], sv)
            return rs[0], rs[1], a, sv

        zero = jnp.zeros((SB_BLK, 1), F32)
        acc0 = jnp.zeros((SB_BLK, LANES), F32)
        r0, r1, acc, sv = step(i, zero, zero, acc0, acc0, True)
        r0, r1, acc, sv = step(jnp.maximum(i - 1, 0), r0, r1, acc, sv, False, keep=i > 0)
        _, _, _, acc, sv = lax.while_loop(
            lambda c: (c[0] >= 0) & (jnp.maximum(jnp.max(c[1]), jnp.max(c[2])) > SB_CUT),
            lambda c: (c[0] - 1,) + step(c[0], c[1], c[2], c[3], c[4], False),
            (i - 2, r0, r1, acc, sv))
        o_ref[...] = acc.astype(BF16)
        sv_ref[...] = sv

    return pl.pallas_call(
        body, name="sb_fwd", grid=(4, nq),
        in_specs=[pl.BlockSpec((SB_BLK, LANES), lambda p, i: (i, p)),
                  pl.BlockSpec((t, LANES), lambda p, i: (0, 4 + p)),
                  pl.BlockSpec((t, LANES), lambda p, i: (0, 8 + p))] + [HBM_SPEC] * n_in,
        out_specs=[pl.BlockSpec((SB_BLK, LANES), lambda p, i: (i, p))] * 2 + [HBM_SPEC] * n_out,
        out_shape=[jax.ShapeDtypeStruct((t, SB_WIDTH), BF16), jax.ShapeDtypeStruct((t, SB_WIDTH), F32)] + rider.out_shapes,
        scratch_shapes=[pltpu.SemaphoreType.DMA((s,)) for s in rider.sems],
        compiler_params=_cparams("arbitrary", "arbitrary"),
    )(proj, proj, proj, *rider.inputs)


def _sb_bwd(proj, dcat, mass, t, rider):
    nq = t // SB_BLK
    n_in, n_out = len(rider.inputs), len(rider.out_shapes)

    def body(q_ref, k_ref, v_ref, do_ref, sv_ref, *rest):
        rin = rest[:n_in]
        dq_ref, dk_ref, dv_ref = rest[n_in:n_in + 3]
        rout = rest[n_in + 3:n_in + 3 + n_out]
        p = pl.program_id(0)
        i = pl.program_id(1)
        _ride(rider, rin, rout, rest[n_in + 3 + n_out:], (p == 0) & (i == 0), (p == 3) & (i == nq - 1))

        @pl.when(i == 0)
        def _():
            dk_ref[...] = jnp.zeros_like(dk_ref)
            dv_ref[...] = jnp.zeros_like(dv_ref)

        strict, u_strict, u_pre, lane = _sb_consts()
        qf = q_ref[...] * SB_SCALE
        dof = do_ref[...]
        hms = [(lane // 64) == hh for hh in range(2)]
        qhs = [jnp.where(hm, qf, 0.0).astype(BF16) for hm in hms]
        dohs = [jnp.where(hm, dof, 0.0).astype(BF16) for hm in hms]

        sv = sv_ref[...]
        zero = jnp.zeros((SB_BLK, 1), F32)

        def mass_right(hh, d):
            return jnp.sum(jnp.where(lane == hh * 64 + d, sv, 0.0), axis=1, keepdims=True)

        dstop = lax.while_loop(
            lambda d: (i - d >= 0) & (jnp.maximum(jnp.max(mass_right(0, d)), jnp.max(mass_right(1, d))) > SB_CUT),
            lambda d: d + 1, 1)
        jstop = i - dstop

        def step(j, carry, diag, keep=None):
            pre_g0, pre_g1, dqa = carry
            rows = pl.ds(pl.multiple_of(j * SB_BLK, SB_BLK), SB_BLK)
            kf = k_ref[rows, :]
            kj = kf.astype(BF16)
            vj = v_ref[rows, :].astype(BF16)
            dv = jnp.zeros((SB_BLK, LANES), F32)
            dk = jnp.zeros((SB_BLK, LANES), F32)
            dqj = jnp.zeros((SB_BLK, LANES), F32)
            pre = []
            for hh, pre_g in enumerate((pre_g0, pre_g1)):
                _, lsz, w = _sb_tile(qhs[hh], kj, diag, strict, u_strict, zero if diag else mass_right(hh, i - j))
                g = w * _dot_nt(dohs[hh], vj)
                gpre = _dot(g.astype(BF16), u_pre) + pre_g
                sig = jnp.exp(lsz)
                dz = g * (1.0 - sig) - gpre * sig
                if diag:
                    dz = jnp.where(strict, dz, 0.0)
                dzb = dz.astype(BF16)
                dv = dv + _dot_tn(w.astype(BF16), dohs[hh])
                dk = dk + _dot_tn(dzb, qhs[hh])
                dqj = dqj + _dot(dzb, jnp.where(hms[hh], kf, 0.0).astype(BF16))
                gsum = jnp.sum(g, axis=1, keepdims=True)
                pre.append(pre_g + (gsum if keep is None else jnp.where(keep, gsum, 0.0)))
            if keep is not None:
                dv, dk, dqj = jnp.where(keep, dv, 0.0), jnp.where(keep, dk, 0.0), jnp.where(keep, dqj, 0.0)
            dv_ref[rows, :] += dv
            dk_ref[rows, :] += dk
            return pre[0], pre[1], dqa + dqj

        carry = lax.fori_loop(jstop + 1, i - 1, lambda j, c: step(j, c, False), (zero, zero, jnp.zeros((SB_BLK, LANES), F32)))
        carry = step(jnp.maximum(i - 1, 0), carry, False, keep=i > 0)
        _, _, dq = step(i, carry, True)
        dq_ref[...] = (dq * SB_SCALE).astype(BF16)

    return pl.pallas_call(
        body, name="sb_bwd", grid=(4, nq),
        in_specs=[pl.BlockSpec((SB_BLK, LANES), lambda p, i: (i, p)),
                  pl.BlockSpec((t, LANES), lambda p, i: (0, 4 + p)),
                  pl.BlockSpec((t, LANES), lambda p, i: (0, 8 + p)),
                  pl.BlockSpec((SB_BLK, LANES), lambda p, i: (i, p)),
                  pl.BlockSpec((SB_BLK, LANES), lambda p, i: (i, p))] + [HBM_SPEC] * n_in,
        out_specs=[pl.BlockSpec((SB_BLK, LANES), lambda p, i: (i, p)),
                   pl.BlockSpec((t, LANES), lambda p, i: (0, p)),
                   pl.BlockSpec((t, LANES), lambda p, i: (0, p))] + [HBM_SPEC] * n_out,
        out_shape=[jax.ShapeDtypeStruct((t, SB_WIDTH), BF16)] + [jax.ShapeDtypeStruct((t, SB_WIDTH), F32)] * 2
        + rider.out_shapes,
        scratch_shapes=[pltpu.SemaphoreType.DMA((s,)) for s in rider.sems],
        compiler_params=_cparams("arbitrary", "arbitrary"),
    )(proj, proj, proj, dcat, mass, *rider.inputs)


def _gla_consts():
    r = lax.broadcasted_iota(jnp.int32, (GLA_ROWS, GLA_ROWS), 0)
    c = lax.broadcasted_iota(jnp.int32, (GLA_ROWS, GLA_ROWS), 1)
    same = (r // CHUNK) == (c // CHUNK)
    causal = same & (c <= r)
    upto_mid = c % CHUNK <= CHUNK // 2 - 1
    fwd_stack = jnp.concatenate([causal, same & upto_mid, same], axis=0).astype(BF16)
    bwd_stack = jnp.concatenate([same & (c >= r), same & (r % CHUNK <= CHUNK // 2 - 1), same], axis=1).astype(BF16)
    rowid = lax.broadcasted_iota(jnp.int32, (GLA_ROWS, 1), 0)
    lane = lax.broadcasted_iota(jnp.int32, (1, LANES), 1)
    sr = lax.broadcasted_iota(jnp.int32, (LANES, 2 * LANES), 0)
    sc = lax.broadcasted_iota(jnp.int32, (LANES, 2 * LANES), 1)
    blockdiag = (sr // 64) == (sc // LANES)
    return causal, fwd_stack, bwd_stack, rowid, lane, blockdiag


def _dot3(u, x):
    hi, mid, lo = _split3(x)
    return _dot(u, hi) + _dot(u, mid) + _dot(u, lo)


def _row_to_col(row):
    return jnp.transpose(jnp.broadcast_to(row, (LANES, LANES)))


def _gla_gates(ga_ref, gup_ref, gbias_ref):
    pre = _dot(ga_ref[...].astype(BF16), gup_ref[...].astype(BF16)) + gbias_ref[...]
    log_a = (jnp.minimum(pre, 0.0) - jnp.log(1.0 + jnp.exp(-jnp.abs(pre)))) / GLA_TAU
    return pre, log_a


def _gla_terms(g2, q2, k2, fwd_stack):
    bs = _dot3(fwd_stack, g2)
    b, b_ref, b_last = bs[:GLA_ROWS], bs[GLA_ROWS:2 * GLA_ROWS], bs[2 * GLA_ROWS:]
    qs = q2 * GLA_SCALE
    e_q = jnp.exp(b - b_ref)
    e_k = jnp.exp(b_ref - b)
    e_d = jnp.exp(b_last - b)
    e_b = jnp.exp(b)
    decay = jnp.exp(b_last)
    return dict(qs=qs, e_q=e_q, e_k=e_k, e_d=e_d, e_b=e_b, decay=decay,
                qi=qs * e_q, ki=k2 * e_k, kd=k2 * e_d, qb=qs * e_b)


def _gla_fwd(proj, gate_up_pad, gate_bias, gnorm, t, rider):
    nsteps = t // GLA_ROWS
    cps = GLA_ROWS // CHUNK
    n_in, n_out = len(rider.inputs), len(rider.out_shapes)

    def body(q_ref, k_ref, v_ref, gg_ref, ga_ref, gup_ref, gbias_ref, gn_ref, *rest):
        o_ref, oraw_ref, st_ref = rest[n_in:n_in + 3]
        s_scr = rest[n_in + 3 + n_out]
        i = pl.program_id(0)
        _ride(rider, rest[:n_in], rest[n_in + 3:n_in + 3 + n_out], rest[n_in + 4 + n_out:], i == 0, i == nsteps - 1)

        @pl.when(i == 0)
        def _():
            s_scr[...] = jnp.zeros_like(s_scr)

        _, fwd_stack, _, _, lane, blockdiag = _gla_consts()
        causal64 = (lax.broadcasted_iota(jnp.int32, (CHUNK, CHUNK), 1) <= lax.broadcasted_iota(jnp.int32, (CHUNK, CHUNK), 0))
        _, log_a = _gla_gates(ga_ref, gup_ref, gbias_ref)
        gn = gn_ref[...]
        pre = []
        for p in range(2):
            kl = slice(p * LANES, (p + 1) * LANES)
            tm = _gla_terms(log_a[:, kl], q_ref[:, kl], k_ref[:, kl], fwd_stack)
            pre.append(dict(
                v2b=v_ref[:, p * 2 * LANES:(p + 1) * 2 * LANES].astype(BF16), kib=tm["ki"].astype(BF16),
                kdb=tm["kd"].astype(BF16), qbb=tm["qb"].astype(BF16), decay=tm["decay"],
                qihb=[jnp.where((lane // 64) == hh, tm["qi"], 0.0).astype(BF16) for hh in range(2)]))
        state = [s_scr[0], s_scr[1]]
        outs = [[], []]
        for cc in range(cps):
            rows = slice(cc * CHUNK, (cc + 1) * CHUNK)
            for p in range(2):
                w = pre[p]
                st_ref[cc, p] = state[p]
                intra = []
                for hh in range(2):
                    a = jnp.where(causal64, _dot_nt(w["qihb"][hh][rows], w["kib"][rows]), 0.0)
                    intra.append(_dot(a.astype(BF16), w["v2b"][rows, hh * LANES:(hh + 1) * LANES]))
                outs[p].append(jnp.concatenate(intra, axis=1) + _dot(w["qbb"][rows], state[p].astype(BF16)))
                upd = jnp.where(blockdiag, _dot_tn(w["kdb"][rows], w["v2b"][rows]), 0.0)
                dcol = _row_to_col(w["decay"][cc * CHUNK:cc * CHUNK + 1])
                state[p] = state[p] * jnp.concatenate([dcol, dcol], axis=1) + upd
        for p in range(2):
            s_scr[p] = state[p]
            vl = slice(p * 2 * LANES, (p + 1) * 2 * LANES)
            o2 = jnp.concatenate(outs[p], axis=0)
            oraw_ref[:, vl] = o2
            for hh in range(2):
                oh = o2[:, hh * LANES:(hh + 1) * LANES]
                gl = slice(p * 2 * LANES + hh * LANES, p * 2 * LANES + (hh + 1) * LANES)
                rinv = lax.rsqrt(jnp.mean(oh * oh, axis=1, keepdims=True) + RMS_EPS)
                gg = gg_ref[:, gl]
                o_ref[:, gl] = (oh * rinv * gn * (gg * _sigmoid(gg))).astype(BF16)

    cb = lambda w, idx: pl.BlockSpec((GLA_ROWS, w), lambda i: (i, idx))
    full = lambda shp: pl.BlockSpec(shp, lambda i: tuple(0 for _ in shp))
    return pl.pallas_call(
        body, name="gla_fwd", grid=(nsteps,),
        in_specs=[cb(256, 6), cb(256, 7), cb(512, 4), cb(512, 5), cb(128, 24),
                  full((LANES, GLA_KEYS)), full((1, GLA_KEYS)), full((1, LANES))] + [HBM_SPEC] * n_in,
        out_specs=[pl.BlockSpec((GLA_ROWS, GLA_WIDTH), lambda i: (i, 0)),
                   pl.BlockSpec((GLA_ROWS, GLA_WIDTH), lambda i: (i, 0)),
                   pl.BlockSpec((cps, 2, LANES, 2 * LANES), lambda i: (i, 0, 0, 0))] + [HBM_SPEC] * n_out,
        out_shape=[jax.ShapeDtypeStruct((t, GLA_WIDTH), BF16), jax.ShapeDtypeStruct((t, GLA_WIDTH), F32),
                   jax.ShapeDtypeStruct((t // CHUNK, 2, LANES, 2 * LANES), F32)]
        + rider.out_shapes,
        input_output_aliases={8 + i: 3 + o for i, o in rider.aliases},
        scratch_shapes=[pltpu.VMEM((2, LANES, 2 * LANES), F32)] + [pltpu.SemaphoreType.DMA((s,)) for s in rider.sems],
        compiler_params=_cparams("arbitrary"),
    )(proj, proj, proj, proj, proj, gate_up_pad, gate_bias, gnorm, *rider.inputs)


def _gla_bwd(proj, dcat, oraw, states, gate_up_pad, gate_bias, gnorm, t, rider):
    nsteps = t // GLA_ROWS
    cps = GLA_ROWS // CHUNK
    wout = 2 * GLA_KEYS + 2 * GLA_WIDTH + LANES
    n_in, n_out = len(rider.inputs), len(rider.out_shapes)

    def body(q_ref, k_ref, v_ref, gg_ref, ga_ref, do_ref, oraw_ref, st_ref, gup_ref, gbias_ref, gn_ref, *rest):
        d_ref, dgup_ref, dgbias_ref, dgn_ref = rest[n_in:n_in + 4]
        ds_scr = rest[n_in + 4 + n_out]
        i = pl.program_id(0)
        _ride(rider, rest[:n_in], rest[n_in + 4:n_in + 4 + n_out], rest[n_in + 5 + n_out:], i == 0, i == nsteps - 1)

        @pl.when(i == 0)
        def _():
            ds_scr[...] = jnp.zeros_like(ds_scr)
            dgup_ref[...] = jnp.zeros_like(dgup_ref)
            dgbias_ref[...] = jnp.zeros_like(dgbias_ref)
            dgn_ref[...] = jnp.zeros_like(dgn_ref)

        causal, fwd_stack, bwd_stack, rowid, lane, blockdiag = _gla_consts()
        pre, log_a = _gla_gates(ga_ref, gup_ref, gbias_ref)
        gn = gn_ref[...]
        dgn = jnp.zeros((1, LANES), F32)
        dgs = []
        pre_p = []
        for p in range(2):
            kl = slice(p * LANES, (p + 1) * LANES)
            vl = slice(p * 2 * LANES, (p + 1) * 2 * LANES)
            tm = _gla_terms(log_a[:, kl], q_ref[:, kl], k_ref[:, kl], fwd_stack)
            v2b = v_ref[:, vl].astype(BF16)
            dos = []
            for hh in range(2):
                gl = slice(p * 2 * LANES + hh * LANES, p * 2 * LANES + (hh + 1) * LANES)
                oh = oraw_ref[:, gl]
                rinv = lax.rsqrt(jnp.mean(oh * oh, axis=1, keepdims=True) + RMS_EPS)
                on = oh * rinv
                gg = gg_ref[:, gl]
                sg = _sigmoid(gg)
                sil = gg * sg
                dgo = do_ref[:, gl]
                d_ref[:, 2 * GLA_KEYS + GLA_WIDTH + gl.start:2 * GLA_KEYS + GLA_WIDTH + gl.stop] = (
                    dgo * on * gn * (sg * (1.0 + gg * (1.0 - sg)))).astype(BF16)
                dgn = dgn + jnp.sum(dgo * sil * on, axis=0, keepdims=True)
                don = dgo * sil * gn
                dos.append(rinv * (don - on * jnp.mean(don * on, axis=1, keepdims=True)))
            do2b = jnp.concatenate(dos, axis=1).astype(BF16)
            qib = tm["qi"].astype(BF16)
            kib = tm["ki"].astype(BF16)
            kdb = tm["kd"].astype(BF16)
            qbb = tm["qb"].astype(BF16)
            dqi = jnp.zeros((GLA_ROWS, LANES), F32)
            dki = jnp.zeros((GLA_ROWS, LANES), F32)
            dvs = []
            for hh in range(2):
                hm = (lane // 64) == hh
                hl = slice(hh * LANES, (hh + 1) * LANES)
                a = jnp.where(causal, _dot_nt(jnp.where(hm, tm["qi"], 0.0).astype(BF16), kib), 0.0).astype(BF16)
                da = jnp.where(causal, _dot_nt(do2b[:, hl], v2b[:, hl]), 0.0).astype(BF16)
                dvs.append(_dot_tn(a, do2b[:, hl]))
                dqi = dqi + jnp.where(hm, _dot(da, kib), 0.0)
                dki = dki + jnp.where(hm, _dot_tn(da, qib), 0.0)
            pre_p.append(dict(tm=tm, v2b=v2b, do2b=do2b, kdb=kdb, qbb=qbb, dqi=dqi, dki=dki, dvs=dvs))
        dstate = [ds_scr[0], ds_scr[1]]
        rec = [dict(dv_st=[None] * cps, dqb=[None] * cps, dkd=[None] * cps, dd=[None] * cps) for _ in range(2)]
        for cc in reversed(range(cps)):
            rows = slice(cc * CHUNK, (cc + 1) * CHUNK)
            for p in range(2):
                w, ds2 = pre_p[p], dstate[p]
                s_prev = st_ref[cc, p]
                ds2b = ds2.astype(BF16)
                rec[p]["dv_st"][cc] = _dot(w["kdb"][rows], ds2b)
                rec[p]["dqb"][cc] = _dot_nt(w["do2b"][rows], s_prev.astype(BF16))
                rec[p]["dkd"][cc] = _dot_nt(w["v2b"][rows], ds2b)
                decay_row = w["tm"]["decay"][cc * CHUNK:cc * CHUNK + 1]
                ddecay_col = jnp.sum(ds2 * s_prev, axis=1, keepdims=True)
                ddecay_row = jnp.transpose(jnp.broadcast_to(ddecay_col, (LANES, LANES)))[0:1, :]
                rec[p]["dd"][cc] = jnp.broadcast_to(ddecay_row * decay_row, (CHUNK, LANES))
                dcol = _row_to_col(decay_row)
                dstate[p] = (jnp.where(blockdiag, _dot_tn(w["qbb"][rows], w["do2b"][rows]), 0.0)
                             + ds2 * jnp.concatenate([dcol, dcol], axis=1))
        for p in range(2):
            ds_scr[p] = dstate[p]
            tm, dqi, dki = pre_p[p]["tm"], pre_p[p]["dqi"], pre_p[p]["dki"]
            dv2 = jnp.concatenate(pre_p[p]["dvs"], axis=1) + jnp.concatenate(rec[p]["dv_st"], axis=0)
            dqb = jnp.concatenate(rec[p]["dqb"], axis=0)
            dkd = jnp.concatenate(rec[p]["dkd"], axis=0)
            dd = rec[p]["dd"]
            dqs = dqi * tm["e_q"] + dqb * tm["e_b"]
            dk = dki * tm["e_k"] + dkd * tm["e_d"]
            t_qi = dqi * tm["qi"]
            t_ki = dki * tm["ki"]
            t_kd = dkd * tm["kd"]
            db = t_qi - t_ki + dqb * tm["qb"] - t_kd
            to_mid = t_ki - t_qi
            to_last = t_kd + jnp.where(rowid % CHUNK == CHUNK - 1, jnp.concatenate(dd, axis=0), 0.0)
            dgs.append(_dot3(bwd_stack, jnp.concatenate([db, to_mid, to_last], axis=0)))
            d_ref[:, p * LANES:(p + 1) * LANES] = (dqs * GLA_SCALE).astype(BF16)
            d_ref[:, GLA_KEYS + p * LANES:GLA_KEYS + (p + 1) * LANES] = dk.astype(BF16)
            d_ref[:, 2 * GLA_KEYS + p * 2 * LANES:2 * GLA_KEYS + (p + 1) * 2 * LANES] = dv2.astype(BF16)
        dlog_a = jnp.concatenate(dgs, axis=1)
        dpre = dlog_a * (1.0 / GLA_TAU) * _sigmoid(-pre)
        dpb = dpre.astype(BF16)
        dgn_ref[...] += dgn
        dgbias_ref[...] += jnp.sum(dpre, axis=0, keepdims=True)
        dgup_ref[...] += _dot_tn(ga_ref[...].astype(BF16), dpb)
        d_ref[:, 2 * GLA_KEYS + 2 * GLA_WIDTH:] = _dot_nt(dpb, gup_ref[...].astype(BF16)).astype(BF16)

    rev = lambda i: nsteps - 1 - i
    cb = lambda w, idx: pl.BlockSpec((GLA_ROWS, w), lambda i: (rev(i), idx))
    full = lambda shp: pl.BlockSpec(shp, lambda i: tuple(0 for _ in shp))
    return pl.pallas_call(
        body, name="gla_bwd", grid=(nsteps,),
        in_specs=[cb(256, 6), cb(256, 7), cb(512, 4), cb(512, 5), cb(128, 24), cb(512, 1), cb(512, 0),
                  pl.BlockSpec((cps, 2, LANES, 2 * LANES), lambda i: (rev(i), 0, 0, 0)),
                  full((LANES, GLA_KEYS)), full((1, GLA_KEYS)), full((1, LANES))] + [HBM_SPEC] * n_in,
        out_specs=[pl.BlockSpec((GLA_ROWS, wout), lambda i: (rev(i), 0)),
                   full((LANES, GLA_KEYS)), full((1, GLA_KEYS)), full((1, LANES))] + [HBM_SPEC] * n_out,
        out_shape=[jax.ShapeDtypeStruct((t, wout), BF16), jax.ShapeDtypeStruct((LANES, GLA_KEYS), F32),
                   jax.ShapeDtypeStruct((1, GLA_KEYS), F32), jax.ShapeDtypeStruct((1, LANES), F32)] + rider.out_shapes,
        scratch_shapes=[pltpu.VMEM((2, LANES, 2 * LANES), F32)] + [pltpu.SemaphoreType.DMA((s,)) for s in rider.sems],
        compiler_params=_cparams("arbitrary"),
    )(proj, proj, proj, proj, proj, dcat, oraw, states, gate_up_pad, gate_bias, gnorm, *rider.inputs)


def _ln_stats(r):
    mu = jnp.mean(r, axis=1, keepdims=True)
    xc = r - mu
    rstd = lax.rsqrt(jnp.mean(xc * xc, axis=1, keepdims=True) + LN_EPS)
    return xc * rstd, rstd


def _ln_bwd(dy_g, xhat, rstd):
    return rstd * (dy_g - jnp.mean(dy_g, axis=1, keepdims=True) - xhat * jnp.mean(dy_g * xhat, axis=1, keepdims=True))


def _outproj_ln1(sb_o, gla_o, x, w_out, g1, b1, t, tm=512):
    def body(sb_ref, gl_ref, x_ref, w_ref, g_ref, b_ref, xhat_ref, rstd_ref, h_ref):
        mix = _dot(sb_ref[...], w_ref[0:SB_WIDTH, :]) + _dot(gl_ref[...], w_ref[SB_WIDTH:, :])
        xhat, rstd = _ln_stats(ALPHA * x_ref[...] + mix)
        xhat_ref[...] = xhat
        rstd_ref[...] = rstd
        h_ref[...] = (xhat * g_ref[...] + b_ref[...]).astype(BF16)

    row = lambda w: pl.BlockSpec((tm, w), lambda i: (i, 0))
    full = lambda shp: pl.BlockSpec(shp, lambda i: (0, 0))
    return pl.pallas_call(
        body, name="outproj_ln1", grid=(t // tm,),
        in_specs=[row(SB_WIDTH), row(GLA_WIDTH), row(D_MODEL), full((D_MODEL, D_MODEL)), full((1, D_MODEL)), full((1, D_MODEL))],
        out_specs=[row(D_MODEL), row(1), row(D_MODEL)],
        out_shape=[jax.ShapeDtypeStruct((t, D_MODEL), F32), jax.ShapeDtypeStruct((t, 1), F32),
                   jax.ShapeDtypeStruct((t, D_MODEL), BF16)],
        compiler_params=_cparams("parallel"),
    )(sb_o, gla_o, x, w_out, g1, b1)


_INV_SQRT2 = 1.0 / math.sqrt(2.0)
_INV_SQRT2PI = 1.0 / math.sqrt(2.0 * math.pi)


def _conv3(xs, w_ref, b_ref, half):
    return (w_ref[half, 0:1, :] * pltpu.roll(xs, 2, 0) + w_ref[half, 1:2, :] * pltpu.roll(xs, 1, 0)
            + w_ref[half, 2:3, :] * xs + b_ref[half])


HALO = 16


def _conv_gelu_fwd(up3, conv_w3, conv_b3, t, tr=512, ct=256):
    nct = D_FF // ct
    hb = tr // HALO

    def body(cur_ref, prev_ref, w_ref, b_ref, gm_ref):
        i = pl.program_id(1)
        keep = (i > 0).astype(F32)
        us = []
        for half in range(2):
            xs = jnp.concatenate([prev_ref[half].astype(F32) * keep, cur_ref[half].astype(F32)], axis=0)
            us.append(_conv3(xs, w_ref, b_ref, half)[HALO:, :])
        a, c = us
        gelu = 0.5 * a * (1.0 + lax.erf(a * _INV_SQRT2))
        gm_ref[...] = (gelu * c).astype(BF16)

    return pl.pallas_call(
        body, name="conv_gelu_fwd", grid=(nct, t // tr),
        in_specs=[pl.BlockSpec((2, tr, ct), lambda j, i: (0, i, j)),
                  pl.BlockSpec((2, HALO, ct), lambda j, i: (0, jnp.maximum(i * hb - 1, 0), j)),
                  pl.BlockSpec((2, 3, ct), lambda j, i: (0, 0, j)),
                  pl.BlockSpec((2, 1, ct), lambda j, i: (0, 0, j))],
        out_specs=pl.BlockSpec((tr, ct), lambda j, i: (i, j)),
        out_shape=jax.ShapeDtypeStruct((t, D_FF), BF16),
        compiler_params=_cparams("parallel", "parallel"),
    )(up3, up3, conv_w3, conv_b3)


def _conv_gelu_bwd(up3, dgm, conv_w3, conv_b3, t, tr=512, ct=256):
    nct = D_FF // ct
    nrt = t // tr
    hb = tr // HALO
    n = tr + 2 * HALO
    lo, hi = HALO, tr + HALO

    def body(cur_ref, prev_ref, next_ref, dg_ref, dgn_ref, w_ref, b_ref, dup_ref, dcw_ref, dcb_ref):
        i = pl.program_id(1)

        @pl.when(i == 0)
        def _():
            dcw_ref[...] = jnp.zeros_like(dcw_ref)
            dcb_ref[...] = jnp.zeros_like(dcb_ref)

        keep_prev = (i > 0).astype(F32)
        keep_next = (i < nrt - 1).astype(F32)
        xs, xm1, xm2, us = [], [], [], []
        for half in range(2):
            x = jnp.concatenate([prev_ref[half].astype(F32) * keep_prev, cur_ref[half].astype(F32),
                                 next_ref[half].astype(F32)], axis=0)
            xs.append(x)
            xm1.append(pltpu.roll(x, 1, 0))
            xm2.append(pltpu.roll(x, 2, 0))
            us.append(w_ref[half, 0:1, :] * xm2[half] + w_ref[half, 1:2, :] * xm1[half]
                      + w_ref[half, 2:3, :] * x + b_ref[half])
        a, c = us
        dg = jnp.concatenate([jnp.zeros((HALO, ct), F32), dg_ref[...].astype(F32),
                              dgn_ref[...].astype(F32) * keep_next], axis=0)
        cdf = 0.5 * (1.0 + lax.erf(a * _INV_SQRT2))
        pdf = jnp.exp(-0.5 * a * a) * _INV_SQRT2PI
        dus = [dg * c * (cdf + a * pdf), dg * (a * cdf)]
        rid = lax.broadcasted_iota(jnp.int32, (8, 1), 0)
        for half in range(2):
            du = dus[half]
            dup = (w_ref[half, 2:3, :] * du + w_ref[half, 1:2, :] * pltpu.roll(du, n - 1, 0)
                   + w_ref[half, 0:1, :] * pltpu.roll(du, n - 2, 0))
            dup_ref[half] = dup[lo:hi, :].astype(BF16)
            duc = du[lo:hi, :]
            s0 = jnp.sum(duc * xm2[half][lo:hi, :], axis=0, keepdims=True)
            s1 = jnp.sum(duc * xm1[half][lo:hi, :], axis=0, keepdims=True)
            s2 = jnp.sum(duc * xs[half][lo:hi, :], axis=0, keepdims=True)
            dcw_ref[half] += jnp.where(rid == 0, s0, jnp.where(rid == 1, s1, jnp.where(rid == 2, s2, 0.0)))
            dcb_ref[half] += jnp.sum(duc, axis=0, keepdims=True)

    last = t // HALO - 1
    return pl.pallas_call(
        body, name="conv_gelu_bwd", grid=(nct, nrt),
        in_specs=[pl.BlockSpec((2, tr, ct), lambda j, i: (0, i, j)),
                  pl.BlockSpec((2, HALO, ct), lambda j, i: (0, jnp.maximum(i * hb - 1, 0), j)),
                  pl.BlockSpec((2, HALO, ct), lambda j, i: (0, jnp.minimum((i + 1) * hb, last), j)),
                  pl.BlockSpec((tr, ct), lambda j, i: (i, j)),
                  pl.BlockSpec((HALO, ct), lambda j, i: (jnp.minimum((i + 1) * hb, last), j)),
                  pl.BlockSpec((2, 3, ct), lambda j, i: (0, 0, j)),
                  pl.BlockSpec((2, 1, ct), lambda j, i: (0, 0, j))],
        out_specs=[pl.BlockSpec((2, tr, ct), lambda j, i: (0, i, j)),
                   pl.BlockSpec((2, 8, ct), lambda j, i: (0, 0, j)),
                   pl.BlockSpec((2, 1, ct), lambda j, i: (0, 0, j))],
        out_shape=[jax.ShapeDtypeStruct((2, t, D_FF), BF16), jax.ShapeDtypeStruct((2, 8, D_FF), F32),
                   jax.ShapeDtypeStruct((2, 1, D_FF), F32)],
        compiler_params=_cparams("parallel", "arbitrary"),
    )(up3, up3, up3, dgm, dgm, conv_w3, conv_b3)


def _down_ln2_loss(gm, w_down, xhat1, g1, b1, g2, b2, target, t, tm=512):
    def body(gm_ref, w_ref, xh_ref, g1_ref, b1_ref, g2_ref, b2_ref, tg_ref, dr_ref, drb_ref, loss_ref, dg_ref, db_ref):
        i = pl.program_id(0)

        @pl.when(i == 0)
        def _():
            loss_ref[...] = jnp.zeros_like(loss_ref)
            dg_ref[...] = jnp.zeros_like(dg_ref)
            db_ref[...] = jnp.zeros_like(db_ref)

        h = xh_ref[...] * g1_ref[...] + b1_ref[...]
        xhat, rstd = _ln_stats(ALPHA * h + _dot(gm_ref[...], w_ref[...]))
        err = xhat * g2_ref[...] + b2_ref[...] - tg_ref[...]
        loss_ref[...] += 0.5 * jnp.sum(jnp.sum(err * err, axis=1, keepdims=True), axis=0, keepdims=True) / D_MODEL
        dy = err * (1.0 / D_MODEL)
        dg_ref[...] += jnp.sum(dy * xhat, axis=0, keepdims=True)
        db_ref[...] += jnp.sum(dy, axis=0, keepdims=True)
        dr = _ln_bwd(dy * g2_ref[...], xhat, rstd)
        dr_ref[...] = dr
        drb_ref[...] = dr.astype(BF16)

    row = lambda w: pl.BlockSpec((tm, w), lambda i: (i, 0))
    full = lambda shp: pl.BlockSpec(shp, lambda i: (0, 0))
    vec = full((1, D_MODEL))
    return pl.pallas_call(
        body, name="down_ln2_loss", grid=(t // tm,),
        in_specs=[row(D_FF), full((D_FF, D_MODEL)), row(D_MODEL), vec, vec, vec, vec, row(D_MODEL)],
        out_specs=[row(D_MODEL), row(D_MODEL), full((1, 1)), vec, vec],
        out_shape=[jax.ShapeDtypeStruct((t, D_MODEL), F32), jax.ShapeDtypeStruct((t, D_MODEL), BF16),
                   jax.ShapeDtypeStruct((1, 1), F32),
                   jax.ShapeDtypeStruct((1, D_MODEL), F32), jax.ShapeDtypeStruct((1, D_MODEL), F32)],
        compiler_params=_cparams("arbitrary"),
    )(gm, w_down, xhat1, g1, b1, g2, b2, target)


def _dh_ln1_bwd(dup3, w_up4, dr2, xhat1, rstd1, g1, t, tm=256):
    ws = 2 * D_FF // 4

    def body(a_ref, w_ref, dr2_ref, xh_ref, rs_ref, g_ref, dr1_ref, dg_ref, db_ref):
        i = pl.program_id(0)

        @pl.when(i == 0)
        def _():
            dg_ref[...] = jnp.zeros_like(dg_ref)
            db_ref[...] = jnp.zeros_like(db_ref)

        dh = ALPHA * dr2_ref[...]
        for s in range(4):
            dh = dh + _dot_nt(a_ref[s // 2, :, (s % 2) * ws:(s % 2 + 1) * ws], w_ref[s])
        xhat = xh_ref[...]
        dg_ref[...] += jnp.sum(dh * xhat, axis=0, keepdims=True)
        db_ref[...] += jnp.sum(dh, axis=0, keepdims=True)
        dr1_ref[...] = _ln_bwd(dh * g_ref[...], xhat, rs_ref[...])

    row = lambda w: pl.BlockSpec((tm, w), lambda i: (i, 0))
    vec = pl.BlockSpec((1, D_MODEL), lambda i: (0, 0))
    return pl.pallas_call(
        body, name="dh_ln1_bwd", grid=(t // tm,),
        in_specs=[pl.BlockSpec((2, tm, D_FF), lambda i: (0, i, 0)),
                  pl.BlockSpec((4, D_MODEL, ws), lambda i: (0, 0, 0)),
                  row(D_MODEL), row(D_MODEL), row(1), vec],
        out_specs=[row(D_MODEL), vec, vec],
        out_shape=[jax.ShapeDtypeStruct((t, D_MODEL), F32), jax.ShapeDtypeStruct((1, D_MODEL), F32),
                   jax.ShapeDtypeStruct((1, D_MODEL), F32)],
        compiler_params=_cparams("arbitrary"),
    )(dup3, w_up4, dr2, xhat1, rstd1, g1)


def _adamw(w, g, m, v, name):
    rows, cols = w.shape
    tr = rows
    for cand in (256, 128, 64, 32, 16, 8):
        if rows % cand == 0 and rows > cand:
            tr = cand
            break
    c1 = 1.0 / (1.0 - ADAM_B1 ** ADAM_STEP)
    c2 = 1.0 / (1.0 - ADAM_B2 ** ADAM_STEP)

    def body(w_ref, g_ref, m_ref, v_ref, d_ref, nm_ref, nv_ref):
        gv = g_ref[...]
        nm = ADAM_B1 * m_ref[...] + (1.0 - ADAM_B1) * gv
        nv = ADAM_B2 * v_ref[...] + (1.0 - ADAM_B2) * (gv * gv)
        d_ref[...] = -ADAM_LR * ((nm * c1) / (jnp.sqrt(nv * c2) + ADAM_EPS) + ADAM_WD * w_ref[...])
        nm_ref[...] = nm
        nv_ref[...] = nv

    spec = pl.BlockSpec((tr, cols), lambda i: (i, 0))
    out = jax.ShapeDtypeStruct((rows, cols), F32)
    return pl.pallas_call(
        body, name=name, grid=(rows // tr,), in_specs=[spec] * 4, out_specs=[spec] * 3, out_shape=[out] * 3,
        compiler_params=_cparams("parallel"),
    )(w, g, m, v)


def _local_step(x, target, w_in_p, late_shards, gate_up_pad, gate_bias, gnorm, ln1_g, ln1_b, conv_w3, conv_b3,
                ln2_g, ln2_b, c_arr, kc_arr):
    t = x.shape[0]
    tq = min(t, 1024)
    s_up, s_out, s_down = late_shards
    sh_up, sh_out, sh_down = LATE_SHAPES
    proj, out_partly = _mm(x, w_in_p, m=t, n=IN_PAD, k=D_MODEL, tm=512, tn=IN_PAD, tk=D_MODEL, name="proj",
                           rider=_gather_rider([s_out], [sh_out]))
    sb_o, sb_mass, up_partly = _sb_fwd(proj, t, _gather_rider([s_up], [sh_up]))
    gla_o, oraw, states, w_up4, w_out4, down_partly = _gla_fwd(
        proj, gate_up_pad, gate_bias, gnorm, t,
        _join(_forward_rider([up_partly, out_partly], [sh_up, sh_out]), _gather_rider([s_down], [sh_down])))
    w_out = w_out4.reshape(D_MODEL, D_MODEL)
    xhat1, rstd1, h_bf = _outproj_ln1(sb_o, gla_o, x, w_out, ln1_g, ln1_b, t)
    up3, w_down4 = _mm(h_bf, w_up4, m=t, n=2 * D_FF, k=D_MODEL, tm=tq, tn=W_UP_S, tk=D_MODEL, name="up",
                       b_spec=pl.BlockSpec((None, D_MODEL, W_UP_S), lambda i, j, kk: (j, 0, 0)),
                       o_spec=pl.BlockSpec((None, tq, W_UP_S), lambda i, j, kk: (j // 2, i, j % 2)),
                       out_shape=jax.ShapeDtypeStruct((2, t, D_FF), BF16), out_dtype=BF16,
                       rider=_forward_rider([down_partly], [sh_down]))
    w_down = w_down4.reshape(D_FF, D_MODEL)
    gm = _conv_gelu_fwd(up3, conv_w3, conv_b3, t, tr=t)
    dr2, dr2b, loss, dln2_g, dln2_b = _down_ln2_loss(gm, w_down, xhat1, ln1_g, ln1_b, ln2_g, ln2_b, target, t)
    dgm = _mm(dr2b, w_down, m=t, n=D_FF, k=D_MODEL, tm=tq, tn=W_UP_S, tk=D_MODEL, tb=True, out_dtype=BF16, name="dgm")
    dw_down = _mm(gm, dr2b, m=D_FF, n=D_MODEL, k=t, tm=W_UP_S, tn=D_MODEL, tk=min(t, 2048), ta=True, name="dw_down")
    dup3, dcw, dcb = _conv_gelu_bwd(up3, dgm, conv_w3, conv_b3, t, tr=tq)
    dr1, dln1_g, dln1_b = _dh_ln1_bwd(dup3, w_up4, dr2, xhat1, rstd1, ln1_g, t)
    dw_up4 = _mm(h_bf, dup3, m=D_MODEL, n=2 * D_FF, k=t, tm=512, tn=W_UP_S, tk=t, ta=True, name="dw_up",
                 b_spec=pl.BlockSpec((None, t, W_UP_S), lambda i, j, kk: (j // 2, kk, j % 2)),
                 o_spec=pl.BlockSpec((None, 512, W_UP_S), lambda i, j, kk: (j, i, 0)),
                 out_shape=jax.ShapeDtypeStruct((4, D_MODEL, W_UP_S), F32))
    dw_out = _dw_out(sb_o, gla_o, dr1, t, tq)
    gs = [dw_up4, dw_out.reshape(4, W_OUT_S, D_MODEL), dw_down.reshape(4, W_DOWN_S, D_MODEL)]
    dcat = _mm(dr1, w_out, m=t, n=D_MODEL, k=D_MODEL, tm=tq, tn=512, tk=D_MODEL, tb=True, name="dcat")
    dgla, dgup_pad, dgbias, dgnorm, *from_sib = _gla_bwd(proj, dcat, oraw, states, gate_up_pad, gate_bias, gnorm, t,
                                                         _sibling_rider(gs, LATE_SHAPES))
    ps = [_add_sibling(gs[m], from_sib[m], c_arr, LATE_ADD_ROWS[m], "add_sibling_late_%d" % m) for m in range(3)]
    small = dict(
        gate_up=dgup_pad[:GATE_RANK], gate_bias=dgbias, gla_norm_g=dgnorm, ln1_g=dln1_g, ln1_b=dln1_b,
        conv_w=jnp.concatenate([dcw[0, :3], dcw[1, :3]], axis=1), conv_b=jnp.concatenate([dcb[0], dcb[1]], axis=1),
        ln2_g=dln2_g, ln2_b=dln2_b, loss=loss)
    dsq, dsk, dsv, *others, vecs = _sb_bwd(proj, dcat, sb_mass, t,
                                           _chips_rider(ps, _pack_vec(small, SMALL_GRADS, GRAD_VEC_ROWS)))
    late_sums = [_add_chips(ps[m], others[m], kc_arr, LATE_ADD_ROWS[m], "add_chips_late_%d" % m) for m in range(3)]
    dproj = [dsq, dsk, dsv, dgla]
    dw_in_p = _dw_in(x, dproj, t, tk=tq)
    g_in = dw_in_p[None]
    from_sib_in, = _run(_sibling_rider([g_in], [(D_MODEL, IN_PAD)]), "exchange_sibling_w_in")
    half_in = _add_sibling(g_in, from_sib_in, c_arr, ADD_ROWS[0], "add_sibling_w_in")[0]
    p_in = jnp.stack([half_in[:, k * W_IN_S:(k + 1) * W_IN_S] for k in range(4)], axis=0)
    dx, others_in = _dx(dproj, w_in_p, dr1, t, _chips_rider([p_in]))
    sum_in = _add_chips(p_in, others_in, kc_arr, ADD_ROWS[0], "add_chips_w_in")
    return dx, [sum_in] + late_sums, vecs


W_IN_S, W_UP_S, W_OUT_S, W_DOWN_S = IN_WIDTH // 4, 2 * D_FF // 4, D_MODEL // 4, D_FF // 4
SHARD_SHAPES = ((D_MODEL, W_IN_S), (D_MODEL, W_UP_S), (W_OUT_S, D_MODEL), (W_DOWN_S, D_MODEL))
ADD_ROWS = (256, 256, 128, 176)
LATE_SHAPES, LATE_ADD_ROWS = SHARD_SHAPES[1:], ADD_ROWS[1:]
SMALL_ROWS = 8
VEC_SIZES = (("gate_bias", GLA_KEYS), ("gla_norm_g", LANES), ("ln1_g", D_MODEL), ("ln1_b", D_MODEL),
             ("conv_b", 2 * D_FF), ("ln2_g", D_MODEL), ("ln2_b", D_MODEL))
SMALL_GRADS = VEC_SIZES + (("conv_w", 3 * 2 * D_FF), ("gate_up", GATE_RANK * GLA_KEYS), ("loss", 1))


def _rows(a):
    flat = a.reshape(-1)
    pad = (-flat.shape[0]) % D_MODEL
    if pad:
        flat = jnp.pad(flat, (0, pad))
    return flat.reshape(-1, D_MODEL)


def _pad_rows(a, rows):
    return jnp.pad(a, ((0, rows - a.shape[0]), (0, 0)))


def _pack_vec(d, sizes, rows):
    flat = jnp.concatenate([d[n].reshape(-1) for n, _ in sizes])
    return _pad_rows(_rows(flat), rows)


def _unpack_vec(v, sizes):
    flat = v.reshape(-1)
    out, o = {}, 0
    for n, size in sizes:
        out[n] = flat[o:o + size].reshape(1, size)
        o += size
    return out


VEC_ROWS = 16
GRAD_VEC_ROWS = 32


HBM_SPEC = pl.BlockSpec(memory_space=pltpu.HBM)


def _position():
    x, y, c = lax.axis_index("x"), lax.axis_index("y"), lax.axis_index("c")
    chips = [(1 - x, y), (x, 1 - y), (1 - x, 1 - y)]
    return x, y, c, chips


def _remote(src, dst, send_sems, recv_sems, k, to):
    return pltpu.make_async_remote_copy(src_ref=src, dst_ref=dst, send_sem=send_sems.at[k], recv_sem=recv_sems.at[k],
                                        device_id=to, device_id_type=MESH)


def _gather_ici(in_refs, out_refs, shapes, send_sems, recv_sems, local_sems):
    x, y, c, chips = _position()
    k_me = 2 * x + y
    local, sends, recvs = [], [], []
    for m, (rows, _) in enumerate(shapes):
        h = rows // 2
        local.append(pltpu.make_async_copy(in_refs[m], out_refs[m].at[k_me], local_sems.at[m]))
        for j, (cx, cy) in enumerate(chips):
            sends.append(_remote(in_refs[m].at[pl.ds(c * h, h), :], out_refs[m].at[k_me, pl.ds(c * h, h), :],
                                 send_sems, recv_sems, 3 * m + j, (cx, cy, c)))
            landed = out_refs[m].at[2 * cx + cy, pl.ds(c * h, h), :]
            recvs.append(_remote(landed, landed, send_sems, recv_sems, 3 * m + j, (x, y, c)))
    return local, sends, recvs


def _gather_d2d(src_refs, dst_refs, shapes, send_sems, recv_sems, base):
    x, y, c, chips = _position()
    sends, recvs = [], []
    for m, (rows, _) in enumerate(shapes):
        h = rows // 2
        for j, (cx, cy) in enumerate(chips):
            k = 2 * cx + cy
            sends.append(_remote(src_refs[m].at[k, pl.ds(c * h, h), :], dst_refs[m].at[k, pl.ds(c * h, h), :],
                                 send_sems, recv_sems, base + 3 * m + j, (x, y, 1 - c)))
            landed = dst_refs[m].at[k, pl.ds((1 - c) * h, h), :]
            recvs.append(_remote(landed, landed, send_sems, recv_sems, base + 3 * m + j, (x, y, c)))
    return sends, recvs


def _gather_weights(shards, small, shapes):
    nm = len(shards)
    n_ici = 3 * nm

    def body(*refs):
        in_refs, small_ref = refs[:nm], refs[nm]
        out_refs, osm_ref = refs[nm + 1:2 * nm + 1], refs[2 * nm + 1]
        send_sems, recv_sems, local_sems = refs[2 * nm + 2:]
        x, y, c, chips = _position()
        k_me = 2 * x + y
        local, sends, recvs = _gather_ici(in_refs, out_refs, shapes, send_sems, recv_sems, local_sems)
        local.append(pltpu.make_async_copy(small_ref, osm_ref.at[k_me], local_sems.at[nm]))
        for j, (cx, cy) in enumerate(chips):
            sends.append(_remote(small_ref, osm_ref.at[k_me], send_sems, recv_sems, n_ici + j, (cx, cy, c)))
        for cp in local + sends:
            cp.start()
        fsends, frecvs = _gather_d2d(out_refs, out_refs, shapes, send_sems, recv_sems, n_ici + 3)
        for landed, onward in zip(recvs, fsends):
            landed.wait_recv()
            onward.start()
        for j, (cx, cy) in enumerate(chips):
            k = 2 * cx + cy
            frecvs.append(_remote(osm_ref.at[k], osm_ref.at[k], send_sems, recv_sems, n_ici + j, (x, y, c)))
        for cp in frecvs:
            cp.wait_recv()
        for cp in sends + fsends:
            cp.wait_send()
        for cp in local:
            cp.wait()

    n_sems = 2 * n_ici + 3
    return pl.pallas_call(
        body, name="gather_weights", in_specs=[HBM_SPEC] * (nm + 1), out_specs=[HBM_SPEC] * (nm + 1),
        out_shape=[jax.ShapeDtypeStruct((4,) + s, BF16) for s in shapes]
        + [jax.ShapeDtypeStruct((4, SMALL_ROWS, D_MODEL), F32)],
        scratch_shapes=[pltpu.SemaphoreType.DMA((n_sems,)), pltpu.SemaphoreType.DMA((n_sems,)),
                        pltpu.SemaphoreType.DMA((nm + 1,))],
    )(*shards, small)


def _gather_rider(shards, shapes):
    n = len(shards)
    return _Rider(shards, [jax.ShapeDtypeStruct((4,) + s, BF16) for s in shapes], (3 * n, 3 * n, n),
                  lambda ins, outs, sems: _gather_ici(ins, outs, shapes, *sems))


def _forward_rider(gathered, shapes):
    n = len(gathered)
    return _Rider(gathered, [jax.ShapeDtypeStruct(a.shape, a.dtype) for a in gathered], (3 * n, 3 * n),
                  lambda ins, outs, sems: ([],) + _gather_d2d(ins, outs, shapes, sems[0], sems[1], 0),
                  aliases=[(m, m) for m in range(n)])


def _sibling_rider(gs, shapes):
    def copies(ins, outs, sems):
        x, y, c, _ = _position()
        both = []
        for m, (rows, _) in enumerate(shapes):
            h = rows // 2
            for k in range(gs[m].shape[0]):
                both.append(_remote(ins[m].at[k, pl.ds((1 - c) * h, h), :], outs[m].at[k], sems[0], sems[1],
                                    4 * m + k, (x, y, 1 - c)))
        return [], both, both

    return _Rider(gs, [jax.ShapeDtypeStruct((g.shape[0], r // 2, cl), F32) for g, (r, cl) in zip(gs, shapes)],
                  (4 * len(gs), 4 * len(gs)), copies)


def _add_sibling(g, r, c_arr, tr, name):
    nblk, rows, cols = g.shape
    nb = rows // 2 // tr

    def body(c_ref, g_ref, r_ref, o_ref):
        o_ref[...] = (g_ref[...] + r_ref[...]).astype(BF16)

    spec = pl.BlockSpec((None, tr, cols), lambda k, i, c: (k, i, 0))
    return pl.pallas_call(
        body, name=name,
        grid_spec=pltpu.PrefetchScalarGridSpec(
            num_scalar_prefetch=1, grid=(nblk, nb),
            in_specs=[pl.BlockSpec((None, tr, cols), lambda k, i, c: (k, c[0] * nb + i, 0)), spec], out_specs=spec),
        out_shape=jax.ShapeDtypeStruct((nblk, rows // 2, cols), BF16), compiler_params=_cparams("parallel", "parallel"),
    )(c_arr, g, r)


def _reduce_ici(p_refs, r_refs, send_sems, recv_sems):
    x, y, c, chips = _position()
    sends, recvs = [], []
    for m in range(len(p_refs)):
        for j, (cx, cy) in enumerate(chips):
            sends.append(_remote(p_refs[m].at[2 * cx + cy], r_refs[m].at[j], send_sems, recv_sems, 3 * m + j, (cx, cy, c)))
            recvs.append(_remote(r_refs[m].at[j], r_refs[m].at[j], send_sems, recv_sems, 3 * m + j, (x, y, c)))
    return sends, recvs


def _chips_rider(ps, vec=None):
    nm = len(ps)
    n_ici = 3 * nm

    def copies(ins, outs, sems):
        sends, recvs = _reduce_ici(ins[:nm], outs[:nm], sems[0], sems[1])
        if vec is None:
            return [], sends, recvs
        x, y, c, _ = _position()
        my_id = 4 * x + 2 * y + c
        vec_ref, vrecv_ref = ins[nm], outs[nm]
        local = [pltpu.make_async_copy(vec_ref, vrecv_ref.at[my_id], sems[2].at[0])]
        for r in range(1, 8):
            peer = (1 - x if r & 4 else x, 1 - y if r & 2 else y, 1 - c if r & 1 else c)
            sends.append(_remote(vec_ref, vrecv_ref.at[my_id], sems[0], sems[1], n_ici + r - 1, peer))
            recvs.append(_remote(vec_ref, vrecv_ref.at[0], sems[0], sems[1], n_ici + r - 1, (x, y, c)))
        return local, sends, recvs

    out_shapes = [jax.ShapeDtypeStruct((3,) + p.shape[1:], p.dtype) for p in ps]
    if vec is None:
        return _Rider(ps, out_shapes, (n_ici, n_ici), copies)
    return _Rider(list(ps) + [vec], out_shapes + [jax.ShapeDtypeStruct((8, GRAD_VEC_ROWS, D_MODEL), F32)],
                  (n_ici + 7, n_ici + 7, 1), copies)


def _add_chips(p, r, kc_arr, tr, name):
    _, h, cols = p.shape
    nb = h // tr

    def body(kc_ref, p_ref, r0_ref, r1_ref, r2_ref, o_ref):
        o_ref[...] = ((p_ref[...].astype(F32) + r0_ref[...].astype(F32)) + r1_ref[...].astype(F32)) + r2_ref[...].astype(F32)

    rspec = lambda j: pl.BlockSpec((None, tr, cols), lambda i, kc: (j, i, 0))
    return pl.pallas_call(
        body, name=name,
        grid_spec=pltpu.PrefetchScalarGridSpec(
            num_scalar_prefetch=1, grid=(nb,),
            in_specs=[pl.BlockSpec((None, tr, cols), lambda i, kc: (kc[0], i, 0)), rspec(0), rspec(1), rspec(2)],
            out_specs=pl.BlockSpec((tr, cols), lambda i, kc: (kc[1] * nb + i, 0))),
        out_shape=jax.ShapeDtypeStruct((2 * h, cols), F32), compiler_params=_cparams("parallel"),
    )(kc_arr, p, r, r, r)


def _reunite_sibling(fs, shapes):
    n_chunks = 2
    nm = len(fs)

    def body(*refs):
        in_refs, out_refs = refs[:nm], refs[nm:2 * nm]
        send_sems, recv_sems = refs[2 * nm:]
        x, y, c, _ = _position()
        sends, recvs = [], []
        for m in range(nm):
            ch = shapes[m][0] // 2 // n_chunks
            for q in range(n_chunks):
                mine = pl.ds((c * n_chunks + q) * ch, ch)
                theirs = pl.ds(((1 - c) * n_chunks + q) * ch, ch)
                s = n_chunks * m + q
                sends.append(_remote(in_refs[m].at[mine, :], out_refs[m].at[mine, :], send_sems, recv_sems, s, (x, y, 1 - c)))
                recvs.append(_remote(in_refs[m].at[theirs, :], out_refs[m].at[theirs, :], send_sems, recv_sems, s, (x, y, c)))
        for cp in sends:
            cp.start()
        for cp in recvs:
            cp.wait_recv()
        for cp in sends:
            cp.wait_send()

    n_sems = n_chunks * nm
    return pl.pallas_call(
        body, name="reunite_sibling", in_specs=[HBM_SPEC] * nm, out_specs=[HBM_SPEC] * nm,
        out_shape=[jax.ShapeDtypeStruct(s, F32) for s in shapes],
        input_output_aliases={m: m for m in range(nm)},
        scratch_shapes=[pltpu.SemaphoreType.DMA((n_sems,)), pltpu.SemaphoreType.DMA((n_sems,))],
    )(*fs)


def _sum_vecs(v):
    def body(v_ref, o_ref):
        acc = v_ref[0]
        for d in range(1, 8):
            acc = acc + v_ref[d]
        o_ref[...] = acc

    return pl.pallas_call(body, name="sum_vecs", out_shape=jax.ShapeDtypeStruct(v.shape[1:], F32))(v)


def kernel(x, w_in, gate_up, gate_bias, gla_norm_g, w_out, ln1_g, ln1_b, w_up, conv_w, conv_b, w_down, ln2_g, ln2_b, loss_target, m_w_in, m_gate_up, m_gate_bias, m_gla_norm_g, m_w_out, m_ln1_g, m_ln1_b, m_w_up, m_conv_w, m_conv_b, m_w_down, m_ln2_g, m_ln2_b, v_w_in, v_gate_up, v_gate_bias, v_gla_norm_g, v_w_out, v_ln1_g, v_ln1_b, v_w_up, v_conv_w, v_conv_b, v_w_down, v_ln2_g, v_ln2_b):
    xi, yi, ci = lax.axis_index("x"), lax.axis_index("y"), lax.axis_index("c")
    k_me = 2 * xi + yi
    c_arr = ci.astype(jnp.int32).reshape(1)
    kc_arr = jnp.stack([k_me, ci]).astype(jnp.int32)
    small = _pad_rows(jnp.concatenate([_rows(conv_w[0]), _rows(gate_up[0])], axis=0), SMALL_ROWS)
    w_in4, gsmall = _gather_weights([w_in[0].astype(BF16)], small, SHARD_SHAPES[:1])
    late_shards = [w_up[0].astype(BF16), w_out[0].astype(BF16), w_down[0].astype(BF16)]
    w_in_p = jnp.pad(jnp.concatenate([w_in4[k] for k in range(4)], axis=1), ((0, 0), (0, IN_PAD - IN_WIDTH)))
    conv_w_f = jnp.concatenate([gsmall[k, :5].reshape(-1)[:3 * W_UP_S].reshape(3, W_UP_S) for k in range(4)], axis=1)
    gate_up_f = jnp.concatenate([gsmall[k, 5].reshape(GATE_RANK, GLA_KEYS // 4) for k in range(4)], axis=1)
    conv_w3 = jnp.transpose(conv_w_f.reshape(3, 2, D_FF), (1, 0, 2))
    conv_b3 = conv_b.reshape(2, 1, D_FF)
    gate_up_pad = jnp.pad(gate_up_f, ((0, LANES - GATE_RANK), (0, 0)))

    dx, sums, vecs = _local_step(
        x[0], loss_target[0], w_in_p, late_shards, gate_up_pad, gate_bias, gla_norm_g, ln1_g, ln1_b, conv_w3, conv_b3,
        ln2_g, ln2_b, c_arr, kc_arr)
    g_w_in, g_w_up, g_w_out, g_w_down = _reunite_sibling(sums, SHARD_SHAPES)
    gsmall_sum = _unpack_vec(_sum_vecs(vecs), SMALL_GRADS)
    g_conv_w = lax.dynamic_slice_in_dim(gsmall_sum["conv_w"].reshape(3, 2 * D_FF), k_me * W_UP_S, W_UP_S, axis=1)
    g_gate_up = lax.dynamic_slice_in_dim(gsmall_sum["gate_up"].reshape(GATE_RANK, GLA_KEYS), k_me * (GLA_KEYS // 4),
                                         GLA_KEYS // 4, axis=1)
    gv = gsmall_sum
    loss = gv["loss"][0, 0]
    gvec = _pack_vec(gv, VEC_SIZES, VEC_ROWS)

    grads = dict(w_in=g_w_in[None], gate_up=g_gate_up[None], gate_bias=gv["gate_bias"], gla_norm_g=gv["gla_norm_g"],
                 w_out=g_w_out[None], ln1_g=gv["ln1_g"], ln1_b=gv["ln1_b"], w_up=g_w_up[None], conv_w=g_conv_w[None],
                 conv_b=gv["conv_b"], w_down=g_w_down[None], ln2_g=gv["ln2_g"], ln2_b=gv["ln2_b"])
    weights = dict(w_in=w_in, gate_up=gate_up, gate_bias=gate_bias, gla_norm_g=gla_norm_g, w_out=w_out, ln1_g=ln1_g,
                   ln1_b=ln1_b, w_up=w_up, conv_w=conv_w, conv_b=conv_b, w_down=w_down, ln2_g=ln2_g, ln2_b=ln2_b)
    ms = dict(w_in=m_w_in, gate_up=m_gate_up, gate_bias=m_gate_bias, gla_norm_g=m_gla_norm_g, w_out=m_w_out, ln1_g=m_ln1_g,
              ln1_b=m_ln1_b, w_up=m_w_up, conv_w=m_conv_w, conv_b=m_conv_b, w_down=m_w_down, ln2_g=m_ln2_g, ln2_b=m_ln2_b)
    vs = dict(w_in=v_w_in, gate_up=v_gate_up, gate_bias=v_gate_bias, gla_norm_g=v_gla_norm_g, w_out=v_w_out, ln1_g=v_ln1_g,
              ln1_b=v_ln1_b, w_up=v_w_up, conv_w=v_conv_w, conv_b=v_conv_b, w_down=v_w_down, ln2_g=v_ln2_g, ln2_b=v_ln2_b)
    names = ["w_in", "gate_up", "gate_bias", "gla_norm_g", "w_out", "ln1_g", "ln1_b", "w_up", "conv_w", "conv_b", "w_down",
             "ln2_g", "ln2_b"]
    delta, new_m, new_v = {}, {}, {}
    for n in ("w_in", "gate_up", "w_out", "w_up", "conv_w", "w_down"):
        tr_ = jnp.transpose if n == "w_in" else (lambda a: a)
        d, nm, nv = _adamw(tr_(weights[n][0]), tr_(grads[n][0]), tr_(ms[n][0]), tr_(vs[n][0]), "adamw_" + n)
        delta[n], new_m[n], new_v[n] = tr_(d)[None], tr_(nm)[None], tr_(nv)[None]
    d, nm, nv = _adamw(_pack_vec(weights, VEC_SIZES, VEC_ROWS), gvec, _pack_vec(ms, VEC_SIZES, VEC_ROWS),
                       _pack_vec(vs, VEC_SIZES, VEC_ROWS), "adamw_vectors")
    for dst, src in ((delta, d), (new_m, nm), (new_v, nv)):
        dst.update(_unpack_vec(src, VEC_SIZES))
    return (loss, dx[None], *[grads[n] for n in names], *[delta[n] for n in names], *[new_m[n] for n in names],
            *[new_v[n] for n in names])
```

```python
import math

import jax
import jax.numpy as jnp
from jax import lax
from jax.experimental import pallas as pl
from jax.experimental.pallas import tpu as pltpu

F32 = jnp.float32
BF16 = jnp.bfloat16

D_MODEL = 1024
SB_WIDTH = 512
GLA_KEYS = 256
GLA_WIDTH = 512
GATE_RANK = 16
IN_WIDTH = 3088
IN_PAD = 3200
D_FF = 2816
CHUNK = 64
LN_EPS = 1e-5
RMS_EPS = 1e-6
ALPHA = 2.0 ** 0.25
GLA_TAU = 16.0
SB_SCALE = 0.125
GLA_SCALE = 0.125
LANES = 128
SB_BLK = 256
SB_CUT = -100.0
GLA_ROWS = 256
VMEM_LIMIT = 56 * 1024 * 1024

ADAM_LR, ADAM_B1, ADAM_B2, ADAM_EPS, ADAM_WD, ADAM_STEP = 0.001, 0.9, 0.999, 1e-08, 0.01, 10

MESH = pl.DeviceIdType.MESH


def _cparams(*sem):
    return pltpu.CompilerParams(dimension_semantics=sem, vmem_limit_bytes=VMEM_LIMIT)


def _dot(a, b):
    return jnp.dot(a, b, preferred_element_type=F32)


def _dot_nt(a, b):
    return lax.dot_general(a, b, (((1,), (1,)), ((), ())), preferred_element_type=F32)


def _dot_tn(a, b):
    return lax.dot_general(a, b, (((0,), (0,)), ((), ())), preferred_element_type=F32)


def _split3(x):
    hi = x.astype(BF16)
    r = x - hi.astype(F32)
    mid = r.astype(BF16)
    lo = (r - mid.astype(F32)).astype(BF16)
    return hi, mid, lo


def _softplus(z):
    return jnp.maximum(z, 0.0) + jnp.log(1.0 + jnp.exp(-jnp.abs(z)))


def _sigmoid(z):
    return 1.0 / (1.0 + jnp.exp(-z))


def _mm(a, b, *, m, n, k, tm, tn, tk, ta=False, tb=False, a_spec=None, b_spec=None, o_spec=None,
        out_shape=None, out_dtype=F32, add=None, add_scale=1.0, rider=None, name):
    nk = k // tk
    dn = (((0 if ta else 1,), (1 if tb else 0,)), ((), ()))
    n_in = len(rider.inputs) if rider else 0
    n_out = len(rider.out_shapes) if rider else 0
    n_add = int(add is not None)
    steps = (m // tm, n // tn, nk)

    def body(*refs):
        a_ref, b_ref = refs[:2]
        add_ref = refs[2] if add is not None else None
        rin = refs[2 + n_add:2 + n_add + n_in]
        o_ref = refs[2 + n_add + n_in]
        rout = refs[3 + n_add + n_in:3 + n_add + n_in + n_out]
        scratch = refs[3 + n_add + n_in + n_out:]
        if rider:
            sems = scratch[len(scratch) - len(rider.sems):]
            ids = [pl.program_id(d) for d in range(3)]
            _ride(rider, rin, rout, sems, (ids[0] == 0) & (ids[1] == 0) & (ids[2] == 0),
                  (ids[0] == steps[0] - 1) & (ids[1] == steps[1] - 1) & (ids[2] == steps[2] - 1))

        part = lax.dot_general(a_ref[...].astype(BF16), b_ref[...].astype(BF16), dn, preferred_element_type=F32)

        def finish(r):
            if add is not None:
                r = r + add_scale * add_ref[...]
            o_ref[...] = r.astype(out_dtype)

        if nk == 1:
            finish(part)
            return
        acc_ref = scratch[0]
        kk = pl.program_id(2)

        @pl.when(kk == 0)
        def _():
            acc_ref[...] = part

        @pl.when((kk > 0) & (kk < nk - 1))
        def _():
            acc_ref[...] += part

        @pl.when(kk == nk - 1)
        def _():
            finish(acc_ref[...] + part)

    if a_spec is None:
        a_spec = pl.BlockSpec((tk, tm), lambda i, j, kk: (kk, i)) if ta else pl.BlockSpec((tm, tk), lambda i, j, kk: (i, kk))
    if b_spec is None:
        b_spec = pl.BlockSpec((tn, tk), lambda i, j, kk: (j, kk)) if tb else pl.BlockSpec((tk, tn), lambda i, j, kk: (kk, j))
    if o_spec is None:
        o_spec = pl.BlockSpec((tm, tn), lambda i, j, kk: (i, j))
    if out_shape is None:
        out_shape = jax.ShapeDtypeStruct((m, n), out_dtype)
    in_specs = [a_spec, b_spec]
    args = [a, b]
    if add is not None:
        in_specs.append(pl.BlockSpec((tm, tn), lambda i, j, kk: (i, j)))
        args.append(add)
    scratch = [pltpu.VMEM((tm, tn), F32)] if nk > 1 else []
    if not rider:
        return pl.pallas_call(
            body, name=name, grid=steps, in_specs=in_specs, out_specs=o_spec, out_shape=out_shape,
            scratch_shapes=scratch, compiler_params=_cparams("parallel", "parallel", "arbitrary"),
        )(*args)
    return pl.pallas_call(
        body, name=name, grid=steps, in_specs=in_specs + [HBM_SPEC] * n_in, out_specs=[o_spec] + [HBM_SPEC] * n_out,
        out_shape=[out_shape] + list(rider.out_shapes),
        input_output_aliases={len(args) + i: 1 + o for i, o in rider.aliases},
        scratch_shapes=scratch + [pltpu.SemaphoreType.DMA((s,)) for s in rider.sems],
        compiler_params=_cparams("arbitrary", "arbitrary", "arbitrary"),
    )(*args, *rider.inputs)


def _col_offsets(pieces):
    offs, o = [], 0
    for a in pieces:
        offs.append(o)
        o += a.shape[1]
    return offs


def _dx(pieces, w_in_p, dr1, t, rider, tm=512):
    offs = _col_offsets(pieces)
    npc = len(pieces)
    n_in, n_out = len(rider.inputs), len(rider.out_shapes)
    steps = t // tm

    def body(*refs):
        p_refs, w_ref, add_ref = refs[:npc], refs[npc], refs[npc + 1]
        rin = refs[npc + 2:npc + 2 + n_in]
        o_ref = refs[npc + 2 + n_in]
        rout = refs[npc + 3 + n_in:npc + 3 + n_in + n_out]
        i = pl.program_id(0)
        _ride(rider, rin, rout, refs[npc + 3 + n_in + n_out:], i == 0, i == steps - 1)
        acc = ALPHA * add_ref[...]
        for p_ref, off in zip(p_refs, offs):
            acc = acc + _dot_nt(p_ref[...].astype(BF16), w_ref[:, off:off + p_ref.shape[1]])
        o_ref[...] = acc

    row = lambda w: pl.BlockSpec((tm, w), lambda i: (i, 0))
    return pl.pallas_call(
        body, name="dx", grid=(steps,),
        in_specs=[row(a.shape[1]) for a in pieces] + [pl.BlockSpec(w_in_p.shape, lambda i: (0, 0)), row(D_MODEL)]
        + [HBM_SPEC] * n_in,
        out_specs=[row(D_MODEL)] + [HBM_SPEC] * n_out,
        out_shape=[jax.ShapeDtypeStruct((t, D_MODEL), F32)] + rider.out_shapes,
        scratch_shapes=[pltpu.SemaphoreType.DMA((s,)) for s in rider.sems],
        compiler_params=_cparams("arbitrary"),
    )(*pieces, w_in_p, dr1, *rider.inputs)


def _dw_out(sb_o, gla_o, dr1, t, tk):
    def body(sb_ref, gl_ref, dr_ref, o_ref):
        kk = pl.program_id(0)
        drb = dr_ref[...].astype(BF16)
        top, bottom = _dot_tn(sb_ref[...], drb), _dot_tn(gl_ref[...], drb)

        @pl.when(kk == 0)
        def _():
            o_ref[0:SB_WIDTH, :] = top
            o_ref[SB_WIDTH:, :] = bottom

        @pl.when(kk > 0)
        def _():
            o_ref[0:SB_WIDTH, :] += top
            o_ref[SB_WIDTH:, :] += bottom

    return pl.pallas_call(
        body, name="dw_out", grid=(t // tk,),
        in_specs=[pl.BlockSpec((tk, SB_WIDTH), lambda kk: (kk, 0)), pl.BlockSpec((tk, GLA_WIDTH), lambda kk: (kk, 0)),
                  pl.BlockSpec((tk, D_MODEL), lambda kk: (kk, 0))],
        out_specs=pl.BlockSpec((D_MODEL, D_MODEL), lambda kk: (0, 0)),
        out_shape=jax.ShapeDtypeStruct((D_MODEL, D_MODEL), F32),
        compiler_params=_cparams("arbitrary"),
    )(sb_o, gla_o, dr1)


def _dw_in(x, pieces, t, tm=512, tk=512):
    offs = _col_offsets(pieces)
    npc = len(pieces)
    n = offs[-1] + pieces[-1].shape[1]
    nk = t // tk

    def body(*refs):
        x_ref, p_refs, o_ref, acc_ref = refs[0], refs[1:1 + npc], refs[1 + npc], refs[2 + npc]
        kk = pl.program_id(1)
        xb = x_ref[...].astype(BF16)
        for p_ref, off in zip(p_refs, offs):
            cols = slice(off, off + p_ref.shape[1])
            part = _dot_tn(xb, p_ref[...].astype(BF16))
            if nk == 1:
                o_ref[:, cols] = part
                continue

            @pl.when(kk == 0)
            def _():
                acc_ref[:, cols] = part

            @pl.when((kk > 0) & (kk < nk - 1))
            def _():
                acc_ref[:, cols] += part

            @pl.when(kk == nk - 1)
            def _():
                o_ref[:, cols] = acc_ref[:, cols] + part

    return pl.pallas_call(
        body, name="dw_in", grid=(D_MODEL // tm, nk),
        in_specs=[pl.BlockSpec((tk, tm), lambda i, kk: (kk, i))]
        + [pl.BlockSpec((tk, a.shape[1]), lambda i, kk: (kk, 0)) for a in pieces],
        out_specs=pl.BlockSpec((tm, n), lambda i, kk: (i, 0)),
        out_shape=jax.ShapeDtypeStruct((D_MODEL, n), F32),
        scratch_shapes=[pltpu.VMEM((tm, n), F32)],
        compiler_params=_cparams("parallel", "arbitrary"),
    )(x, *pieces)


class _Rider:
    def __init__(self, inputs, out_shapes, sems, copies, aliases=()):
        self.inputs, self.out_shapes, self.sems, self.copies = list(inputs), list(out_shapes), tuple(sems), copies
        self.aliases = tuple(aliases)


def _join(a, b):
    na_in, na_out, na_sems = len(a.inputs), len(a.out_shapes), len(a.sems)

    def copies(ins, outs, sems):
        first = a.copies(ins[:na_in], outs[:na_out], sems[:na_sems])
        second = b.copies(ins[na_in:], outs[na_out:], sems[na_sems:])
        return tuple(u + v for u, v in zip(first, second))

    return _Rider(a.inputs + b.inputs, a.out_shapes + b.out_shapes, a.sems + b.sems, copies,
                  a.aliases + tuple((i + na_in, o + na_out) for i, o in b.aliases))


def _ride(rider, in_refs, out_refs, sems, first, last):
    @pl.when(first)
    def _():
        local, sends, _ = rider.copies(in_refs, out_refs, sems)
        for cp in local + sends:
            cp.start()

    @pl.when(last)
    def _():
        local, sends, recvs = rider.copies(in_refs, out_refs, sems)
        for cp in recvs:
            cp.wait_recv()
        for cp in sends:
            cp.wait_send()
        for cp in local:
            cp.wait()


def _run(rider, name):
    n_in, n_out = len(rider.inputs), len(rider.out_shapes)

    def body(*refs):
        local, sends, recvs = rider.copies(refs[:n_in], refs[n_in:n_in + n_out], refs[n_in + n_out:])
        for cp in local + sends:
            cp.start()
        for cp in recvs:
            cp.wait_recv()
        for cp in sends:
            cp.wait_send()
        for cp in local:
            cp.wait()

    return pl.pallas_call(
        body, name=name, in_specs=[HBM_SPEC] * n_in, out_specs=[HBM_SPEC] * n_out, out_shape=list(rider.out_shapes),
        scratch_shapes=[pltpu.SemaphoreType.DMA((s,)) for s in rider.sems],
    )(*rider.inputs)


def _sb_tile(qh, kj, diag, strict, u_strict, r_in):
    z = _dot_nt(qh, kj)
    sp = _softplus(z)
    l1m = -sp
    lsz = z - sp
    if diag:
        l1m = jnp.where(strict, l1m, 0.0)
    cs = _dot(l1m.astype(BF16), u_strict) + r_in
    w = jnp.exp(lsz + cs)
    if diag:
        w = jnp.where(strict, w, 0.0)
    return l1m, lsz, w


def _sb_matrices():
    row = lax.broadcasted_iota(jnp.int32, (SB_BLK, SB_BLK), 0)
    col = lax.broadcasted_iota(jnp.int32, (SB_BLK, SB_BLK), 1)
    return jnp.stack([row > col, row < col]).astype(BF16)


def _sb_consts(u_ref):
    row = lax.broadcasted_iota(jnp.int32, (SB_BLK, SB_BLK), 0)
    col = lax.broadcasted_iota(jnp.int32, (SB_BLK, SB_BLK), 1)
    lane = lax.broadcasted_iota(jnp.int32, (1, LANES), 1)
    return col < row, u_ref[0], u_ref[1], lane


U_SPEC = pl.BlockSpec((2, SB_BLK, SB_BLK), lambda p, i: (0, 0, 0))


def _sb_fwd(proj, t, rider):
    nq = t // SB_BLK
    n_in, n_out = len(rider.inputs), len(rider.out_shapes)

    def body(u_ref, q_ref, k_ref, v_ref, *rest):
        rin, o_ref, sv_ref, rout = rest[:n_in], rest[n_in], rest[n_in + 1], rest[n_in + 2:n_in + 2 + n_out]
        p = pl.program_id(0)
        i = pl.program_id(1)
        _ride(rider, rin, rout, rest[n_in + 2 + n_out:], (p == 0) & (i == 0), (p == 3) & (i == nq - 1))

        strict, u_strict, _, lane = _sb_consts(u_ref)
        qf = q_ref[...] * SB_SCALE
        hms = [(lane // 64) == hh for hh in range(2)]
        qhs = [jnp.where(hm, qf, 0.0).astype(BF16) for hm in hms]

        def step(j, r0, r1, a, sv, diag, keep=None):
            rows = pl.ds(pl.multiple_of(j * SB_BLK, SB_BLK), SB_BLK)
            kj = k_ref[rows, :].astype(BF16)
            vf = v_ref[rows, :]
            rs = []
            for hh, r in enumerate((r0, r1)):
                l1m, _, w = _sb_tile(qhs[hh], kj, diag, strict, u_strict, r)
                pv = _dot(w.astype(BF16), jnp.where(hms[hh], vf, 0.0).astype(BF16))
                mass = jnp.sum(l1m, axis=1, keepdims=True)
                if keep is not None:
                    pv, mass = jnp.where(keep, pv, 0.0), jnp.where(keep, mass, 0.0)
                a = a + pv
                rs.append(r + mass)
                sv = jnp.where(lane == hh * 64 + (i - j + 1), rs[hh], sv)
            return rs[0], rs[1], a, sv

        zero = jnp.zeros((SB_BLK, 1), F32)
        acc0 = jnp.zeros((SB_BLK, LANES), F32)
        r0, r1, acc, sv = step(i, zero, zero, acc0, acc0, True)
        r0, r1, acc, sv = step(jnp.maximum(i - 1, 0), r0, r1, acc, sv, False, keep=i > 0)
        _, _, _, acc, sv = lax.while_loop(
            lambda c: (c[0] >= 0) & (jnp.maximum(jnp.max(c[1]), jnp.max(c[2])) > SB_CUT),
            lambda c: (c[0] - 1,) + step(c[0], c[1], c[2], c[3], c[4], False),
            (i - 2, r0, r1, acc, sv))
        o_ref[...] = acc.astype(BF16)
        sv_ref[...] = sv

    return pl.pallas_call(
        body, name="sb_fwd", grid=(4, nq),
        in_specs=[U_SPEC, pl.BlockSpec((SB_BLK, LANES), lambda p, i: (i, p)),
                  pl.BlockSpec((t, LANES), lambda p, i: (0, 4 + p)),
                  pl.BlockSpec((t, LANES), lambda p, i: (0, 8 + p))] + [HBM_SPEC] * n_in,
        out_specs=[pl.BlockSpec((SB_BLK, LANES), lambda p, i: (i, p))] * 2 + [HBM_SPEC] * n_out,
        out_shape=[jax.ShapeDtypeStruct((t, SB_WIDTH), BF16), jax.ShapeDtypeStruct((t, SB_WIDTH), F32)] + rider.out_shapes,
        scratch_shapes=[pltpu.SemaphoreType.DMA((s,)) for s in rider.sems],
        compiler_params=_cparams("arbitrary", "arbitrary"),
    )(_sb_matrices(), proj, proj, proj, *rider.inputs)


def _sb_bwd(proj, dcat, mass, t, rider):
    nq = t // SB_BLK
    n_in, n_out = len(rider.inputs), len(rider.out_shapes)

    def body(u_ref, q_ref, k_ref, v_ref, do_ref, sv_ref, *rest):
        rin = rest[:n_in]
        dq_ref, dk_ref, dv_ref = rest[n_in:n_in + 3]
        rout = rest[n_in + 3:n_in + 3 + n_out]
        p = pl.program_id(0)
        i = pl.program_id(1)
        _ride(rider, rin, rout, rest[n_in + 3 + n_out:], (p == 0) & (i == 0), (p == 3) & (i == nq - 1))

        @pl.when(i == 0)
        def _():
            dk_ref[...] = jnp.zeros_like(dk_ref)
            dv_ref[...] = jnp.zeros_like(dv_ref)

        strict, u_strict, u_pre, lane = _sb_consts(u_ref)
        qf = q_ref[...] * SB_SCALE
        dof = do_ref[...]
        hms = [(lane // 64) == hh for hh in range(2)]
        qhs = [jnp.where(hm, qf, 0.0).astype(BF16) for hm in hms]
        dohs = [jnp.where(hm, dof, 0.0).astype(BF16) for hm in hms]

        sv = sv_ref[...]
        zero = jnp.zeros((SB_BLK, 1), F32)

        def mass_right(hh, d):
            return jnp.sum(jnp.where(lane == hh * 64 + d, sv, 0.0), axis=1, keepdims=True)

        dstop = lax.while_loop(
            lambda d: (i - d >= 0) & (jnp.maximum(jnp.max(mass_right(0, d)), jnp.max(mass_right(1, d))) > SB_CUT),
            lambda d: d + 1, 1)
        jstop = i - dstop

        def step(j, carry, diag, keep=None):
            pre_g0, pre_g1, dqa = carry
            rows = pl.ds(pl.multiple_of(j * SB_BLK, SB_BLK), SB_BLK)
            kf = k_ref[rows, :]
            kj = kf.astype(BF16)
            vj = v_ref[rows, :].astype(BF16)
            dv = jnp.zeros((SB_BLK, LANES), F32)
            dk = jnp.zeros((SB_BLK, LANES), F32)
            dqj = jnp.zeros((SB_BLK, LANES), F32)
            pre = []
            for hh, pre_g in enumerate((pre_g0, pre_g1)):
                _, lsz, w = _sb_tile(qhs[hh], kj, diag, strict, u_strict, zero if diag else mass_right(hh, i - j))
                g = w * _dot_nt(dohs[hh], vj)
                gpre = _dot(g.astype(BF16), u_pre) + pre_g
                sig = jnp.exp(lsz)
                dz = g * (1.0 - sig) - gpre * sig
                if diag:
                    dz = jnp.where(strict, dz, 0.0)
                dzb = dz.astype(BF16)
                dv = dv + _dot_tn(w.astype(BF16), dohs[hh])
                dk = dk + _dot_tn(dzb, qhs[hh])
                dqj = dqj + _dot(dzb, jnp.where(hms[hh], kf, 0.0).astype(BF16))
                gsum = jnp.sum(g, axis=1, keepdims=True)
                pre.append(pre_g + (gsum if keep is None else jnp.where(keep, gsum, 0.0)))
            if keep is not None:
                dv, dk, dqj = jnp.where(keep, dv, 0.0), jnp.where(keep, dk, 0.0), jnp.where(keep, dqj, 0.0)
            dv_ref[rows, :] += dv
            dk_ref[rows, :] += dk
            return pre[0], pre[1], dqa + dqj

        carry = lax.fori_loop(jstop + 1, i - 1, lambda j, c: step(j, c, False), (zero, zero, jnp.zeros((SB_BLK, LANES), F32)))
        carry = step(jnp.maximum(i - 1, 0), carry, False, keep=i > 0)
        _, _, dq = step(i, carry, True)
        dq_ref[...] = (dq * SB_SCALE).astype(BF16)

    return pl.pallas_call(
        body, name="sb_bwd", grid=(4, nq),
        in_specs=[U_SPEC, pl.BlockSpec((SB_BLK, LANES), lambda p, i: (i, p)),
                  pl.BlockSpec((t, LANES), lambda p, i: (0, 4 + p)),
                  pl.BlockSpec((t, LANES), lambda p, i: (0, 8 + p)),
                  pl.BlockSpec((SB_BLK, LANES), lambda p, i: (i, p)),
                  pl.BlockSpec((SB_BLK, LANES), lambda p, i: (i, p))] + [HBM_SPEC] * n_in,
        out_specs=[pl.BlockSpec((SB_BLK, LANES), lambda p, i: (i, p)),
                   pl.BlockSpec((t, LANES), lambda p, i: (0, p)),
                   pl.BlockSpec((t, LANES), lambda p, i: (0, p))] + [HBM_SPEC] * n_out,
        out_shape=[jax.ShapeDtypeStruct((t, SB_WIDTH), BF16)] + [jax.ShapeDtypeStruct((t, SB_WIDTH), F32)] * 2
        + rider.out_shapes,
        scratch_shapes=[pltpu.SemaphoreType.DMA((s,)) for s in rider.sems],
        compiler_params=_cparams("arbitrary", "arbitrary"),
    )(_sb_matrices(), proj, proj, proj, dcat, mass, *rider.inputs)


def _gla_consts():
    r = lax.broadcasted_iota(jnp.int32, (GLA_ROWS, GLA_ROWS), 0)
    c = lax.broadcasted_iota(jnp.int32, (GLA_ROWS, GLA_ROWS), 1)
    same = (r // CHUNK) == (c // CHUNK)
    causal = same & (c <= r)
    upto_mid = c % CHUNK <= CHUNK // 2 - 1
    fwd_stack = jnp.concatenate([causal, same & upto_mid, same], axis=0).astype(BF16)
    bwd_stack = jnp.concatenate([same & (c >= r), same & (r % CHUNK <= CHUNK // 2 - 1), same], axis=1).astype(BF16)
    rowid = lax.broadcasted_iota(jnp.int32, (GLA_ROWS, 1), 0)
    lane = lax.broadcasted_iota(jnp.int32, (1, LANES), 1)
    sr = lax.broadcasted_iota(jnp.int32, (LANES, 2 * LANES), 0)
    sc = lax.broadcasted_iota(jnp.int32, (LANES, 2 * LANES), 1)
    blockdiag = (sr // 64) == (sc // LANES)
    return causal, fwd_stack, bwd_stack, rowid, lane, blockdiag


def _dot3(u, x):
    hi, mid, lo = _split3(x)
    return _dot(u, hi) + _dot(u, mid) + _dot(u, lo)


def _row_to_col(row):
    return jnp.transpose(jnp.broadcast_to(row, (LANES, LANES)))


def _gla_gates(ga_ref, gup_ref, gbias_ref):
    pre = _dot(ga_ref[...].astype(BF16), gup_ref[...].astype(BF16)) + gbias_ref[...]
    log_a = (jnp.minimum(pre, 0.0) - jnp.log(1.0 + jnp.exp(-jnp.abs(pre)))) / GLA_TAU
    return pre, log_a


def _gla_terms(g2, q2, k2, fwd_stack):
    bs = _dot3(fwd_stack, g2)
    b, b_ref, b_last = bs[:GLA_ROWS], bs[GLA_ROWS:2 * GLA_ROWS], bs[2 * GLA_ROWS:]
    qs = q2 * GLA_SCALE
    e_q = jnp.exp(b - b_ref)
    e_k = jnp.exp(b_ref - b)
    e_d = jnp.exp(b_last - b)
    e_b = jnp.exp(b)
    decay = jnp.exp(b_last)
    return dict(qs=qs, e_q=e_q, e_k=e_k, e_d=e_d, e_b=e_b, decay=decay,
                qi=qs * e_q, ki=k2 * e_k, kd=k2 * e_d, qb=qs * e_b)


def _gla_fwd(proj, gate_up_pad, gate_bias, gnorm, t, rider):
    nsteps = t // GLA_ROWS
    cps = GLA_ROWS // CHUNK
    n_in, n_out = len(rider.inputs), len(rider.out_shapes)

    def body(q_ref, k_ref, v_ref, gg_ref, ga_ref, gup_ref, gbias_ref, gn_ref, *rest):
        o_ref, oraw_ref, st_ref = rest[n_in:n_in + 3]
        s_scr = rest[n_in + 3 + n_out]
        i = pl.program_id(0)
        _ride(rider, rest[:n_in], rest[n_in + 3:n_in + 3 + n_out], rest[n_in + 4 + n_out:], i == 0, i == nsteps - 1)

        @pl.when(i == 0)
        def _():
            s_scr[...] = jnp.zeros_like(s_scr)

        _, fwd_stack, _, _, lane, blockdiag = _gla_consts()
        causal64 = (lax.broadcasted_iota(jnp.int32, (CHUNK, CHUNK), 1) <= lax.broadcasted_iota(jnp.int32, (CHUNK, CHUNK), 0))
        _, log_a = _gla_gates(ga_ref, gup_ref, gbias_ref)
        gn = gn_ref[...]
        pre = []
        for p in range(2):
            kl = slice(p * LANES, (p + 1) * LANES)
            tm = _gla_terms(log_a[:, kl], q_ref[:, kl], k_ref[:, kl], fwd_stack)
            pre.append(dict(
                v2b=v_ref[:, p * 2 * LANES:(p + 1) * 2 * LANES].astype(BF16), kib=tm["ki"].astype(BF16),
                kdb=tm["kd"].astype(BF16), qbb=tm["qb"].astype(BF16), decay=tm["decay"],
                qihb=[jnp.where((lane // 64) == hh, tm["qi"], 0.0).astype(BF16) for hh in range(2)]))
        state = [s_scr[0], s_scr[1]]
        outs = [[], []]
        for cc in range(cps):
            rows = slice(cc * CHUNK, (cc + 1) * CHUNK)
            for p in range(2):
                w = pre[p]
                st_ref[cc, p] = state[p]
                intra = []
                for hh in range(2):
                    a = jnp.where(causal64, _dot_nt(w["qihb"][hh][rows], w["kib"][rows]), 0.0)
                    intra.append(_dot(a.astype(BF16), w["v2b"][rows, hh * LANES:(hh + 1) * LANES]))
                outs[p].append(jnp.concatenate(intra, axis=1) + _dot(w["qbb"][rows], state[p].astype(BF16)))
                upd = jnp.where(blockdiag, _dot_tn(w["kdb"][rows], w["v2b"][rows]), 0.0)
                dcol = _row_to_col(w["decay"][cc * CHUNK:cc * CHUNK + 1])
                state[p] = state[p] * jnp.concatenate([dcol, dcol], axis=1) + upd
        for p in range(2):
            s_scr[p] = state[p]
            vl = slice(p * 2 * LANES, (p + 1) * 2 * LANES)
            o2 = jnp.concatenate(outs[p], axis=0)
            oraw_ref[:, vl] = o2
            for hh in range(2):
                oh = o2[:, hh * LANES:(hh + 1) * LANES]
                gl = slice(p * 2 * LANES + hh * LANES, p * 2 * LANES + (hh + 1) * LANES)
                rinv = lax.rsqrt(jnp.mean(oh * oh, axis=1, keepdims=True) + RMS_EPS)
                gg = gg_ref[:, gl]
                o_ref[:, gl] = (oh * rinv * gn * (gg * _sigmoid(gg))).astype(BF16)

    cb = lambda w, idx: pl.BlockSpec((GLA_ROWS, w), lambda i: (i, idx))
    full = lambda shp: pl.BlockSpec(shp, lambda i: tuple(0 for _ in shp))
    return pl.pallas_call(
        body, name="gla_fwd", grid=(nsteps,),
        in_specs=[cb(256, 6), cb(256, 7), cb(512, 4), cb(512, 5), cb(128, 24),
                  full((LANES, GLA_KEYS)), full((1, GLA_KEYS)), full((1, LANES))] + [HBM_SPEC] * n_in,
        out_specs=[pl.BlockSpec((GLA_ROWS, GLA_WIDTH), lambda i: (i, 0)),
                   pl.BlockSpec((GLA_ROWS, GLA_WIDTH), lambda i: (i, 0)),
                   pl.BlockSpec((cps, 2, LANES, 2 * LANES), lambda i: (i, 0, 0, 0))] + [HBM_SPEC] * n_out,
        out_shape=[jax.ShapeDtypeStruct((t, GLA_WIDTH), BF16), jax.ShapeDtypeStruct((t, GLA_WIDTH), F32),
                   jax.ShapeDtypeStruct((t // CHUNK, 2, LANES, 2 * LANES), F32)]
        + rider.out_shapes,
        input_output_aliases={8 + i: 3 + o for i, o in rider.aliases},
        scratch_shapes=[pltpu.VMEM((2, LANES, 2 * LANES), F32)] + [pltpu.SemaphoreType.DMA((s,)) for s in rider.sems],
        compiler_params=_cparams("arbitrary"),
    )(proj, proj, proj, proj, proj, gate_up_pad, gate_bias, gnorm, *rider.inputs)


def _gla_bwd(proj, dcat, oraw, states, gate_up_pad, gate_bias, gnorm, t, rider):
    nsteps = t // GLA_ROWS
    cps = GLA_ROWS // CHUNK
    wout = 2 * GLA_KEYS + 2 * GLA_WIDTH + LANES
    n_in, n_out = len(rider.inputs), len(rider.out_shapes)

    def body(q_ref, k_ref, v_ref, gg_ref, ga_ref, do_ref, oraw_ref, st_ref, gup_ref, gbias_ref, gn_ref, *rest):
        d_ref, dgup_ref, dgbias_ref, dgn_ref = rest[n_in:n_in + 4]
        ds_scr = rest[n_in + 4 + n_out]
        i = pl.program_id(0)
        _ride(rider, rest[:n_in], rest[n_in + 4:n_in + 4 + n_out], rest[n_in + 5 + n_out:], i == 0, i == nsteps - 1)

        @pl.when(i == 0)
        def _():
            ds_scr[...] = jnp.zeros_like(ds_scr)
            dgup_ref[...] = jnp.zeros_like(dgup_ref)
            dgbias_ref[...] = jnp.zeros_like(dgbias_ref)
            dgn_ref[...] = jnp.zeros_like(dgn_ref)

        causal, fwd_stack, bwd_stack, rowid, lane, blockdiag = _gla_consts()
        pre, log_a = _gla_gates(ga_ref, gup_ref, gbias_ref)
        gn = gn_ref[...]
        dgn = jnp.zeros((1, LANES), F32)
        dgs = []
        pre_p = []
        for p in range(2):
            kl = slice(p * LANES, (p + 1) * LANES)
            vl = slice(p * 2 * LANES, (p + 1) * 2 * LANES)
            tm = _gla_terms(log_a[:, kl], q_ref[:, kl], k_ref[:, kl], fwd_stack)
            v2b = v_ref[:, vl].astype(BF16)
            dos = []
            for hh in range(2):
                gl = slice(p * 2 * LANES + hh * LANES, p * 2 * LANES + (hh + 1) * LANES)
                oh = oraw_ref[:, gl]
                rinv = lax.rsqrt(jnp.mean(oh * oh, axis=1, keepdims=True) + RMS_EPS)
                on = oh * rinv
                gg = gg_ref[:, gl]
                sg = _sigmoid(gg)
                sil = gg * sg
                dgo = do_ref[:, gl]
                d_ref[:, 2 * GLA_KEYS + GLA_WIDTH + gl.start:2 * GLA_KEYS + GLA_WIDTH + gl.stop] = (
                    dgo * on * gn * (sg * (1.0 + gg * (1.0 - sg)))).astype(BF16)
                dgn = dgn + jnp.sum(dgo * sil * on, axis=0, keepdims=True)
                don = dgo * sil * gn
                dos.append(rinv * (don - on * jnp.mean(don * on, axis=1, keepdims=True)))
            do2b = jnp.concatenate(dos, axis=1).astype(BF16)
            qib = tm["qi"].astype(BF16)
            kib = tm["ki"].astype(BF16)
            kdb = tm["kd"].astype(BF16)
            qbb = tm["qb"].astype(BF16)
            dqi = jnp.zeros((GLA_ROWS, LANES), F32)
            dki = jnp.zeros((GLA_ROWS, LANES), F32)
            dvs = []
            for hh in range(2):
                hm = (lane // 64) == hh
                hl = slice(hh * LANES, (hh + 1) * LANES)
                a = jnp.where(causal, _dot_nt(jnp.where(hm, tm["qi"], 0.0).astype(BF16), kib), 0.0).astype(BF16)
                da = jnp.where(causal, _dot_nt(do2b[:, hl], v2b[:, hl]), 0.0).astype(BF16)
                dvs.append(_dot_tn(a, do2b[:, hl]))
                dqi = dqi + jnp.where(hm, _dot(da, kib), 0.0)
                dki = dki + jnp.where(hm, _dot_tn(da, qib), 0.0)
            pre_p.append(dict(tm=tm, v2b=v2b, do2b=do2b, kdb=kdb, qbb=qbb, dqi=dqi, dki=dki, dvs=dvs))
        dstate = [ds_scr[0], ds_scr[1]]
        rec = [dict(dv_st=[None] * cps, dqb=[None] * cps, dkd=[None] * cps, dd=[None] * cps) for _ in range(2)]
        for cc in reversed(range(cps)):
            rows = slice(cc * CHUNK, (cc + 1) * CHUNK)
            for p in range(2):
                w, ds2 = pre_p[p], dstate[p]
                s_prev = st_ref[cc, p]
                ds2b = ds2.astype(BF16)
                rec[p]["dv_st"][cc] = _dot(w["kdb"][rows], ds2b)
                rec[p]["dqb"][cc] = _dot_nt(w["do2b"][rows], s_prev.astype(BF16))
                rec[p]["dkd"][cc] = _dot_nt(w["v2b"][rows], ds2b)
                decay_row = w["tm"]["decay"][cc * CHUNK:cc * CHUNK + 1]
                ddecay_col = jnp.sum(ds2 * s_prev, axis=1, keepdims=True)
                ddecay_row = jnp.transpose(jnp.broadcast_to(ddecay_col, (LANES, LANES)))[0:1, :]
                rec[p]["dd"][cc] = jnp.broadcast_to(ddecay_row * decay_row, (CHUNK, LANES))
                dcol = _row_to_col(decay_row)
                dstate[p] = (jnp.where(blockdiag, _dot_tn(w["qbb"][rows], w["do2b"][rows]), 0.0)
                             + ds2 * jnp.concatenate([dcol, dcol], axis=1))
        for p in range(2):
            ds_scr[p] = dstate[p]
            tm, dqi, dki = pre_p[p]["tm"], pre_p[p]["dqi"], pre_p[p]["dki"]
            dv2 = jnp.concatenate(pre_p[p]["dvs"], axis=1) + jnp.concatenate(rec[p]["dv_st"], axis=0)
            dqb = jnp.concatenate(rec[p]["dqb"], axis=0)
            dkd = jnp.concatenate(rec[p]["dkd"], axis=0)
            dd = rec[p]["dd"]
            dqs = dqi * tm["e_q"] + dqb * tm["e_b"]
            dk = dki * tm["e_k"] + dkd * tm["e_d"]
            t_qi = dqi * tm["qi"]
            t_ki = dki * tm["ki"]
            t_kd = dkd * tm["kd"]
            db = t_qi - t_ki + dqb * tm["qb"] - t_kd
            to_mid = t_ki - t_qi
            to_last = t_kd + jnp.where(rowid % CHUNK == CHUNK - 1, jnp.concatenate(dd, axis=0), 0.0)
            dgs.append(_dot3(bwd_stack, jnp.concatenate([db, to_mid, to_last], axis=0)))
            d_ref[:, p * LANES:(p + 1) * LANES] = (dqs * GLA_SCALE).astype(BF16)
            d_ref[:, GLA_KEYS + p * LANES:GLA_KEYS + (p + 1) * LANES] = dk.astype(BF16)
            d_ref[:, 2 * GLA_KEYS + p * 2 * LANES:2 * GLA_KEYS + (p + 1) * 2 * LANES] = dv2.astype(BF16)
        dlog_a = jnp.concatenate(dgs, axis=1)
        dpre = dlog_a * (1.0 / GLA_TAU) * _sigmoid(-pre)
        dpb = dpre.astype(BF16)
        dgn_ref[...] += dgn
        dgbias_ref[...] += jnp.sum(dpre, axis=0, keepdims=True)
        dgup_ref[...] += _dot_tn(ga_ref[...].astype(BF16), dpb)
        d_ref[:, 2 * GLA_KEYS + 2 * GLA_WIDTH:] = _dot_nt(dpb, gup_ref[...].astype(BF16)).astype(BF16)

    rev = lambda i: nsteps - 1 - i
    cb = lambda w, idx: pl.BlockSpec((GLA_ROWS, w), lambda i: (rev(i), idx))
    full = lambda shp: pl.BlockSpec(shp, lambda i: tuple(0 for _ in shp))
    return pl.pallas_call(
        body, name="gla_bwd", grid=(nsteps,),
        in_specs=[cb(256, 6), cb(256, 7), cb(512, 4), cb(512, 5), cb(128, 24), cb(512, 1), cb(512, 0),
                  pl.BlockSpec((cps, 2, LANES, 2 * LANES), lambda i: (rev(i), 0, 0, 0)),
                  full((LANES, GLA_KEYS)), full((1, GLA_KEYS)), full((1, LANES))] + [HBM_SPEC] * n_in,
        out_specs=[pl.BlockSpec((GLA_ROWS, wout), lambda i: (rev(i), 0)),
                   full((LANES, GLA_KEYS)), full((1, GLA_KEYS)), full((1, LANES))] + [HBM_SPEC] * n_out,
        out_shape=[jax.ShapeDtypeStruct((t, wout), BF16), jax.ShapeDtypeStruct((LANES, GLA_KEYS), F32),
                   jax.ShapeDtypeStruct((1, GLA_KEYS), F32), jax.ShapeDtypeStruct((1, LANES), F32)] + rider.out_shapes,
        scratch_shapes=[pltpu.VMEM((2, LANES, 2 * LANES), F32)] + [pltpu.SemaphoreType.DMA((s,)) for s in rider.sems],
        compiler_params=_cparams("arbitrary"),
    )(proj, proj, proj, proj, proj, dcat, oraw, states, gate_up_pad, gate_bias, gnorm, *rider.inputs)


def _ln_stats(r):
    mu = jnp.mean(r, axis=1, keepdims=True)
    xc = r - mu
    rstd = lax.rsqrt(jnp.mean(xc * xc, axis=1, keepdims=True) + LN_EPS)
    return xc * rstd, rstd


def _ln_bwd(dy_g, xhat, rstd):
    return rstd * (dy_g - jnp.mean(dy_g, axis=1, keepdims=True) - xhat * jnp.mean(dy_g * xhat, axis=1, keepdims=True))


def _outproj_ln1(sb_o, gla_o, x, w_out, g1, b1, t, tm=512):
    def body(sb_ref, gl_ref, x_ref, w_ref, g_ref, b_ref, xhat_ref, rstd_ref, h_ref):
        mix = _dot(sb_ref[...], w_ref[0:SB_WIDTH, :]) + _dot(gl_ref[...], w_ref[SB_WIDTH:, :])
        xhat, rstd = _ln_stats(ALPHA * x_ref[...] + mix)
        xhat_ref[...] = xhat
        rstd_ref[...] = rstd
        h_ref[...] = (xhat * g_ref[...] + b_ref[...]).astype(BF16)

    row = lambda w: pl.BlockSpec((tm, w), lambda i: (i, 0))
    full = lambda shp: pl.BlockSpec(shp, lambda i: (0, 0))
    return pl.pallas_call(
        body, name="outproj_ln1", grid=(t // tm,),
        in_specs=[row(SB_WIDTH), row(GLA_WIDTH), row(D_MODEL), full((D_MODEL, D_MODEL)), full((1, D_MODEL)), full((1, D_MODEL))],
        out_specs=[row(D_MODEL), row(1), row(D_MODEL)],
        out_shape=[jax.ShapeDtypeStruct((t, D_MODEL), F32), jax.ShapeDtypeStruct((t, 1), F32),
                   jax.ShapeDtypeStruct((t, D_MODEL), BF16)],
        compiler_params=_cparams("parallel"),
    )(sb_o, gla_o, x, w_out, g1, b1)


_INV_SQRT2 = 1.0 / math.sqrt(2.0)
_INV_SQRT2PI = 1.0 / math.sqrt(2.0 * math.pi)


def _conv3(xs, w_ref, b_ref, half):
    return (w_ref[half, 0:1, :] * pltpu.roll(xs, 2, 0) + w_ref[half, 1:2, :] * pltpu.roll(xs, 1, 0)
            + w_ref[half, 2:3, :] * xs + b_ref[half])


HALO = 16


def _conv_gelu_fwd(up3, conv_w3, conv_b3, t, tr=512, ct=256):
    nct = D_FF // ct
    hb = tr // HALO

    def body(cur_ref, prev_ref, w_ref, b_ref, gm_ref):
        i = pl.program_id(1)
        keep = (i > 0).astype(F32)
        us = []
        for half in range(2):
            xs = jnp.concatenate([prev_ref[half].astype(F32) * keep, cur_ref[half].astype(F32)], axis=0)
            us.append(_conv3(xs, w_ref, b_ref, half)[HALO:, :])
        a, c = us
        gelu = 0.5 * a * (1.0 + lax.erf(a * _INV_SQRT2))
        gm_ref[...] = (gelu * c).astype(BF16)

    return pl.pallas_call(
        body, name="conv_gelu_fwd", grid=(nct, t // tr),
        in_specs=[pl.BlockSpec((2, tr, ct), lambda j, i: (0, i, j)),
                  pl.BlockSpec((2, HALO, ct), lambda j, i: (0, jnp.maximum(i * hb - 1, 0), j)),
                  pl.BlockSpec((2, 3, ct), lambda j, i: (0, 0, j)),
                  pl.BlockSpec((2, 1, ct), lambda j, i: (0, 0, j))],
        out_specs=pl.BlockSpec((tr, ct), lambda j, i: (i, j)),
        out_shape=jax.ShapeDtypeStruct((t, D_FF), BF16),
        compiler_params=_cparams("parallel", "parallel"),
    )(up3, up3, conv_w3, conv_b3)


def _conv_gelu_bwd(up3, dgm, conv_w3, conv_b3, t, tr=512, ct=256):
    nct = D_FF // ct
    nrt = t // tr
    hb = tr // HALO
    n = tr + 2 * HALO
    lo, hi = HALO, tr + HALO

    def body(cur_ref, prev_ref, next_ref, dg_ref, dgn_ref, w_ref, b_ref, dup_ref, dcw_ref, dcb_ref):
        i = pl.program_id(1)

        @pl.when(i == 0)
        def _():
            dcw_ref[...] = jnp.zeros_like(dcw_ref)
            dcb_ref[...] = jnp.zeros_like(dcb_ref)

        keep_prev = (i > 0).astype(F32)
        keep_next = (i < nrt - 1).astype(F32)
        xs, xm1, xm2, us = [], [], [], []
        for half in range(2):
            x = jnp.concatenate([prev_ref[half].astype(F32) * keep_prev, cur_ref[half].astype(F32),
                                 next_ref[half].astype(F32)], axis=0)
            xs.append(x)
            xm1.append(pltpu.roll(x, 1, 0))
            xm2.append(pltpu.roll(x, 2, 0))
            us.append(w_ref[half, 0:1, :] * xm2[half] + w_ref[half, 1:2, :] * xm1[half]
                      + w_ref[half, 2:3, :] * x + b_ref[half])
        a, c = us
        dg = jnp.concatenate([jnp.zeros((HALO, ct), F32), dg_ref[...].astype(F32),
                              dgn_ref[...].astype(F32) * keep_next], axis=0)
        cdf = 0.5 * (1.0 + lax.erf(a * _INV_SQRT2))
        pdf = jnp.exp(-0.5 * a * a) * _INV_SQRT2PI
        dus = [dg * c * (cdf + a * pdf), dg * (a * cdf)]
        rid = lax.broadcasted_iota(jnp.int32, (8, 1), 0)
        for half in range(2):
            du = dus[half]
            dup = (w_ref[half, 2:3, :] * du + w_ref[half, 1:2, :] * pltpu.roll(du, n - 1, 0)
                   + w_ref[half, 0:1, :] * pltpu.roll(du, n - 2, 0))
            dup_ref[half] = dup[lo:hi, :].astype(BF16)
            duc = du[lo:hi, :]
            s0 = jnp.sum(duc * xm2[half][lo:hi, :], axis=0, keepdims=True)
            s1 = jnp.sum(duc * xm1[half][lo:hi, :], axis=0, keepdims=True)
            s2 = jnp.sum(duc * xs[half][lo:hi, :], axis=0, keepdims=True)
            dcw_ref[half] += jnp.where(rid == 0, s0, jnp.where(rid == 1, s1, jnp.where(rid == 2, s2, 0.0)))
            dcb_ref[half] += jnp.sum(duc, axis=0, keepdims=True)

    last = t // HALO - 1
    return pl.pallas_call(
        body, name="conv_gelu_bwd", grid=(nct, nrt),
        in_specs=[pl.BlockSpec((2, tr, ct), lambda j, i: (0, i, j)),
                  pl.BlockSpec((2, HALO, ct), lambda j, i: (0, jnp.maximum(i * hb - 1, 0), j)),
                  pl.BlockSpec((2, HALO, ct), lambda j, i: (0, jnp.minimum((i + 1) * hb, last), j)),
                  pl.BlockSpec((tr, ct), lambda j, i: (i, j)),
                  pl.BlockSpec((HALO, ct), lambda j, i: (jnp.minimum((i + 1) * hb, last), j)),
                  pl.BlockSpec((2, 3, ct), lambda j, i: (0, 0, j)),
                  pl.BlockSpec((2, 1, ct), lambda j, i: (0, 0, j))],
        out_specs=[pl.BlockSpec((2, tr, ct), lambda j, i: (0, i, j)),
                   pl.BlockSpec((2, 8, ct), lambda j, i: (0, 0, j)),
                   pl.BlockSpec((2, 1, ct), lambda j, i: (0, 0, j))],
        out_shape=[jax.ShapeDtypeStruct((2, t, D_FF), BF16), jax.ShapeDtypeStruct((2, 8, D_FF), F32),
                   jax.ShapeDtypeStruct((2, 1, D_FF), F32)],
        compiler_params=_cparams("parallel", "arbitrary"),
    )(up3, up3, up3, dgm, dgm, conv_w3, conv_b3)


def _down_ln2_loss(gm, w_down, xhat1, g1, b1, g2, b2, target, t, tm=512):
    def body(gm_ref, w_ref, xh_ref, g1_ref, b1_ref, g2_ref, b2_ref, tg_ref, dr_ref, drb_ref, loss_ref, dg_ref, db_ref):
        i = pl.program_id(0)

        @pl.when(i == 0)
        def _():
            loss_ref[...] = jnp.zeros_like(loss_ref)
            dg_ref[...] = jnp.zeros_like(dg_ref)
            db_ref[...] = jnp.zeros_like(db_ref)

        h = xh_ref[...] * g1_ref[...] + b1_ref[...]
        xhat, rstd = _ln_stats(ALPHA * h + _dot(gm_ref[...], w_ref[...]))
        err = xhat * g2_ref[...] + b2_ref[...] - tg_ref[...]
        loss_ref[...] += 0.5 * jnp.sum(jnp.sum(err * err, axis=1, keepdims=True), axis=0, keepdims=True) / D_MODEL
        dy = err * (1.0 / D_MODEL)
        dg_ref[...] += jnp.sum(dy * xhat, axis=0, keepdims=True)
        db_ref[...] += jnp.sum(dy, axis=0, keepdims=True)
        dr = _ln_bwd(dy * g2_ref[...], xhat, rstd)
        dr_ref[...] = dr
        drb_ref[...] = dr.astype(BF16)

    row = lambda w: pl.BlockSpec((tm, w), lambda i: (i, 0))
    full = lambda shp: pl.BlockSpec(shp, lambda i: (0, 0))
    vec = full((1, D_MODEL))
    return pl.pallas_call(
        body, name="down_ln2_loss", grid=(t // tm,),
        in_specs=[row(D_FF), full((D_FF, D_MODEL)), row(D_MODEL), vec, vec, vec, vec, row(D_MODEL)],
        out_specs=[row(D_MODEL), row(D_MODEL), full((1, 1)), vec, vec],
        out_shape=[jax.ShapeDtypeStruct((t, D_MODEL), F32), jax.ShapeDtypeStruct((t, D_MODEL), BF16),
                   jax.ShapeDtypeStruct((1, 1), F32),
                   jax.ShapeDtypeStruct((1, D_MODEL), F32), jax.ShapeDtypeStruct((1, D_MODEL), F32)],
        compiler_params=_cparams("arbitrary"),
    )(gm, w_down, xhat1, g1, b1, g2, b2, target)


def _dh_ln1_bwd(dup3, w_up4, dr2, xhat1, rstd1, g1, t, tm=256):
    ws = 2 * D_FF // 4

    def body(a_ref, w_ref, dr2_ref, xh_ref, rs_ref, g_ref, dr1_ref, dg_ref, db_ref):
        i = pl.program_id(0)

        @pl.when(i == 0)
        def _():
            dg_ref[...] = jnp.zeros_like(dg_ref)
            db_ref[...] = jnp.zeros_like(db_ref)

        dh = ALPHA * dr2_ref[...]
        for s in range(4):
            dh = dh + _dot_nt(a_ref[s // 2, :, (s % 2) * ws:(s % 2 + 1) * ws], w_ref[s])
        xhat = xh_ref[...]
        dg_ref[...] += jnp.sum(dh * xhat, axis=0, keepdims=True)
        db_ref[...] += jnp.sum(dh, axis=0, keepdims=True)
        dr1_ref[...] = _ln_bwd(dh * g_ref[...], xhat, rs_ref[...])

    row = lambda w: pl.BlockSpec((tm, w), lambda i: (i, 0))
    vec = pl.BlockSpec((1, D_MODEL), lambda i: (0, 0))
    return pl.pallas_call(
        body, name="dh_ln1_bwd", grid=(t // tm,),
        in_specs=[pl.BlockSpec((2, tm, D_FF), lambda i: (0, i, 0)),
                  pl.BlockSpec((4, D_MODEL, ws), lambda i: (0, 0, 0)),
                  row(D_MODEL), row(D_MODEL), row(1), vec],
        out_specs=[row(D_MODEL), vec, vec],
        out_shape=[jax.ShapeDtypeStruct((t, D_MODEL), F32), jax.ShapeDtypeStruct((1, D_MODEL), F32),
                   jax.ShapeDtypeStruct((1, D_MODEL), F32)],
        compiler_params=_cparams("arbitrary"),
    )(dup3, w_up4, dr2, xhat1, rstd1, g1)


def _adamw(w, g, m, v, name):
    rows, cols = w.shape
    tr = rows
    for cand in (256, 128, 64, 32, 16, 8):
        if rows % cand == 0 and rows > cand:
            tr = cand
            break
    c1 = 1.0 / (1.0 - ADAM_B1 ** ADAM_STEP)
    c2 = 1.0 / (1.0 - ADAM_B2 ** ADAM_STEP)

    def body(w_ref, g_ref, m_ref, v_ref, d_ref, nm_ref, nv_ref):
        gv = g_ref[...]
        nm = ADAM_B1 * m_ref[...] + (1.0 - ADAM_B1) * gv
        nv = ADAM_B2 * v_ref[...] + (1.0 - ADAM_B2) * (gv * gv)
        d_ref[...] = -ADAM_LR * ((nm * c1) / (jnp.sqrt(nv * c2) + ADAM_EPS) + ADAM_WD * w_ref[...])
        nm_ref[...] = nm
        nv_ref[...] = nv

    spec = pl.BlockSpec((tr, cols), lambda i: (i, 0))
    out = jax.ShapeDtypeStruct((rows, cols), F32)
    return pl.pallas_call(
        body, name=name, grid=(rows // tr,), in_specs=[spec] * 4, out_specs=[spec] * 3, out_shape=[out] * 3,
        compiler_params=_cparams("parallel"),
    )(w, g, m, v)


def _local_step(x, target, w_in_p, late_shards, gate_up_pad, gate_bias, gnorm, ln1_g, ln1_b, conv_w3, conv_b3,
                ln2_g, ln2_b, c_arr, kc_arr):
    t = x.shape[0]
    tq = min(t, 1024)
    s_up, s_out, s_down = late_shards
    sh_up, sh_out, sh_down = LATE_SHAPES
    proj, out_partly = _mm(x, w_in_p, m=t, n=IN_PAD, k=D_MODEL, tm=512, tn=IN_PAD, tk=D_MODEL, name="proj",
                           rider=_gather_rider([s_out], [sh_out]))
    sb_o, sb_mass, up_partly = _sb_fwd(proj, t, _gather_rider([s_up], [sh_up]))
    gla_o, oraw, states, w_up4, w_out4, down_partly = _gla_fwd(
        proj, gate_up_pad, gate_bias, gnorm, t,
        _join(_forward_rider([up_partly, out_partly], [sh_up, sh_out]), _gather_rider([s_down], [sh_down])))
    w_out = w_out4.reshape(D_MODEL, D_MODEL)
    xhat1, rstd1, h_bf = _outproj_ln1(sb_o, gla_o, x, w_out, ln1_g, ln1_b, t)
    up3, w_down4 = _mm(h_bf, w_up4, m=t, n=2 * D_FF, k=D_MODEL, tm=tq, tn=W_UP_S, tk=D_MODEL, name="up",
                       b_spec=pl.BlockSpec((None, D_MODEL, W_UP_S), lambda i, j, kk: (j, 0, 0)),
                       o_spec=pl.BlockSpec((None, tq, W_UP_S), lambda i, j, kk: (j // 2, i, j % 2)),
                       out_shape=jax.ShapeDtypeStruct((2, t, D_FF), BF16), out_dtype=BF16,
                       rider=_forward_rider([down_partly], [sh_down]))
    w_down = w_down4.reshape(D_FF, D_MODEL)
    gm = _conv_gelu_fwd(up3, conv_w3, conv_b3, t, tr=min(t, 2048))
    dr2, dr2b, loss, dln2_g, dln2_b = _down_ln2_loss(gm, w_down, xhat1, ln1_g, ln1_b, ln2_g, ln2_b, target, t)
    dgm = _mm(dr2b, w_down, m=t, n=D_FF, k=D_MODEL, tm=tq, tn=W_UP_S, tk=D_MODEL, tb=True, out_dtype=BF16, name="dgm")
    dw_down = _mm(gm, dr2b, m=D_FF, n=D_MODEL, k=t, tm=W_UP_S, tn=D_MODEL, tk=min(t, 2048), ta=True, name="dw_down")
    dup3, dcw, dcb = _conv_gelu_bwd(up3, dgm, conv_w3, conv_b3, t, tr=tq)
    dr1, dln1_g, dln1_b = _dh_ln1_bwd(dup3, w_up4, dr2, xhat1, rstd1, ln1_g, t)
    dw_up4 = _mm(h_bf, dup3, m=D_MODEL, n=2 * D_FF, k=t, tm=512, tn=W_UP_S, tk=t, ta=True, name="dw_up",
                 b_spec=pl.BlockSpec((None, t, W_UP_S), lambda i, j, kk: (j // 2, kk, j % 2)),
                 o_spec=pl.BlockSpec((None, 512, W_UP_S), lambda i, j, kk: (j, i, 0)),
                 out_shape=jax.ShapeDtypeStruct((4, D_MODEL, W_UP_S), F32))
    dw_out = _dw_out(sb_o, gla_o, dr1, t, tq)
    gs = [dw_up4, dw_out.reshape(4, W_OUT_S, D_MODEL), dw_down.reshape(4, W_DOWN_S, D_MODEL)]
    dcat = _mm(dr1, w_out, m=t, n=D_MODEL, k=D_MODEL, tm=tq, tn=512, tk=D_MODEL, tb=True, name="dcat")
    dgla, dgup_pad, dgbias, dgnorm, *from_sib = _gla_bwd(proj, dcat, oraw, states, gate_up_pad, gate_bias, gnorm, t,
                                                         _sibling_rider(gs, LATE_SHAPES))
    ps = [_add_sibling(gs[m], from_sib[m], c_arr, LATE_ADD_ROWS[m], "add_sibling_late_%d" % m) for m in range(3)]
    small = dict(
        gate_up=dgup_pad[:GATE_RANK], gate_bias=dgbias, gla_norm_g=dgnorm, ln1_g=dln1_g, ln1_b=dln1_b,
        conv_w=jnp.concatenate([dcw[0, :3], dcw[1, :3]], axis=1), conv_b=jnp.concatenate([dcb[0], dcb[1]], axis=1),
        ln2_g=dln2_g, ln2_b=dln2_b, loss=loss)
    dsq, dsk, dsv, *others, vecs = _sb_bwd(proj, dcat, sb_mass, t,
                                           _chips_rider(ps, _pack_vec(small, SMALL_GRADS, GRAD_VEC_ROWS)))
    late_sums = [_add_chips(ps[m], others[m], kc_arr, LATE_ADD_ROWS[m], "add_chips_late_%d" % m) for m in range(3)]
    dproj = [dsq, dsk, dsv, dgla]
    dw_in_p = _dw_in(x, dproj, t, tk=tq)
    g_in = dw_in_p[None]
    from_sib_in, = _run(_sibling_rider([g_in], [(D_MODEL, IN_PAD)]), "exchange_sibling_w_in")
    half_in = _add_sibling(g_in, from_sib_in, c_arr, ADD_ROWS[0], "add_sibling_w_in")[0]
    p_in = jnp.stack([half_in[:, k * W_IN_S:(k + 1) * W_IN_S] for k in range(4)], axis=0)
    dx, others_in = _dx(dproj, w_in_p, dr1, t, _chips_rider([p_in]))
    sum_in = _add_chips(p_in, others_in, kc_arr, ADD_ROWS[0], "add_chips_w_in")
    return dx, [sum_in] + late_sums, vecs


W_IN_S, W_UP_S, W_OUT_S, W_DOWN_S = IN_WIDTH // 4, 2 * D_FF // 4, D_MODEL // 4, D_FF // 4
SHARD_SHAPES = ((D_MODEL, W_IN_S), (D_MODEL, W_UP_S), (W_OUT_S, D_MODEL), (W_DOWN_S, D_MODEL))
ADD_ROWS = (256, 256, 128, 176)
LATE_SHAPES, LATE_ADD_ROWS = SHARD_SHAPES[1:], ADD_ROWS[1:]
SMALL_ROWS = 8
VEC_SIZES = (("gate_bias", GLA_KEYS), ("gla_norm_g", LANES), ("ln1_g", D_MODEL), ("ln1_b", D_MODEL),
             ("conv_b", 2 * D_FF), ("ln2_g", D_MODEL), ("ln2_b", D_MODEL))
SMALL_GRADS = VEC_SIZES + (("conv_w", 3 * 2 * D_FF), ("gate_up", GATE_RANK * GLA_KEYS), ("loss", 1))


def _rows(a):
    flat = a.reshape(-1)
    pad = (-flat.shape[0]) % D_MODEL
    if pad:
        flat = jnp.pad(flat, (0, pad))
    return flat.reshape(-1, D_MODEL)


def _pad_rows(a, rows):
    return jnp.pad(a, ((0, rows - a.shape[0]), (0, 0)))


def _pack_vec(d, sizes, rows):
    flat = jnp.concatenate([d[n].reshape(-1) for n, _ in sizes])
    return _pad_rows(_rows(flat), rows)


def _unpack_vec(v, sizes):
    flat = v.reshape(-1)
    out, o = {}, 0
    for n, size in sizes:
        out[n] = flat[o:o + size].reshape(1, size)
        o += size
    return out


VEC_ROWS = 16
GRAD_VEC_ROWS = 32


HBM_SPEC = pl.BlockSpec(memory_space=pltpu.HBM)


def _position():
    x, y, c = lax.axis_index("x"), lax.axis_index("y"), lax.axis_index("c")
    chips = [(1 - x, y), (x, 1 - y), (1 - x, 1 - y)]
    return x, y, c, chips


def _remote(src, dst, send_sems, recv_sems, k, to):
    return pltpu.make_async_remote_copy(src_ref=src, dst_ref=dst, send_sem=send_sems.at[k], recv_sem=recv_sems.at[k],
                                        device_id=to, device_id_type=MESH)


def _gather_ici(in_refs, out_refs, shapes, send_sems, recv_sems, local_sems):
    x, y, c, chips = _position()
    k_me = 2 * x + y
    local, sends, recvs = [], [], []
    for m, (rows, _) in enumerate(shapes):
        h = rows // 2
        local.append(pltpu.make_async_copy(in_refs[m], out_refs[m].at[k_me], local_sems.at[m]))
        for j, (cx, cy) in enumerate(chips):
            sends.append(_remote(in_refs[m].at[pl.ds(c * h, h), :], out_refs[m].at[k_me, pl.ds(c * h, h), :],
                                 send_sems, recv_sems, 3 * m + j, (cx, cy, c)))
            landed = out_refs[m].at[2 * cx + cy, pl.ds(c * h, h), :]
            recvs.append(_remote(landed, landed, send_sems, recv_sems, 3 * m + j, (x, y, c)))
    return local, sends, recvs


def _gather_d2d(src_refs, dst_refs, shapes, send_sems, recv_sems, base):
    x, y, c, chips = _position()
    sends, recvs = [], []
    for m, (rows, _) in enumerate(shapes):
        h = rows // 2
        for j, (cx, cy) in enumerate(chips):
            k = 2 * cx + cy
            sends.append(_remote(src_refs[m].at[k, pl.ds(c * h, h), :], dst_refs[m].at[k, pl.ds(c * h, h), :],
                                 send_sems, recv_sems, base + 3 * m + j, (x, y, 1 - c)))
            landed = dst_refs[m].at[k, pl.ds((1 - c) * h, h), :]
            recvs.append(_remote(landed, landed, send_sems, recv_sems, base + 3 * m + j, (x, y, c)))
    return sends, recvs


def _gather_weights(shards, small, shapes):
    nm = len(shards)
    n_ici = 3 * nm

    def body(*refs):
        in_refs, small_ref = refs[:nm], refs[nm]
        out_refs, osm_ref = refs[nm + 1:2 * nm + 1], refs[2 * nm + 1]
        send_sems, recv_sems, local_sems = refs[2 * nm + 2:]
        x, y, c, chips = _position()
        k_me = 2 * x + y
        local, sends, recvs = _gather_ici(in_refs, out_refs, shapes, send_sems, recv_sems, local_sems)
        local.append(pltpu.make_async_copy(small_ref, osm_ref.at[k_me], local_sems.at[nm]))
        for j, (cx, cy) in enumerate(chips):
            sends.append(_remote(small_ref, osm_ref.at[k_me], send_sems, recv_sems, n_ici + j, (cx, cy, c)))
        for cp in local + sends:
            cp.start()
        fsends, frecvs = _gather_d2d(out_refs, out_refs, shapes, send_sems, recv_sems, n_ici + 3)
        for landed, onward in zip(recvs, fsends):
            landed.wait_recv()
            onward.start()
        for j, (cx, cy) in enumerate(chips):
            k = 2 * cx + cy
            frecvs.append(_remote(osm_ref.at[k], osm_ref.at[k], send_sems, recv_sems, n_ici + j, (x, y, c)))
        for cp in frecvs:
            cp.wait_recv()
        for cp in sends + fsends:
            cp.wait_send()
        for cp in local:
            cp.wait()

    n_sems = 2 * n_ici + 3
    return pl.pallas_call(
        body, name="gather_weights", in_specs=[HBM_SPEC] * (nm + 1), out_specs=[HBM_SPEC] * (nm + 1),
        out_shape=[jax.ShapeDtypeStruct((4,) + s, BF16) for s in shapes]
        + [jax.ShapeDtypeStruct((4, SMALL_ROWS, D_MODEL), F32)],
        scratch_shapes=[pltpu.SemaphoreType.DMA((n_sems,)), pltpu.SemaphoreType.DMA((n_sems,)),
                        pltpu.SemaphoreType.DMA((nm + 1,))],
    )(*shards, small)


def _gather_rider(shards, shapes):
    n = len(shards)
    return _Rider(shards, [jax.ShapeDtypeStruct((4,) + s, BF16) for s in shapes], (3 * n, 3 * n, n),
                  lambda ins, outs, sems: _gather_ici(ins, outs, shapes, *sems))


def _forward_rider(gathered, shapes):
    n = len(gathered)
    return _Rider(gathered, [jax.ShapeDtypeStruct(a.shape, a.dtype) for a in gathered], (3 * n, 3 * n),
                  lambda ins, outs, sems: ([],) + _gather_d2d(ins, outs, shapes, sems[0], sems[1], 0),
                  aliases=[(m, m) for m in range(n)])


def _sibling_rider(gs, shapes):
    def copies(ins, outs, sems):
        x, y, c, _ = _position()
        both = []
        for m, (rows, _) in enumerate(shapes):
            h = rows // 2
            for k in range(gs[m].shape[0]):
                both.append(_remote(ins[m].at[k, pl.ds((1 - c) * h, h), :], outs[m].at[k], sems[0], sems[1],
                                    4 * m + k, (x, y, 1 - c)))
        return [], both, both

    return _Rider(gs, [jax.ShapeDtypeStruct((g.shape[0], r // 2, cl), F32) for g, (r, cl) in zip(gs, shapes)],
                  (4 * len(gs), 4 * len(gs)), copies)


def _add_sibling(g, r, c_arr, tr, name):
    nblk, rows, cols = g.shape
    nb = rows // 2 // tr

    def body(c_ref, g_ref, r_ref, o_ref):
        o_ref[...] = (g_ref[...] + r_ref[...]).astype(BF16)

    spec = pl.BlockSpec((None, tr, cols), lambda k, i, c: (k, i, 0))
    return pl.pallas_call(
        body, name=name,
        grid_spec=pltpu.PrefetchScalarGridSpec(
            num_scalar_prefetch=1, grid=(nblk, nb),
            in_specs=[pl.BlockSpec((None, tr, cols), lambda k, i, c: (k, c[0] * nb + i, 0)), spec], out_specs=spec),
        out_shape=jax.ShapeDtypeStruct((nblk, rows // 2, cols), BF16), compiler_params=_cparams("parallel", "parallel"),
    )(c_arr, g, r)


def _reduce_ici(p_refs, r_refs, send_sems, recv_sems):
    x, y, c, chips = _position()
    sends, recvs = [], []
    for m in range(len(p_refs)):
        for j, (cx, cy) in enumerate(chips):
            sends.append(_remote(p_refs[m].at[2 * cx + cy], r_refs[m].at[j], send_sems, recv_sems, 3 * m + j, (cx, cy, c)))
            recvs.append(_remote(r_refs[m].at[j], r_refs[m].at[j], send_sems, recv_sems, 3 * m + j, (x, y, c)))
    return sends, recvs


def _chips_rider(ps, vec=None):
    nm = len(ps)
    n_ici = 3 * nm

    def copies(ins, outs, sems):
        sends, recvs = _reduce_ici(ins[:nm], outs[:nm], sems[0], sems[1])
        if vec is None:
            return [], sends, recvs
        x, y, c, _ = _position()
        my_id = 4 * x + 2 * y + c
        vec_ref, vrecv_ref = ins[nm], outs[nm]
        local = [pltpu.make_async_copy(vec_ref, vrecv_ref.at[my_id], sems[2].at[0])]
        for r in range(1, 8):
            peer = (1 - x if r & 4 else x, 1 - y if r & 2 else y, 1 - c if r & 1 else c)
            sends.append(_remote(vec_ref, vrecv_ref.at[my_id], sems[0], sems[1], n_ici + r - 1, peer))
            recvs.append(_remote(vec_ref, vrecv_ref.at[0], sems[0], sems[1], n_ici + r - 1, (x, y, c)))
        return local, sends, recvs

    out_shapes = [jax.ShapeDtypeStruct((3,) + p.shape[1:], p.dtype) for p in ps]
    if vec is None:
        return _Rider(ps, out_shapes, (n_ici, n_ici), copies)
    return _Rider(list(ps) + [vec], out_shapes + [jax.ShapeDtypeStruct((8, GRAD_VEC_ROWS, D_MODEL), F32)],
                  (n_ici + 7, n_ici + 7, 1), copies)


def _add_chips(p, r, kc_arr, tr, name):
    _, h, cols = p.shape
    nb = h // tr

    def body(kc_ref, p_ref, r0_ref, r1_ref, r2_ref, o_ref):
        o_ref[...] = ((p_ref[...].astype(F32) + r0_ref[...].astype(F32)) + r1_ref[...].astype(F32)) + r2_ref[...].astype(F32)

    rspec = lambda j: pl.BlockSpec((None, tr, cols), lambda i, kc: (j, i, 0))
    return pl.pallas_call(
        body, name=name,
        grid_spec=pltpu.PrefetchScalarGridSpec(
            num_scalar_prefetch=1, grid=(nb,),
            in_specs=[pl.BlockSpec((None, tr, cols), lambda i, kc: (kc[0], i, 0)), rspec(0), rspec(1), rspec(2)],
            out_specs=pl.BlockSpec((tr, cols), lambda i, kc: (kc[1] * nb + i, 0))),
        out_shape=jax.ShapeDtypeStruct((2 * h, cols), F32), compiler_params=_cparams("parallel"),
    )(kc_arr, p, r, r, r)


def _reunite_sibling(fs, shapes):
    n_chunks = 2
    nm = len(fs)

    def body(*refs):
        in_refs, out_refs = refs[:nm], refs[nm:2 * nm]
        send_sems, recv_sems = refs[2 * nm:]
        x, y, c, _ = _position()
        sends, recvs = [], []
        for m in range(nm):
            ch = shapes[m][0] // 2 // n_chunks
            for q in range(n_chunks):
                mine = pl.ds((c * n_chunks + q) * ch, ch)
                theirs = pl.ds(((1 - c) * n_chunks + q) * ch, ch)
                s = n_chunks * m + q
                sends.append(_remote(in_refs[m].at[mine, :], out_refs[m].at[mine, :], send_sems, recv_sems, s, (x, y, 1 - c)))
                recvs.append(_remote(in_refs[m].at[theirs, :], out_refs[m].at[theirs, :], send_sems, recv_sems, s, (x, y, c)))
        for cp in sends:
            cp.start()
        for cp in recvs:
            cp.wait_recv()
        for cp in sends:
            cp.wait_send()

    n_sems = n_chunks * nm
    return pl.pallas_call(
        body, name="reunite_sibling", in_specs=[HBM_SPEC] * nm, out_specs=[HBM_SPEC] * nm,
        out_shape=[jax.ShapeDtypeStruct(s, F32) for s in shapes],
        input_output_aliases={m: m for m in range(nm)},
        scratch_shapes=[pltpu.SemaphoreType.DMA((n_sems,)), pltpu.SemaphoreType.DMA((n_sems,))],
    )(*fs)


def _sum_vecs(v):
    def body(v_ref, o_ref):
        acc = v_ref[0]
        for d in range(1, 8):
            acc = acc + v_ref[d]
        o_ref[...] = acc

    return pl.pallas_call(body, name="sum_vecs", out_shape=jax.ShapeDtypeStruct(v.shape[1:], F32))(v)


def kernel(x, w_in, gate_up, gate_bias, gla_norm_g, w_out, ln1_g, ln1_b, w_up, conv_w, conv_b, w_down, ln2_g, ln2_b, loss_target, m_w_in, m_gate_up, m_gate_bias, m_gla_norm_g, m_w_out, m_ln1_g, m_ln1_b, m_w_up, m_conv_w, m_conv_b, m_w_down, m_ln2_g, m_ln2_b, v_w_in, v_gate_up, v_gate_bias, v_gla_norm_g, v_w_out, v_ln1_g, v_ln1_b, v_w_up, v_conv_w, v_conv_b, v_w_down, v_ln2_g, v_ln2_b):
    xi, yi, ci = lax.axis_index("x"), lax.axis_index("y"), lax.axis_index("c")
    k_me = 2 * xi + yi
    c_arr = ci.astype(jnp.int32).reshape(1)
    kc_arr = jnp.stack([k_me, ci]).astype(jnp.int32)
    small = _pad_rows(jnp.concatenate([_rows(conv_w[0]), _rows(gate_up[0])], axis=0), SMALL_ROWS)
    w_in4, gsmall = _gather_weights([w_in[0].astype(BF16)], small, SHARD_SHAPES[:1])
    late_shards = [w_up[0].astype(BF16), w_out[0].astype(BF16), w_down[0].astype(BF16)]
    w_in_p = jnp.pad(jnp.concatenate([w_in4[k] for k in range(4)], axis=1), ((0, 0), (0, IN_PAD - IN_WIDTH)))
    conv_w_f = jnp.concatenate([gsmall[k, :5].reshape(-1)[:3 * W_UP_S].reshape(3, W_UP_S) for k in range(4)], axis=1)
    gate_up_f = jnp.concatenate([gsmall[k, 5].reshape(GATE_RANK, GLA_KEYS // 4) for k in range(4)], axis=1)
    conv_w3 = jnp.transpose(conv_w_f.reshape(3, 2, D_FF), (1, 0, 2))
    conv_b3 = conv_b.reshape(2, 1, D_FF)
    gate_up_pad = jnp.pad(gate_up_f, ((0, LANES - GATE_RANK), (0, 0)))

    dx, sums, vecs = _local_step(
        x[0], loss_target[0], w_in_p, late_shards, gate_up_pad, gate_bias, gla_norm_g, ln1_g, ln1_b, conv_w3, conv_b3,
        ln2_g, ln2_b, c_arr, kc_arr)
    g_w_in, g_w_up, g_w_out, g_w_down = _reunite_sibling(sums, SHARD_SHAPES)
    gsmall_sum = _unpack_vec(_sum_vecs(vecs), SMALL_GRADS)
    g_conv_w = lax.dynamic_slice_in_dim(gsmall_sum["conv_w"].reshape(3, 2 * D_FF), k_me * W_UP_S, W_UP_S, axis=1)
    g_gate_up = lax.dynamic_slice_in_dim(gsmall_sum["gate_up"].reshape(GATE_RANK, GLA_KEYS), k_me * (GLA_KEYS // 4),
                                         GLA_KEYS // 4, axis=1)
    gv = gsmall_sum
    loss = gv["loss"][0, 0]
    gvec = _pack_vec(gv, VEC_SIZES, VEC_ROWS)

    grads = dict(w_in=g_w_in[None], gate_up=g_gate_up[None], gate_bias=gv["gate_bias"], gla_norm_g=gv["gla_norm_g"],
                 w_out=g_w_out[None], ln1_g=gv["ln1_g"], ln1_b=gv["ln1_b"], w_up=g_w_up[None], conv_w=g_conv_w[None],
                 conv_b=gv["conv_b"], w_down=g_w_down[None], ln2_g=gv["ln2_g"], ln2_b=gv["ln2_b"])
    weights = dict(w_in=w_in, gate_up=gate_up, gate_bias=gate_bias, gla_norm_g=gla_norm_g, w_out=w_out, ln1_g=ln1_g,
                   ln1_b=ln1_b, w_up=w_up, conv_w=conv_w, conv_b=conv_b, w_down=w_down, ln2_g=ln2_g, ln2_b=ln2_b)
    ms = dict(w_in=m_w_in, gate_up=m_gate_up, gate_bias=m_gate_bias, gla_norm_g=m_gla_norm_g, w_out=m_w_out, ln1_g=m_ln1_g,
              ln1_b=m_ln1_b, w_up=m_w_up, conv_w=m_conv_w, conv_b=m_conv_b, w_down=m_w_down, ln2_g=m_ln2_g, ln2_b=m_ln2_b)
    vs = dict(w_in=v_w_in, gate_up=v_gate_up, gate_bias=v_gate_bias, gla_norm_g=v_gla_norm_g, w_out=v_w_out, ln1_g=v_ln1_g,
              ln1_b=v_ln1_b, w_up=v_w_up, conv_w=v_conv_w, conv_b=v_conv_b, w_down=v_w_down, ln2_g=v_ln2_g, ln2_b=v_ln2_b)
    names = ["w_in", "gate_up", "gate_bias", "gla_norm_g", "w_out", "ln1_g", "ln1_b", "w_up", "conv_w", "conv_b", "w_down",
             "ln2_g", "ln2_b"]
    delta, new_m, new_v = {}, {}, {}
    for n in ("w_in", "gate_up", "w_out", "w_up", "conv_w", "w_down"):
        tr_ = jnp.transpose if n == "w_in" else (lambda a: a)
        d, nm, nv = _adamw(tr_(weights[n][0]), tr_(grads[n][0]), tr_(ms[n][0]), tr_(vs[n][0]), "adamw_" + n)
        delta[n], new_m[n], new_v[n] = tr_(d)[None], tr_(nm)[None], tr_(nv)[None]
    d, nm, nv = _adamw(_pack_vec(weights, VEC_SIZES, VEC_ROWS), gvec, _pack_vec(ms, VEC_SIZES, VEC_ROWS),
                       _pack_vec(vs, VEC_SIZES, VEC_ROWS), "adamw_vectors")
    for dst, src in ((delta, d), (new_m, nm), (new_v, nv)):
        dst.update(_unpack_vec(src, VEC_SIZES))
    return (loss, dx[None], *[grads[n] for n in names], *[delta[n] for n in names], *[new_m[n] for n in names],
            *[new_v[n] for n in names])
```
